```python
import jax, jax.numpy as jnp
from jax import lax
import numpy as np

D_MODEL = 1024
BATCH = 32
SEQ = 2048
DEPTH = 1

CHUNK = 64
Q_BLOCK = 128
N_MEM = 256
EPS = 1e-6

HG_HEADS = 4
HG_DK = 128
HG_DV = 128
HG_WIDTH = HG_HEADS * HG_DK
FOX_HEADS = 8
FOX_DH = 64
FOX_WIDTH = FOX_HEADS * FOX_DH
MEM_HEADS = 4
MEM_DH = 128
MEM_WIDTH = MEM_HEADS * MEM_DH
N_BRANCH = 3
D_FF = 2816
CONV_W = 3

IN_SPLITS = (HG_WIDTH, HG_WIDTH, HG_WIDTH, HG_WIDTH,
             FOX_WIDTH, FOX_WIDTH, FOX_WIDTH, FOX_HEADS,
             MEM_WIDTH, N_BRANCH * D_MODEL)
IN_COLS = 4 * HG_WIDTH + 3 * FOX_WIDTH + FOX_HEADS + MEM_WIDTH + N_BRANCH * D_MODEL

kernel_name = "hybrid_hgrn2_fox_memory_convglu"


def _split_points():
    pts, acc = [], 0
    for s in IN_SPLITS[:-1]:
        acc += s
        pts.append(acc)
    return pts


def _rmsnorm(x, g):
    xf = x.astype(jnp.float32)
    return xf * lax.rsqrt(jnp.mean(xf * xf, axis=-1, keepdims=True) + EPS) * g.astype(jnp.float32)


def _hgrn2_mixer(q, f_logit, i, g_out, lb, norm_g):
    B, T, _ = q.shape
    n = T // CHUNK

    def heads(z):
        return z.reshape(B, n, CHUNK, HG_HEADS, -1).transpose(1, 0, 3, 2, 4)

    f = lb + (1.0 - lb) * jax.nn.sigmoid(f_logit.astype(jnp.float32))
    qh = heads(jax.nn.silu(q.astype(jnp.float32)))
    kh = heads(1.0 - f)
    ih = heads(i.astype(jnp.float32))
    G = jnp.cumsum(heads(jnp.log(f)), axis=3)
    causal = jnp.tril(jnp.ones((CHUNK, CHUNK), dtype=bool))[:, :, None]

    def step(S, inp):
        qc, kc, ic, Gc = inp
        diff = Gc[:, :, :, None, :] - Gc[:, :, None, :, :]
        decay = jnp.exp(jnp.where(causal, diff, -jnp.inf))
        A = jnp.einsum('bhtc,bhsc,bhtsc->bhts', qc, kc, decay)
        o = (jnp.einsum('bhts,bhsv->bhtv', A, ic)
             + jnp.einsum('bhtc,bhcv->bhtv', qc * jnp.exp(Gc), S))
        G_last = Gc[:, :, -1, :]
        S_new = (S * jnp.exp(G_last)[..., None]
                 + jnp.einsum('bhsc,bhsv->bhcv', kc * jnp.exp(G_last[:, :, None, :] - Gc), ic))
        return S_new, o

    S0 = jnp.zeros((B, HG_HEADS, HG_DK, HG_DV), jnp.float32)
    _, o = lax.scan(step, S0, (qh, kh, ih, G))
    o = o.transpose(1, 0, 3, 2, 4).reshape(B, T, HG_HEADS, HG_DV)
    o = _rmsnorm(o, norm_g).reshape(B, T, HG_WIDTH)
    return o * jax.nn.silu(g_out.astype(jnp.float32))


def _fox_mixer(q, k, v, f_logit, f_bias, q_g, k_g):
    B, T, _ = q.shape
    qh = _rmsnorm(q.reshape(B, T, FOX_HEADS, FOX_DH), q_g).transpose(0, 2, 1, 3)
    kh = _rmsnorm(k.reshape(B, T, FOX_HEADS, FOX_DH), k_g).transpose(0, 2, 1, 3)
    vh = v.astype(jnp.float32).reshape(B, T, FOX_HEADS, FOX_DH).transpose(0, 2, 1, 3)
    log_f = jax.nn.log_sigmoid(f_logit.astype(jnp.float32) + f_bias.astype(jnp.float32))
    Fc = jnp.cumsum(log_f, axis=1).transpose(0, 2, 1)
    scale = FOX_DH ** -0.5
    outs = []
    for blk in range(T // Q_BLOCK):
        lo, hi = blk * Q_BLOCK, (blk + 1) * Q_BLOCK
        s = (jnp.einsum('bhqd,bhkd->bhqk', qh[:, :, lo:hi], kh[:, :, :hi]) * scale
             + Fc[:, :, lo:hi, None] - Fc[:, :, None, :hi])
        mask = (lo + jnp.arange(Q_BLOCK))[:, None] >= jnp.arange(hi)[None, :]
        p = jax.nn.softmax(jnp.where(mask, s, -jnp.inf), axis=-1)
        outs.append(jnp.einsum('bhqk,bhkd->bhqd', p, vh[:, :, :hi]))
    o = jnp.concatenate(outs, axis=2)
    return o.transpose(0, 2, 1, 3).reshape(B, T, FOX_WIDTH)


def _memory_mixer(q, mem_kv, q_g, k_g):
    B, T, _ = q.shape
    M = mem_kv.shape[1]
    qh = _rmsnorm(q.reshape(B, T, MEM_HEADS, MEM_DH), q_g)
    mk, mv = jnp.split(mem_kv, 2, axis=-1)
    kh = _rmsnorm(mk.reshape(B, M, MEM_HEADS, MEM_DH), k_g)
    vh = mv.astype(jnp.float32).reshape(B, M, MEM_HEADS, MEM_DH)
    s = jnp.einsum('bthd,bmhd->bhtm', qh, kh) * (MEM_DH ** -0.5)
    p = jax.nn.softmax(s, axis=-1)
    return jnp.einsum('bhtm,bmhd->bthd', p, vh).reshape(B, T, MEM_WIDTH)


def _conv_glu_ffn(h, w_up, conv_w, conv_b, w_down):
    T = h.shape[1]
    a, v = jnp.split(h @ w_up.astype(jnp.float32), 2, axis=-1)
    ap = jnp.pad(a, ((0, 0), (CONV_W - 1, 0), (0, 0)))
    a = sum(ap[:, j:j + T] * conv_w[j].astype(jnp.float32) for j in range(CONV_W)) + conv_b.astype(jnp.float32)
    return (jax.nn.gelu(a, approximate=False) * v) @ w_down.astype(jnp.float32)


def _fwd_setup_inputs(seed: int = 0) -> dict:
    key = jax.random.key(seed)
    ks = jax.random.split(key, 24)
    f32 = jnp.float32
    L = DEPTH

    def nrm(k, shape, fan_in):
        return jax.random.normal(k, shape, f32) * (fan_in ** -0.5)

    def gain(k, shape):
        return 1.0 + 0.02 * jax.random.normal(k, shape, f32)

    return {
        "x": jax.random.normal(ks[0], (BATCH, SEQ, D_MODEL), f32),
        "mem": jax.random.normal(ks[1], (BATCH, N_MEM, D_MODEL), f32),
        "norm_mix_g": gain(ks[2], (L, D_MODEL)),
        "norm_mem_g": gain(ks[3], (L, D_MODEL)),
        "w_in": nrm(ks[4], (L, D_MODEL, IN_COLS), D_MODEL),
        "hgrn_lb_logits": 0.1 * jax.random.normal(ks[5], (L + 1, HG_WIDTH), f32),
        "hgrn_norm_g": gain(ks[6], (L, HG_DV)),
        "fox_f_bias": 1.0 + 0.1 * jax.random.normal(ks[7], (L, FOX_HEADS), f32),
        "fox_q_norm_g": gain(ks[8], (L, FOX_DH)),
        "fox_k_norm_g": gain(ks[9], (L, FOX_DH)),
        "mem_kv_w": nrm(ks[10], (L, D_MODEL, 2 * MEM_WIDTH), D_MODEL),
        "mem_q_norm_g": gain(ks[11], (L, MEM_DH)),
        "mem_k_norm_g": gain(ks[12], (L, MEM_DH)),
        "w_br_hgrn": nrm(ks[13], (L, HG_WIDTH, D_MODEL), HG_WIDTH),
        "w_br_fox": nrm(ks[14], (L, FOX_WIDTH, D_MODEL), FOX_WIDTH),
        "w_br_mem": nrm(ks[15], (L, MEM_WIDTH, D_MODEL), MEM_WIDTH),
        "w_out": nrm(ks[16], (L, D_MODEL, D_MODEL), D_MODEL),
        "norm_ffn_g": gain(ks[17], (L, D_MODEL)),
        "ffn_w_up": nrm(ks[18], (L, D_MODEL, 2 * D_FF), D_MODEL),
        "ffn_conv_w": nrm(ks[19], (L, CONV_W, D_FF), CONV_W),
        "ffn_conv_b": 0.02 * jax.random.normal(ks[20], (L, D_FF), f32),
        "ffn_w_down": nrm(ks[21], (L, D_FF, D_MODEL), D_FF),
    }


def _fwd_reference(x, mem, norm_mix_g, norm_mem_g, w_in, hgrn_lb_logits, hgrn_norm_g, fox_f_bias,
              fox_q_norm_g, fox_k_norm_g, mem_kv_w, mem_q_norm_g, mem_k_norm_g,
              w_br_hgrn, w_br_fox, w_br_mem, w_out, norm_ffn_g, ffn_w_up, ffn_conv_w,
              ffn_conv_b, ffn_w_down):
    B, T, _ = x.shape
    lower_bounds = jnp.cumsum(jax.nn.softmax(hgrn_lb_logits.astype(jnp.float32), axis=0), axis=0)
    pts = _split_points()
    for l in range(DEPTH):
        h = _rmsnorm(x, norm_mix_g[l])
        z = h @ w_in[l].astype(jnp.float32)
        (hq, hf, hi, hg, fq, fk, fv, ff, mq, gate_logits) = jnp.split(z, pts, axis=-1)
        y_a = _hgrn2_mixer(hq, hf, hi, hg, lower_bounds[l], hgrn_norm_g[l])
        y_b = _fox_mixer(fq, fk, fv, ff, fox_f_bias[l], fox_q_norm_g[l], fox_k_norm_g[l])
        mem_kv = _rmsnorm(mem, norm_mem_g[l]) @ mem_kv_w[l].astype(jnp.float32)
        y_c = _memory_mixer(mq, mem_kv, mem_q_norm_g[l], mem_k_norm_g[l])
        gates = jax.nn.sigmoid(gate_logits).reshape(B, T, N_BRANCH, D_MODEL)
        merged = (gates[:, :, 0] * (y_a @ w_br_hgrn[l].astype(jnp.float32))
                  + gates[:, :, 1] * (y_b @ w_br_fox[l].astype(jnp.float32))
                  + gates[:, :, 2] * (y_c @ w_br_mem[l].astype(jnp.float32)))
        x = x + (merged @ w_out[l].astype(jnp.float32)).astype(x.dtype)
        h2 = _rmsnorm(x, norm_ffn_g[l])
        x = x + _conv_glu_ffn(h2, ffn_w_up[l], ffn_conv_w[l], ffn_conv_b[l], ffn_w_down[l]).astype(x.dtype)
    return x


import jax as _jax
import jax.numpy as _jnp

TWIN_FORMAT = 'train_step'
FWD_PARAMS = ['x', 'mem', 'norm_mix_g', 'norm_mem_g', 'w_in', 'hgrn_lb_logits', 'hgrn_norm_g', 'fox_f_bias', 'fox_q_norm_g', 'fox_k_norm_g', 'mem_kv_w', 'mem_q_norm_g', 'mem_k_norm_g', 'w_br_hgrn', 'w_br_fox', 'w_br_mem', 'w_out', 'norm_ffn_g', 'ffn_w_up', 'ffn_conv_w', 'ffn_conv_b', 'ffn_w_down']
TWIN_WEIGHTS = ['norm_mix_g', 'norm_mem_g', 'w_in', 'hgrn_lb_logits', 'hgrn_norm_g', 'fox_f_bias', 'fox_q_norm_g', 'fox_k_norm_g', 'mem_kv_w', 'mem_q_norm_g', 'mem_k_norm_g', 'w_br_hgrn', 'w_br_fox', 'w_br_mem', 'w_out', 'norm_ffn_g', 'ffn_w_up', 'ffn_conv_w', 'ffn_conv_b', 'ffn_w_down']
TWIN_DIFF_INPUT = 'x'
TWIN_INPUTS = ['x', 'mem', 'norm_mix_g', 'norm_mem_g', 'w_in', 'hgrn_lb_logits', 'hgrn_norm_g', 'fox_f_bias', 'fox_q_norm_g', 'fox_k_norm_g', 'mem_kv_w', 'mem_q_norm_g', 'mem_k_norm_g', 'w_br_hgrn', 'w_br_fox', 'w_br_mem', 'w_out', 'norm_ffn_g', 'ffn_w_up', 'ffn_conv_w', 'ffn_conv_b', 'ffn_w_down', 'loss_target', 'm_norm_mix_g', 'm_norm_mem_g', 'm_w_in', 'm_hgrn_lb_logits', 'm_hgrn_norm_g', 'm_fox_f_bias', 'm_fox_q_norm_g', 'm_fox_k_norm_g', 'm_mem_kv_w', 'm_mem_q_norm_g', 'm_mem_k_norm_g', 'm_w_br_hgrn', 'm_w_br_fox', 'm_w_br_mem', 'm_w_out', 'm_norm_ffn_g', 'm_ffn_w_up', 'm_ffn_conv_w', 'm_ffn_conv_b', 'm_ffn_w_down', 'v_norm_mix_g', 'v_norm_mem_g', 'v_w_in', 'v_hgrn_lb_logits', 'v_hgrn_norm_g', 'v_fox_f_bias', 'v_fox_q_norm_g', 'v_fox_k_norm_g', 'v_mem_kv_w', 'v_mem_q_norm_g', 'v_mem_k_norm_g', 'v_w_br_hgrn', 'v_w_br_fox', 'v_w_br_mem', 'v_w_out', 'v_norm_ffn_g', 'v_ffn_w_up', 'v_ffn_conv_w', 'v_ffn_conv_b', 'v_ffn_w_down']
TWIN_OUTPUTS = ['loss', 'grad_x', 'grad_norm_mix_g', 'grad_norm_mem_g', 'grad_w_in', 'grad_hgrn_lb_logits', 'grad_hgrn_norm_g', 'grad_fox_f_bias', 'grad_fox_q_norm_g', 'grad_fox_k_norm_g', 'grad_mem_kv_w', 'grad_mem_q_norm_g', 'grad_mem_k_norm_g', 'grad_w_br_hgrn', 'grad_w_br_fox', 'grad_w_br_mem', 'grad_w_out', 'grad_norm_ffn_g', 'grad_ffn_w_up', 'grad_ffn_conv_w', 'grad_ffn_conv_b', 'grad_ffn_w_down', 'delta_norm_mix_g', 'delta_norm_mem_g', 'delta_w_in', 'delta_hgrn_lb_logits', 'delta_hgrn_norm_g', 'delta_fox_f_bias', 'delta_fox_q_norm_g', 'delta_fox_k_norm_g', 'delta_mem_kv_w', 'delta_mem_q_norm_g', 'delta_mem_k_norm_g', 'delta_w_br_hgrn', 'delta_w_br_fox', 'delta_w_br_mem', 'delta_w_out', 'delta_norm_ffn_g', 'delta_ffn_w_up', 'delta_ffn_conv_w', 'delta_ffn_conv_b', 'delta_ffn_w_down', 'new_m_norm_mix_g', 'new_m_norm_mem_g', 'new_m_w_in', 'new_m_hgrn_lb_logits', 'new_m_hgrn_norm_g', 'new_m_fox_f_bias', 'new_m_fox_q_norm_g', 'new_m_fox_k_norm_g', 'new_m_mem_kv_w', 'new_m_mem_q_norm_g', 'new_m_mem_k_norm_g', 'new_m_w_br_hgrn', 'new_m_w_br_fox', 'new_m_w_br_mem', 'new_m_w_out', 'new_m_norm_ffn_g', 'new_m_ffn_w_up', 'new_m_ffn_conv_w', 'new_m_ffn_conv_b', 'new_m_ffn_w_down', 'new_v_norm_mix_g', 'new_v_norm_mem_g', 'new_v_w_in', 'new_v_hgrn_lb_logits', 'new_v_hgrn_norm_g', 'new_v_fox_f_bias', 'new_v_fox_q_norm_g', 'new_v_fox_k_norm_g', 'new_v_mem_kv_w', 'new_v_mem_q_norm_g', 'new_v_mem_k_norm_g', 'new_v_w_br_hgrn', 'new_v_w_br_fox', 'new_v_w_br_mem', 'new_v_w_out', 'new_v_norm_ffn_g', 'new_v_ffn_w_up', 'new_v_ffn_conv_w', 'new_v_ffn_conv_b', 'new_v_ffn_w_down']
TWIN_LEAF_KINDS = {'loss': 'loss', 'grad_x': 'grad_x', 'grad_norm_mix_g': 'grad_w', 'grad_norm_mem_g': 'grad_w', 'grad_w_in': 'grad_w', 'grad_hgrn_lb_logits': 'grad_w', 'grad_hgrn_norm_g': 'grad_w', 'grad_fox_f_bias': 'grad_w', 'grad_fox_q_norm_g': 'grad_w', 'grad_fox_k_norm_g': 'grad_w', 'grad_mem_kv_w': 'grad_w', 'grad_mem_q_norm_g': 'grad_w', 'grad_mem_k_norm_g': 'grad_w', 'grad_w_br_hgrn': 'grad_w', 'grad_w_br_fox': 'grad_w', 'grad_w_br_mem': 'grad_w', 'grad_w_out': 'grad_w', 'grad_norm_ffn_g': 'grad_w', 'grad_ffn_w_up': 'grad_w', 'grad_ffn_conv_w': 'grad_w', 'grad_ffn_conv_b': 'grad_w', 'grad_ffn_w_down': 'grad_w', 'delta_norm_mix_g': 'delta_w', 'delta_norm_mem_g': 'delta_w', 'delta_w_in': 'delta_w', 'delta_hgrn_lb_logits': 'delta_w', 'delta_hgrn_norm_g': 'delta_w', 'delta_fox_f_bias': 'delta_w', 'delta_fox_q_norm_g': 'delta_w', 'delta_fox_k_norm_g': 'delta_w', 'delta_mem_kv_w': 'delta_w', 'delta_mem_q_norm_g': 'delta_w', 'delta_mem_k_norm_g': 'delta_w', 'delta_w_br_hgrn': 'delta_w', 'delta_w_br_fox': 'delta_w', 'delta_w_br_mem': 'delta_w', 'delta_w_out': 'delta_w', 'delta_norm_ffn_g': 'delta_w', 'delta_ffn_w_up': 'delta_w', 'delta_ffn_conv_w': 'delta_w', 'delta_ffn_conv_b': 'delta_w', 'delta_ffn_w_down': 'delta_w', 'new_m_norm_mix_g': 'new_m', 'new_m_norm_mem_g': 'new_m', 'new_m_w_in': 'new_m', 'new_m_hgrn_lb_logits': 'new_m', 'new_m_hgrn_norm_g': 'new_m', 'new_m_fox_f_bias': 'new_m', 'new_m_fox_q_norm_g': 'new_m', 'new_m_fox_k_norm_g': 'new_m', 'new_m_mem_kv_w': 'new_m', 'new_m_mem_q_norm_g': 'new_m', 'new_m_mem_k_norm_g': 'new_m', 'new_m_w_br_hgrn': 'new_m', 'new_m_w_br_fox': 'new_m', 'new_m_w_br_mem': 'new_m', 'new_m_w_out': 'new_m', 'new_m_norm_ffn_g': 'new_m', 'new_m_ffn_w_up': 'new_m', 'new_m_ffn_conv_w': 'new_m', 'new_m_ffn_conv_b': 'new_m', 'new_m_ffn_w_down': 'new_m', 'new_v_norm_mix_g': 'new_v', 'new_v_norm_mem_g': 'new_v', 'new_v_w_in': 'new_v', 'new_v_hgrn_lb_logits': 'new_v', 'new_v_hgrn_norm_g': 'new_v', 'new_v_fox_f_bias': 'new_v', 'new_v_fox_q_norm_g': 'new_v', 'new_v_fox_k_norm_g': 'new_v', 'new_v_mem_kv_w': 'new_v', 'new_v_mem_q_norm_g': 'new_v', 'new_v_mem_k_norm_g': 'new_v', 'new_v_w_br_hgrn': 'new_v', 'new_v_w_br_fox': 'new_v', 'new_v_w_br_mem': 'new_v', 'new_v_w_out': 'new_v', 'new_v_norm_ffn_g': 'new_v', 'new_v_ffn_w_up': 'new_v', 'new_v_ffn_conv_w': 'new_v', 'new_v_ffn_conv_b': 'new_v', 'new_v_ffn_w_down': 'new_v'}


def _forward(args):
    return _fwd_reference(*[args[k] for k in FWD_PARAMS])


def _output_shape():
    out = _jax.eval_shape(lambda: _forward(_fwd_setup_inputs(0)))
    return out.shape, out.dtype

N_MICROBATCH = 1
ADAM_LR = 0.001
ADAM_B1 = 0.9
ADAM_B2 = 0.999
ADAM_EPS = 1e-08
ADAM_WD = 0.01
ADAM_STEP = 10
PER_EXAMPLE_BATCH_AXIS = {'x': 0, 'mem': 0, 'loss_target': 0}
SHARED_INPUTS = []
_WEIGHT_DTYPES = {'norm_mix_g': _jnp.float32, 'norm_mem_g': _jnp.float32, 'w_in': _jnp.float32, 'hgrn_lb_logits': _jnp.float32, 'hgrn_norm_g': _jnp.float32, 'fox_f_bias': _jnp.float32, 'fox_q_norm_g': _jnp.float32, 'fox_k_norm_g': _jnp.float32, 'mem_kv_w': _jnp.float32, 'mem_q_norm_g': _jnp.float32, 'mem_k_norm_g': _jnp.float32, 'w_br_hgrn': _jnp.float32, 'w_br_fox': _jnp.float32, 'w_br_mem': _jnp.float32, 'w_out': _jnp.float32, 'norm_ffn_g': _jnp.float32, 'ffn_w_up': _jnp.float32, 'ffn_conv_w': _jnp.float32, 'ffn_conv_b': _jnp.float32, 'ffn_w_down': _jnp.float32}
MOMENT_SCALE = {'norm_mix_g': 1.490215e+01, 'norm_mem_g': 2.184070e-01, 'w_in': 2.102831e-01, 'hgrn_lb_logits': 3.359355e-02, 'hgrn_norm_g': 5.203239e+01, 'fox_f_bias': 1.716631e+02, 'fox_q_norm_g': 1.791737e+01, 'fox_k_norm_g': 1.796412e+01, 'mem_kv_w': 1.650886e-01, 'mem_q_norm_g': 1.236319e+00, 'mem_k_norm_g': 1.230699e+00, 'w_br_hgrn': 2.886982e-01, 'w_br_fox': 3.108331e-01, 'w_br_mem': 1.833595e-01, 'w_out': 4.528855e-01, 'norm_ffn_g': 5.776803e+01, 'ffn_w_up': 3.368803e-01, 'ffn_conv_w': 6.367743e+00, 'ffn_conv_b': 8.306776e+00, 'ffn_w_down': 5.671510e-01}


def _to_microbatches(a, axis):
    t = _jnp.moveaxis(a, axis, 0)
    t = t.reshape((N_MICROBATCH, t.shape[0] // N_MICROBATCH) + t.shape[1:])
    return _jnp.moveaxis(t, 1, axis + 1)


def setup_inputs(seed: int = 0) -> dict:
    inp = _fwd_setup_inputs(seed)
    key = _jax.random.fold_in(_jax.random.key(seed), 7919)
    shape, _ = _output_shape()
    out = dict(inp)
    out["loss_target"] = _jax.random.normal(_jax.random.fold_in(key, 0), shape, _jnp.float32)
    for i, name in enumerate(TWIN_WEIGHTS):
        w = inp[name].astype(_jnp.float32)
        if MOMENT_SCALE is None:
            s = _jnp.sqrt(_jnp.mean(_jnp.square(w)) + 1e-30)
        else:
            s = MOMENT_SCALE[name]
        km, kv = _jax.random.split(_jax.random.fold_in(key, i + 1))
        out[name] = w
        out["m_" + name] = s * _jax.random.normal(km, w.shape, _jnp.float32)
        out["v_" + name] = (s * s) * _jax.random.uniform(kv, w.shape, _jnp.float32, 0.5, 1.5)
    if N_MICROBATCH > 1:
        for name, axis in PER_EXAMPLE_BATCH_AXIS.items():
            out[name] = _to_microbatches(out[name], axis)
    return {'x': out['x'], 'mem': out['mem'], 'norm_mix_g': out['norm_mix_g'], 'norm_mem_g': out['norm_mem_g'], 'w_in': out['w_in'], 'hgrn_lb_logits': out['hgrn_lb_logits'], 'hgrn_norm_g': out['hgrn_norm_g'], 'fox_f_bias': out['fox_f_bias'], 'fox_q_norm_g': out['fox_q_norm_g'], 'fox_k_norm_g': out['fox_k_norm_g'], 'mem_kv_w': out['mem_kv_w'], 'mem_q_norm_g': out['mem_q_norm_g'], 'mem_k_norm_g': out['mem_k_norm_g'], 'w_br_hgrn': out['w_br_hgrn'], 'w_br_fox': out['w_br_fox'], 'w_br_mem': out['w_br_mem'], 'w_out': out['w_out'], 'norm_ffn_g': out['norm_ffn_g'], 'ffn_w_up': out['ffn_w_up'], 'ffn_conv_w': out['ffn_conv_w'], 'ffn_conv_b': out['ffn_conv_b'], 'ffn_w_down': out['ffn_w_down'], 'loss_target': out['loss_target'], 'm_norm_mix_g': out['m_norm_mix_g'], 'm_norm_mem_g': out['m_norm_mem_g'], 'm_w_in': out['m_w_in'], 'm_hgrn_lb_logits': out['m_hgrn_lb_logits'], 'm_hgrn_norm_g': out['m_hgrn_norm_g'], 'm_fox_f_bias': out['m_fox_f_bias'], 'm_fox_q_norm_g': out['m_fox_q_norm_g'], 'm_fox_k_norm_g': out['m_fox_k_norm_g'], 'm_mem_kv_w': out['m_mem_kv_w'], 'm_mem_q_norm_g': out['m_mem_q_norm_g'], 'm_mem_k_norm_g': out['m_mem_k_norm_g'], 'm_w_br_hgrn': out['m_w_br_hgrn'], 'm_w_br_fox': out['m_w_br_fox'], 'm_w_br_mem': out['m_w_br_mem'], 'm_w_out': out['m_w_out'], 'm_norm_ffn_g': out['m_norm_ffn_g'], 'm_ffn_w_up': out['m_ffn_w_up'], 'm_ffn_conv_w': out['m_ffn_conv_w'], 'm_ffn_conv_b': out['m_ffn_conv_b'], 'm_ffn_w_down': out['m_ffn_w_down'], 'v_norm_mix_g': out['v_norm_mix_g'], 'v_norm_mem_g': out['v_norm_mem_g'], 'v_w_in': out['v_w_in'], 'v_hgrn_lb_logits': out['v_hgrn_lb_logits'], 'v_hgrn_norm_g': out['v_hgrn_norm_g'], 'v_fox_f_bias': out['v_fox_f_bias'], 'v_fox_q_norm_g': out['v_fox_q_norm_g'], 'v_fox_k_norm_g': out['v_fox_k_norm_g'], 'v_mem_kv_w': out['v_mem_kv_w'], 'v_mem_q_norm_g': out['v_mem_q_norm_g'], 'v_mem_k_norm_g': out['v_mem_k_norm_g'], 'v_w_br_hgrn': out['v_w_br_hgrn'], 'v_w_br_fox': out['v_w_br_fox'], 'v_w_br_mem': out['v_w_br_mem'], 'v_w_out': out['v_w_out'], 'v_norm_ffn_g': out['v_norm_ffn_g'], 'v_ffn_w_up': out['v_ffn_w_up'], 'v_ffn_conv_w': out['v_ffn_conv_w'], 'v_ffn_conv_b': out['v_ffn_conv_b'], 'v_ffn_w_down': out['v_ffn_w_down']}


def _loss(weights, diff, rest, loss_target):
    with _jax.named_scope("forward"):
        args = {**rest, TWIN_DIFF_INPUT: diff, **{k: w.astype(_WEIGHT_DTYPES[k]) for k, w in weights.items()}}
        y = _forward(args)
    with _jax.named_scope("loss_head"):
        err = _jnp.square(y.astype(_jnp.float32) - loss_target)
        return 0.5 * _jnp.sum(_jnp.mean(err, axis=-1)) if err.ndim else 0.5 * err


def _adamw(w, g, m, v):
    m = ADAM_B1 * m + (1.0 - ADAM_B1) * g
    v = ADAM_B2 * v + (1.0 - ADAM_B2) * _jnp.square(g)
    m_hat = m / (1.0 - ADAM_B1 ** ADAM_STEP)
    v_hat = v / (1.0 - ADAM_B2 ** ADAM_STEP)
    delta = -ADAM_LR * (m_hat / (_jnp.sqrt(v_hat) + ADAM_EPS) + ADAM_WD * w)
    return delta, m, v


def reference(x, mem, norm_mix_g, norm_mem_g, w_in, hgrn_lb_logits, hgrn_norm_g, fox_f_bias, fox_q_norm_g, fox_k_norm_g, mem_kv_w, mem_q_norm_g, mem_k_norm_g, w_br_hgrn, w_br_fox, w_br_mem, w_out, norm_ffn_g, ffn_w_up, ffn_conv_w, ffn_conv_b, ffn_w_down, loss_target, m_norm_mix_g, m_norm_mem_g, m_w_in, m_hgrn_lb_logits, m_hgrn_norm_g, m_fox_f_bias, m_fox_q_norm_g, m_fox_k_norm_g, m_mem_kv_w, m_mem_q_norm_g, m_mem_k_norm_g, m_w_br_hgrn, m_w_br_fox, m_w_br_mem, m_w_out, m_norm_ffn_g, m_ffn_w_up, m_ffn_conv_w, m_ffn_conv_b, m_ffn_w_down, v_norm_mix_g, v_norm_mem_g, v_w_in, v_hgrn_lb_logits, v_hgrn_norm_g, v_fox_f_bias, v_fox_q_norm_g, v_fox_k_norm_g, v_mem_kv_w, v_mem_q_norm_g, v_mem_k_norm_g, v_w_br_hgrn, v_w_br_fox, v_w_br_mem, v_w_out, v_norm_ffn_g, v_ffn_w_up, v_ffn_conv_w, v_ffn_conv_b, v_ffn_w_down):
    given = dict(x=x, mem=mem, norm_mix_g=norm_mix_g, norm_mem_g=norm_mem_g, w_in=w_in, hgrn_lb_logits=hgrn_lb_logits, hgrn_norm_g=hgrn_norm_g, fox_f_bias=fox_f_bias, fox_q_norm_g=fox_q_norm_g, fox_k_norm_g=fox_k_norm_g, mem_kv_w=mem_kv_w, mem_q_norm_g=mem_q_norm_g, mem_k_norm_g=mem_k_norm_g, w_br_hgrn=w_br_hgrn, w_br_fox=w_br_fox, w_br_mem=w_br_mem, w_out=w_out, norm_ffn_g=norm_ffn_g, ffn_w_up=ffn_w_up, ffn_conv_w=ffn_conv_w, ffn_conv_b=ffn_conv_b, ffn_w_down=ffn_w_down, loss_target=loss_target, m_norm_mix_g=m_norm_mix_g, m_norm_mem_g=m_norm_mem_g, m_w_in=m_w_in, m_hgrn_lb_logits=m_hgrn_lb_logits, m_hgrn_norm_g=m_hgrn_norm_g, m_fox_f_bias=m_fox_f_bias, m_fox_q_norm_g=m_fox_q_norm_g, m_fox_k_norm_g=m_fox_k_norm_g, m_mem_kv_w=m_mem_kv_w, m_mem_q_norm_g=m_mem_q_norm_g, m_mem_k_norm_g=m_mem_k_norm_g, m_w_br_hgrn=m_w_br_hgrn, m_w_br_fox=m_w_br_fox, m_w_br_mem=m_w_br_mem, m_w_out=m_w_out, m_norm_ffn_g=m_norm_ffn_g, m_ffn_w_up=m_ffn_w_up, m_ffn_conv_w=m_ffn_conv_w, m_ffn_conv_b=m_ffn_conv_b, m_ffn_w_down=m_ffn_w_down, v_norm_mix_g=v_norm_mix_g, v_norm_mem_g=v_norm_mem_g, v_w_in=v_w_in, v_hgrn_lb_logits=v_hgrn_lb_logits, v_hgrn_norm_g=v_hgrn_norm_g, v_fox_f_bias=v_fox_f_bias, v_fox_q_norm_g=v_fox_q_norm_g, v_fox_k_norm_g=v_fox_k_norm_g, v_mem_kv_w=v_mem_kv_w, v_mem_q_norm_g=v_mem_q_norm_g, v_mem_k_norm_g=v_mem_k_norm_g, v_w_br_hgrn=v_w_br_hgrn, v_w_br_fox=v_w_br_fox, v_w_br_mem=v_w_br_mem, v_w_out=v_w_out, v_norm_ffn_g=v_norm_ffn_g, v_ffn_w_up=v_ffn_w_up, v_ffn_conv_w=v_ffn_conv_w, v_ffn_conv_b=v_ffn_conv_b, v_ffn_w_down=v_ffn_w_down)
    weights = {n: given[n] for n in TWIN_WEIGHTS}
    shared = {n: given[n] for n in SHARED_INPUTS}
    per_example = {n: given[n] for n in ['x', 'mem']}
    grad_fn = _jax.value_and_grad(_loss, argnums=(0, 1))

    def one_microbatch(ex, loss_target):
        ex = dict(ex)
        diff = ex.pop(TWIN_DIFF_INPUT)
        return grad_fn(weights, diff, {**shared, **ex}, loss_target)

    if N_MICROBATCH == 1:
        loss, (grad_w, grad_x) = one_microbatch(per_example, given["loss_target"])
    else:
        def body(carry, xs):
            loss_sum, grad_sum = carry
            l_k, (gw_k, gx_k) = one_microbatch(xs[0], xs[1])
            with _jax.named_scope("update"):
                return (loss_sum + l_k, _jax.tree.map(_jnp.add, grad_sum, gw_k)), gx_k

        init = (_jnp.zeros((), _jnp.float32), _jax.tree.map(_jnp.zeros_like, weights))
        (loss, grad_w), grad_x = _jax.lax.scan(body, init, (per_example, given["loss_target"]))
    with _jax.named_scope("update"):
        delta_w, new_m, new_v = {}, {}, {}
        for n in TWIN_WEIGHTS:
            delta_w[n], new_m[n], new_v[n] = _adamw(weights[n], grad_w[n], given["m_" + n], given["v_" + n])
    return (loss, grad_x, *[grad_w[n] for n in TWIN_WEIGHTS], *[delta_w[n] for n in TWIN_WEIGHTS],
            *[new_m[n] for n in TWIN_WEIGHTS], *[new_v[n] for n in TWIN_WEIGHTS])
```

```python
import functools
import math

import jax
import jax.numpy as jnp
from jax import lax
from jax.experimental import pallas as pl
from jax.experimental.pallas import tpu as pltpu

F32 = jnp.float32
BF16 = jnp.bfloat16
MXU_DTYPE = jnp.bfloat16

EPS = 1e-6
HG_HEADS, HG_D = 4, 128
FOX_HEADS, FOX_DH = 8, 64
MEM_HEADS, MEM_DH = 4, 128
HG_CHUNK = 64
FOX_BLOCK = 256
LANE = 128
FFN_GROUP = 256
FLAT_W = 1024
VMEM_LIMIT = 56 * 2 ** 20
NEG = -1e30
N_CHIPS = 4

ADAM_LR, ADAM_B1, ADAM_B2, ADAM_EPS, ADAM_WD, ADAM_STEP = 0.001, 0.9, 0.999, 1e-08, 0.01, 10

MESH = pl.DeviceIdType.MESH
ANY = pl.BlockSpec(memory_space=pl.ANY)


def _mx(x):
    return x.astype(MXU_DTYPE)


def _dot(a, b, ca, cb):
    return lax.dot_general(_mx(a), _mx(b), (((ca,), (cb,)), ((), ())), preferred_element_type=F32)


def _nn(a, b):
    return _dot(a, b, 1, 0)


def _nt(a, b):
    return _dot(a, b, 1, 1)


def _tn(a, b):
    return _dot(a, b, 0, 0)


def _dotp(a, b, ca, cb):
    return lax.dot_general(a, b, (((ca,), (cb,)), ((), ())), precision=lax.Precision.HIGHEST,
                           preferred_element_type=F32)


def _tri_dot(tri_bf, x):
    hi = x.astype(BF16)
    r = x - hi.astype(F32)
    mid = r.astype(BF16)
    lo = (r - mid.astype(F32)).astype(BF16)

    def d(v):
        return lax.dot_general(tri_bf, v, (((1,), (0,)), ((), ())), preferred_element_type=F32)

    return d(hi) + d(mid) + d(lo)


def _sig(x):
    return jax.nn.sigmoid(x)


def _erf(x):
    a = jnp.abs(x)
    t = 1.0 / (1.0 + 0.3275911 * a)
    poly = t * (0.254829592 + t * (-0.284496736 + t * (1.421413741 + t * (-1.453152027 + t * 1.061405429))))
    y = 1.0 - poly * jnp.exp(-a * a)
    return jnp.where(x < 0, -y, y)


def _tile(dim, pref, unit=LANE):
    if dim <= pref:
        return dim
    t = pref - pref % unit
    while t >= unit:
        if dim % t == 0:
            return t
        t -= unit
    return dim


def _params(n_grid):
    return pltpu.CompilerParams(dimension_semantics=("arbitrary",) * n_grid, vmem_limit_bytes=VMEM_LIMIT)


def _acc(ref, val, first):
    @pl.when(first)
    def _():
        ref[...] = val

    @pl.when(jnp.logical_not(first))
    def _():
        ref[...] += val


def _matmul(a, b, *, name, ta=False, tb=False, out_dtype=F32, tm=1024, tn=1024, tk=1024):
    m, k = (a.shape[1], a.shape[0]) if ta else a.shape
    n = b.shape[0] if tb else b.shape[1]
    tm, tn, tk = _tile(m, tm), _tile(n, tn), _tile(k, tk)
    nk = k // tk

    def body(a_ref, b_ref, o_ref, acc_ref):
        kk = pl.program_id(2)
        p = _dot(a_ref[...], b_ref[...], 0 if ta else 1, 1 if tb else 0)
        _acc(acc_ref, p, kk == 0)

        @pl.when(kk == nk - 1)
        def _():
            o_ref[...] = acc_ref[...].astype(o_ref.dtype)

    a_spec = pl.BlockSpec((tk, tm), lambda i, j, kk: (kk, i)) if ta else pl.BlockSpec((tm, tk), lambda i, j, kk: (i, kk))
    b_spec = pl.BlockSpec((tn, tk), lambda i, j, kk: (j, kk)) if tb else pl.BlockSpec((tk, tn), lambda i, j, kk: (kk, j))
    return pl.pallas_call(
        body, name=name, grid=(m // tm, n // tn, nk),
        in_specs=[a_spec, b_spec],
        out_specs=pl.BlockSpec((tm, tn), lambda i, j, kk: (i, j)),
        out_shape=jax.ShapeDtypeStruct((m, n), out_dtype),
        scratch_shapes=[pltpu.VMEM((tm, tn), F32)],
        compiler_params=_params(3),
    )(a, b)


def _rmsnorm_fwd(x, g, *, name, tm=512):
    n, d = x.shape
    tm = _tile(n, tm, 8)

    def body(x_ref, g_ref, o_ref):
        xv = x_ref[...]
        r = lax.rsqrt(jnp.mean(xv * xv, axis=-1, keepdims=True) + EPS)
        o_ref[...] = (xv * r * g_ref[...]).astype(o_ref.dtype)

    return pl.pallas_call(
        body, name=name, grid=(n // tm,),
        in_specs=[pl.BlockSpec((tm, d), lambda i: (i, 0)), pl.BlockSpec((1, d), lambda i: (0, 0))],
        out_specs=pl.BlockSpec((tm, d), lambda i: (i, 0)),
        out_shape=jax.ShapeDtypeStruct((n, d), MXU_DTYPE),
        compiler_params=_params(1),
    )(x, g)


def _rmsnorm_bwd(x, dhs, g, res, *, name, tm=512):
    n, d = x.shape
    tm = _tile(n, tm, 8)
    n_dh = len(dhs)
    has_res = res is not None

    def body(*refs):
        x_ref, dh_refs, g_ref = refs[0], refs[1:1 + n_dh], refs[1 + n_dh]
        res_ref = refs[2 + n_dh] if has_res else None
        dx_ref, dg_ref = refs[-2], refs[-1]
        xv = x_ref[...]
        dh = dh_refs[0][...].astype(F32)
        for r_ in dh_refs[1:]:
            dh = dh + r_[...].astype(F32)
        r = lax.rsqrt(jnp.mean(xv * xv, axis=-1, keepdims=True) + EPS)
        dhg = dh * g_ref[...]
        dx = r * dhg - xv * (r * r * r) * jnp.mean(dhg * xv, axis=-1, keepdims=True)
        if has_res:
            dx = dx + res_ref[...]
        dx_ref[...] = dx
        _acc(dg_ref, jnp.sum(dh * xv * r, axis=0, keepdims=True), pl.program_id(0) == 0)

    row = pl.BlockSpec((tm, d), lambda i: (i, 0))
    vec = pl.BlockSpec((1, d), lambda i: (0, 0))
    ins = [x] + list(dhs) + [g] + ([res] if has_res else [])
    return pl.pallas_call(
        body, name=name, grid=(n // tm,),
        in_specs=[row] * (1 + n_dh) + [vec] + ([row] if has_res else []),
        out_specs=[row, vec],
        out_shape=[jax.ShapeDtypeStruct((n, d), F32), jax.ShapeDtypeStruct((1, d), F32)],
        compiler_params=_params(1),
    )(*ins)


def _adamw(w, g, m, v, *, name, tr=256):
    r, c = w.shape
    tr = _tile(r, tr, 8)
    c1 = 1.0 / (1.0 - ADAM_B1 ** ADAM_STEP)
    c2 = 1.0 / (1.0 - ADAM_B2 ** ADAM_STEP)

    def body(w_ref, g_ref, m_ref, v_ref, d_ref, mo_ref, vo_ref):
        gv = g_ref[...]
        mn = ADAM_B1 * m_ref[...] + (1.0 - ADAM_B1) * gv
        vn = ADAM_B2 * v_ref[...] + (1.0 - ADAM_B2) * (gv * gv)
        d_ref[...] = -ADAM_LR * ((mn * c1) / (jnp.sqrt(vn * c2) + ADAM_EPS) + ADAM_WD * w_ref[...])
        mo_ref[...] = mn
        vo_ref[...] = vn

    blk = pl.BlockSpec((tr, c), lambda i: (i, 0))
    sds = jax.ShapeDtypeStruct((r, c), F32)
    return pl.pallas_call(
        body, name=name, grid=(r // tr,), in_specs=[blk] * 4, out_specs=[blk] * 3, out_shape=[sds] * 3,
        compiler_params=_params(1),
    )(w, g, m, v)


def _hgrn_chunk(hq, hf, hi, lbv, tril, tril_bf, st):
    c = hq.shape[0]
    sf = _sig(hf)
    f = lbv + (1.0 - lbv) * sf
    k = 1.0 - f
    gcum = _tri_dot(tril_bf, jnp.log(f))
    mid = gcum[c // 2 - 1:c // 2, :]
    glast = gcum[c - 1:c, :]
    sq = _sig(hq)
    q = hq * sq
    e_q = jnp.exp(gcum - mid)
    e_k = jnp.exp(mid - gcum)
    qe, ke = q * e_q, k * e_k
    a = jnp.where(tril, _nt(qe, ke), 0.0)
    e_g = jnp.exp(gcum)
    qg = q * e_g
    o = _nn(a, hi) + _nt(qg, st)
    e_s = jnp.exp(glast - gcum)
    kg = k * e_s
    e_l = jnp.exp(glast)
    st_new = st * e_l + _tn(hi, kg)
    return dict(sf=sf, f=f, k=k, sq=sq, q=q, e_q=e_q, e_k=e_k, qe=qe, ke=ke, a=a, e_g=e_g, qg=qg, o=o,
                e_s=e_s, kg=kg, e_l=e_l, st_new=st_new)


def _hgrn_specs(b_, t_, hw):
    nb = hw // LANE

    def col(off):
        return pl.BlockSpec((1, t_, LANE), lambda h, b: (b, 0, off * nb + h))

    vec = pl.BlockSpec((2, LANE), lambda h, b: (0, h))
    one = pl.BlockSpec((1, LANE), lambda h, b: (0, 0))
    return col, vec, one


def _hgrn_fwd(zm, lb, gn, hw):
    b_, t_, _ = zm.shape
    c = min(HG_CHUNK, t_)
    nc = t_ // c
    col, vec, one = _hgrn_specs(b_, t_, hw)

    def body(q_ref, f_ref, i_ref, g_ref, lb_ref, gn_ref, y_ref):
        lbv, gnv = _sig(lb_ref[0:1, :] - lb_ref[1:2, :]), gn_ref[...]
        tril = lax.broadcasted_iota(jnp.int32, (c, c), 0) >= lax.broadcasted_iota(jnp.int32, (c, c), 1)
        tril_bf = tril.astype(BF16)

        def chunk(n, st):
            rows = pl.ds(pl.multiple_of(n * c, c), c)
            p = _hgrn_chunk(q_ref[0, rows, :], f_ref[0, rows, :], i_ref[0, rows, :], lbv, tril, tril_bf, st)
            o = p["o"]
            r = lax.rsqrt(jnp.mean(o * o, axis=-1, keepdims=True) + EPS)
            hg = g_ref[0, rows, :]
            y_ref[0, rows, :] = o * r * gnv * (hg * _sig(hg))
            return p["st_new"]

        lax.fori_loop(0, nc, chunk, jnp.zeros((HG_D, HG_D), F32))

    return pl.pallas_call(
        body, name="hgrn_fwd", grid=(HG_HEADS, b_),
        in_specs=[col(0), col(1), col(2), col(3), vec, one],
        out_specs=pl.BlockSpec((1, t_, LANE), lambda h, b: (b, 0, h)),
        out_shape=jax.ShapeDtypeStruct((b_, t_, hw), F32),
        compiler_params=_params(2),
    )(zm, zm, zm, zm, lb, gn)


def _hgrn_bwd(zm, dy, lb, gn, hw):
    b_, t_, _ = zm.shape
    c = min(HG_CHUNK, t_)
    nc = t_ // c
    col, vec, one = _hgrn_specs(b_, t_, hw)

    def body(q_ref, f_ref, i_ref, g_ref, dy_ref, lb_ref, gn_ref, dq_ref, df_ref, di_ref, dg_ref, dlb_ref, dgn_ref,
             st_all):
        h, b = pl.program_id(0), pl.program_id(1)
        lbv, gnv = _sig(lb_ref[0:1, :] - lb_ref[1:2, :]), gn_ref[...]
        row = lax.broadcasted_iota(jnp.int32, (c, c), 0)
        cl = lax.broadcasted_iota(jnp.int32, (c, c), 1)
        tril = row >= cl
        tril_bf = tril.astype(BF16)
        triu_bf = (row <= cl).astype(BF16)
        last_row = lax.broadcasted_iota(jnp.int32, (c, LANE), 0) == c - 1

        def fwd(n, st):
            rows = pl.ds(pl.multiple_of(n * c, c), c)
            st_all[n] = st
            return _hgrn_chunk(q_ref[0, rows, :], f_ref[0, rows, :], i_ref[0, rows, :], lbv, tril, tril_bf, st)["st_new"]

        lax.fori_loop(0, nc, fwd, jnp.zeros((HG_D, HG_D), F32))

        def bwd(m, carry):
            dst, dlb, dgn = carry
            n = nc - 1 - m
            rows = pl.ds(pl.multiple_of(n * c, c), c)
            hq, hi, hg = q_ref[0, rows, :], i_ref[0, rows, :], g_ref[0, rows, :]
            st = st_all[n]
            p = _hgrn_chunk(hq, f_ref[0, rows, :], hi, lbv, tril, tril_bf, st)
            o, q, k = p["o"], p["q"], p["k"]
            dyv = dy_ref[0, rows, :]
            sg = _sig(hg)
            r = lax.rsqrt(jnp.mean(o * o, axis=-1, keepdims=True) + EPS)
            nrm = o * r * gnv
            dn = dyv * (hg * sg)
            dg_ref[0, rows, :] = (dyv * nrm * (sg * (1.0 + hg * (1.0 - sg)))).astype(dg_ref.dtype)
            dgn = dgn + jnp.sum(dn * o * r, axis=0, keepdims=True)
            dng = dn * gnv
            do = r * dng - o * (r * r * r) * jnp.mean(dng * o, axis=-1, keepdims=True)
            da = jnp.where(tril, _dotp(do, hi, 1, 1), 0.0)
            dq = _dotp(da, p["ke"], 1, 0) * p["e_q"] + _dotp(do, st, 1, 0) * p["e_g"]
            dkg = _dotp(hi, dst, 1, 0)
            dk_state = dkg * p["e_s"]
            dk = _dotp(da, p["qe"], 0, 0) * p["e_k"] + dk_state
            di_ref[0, rows, :] = (_tn(p["a"], do) + _nt(p["kg"], dst)).astype(di_ref.dtype)
            dgc = q * dq - k * dk
            extra = (jnp.sum(k * dk_state, axis=0, keepdims=True)
                     + p["e_l"] * jnp.sum(st * dst, axis=0, keepdims=True))
            dgc = dgc + jnp.where(last_row, extra, 0.0)
            dlf = _tri_dot(triu_bf, dgc)
            dfv = dlf / p["f"] - dk
            sf, sq = p["sf"], p["sq"]
            df_ref[0, rows, :] = (dfv * (1.0 - lbv) * sf * (1.0 - sf)).astype(df_ref.dtype)
            dlb = dlb + jnp.sum(dfv * (1.0 - sf), axis=0, keepdims=True)
            dq_ref[0, rows, :] = (dq * (sq * (1.0 + hq * (1.0 - sq)))).astype(dq_ref.dtype)
            dst = dst * p["e_l"] + _dotp(do, p["qg"], 0, 0)
            return dst, dlb, dgn

        z1 = jnp.zeros((1, LANE), F32)
        _, dlb, dgn = lax.fori_loop(0, nc, bwd, (jnp.zeros((HG_D, HG_D), F32), z1, z1))
        dl0 = dlb * lbv * (1.0 - lbv)
        _acc(dlb_ref, jnp.concatenate([dl0, -dl0], axis=0), b == 0)
        _acc(dgn_ref, dgn, jnp.logical_and(b == 0, h == 0))

    blk = pl.BlockSpec((1, t_, LANE), lambda h, b: (b, 0, h))
    sds = jax.ShapeDtypeStruct((b_, t_, hw), MXU_DTYPE)
    return pl.pallas_call(
        body, name="hgrn_bwd", grid=(HG_HEADS, b_),
        in_specs=[col(0), col(1), col(2), col(3), blk, vec, one],
        out_specs=[blk, blk, blk, blk, vec, one],
        out_shape=[sds, sds, sds, sds, jax.ShapeDtypeStruct((2, hw), F32), jax.ShapeDtypeStruct((1, LANE), F32)],
        scratch_shapes=[pltpu.VMEM((nc, HG_D, HG_D), F32)],
        compiler_params=_params(2),
    )(zm, zm, zm, zm, dy, lb, gn)


def _fox_logf(x):
    return jnp.minimum(x, 0.0) - jnp.log(1.0 + jnp.exp(-jnp.abs(x)))


def _fox_prep(zf, bias):
    b_, t_, _ = zf.shape
    tb = min(FOX_BLOCK, t_)
    nb = t_ // tb

    def body(z_ref, b_ref, fc_ref):
        tril_bf = (lax.broadcasted_iota(jnp.int32, (tb, tb), 0) >= lax.broadcasted_iota(jnp.int32, (tb, tb), 1)).astype(BF16)
        bv = b_ref[...]

        def blk(i, carry):
            rows = pl.ds(pl.multiple_of(i * tb, tb), tb)
            fc = _tri_dot(tril_bf, _fox_logf(z_ref[0, rows, :] + bv)) + carry
            fc_ref[0, rows, :] = fc
            return fc[tb - 1:tb, :]

        lax.fori_loop(0, nb, blk, jnp.zeros((1, LANE), F32))

    blk_spec = pl.BlockSpec((1, t_, LANE), lambda b: (b, 0, 0))
    return pl.pallas_call(
        body, name="fox_prep", grid=(b_,),
        in_specs=[blk_spec, pl.BlockSpec((1, LANE), lambda b: (0, 0))], out_specs=blk_spec,
        out_shape=jax.ShapeDtypeStruct((b_, t_, LANE), F32), compiler_params=_params(1),
    )(zf, bias)


def _fox_post(dfc, zf, bias):
    b_, t_, _ = zf.shape
    tb = min(FOX_BLOCK, t_)
    nb = t_ // tb

    def body(d_ref, z_ref, b_ref, dz_ref, db_ref):
        triu_bf = (lax.broadcasted_iota(jnp.int32, (tb, tb), 0) <= lax.broadcasted_iota(jnp.int32, (tb, tb), 1)).astype(BF16)
        valid = lax.broadcasted_iota(jnp.int32, (tb, LANE), 1) < FOX_HEADS
        bv = b_ref[...]

        def blk(m, carry):
            tail, db = carry
            rows = pl.ds(pl.multiple_of((nb - 1 - m) * tb, tb), tb)
            dlf = _tri_dot(triu_bf, d_ref[0, rows, :]) + tail
            dx = jnp.where(valid, dlf * _sig(-(z_ref[0, rows, :] + bv)), 0.0)
            dz_ref[0, rows, :] = dx.astype(dz_ref.dtype)
            return dlf[0:1, :], db + jnp.sum(dx, axis=0, keepdims=True)

        z1 = jnp.zeros((1, LANE), F32)
        _, db = lax.fori_loop(0, nb, blk, (z1, z1))
        _acc(db_ref, db, pl.program_id(0) == 0)

    blk_spec = pl.BlockSpec((1, t_, LANE), lambda b: (b, 0, 0))
    vec = pl.BlockSpec((1, LANE), lambda b: (0, 0))
    return pl.pallas_call(
        body, name="fox_post", grid=(b_,), in_specs=[blk_spec, blk_spec, vec], out_specs=[blk_spec, vec],
        out_shape=[jax.ShapeDtypeStruct((b_, t_, LANE), MXU_DTYPE), jax.ShapeDtypeStruct((1, LANE), F32)],
        compiler_params=_params(1),
    )(dfc, zf, bias)


def _fox_specs(t_, fw, col0):
    npair = fw // LANE

    def col(off):
        return pl.BlockSpec((1, t_, LANE), lambda b, p: (b, 0, col0 + off * npair + p))

    pair = pl.BlockSpec((1, t_, LANE), lambda b, p: (b, 0, p))
    full = pl.BlockSpec((1, t_, LANE), lambda b, p: (b, 0, 0))
    rowf = pl.BlockSpec((1, FOX_HEADS, t_), lambda b, p: (b, 0, 0))
    gvec = pl.BlockSpec((1, FOX_DH), lambda b, p: (0, 0))
    lse = pl.BlockSpec((1, 1, t_, LANE), lambda b, p: (b, p, 0, 0))
    return col, pair, full, rowf, gvec, lse


def _fox_fwd(zm, fc, fct, gq, gk, fw, col0):
    b_, t_, _ = zm.shape
    npair = fw // LANE
    tq = min(FOX_BLOCK, t_)
    nb = t_ // tq
    dh = FOX_DH
    scale = dh ** -0.5
    col, pair, full, rowf, gvec, lse_spec = _fox_specs(t_, fw, col0)

    def body(q_ref, k_ref, v_ref, fc_ref, fct_ref, gq_ref, gk_ref, o_ref, lse_ref, qs, ks, vs):
        p = pl.program_id(1)
        gqv, gkv = gq_ref[...] * scale, gk_ref[...]
        lane = lax.broadcasted_iota(jnp.int32, (tq, LANE), 1)
        ri = lax.broadcasted_iota(jnp.int32, (tq, tq), 0)
        ci = lax.broadcasted_iota(jnp.int32, (tq, tq), 1)
        lse_ref[...] = jnp.zeros(lse_ref.shape, F32)
        for hh in range(2):
            sl = slice(hh * dh, (hh + 1) * dh)
            qv, kv = q_ref[0, :, sl], k_ref[0, :, sl]
            qs[hh] = (qv * lax.rsqrt(jnp.mean(qv * qv, axis=-1, keepdims=True) + EPS) * gqv).astype(MXU_DTYPE)
            ks[hh] = (kv * lax.rsqrt(jnp.mean(kv * kv, axis=-1, keepdims=True) + EPS) * gkv).astype(MXU_DTYPE)
            vs[hh] = v_ref[0, :, sl].astype(MXU_DTYPE)
        for hh in range(2):
            sl = slice(hh * dh, (hh + 1) * dh)
            h = 2 * p + hh

            def qblock(i, _, hh=hh, sl=sl, h=h):
                r0 = pl.multiple_of(i * tq, tq)
                rows = pl.ds(r0, tq)
                qb = qs[hh, rows, :]
                fcol = jnp.sum(jnp.where(lane == h, fc_ref[0, rows, :], 0.0), axis=-1, keepdims=True)

                def kvblock(j, carry):
                    m, l, acc = carry
                    c0 = pl.multiple_of(j * tq, tq)
                    cols = pl.ds(c0, tq)
                    s = _nt(qb, ks[hh, cols, :]) + fcol - fct_ref[0, pl.ds(h, 1), cols]
                    s = jnp.where(r0 + ri >= c0 + ci, s, NEG)
                    m2 = jnp.maximum(m, jnp.max(s, axis=-1, keepdims=True))
                    al = jnp.exp(m - m2)
                    pm = jnp.exp(s - m2)
                    return m2, al * l + jnp.sum(pm, axis=-1, keepdims=True), al * acc + _nn(pm, vs[hh, cols, :])

                m, l, acc = lax.fori_loop(0, i + 1, kvblock, (jnp.full((tq, 1), NEG, F32), jnp.zeros((tq, 1), F32),
                                                              jnp.zeros((tq, dh), F32)))
                o_ref[0, rows, sl] = acc / l
                lse_ref[0, 0, rows, hh:hh + 1] = m + jnp.log(l)
                return 0

            lax.fori_loop(0, nb, qblock, 0)

    return pl.pallas_call(
        body, name="fox_fwd", grid=(b_, npair),
        in_specs=[col(0), col(1), col(2), full, rowf, gvec, gvec],
        out_specs=[pair, lse_spec],
        out_shape=[jax.ShapeDtypeStruct((b_, t_, fw), F32), jax.ShapeDtypeStruct((b_, npair, t_, LANE), F32)],
        scratch_shapes=[pltpu.VMEM((2, t_, dh), MXU_DTYPE)] * 3,
        compiler_params=_params(2),
    )(zm, zm, zm, fc, fct, gq, gk)


def _norm_bwd(x, dy, g):
    r = lax.rsqrt(jnp.mean(x * x, axis=-1, keepdims=True) + EPS)
    dyg = dy * g
    dx = r * dyg - x * (r * r * r) * jnp.mean(dyg * x, axis=-1, keepdims=True)
    return dx, jnp.sum(dy * x * r, axis=0, keepdims=True)


def _fox_bwd(zm, o, do, lse, fc, fct, gq, gk, fw, col0):
    b_, t_, _ = zm.shape
    npair = fw // LANE
    tq = min(FOX_BLOCK, t_)
    nb = t_ // tq
    dh = FOX_DH
    scale = dh ** -0.5
    col, pair, full, rowf, gvec, lse_spec = _fox_specs(t_, fw, col0)

    def body(q_ref, k_ref, v_ref, o_ref, do_ref, lse_ref, fc_ref, fct_ref, gq_ref, gk_ref,
             dq_ref, dk_ref, dv_ref, dfct_ref, dfr_ref, dgq_ref, dgk_ref, qs, ks, vs, dos, cvec, dq_acc, dk_acc):
        b, p = pl.program_id(0), pl.program_id(1)
        gqv, gkv = gq_ref[...] * scale, gk_ref[...]
        lane = lax.broadcasted_iota(jnp.int32, (t_, LANE), 1)
        ri = lax.broadcasted_iota(jnp.int32, (tq, tq), 0)
        ci = lax.broadcasted_iota(jnp.int32, (tq, tq), 1)
        dfr_ref[...] = jnp.zeros(dfr_ref.shape, F32)
        for hh in range(2):
            sl = slice(hh * dh, (hh + 1) * dh)
            qv, kv = q_ref[0, :, sl], k_ref[0, :, sl]
            qs[hh] = (qv * lax.rsqrt(jnp.mean(qv * qv, axis=-1, keepdims=True) + EPS) * gqv).astype(MXU_DTYPE)
            ks[hh] = (kv * lax.rsqrt(jnp.mean(kv * kv, axis=-1, keepdims=True) + EPS) * gkv).astype(MXU_DTYPE)
            vs[hh] = v_ref[0, :, sl].astype(MXU_DTYPE)
            dos[hh] = do_ref[0, :, sl].astype(MXU_DTYPE)
        dgq = jnp.zeros((1, dh), F32)
        dgk = jnp.zeros((1, dh), F32)
        for hh in range(2):
            sl = slice(hh * dh, (hh + 1) * dh)
            h = 2 * p + hh
            cvec[0] = jnp.sum(jnp.where(lane == h, fc_ref[0], 0.0), axis=-1, keepdims=True)
            cvec[1] = lse_ref[0, 0, :, hh:hh + 1]
            cvec[2] = jnp.sum(dos[hh].astype(F32) * o_ref[0, :, sl], axis=-1, keepdims=True)
            cvec[3] = jnp.zeros((t_, 1), F32)
            dq_acc[...] = jnp.zeros((t_, dh), F32)

            def kvblock(j, _, hh=hh, sl=sl, h=h):
                c0 = pl.multiple_of(j * tq, tq)
                cols = pl.ds(c0, tq)
                kb, vb = ks[hh, cols, :], vs[hh, cols, :]
                frow = fct_ref[0, pl.ds(h, 1), cols]

                def qblock(i, carry):
                    dk_a, dv_a, dfr = carry
                    r0 = pl.multiple_of(i * tq, tq)
                    rows = pl.ds(r0, tq)
                    qb, dob = qs[hh, rows, :], dos[hh, rows, :]
                    s = _nt(qb, kb) + cvec[0, rows, :] - frow
                    pm = jnp.where(r0 + ri >= c0 + ci, jnp.exp(s - cvec[1, rows, :]), 0.0)
                    ds = pm * (_nt(dob, vb) - cvec[2, rows, :])
                    dq_acc[rows, :] += _nn(ds, kb)
                    cvec[3, rows, :] += jnp.sum(ds, axis=-1, keepdims=True)
                    return dk_a + _tn(ds, qb), dv_a + _tn(pm, dob), dfr + jnp.sum(ds, axis=0, keepdims=True)

                z = jnp.zeros((tq, dh), F32)
                dk_a, dv_a, dfr = lax.fori_loop(j, nb, qblock, (z, z, jnp.zeros((1, tq), F32)))
                dk_acc[cols, :] = dk_a
                dv_ref[0, cols, sl] = dv_a.astype(dv_ref.dtype)
                dfct_ref[0, pl.ds(h, 1), cols] = -dfr
                return 0

            lax.fori_loop(0, nb, kvblock, 0)
            dfr_ref[0, 0, :, hh:hh + 1] = cvec[3]
            dqv, gq_part = _norm_bwd(q_ref[0, :, sl], dq_acc[...], gqv)
            dkv, gk_part = _norm_bwd(k_ref[0, :, sl], dk_acc[...], gkv)
            dq_ref[0, :, sl] = dqv.astype(dq_ref.dtype)
            dk_ref[0, :, sl] = dkv.astype(dk_ref.dtype)
            dgq = dgq + gq_part * scale
            dgk = dgk + gk_part
        first = jnp.logical_and(b == 0, p == 0)
        _acc(dgq_ref, dgq, first)
        _acc(dgk_ref, dgk, first)

    sds = jax.ShapeDtypeStruct((b_, t_, fw), MXU_DTYPE)
    gs = jax.ShapeDtypeStruct((1, dh), F32)
    return pl.pallas_call(
        body, name="fox_bwd", grid=(b_, npair),
        in_specs=[col(0), col(1), col(2), pair, pair, lse_spec, full, rowf, gvec, gvec],
        out_specs=[pair, pair, pair, rowf, lse_spec, gvec, gvec],
        out_shape=[sds, sds, sds, jax.ShapeDtypeStruct((b_, FOX_HEADS, t_), F32),
                   jax.ShapeDtypeStruct((b_, npair, t_, LANE), F32), gs, gs],
        scratch_shapes=[pltpu.VMEM((2, t_, dh), MXU_DTYPE)] * 4
        + [pltpu.VMEM((4, t_, 1), F32), pltpu.VMEM((t_, dh), F32), pltpu.VMEM((t_, dh), F32)],
        compiler_params=_params(2),
    )(zm, zm, zm, o, do, lse, fc, fct, gq, gk)


def _mem_specs(t_, m_, mw, col0):
    nh = mw // LANE
    qcol = pl.BlockSpec((1, t_, LANE), lambda b, h: (b, 0, col0 + h))
    kcol = pl.BlockSpec((1, m_, LANE), lambda b, h: (b, 0, h))
    vcol = pl.BlockSpec((1, m_, LANE), lambda b, h: (b, 0, nh + h))
    ycol = pl.BlockSpec((1, t_, LANE), lambda b, h: (b, 0, h))
    gvec = pl.BlockSpec((1, LANE), lambda b, h: (0, 0))
    return qcol, kcol, vcol, ycol, gvec


def _mem_fwd(zm, mkv, gq, gk, mw, col0):
    b_, t_, _ = zm.shape
    m_ = mkv.shape[1]
    tq = min(512, t_)
    nb = t_ // tq
    scale = MEM_DH ** -0.5
    qcol, kcol, vcol, ycol, gvec = _mem_specs(t_, m_, mw, col0)

    def body(q_ref, k_ref, v_ref, gq_ref, gk_ref, y_ref):
        gqv, gkv = gq_ref[...] * scale, gk_ref[...]
        kv = k_ref[0]
        kn = _mx(kv * lax.rsqrt(jnp.mean(kv * kv, axis=-1, keepdims=True) + EPS) * gkv)
        vv = _mx(v_ref[0])

        def blk(i, _):
            rows = pl.ds(pl.multiple_of(i * tq, tq), tq)
            qv = q_ref[0, rows, :]
            s = _nt(qv * lax.rsqrt(jnp.mean(qv * qv, axis=-1, keepdims=True) + EPS) * gqv, kn)
            e = jnp.exp(s - jnp.max(s, axis=-1, keepdims=True))
            y_ref[0, rows, :] = _nn(e / jnp.sum(e, axis=-1, keepdims=True), vv)
            return 0

        lax.fori_loop(0, nb, blk, 0)

    return pl.pallas_call(
        body, name="mem_fwd", grid=(b_, MEM_HEADS), in_specs=[qcol, kcol, vcol, gvec, gvec], out_specs=ycol,
        out_shape=jax.ShapeDtypeStruct((b_, t_, mw), F32), compiler_params=_params(2),
    )(zm, mkv, mkv, gq, gk)


def _mem_bwd(zm, mkv, dy, gq, gk, mw, col0):
    b_, t_, _ = zm.shape
    m_ = mkv.shape[1]
    tq = min(512, t_)
    nb = t_ // tq
    scale = MEM_DH ** -0.5
    qcol, kcol, vcol, ycol, gvec = _mem_specs(t_, m_, mw, col0)

    def body(q_ref, k_ref, v_ref, dy_ref, gq_ref, gk_ref, dq_ref, dk_ref, dv_ref, dgq_ref, dgk_ref):
        gqv, gkv = gq_ref[...] * scale, gk_ref[...]
        kv = k_ref[0]
        kn = _mx(kv * lax.rsqrt(jnp.mean(kv * kv, axis=-1, keepdims=True) + EPS) * gkv)
        vv = _mx(v_ref[0])

        def blk(i, carry):
            dkn, dvv, dgq = carry
            rows = pl.ds(pl.multiple_of(i * tq, tq), tq)
            qv = q_ref[0, rows, :]
            qn = _mx(qv * lax.rsqrt(jnp.mean(qv * qv, axis=-1, keepdims=True) + EPS) * gqv)
            s = _nt(qn, kn)
            e = jnp.exp(s - jnp.max(s, axis=-1, keepdims=True))
            pm = e / jnp.sum(e, axis=-1, keepdims=True)
            dob = _mx(dy_ref[0, rows, :])
            dp = _nt(dob, vv)
            ds = pm * (dp - jnp.sum(dp * pm, axis=-1, keepdims=True))
            dqv, gq_part = _norm_bwd(qv, _nn(ds, kn), gqv)
            dq_ref[0, rows, :] = dqv.astype(dq_ref.dtype)
            return dkn + _tn(ds, qn), dvv + _tn(pm, dob), dgq + gq_part * scale

        z = jnp.zeros((m_, LANE), F32)
        dkn, dvv, dgq = lax.fori_loop(0, nb, blk, (z, z, jnp.zeros((1, LANE), F32)))
        dkv, dgk = _norm_bwd(kv, dkn, gkv)
        dk_ref[0] = dkv
        dv_ref[0] = dvv
        first = jnp.logical_and(pl.program_id(0) == 0, pl.program_id(1) == 0)
        _acc(dgq_ref, dgq, first)
        _acc(dgk_ref, dgk, first)

    kblk = pl.BlockSpec((1, m_, LANE), lambda b, h: (b, 0, h))
    gs = jax.ShapeDtypeStruct((1, LANE), F32)
    ks = jax.ShapeDtypeStruct((b_, m_, mw), F32)
    return pl.pallas_call(
        body, name="mem_bwd", grid=(b_, MEM_HEADS), in_specs=[qcol, kcol, vcol, ycol, gvec, gvec],
        out_specs=[ycol, kblk, kblk, gvec, gvec],
        out_shape=[jax.ShapeDtypeStruct((b_, t_, mw), MXU_DTYPE), ks, ks, gs, gs], compiler_params=_params(2),
    )(zm, mkv, mkv, dy, gq, gk)


def _merge_specs(tm, d, w, gcol):
    row_d = pl.BlockSpec((tm, d), lambda i: (i, 0))
    row_w = pl.BlockSpec((tm, w), lambda i: (i, 0))
    gates = [pl.BlockSpec((tm, d), functools.partial(lambda i, k: (i, gcol + k), k=k)) for k in range(3)]
    w_br = pl.BlockSpec((w, d), lambda i: (0, 0))
    w_o = pl.BlockSpec((d, d), lambda i: (0, 0))
    return row_d, row_w, gates, w_br, w_o


def _merge_fwd(x, ys, zm, w_brs, w_out, gcol, tm=256):
    n, d = x.shape
    w = ys[0].shape[1]
    tm = _tile(n, tm, 8)
    row_d, row_w, gates, w_br, w_o = _merge_specs(tm, d, w, gcol)

    def body(x_ref, ya, yb, yc, g0, g1, g2, wa, wb, wc, wo, x1_ref, mg_ref):
        mg = (_sig(g0[...]) * _nn(ya[...], wa[...]) + _sig(g1[...]) * _nn(yb[...], wb[...])
              + _sig(g2[...]) * _nn(yc[...], wc[...]))
        mg_ref[...] = mg.astype(mg_ref.dtype)
        x1_ref[...] = x_ref[...] + _nn(mg, wo[...])

    return pl.pallas_call(
        body, name="merge_fwd", grid=(n // tm,),
        in_specs=[row_d, row_w, row_w, row_w] + gates + [w_br, w_br, w_br, w_o],
        out_specs=[row_d, row_d],
        out_shape=[jax.ShapeDtypeStruct((n, d), F32), jax.ShapeDtypeStruct((n, d), MXU_DTYPE)],
        compiler_params=_params(1),
    )(x, *ys, zm, zm, zm, *w_brs, w_out)


def _merge_bwd(dx1, ys, zm, w_brs, w_out, gcol, tm=256):
    n, d = dx1.shape
    w = ys[0].shape[1]
    tm = _tile(n, tm, 8)
    row_d, row_w, gates, w_br, w_o = _merge_specs(tm, d, w, gcol)

    def body(dx_ref, ya, yb, yc, g0, g1, g2, wa, wb, wc, wo, dgl_ref, dpa, dpb, dpc, dya, dyb, dyc):
        dm = _nt(dx_ref[...], wo[...])
        for k, (y, g, wr, dp_ref, dy_ref) in enumerate(((ya, g0, wa, dpa, dya), (yb, g1, wb, dpb, dyb),
                                                        (yc, g2, wc, dpc, dyc))):
            sg = _sig(g[...])
            pr = _nn(y[...], wr[...])
            dgl_ref[:, k * d:(k + 1) * d] = (dm * pr * sg * (1.0 - sg)).astype(dgl_ref.dtype)
            dp = (dm * sg).astype(dp_ref.dtype)
            dp_ref[...] = dp
            dy_ref[...] = _nt(dp, wr[...])

    sd = jax.ShapeDtypeStruct((n, d), MXU_DTYPE)
    sw = jax.ShapeDtypeStruct((n, w), F32)
    return pl.pallas_call(
        body, name="merge_bwd", grid=(n // tm,),
        in_specs=[row_d, row_w, row_w, row_w] + gates + [w_br, w_br, w_br, w_o],
        out_specs=[pl.BlockSpec((tm, 3 * d), lambda i: (i, 0)), row_d, row_d, row_d, row_w, row_w, row_w],
        out_shape=[jax.ShapeDtypeStruct((n, 3 * d), MXU_DTYPE), sd, sd, sd, sw, sw, sw],
        compiler_params=_params(1),
    )(dx1, *ys, zm, zm, zm, *w_brs, w_out)


CONV_ROWS = 256
HALO = 8


def _ext(ref, r0, t_, lo, hi):
    rc = min(CONV_ROWS, t_)
    a, b = max(r0 - HALO, 0), min(r0 + rc + HALO, t_)
    parts = []
    if r0 - HALO < 0:
        parts.append(jnp.zeros((HALO, hi - lo), F32))
    parts.append(ref[0, a:b, lo:hi].astype(F32))
    if r0 + rc + HALO > t_:
        parts.append(jnp.zeros((HALO, hi - lo), F32))
    return jnp.concatenate(parts, axis=0) if len(parts) > 1 else parts[0]


def _gelu_parts(ac):
    cdf = 0.5 * (1.0 + _erf(ac * (2.0 ** -0.5)))
    pdf = jnp.exp(-0.5 * ac * ac) * ((2.0 * math.pi) ** -0.5)
    return cdf, pdf


def _conv_taps(a_ext, cw, cb):
    return cw[0:1, :] * pltpu.roll(a_ext, 2, 0) + cw[1:2, :] * pltpu.roll(a_ext, 1, 0) + cw[2:3, :] * a_ext + cb


def _glu_fwd(u, cw, cb):
    b_, t_, f2 = u.shape
    f = f2 // 2
    g = min(FFN_GROUP, f)
    rc = min(CONV_ROWS, t_)

    def body(u_ref, cw_ref, cb_ref, y_ref):
        cwv, cbv = cw_ref[...], cb_ref[...]
        for r0 in range(0, t_, rc):
            ac = _conv_taps(_ext(u_ref, r0, t_, 0, g), cwv, cbv)[HALO:HALO + rc]
            cdf, _ = _gelu_parts(ac)
            y_ref[0, r0:r0 + rc, :] = (ac * cdf * u_ref[0, r0:r0 + rc, g:2 * g]).astype(y_ref.dtype)

    return pl.pallas_call(
        body, name="glu_fwd", grid=(f // g, b_),
        in_specs=[pl.BlockSpec((1, t_, 2 * g), lambda j, b: (b, 0, j)), pl.BlockSpec((3, g), lambda j, b: (0, j)),
                  pl.BlockSpec((1, g), lambda j, b: (0, j))],
        out_specs=pl.BlockSpec((1, t_, g), lambda j, b: (b, 0, j)),
        out_shape=jax.ShapeDtypeStruct((b_, t_, f), MXU_DTYPE), compiler_params=_params(2),
    )(u, cw, cb)


def _glu_bwd(u, dy, cw, cb):
    b_, t_, f2 = u.shape
    f = f2 // 2
    g = min(FFN_GROUP, f)
    rc = min(CONV_ROWS, t_)
    ne = rc + 2 * HALO

    def body(u_ref, dy_ref, cw_ref, cb_ref, du_ref, dcw_ref, dcb_ref):
        cwv, cbv = cw_ref[...], cb_ref[...]
        dcw = [jnp.zeros((1, g), F32) for _ in range(3)]
        dcb = jnp.zeros((1, g), F32)
        for r0 in range(0, t_, rc):
            a_ext = _ext(u_ref, r0, t_, 0, g)
            v_ext = _ext(u_ref, r0, t_, g, 2 * g)
            dy_ext = _ext(dy_ref, r0, t_, 0, g)
            ac = _conv_taps(a_ext, cwv, cbv)
            cdf, pdf = _gelu_parts(ac)
            dac = dy_ext * v_ext * (cdf + ac * pdf)
            da = cwv[2:3, :] * dac + cwv[1:2, :] * pltpu.roll(dac, ne - 1, 0) + cwv[0:1, :] * pltpu.roll(dac, ne - 2, 0)
            mid = slice(HALO, HALO + rc)
            du_ref[0, r0:r0 + rc, 0:g] = da[mid].astype(du_ref.dtype)
            du_ref[0, r0:r0 + rc, g:2 * g] = (dy_ext[mid] * ac[mid] * cdf[mid]).astype(du_ref.dtype)
            dacm = dac[mid]
            dcw[0] = dcw[0] + jnp.sum(dacm * pltpu.roll(a_ext, 2, 0)[mid], axis=0, keepdims=True)
            dcw[1] = dcw[1] + jnp.sum(dacm * pltpu.roll(a_ext, 1, 0)[mid], axis=0, keepdims=True)
            dcw[2] = dcw[2] + jnp.sum(dacm * a_ext[mid], axis=0, keepdims=True)
            dcb = dcb + jnp.sum(dacm, axis=0, keepdims=True)
        first = pl.program_id(1) == 0
        _acc(dcw_ref, jnp.concatenate(dcw, axis=0), first)
        _acc(dcb_ref, dcb, first)

    ublk = pl.BlockSpec((1, t_, 2 * g), lambda j, b: (b, 0, j))
    cwb = pl.BlockSpec((3, g), lambda j, b: (0, j))
    cbb = pl.BlockSpec((1, g), lambda j, b: (0, j))
    return pl.pallas_call(
        body, name="glu_bwd", grid=(f // g, b_),
        in_specs=[ublk, pl.BlockSpec((1, t_, g), lambda j, b: (b, 0, j)), cwb, cbb],
        out_specs=[ublk, cwb, cbb],
        out_shape=[jax.ShapeDtypeStruct((b_, t_, f2), MXU_DTYPE), jax.ShapeDtypeStruct((3, f), F32),
                   jax.ShapeDtypeStruct((1, f), F32)],
        compiler_params=_params(2),
    )(u, dy, cw, cb)


def _loss_head(x1, ffn, target, tm=512):
    n, d = x1.shape
    tm = _tile(n, tm, 8)

    def body(x_ref, f_ref, t_ref, dy_ref, l_ref):
        err = x_ref[...] + f_ref[...] - t_ref[...]
        dy_ref[...] = err * (1.0 / d)
        _acc(l_ref, jnp.sum(err * err, axis=0, keepdims=True) * (0.5 / d), pl.program_id(0) == 0)

    row = pl.BlockSpec((tm, d), lambda i: (i, 0))
    vec = pl.BlockSpec((1, d), lambda i: (0, 0))
    return pl.pallas_call(
        body, name="loss_head", grid=(n // tm,), in_specs=[row, row, row], out_specs=[row, vec],
        out_shape=[jax.ShapeDtypeStruct((n, d), F32), jax.ShapeDtypeStruct((1, d), F32)], compiler_params=_params(1),
    )(x1, ffn, target)


def _place():
    x, y, c = lax.axis_index("x"), lax.axis_index("y"), lax.axis_index("c")
    chips = [(1 - x, y), (x, 1 - y), (1 - x, 1 - y)]
    return x, y, c, chips


def _remote(src, dst, send_sem, recv_sem, to):
    return pltpu.make_async_remote_copy(src_ref=src, dst_ref=dst, send_sem=send_sem, recv_sem=recv_sem,
                                        device_id=to, device_id_type=MESH)


def _gather_shards(wflat):
    r, w = wflat.shape
    half = r // 2

    def body(w_ref, out_ref, send_sems, recv_sems, local_sem):
        x, y, c, chips = _place()
        me, sib = 2 * x + y, (x, y, 1 - c)
        mine = pl.ds(pl.multiple_of(c * half, 128), half)
        theirs = pl.ds(pl.multiple_of((1 - c) * half, 128), half)
        own = pltpu.make_async_copy(w_ref, out_ref.at[me], local_sem)
        own.start()
        first = [_remote(w_ref.at[mine], out_ref.at[me, mine], send_sems.at[j], recv_sems.at[j], (*chip, c))
                 for j, chip in enumerate(chips)]
        for cp in first:
            cp.start()
        passed = []
        for j, (px, py) in enumerate(chips):
            blk = out_ref.at[2 * px + py, mine]
            _remote(blk, blk, send_sems.at[j], recv_sems.at[j], sib).wait_recv()
            passed.append(_remote(blk, blk, send_sems.at[3 + j], recv_sems.at[3 + j], sib))
            passed[-1].start()
        for j, (px, py) in enumerate(chips):
            blk = out_ref.at[2 * px + py, theirs]
            _remote(blk, blk, send_sems.at[3 + j], recv_sems.at[3 + j], sib).wait_recv()
        for cp in first + passed:
            cp.wait_send()
        own.wait()

    return pl.pallas_call(
        body, name="gather_shards", in_specs=[ANY], out_specs=ANY,
        out_shape=jax.ShapeDtypeStruct((N_CHIPS, r, w), wflat.dtype),
        scratch_shapes=[pltpu.SemaphoreType.DMA((6,)), pltpu.SemaphoreType.DMA((6,)), pltpu.SemaphoreType.DMA],
    )(wflat)


def _pair_swap_halves(g):
    n, r, w = g.shape
    half = r // 2

    def body(g_ref, a_ref, send_sem, recv_sem):
        x, y, c, _ = _place()
        theirs = pl.ds(pl.multiple_of((1 - c) * half, 128), half)
        cp = _remote(g_ref.at[:, theirs], a_ref, send_sem, recv_sem, (x, y, 1 - c))
        cp.start()
        cp.wait()

    return pl.pallas_call(
        body, name="pair_swap_halves", in_specs=[ANY], out_specs=ANY,
        out_shape=jax.ShapeDtypeStruct((n, half, w), g.dtype),
        scratch_shapes=[pltpu.SemaphoreType.DMA, pltpu.SemaphoreType.DMA],
    )(g)


def _add_half(g, a, c_idx, tr=640):
    n, r, w = g.shape
    half = r // 2
    tr = _tile(half, tr, 8)
    nblk = half // tr

    def body(c_ref, g_ref, a_ref, o_ref):
        o_ref[...] = g_ref[...] + a_ref[...]

    return pl.pallas_call(
        body, name="add_half",
        grid_spec=pltpu.PrefetchScalarGridSpec(
            num_scalar_prefetch=1, grid=(n, nblk),
            in_specs=[pl.BlockSpec((1, tr, w), lambda s, i, c_ref: (s, c_ref[0] * nblk + i, 0)),
                      pl.BlockSpec((1, tr, w), lambda s, i, c_ref: (s, i, 0))],
            out_specs=pl.BlockSpec((1, tr, w), lambda s, i, c_ref: (s, i, 0))),
        out_shape=jax.ShapeDtypeStruct((n, half, w), F32), compiler_params=_params(2),
    )(c_idx, g, a)


def _chip_exchange(p):
    n, h, w = p.shape

    def body(p_ref, b_ref, send_sems, recv_sems, local_sem):
        x, y, c, chips = _place()
        me = 2 * x + y
        own = pltpu.make_async_copy(p_ref.at[me], b_ref.at[me], local_sem)
        own.start()
        sent = [_remote(p_ref.at[2 * px + py], b_ref.at[me], send_sems.at[j], recv_sems.at[j], (px, py, c))
                for j, (px, py) in enumerate(chips)]
        for cp in sent:
            cp.start()
        for j, (px, py) in enumerate(chips):
            blk = b_ref.at[2 * px + py]
            _remote(blk, blk, send_sems.at[j], recv_sems.at[j], (px, py, c)).wait_recv()
        for cp in sent:
            cp.wait_send()
        own.wait()

    return pl.pallas_call(
        body, name="chip_exchange", in_specs=[ANY], out_specs=ANY,
        out_shape=jax.ShapeDtypeStruct((n, h, w), p.dtype),
        scratch_shapes=[pltpu.SemaphoreType.DMA((3,)), pltpu.SemaphoreType.DMA((3,)), pltpu.SemaphoreType.DMA],
    )(p)


def _sum_chips(bq, tr=640):
    n, h, w = bq.shape
    tr = _tile(h, tr, 8)

    def body(b_ref, o_ref):
        acc = b_ref[0]
        for s in range(1, n):
            acc = acc + b_ref[s]
        o_ref[...] = acc

    return pl.pallas_call(
        body, name="sum_chips", grid=(h // tr,),
        in_specs=[pl.BlockSpec((n, tr, w), lambda i: (0, i, 0))], out_specs=pl.BlockSpec((tr, w), lambda i: (i, 0)),
        out_shape=jax.ShapeDtypeStruct((h, w), F32), compiler_params=_params(1),
    )(bq)


def _pair_join_halves(q):
    h, w = q.shape

    def body(q_ref, o_ref, send_sem, recv_sem, local_sem):
        x, y, c, _ = _place()
        mine = pl.ds(pl.multiple_of(c * h, 128), h)
        theirs = pl.ds(pl.multiple_of((1 - c) * h, 128), h)
        own = pltpu.make_async_copy(q_ref, o_ref.at[mine], local_sem)
        own.start()
        cp = _remote(q_ref, o_ref.at[mine], send_sem, recv_sem, (x, y, 1 - c))
        cp.start()
        _remote(q_ref, o_ref.at[theirs], send_sem, recv_sem, (x, y, 1 - c)).wait_recv()
        cp.wait_send()
        own.wait()

    return pl.pallas_call(
        body, name="pair_join_halves", in_specs=[ANY], out_specs=ANY,
        out_shape=jax.ShapeDtypeStruct((2 * h, w), q.dtype),
        scratch_shapes=[pltpu.SemaphoreType.DMA, pltpu.SemaphoreType.DMA, pltpu.SemaphoreType.DMA],
    )(q)


def _all_sum_small(s, name):
    sr, w = s.shape

    def body(s_ref, o_ref, buf, send_sems, recv_sems):
        x, y, c, _ = _place()
        me = 4 * x + 2 * y + c
        buf[me] = s_ref[...]
        peers = []
        for k in range(1, 8):
            px = 1 - x if k & 4 else x
            py = 1 - y if k & 2 else y
            pc = 1 - c if k & 1 else c
            peers.append((px, py, pc))
        sent = [_remote(s_ref, buf.at[me], send_sems.at[k], recv_sems.at[k], peer) for k, peer in enumerate(peers)]
        for cp in sent:
            cp.start()
        for k, (px, py, pc) in enumerate(peers):
            _remote(s_ref, buf.at[4 * px + 2 * py + pc], send_sems.at[k], recv_sems.at[k], (px, py, pc)).wait_recv()
        for cp in sent:
            cp.wait_send()
        acc = buf[0]
        for d in range(1, 8):
            acc = acc + buf[d]
        o_ref[...] = acc

    vm = pl.BlockSpec(memory_space=pltpu.VMEM)
    return pl.pallas_call(
        body, name=name, in_specs=[vm], out_specs=vm, out_shape=jax.ShapeDtypeStruct((sr, w), F32),
        scratch_shapes=[pltpu.VMEM((8, sr, w), F32), pltpu.SemaphoreType.DMA((7,)), pltpu.SemaphoreType.DMA((7,))],
    )(s)


BIG = ("w_in", "mem_kv_w", "w_br_hgrn", "w_br_fox", "w_br_mem", "w_out", "ffn_w_up", "ffn_w_down")
COL_SHARDED = ("w_in", "w_br_hgrn", "w_br_fox", "w_br_mem", "ffn_w_up")
PIECE_ALIGN = 16 * FLAT_W
ROWS_ALIGN = 256


def _piece_rows(numel):
    return -(-numel // PIECE_ALIGN) * (PIECE_ALIGN // FLAT_W)


def _pack(pieces, lead):
    out, total = [], 0
    for p in pieces:
        flat = p.reshape(lead + (-1,))
        rows = _piece_rows(flat.shape[-1])
        flat = jnp.pad(flat, [(0, 0)] * len(lead) + [(0, rows * FLAT_W - flat.shape[-1])])
        out.append(flat.reshape(lead + (rows, FLAT_W)))
        total += rows
    pad = -total % ROWS_ALIGN
    if pad:
        out.append(jnp.zeros(lead + (pad, FLAT_W), out[0].dtype))
    return jnp.concatenate(out, axis=len(lead))


def _unpack(buf, shapes, lead):
    res, off = [], 0
    for shp in shapes:
        numel = math.prod(shp)
        rows = _piece_rows(numel)
        piece = lax.slice_in_dim(buf, off, off + rows, axis=len(lead)).reshape(lead + (rows * FLAT_W,))
        res.append(lax.slice_in_dim(piece, 0, numel, axis=len(lead)).reshape(lead + tuple(shp)))
        off += rows
    return res


def _to_full(name, shards):
    if name in COL_SHARDED:
        return shards.transpose(1, 0, 2).reshape(shards.shape[1], -1)
    return shards.reshape(-1, shards.shape[2])


def _to_shards(name, full):
    if name in COL_SHARDED:
        return full.reshape(full.shape[0], N_CHIPS, -1).transpose(1, 0, 2)
    return full.reshape(N_CHIPS, -1, full.shape[1])


SMALL = ("norm_mix_g", "norm_mem_g", "norm_ffn_g", "hgrn_lb_logits", "hgrn_norm_g", "fox_f_bias", "fox_q_norm_g",
         "fox_k_norm_g", "mem_q_norm_g", "mem_k_norm_g", "ffn_conv_b")


def _pack_small(vals):
    flats, total = [], 0
    for v in vals:
        flat = v.reshape(-1).astype(F32)
        n = -(-flat.shape[0] // FLAT_W)
        flats.append(jnp.pad(flat, (0, n * FLAT_W - flat.shape[0])))
        total += n
    if -total % 8:
        flats.append(jnp.zeros((-total % 8 * FLAT_W,), F32))
    return jnp.concatenate(flats).reshape(-1, FLAT_W)


def _unpack_small(buf, shapes):
    res, off = [], 0
    for shp in shapes:
        numel = math.prod(shp)
        n = -(-numel // FLAT_W)
        res.append(buf[off:off + n].reshape(-1)[:numel].reshape(shp))
        off += n
    return res


def _pad_lanes(v, width=LANE):
    return jnp.pad(v, ((0, 0), (0, width - v.shape[1])))


WEIGHTS = ("norm_mix_g", "norm_mem_g", "w_in", "hgrn_lb_logits", "hgrn_norm_g", "fox_f_bias", "fox_q_norm_g",
           "fox_k_norm_g", "mem_kv_w", "mem_q_norm_g", "mem_k_norm_g", "w_br_hgrn", "w_br_fox", "w_br_mem", "w_out",
           "norm_ffn_g", "ffn_w_up", "ffn_conv_w", "ffn_conv_b", "ffn_w_down")


def _local_step(x, mem, target, w, full, conv_w):
    b_, t_, d = x.shape
    n = b_ * t_
    hw, fw, mw = HG_HEADS * HG_D, FOX_HEADS * FOX_DH, MEM_HEADS * MEM_DH
    m_ = mem.shape[1]
    f = conv_w.shape[1]
    g = min(FFN_GROUP, f)
    s1 = 4 * hw + 3 * fw
    fox_col, mem_col, gate_col = 4 * hw // LANE, s1 // LANE, (s1 + mw) // d

    w_in = full["w_in"]
    w_main = jnp.concatenate([w_in[:, :s1], w_in[:, s1 + FOX_HEADS:]], axis=1)
    w_ff = _pad_lanes(w_in[:, s1:s1 + FOX_HEADS])
    w_up = full["ffn_w_up"].reshape(d, 2, f // g, g).transpose(0, 2, 1, 3).reshape(d, 2 * f)
    w_brs = [full["w_br_hgrn"], full["w_br_fox"], full["w_br_mem"]]
    w_out, w_kv, w_down = full["w_out"], full["mem_kv_w"], full["ffn_w_down"]
    f_bias = _pad_lanes(w["fox_f_bias"])
    cb = w["ffn_conv_b"]

    x2 = x.reshape(n, d)
    h = _rmsnorm_fwd(x2, w["norm_mix_g"], name="norm_mix_fwd")
    zm = _matmul(h, w_main, name="in_proj")
    zf = _matmul(h, w_ff, name="in_proj_forget")
    zm3, zf3 = zm.reshape(b_, t_, -1), zf.reshape(b_, t_, LANE)
    ya = _hgrn_fwd(zm3, w["hgrn_lb_logits"], w["hgrn_norm_g"], hw)
    fc = _fox_prep(zf3, f_bias)
    fct = jnp.swapaxes(fc[:, :, :FOX_HEADS], 1, 2)
    yb, lse = _fox_fwd(zm3, fc, fct, w["fox_q_norm_g"], w["fox_k_norm_g"], fw, fox_col)
    mem2 = mem.reshape(b_ * m_, d)
    hm = _rmsnorm_fwd(mem2, w["norm_mem_g"], name="norm_mem_fwd")
    mkv = _matmul(hm, w_kv, name="mem_kv_proj").reshape(b_, m_, 2 * mw)
    yc = _mem_fwd(zm3, mkv, w["mem_q_norm_g"], w["mem_k_norm_g"], mw, mem_col)
    ys = [ya.reshape(n, hw), yb.reshape(n, fw), yc.reshape(n, mw)]
    x1, merged = _merge_fwd(x2, ys, zm, w_brs, w_out, gate_col)
    h2 = _rmsnorm_fwd(x1, w["norm_ffn_g"], name="norm_ffn_fwd")
    u = _matmul(h2, w_up, name="ffn_up")
    u3 = u.reshape(b_, t_, 2 * f)
    yff = _glu_fwd(u3, conv_w, cb).reshape(n, f)
    ffn = _matmul(yff, w_down, name="ffn_down", tk=1408)
    dy, loss_vec = _loss_head(x1, ffn, target.reshape(n, d))

    grads = {}
    dyff = _matmul(dy, w_down, tb=True, name="ffn_down_dx", tn=1408)
    grads["ffn_w_down"] = _matmul(yff, dy, ta=True, name="ffn_down_dw", tm=1408)
    du, grads["ffn_conv_w"], grads["ffn_conv_b"] = _glu_bwd(u3, dyff.reshape(b_, t_, f), conv_w, cb)
    du2 = du.reshape(n, 2 * f)
    dh2 = _matmul(du2, w_up, tb=True, name="ffn_up_dx")
    g_up = _matmul(h2, du2, ta=True, name="ffn_up_dw")
    grads["ffn_w_up"] = g_up.reshape(d, f // g, 2, g).transpose(0, 2, 1, 3).reshape(d, 2 * f)
    dx1, grads["norm_ffn_g"] = _rmsnorm_bwd(x1, [dh2], w["norm_ffn_g"], dy, name="norm_ffn_bwd")

    dgl, dpa, dpb, dpc, dya, dyb, dyc = _merge_bwd(dx1, ys, zm, w_brs, w_out, gate_col)
    grads["w_out"] = _matmul(merged, dx1, ta=True, name="out_proj_dw")
    for nm, y_, dp_ in zip(("w_br_hgrn", "w_br_fox", "w_br_mem"), ys, (dpa, dpb, dpc)):
        grads[nm] = _matmul(y_, dp_, ta=True, name=nm + "_dw")

    dmq, dmk, dmv, grads["mem_q_norm_g"], grads["mem_k_norm_g"] = _mem_bwd(
        zm3, mkv, dyc.reshape(b_, t_, mw), w["mem_q_norm_g"], w["mem_k_norm_g"], mw, mem_col)
    dmkv = jnp.concatenate([dmk, dmv], axis=-1).reshape(b_ * m_, 2 * mw)
    grads["mem_kv_w"] = _matmul(hm, dmkv, ta=True, name="mem_kv_dw")
    dhm = _matmul(dmkv, w_kv, tb=True, name="mem_kv_dx")
    _, grads["norm_mem_g"] = _rmsnorm_bwd(mem2, [dhm], w["norm_mem_g"], None, name="norm_mem_bwd")

    dfq, dfk, dfv, dfct, dfr, grads["fox_q_norm_g"], grads["fox_k_norm_g"] = _fox_bwd(
        zm3, yb, dyb.reshape(b_, t_, fw), lse, fc, fct, w["fox_q_norm_g"], w["fox_k_norm_g"], fw, fox_col)
    dfc = dfr[..., :2].transpose(0, 2, 1, 3).reshape(b_, t_, FOX_HEADS) + jnp.swapaxes(dfct, 1, 2)
    dfc = jnp.pad(dfc, ((0, 0), (0, 0), (0, LANE - FOX_HEADS)))
    dzf, g_fb = _fox_post(dfc, zf3, f_bias)
    grads["fox_f_bias"] = g_fb[:, :FOX_HEADS]

    dhq, dhf, dhi, dhg, grads["hgrn_lb_logits"], grads["hgrn_norm_g"] = _hgrn_bwd(
        zm3, dya.reshape(b_, t_, hw), w["hgrn_lb_logits"], w["hgrn_norm_g"], hw)

    dzm = jnp.concatenate([dhq, dhf, dhi, dhg, dfq, dfk, dfv, dmq, dgl.reshape(b_, t_, 3 * d)], axis=-1).reshape(n, -1)
    dzf2 = dzf.reshape(n, LANE)
    dh_a = _matmul(dzm, w_main, tb=True, name="in_proj_dx")
    dh_b = _matmul(dzf2, w_ff, tb=True, name="in_proj_forget_dx")
    g_main = _matmul(h, dzm, ta=True, name="in_proj_dw")
    g_ff = _matmul(h, dzf2, ta=True, name="in_proj_forget_dw")
    grads["w_in"] = jnp.concatenate([g_main[:, :s1], g_ff[:, :FOX_HEADS], g_main[:, s1:]], axis=1)
    grad_x, grads["norm_mix_g"] = _rmsnorm_bwd(x2, [dh_a, dh_b], w["norm_mix_g"], dx1, name="norm_mix_bwd")
    return loss_vec, grad_x.reshape(b_, t_, d), grads


def kernel(x, mem, norm_mix_g, norm_mem_g, w_in, hgrn_lb_logits, hgrn_norm_g, fox_f_bias, fox_q_norm_g, fox_k_norm_g, mem_kv_w, mem_q_norm_g, mem_k_norm_g, w_br_hgrn, w_br_fox, w_br_mem, w_out, norm_ffn_g, ffn_w_up, ffn_conv_w, ffn_conv_b, ffn_w_down, loss_target, m_norm_mix_g, m_norm_mem_g, m_w_in, m_hgrn_lb_logits, m_hgrn_norm_g, m_fox_f_bias, m_fox_q_norm_g, m_fox_k_norm_g, m_mem_kv_w, m_mem_q_norm_g, m_mem_k_norm_g, m_w_br_hgrn, m_w_br_fox, m_w_br_mem, m_w_out, m_norm_ffn_g, m_ffn_w_up, m_ffn_conv_w, m_ffn_conv_b, m_ffn_w_down, v_norm_mix_g, v_norm_mem_g, v_w_in, v_hgrn_lb_logits, v_hgrn_norm_g, v_fox_f_bias, v_fox_q_norm_g, v_fox_k_norm_g, v_mem_kv_w, v_mem_q_norm_g, v_mem_k_norm_g, v_w_br_hgrn, v_w_br_fox, v_w_br_mem, v_w_out, v_norm_ffn_g, v_ffn_w_up, v_ffn_conv_w, v_ffn_conv_b, v_ffn_w_down):
    w = dict(zip(WEIGHTS, (norm_mix_g, norm_mem_g, w_in, hgrn_lb_logits, hgrn_norm_g, fox_f_bias, fox_q_norm_g,
                           fox_k_norm_g, mem_kv_w, mem_q_norm_g, mem_k_norm_g, w_br_hgrn, w_br_fox, w_br_mem, w_out,
                           norm_ffn_g, ffn_w_up, ffn_conv_w, ffn_conv_b, ffn_w_down)))
    m = dict(zip(WEIGHTS, (m_norm_mix_g, m_norm_mem_g, m_w_in, m_hgrn_lb_logits, m_hgrn_norm_g, m_fox_f_bias,
                           m_fox_q_norm_g, m_fox_k_norm_g, m_mem_kv_w, m_mem_q_norm_g, m_mem_k_norm_g, m_w_br_hgrn,
                           m_w_br_fox, m_w_br_mem, m_w_out, m_norm_ffn_g, m_ffn_w_up, m_ffn_conv_w, m_ffn_conv_b,
                           m_ffn_w_down)))
    v = dict(zip(WEIGHTS, (v_norm_mix_g, v_norm_mem_g, v_w_in, v_hgrn_lb_logits, v_hgrn_norm_g, v_fox_f_bias,
                           v_fox_q_norm_g, v_fox_k_norm_g, v_mem_kv_w, v_mem_q_norm_g, v_mem_k_norm_g, v_w_br_hgrn,
                           v_w_br_fox, v_w_br_mem, v_w_out, v_norm_ffn_g, v_ffn_w_up, v_ffn_conv_w, v_ffn_conv_b,
                           v_ffn_w_down)))
    c_idx = lax.axis_index("c")
    chip = 2 * lax.axis_index("x") + lax.axis_index("y")

    shard_shapes = [w[nm].shape[1:] for nm in BIG]
    gathered = _gather_shards(_pack([w[nm][0].astype(MXU_DTYPE) for nm in BIG], ()))
    full = {nm: _to_full(nm, s) for nm, s in zip(BIG, _unpack(gathered, shard_shapes, (N_CHIPS,)))}
    cs = ffn_conv_w.shape[2]
    f = cs * N_CHIPS
    placed = lax.dynamic_update_slice(jnp.zeros((3, f), F32), ffn_conv_w[0] * (c_idx == 0).astype(F32), (0, chip * cs))
    conv_w = _unpack_small(_all_sum_small(_pack_small([placed]), "gather_conv_w"), [(3, f)])[0]

    loss_vec, grad_x, grads = _local_step(x, mem, loss_target, w, full, conv_w)

    gflat = _pack([_to_shards(nm, grads[nm]) for nm in BIG], (N_CHIPS,))
    from_sibling = _pair_swap_halves(gflat)
    chip_partial = _add_half(gflat, from_sibling, jnp.reshape(c_idx, (1,)).astype(jnp.int32))
    reduced_half = _sum_chips(_chip_exchange(chip_partial))
    gshards = dict(zip(BIG, _unpack(_pair_join_halves(reduced_half), shard_shapes, ())))

    small_names = SMALL + ("ffn_conv_w",)
    summed = _unpack_small(
        _all_sum_small(_pack_small([grads[nm] for nm in small_names] + [loss_vec]), "all_sum_small_grads"),
        [grads[nm].shape for nm in small_names] + [loss_vec.shape])
    gsmall = dict(zip(small_names, summed[:-1]))
    loss = jnp.sum(summed[-1])
    g_out = {nm: gshards[nm][None] for nm in BIG}
    for nm in SMALL:
        g_out[nm] = gsmall[nm].reshape(w[nm].shape)
    g_out["ffn_conv_w"] = lax.dynamic_slice(gsmall["ffn_conv_w"], (0, chip * cs), (3, cs))[None]

    delta, new_m, new_v = {}, {}, {}
    for nm in BIG + ("ffn_conv_w",):
        d_, m_, v_ = _adamw(w[nm][0], g_out[nm][0], m[nm][0], v[nm][0], name="adamw_" + nm)
        delta[nm], new_m[nm], new_v[nm] = d_[None], m_[None], v_[None]
    packed = [_pack_small([t[nm] for nm in SMALL]) for t in (w, g_out, m, v)]
    outs = _adamw(*packed, name="adamw_small")
    shapes = [w[nm].shape for nm in SMALL]
    for res, o in zip((delta, new_m, new_v), outs):
        res.update(zip(SMALL, _unpack_small(o, shapes)))

    return (loss, grad_x, *[g_out[nm] for nm in WEIGHTS], *[delta[nm] for nm in WEIGHTS],
            *[new_m[nm] for nm in WEIGHTS], *[new_v[nm] for nm in WEIGHTS])
```

```python
import functools
import math

import jax
import jax.numpy as jnp
from jax import lax
from jax.experimental import pallas as pl
from jax.experimental.pallas import tpu as pltpu

F32 = jnp.float32
BF16 = jnp.bfloat16
MXU_DTYPE = jnp.bfloat16
EXCHANGE_DTYPE = jnp.bfloat16

EPS = 1e-6
HG_HEADS, HG_D = 4, 128
FOX_HEADS, FOX_DH = 8, 64
MEM_HEADS, MEM_DH = 4, 128
HG_CHUNK = 64
FOX_BLOCK = 256
LANE = 128
FFN_GROUP = 256
FLAT_W = 1024
VMEM_LIMIT = 56 * 2 ** 20
NEG = -1e30
N_CHIPS = 4

ADAM_LR, ADAM_B1, ADAM_B2, ADAM_EPS, ADAM_WD, ADAM_STEP = 0.001, 0.9, 0.999, 1e-08, 0.01, 10

MESH = pl.DeviceIdType.MESH
ANY = pl.BlockSpec(memory_space=pl.ANY)


def _mx(x):
    return x.astype(MXU_DTYPE)


def _dot(a, b, ca, cb):
    return lax.dot_general(_mx(a), _mx(b), (((ca,), (cb,)), ((), ())), preferred_element_type=F32)


def _nn(a, b):
    return _dot(a, b, 1, 0)


def _nt(a, b):
    return _dot(a, b, 1, 1)


def _tn(a, b):
    return _dot(a, b, 0, 0)


def _dotp(a, b, ca, cb):
    return lax.dot_general(a, b, (((ca,), (cb,)), ((), ())), precision=lax.Precision.HIGHEST,
                           preferred_element_type=F32)


def _tri_dot(tri_bf, x):
    hi = x.astype(BF16)
    r = x - hi.astype(F32)
    mid = r.astype(BF16)
    lo = (r - mid.astype(F32)).astype(BF16)

    def d(v):
        return lax.dot_general(tri_bf, v, (((1,), (0,)), ((), ())), preferred_element_type=F32)

    return d(hi) + d(mid) + d(lo)


def _sig(x):
    return jax.nn.sigmoid(x)


def _erf(x):
    a = jnp.abs(x)
    t = 1.0 / (1.0 + 0.3275911 * a)
    poly = t * (0.254829592 + t * (-0.284496736 + t * (1.421413741 + t * (-1.453152027 + t * 1.061405429))))
    y = 1.0 - poly * jnp.exp(-a * a)
    return jnp.where(x < 0, -y, y)


def _tile(dim, pref, unit=LANE):
    if dim <= pref:
        return dim
    t = pref - pref % unit
    while t >= unit:
        if dim % t == 0:
            return t
        t -= unit
    return dim


def _params(n_grid):
    return pltpu.CompilerParams(dimension_semantics=("arbitrary",) * n_grid, vmem_limit_bytes=VMEM_LIMIT)


def _acc(ref, val, first):
    @pl.when(first)
    def _():
        ref[...] = val

    @pl.when(jnp.logical_not(first))
    def _():
        ref[...] += val


def _matmul(a, b, *, name, ta=False, tb=False, out_dtype=F32, tm=1024, tn=1024, tk=1024):
    m, k = (a.shape[1], a.shape[0]) if ta else a.shape
    n = b.shape[0] if tb else b.shape[1]
    tm, tn, tk = _tile(m, tm), _tile(n, tn), _tile(k, tk)
    nk = k // tk

    def body(a_ref, b_ref, o_ref, acc_ref):
        kk = pl.program_id(2)
        p = _dot(a_ref[...], b_ref[...], 0 if ta else 1, 1 if tb else 0)
        _acc(acc_ref, p, kk == 0)

        @pl.when(kk == nk - 1)
        def _():
            o_ref[...] = acc_ref[...].astype(o_ref.dtype)

    a_spec = pl.BlockSpec((tk, tm), lambda i, j, kk: (kk, i)) if ta else pl.BlockSpec((tm, tk), lambda i, j, kk: (i, kk))
    b_spec = pl.BlockSpec((tn, tk), lambda i, j, kk: (j, kk)) if tb else pl.BlockSpec((tk, tn), lambda i, j, kk: (kk, j))
    return pl.pallas_call(
        body, name=name, grid=(m // tm, n // tn, nk),
        in_specs=[a_spec, b_spec],
        out_specs=pl.BlockSpec((tm, tn), lambda i, j, kk: (i, j)),
        out_shape=jax.ShapeDtypeStruct((m, n), out_dtype),
        scratch_shapes=[pltpu.VMEM((tm, tn), F32)],
        compiler_params=_params(3),
    )(a, b)


def _rmsnorm_fwd(x, g, *, name, tm=512):
    n, d = x.shape
    tm = _tile(n, tm, 8)

    def body(x_ref, g_ref, o_ref):
        xv = x_ref[...]
        r = lax.rsqrt(jnp.mean(xv * xv, axis=-1, keepdims=True) + EPS)
        o_ref[...] = (xv * r * g_ref[...]).astype(o_ref.dtype)

    return pl.pallas_call(
        body, name=name, grid=(n // tm,),
        in_specs=[pl.BlockSpec((tm, d), lambda i: (i, 0)), pl.BlockSpec((1, d), lambda i: (0, 0))],
        out_specs=pl.BlockSpec((tm, d), lambda i: (i, 0)),
        out_shape=jax.ShapeDtypeStruct((n, d), MXU_DTYPE),
        compiler_params=_params(1),
    )(x, g)


def _rmsnorm_bwd(x, dhs, g, res, *, name, tm=512):
    n, d = x.shape
    tm = _tile(n, tm, 8)
    n_dh = len(dhs)
    has_res = res is not None

    def body(*refs):
        x_ref, dh_refs, g_ref = refs[0], refs[1:1 + n_dh], refs[1 + n_dh]
        res_ref = refs[2 + n_dh] if has_res else None
        dx_ref, dg_ref = refs[-2], refs[-1]
        xv = x_ref[...]
        dh = dh_refs[0][...].astype(F32)
        for r_ in dh_refs[1:]:
            dh = dh + r_[...].astype(F32)
        r = lax.rsqrt(jnp.mean(xv * xv, axis=-1, keepdims=True) + EPS)
        dhg = dh * g_ref[...]
        dx = r * dhg - xv * (r * r * r) * jnp.mean(dhg * xv, axis=-1, keepdims=True)
        if has_res:
            dx = dx + res_ref[...]
        dx_ref[...] = dx
        _acc(dg_ref, jnp.sum(dh * xv * r, axis=0, keepdims=True), pl.program_id(0) == 0)

    row = pl.BlockSpec((tm, d), lambda i: (i, 0))
    vec = pl.BlockSpec((1, d), lambda i: (0, 0))
    ins = [x] + list(dhs) + [g] + ([res] if has_res else [])
    return pl.pallas_call(
        body, name=name, grid=(n // tm,),
        in_specs=[row] * (1 + n_dh) + [vec] + ([row] if has_res else []),
        out_specs=[row, vec],
        out_shape=[jax.ShapeDtypeStruct((n, d), F32), jax.ShapeDtypeStruct((1, d), F32)],
        compiler_params=_params(1),
    )(*ins)


def _adamw(w, g, m, v, *, name, tr=256):
    r, c = w.shape
    tr = _tile(r, tr, 8)
    c1 = 1.0 / (1.0 - ADAM_B1 ** ADAM_STEP)
    c2 = 1.0 / (1.0 - ADAM_B2 ** ADAM_STEP)

    def body(w_ref, g_ref, m_ref, v_ref, d_ref, mo_ref, vo_ref):
        gv = g_ref[...]
        mn = ADAM_B1 * m_ref[...] + (1.0 - ADAM_B1) * gv
        vn = ADAM_B2 * v_ref[...] + (1.0 - ADAM_B2) * (gv * gv)
        d_ref[...] = -ADAM_LR * ((mn * c1) / (jnp.sqrt(vn * c2) + ADAM_EPS) + ADAM_WD * w_ref[...])
        mo_ref[...] = mn
        vo_ref[...] = vn

    blk = pl.BlockSpec((tr, c), lambda i: (i, 0))
    sds = jax.ShapeDtypeStruct((r, c), F32)
    return pl.pallas_call(
        body, name=name, grid=(r // tr,), in_specs=[blk] * 4, out_specs=[blk] * 3, out_shape=[sds] * 3,
        compiler_params=_params(1),
    )(w, g, m, v)


def _hgrn_chunk(hq, hf, hi, lbv, tril, tril_bf, st):
    c = hq.shape[0]
    sf = _sig(hf)
    f = lbv + (1.0 - lbv) * sf
    k = 1.0 - f
    gcum = _tri_dot(tril_bf, jnp.log(f))
    mid = gcum[c // 2 - 1:c // 2, :]
    glast = gcum[c - 1:c, :]
    sq = _sig(hq)
    q = hq * sq
    e_q = jnp.exp(gcum - mid)
    e_k = jnp.exp(mid - gcum)
    qe, ke = q * e_q, k * e_k
    a = jnp.where(tril, _nt(qe, ke), 0.0)
    e_g = jnp.exp(gcum)
    qg = q * e_g
    o = _nn(a, hi) + _nt(qg, st)
    e_s = jnp.exp(glast - gcum)
    kg = k * e_s
    e_l = jnp.exp(glast)
    st_new = st * e_l + _tn(hi, kg)
    return dict(sf=sf, f=f, k=k, sq=sq, q=q, e_q=e_q, e_k=e_k, qe=qe, ke=ke, a=a, e_g=e_g, qg=qg, o=o,
                e_s=e_s, kg=kg, e_l=e_l, st_new=st_new)


HG_EXAMPLES = 2


def _hgrn_specs(t_, hw, ne):
    nb = hw // LANE

    def col(off):
        return pl.BlockSpec((ne, t_, LANE), lambda h, b: (b, 0, off * nb + h))

    vec = pl.BlockSpec((2, LANE), lambda h, b: (0, h))
    one = pl.BlockSpec((1, LANE), lambda h, b: (0, 0))
    blk = pl.BlockSpec((ne, t_, LANE), lambda h, b: (b, 0, h))
    return col, vec, one, blk


def _hgrn_fwd(zm, lb, gn, hw):
    b_, t_, _ = zm.shape
    c = min(HG_CHUNK, t_)
    nc = t_ // c
    ne = min(HG_EXAMPLES, b_)
    col, vec, one, blk = _hgrn_specs(t_, hw, ne)

    def body(q_ref, f_ref, i_ref, g_ref, lb_ref, gn_ref, y_ref):
        lbv, gnv = _sig(lb_ref[0:1, :] - lb_ref[1:2, :]), gn_ref[...]
        tril = lax.broadcasted_iota(jnp.int32, (c, c), 0) >= lax.broadcasted_iota(jnp.int32, (c, c), 1)
        tril_bf = tril.astype(BF16)

        def chunk(n, sts):
            rows = pl.ds(pl.multiple_of(n * c, c), c)
            new = []
            for e in range(ne):
                p = _hgrn_chunk(q_ref[e, rows, :], f_ref[e, rows, :], i_ref[e, rows, :], lbv, tril, tril_bf, sts[e])
                o = p["o"]
                r = lax.rsqrt(jnp.mean(o * o, axis=-1, keepdims=True) + EPS)
                hg = g_ref[e, rows, :]
                y_ref[e, rows, :] = o * r * gnv * (hg * _sig(hg))
                new.append(p["st_new"])
            return tuple(new)

        lax.fori_loop(0, nc, chunk, tuple(jnp.zeros((HG_D, HG_D), F32) for _ in range(ne)))

    return pl.pallas_call(
        body, name="hgrn_fwd", grid=(HG_HEADS, b_ // ne),
        in_specs=[col(0), col(1), col(2), col(3), vec, one], out_specs=blk,
        out_shape=jax.ShapeDtypeStruct((b_, t_, hw), F32),
        compiler_params=_params(2),
    )(zm, zm, zm, zm, lb, gn)


def _hgrn_bwd(zm, dy, lb, gn, hw):
    b_, t_, _ = zm.shape
    c = min(HG_CHUNK, t_)
    nc = t_ // c
    ne = min(HG_EXAMPLES, b_)
    col, vec, one, blk = _hgrn_specs(t_, hw, ne)

    def body(q_ref, f_ref, i_ref, g_ref, dy_ref, lb_ref, gn_ref, dq_ref, df_ref, di_ref, dg_ref, dlb_ref, dgn_ref,
             st_all):
        h, b = pl.program_id(0), pl.program_id(1)
        lbv, gnv = _sig(lb_ref[0:1, :] - lb_ref[1:2, :]), gn_ref[...]
        row = lax.broadcasted_iota(jnp.int32, (c, c), 0)
        cl = lax.broadcasted_iota(jnp.int32, (c, c), 1)
        tril = row >= cl
        tril_bf = tril.astype(BF16)
        triu_bf = (row <= cl).astype(BF16)
        last_row = lax.broadcasted_iota(jnp.int32, (c, LANE), 0) == c - 1

        def fwd(n, sts):
            rows = pl.ds(pl.multiple_of(n * c, c), c)
            new = []
            for e in range(ne):
                st_all[e, n] = sts[e]
                new.append(_hgrn_chunk(q_ref[e, rows, :], f_ref[e, rows, :], i_ref[e, rows, :], lbv, tril, tril_bf,
                                       sts[e])["st_new"])
            return tuple(new)

        zst = tuple(jnp.zeros((HG_D, HG_D), F32) for _ in range(ne))
        lax.fori_loop(0, nc, fwd, zst)

        def bwd_one(e, n, rows, dst):
            hq, hi, hg = q_ref[e, rows, :], i_ref[e, rows, :], g_ref[e, rows, :]
            st = st_all[e, n]
            p = _hgrn_chunk(hq, f_ref[e, rows, :], hi, lbv, tril, tril_bf, st)
            o, q, k = p["o"], p["q"], p["k"]
            dyv = dy_ref[e, rows, :]
            sg = _sig(hg)
            r = lax.rsqrt(jnp.mean(o * o, axis=-1, keepdims=True) + EPS)
            nrm = o * r * gnv
            dn = dyv * (hg * sg)
            dg_ref[e, rows, :] = (dyv * nrm * (sg * (1.0 + hg * (1.0 - sg)))).astype(dg_ref.dtype)
            dgn = jnp.sum(dn * o * r, axis=0, keepdims=True)
            dng = dn * gnv
            do = r * dng - o * (r * r * r) * jnp.mean(dng * o, axis=-1, keepdims=True)
            da = jnp.where(tril, _dotp(do, hi, 1, 1), 0.0)
            dq = _dotp(da, p["ke"], 1, 0) * p["e_q"] + _dotp(do, st, 1, 0) * p["e_g"]
            dkg = _dotp(hi, dst, 1, 0)
            dk_state = dkg * p["e_s"]
            dk = _dotp(da, p["qe"], 0, 0) * p["e_k"] + dk_state
            di_ref[e, rows, :] = (_tn(p["a"], do) + _nt(p["kg"], dst)).astype(di_ref.dtype)
            dgc = q * dq - k * dk
            extra = (jnp.sum(k * dk_state, axis=0, keepdims=True)
                     + p["e_l"] * jnp.sum(st * dst, axis=0, keepdims=True))
            dgc = dgc + jnp.where(last_row, extra, 0.0)
            dlf = _tri_dot(triu_bf, dgc)
            dfv = dlf / p["f"] - dk
            sf, sq = p["sf"], p["sq"]
            df_ref[e, rows, :] = (dfv * (1.0 - lbv) * sf * (1.0 - sf)).astype(df_ref.dtype)
            dlb = jnp.sum(dfv * (1.0 - sf), axis=0, keepdims=True)
            dq_ref[e, rows, :] = (dq * (sq * (1.0 + hq * (1.0 - sq)))).astype(dq_ref.dtype)
            return dst * p["e_l"] + _dotp(do, p["qg"], 0, 0), dlb, dgn

        def bwd(m, carry):
            dsts, dlb, dgn = carry
            n = nc - 1 - m
            rows = pl.ds(pl.multiple_of(n * c, c), c)
            new = []
            for e in range(ne):
                dst, dlb_e, dgn_e = bwd_one(e, n, rows, dsts[e])
                new.append(dst)
                dlb, dgn = dlb + dlb_e, dgn + dgn_e
            return tuple(new), dlb, dgn

        z1 = jnp.zeros((1, LANE), F32)
        _, dlb, dgn = lax.fori_loop(0, nc, bwd, (zst, z1, z1))
        dl0 = dlb * lbv * (1.0 - lbv)
        _acc(dlb_ref, jnp.concatenate([dl0, -dl0], axis=0), b == 0)
        _acc(dgn_ref, dgn, jnp.logical_and(b == 0, h == 0))

    sds = jax.ShapeDtypeStruct((b_, t_, hw), MXU_DTYPE)
    return pl.pallas_call(
        body, name="hgrn_bwd", grid=(HG_HEADS, b_ // ne),
        in_specs=[col(0), col(1), col(2), col(3), blk, vec, one],
        out_specs=[blk, blk, blk, blk, vec, one],
        out_shape=[sds, sds, sds, sds, jax.ShapeDtypeStruct((2, hw), F32), jax.ShapeDtypeStruct((1, LANE), F32)],
        scratch_shapes=[pltpu.VMEM((ne, nc, HG_D, HG_D), F32)],
        compiler_params=_params(2),
    )(zm, zm, zm, zm, dy, lb, gn)


def _fox_logf(x):
    return jnp.minimum(x, 0.0) - jnp.log(1.0 + jnp.exp(-jnp.abs(x)))


def _fox_prep(zf, bias):
    b_, t_, _ = zf.shape
    tb = min(FOX_BLOCK, t_)
    nb = t_ // tb

    def body(z_ref, b_ref, fc_ref):
        tril_bf = (lax.broadcasted_iota(jnp.int32, (tb, tb), 0) >= lax.broadcasted_iota(jnp.int32, (tb, tb), 1)).astype(BF16)
        bv = b_ref[...]

        def blk(i, carry):
            rows = pl.ds(pl.multiple_of(i * tb, tb), tb)
            fc = _tri_dot(tril_bf, _fox_logf(z_ref[0, rows, :] + bv)) + carry
            fc_ref[0, rows, :] = fc
            return fc[tb - 1:tb, :]

        lax.fori_loop(0, nb, blk, jnp.zeros((1, LANE), F32))

    blk_spec = pl.BlockSpec((1, t_, LANE), lambda b: (b, 0, 0))
    return pl.pallas_call(
        body, name="fox_prep", grid=(b_,),
        in_specs=[blk_spec, pl.BlockSpec((1, LANE), lambda b: (0, 0))], out_specs=blk_spec,
        out_shape=jax.ShapeDtypeStruct((b_, t_, LANE), F32), compiler_params=_params(1),
    )(zf, bias)


def _fox_post(dfc, zf, bias):
    b_, t_, _ = zf.shape
    tb = min(FOX_BLOCK, t_)
    nb = t_ // tb

    def body(d_ref, z_ref, b_ref, dz_ref, db_ref):
        triu_bf = (lax.broadcasted_iota(jnp.int32, (tb, tb), 0) <= lax.broadcasted_iota(jnp.int32, (tb, tb), 1)).astype(BF16)
        valid = lax.broadcasted_iota(jnp.int32, (tb, LANE), 1) < FOX_HEADS
        bv = b_ref[...]

        def blk(m, carry):
            tail, db = carry
            rows = pl.ds(pl.multiple_of((nb - 1 - m) * tb, tb), tb)
            dlf = _tri_dot(triu_bf, d_ref[0, rows, :]) + tail
            dx = jnp.where(valid, dlf * _sig(-(z_ref[0, rows, :] + bv)), 0.0)
            dz_ref[0, rows, :] = dx.astype(dz_ref.dtype)
            return dlf[0:1, :], db + jnp.sum(dx, axis=0, keepdims=True)

        z1 = jnp.zeros((1, LANE), F32)
        _, db = lax.fori_loop(0, nb, blk, (z1, z1))
        _acc(db_ref, db, pl.program_id(0) == 0)

    blk_spec = pl.BlockSpec((1, t_, LANE), lambda b: (b, 0, 0))
    vec = pl.BlockSpec((1, LANE), lambda b: (0, 0))
    return pl.pallas_call(
        body, name="fox_post", grid=(b_,), in_specs=[blk_spec, blk_spec, vec], out_specs=[blk_spec, vec],
        out_shape=[jax.ShapeDtypeStruct((b_, t_, LANE), MXU_DTYPE), jax.ShapeDtypeStruct((1, LANE), F32)],
        compiler_params=_params(1),
    )(dfc, zf, bias)


FOX_TILE = 128
AUG = 64


def _head_mean_matrix():
    r = lax.broadcasted_iota(jnp.int32, (LANE, LANE), 0) // FOX_DH
    c = lax.broadcasted_iota(jnp.int32, (LANE, LANE), 1) // FOX_DH
    return (r == c).astype(BF16)


def _dot_right_exact(x, m_bf):
    hi = x.astype(BF16)
    r = x - hi.astype(F32)
    mid = r.astype(BF16)
    lo = (r - mid.astype(F32)).astype(BF16)

    def d(v):
        return lax.dot_general(v, m_bf, (((1,), (0,)), ((), ())), preferred_element_type=F32)

    return d(hi) + d(mid) + d(lo)


def _pair_norm(x, g2, bd):
    r = lax.rsqrt(_dot_right_exact(x * x, bd) * (1.0 / FOX_DH) + EPS)
    return x * r * g2, r


def _pair_norm_bwd(x, r, dy, g2, bd):
    dyg = dy * g2
    dx = r * dyg - x * (r * r * r) * (_dot_right_exact(dyg * x, bd) * (1.0 / FOX_DH))
    return dx, jnp.sum(dy * x * r, axis=0, keepdims=True)


def _head_lanes(xn, hh):
    return xn if hh == 0 else pltpu.roll(xn, FOX_DH, 1)


def _split3(x):
    hi = x.astype(BF16).astype(F32)
    mid = (x - hi).astype(BF16).astype(F32)
    return hi, mid, x - hi - mid


def _fox_operands(q_ref, k_ref, v_ref, fc_ref, gq2, gk2, p, qa, ka, va):
    t_ = q_ref.shape[1]
    bd = _head_mean_matrix()
    lane = lax.broadcasted_iota(jnp.int32, (t_, LANE), 1)
    qx, kx = q_ref[0], k_ref[0]
    qn, rq = _pair_norm(qx, gq2, bd)
    kn, rk = _pair_norm(kx, gk2, bd)
    vv = v_ref[0]
    q_aug = jnp.where(jnp.logical_and(lane >= AUG, lane < AUG + 3), 1.0, 0.0)
    for hh in range(2):
        fcol = jnp.sum(jnp.where(lane == 2 * p + hh, fc_ref[0], 0.0), axis=-1, keepdims=True)
        hi, mid, lo = _split3(-fcol)
        k_aug = jnp.where(lane == AUG, hi, jnp.where(lane == AUG + 1, mid, jnp.where(lane == AUG + 2, lo,
                          jnp.where(lane == AUG + 3, 1.0, 0.0))))
        head = lane < FOX_DH
        qa[hh] = jnp.where(head, _head_lanes(qn, hh), q_aug).astype(MXU_DTYPE)
        ka[hh] = jnp.where(head, _head_lanes(kn, hh), k_aug).astype(MXU_DTYPE)
        va[hh] = jnp.where(head, _head_lanes(vv, hh), 0.0).astype(MXU_DTYPE)
    return bd, lane, qx, kx, rq, rk


def _fox_specs(t_, fw, col0):
    npair = fw // LANE

    def col(off):
        return pl.BlockSpec((1, t_, LANE), lambda b, p: (b, 0, col0 + off * npair + p))

    pair = pl.BlockSpec((1, t_, LANE), lambda b, p: (b, 0, p))
    full = pl.BlockSpec((1, t_, LANE), lambda b, p: (b, 0, 0))
    gvec = pl.BlockSpec((1, LANE), lambda b, p: (0, 0))
    lse = pl.BlockSpec((1, 1, t_, LANE), lambda b, p: (b, p, 0, 0))
    return col, pair, full, gvec, lse


def _fox_fwd(zm, fc, gq2, gk2, fw, col0):
    b_, t_, _ = zm.shape
    npair = fw // LANE
    tq = min(FOX_TILE, t_)
    nb = t_ // tq
    scale = FOX_DH ** -0.5
    col, pair, full, gvec, lse_spec = _fox_specs(t_, fw, col0)

    def body(q_ref, k_ref, v_ref, fc_ref, gq_ref, gk_ref, o_ref, lse_ref, qa, ka, va):
        p = pl.program_id(1)
        _fox_operands(q_ref, k_ref, v_ref, fc_ref, gq_ref[...] * scale, gk_ref[...], p, qa, ka, va)
        diag = lax.broadcasted_iota(jnp.int32, (tq, tq), 0) >= lax.broadcasted_iota(jnp.int32, (tq, tq), 1)
        lane = lax.broadcasted_iota(jnp.int32, (tq, LANE), 1)

        def qblock(i, _):
            rows = pl.ds(pl.multiple_of(i * tq, tq), tq)
            qb = [qa[hh, rows, :] for hh in range(2)]

            def step(j, carry, masked):
                cols = pl.ds(pl.multiple_of(j * tq, tq), tq)
                out = []
                for hh in range(2):
                    m, l, acc = carry[hh]
                    s = _nt(qb[hh], ka[hh, cols, :])
                    if masked:
                        s = jnp.where(diag, s, NEG)
                    m2 = jnp.maximum(m, jnp.max(s, axis=-1, keepdims=True))
                    al = jnp.exp(m - m2)
                    pm = jnp.exp(s - m2)
                    out.append((m2, al * l + jnp.sum(pm, axis=-1, keepdims=True), al * acc + _nn(pm, va[hh, cols, :])))
                return tuple(out)

            init = tuple((jnp.full((tq, 1), NEG, F32), jnp.zeros((tq, 1), F32), jnp.zeros((tq, LANE), F32))
                         for _ in range(2))
            carry = lax.fori_loop(0, i, lambda j, c: step(j, c, False), init)
            (m0, l0, a0), (m1, l1, a1) = step(i, carry, True)
            o_ref[0, rows, :] = jnp.where(lane < FOX_DH, a0 / l0, pltpu.roll(a1 / l1, FOX_DH, 1))
            lse_ref[0, 0, rows, :] = jnp.where(lane == 0, m0 + jnp.log(l0), jnp.where(lane == 1, m1 + jnp.log(l1), 0.0))
            return 0

        lax.fori_loop(0, nb, qblock, 0)

    return pl.pallas_call(
        body, name="fox_fwd", grid=(b_, npair),
        in_specs=[col(0), col(1), col(2), full, gvec, gvec],
        out_specs=[pair, lse_spec],
        out_shape=[jax.ShapeDtypeStruct((b_, t_, fw), F32), jax.ShapeDtypeStruct((b_, npair, t_, LANE), F32)],
        scratch_shapes=[pltpu.VMEM((2, t_, LANE), MXU_DTYPE)] * 3,
        compiler_params=_params(2),
    )(zm, zm, zm, fc, gq2, gk2)


def _norm_bwd(x, dy, g):
    r = lax.rsqrt(jnp.mean(x * x, axis=-1, keepdims=True) + EPS)
    dyg = dy * g
    dx = r * dyg - x * (r * r * r) * jnp.mean(dyg * x, axis=-1, keepdims=True)
    return dx, jnp.sum(dy * x * r, axis=0, keepdims=True)


def _fox_bwd(zm, o, do, lse, fc, gq2, gk2, fw, col0):
    b_, t_, _ = zm.shape
    npair = fw // LANE
    tq = min(FOX_TILE, t_)
    nb = t_ // tq
    scale = FOX_DH ** -0.5
    col, pair, full, gvec, lse_spec = _fox_specs(t_, fw, col0)

    def body(q_ref, k_ref, v_ref, o_ref, do_ref, lse_ref, fc_ref, gq_ref, gk_ref,
             dq_ref, dk_ref, dv_ref, dfc_ref, dgq_ref, dgk_ref, qa, ka, va, da, rowv, dq_acc, dk_acc, dv_acc):
        b, p = pl.program_id(0), pl.program_id(1)
        gq2v, gk2v = gq_ref[...] * scale, gk_ref[...]
        bd, lane, qx, kx, rq, rk = _fox_operands(q_ref, k_ref, v_ref, fc_ref, gq2v, gk2v, p, qa, ka, va)
        head = lane < FOX_DH
        dov = do_ref[0]
        dsum = _dot_right_exact(dov * o_ref[0], bd)
        eye = (lax.broadcasted_iota(jnp.int32, (tq, tq), 0) == lax.broadcasted_iota(jnp.int32, (tq, tq), 1)).astype(F32)
        for hh in range(2):
            da[hh] = jnp.where(head, _head_lanes(dov, hh), 0.0).astype(MXU_DTYPE)
            for blk in range(nb):
                rs = slice(blk * tq, (blk + 1) * tq)
                rowv[2 * hh:2 * hh + 1, rs] = jnp.sum(eye * lse_ref[0, 0, rs, hh:hh + 1], axis=0, keepdims=True)
                rowv[2 * hh + 1:2 * hh + 2, rs] = jnp.sum(eye * dsum[rs, hh * FOX_DH:hh * FOX_DH + 1], axis=0, keepdims=True)
        dq_acc[...] = jnp.zeros(dq_acc.shape, F32)
        valid = lax.broadcasted_iota(jnp.int32, (tq, tq), 1) >= lax.broadcasted_iota(jnp.int32, (tq, tq), 0)

        def kvblock(j, _):
            cols = pl.ds(pl.multiple_of(j * tq, tq), tq)
            kb = [ka[hh, cols, :] for hh in range(2)]
            vb = [va[hh, cols, :] for hh in range(2)]

            def step(i, carry, masked):
                rows = pl.ds(pl.multiple_of(i * tq, tq), tq)
                out = []
                for hh in range(2):
                    dk_a, dv_a = carry[hh]
                    qb, dob = qa[hh, rows, :], da[hh, rows, :]
                    pt = jnp.exp(_nt(kb[hh], qb) - rowv[2 * hh:2 * hh + 1, rows])
                    if masked:
                        pt = jnp.where(valid, pt, 0.0)
                    dst = pt * (_nt(vb[hh], dob) - rowv[2 * hh + 1:2 * hh + 2, rows])
                    dq_acc[hh, rows, :] += _tn(dst, kb[hh])
                    out.append((dk_a + _nn(dst, qb), dv_a + _nn(pt, dob)))
                return tuple(out)

            z = jnp.zeros((tq, LANE), F32)
            carry = step(j, ((z, z), (z, z)), True)
            carry = lax.fori_loop(j + 1, nb, lambda i, c: step(i, c, False), carry)
            for hh in range(2):
                dk_acc[hh, cols, :] = carry[hh][0]
                dv_acc[hh, cols, :] = carry[hh][1]
            return 0

        lax.fori_loop(0, nb, kvblock, 0)
        dq0, dq1, dk0, dk1 = dq_acc[0], dq_acc[1], dk_acc[0], dk_acc[1]
        dqn = jnp.where(head, dq0, pltpu.roll(dq1, FOX_DH, 1))
        dkn = jnp.where(head, dk0, pltpu.roll(dk1, FOX_DH, 1))
        dqx, gq_part = _pair_norm_bwd(qx, rq, dqn, gq2v, bd)
        dkx, gk_part = _pair_norm_bwd(kx, rk, dkn, gk2v, bd)
        dq_ref[0] = dqx.astype(dq_ref.dtype)
        dk_ref[0] = dkx.astype(dk_ref.dtype)
        dv_ref[0] = jnp.where(head, dv_acc[0], pltpu.roll(dv_acc[1], FOX_DH, 1)).astype(dv_ref.dtype)

        def bias_grad(dqh, dkh):
            return (jnp.sum(jnp.where(lane == AUG + 3, dqh, 0.0), axis=-1, keepdims=True)
                    - jnp.sum(jnp.where(lane == AUG, dkh, 0.0), axis=-1, keepdims=True))

        dfc_ref[0, 0] = jnp.where(lane == 0, bias_grad(dq0, dk0), jnp.where(lane == 1, bias_grad(dq1, dk1), 0.0))
        first = jnp.logical_and(b == 0, p == 0)
        _acc(dgq_ref, gq_part * scale, first)
        _acc(dgk_ref, gk_part, first)

    sds = jax.ShapeDtypeStruct((b_, t_, fw), MXU_DTYPE)
    gs = jax.ShapeDtypeStruct((1, LANE), F32)
    return pl.pallas_call(
        body, name="fox_bwd", grid=(b_, npair),
        in_specs=[col(0), col(1), col(2), pair, pair, lse_spec, full, gvec, gvec],
        out_specs=[pair, pair, pair, lse_spec, gvec, gvec],
        out_shape=[sds, sds, sds, jax.ShapeDtypeStruct((b_, npair, t_, LANE), F32), gs, gs],
        scratch_shapes=[pltpu.VMEM((2, t_, LANE), MXU_DTYPE)] * 4
        + [pltpu.VMEM((8, t_), F32)] + [pltpu.VMEM((2, t_, LANE), F32)] * 3,
        compiler_params=_params(2),
    )(zm, zm, zm, o, do, lse, fc, gq2, gk2)


def _mem_specs(t_, m_, mw, col0):
    nh = mw // LANE
    qcol = pl.BlockSpec((1, t_, LANE), lambda b, h: (b, 0, col0 + h))
    kcol = pl.BlockSpec((1, m_, LANE), lambda b, h: (b, 0, h))
    vcol = pl.BlockSpec((1, m_, LANE), lambda b, h: (b, 0, nh + h))
    ycol = pl.BlockSpec((1, t_, LANE), lambda b, h: (b, 0, h))
    gvec = pl.BlockSpec((1, LANE), lambda b, h: (0, 0))
    return qcol, kcol, vcol, ycol, gvec


def _mem_fwd(zm, mkv, gq, gk, mw, col0):
    b_, t_, _ = zm.shape
    m_ = mkv.shape[1]
    tq = min(512, t_)
    nb = t_ // tq
    scale = MEM_DH ** -0.5
    qcol, kcol, vcol, ycol, gvec = _mem_specs(t_, m_, mw, col0)

    def body(q_ref, k_ref, v_ref, gq_ref, gk_ref, y_ref):
        gqv, gkv = gq_ref[...] * scale, gk_ref[...]
        kv = k_ref[0]
        kn = _mx(kv * lax.rsqrt(jnp.mean(kv * kv, axis=-1, keepdims=True) + EPS) * gkv)
        vv = _mx(v_ref[0])

        def blk(i, _):
            rows = pl.ds(pl.multiple_of(i * tq, tq), tq)
            qv = q_ref[0, rows, :]
            s = _nt(qv * lax.rsqrt(jnp.mean(qv * qv, axis=-1, keepdims=True) + EPS) * gqv, kn)
            e = jnp.exp(s - jnp.max(s, axis=-1, keepdims=True))
            y_ref[0, rows, :] = _nn(e / jnp.sum(e, axis=-1, keepdims=True), vv)
            return 0

        lax.fori_loop(0, nb, blk, 0)

    return pl.pallas_call(
        body, name="mem_fwd", grid=(b_, MEM_HEADS), in_specs=[qcol, kcol, vcol, gvec, gvec], out_specs=ycol,
        out_shape=jax.ShapeDtypeStruct((b_, t_, mw), F32), compiler_params=_params(2),
    )(zm, mkv, mkv, gq, gk)


def _mem_bwd(zm, mkv, dy, gq, gk, mw, col0):
    b_, t_, _ = zm.shape
    m_ = mkv.shape[1]
    tq = min(512, t_)
    nb = t_ // tq
    scale = MEM_DH ** -0.5
    qcol, kcol, vcol, ycol, gvec = _mem_specs(t_, m_, mw, col0)

    def body(q_ref, k_ref, v_ref, dy_ref, gq_ref, gk_ref, dq_ref, dk_ref, dv_ref, dgq_ref, dgk_ref):
        gqv, gkv = gq_ref[...] * scale, gk_ref[...]
        kv = k_ref[0]
        kn = _mx(kv * lax.rsqrt(jnp.mean(kv * kv, axis=-1, keepdims=True) + EPS) * gkv)
        vv = _mx(v_ref[0])

        def blk(i, carry):
            dkn, dvv, dgq = carry
            rows = pl.ds(pl.multiple_of(i * tq, tq), tq)
            qv = q_ref[0, rows, :]
            qn = _mx(qv * lax.rsqrt(jnp.mean(qv * qv, axis=-1, keepdims=True) + EPS) * gqv)
            s = _nt(qn, kn)
            e = jnp.exp(s - jnp.max(s, axis=-1, keepdims=True))
            pm = e / jnp.sum(e, axis=-1, keepdims=True)
            dob = _mx(dy_ref[0, rows, :])
            dp = _nt(dob, vv)
            ds = pm * (dp - jnp.sum(dp * pm, axis=-1, keepdims=True))
            dqv, gq_part = _norm_bwd(qv, _nn(ds, kn), gqv)
            dq_ref[0, rows, :] = dqv.astype(dq_ref.dtype)
            return dkn + _tn(ds, qn), dvv + _tn(pm, dob), dgq + gq_part * scale

        z = jnp.zeros((m_, LANE), F32)
        dkn, dvv, dgq = lax.fori_loop(0, nb, blk, (z, z, jnp.zeros((1, LANE), F32)))
        dkv, dgk = _norm_bwd(kv, dkn, gkv)
        dk_ref[0] = dkv
        dv_ref[0] = dvv
        first = jnp.logical_and(pl.program_id(0) == 0, pl.program_id(1) == 0)
        _acc(dgq_ref, dgq, first)
        _acc(dgk_ref, dgk, first)

    kblk = pl.BlockSpec((1, m_, LANE), lambda b, h: (b, 0, h))
    gs = jax.ShapeDtypeStruct((1, LANE), F32)
    ks = jax.ShapeDtypeStruct((b_, m_, mw), F32)
    return pl.pallas_call(
        body, name="mem_bwd", grid=(b_, MEM_HEADS), in_specs=[qcol, kcol, vcol, ycol, gvec, gvec],
        out_specs=[ycol, kblk, kblk, gvec, gvec],
        out_shape=[jax.ShapeDtypeStruct((b_, t_, mw), MXU_DTYPE), ks, ks, gs, gs], compiler_params=_params(2),
    )(zm, mkv, mkv, dy, gq, gk)


def _merge_specs(tm, d, w, gcol):
    row_d = pl.BlockSpec((tm, d), lambda i: (i, 0))
    row_w = pl.BlockSpec((tm, w), lambda i: (i, 0))
    gates = [pl.BlockSpec((tm, d), functools.partial(lambda i, k: (i, gcol + k), k=k)) for k in range(3)]
    w_br = pl.BlockSpec((w, d), lambda i: (0, 0))
    w_o = pl.BlockSpec((d, d), lambda i: (0, 0))
    return row_d, row_w, gates, w_br, w_o


def _merge_fwd(x, ys, zm, w_brs, w_out, gcol, tm=256):
    n, d = x.shape
    w = ys[0].shape[1]
    tm = _tile(n, tm, 8)
    row_d, row_w, gates, w_br, w_o = _merge_specs(tm, d, w, gcol)

    def body(x_ref, ya, yb, yc, g0, g1, g2, wa, wb, wc, wo, x1_ref, mg_ref):
        mg = (_sig(g0[...]) * _nn(ya[...], wa[...]) + _sig(g1[...]) * _nn(yb[...], wb[...])
              + _sig(g2[...]) * _nn(yc[...], wc[...]))
        mg_ref[...] = mg.astype(mg_ref.dtype)
        x1_ref[...] = x_ref[...] + _nn(mg, wo[...])

    return pl.pallas_call(
        body, name="merge_fwd", grid=(n // tm,),
        in_specs=[row_d, row_w, row_w, row_w] + gates + [w_br, w_br, w_br, w_o],
        out_specs=[row_d, row_d],
        out_shape=[jax.ShapeDtypeStruct((n, d), F32), jax.ShapeDtypeStruct((n, d), MXU_DTYPE)],
        compiler_params=_params(1),
    )(x, *ys, zm, zm, zm, *w_brs, w_out)


def _merge_bwd(dx1, ys, zm, w_brs, w_out, gcol, tm=256):
    n, d = dx1.shape
    w = ys[0].shape[1]
    tm = _tile(n, tm, 8)
    row_d, row_w, gates, w_br, w_o = _merge_specs(tm, d, w, gcol)

    def body(dx_ref, ya, yb, yc, g0, g1, g2, wa, wb, wc, wo, dgl_ref, dpa, dpb, dpc, dya, dyb, dyc):
        dm = _nt(dx_ref[...], wo[...])
        for k, (y, g, wr, dp_ref, dy_ref) in enumerate(((ya, g0, wa, dpa, dya), (yb, g1, wb, dpb, dyb),
                                                        (yc, g2, wc, dpc, dyc))):
            sg = _sig(g[...])
            pr = _nn(y[...], wr[...])
            dgl_ref[:, k * d:(k + 1) * d] = (dm * pr * sg * (1.0 - sg)).astype(dgl_ref.dtype)
            dp = (dm * sg).astype(dp_ref.dtype)
            dp_ref[...] = dp
            dy_ref[...] = _nt(dp, wr[...])

    sd = jax.ShapeDtypeStruct((n, d), MXU_DTYPE)
    sw = jax.ShapeDtypeStruct((n, w), F32)
    return pl.pallas_call(
        body, name="merge_bwd", grid=(n // tm,),
        in_specs=[row_d, row_w, row_w, row_w] + gates + [w_br, w_br, w_br, w_o],
        out_specs=[pl.BlockSpec((tm, 3 * d), lambda i: (i, 0)), row_d, row_d, row_d, row_w, row_w, row_w],
        out_shape=[jax.ShapeDtypeStruct((n, 3 * d), MXU_DTYPE), sd, sd, sd, sw, sw, sw],
        compiler_params=_params(1),
    )(dx1, *ys, zm, zm, zm, *w_brs, w_out)


CONV_ROWS = 256
HALO = 8


def _ext(ref, r0, t_, lo, hi):
    rc = min(CONV_ROWS, t_)
    a, b = max(r0 - HALO, 0), min(r0 + rc + HALO, t_)
    parts = []
    if r0 - HALO < 0:
        parts.append(jnp.zeros((HALO, hi - lo), F32))
    parts.append(ref[0, a:b, lo:hi].astype(F32))
    if r0 + rc + HALO > t_:
        parts.append(jnp.zeros((HALO, hi - lo), F32))
    return jnp.concatenate(parts, axis=0) if len(parts) > 1 else parts[0]


def _gelu_parts(ac):
    cdf = 0.5 * (1.0 + _erf(ac * (2.0 ** -0.5)))
    pdf = jnp.exp(-0.5 * ac * ac) * ((2.0 * math.pi) ** -0.5)
    return cdf, pdf


def _conv_taps(a_ext, cw, cb):
    return cw[0:1, :] * pltpu.roll(a_ext, 2, 0) + cw[1:2, :] * pltpu.roll(a_ext, 1, 0) + cw[2:3, :] * a_ext + cb


def _glu_fwd(u, cw, cb):
    b_, t_, f2 = u.shape
    f = f2 // 2
    g = min(FFN_GROUP, f)
    rc = min(CONV_ROWS, t_)

    def body(u_ref, cw_ref, cb_ref, y_ref):
        cwv, cbv = cw_ref[...], cb_ref[...]
        for r0 in range(0, t_, rc):
            ac = _conv_taps(_ext(u_ref, r0, t_, 0, g), cwv, cbv)[HALO:HALO + rc]
            cdf, _ = _gelu_parts(ac)
            y_ref[0, r0:r0 + rc, :] = (ac * cdf * u_ref[0, r0:r0 + rc, g:2 * g]).astype(y_ref.dtype)

    return pl.pallas_call(
        body, name="glu_fwd", grid=(f // g, b_),
        in_specs=[pl.BlockSpec((1, t_, 2 * g), lambda j, b: (b, 0, j)), pl.BlockSpec((3, g), lambda j, b: (0, j)),
                  pl.BlockSpec((1, g), lambda j, b: (0, j))],
        out_specs=pl.BlockSpec((1, t_, g), lambda j, b: (b, 0, j)),
        out_shape=jax.ShapeDtypeStruct((b_, t_, f), MXU_DTYPE), compiler_params=_params(2),
    )(u, cw, cb)


def _glu_bwd(u, dy, cw, cb):
    b_, t_, f2 = u.shape
    f = f2 // 2
    g = min(FFN_GROUP, f)
    rc = min(CONV_ROWS, t_)
    ne = rc + 2 * HALO

    def body(u_ref, dy_ref, cw_ref, cb_ref, du_ref, dcw_ref, dcb_ref):
        cwv, cbv = cw_ref[...], cb_ref[...]
        dcw = [jnp.zeros((1, g), F32) for _ in range(3)]
        dcb = jnp.zeros((1, g), F32)
        for r0 in range(0, t_, rc):
            a_ext = _ext(u_ref, r0, t_, 0, g)
            v_ext = _ext(u_ref, r0, t_, g, 2 * g)
            dy_ext = _ext(dy_ref, r0, t_, 0, g)
            ac = _conv_taps(a_ext, cwv, cbv)
            cdf, pdf = _gelu_parts(ac)
            dac = dy_ext * v_ext * (cdf + ac * pdf)
            da = cwv[2:3, :] * dac + cwv[1:2, :] * pltpu.roll(dac, ne - 1, 0) + cwv[0:1, :] * pltpu.roll(dac, ne - 2, 0)
            mid = slice(HALO, HALO + rc)
            du_ref[0, r0:r0 + rc, 0:g] = da[mid].astype(du_ref.dtype)
            du_ref[0, r0:r0 + rc, g:2 * g] = (dy_ext[mid] * ac[mid] * cdf[mid]).astype(du_ref.dtype)
            dacm = dac[mid]
            dcw[0] = dcw[0] + jnp.sum(dacm * pltpu.roll(a_ext, 2, 0)[mid], axis=0, keepdims=True)
            dcw[1] = dcw[1] + jnp.sum(dacm * pltpu.roll(a_ext, 1, 0)[mid], axis=0, keepdims=True)
            dcw[2] = dcw[2] + jnp.sum(dacm * a_ext[mid], axis=0, keepdims=True)
            dcb = dcb + jnp.sum(dacm, axis=0, keepdims=True)
        first = pl.program_id(1) == 0
        _acc(dcw_ref, jnp.concatenate(dcw, axis=0), first)
        _acc(dcb_ref, dcb, first)

    ublk = pl.BlockSpec((1, t_, 2 * g), lambda j, b: (b, 0, j))
    cwb = pl.BlockSpec((3, g), lambda j, b: (0, j))
    cbb = pl.BlockSpec((1, g), lambda j, b: (0, j))
    return pl.pallas_call(
        body, name="glu_bwd", grid=(f // g, b_),
        in_specs=[ublk, pl.BlockSpec((1, t_, g), lambda j, b: (b, 0, j)), cwb, cbb],
        out_specs=[ublk, cwb, cbb],
        out_shape=[jax.ShapeDtypeStruct((b_, t_, f2), MXU_DTYPE), jax.ShapeDtypeStruct((3, f), F32),
                   jax.ShapeDtypeStruct((1, f), F32)],
        compiler_params=_params(2),
    )(u, dy, cw, cb)


def _loss_head(x1, ffn, target, tm=512):
    n, d = x1.shape
    tm = _tile(n, tm, 8)

    def body(x_ref, f_ref, t_ref, dy_ref, l_ref):
        err = x_ref[...] + f_ref[...] - t_ref[...]
        dy_ref[...] = err * (1.0 / d)
        _acc(l_ref, jnp.sum(err * err, axis=0, keepdims=True) * (0.5 / d), pl.program_id(0) == 0)

    row = pl.BlockSpec((tm, d), lambda i: (i, 0))
    vec = pl.BlockSpec((1, d), lambda i: (0, 0))
    return pl.pallas_call(
        body, name="loss_head", grid=(n // tm,), in_specs=[row, row, row], out_specs=[row, vec],
        out_shape=[jax.ShapeDtypeStruct((n, d), F32), jax.ShapeDtypeStruct((1, d), F32)], compiler_params=_params(1),
    )(x1, ffn, target)


def _place():
    x, y, c = lax.axis_index("x"), lax.axis_index("y"), lax.axis_index("c")
    chips = [(1 - x, y), (x, 1 - y), (1 - x, 1 - y)]
    return x, y, c, chips


def _remote(src, dst, send_sem, recv_sem, to):
    return pltpu.make_async_remote_copy(src_ref=src, dst_ref=dst, send_sem=send_sem, recv_sem=recv_sem,
                                        device_id=to, device_id_type=MESH)


STACK, COLS = "stack", "cols"


def _shard_ref(ref, kind, s, rows, c):
    if kind == COLS:
        cols = pl.ds(pl.multiple_of(s * c, LANE), c)
        return ref.at[:, cols] if rows is None else ref.at[rows, cols]
    return ref.at[s] if rows is None else ref.at[s, rows, :]


def _halves(c, half):
    mine = pl.ds(pl.multiple_of(c * half, 16), half)
    theirs = pl.ds(pl.multiple_of((1 - c) * half, 16), half)
    return mine, theirs


def _gather_shards(shards, kinds):
    nw = len(shards)

    def body(*refs):
        ins, outs = refs[:nw], refs[nw:2 * nw]
        send_sems, recv_sems, local_sems = refs[2 * nw:]
        x, y, c, chips = _place()
        me, sib = 2 * x + y, (x, y, 1 - c)
        owns, first, passed = [], [], []
        for i, (w_ref, o_ref, kind) in enumerate(zip(ins, outs, kinds)):
            r, cw = w_ref.shape
            mine, _ = _halves(c, r // 2)
            owns.append(pltpu.make_async_copy(w_ref, _shard_ref(o_ref, kind, me, None, cw), local_sems.at[i]))
            owns[-1].start()
            for j, chip in enumerate(chips):
                first.append(_remote(w_ref.at[mine], _shard_ref(o_ref, kind, me, mine, cw), send_sems.at[6 * i + j],
                                     recv_sems.at[6 * i + j], (*chip, c)))
                first[-1].start()
        for i, (w_ref, o_ref, kind) in enumerate(zip(ins, outs, kinds)):
            r, cw = w_ref.shape
            mine, _ = _halves(c, r // 2)
            for j, (px, py) in enumerate(chips):
                blk = _shard_ref(o_ref, kind, 2 * px + py, mine, cw)
                _remote(blk, blk, send_sems.at[6 * i + j], recv_sems.at[6 * i + j], sib).wait_recv()
                passed.append(_remote(blk, blk, send_sems.at[6 * i + 3 + j], recv_sems.at[6 * i + 3 + j], sib))
                passed[-1].start()
        for i, (w_ref, o_ref, kind) in enumerate(zip(ins, outs, kinds)):
            r, cw = w_ref.shape
            _, theirs = _halves(c, r // 2)
            for j, (px, py) in enumerate(chips):
                blk = _shard_ref(o_ref, kind, 2 * px + py, theirs, cw)
                _remote(blk, blk, send_sems.at[6 * i + 3 + j], recv_sems.at[6 * i + 3 + j], sib).wait_recv()
        for cp in first + passed:
            cp.wait_send()
        for cp in owns:
            cp.wait()

    def out_sds(a, kind):
        r, c = a.shape
        return jax.ShapeDtypeStruct((r, N_CHIPS * c) if kind == COLS else (N_CHIPS, r, c), a.dtype)

    return pl.pallas_call(
        body, name="gather_shards", in_specs=[ANY] * nw, out_specs=[ANY] * nw,
        out_shape=[out_sds(a, k) for a, k in zip(shards, kinds)],
        scratch_shapes=[pltpu.SemaphoreType.DMA((6 * nw,)), pltpu.SemaphoreType.DMA((6 * nw,)),
                        pltpu.SemaphoreType.DMA((nw,))],
    )(*shards)


def _half_shape(g, kind):
    if kind == COLS:
        return (g.shape[0] // 2, g.shape[1])
    return (g.shape[0], g.shape[1] // 2, g.shape[2])


def _pair_swap_halves(gs, kinds):
    nw = len(gs)

    def body(*refs):
        ins, outs = refs[:nw], refs[nw:2 * nw]
        send_sems, recv_sems = refs[2 * nw:]
        x, y, c, _ = _place()
        cps = []
        for i, (g_ref, a_ref, kind) in enumerate(zip(ins, outs, kinds)):
            r = g_ref.shape[0] if kind == COLS else g_ref.shape[1]
            _, theirs = _halves(c, r // 2)
            src = g_ref.at[theirs] if kind == COLS else g_ref.at[:, theirs]
            cps.append(_remote(src, a_ref, send_sems.at[i], recv_sems.at[i], (x, y, 1 - c)))
            cps[-1].start()
        for cp in cps:
            cp.wait()

    return pl.pallas_call(
        body, name="pair_swap_halves", in_specs=[ANY] * nw, out_specs=[ANY] * nw,
        out_shape=[jax.ShapeDtypeStruct(_half_shape(g, k), g.dtype) for g, k in zip(gs, kinds)],
        scratch_shapes=[pltpu.SemaphoreType.DMA((nw,)), pltpu.SemaphoreType.DMA((nw,))],
    )(*gs)


def _row_tile(rows, width, itemsize=4, target=2 ** 21):
    return _tile(rows, max(8, target // (width * itemsize)), 8)


def _add_half(g, a, kind, c_idx, name):
    if kind == COLS:
        half, wd = a.shape
        tr = _row_tile(half, wd)
        nblk = half // tr
        grid = (nblk,)
        g_spec = pl.BlockSpec((tr, wd), lambda i, c_ref: (c_ref[0] * nblk + i, 0))
        a_spec = pl.BlockSpec((tr, wd), lambda i, c_ref: (i, 0))
    else:
        n, half, wd = a.shape
        tr = _row_tile(half, wd)
        nblk = half // tr
        grid = (n, nblk)
        g_spec = pl.BlockSpec((1, tr, wd), lambda s, i, c_ref: (s, c_ref[0] * nblk + i, 0))
        a_spec = pl.BlockSpec((1, tr, wd), lambda s, i, c_ref: (s, i, 0))

    def body(c_ref, g_ref, a_ref, o_ref):
        o_ref[...] = (g_ref[...] + a_ref[...]).astype(o_ref.dtype)

    return pl.pallas_call(
        body, name=name,
        grid_spec=pltpu.PrefetchScalarGridSpec(num_scalar_prefetch=1, grid=grid, in_specs=[g_spec, a_spec],
                                               out_specs=a_spec),
        out_shape=jax.ShapeDtypeStruct(a.shape, EXCHANGE_DTYPE), compiler_params=_params(len(grid)),
    )(c_idx, g, a)


def _chip_exchange(ps, kinds):
    nw = len(ps)

    def shard_shape(p, kind):
        return (p.shape[0], p.shape[1] // N_CHIPS) if kind == COLS else p.shape[1:]

    def body(*refs):
        ins, outs = refs[:nw], refs[nw:2 * nw]
        send_sems, recv_sems, local_sems = refs[2 * nw:]
        x, y, c, chips = _place()
        me = 2 * x + y
        owns, sent = [], []
        for i, (p_ref, b_ref, kind) in enumerate(zip(ins, outs, kinds)):
            cw = b_ref.shape[2]
            owns.append(pltpu.make_async_copy(_shard_ref(p_ref, kind, me, None, cw), b_ref.at[me], local_sems.at[i]))
            owns[-1].start()
            for j, (px, py) in enumerate(chips):
                sent.append(_remote(_shard_ref(p_ref, kind, 2 * px + py, None, cw), b_ref.at[me],
                                    send_sems.at[3 * i + j], recv_sems.at[3 * i + j], (px, py, c)))
                sent[-1].start()
        for i, b_ref in enumerate(outs):
            for j, (px, py) in enumerate(chips):
                blk = b_ref.at[2 * px + py]
                _remote(blk, blk, send_sems.at[3 * i + j], recv_sems.at[3 * i + j], (px, py, c)).wait_recv()
        for cp in sent:
            cp.wait_send()
        for cp in owns:
            cp.wait()

    return pl.pallas_call(
        body, name="chip_exchange", in_specs=[ANY] * nw, out_specs=[ANY] * nw,
        out_shape=[jax.ShapeDtypeStruct((N_CHIPS,) + tuple(shard_shape(p, k)), p.dtype) for p, k in zip(ps, kinds)],
        scratch_shapes=[pltpu.SemaphoreType.DMA((3 * nw,)), pltpu.SemaphoreType.DMA((3 * nw,)),
                        pltpu.SemaphoreType.DMA((nw,))],
    )(*ps)


def _sum_chips(bq, name):
    n, h, wd = bq.shape
    tr = _row_tile(h, wd * n)

    def body(b_ref, o_ref):
        acc = b_ref[0].astype(F32)
        for s in range(1, n):
            acc = acc + b_ref[s].astype(F32)
        o_ref[...] = acc

    return pl.pallas_call(
        body, name=name, grid=(h // tr,),
        in_specs=[pl.BlockSpec((n, tr, wd), lambda i: (0, i, 0))], out_specs=pl.BlockSpec((tr, wd), lambda i: (i, 0)),
        out_shape=jax.ShapeDtypeStruct((h, wd), F32), compiler_params=_params(1),
    )(bq)


def _pair_join_halves(qs):
    nw = len(qs)

    def body(*refs):
        ins, outs = refs[:nw], refs[nw:2 * nw]
        send_sems, recv_sems, local_sems = refs[2 * nw:]
        x, y, c, _ = _place()
        owns, sent = [], []
        for i, (q_ref, o_ref) in enumerate(zip(ins, outs)):
            mine, _ = _halves(c, q_ref.shape[0])
            owns.append(pltpu.make_async_copy(q_ref, o_ref.at[mine], local_sems.at[i]))
            owns[-1].start()
            sent.append(_remote(q_ref, o_ref.at[mine], send_sems.at[i], recv_sems.at[i], (x, y, 1 - c)))
            sent[-1].start()
        for i, (q_ref, o_ref) in enumerate(zip(ins, outs)):
            _, theirs = _halves(c, q_ref.shape[0])
            _remote(q_ref, o_ref.at[theirs], send_sems.at[i], recv_sems.at[i], (x, y, 1 - c)).wait_recv()
        for cp in sent:
            cp.wait_send()
        for cp in owns:
            cp.wait()

    return pl.pallas_call(
        body, name="pair_join_halves", in_specs=[ANY] * nw, out_specs=[ANY] * nw,
        out_shape=[jax.ShapeDtypeStruct((2 * q.shape[0], q.shape[1]), q.dtype) for q in qs],
        scratch_shapes=[pltpu.SemaphoreType.DMA((nw,)), pltpu.SemaphoreType.DMA((nw,)), pltpu.SemaphoreType.DMA((nw,))],
    )(*qs)


def _all_sum_small(s, name):
    sr, w = s.shape

    def body(s_ref, o_ref, buf, send_sems, recv_sems):
        x, y, c, _ = _place()
        me = 4 * x + 2 * y + c
        buf[me] = s_ref[...]
        peers = []
        for k in range(1, 8):
            px = 1 - x if k & 4 else x
            py = 1 - y if k & 2 else y
            pc = 1 - c if k & 1 else c
            peers.append((px, py, pc))
        sent = [_remote(s_ref, buf.at[me], send_sems.at[k], recv_sems.at[k], peer) for k, peer in enumerate(peers)]
        for cp in sent:
            cp.start()
        for k, (px, py, pc) in enumerate(peers):
            _remote(s_ref, buf.at[4 * px + 2 * py + pc], send_sems.at[k], recv_sems.at[k], (px, py, pc)).wait_recv()
        for cp in sent:
            cp.wait_send()
        acc = buf[0]
        for d in range(1, 8):
            acc = acc + buf[d]
        o_ref[...] = acc

    vm = pl.BlockSpec(memory_space=pltpu.VMEM)
    return pl.pallas_call(
        body, name=name, in_specs=[vm], out_specs=vm, out_shape=jax.ShapeDtypeStruct((sr, w), F32),
        scratch_shapes=[pltpu.VMEM((8, sr, w), F32), pltpu.SemaphoreType.DMA((7,)), pltpu.SemaphoreType.DMA((7,))],
    )(s)


BIG = ("w_in", "mem_kv_w", "w_br_hgrn", "w_br_fox", "w_br_mem", "w_out", "ffn_w_up", "ffn_w_down")
KIND = {"w_in": STACK, "mem_kv_w": STACK, "w_br_hgrn": COLS, "w_br_fox": COLS, "w_br_mem": COLS, "w_out": STACK,
        "ffn_w_up": COLS, "ffn_w_down": STACK}
ROW_SHARDED = ("mem_kv_w", "w_out", "ffn_w_down")


def _w_in_pieces(cs, s1, nf):
    out = []
    for s in range(N_CHIPS):
        lo, hi = cs * s, cs * (s + 1)
        for a, b, forget in ((lo, min(hi, s1), False), (max(lo, s1), min(hi, s1 + nf), True), (max(lo, s1 + nf), hi, False)):
            if a < b:
                out.append((s, a - lo, b - lo, forget, a - s1 if forget else (a if a < s1 else a - nf)))
    return out


def _split_w_in(stacked, s1, nf):
    pieces = _w_in_pieces(stacked.shape[2], s1, nf)
    main = [stacked[s, :, a:b] for s, a, b, forget, _ in pieces if not forget]
    ff = [stacked[s, :, a:b] for s, a, b, forget, _ in pieces if forget]
    return jnp.concatenate(main, axis=1), jnp.concatenate(ff, axis=1)


def _join_w_in(g_main, g_ff, s1, nf):
    cs = (g_main.shape[1] + nf) // N_CHIPS
    shards = [[] for _ in range(N_CHIPS)]
    for s, a, b, forget, off in _w_in_pieces(cs, s1, nf):
        shards[s].append((g_ff if forget else g_main)[:, off:off + b - a])
    return jnp.stack([jnp.concatenate(p, axis=1) if len(p) > 1 else p[0] for p in shards])


SMALL = ("norm_mix_g", "norm_mem_g", "norm_ffn_g", "hgrn_lb_logits", "hgrn_norm_g", "fox_f_bias", "fox_q_norm_g",
         "fox_k_norm_g", "mem_q_norm_g", "mem_k_norm_g", "ffn_conv_b")


def _pack_small(vals):
    flats, total = [], 0
    for v in vals:
        flat = v.reshape(-1).astype(F32)
        n = -(-flat.shape[0] // FLAT_W)
        flats.append(jnp.pad(flat, (0, n * FLAT_W - flat.shape[0])))
        total += n
    if -total % 8:
        flats.append(jnp.zeros((-total % 8 * FLAT_W,), F32))
    return jnp.concatenate(flats).reshape(-1, FLAT_W)


def _unpack_small(buf, shapes):
    res, off = [], 0
    for shp in shapes:
        numel = math.prod(shp)
        n = -(-numel // FLAT_W)
        res.append(buf[off:off + n].reshape(-1)[:numel].reshape(shp))
        off += n
    return res


def _pad_lanes(v, width=LANE):
    return jnp.pad(v, ((0, 0), (0, width - v.shape[1])))


WEIGHTS = ("norm_mix_g", "norm_mem_g", "w_in", "hgrn_lb_logits", "hgrn_norm_g", "fox_f_bias", "fox_q_norm_g",
           "fox_k_norm_g", "mem_kv_w", "mem_q_norm_g", "mem_k_norm_g", "w_br_hgrn", "w_br_fox", "w_br_mem", "w_out",
           "norm_ffn_g", "ffn_w_up", "ffn_conv_w", "ffn_conv_b", "ffn_w_down")


def _local_step(x, mem, target, w, full, conv_w):
    b_, t_, d = x.shape
    n = b_ * t_
    hw, fw, mw = HG_HEADS * HG_D, FOX_HEADS * FOX_DH, MEM_HEADS * MEM_DH
    m_ = mem.shape[1]
    f = conv_w.shape[1]
    g = min(FFN_GROUP, f)
    s1 = 4 * hw + 3 * fw
    fox_col, mem_col, gate_col = 4 * hw // LANE, s1 // LANE, (s1 + mw) // d

    w_main, w_ff = _split_w_in(full["w_in"], s1, FOX_HEADS)
    w_ff = _pad_lanes(w_ff)
    w_up = full["ffn_w_up"].reshape(d, 2, f // g, g).transpose(0, 2, 1, 3).reshape(d, 2 * f)
    w_brs = [full["w_br_hgrn"], full["w_br_fox"], full["w_br_mem"]]
    w_out, w_kv, w_down = full["w_out"], full["mem_kv_w"], full["ffn_w_down"]
    f_bias = _pad_lanes(w["fox_f_bias"])
    cb = w["ffn_conv_b"]

    x2 = x.reshape(n, d)
    h = _rmsnorm_fwd(x2, w["norm_mix_g"], name="norm_mix_fwd")
    zm = _matmul(h, w_main, name="in_proj")
    zf = _matmul(h, w_ff, name="in_proj_forget")
    zm3, zf3 = zm.reshape(b_, t_, -1), zf.reshape(b_, t_, LANE)
    ya = _hgrn_fwd(zm3, w["hgrn_lb_logits"], w["hgrn_norm_g"], hw)
    fc = _fox_prep(zf3, f_bias)
    fox_gq, fox_gk = jnp.tile(w["fox_q_norm_g"], (1, 2)), jnp.tile(w["fox_k_norm_g"], (1, 2))
    yb, lse = _fox_fwd(zm3, fc, fox_gq, fox_gk, fw, fox_col)
    mem2 = mem.reshape(b_ * m_, d)
    hm = _rmsnorm_fwd(mem2, w["norm_mem_g"], name="norm_mem_fwd")
    mkv = _matmul(hm, w_kv, name="mem_kv_proj").reshape(b_, m_, 2 * mw)
    yc = _mem_fwd(zm3, mkv, w["mem_q_norm_g"], w["mem_k_norm_g"], mw, mem_col)
    ys = [ya.reshape(n, hw), yb.reshape(n, fw), yc.reshape(n, mw)]
    x1, merged = _merge_fwd(x2, ys, zm, w_brs, w_out, gate_col)
    h2 = _rmsnorm_fwd(x1, w["norm_ffn_g"], name="norm_ffn_fwd")
    u = _matmul(h2, w_up, name="ffn_up")
    u3 = u.reshape(b_, t_, 2 * f)
    yff = _glu_fwd(u3, conv_w, cb).reshape(n, f)
    ffn = _matmul(yff, w_down, name="ffn_down", tk=1408)
    dy, loss_vec = _loss_head(x1, ffn, target.reshape(n, d))

    grads = {}
    dyff = _matmul(dy, w_down, tb=True, name="ffn_down_dx", tn=1408)
    grads["ffn_w_down"] = _matmul(yff, dy, ta=True, name="ffn_down_dw", tm=1408)
    du, grads["ffn_conv_w"], grads["ffn_conv_b"] = _glu_bwd(u3, dyff.reshape(b_, t_, f), conv_w, cb)
    du2 = du.reshape(n, 2 * f)
    dh2 = _matmul(du2, w_up, tb=True, name="ffn_up_dx")
    g_up = _matmul(h2, du2, ta=True, name="ffn_up_dw")
    grads["ffn_w_up"] = g_up.reshape(d, f // g, 2, g).transpose(0, 2, 1, 3).reshape(d, 2 * f)
    dx1, grads["norm_ffn_g"] = _rmsnorm_bwd(x1, [dh2], w["norm_ffn_g"], dy, name="norm_ffn_bwd")

    dgl, dpa, dpb, dpc, dya, dyb, dyc = _merge_bwd(dx1, ys, zm, w_brs, w_out, gate_col)
    grads["w_out"] = _matmul(merged, dx1, ta=True, name="out_proj_dw")
    for nm, y_, dp_ in zip(("w_br_hgrn", "w_br_fox", "w_br_mem"), ys, (dpa, dpb, dpc)):
        grads[nm] = _matmul(y_, dp_, ta=True, name=nm + "_dw")

    dmq, dmk, dmv, grads["mem_q_norm_g"], grads["mem_k_norm_g"] = _mem_bwd(
        zm3, mkv, dyc.reshape(b_, t_, mw), w["mem_q_norm_g"], w["mem_k_norm_g"], mw, mem_col)
    dmkv = jnp.concatenate([dmk, dmv], axis=-1).reshape(b_ * m_, 2 * mw)
    grads["mem_kv_w"] = _matmul(hm, dmkv, ta=True, name="mem_kv_dw")
    dhm = _matmul(dmkv, w_kv, tb=True, name="mem_kv_dx")
    _, grads["norm_mem_g"] = _rmsnorm_bwd(mem2, [dhm], w["norm_mem_g"], None, name="norm_mem_bwd")

    dfq, dfk, dfv, dfc, g_fq, g_fk = _fox_bwd(zm3, yb, dyb.reshape(b_, t_, fw), lse, fc, fox_gq, fox_gk, fw, fox_col)
    grads["fox_q_norm_g"] = g_fq[:, :FOX_DH] + g_fq[:, FOX_DH:]
    grads["fox_k_norm_g"] = g_fk[:, :FOX_DH] + g_fk[:, FOX_DH:]
    dfc = dfc[..., :2].transpose(0, 2, 1, 3).reshape(b_, t_, FOX_HEADS)
    dfc = jnp.pad(dfc, ((0, 0), (0, 0), (0, LANE - FOX_HEADS)))
    dzf, g_fb = _fox_post(dfc, zf3, f_bias)
    grads["fox_f_bias"] = g_fb[:, :FOX_HEADS]

    dhq, dhf, dhi, dhg, grads["hgrn_lb_logits"], grads["hgrn_norm_g"] = _hgrn_bwd(
        zm3, dya.reshape(b_, t_, hw), w["hgrn_lb_logits"], w["hgrn_norm_g"], hw)

    dzm = jnp.concatenate([dhq, dhf, dhi, dhg, dfq, dfk, dfv, dmq, dgl.reshape(b_, t_, 3 * d)], axis=-1).reshape(n, -1)
    dzf2 = dzf.reshape(n, LANE)
    dh_a = _matmul(dzm, w_main, tb=True, name="in_proj_dx")
    dh_b = _matmul(dzf2, w_ff, tb=True, name="in_proj_forget_dx")
    g_main = _matmul(h, dzm, ta=True, name="in_proj_dw")
    g_ff = _matmul(h, dzf2, ta=True, name="in_proj_forget_dw")
    grads["w_in"] = _join_w_in(g_main, g_ff[:, :FOX_HEADS], s1, FOX_HEADS)
    grad_x, grads["norm_mix_g"] = _rmsnorm_bwd(x2, [dh_a, dh_b], w["norm_mix_g"], dx1, name="norm_mix_bwd")
    return loss_vec, grad_x.reshape(b_, t_, d), grads


def kernel(x, mem, norm_mix_g, norm_mem_g, w_in, hgrn_lb_logits, hgrn_norm_g, fox_f_bias, fox_q_norm_g, fox_k_norm_g, mem_kv_w, mem_q_norm_g, mem_k_norm_g, w_br_hgrn, w_br_fox, w_br_mem, w_out, norm_ffn_g, ffn_w_up, ffn_conv_w, ffn_conv_b, ffn_w_down, loss_target, m_norm_mix_g, m_norm_mem_g, m_w_in, m_hgrn_lb_logits, m_hgrn_norm_g, m_fox_f_bias, m_fox_q_norm_g, m_fox_k_norm_g, m_mem_kv_w, m_mem_q_norm_g, m_mem_k_norm_g, m_w_br_hgrn, m_w_br_fox, m_w_br_mem, m_w_out, m_norm_ffn_g, m_ffn_w_up, m_ffn_conv_w, m_ffn_conv_b, m_ffn_w_down, v_norm_mix_g, v_norm_mem_g, v_w_in, v_hgrn_lb_logits, v_hgrn_norm_g, v_fox_f_bias, v_fox_q_norm_g, v_fox_k_norm_g, v_mem_kv_w, v_mem_q_norm_g, v_mem_k_norm_g, v_w_br_hgrn, v_w_br_fox, v_w_br_mem, v_w_out, v_norm_ffn_g, v_ffn_w_up, v_ffn_conv_w, v_ffn_conv_b, v_ffn_w_down):
    w = dict(zip(WEIGHTS, (norm_mix_g, norm_mem_g, w_in, hgrn_lb_logits, hgrn_norm_g, fox_f_bias, fox_q_norm_g,
                           fox_k_norm_g, mem_kv_w, mem_q_norm_g, mem_k_norm_g, w_br_hgrn, w_br_fox, w_br_mem, w_out,
                           norm_ffn_g, ffn_w_up, ffn_conv_w, ffn_conv_b, ffn_w_down)))
    m = dict(zip(WEIGHTS, (m_norm_mix_g, m_norm_mem_g, m_w_in, m_hgrn_lb_logits, m_hgrn_norm_g, m_fox_f_bias,
                           m_fox_q_norm_g, m_fox_k_norm_g, m_mem_kv_w, m_mem_q_norm_g, m_mem_k_norm_g, m_w_br_hgrn,
                           m_w_br_fox, m_w_br_mem, m_w_out, m_norm_ffn_g, m_ffn_w_up, m_ffn_conv_w, m_ffn_conv_b,
                           m_ffn_w_down)))
    v = dict(zip(WEIGHTS, (v_norm_mix_g, v_norm_mem_g, v_w_in, v_hgrn_lb_logits, v_hgrn_norm_g, v_fox_f_bias,
                           v_fox_q_norm_g, v_fox_k_norm_g, v_mem_kv_w, v_mem_q_norm_g, v_mem_k_norm_g, v_w_br_hgrn,
                           v_w_br_fox, v_w_br_mem, v_w_out, v_norm_ffn_g, v_ffn_w_up, v_ffn_conv_w, v_ffn_conv_b,
                           v_ffn_w_down)))
    c_idx = lax.axis_index("c")
    chip = 2 * lax.axis_index("x") + lax.axis_index("y")

    kinds = [KIND[nm] for nm in BIG]
    full = dict(zip(BIG, _gather_shards([w[nm][0].astype(MXU_DTYPE) for nm in BIG], kinds)))
    for nm in ROW_SHARDED:
        full[nm] = full[nm].reshape(-1, full[nm].shape[2])
    cs = ffn_conv_w.shape[2]
    f = cs * N_CHIPS
    placed = lax.dynamic_update_slice(jnp.zeros((3, f), F32), ffn_conv_w[0] * (c_idx == 0).astype(F32), (0, chip * cs))
    conv_w = _unpack_small(_all_sum_small(_pack_small([placed]), "gather_conv_w"), [(3, f)])[0]

    loss_vec, grad_x, grads = _local_step(x, mem, loss_target, w, full, conv_w)

    gs = [grads[nm].reshape(N_CHIPS, -1, grads[nm].shape[1]) if nm in ROW_SHARDED else grads[nm] for nm in BIG]
    from_sibling = _pair_swap_halves(gs, kinds)
    c_arr = jnp.reshape(c_idx, (1,)).astype(jnp.int32)
    chip_partial = [_add_half(g, a, k, c_arr, "add_half_" + nm) for g, a, k, nm in zip(gs, from_sibling, kinds, BIG)]
    landed = _chip_exchange(chip_partial, kinds)
    reduced_half = [_sum_chips(bq, "sum_chips_" + nm) for bq, nm in zip(landed, BIG)]
    gshards = dict(zip(BIG, _pair_join_halves(reduced_half)))

    small_names = SMALL + ("ffn_conv_w",)
    summed = _unpack_small(
        _all_sum_small(_pack_small([grads[nm] for nm in small_names] + [loss_vec]), "all_sum_small_grads"),
        [grads[nm].shape for nm in small_names] + [loss_vec.shape])
    gsmall = dict(zip(small_names, summed[:-1]))
    loss = jnp.sum(summed[-1])
    g_out = {nm: gshards[nm][None] for nm in BIG}
    for nm in SMALL:
        g_out[nm] = gsmall[nm].reshape(w[nm].shape)
    g_out["ffn_conv_w"] = lax.dynamic_slice(gsmall["ffn_conv_w"], (0, chip * cs), (3, cs))[None]

    delta, new_m, new_v = {}, {}, {}
    for nm in BIG + ("ffn_conv_w",):
        d_, m_, v_ = _adamw(w[nm][0], g_out[nm][0], m[nm][0], v[nm][0], name="adamw_" + nm)
        delta[nm], new_m[nm], new_v[nm] = d_[None], m_[None], v_[None]
    packed = [_pack_small([t[nm] for nm in SMALL]) for t in (w, g_out, m, v)]
    outs = _adamw(*packed, name="adamw_small")
    shapes = [w[nm].shape for nm in SMALL]
    for res, o in zip((delta, new_m, new_v), outs):
        res.update(zip(SMALL, _unpack_small(o, shapes)))

    return (loss, grad_x, *[g_out[nm] for nm in WEIGHTS], *[delta[nm] for nm in WEIGHTS],
            *[new_m[nm] for nm in WEIGHTS], *[new_v[nm] for nm in WEIGHTS])
```

```python
import functools
import math

import jax
import jax.numpy as jnp
from jax import lax
from jax.experimental import pallas as pl
from jax.experimental.pallas import tpu as pltpu

F32 = jnp.float32
BF16 = jnp.bfloat16
MXU_DTYPE = jnp.bfloat16
EXCHANGE_DTYPE = jnp.bfloat16

EPS = 1e-6
HG_HEADS, HG_D = 4, 128
FOX_HEADS, FOX_DH = 8, 64
MEM_HEADS, MEM_DH = 4, 128
HG_CHUNK = 64
FOX_BLOCK = 256
LANE = 128
FFN_GROUP = 256
FLAT_W = 1024
VMEM_LIMIT = 56 * 2 ** 20
NEG = -1e30
N_CHIPS = 4

ADAM_LR, ADAM_B1, ADAM_B2, ADAM_EPS, ADAM_WD, ADAM_STEP = 0.001, 0.9, 0.999, 1e-08, 0.01, 10

MESH = pl.DeviceIdType.MESH
ANY = pl.BlockSpec(memory_space=pl.ANY)


def _mx(x):
    return x.astype(MXU_DTYPE)


def _dot(a, b, ca, cb):
    return lax.dot_general(_mx(a), _mx(b), (((ca,), (cb,)), ((), ())), preferred_element_type=F32)


def _nn(a, b):
    return _dot(a, b, 1, 0)


def _nt(a, b):
    return _dot(a, b, 1, 1)


def _tn(a, b):
    return _dot(a, b, 0, 0)


def _dotp(a, b, ca, cb):
    return lax.dot_general(a, b, (((ca,), (cb,)), ((), ())), precision=lax.Precision.HIGHEST,
                           preferred_element_type=F32)


def _tri_dot(tri_bf, x):
    hi = x.astype(BF16)
    r = x - hi.astype(F32)
    mid = r.astype(BF16)
    lo = (r - mid.astype(F32)).astype(BF16)

    def d(v):
        return lax.dot_general(tri_bf, v, (((1,), (0,)), ((), ())), preferred_element_type=F32)

    return d(hi) + d(mid) + d(lo)


def _sig(x):
    return jax.nn.sigmoid(x)


def _erf(x):
    a = jnp.abs(x)
    t = 1.0 / (1.0 + 0.3275911 * a)
    poly = t * (0.254829592 + t * (-0.284496736 + t * (1.421413741 + t * (-1.453152027 + t * 1.061405429))))
    y = 1.0 - poly * jnp.exp(-a * a)
    return jnp.where(x < 0, -y, y)


def _tile(dim, pref, unit=LANE):
    if dim <= pref:
        return dim
    t = pref - pref % unit
    while t >= unit:
        if dim % t == 0:
            return t
        t -= unit
    return dim


def _params(n_grid):
    return pltpu.CompilerParams(dimension_semantics=("arbitrary",) * n_grid, vmem_limit_bytes=VMEM_LIMIT)


def _acc(ref, val, first):
    @pl.when(first)
    def _():
        ref[...] = val

    @pl.when(jnp.logical_not(first))
    def _():
        ref[...] += val


def _matmul(a, b, *, name, ta=False, tb=False, out_dtype=F32, tm=1024, tn=1024, tk=1024):
    m, k = (a.shape[1], a.shape[0]) if ta else a.shape
    n = b.shape[0] if tb else b.shape[1]
    tm, tn, tk = _tile(m, tm), _tile(n, tn), _tile(k, tk)
    nk = k // tk

    def body(a_ref, b_ref, o_ref, acc_ref):
        kk = pl.program_id(2)
        p = _dot(a_ref[...], b_ref[...], 0 if ta else 1, 1 if tb else 0)
        _acc(acc_ref, p, kk == 0)

        @pl.when(kk == nk - 1)
        def _():
            o_ref[...] = acc_ref[...].astype(o_ref.dtype)

    a_spec = pl.BlockSpec((tk, tm), lambda i, j, kk: (kk, i)) if ta else pl.BlockSpec((tm, tk), lambda i, j, kk: (i, kk))
    b_spec = pl.BlockSpec((tn, tk), lambda i, j, kk: (j, kk)) if tb else pl.BlockSpec((tk, tn), lambda i, j, kk: (kk, j))
    return pl.pallas_call(
        body, name=name, grid=(m // tm, n // tn, nk),
        in_specs=[a_spec, b_spec],
        out_specs=pl.BlockSpec((tm, tn), lambda i, j, kk: (i, j)),
        out_shape=jax.ShapeDtypeStruct((m, n), out_dtype),
        scratch_shapes=[pltpu.VMEM((tm, tn), F32)],
        compiler_params=_params(3),
    )(a, b)


def _rmsnorm_fwd(x, g, *, name, tm=512):
    n, d = x.shape
    tm = _tile(n, tm, 8)

    def body(x_ref, g_ref, o_ref):
        xv = x_ref[...]
        r = lax.rsqrt(jnp.mean(xv * xv, axis=-1, keepdims=True) + EPS)
        o_ref[...] = (xv * r * g_ref[...]).astype(o_ref.dtype)

    return pl.pallas_call(
        body, name=name, grid=(n // tm,),
        in_specs=[pl.BlockSpec((tm, d), lambda i: (i, 0)), pl.BlockSpec((1, d), lambda i: (0, 0))],
        out_specs=pl.BlockSpec((tm, d), lambda i: (i, 0)),
        out_shape=jax.ShapeDtypeStruct((n, d), MXU_DTYPE),
        compiler_params=_params(1),
    )(x, g)


def _rmsnorm_bwd(x, dhs, g, res, *, name, tm=512):
    n, d = x.shape
    tm = _tile(n, tm, 8)
    n_dh = len(dhs)
    has_res = res is not None

    def body(*refs):
        x_ref, dh_refs, g_ref = refs[0], refs[1:1 + n_dh], refs[1 + n_dh]
        res_ref = refs[2 + n_dh] if has_res else None
        dx_ref, dg_ref = refs[-2], refs[-1]
        xv = x_ref[...]
        dh = dh_refs[0][...].astype(F32)
        for r_ in dh_refs[1:]:
            dh = dh + r_[...].astype(F32)
        r = lax.rsqrt(jnp.mean(xv * xv, axis=-1, keepdims=True) + EPS)
        dhg = dh * g_ref[...]
        dx = r * dhg - xv * (r * r * r) * jnp.mean(dhg * xv, axis=-1, keepdims=True)
        if has_res:
            dx = dx + res_ref[...]
        dx_ref[...] = dx
        _acc(dg_ref, jnp.sum(dh * xv * r, axis=0, keepdims=True), pl.program_id(0) == 0)

    row = pl.BlockSpec((tm, d), lambda i: (i, 0))
    vec = pl.BlockSpec((1, d), lambda i: (0, 0))
    ins = [x] + list(dhs) + [g] + ([res] if has_res else [])
    return pl.pallas_call(
        body, name=name, grid=(n // tm,),
        in_specs=[row] * (1 + n_dh) + [vec] + ([row] if has_res else []),
        out_specs=[row, vec],
        out_shape=[jax.ShapeDtypeStruct((n, d), F32), jax.ShapeDtypeStruct((1, d), F32)],
        compiler_params=_params(1),
    )(*ins)


def _adamw(w, g, m, v, *, name, tr=256):
    r, c = w.shape
    tr = _tile(r, tr, 8)
    c1 = 1.0 / (1.0 - ADAM_B1 ** ADAM_STEP)
    c2 = 1.0 / (1.0 - ADAM_B2 ** ADAM_STEP)

    def body(w_ref, g_ref, m_ref, v_ref, d_ref, mo_ref, vo_ref):
        gv = g_ref[...]
        mn = ADAM_B1 * m_ref[...] + (1.0 - ADAM_B1) * gv
        vn = ADAM_B2 * v_ref[...] + (1.0 - ADAM_B2) * (gv * gv)
        d_ref[...] = -ADAM_LR * ((mn * c1) / (jnp.sqrt(vn * c2) + ADAM_EPS) + ADAM_WD * w_ref[...])
        mo_ref[...] = mn
        vo_ref[...] = vn

    blk = pl.BlockSpec((tr, c), lambda i: (i, 0))
    sds = jax.ShapeDtypeStruct((r, c), F32)
    return pl.pallas_call(
        body, name=name, grid=(r // tr,), in_specs=[blk] * 4, out_specs=[blk] * 3, out_shape=[sds] * 3,
        compiler_params=_params(1),
    )(w, g, m, v)


def _hgrn_chunk(hq, hf, hi, lbv, tril, tril_bf, st):
    c = hq.shape[0]
    sf = _sig(hf)
    f = lbv + (1.0 - lbv) * sf
    k = 1.0 - f
    gcum = _tri_dot(tril_bf, jnp.log(f))
    mid = gcum[c // 2 - 1:c // 2, :]
    glast = gcum[c - 1:c, :]
    sq = _sig(hq)
    q = hq * sq
    e_q = jnp.exp(gcum - mid)
    e_k = jnp.exp(mid - gcum)
    qe, ke = q * e_q, k * e_k
    a = jnp.where(tril, _nt(qe, ke), 0.0)
    e_g = jnp.exp(gcum)
    qg = q * e_g
    o = _nn(a, hi) + _nt(qg, st)
    e_s = jnp.exp(glast - gcum)
    kg = k * e_s
    e_l = jnp.exp(glast)
    st_new = st * e_l + _tn(hi, kg)
    return dict(sf=sf, f=f, k=k, sq=sq, q=q, e_q=e_q, e_k=e_k, qe=qe, ke=ke, a=a, e_g=e_g, qg=qg, o=o,
                e_s=e_s, kg=kg, e_l=e_l, st_new=st_new)


HG_EXAMPLES = 2


def _hgrn_specs(t_, hw, ne):
    nb = hw // LANE

    def col(off):
        return pl.BlockSpec((ne, t_, LANE), lambda h, b: (b, 0, off * nb + h))

    vec = pl.BlockSpec((2, LANE), lambda h, b: (0, h))
    one = pl.BlockSpec((1, LANE), lambda h, b: (0, 0))
    blk = pl.BlockSpec((ne, t_, LANE), lambda h, b: (b, 0, h))
    return col, vec, one, blk


def _hgrn_fwd(zm, lb, gn, hw):
    b_, t_, _ = zm.shape
    c = min(HG_CHUNK, t_)
    nc = t_ // c
    ne = min(HG_EXAMPLES, b_)
    col, vec, one, blk = _hgrn_specs(t_, hw, ne)

    def body(q_ref, f_ref, i_ref, g_ref, lb_ref, gn_ref, y_ref):
        lbv, gnv = _sig(lb_ref[0:1, :] - lb_ref[1:2, :]), gn_ref[...]
        tril = lax.broadcasted_iota(jnp.int32, (c, c), 0) >= lax.broadcasted_iota(jnp.int32, (c, c), 1)
        tril_bf = tril.astype(BF16)

        def chunk(n, sts):
            rows = pl.ds(pl.multiple_of(n * c, c), c)
            new = []
            for e in range(ne):
                p = _hgrn_chunk(q_ref[e, rows, :], f_ref[e, rows, :], i_ref[e, rows, :], lbv, tril, tril_bf, sts[e])
                o = p["o"]
                r = lax.rsqrt(jnp.mean(o * o, axis=-1, keepdims=True) + EPS)
                hg = g_ref[e, rows, :]
                y_ref[e, rows, :] = o * r * gnv * (hg * _sig(hg))
                new.append(p["st_new"])
            return tuple(new)

        lax.fori_loop(0, nc, chunk, tuple(jnp.zeros((HG_D, HG_D), F32) for _ in range(ne)))

    return pl.pallas_call(
        body, name="hgrn_fwd", grid=(HG_HEADS, b_ // ne),
        in_specs=[col(0), col(1), col(2), col(3), vec, one], out_specs=blk,
        out_shape=jax.ShapeDtypeStruct((b_, t_, hw), F32),
        compiler_params=_params(2),
    )(zm, zm, zm, zm, lb, gn)


def _hgrn_bwd(zm, dy, lb, gn, hw):
    b_, t_, _ = zm.shape
    c = min(HG_CHUNK, t_)
    nc = t_ // c
    ne = min(HG_EXAMPLES, b_)
    col, vec, one, blk = _hgrn_specs(t_, hw, ne)

    def body(q_ref, f_ref, i_ref, g_ref, dy_ref, lb_ref, gn_ref, dq_ref, df_ref, di_ref, dg_ref, dlb_ref, dgn_ref,
             st_all):
        h, b = pl.program_id(0), pl.program_id(1)
        lbv, gnv = _sig(lb_ref[0:1, :] - lb_ref[1:2, :]), gn_ref[...]
        row = lax.broadcasted_iota(jnp.int32, (c, c), 0)
        cl = lax.broadcasted_iota(jnp.int32, (c, c), 1)
        tril = row >= cl
        tril_bf = tril.astype(BF16)
        triu_bf = (row <= cl).astype(BF16)
        last_row = lax.broadcasted_iota(jnp.int32, (c, LANE), 0) == c - 1

        def fwd(n, sts):
            rows = pl.ds(pl.multiple_of(n * c, c), c)
            new = []
            for e in range(ne):
                st_all[e, n] = sts[e]
                new.append(_hgrn_chunk(q_ref[e, rows, :], f_ref[e, rows, :], i_ref[e, rows, :], lbv, tril, tril_bf,
                                       sts[e])["st_new"])
            return tuple(new)

        zst = tuple(jnp.zeros((HG_D, HG_D), F32) for _ in range(ne))
        lax.fori_loop(0, nc, fwd, zst)

        def bwd_one(e, n, rows, dst):
            hq, hi, hg = q_ref[e, rows, :], i_ref[e, rows, :], g_ref[e, rows, :]
            st = st_all[e, n]
            p = _hgrn_chunk(hq, f_ref[e, rows, :], hi, lbv, tril, tril_bf, st)
            o, q, k = p["o"], p["q"], p["k"]
            dyv = dy_ref[e, rows, :]
            sg = _sig(hg)
            r = lax.rsqrt(jnp.mean(o * o, axis=-1, keepdims=True) + EPS)
            nrm = o * r * gnv
            dn = dyv * (hg * sg)
            dg_ref[e, rows, :] = (dyv * nrm * (sg * (1.0 + hg * (1.0 - sg)))).astype(dg_ref.dtype)
            dgn = jnp.sum(dn * o * r, axis=0, keepdims=True)
            dng = dn * gnv
            do = r * dng - o * (r * r * r) * jnp.mean(dng * o, axis=-1, keepdims=True)
            da = jnp.where(tril, _dotp(do, hi, 1, 1), 0.0)
            dq = _dotp(da, p["ke"], 1, 0) * p["e_q"] + _dotp(do, st, 1, 0) * p["e_g"]
            dkg = _dotp(hi, dst, 1, 0)
            dk_state = dkg * p["e_s"]
            dk = _dotp(da, p["qe"], 0, 0) * p["e_k"] + dk_state
            di_ref[e, rows, :] = (_tn(p["a"], do) + _nt(p["kg"], dst)).astype(di_ref.dtype)
            dgc = q * dq - k * dk
            extra = (jnp.sum(k * dk_state, axis=0, keepdims=True)
                     + p["e_l"] * jnp.sum(st * dst, axis=0, keepdims=True))
            dgc = dgc + jnp.where(last_row, extra, 0.0)
            dlf = _tri_dot(triu_bf, dgc)
            dfv = dlf / p["f"] - dk
            sf, sq = p["sf"], p["sq"]
            df_ref[e, rows, :] = (dfv * (1.0 - lbv) * sf * (1.0 - sf)).astype(df_ref.dtype)
            dlb = jnp.sum(dfv * (1.0 - sf), axis=0, keepdims=True)
            dq_ref[e, rows, :] = (dq * (sq * (1.0 + hq * (1.0 - sq)))).astype(dq_ref.dtype)
            return dst * p["e_l"] + _dotp(do, p["qg"], 0, 0), dlb, dgn

        def bwd(m, carry):
            dsts, dlb, dgn = carry
            n = nc - 1 - m
            rows = pl.ds(pl.multiple_of(n * c, c), c)
            new = []
            for e in range(ne):
                dst, dlb_e, dgn_e = bwd_one(e, n, rows, dsts[e])
                new.append(dst)
                dlb, dgn = dlb + dlb_e, dgn + dgn_e
            return tuple(new), dlb, dgn

        z1 = jnp.zeros((1, LANE), F32)
        _, dlb, dgn = lax.fori_loop(0, nc, bwd, (zst, z1, z1))
        dl0 = dlb * lbv * (1.0 - lbv)
        _acc(dlb_ref, jnp.concatenate([dl0, -dl0], axis=0), b == 0)
        _acc(dgn_ref, dgn, jnp.logical_and(b == 0, h == 0))

    sds = jax.ShapeDtypeStruct((b_, t_, hw), MXU_DTYPE)
    return pl.pallas_call(
        body, name="hgrn_bwd", grid=(HG_HEADS, b_ // ne),
        in_specs=[col(0), col(1), col(2), col(3), blk, vec, one],
        out_specs=[blk, blk, blk, blk, vec, one],
        out_shape=[sds, sds, sds, sds, jax.ShapeDtypeStruct((2, hw), F32), jax.ShapeDtypeStruct((1, LANE), F32)],
        scratch_shapes=[pltpu.VMEM((ne, nc, HG_D, HG_D), F32)],
        compiler_params=_params(2),
    )(zm, zm, zm, zm, dy, lb, gn)


def _fox_logf(x):
    return jnp.minimum(x, 0.0) - jnp.log(1.0 + jnp.exp(-jnp.abs(x)))


def _fox_prep(zf, bias):
    b_, t_, _ = zf.shape
    tb = min(FOX_BLOCK, t_)
    nb = t_ // tb

    def body(z_ref, b_ref, fc_ref):
        tril_bf = (lax.broadcasted_iota(jnp.int32, (tb, tb), 0) >= lax.broadcasted_iota(jnp.int32, (tb, tb), 1)).astype(BF16)
        bv = b_ref[...]

        def blk(i, carry):
            rows = pl.ds(pl.multiple_of(i * tb, tb), tb)
            fc = _tri_dot(tril_bf, _fox_logf(z_ref[0, rows, :] + bv)) + carry
            fc_ref[0, rows, :] = fc
            return fc[tb - 1:tb, :]

        lax.fori_loop(0, nb, blk, jnp.zeros((1, LANE), F32))

    blk_spec = pl.BlockSpec((1, t_, LANE), lambda b: (b, 0, 0))
    return pl.pallas_call(
        body, name="fox_prep", grid=(b_,),
        in_specs=[blk_spec, pl.BlockSpec((1, LANE), lambda b: (0, 0))], out_specs=blk_spec,
        out_shape=jax.ShapeDtypeStruct((b_, t_, LANE), F32), compiler_params=_params(1),
    )(zf, bias)


def _fox_post(dfc, zf, bias):
    b_, t_, _ = zf.shape
    tb = min(FOX_BLOCK, t_)
    nb = t_ // tb

    def body(d_ref, z_ref, b_ref, dz_ref, db_ref):
        triu_bf = (lax.broadcasted_iota(jnp.int32, (tb, tb), 0) <= lax.broadcasted_iota(jnp.int32, (tb, tb), 1)).astype(BF16)
        valid = lax.broadcasted_iota(jnp.int32, (tb, LANE), 1) < FOX_HEADS
        bv = b_ref[...]

        def blk(m, carry):
            tail, db = carry
            rows = pl.ds(pl.multiple_of((nb - 1 - m) * tb, tb), tb)
            dlf = _tri_dot(triu_bf, d_ref[0, rows, :]) + tail
            dx = jnp.where(valid, dlf * _sig(-(z_ref[0, rows, :] + bv)), 0.0)
            dz_ref[0, rows, :] = dx.astype(dz_ref.dtype)
            return dlf[0:1, :], db + jnp.sum(dx, axis=0, keepdims=True)

        z1 = jnp.zeros((1, LANE), F32)
        _, db = lax.fori_loop(0, nb, blk, (z1, z1))
        _acc(db_ref, db, pl.program_id(0) == 0)

    blk_spec = pl.BlockSpec((1, t_, LANE), lambda b: (b, 0, 0))
    vec = pl.BlockSpec((1, LANE), lambda b: (0, 0))
    return pl.pallas_call(
        body, name="fox_post", grid=(b_,), in_specs=[blk_spec, blk_spec, vec], out_specs=[blk_spec, vec],
        out_shape=[jax.ShapeDtypeStruct((b_, t_, LANE), MXU_DTYPE), jax.ShapeDtypeStruct((1, LANE), F32)],
        compiler_params=_params(1),
    )(dfc, zf, bias)


FOX_TILE = 128
FOX_BAND = 512
AUG = 64


def _head_mean_matrix():
    r = lax.broadcasted_iota(jnp.int32, (LANE, LANE), 0) // FOX_DH
    c = lax.broadcasted_iota(jnp.int32, (LANE, LANE), 1) // FOX_DH
    return (r == c).astype(BF16)


def _dot_right_exact(x, m_bf):
    hi = x.astype(BF16)
    r = x - hi.astype(F32)
    mid = r.astype(BF16)
    lo = (r - mid.astype(F32)).astype(BF16)

    def d(v):
        return lax.dot_general(v, m_bf, (((1,), (0,)), ((), ())), preferred_element_type=F32)

    return d(hi) + d(mid) + d(lo)


def _pair_norm(x, g2, bd):
    r = lax.rsqrt(_dot_right_exact(x * x, bd) * (1.0 / FOX_DH) + EPS)
    return x * r * g2, r


def _pair_norm_bwd(x, r, dy, g2, bd):
    dyg = dy * g2
    dx = r * dyg - x * (r * r * r) * (_dot_right_exact(dyg * x, bd) * (1.0 / FOX_DH))
    return dx, jnp.sum(dy * x * r, axis=0, keepdims=True)


def _head_lanes(xn, hh):
    return xn if hh == 0 else pltpu.roll(xn, FOX_DH, 1)


def _split3(x):
    hi = x.astype(BF16).astype(F32)
    mid = (x - hi).astype(BF16).astype(F32)
    return hi, mid, x - hi - mid


def _fox_operands(q_ref, k_ref, v_ref, fc_ref, gq2, gk2, p, qa, ka, va):
    t_ = q_ref.shape[1]
    bd = _head_mean_matrix()
    lane = lax.broadcasted_iota(jnp.int32, (t_, LANE), 1)
    qx, kx = q_ref[0], k_ref[0]
    qn, rq = _pair_norm(qx, gq2, bd)
    kn, rk = _pair_norm(kx, gk2, bd)
    vv = v_ref[0]
    q_aug = jnp.where(jnp.logical_and(lane >= AUG, lane < AUG + 3), 1.0, 0.0)
    for hh in range(2):
        fcol = jnp.sum(jnp.where(lane == 2 * p + hh, fc_ref[0], 0.0), axis=-1, keepdims=True)
        hi, mid, lo = _split3(-fcol)
        k_aug = jnp.where(lane == AUG, hi, jnp.where(lane == AUG + 1, mid, jnp.where(lane == AUG + 2, lo,
                          jnp.where(lane == AUG + 3, 1.0, 0.0))))
        head = lane < FOX_DH
        qa[hh] = jnp.where(head, _head_lanes(qn, hh), q_aug).astype(MXU_DTYPE)
        ka[hh] = jnp.where(head, _head_lanes(kn, hh), k_aug).astype(MXU_DTYPE)
        va[hh] = jnp.where(head, _head_lanes(vv, hh), 0.0).astype(MXU_DTYPE)
    return bd, lane, qx, kx, rq, rk


def _fox_specs(t_, fw, col0):
    npair = fw // LANE

    def col(off):
        return pl.BlockSpec((1, t_, LANE), lambda b, p: (b, 0, col0 + off * npair + p))

    pair = pl.BlockSpec((1, t_, LANE), lambda b, p: (b, 0, p))
    full = pl.BlockSpec((1, t_, LANE), lambda b, p: (b, 0, 0))
    gvec = pl.BlockSpec((1, LANE), lambda b, p: (0, 0))
    lse = pl.BlockSpec((1, 1, t_, LANE), lambda b, p: (b, p, 0, 0))
    return col, pair, full, gvec, lse


def _fox_fwd(zm, fc, gq2, gk2, fw, col0):
    b_, t_, _ = zm.shape
    npair = fw // LANE
    tq = min(FOX_TILE, t_)
    bw = min(FOX_BAND, t_)
    nband, tpb = t_ // bw, bw // tq
    scale = FOX_DH ** -0.5
    col, pair, full, gvec, lse_spec = _fox_specs(t_, fw, col0)

    def body(q_ref, k_ref, v_ref, fc_ref, gq_ref, gk_ref, o_ref, lse_ref, qa, ka, va):
        p = pl.program_id(1)
        _fox_operands(q_ref, k_ref, v_ref, fc_ref, gq_ref[...] * scale, gk_ref[...], p, qa, ka, va)
        ri = lax.broadcasted_iota(jnp.int32, (tq, bw), 0)
        ci = lax.broadcasted_iota(jnp.int32, (tq, bw), 1)
        lane = lax.broadcasted_iota(jnp.int32, (tq, LANE), 1)

        for band in range(nband):
            c0 = band * bw

            def qtile(ii, _, c0=c0):
                r0 = pl.multiple_of(c0 + ii * tq, tq)
                rows = pl.ds(r0, tq)
                keep = c0 + ci <= r0 + ri
                res = []
                for hh in range(2):
                    qb = qa[hh, rows, :]
                    s_b = jnp.where(keep, _nt(qb, ka[hh, c0:c0 + bw, :]), NEG)
                    m = jnp.max(s_b, axis=-1, keepdims=True)
                    if c0:
                        s_a = _nt(qb, ka[hh, 0:c0, :])
                        m = jnp.maximum(m, jnp.max(s_a, axis=-1, keepdims=True))
                    p_b = jnp.exp(s_b - m)
                    l = jnp.sum(p_b, axis=-1, keepdims=True)
                    acc = _nn(p_b, va[hh, c0:c0 + bw, :])
                    if c0:
                        p_a = jnp.exp(s_a - m)
                        l = l + jnp.sum(p_a, axis=-1, keepdims=True)
                        acc = acc + _nn(p_a, va[hh, 0:c0, :])
                    res.append((acc / l, m + jnp.log(l)))
                (o0, e0), (o1, e1) = res
                o_ref[0, rows, :] = jnp.where(lane < FOX_DH, o0, pltpu.roll(o1, FOX_DH, 1))
                lse_ref[0, 0, rows, :] = jnp.where(lane == 0, e0, jnp.where(lane == 1, e1, 0.0))
                return 0

            lax.fori_loop(0, tpb, qtile, 0)

    return pl.pallas_call(
        body, name="fox_fwd", grid=(b_, npair),
        in_specs=[col(0), col(1), col(2), full, gvec, gvec],
        out_specs=[pair, lse_spec],
        out_shape=[jax.ShapeDtypeStruct((b_, t_, fw), F32), jax.ShapeDtypeStruct((b_, npair, t_, LANE), F32)],
        scratch_shapes=[pltpu.VMEM((2, t_, LANE), MXU_DTYPE)] * 3,
        compiler_params=_params(2),
    )(zm, zm, zm, fc, gq2, gk2)


def _norm_bwd(x, dy, g):
    r = lax.rsqrt(jnp.mean(x * x, axis=-1, keepdims=True) + EPS)
    dyg = dy * g
    dx = r * dyg - x * (r * r * r) * jnp.mean(dyg * x, axis=-1, keepdims=True)
    return dx, jnp.sum(dy * x * r, axis=0, keepdims=True)


def _fox_bwd(zm, o, do, lse, fc, gq2, gk2, fw, col0):
    b_, t_, _ = zm.shape
    npair = fw // LANE
    tq = min(FOX_TILE, t_)
    nb = t_ // tq
    bw = min(FOX_BAND, t_)
    nband, tpb = t_ // bw, bw // tq
    scale = FOX_DH ** -0.5
    col, pair, full, gvec, lse_spec = _fox_specs(t_, fw, col0)

    def body(q_ref, k_ref, v_ref, o_ref, do_ref, lse_ref, fc_ref, gq_ref, gk_ref,
             dq_ref, dk_ref, dv_ref, dfc_ref, dgq_ref, dgk_ref, qa, ka, va, da, rowv, dq_acc, dk_acc, dv_acc):
        b, p = pl.program_id(0), pl.program_id(1)
        gq2v, gk2v = gq_ref[...] * scale, gk_ref[...]
        bd, lane, qx, kx, rq, rk = _fox_operands(q_ref, k_ref, v_ref, fc_ref, gq2v, gk2v, p, qa, ka, va)
        head = lane < FOX_DH
        dov = do_ref[0]
        dsum = _dot_right_exact(dov * o_ref[0], bd)
        eye = (lax.broadcasted_iota(jnp.int32, (tq, tq), 0) == lax.broadcasted_iota(jnp.int32, (tq, tq), 1)).astype(F32)
        for hh in range(2):
            da[hh] = jnp.where(head, _head_lanes(dov, hh), 0.0).astype(MXU_DTYPE)
            for blk in range(nb):
                rs = slice(blk * tq, (blk + 1) * tq)
                rowv[2 * hh:2 * hh + 1, rs] = jnp.sum(eye * lse_ref[0, 0, rs, hh:hh + 1], axis=0, keepdims=True)
                rowv[2 * hh + 1:2 * hh + 2, rs] = jnp.sum(eye * dsum[rs, hh * FOX_DH:hh * FOX_DH + 1], axis=0, keepdims=True)
        dq_acc[...] = jnp.zeros(dq_acc.shape, F32)
        ri = lax.broadcasted_iota(jnp.int32, (tq, bw), 0)
        ci = lax.broadcasted_iota(jnp.int32, (tq, bw), 1)

        def part(hh, kb, vb, lo, hi, keep):
            qm, dm = qa[hh, lo:hi, :], da[hh, lo:hi, :]
            pt = jnp.exp(_nt(kb, qm) - rowv[2 * hh:2 * hh + 1, lo:hi])
            if keep is not None:
                pt = jnp.where(keep, pt, 0.0)
            dst = pt * (_nt(vb, dm) - rowv[2 * hh + 1:2 * hh + 2, lo:hi])
            dq_acc[hh, lo:hi, :] += _tn(dst, kb)
            return _nn(dst, qm), _nn(pt, dm)

        for band in range(nband):
            c0 = band * bw

            def kvtile(jj, _, c0=c0):
                r0 = pl.multiple_of(c0 + jj * tq, tq)
                rows = pl.ds(r0, tq)
                keep = c0 + ci >= r0 + ri
                for hh in range(2):
                    kb, vb = ka[hh, rows, :], va[hh, rows, :]
                    dk_t, dv_t = part(hh, kb, vb, c0, c0 + bw, keep)
                    if c0 + bw < t_:
                        dk_u, dv_u = part(hh, kb, vb, c0 + bw, t_, None)
                        dk_t, dv_t = dk_t + dk_u, dv_t + dv_u
                    dk_acc[hh, rows, :] = dk_t
                    dv_acc[hh, rows, :] = dv_t
                return 0

            lax.fori_loop(0, tpb, kvtile, 0)

        dq0, dq1, dk0, dk1 = dq_acc[0], dq_acc[1], dk_acc[0], dk_acc[1]
        dqn = jnp.where(head, dq0, pltpu.roll(dq1, FOX_DH, 1))
        dkn = jnp.where(head, dk0, pltpu.roll(dk1, FOX_DH, 1))
        dqx, gq_part = _pair_norm_bwd(qx, rq, dqn, gq2v, bd)
        dkx, gk_part = _pair_norm_bwd(kx, rk, dkn, gk2v, bd)
        dq_ref[0] = dqx.astype(dq_ref.dtype)
        dk_ref[0] = dkx.astype(dk_ref.dtype)
        dv_ref[0] = jnp.where(head, dv_acc[0], pltpu.roll(dv_acc[1], FOX_DH, 1)).astype(dv_ref.dtype)

        def bias_grad(dqh, dkh):
            return (jnp.sum(jnp.where(lane == AUG + 3, dqh, 0.0), axis=-1, keepdims=True)
                    - jnp.sum(jnp.where(lane == AUG, dkh, 0.0), axis=-1, keepdims=True))

        dfc_ref[0, 0] = jnp.where(lane == 0, bias_grad(dq0, dk0), jnp.where(lane == 1, bias_grad(dq1, dk1), 0.0))
        first = jnp.logical_and(b == 0, p == 0)
        _acc(dgq_ref, gq_part * scale, first)
        _acc(dgk_ref, gk_part, first)

    sds = jax.ShapeDtypeStruct((b_, t_, fw), MXU_DTYPE)
    gs = jax.ShapeDtypeStruct((1, LANE), F32)
    return pl.pallas_call(
        body, name="fox_bwd", grid=(b_, npair),
        in_specs=[col(0), col(1), col(2), pair, pair, lse_spec, full, gvec, gvec],
        out_specs=[pair, pair, pair, lse_spec, gvec, gvec],
        out_shape=[sds, sds, sds, jax.ShapeDtypeStruct((b_, npair, t_, LANE), F32), gs, gs],
        scratch_shapes=[pltpu.VMEM((2, t_, LANE), MXU_DTYPE)] * 4
        + [pltpu.VMEM((8, t_), F32)] + [pltpu.VMEM((2, t_, LANE), F32)] * 3,
        compiler_params=_params(2),
    )(zm, zm, zm, o, do, lse, fc, gq2, gk2)


def _mem_specs(t_, m_, mw, col0):
    nh = mw // LANE
    qcol = pl.BlockSpec((1, t_, LANE), lambda b, h: (b, 0, col0 + h))
    kcol = pl.BlockSpec((1, m_, LANE), lambda b, h: (b, 0, h))
    vcol = pl.BlockSpec((1, m_, LANE), lambda b, h: (b, 0, nh + h))
    ycol = pl.BlockSpec((1, t_, LANE), lambda b, h: (b, 0, h))
    gvec = pl.BlockSpec((1, LANE), lambda b, h: (0, 0))
    return qcol, kcol, vcol, ycol, gvec


def _mem_fwd(zm, mkv, gq, gk, mw, col0):
    b_, t_, _ = zm.shape
    m_ = mkv.shape[1]
    tq = min(512, t_)
    nb = t_ // tq
    scale = MEM_DH ** -0.5
    qcol, kcol, vcol, ycol, gvec = _mem_specs(t_, m_, mw, col0)

    def body(q_ref, k_ref, v_ref, gq_ref, gk_ref, y_ref):
        gqv, gkv = gq_ref[...] * scale, gk_ref[...]
        kv = k_ref[0]
        kn = _mx(kv * lax.rsqrt(jnp.mean(kv * kv, axis=-1, keepdims=True) + EPS) * gkv)
        vv = _mx(v_ref[0])

        def blk(i, _):
            rows = pl.ds(pl.multiple_of(i * tq, tq), tq)
            qv = q_ref[0, rows, :]
            s = _nt(qv * lax.rsqrt(jnp.mean(qv * qv, axis=-1, keepdims=True) + EPS) * gqv, kn)
            e = jnp.exp(s - jnp.max(s, axis=-1, keepdims=True))
            y_ref[0, rows, :] = _nn(e / jnp.sum(e, axis=-1, keepdims=True), vv)
            return 0

        lax.fori_loop(0, nb, blk, 0)

    return pl.pallas_call(
        body, name="mem_fwd", grid=(b_, MEM_HEADS), in_specs=[qcol, kcol, vcol, gvec, gvec], out_specs=ycol,
        out_shape=jax.ShapeDtypeStruct((b_, t_, mw), F32), compiler_params=_params(2),
    )(zm, mkv, mkv, gq, gk)


def _mem_bwd(zm, mkv, dy, gq, gk, mw, col0):
    b_, t_, _ = zm.shape
    m_ = mkv.shape[1]
    tq = min(512, t_)
    nb = t_ // tq
    scale = MEM_DH ** -0.5
    qcol, kcol, vcol, ycol, gvec = _mem_specs(t_, m_, mw, col0)

    def body(q_ref, k_ref, v_ref, dy_ref, gq_ref, gk_ref, dq_ref, dk_ref, dv_ref, dgq_ref, dgk_ref):
        gqv, gkv = gq_ref[...] * scale, gk_ref[...]
        kv = k_ref[0]
        kn = _mx(kv * lax.rsqrt(jnp.mean(kv * kv, axis=-1, keepdims=True) + EPS) * gkv)
        vv = _mx(v_ref[0])

        def blk(i, carry):
            dkn, dvv, dgq = carry
            rows = pl.ds(pl.multiple_of(i * tq, tq), tq)
            qv = q_ref[0, rows, :]
            qn = _mx(qv * lax.rsqrt(jnp.mean(qv * qv, axis=-1, keepdims=True) + EPS) * gqv)
            s = _nt(qn, kn)
            e = jnp.exp(s - jnp.max(s, axis=-1, keepdims=True))
            pm = e / jnp.sum(e, axis=-1, keepdims=True)
            dob = _mx(dy_ref[0, rows, :])
            dp = _nt(dob, vv)
            ds = pm * (dp - jnp.sum(dp * pm, axis=-1, keepdims=True))
            dqv, gq_part = _norm_bwd(qv, _nn(ds, kn), gqv)
            dq_ref[0, rows, :] = dqv.astype(dq_ref.dtype)
            return dkn + _tn(ds, qn), dvv + _tn(pm, dob), dgq + gq_part * scale

        z = jnp.zeros((m_, LANE), F32)
        dkn, dvv, dgq = lax.fori_loop(0, nb, blk, (z, z, jnp.zeros((1, LANE), F32)))
        dkv, dgk = _norm_bwd(kv, dkn, gkv)
        dk_ref[0] = dkv
        dv_ref[0] = dvv
        first = jnp.logical_and(pl.program_id(0) == 0, pl.program_id(1) == 0)
        _acc(dgq_ref, dgq, first)
        _acc(dgk_ref, dgk, first)

    kblk = pl.BlockSpec((1, m_, LANE), lambda b, h: (b, 0, h))
    gs = jax.ShapeDtypeStruct((1, LANE), F32)
    ks = jax.ShapeDtypeStruct((b_, m_, mw), F32)
    return pl.pallas_call(
        body, name="mem_bwd", grid=(b_, MEM_HEADS), in_specs=[qcol, kcol, vcol, ycol, gvec, gvec],
        out_specs=[ycol, kblk, kblk, gvec, gvec],
        out_shape=[jax.ShapeDtypeStruct((b_, t_, mw), MXU_DTYPE), ks, ks, gs, gs], compiler_params=_params(2),
    )(zm, mkv, mkv, dy, gq, gk)


def _merge_specs(tm, d, w, gcol):
    row_d = pl.BlockSpec((tm, d), lambda i: (i, 0))
    row_w = pl.BlockSpec((tm, w), lambda i: (i, 0))
    gates = [pl.BlockSpec((tm, d), functools.partial(lambda i, k: (i, gcol + k), k=k)) for k in range(3)]
    w_br = pl.BlockSpec((w, d), lambda i: (0, 0))
    w_o = pl.BlockSpec((d, d), lambda i: (0, 0))
    return row_d, row_w, gates, w_br, w_o


def _merge_fwd(x, ys, zm, w_brs, w_out, gcol, tm=256):
    n, d = x.shape
    w = ys[0].shape[1]
    tm = _tile(n, tm, 8)
    row_d, row_w, gates, w_br, w_o = _merge_specs(tm, d, w, gcol)

    def body(x_ref, ya, yb, yc, g0, g1, g2, wa, wb, wc, wo, x1_ref, mg_ref):
        mg = (_sig(g0[...]) * _nn(ya[...], wa[...]) + _sig(g1[...]) * _nn(yb[...], wb[...])
              + _sig(g2[...]) * _nn(yc[...], wc[...]))
        mg_ref[...] = mg.astype(mg_ref.dtype)
        x1_ref[...] = x_ref[...] + _nn(mg, wo[...])

    return pl.pallas_call(
        body, name="merge_fwd", grid=(n // tm,),
        in_specs=[row_d, row_w, row_w, row_w] + gates + [w_br, w_br, w_br, w_o],
        out_specs=[row_d, row_d],
        out_shape=[jax.ShapeDtypeStruct((n, d), F32), jax.ShapeDtypeStruct((n, d), MXU_DTYPE)],
        compiler_params=_params(1),
    )(x, *ys, zm, zm, zm, *w_brs, w_out)


def _merge_bwd(dx1, ys, zm, w_brs, w_out, gcol, tm=256):
    n, d = dx1.shape
    w = ys[0].shape[1]
    tm = _tile(n, tm, 8)
    row_d, row_w, gates, w_br, w_o = _merge_specs(tm, d, w, gcol)

    def body(dx_ref, ya, yb, yc, g0, g1, g2, wa, wb, wc, wo, dgl_ref, dpa, dpb, dpc, dya, dyb, dyc):
        dm = _nt(dx_ref[...], wo[...])
        for k, (y, g, wr, dp_ref, dy_ref) in enumerate(((ya, g0, wa, dpa, dya), (yb, g1, wb, dpb, dyb),
                                                        (yc, g2, wc, dpc, dyc))):
            sg = _sig(g[...])
            pr = _nn(y[...], wr[...])
            dgl_ref[:, k * d:(k + 1) * d] = (dm * pr * sg * (1.0 - sg)).astype(dgl_ref.dtype)
            dp = (dm * sg).astype(dp_ref.dtype)
            dp_ref[...] = dp
            dy_ref[...] = _nt(dp, wr[...])

    sd = jax.ShapeDtypeStruct((n, d), MXU_DTYPE)
    sw = jax.ShapeDtypeStruct((n, w), F32)
    return pl.pallas_call(
        body, name="merge_bwd", grid=(n // tm,),
        in_specs=[row_d, row_w, row_w, row_w] + gates + [w_br, w_br, w_br, w_o],
        out_specs=[pl.BlockSpec((tm, 3 * d), lambda i: (i, 0)), row_d, row_d, row_d, row_w, row_w, row_w],
        out_shape=[jax.ShapeDtypeStruct((n, 3 * d), MXU_DTYPE), sd, sd, sd, sw, sw, sw],
        compiler_params=_params(1),
    )(dx1, *ys, zm, zm, zm, *w_brs, w_out)


CONV_ROWS = 256
HALO = 8


def _ext(ref, r0, t_, lo, hi):
    rc = min(CONV_ROWS, t_)
    a, b = max(r0 - HALO, 0), min(r0 + rc + HALO, t_)
    parts = []
    if r0 - HALO < 0:
        parts.append(jnp.zeros((HALO, hi - lo), F32))
    parts.append(ref[0, a:b, lo:hi].astype(F32))
    if r0 + rc + HALO > t_:
        parts.append(jnp.zeros((HALO, hi - lo), F32))
    return jnp.concatenate(parts, axis=0) if len(parts) > 1 else parts[0]


def _gelu_parts(ac):
    cdf = 0.5 * (1.0 + _erf(ac * (2.0 ** -0.5)))
    pdf = jnp.exp(-0.5 * ac * ac) * ((2.0 * math.pi) ** -0.5)
    return cdf, pdf


def _conv_taps(a_ext, cw, cb):
    return cw[0:1, :] * pltpu.roll(a_ext, 2, 0) + cw[1:2, :] * pltpu.roll(a_ext, 1, 0) + cw[2:3, :] * a_ext + cb


def _glu_fwd(u, cw, cb):
    b_, t_, f2 = u.shape
    f = f2 // 2
    g = min(FFN_GROUP, f)
    rc = min(CONV_ROWS, t_)

    def body(u_ref, cw_ref, cb_ref, y_ref):
        cwv, cbv = cw_ref[...], cb_ref[...]
        for r0 in range(0, t_, rc):
            ac = _conv_taps(_ext(u_ref, r0, t_, 0, g), cwv, cbv)[HALO:HALO + rc]
            cdf, _ = _gelu_parts(ac)
            y_ref[0, r0:r0 + rc, :] = (ac * cdf * u_ref[0, r0:r0 + rc, g:2 * g]).astype(y_ref.dtype)

    return pl.pallas_call(
        body, name="glu_fwd", grid=(f // g, b_),
        in_specs=[pl.BlockSpec((1, t_, 2 * g), lambda j, b: (b, 0, j)), pl.BlockSpec((3, g), lambda j, b: (0, j)),
                  pl.BlockSpec((1, g), lambda j, b: (0, j))],
        out_specs=pl.BlockSpec((1, t_, g), lambda j, b: (b, 0, j)),
        out_shape=jax.ShapeDtypeStruct((b_, t_, f), MXU_DTYPE), compiler_params=_params(2),
    )(u, cw, cb)


def _glu_bwd(u, dy, cw, cb):
    b_, t_, f2 = u.shape
    f = f2 // 2
    g = min(FFN_GROUP, f)
    rc = min(CONV_ROWS, t_)
    ne = rc + 2 * HALO

    def body(u_ref, dy_ref, cw_ref, cb_ref, du_ref, dcw_ref, dcb_ref):
        cwv, cbv = cw_ref[...], cb_ref[...]
        dcw = [jnp.zeros((1, g), F32) for _ in range(3)]
        dcb = jnp.zeros((1, g), F32)
        for r0 in range(0, t_, rc):
            a_ext = _ext(u_ref, r0, t_, 0, g)
            v_ext = _ext(u_ref, r0, t_, g, 2 * g)
            dy_ext = _ext(dy_ref, r0, t_, 0, g)
            ac = _conv_taps(a_ext, cwv, cbv)
            cdf, pdf = _gelu_parts(ac)
            dac = dy_ext * v_ext * (cdf + ac * pdf)
            da = cwv[2:3, :] * dac + cwv[1:2, :] * pltpu.roll(dac, ne - 1, 0) + cwv[0:1, :] * pltpu.roll(dac, ne - 2, 0)
            mid = slice(HALO, HALO + rc)
            du_ref[0, r0:r0 + rc, 0:g] = da[mid].astype(du_ref.dtype)
            du_ref[0, r0:r0 + rc, g:2 * g] = (dy_ext[mid] * ac[mid] * cdf[mid]).astype(du_ref.dtype)
            dacm = dac[mid]
            dcw[0] = dcw[0] + jnp.sum(dacm * pltpu.roll(a_ext, 2, 0)[mid], axis=0, keepdims=True)
            dcw[1] = dcw[1] + jnp.sum(dacm * pltpu.roll(a_ext, 1, 0)[mid], axis=0, keepdims=True)
            dcw[2] = dcw[2] + jnp.sum(dacm * a_ext[mid], axis=0, keepdims=True)
            dcb = dcb + jnp.sum(dacm, axis=0, keepdims=True)
        first = pl.program_id(1) == 0
        _acc(dcw_ref, jnp.concatenate(dcw, axis=0), first)
        _acc(dcb_ref, dcb, first)

    ublk = pl.BlockSpec((1, t_, 2 * g), lambda j, b: (b, 0, j))
    cwb = pl.BlockSpec((3, g), lambda j, b: (0, j))
    cbb = pl.BlockSpec((1, g), lambda j, b: (0, j))
    return pl.pallas_call(
        body, name="glu_bwd", grid=(f // g, b_),
        in_specs=[ublk, pl.BlockSpec((1, t_, g), lambda j, b: (b, 0, j)), cwb, cbb],
        out_specs=[ublk, cwb, cbb],
        out_shape=[jax.ShapeDtypeStruct((b_, t_, f2), MXU_DTYPE), jax.ShapeDtypeStruct((3, f), F32),
                   jax.ShapeDtypeStruct((1, f), F32)],
        compiler_params=_params(2),
    )(u, dy, cw, cb)


def _loss_head(x1, ffn, target, tm=512):
    n, d = x1.shape
    tm = _tile(n, tm, 8)

    def body(x_ref, f_ref, t_ref, dy_ref, l_ref):
        err = x_ref[...] + f_ref[...] - t_ref[...]
        dy_ref[...] = err * (1.0 / d)
        _acc(l_ref, jnp.sum(err * err, axis=0, keepdims=True) * (0.5 / d), pl.program_id(0) == 0)

    row = pl.BlockSpec((tm, d), lambda i: (i, 0))
    vec = pl.BlockSpec((1, d), lambda i: (0, 0))
    return pl.pallas_call(
        body, name="loss_head", grid=(n // tm,), in_specs=[row, row, row], out_specs=[row, vec],
        out_shape=[jax.ShapeDtypeStruct((n, d), F32), jax.ShapeDtypeStruct((1, d), F32)], compiler_params=_params(1),
    )(x1, ffn, target)


def _place():
    x, y, c = lax.axis_index("x"), lax.axis_index("y"), lax.axis_index("c")
    chips = [(1 - x, y), (x, 1 - y), (1 - x, 1 - y)]
    return x, y, c, chips


def _remote(src, dst, send_sem, recv_sem, to):
    return pltpu.make_async_remote_copy(src_ref=src, dst_ref=dst, send_sem=send_sem, recv_sem=recv_sem,
                                        device_id=to, device_id_type=MESH)


STACK, COLS = "stack", "cols"


def _shard_ref(ref, kind, s, rows, c):
    if kind == COLS:
        cols = pl.ds(pl.multiple_of(s * c, LANE), c)
        return ref.at[:, cols] if rows is None else ref.at[rows, cols]
    return ref.at[s] if rows is None else ref.at[s, rows, :]


def _halves(c, half):
    mine = pl.ds(pl.multiple_of(c * half, 16), half)
    theirs = pl.ds(pl.multiple_of((1 - c) * half, 16), half)
    return mine, theirs


def _gather_shards(shards, kinds):
    nw = len(shards)

    def body(*refs):
        ins, outs = refs[:nw], refs[nw:2 * nw]
        send_sems, recv_sems = refs[2 * nw:]
        x, y, c, chips = _place()
        me, sib = 2 * x + y, (x, y, 1 - c)
        first, passed = [], []
        for i, (w_ref, o_ref, kind) in enumerate(zip(ins, outs, kinds)):
            r, cw = w_ref.shape
            mine, _ = _halves(c, r // 2)
            for j, chip in enumerate(chips):
                first.append(_remote(w_ref.at[mine], _shard_ref(o_ref, kind, me, mine, cw), send_sems.at[6 * i + j],
                                     recv_sems.at[6 * i + j], (*chip, c)))
                first[-1].start()
        for i, (w_ref, o_ref, kind) in enumerate(zip(ins, outs, kinds)):
            r, cw = w_ref.shape
            mine, _ = _halves(c, r // 2)
            for j, (px, py) in enumerate(chips):
                blk = _shard_ref(o_ref, kind, 2 * px + py, mine, cw)
                _remote(blk, blk, send_sems.at[6 * i + j], recv_sems.at[6 * i + j], sib).wait_recv()
                passed.append(_remote(blk, blk, send_sems.at[6 * i + 3 + j], recv_sems.at[6 * i + 3 + j], sib))
                passed[-1].start()
        for i, (w_ref, o_ref, kind) in enumerate(zip(ins, outs, kinds)):
            r, cw = w_ref.shape
            _, theirs = _halves(c, r // 2)
            for j, (px, py) in enumerate(chips):
                blk = _shard_ref(o_ref, kind, 2 * px + py, theirs, cw)
                _remote(blk, blk, send_sems.at[6 * i + 3 + j], recv_sems.at[6 * i + 3 + j], sib).wait_recv()
        for cp in first + passed:
            cp.wait_send()

    def out_sds(a, kind):
        r, c = a.shape
        return jax.ShapeDtypeStruct((r, N_CHIPS * c) if kind == COLS else (N_CHIPS, r, c), a.dtype)

    return pl.pallas_call(
        body, name="gather_shards", in_specs=[ANY] * nw, out_specs=[ANY] * nw,
        out_shape=[out_sds(a, k) for a, k in zip(shards, kinds)],
        scratch_shapes=[pltpu.SemaphoreType.DMA((6 * nw,)), pltpu.SemaphoreType.DMA((6 * nw,))],
    )(*shards)


def _half_shape(g, kind):
    if kind == COLS:
        return (g.shape[0] // 2, g.shape[1])
    return (g.shape[0], g.shape[1] // 2, g.shape[2])


def _pair_swap_halves(gs, kinds):
    nw = len(gs)

    def body(*refs):
        ins, outs = refs[:nw], refs[nw:2 * nw]
        send_sems, recv_sems = refs[2 * nw:]
        x, y, c, _ = _place()
        cps = []
        for i, (g_ref, a_ref, kind) in enumerate(zip(ins, outs, kinds)):
            r = g_ref.shape[0] if kind == COLS else g_ref.shape[1]
            _, theirs = _halves(c, r // 2)
            src = g_ref.at[theirs] if kind == COLS else g_ref.at[:, theirs]
            cps.append(_remote(src, a_ref, send_sems.at[i], recv_sems.at[i], (x, y, 1 - c)))
            cps[-1].start()
        for cp in cps:
            cp.wait()

    return pl.pallas_call(
        body, name="pair_swap_halves", in_specs=[ANY] * nw, out_specs=[ANY] * nw,
        out_shape=[jax.ShapeDtypeStruct(_half_shape(g, k), g.dtype) for g, k in zip(gs, kinds)],
        scratch_shapes=[pltpu.SemaphoreType.DMA((nw,)), pltpu.SemaphoreType.DMA((nw,))],
    )(*gs)


def _row_tile(rows, width, itemsize=4, target=2 ** 21):
    return _tile(rows, max(8, target // (width * itemsize)), 8)


def _add_half(g, a, kind, c_idx, name):
    if kind == COLS:
        half, wd = a.shape
        tr = _row_tile(half, wd)
        nblk = half // tr
        grid = (nblk,)
        g_spec = pl.BlockSpec((tr, wd), lambda i, c_ref: (c_ref[0] * nblk + i, 0))
        a_spec = pl.BlockSpec((tr, wd), lambda i, c_ref: (i, 0))
    else:
        n, half, wd = a.shape
        tr = _row_tile(half, wd)
        nblk = half // tr
        grid = (n, nblk)
        g_spec = pl.BlockSpec((1, tr, wd), lambda s, i, c_ref: (s, c_ref[0] * nblk + i, 0))
        a_spec = pl.BlockSpec((1, tr, wd), lambda s, i, c_ref: (s, i, 0))

    def body(c_ref, g_ref, a_ref, o_ref):
        o_ref[...] = (g_ref[...] + a_ref[...]).astype(o_ref.dtype)

    return pl.pallas_call(
        body, name=name,
        grid_spec=pltpu.PrefetchScalarGridSpec(num_scalar_prefetch=1, grid=grid, in_specs=[g_spec, a_spec],
                                               out_specs=a_spec),
        out_shape=jax.ShapeDtypeStruct(a.shape, EXCHANGE_DTYPE), compiler_params=_params(len(grid)),
    )(c_idx, g, a)


def _chip_exchange(ps, kinds):
    nw = len(ps)

    def shard_shape(p, kind):
        return (p.shape[0], p.shape[1] // N_CHIPS) if kind == COLS else p.shape[1:]

    def body(*refs):
        ins, outs = refs[:nw], refs[nw:2 * nw]
        send_sems, recv_sems = refs[2 * nw:]
        x, y, c, chips = _place()
        me = 2 * x + y
        sent = []
        for i, (p_ref, b_ref, kind) in enumerate(zip(ins, outs, kinds)):
            cw = b_ref.shape[2]
            for j, (px, py) in enumerate(chips):
                sent.append(_remote(_shard_ref(p_ref, kind, 2 * px + py, None, cw), b_ref.at[me],
                                    send_sems.at[3 * i + j], recv_sems.at[3 * i + j], (px, py, c)))
                sent[-1].start()
        for i, b_ref in enumerate(outs):
            for j, (px, py) in enumerate(chips):
                blk = b_ref.at[2 * px + py]
                _remote(blk, blk, send_sems.at[3 * i + j], recv_sems.at[3 * i + j], (px, py, c)).wait_recv()
        for cp in sent:
            cp.wait_send()

    return pl.pallas_call(
        body, name="chip_exchange", in_specs=[ANY] * nw, out_specs=[ANY] * nw,
        out_shape=[jax.ShapeDtypeStruct((N_CHIPS,) + tuple(shard_shape(p, k)), p.dtype) for p, k in zip(ps, kinds)],
        scratch_shapes=[pltpu.SemaphoreType.DMA((3 * nw,)), pltpu.SemaphoreType.DMA((3 * nw,))],
    )(*ps)


def _sum_chips(bq, name):
    n, h, wd = bq.shape
    tr = _row_tile(h, wd * n)

    def body(b_ref, o_ref):
        acc = b_ref[0].astype(F32)
        for s in range(1, n):
            acc = acc + b_ref[s].astype(F32)
        o_ref[...] = acc

    return pl.pallas_call(
        body, name=name, grid=(h // tr,),
        in_specs=[pl.BlockSpec((n, tr, wd), lambda i: (0, i, 0))], out_specs=pl.BlockSpec((tr, wd), lambda i: (i, 0)),
        out_shape=jax.ShapeDtypeStruct((h, wd), F32), compiler_params=_params(1),
    )(bq)


def _pair_join_halves(qs):
    nw = len(qs)

    def body(*refs):
        ins, outs = refs[:nw], refs[nw:2 * nw]
        send_sems, recv_sems = refs[2 * nw:]
        x, y, c, _ = _place()
        sent = []
        for i, (q_ref, o_ref) in enumerate(zip(ins, outs)):
            mine, _ = _halves(c, q_ref.shape[0])
            sent.append(_remote(q_ref, o_ref.at[mine], send_sems.at[i], recv_sems.at[i], (x, y, 1 - c)))
            sent[-1].start()
        for i, (q_ref, o_ref) in enumerate(zip(ins, outs)):
            _, theirs = _halves(c, q_ref.shape[0])
            _remote(q_ref, o_ref.at[theirs], send_sems.at[i], recv_sems.at[i], (x, y, 1 - c)).wait_recv()
        for cp in sent:
            cp.wait_send()

    return pl.pallas_call(
        body, name="pair_join_halves", in_specs=[ANY] * nw, out_specs=[ANY] * nw,
        out_shape=[jax.ShapeDtypeStruct((2 * q.shape[0], q.shape[1]), q.dtype) for q in qs],
        scratch_shapes=[pltpu.SemaphoreType.DMA((nw,)), pltpu.SemaphoreType.DMA((nw,))],
    )(*qs)


def _all_sum_small(s, name):
    sr, w = s.shape

    def body(s_ref, o_ref, buf, send_sems, recv_sems):
        x, y, c, _ = _place()
        me = 4 * x + 2 * y + c
        buf[me] = s_ref[...]
        peers = []
        for k in range(1, 8):
            px = 1 - x if k & 4 else x
            py = 1 - y if k & 2 else y
            pc = 1 - c if k & 1 else c
            peers.append((px, py, pc))
        sent = [_remote(s_ref, buf.at[me], send_sems.at[k], recv_sems.at[k], peer) for k, peer in enumerate(peers)]
        for cp in sent:
            cp.start()
        for k, (px, py, pc) in enumerate(peers):
            _remote(s_ref, buf.at[4 * px + 2 * py + pc], send_sems.at[k], recv_sems.at[k], (px, py, pc)).wait_recv()
        for cp in sent:
            cp.wait_send()
        acc = buf[0]
        for d in range(1, 8):
            acc = acc + buf[d]
        o_ref[...] = acc

    vm = pl.BlockSpec(memory_space=pltpu.VMEM)
    return pl.pallas_call(
        body, name=name, in_specs=[vm], out_specs=vm, out_shape=jax.ShapeDtypeStruct((sr, w), F32),
        scratch_shapes=[pltpu.VMEM((8, sr, w), F32), pltpu.SemaphoreType.DMA((7,)), pltpu.SemaphoreType.DMA((7,))],
    )(s)


BIG = ("w_in", "mem_kv_w", "w_br_hgrn", "w_br_fox", "w_br_mem", "w_out", "ffn_w_up", "ffn_w_down")
KIND = {"w_in": STACK, "mem_kv_w": STACK, "w_br_hgrn": COLS, "w_br_fox": COLS, "w_br_mem": COLS, "w_out": STACK,
        "ffn_w_up": COLS, "ffn_w_down": STACK}
ROW_SHARDED = ("mem_kv_w", "w_out", "ffn_w_down")


def _put_shard(arr, kind, s, piece):
    if kind == COLS:
        return lax.dynamic_update_slice(arr, piece, (0, s * piece.shape[1]))
    return lax.dynamic_update_slice(arr, piece[None], (s, 0, 0))


def _take_shard(arr, kind, s):
    if kind == COLS:
        return lax.dynamic_slice(arr, (0, s * (arr.shape[1] // N_CHIPS)), (arr.shape[0], arr.shape[1] // N_CHIPS))
    return lax.dynamic_index_in_dim(arr, s, 0, keepdims=False)


def _w_in_pieces(cs, s1, nf):
    out = []
    for s in range(N_CHIPS):
        lo, hi = cs * s, cs * (s + 1)
        for a, b, forget in ((lo, min(hi, s1), False), (max(lo, s1), min(hi, s1 + nf), True), (max(lo, s1 + nf), hi, False)):
            if a < b:
                out.append((s, a - lo, b - lo, forget, a - s1 if forget else (a if a < s1 else a - nf)))
    return out


def _split_w_in(stacked, s1, nf):
    pieces = _w_in_pieces(stacked.shape[2], s1, nf)
    main = [stacked[s, :, a:b] for s, a, b, forget, _ in pieces if not forget]
    ff = [stacked[s, :, a:b] for s, a, b, forget, _ in pieces if forget]
    return jnp.concatenate(main, axis=1), jnp.concatenate(ff, axis=1)


def _join_w_in(g_main, g_ff, s1, nf):
    cs = (g_main.shape[1] + nf) // N_CHIPS
    shards = [[] for _ in range(N_CHIPS)]
    for s, a, b, forget, off in _w_in_pieces(cs, s1, nf):
        shards[s].append((g_ff if forget else g_main)[:, off:off + b - a])
    return jnp.stack([jnp.concatenate(p, axis=1) if len(p) > 1 else p[0] for p in shards])


SMALL = ("norm_mix_g", "norm_mem_g", "norm_ffn_g", "hgrn_lb_logits", "hgrn_norm_g", "fox_f_bias", "fox_q_norm_g",
         "fox_k_norm_g", "mem_q_norm_g", "mem_k_norm_g", "ffn_conv_b")


def _pack_small(vals):
    flats, total = [], 0
    for v in vals:
        flat = v.reshape(-1).astype(F32)
        n = -(-flat.shape[0] // FLAT_W)
        flats.append(jnp.pad(flat, (0, n * FLAT_W - flat.shape[0])))
        total += n
    if -total % 8:
        flats.append(jnp.zeros((-total % 8 * FLAT_W,), F32))
    return jnp.concatenate(flats).reshape(-1, FLAT_W)


def _unpack_small(buf, shapes):
    res, off = [], 0
    for shp in shapes:
        numel = math.prod(shp)
        n = -(-numel // FLAT_W)
        res.append(buf[off:off + n].reshape(-1)[:numel].reshape(shp))
        off += n
    return res


def _pad_lanes(v, width=LANE):
    return jnp.pad(v, ((0, 0), (0, width - v.shape[1])))


WEIGHTS = ("norm_mix_g", "norm_mem_g", "w_in", "hgrn_lb_logits", "hgrn_norm_g", "fox_f_bias", "fox_q_norm_g",
           "fox_k_norm_g", "mem_kv_w", "mem_q_norm_g", "mem_k_norm_g", "w_br_hgrn", "w_br_fox", "w_br_mem", "w_out",
           "norm_ffn_g", "ffn_w_up", "ffn_conv_w", "ffn_conv_b", "ffn_w_down")


def _local_step(x, mem, target, w, full, conv_w):
    b_, t_, d = x.shape
    n = b_ * t_
    hw, fw, mw = HG_HEADS * HG_D, FOX_HEADS * FOX_DH, MEM_HEADS * MEM_DH
    m_ = mem.shape[1]
    f = conv_w.shape[1]
    g = min(FFN_GROUP, f)
    s1 = 4 * hw + 3 * fw
    fox_col, mem_col, gate_col = 4 * hw // LANE, s1 // LANE, (s1 + mw) // d

    w_main, w_ff = _split_w_in(full["w_in"], s1, FOX_HEADS)
    w_ff = _pad_lanes(w_ff)
    w_up = full["ffn_w_up"].reshape(d, 2, f // g, g).transpose(0, 2, 1, 3).reshape(d, 2 * f)
    w_brs = [full["w_br_hgrn"], full["w_br_fox"], full["w_br_mem"]]
    w_out, w_kv, w_down = full["w_out"], full["mem_kv_w"], full["ffn_w_down"]
    f_bias = _pad_lanes(w["fox_f_bias"])
    cb = w["ffn_conv_b"]

    x2 = x.reshape(n, d)
    h = _rmsnorm_fwd(x2, w["norm_mix_g"], name="norm_mix_fwd")
    zm = _matmul(h, w_main, name="in_proj")
    zf = _matmul(h, w_ff, name="in_proj_forget")
    zm3, zf3 = zm.reshape(b_, t_, -1), zf.reshape(b_, t_, LANE)
    ya = _hgrn_fwd(zm3, w["hgrn_lb_logits"], w["hgrn_norm_g"], hw)
    fc = _fox_prep(zf3, f_bias)
    fox_gq, fox_gk = jnp.tile(w["fox_q_norm_g"], (1, 2)), jnp.tile(w["fox_k_norm_g"], (1, 2))
    yb, lse = _fox_fwd(zm3, fc, fox_gq, fox_gk, fw, fox_col)
    mem2 = mem.reshape(b_ * m_, d)
    hm = _rmsnorm_fwd(mem2, w["norm_mem_g"], name="norm_mem_fwd")
    mkv = _matmul(hm, w_kv, name="mem_kv_proj").reshape(b_, m_, 2 * mw)
    yc = _mem_fwd(zm3, mkv, w["mem_q_norm_g"], w["mem_k_norm_g"], mw, mem_col)
    ys = [ya.reshape(n, hw), yb.reshape(n, fw), yc.reshape(n, mw)]
    x1, merged = _merge_fwd(x2, ys, zm, w_brs, w_out, gate_col)
    h2 = _rmsnorm_fwd(x1, w["norm_ffn_g"], name="norm_ffn_fwd")
    u = _matmul(h2, w_up, name="ffn_up")
    u3 = u.reshape(b_, t_, 2 * f)
    yff = _glu_fwd(u3, conv_w, cb).reshape(n, f)
    ffn = _matmul(yff, w_down, name="ffn_down", tk=1408)
    dy, loss_vec = _loss_head(x1, ffn, target.reshape(n, d))

    grads = {}
    dyff = _matmul(dy, w_down, tb=True, name="ffn_down_dx", tn=1408)
    grads["ffn_w_down"] = _matmul(yff, dy, ta=True, name="ffn_down_dw", tm=1408)
    du, grads["ffn_conv_w"], grads["ffn_conv_b"] = _glu_bwd(u3, dyff.reshape(b_, t_, f), conv_w, cb)
    du2 = du.reshape(n, 2 * f)
    dh2 = _matmul(du2, w_up, tb=True, name="ffn_up_dx")
    g_up = _matmul(h2, du2, ta=True, name="ffn_up_dw")
    grads["ffn_w_up"] = g_up.reshape(d, f // g, 2, g).transpose(0, 2, 1, 3).reshape(d, 2 * f)
    dx1, grads["norm_ffn_g"] = _rmsnorm_bwd(x1, [dh2], w["norm_ffn_g"], dy, name="norm_ffn_bwd")

    dgl, dpa, dpb, dpc, dya, dyb, dyc = _merge_bwd(dx1, ys, zm, w_brs, w_out, gate_col)
    grads["w_out"] = _matmul(merged, dx1, ta=True, name="out_proj_dw")
    for nm, y_, dp_ in zip(("w_br_hgrn", "w_br_fox", "w_br_mem"), ys, (dpa, dpb, dpc)):
        grads[nm] = _matmul(y_, dp_, ta=True, name=nm + "_dw")

    dmq, dmk, dmv, grads["mem_q_norm_g"], grads["mem_k_norm_g"] = _mem_bwd(
        zm3, mkv, dyc.reshape(b_, t_, mw), w["mem_q_norm_g"], w["mem_k_norm_g"], mw, mem_col)
    dmkv = jnp.concatenate([dmk, dmv], axis=-1).reshape(b_ * m_, 2 * mw)
    grads["mem_kv_w"] = _matmul(hm, dmkv, ta=True, name="mem_kv_dw")
    dhm = _matmul(dmkv, w_kv, tb=True, name="mem_kv_dx")
    _, grads["norm_mem_g"] = _rmsnorm_bwd(mem2, [dhm], w["norm_mem_g"], None, name="norm_mem_bwd")

    dfq, dfk, dfv, dfc, g_fq, g_fk = _fox_bwd(zm3, yb, dyb.reshape(b_, t_, fw), lse, fc, fox_gq, fox_gk, fw, fox_col)
    grads["fox_q_norm_g"] = g_fq[:, :FOX_DH] + g_fq[:, FOX_DH:]
    grads["fox_k_norm_g"] = g_fk[:, :FOX_DH] + g_fk[:, FOX_DH:]
    dfc = dfc[..., :2].transpose(0, 2, 1, 3).reshape(b_, t_, FOX_HEADS)
    dfc = jnp.pad(dfc, ((0, 0), (0, 0), (0, LANE - FOX_HEADS)))
    dzf, g_fb = _fox_post(dfc, zf3, f_bias)
    grads["fox_f_bias"] = g_fb[:, :FOX_HEADS]

    dhq, dhf, dhi, dhg, grads["hgrn_lb_logits"], grads["hgrn_norm_g"] = _hgrn_bwd(
        zm3, dya.reshape(b_, t_, hw), w["hgrn_lb_logits"], w["hgrn_norm_g"], hw)

    dzm = jnp.concatenate([dhq, dhf, dhi, dhg, dfq, dfk, dfv, dmq, dgl.reshape(b_, t_, 3 * d)], axis=-1).reshape(n, -1)
    dzf2 = dzf.reshape(n, LANE)
    dh_a = _matmul(dzm, w_main, tb=True, name="in_proj_dx")
    dh_b = _matmul(dzf2, w_ff, tb=True, name="in_proj_forget_dx")
    g_main = _matmul(h, dzm, ta=True, name="in_proj_dw")
    g_ff = _matmul(h, dzf2, ta=True, name="in_proj_forget_dw")
    grads["w_in"] = _join_w_in(g_main, g_ff[:, :FOX_HEADS], s1, FOX_HEADS)
    grad_x, grads["norm_mix_g"] = _rmsnorm_bwd(x2, [dh_a, dh_b], w["norm_mix_g"], dx1, name="norm_mix_bwd")
    return loss_vec, grad_x.reshape(b_, t_, d), grads


def kernel(x, mem, norm_mix_g, norm_mem_g, w_in, hgrn_lb_logits, hgrn_norm_g, fox_f_bias, fox_q_norm_g, fox_k_norm_g, mem_kv_w, mem_q_norm_g, mem_k_norm_g, w_br_hgrn, w_br_fox, w_br_mem, w_out, norm_ffn_g, ffn_w_up, ffn_conv_w, ffn_conv_b, ffn_w_down, loss_target, m_norm_mix_g, m_norm_mem_g, m_w_in, m_hgrn_lb_logits, m_hgrn_norm_g, m_fox_f_bias, m_fox_q_norm_g, m_fox_k_norm_g, m_mem_kv_w, m_mem_q_norm_g, m_mem_k_norm_g, m_w_br_hgrn, m_w_br_fox, m_w_br_mem, m_w_out, m_norm_ffn_g, m_ffn_w_up, m_ffn_conv_w, m_ffn_conv_b, m_ffn_w_down, v_norm_mix_g, v_norm_mem_g, v_w_in, v_hgrn_lb_logits, v_hgrn_norm_g, v_fox_f_bias, v_fox_q_norm_g, v_fox_k_norm_g, v_mem_kv_w, v_mem_q_norm_g, v_mem_k_norm_g, v_w_br_hgrn, v_w_br_fox, v_w_br_mem, v_w_out, v_norm_ffn_g, v_ffn_w_up, v_ffn_conv_w, v_ffn_conv_b, v_ffn_w_down):
    w = dict(zip(WEIGHTS, (norm_mix_g, norm_mem_g, w_in, hgrn_lb_logits, hgrn_norm_g, fox_f_bias, fox_q_norm_g,
                           fox_k_norm_g, mem_kv_w, mem_q_norm_g, mem_k_norm_g, w_br_hgrn, w_br_fox, w_br_mem, w_out,
                           norm_ffn_g, ffn_w_up, ffn_conv_w, ffn_conv_b, ffn_w_down)))
    m = dict(zip(WEIGHTS, (m_norm_mix_g, m_norm_mem_g, m_w_in, m_hgrn_lb_logits, m_hgrn_norm_g, m_fox_f_bias,
                           m_fox_q_norm_g, m_fox_k_norm_g, m_mem_kv_w, m_mem_q_norm_g, m_mem_k_norm_g, m_w_br_hgrn,
                           m_w_br_fox, m_w_br_mem, m_w_out, m_norm_ffn_g, m_ffn_w_up, m_ffn_conv_w, m_ffn_conv_b,
                           m_ffn_w_down)))
    v = dict(zip(WEIGHTS, (v_norm_mix_g, v_norm_mem_g, v_w_in, v_hgrn_lb_logits, v_hgrn_norm_g, v_fox_f_bias,
                           v_fox_q_norm_g, v_fox_k_norm_g, v_mem_kv_w, v_mem_q_norm_g, v_mem_k_norm_g, v_w_br_hgrn,
                           v_w_br_fox, v_w_br_mem, v_w_out, v_norm_ffn_g, v_ffn_w_up, v_ffn_conv_w, v_ffn_conv_b,
                           v_ffn_w_down)))
    c_idx = lax.axis_index("c")
    chip = 2 * lax.axis_index("x") + lax.axis_index("y")

    kinds = [KIND[nm] for nm in BIG]
    mine = [w[nm][0].astype(MXU_DTYPE) for nm in BIG]
    full = {nm: _put_shard(g, k, chip, own) for nm, g, k, own in zip(BIG, _gather_shards(mine, kinds), kinds, mine)}
    for nm in ROW_SHARDED:
        full[nm] = full[nm].reshape(-1, full[nm].shape[2])
    cs = ffn_conv_w.shape[2]
    f = cs * N_CHIPS
    placed = lax.dynamic_update_slice(jnp.zeros((3, f), F32), ffn_conv_w[0] * (c_idx == 0).astype(F32), (0, chip * cs))
    conv_w = _unpack_small(_all_sum_small(_pack_small([placed]), "gather_conv_w"), [(3, f)])[0]

    loss_vec, grad_x, grads = _local_step(x, mem, loss_target, w, full, conv_w)

    gs = [grads[nm].reshape(N_CHIPS, -1, grads[nm].shape[1]) if nm in ROW_SHARDED else grads[nm] for nm in BIG]
    from_sibling = _pair_swap_halves(gs, kinds)
    c_arr = jnp.reshape(c_idx, (1,)).astype(jnp.int32)
    chip_partial = [_add_half(g, a, k, c_arr, "add_half_" + nm) for g, a, k, nm in zip(gs, from_sibling, kinds, BIG)]
    landed = [_put_shard(bq, STACK, chip, _take_shard(p, k, chip))
              for bq, p, k in zip(_chip_exchange(chip_partial, kinds), chip_partial, kinds)]
    reduced_half = [_sum_chips(bq, "sum_chips_" + nm) for bq, nm in zip(landed, BIG)]
    joined = [lax.dynamic_update_slice(o, q, (c_idx * q.shape[0], 0))
              for o, q in zip(_pair_join_halves(reduced_half), reduced_half)]
    gshards = dict(zip(BIG, joined))

    small_names = SMALL + ("ffn_conv_w",)
    summed = _unpack_small(
        _all_sum_small(_pack_small([grads[nm] for nm in small_names] + [loss_vec]), "all_sum_small_grads"),
        [grads[nm].shape for nm in small_names] + [loss_vec.shape])
    gsmall = dict(zip(small_names, summed[:-1]))
    loss = jnp.sum(summed[-1])
    g_out = {nm: gshards[nm][None] for nm in BIG}
    for nm in SMALL:
        g_out[nm] = gsmall[nm].reshape(w[nm].shape)
    g_out["ffn_conv_w"] = lax.dynamic_slice(gsmall["ffn_conv_w"], (0, chip * cs), (3, cs))[None]

    delta, new_m, new_v = {}, {}, {}
    for nm in BIG + ("ffn_conv_w",):
        d_, m_, v_ = _adamw(w[nm][0], g_out[nm][0], m[nm][0], v[nm][0], name="adamw_" + nm)
        delta[nm], new_m[nm], new_v[nm] = d_[None], m_[None], v_[None]
    packed = [_pack_small([t[nm] for nm in SMALL]) for t in (w, g_out, m, v)]
    outs = _adamw(*packed, name="adamw_small")
    shapes = [w[nm].shape for nm in SMALL]
    for res, o in zip((delta, new_m, new_v), outs):
        res.update(zip(SMALL, _unpack_small(o, shapes)))

    return (loss, grad_x, *[g_out[nm] for nm in WEIGHTS], *[delta[nm] for nm in WEIGHTS],
            *[new_m[nm] for nm in WEIGHTS], *[new_v[nm] for nm in WEIGHTS])
```

```python
import functools
import math

import jax
import jax.numpy as jnp
from jax import lax
from jax.experimental import pallas as pl
from jax.experimental.pallas import tpu as pltpu

F32 = jnp.float32
BF16 = jnp.bfloat16
MXU_DTYPE = jnp.bfloat16
EXCHANGE_DTYPE = jnp.bfloat16

EPS = 1e-6
HG_HEADS, HG_D = 4, 128
FOX_HEADS, FOX_DH = 8, 64
MEM_HEADS, MEM_DH = 4, 128
HG_CHUNK = 64
FOX_BLOCK = 256
LANE = 128
FFN_GROUP = 256
FLAT_W = 1024
VMEM_LIMIT = 56 * 2 ** 20
NEG = -1e30
N_CHIPS = 4

ADAM_LR, ADAM_B1, ADAM_B2, ADAM_EPS, ADAM_WD, ADAM_STEP = 0.001, 0.9, 0.999, 1e-08, 0.01, 10

MESH = pl.DeviceIdType.MESH
ANY = pl.BlockSpec(memory_space=pl.ANY)


def _mx(x):
    return x.astype(MXU_DTYPE)


def _dot(a, b, ca, cb):
    return lax.dot_general(_mx(a), _mx(b), (((ca,), (cb,)), ((), ())), preferred_element_type=F32)


def _nn(a, b):
    return _dot(a, b, 1, 0)


def _nt(a, b):
    return _dot(a, b, 1, 1)


def _tn(a, b):
    return _dot(a, b, 0, 0)


def _dotp(a, b, ca, cb):
    return lax.dot_general(a, b, (((ca,), (cb,)), ((), ())), precision=lax.Precision.HIGHEST,
                           preferred_element_type=F32)


def _tri_dot(tri_bf, x):
    hi = x.astype(BF16)
    r = x - hi.astype(F32)
    mid = r.astype(BF16)
    lo = (r - mid.astype(F32)).astype(BF16)

    def d(v):
        return lax.dot_general(tri_bf, v, (((1,), (0,)), ((), ())), preferred_element_type=F32)

    return d(hi) + d(mid) + d(lo)


def _sig(x):
    return jax.nn.sigmoid(x)


def _erf(x):
    a = jnp.abs(x)
    t = 1.0 / (1.0 + 0.3275911 * a)
    poly = t * (0.254829592 + t * (-0.284496736 + t * (1.421413741 + t * (-1.453152027 + t * 1.061405429))))
    y = 1.0 - poly * jnp.exp(-a * a)
    return jnp.where(x < 0, -y, y)


def _tile(dim, pref, unit=LANE):
    if dim <= pref:
        return dim
    t = pref - pref % unit
    while t >= unit:
        if dim % t == 0:
            return t
        t -= unit
    return dim


def _params(n_grid):
    return pltpu.CompilerParams(dimension_semantics=("arbitrary",) * n_grid, vmem_limit_bytes=VMEM_LIMIT)


def _acc(ref, val, first):
    @pl.when(first)
    def _():
        ref[...] = val

    @pl.when(jnp.logical_not(first))
    def _():
        ref[...] += val


def _matmul(a, b, *, name, ta=False, tb=False, out_dtype=F32, tm=1024, tn=1024, tk=1024):
    m, k = (a.shape[1], a.shape[0]) if ta else a.shape
    n = b.shape[0] if tb else b.shape[1]
    tm, tn, tk = _tile(m, tm), _tile(n, tn), _tile(k, tk)
    nk = k // tk

    def body(a_ref, b_ref, o_ref, acc_ref):
        kk = pl.program_id(2)
        p = _dot(a_ref[...], b_ref[...], 0 if ta else 1, 1 if tb else 0)
        _acc(acc_ref, p, kk == 0)

        @pl.when(kk == nk - 1)
        def _():
            o_ref[...] = acc_ref[...].astype(o_ref.dtype)

    a_spec = pl.BlockSpec((tk, tm), lambda i, j, kk: (kk, i)) if ta else pl.BlockSpec((tm, tk), lambda i, j, kk: (i, kk))
    b_spec = pl.BlockSpec((tn, tk), lambda i, j, kk: (j, kk)) if tb else pl.BlockSpec((tk, tn), lambda i, j, kk: (kk, j))
    return pl.pallas_call(
        body, name=name, grid=(m // tm, n // tn, nk),
        in_specs=[a_spec, b_spec],
        out_specs=pl.BlockSpec((tm, tn), lambda i, j, kk: (i, j)),
        out_shape=jax.ShapeDtypeStruct((m, n), out_dtype),
        scratch_shapes=[pltpu.VMEM((tm, tn), F32)],
        compiler_params=_params(3),
    )(a, b)


def _rmsnorm_fwd(x, g, *, name, tm=512):
    n, d = x.shape
    tm = _tile(n, tm, 8)

    def body(x_ref, g_ref, o_ref):
        xv = x_ref[...]
        r = lax.rsqrt(jnp.mean(xv * xv, axis=-1, keepdims=True) + EPS)
        o_ref[...] = (xv * r * g_ref[...]).astype(o_ref.dtype)

    return pl.pallas_call(
        body, name=name, grid=(n // tm,),
        in_specs=[pl.BlockSpec((tm, d), lambda i: (i, 0)), pl.BlockSpec((1, d), lambda i: (0, 0))],
        out_specs=pl.BlockSpec((tm, d), lambda i: (i, 0)),
        out_shape=jax.ShapeDtypeStruct((n, d), MXU_DTYPE),
        compiler_params=_params(1),
    )(x, g)


def _rmsnorm_bwd(x, dhs, g, res, *, name, tm=512):
    n, d = x.shape
    tm = _tile(n, tm, 8)
    n_dh = len(dhs)
    has_res = res is not None

    def body(*refs):
        x_ref, dh_refs, g_ref = refs[0], refs[1:1 + n_dh], refs[1 + n_dh]
        res_ref = refs[2 + n_dh] if has_res else None
        dx_ref, dg_ref = refs[-2], refs[-1]
        xv = x_ref[...]
        dh = dh_refs[0][...].astype(F32)
        for r_ in dh_refs[1:]:
            dh = dh + r_[...].astype(F32)
        r = lax.rsqrt(jnp.mean(xv * xv, axis=-1, keepdims=True) + EPS)
        dhg = dh * g_ref[...]
        dx = r * dhg - xv * (r * r * r) * jnp.mean(dhg * xv, axis=-1, keepdims=True)
        if has_res:
            dx = dx + res_ref[...]
        dx_ref[...] = dx
        _acc(dg_ref, jnp.sum(dh * xv * r, axis=0, keepdims=True), pl.program_id(0) == 0)

    row = pl.BlockSpec((tm, d), lambda i: (i, 0))
    vec = pl.BlockSpec((1, d), lambda i: (0, 0))
    ins = [x] + list(dhs) + [g] + ([res] if has_res else [])
    return pl.pallas_call(
        body, name=name, grid=(n // tm,),
        in_specs=[row] * (1 + n_dh) + [vec] + ([row] if has_res else []),
        out_specs=[row, vec],
        out_shape=[jax.ShapeDtypeStruct((n, d), F32), jax.ShapeDtypeStruct((1, d), F32)],
        compiler_params=_params(1),
    )(*ins)


def _adamw(w, g, m, v, *, name, tr=256):
    r, c = w.shape
    tr = _tile(r, tr, 8)
    c1 = 1.0 / (1.0 - ADAM_B1 ** ADAM_STEP)
    c2 = 1.0 / (1.0 - ADAM_B2 ** ADAM_STEP)

    def body(w_ref, g_ref, m_ref, v_ref, d_ref, mo_ref, vo_ref):
        gv = g_ref[...]
        mn = ADAM_B1 * m_ref[...] + (1.0 - ADAM_B1) * gv
        vn = ADAM_B2 * v_ref[...] + (1.0 - ADAM_B2) * (gv * gv)
        d_ref[...] = -ADAM_LR * ((mn * c1) / (jnp.sqrt(vn * c2) + ADAM_EPS) + ADAM_WD * w_ref[...])
        mo_ref[...] = mn
        vo_ref[...] = vn

    blk = pl.BlockSpec((tr, c), lambda i: (i, 0))
    sds = jax.ShapeDtypeStruct((r, c), F32)
    return pl.pallas_call(
        body, name=name, grid=(r // tr,), in_specs=[blk] * 4, out_specs=[blk] * 3, out_shape=[sds] * 3,
        compiler_params=_params(1),
    )(w, g, m, v)


def _bdot(a, b, ca, cb):
    return lax.dot_general(_mx(a), _mx(b), (((ca,), (cb,)), ((0,), (0,))), preferred_element_type=F32)


def _bdotp(a, b, ca, cb):
    return lax.dot_general(a, b, (((ca,), (cb,)), ((0,), (0,))), precision=lax.Precision.HIGHEST,
                           preferred_element_type=F32)


def _tri_dot_b(tri_bf, x):
    hi = x.astype(BF16)
    r = x - hi.astype(F32)
    mid = r.astype(BF16)
    lo = (r - mid.astype(F32)).astype(BF16)

    def d(v):
        return lax.dot_general(tri_bf, v, (((2,), (1,)), ((0,), (0,))), preferred_element_type=F32)

    return d(hi) + d(mid) + d(lo)


def _hgrn_forward(hq, hf, hi, lbv, tril, tril_bf):
    nc, c, _ = hq.shape
    sf = _sig(hf)
    f = lbv + (1.0 - lbv) * sf
    k = 1.0 - f
    gcum = _tri_dot_b(tril_bf, jnp.log(f))
    mid = gcum[:, c // 2 - 1:c // 2, :]
    glast = gcum[:, c - 1:c, :]
    sq = _sig(hq)
    q = hq * sq
    e_q = jnp.exp(gcum - mid)
    e_k = jnp.exp(mid - gcum)
    qe, ke = q * e_q, k * e_k
    a = jnp.where(tril, _bdot(qe, ke, 2, 2), 0.0)
    e_g = jnp.exp(gcum)
    qg = q * e_g
    e_s = jnp.exp(glast - gcum)
    kg = k * e_s
    e_l = jnp.exp(glast)
    upd = _bdot(hi, kg, 1, 1)
    st = jnp.zeros((HG_D, HG_D), F32)
    states = []
    for n in range(nc):
        states.append(st)
        st = st * e_l[n] + upd[n]
    st_all = jnp.stack(states)
    o = _bdot(a, hi, 2, 1) + _bdot(qg, st_all, 2, 2)
    return dict(sf=sf, f=f, k=k, sq=sq, q=q, e_q=e_q, e_k=e_k, qe=qe, ke=ke, a=a, e_g=e_g, qg=qg, o=o,
                e_s=e_s, kg=kg, e_l=e_l, st_all=st_all)


def _hgrn_specs(t_, hw):
    nb = hw // LANE

    def col(off):
        return pl.BlockSpec((1, t_, LANE), lambda h, b: (b, 0, off * nb + h))

    vec = pl.BlockSpec((2, LANE), lambda h, b: (0, h))
    one = pl.BlockSpec((1, LANE), lambda h, b: (0, 0))
    blk = pl.BlockSpec((1, t_, LANE), lambda h, b: (b, 0, h))
    return col, vec, one, blk


def _chunk_masks(nc, c):
    row = lax.broadcasted_iota(jnp.int32, (nc, c, c), 1)
    cl = lax.broadcasted_iota(jnp.int32, (nc, c, c), 2)
    return row >= cl, (row >= cl).astype(BF16), (row <= cl).astype(BF16)


def _hgrn_fwd(zm, lb, gn, hw):
    b_, t_, _ = zm.shape
    c = min(HG_CHUNK, t_)
    nc = t_ // c
    col, vec, one, blk = _hgrn_specs(t_, hw)

    def body(q_ref, f_ref, i_ref, g_ref, lb_ref, gn_ref, y_ref):
        lbv, gnv = _sig(lb_ref[0:1, :] - lb_ref[1:2, :]), gn_ref[...]
        tril, tril_bf, _ = _chunk_masks(nc, c)
        chunks = lambda ref: ref[0].reshape(nc, c, LANE)
        o = _hgrn_forward(chunks(q_ref), chunks(f_ref), chunks(i_ref), lbv, tril, tril_bf)["o"]
        r = lax.rsqrt(jnp.mean(o * o, axis=-1, keepdims=True) + EPS)
        hg = chunks(g_ref)
        y_ref[0] = (o * r * gnv * (hg * _sig(hg))).reshape(t_, LANE)

    return pl.pallas_call(
        body, name="hgrn_fwd", grid=(HG_HEADS, b_),
        in_specs=[col(0), col(1), col(2), col(3), vec, one], out_specs=blk,
        out_shape=jax.ShapeDtypeStruct((b_, t_, hw), F32),
        compiler_params=_params(2),
    )(zm, zm, zm, zm, lb, gn)


def _hgrn_bwd(zm, dy, lb, gn, hw):
    b_, t_, _ = zm.shape
    c = min(HG_CHUNK, t_)
    nc = t_ // c
    col, vec, one, blk = _hgrn_specs(t_, hw)

    def body(q_ref, f_ref, i_ref, g_ref, dy_ref, lb_ref, gn_ref, dq_ref, df_ref, di_ref, dg_ref, dlb_ref, dgn_ref):
        h, b = pl.program_id(0), pl.program_id(1)
        lbv, gnv = _sig(lb_ref[0:1, :] - lb_ref[1:2, :]), gn_ref[...]
        tril, tril_bf, triu_bf = _chunk_masks(nc, c)
        last_row = lax.broadcasted_iota(jnp.int32, (nc, c, LANE), 1) == c - 1
        chunks = lambda ref: ref[0].reshape(nc, c, LANE)
        flat = lambda x: x.reshape(t_, LANE)
        hq, hi, hg = chunks(q_ref), chunks(i_ref), chunks(g_ref)
        p = _hgrn_forward(hq, chunks(f_ref), hi, lbv, tril, tril_bf)
        o, q, k, st_all, e_l = p["o"], p["q"], p["k"], p["st_all"], p["e_l"]
        dyv = chunks(dy_ref)
        sg = _sig(hg)
        r = lax.rsqrt(jnp.mean(o * o, axis=-1, keepdims=True) + EPS)
        dn = dyv * (hg * sg)
        dg_ref[0] = flat(dyv * (o * r * gnv) * (sg * (1.0 + hg * (1.0 - sg)))).astype(dg_ref.dtype)
        dgn = jnp.sum(flat(dn * o * r), axis=0, keepdims=True)
        dng = dn * gnv
        do = r * dng - o * (r * r * r) * jnp.mean(dng * o, axis=-1, keepdims=True)
        back = _bdotp(do, p["qg"], 1, 1)
        dst = jnp.zeros((HG_D, HG_D), F32)
        dsts = [None] * nc
        for n in range(nc - 1, -1, -1):
            dsts[n] = dst
            dst = dst * e_l[n] + back[n]
        dst_all = jnp.stack(dsts)
        da = jnp.where(tril, _bdotp(do, hi, 2, 2), 0.0)
        dq = _bdotp(da, p["ke"], 2, 1) * p["e_q"] + _bdotp(do, st_all, 2, 1) * p["e_g"]
        dk_state = _bdotp(hi, dst_all, 2, 1) * p["e_s"]
        dk = _bdotp(da, p["qe"], 1, 1) * p["e_k"] + dk_state
        di_ref[0] = flat(_bdot(p["a"], do, 1, 1) + _bdot(p["kg"], dst_all, 2, 2)).astype(di_ref.dtype)
        extra = (jnp.sum(k * dk_state, axis=1, keepdims=True) + e_l * jnp.sum(st_all * dst_all, axis=1, keepdims=True))
        dgc = q * dq - k * dk + jnp.where(last_row, extra, 0.0)
        dfv = _tri_dot_b(triu_bf, dgc) / p["f"] - dk
        sf, sq = p["sf"], p["sq"]
        df_ref[0] = flat(dfv * (1.0 - lbv) * sf * (1.0 - sf)).astype(df_ref.dtype)
        dlb = jnp.sum(flat(dfv * (1.0 - sf)), axis=0, keepdims=True)
        dq_ref[0] = flat(dq * (sq * (1.0 + hq * (1.0 - sq)))).astype(dq_ref.dtype)
        dl0 = dlb * lbv * (1.0 - lbv)
        _acc(dlb_ref, jnp.concatenate([dl0, -dl0], axis=0), b == 0)
        _acc(dgn_ref, dgn, jnp.logical_and(b == 0, h == 0))

    sds = jax.ShapeDtypeStruct((b_, t_, hw), MXU_DTYPE)
    return pl.pallas_call(
        body, name="hgrn_bwd", grid=(HG_HEADS, b_),
        in_specs=[col(0), col(1), col(2), col(3), blk, vec, one],
        out_specs=[blk, blk, blk, blk, vec, one],
        out_shape=[sds, sds, sds, sds, jax.ShapeDtypeStruct((2, hw), F32), jax.ShapeDtypeStruct((1, LANE), F32)],
        compiler_params=_params(2),
    )(zm, zm, zm, zm, dy, lb, gn)


def _fox_logf(x):
    return jnp.minimum(x, 0.0) - jnp.log(1.0 + jnp.exp(-jnp.abs(x)))


def _fox_prep(zf, bias):
    b_, t_, _ = zf.shape
    tb = min(FOX_BLOCK, t_)
    nb = t_ // tb

    def body(z_ref, b_ref, fc_ref):
        tril_bf = (lax.broadcasted_iota(jnp.int32, (tb, tb), 0) >= lax.broadcasted_iota(jnp.int32, (tb, tb), 1)).astype(BF16)
        bv = b_ref[...]

        def blk(i, carry):
            rows = pl.ds(pl.multiple_of(i * tb, tb), tb)
            fc = _tri_dot(tril_bf, _fox_logf(z_ref[0, rows, :] + bv)) + carry
            fc_ref[0, rows, :] = fc
            return fc[tb - 1:tb, :]

        lax.fori_loop(0, nb, blk, jnp.zeros((1, LANE), F32))

    blk_spec = pl.BlockSpec((1, t_, LANE), lambda b: (b, 0, 0))
    return pl.pallas_call(
        body, name="fox_prep", grid=(b_,),
        in_specs=[blk_spec, pl.BlockSpec((1, LANE), lambda b: (0, 0))], out_specs=blk_spec,
        out_shape=jax.ShapeDtypeStruct((b_, t_, LANE), F32), compiler_params=_params(1),
    )(zf, bias)


def _fox_post(dfc, zf, bias):
    b_, t_, _ = zf.shape
    tb = min(FOX_BLOCK, t_)
    nb = t_ // tb

    def body(d_ref, z_ref, b_ref, dz_ref, db_ref):
        triu_bf = (lax.broadcasted_iota(jnp.int32, (tb, tb), 0) <= lax.broadcasted_iota(jnp.int32, (tb, tb), 1)).astype(BF16)
        valid = lax.broadcasted_iota(jnp.int32, (tb, LANE), 1) < FOX_HEADS
        bv = b_ref[...]

        def blk(m, carry):
            tail, db = carry
            rows = pl.ds(pl.multiple_of((nb - 1 - m) * tb, tb), tb)
            dlf = _tri_dot(triu_bf, d_ref[0, rows, :]) + tail
            dx = jnp.where(valid, dlf * _sig(-(z_ref[0, rows, :] + bv)), 0.0)
            dz_ref[0, rows, :] = dx.astype(dz_ref.dtype)
            return dlf[0:1, :], db + jnp.sum(dx, axis=0, keepdims=True)

        z1 = jnp.zeros((1, LANE), F32)
        _, db = lax.fori_loop(0, nb, blk, (z1, z1))
        _acc(db_ref, db, pl.program_id(0) == 0)

    blk_spec = pl.BlockSpec((1, t_, LANE), lambda b: (b, 0, 0))
    vec = pl.BlockSpec((1, LANE), lambda b: (0, 0))
    return pl.pallas_call(
        body, name="fox_post", grid=(b_,), in_specs=[blk_spec, blk_spec, vec], out_specs=[blk_spec, vec],
        out_shape=[jax.ShapeDtypeStruct((b_, t_, LANE), MXU_DTYPE), jax.ShapeDtypeStruct((1, LANE), F32)],
        compiler_params=_params(1),
    )(dfc, zf, bias)


FOX_TILE = 128
FOX_BAND = 512
AUG = 64


def _head_mean_matrix():
    r = lax.broadcasted_iota(jnp.int32, (LANE, LANE), 0) // FOX_DH
    c = lax.broadcasted_iota(jnp.int32, (LANE, LANE), 1) // FOX_DH
    return (r == c).astype(BF16)


def _dot_right_exact(x, m_bf):
    hi = x.astype(BF16)
    r = x - hi.astype(F32)
    mid = r.astype(BF16)
    lo = (r - mid.astype(F32)).astype(BF16)

    def d(v):
        return lax.dot_general(v, m_bf, (((1,), (0,)), ((), ())), preferred_element_type=F32)

    return d(hi) + d(mid) + d(lo)


def _pair_norm(x, g2, bd):
    r = lax.rsqrt(_dot_right_exact(x * x, bd) * (1.0 / FOX_DH) + EPS)
    return x * r * g2, r


def _pair_norm_bwd(x, r, dy, g2, bd):
    dyg = dy * g2
    dx = r * dyg - x * (r * r * r) * (_dot_right_exact(dyg * x, bd) * (1.0 / FOX_DH))
    return dx, jnp.sum(dy * x * r, axis=0, keepdims=True)


def _head_lanes(xn, hh):
    return xn if hh == 0 else pltpu.roll(xn, FOX_DH, 1)


def _split3(x):
    hi = x.astype(BF16).astype(F32)
    mid = (x - hi).astype(BF16).astype(F32)
    return hi, mid, x - hi - mid


def _fox_operands(q_ref, k_ref, v_ref, fc_ref, gq2, gk2, p, qa, ka, va):
    t_ = q_ref.shape[1]
    bd = _head_mean_matrix()
    lane = lax.broadcasted_iota(jnp.int32, (t_, LANE), 1)
    qx, kx = q_ref[0], k_ref[0]
    qn, rq = _pair_norm(qx, gq2, bd)
    kn, rk = _pair_norm(kx, gk2, bd)
    vv = v_ref[0]
    q_aug = jnp.where(jnp.logical_and(lane >= AUG, lane < AUG + 3), 1.0, 0.0)
    for hh in range(2):
        fcol = jnp.sum(jnp.where(lane == 2 * p + hh, fc_ref[0], 0.0), axis=-1, keepdims=True)
        hi, mid, lo = _split3(-fcol)
        k_aug = jnp.where(lane == AUG, hi, jnp.where(lane == AUG + 1, mid, jnp.where(lane == AUG + 2, lo,
                          jnp.where(lane == AUG + 3, 1.0, 0.0))))
        head = lane < FOX_DH
        qa[hh] = jnp.where(head, _head_lanes(qn, hh), q_aug).astype(MXU_DTYPE)
        ka[hh] = jnp.where(head, _head_lanes(kn, hh), k_aug).astype(MXU_DTYPE)
        va[hh] = jnp.where(head, _head_lanes(vv, hh), 0.0).astype(MXU_DTYPE)
    return bd, lane, qx, kx, rq, rk


def _fox_specs(t_, fw, col0):
    npair = fw // LANE

    def col(off):
        return pl.BlockSpec((1, t_, LANE), lambda b, p: (b, 0, col0 + off * npair + p))

    pair = pl.BlockSpec((1, t_, LANE), lambda b, p: (b, 0, p))
    full = pl.BlockSpec((1, t_, LANE), lambda b, p: (b, 0, 0))
    gvec = pl.BlockSpec((1, LANE), lambda b, p: (0, 0))
    lse = pl.BlockSpec((1, 1, t_, LANE), lambda b, p: (b, p, 0, 0))
    return col, pair, full, gvec, lse


def _fox_fwd(zm, fc, gq2, gk2, fw, col0):
    b_, t_, _ = zm.shape
    npair = fw // LANE
    tq = min(FOX_TILE, t_)
    bw = min(FOX_BAND, t_)
    nband, tpb = t_ // bw, bw // tq
    scale = FOX_DH ** -0.5
    col, pair, full, gvec, lse_spec = _fox_specs(t_, fw, col0)

    def body(q_ref, k_ref, v_ref, fc_ref, gq_ref, gk_ref, o_ref, lse_ref, qa, ka, va):
        p = pl.program_id(1)
        _fox_operands(q_ref, k_ref, v_ref, fc_ref, gq_ref[...] * scale, gk_ref[...], p, qa, ka, va)
        ri = lax.broadcasted_iota(jnp.int32, (tq, bw), 0)
        ci = lax.broadcasted_iota(jnp.int32, (tq, bw), 1)
        lane = lax.broadcasted_iota(jnp.int32, (tq, LANE), 1)

        for band in range(nband):
            c0 = band * bw

            def qtile(ii, _, c0=c0):
                r0 = pl.multiple_of(c0 + ii * tq, tq)
                rows = pl.ds(r0, tq)
                keep = c0 + ci <= r0 + ri
                res = []
                for hh in range(2):
                    qb = qa[hh, rows, :]
                    s_b = jnp.where(keep, _nt(qb, ka[hh, c0:c0 + bw, :]), NEG)
                    m = jnp.max(s_b, axis=-1, keepdims=True)
                    if c0:
                        s_a = _nt(qb, ka[hh, 0:c0, :])
                        m = jnp.maximum(m, jnp.max(s_a, axis=-1, keepdims=True))
                    p_b = jnp.exp(s_b - m)
                    l = jnp.sum(p_b, axis=-1, keepdims=True)
                    acc = _nn(p_b, va[hh, c0:c0 + bw, :])
                    if c0:
                        p_a = jnp.exp(s_a - m)
                        l = l + jnp.sum(p_a, axis=-1, keepdims=True)
                        acc = acc + _nn(p_a, va[hh, 0:c0, :])
                    res.append((acc / l, m + jnp.log(l)))
                (o0, e0), (o1, e1) = res
                o_ref[0, rows, :] = jnp.where(lane < FOX_DH, o0, pltpu.roll(o1, FOX_DH, 1))
                lse_ref[0, 0, rows, :] = jnp.where(lane == 0, e0, jnp.where(lane == 1, e1, 0.0))
                return 0

            lax.fori_loop(0, tpb, qtile, 0)

    return pl.pallas_call(
        body, name="fox_fwd", grid=(b_, npair),
        in_specs=[col(0), col(1), col(2), full, gvec, gvec],
        out_specs=[pair, lse_spec],
        out_shape=[jax.ShapeDtypeStruct((b_, t_, fw), F32), jax.ShapeDtypeStruct((b_, npair, t_, LANE), F32)],
        scratch_shapes=[pltpu.VMEM((2, t_, LANE), MXU_DTYPE)] * 3,
        compiler_params=_params(2),
    )(zm, zm, zm, fc, gq2, gk2)


def _norm_bwd(x, dy, g):
    r = lax.rsqrt(jnp.mean(x * x, axis=-1, keepdims=True) + EPS)
    dyg = dy * g
    dx = r * dyg - x * (r * r * r) * jnp.mean(dyg * x, axis=-1, keepdims=True)
    return dx, jnp.sum(dy * x * r, axis=0, keepdims=True)


def _fox_bwd(zm, o, do, lse, fc, gq2, gk2, fw, col0):
    b_, t_, _ = zm.shape
    npair = fw // LANE
    tq = min(FOX_TILE, t_)
    nb = t_ // tq
    bw = min(FOX_BAND, t_)
    nband, tpb = t_ // bw, bw // tq
    scale = FOX_DH ** -0.5
    col, pair, full, gvec, lse_spec = _fox_specs(t_, fw, col0)

    def body(q_ref, k_ref, v_ref, o_ref, do_ref, lse_ref, fc_ref, gq_ref, gk_ref,
             dq_ref, dk_ref, dv_ref, dfc_ref, dgq_ref, dgk_ref, qa, ka, va, da, rowv, dq_acc, dk_acc, dv_acc):
        b, p = pl.program_id(0), pl.program_id(1)
        gq2v, gk2v = gq_ref[...] * scale, gk_ref[...]
        bd, lane, qx, kx, rq, rk = _fox_operands(q_ref, k_ref, v_ref, fc_ref, gq2v, gk2v, p, qa, ka, va)
        head = lane < FOX_DH
        dov = do_ref[0]
        dsum = _dot_right_exact(dov * o_ref[0], bd)
        eye = (lax.broadcasted_iota(jnp.int32, (tq, tq), 0) == lax.broadcasted_iota(jnp.int32, (tq, tq), 1)).astype(F32)
        for hh in range(2):
            da[hh] = jnp.where(head, _head_lanes(dov, hh), 0.0).astype(MXU_DTYPE)
            for blk in range(nb):
                rs = slice(blk * tq, (blk + 1) * tq)
                rowv[2 * hh:2 * hh + 1, rs] = jnp.sum(eye * lse_ref[0, 0, rs, hh:hh + 1], axis=0, keepdims=True)
                rowv[2 * hh + 1:2 * hh + 2, rs] = jnp.sum(eye * dsum[rs, hh * FOX_DH:hh * FOX_DH + 1], axis=0, keepdims=True)
        dq_acc[...] = jnp.zeros(dq_acc.shape, F32)
        ri = lax.broadcasted_iota(jnp.int32, (tq, bw), 0)
        ci = lax.broadcasted_iota(jnp.int32, (tq, bw), 1)

        def part(hh, kb, vb, lo, hi, keep):
            qm, dm = qa[hh, lo:hi, :], da[hh, lo:hi, :]
            pt = jnp.exp(_nt(kb, qm) - rowv[2 * hh:2 * hh + 1, lo:hi])
            if keep is not None:
                pt = jnp.where(keep, pt, 0.0)
            dst = pt * (_nt(vb, dm) - rowv[2 * hh + 1:2 * hh + 2, lo:hi])
            dq_acc[hh, lo:hi, :] += _tn(dst, kb)
            return _nn(dst, qm), _nn(pt, dm)

        for band in range(nband):
            c0 = band * bw

            def kvtile(jj, _, c0=c0):
                r0 = pl.multiple_of(c0 + jj * tq, tq)
                rows = pl.ds(r0, tq)
                keep = c0 + ci >= r0 + ri
                for hh in range(2):
                    kb, vb = ka[hh, rows, :], va[hh, rows, :]
                    dk_t, dv_t = part(hh, kb, vb, c0, c0 + bw, keep)
                    if c0 + bw < t_:
                        dk_u, dv_u = part(hh, kb, vb, c0 + bw, t_, None)
                        dk_t, dv_t = dk_t + dk_u, dv_t + dv_u
                    dk_acc[hh, rows, :] = dk_t
                    dv_acc[hh, rows, :] = dv_t
                return 0

            lax.fori_loop(0, tpb, kvtile, 0)

        dq0, dq1, dk0, dk1 = dq_acc[0], dq_acc[1], dk_acc[0], dk_acc[1]
        dqn = jnp.where(head, dq0, pltpu.roll(dq1, FOX_DH, 1))
        dkn = jnp.where(head, dk0, pltpu.roll(dk1, FOX_DH, 1))
        dqx, gq_part = _pair_norm_bwd(qx, rq, dqn, gq2v, bd)
        dkx, gk_part = _pair_norm_bwd(kx, rk, dkn, gk2v, bd)
        dq_ref[0] = dqx.astype(dq_ref.dtype)
        dk_ref[0] = dkx.astype(dk_ref.dtype)
        dv_ref[0] = jnp.where(head, dv_acc[0], pltpu.roll(dv_acc[1], FOX_DH, 1)).astype(dv_ref.dtype)

        def bias_grad(dqh, dkh):
            return (jnp.sum(jnp.where(lane == AUG + 3, dqh, 0.0), axis=-1, keepdims=True)
                    - jnp.sum(jnp.where(lane == AUG, dkh, 0.0), axis=-1, keepdims=True))

        dfc_ref[0, 0] = jnp.where(lane == 0, bias_grad(dq0, dk0), jnp.where(lane == 1, bias_grad(dq1, dk1), 0.0))
        first = jnp.logical_and(b == 0, p == 0)
        _acc(dgq_ref, gq_part * scale, first)
        _acc(dgk_ref, gk_part, first)

    sds = jax.ShapeDtypeStruct((b_, t_, fw), MXU_DTYPE)
    gs = jax.ShapeDtypeStruct((1, LANE), F32)
    return pl.pallas_call(
        body, name="fox_bwd", grid=(b_, npair),
        in_specs=[col(0), col(1), col(2), pair, pair, lse_spec, full, gvec, gvec],
        out_specs=[pair, pair, pair, lse_spec, gvec, gvec],
        out_shape=[sds, sds, sds, jax.ShapeDtypeStruct((b_, npair, t_, LANE), F32), gs, gs],
        scratch_shapes=[pltpu.VMEM((2, t_, LANE), MXU_DTYPE)] * 4
        + [pltpu.VMEM((8, t_), F32)] + [pltpu.VMEM((2, t_, LANE), F32)] * 3,
        compiler_params=_params(2),
    )(zm, zm, zm, o, do, lse, fc, gq2, gk2)


def _mem_specs(t_, m_, mw, col0):
    nh = mw // LANE
    qcol = pl.BlockSpec((1, t_, LANE), lambda b, h: (b, 0, col0 + h))
    kcol = pl.BlockSpec((1, m_, LANE), lambda b, h: (b, 0, h))
    vcol = pl.BlockSpec((1, m_, LANE), lambda b, h: (b, 0, nh + h))
    ycol = pl.BlockSpec((1, t_, LANE), lambda b, h: (b, 0, h))
    gvec = pl.BlockSpec((1, LANE), lambda b, h: (0, 0))
    return qcol, kcol, vcol, ycol, gvec


def _mem_fwd(zm, mkv, gq, gk, mw, col0):
    b_, t_, _ = zm.shape
    m_ = mkv.shape[1]
    tq = min(512, t_)
    nb = t_ // tq
    scale = MEM_DH ** -0.5
    qcol, kcol, vcol, ycol, gvec = _mem_specs(t_, m_, mw, col0)

    def body(q_ref, k_ref, v_ref, gq_ref, gk_ref, y_ref):
        gqv, gkv = gq_ref[...] * scale, gk_ref[...]
        kv = k_ref[0]
        kn = _mx(kv * lax.rsqrt(jnp.mean(kv * kv, axis=-1, keepdims=True) + EPS) * gkv)
        vv = _mx(v_ref[0])

        def blk(i, _):
            rows = pl.ds(pl.multiple_of(i * tq, tq), tq)
            qv = q_ref[0, rows, :]
            s = _nt(qv * lax.rsqrt(jnp.mean(qv * qv, axis=-1, keepdims=True) + EPS) * gqv, kn)
            e = jnp.exp(s - jnp.max(s, axis=-1, keepdims=True))
            y_ref[0, rows, :] = _nn(e / jnp.sum(e, axis=-1, keepdims=True), vv)
            return 0

        lax.fori_loop(0, nb, blk, 0)

    return pl.pallas_call(
        body, name="mem_fwd", grid=(b_, MEM_HEADS), in_specs=[qcol, kcol, vcol, gvec, gvec], out_specs=ycol,
        out_shape=jax.ShapeDtypeStruct((b_, t_, mw), F32), compiler_params=_params(2),
    )(zm, mkv, mkv, gq, gk)


def _mem_bwd(zm, mkv, dy, gq, gk, mw, col0):
    b_, t_, _ = zm.shape
    m_ = mkv.shape[1]
    tq = min(512, t_)
    nb = t_ // tq
    scale = MEM_DH ** -0.5
    qcol, kcol, vcol, ycol, gvec = _mem_specs(t_, m_, mw, col0)

    def body(q_ref, k_ref, v_ref, dy_ref, gq_ref, gk_ref, dq_ref, dk_ref, dv_ref, dgq_ref, dgk_ref):
        gqv, gkv = gq_ref[...] * scale, gk_ref[...]
        kv = k_ref[0]
        kn = _mx(kv * lax.rsqrt(jnp.mean(kv * kv, axis=-1, keepdims=True) + EPS) * gkv)
        vv = _mx(v_ref[0])

        def blk(i, carry):
            dkn, dvv, dgq = carry
            rows = pl.ds(pl.multiple_of(i * tq, tq), tq)
            qv = q_ref[0, rows, :]
            qn = _mx(qv * lax.rsqrt(jnp.mean(qv * qv, axis=-1, keepdims=True) + EPS) * gqv)
            s = _nt(qn, kn)
            e = jnp.exp(s - jnp.max(s, axis=-1, keepdims=True))
            pm = e / jnp.sum(e, axis=-1, keepdims=True)
            dob = _mx(dy_ref[0, rows, :])
            dp = _nt(dob, vv)
            ds = pm * (dp - jnp.sum(dp * pm, axis=-1, keepdims=True))
            dqv, gq_part = _norm_bwd(qv, _nn(ds, kn), gqv)
            dq_ref[0, rows, :] = dqv.astype(dq_ref.dtype)
            return dkn + _tn(ds, qn), dvv + _tn(pm, dob), dgq + gq_part * scale

        z = jnp.zeros((m_, LANE), F32)
        dkn, dvv, dgq = lax.fori_loop(0, nb, blk, (z, z, jnp.zeros((1, LANE), F32)))
        dkv, dgk = _norm_bwd(kv, dkn, gkv)
        dk_ref[0] = dkv
        dv_ref[0] = dvv
        first = jnp.logical_and(pl.program_id(0) == 0, pl.program_id(1) == 0)
        _acc(dgq_ref, dgq, first)
        _acc(dgk_ref, dgk, first)

    kblk = pl.BlockSpec((1, m_, LANE), lambda b, h: (b, 0, h))
    gs = jax.ShapeDtypeStruct((1, LANE), F32)
    ks = jax.ShapeDtypeStruct((b_, m_, mw), F32)
    return pl.pallas_call(
        body, name="mem_bwd", grid=(b_, MEM_HEADS), in_specs=[qcol, kcol, vcol, ycol, gvec, gvec],
        out_specs=[ycol, kblk, kblk, gvec, gvec],
        out_shape=[jax.ShapeDtypeStruct((b_, t_, mw), MXU_DTYPE), ks, ks, gs, gs], compiler_params=_params(2),
    )(zm, mkv, mkv, dy, gq, gk)


def _merge_specs(tm, d, w, gcol):
    row_d = pl.BlockSpec((tm, d), lambda i: (i, 0))
    row_w = pl.BlockSpec((tm, w), lambda i: (i, 0))
    gates = [pl.BlockSpec((tm, d), functools.partial(lambda i, k: (i, gcol + k), k=k)) for k in range(3)]
    w_br = pl.BlockSpec((w, d), lambda i: (0, 0))
    w_o = pl.BlockSpec((d, d), lambda i: (0, 0))
    return row_d, row_w, gates, w_br, w_o


def _merge_fwd(x, ys, zm, w_brs, w_out, gcol, tm=256):
    n, d = x.shape
    w = ys[0].shape[1]
    tm = _tile(n, tm, 8)
    row_d, row_w, gates, w_br, w_o = _merge_specs(tm, d, w, gcol)

    def body(x_ref, ya, yb, yc, g0, g1, g2, wa, wb, wc, wo, x1_ref, mg_ref):
        mg = (_sig(g0[...]) * _nn(ya[...], wa[...]) + _sig(g1[...]) * _nn(yb[...], wb[...])
              + _sig(g2[...]) * _nn(yc[...], wc[...]))
        mg_ref[...] = mg.astype(mg_ref.dtype)
        x1_ref[...] = x_ref[...] + _nn(mg, wo[...])

    return pl.pallas_call(
        body, name="merge_fwd", grid=(n // tm,),
        in_specs=[row_d, row_w, row_w, row_w] + gates + [w_br, w_br, w_br, w_o],
        out_specs=[row_d, row_d],
        out_shape=[jax.ShapeDtypeStruct((n, d), F32), jax.ShapeDtypeStruct((n, d), MXU_DTYPE)],
        compiler_params=_params(1),
    )(x, *ys, zm, zm, zm, *w_brs, w_out)


def _merge_bwd(dx1, ys, zm, w_brs, w_out, gcol, tm=256):
    n, d = dx1.shape
    w = ys[0].shape[1]
    tm = _tile(n, tm, 8)
    row_d, row_w, gates, w_br, w_o = _merge_specs(tm, d, w, gcol)

    def body(dx_ref, ya, yb, yc, g0, g1, g2, wa, wb, wc, wo, dgl_ref, dpa, dpb, dpc, dya, dyb, dyc):
        dm = _nt(dx_ref[...], wo[...])
        for k, (y, g, wr, dp_ref, dy_ref) in enumerate(((ya, g0, wa, dpa, dya), (yb, g1, wb, dpb, dyb),
                                                        (yc, g2, wc, dpc, dyc))):
            sg = _sig(g[...])
            pr = _nn(y[...], wr[...])
            dgl_ref[:, k * d:(k + 1) * d] = (dm * pr * sg * (1.0 - sg)).astype(dgl_ref.dtype)
            dp = (dm * sg).astype(dp_ref.dtype)
            dp_ref[...] = dp
            dy_ref[...] = _nt(dp, wr[...])

    sd = jax.ShapeDtypeStruct((n, d), MXU_DTYPE)
    sw = jax.ShapeDtypeStruct((n, w), F32)
    return pl.pallas_call(
        body, name="merge_bwd", grid=(n // tm,),
        in_specs=[row_d, row_w, row_w, row_w] + gates + [w_br, w_br, w_br, w_o],
        out_specs=[pl.BlockSpec((tm, 3 * d), lambda i: (i, 0)), row_d, row_d, row_d, row_w, row_w, row_w],
        out_shape=[jax.ShapeDtypeStruct((n, 3 * d), MXU_DTYPE), sd, sd, sd, sw, sw, sw],
        compiler_params=_params(1),
    )(dx1, *ys, zm, zm, zm, *w_brs, w_out)


CONV_ROWS = 256
HALO = 8


def _ext(ref, r0, t_):
    rc = min(CONV_ROWS, t_)
    a, b = max(r0 - HALO, 0), min(r0 + rc + HALO, t_)
    parts = []
    if r0 - HALO < 0:
        parts.append(jnp.zeros((HALO, ref.shape[2]), F32))
    parts.append(ref[0, a:b, :].astype(F32))
    if r0 + rc + HALO > t_:
        parts.append(jnp.zeros((HALO, ref.shape[2]), F32))
    return jnp.concatenate(parts, axis=0) if len(parts) > 1 else parts[0]


def _gelu_parts(ac):
    cdf = 0.5 * (1.0 + _erf(ac * (2.0 ** -0.5)))
    pdf = jnp.exp(-0.5 * ac * ac) * ((2.0 * math.pi) ** -0.5)
    return cdf, pdf


def _conv_taps(a_ext, cw, cb):
    return cw[0:1, :] * pltpu.roll(a_ext, 2, 0) + cw[1:2, :] * pltpu.roll(a_ext, 1, 0) + cw[2:3, :] * a_ext + cb


def _glu_specs(t_, f, g):
    gate = pl.BlockSpec((1, t_, g), lambda j, b: (b, 0, j))
    value = pl.BlockSpec((1, t_, g), lambda j, b: (b, 0, f // g + j))
    cwb = pl.BlockSpec((3, g), lambda j, b: (0, j))
    cbb = pl.BlockSpec((1, g), lambda j, b: (0, j))
    return gate, value, cwb, cbb


def _glu_fwd(u, cw, cb):
    b_, t_, f2 = u.shape
    f = f2 // 2
    g = min(FFN_GROUP, f)
    rc = min(CONV_ROWS, t_)
    gate, value, cwb, cbb = _glu_specs(t_, f, g)

    def body(a_ref, v_ref, cw_ref, cb_ref, y_ref):
        cwv, cbv = cw_ref[...], cb_ref[...]
        for r0 in range(0, t_, rc):
            ac = _conv_taps(_ext(a_ref, r0, t_), cwv, cbv)[HALO:HALO + rc]
            cdf, _ = _gelu_parts(ac)
            y_ref[0, r0:r0 + rc, :] = (ac * cdf * v_ref[0, r0:r0 + rc, :]).astype(y_ref.dtype)

    return pl.pallas_call(
        body, name="glu_fwd", grid=(f // g, b_), in_specs=[gate, value, cwb, cbb], out_specs=gate,
        out_shape=jax.ShapeDtypeStruct((b_, t_, f), MXU_DTYPE), compiler_params=_params(2),
    )(u, u, cw, cb)


def _glu_bwd(u, dy, cw, cb):
    b_, t_, f2 = u.shape
    f = f2 // 2
    g = min(FFN_GROUP, f)
    rc = min(CONV_ROWS, t_)
    ne = rc + 2 * HALO
    gate, value, cwb, cbb = _glu_specs(t_, f, g)

    def body(a_ref, v_ref, dy_ref, cw_ref, cb_ref, da_ref, dv_ref, dcw_ref, dcb_ref):
        cwv, cbv = cw_ref[...], cb_ref[...]
        dcw = [jnp.zeros((1, g), F32) for _ in range(3)]
        dcb = jnp.zeros((1, g), F32)
        for r0 in range(0, t_, rc):
            a_ext, v_ext, dy_ext = _ext(a_ref, r0, t_), _ext(v_ref, r0, t_), _ext(dy_ref, r0, t_)
            ac = _conv_taps(a_ext, cwv, cbv)
            cdf, pdf = _gelu_parts(ac)
            dac = dy_ext * v_ext * (cdf + ac * pdf)
            da = cwv[2:3, :] * dac + cwv[1:2, :] * pltpu.roll(dac, ne - 1, 0) + cwv[0:1, :] * pltpu.roll(dac, ne - 2, 0)
            mid = slice(HALO, HALO + rc)
            da_ref[0, r0:r0 + rc, :] = da[mid].astype(da_ref.dtype)
            dv_ref[0, r0:r0 + rc, :] = (dy_ext[mid] * ac[mid] * cdf[mid]).astype(dv_ref.dtype)
            dacm = dac[mid]
            dcw[0] = dcw[0] + jnp.sum(dacm * pltpu.roll(a_ext, 2, 0)[mid], axis=0, keepdims=True)
            dcw[1] = dcw[1] + jnp.sum(dacm * pltpu.roll(a_ext, 1, 0)[mid], axis=0, keepdims=True)
            dcw[2] = dcw[2] + jnp.sum(dacm * a_ext[mid], axis=0, keepdims=True)
            dcb = dcb + jnp.sum(dacm, axis=0, keepdims=True)
        first = pl.program_id(1) == 0
        _acc(dcw_ref, jnp.concatenate(dcw, axis=0), first)
        _acc(dcb_ref, dcb, first)

    sds = jax.ShapeDtypeStruct((b_, t_, f), MXU_DTYPE)
    return pl.pallas_call(
        body, name="glu_bwd", grid=(f // g, b_), in_specs=[gate, value, gate, cwb, cbb],
        out_specs=[gate, gate, cwb, cbb],
        out_shape=[sds, sds, jax.ShapeDtypeStruct((3, f), F32), jax.ShapeDtypeStruct((1, f), F32)],
        compiler_params=_params(2),
    )(u, u, dy, cw, cb)


def _loss_head(x1, ffn, target, tm=512):
    n, d = x1.shape
    tm = _tile(n, tm, 8)

    def body(x_ref, f_ref, t_ref, dy_ref, l_ref):
        err = x_ref[...] + f_ref[...] - t_ref[...]
        dy_ref[...] = err * (1.0 / d)
        _acc(l_ref, jnp.sum(err * err, axis=0, keepdims=True) * (0.5 / d), pl.program_id(0) == 0)

    row = pl.BlockSpec((tm, d), lambda i: (i, 0))
    vec = pl.BlockSpec((1, d), lambda i: (0, 0))
    return pl.pallas_call(
        body, name="loss_head", grid=(n // tm,), in_specs=[row, row, row], out_specs=[row, vec],
        out_shape=[jax.ShapeDtypeStruct((n, d), F32), jax.ShapeDtypeStruct((1, d), F32)], compiler_params=_params(1),
    )(x1, ffn, target)


def _place():
    x, y, c = lax.axis_index("x"), lax.axis_index("y"), lax.axis_index("c")
    chips = [(1 - x, y), (x, 1 - y), (1 - x, 1 - y)]
    return x, y, c, chips


def _remote(src, dst, send_sem, recv_sem, to):
    return pltpu.make_async_remote_copy(src_ref=src, dst_ref=dst, send_sem=send_sem, recv_sem=recv_sem,
                                        device_id=to, device_id_type=MESH)


STACK, COLS = "stack", "cols"


def _shard_ref(ref, kind, s, rows, c):
    if kind == COLS:
        cols = pl.ds(pl.multiple_of(s * c, LANE), c)
        return ref.at[:, cols] if rows is None else ref.at[rows, cols]
    return ref.at[s] if rows is None else ref.at[s, rows, :]


def _halves(c, half):
    mine = pl.ds(pl.multiple_of(c * half, 16), half)
    theirs = pl.ds(pl.multiple_of((1 - c) * half, 16), half)
    return mine, theirs


def _gather_shards(shards, kinds):
    nw = len(shards)

    def body(*refs):
        ins, outs = refs[:nw], refs[nw:2 * nw]
        send_sems, recv_sems = refs[2 * nw:]
        x, y, c, chips = _place()
        me, sib = 2 * x + y, (x, y, 1 - c)
        first, passed = [], []
        for i, (w_ref, o_ref, kind) in enumerate(zip(ins, outs, kinds)):
            r, cw = w_ref.shape
            mine, _ = _halves(c, r // 2)
            for j, chip in enumerate(chips):
                first.append(_remote(w_ref.at[mine], _shard_ref(o_ref, kind, me, mine, cw), send_sems.at[6 * i + j],
                                     recv_sems.at[6 * i + j], (*chip, c)))
                first[-1].start()
        for i, (w_ref, o_ref, kind) in enumerate(zip(ins, outs, kinds)):
            r, cw = w_ref.shape
            mine, _ = _halves(c, r // 2)
            for j, (px, py) in enumerate(chips):
                blk = _shard_ref(o_ref, kind, 2 * px + py, mine, cw)
                _remote(blk, blk, send_sems.at[6 * i + j], recv_sems.at[6 * i + j], sib).wait_recv()
                passed.append(_remote(blk, blk, send_sems.at[6 * i + 3 + j], recv_sems.at[6 * i + 3 + j], sib))
                passed[-1].start()
        for i, (w_ref, o_ref, kind) in enumerate(zip(ins, outs, kinds)):
            r, cw = w_ref.shape
            _, theirs = _halves(c, r // 2)
            for j, (px, py) in enumerate(chips):
                blk = _shard_ref(o_ref, kind, 2 * px + py, theirs, cw)
                _remote(blk, blk, send_sems.at[6 * i + 3 + j], recv_sems.at[6 * i + 3 + j], sib).wait_recv()
        for cp in first + passed:
            cp.wait_send()

    def out_sds(a, kind):
        r, c = a.shape
        return jax.ShapeDtypeStruct((r, N_CHIPS * c) if kind == COLS else (N_CHIPS, r, c), a.dtype)

    return pl.pallas_call(
        body, name="gather_shards", in_specs=[ANY] * nw, out_specs=[ANY] * nw,
        out_shape=[out_sds(a, k) for a, k in zip(shards, kinds)],
        scratch_shapes=[pltpu.SemaphoreType.DMA((6 * nw,)), pltpu.SemaphoreType.DMA((6 * nw,))],
    )(*shards)


def _half_shape(g, kind):
    if kind == COLS:
        return (g.shape[0] // 2, g.shape[1])
    return (g.shape[0], g.shape[1] // 2, g.shape[2])


def _pair_swap_halves(gs, kinds):
    nw = len(gs)

    def body(*refs):
        ins, outs = refs[:nw], refs[nw:2 * nw]
        send_sems, recv_sems = refs[2 * nw:]
        x, y, c, _ = _place()
        cps = []
        for i, (g_ref, a_ref, kind) in enumerate(zip(ins, outs, kinds)):
            r = g_ref.shape[0] if kind == COLS else g_ref.shape[1]
            _, theirs = _halves(c, r // 2)
            src = g_ref.at[theirs] if kind == COLS else g_ref.at[:, theirs]
            cps.append(_remote(src, a_ref, send_sems.at[i], recv_sems.at[i], (x, y, 1 - c)))
            cps[-1].start()
        for cp in cps:
            cp.wait()

    return pl.pallas_call(
        body, name="pair_swap_halves", in_specs=[ANY] * nw, out_specs=[ANY] * nw,
        out_shape=[jax.ShapeDtypeStruct(_half_shape(g, k), g.dtype) for g, k in zip(gs, kinds)],
        scratch_shapes=[pltpu.SemaphoreType.DMA((nw,)), pltpu.SemaphoreType.DMA((nw,))],
    )(*gs)


def _row_tile(rows, width, itemsize=4, target=2 ** 21):
    return _tile(rows, max(8, target // (width * itemsize)), 8)


def _add_half(g, a, kind, c_idx, name):
    if kind == COLS:
        half, wd = a.shape
        tr = _row_tile(half, wd)
        nblk = half // tr
        grid = (nblk,)
        g_spec = pl.BlockSpec((tr, wd), lambda i, c_ref: (c_ref[0] * nblk + i, 0))
        a_spec = pl.BlockSpec((tr, wd), lambda i, c_ref: (i, 0))
    else:
        n, half, wd = a.shape
        tr = _row_tile(half, wd)
        nblk = half // tr
        grid = (n, nblk)
        g_spec = pl.BlockSpec((1, tr, wd), lambda s, i, c_ref: (s, c_ref[0] * nblk + i, 0))
        a_spec = pl.BlockSpec((1, tr, wd), lambda s, i, c_ref: (s, i, 0))

    def body(c_ref, g_ref, a_ref, o_ref):
        o_ref[...] = (g_ref[...] + a_ref[...]).astype(o_ref.dtype)

    return pl.pallas_call(
        body, name=name,
        grid_spec=pltpu.PrefetchScalarGridSpec(num_scalar_prefetch=1, grid=grid, in_specs=[g_spec, a_spec],
                                               out_specs=a_spec),
        out_shape=jax.ShapeDtypeStruct(a.shape, EXCHANGE_DTYPE), compiler_params=_params(len(grid)),
    )(c_idx, g, a)


def _chip_exchange(ps, kinds):
    nw = len(ps)

    def shard_shape(p, kind):
        return (p.shape[0], p.shape[1] // N_CHIPS) if kind == COLS else p.shape[1:]

    def body(*refs):
        ins, outs = refs[:nw], refs[nw:2 * nw]
        send_sems, recv_sems = refs[2 * nw:]
        x, y, c, chips = _place()
        me = 2 * x + y
        sent = []
        for i, (p_ref, b_ref, kind) in enumerate(zip(ins, outs, kinds)):
            cw = b_ref.shape[2]
            for j, (px, py) in enumerate(chips):
                sent.append(_remote(_shard_ref(p_ref, kind, 2 * px + py, None, cw), b_ref.at[me],
                                    send_sems.at[3 * i + j], recv_sems.at[3 * i + j], (px, py, c)))
                sent[-1].start()
        for i, b_ref in enumerate(outs):
            for j, (px, py) in enumerate(chips):
                blk = b_ref.at[2 * px + py]
                _remote(blk, blk, send_sems.at[3 * i + j], recv_sems.at[3 * i + j], (px, py, c)).wait_recv()
        for cp in sent:
            cp.wait_send()

    return pl.pallas_call(
        body, name="chip_exchange", in_specs=[ANY] * nw, out_specs=[ANY] * nw,
        out_shape=[jax.ShapeDtypeStruct((N_CHIPS,) + tuple(shard_shape(p, k)), p.dtype) for p, k in zip(ps, kinds)],
        scratch_shapes=[pltpu.SemaphoreType.DMA((3 * nw,)), pltpu.SemaphoreType.DMA((3 * nw,))],
    )(*ps)


def _sum_chips(bq, name):
    n, h, wd = bq.shape
    tr = _row_tile(h, wd * n)

    def body(b_ref, o_ref):
        acc = b_ref[0].astype(F32)
        for s in range(1, n):
            acc = acc + b_ref[s].astype(F32)
        o_ref[...] = acc

    return pl.pallas_call(
        body, name=name, grid=(h // tr,),
        in_specs=[pl.BlockSpec((n, tr, wd), lambda i: (0, i, 0))], out_specs=pl.BlockSpec((tr, wd), lambda i: (i, 0)),
        out_shape=jax.ShapeDtypeStruct((h, wd), F32), compiler_params=_params(1),
    )(bq)


def _pair_join_halves(qs):
    nw = len(qs)

    def body(*refs):
        ins, outs = refs[:nw], refs[nw:2 * nw]
        send_sems, recv_sems = refs[2 * nw:]
        x, y, c, _ = _place()
        sent = []
        for i, (q_ref, o_ref) in enumerate(zip(ins, outs)):
            mine, _ = _halves(c, q_ref.shape[0])
            sent.append(_remote(q_ref, o_ref.at[mine], send_sems.at[i], recv_sems.at[i], (x, y, 1 - c)))
            sent[-1].start()
        for i, (q_ref, o_ref) in enumerate(zip(ins, outs)):
            _, theirs = _halves(c, q_ref.shape[0])
            _remote(q_ref, o_ref.at[theirs], send_sems.at[i], recv_sems.at[i], (x, y, 1 - c)).wait_recv()
        for cp in sent:
            cp.wait_send()

    return pl.pallas_call(
        body, name="pair_join_halves", in_specs=[ANY] * nw, out_specs=[ANY] * nw,
        out_shape=[jax.ShapeDtypeStruct((2 * q.shape[0], q.shape[1]), q.dtype) for q in qs],
        scratch_shapes=[pltpu.SemaphoreType.DMA((nw,)), pltpu.SemaphoreType.DMA((nw,))],
    )(*qs)


def _all_sum_small(s, name):
    sr, w = s.shape

    def body(s_ref, o_ref, buf, send_sems, recv_sems):
        x, y, c, _ = _place()
        me = 4 * x + 2 * y + c
        buf[me] = s_ref[...]
        peers = []
        for k in range(1, 8):
            px = 1 - x if k & 4 else x
            py = 1 - y if k & 2 else y
            pc = 1 - c if k & 1 else c
            peers.append((px, py, pc))
        sent = [_remote(s_ref, buf.at[me], send_sems.at[k], recv_sems.at[k], peer) for k, peer in enumerate(peers)]
        for cp in sent:
            cp.start()
        for k, (px, py, pc) in enumerate(peers):
            _remote(s_ref, buf.at[4 * px + 2 * py + pc], send_sems.at[k], recv_sems.at[k], (px, py, pc)).wait_recv()
        for cp in sent:
            cp.wait_send()
        acc = buf[0]
        for d in range(1, 8):
            acc = acc + buf[d]
        o_ref[...] = acc

    vm = pl.BlockSpec(memory_space=pltpu.VMEM)
    return pl.pallas_call(
        body, name=name, in_specs=[vm], out_specs=vm, out_shape=jax.ShapeDtypeStruct((sr, w), F32),
        scratch_shapes=[pltpu.VMEM((8, sr, w), F32), pltpu.SemaphoreType.DMA((7,)), pltpu.SemaphoreType.DMA((7,))],
    )(s)


BIG = ("w_in", "mem_kv_w", "w_br_hgrn", "w_br_fox", "w_br_mem", "w_out", "ffn_w_up", "ffn_w_down")
KIND = {"w_in": STACK, "mem_kv_w": STACK, "w_br_hgrn": COLS, "w_br_fox": COLS, "w_br_mem": COLS, "w_out": STACK,
        "ffn_w_up": COLS, "ffn_w_down": STACK}
ROW_SHARDED = ("mem_kv_w", "w_out", "ffn_w_down")


def _put_shard(arr, kind, s, piece):
    if kind == COLS:
        return lax.dynamic_update_slice(arr, piece, (0, s * piece.shape[1]))
    return lax.dynamic_update_slice(arr, piece[None], (s, 0, 0))


def _take_shard(arr, kind, s):
    if kind == COLS:
        return lax.dynamic_slice(arr, (0, s * (arr.shape[1] // N_CHIPS)), (arr.shape[0], arr.shape[1] // N_CHIPS))
    return lax.dynamic_index_in_dim(arr, s, 0, keepdims=False)


def _w_in_pieces(cs, s1, nf):
    out = []
    for s in range(N_CHIPS):
        lo, hi = cs * s, cs * (s + 1)
        for a, b, forget in ((lo, min(hi, s1), False), (max(lo, s1), min(hi, s1 + nf), True), (max(lo, s1 + nf), hi, False)):
            if a < b:
                out.append((s, a - lo, b - lo, forget, a - s1 if forget else (a if a < s1 else a - nf)))
    return out


def _split_w_in(stacked, s1, nf):
    pieces = _w_in_pieces(stacked.shape[2], s1, nf)
    main = [stacked[s, :, a:b] for s, a, b, forget, _ in pieces if not forget]
    ff = [stacked[s, :, a:b] for s, a, b, forget, _ in pieces if forget]
    return jnp.concatenate(main, axis=1), jnp.concatenate(ff, axis=1)


def _join_w_in(g_main, g_ff, s1, nf):
    cs = (g_main.shape[1] + nf) // N_CHIPS
    shards = [[] for _ in range(N_CHIPS)]
    for s, a, b, forget, off in _w_in_pieces(cs, s1, nf):
        shards[s].append((g_ff if forget else g_main)[:, off:off + b - a])
    return jnp.stack([jnp.concatenate(p, axis=1) if len(p) > 1 else p[0] for p in shards])


SMALL = ("norm_mix_g", "norm_mem_g", "norm_ffn_g", "hgrn_lb_logits", "hgrn_norm_g", "fox_f_bias", "fox_q_norm_g",
         "fox_k_norm_g", "mem_q_norm_g", "mem_k_norm_g", "ffn_conv_b")


def _pack_small(vals):
    flats, total = [], 0
    for v in vals:
        flat = v.reshape(-1).astype(F32)
        n = -(-flat.shape[0] // FLAT_W)
        flats.append(jnp.pad(flat, (0, n * FLAT_W - flat.shape[0])))
        total += n
    if -total % 8:
        flats.append(jnp.zeros((-total % 8 * FLAT_W,), F32))
    return jnp.concatenate(flats).reshape(-1, FLAT_W)


def _unpack_small(buf, shapes):
    res, off = [], 0
    for shp in shapes:
        numel = math.prod(shp)
        n = -(-numel // FLAT_W)
        res.append(buf[off:off + n].reshape(-1)[:numel].reshape(shp))
        off += n
    return res


def _pad_lanes(v, width=LANE):
    return jnp.pad(v, ((0, 0), (0, width - v.shape[1])))


WEIGHTS = ("norm_mix_g", "norm_mem_g", "w_in", "hgrn_lb_logits", "hgrn_norm_g", "fox_f_bias", "fox_q_norm_g",
           "fox_k_norm_g", "mem_kv_w", "mem_q_norm_g", "mem_k_norm_g", "w_br_hgrn", "w_br_fox", "w_br_mem", "w_out",
           "norm_ffn_g", "ffn_w_up", "ffn_conv_w", "ffn_conv_b", "ffn_w_down")


def _local_step(x, mem, target, w, full, conv_w):
    b_, t_, d = x.shape
    n = b_ * t_
    hw, fw, mw = HG_HEADS * HG_D, FOX_HEADS * FOX_DH, MEM_HEADS * MEM_DH
    m_ = mem.shape[1]
    f = conv_w.shape[1]
    s1 = 4 * hw + 3 * fw
    fox_col, mem_col, gate_col = 4 * hw // LANE, s1 // LANE, (s1 + mw) // d

    w_main, w_ff = _split_w_in(full["w_in"], s1, FOX_HEADS)
    w_ff = _pad_lanes(w_ff)
    w_up = full["ffn_w_up"]
    w_brs = [full["w_br_hgrn"], full["w_br_fox"], full["w_br_mem"]]
    w_out, w_kv, w_down = full["w_out"], full["mem_kv_w"], full["ffn_w_down"]
    f_bias = _pad_lanes(w["fox_f_bias"])
    cb = w["ffn_conv_b"]

    x2 = x.reshape(n, d)
    h = _rmsnorm_fwd(x2, w["norm_mix_g"], name="norm_mix_fwd")
    zm = _matmul(h, w_main, name="in_proj")
    zf = _matmul(h, w_ff, name="in_proj_forget")
    zm3, zf3 = zm.reshape(b_, t_, -1), zf.reshape(b_, t_, LANE)
    ya = _hgrn_fwd(zm3, w["hgrn_lb_logits"], w["hgrn_norm_g"], hw)
    fc = _fox_prep(zf3, f_bias)
    fox_gq, fox_gk = jnp.tile(w["fox_q_norm_g"], (1, 2)), jnp.tile(w["fox_k_norm_g"], (1, 2))
    yb, lse = _fox_fwd(zm3, fc, fox_gq, fox_gk, fw, fox_col)
    mem2 = mem.reshape(b_ * m_, d)
    hm = _rmsnorm_fwd(mem2, w["norm_mem_g"], name="norm_mem_fwd")
    mkv = _matmul(hm, w_kv, name="mem_kv_proj").reshape(b_, m_, 2 * mw)
    yc = _mem_fwd(zm3, mkv, w["mem_q_norm_g"], w["mem_k_norm_g"], mw, mem_col)
    ys = [ya.reshape(n, hw), yb.reshape(n, fw), yc.reshape(n, mw)]
    x1, merged = _merge_fwd(x2, ys, zm, w_brs, w_out, gate_col)
    h2 = _rmsnorm_fwd(x1, w["norm_ffn_g"], name="norm_ffn_fwd")
    u = _matmul(h2, w_up, name="ffn_up")
    u3 = u.reshape(b_, t_, 2 * f)
    yff = _glu_fwd(u3, conv_w, cb).reshape(n, f)
    ffn = _matmul(yff, w_down, name="ffn_down", tk=1408)
    dy, loss_vec = _loss_head(x1, ffn, target.reshape(n, d))

    grads = {}
    dyff = _matmul(dy, w_down, tb=True, name="ffn_down_dx", tn=1408)
    grads["ffn_w_down"] = _matmul(yff, dy, ta=True, name="ffn_down_dw", tm=1408)
    du_a, du_v, grads["ffn_conv_w"], grads["ffn_conv_b"] = _glu_bwd(u3, dyff.reshape(b_, t_, f), conv_w, cb)
    du2 = jnp.concatenate([du_a, du_v], axis=-1).reshape(n, 2 * f)
    dh2 = _matmul(du2, w_up, tb=True, name="ffn_up_dx")
    grads["ffn_w_up"] = _matmul(h2, du2, ta=True, name="ffn_up_dw")
    dx1, grads["norm_ffn_g"] = _rmsnorm_bwd(x1, [dh2], w["norm_ffn_g"], dy, name="norm_ffn_bwd")

    dgl, dpa, dpb, dpc, dya, dyb, dyc = _merge_bwd(dx1, ys, zm, w_brs, w_out, gate_col)
    grads["w_out"] = _matmul(merged, dx1, ta=True, name="out_proj_dw")
    for nm, y_, dp_ in zip(("w_br_hgrn", "w_br_fox", "w_br_mem"), ys, (dpa, dpb, dpc)):
        grads[nm] = _matmul(y_, dp_, ta=True, name=nm + "_dw")

    dmq, dmk, dmv, grads["mem_q_norm_g"], grads["mem_k_norm_g"] = _mem_bwd(
        zm3, mkv, dyc.reshape(b_, t_, mw), w["mem_q_norm_g"], w["mem_k_norm_g"], mw, mem_col)
    dmkv = jnp.concatenate([dmk, dmv], axis=-1).reshape(b_ * m_, 2 * mw)
    grads["mem_kv_w"] = _matmul(hm, dmkv, ta=True, name="mem_kv_dw")
    dhm = _matmul(dmkv, w_kv, tb=True, name="mem_kv_dx")
    _, grads["norm_mem_g"] = _rmsnorm_bwd(mem2, [dhm], w["norm_mem_g"], None, name="norm_mem_bwd")

    dfq, dfk, dfv, dfc, g_fq, g_fk = _fox_bwd(zm3, yb, dyb.reshape(b_, t_, fw), lse, fc, fox_gq, fox_gk, fw, fox_col)
    grads["fox_q_norm_g"] = g_fq[:, :FOX_DH] + g_fq[:, FOX_DH:]
    grads["fox_k_norm_g"] = g_fk[:, :FOX_DH] + g_fk[:, FOX_DH:]
    dfc = dfc[..., :2].transpose(0, 2, 1, 3).reshape(b_, t_, FOX_HEADS)
    dfc = jnp.pad(dfc, ((0, 0), (0, 0), (0, LANE - FOX_HEADS)))
    dzf, g_fb = _fox_post(dfc, zf3, f_bias)
    grads["fox_f_bias"] = g_fb[:, :FOX_HEADS]

    dhq, dhf, dhi, dhg, grads["hgrn_lb_logits"], grads["hgrn_norm_g"] = _hgrn_bwd(
        zm3, dya.reshape(b_, t_, hw), w["hgrn_lb_logits"], w["hgrn_norm_g"], hw)

    dzm = jnp.concatenate([dhq, dhf, dhi, dhg, dfq, dfk, dfv, dmq, dgl.reshape(b_, t_, 3 * d)], axis=-1).reshape(n, -1)
    dzf2 = dzf.reshape(n, LANE)
    dh_a = _matmul(dzm, w_main, tb=True, name="in_proj_dx")
    dh_b = _matmul(dzf2, w_ff, tb=True, name="in_proj_forget_dx")
    g_main = _matmul(h, dzm, ta=True, name="in_proj_dw")
    g_ff = _matmul(h, dzf2, ta=True, name="in_proj_forget_dw")
    grads["w_in"] = _join_w_in(g_main, g_ff[:, :FOX_HEADS], s1, FOX_HEADS)
    grad_x, grads["norm_mix_g"] = _rmsnorm_bwd(x2, [dh_a, dh_b], w["norm_mix_g"], dx1, name="norm_mix_bwd")
    return loss_vec, grad_x.reshape(b_, t_, d), grads


def kernel(x, mem, norm_mix_g, norm_mem_g, w_in, hgrn_lb_logits, hgrn_norm_g, fox_f_bias, fox_q_norm_g, fox_k_norm_g, mem_kv_w, mem_q_norm_g, mem_k_norm_g, w_br_hgrn, w_br_fox, w_br_mem, w_out, norm_ffn_g, ffn_w_up, ffn_conv_w, ffn_conv_b, ffn_w_down, loss_target, m_norm_mix_g, m_norm_mem_g, m_w_in, m_hgrn_lb_logits, m_hgrn_norm_g, m_fox_f_bias, m_fox_q_norm_g, m_fox_k_norm_g, m_mem_kv_w, m_mem_q_norm_g, m_mem_k_norm_g, m_w_br_hgrn, m_w_br_fox, m_w_br_mem, m_w_out, m_norm_ffn_g, m_ffn_w_up, m_ffn_conv_w, m_ffn_conv_b, m_ffn_w_down, v_norm_mix_g, v_norm_mem_g, v_w_in, v_hgrn_lb_logits, v_hgrn_norm_g, v_fox_f_bias, v_fox_q_norm_g, v_fox_k_norm_g, v_mem_kv_w, v_mem_q_norm_g, v_mem_k_norm_g, v_w_br_hgrn, v_w_br_fox, v_w_br_mem, v_w_out, v_norm_ffn_g, v_ffn_w_up, v_ffn_conv_w, v_ffn_conv_b, v_ffn_w_down):
    w = dict(zip(WEIGHTS, (norm_mix_g, norm_mem_g, w_in, hgrn_lb_logits, hgrn_norm_g, fox_f_bias, fox_q_norm_g,
                           fox_k_norm_g, mem_kv_w, mem_q_norm_g, mem_k_norm_g, w_br_hgrn, w_br_fox, w_br_mem, w_out,
                           norm_ffn_g, ffn_w_up, ffn_conv_w, ffn_conv_b, ffn_w_down)))
    m = dict(zip(WEIGHTS, (m_norm_mix_g, m_norm_mem_g, m_w_in, m_hgrn_lb_logits, m_hgrn_norm_g, m_fox_f_bias,
                           m_fox_q_norm_g, m_fox_k_norm_g, m_mem_kv_w, m_mem_q_norm_g, m_mem_k_norm_g, m_w_br_hgrn,
                           m_w_br_fox, m_w_br_mem, m_w_out, m_norm_ffn_g, m_ffn_w_up, m_ffn_conv_w, m_ffn_conv_b,
                           m_ffn_w_down)))
    v = dict(zip(WEIGHTS, (v_norm_mix_g, v_norm_mem_g, v_w_in, v_hgrn_lb_logits, v_hgrn_norm_g, v_fox_f_bias,
                           v_fox_q_norm_g, v_fox_k_norm_g, v_mem_kv_w, v_mem_q_norm_g, v_mem_k_norm_g, v_w_br_hgrn,
                           v_w_br_fox, v_w_br_mem, v_w_out, v_norm_ffn_g, v_ffn_w_up, v_ffn_conv_w, v_ffn_conv_b,
                           v_ffn_w_down)))
    c_idx = lax.axis_index("c")
    chip = 2 * lax.axis_index("x") + lax.axis_index("y")

    kinds = [KIND[nm] for nm in BIG]
    mine = [w[nm][0].astype(MXU_DTYPE) for nm in BIG]
    full = {nm: _put_shard(g, k, chip, own) for nm, g, k, own in zip(BIG, _gather_shards(mine, kinds), kinds, mine)}
    for nm in ROW_SHARDED:
        full[nm] = full[nm].reshape(-1, full[nm].shape[2])
    cs = ffn_conv_w.shape[2]
    f = cs * N_CHIPS
    placed = lax.dynamic_update_slice(jnp.zeros((3, f), F32), ffn_conv_w[0] * (c_idx == 0).astype(F32), (0, chip * cs))
    conv_w = _unpack_small(_all_sum_small(_pack_small([placed]), "gather_conv_w"), [(3, f)])[0]

    loss_vec, grad_x, grads = _local_step(x, mem, loss_target, w, full, conv_w)

    gs = [grads[nm].reshape(N_CHIPS, -1, grads[nm].shape[1]) if nm in ROW_SHARDED else grads[nm] for nm in BIG]
    from_sibling = _pair_swap_halves(gs, kinds)
    c_arr = jnp.reshape(c_idx, (1,)).astype(jnp.int32)
    chip_partial = [_add_half(g, a, k, c_arr, "add_half_" + nm) for g, a, k, nm in zip(gs, from_sibling, kinds, BIG)]
    landed = [_put_shard(bq, STACK, chip, _take_shard(p, k, chip))
              for bq, p, k in zip(_chip_exchange(chip_partial, kinds), chip_partial, kinds)]
    reduced_half = [_sum_chips(bq, "sum_chips_" + nm) for bq, nm in zip(landed, BIG)]
    joined = [lax.dynamic_update_slice(o, q, (c_idx * q.shape[0], 0))
              for o, q in zip(_pair_join_halves(reduced_half), reduced_half)]
    gshards = dict(zip(BIG, joined))

    small_names = SMALL + ("ffn_conv_w",)
    summed = _unpack_small(
        _all_sum_small(_pack_small([grads[nm] for nm in small_names] + [loss_vec]), "all_sum_small_grads"),
        [grads[nm].shape for nm in small_names] + [loss_vec.shape])
    gsmall = dict(zip(small_names, summed[:-1]))
    loss = jnp.sum(summed[-1])
    g_out = {nm: gshards[nm][None] for nm in BIG}
    for nm in SMALL:
        g_out[nm] = gsmall[nm].reshape(w[nm].shape)
    g_out["ffn_conv_w"] = lax.dynamic_slice(gsmall["ffn_conv_w"], (0, chip * cs), (3, cs))[None]

    delta, new_m, new_v = {}, {}, {}
    for nm in BIG + ("ffn_conv_w",):
        d_, m_, v_ = _adamw(w[nm][0], g_out[nm][0], m[nm][0], v[nm][0], name="adamw_" + nm)
        delta[nm], new_m[nm], new_v[nm] = d_[None], m_[None], v_[None]
    packed = [_pack_small([t[nm] for nm in SMALL]) for t in (w, g_out, m, v)]
    outs = _adamw(*packed, name="adamw_small")
    shapes = [w[nm].shape for nm in SMALL]
    for res, o in zip((delta, new_m, new_v), outs):
        res.update(zip(SMALL, _unpack_small(o, shapes)))

    return (loss, grad_x, *[g_out[nm] for nm in WEIGHTS], *[delta[nm] for nm in WEIGHTS],
            *[new_m[nm] for nm in WEIGHTS], *[new_v[nm] for nm in WEIGHTS])
```

```python
import functools
import math

import jax
import jax.numpy as jnp
from jax import lax
from jax.experimental import pallas as pl
from jax.experimental.pallas import tpu as pltpu

F32 = jnp.float32
BF16 = jnp.bfloat16
MXU_DTYPE = jnp.bfloat16
EXCHANGE_DTYPE = jnp.bfloat16

EPS = 1e-6
HG_HEADS, HG_D = 4, 128
FOX_HEADS, FOX_DH = 8, 64
MEM_HEADS, MEM_DH = 4, 128
HG_CHUNK = 64
FOX_BLOCK = 256
LANE = 128
FFN_GROUP = 256
FLAT_W = 1024
VMEM_LIMIT = 56 * 2 ** 20
NEG = -1e30
N_CHIPS = 4

ADAM_LR, ADAM_B1, ADAM_B2, ADAM_EPS, ADAM_WD, ADAM_STEP = 0.001, 0.9, 0.999, 1e-08, 0.01, 10

MESH = pl.DeviceIdType.MESH
ANY = pl.BlockSpec(memory_space=pl.ANY)


def _mx(x):
    return x.astype(MXU_DTYPE)


def _dot(a, b, ca, cb):
    return lax.dot_general(_mx(a), _mx(b), (((ca,), (cb,)), ((), ())), preferred_element_type=F32)


def _nn(a, b):
    return _dot(a, b, 1, 0)


def _nt(a, b):
    return _dot(a, b, 1, 1)


def _tn(a, b):
    return _dot(a, b, 0, 0)


def _dotp(a, b, ca, cb):
    return lax.dot_general(a, b, (((ca,), (cb,)), ((), ())), precision=lax.Precision.HIGHEST,
                           preferred_element_type=F32)


def _tri_dot(tri_bf, x):
    hi = x.astype(BF16)
    r = x - hi.astype(F32)
    mid = r.astype(BF16)
    lo = (r - mid.astype(F32)).astype(BF16)

    def d(v):
        return lax.dot_general(tri_bf, v, (((1,), (0,)), ((), ())), preferred_element_type=F32)

    return d(hi) + d(mid) + d(lo)


def _sig(x):
    return jax.nn.sigmoid(x)


def _erf(x):
    a = jnp.abs(x)
    t = 1.0 / (1.0 + 0.3275911 * a)
    poly = t * (0.254829592 + t * (-0.284496736 + t * (1.421413741 + t * (-1.453152027 + t * 1.061405429))))
    y = 1.0 - poly * jnp.exp(-a * a)
    return jnp.where(x < 0, -y, y)


def _tile(dim, pref, unit=LANE):
    if dim <= pref:
        return dim
    t = pref - pref % unit
    while t >= unit:
        if dim % t == 0:
            return t
        t -= unit
    return dim


def _params(n_grid):
    return pltpu.CompilerParams(dimension_semantics=("arbitrary",) * n_grid, vmem_limit_bytes=VMEM_LIMIT)


def _acc(ref, val, first):
    @pl.when(first)
    def _():
        ref[...] = val

    @pl.when(jnp.logical_not(first))
    def _():
        ref[...] += val


def _matmul(a, b, *, name, ta=False, tb=False, tm=1024, tn=2048, tk=None):
    m, k = (a.shape[1], a.shape[0]) if ta else a.shape
    n = b.shape[0] if tb else b.shape[1]
    tk = tk or (1024 if ta else 2048)
    tm, tn, tk = _tile(m, tm), _tile(n, tn), _tile(k, tk)
    nk = k // tk

    def body(a_ref, b_ref, o_ref):
        p = _dot(a_ref[...], b_ref[...], 0 if ta else 1, 1 if tb else 0)
        if nk == 1:
            o_ref[...] = p
        else:
            _acc(o_ref, p, pl.program_id(2) == 0)

    a_spec = pl.BlockSpec((tk, tm), lambda i, j, kk: (kk, i)) if ta else pl.BlockSpec((tm, tk), lambda i, j, kk: (i, kk))
    b_spec = pl.BlockSpec((tn, tk), lambda i, j, kk: (j, kk)) if tb else pl.BlockSpec((tk, tn), lambda i, j, kk: (kk, j))
    return pl.pallas_call(
        body, name=name, grid=(m // tm, n // tn, nk),
        in_specs=[a_spec, b_spec],
        out_specs=pl.BlockSpec((tm, tn), lambda i, j, kk: (i, j)),
        out_shape=jax.ShapeDtypeStruct((m, n), F32),
        compiler_params=_params(3),
    )(a, b)


def _rmsnorm_fwd(x, g, *, name, tm=512):
    n, d = x.shape
    tm = _tile(n, tm, 8)

    def body(x_ref, g_ref, o_ref):
        xv = x_ref[...]
        r = lax.rsqrt(jnp.mean(xv * xv, axis=-1, keepdims=True) + EPS)
        o_ref[...] = (xv * r * g_ref[...]).astype(o_ref.dtype)

    return pl.pallas_call(
        body, name=name, grid=(n // tm,),
        in_specs=[pl.BlockSpec((tm, d), lambda i: (i, 0)), pl.BlockSpec((1, d), lambda i: (0, 0))],
        out_specs=pl.BlockSpec((tm, d), lambda i: (i, 0)),
        out_shape=jax.ShapeDtypeStruct((n, d), MXU_DTYPE),
        compiler_params=_params(1),
    )(x, g)


def _rmsnorm_bwd(x, dhs, g, res, *, name, tm=512):
    n, d = x.shape
    tm = _tile(n, tm, 8)
    n_dh = len(dhs)
    has_res = res is not None

    def body(*refs):
        x_ref, dh_refs, g_ref = refs[0], refs[1:1 + n_dh], refs[1 + n_dh]
        res_ref = refs[2 + n_dh] if has_res else None
        dx_ref, dg_ref = refs[-2], refs[-1]
        xv = x_ref[...]
        dh = dh_refs[0][...].astype(F32)
        for r_ in dh_refs[1:]:
            dh = dh + r_[...].astype(F32)
        r = lax.rsqrt(jnp.mean(xv * xv, axis=-1, keepdims=True) + EPS)
        dhg = dh * g_ref[...]
        dx = r * dhg - xv * (r * r * r) * jnp.mean(dhg * xv, axis=-1, keepdims=True)
        if has_res:
            dx = dx + res_ref[...]
        dx_ref[...] = dx
        _acc(dg_ref, jnp.sum(dh * xv * r, axis=0, keepdims=True), pl.program_id(0) == 0)

    row = pl.BlockSpec((tm, d), lambda i: (i, 0))
    vec = pl.BlockSpec((1, d), lambda i: (0, 0))
    ins = [x] + list(dhs) + [g] + ([res] if has_res else [])
    return pl.pallas_call(
        body, name=name, grid=(n // tm,),
        in_specs=[row] * (1 + n_dh) + [vec] + ([row] if has_res else []),
        out_specs=[row, vec],
        out_shape=[jax.ShapeDtypeStruct((n, d), F32), jax.ShapeDtypeStruct((1, d), F32)],
        compiler_params=_params(1),
    )(*ins)


def _adamw(w, g, m, v, *, name, tr=256):
    _, r, c = w.shape
    tr = _tile(r, tr, 8)
    c1 = 1.0 / (1.0 - ADAM_B1 ** ADAM_STEP)
    c2 = 1.0 / (1.0 - ADAM_B2 ** ADAM_STEP)

    def body(w_ref, g_ref, m_ref, v_ref, d_ref, mo_ref, vo_ref):
        gv = g_ref[...]
        mn = ADAM_B1 * m_ref[...] + (1.0 - ADAM_B1) * gv
        vn = ADAM_B2 * v_ref[...] + (1.0 - ADAM_B2) * (gv * gv)
        d_ref[...] = -ADAM_LR * ((mn * c1) / (jnp.sqrt(vn * c2) + ADAM_EPS) + ADAM_WD * w_ref[...])
        mo_ref[...] = mn
        vo_ref[...] = vn

    blk = pl.BlockSpec((1, tr, c), lambda i: (0, i, 0))
    sds = jax.ShapeDtypeStruct((1, r, c), F32)
    return pl.pallas_call(
        body, name=name, grid=(r // tr,), in_specs=[blk] * 4, out_specs=[blk] * 3, out_shape=[sds] * 3,
        compiler_params=_params(1),
    )(w, g, m, v)


def _bdot(a, b, ca, cb):
    return lax.dot_general(_mx(a), _mx(b), (((ca,), (cb,)), ((0,), (0,))), preferred_element_type=F32)


def _bdotp(a, b, ca, cb):
    return lax.dot_general(a, b, (((ca,), (cb,)), ((0,), (0,))), precision=lax.Precision.HIGHEST,
                           preferred_element_type=F32)


def _tri_dot_b(tri_bf, x):
    hi = x.astype(BF16)
    r = x - hi.astype(F32)
    mid = r.astype(BF16)
    lo = (r - mid.astype(F32)).astype(BF16)

    def d(v):
        return lax.dot_general(tri_bf, v, (((2,), (1,)), ((0,), (0,))), preferred_element_type=F32)

    return d(hi) + d(mid) + d(lo)


def _hgrn_forward(hq, hf, hi, lbv, tril, tril_bf):
    nc, c, _ = hq.shape
    sf = _sig(hf)
    f = lbv + (1.0 - lbv) * sf
    k = 1.0 - f
    gcum = _tri_dot_b(tril_bf, jnp.log(f))
    mid = gcum[:, c // 2 - 1:c // 2, :]
    glast = gcum[:, c - 1:c, :]
    sq = _sig(hq)
    q = hq * sq
    e_q = jnp.exp(gcum - mid)
    e_k = jnp.exp(mid - gcum)
    qe, ke = q * e_q, k * e_k
    a = jnp.where(tril, _bdot(qe, ke, 2, 2), 0.0)
    e_g = jnp.exp(gcum)
    qg = q * e_g
    e_s = jnp.exp(glast - gcum)
    kg = k * e_s
    e_l = jnp.exp(glast)
    upd = _bdot(hi, kg, 1, 1)
    st = jnp.zeros((HG_D, HG_D), F32)
    states = []
    for n in range(nc):
        states.append(st)
        st = st * e_l[n] + upd[n]
    st_all = jnp.stack(states)
    o = _bdot(a, hi, 2, 1) + _bdot(qg, st_all, 2, 2)
    return dict(sf=sf, f=f, k=k, sq=sq, q=q, e_q=e_q, e_k=e_k, qe=qe, ke=ke, a=a, e_g=e_g, qg=qg, o=o,
                e_s=e_s, kg=kg, e_l=e_l, st_all=st_all)


def _hgrn_specs(t_, hw):
    nb = hw // LANE

    def col(off):
        return pl.BlockSpec((1, t_, LANE), lambda h, b: (b, 0, off * nb + h))

    vec = pl.BlockSpec((2, LANE), lambda h, b: (0, h))
    one = pl.BlockSpec((1, LANE), lambda h, b: (0, 0))
    blk = pl.BlockSpec((1, t_, LANE), lambda h, b: (b, 0, h))
    return col, vec, one, blk


def _chunk_masks(nc, c):
    row = lax.broadcasted_iota(jnp.int32, (nc, c, c), 1)
    cl = lax.broadcasted_iota(jnp.int32, (nc, c, c), 2)
    return row >= cl, (row >= cl).astype(BF16), (row <= cl).astype(BF16)


def _hgrn_fwd(zm, lb, gn, hw):
    b_, t_, _ = zm.shape
    c = min(HG_CHUNK, t_)
    nc = t_ // c
    col, vec, one, blk = _hgrn_specs(t_, hw)

    def body(q_ref, f_ref, i_ref, g_ref, lb_ref, gn_ref, y_ref):
        lbv, gnv = _sig(lb_ref[0:1, :] - lb_ref[1:2, :]), gn_ref[...]
        tril, tril_bf, _ = _chunk_masks(nc, c)
        chunks = lambda ref: ref[0].reshape(nc, c, LANE)
        o = _hgrn_forward(chunks(q_ref), chunks(f_ref), chunks(i_ref), lbv, tril, tril_bf)["o"]
        r = lax.rsqrt(jnp.mean(o * o, axis=-1, keepdims=True) + EPS)
        hg = chunks(g_ref)
        y_ref[0] = (o * r * gnv * (hg * _sig(hg))).reshape(t_, LANE)

    return pl.pallas_call(
        body, name="hgrn_fwd", grid=(HG_HEADS, b_),
        in_specs=[col(0), col(1), col(2), col(3), vec, one], out_specs=blk,
        out_shape=jax.ShapeDtypeStruct((b_, t_, hw), F32),
        compiler_params=_params(2),
    )(zm, zm, zm, zm, lb, gn)


def _hgrn_bwd(zm, dy, lb, gn, hw):
    b_, t_, _ = zm.shape
    c = min(HG_CHUNK, t_)
    nc = t_ // c
    col, vec, one, blk = _hgrn_specs(t_, hw)

    def body(q_ref, f_ref, i_ref, g_ref, dy_ref, lb_ref, gn_ref, dq_ref, df_ref, di_ref, dg_ref, dlb_ref, dgn_ref):
        h, b = pl.program_id(0), pl.program_id(1)
        lbv, gnv = _sig(lb_ref[0:1, :] - lb_ref[1:2, :]), gn_ref[...]
        tril, tril_bf, triu_bf = _chunk_masks(nc, c)
        last_row = lax.broadcasted_iota(jnp.int32, (nc, c, LANE), 1) == c - 1
        chunks = lambda ref: ref[0].reshape(nc, c, LANE)
        flat = lambda x: x.reshape(t_, LANE)
        hq, hi, hg = chunks(q_ref), chunks(i_ref), chunks(g_ref)
        p = _hgrn_forward(hq, chunks(f_ref), hi, lbv, tril, tril_bf)
        o, q, k, st_all, e_l = p["o"], p["q"], p["k"], p["st_all"], p["e_l"]
        dyv = chunks(dy_ref)
        sg = _sig(hg)
        r = lax.rsqrt(jnp.mean(o * o, axis=-1, keepdims=True) + EPS)
        dn = dyv * (hg * sg)
        dg_ref[0] = flat(dyv * (o * r * gnv) * (sg * (1.0 + hg * (1.0 - sg)))).astype(dg_ref.dtype)
        dgn = jnp.sum(flat(dn * o * r), axis=0, keepdims=True)
        dng = dn * gnv
        do = r * dng - o * (r * r * r) * jnp.mean(dng * o, axis=-1, keepdims=True)
        back = _bdotp(do, p["qg"], 1, 1)
        dst = jnp.zeros((HG_D, HG_D), F32)
        dsts = [None] * nc
        for n in range(nc - 1, -1, -1):
            dsts[n] = dst
            dst = dst * e_l[n] + back[n]
        dst_all = jnp.stack(dsts)
        da = jnp.where(tril, _bdotp(do, hi, 2, 2), 0.0)
        dq = _bdotp(da, p["ke"], 2, 1) * p["e_q"] + _bdotp(do, st_all, 2, 1) * p["e_g"]
        dk_state = _bdotp(hi, dst_all, 2, 1) * p["e_s"]
        dk = _bdotp(da, p["qe"], 1, 1) * p["e_k"] + dk_state
        di_ref[0] = flat(_bdot(p["a"], do, 1, 1) + _bdot(p["kg"], dst_all, 2, 2)).astype(di_ref.dtype)
        extra = (jnp.sum(k * dk_state, axis=1, keepdims=True) + e_l * jnp.sum(st_all * dst_all, axis=1, keepdims=True))
        dgc = q * dq - k * dk + jnp.where(last_row, extra, 0.0)
        dfv = _tri_dot_b(triu_bf, dgc) / p["f"] - dk
        sf, sq = p["sf"], p["sq"]
        df_ref[0] = flat(dfv * (1.0 - lbv) * sf * (1.0 - sf)).astype(df_ref.dtype)
        dlb = jnp.sum(flat(dfv * (1.0 - sf)), axis=0, keepdims=True)
        dq_ref[0] = flat(dq * (sq * (1.0 + hq * (1.0 - sq)))).astype(dq_ref.dtype)
        dl0 = dlb * lbv * (1.0 - lbv)
        _acc(dlb_ref, jnp.concatenate([dl0, -dl0], axis=0), b == 0)
        _acc(dgn_ref, dgn, jnp.logical_and(b == 0, h == 0))

    sds = jax.ShapeDtypeStruct((b_, t_, hw), MXU_DTYPE)
    return pl.pallas_call(
        body, name="hgrn_bwd", grid=(HG_HEADS, b_),
        in_specs=[col(0), col(1), col(2), col(3), blk, vec, one],
        out_specs=[blk, blk, blk, blk, vec, one],
        out_shape=[sds, sds, sds, sds, jax.ShapeDtypeStruct((2, hw), F32), jax.ShapeDtypeStruct((1, LANE), F32)],
        compiler_params=_params(2),
    )(zm, zm, zm, zm, dy, lb, gn)


def _fox_logf(x):
    return jnp.minimum(x, 0.0) - jnp.log(1.0 + jnp.exp(-jnp.abs(x)))


def _fox_prep(zf, bias):
    b_, t_, _ = zf.shape
    tb = min(FOX_BLOCK, t_)
    nb = t_ // tb

    def body(z_ref, b_ref, fc_ref):
        tril_bf = (lax.broadcasted_iota(jnp.int32, (tb, tb), 0) >= lax.broadcasted_iota(jnp.int32, (tb, tb), 1)).astype(BF16)
        bv = b_ref[...]

        def blk(i, carry):
            rows = pl.ds(pl.multiple_of(i * tb, tb), tb)
            fc = _tri_dot(tril_bf, _fox_logf(z_ref[0, rows, :] + bv)) + carry
            fc_ref[0, rows, :] = fc
            return fc[tb - 1:tb, :]

        lax.fori_loop(0, nb, blk, jnp.zeros((1, LANE), F32))

    blk_spec = pl.BlockSpec((1, t_, LANE), lambda b: (b, 0, 0))
    return pl.pallas_call(
        body, name="fox_prep", grid=(b_,),
        in_specs=[blk_spec, pl.BlockSpec((1, LANE), lambda b: (0, 0))], out_specs=blk_spec,
        out_shape=jax.ShapeDtypeStruct((b_, t_, LANE), F32), compiler_params=_params(1),
    )(zf, bias)


def _fox_post(dfc, zf, bias):
    b_, t_, _ = zf.shape
    tb = min(FOX_BLOCK, t_)
    nb = t_ // tb

    def body(d_ref, z_ref, b_ref, dz_ref, db_ref):
        triu_bf = (lax.broadcasted_iota(jnp.int32, (tb, tb), 0) <= lax.broadcasted_iota(jnp.int32, (tb, tb), 1)).astype(BF16)
        valid = lax.broadcasted_iota(jnp.int32, (tb, LANE), 1) < FOX_HEADS
        bv = b_ref[...]

        def blk(m, carry):
            tail, db = carry
            rows = pl.ds(pl.multiple_of((nb - 1 - m) * tb, tb), tb)
            dlf = _tri_dot(triu_bf, d_ref[0, rows, :]) + tail
            dx = jnp.where(valid, dlf * _sig(-(z_ref[0, rows, :] + bv)), 0.0)
            dz_ref[0, rows, :] = dx.astype(dz_ref.dtype)
            return dlf[0:1, :], db + jnp.sum(dx, axis=0, keepdims=True)

        z1 = jnp.zeros((1, LANE), F32)
        _, db = lax.fori_loop(0, nb, blk, (z1, z1))
        _acc(db_ref, db, pl.program_id(0) == 0)

    blk_spec = pl.BlockSpec((1, t_, LANE), lambda b: (b, 0, 0))
    vec = pl.BlockSpec((1, LANE), lambda b: (0, 0))
    return pl.pallas_call(
        body, name="fox_post", grid=(b_,), in_specs=[blk_spec, blk_spec, vec], out_specs=[blk_spec, vec],
        out_shape=[jax.ShapeDtypeStruct((b_, t_, LANE), MXU_DTYPE), jax.ShapeDtypeStruct((1, LANE), F32)],
        compiler_params=_params(1),
    )(dfc, zf, bias)


FOX_TILE = 128
FOX_BAND = 512
AUG = 64


def _head_mean_matrix():
    r = lax.broadcasted_iota(jnp.int32, (LANE, LANE), 0) // FOX_DH
    c = lax.broadcasted_iota(jnp.int32, (LANE, LANE), 1) // FOX_DH
    return (r == c).astype(BF16)


def _dot_right_exact(x, m_bf):
    hi = x.astype(BF16)
    r = x - hi.astype(F32)
    mid = r.astype(BF16)
    lo = (r - mid.astype(F32)).astype(BF16)

    def d(v):
        return lax.dot_general(v, m_bf, (((1,), (0,)), ((), ())), preferred_element_type=F32)

    return d(hi) + d(mid) + d(lo)


def _pair_norm(x, g2, bd):
    r = lax.rsqrt(_dot_right_exact(x * x, bd) * (1.0 / FOX_DH) + EPS)
    return x * r * g2, r


def _pair_norm_bwd(x, r, dy, g2, bd):
    dyg = dy * g2
    dx = r * dyg - x * (r * r * r) * (_dot_right_exact(dyg * x, bd) * (1.0 / FOX_DH))
    return dx, jnp.sum(dy * x * r, axis=0, keepdims=True)


def _head_lanes(xn, hh):
    return xn if hh == 0 else pltpu.roll(xn, FOX_DH, 1)


def _split3(x):
    hi = x.astype(BF16).astype(F32)
    mid = (x - hi).astype(BF16).astype(F32)
    return hi, mid, x - hi - mid


def _fox_operands(q_ref, k_ref, v_ref, fc_ref, gq2, gk2, p, qa, ka, va):
    t_ = q_ref.shape[1]
    bd = _head_mean_matrix()
    lane = lax.broadcasted_iota(jnp.int32, (t_, LANE), 1)
    qx, kx = q_ref[0], k_ref[0]
    qn, rq = _pair_norm(qx, gq2, bd)
    kn, rk = _pair_norm(kx, gk2, bd)
    vv = v_ref[0]
    q_aug = jnp.where(jnp.logical_and(lane >= AUG, lane < AUG + 3), 1.0, 0.0)
    for hh in range(2):
        fcol = jnp.sum(jnp.where(lane == 2 * p + hh, fc_ref[0], 0.0), axis=-1, keepdims=True)
        hi, mid, lo = _split3(-fcol)
        k_aug = jnp.where(lane == AUG, hi, jnp.where(lane == AUG + 1, mid, jnp.where(lane == AUG + 2, lo,
                          jnp.where(lane == AUG + 3, 1.0, 0.0))))
        head = lane < FOX_DH
        qa[hh] = jnp.where(head, _head_lanes(qn, hh), q_aug).astype(MXU_DTYPE)
        ka[hh] = jnp.where(head, _head_lanes(kn, hh), k_aug).astype(MXU_DTYPE)
        va[hh] = jnp.where(head, _head_lanes(vv, hh), 0.0).astype(MXU_DTYPE)
    return bd, lane, qx, kx, rq, rk


def _fox_specs(t_, fw, col0):
    npair = fw // LANE

    def col(off):
        return pl.BlockSpec((1, t_, LANE), lambda b, p: (b, 0, col0 + off * npair + p))

    pair = pl.BlockSpec((1, t_, LANE), lambda b, p: (b, 0, p))
    full = pl.BlockSpec((1, t_, LANE), lambda b, p: (b, 0, 0))
    gvec = pl.BlockSpec((1, LANE), lambda b, p: (0, 0))
    lse = pl.BlockSpec((1, 1, t_, LANE), lambda b, p: (b, p, 0, 0))
    return col, pair, full, gvec, lse


def _fox_fwd(zm, fc, gq2, gk2, fw, col0):
    b_, t_, _ = zm.shape
    npair = fw // LANE
    tq = min(FOX_TILE, t_)
    bw = min(FOX_BAND, t_)
    nband, tpb = t_ // bw, bw // tq
    scale = FOX_DH ** -0.5
    col, pair, full, gvec, lse_spec = _fox_specs(t_, fw, col0)

    def body(q_ref, k_ref, v_ref, fc_ref, gq_ref, gk_ref, o_ref, lse_ref, qa, ka, va):
        p = pl.program_id(1)
        _fox_operands(q_ref, k_ref, v_ref, fc_ref, gq_ref[...] * scale, gk_ref[...], p, qa, ka, va)
        ri = lax.broadcasted_iota(jnp.int32, (tq, bw), 0)
        ci = lax.broadcasted_iota(jnp.int32, (tq, bw), 1)
        lane = lax.broadcasted_iota(jnp.int32, (tq, LANE), 1)

        for band in range(nband):
            c0 = band * bw

            def qtile(ii, _, c0=c0):
                r0 = pl.multiple_of(c0 + ii * tq, tq)
                rows = pl.ds(r0, tq)
                keep = c0 + ci <= r0 + ri
                res = []
                for hh in range(2):
                    qb = qa[hh, rows, :]
                    s_b = jnp.where(keep, _nt(qb, ka[hh, c0:c0 + bw, :]), NEG)
                    m = jnp.max(s_b, axis=-1, keepdims=True)
                    if c0:
                        s_a = _nt(qb, ka[hh, 0:c0, :])
                        m = jnp.maximum(m, jnp.max(s_a, axis=-1, keepdims=True))
                    p_b = jnp.exp(s_b - m)
                    l = jnp.sum(p_b, axis=-1, keepdims=True)
                    acc = _nn(p_b, va[hh, c0:c0 + bw, :])
                    if c0:
                        p_a = jnp.exp(s_a - m)
                        l = l + jnp.sum(p_a, axis=-1, keepdims=True)
                        acc = acc + _nn(p_a, va[hh, 0:c0, :])
                    res.append((acc / l, m + jnp.log(l)))
                (o0, e0), (o1, e1) = res
                o_ref[0, rows, :] = jnp.where(lane < FOX_DH, o0, pltpu.roll(o1, FOX_DH, 1))
                lse_ref[0, 0, rows, :] = jnp.where(lane == 0, e0, jnp.where(lane == 1, e1, 0.0))
                return 0

            lax.fori_loop(0, tpb, qtile, 0)

    return pl.pallas_call(
        body, name="fox_fwd", grid=(b_, npair),
        in_specs=[col(0), col(1), col(2), full, gvec, gvec],
        out_specs=[pair, lse_spec],
        out_shape=[jax.ShapeDtypeStruct((b_, t_, fw), F32), jax.ShapeDtypeStruct((b_, npair, t_, LANE), F32)],
        scratch_shapes=[pltpu.VMEM((2, t_, LANE), MXU_DTYPE)] * 3,
        compiler_params=_params(2),
    )(zm, zm, zm, fc, gq2, gk2)


def _norm_bwd(x, dy, g):
    r = lax.rsqrt(jnp.mean(x * x, axis=-1, keepdims=True) + EPS)
    dyg = dy * g
    dx = r * dyg - x * (r * r * r) * jnp.mean(dyg * x, axis=-1, keepdims=True)
    return dx, jnp.sum(dy * x * r, axis=0, keepdims=True)


def _fox_bwd(zm, o, do, lse, fc, gq2, gk2, fw, col0):
    b_, t_, _ = zm.shape
    npair = fw // LANE
    tq = min(FOX_TILE, t_)
    nb = t_ // tq
    bw = min(FOX_BAND, t_)
    nband, tpb = t_ // bw, bw // tq
    scale = FOX_DH ** -0.5
    col, pair, full, gvec, lse_spec = _fox_specs(t_, fw, col0)

    def body(q_ref, k_ref, v_ref, o_ref, do_ref, lse_ref, fc_ref, gq_ref, gk_ref,
             dq_ref, dk_ref, dv_ref, dfc_ref, dgq_ref, dgk_ref, qa, ka, va, da, rowv, dq_acc, dk_acc, dv_acc):
        b, p = pl.program_id(0), pl.program_id(1)
        gq2v, gk2v = gq_ref[...] * scale, gk_ref[...]
        bd, lane, qx, kx, rq, rk = _fox_operands(q_ref, k_ref, v_ref, fc_ref, gq2v, gk2v, p, qa, ka, va)
        head = lane < FOX_DH
        dov = do_ref[0]
        dsum = _dot_right_exact(dov * o_ref[0], bd)
        eye = (lax.broadcasted_iota(jnp.int32, (tq, tq), 0) == lax.broadcasted_iota(jnp.int32, (tq, tq), 1)).astype(F32)
        for hh in range(2):
            da[hh] = jnp.where(head, _head_lanes(dov, hh), 0.0).astype(MXU_DTYPE)
            for blk in range(nb):
                rs = slice(blk * tq, (blk + 1) * tq)
                rowv[2 * hh:2 * hh + 1, rs] = jnp.sum(eye * lse_ref[0, 0, rs, hh:hh + 1], axis=0, keepdims=True)
                rowv[2 * hh + 1:2 * hh + 2, rs] = jnp.sum(eye * dsum[rs, hh * FOX_DH:hh * FOX_DH + 1], axis=0, keepdims=True)
        dq_acc[...] = jnp.zeros(dq_acc.shape, F32)
        ri = lax.broadcasted_iota(jnp.int32, (tq, bw), 0)
        ci = lax.broadcasted_iota(jnp.int32, (tq, bw), 1)

        def part(hh, kb, vb, lo, hi, keep):
            qm, dm = qa[hh, lo:hi, :], da[hh, lo:hi, :]
            pt = jnp.exp(_nt(kb, qm) - rowv[2 * hh:2 * hh + 1, lo:hi])
            if keep is not None:
                pt = jnp.where(keep, pt, 0.0)
            dst = pt * (_nt(vb, dm) - rowv[2 * hh + 1:2 * hh + 2, lo:hi])
            dq_acc[hh, lo:hi, :] += _tn(dst, kb)
            return _nn(dst, qm), _nn(pt, dm)

        for band in range(nband):
            c0 = band * bw

            def kvtile(jj, _, c0=c0):
                r0 = pl.multiple_of(c0 + jj * tq, tq)
                rows = pl.ds(r0, tq)
                keep = c0 + ci >= r0 + ri
                for hh in range(2):
                    kb, vb = ka[hh, rows, :], va[hh, rows, :]
                    dk_t, dv_t = part(hh, kb, vb, c0, c0 + bw, keep)
                    if c0 + bw < t_:
                        dk_u, dv_u = part(hh, kb, vb, c0 + bw, t_, None)
                        dk_t, dv_t = dk_t + dk_u, dv_t + dv_u
                    dk_acc[hh, rows, :] = dk_t
                    dv_acc[hh, rows, :] = dv_t
                return 0

            lax.fori_loop(0, tpb, kvtile, 0)

        dq0, dq1, dk0, dk1 = dq_acc[0], dq_acc[1], dk_acc[0], dk_acc[1]
        dqn = jnp.where(head, dq0, pltpu.roll(dq1, FOX_DH, 1))
        dkn = jnp.where(head, dk0, pltpu.roll(dk1, FOX_DH, 1))
        dqx, gq_part = _pair_norm_bwd(qx, rq, dqn, gq2v, bd)
        dkx, gk_part = _pair_norm_bwd(kx, rk, dkn, gk2v, bd)
        dq_ref[0] = dqx.astype(dq_ref.dtype)
        dk_ref[0] = dkx.astype(dk_ref.dtype)
        dv_ref[0] = jnp.where(head, dv_acc[0], pltpu.roll(dv_acc[1], FOX_DH, 1)).astype(dv_ref.dtype)

        def bias_grad(dqh, dkh):
            return (jnp.sum(jnp.where(lane == AUG + 3, dqh, 0.0), axis=-1, keepdims=True)
                    - jnp.sum(jnp.where(lane == AUG, dkh, 0.0), axis=-1, keepdims=True))

        dfc_ref[0, 0] = jnp.where(lane == 0, bias_grad(dq0, dk0), jnp.where(lane == 1, bias_grad(dq1, dk1), 0.0))
        first = jnp.logical_and(b == 0, p == 0)
        _acc(dgq_ref, gq_part * scale, first)
        _acc(dgk_ref, gk_part, first)

    sds = jax.ShapeDtypeStruct((b_, t_, fw), MXU_DTYPE)
    gs = jax.ShapeDtypeStruct((1, LANE), F32)
    return pl.pallas_call(
        body, name="fox_bwd", grid=(b_, npair),
        in_specs=[col(0), col(1), col(2), pair, pair, lse_spec, full, gvec, gvec],
        out_specs=[pair, pair, pair, lse_spec, gvec, gvec],
        out_shape=[sds, sds, sds, jax.ShapeDtypeStruct((b_, npair, t_, LANE), F32), gs, gs],
        scratch_shapes=[pltpu.VMEM((2, t_, LANE), MXU_DTYPE)] * 4
        + [pltpu.VMEM((8, t_), F32)] + [pltpu.VMEM((2, t_, LANE), F32)] * 3,
        compiler_params=_params(2),
    )(zm, zm, zm, o, do, lse, fc, gq2, gk2)


def _mem_specs(t_, m_, mw, col0):
    nh = mw // LANE
    qcol = pl.BlockSpec((1, t_, LANE), lambda b, h: (b, 0, col0 + h))
    kcol = pl.BlockSpec((1, m_, LANE), lambda b, h: (b, 0, h))
    vcol = pl.BlockSpec((1, m_, LANE), lambda b, h: (b, 0, nh + h))
    ycol = pl.BlockSpec((1, t_, LANE), lambda b, h: (b, 0, h))
    gvec = pl.BlockSpec((1, LANE), lambda b, h: (0, 0))
    return qcol, kcol, vcol, ycol, gvec


def _mem_fwd(zm, mkv, gq, gk, mw, col0):
    b_, t_, _ = zm.shape
    m_ = mkv.shape[1]
    tq = min(512, t_)
    nb = t_ // tq
    scale = MEM_DH ** -0.5
    qcol, kcol, vcol, ycol, gvec = _mem_specs(t_, m_, mw, col0)

    def body(q_ref, k_ref, v_ref, gq_ref, gk_ref, y_ref):
        gqv, gkv = gq_ref[...] * scale, gk_ref[...]
        kv = k_ref[0]
        kn = _mx(kv * lax.rsqrt(jnp.mean(kv * kv, axis=-1, keepdims=True) + EPS) * gkv)
        vv = _mx(v_ref[0])

        def blk(i, _):
            rows = pl.ds(pl.multiple_of(i * tq, tq), tq)
            qv = q_ref[0, rows, :]
            s = _nt(qv * lax.rsqrt(jnp.mean(qv * qv, axis=-1, keepdims=True) + EPS) * gqv, kn)
            e = jnp.exp(s - jnp.max(s, axis=-1, keepdims=True))
            y_ref[0, rows, :] = _nn(e / jnp.sum(e, axis=-1, keepdims=True), vv)
            return 0

        lax.fori_loop(0, nb, blk, 0)

    return pl.pallas_call(
        body, name="mem_fwd", grid=(b_, MEM_HEADS), in_specs=[qcol, kcol, vcol, gvec, gvec], out_specs=ycol,
        out_shape=jax.ShapeDtypeStruct((b_, t_, mw), F32), compiler_params=_params(2),
    )(zm, mkv, mkv, gq, gk)


def _mem_bwd(zm, mkv, dy, gq, gk, mw, col0):
    b_, t_, _ = zm.shape
    m_ = mkv.shape[1]
    tq = min(512, t_)
    nb = t_ // tq
    scale = MEM_DH ** -0.5
    qcol, kcol, vcol, ycol, gvec = _mem_specs(t_, m_, mw, col0)

    def body(q_ref, k_ref, v_ref, dy_ref, gq_ref, gk_ref, dq_ref, dk_ref, dv_ref, dgq_ref, dgk_ref):
        gqv, gkv = gq_ref[...] * scale, gk_ref[...]
        kv = k_ref[0]
        kn = _mx(kv * lax.rsqrt(jnp.mean(kv * kv, axis=-1, keepdims=True) + EPS) * gkv)
        vv = _mx(v_ref[0])

        def blk(i, carry):
            dkn, dvv, dgq = carry
            rows = pl.ds(pl.multiple_of(i * tq, tq), tq)
            qv = q_ref[0, rows, :]
            qn = _mx(qv * lax.rsqrt(jnp.mean(qv * qv, axis=-1, keepdims=True) + EPS) * gqv)
            s = _nt(qn, kn)
            e = jnp.exp(s - jnp.max(s, axis=-1, keepdims=True))
            pm = e / jnp.sum(e, axis=-1, keepdims=True)
            dob = _mx(dy_ref[0, rows, :])
            dp = _nt(dob, vv)
            ds = pm * (dp - jnp.sum(dp * pm, axis=-1, keepdims=True))
            dqv, gq_part = _norm_bwd(qv, _nn(ds, kn), gqv)
            dq_ref[0, rows, :] = dqv.astype(dq_ref.dtype)
            return dkn + _tn(ds, qn), dvv + _tn(pm, dob), dgq + gq_part * scale

        z = jnp.zeros((m_, LANE), F32)
        dkn, dvv, dgq = lax.fori_loop(0, nb, blk, (z, z, jnp.zeros((1, LANE), F32)))
        dkv, dgk = _norm_bwd(kv, dkn, gkv)
        dk_ref[0] = dkv
        dv_ref[0] = dvv
        first = jnp.logical_and(pl.program_id(0) == 0, pl.program_id(1) == 0)
        _acc(dgq_ref, dgq, first)
        _acc(dgk_ref, dgk, first)

    kblk = pl.BlockSpec((1, m_, LANE), lambda b, h: (b, 0, h))
    gs = jax.ShapeDtypeStruct((1, LANE), F32)
    ks = jax.ShapeDtypeStruct((b_, m_, mw), F32)
    return pl.pallas_call(
        body, name="mem_bwd", grid=(b_, MEM_HEADS), in_specs=[qcol, kcol, vcol, ycol, gvec, gvec],
        out_specs=[ycol, kblk, kblk, gvec, gvec],
        out_shape=[jax.ShapeDtypeStruct((b_, t_, mw), MXU_DTYPE), ks, ks, gs, gs], compiler_params=_params(2),
    )(zm, mkv, mkv, dy, gq, gk)


def _merge_specs(tm, d, w, gcol):
    row_d = pl.BlockSpec((tm, d), lambda i: (i, 0))
    row_w = pl.BlockSpec((tm, w), lambda i: (i, 0))
    gates = [pl.BlockSpec((tm, d), functools.partial(lambda i, k: (i, gcol + k), k=k)) for k in range(3)]
    w_br = pl.BlockSpec((w, d), lambda i: (0, 0))
    w_o = pl.BlockSpec((d, d), lambda i: (0, 0))
    return row_d, row_w, gates, w_br, w_o


def _merge_fwd(x, ys, zm, w_brs, w_out, gcol, tm=256):
    n, d = x.shape
    w = ys[0].shape[1]
    tm = _tile(n, tm, 8)
    row_d, row_w, gates, w_br, w_o = _merge_specs(tm, d, w, gcol)

    def body(x_ref, ya, yb, yc, g0, g1, g2, wa, wb, wc, wo, x1_ref, mg_ref):
        mg = (_sig(g0[...]) * _nn(ya[...], wa[...]) + _sig(g1[...]) * _nn(yb[...], wb[...])
              + _sig(g2[...]) * _nn(yc[...], wc[...]))
        mg_ref[...] = mg.astype(mg_ref.dtype)
        x1_ref[...] = x_ref[...] + _nn(mg, wo[...])

    return pl.pallas_call(
        body, name="merge_fwd", grid=(n // tm,),
        in_specs=[row_d, row_w, row_w, row_w] + gates + [w_br, w_br, w_br, w_o],
        out_specs=[row_d, row_d],
        out_shape=[jax.ShapeDtypeStruct((n, d), F32), jax.ShapeDtypeStruct((n, d), MXU_DTYPE)],
        compiler_params=_params(1),
    )(x, *ys, zm, zm, zm, *w_brs, w_out)


def _merge_bwd(dx1, ys, zm, w_brs, w_out, gcol, tm=256):
    n, d = dx1.shape
    w = ys[0].shape[1]
    tm = _tile(n, tm, 8)
    row_d, row_w, gates, w_br, w_o = _merge_specs(tm, d, w, gcol)

    def body(dx_ref, ya, yb, yc, g0, g1, g2, wa, wb, wc, wo, dgl_ref, dpa, dpb, dpc, dya, dyb, dyc):
        dm = _nt(dx_ref[...], wo[...])
        for k, (y, g, wr, dp_ref, dy_ref) in enumerate(((ya, g0, wa, dpa, dya), (yb, g1, wb, dpb, dyb),
                                                        (yc, g2, wc, dpc, dyc))):
            sg = _sig(g[...])
            pr = _nn(y[...], wr[...])
            dgl_ref[:, k * d:(k + 1) * d] = (dm * pr * sg * (1.0 - sg)).astype(dgl_ref.dtype)
            dp = (dm * sg).astype(dp_ref.dtype)
            dp_ref[...] = dp
            dy_ref[...] = _nt(dp, wr[...])

    sd = jax.ShapeDtypeStruct((n, d), MXU_DTYPE)
    sw = jax.ShapeDtypeStruct((n, w), F32)
    return pl.pallas_call(
        body, name="merge_bwd", grid=(n // tm,),
        in_specs=[row_d, row_w, row_w, row_w] + gates + [w_br, w_br, w_br, w_o],
        out_specs=[pl.BlockSpec((tm, 3 * d), lambda i: (i, 0)), row_d, row_d, row_d, row_w, row_w, row_w],
        out_shape=[jax.ShapeDtypeStruct((n, 3 * d), MXU_DTYPE), sd, sd, sd, sw, sw, sw],
        compiler_params=_params(1),
    )(dx1, *ys, zm, zm, zm, *w_brs, w_out)


CONV_ROWS = 256
HALO = 8


def _ext(ref, r0, t_):
    rc = min(CONV_ROWS, t_)
    a, b = max(r0 - HALO, 0), min(r0 + rc + HALO, t_)
    parts = []
    if r0 - HALO < 0:
        parts.append(jnp.zeros((HALO, ref.shape[2]), F32))
    parts.append(ref[0, a:b, :].astype(F32))
    if r0 + rc + HALO > t_:
        parts.append(jnp.zeros((HALO, ref.shape[2]), F32))
    return jnp.concatenate(parts, axis=0) if len(parts) > 1 else parts[0]


def _gelu_parts(ac):
    cdf = 0.5 * (1.0 + _erf(ac * (2.0 ** -0.5)))
    pdf = jnp.exp(-0.5 * ac * ac) * ((2.0 * math.pi) ** -0.5)
    return cdf, pdf


def _conv_taps(a_ext, cw, cb):
    return cw[0:1, :] * pltpu.roll(a_ext, 2, 0) + cw[1:2, :] * pltpu.roll(a_ext, 1, 0) + cw[2:3, :] * a_ext + cb


def _glu_specs(t_, f, g):
    gate = pl.BlockSpec((1, t_, g), lambda j, b: (b, 0, j))
    value = pl.BlockSpec((1, t_, g), lambda j, b: (b, 0, f // g + j))
    cwb = pl.BlockSpec((3, g), lambda j, b: (0, j))
    cbb = pl.BlockSpec((1, g), lambda j, b: (0, j))
    return gate, value, cwb, cbb


def _glu_fwd(u, cw, cb):
    b_, t_, f2 = u.shape
    f = f2 // 2
    g = min(FFN_GROUP, f)
    rc = min(CONV_ROWS, t_)
    gate, value, cwb, cbb = _glu_specs(t_, f, g)

    def body(a_ref, v_ref, cw_ref, cb_ref, y_ref):
        cwv, cbv = cw_ref[...], cb_ref[...]
        for r0 in range(0, t_, rc):
            ac = _conv_taps(_ext(a_ref, r0, t_), cwv, cbv)[HALO:HALO + rc]
            cdf, _ = _gelu_parts(ac)
            y_ref[0, r0:r0 + rc, :] = (ac * cdf * v_ref[0, r0:r0 + rc, :]).astype(y_ref.dtype)

    return pl.pallas_call(
        body, name="glu_fwd", grid=(f // g, b_), in_specs=[gate, value, cwb, cbb], out_specs=gate,
        out_shape=jax.ShapeDtypeStruct((b_, t_, f), MXU_DTYPE), compiler_params=_params(2),
    )(u, u, cw, cb)


def _glu_bwd(u, dy, cw, cb):
    b_, t_, f2 = u.shape
    f = f2 // 2
    g = min(FFN_GROUP, f)
    rc = min(CONV_ROWS, t_)
    ne = rc + 2 * HALO
    gate, value, cwb, cbb = _glu_specs(t_, f, g)

    def body(a_ref, v_ref, dy_ref, cw_ref, cb_ref, da_ref, dv_ref, dcw_ref, dcb_ref):
        cwv, cbv = cw_ref[...], cb_ref[...]
        dcw = [jnp.zeros((1, g), F32) for _ in range(3)]
        dcb = jnp.zeros((1, g), F32)
        for r0 in range(0, t_, rc):
            a_ext, v_ext, dy_ext = _ext(a_ref, r0, t_), _ext(v_ref, r0, t_), _ext(dy_ref, r0, t_)
            ac = _conv_taps(a_ext, cwv, cbv)
            cdf, pdf = _gelu_parts(ac)
            dac = dy_ext * v_ext * (cdf + ac * pdf)
            da = cwv[2:3, :] * dac + cwv[1:2, :] * pltpu.roll(dac, ne - 1, 0) + cwv[0:1, :] * pltpu.roll(dac, ne - 2, 0)
            mid = slice(HALO, HALO + rc)
            da_ref[0, r0:r0 + rc, :] = da[mid].astype(da_ref.dtype)
            dv_ref[0, r0:r0 + rc, :] = (dy_ext[mid] * ac[mid] * cdf[mid]).astype(dv_ref.dtype)
            dacm = dac[mid]
            dcw[0] = dcw[0] + jnp.sum(dacm * pltpu.roll(a_ext, 2, 0)[mid], axis=0, keepdims=True)
            dcw[1] = dcw[1] + jnp.sum(dacm * pltpu.roll(a_ext, 1, 0)[mid], axis=0, keepdims=True)
            dcw[2] = dcw[2] + jnp.sum(dacm * a_ext[mid], axis=0, keepdims=True)
            dcb = dcb + jnp.sum(dacm, axis=0, keepdims=True)
        first = pl.program_id(1) == 0
        _acc(dcw_ref, jnp.concatenate(dcw, axis=0), first)
        _acc(dcb_ref, dcb, first)

    sds = jax.ShapeDtypeStruct((b_, t_, f), MXU_DTYPE)
    return pl.pallas_call(
        body, name="glu_bwd", grid=(f // g, b_), in_specs=[gate, value, gate, cwb, cbb],
        out_specs=[gate, gate, cwb, cbb],
        out_shape=[sds, sds, jax.ShapeDtypeStruct((3, f), F32), jax.ShapeDtypeStruct((1, f), F32)],
        compiler_params=_params(2),
    )(u, u, dy, cw, cb)


def _loss_head(x1, ffn, target, tm=512):
    n, d = x1.shape
    tm = _tile(n, tm, 8)

    def body(x_ref, f_ref, t_ref, dy_ref, l_ref):
        err = x_ref[...] + f_ref[...] - t_ref[...]
        dy_ref[...] = err * (1.0 / d)
        _acc(l_ref, jnp.sum(err * err, axis=0, keepdims=True) * (0.5 / d), pl.program_id(0) == 0)

    row = pl.BlockSpec((tm, d), lambda i: (i, 0))
    vec = pl.BlockSpec((1, d), lambda i: (0, 0))
    return pl.pallas_call(
        body, name="loss_head", grid=(n // tm,), in_specs=[row, row, row], out_specs=[row, vec],
        out_shape=[jax.ShapeDtypeStruct((n, d), F32), jax.ShapeDtypeStruct((1, d), F32)], compiler_params=_params(1),
    )(x1, ffn, target)


def _place():
    x, y, c = lax.axis_index("x"), lax.axis_index("y"), lax.axis_index("c")
    chips = [(1 - x, y), (x, 1 - y), (1 - x, 1 - y)]
    return x, y, c, chips


def _remote(src, dst, send_sem, recv_sem, to):
    return pltpu.make_async_remote_copy(src_ref=src, dst_ref=dst, send_sem=send_sem, recv_sem=recv_sem,
                                        device_id=to, device_id_type=MESH)


STACK, COLS = "stack", "cols"


def _shard_ref(ref, kind, s, rows, c):
    if kind == COLS:
        cols = pl.ds(pl.multiple_of(s * c, LANE), c)
        return ref.at[:, cols] if rows is None else ref.at[rows, cols]
    return ref.at[s] if rows is None else ref.at[s, rows, :]


def _halves(c, half):
    mine = pl.ds(pl.multiple_of(c * half, 16), half)
    theirs = pl.ds(pl.multiple_of((1 - c) * half, 16), half)
    return mine, theirs


def _gather_shards(shards, kinds):
    nw = len(shards)

    def body(*refs):
        ins, outs = refs[:nw], refs[nw:2 * nw]
        send_sems, recv_sems = refs[2 * nw:]
        x, y, c, chips = _place()
        me, sib = 2 * x + y, (x, y, 1 - c)
        first, passed = [], []
        for i, (w_ref, o_ref, kind) in enumerate(zip(ins, outs, kinds)):
            r, cw = w_ref.shape
            mine, _ = _halves(c, r // 2)
            for j, chip in enumerate(chips):
                first.append(_remote(w_ref.at[mine], _shard_ref(o_ref, kind, me, mine, cw), send_sems.at[6 * i + j],
                                     recv_sems.at[6 * i + j], (*chip, c)))
                first[-1].start()
        for i, (w_ref, o_ref, kind) in enumerate(zip(ins, outs, kinds)):
            r, cw = w_ref.shape
            mine, _ = _halves(c, r // 2)
            for j, (px, py) in enumerate(chips):
                blk = _shard_ref(o_ref, kind, 2 * px + py, mine, cw)
                _remote(blk, blk, send_sems.at[6 * i + j], recv_sems.at[6 * i + j], sib).wait_recv()
                passed.append(_remote(blk, blk, send_sems.at[6 * i + 3 + j], recv_sems.at[6 * i + 3 + j], sib))
                passed[-1].start()
        for i, (w_ref, o_ref, kind) in enumerate(zip(ins, outs, kinds)):
            r, cw = w_ref.shape
            _, theirs = _halves(c, r // 2)
            for j, (px, py) in enumerate(chips):
                blk = _shard_ref(o_ref, kind, 2 * px + py, theirs, cw)
                _remote(blk, blk, send_sems.at[6 * i + 3 + j], recv_sems.at[6 * i + 3 + j], sib).wait_recv()
        for cp in first + passed:
            cp.wait_send()

    def out_sds(a, kind):
        r, c = a.shape
        return jax.ShapeDtypeStruct((r, N_CHIPS * c) if kind == COLS else (N_CHIPS, r, c), a.dtype)

    return pl.pallas_call(
        body, name="gather_shards", in_specs=[ANY] * nw, out_specs=[ANY] * nw,
        out_shape=[out_sds(a, k) for a, k in zip(shards, kinds)],
        scratch_shapes=[pltpu.SemaphoreType.DMA((6 * nw,)), pltpu.SemaphoreType.DMA((6 * nw,))],
    )(*shards)


def _half_shape(g, kind):
    if kind == COLS:
        return (g.shape[0] // 2, g.shape[1])
    return (g.shape[0], g.shape[1] // 2, g.shape[2])


def _pair_swap_halves(gs, kinds):
    nw = len(gs)

    def body(*refs):
        ins, outs = refs[:nw], refs[nw:2 * nw]
        send_sems, recv_sems = refs[2 * nw:]
        x, y, c, _ = _place()
        cps = []
        for i, (g_ref, a_ref, kind) in enumerate(zip(ins, outs, kinds)):
            r = g_ref.shape[0] if kind == COLS else g_ref.shape[1]
            _, theirs = _halves(c, r // 2)
            src = g_ref.at[theirs] if kind == COLS else g_ref.at[:, theirs]
            cps.append(_remote(src, a_ref, send_sems.at[i], recv_sems.at[i], (x, y, 1 - c)))
            cps[-1].start()
        for cp in cps:
            cp.wait()

    return pl.pallas_call(
        body, name="pair_swap_halves", in_specs=[ANY] * nw, out_specs=[ANY] * nw,
        out_shape=[jax.ShapeDtypeStruct(_half_shape(g, k), g.dtype) for g, k in zip(gs, kinds)],
        scratch_shapes=[pltpu.SemaphoreType.DMA((nw,)), pltpu.SemaphoreType.DMA((nw,))],
    )(*gs)


def _row_tile(rows, width, itemsize=4, target=2 ** 21):
    return _tile(rows, max(8, target // (width * itemsize)), 8)


def _add_half(g, a, kind, c_idx, name):
    if kind == COLS:
        half, wd = a.shape
        tr = _row_tile(half, wd)
        nblk = half // tr
        grid = (nblk,)
        g_spec = pl.BlockSpec((tr, wd), lambda i, c_ref: (c_ref[0] * nblk + i, 0))
        a_spec = pl.BlockSpec((tr, wd), lambda i, c_ref: (i, 0))
    else:
        n, half, wd = a.shape
        tr = _row_tile(half, wd)
        nblk = half // tr
        grid = (n, nblk)
        g_spec = pl.BlockSpec((1, tr, wd), lambda s, i, c_ref: (s, c_ref[0] * nblk + i, 0))
        a_spec = pl.BlockSpec((1, tr, wd), lambda s, i, c_ref: (s, i, 0))

    def body(c_ref, g_ref, a_ref, o_ref):
        o_ref[...] = (g_ref[...] + a_ref[...]).astype(o_ref.dtype)

    return pl.pallas_call(
        body, name=name,
        grid_spec=pltpu.PrefetchScalarGridSpec(num_scalar_prefetch=1, grid=grid, in_specs=[g_spec, a_spec],
                                               out_specs=a_spec),
        out_shape=jax.ShapeDtypeStruct(a.shape, EXCHANGE_DTYPE), compiler_params=_params(len(grid)),
    )(c_idx, g, a)


def _chip_exchange(ps, kinds):
    nw = len(ps)

    def shard_shape(p, kind):
        return (p.shape[0], p.shape[1] // N_CHIPS) if kind == COLS else p.shape[1:]

    def body(*refs):
        ins, outs = refs[:nw], refs[nw:2 * nw]
        send_sems, recv_sems = refs[2 * nw:]
        x, y, c, chips = _place()
        me = 2 * x + y
        sent = []
        for i, (p_ref, b_ref, kind) in enumerate(zip(ins, outs, kinds)):
            cw = b_ref.shape[2]
            for j, (px, py) in enumerate(chips):
                sent.append(_remote(_shard_ref(p_ref, kind, 2 * px + py, None, cw), b_ref.at[me],
                                    send_sems.at[3 * i + j], recv_sems.at[3 * i + j], (px, py, c)))
                sent[-1].start()
        for i, b_ref in enumerate(outs):
            for j, (px, py) in enumerate(chips):
                blk = b_ref.at[2 * px + py]
                _remote(blk, blk, send_sems.at[3 * i + j], recv_sems.at[3 * i + j], (px, py, c)).wait_recv()
        for cp in sent:
            cp.wait_send()

    return pl.pallas_call(
        body, name="chip_exchange", in_specs=[ANY] * nw, out_specs=[ANY] * nw,
        out_shape=[jax.ShapeDtypeStruct((N_CHIPS,) + tuple(shard_shape(p, k)), p.dtype) for p, k in zip(ps, kinds)],
        scratch_shapes=[pltpu.SemaphoreType.DMA((3 * nw,)), pltpu.SemaphoreType.DMA((3 * nw,))],
    )(*ps)


def _sum_chips(bq, name):
    n, h, wd = bq.shape
    tr = _row_tile(h, wd * n)

    def body(b_ref, o_ref):
        acc = b_ref[0].astype(F32)
        for s in range(1, n):
            acc = acc + b_ref[s].astype(F32)
        o_ref[...] = acc

    return pl.pallas_call(
        body, name=name, grid=(h // tr,),
        in_specs=[pl.BlockSpec((n, tr, wd), lambda i: (0, i, 0))], out_specs=pl.BlockSpec((tr, wd), lambda i: (i, 0)),
        out_shape=jax.ShapeDtypeStruct((h, wd), F32), compiler_params=_params(1),
    )(bq)


def _pair_join_halves(qs):
    nw = len(qs)

    def body(*refs):
        ins, outs = refs[:nw], refs[nw:2 * nw]
        send_sems, recv_sems = refs[2 * nw:]
        x, y, c, _ = _place()
        sent = []
        for i, (q_ref, o_ref) in enumerate(zip(ins, outs)):
            mine, _ = _halves(c, q_ref.shape[0])
            sent.append(_remote(q_ref, o_ref.at[mine], send_sems.at[i], recv_sems.at[i], (x, y, 1 - c)))
            sent[-1].start()
        for i, (q_ref, o_ref) in enumerate(zip(ins, outs)):
            _, theirs = _halves(c, q_ref.shape[0])
            _remote(q_ref, o_ref.at[theirs], send_sems.at[i], recv_sems.at[i], (x, y, 1 - c)).wait_recv()
        for cp in sent:
            cp.wait_send()

    return pl.pallas_call(
        body, name="pair_join_halves", in_specs=[ANY] * nw, out_specs=[ANY] * nw,
        out_shape=[jax.ShapeDtypeStruct((2 * q.shape[0], q.shape[1]), q.dtype) for q in qs],
        scratch_shapes=[pltpu.SemaphoreType.DMA((nw,)), pltpu.SemaphoreType.DMA((nw,))],
    )(*qs)


def _all_sum_small(s, name):
    sr, w = s.shape

    def body(s_ref, o_ref, buf, send_sems, recv_sems):
        x, y, c, _ = _place()
        me = 4 * x + 2 * y + c
        buf[me] = s_ref[...]
        peers = []
        for k in range(1, 8):
            px = 1 - x if k & 4 else x
            py = 1 - y if k & 2 else y
            pc = 1 - c if k & 1 else c
            peers.append((px, py, pc))
        sent = [_remote(s_ref, buf.at[me], send_sems.at[k], recv_sems.at[k], peer) for k, peer in enumerate(peers)]
        for cp in sent:
            cp.start()
        for k, (px, py, pc) in enumerate(peers):
            _remote(s_ref, buf.at[4 * px + 2 * py + pc], send_sems.at[k], recv_sems.at[k], (px, py, pc)).wait_recv()
        for cp in sent:
            cp.wait_send()
        acc = buf[0]
        for d in range(1, 8):
            acc = acc + buf[d]
        o_ref[...] = acc

    vm = pl.BlockSpec(memory_space=pltpu.VMEM)
    return pl.pallas_call(
        body, name=name, in_specs=[vm], out_specs=vm, out_shape=jax.ShapeDtypeStruct((sr, w), F32),
        scratch_shapes=[pltpu.VMEM((8, sr, w), F32), pltpu.SemaphoreType.DMA((7,)), pltpu.SemaphoreType.DMA((7,))],
    )(s)


BIG = ("w_in", "mem_kv_w", "w_br_hgrn", "w_br_fox", "w_br_mem", "w_out", "ffn_w_up", "ffn_w_down")
KIND = {"w_in": STACK, "mem_kv_w": STACK, "w_br_hgrn": COLS, "w_br_fox": COLS, "w_br_mem": COLS, "w_out": STACK,
        "ffn_w_up": COLS, "ffn_w_down": STACK}
ROW_SHARDED = ("mem_kv_w", "w_out", "ffn_w_down")


def _put_shard(arr, kind, s, piece):
    if kind == COLS:
        return lax.dynamic_update_slice(arr, piece, (0, s * piece.shape[1]))
    return lax.dynamic_update_slice(arr, piece[None], (s, 0, 0))


def _take_shard(arr, kind, s):
    if kind == COLS:
        return lax.dynamic_slice(arr, (0, s * (arr.shape[1] // N_CHIPS)), (arr.shape[0], arr.shape[1] // N_CHIPS))
    return lax.dynamic_index_in_dim(arr, s, 0, keepdims=False)


def _w_in_pieces(cs, s1, nf):
    out = []
    for s in range(N_CHIPS):
        lo, hi = cs * s, cs * (s + 1)
        for a, b, forget in ((lo, min(hi, s1), False), (max(lo, s1), min(hi, s1 + nf), True), (max(lo, s1 + nf), hi, False)):
            if a < b:
                out.append((s, a - lo, b - lo, forget, a - s1 if forget else (a if a < s1 else a - nf)))
    return out


def _split_w_in(stacked, s1, nf):
    pieces = _w_in_pieces(stacked.shape[2], s1, nf)
    main = [stacked[s, :, a:b] for s, a, b, forget, _ in pieces if not forget]
    ff = [stacked[s, :, a:b] for s, a, b, forget, _ in pieces if forget]
    return jnp.concatenate(main, axis=1), jnp.concatenate(ff, axis=1)


def _join_w_in(g_main, g_ff, s1, nf):
    cs = (g_main.shape[1] + nf) // N_CHIPS
    shards = [[] for _ in range(N_CHIPS)]
    for s, a, b, forget, off in _w_in_pieces(cs, s1, nf):
        shards[s].append((g_ff if forget else g_main)[:, off:off + b - a])
    return jnp.stack([jnp.concatenate(p, axis=1) if len(p) > 1 else p[0] for p in shards])


SMALL = ("norm_mix_g", "norm_mem_g", "norm_ffn_g", "hgrn_lb_logits", "hgrn_norm_g", "fox_f_bias", "fox_q_norm_g",
         "fox_k_norm_g", "mem_q_norm_g", "mem_k_norm_g", "ffn_conv_b")


def _pack_small(vals):
    flats, total = [], 0
    for v in vals:
        flat = v.reshape(-1).astype(F32)
        n = -(-flat.shape[0] // FLAT_W)
        flats.append(jnp.pad(flat, (0, n * FLAT_W - flat.shape[0])))
        total += n
    if -total % 8:
        flats.append(jnp.zeros((-total % 8 * FLAT_W,), F32))
    return jnp.concatenate(flats).reshape(-1, FLAT_W)


def _unpack_small(buf, shapes):
    res, off = [], 0
    for shp in shapes:
        numel = math.prod(shp)
        n = -(-numel // FLAT_W)
        res.append(buf[off:off + n].reshape(-1)[:numel].reshape(shp))
        off += n
    return res


def _pad_lanes(v, width=LANE):
    return jnp.pad(v, ((0, 0), (0, width - v.shape[1])))


WEIGHTS = ("norm_mix_g", "norm_mem_g", "w_in", "hgrn_lb_logits", "hgrn_norm_g", "fox_f_bias", "fox_q_norm_g",
           "fox_k_norm_g", "mem_kv_w", "mem_q_norm_g", "mem_k_norm_g", "w_br_hgrn", "w_br_fox", "w_br_mem", "w_out",
           "norm_ffn_g", "ffn_w_up", "ffn_conv_w", "ffn_conv_b", "ffn_w_down")


def _local_step(x, mem, target, w, full, conv_w):
    b_, t_, d = x.shape
    n = b_ * t_
    hw, fw, mw = HG_HEADS * HG_D, FOX_HEADS * FOX_DH, MEM_HEADS * MEM_DH
    m_ = mem.shape[1]
    f = conv_w.shape[1]
    s1 = 4 * hw + 3 * fw
    fox_col, mem_col, gate_col = 4 * hw // LANE, s1 // LANE, (s1 + mw) // d

    w_main, w_ff = _split_w_in(full["w_in"], s1, FOX_HEADS)
    w_ff = _pad_lanes(w_ff)
    w_up = full["ffn_w_up"]
    w_brs = [full["w_br_hgrn"], full["w_br_fox"], full["w_br_mem"]]
    w_out, w_kv, w_down = full["w_out"], full["mem_kv_w"], full["ffn_w_down"]
    f_bias = _pad_lanes(w["fox_f_bias"])
    cb = w["ffn_conv_b"]

    x2 = x.reshape(n, d)
    h = _rmsnorm_fwd(x2, w["norm_mix_g"], name="norm_mix_fwd")
    zm = _matmul(h, w_main, name="in_proj")
    zf = _matmul(h, w_ff, name="in_proj_forget")
    zm3, zf3 = zm.reshape(b_, t_, -1), zf.reshape(b_, t_, LANE)
    ya = _hgrn_fwd(zm3, w["hgrn_lb_logits"], w["hgrn_norm_g"], hw)
    fc = _fox_prep(zf3, f_bias)
    fox_gq, fox_gk = jnp.tile(w["fox_q_norm_g"], (1, 2)), jnp.tile(w["fox_k_norm_g"], (1, 2))
    yb, lse = _fox_fwd(zm3, fc, fox_gq, fox_gk, fw, fox_col)
    mem2 = mem.reshape(b_ * m_, d)
    hm = _rmsnorm_fwd(mem2, w["norm_mem_g"], name="norm_mem_fwd")
    mkv = _matmul(hm, w_kv, name="mem_kv_proj").reshape(b_, m_, 2 * mw)
    yc = _mem_fwd(zm3, mkv, w["mem_q_norm_g"], w["mem_k_norm_g"], mw, mem_col)
    ys = [ya.reshape(n, hw), yb.reshape(n, fw), yc.reshape(n, mw)]
    x1, merged = _merge_fwd(x2, ys, zm, w_brs, w_out, gate_col)
    h2 = _rmsnorm_fwd(x1, w["norm_ffn_g"], name="norm_ffn_fwd")
    u = _matmul(h2, w_up, name="ffn_up")
    u3 = u.reshape(b_, t_, 2 * f)
    yff = _glu_fwd(u3, conv_w, cb).reshape(n, f)
    ffn = _matmul(yff, w_down, name="ffn_down")
    dy, loss_vec = _loss_head(x1, ffn, target.reshape(n, d))

    grads = {}
    dyff = _matmul(dy, w_down, tb=True, name="ffn_down_dx")
    grads["ffn_w_down"] = _matmul(yff, dy, ta=True, name="ffn_down_dw", tm=1408)
    du_a, du_v, grads["ffn_conv_w"], grads["ffn_conv_b"] = _glu_bwd(u3, dyff.reshape(b_, t_, f), conv_w, cb)
    du2 = jnp.concatenate([du_a, du_v], axis=-1).reshape(n, 2 * f)
    dh2 = _matmul(du2, w_up, tb=True, name="ffn_up_dx")
    grads["ffn_w_up"] = _matmul(h2, du2, ta=True, name="ffn_up_dw")
    dx1, grads["norm_ffn_g"] = _rmsnorm_bwd(x1, [dh2], w["norm_ffn_g"], dy, name="norm_ffn_bwd")

    dgl, dpa, dpb, dpc, dya, dyb, dyc = _merge_bwd(dx1, ys, zm, w_brs, w_out, gate_col)
    grads["w_out"] = _matmul(merged, dx1, ta=True, name="out_proj_dw")
    for nm, y_, dp_ in zip(("w_br_hgrn", "w_br_fox", "w_br_mem"), ys, (dpa, dpb, dpc)):
        grads[nm] = _matmul(y_, dp_, ta=True, name=nm + "_dw")

    dmq, dmk, dmv, grads["mem_q_norm_g"], grads["mem_k_norm_g"] = _mem_bwd(
        zm3, mkv, dyc.reshape(b_, t_, mw), w["mem_q_norm_g"], w["mem_k_norm_g"], mw, mem_col)
    dmkv = jnp.concatenate([dmk, dmv], axis=-1).reshape(b_ * m_, 2 * mw)
    grads["mem_kv_w"] = _matmul(hm, dmkv, ta=True, name="mem_kv_dw")
    dhm = _matmul(dmkv, w_kv, tb=True, name="mem_kv_dx")
    _, grads["norm_mem_g"] = _rmsnorm_bwd(mem2, [dhm], w["norm_mem_g"], None, name="norm_mem_bwd")

    dfq, dfk, dfv, dfc, g_fq, g_fk = _fox_bwd(zm3, yb, dyb.reshape(b_, t_, fw), lse, fc, fox_gq, fox_gk, fw, fox_col)
    grads["fox_q_norm_g"] = g_fq[:, :FOX_DH] + g_fq[:, FOX_DH:]
    grads["fox_k_norm_g"] = g_fk[:, :FOX_DH] + g_fk[:, FOX_DH:]
    dfc = dfc[..., :2].transpose(0, 2, 1, 3).reshape(b_, t_, FOX_HEADS)
    dfc = jnp.pad(dfc, ((0, 0), (0, 0), (0, LANE - FOX_HEADS)))
    dzf, g_fb = _fox_post(dfc, zf3, f_bias)
    grads["fox_f_bias"] = g_fb[:, :FOX_HEADS]

    dhq, dhf, dhi, dhg, grads["hgrn_lb_logits"], grads["hgrn_norm_g"] = _hgrn_bwd(
        zm3, dya.reshape(b_, t_, hw), w["hgrn_lb_logits"], w["hgrn_norm_g"], hw)

    dzm = jnp.concatenate([dhq, dhf, dhi, dhg, dfq, dfk, dfv, dmq, dgl.reshape(b_, t_, 3 * d)], axis=-1).reshape(n, -1)
    dzf2 = dzf.reshape(n, LANE)
    dh_a = _matmul(dzm, w_main, tb=True, name="in_proj_dx")
    dh_b = _matmul(dzf2, w_ff, tb=True, name="in_proj_forget_dx")
    g_main = _matmul(h, dzm, ta=True, name="in_proj_dw")
    g_ff = _matmul(h, dzf2, ta=True, name="in_proj_forget_dw")
    grads["w_in"] = _join_w_in(g_main, g_ff[:, :FOX_HEADS], s1, FOX_HEADS)
    grad_x, grads["norm_mix_g"] = _rmsnorm_bwd(x2, [dh_a, dh_b], w["norm_mix_g"], dx1, name="norm_mix_bwd")
    return loss_vec, grad_x.reshape(b_, t_, d), grads


def kernel(x, mem, norm_mix_g, norm_mem_g, w_in, hgrn_lb_logits, hgrn_norm_g, fox_f_bias, fox_q_norm_g, fox_k_norm_g, mem_kv_w, mem_q_norm_g, mem_k_norm_g, w_br_hgrn, w_br_fox, w_br_mem, w_out, norm_ffn_g, ffn_w_up, ffn_conv_w, ffn_conv_b, ffn_w_down, loss_target, m_norm_mix_g, m_norm_mem_g, m_w_in, m_hgrn_lb_logits, m_hgrn_norm_g, m_fox_f_bias, m_fox_q_norm_g, m_fox_k_norm_g, m_mem_kv_w, m_mem_q_norm_g, m_mem_k_norm_g, m_w_br_hgrn, m_w_br_fox, m_w_br_mem, m_w_out, m_norm_ffn_g, m_ffn_w_up, m_ffn_conv_w, m_ffn_conv_b, m_ffn_w_down, v_norm_mix_g, v_norm_mem_g, v_w_in, v_hgrn_lb_logits, v_hgrn_norm_g, v_fox_f_bias, v_fox_q_norm_g, v_fox_k_norm_g, v_mem_kv_w, v_mem_q_norm_g, v_mem_k_norm_g, v_w_br_hgrn, v_w_br_fox, v_w_br_mem, v_w_out, v_norm_ffn_g, v_ffn_w_up, v_ffn_conv_w, v_ffn_conv_b, v_ffn_w_down):
    w = dict(zip(WEIGHTS, (norm_mix_g, norm_mem_g, w_in, hgrn_lb_logits, hgrn_norm_g, fox_f_bias, fox_q_norm_g,
                           fox_k_norm_g, mem_kv_w, mem_q_norm_g, mem_k_norm_g, w_br_hgrn, w_br_fox, w_br_mem, w_out,
                           norm_ffn_g, ffn_w_up, ffn_conv_w, ffn_conv_b, ffn_w_down)))
    m = dict(zip(WEIGHTS, (m_norm_mix_g, m_norm_mem_g, m_w_in, m_hgrn_lb_logits, m_hgrn_norm_g, m_fox_f_bias,
                           m_fox_q_norm_g, m_fox_k_norm_g, m_mem_kv_w, m_mem_q_norm_g, m_mem_k_norm_g, m_w_br_hgrn,
                           m_w_br_fox, m_w_br_mem, m_w_out, m_norm_ffn_g, m_ffn_w_up, m_ffn_conv_w, m_ffn_conv_b,
                           m_ffn_w_down)))
    v = dict(zip(WEIGHTS, (v_norm_mix_g, v_norm_mem_g, v_w_in, v_hgrn_lb_logits, v_hgrn_norm_g, v_fox_f_bias,
                           v_fox_q_norm_g, v_fox_k_norm_g, v_mem_kv_w, v_mem_q_norm_g, v_mem_k_norm_g, v_w_br_hgrn,
                           v_w_br_fox, v_w_br_mem, v_w_out, v_norm_ffn_g, v_ffn_w_up, v_ffn_conv_w, v_ffn_conv_b,
                           v_ffn_w_down)))
    c_idx = lax.axis_index("c")
    chip = 2 * lax.axis_index("x") + lax.axis_index("y")

    kinds = [KIND[nm] for nm in BIG]
    mine = [w[nm][0].astype(MXU_DTYPE) for nm in BIG]
    full = {nm: _put_shard(g, k, chip, own) for nm, g, k, own in zip(BIG, _gather_shards(mine, kinds), kinds, mine)}
    for nm in ROW_SHARDED:
        full[nm] = full[nm].reshape(-1, full[nm].shape[2])
    cs = ffn_conv_w.shape[2]
    f = cs * N_CHIPS
    placed = lax.dynamic_update_slice(jnp.zeros((3, f), F32), ffn_conv_w[0] * (c_idx == 0).astype(F32), (0, chip * cs))
    conv_w = _unpack_small(_all_sum_small(_pack_small([placed]), "gather_conv_w"), [(3, f)])[0]

    loss_vec, grad_x, grads = _local_step(x, mem, loss_target, w, full, conv_w)

    gs = [grads[nm].reshape(N_CHIPS, -1, grads[nm].shape[1]) if nm in ROW_SHARDED else grads[nm] for nm in BIG]
    from_sibling = _pair_swap_halves(gs, kinds)
    c_arr = jnp.reshape(c_idx, (1,)).astype(jnp.int32)
    chip_partial = [_add_half(g, a, k, c_arr, "add_half_" + nm) for g, a, k, nm in zip(gs, from_sibling, kinds, BIG)]
    landed = [_put_shard(bq, STACK, chip, _take_shard(p, k, chip))
              for bq, p, k in zip(_chip_exchange(chip_partial, kinds), chip_partial, kinds)]
    reduced_half = [_sum_chips(bq, "sum_chips_" + nm) for bq, nm in zip(landed, BIG)]
    joined = [lax.dynamic_update_slice(o, q, (c_idx * q.shape[0], 0))
              for o, q in zip(_pair_join_halves(reduced_half), reduced_half)]
    gshards = dict(zip(BIG, joined))

    small_names = SMALL + ("ffn_conv_w",)
    summed = _unpack_small(
        _all_sum_small(_pack_small([grads[nm] for nm in small_names] + [loss_vec]), "all_sum_small_grads"),
        [grads[nm].shape for nm in small_names] + [loss_vec.shape])
    gsmall = dict(zip(small_names, summed[:-1]))
    loss = jnp.sum(summed[-1])
    g_out = {nm: gshards[nm][None] for nm in BIG}
    for nm in SMALL:
        g_out[nm] = gsmall[nm].reshape(w[nm].shape)
    g_out["ffn_conv_w"] = lax.dynamic_slice(gsmall["ffn_conv_w"], (0, chip * cs), (3, cs))[None]

    delta, new_m, new_v = {}, {}, {}
    for nm in BIG + ("ffn_conv_w",):
        delta[nm], new_m[nm], new_v[nm] = _adamw(w[nm], g_out[nm], m[nm], v[nm], name="adamw_" + nm)
    packed = [_pack_small([t[nm] for nm in SMALL])[None] for t in (w, g_out, m, v)]
    outs = _adamw(*packed, name="adamw_small")
    shapes = [w[nm].shape for nm in SMALL]
    for res, o in zip((delta, new_m, new_v), outs):
        res.update(zip(SMALL, _unpack_small(o[0], shapes)))

    return (loss, grad_x, *[g_out[nm] for nm in WEIGHTS], *[delta[nm] for nm in WEIGHTS],
            *[new_m[nm] for nm in WEIGHTS], *[new_v[nm] for nm in WEIGHTS])
```

```python
import functools
import math

import jax
import jax.numpy as jnp
from jax import lax
from jax.experimental import pallas as pl
from jax.experimental.pallas import tpu as pltpu

F32 = jnp.float32
BF16 = jnp.bfloat16
MXU_DTYPE = jnp.bfloat16
EXCHANGE_DTYPE = jnp.bfloat16

EPS = 1e-6
HG_HEADS, HG_D = 4, 128
FOX_HEADS, FOX_DH = 8, 64
MEM_HEADS, MEM_DH = 4, 128
HG_CHUNK = 64
FOX_BLOCK = 256
LANE = 128
FFN_GROUP = 256
FLAT_W = 1024
VMEM_LIMIT = 56 * 2 ** 20
NEG = -1e30
N_CHIPS = 4

ADAM_LR, ADAM_B1, ADAM_B2, ADAM_EPS, ADAM_WD, ADAM_STEP = 0.001, 0.9, 0.999, 1e-08, 0.01, 10

MESH = pl.DeviceIdType.MESH
ANY = pl.BlockSpec(memory_space=pl.ANY)


def _mx(x):
    return x.astype(MXU_DTYPE)


def _dot(a, b, ca, cb):
    return lax.dot_general(_mx(a), _mx(b), (((ca,), (cb,)), ((), ())), preferred_element_type=F32)


def _nn(a, b):
    return _dot(a, b, 1, 0)


def _nt(a, b):
    return _dot(a, b, 1, 1)


def _tn(a, b):
    return _dot(a, b, 0, 0)


def _dotp(a, b, ca, cb):
    return lax.dot_general(a, b, (((ca,), (cb,)), ((), ())), precision=lax.Precision.HIGHEST,
                           preferred_element_type=F32)


def _tri_dot(tri_bf, x):
    hi = x.astype(BF16)
    r = x - hi.astype(F32)
    mid = r.astype(BF16)
    lo = (r - mid.astype(F32)).astype(BF16)

    def d(v):
        return lax.dot_general(tri_bf, v, (((1,), (0,)), ((), ())), preferred_element_type=F32)

    return d(hi) + d(mid) + d(lo)


def _sig(x):
    return jax.nn.sigmoid(x)


def _erf(x):
    a = jnp.abs(x)
    t = 1.0 / (1.0 + 0.3275911 * a)
    poly = t * (0.254829592 + t * (-0.284496736 + t * (1.421413741 + t * (-1.453152027 + t * 1.061405429))))
    y = 1.0 - poly * jnp.exp(-a * a)
    return jnp.where(x < 0, -y, y)


def _tile(dim, pref, unit=LANE):
    if dim <= pref:
        return dim
    t = pref - pref % unit
    while t >= unit:
        if dim % t == 0:
            return t
        t -= unit
    return dim


def _params(n_grid):
    return pltpu.CompilerParams(dimension_semantics=("arbitrary",) * n_grid, vmem_limit_bytes=VMEM_LIMIT)


def _acc(ref, val, first):
    @pl.when(first)
    def _():
        ref[...] = val

    @pl.when(jnp.logical_not(first))
    def _():
        ref[...] += val


class _Rider:
    def __init__(self, inputs, out_shapes, scratch, start, finish):
        self.inputs, self.out_shapes, self.scratch, self.start, self.finish = inputs, out_shapes, scratch, start, finish


def _matmul(a, b, *, name, ta=False, tb=False, tm=1024, tn=2048, tk=None, rider=None):
    m, k = (a.shape[1], a.shape[0]) if ta else a.shape
    n = b.shape[0] if tb else b.shape[1]
    tk = tk or (1024 if ta else 2048)
    tm, tn, tk = _tile(m, tm), _tile(n, tn), _tile(k, tk)
    nk = k // tk
    grid = (m // tm, n // tn, nk)
    n_in = len(rider.inputs) if rider else 0
    n_out = len(rider.out_shapes) if rider else 0

    def body(a_ref, b_ref, *refs):
        o_ref = refs[n_in]
        if rider:
            r_in, r_out, r_scr = refs[:n_in], refs[n_in + 1:n_in + 1 + n_out], refs[n_in + 1 + n_out:]
            step = (pl.program_id(0) * grid[1] + pl.program_id(1)) * grid[2] + pl.program_id(2)

            @pl.when(step == 0)
            def _():
                rider.start(r_in, r_out, r_scr)

        p = _dot(a_ref[...], b_ref[...], 0 if ta else 1, 1 if tb else 0)
        if nk == 1:
            o_ref[...] = p
        else:
            _acc(o_ref, p, pl.program_id(2) == 0)
        if rider:
            @pl.when(step == grid[0] * grid[1] * grid[2] - 1)
            def _():
                rider.finish(r_in, r_out, r_scr)

    a_spec = pl.BlockSpec((tk, tm), lambda i, j, kk: (kk, i)) if ta else pl.BlockSpec((tm, tk), lambda i, j, kk: (i, kk))
    b_spec = pl.BlockSpec((tn, tk), lambda i, j, kk: (j, kk)) if tb else pl.BlockSpec((tk, tn), lambda i, j, kk: (kk, j))
    o_spec = pl.BlockSpec((tm, tn), lambda i, j, kk: (i, j))
    o_sds = jax.ShapeDtypeStruct((m, n), F32)
    if not rider:
        return pl.pallas_call(body, name=name, grid=grid, in_specs=[a_spec, b_spec], out_specs=o_spec, out_shape=o_sds,
                              compiler_params=_params(3))(a, b)
    outs = pl.pallas_call(
        body, name=name, grid=grid, in_specs=[a_spec, b_spec] + [ANY] * n_in, out_specs=[o_spec] + [ANY] * n_out,
        out_shape=[o_sds] + list(rider.out_shapes), scratch_shapes=list(rider.scratch), compiler_params=_params(3),
    )(a, b, *rider.inputs)
    return outs[0], list(outs[1:])


def _rmsnorm_fwd(x, g, *, name, tm=512):
    n, d = x.shape
    tm = _tile(n, tm, 8)

    def body(x_ref, g_ref, o_ref):
        xv = x_ref[...]
        r = lax.rsqrt(jnp.mean(xv * xv, axis=-1, keepdims=True) + EPS)
        o_ref[...] = (xv * r * g_ref[...]).astype(o_ref.dtype)

    return pl.pallas_call(
        body, name=name, grid=(n // tm,),
        in_specs=[pl.BlockSpec((tm, d), lambda i: (i, 0)), pl.BlockSpec((1, d), lambda i: (0, 0))],
        out_specs=pl.BlockSpec((tm, d), lambda i: (i, 0)),
        out_shape=jax.ShapeDtypeStruct((n, d), MXU_DTYPE),
        compiler_params=_params(1),
    )(x, g)


def _rmsnorm_bwd(x, dhs, g, res, *, name, tm=512):
    n, d = x.shape
    tm = _tile(n, tm, 8)
    n_dh = len(dhs)
    has_res = res is not None

    def body(*refs):
        x_ref, dh_refs, g_ref = refs[0], refs[1:1 + n_dh], refs[1 + n_dh]
        res_ref = refs[2 + n_dh] if has_res else None
        dx_ref, dg_ref = refs[-2], refs[-1]
        xv = x_ref[...]
        dh = dh_refs[0][...].astype(F32)
        for r_ in dh_refs[1:]:
            dh = dh + r_[...].astype(F32)
        r = lax.rsqrt(jnp.mean(xv * xv, axis=-1, keepdims=True) + EPS)
        dhg = dh * g_ref[...]
        dx = r * dhg - xv * (r * r * r) * jnp.mean(dhg * xv, axis=-1, keepdims=True)
        if has_res:
            dx = dx + res_ref[...]
        dx_ref[...] = dx
        _acc(dg_ref, jnp.sum(dh * xv * r, axis=0, keepdims=True), pl.program_id(0) == 0)

    row = pl.BlockSpec((tm, d), lambda i: (i, 0))
    vec = pl.BlockSpec((1, d), lambda i: (0, 0))
    ins = [x] + list(dhs) + [g] + ([res] if has_res else [])
    return pl.pallas_call(
        body, name=name, grid=(n // tm,),
        in_specs=[row] * (1 + n_dh) + [vec] + ([row] if has_res else []),
        out_specs=[row, vec],
        out_shape=[jax.ShapeDtypeStruct((n, d), F32), jax.ShapeDtypeStruct((1, d), F32)],
        compiler_params=_params(1),
    )(*ins)


def _adamw(w, g, m, v, *, name, tr=256):
    _, r, c = w.shape
    tr = _tile(r, tr, 8)
    c1 = 1.0 / (1.0 - ADAM_B1 ** ADAM_STEP)
    c2 = 1.0 / (1.0 - ADAM_B2 ** ADAM_STEP)

    def body(w_ref, g_ref, m_ref, v_ref, d_ref, mo_ref, vo_ref):
        gv = g_ref[...]
        mn = ADAM_B1 * m_ref[...] + (1.0 - ADAM_B1) * gv
        vn = ADAM_B2 * v_ref[...] + (1.0 - ADAM_B2) * (gv * gv)
        d_ref[...] = -ADAM_LR * ((mn * c1) / (jnp.sqrt(vn * c2) + ADAM_EPS) + ADAM_WD * w_ref[...])
        mo_ref[...] = mn
        vo_ref[...] = vn

    blk = pl.BlockSpec((1, tr, c), lambda i: (0, i, 0))
    sds = jax.ShapeDtypeStruct((1, r, c), F32)
    return pl.pallas_call(
        body, name=name, grid=(r // tr,), in_specs=[blk] * 4, out_specs=[blk] * 3, out_shape=[sds] * 3,
        compiler_params=_params(1),
    )(w, g, m, v)


def _bdot(a, b, ca, cb):
    return lax.dot_general(_mx(a), _mx(b), (((ca,), (cb,)), ((0,), (0,))), preferred_element_type=F32)


def _bdotp(a, b, ca, cb):
    return lax.dot_general(a, b, (((ca,), (cb,)), ((0,), (0,))), precision=lax.Precision.HIGHEST,
                           preferred_element_type=F32)


def _tri_dot_b(tri_bf, x):
    hi = x.astype(BF16)
    r = x - hi.astype(F32)
    mid = r.astype(BF16)
    lo = (r - mid.astype(F32)).astype(BF16)

    def d(v):
        return lax.dot_general(tri_bf, v, (((2,), (1,)), ((0,), (0,))), preferred_element_type=F32)

    return d(hi) + d(mid) + d(lo)


def _hgrn_forward(hq, hf, hi, lbv, tril, tril_bf):
    nc, c, _ = hq.shape
    sf = _sig(hf)
    f = lbv + (1.0 - lbv) * sf
    k = 1.0 - f
    gcum = _tri_dot_b(tril_bf, jnp.log(f))
    mid = gcum[:, c // 2 - 1:c // 2, :]
    glast = gcum[:, c - 1:c, :]
    sq = _sig(hq)
    q = hq * sq
    e_q = jnp.exp(gcum - mid)
    e_k = jnp.exp(mid - gcum)
    qe, ke = q * e_q, k * e_k
    a = jnp.where(tril, _bdot(qe, ke, 2, 2), 0.0)
    e_g = jnp.exp(gcum)
    qg = q * e_g
    e_s = jnp.exp(glast - gcum)
    kg = k * e_s
    e_l = jnp.exp(glast)
    upd = _bdot(hi, kg, 1, 1)
    st = jnp.zeros((HG_D, HG_D), F32)
    states = []
    for n in range(nc):
        states.append(st)
        st = st * e_l[n] + upd[n]
    st_all = jnp.stack(states)
    o = _bdot(a, hi, 2, 1) + _bdot(qg, st_all, 2, 2)
    return dict(sf=sf, f=f, k=k, sq=sq, q=q, e_q=e_q, e_k=e_k, qe=qe, ke=ke, a=a, e_g=e_g, qg=qg, o=o,
                e_s=e_s, kg=kg, e_l=e_l, st_all=st_all)


def _hgrn_specs(t_, hw):
    nb = hw // LANE

    def col(off):
        return pl.BlockSpec((1, t_, LANE), lambda h, b: (b, 0, off * nb + h))

    vec = pl.BlockSpec((2, LANE), lambda h, b: (0, h))
    one = pl.BlockSpec((1, LANE), lambda h, b: (0, 0))
    blk = pl.BlockSpec((1, t_, LANE), lambda h, b: (b, 0, h))
    return col, vec, one, blk


def _chunk_masks(nc, c):
    row = lax.broadcasted_iota(jnp.int32, (nc, c, c), 1)
    cl = lax.broadcasted_iota(jnp.int32, (nc, c, c), 2)
    return row >= cl, (row >= cl).astype(BF16), (row <= cl).astype(BF16)


def _hgrn_fwd(zm, lb, gn, hw):
    b_, t_, _ = zm.shape
    c = min(HG_CHUNK, t_)
    nc = t_ // c
    col, vec, one, blk = _hgrn_specs(t_, hw)

    def body(q_ref, f_ref, i_ref, g_ref, lb_ref, gn_ref, y_ref):
        lbv, gnv = _sig(lb_ref[0:1, :] - lb_ref[1:2, :]), gn_ref[...]
        tril, tril_bf, _ = _chunk_masks(nc, c)
        chunks = lambda ref: ref[0].reshape(nc, c, LANE)
        o = _hgrn_forward(chunks(q_ref), chunks(f_ref), chunks(i_ref), lbv, tril, tril_bf)["o"]
        r = lax.rsqrt(jnp.mean(o * o, axis=-1, keepdims=True) + EPS)
        hg = chunks(g_ref)
        y_ref[0] = (o * r * gnv * (hg * _sig(hg))).reshape(t_, LANE)

    return pl.pallas_call(
        body, name="hgrn_fwd", grid=(HG_HEADS, b_),
        in_specs=[col(0), col(1), col(2), col(3), vec, one], out_specs=blk,
        out_shape=jax.ShapeDtypeStruct((b_, t_, hw), F32),
        compiler_params=_params(2),
    )(zm, zm, zm, zm, lb, gn)


def _hgrn_bwd(zm, dy, lb, gn, hw):
    b_, t_, _ = zm.shape
    c = min(HG_CHUNK, t_)
    nc = t_ // c
    col, vec, one, blk = _hgrn_specs(t_, hw)

    def body(q_ref, f_ref, i_ref, g_ref, dy_ref, lb_ref, gn_ref, dq_ref, df_ref, di_ref, dg_ref, dlb_ref, dgn_ref):
        h, b = pl.program_id(0), pl.program_id(1)
        lbv, gnv = _sig(lb_ref[0:1, :] - lb_ref[1:2, :]), gn_ref[...]
        tril, tril_bf, triu_bf = _chunk_masks(nc, c)
        last_row = lax.broadcasted_iota(jnp.int32, (nc, c, LANE), 1) == c - 1
        chunks = lambda ref: ref[0].reshape(nc, c, LANE)
        flat = lambda x: x.reshape(t_, LANE)
        hq, hi, hg = chunks(q_ref), chunks(i_ref), chunks(g_ref)
        p = _hgrn_forward(hq, chunks(f_ref), hi, lbv, tril, tril_bf)
        o, q, k, st_all, e_l = p["o"], p["q"], p["k"], p["st_all"], p["e_l"]
        dyv = chunks(dy_ref)
        sg = _sig(hg)
        r = lax.rsqrt(jnp.mean(o * o, axis=-1, keepdims=True) + EPS)
        dn = dyv * (hg * sg)
        dg_ref[0] = flat(dyv * (o * r * gnv) * (sg * (1.0 + hg * (1.0 - sg)))).astype(dg_ref.dtype)
        dgn = jnp.sum(flat(dn * o * r), axis=0, keepdims=True)
        dng = dn * gnv
        do = r * dng - o * (r * r * r) * jnp.mean(dng * o, axis=-1, keepdims=True)
        back = _bdotp(do, p["qg"], 1, 1)
        dst = jnp.zeros((HG_D, HG_D), F32)
        dsts = [None] * nc
        for n in range(nc - 1, -1, -1):
            dsts[n] = dst
            dst = dst * e_l[n] + back[n]
        dst_all = jnp.stack(dsts)
        da = jnp.where(tril, _bdotp(do, hi, 2, 2), 0.0)
        dq = _bdotp(da, p["ke"], 2, 1) * p["e_q"] + _bdotp(do, st_all, 2, 1) * p["e_g"]
        dk_state = _bdotp(hi, dst_all, 2, 1) * p["e_s"]
        dk = _bdotp(da, p["qe"], 1, 1) * p["e_k"] + dk_state
        di_ref[0] = flat(_bdot(p["a"], do, 1, 1) + _bdot(p["kg"], dst_all, 2, 2)).astype(di_ref.dtype)
        extra = (jnp.sum(k * dk_state, axis=1, keepdims=True) + e_l * jnp.sum(st_all * dst_all, axis=1, keepdims=True))
        dgc = q * dq - k * dk + jnp.where(last_row, extra, 0.0)
        dfv = _tri_dot_b(triu_bf, dgc) / p["f"] - dk
        sf, sq = p["sf"], p["sq"]
        df_ref[0] = flat(dfv * (1.0 - lbv) * sf * (1.0 - sf)).astype(df_ref.dtype)
        dlb = jnp.sum(flat(dfv * (1.0 - sf)), axis=0, keepdims=True)
        dq_ref[0] = flat(dq * (sq * (1.0 + hq * (1.0 - sq)))).astype(dq_ref.dtype)
        dl0 = dlb * lbv * (1.0 - lbv)
        _acc(dlb_ref, jnp.concatenate([dl0, -dl0], axis=0), b == 0)
        _acc(dgn_ref, dgn, jnp.logical_and(b == 0, h == 0))

    sds = jax.ShapeDtypeStruct((b_, t_, hw), MXU_DTYPE)
    return pl.pallas_call(
        body, name="hgrn_bwd", grid=(HG_HEADS, b_),
        in_specs=[col(0), col(1), col(2), col(3), blk, vec, one],
        out_specs=[blk, blk, blk, blk, vec, one],
        out_shape=[sds, sds, sds, sds, jax.ShapeDtypeStruct((2, hw), F32), jax.ShapeDtypeStruct((1, LANE), F32)],
        compiler_params=_params(2),
    )(zm, zm, zm, zm, dy, lb, gn)


def _fox_logf(x):
    return jnp.minimum(x, 0.0) - jnp.log(1.0 + jnp.exp(-jnp.abs(x)))


def _fox_prep(zf, bias):
    b_, t_, _ = zf.shape
    tb = min(FOX_BLOCK, t_)
    nb = t_ // tb

    def body(z_ref, b_ref, fc_ref):
        tril_bf = (lax.broadcasted_iota(jnp.int32, (tb, tb), 0) >= lax.broadcasted_iota(jnp.int32, (tb, tb), 1)).astype(BF16)
        bv = b_ref[...]

        def blk(i, carry):
            rows = pl.ds(pl.multiple_of(i * tb, tb), tb)
            fc = _tri_dot(tril_bf, _fox_logf(z_ref[0, rows, :] + bv)) + carry
            fc_ref[0, rows, :] = fc
            return fc[tb - 1:tb, :]

        lax.fori_loop(0, nb, blk, jnp.zeros((1, LANE), F32))

    blk_spec = pl.BlockSpec((1, t_, LANE), lambda b: (b, 0, 0))
    return pl.pallas_call(
        body, name="fox_prep", grid=(b_,),
        in_specs=[blk_spec, pl.BlockSpec((1, LANE), lambda b: (0, 0))], out_specs=blk_spec,
        out_shape=jax.ShapeDtypeStruct((b_, t_, LANE), F32), compiler_params=_params(1),
    )(zf, bias)


def _fox_post(dfc, zf, bias):
    b_, t_, _ = zf.shape
    tb = min(FOX_BLOCK, t_)
    nb = t_ // tb

    def body(d_ref, z_ref, b_ref, dz_ref, db_ref):
        triu_bf = (lax.broadcasted_iota(jnp.int32, (tb, tb), 0) <= lax.broadcasted_iota(jnp.int32, (tb, tb), 1)).astype(BF16)
        valid = lax.broadcasted_iota(jnp.int32, (tb, LANE), 1) < FOX_HEADS
        bv = b_ref[...]

        def blk(m, carry):
            tail, db = carry
            rows = pl.ds(pl.multiple_of((nb - 1 - m) * tb, tb), tb)
            dlf = _tri_dot(triu_bf, d_ref[0, rows, :]) + tail
            dx = jnp.where(valid, dlf * _sig(-(z_ref[0, rows, :] + bv)), 0.0)
            dz_ref[0, rows, :] = dx.astype(dz_ref.dtype)
            return dlf[0:1, :], db + jnp.sum(dx, axis=0, keepdims=True)

        z1 = jnp.zeros((1, LANE), F32)
        _, db = lax.fori_loop(0, nb, blk, (z1, z1))
        _acc(db_ref, db, pl.program_id(0) == 0)

    blk_spec = pl.BlockSpec((1, t_, LANE), lambda b: (b, 0, 0))
    vec = pl.BlockSpec((1, LANE), lambda b: (0, 0))
    return pl.pallas_call(
        body, name="fox_post", grid=(b_,), in_specs=[blk_spec, blk_spec, vec], out_specs=[blk_spec, vec],
        out_shape=[jax.ShapeDtypeStruct((b_, t_, LANE), MXU_DTYPE), jax.ShapeDtypeStruct((1, LANE), F32)],
        compiler_params=_params(1),
    )(dfc, zf, bias)


FOX_TILE = 128
FOX_BAND = 512
AUG = 64


def _head_mean_matrix():
    r = lax.broadcasted_iota(jnp.int32, (LANE, LANE), 0) // FOX_DH
    c = lax.broadcasted_iota(jnp.int32, (LANE, LANE), 1) // FOX_DH
    return (r == c).astype(BF16)


def _dot_right_exact(x, m_bf):
    hi = x.astype(BF16)
    r = x - hi.astype(F32)
    mid = r.astype(BF16)
    lo = (r - mid.astype(F32)).astype(BF16)

    def d(v):
        return lax.dot_general(v, m_bf, (((1,), (0,)), ((), ())), preferred_element_type=F32)

    return d(hi) + d(mid) + d(lo)


def _pair_norm(x, g2, bd):
    r = lax.rsqrt(_dot_right_exact(x * x, bd) * (1.0 / FOX_DH) + EPS)
    return x * r * g2, r


def _pair_norm_bwd(x, r, dy, g2, bd):
    dyg = dy * g2
    dx = r * dyg - x * (r * r * r) * (_dot_right_exact(dyg * x, bd) * (1.0 / FOX_DH))
    return dx, jnp.sum(dy * x * r, axis=0, keepdims=True)


def _head_lanes(xn, hh):
    return xn if hh == 0 else pltpu.roll(xn, FOX_DH, 1)


def _split3(x):
    hi = x.astype(BF16).astype(F32)
    mid = (x - hi).astype(BF16).astype(F32)
    return hi, mid, x - hi - mid


def _fox_operands(q_ref, k_ref, v_ref, fc_ref, gq2, gk2, p, qa, ka, va):
    t_ = q_ref.shape[1]
    bd = _head_mean_matrix()
    lane = lax.broadcasted_iota(jnp.int32, (t_, LANE), 1)
    qx, kx = q_ref[0], k_ref[0]
    qn, rq = _pair_norm(qx, gq2, bd)
    kn, rk = _pair_norm(kx, gk2, bd)
    vv = v_ref[0]
    q_aug = jnp.where(jnp.logical_and(lane >= AUG, lane < AUG + 3), 1.0, 0.0)
    for hh in range(2):
        fcol = jnp.sum(jnp.where(lane == 2 * p + hh, fc_ref[0], 0.0), axis=-1, keepdims=True)
        hi, mid, lo = _split3(-fcol)
        k_aug = jnp.where(lane == AUG, hi, jnp.where(lane == AUG + 1, mid, jnp.where(lane == AUG + 2, lo,
                          jnp.where(lane == AUG + 3, 1.0, 0.0))))
        head = lane < FOX_DH
        qa[hh] = jnp.where(head, _head_lanes(qn, hh), q_aug).astype(MXU_DTYPE)
        ka[hh] = jnp.where(head, _head_lanes(kn, hh), k_aug).astype(MXU_DTYPE)
        va[hh] = jnp.where(head, _head_lanes(vv, hh), 0.0).astype(MXU_DTYPE)
    return bd, lane, qx, kx, rq, rk


def _fox_specs(t_, fw, col0):
    npair = fw // LANE

    def col(off):
        return pl.BlockSpec((1, t_, LANE), lambda b, p: (b, 0, col0 + off * npair + p))

    pair = pl.BlockSpec((1, t_, LANE), lambda b, p: (b, 0, p))
    full = pl.BlockSpec((1, t_, LANE), lambda b, p: (b, 0, 0))
    gvec = pl.BlockSpec((1, LANE), lambda b, p: (0, 0))
    lse = pl.BlockSpec((1, 1, t_, LANE), lambda b, p: (b, p, 0, 0))
    return col, pair, full, gvec, lse


def _fox_fwd(zm, fc, gq2, gk2, fw, col0):
    b_, t_, _ = zm.shape
    npair = fw // LANE
    tq = min(FOX_TILE, t_)
    bw = min(FOX_BAND, t_)
    nband, tpb = t_ // bw, bw // tq
    scale = FOX_DH ** -0.5
    col, pair, full, gvec, lse_spec = _fox_specs(t_, fw, col0)

    def body(q_ref, k_ref, v_ref, fc_ref, gq_ref, gk_ref, o_ref, lse_ref, qa, ka, va):
        p = pl.program_id(1)
        _fox_operands(q_ref, k_ref, v_ref, fc_ref, gq_ref[...] * scale, gk_ref[...], p, qa, ka, va)
        ri = lax.broadcasted_iota(jnp.int32, (tq, bw), 0)
        ci = lax.broadcasted_iota(jnp.int32, (tq, bw), 1)
        lane = lax.broadcasted_iota(jnp.int32, (tq, LANE), 1)

        for band in range(nband):
            c0 = band * bw

            def qtile(ii, _, c0=c0):
                r0 = pl.multiple_of(c0 + ii * tq, tq)
                rows = pl.ds(r0, tq)
                keep = c0 + ci <= r0 + ri
                res = []
                for hh in range(2):
                    qb = qa[hh, rows, :]
                    s_b = jnp.where(keep, _nt(qb, ka[hh, c0:c0 + bw, :]), NEG)
                    m = jnp.max(s_b, axis=-1, keepdims=True)
                    if c0:
                        s_a = _nt(qb, ka[hh, 0:c0, :])
                        m = jnp.maximum(m, jnp.max(s_a, axis=-1, keepdims=True))
                    p_b = jnp.exp(s_b - m)
                    l = jnp.sum(p_b, axis=-1, keepdims=True)
                    acc = _nn(p_b, va[hh, c0:c0 + bw, :])
                    if c0:
                        p_a = jnp.exp(s_a - m)
                        l = l + jnp.sum(p_a, axis=-1, keepdims=True)
                        acc = acc + _nn(p_a, va[hh, 0:c0, :])
                    res.append((acc / l, m + jnp.log(l)))
                (o0, e0), (o1, e1) = res
                o_ref[0, rows, :] = jnp.where(lane < FOX_DH, o0, pltpu.roll(o1, FOX_DH, 1))
                lse_ref[0, 0, rows, :] = jnp.where(lane == 0, e0, jnp.where(lane == 1, e1, 0.0))
                return 0

            lax.fori_loop(0, tpb, qtile, 0)

    return pl.pallas_call(
        body, name="fox_fwd", grid=(b_, npair),
        in_specs=[col(0), col(1), col(2), full, gvec, gvec],
        out_specs=[pair, lse_spec],
        out_shape=[jax.ShapeDtypeStruct((b_, t_, fw), F32), jax.ShapeDtypeStruct((b_, npair, t_, LANE), F32)],
        scratch_shapes=[pltpu.VMEM((2, t_, LANE), MXU_DTYPE)] * 3,
        compiler_params=_params(2),
    )(zm, zm, zm, fc, gq2, gk2)


def _norm_bwd(x, dy, g):
    r = lax.rsqrt(jnp.mean(x * x, axis=-1, keepdims=True) + EPS)
    dyg = dy * g
    dx = r * dyg - x * (r * r * r) * jnp.mean(dyg * x, axis=-1, keepdims=True)
    return dx, jnp.sum(dy * x * r, axis=0, keepdims=True)


def _fox_bwd(zm, o, do, lse, fc, gq2, gk2, fw, col0):
    b_, t_, _ = zm.shape
    npair = fw // LANE
    tq = min(FOX_TILE, t_)
    nb = t_ // tq
    bw = min(FOX_BAND, t_)
    nband, tpb = t_ // bw, bw // tq
    scale = FOX_DH ** -0.5
    col, pair, full, gvec, lse_spec = _fox_specs(t_, fw, col0)

    def body(q_ref, k_ref, v_ref, o_ref, do_ref, lse_ref, fc_ref, gq_ref, gk_ref,
             dq_ref, dk_ref, dv_ref, dfc_ref, dgq_ref, dgk_ref, qa, ka, va, da, rowv, dq_acc, dk_acc, dv_acc):
        b, p = pl.program_id(0), pl.program_id(1)
        gq2v, gk2v = gq_ref[...] * scale, gk_ref[...]
        bd, lane, qx, kx, rq, rk = _fox_operands(q_ref, k_ref, v_ref, fc_ref, gq2v, gk2v, p, qa, ka, va)
        head = lane < FOX_DH
        dov = do_ref[0]
        dsum = _dot_right_exact(dov * o_ref[0], bd)
        eye = (lax.broadcasted_iota(jnp.int32, (tq, tq), 0) == lax.broadcasted_iota(jnp.int32, (tq, tq), 1)).astype(F32)
        for hh in range(2):
            da[hh] = jnp.where(head, _head_lanes(dov, hh), 0.0).astype(MXU_DTYPE)
            for blk in range(nb):
                rs = slice(blk * tq, (blk + 1) * tq)
                rowv[2 * hh:2 * hh + 1, rs] = jnp.sum(eye * lse_ref[0, 0, rs, hh:hh + 1], axis=0, keepdims=True)
                rowv[2 * hh + 1:2 * hh + 2, rs] = jnp.sum(eye * dsum[rs, hh * FOX_DH:hh * FOX_DH + 1], axis=0, keepdims=True)
        dq_acc[...] = jnp.zeros(dq_acc.shape, F32)
        ri = lax.broadcasted_iota(jnp.int32, (tq, bw), 0)
        ci = lax.broadcasted_iota(jnp.int32, (tq, bw), 1)

        def part(hh, kb, vb, lo, hi, keep):
            qm, dm = qa[hh, lo:hi, :], da[hh, lo:hi, :]
            pt = jnp.exp(_nt(kb, qm) - rowv[2 * hh:2 * hh + 1, lo:hi])
            if keep is not None:
                pt = jnp.where(keep, pt, 0.0)
            dst = pt * (_nt(vb, dm) - rowv[2 * hh + 1:2 * hh + 2, lo:hi])
            dq_acc[hh, lo:hi, :] += _tn(dst, kb)
            return _nn(dst, qm), _nn(pt, dm)

        for band in range(nband):
            c0 = band * bw

            def kvtile(jj, _, c0=c0):
                r0 = pl.multiple_of(c0 + jj * tq, tq)
                rows = pl.ds(r0, tq)
                keep = c0 + ci >= r0 + ri
                for hh in range(2):
                    kb, vb = ka[hh, rows, :], va[hh, rows, :]
                    dk_t, dv_t = part(hh, kb, vb, c0, c0 + bw, keep)
                    if c0 + bw < t_:
                        dk_u, dv_u = part(hh, kb, vb, c0 + bw, t_, None)
                        dk_t, dv_t = dk_t + dk_u, dv_t + dv_u
                    dk_acc[hh, rows, :] = dk_t
                    dv_acc[hh, rows, :] = dv_t
                return 0

            lax.fori_loop(0, tpb, kvtile, 0)

        dq0, dq1, dk0, dk1 = dq_acc[0], dq_acc[1], dk_acc[0], dk_acc[1]
        dqn = jnp.where(head, dq0, pltpu.roll(dq1, FOX_DH, 1))
        dkn = jnp.where(head, dk0, pltpu.roll(dk1, FOX_DH, 1))
        dqx, gq_part = _pair_norm_bwd(qx, rq, dqn, gq2v, bd)
        dkx, gk_part = _pair_norm_bwd(kx, rk, dkn, gk2v, bd)
        dq_ref[0] = dqx.astype(dq_ref.dtype)
        dk_ref[0] = dkx.astype(dk_ref.dtype)
        dv_ref[0] = jnp.where(head, dv_acc[0], pltpu.roll(dv_acc[1], FOX_DH, 1)).astype(dv_ref.dtype)

        def bias_grad(dqh, dkh):
            return (jnp.sum(jnp.where(lane == AUG + 3, dqh, 0.0), axis=-1, keepdims=True)
                    - jnp.sum(jnp.where(lane == AUG, dkh, 0.0), axis=-1, keepdims=True))

        dfc_ref[0, 0] = jnp.where(lane == 0, bias_grad(dq0, dk0), jnp.where(lane == 1, bias_grad(dq1, dk1), 0.0))
        first = jnp.logical_and(b == 0, p == 0)
        _acc(dgq_ref, gq_part * scale, first)
        _acc(dgk_ref, gk_part, first)

    sds = jax.ShapeDtypeStruct((b_, t_, fw), MXU_DTYPE)
    gs = jax.ShapeDtypeStruct((1, LANE), F32)
    return pl.pallas_call(
        body, name="fox_bwd", grid=(b_, npair),
        in_specs=[col(0), col(1), col(2), pair, pair, lse_spec, full, gvec, gvec],
        out_specs=[pair, pair, pair, lse_spec, gvec, gvec],
        out_shape=[sds, sds, sds, jax.ShapeDtypeStruct((b_, npair, t_, LANE), F32), gs, gs],
        scratch_shapes=[pltpu.VMEM((2, t_, LANE), MXU_DTYPE)] * 4
        + [pltpu.VMEM((8, t_), F32)] + [pltpu.VMEM((2, t_, LANE), F32)] * 3,
        compiler_params=_params(2),
    )(zm, zm, zm, o, do, lse, fc, gq2, gk2)


def _mem_specs(t_, m_, mw, col0):
    nh = mw // LANE
    qcol = pl.BlockSpec((1, t_, LANE), lambda b, h: (b, 0, col0 + h))
    kcol = pl.BlockSpec((1, m_, LANE), lambda b, h: (b, 0, h))
    vcol = pl.BlockSpec((1, m_, LANE), lambda b, h: (b, 0, nh + h))
    ycol = pl.BlockSpec((1, t_, LANE), lambda b, h: (b, 0, h))
    gvec = pl.BlockSpec((1, LANE), lambda b, h: (0, 0))
    return qcol, kcol, vcol, ycol, gvec


def _mem_fwd(zm, mkv, gq, gk, mw, col0):
    b_, t_, _ = zm.shape
    m_ = mkv.shape[1]
    tq = min(512, t_)
    nb = t_ // tq
    scale = MEM_DH ** -0.5
    qcol, kcol, vcol, ycol, gvec = _mem_specs(t_, m_, mw, col0)

    def body(q_ref, k_ref, v_ref, gq_ref, gk_ref, y_ref):
        gqv, gkv = gq_ref[...] * scale, gk_ref[...]
        kv = k_ref[0]
        kn = _mx(kv * lax.rsqrt(jnp.mean(kv * kv, axis=-1, keepdims=True) + EPS) * gkv)
        vv = _mx(v_ref[0])

        def blk(i, _):
            rows = pl.ds(pl.multiple_of(i * tq, tq), tq)
            qv = q_ref[0, rows, :]
            s = _nt(qv * lax.rsqrt(jnp.mean(qv * qv, axis=-1, keepdims=True) + EPS) * gqv, kn)
            e = jnp.exp(s - jnp.max(s, axis=-1, keepdims=True))
            y_ref[0, rows, :] = _nn(e / jnp.sum(e, axis=-1, keepdims=True), vv)
            return 0

        lax.fori_loop(0, nb, blk, 0)

    return pl.pallas_call(
        body, name="mem_fwd", grid=(b_, MEM_HEADS), in_specs=[qcol, kcol, vcol, gvec, gvec], out_specs=ycol,
        out_shape=jax.ShapeDtypeStruct((b_, t_, mw), F32), compiler_params=_params(2),
    )(zm, mkv, mkv, gq, gk)


def _mem_bwd(zm, mkv, dy, gq, gk, mw, col0):
    b_, t_, _ = zm.shape
    m_ = mkv.shape[1]
    tq = min(512, t_)
    nb = t_ // tq
    scale = MEM_DH ** -0.5
    qcol, kcol, vcol, ycol, gvec = _mem_specs(t_, m_, mw, col0)

    def body(q_ref, k_ref, v_ref, dy_ref, gq_ref, gk_ref, dq_ref, dk_ref, dv_ref, dgq_ref, dgk_ref):
        gqv, gkv = gq_ref[...] * scale, gk_ref[...]
        kv = k_ref[0]
        kn = _mx(kv * lax.rsqrt(jnp.mean(kv * kv, axis=-1, keepdims=True) + EPS) * gkv)
        vv = _mx(v_ref[0])

        def blk(i, carry):
            dkn, dvv, dgq = carry
            rows = pl.ds(pl.multiple_of(i * tq, tq), tq)
            qv = q_ref[0, rows, :]
            qn = _mx(qv * lax.rsqrt(jnp.mean(qv * qv, axis=-1, keepdims=True) + EPS) * gqv)
            s = _nt(qn, kn)
            e = jnp.exp(s - jnp.max(s, axis=-1, keepdims=True))
            pm = e / jnp.sum(e, axis=-1, keepdims=True)
            dob = _mx(dy_ref[0, rows, :])
            dp = _nt(dob, vv)
            ds = pm * (dp - jnp.sum(dp * pm, axis=-1, keepdims=True))
            dqv, gq_part = _norm_bwd(qv, _nn(ds, kn), gqv)
            dq_ref[0, rows, :] = dqv.astype(dq_ref.dtype)
            return dkn + _tn(ds, qn), dvv + _tn(pm, dob), dgq + gq_part * scale

        z = jnp.zeros((m_, LANE), F32)
        dkn, dvv, dgq = lax.fori_loop(0, nb, blk, (z, z, jnp.zeros((1, LANE), F32)))
        dkv, dgk = _norm_bwd(kv, dkn, gkv)
        dk_ref[0] = dkv
        dv_ref[0] = dvv
        first = jnp.logical_and(pl.program_id(0) == 0, pl.program_id(1) == 0)
        _acc(dgq_ref, dgq, first)
        _acc(dgk_ref, dgk, first)

    kblk = pl.BlockSpec((1, m_, LANE), lambda b, h: (b, 0, h))
    gs = jax.ShapeDtypeStruct((1, LANE), F32)
    ks = jax.ShapeDtypeStruct((b_, m_, mw), F32)
    return pl.pallas_call(
        body, name="mem_bwd", grid=(b_, MEM_HEADS), in_specs=[qcol, kcol, vcol, ycol, gvec, gvec],
        out_specs=[ycol, kblk, kblk, gvec, gvec],
        out_shape=[jax.ShapeDtypeStruct((b_, t_, mw), MXU_DTYPE), ks, ks, gs, gs], compiler_params=_params(2),
    )(zm, mkv, mkv, dy, gq, gk)


def _merge_specs(tm, d, w, gcol):
    row_d = pl.BlockSpec((tm, d), lambda i: (i, 0))
    row_w = pl.BlockSpec((tm, w), lambda i: (i, 0))
    gates = [pl.BlockSpec((tm, d), functools.partial(lambda i, k: (i, gcol + k), k=k)) for k in range(3)]
    w_br = pl.BlockSpec((w, d), lambda i: (0, 0))
    w_o = pl.BlockSpec((d, d), lambda i: (0, 0))
    return row_d, row_w, gates, w_br, w_o


def _merge_fwd(x, ys, zm, w_brs, w_out, gcol, tm=256):
    n, d = x.shape
    w = ys[0].shape[1]
    tm = _tile(n, tm, 8)
    row_d, row_w, gates, w_br, w_o = _merge_specs(tm, d, w, gcol)

    def body(x_ref, ya, yb, yc, g0, g1, g2, wa, wb, wc, wo, x1_ref, mg_ref):
        mg = (_sig(g0[...]) * _nn(ya[...], wa[...]) + _sig(g1[...]) * _nn(yb[...], wb[...])
              + _sig(g2[...]) * _nn(yc[...], wc[...]))
        mg_ref[...] = mg.astype(mg_ref.dtype)
        x1_ref[...] = x_ref[...] + _nn(mg, wo[...])

    return pl.pallas_call(
        body, name="merge_fwd", grid=(n // tm,),
        in_specs=[row_d, row_w, row_w, row_w] + gates + [w_br, w_br, w_br, w_o],
        out_specs=[row_d, row_d],
        out_shape=[jax.ShapeDtypeStruct((n, d), F32), jax.ShapeDtypeStruct((n, d), MXU_DTYPE)],
        compiler_params=_params(1),
    )(x, *ys, zm, zm, zm, *w_brs, w_out)


def _merge_bwd(dx1, ys, zm, w_brs, w_out, gcol, tm=256):
    n, d = dx1.shape
    w = ys[0].shape[1]
    tm = _tile(n, tm, 8)
    row_d, row_w, gates, w_br, w_o = _merge_specs(tm, d, w, gcol)

    def body(dx_ref, ya, yb, yc, g0, g1, g2, wa, wb, wc, wo, dgl_ref, dpa, dpb, dpc, dya, dyb, dyc):
        dm = _nt(dx_ref[...], wo[...])
        for k, (y, g, wr, dp_ref, dy_ref) in enumerate(((ya, g0, wa, dpa, dya), (yb, g1, wb, dpb, dyb),
                                                        (yc, g2, wc, dpc, dyc))):
            sg = _sig(g[...])
            pr = _nn(y[...], wr[...])
            dgl_ref[:, k * d:(k + 1) * d] = (dm * pr * sg * (1.0 - sg)).astype(dgl_ref.dtype)
            dp = (dm * sg).astype(dp_ref.dtype)
            dp_ref[...] = dp
            dy_ref[...] = _nt(dp, wr[...])

    sd = jax.ShapeDtypeStruct((n, d), MXU_DTYPE)
    sw = jax.ShapeDtypeStruct((n, w), F32)
    return pl.pallas_call(
        body, name="merge_bwd", grid=(n // tm,),
        in_specs=[row_d, row_w, row_w, row_w] + gates + [w_br, w_br, w_br, w_o],
        out_specs=[pl.BlockSpec((tm, 3 * d), lambda i: (i, 0)), row_d, row_d, row_d, row_w, row_w, row_w],
        out_shape=[jax.ShapeDtypeStruct((n, 3 * d), MXU_DTYPE), sd, sd, sd, sw, sw, sw],
        compiler_params=_params(1),
    )(dx1, *ys, zm, zm, zm, *w_brs, w_out)


CONV_ROWS = 256
HALO = 8


def _ext(ref, r0, t_):
    rc = min(CONV_ROWS, t_)
    a, b = max(r0 - HALO, 0), min(r0 + rc + HALO, t_)
    parts = []
    if r0 - HALO < 0:
        parts.append(jnp.zeros((HALO, ref.shape[2]), F32))
    parts.append(ref[0, a:b, :].astype(F32))
    if r0 + rc + HALO > t_:
        parts.append(jnp.zeros((HALO, ref.shape[2]), F32))
    return jnp.concatenate(parts, axis=0) if len(parts) > 1 else parts[0]


def _gelu_parts(ac):
    cdf = 0.5 * (1.0 + _erf(ac * (2.0 ** -0.5)))
    pdf = jnp.exp(-0.5 * ac * ac) * ((2.0 * math.pi) ** -0.5)
    return cdf, pdf


def _conv_taps(a_ext, cw, cb):
    return cw[0:1, :] * pltpu.roll(a_ext, 2, 0) + cw[1:2, :] * pltpu.roll(a_ext, 1, 0) + cw[2:3, :] * a_ext + cb


def _glu_specs(t_, f, g):
    gate = pl.BlockSpec((1, t_, g), lambda j, b: (b, 0, j))
    value = pl.BlockSpec((1, t_, g), lambda j, b: (b, 0, f // g + j))
    cwb = pl.BlockSpec((3, g), lambda j, b: (0, j))
    cbb = pl.BlockSpec((1, g), lambda j, b: (0, j))
    return gate, value, cwb, cbb


def _glu_fwd(u, cw, cb):
    b_, t_, f2 = u.shape
    f = f2 // 2
    g = min(FFN_GROUP, f)
    rc = min(CONV_ROWS, t_)
    gate, value, cwb, cbb = _glu_specs(t_, f, g)

    def body(a_ref, v_ref, cw_ref, cb_ref, y_ref):
        cwv, cbv = cw_ref[...], cb_ref[...]
        for r0 in range(0, t_, rc):
            ac = _conv_taps(_ext(a_ref, r0, t_), cwv, cbv)[HALO:HALO + rc]
            cdf, _ = _gelu_parts(ac)
            y_ref[0, r0:r0 + rc, :] = (ac * cdf * v_ref[0, r0:r0 + rc, :]).astype(y_ref.dtype)

    return pl.pallas_call(
        body, name="glu_fwd", grid=(f // g, b_), in_specs=[gate, value, cwb, cbb], out_specs=gate,
        out_shape=jax.ShapeDtypeStruct((b_, t_, f), MXU_DTYPE), compiler_params=_params(2),
    )(u, u, cw, cb)


def _glu_bwd(u, dy, cw, cb):
    b_, t_, f2 = u.shape
    f = f2 // 2
    g = min(FFN_GROUP, f)
    rc = min(CONV_ROWS, t_)
    ne = rc + 2 * HALO
    gate, value, cwb, cbb = _glu_specs(t_, f, g)

    def body(a_ref, v_ref, dy_ref, cw_ref, cb_ref, da_ref, dv_ref, dcw_ref, dcb_ref):
        cwv, cbv = cw_ref[...], cb_ref[...]
        dcw = [jnp.zeros((1, g), F32) for _ in range(3)]
        dcb = jnp.zeros((1, g), F32)
        for r0 in range(0, t_, rc):
            a_ext, v_ext, dy_ext = _ext(a_ref, r0, t_), _ext(v_ref, r0, t_), _ext(dy_ref, r0, t_)
            ac = _conv_taps(a_ext, cwv, cbv)
            cdf, pdf = _gelu_parts(ac)
            dac = dy_ext * v_ext * (cdf + ac * pdf)
            da = cwv[2:3, :] * dac + cwv[1:2, :] * pltpu.roll(dac, ne - 1, 0) + cwv[0:1, :] * pltpu.roll(dac, ne - 2, 0)
            mid = slice(HALO, HALO + rc)
            da_ref[0, r0:r0 + rc, :] = da[mid].astype(da_ref.dtype)
            dv_ref[0, r0:r0 + rc, :] = (dy_ext[mid] * ac[mid] * cdf[mid]).astype(dv_ref.dtype)
            dacm = dac[mid]
            dcw[0] = dcw[0] + jnp.sum(dacm * pltpu.roll(a_ext, 2, 0)[mid], axis=0, keepdims=True)
            dcw[1] = dcw[1] + jnp.sum(dacm * pltpu.roll(a_ext, 1, 0)[mid], axis=0, keepdims=True)
            dcw[2] = dcw[2] + jnp.sum(dacm * a_ext[mid], axis=0, keepdims=True)
            dcb = dcb + jnp.sum(dacm, axis=0, keepdims=True)
        first = pl.program_id(1) == 0
        _acc(dcw_ref, jnp.concatenate(dcw, axis=0), first)
        _acc(dcb_ref, dcb, first)

    sds = jax.ShapeDtypeStruct((b_, t_, f), MXU_DTYPE)
    return pl.pallas_call(
        body, name="glu_bwd", grid=(f // g, b_), in_specs=[gate, value, gate, cwb, cbb],
        out_specs=[gate, gate, cwb, cbb],
        out_shape=[sds, sds, jax.ShapeDtypeStruct((3, f), F32), jax.ShapeDtypeStruct((1, f), F32)],
        compiler_params=_params(2),
    )(u, u, dy, cw, cb)


def _loss_head(x1, ffn, target, tm=512):
    n, d = x1.shape
    tm = _tile(n, tm, 8)

    def body(x_ref, f_ref, t_ref, dy_ref, l_ref):
        err = x_ref[...] + f_ref[...] - t_ref[...]
        dy_ref[...] = err * (1.0 / d)
        _acc(l_ref, jnp.sum(err * err, axis=0, keepdims=True) * (0.5 / d), pl.program_id(0) == 0)

    row = pl.BlockSpec((tm, d), lambda i: (i, 0))
    vec = pl.BlockSpec((1, d), lambda i: (0, 0))
    return pl.pallas_call(
        body, name="loss_head", grid=(n // tm,), in_specs=[row, row, row], out_specs=[row, vec],
        out_shape=[jax.ShapeDtypeStruct((n, d), F32), jax.ShapeDtypeStruct((1, d), F32)], compiler_params=_params(1),
    )(x1, ffn, target)


def _place():
    x, y, c = lax.axis_index("x"), lax.axis_index("y"), lax.axis_index("c")
    chips = [(1 - x, y), (x, 1 - y), (1 - x, 1 - y)]
    return x, y, c, chips


def _remote(src, dst, send_sem, recv_sem, to):
    return pltpu.make_async_remote_copy(src_ref=src, dst_ref=dst, send_sem=send_sem, recv_sem=recv_sem,
                                        device_id=to, device_id_type=MESH)


STACK, COLS = "stack", "cols"


def _shard_ref(ref, kind, s, rows, c):
    if kind == COLS:
        cols = pl.ds(pl.multiple_of(s * c, LANE), c)
        return ref.at[:, cols] if rows is None else ref.at[rows, cols]
    return ref.at[s] if rows is None else ref.at[s, rows, :]


def _halves(c, half):
    mine = pl.ds(pl.multiple_of(c * half, 16), half)
    theirs = pl.ds(pl.multiple_of((1 - c) * half, 16), half)
    return mine, theirs


def _gather_parts(kinds):
    def first_copies(ins, outs, sems):
        x, y, c, chips = _place()
        me = 2 * x + y
        cps = []
        for i, (w_ref, o_ref, kind) in enumerate(zip(ins, outs, kinds)):
            r, cw = w_ref.shape
            mine, _ = _halves(c, r // 2)
            for j, chip in enumerate(chips):
                cps.append(_remote(w_ref.at[mine], _shard_ref(o_ref, kind, me, mine, cw), sems[0].at[6 * i + j],
                                   sems[1].at[6 * i + j], (*chip, c)))
        return cps

    def start(ins, outs, sems):
        for cp in first_copies(ins, outs, sems):
            cp.start()

    def finish(ins, outs, sems):
        x, y, c, chips = _place()
        sib = (x, y, 1 - c)
        passed = []
        for i, (w_ref, o_ref, kind) in enumerate(zip(ins, outs, kinds)):
            r, cw = w_ref.shape
            mine, _ = _halves(c, r // 2)
            for j, (px, py) in enumerate(chips):
                blk = _shard_ref(o_ref, kind, 2 * px + py, mine, cw)
                _remote(blk, blk, sems[0].at[6 * i + j], sems[1].at[6 * i + j], sib).wait_recv()
                passed.append(_remote(blk, blk, sems[0].at[6 * i + 3 + j], sems[1].at[6 * i + 3 + j], sib))
                passed[-1].start()
        for i, (w_ref, o_ref, kind) in enumerate(zip(ins, outs, kinds)):
            r, cw = w_ref.shape
            _, theirs = _halves(c, r // 2)
            for j, (px, py) in enumerate(chips):
                blk = _shard_ref(o_ref, kind, 2 * px + py, theirs, cw)
                _remote(blk, blk, sems[0].at[6 * i + 3 + j], sems[1].at[6 * i + 3 + j], sib).wait_recv()
        for cp in first_copies(ins, outs, sems) + passed:
            cp.wait_send()

    return start, finish


def _gather_shapes(shards, kinds):
    return [jax.ShapeDtypeStruct((a.shape[0], N_CHIPS * a.shape[1]) if k == COLS else (N_CHIPS,) + a.shape, a.dtype)
            for a, k in zip(shards, kinds)]


def _gather_sems(nw):
    return [pltpu.SemaphoreType.DMA((6 * nw,)), pltpu.SemaphoreType.DMA((6 * nw,))]


def _gather_shards(shards, kinds):
    nw = len(shards)
    start, finish = _gather_parts(kinds)

    def body(*refs):
        ins, outs, sems = refs[:nw], refs[nw:2 * nw], refs[2 * nw:]
        start(ins, outs, sems)
        finish(ins, outs, sems)

    return pl.pallas_call(
        body, name="gather_shards", in_specs=[ANY] * nw, out_specs=[ANY] * nw,
        out_shape=_gather_shapes(shards, kinds), scratch_shapes=_gather_sems(nw),
    )(*shards)


def _gather_rider(shards, kinds):
    start, finish = _gather_parts(kinds)
    return _Rider(list(shards), _gather_shapes(shards, kinds), _gather_sems(len(shards)), start, finish)


def _half_shape(g, kind):
    if kind == COLS:
        return (g.shape[0] // 2, g.shape[1])
    return (g.shape[0], g.shape[1] // 2, g.shape[2])


def _pair_swap_halves(gs, kinds, name):
    nw = len(gs)

    def body(*refs):
        ins, outs = refs[:nw], refs[nw:2 * nw]
        send_sems, recv_sems = refs[2 * nw:]
        x, y, c, _ = _place()
        cps = []
        for i, (g_ref, a_ref, kind) in enumerate(zip(ins, outs, kinds)):
            r = g_ref.shape[0] if kind == COLS else g_ref.shape[1]
            _, theirs = _halves(c, r // 2)
            src = g_ref.at[theirs] if kind == COLS else g_ref.at[:, theirs]
            cps.append(_remote(src, a_ref, send_sems.at[i], recv_sems.at[i], (x, y, 1 - c)))
            cps[-1].start()
        for cp in cps:
            cp.wait()

    return pl.pallas_call(
        body, name=name, in_specs=[ANY] * nw, out_specs=[ANY] * nw,
        out_shape=[jax.ShapeDtypeStruct(_half_shape(g, k), g.dtype) for g, k in zip(gs, kinds)],
        scratch_shapes=[pltpu.SemaphoreType.DMA((nw,)), pltpu.SemaphoreType.DMA((nw,))],
    )(*gs)


def _row_tile(rows, width, itemsize=4, target=2 ** 21):
    return _tile(rows, max(8, target // (width * itemsize)), 8)


def _add_half(g, a, kind, c_idx, name):
    if kind == COLS:
        half, wd = a.shape
        tr = _row_tile(half, wd)
        nblk = half // tr
        grid = (nblk,)
        g_spec = pl.BlockSpec((tr, wd), lambda i, c_ref: (c_ref[0] * nblk + i, 0))
        a_spec = pl.BlockSpec((tr, wd), lambda i, c_ref: (i, 0))
    else:
        n, half, wd = a.shape
        tr = _row_tile(half, wd)
        nblk = half // tr
        grid = (n, nblk)
        g_spec = pl.BlockSpec((1, tr, wd), lambda s, i, c_ref: (s, c_ref[0] * nblk + i, 0))
        a_spec = pl.BlockSpec((1, tr, wd), lambda s, i, c_ref: (s, i, 0))

    def body(c_ref, g_ref, a_ref, o_ref):
        o_ref[...] = (g_ref[...] + a_ref[...]).astype(o_ref.dtype)

    return pl.pallas_call(
        body, name=name,
        grid_spec=pltpu.PrefetchScalarGridSpec(num_scalar_prefetch=1, grid=grid, in_specs=[g_spec, a_spec],
                                               out_specs=a_spec),
        out_shape=jax.ShapeDtypeStruct(a.shape, EXCHANGE_DTYPE), compiler_params=_params(len(grid)),
    )(c_idx, g, a)


def _exchange_parts(kinds):
    def copies(ins, outs, sems):
        x, y, c, chips = _place()
        me = 2 * x + y
        cps = []
        for i, (p_ref, b_ref, kind) in enumerate(zip(ins, outs, kinds)):
            cw = b_ref.shape[2]
            for j, (px, py) in enumerate(chips):
                cps.append(_remote(_shard_ref(p_ref, kind, 2 * px + py, None, cw), b_ref.at[me],
                                   sems[0].at[3 * i + j], sems[1].at[3 * i + j], (px, py, c)))
        return cps

    def start(ins, outs, sems):
        for cp in copies(ins, outs, sems):
            cp.start()

    def finish(ins, outs, sems):
        x, y, c, chips = _place()
        for i, b_ref in enumerate(outs):
            for j, (px, py) in enumerate(chips):
                blk = b_ref.at[2 * px + py]
                _remote(blk, blk, sems[0].at[3 * i + j], sems[1].at[3 * i + j], (px, py, c)).wait_recv()
        for cp in copies(ins, outs, sems):
            cp.wait_send()

    return start, finish


def _exchange_shapes(ps, kinds):
    return [jax.ShapeDtypeStruct((N_CHIPS,) + ((p.shape[0], p.shape[1] // N_CHIPS) if k == COLS else tuple(p.shape[1:])),
                                 p.dtype) for p, k in zip(ps, kinds)]


def _exchange_sems(nw):
    return [pltpu.SemaphoreType.DMA((3 * nw,)), pltpu.SemaphoreType.DMA((3 * nw,))]


def _chip_exchange(ps, kinds):
    nw = len(ps)
    start, finish = _exchange_parts(kinds)

    def body(*refs):
        ins, outs, sems = refs[:nw], refs[nw:2 * nw], refs[2 * nw:]
        start(ins, outs, sems)
        finish(ins, outs, sems)

    return pl.pallas_call(
        body, name="chip_exchange", in_specs=[ANY] * nw, out_specs=[ANY] * nw,
        out_shape=_exchange_shapes(ps, kinds), scratch_shapes=_exchange_sems(nw),
    )(*ps)


def _exchange_rider(ps, kinds):
    start, finish = _exchange_parts(kinds)
    return _Rider(list(ps), _exchange_shapes(ps, kinds), _exchange_sems(len(ps)), start, finish)


def _sum_chips(bq, name):
    n, h, wd = bq.shape
    tr = _row_tile(h, wd * n)

    def body(b_ref, o_ref):
        acc = b_ref[0].astype(F32)
        for s in range(1, n):
            acc = acc + b_ref[s].astype(F32)
        o_ref[...] = acc

    return pl.pallas_call(
        body, name=name, grid=(h // tr,),
        in_specs=[pl.BlockSpec((n, tr, wd), lambda i: (0, i, 0))], out_specs=pl.BlockSpec((tr, wd), lambda i: (i, 0)),
        out_shape=jax.ShapeDtypeStruct((h, wd), F32), compiler_params=_params(1),
    )(bq)


def _pair_join_halves(qs):
    nw = len(qs)

    def body(*refs):
        ins, outs = refs[:nw], refs[nw:2 * nw]
        send_sems, recv_sems = refs[2 * nw:]
        x, y, c, _ = _place()
        sent = []
        for i, (q_ref, o_ref) in enumerate(zip(ins, outs)):
            mine, _ = _halves(c, q_ref.shape[0])
            sent.append(_remote(q_ref, o_ref.at[mine], send_sems.at[i], recv_sems.at[i], (x, y, 1 - c)))
            sent[-1].start()
        for i, (q_ref, o_ref) in enumerate(zip(ins, outs)):
            _, theirs = _halves(c, q_ref.shape[0])
            _remote(q_ref, o_ref.at[theirs], send_sems.at[i], recv_sems.at[i], (x, y, 1 - c)).wait_recv()
        for cp in sent:
            cp.wait_send()

    return pl.pallas_call(
        body, name="pair_join_halves", in_specs=[ANY] * nw, out_specs=[ANY] * nw,
        out_shape=[jax.ShapeDtypeStruct((2 * q.shape[0], q.shape[1]), q.dtype) for q in qs],
        scratch_shapes=[pltpu.SemaphoreType.DMA((nw,)), pltpu.SemaphoreType.DMA((nw,))],
    )(*qs)


def _all_sum_small(s, name):
    sr, w = s.shape

    def body(s_ref, o_ref, buf, send_sems, recv_sems):
        x, y, c, _ = _place()
        me = 4 * x + 2 * y + c
        buf[me] = s_ref[...]
        peers = []
        for k in range(1, 8):
            px = 1 - x if k & 4 else x
            py = 1 - y if k & 2 else y
            pc = 1 - c if k & 1 else c
            peers.append((px, py, pc))
        sent = [_remote(s_ref, buf.at[me], send_sems.at[k], recv_sems.at[k], peer) for k, peer in enumerate(peers)]
        for cp in sent:
            cp.start()
        for k, (px, py, pc) in enumerate(peers):
            _remote(s_ref, buf.at[4 * px + 2 * py + pc], send_sems.at[k], recv_sems.at[k], (px, py, pc)).wait_recv()
        for cp in sent:
            cp.wait_send()
        acc = buf[0]
        for d in range(1, 8):
            acc = acc + buf[d]
        o_ref[...] = acc

    vm = pl.BlockSpec(memory_space=pltpu.VMEM)
    return pl.pallas_call(
        body, name=name, in_specs=[vm], out_specs=vm, out_shape=jax.ShapeDtypeStruct((sr, w), F32),
        scratch_shapes=[pltpu.VMEM((8, sr, w), F32), pltpu.SemaphoreType.DMA((7,)), pltpu.SemaphoreType.DMA((7,))],
    )(s)


BIG = ("w_in", "mem_kv_w", "w_br_hgrn", "w_br_fox", "w_br_mem", "w_out", "ffn_w_up", "ffn_w_down")
KIND = {"w_in": STACK, "mem_kv_w": STACK, "w_br_hgrn": COLS, "w_br_fox": COLS, "w_br_mem": COLS, "w_out": STACK,
        "ffn_w_up": COLS, "ffn_w_down": STACK}
ROW_SHARDED = ("mem_kv_w", "w_out", "ffn_w_down")
FIRST = ("w_in",)
REST = tuple(nm for nm in BIG if nm not in FIRST)
LAST = ("w_in",)


def _put_shard(arr, kind, s, piece):
    if kind == COLS:
        return lax.dynamic_update_slice(arr, piece, (0, s * piece.shape[1]))
    return lax.dynamic_update_slice(arr, piece[None], (s, 0, 0))


def _take_shard(arr, kind, s):
    if kind == COLS:
        return lax.dynamic_slice(arr, (0, s * (arr.shape[1] // N_CHIPS)), (arr.shape[0], arr.shape[1] // N_CHIPS))
    return lax.dynamic_index_in_dim(arr, s, 0, keepdims=False)


def _w_in_pieces(cs, s1, nf):
    out = []
    for s in range(N_CHIPS):
        lo, hi = cs * s, cs * (s + 1)
        for a, b, forget in ((lo, min(hi, s1), False), (max(lo, s1), min(hi, s1 + nf), True), (max(lo, s1 + nf), hi, False)):
            if a < b:
                out.append((s, a - lo, b - lo, forget, a - s1 if forget else (a if a < s1 else a - nf)))
    return out


def _split_w_in(stacked, s1, nf):
    pieces = _w_in_pieces(stacked.shape[2], s1, nf)
    main = [stacked[s, :, a:b] for s, a, b, forget, _ in pieces if not forget]
    ff = [stacked[s, :, a:b] for s, a, b, forget, _ in pieces if forget]
    return jnp.concatenate(main, axis=1), jnp.concatenate(ff, axis=1)


def _join_w_in(g_main, g_ff, s1, nf):
    cs = (g_main.shape[1] + nf) // N_CHIPS
    shards = [[] for _ in range(N_CHIPS)]
    for s, a, b, forget, off in _w_in_pieces(cs, s1, nf):
        shards[s].append((g_ff if forget else g_main)[:, off:off + b - a])
    return jnp.stack([jnp.concatenate(p, axis=1) if len(p) > 1 else p[0] for p in shards])


SMALL = ("norm_mix_g", "norm_mem_g", "norm_ffn_g", "hgrn_lb_logits", "hgrn_norm_g", "fox_f_bias", "fox_q_norm_g",
         "fox_k_norm_g", "mem_q_norm_g", "mem_k_norm_g", "ffn_conv_b")


def _pack_small(vals):
    flats, total = [], 0
    for v in vals:
        flat = v.reshape(-1).astype(F32)
        n = -(-flat.shape[0] // FLAT_W)
        flats.append(jnp.pad(flat, (0, n * FLAT_W - flat.shape[0])))
        total += n
    if -total % 8:
        flats.append(jnp.zeros((-total % 8 * FLAT_W,), F32))
    return jnp.concatenate(flats).reshape(-1, FLAT_W)


def _unpack_small(buf, shapes):
    res, off = [], 0
    for shp in shapes:
        numel = math.prod(shp)
        n = -(-numel // FLAT_W)
        res.append(buf[off:off + n].reshape(-1)[:numel].reshape(shp))
        off += n
    return res


def _pad_lanes(v, width=LANE):
    return jnp.pad(v, ((0, 0), (0, width - v.shape[1])))


WEIGHTS = ("norm_mix_g", "norm_mem_g", "w_in", "hgrn_lb_logits", "hgrn_norm_g", "fox_f_bias", "fox_q_norm_g",
           "fox_k_norm_g", "mem_kv_w", "mem_q_norm_g", "mem_k_norm_g", "w_br_hgrn", "w_br_fox", "w_br_mem", "w_out",
           "norm_ffn_g", "ffn_w_up", "ffn_conv_w", "ffn_conv_b", "ffn_w_down")


def _local_step(x, mem, target, w, full, conv_w, late=None, early=None):
    b_, t_, d = x.shape
    n = b_ * t_
    hw, fw, mw = HG_HEADS * HG_D, FOX_HEADS * FOX_DH, MEM_HEADS * MEM_DH
    m_ = mem.shape[1]
    f = conv_w.shape[1]
    s1 = 4 * hw + 3 * fw
    fox_col, mem_col, gate_col = 4 * hw // LANE, s1 // LANE, (s1 + mw) // d

    w_main, w_ff = _split_w_in(full["w_in"], s1, FOX_HEADS)
    w_ff = _pad_lanes(w_ff)
    f_bias = _pad_lanes(w["fox_f_bias"])
    cb = w["ffn_conv_b"]

    x2 = x.reshape(n, d)
    h = _rmsnorm_fwd(x2, w["norm_mix_g"], name="norm_mix_fwd")
    if late:
        zm, gathered = _matmul(h, w_main, name="in_proj", rider=_gather_rider(late[0], late[1]))
        full = {**full, **late[2](gathered)}
    else:
        zm = _matmul(h, w_main, name="in_proj")
    w_up = full["ffn_w_up"]
    w_brs = [full["w_br_hgrn"], full["w_br_fox"], full["w_br_mem"]]
    w_out, w_kv, w_down = full["w_out"], full["mem_kv_w"], full["ffn_w_down"]
    zf = _matmul(h, w_ff, name="in_proj_forget")
    zm3, zf3 = zm.reshape(b_, t_, -1), zf.reshape(b_, t_, LANE)
    ya = _hgrn_fwd(zm3, w["hgrn_lb_logits"], w["hgrn_norm_g"], hw)
    fc = _fox_prep(zf3, f_bias)
    fox_gq, fox_gk = jnp.tile(w["fox_q_norm_g"], (1, 2)), jnp.tile(w["fox_k_norm_g"], (1, 2))
    yb, lse = _fox_fwd(zm3, fc, fox_gq, fox_gk, fw, fox_col)
    mem2 = mem.reshape(b_ * m_, d)
    hm = _rmsnorm_fwd(mem2, w["norm_mem_g"], name="norm_mem_fwd")
    mkv = _matmul(hm, w_kv, name="mem_kv_proj").reshape(b_, m_, 2 * mw)
    yc = _mem_fwd(zm3, mkv, w["mem_q_norm_g"], w["mem_k_norm_g"], mw, mem_col)
    ys = [ya.reshape(n, hw), yb.reshape(n, fw), yc.reshape(n, mw)]
    x1, merged = _merge_fwd(x2, ys, zm, w_brs, w_out, gate_col)
    h2 = _rmsnorm_fwd(x1, w["norm_ffn_g"], name="norm_ffn_fwd")
    u = _matmul(h2, w_up, name="ffn_up")
    u3 = u.reshape(b_, t_, 2 * f)
    yff = _glu_fwd(u3, conv_w, cb).reshape(n, f)
    ffn = _matmul(yff, w_down, name="ffn_down")
    dy, loss_vec = _loss_head(x1, ffn, target.reshape(n, d))

    grads = {}
    dyff = _matmul(dy, w_down, tb=True, name="ffn_down_dx")
    grads["ffn_w_down"] = _matmul(yff, dy, ta=True, name="ffn_down_dw", tm=1408)
    du_a, du_v, grads["ffn_conv_w"], grads["ffn_conv_b"] = _glu_bwd(u3, dyff.reshape(b_, t_, f), conv_w, cb)
    du2 = jnp.concatenate([du_a, du_v], axis=-1).reshape(n, 2 * f)
    dh2 = _matmul(du2, w_up, tb=True, name="ffn_up_dx")
    grads["ffn_w_up"] = _matmul(h2, du2, ta=True, name="ffn_up_dw")
    dx1, grads["norm_ffn_g"] = _rmsnorm_bwd(x1, [dh2], w["norm_ffn_g"], dy, name="norm_ffn_bwd")

    dgl, dpa, dpb, dpc, dya, dyb, dyc = _merge_bwd(dx1, ys, zm, w_brs, w_out, gate_col)
    grads["w_out"] = _matmul(merged, dx1, ta=True, name="out_proj_dw")
    for nm, y_, dp_ in zip(("w_br_hgrn", "w_br_fox", "w_br_mem"), ys, (dpa, dpb, dpc)):
        grads[nm] = _matmul(y_, dp_, ta=True, name=nm + "_dw")

    dmq, dmk, dmv, grads["mem_q_norm_g"], grads["mem_k_norm_g"] = _mem_bwd(
        zm3, mkv, dyc.reshape(b_, t_, mw), w["mem_q_norm_g"], w["mem_k_norm_g"], mw, mem_col)
    dmkv = jnp.concatenate([dmk, dmv], axis=-1).reshape(b_ * m_, 2 * mw)
    grads["mem_kv_w"] = _matmul(hm, dmkv, ta=True, name="mem_kv_dw")
    dhm = _matmul(dmkv, w_kv, tb=True, name="mem_kv_dx")
    _, grads["norm_mem_g"] = _rmsnorm_bwd(mem2, [dhm], w["norm_mem_g"], None, name="norm_mem_bwd")

    dfq, dfk, dfv, dfc, g_fq, g_fk = _fox_bwd(zm3, yb, dyb.reshape(b_, t_, fw), lse, fc, fox_gq, fox_gk, fw, fox_col)
    grads["fox_q_norm_g"] = g_fq[:, :FOX_DH] + g_fq[:, FOX_DH:]
    grads["fox_k_norm_g"] = g_fk[:, :FOX_DH] + g_fk[:, FOX_DH:]
    dfc = dfc[..., :2].transpose(0, 2, 1, 3).reshape(b_, t_, FOX_HEADS)
    dfc = jnp.pad(dfc, ((0, 0), (0, 0), (0, LANE - FOX_HEADS)))
    dzf, g_fb = _fox_post(dfc, zf3, f_bias)
    grads["fox_f_bias"] = g_fb[:, :FOX_HEADS]

    dhq, dhf, dhi, dhg, grads["hgrn_lb_logits"], grads["hgrn_norm_g"] = _hgrn_bwd(
        zm3, dya.reshape(b_, t_, hw), w["hgrn_lb_logits"], w["hgrn_norm_g"], hw)

    dzm = jnp.concatenate([dhq, dhf, dhi, dhg, dfq, dfk, dfv, dmq, dgl.reshape(b_, t_, 3 * d)], axis=-1).reshape(n, -1)
    dzf2 = dzf.reshape(n, LANE)
    after = None
    if early:
        rider, after_fn = early(grads)
        dh_a, landed = _matmul(dzm, w_main, tb=True, name="in_proj_dx", rider=rider)
        after = after_fn(landed)
    else:
        dh_a = _matmul(dzm, w_main, tb=True, name="in_proj_dx")
    dh_b = _matmul(dzf2, w_ff, tb=True, name="in_proj_forget_dx")
    g_main = _matmul(h, dzm, ta=True, name="in_proj_dw")
    g_ff = _matmul(h, dzf2, ta=True, name="in_proj_forget_dw")
    grads["w_in"] = _join_w_in(g_main, g_ff[:, :FOX_HEADS], s1, FOX_HEADS)
    grad_x, grads["norm_mix_g"] = _rmsnorm_bwd(x2, [dh_a, dh_b], w["norm_mix_g"], dx1, name="norm_mix_bwd")
    return loss_vec, grad_x.reshape(b_, t_, d), grads, after


def kernel(x, mem, norm_mix_g, norm_mem_g, w_in, hgrn_lb_logits, hgrn_norm_g, fox_f_bias, fox_q_norm_g, fox_k_norm_g, mem_kv_w, mem_q_norm_g, mem_k_norm_g, w_br_hgrn, w_br_fox, w_br_mem, w_out, norm_ffn_g, ffn_w_up, ffn_conv_w, ffn_conv_b, ffn_w_down, loss_target, m_norm_mix_g, m_norm_mem_g, m_w_in, m_hgrn_lb_logits, m_hgrn_norm_g, m_fox_f_bias, m_fox_q_norm_g, m_fox_k_norm_g, m_mem_kv_w, m_mem_q_norm_g, m_mem_k_norm_g, m_w_br_hgrn, m_w_br_fox, m_w_br_mem, m_w_out, m_norm_ffn_g, m_ffn_w_up, m_ffn_conv_w, m_ffn_conv_b, m_ffn_w_down, v_norm_mix_g, v_norm_mem_g, v_w_in, v_hgrn_lb_logits, v_hgrn_norm_g, v_fox_f_bias, v_fox_q_norm_g, v_fox_k_norm_g, v_mem_kv_w, v_mem_q_norm_g, v_mem_k_norm_g, v_w_br_hgrn, v_w_br_fox, v_w_br_mem, v_w_out, v_norm_ffn_g, v_ffn_w_up, v_ffn_conv_w, v_ffn_conv_b, v_ffn_w_down):
    w = dict(zip(WEIGHTS, (norm_mix_g, norm_mem_g, w_in, hgrn_lb_logits, hgrn_norm_g, fox_f_bias, fox_q_norm_g,
                           fox_k_norm_g, mem_kv_w, mem_q_norm_g, mem_k_norm_g, w_br_hgrn, w_br_fox, w_br_mem, w_out,
                           norm_ffn_g, ffn_w_up, ffn_conv_w, ffn_conv_b, ffn_w_down)))
    m = dict(zip(WEIGHTS, (m_norm_mix_g, m_norm_mem_g, m_w_in, m_hgrn_lb_logits, m_hgrn_norm_g, m_fox_f_bias,
                           m_fox_q_norm_g, m_fox_k_norm_g, m_mem_kv_w, m_mem_q_norm_g, m_mem_k_norm_g, m_w_br_hgrn,
                           m_w_br_fox, m_w_br_mem, m_w_out, m_norm_ffn_g, m_ffn_w_up, m_ffn_conv_w, m_ffn_conv_b,
                           m_ffn_w_down)))
    v = dict(zip(WEIGHTS, (v_norm_mix_g, v_norm_mem_g, v_w_in, v_hgrn_lb_logits, v_hgrn_norm_g, v_fox_f_bias,
                           v_fox_q_norm_g, v_fox_k_norm_g, v_mem_kv_w, v_mem_q_norm_g, v_mem_k_norm_g, v_w_br_hgrn,
                           v_w_br_fox, v_w_br_mem, v_w_out, v_norm_ffn_g, v_ffn_w_up, v_ffn_conv_w, v_ffn_conv_b,
                           v_ffn_w_down)))
    c_idx = lax.axis_index("c")
    chip = 2 * lax.axis_index("x") + lax.axis_index("y")

    mine = {nm: w[nm][0].astype(MXU_DTYPE) for nm in BIG}

    def gathered_full(names, arrays):
        out = {nm: _put_shard(g, KIND[nm], chip, mine[nm]) for nm, g in zip(names, arrays)}
        return {nm: g.reshape(-1, g.shape[2]) if nm in ROW_SHARDED else g for nm, g in out.items()}

    full = gathered_full(FIRST, _gather_shards([mine[nm] for nm in FIRST], [KIND[nm] for nm in FIRST]))
    late = ([mine[nm] for nm in REST], [KIND[nm] for nm in REST], lambda arrays: gathered_full(REST, arrays))
    cs = ffn_conv_w.shape[2]
    f = cs * N_CHIPS
    placed = lax.dynamic_update_slice(jnp.zeros((3, f), F32), ffn_conv_w[0] * (c_idx == 0).astype(F32), (0, chip * cs))
    conv_w = _unpack_small(_all_sum_small(_pack_small([placed]), "gather_conv_w"), [(3, f)])[0]

    c_arr = jnp.reshape(c_idx, (1,)).astype(jnp.int32)

    def stacked(nm, g):
        return g.reshape(N_CHIPS, -1, g.shape[1]) if nm in ROW_SHARDED else g

    def chip_partials(names, grads, tag):
        gs = [stacked(nm, grads[nm]) for nm in names]
        kinds = [KIND[nm] for nm in names]
        from_sibling = _pair_swap_halves(gs, kinds, "pair_swap_halves_" + tag)
        return [_add_half(g, a, k, c_arr, "add_half_" + nm) for g, a, k, nm in zip(gs, from_sibling, kinds, names)], kinds

    def with_own(landed, partial, kinds):
        return [_put_shard(bq, STACK, chip, _take_shard(p, k, chip)) for bq, p, k in zip(landed, partial, kinds)]

    def early(grads):
        partial, kinds = chip_partials(REST, grads, "rest")
        return _exchange_rider(partial, kinds), lambda landed: with_own(landed, partial, kinds)

    loss_vec, grad_x, grads, landed_rest = _local_step(x, mem, loss_target, w, full, conv_w, late, early)

    partial, kinds = chip_partials(LAST, grads, "last")
    landed = dict(zip(LAST + REST, with_own(_chip_exchange(partial, kinds), partial, kinds) + landed_rest))
    reduced_half = [_sum_chips(landed[nm], "sum_chips_" + nm) for nm in BIG]
    joined = [lax.dynamic_update_slice(o, q, (c_idx * q.shape[0], 0))
              for o, q in zip(_pair_join_halves(reduced_half), reduced_half)]
    gshards = dict(zip(BIG, joined))

    small_names = SMALL + ("ffn_conv_w",)
    summed = _unpack_small(
        _all_sum_small(_pack_small([grads[nm] for nm in small_names] + [loss_vec]), "all_sum_small_grads"),
        [grads[nm].shape for nm in small_names] + [loss_vec.shape])
    gsmall = dict(zip(small_names, summed[:-1]))
    loss = jnp.sum(summed[-1])
    g_out = {nm: gshards[nm][None] for nm in BIG}
    for nm in SMALL:
        g_out[nm] = gsmall[nm].reshape(w[nm].shape)
    g_out["ffn_conv_w"] = lax.dynamic_slice(gsmall["ffn_conv_w"], (0, chip * cs), (3, cs))[None]

    delta, new_m, new_v = {}, {}, {}
    for nm in BIG + ("ffn_conv_w",):
        delta[nm], new_m[nm], new_v[nm] = _adamw(w[nm], g_out[nm], m[nm], v[nm], name="adamw_" + nm)
    packed = [_pack_small([t[nm] for nm in SMALL])[None] for t in (w, g_out, m, v)]
    outs = _adamw(*packed, name="adamw_small")
    shapes = [w[nm].shape for nm in SMALL]
    for res, o in zip((delta, new_m, new_v), outs):
        res.update(zip(SMALL, _unpack_small(o[0], shapes)))

    return (loss, grad_x, *[g_out[nm] for nm in WEIGHTS], *[delta[nm] for nm in WEIGHTS],
            *[new_m[nm] for nm in WEIGHTS], *[new_v[nm] for nm in WEIGHTS])
```

```python
import functools
import math

import jax
import jax.numpy as jnp
from jax import lax
from jax.experimental import pallas as pl
from jax.experimental.pallas import tpu as pltpu

F32 = jnp.float32
BF16 = jnp.bfloat16
MXU_DTYPE = jnp.bfloat16
EXCHANGE_DTYPE = jnp.bfloat16

EPS = 1e-6
HG_HEADS, HG_D = 4, 128
FOX_HEADS, FOX_DH = 8, 64
MEM_HEADS, MEM_DH = 4, 128
HG_CHUNK = 64
FOX_BLOCK = 256
LANE = 128
FFN_GROUP = 256
FLAT_W = 1024
VMEM_LIMIT = 56 * 2 ** 20
NEG = -1e30
N_CHIPS = 4

ADAM_LR, ADAM_B1, ADAM_B2, ADAM_EPS, ADAM_WD, ADAM_STEP = 0.001, 0.9, 0.999, 1e-08, 0.01, 10

MESH = pl.DeviceIdType.MESH
ANY = pl.BlockSpec(memory_space=pl.ANY)


def _mx(x):
    return x.astype(MXU_DTYPE)


def _dot(a, b, ca, cb):
    return lax.dot_general(_mx(a), _mx(b), (((ca,), (cb,)), ((), ())), preferred_element_type=F32)


def _nn(a, b):
    return _dot(a, b, 1, 0)


def _nt(a, b):
    return _dot(a, b, 1, 1)


def _tn(a, b):
    return _dot(a, b, 0, 0)


def _dotp(a, b, ca, cb):
    return lax.dot_general(a, b, (((ca,), (cb,)), ((), ())), precision=lax.Precision.HIGHEST,
                           preferred_element_type=F32)


def _tri_dot(tri_bf, x):
    hi = x.astype(BF16)
    r = x - hi.astype(F32)
    mid = r.astype(BF16)
    lo = (r - mid.astype(F32)).astype(BF16)

    def d(v):
        return lax.dot_general(tri_bf, v, (((1,), (0,)), ((), ())), preferred_element_type=F32)

    return d(hi) + d(mid) + d(lo)


def _sig(x):
    return jax.nn.sigmoid(x)


def _erf(x):
    a = jnp.abs(x)
    t = 1.0 / (1.0 + 0.3275911 * a)
    poly = t * (0.254829592 + t * (-0.284496736 + t * (1.421413741 + t * (-1.453152027 + t * 1.061405429))))
    y = 1.0 - poly * jnp.exp(-a * a)
    return jnp.where(x < 0, -y, y)


def _tile(dim, pref, unit=LANE):
    if dim <= pref:
        return dim
    t = pref - pref % unit
    while t >= unit:
        if dim % t == 0:
            return t
        t -= unit
    return dim


def _params(n_grid):
    return pltpu.CompilerParams(dimension_semantics=("arbitrary",) * n_grid, vmem_limit_bytes=VMEM_LIMIT)


def _acc(ref, val, first):
    @pl.when(first)
    def _():
        ref[...] = val

    @pl.when(jnp.logical_not(first))
    def _():
        ref[...] += val


class _Rider:
    def __init__(self, inputs, out_shapes, scratch, start, finish):
        self.inputs, self.out_shapes, self.scratch, self.start, self.finish = inputs, out_shapes, scratch, start, finish


def _ride(body, rider, n_in, n_out, grid):
    if rider is None:
        return body
    ri, ro, rs = len(rider.inputs), len(rider.out_shapes), len(rider.scratch)

    def wrapped(*refs):
        a, b, c = n_in + ri, n_in + ri + n_out, n_in + ri + n_out + ro
        base = refs[:n_in] + refs[a:b] + refs[c:len(refs) - rs]
        r_in, r_out, r_scr = refs[n_in:a], refs[b:c], refs[len(refs) - rs:]
        step = pl.program_id(0)
        for ax in range(1, len(grid)):
            step = step * grid[ax] + pl.program_id(ax)

        @pl.when(step == 0)
        def _():
            rider.start(r_in, r_out, r_scr)

        body(*base)

        @pl.when(step == math.prod(grid) - 1)
        def _():
            rider.finish(r_in, r_out, r_scr)

    return wrapped


def _ride_call(body, rider, *, name, grid, in_specs, out_specs, out_shape, scratch, args):
    n_in, n_out = len(in_specs), len(out_specs)
    if rider is None:
        outs = pl.pallas_call(body, name=name, grid=grid, in_specs=in_specs, out_specs=out_specs, out_shape=out_shape,
                              scratch_shapes=scratch, compiler_params=_params(len(grid)))(*args)
        return list(outs), None
    outs = pl.pallas_call(
        _ride(body, rider, n_in, n_out, grid), name=name, grid=grid,
        in_specs=list(in_specs) + [ANY] * len(rider.inputs), out_specs=list(out_specs) + [ANY] * len(rider.out_shapes),
        out_shape=list(out_shape) + list(rider.out_shapes), scratch_shapes=list(scratch) + list(rider.scratch),
        compiler_params=_params(len(grid)),
    )(*args, *rider.inputs)
    return list(outs[:n_out]), list(outs[n_out:])


def _matmul(a, b, *, name, ta=False, tb=False, tm=1024, tn=2048, tk=None, rider=None):
    m, k = (a.shape[1], a.shape[0]) if ta else a.shape
    n = b.shape[0] if tb else b.shape[1]
    tk = tk or (1024 if ta else 2048)
    tm, tn, tk = _tile(m, tm), _tile(n, tn), _tile(k, tk)
    nk = k // tk

    def body(a_ref, b_ref, o_ref):
        p = _dot(a_ref[...], b_ref[...], 0 if ta else 1, 1 if tb else 0)
        if nk == 1:
            o_ref[...] = p
        else:
            _acc(o_ref, p, pl.program_id(2) == 0)

    a_spec = pl.BlockSpec((tk, tm), lambda i, j, kk: (kk, i)) if ta else pl.BlockSpec((tm, tk), lambda i, j, kk: (i, kk))
    b_spec = pl.BlockSpec((tn, tk), lambda i, j, kk: (j, kk)) if tb else pl.BlockSpec((tk, tn), lambda i, j, kk: (kk, j))
    outs, extra = _ride_call(
        body, rider, name=name, grid=(m // tm, n // tn, nk), in_specs=[a_spec, b_spec],
        out_specs=[pl.BlockSpec((tm, tn), lambda i, j, kk: (i, j))], out_shape=[jax.ShapeDtypeStruct((m, n), F32)],
        scratch=[], args=(a, b))
    return (outs[0], extra) if rider else outs[0]


def _rmsnorm_fwd(x, g, *, name, tm=512):
    n, d = x.shape
    tm = _tile(n, tm, 8)

    def body(x_ref, g_ref, o_ref):
        xv = x_ref[...]
        r = lax.rsqrt(jnp.mean(xv * xv, axis=-1, keepdims=True) + EPS)
        o_ref[...] = (xv * r * g_ref[...]).astype(o_ref.dtype)

    return pl.pallas_call(
        body, name=name, grid=(n // tm,),
        in_specs=[pl.BlockSpec((tm, d), lambda i: (i, 0)), pl.BlockSpec((1, d), lambda i: (0, 0))],
        out_specs=pl.BlockSpec((tm, d), lambda i: (i, 0)),
        out_shape=jax.ShapeDtypeStruct((n, d), MXU_DTYPE),
        compiler_params=_params(1),
    )(x, g)


def _rmsnorm_bwd(x, dhs, g, res, *, name, tm=512):
    n, d = x.shape
    tm = _tile(n, tm, 8)
    n_dh = len(dhs)
    has_res = res is not None

    def body(*refs):
        x_ref, dh_refs, g_ref = refs[0], refs[1:1 + n_dh], refs[1 + n_dh]
        res_ref = refs[2 + n_dh] if has_res else None
        dx_ref, dg_ref = refs[-2], refs[-1]
        xv = x_ref[...]
        dh = dh_refs[0][...].astype(F32)
        for r_ in dh_refs[1:]:
            dh = dh + r_[...].astype(F32)
        r = lax.rsqrt(jnp.mean(xv * xv, axis=-1, keepdims=True) + EPS)
        dhg = dh * g_ref[...]
        dx = r * dhg - xv * (r * r * r) * jnp.mean(dhg * xv, axis=-1, keepdims=True)
        if has_res:
            dx = dx + res_ref[...]
        dx_ref[...] = dx
        _acc(dg_ref, jnp.sum(dh * xv * r, axis=0, keepdims=True), pl.program_id(0) == 0)

    row = pl.BlockSpec((tm, d), lambda i: (i, 0))
    vec = pl.BlockSpec((1, d), lambda i: (0, 0))
    ins = [x] + list(dhs) + [g] + ([res] if has_res else [])
    return pl.pallas_call(
        body, name=name, grid=(n // tm,),
        in_specs=[row] * (1 + n_dh) + [vec] + ([row] if has_res else []),
        out_specs=[row, vec],
        out_shape=[jax.ShapeDtypeStruct((n, d), F32), jax.ShapeDtypeStruct((1, d), F32)],
        compiler_params=_params(1),
    )(*ins)


def _adamw(w, g, m, v, *, name, tr=256):
    _, r, c = w.shape
    tr = _tile(r, tr, 8)
    c1 = 1.0 / (1.0 - ADAM_B1 ** ADAM_STEP)
    c2 = 1.0 / (1.0 - ADAM_B2 ** ADAM_STEP)

    def body(w_ref, g_ref, m_ref, v_ref, d_ref, mo_ref, vo_ref):
        gv = g_ref[...]
        mn = ADAM_B1 * m_ref[...] + (1.0 - ADAM_B1) * gv
        vn = ADAM_B2 * v_ref[...] + (1.0 - ADAM_B2) * (gv * gv)
        d_ref[...] = -ADAM_LR * ((mn * c1) / (jnp.sqrt(vn * c2) + ADAM_EPS) + ADAM_WD * w_ref[...])
        mo_ref[...] = mn
        vo_ref[...] = vn

    blk = pl.BlockSpec((1, tr, c), lambda i: (0, i, 0))
    sds = jax.ShapeDtypeStruct((1, r, c), F32)
    return pl.pallas_call(
        body, name=name, grid=(r // tr,), in_specs=[blk] * 4, out_specs=[blk] * 3, out_shape=[sds] * 3,
        compiler_params=_params(1),
    )(w, g, m, v)


def _bdot(a, b, ca, cb):
    return lax.dot_general(_mx(a), _mx(b), (((ca,), (cb,)), ((0,), (0,))), preferred_element_type=F32)


def _bdotp(a, b, ca, cb):
    return lax.dot_general(a, b, (((ca,), (cb,)), ((0,), (0,))), precision=lax.Precision.HIGHEST,
                           preferred_element_type=F32)


def _tri_dot_b(tri_bf, x):
    hi = x.astype(BF16)
    r = x - hi.astype(F32)
    mid = r.astype(BF16)
    lo = (r - mid.astype(F32)).astype(BF16)

    def d(v):
        return lax.dot_general(tri_bf, v, (((2,), (1,)), ((0,), (0,))), preferred_element_type=F32)

    return d(hi) + d(mid) + d(lo)


def _hgrn_forward(hq, hf, hi, lbv, tril, tril_bf):
    nc, c, _ = hq.shape
    sf = _sig(hf)
    f = lbv + (1.0 - lbv) * sf
    k = 1.0 - f
    gcum = _tri_dot_b(tril_bf, jnp.log(f))
    mid = gcum[:, c // 2 - 1:c // 2, :]
    glast = gcum[:, c - 1:c, :]
    sq = _sig(hq)
    q = hq * sq
    e_q = jnp.exp(gcum - mid)
    e_k = jnp.exp(mid - gcum)
    qe, ke = q * e_q, k * e_k
    a = jnp.where(tril, _bdot(qe, ke, 2, 2), 0.0)
    e_g = jnp.exp(gcum)
    qg = q * e_g
    e_s = jnp.exp(glast - gcum)
    kg = k * e_s
    e_l = jnp.exp(glast)
    upd = _bdot(hi, kg, 1, 1)
    st = jnp.zeros((HG_D, HG_D), F32)
    states = []
    for n in range(nc):
        states.append(st)
        st = st * e_l[n] + upd[n]
    st_all = jnp.stack(states)
    o = _bdot(a, hi, 2, 1) + _bdot(qg, st_all, 2, 2)
    return dict(sf=sf, f=f, k=k, sq=sq, q=q, e_q=e_q, e_k=e_k, qe=qe, ke=ke, a=a, e_g=e_g, qg=qg, o=o,
                e_s=e_s, kg=kg, e_l=e_l, st_all=st_all)


def _hgrn_specs(t_, hw):
    nb = hw // LANE

    def col(off):
        return pl.BlockSpec((1, t_, LANE), lambda h, b: (b, 0, off * nb + h))

    vec = pl.BlockSpec((2, LANE), lambda h, b: (0, h))
    one = pl.BlockSpec((1, LANE), lambda h, b: (0, 0))
    blk = pl.BlockSpec((1, t_, LANE), lambda h, b: (b, 0, h))
    return col, vec, one, blk


def _chunk_masks(nc, c):
    row = lax.broadcasted_iota(jnp.int32, (nc, c, c), 1)
    cl = lax.broadcasted_iota(jnp.int32, (nc, c, c), 2)
    return row >= cl, (row >= cl).astype(BF16), (row <= cl).astype(BF16)


def _hgrn_fwd(zm, lb, gn, hw):
    b_, t_, _ = zm.shape
    c = min(HG_CHUNK, t_)
    nc = t_ // c
    col, vec, one, blk = _hgrn_specs(t_, hw)

    def body(q_ref, f_ref, i_ref, g_ref, lb_ref, gn_ref, y_ref):
        lbv, gnv = _sig(lb_ref[0:1, :] - lb_ref[1:2, :]), gn_ref[...]
        tril, tril_bf, _ = _chunk_masks(nc, c)
        chunks = lambda ref: ref[0].reshape(nc, c, LANE)
        o = _hgrn_forward(chunks(q_ref), chunks(f_ref), chunks(i_ref), lbv, tril, tril_bf)["o"]
        r = lax.rsqrt(jnp.mean(o * o, axis=-1, keepdims=True) + EPS)
        hg = chunks(g_ref)
        y_ref[0] = (o * r * gnv * (hg * _sig(hg))).reshape(t_, LANE)

    return pl.pallas_call(
        body, name="hgrn_fwd", grid=(HG_HEADS, b_),
        in_specs=[col(0), col(1), col(2), col(3), vec, one], out_specs=blk,
        out_shape=jax.ShapeDtypeStruct((b_, t_, hw), F32),
        compiler_params=_params(2),
    )(zm, zm, zm, zm, lb, gn)


def _hgrn_bwd(zm, dy, lb, gn, hw, rider=None):
    b_, t_, _ = zm.shape
    c = min(HG_CHUNK, t_)
    nc = t_ // c
    col, vec, one, blk = _hgrn_specs(t_, hw)

    def body(q_ref, f_ref, i_ref, g_ref, dy_ref, lb_ref, gn_ref, dq_ref, df_ref, di_ref, dg_ref, dlb_ref, dgn_ref):
        h, b = pl.program_id(0), pl.program_id(1)
        lbv, gnv = _sig(lb_ref[0:1, :] - lb_ref[1:2, :]), gn_ref[...]
        tril, tril_bf, triu_bf = _chunk_masks(nc, c)
        last_row = lax.broadcasted_iota(jnp.int32, (nc, c, LANE), 1) == c - 1
        chunks = lambda ref: ref[0].reshape(nc, c, LANE)
        flat = lambda x: x.reshape(t_, LANE)
        hq, hi, hg = chunks(q_ref), chunks(i_ref), chunks(g_ref)
        p = _hgrn_forward(hq, chunks(f_ref), hi, lbv, tril, tril_bf)
        o, q, k, st_all, e_l = p["o"], p["q"], p["k"], p["st_all"], p["e_l"]
        dyv = chunks(dy_ref)
        sg = _sig(hg)
        r = lax.rsqrt(jnp.mean(o * o, axis=-1, keepdims=True) + EPS)
        dn = dyv * (hg * sg)
        dg_ref[0] = flat(dyv * (o * r * gnv) * (sg * (1.0 + hg * (1.0 - sg)))).astype(dg_ref.dtype)
        dgn = jnp.sum(flat(dn * o * r), axis=0, keepdims=True)
        dng = dn * gnv
        do = r * dng - o * (r * r * r) * jnp.mean(dng * o, axis=-1, keepdims=True)
        back = _bdotp(do, p["qg"], 1, 1)
        dst = jnp.zeros((HG_D, HG_D), F32)
        dsts = [None] * nc
        for n in range(nc - 1, -1, -1):
            dsts[n] = dst
            dst = dst * e_l[n] + back[n]
        dst_all = jnp.stack(dsts)
        da = jnp.where(tril, _bdotp(do, hi, 2, 2), 0.0)
        dq = _bdotp(da, p["ke"], 2, 1) * p["e_q"] + _bdotp(do, st_all, 2, 1) * p["e_g"]
        dk_state = _bdotp(hi, dst_all, 2, 1) * p["e_s"]
        dk = _bdotp(da, p["qe"], 1, 1) * p["e_k"] + dk_state
        di_ref[0] = flat(_bdot(p["a"], do, 1, 1) + _bdot(p["kg"], dst_all, 2, 2)).astype(di_ref.dtype)
        extra = (jnp.sum(k * dk_state, axis=1, keepdims=True) + e_l * jnp.sum(st_all * dst_all, axis=1, keepdims=True))
        dgc = q * dq - k * dk + jnp.where(last_row, extra, 0.0)
        dfv = _tri_dot_b(triu_bf, dgc) / p["f"] - dk
        sf, sq = p["sf"], p["sq"]
        df_ref[0] = flat(dfv * (1.0 - lbv) * sf * (1.0 - sf)).astype(df_ref.dtype)
        dlb = jnp.sum(flat(dfv * (1.0 - sf)), axis=0, keepdims=True)
        dq_ref[0] = flat(dq * (sq * (1.0 + hq * (1.0 - sq)))).astype(dq_ref.dtype)
        dl0 = dlb * lbv * (1.0 - lbv)
        _acc(dlb_ref, jnp.concatenate([dl0, -dl0], axis=0), b == 0)
        _acc(dgn_ref, dgn, jnp.logical_and(b == 0, h == 0))

    sds = jax.ShapeDtypeStruct((b_, t_, hw), MXU_DTYPE)
    return _ride_call(
        body, rider, name="hgrn_bwd", grid=(HG_HEADS, b_),
        in_specs=[col(0), col(1), col(2), col(3), blk, vec, one],
        out_specs=[blk, blk, blk, blk, vec, one],
        out_shape=[sds, sds, sds, sds, jax.ShapeDtypeStruct((2, hw), F32), jax.ShapeDtypeStruct((1, LANE), F32)],
        scratch=[], args=(zm, zm, zm, zm, dy, lb, gn))


def _fox_logf(x):
    return jnp.minimum(x, 0.0) - jnp.log(1.0 + jnp.exp(-jnp.abs(x)))


def _fox_prep(zf, bias):
    b_, t_, _ = zf.shape
    tb = min(FOX_BLOCK, t_)
    nb = t_ // tb

    def body(z_ref, b_ref, fc_ref):
        tril_bf = (lax.broadcasted_iota(jnp.int32, (tb, tb), 0) >= lax.broadcasted_iota(jnp.int32, (tb, tb), 1)).astype(BF16)
        bv = b_ref[...]

        def blk(i, carry):
            rows = pl.ds(pl.multiple_of(i * tb, tb), tb)
            fc = _tri_dot(tril_bf, _fox_logf(z_ref[0, rows, :] + bv)) + carry
            fc_ref[0, rows, :] = fc
            return fc[tb - 1:tb, :]

        lax.fori_loop(0, nb, blk, jnp.zeros((1, LANE), F32))

    blk_spec = pl.BlockSpec((1, t_, LANE), lambda b: (b, 0, 0))
    return pl.pallas_call(
        body, name="fox_prep", grid=(b_,),
        in_specs=[blk_spec, pl.BlockSpec((1, LANE), lambda b: (0, 0))], out_specs=blk_spec,
        out_shape=jax.ShapeDtypeStruct((b_, t_, LANE), F32), compiler_params=_params(1),
    )(zf, bias)


def _fox_post(dfc, zf, bias):
    b_, t_, _ = zf.shape
    tb = min(FOX_BLOCK, t_)
    nb = t_ // tb

    def body(d_ref, z_ref, b_ref, dz_ref, db_ref):
        triu_bf = (lax.broadcasted_iota(jnp.int32, (tb, tb), 0) <= lax.broadcasted_iota(jnp.int32, (tb, tb), 1)).astype(BF16)
        valid = lax.broadcasted_iota(jnp.int32, (tb, LANE), 1) < FOX_HEADS
        bv = b_ref[...]

        def blk(m, carry):
            tail, db = carry
            rows = pl.ds(pl.multiple_of((nb - 1 - m) * tb, tb), tb)
            dlf = _tri_dot(triu_bf, d_ref[0, rows, :]) + tail
            dx = jnp.where(valid, dlf * _sig(-(z_ref[0, rows, :] + bv)), 0.0)
            dz_ref[0, rows, :] = dx.astype(dz_ref.dtype)
            return dlf[0:1, :], db + jnp.sum(dx, axis=0, keepdims=True)

        z1 = jnp.zeros((1, LANE), F32)
        _, db = lax.fori_loop(0, nb, blk, (z1, z1))
        _acc(db_ref, db, pl.program_id(0) == 0)

    blk_spec = pl.BlockSpec((1, t_, LANE), lambda b: (b, 0, 0))
    vec = pl.BlockSpec((1, LANE), lambda b: (0, 0))
    return pl.pallas_call(
        body, name="fox_post", grid=(b_,), in_specs=[blk_spec, blk_spec, vec], out_specs=[blk_spec, vec],
        out_shape=[jax.ShapeDtypeStruct((b_, t_, LANE), MXU_DTYPE), jax.ShapeDtypeStruct((1, LANE), F32)],
        compiler_params=_params(1),
    )(dfc, zf, bias)


FOX_TILE = 128
FOX_BAND = 512
AUG = 64


def _head_mean_matrix():
    r = lax.broadcasted_iota(jnp.int32, (LANE, LANE), 0) // FOX_DH
    c = lax.broadcasted_iota(jnp.int32, (LANE, LANE), 1) // FOX_DH
    return (r == c).astype(BF16)


def _dot_right_exact(x, m_bf):
    hi = x.astype(BF16)
    r = x - hi.astype(F32)
    mid = r.astype(BF16)
    lo = (r - mid.astype(F32)).astype(BF16)

    def d(v):
        return lax.dot_general(v, m_bf, (((1,), (0,)), ((), ())), preferred_element_type=F32)

    return d(hi) + d(mid) + d(lo)


def _pair_norm(x, g2, bd):
    r = lax.rsqrt(_dot_right_exact(x * x, bd) * (1.0 / FOX_DH) + EPS)
    return x * r * g2, r


def _pair_norm_bwd(x, r, dy, g2, bd):
    dyg = dy * g2
    dx = r * dyg - x * (r * r * r) * (_dot_right_exact(dyg * x, bd) * (1.0 / FOX_DH))
    return dx, jnp.sum(dy * x * r, axis=0, keepdims=True)


def _head_lanes(xn, hh):
    return xn if hh == 0 else pltpu.roll(xn, FOX_DH, 1)


def _split3(x):
    hi = x.astype(BF16).astype(F32)
    mid = (x - hi).astype(BF16).astype(F32)
    return hi, mid, x - hi - mid


def _fox_operands(q_ref, k_ref, v_ref, fc_ref, gq2, gk2, p, qa, ka, va):
    t_ = q_ref.shape[1]
    bd = _head_mean_matrix()
    lane = lax.broadcasted_iota(jnp.int32, (t_, LANE), 1)
    qx, kx = q_ref[0], k_ref[0]
    qn, rq = _pair_norm(qx, gq2, bd)
    kn, rk = _pair_norm(kx, gk2, bd)
    vv = v_ref[0]
    q_aug = jnp.where(jnp.logical_and(lane >= AUG, lane < AUG + 3), 1.0, 0.0)
    for hh in range(2):
        fcol = jnp.sum(jnp.where(lane == 2 * p + hh, fc_ref[0], 0.0), axis=-1, keepdims=True)
        hi, mid, lo = _split3(-fcol)
        k_aug = jnp.where(lane == AUG, hi, jnp.where(lane == AUG + 1, mid, jnp.where(lane == AUG + 2, lo,
                          jnp.where(lane == AUG + 3, 1.0, 0.0))))
        head = lane < FOX_DH
        qa[hh] = jnp.where(head, _head_lanes(qn, hh), q_aug).astype(MXU_DTYPE)
        ka[hh] = jnp.where(head, _head_lanes(kn, hh), k_aug).astype(MXU_DTYPE)
        va[hh] = jnp.where(head, _head_lanes(vv, hh), 0.0).astype(MXU_DTYPE)
    return bd, lane, qx, kx, rq, rk


def _fox_specs(t_, fw, col0):
    npair = fw // LANE

    def col(off):
        return pl.BlockSpec((1, t_, LANE), lambda b, p: (b, 0, col0 + off * npair + p))

    pair = pl.BlockSpec((1, t_, LANE), lambda b, p: (b, 0, p))
    full = pl.BlockSpec((1, t_, LANE), lambda b, p: (b, 0, 0))
    gvec = pl.BlockSpec((1, LANE), lambda b, p: (0, 0))
    lse = pl.BlockSpec((1, 1, t_, LANE), lambda b, p: (b, p, 0, 0))
    return col, pair, full, gvec, lse


def _fox_fwd(zm, fc, gq2, gk2, fw, col0):
    b_, t_, _ = zm.shape
    npair = fw // LANE
    tq = min(FOX_TILE, t_)
    bw = min(FOX_BAND, t_)
    nband, tpb = t_ // bw, bw // tq
    scale = FOX_DH ** -0.5
    col, pair, full, gvec, lse_spec = _fox_specs(t_, fw, col0)

    def body(q_ref, k_ref, v_ref, fc_ref, gq_ref, gk_ref, o_ref, lse_ref, qa, ka, va):
        p = pl.program_id(1)
        _fox_operands(q_ref, k_ref, v_ref, fc_ref, gq_ref[...] * scale, gk_ref[...], p, qa, ka, va)
        ri = lax.broadcasted_iota(jnp.int32, (tq, bw), 0)
        ci = lax.broadcasted_iota(jnp.int32, (tq, bw), 1)
        lane = lax.broadcasted_iota(jnp.int32, (tq, LANE), 1)

        for band in range(nband):
            c0 = band * bw

            def qtile(ii, _, c0=c0):
                r0 = pl.multiple_of(c0 + ii * tq, tq)
                rows = pl.ds(r0, tq)
                keep = c0 + ci <= r0 + ri
                res = []
                for hh in range(2):
                    qb = qa[hh, rows, :]
                    s_b = jnp.where(keep, _nt(qb, ka[hh, c0:c0 + bw, :]), NEG)
                    m = jnp.max(s_b, axis=-1, keepdims=True)
                    if c0:
                        s_a = _nt(qb, ka[hh, 0:c0, :])
                        m = jnp.maximum(m, jnp.max(s_a, axis=-1, keepdims=True))
                    p_b = jnp.exp(s_b - m)
                    l = jnp.sum(p_b, axis=-1, keepdims=True)
                    acc = _nn(p_b, va[hh, c0:c0 + bw, :])
                    if c0:
                        p_a = jnp.exp(s_a - m)
                        l = l + jnp.sum(p_a, axis=-1, keepdims=True)
                        acc = acc + _nn(p_a, va[hh, 0:c0, :])
                    res.append((acc / l, m + jnp.log(l)))
                (o0, e0), (o1, e1) = res
                o_ref[0, rows, :] = jnp.where(lane < FOX_DH, o0, pltpu.roll(o1, FOX_DH, 1))
                lse_ref[0, 0, rows, :] = jnp.where(lane == 0, e0, jnp.where(lane == 1, e1, 0.0))
                return 0

            lax.fori_loop(0, tpb, qtile, 0)

    return pl.pallas_call(
        body, name="fox_fwd", grid=(b_, npair),
        in_specs=[col(0), col(1), col(2), full, gvec, gvec],
        out_specs=[pair, lse_spec],
        out_shape=[jax.ShapeDtypeStruct((b_, t_, fw), F32), jax.ShapeDtypeStruct((b_, npair, t_, LANE), F32)],
        scratch_shapes=[pltpu.VMEM((2, t_, LANE), MXU_DTYPE)] * 3,
        compiler_params=_params(2),
    )(zm, zm, zm, fc, gq2, gk2)


def _norm_bwd(x, dy, g):
    r = lax.rsqrt(jnp.mean(x * x, axis=-1, keepdims=True) + EPS)
    dyg = dy * g
    dx = r * dyg - x * (r * r * r) * jnp.mean(dyg * x, axis=-1, keepdims=True)
    return dx, jnp.sum(dy * x * r, axis=0, keepdims=True)


def _fox_bwd(zm, o, do, lse, fc, gq2, gk2, fw, col0, rider=None):
    b_, t_, _ = zm.shape
    npair = fw // LANE
    tq = min(FOX_TILE, t_)
    nb = t_ // tq
    bw = min(FOX_BAND, t_)
    nband, tpb = t_ // bw, bw // tq
    scale = FOX_DH ** -0.5
    col, pair, full, gvec, lse_spec = _fox_specs(t_, fw, col0)

    def body(q_ref, k_ref, v_ref, o_ref, do_ref, lse_ref, fc_ref, gq_ref, gk_ref,
             dq_ref, dk_ref, dv_ref, dfc_ref, dgq_ref, dgk_ref, qa, ka, va, da, rowv, dq_acc, dk_acc, dv_acc):
        b, p = pl.program_id(0), pl.program_id(1)
        gq2v, gk2v = gq_ref[...] * scale, gk_ref[...]
        bd, lane, qx, kx, rq, rk = _fox_operands(q_ref, k_ref, v_ref, fc_ref, gq2v, gk2v, p, qa, ka, va)
        head = lane < FOX_DH
        dov = do_ref[0]
        dsum = _dot_right_exact(dov * o_ref[0], bd)
        eye = (lax.broadcasted_iota(jnp.int32, (tq, tq), 0) == lax.broadcasted_iota(jnp.int32, (tq, tq), 1)).astype(F32)
        for hh in range(2):
            da[hh] = jnp.where(head, _head_lanes(dov, hh), 0.0).astype(MXU_DTYPE)
            for blk in range(nb):
                rs = slice(blk * tq, (blk + 1) * tq)
                rowv[2 * hh:2 * hh + 1, rs] = jnp.sum(eye * lse_ref[0, 0, rs, hh:hh + 1], axis=0, keepdims=True)
                rowv[2 * hh + 1:2 * hh + 2, rs] = jnp.sum(eye * dsum[rs, hh * FOX_DH:hh * FOX_DH + 1], axis=0, keepdims=True)
        dq_acc[...] = jnp.zeros(dq_acc.shape, F32)
        ri = lax.broadcasted_iota(jnp.int32, (tq, bw), 0)
        ci = lax.broadcasted_iota(jnp.int32, (tq, bw), 1)

        def part(hh, kb, vb, lo, hi, keep):
            qm, dm = qa[hh, lo:hi, :], da[hh, lo:hi, :]
            pt = jnp.exp(_nt(kb, qm) - rowv[2 * hh:2 * hh + 1, lo:hi])
            if keep is not None:
                pt = jnp.where(keep, pt, 0.0)
            dst = pt * (_nt(vb, dm) - rowv[2 * hh + 1:2 * hh + 2, lo:hi])
            dq_acc[hh, lo:hi, :] += _tn(dst, kb)
            return _nn(dst, qm), _nn(pt, dm)

        for band in range(nband):
            c0 = band * bw

            def kvtile(jj, _, c0=c0):
                r0 = pl.multiple_of(c0 + jj * tq, tq)
                rows = pl.ds(r0, tq)
                keep = c0 + ci >= r0 + ri
                for hh in range(2):
                    kb, vb = ka[hh, rows, :], va[hh, rows, :]
                    dk_t, dv_t = part(hh, kb, vb, c0, c0 + bw, keep)
                    if c0 + bw < t_:
                        dk_u, dv_u = part(hh, kb, vb, c0 + bw, t_, None)
                        dk_t, dv_t = dk_t + dk_u, dv_t + dv_u
                    dk_acc[hh, rows, :] = dk_t
                    dv_acc[hh, rows, :] = dv_t
                return 0

            lax.fori_loop(0, tpb, kvtile, 0)

        dq0, dq1, dk0, dk1 = dq_acc[0], dq_acc[1], dk_acc[0], dk_acc[1]
        dqn = jnp.where(head, dq0, pltpu.roll(dq1, FOX_DH, 1))
        dkn = jnp.where(head, dk0, pltpu.roll(dk1, FOX_DH, 1))
        dqx, gq_part = _pair_norm_bwd(qx, rq, dqn, gq2v, bd)
        dkx, gk_part = _pair_norm_bwd(kx, rk, dkn, gk2v, bd)
        dq_ref[0] = dqx.astype(dq_ref.dtype)
        dk_ref[0] = dkx.astype(dk_ref.dtype)
        dv_ref[0] = jnp.where(head, dv_acc[0], pltpu.roll(dv_acc[1], FOX_DH, 1)).astype(dv_ref.dtype)

        def bias_grad(dqh, dkh):
            return (jnp.sum(jnp.where(lane == AUG + 3, dqh, 0.0), axis=-1, keepdims=True)
                    - jnp.sum(jnp.where(lane == AUG, dkh, 0.0), axis=-1, keepdims=True))

        dfc_ref[0, 0] = jnp.where(lane == 0, bias_grad(dq0, dk0), jnp.where(lane == 1, bias_grad(dq1, dk1), 0.0))
        first = jnp.logical_and(b == 0, p == 0)
        _acc(dgq_ref, gq_part * scale, first)
        _acc(dgk_ref, gk_part, first)

    sds = jax.ShapeDtypeStruct((b_, t_, fw), MXU_DTYPE)
    gs = jax.ShapeDtypeStruct((1, LANE), F32)
    return _ride_call(
        body, rider, name="fox_bwd", grid=(b_, npair),
        in_specs=[col(0), col(1), col(2), pair, pair, lse_spec, full, gvec, gvec],
        out_specs=[pair, pair, pair, lse_spec, gvec, gvec],
        out_shape=[sds, sds, sds, jax.ShapeDtypeStruct((b_, npair, t_, LANE), F32), gs, gs],
        scratch=[pltpu.VMEM((2, t_, LANE), MXU_DTYPE)] * 4
        + [pltpu.VMEM((8, t_), F32)] + [pltpu.VMEM((2, t_, LANE), F32)] * 3,
        args=(zm, zm, zm, o, do, lse, fc, gq2, gk2))


def _mem_specs(t_, m_, mw, col0):
    nh = mw // LANE
    qcol = pl.BlockSpec((1, t_, LANE), lambda b, h: (b, 0, col0 + h))
    kcol = pl.BlockSpec((1, m_, LANE), lambda b, h: (b, 0, h))
    vcol = pl.BlockSpec((1, m_, LANE), lambda b, h: (b, 0, nh + h))
    ycol = pl.BlockSpec((1, t_, LANE), lambda b, h: (b, 0, h))
    gvec = pl.BlockSpec((1, LANE), lambda b, h: (0, 0))
    return qcol, kcol, vcol, ycol, gvec


def _mem_fwd(zm, mkv, gq, gk, mw, col0):
    b_, t_, _ = zm.shape
    m_ = mkv.shape[1]
    tq = min(512, t_)
    nb = t_ // tq
    scale = MEM_DH ** -0.5
    qcol, kcol, vcol, ycol, gvec = _mem_specs(t_, m_, mw, col0)

    def body(q_ref, k_ref, v_ref, gq_ref, gk_ref, y_ref):
        gqv, gkv = gq_ref[...] * scale, gk_ref[...]
        kv = k_ref[0]
        kn = _mx(kv * lax.rsqrt(jnp.mean(kv * kv, axis=-1, keepdims=True) + EPS) * gkv)
        vv = _mx(v_ref[0])

        def blk(i, _):
            rows = pl.ds(pl.multiple_of(i * tq, tq), tq)
            qv = q_ref[0, rows, :]
            s = _nt(qv * lax.rsqrt(jnp.mean(qv * qv, axis=-1, keepdims=True) + EPS) * gqv, kn)
            e = jnp.exp(s - jnp.max(s, axis=-1, keepdims=True))
            y_ref[0, rows, :] = _nn(e / jnp.sum(e, axis=-1, keepdims=True), vv)
            return 0

        lax.fori_loop(0, nb, blk, 0)

    return pl.pallas_call(
        body, name="mem_fwd", grid=(b_, MEM_HEADS), in_specs=[qcol, kcol, vcol, gvec, gvec], out_specs=ycol,
        out_shape=jax.ShapeDtypeStruct((b_, t_, mw), F32), compiler_params=_params(2),
    )(zm, mkv, mkv, gq, gk)


def _mem_bwd(zm, mkv, dy, gq, gk, mw, col0):
    b_, t_, _ = zm.shape
    m_ = mkv.shape[1]
    tq = min(512, t_)
    nb = t_ // tq
    scale = MEM_DH ** -0.5
    qcol, kcol, vcol, ycol, gvec = _mem_specs(t_, m_, mw, col0)

    def body(q_ref, k_ref, v_ref, dy_ref, gq_ref, gk_ref, dq_ref, dk_ref, dv_ref, dgq_ref, dgk_ref):
        gqv, gkv = gq_ref[...] * scale, gk_ref[...]
        kv = k_ref[0]
        kn = _mx(kv * lax.rsqrt(jnp.mean(kv * kv, axis=-1, keepdims=True) + EPS) * gkv)
        vv = _mx(v_ref[0])

        def blk(i, carry):
            dkn, dvv, dgq = carry
            rows = pl.ds(pl.multiple_of(i * tq, tq), tq)
            qv = q_ref[0, rows, :]
            qn = _mx(qv * lax.rsqrt(jnp.mean(qv * qv, axis=-1, keepdims=True) + EPS) * gqv)
            s = _nt(qn, kn)
            e = jnp.exp(s - jnp.max(s, axis=-1, keepdims=True))
            pm = e / jnp.sum(e, axis=-1, keepdims=True)
            dob = _mx(dy_ref[0, rows, :])
            dp = _nt(dob, vv)
            ds = pm * (dp - jnp.sum(dp * pm, axis=-1, keepdims=True))
            dqv, gq_part = _norm_bwd(qv, _nn(ds, kn), gqv)
            dq_ref[0, rows, :] = dqv.astype(dq_ref.dtype)
            return dkn + _tn(ds, qn), dvv + _tn(pm, dob), dgq + gq_part * scale

        z = jnp.zeros((m_, LANE), F32)
        dkn, dvv, dgq = lax.fori_loop(0, nb, blk, (z, z, jnp.zeros((1, LANE), F32)))
        dkv, dgk = _norm_bwd(kv, dkn, gkv)
        dk_ref[0] = dkv
        dv_ref[0] = dvv
        first = jnp.logical_and(pl.program_id(0) == 0, pl.program_id(1) == 0)
        _acc(dgq_ref, dgq, first)
        _acc(dgk_ref, dgk, first)

    kblk = pl.BlockSpec((1, m_, LANE), lambda b, h: (b, 0, h))
    gs = jax.ShapeDtypeStruct((1, LANE), F32)
    ks = jax.ShapeDtypeStruct((b_, m_, mw), F32)
    return pl.pallas_call(
        body, name="mem_bwd", grid=(b_, MEM_HEADS), in_specs=[qcol, kcol, vcol, ycol, gvec, gvec],
        out_specs=[ycol, kblk, kblk, gvec, gvec],
        out_shape=[jax.ShapeDtypeStruct((b_, t_, mw), MXU_DTYPE), ks, ks, gs, gs], compiler_params=_params(2),
    )(zm, mkv, mkv, dy, gq, gk)


def _merge_specs(tm, d, w, gcol):
    row_d = pl.BlockSpec((tm, d), lambda i: (i, 0))
    row_w = pl.BlockSpec((tm, w), lambda i: (i, 0))
    gates = [pl.BlockSpec((tm, d), functools.partial(lambda i, k: (i, gcol + k), k=k)) for k in range(3)]
    w_br = pl.BlockSpec((w, d), lambda i: (0, 0))
    w_o = pl.BlockSpec((d, d), lambda i: (0, 0))
    return row_d, row_w, gates, w_br, w_o


def _merge_fwd(x, ys, zm, w_brs, w_out, gcol, tm=256):
    n, d = x.shape
    w = ys[0].shape[1]
    tm = _tile(n, tm, 8)
    row_d, row_w, gates, w_br, w_o = _merge_specs(tm, d, w, gcol)

    def body(x_ref, ya, yb, yc, g0, g1, g2, wa, wb, wc, wo, x1_ref, mg_ref):
        mg = (_sig(g0[...]) * _nn(ya[...], wa[...]) + _sig(g1[...]) * _nn(yb[...], wb[...])
              + _sig(g2[...]) * _nn(yc[...], wc[...]))
        mg_ref[...] = mg.astype(mg_ref.dtype)
        x1_ref[...] = x_ref[...] + _nn(mg, wo[...])

    return pl.pallas_call(
        body, name="merge_fwd", grid=(n // tm,),
        in_specs=[row_d, row_w, row_w, row_w] + gates + [w_br, w_br, w_br, w_o],
        out_specs=[row_d, row_d],
        out_shape=[jax.ShapeDtypeStruct((n, d), F32), jax.ShapeDtypeStruct((n, d), MXU_DTYPE)],
        compiler_params=_params(1),
    )(x, *ys, zm, zm, zm, *w_brs, w_out)


def _merge_bwd(dx1, ys, zm, w_brs, w_out, gcol, tm=256):
    n, d = dx1.shape
    w = ys[0].shape[1]
    tm = _tile(n, tm, 8)
    row_d, row_w, gates, w_br, w_o = _merge_specs(tm, d, w, gcol)

    def body(dx_ref, ya, yb, yc, g0, g1, g2, wa, wb, wc, wo, dgl_ref, dpa, dpb, dpc, dya, dyb, dyc):
        dm = _nt(dx_ref[...], wo[...])
        for k, (y, g, wr, dp_ref, dy_ref) in enumerate(((ya, g0, wa, dpa, dya), (yb, g1, wb, dpb, dyb),
                                                        (yc, g2, wc, dpc, dyc))):
            sg = _sig(g[...])
            pr = _nn(y[...], wr[...])
            dgl_ref[:, k * d:(k + 1) * d] = (dm * pr * sg * (1.0 - sg)).astype(dgl_ref.dtype)
            dp = (dm * sg).astype(dp_ref.dtype)
            dp_ref[...] = dp
            dy_ref[...] = _nt(dp, wr[...])

    sd = jax.ShapeDtypeStruct((n, d), MXU_DTYPE)
    sw = jax.ShapeDtypeStruct((n, w), F32)
    return pl.pallas_call(
        body, name="merge_bwd", grid=(n // tm,),
        in_specs=[row_d, row_w, row_w, row_w] + gates + [w_br, w_br, w_br, w_o],
        out_specs=[pl.BlockSpec((tm, 3 * d), lambda i: (i, 0)), row_d, row_d, row_d, row_w, row_w, row_w],
        out_shape=[jax.ShapeDtypeStruct((n, 3 * d), MXU_DTYPE), sd, sd, sd, sw, sw, sw],
        compiler_params=_params(1),
    )(dx1, *ys, zm, zm, zm, *w_brs, w_out)


CONV_ROWS = 256
HALO = 8


def _ext(ref, r0, t_):
    rc = min(CONV_ROWS, t_)
    a, b = max(r0 - HALO, 0), min(r0 + rc + HALO, t_)
    parts = []
    if r0 - HALO < 0:
        parts.append(jnp.zeros((HALO, ref.shape[2]), F32))
    parts.append(ref[0, a:b, :].astype(F32))
    if r0 + rc + HALO > t_:
        parts.append(jnp.zeros((HALO, ref.shape[2]), F32))
    return jnp.concatenate(parts, axis=0) if len(parts) > 1 else parts[0]


def _gelu_parts(ac):
    cdf = 0.5 * (1.0 + _erf(ac * (2.0 ** -0.5)))
    pdf = jnp.exp(-0.5 * ac * ac) * ((2.0 * math.pi) ** -0.5)
    return cdf, pdf


def _conv_taps(a_ext, cw, cb):
    return cw[0:1, :] * pltpu.roll(a_ext, 2, 0) + cw[1:2, :] * pltpu.roll(a_ext, 1, 0) + cw[2:3, :] * a_ext + cb


def _glu_specs(t_, f, g):
    gate = pl.BlockSpec((1, t_, g), lambda j, b: (b, 0, j))
    value = pl.BlockSpec((1, t_, g), lambda j, b: (b, 0, f // g + j))
    cwb = pl.BlockSpec((3, g), lambda j, b: (0, j))
    cbb = pl.BlockSpec((1, g), lambda j, b: (0, j))
    return gate, value, cwb, cbb


def _glu_fwd(u, cw, cb):
    b_, t_, f2 = u.shape
    f = f2 // 2
    g = min(FFN_GROUP, f)
    rc = min(CONV_ROWS, t_)
    gate, value, cwb, cbb = _glu_specs(t_, f, g)

    def body(a_ref, v_ref, cw_ref, cb_ref, y_ref):
        cwv, cbv = cw_ref[...], cb_ref[...]
        for r0 in range(0, t_, rc):
            ac = _conv_taps(_ext(a_ref, r0, t_), cwv, cbv)[HALO:HALO + rc]
            cdf, _ = _gelu_parts(ac)
            y_ref[0, r0:r0 + rc, :] = (ac * cdf * v_ref[0, r0:r0 + rc, :]).astype(y_ref.dtype)

    return pl.pallas_call(
        body, name="glu_fwd", grid=(f // g, b_), in_specs=[gate, value, cwb, cbb], out_specs=gate,
        out_shape=jax.ShapeDtypeStruct((b_, t_, f), MXU_DTYPE), compiler_params=_params(2),
    )(u, u, cw, cb)


def _glu_bwd(u, dy, cw, cb):
    b_, t_, f2 = u.shape
    f = f2 // 2
    g = min(FFN_GROUP, f)
    rc = min(CONV_ROWS, t_)
    ne = rc + 2 * HALO
    gate, value, cwb, cbb = _glu_specs(t_, f, g)

    def body(a_ref, v_ref, dy_ref, cw_ref, cb_ref, da_ref, dv_ref, dcw_ref, dcb_ref):
        cwv, cbv = cw_ref[...], cb_ref[...]
        dcw = [jnp.zeros((1, g), F32) for _ in range(3)]
        dcb = jnp.zeros((1, g), F32)
        for r0 in range(0, t_, rc):
            a_ext, v_ext, dy_ext = _ext(a_ref, r0, t_), _ext(v_ref, r0, t_), _ext(dy_ref, r0, t_)
            ac = _conv_taps(a_ext, cwv, cbv)
            cdf, pdf = _gelu_parts(ac)
            dac = dy_ext * v_ext * (cdf + ac * pdf)
            da = cwv[2:3, :] * dac + cwv[1:2, :] * pltpu.roll(dac, ne - 1, 0) + cwv[0:1, :] * pltpu.roll(dac, ne - 2, 0)
            mid = slice(HALO, HALO + rc)
            da_ref[0, r0:r0 + rc, :] = da[mid].astype(da_ref.dtype)
            dv_ref[0, r0:r0 + rc, :] = (dy_ext[mid] * ac[mid] * cdf[mid]).astype(dv_ref.dtype)
            dacm = dac[mid]
            dcw[0] = dcw[0] + jnp.sum(dacm * pltpu.roll(a_ext, 2, 0)[mid], axis=0, keepdims=True)
            dcw[1] = dcw[1] + jnp.sum(dacm * pltpu.roll(a_ext, 1, 0)[mid], axis=0, keepdims=True)
            dcw[2] = dcw[2] + jnp.sum(dacm * a_ext[mid], axis=0, keepdims=True)
            dcb = dcb + jnp.sum(dacm, axis=0, keepdims=True)
        first = pl.program_id(1) == 0
        _acc(dcw_ref, jnp.concatenate(dcw, axis=0), first)
        _acc(dcb_ref, dcb, first)

    sds = jax.ShapeDtypeStruct((b_, t_, f), MXU_DTYPE)
    return pl.pallas_call(
        body, name="glu_bwd", grid=(f // g, b_), in_specs=[gate, value, gate, cwb, cbb],
        out_specs=[gate, gate, cwb, cbb],
        out_shape=[sds, sds, jax.ShapeDtypeStruct((3, f), F32), jax.ShapeDtypeStruct((1, f), F32)],
        compiler_params=_params(2),
    )(u, u, dy, cw, cb)


def _loss_head(x1, ffn, target, tm=512):
    n, d = x1.shape
    tm = _tile(n, tm, 8)

    def body(x_ref, f_ref, t_ref, dy_ref, l_ref):
        err = x_ref[...] + f_ref[...] - t_ref[...]
        dy_ref[...] = err * (1.0 / d)
        _acc(l_ref, jnp.sum(err * err, axis=0, keepdims=True) * (0.5 / d), pl.program_id(0) == 0)

    row = pl.BlockSpec((tm, d), lambda i: (i, 0))
    vec = pl.BlockSpec((1, d), lambda i: (0, 0))
    return pl.pallas_call(
        body, name="loss_head", grid=(n // tm,), in_specs=[row, row, row], out_specs=[row, vec],
        out_shape=[jax.ShapeDtypeStruct((n, d), F32), jax.ShapeDtypeStruct((1, d), F32)], compiler_params=_params(1),
    )(x1, ffn, target)


def _place():
    x, y, c = lax.axis_index("x"), lax.axis_index("y"), lax.axis_index("c")
    chips = [(1 - x, y), (x, 1 - y), (1 - x, 1 - y)]
    return x, y, c, chips


def _remote(src, dst, send_sem, recv_sem, to):
    return pltpu.make_async_remote_copy(src_ref=src, dst_ref=dst, send_sem=send_sem, recv_sem=recv_sem,
                                        device_id=to, device_id_type=MESH)


STACK, COLS = "stack", "cols"


def _shard_ref(ref, kind, s, rows, c):
    if kind == COLS:
        cols = pl.ds(pl.multiple_of(s * c, LANE), c)
        return ref.at[:, cols] if rows is None else ref.at[rows, cols]
    return ref.at[s] if rows is None else ref.at[s, rows, :]


def _halves(c, half):
    mine = pl.ds(pl.multiple_of(c * half, 16), half)
    theirs = pl.ds(pl.multiple_of((1 - c) * half, 16), half)
    return mine, theirs


def _gather_parts(kinds):
    def first_copies(ins, outs, sems):
        x, y, c, chips = _place()
        me = 2 * x + y
        cps = []
        for i, (w_ref, o_ref, kind) in enumerate(zip(ins, outs, kinds)):
            r, cw = w_ref.shape
            mine, _ = _halves(c, r // 2)
            for j, chip in enumerate(chips):
                cps.append(_remote(w_ref.at[mine], _shard_ref(o_ref, kind, me, mine, cw), sems[0].at[6 * i + j],
                                   sems[1].at[6 * i + j], (*chip, c)))
        return cps

    def start(ins, outs, sems):
        for cp in first_copies(ins, outs, sems):
            cp.start()

    def finish(ins, outs, sems):
        x, y, c, chips = _place()
        sib = (x, y, 1 - c)
        passed = []
        for i, (w_ref, o_ref, kind) in enumerate(zip(ins, outs, kinds)):
            r, cw = w_ref.shape
            mine, _ = _halves(c, r // 2)
            for j, (px, py) in enumerate(chips):
                blk = _shard_ref(o_ref, kind, 2 * px + py, mine, cw)
                _remote(blk, blk, sems[0].at[6 * i + j], sems[1].at[6 * i + j], sib).wait_recv()
                passed.append(_remote(blk, blk, sems[0].at[6 * i + 3 + j], sems[1].at[6 * i + 3 + j], sib))
                passed[-1].start()
        for i, (w_ref, o_ref, kind) in enumerate(zip(ins, outs, kinds)):
            r, cw = w_ref.shape
            _, theirs = _halves(c, r // 2)
            for j, (px, py) in enumerate(chips):
                blk = _shard_ref(o_ref, kind, 2 * px + py, theirs, cw)
                _remote(blk, blk, sems[0].at[6 * i + 3 + j], sems[1].at[6 * i + 3 + j], sib).wait_recv()
        for cp in first_copies(ins, outs, sems) + passed:
            cp.wait_send()

    return start, finish


def _gather_shapes(shards, kinds):
    return [jax.ShapeDtypeStruct((a.shape[0], N_CHIPS * a.shape[1]) if k == COLS else (N_CHIPS,) + a.shape, a.dtype)
            for a, k in zip(shards, kinds)]


def _gather_sems(nw):
    return [pltpu.SemaphoreType.DMA((6 * nw,)), pltpu.SemaphoreType.DMA((6 * nw,))]


def _gather_shards(shards, kinds):
    nw = len(shards)
    start, finish = _gather_parts(kinds)

    def body(*refs):
        ins, outs, sems = refs[:nw], refs[nw:2 * nw], refs[2 * nw:]
        start(ins, outs, sems)
        finish(ins, outs, sems)

    return pl.pallas_call(
        body, name="gather_shards", in_specs=[ANY] * nw, out_specs=[ANY] * nw,
        out_shape=_gather_shapes(shards, kinds), scratch_shapes=_gather_sems(nw),
    )(*shards)


def _gather_rider(shards, kinds):
    start, finish = _gather_parts(kinds)
    return _Rider(list(shards), _gather_shapes(shards, kinds), _gather_sems(len(shards)), start, finish)


def _half_shape(g, kind):
    if kind == COLS:
        return (g.shape[0] // 2, g.shape[1])
    return (g.shape[0], g.shape[1] // 2, g.shape[2])


def _swap_parts(kinds):
    def copies(ins, outs, sems):
        x, y, c, _ = _place()
        cps = []
        for i, (g_ref, a_ref, kind) in enumerate(zip(ins, outs, kinds)):
            r = g_ref.shape[0] if kind == COLS else g_ref.shape[1]
            _, theirs = _halves(c, r // 2)
            src = g_ref.at[theirs] if kind == COLS else g_ref.at[:, theirs]
            cps.append(_remote(src, a_ref, sems[0].at[i], sems[1].at[i], (x, y, 1 - c)))
        return cps

    def start(ins, outs, sems):
        for cp in copies(ins, outs, sems):
            cp.start()

    def finish(ins, outs, sems):
        for cp in copies(ins, outs, sems):
            cp.wait()

    return start, finish


def _swap_shapes(gs, kinds):
    return [jax.ShapeDtypeStruct(_half_shape(g, k), g.dtype) for g, k in zip(gs, kinds)]


def _pair_swap_halves(gs, kinds, name):
    nw = len(gs)
    start, finish = _swap_parts(kinds)

    def body(*refs):
        ins, outs, sems = refs[:nw], refs[nw:2 * nw], refs[2 * nw:]
        start(ins, outs, sems)
        finish(ins, outs, sems)

    return pl.pallas_call(
        body, name=name, in_specs=[ANY] * nw, out_specs=[ANY] * nw, out_shape=_swap_shapes(gs, kinds),
        scratch_shapes=[pltpu.SemaphoreType.DMA((nw,)), pltpu.SemaphoreType.DMA((nw,))],
    )(*gs)


def _swap_rider(gs, kinds):
    start, finish = _swap_parts(kinds)
    nw = len(gs)
    return _Rider(list(gs), _swap_shapes(gs, kinds), [pltpu.SemaphoreType.DMA((nw,)), pltpu.SemaphoreType.DMA((nw,))],
                  start, finish)


def _row_tile(rows, width, itemsize=4, target=2 ** 21):
    return _tile(rows, max(8, target // (width * itemsize)), 8)


def _add_half(g, a, kind, c_idx, name):
    if kind == COLS:
        half, wd = a.shape
        tr = _row_tile(half, wd)
        nblk = half // tr
        grid = (nblk,)
        g_spec = pl.BlockSpec((tr, wd), lambda i, c_ref: (c_ref[0] * nblk + i, 0))
        a_spec = pl.BlockSpec((tr, wd), lambda i, c_ref: (i, 0))
    else:
        n, half, wd = a.shape
        tr = _row_tile(half, wd)
        nblk = half // tr
        grid = (n, nblk)
        g_spec = pl.BlockSpec((1, tr, wd), lambda s, i, c_ref: (s, c_ref[0] * nblk + i, 0))
        a_spec = pl.BlockSpec((1, tr, wd), lambda s, i, c_ref: (s, i, 0))

    def body(c_ref, g_ref, a_ref, o_ref):
        o_ref[...] = (g_ref[...] + a_ref[...]).astype(o_ref.dtype)

    return pl.pallas_call(
        body, name=name,
        grid_spec=pltpu.PrefetchScalarGridSpec(num_scalar_prefetch=1, grid=grid, in_specs=[g_spec, a_spec],
                                               out_specs=a_spec),
        out_shape=jax.ShapeDtypeStruct(a.shape, EXCHANGE_DTYPE), compiler_params=_params(len(grid)),
    )(c_idx, g, a)


def _exchange_parts(kinds):
    def copies(ins, outs, sems):
        x, y, c, chips = _place()
        me = 2 * x + y
        cps = []
        for i, (p_ref, b_ref, kind) in enumerate(zip(ins, outs, kinds)):
            cw = b_ref.shape[2]
            for j, (px, py) in enumerate(chips):
                cps.append(_remote(_shard_ref(p_ref, kind, 2 * px + py, None, cw), b_ref.at[me],
                                   sems[0].at[3 * i + j], sems[1].at[3 * i + j], (px, py, c)))
        return cps

    def start(ins, outs, sems):
        for cp in copies(ins, outs, sems):
            cp.start()

    def finish(ins, outs, sems):
        x, y, c, chips = _place()
        for i, b_ref in enumerate(outs):
            for j, (px, py) in enumerate(chips):
                blk = b_ref.at[2 * px + py]
                _remote(blk, blk, sems[0].at[3 * i + j], sems[1].at[3 * i + j], (px, py, c)).wait_recv()
        for cp in copies(ins, outs, sems):
            cp.wait_send()

    return start, finish


def _exchange_shapes(ps, kinds):
    return [jax.ShapeDtypeStruct((N_CHIPS,) + ((p.shape[0], p.shape[1] // N_CHIPS) if k == COLS else tuple(p.shape[1:])),
                                 p.dtype) for p, k in zip(ps, kinds)]


def _exchange_sems(nw):
    return [pltpu.SemaphoreType.DMA((3 * nw,)), pltpu.SemaphoreType.DMA((3 * nw,))]


def _exchange_rider(ps, kinds):
    start, finish = _exchange_parts(kinds)
    return _Rider(list(ps), _exchange_shapes(ps, kinds), _exchange_sems(len(ps)), start, finish)


def _sum_chips(bq, name):
    n, h, wd = bq.shape
    tr = _row_tile(h, wd * n)

    def body(b_ref, o_ref):
        acc = b_ref[0].astype(F32)
        for s in range(1, n):
            acc = acc + b_ref[s].astype(F32)
        o_ref[...] = acc

    return pl.pallas_call(
        body, name=name, grid=(h // tr,),
        in_specs=[pl.BlockSpec((n, tr, wd), lambda i: (0, i, 0))], out_specs=pl.BlockSpec((tr, wd), lambda i: (i, 0)),
        out_shape=jax.ShapeDtypeStruct((h, wd), F32), compiler_params=_params(1),
    )(bq)


def _pair_join_halves(qs):
    nw = len(qs)

    def body(*refs):
        ins, outs = refs[:nw], refs[nw:2 * nw]
        send_sems, recv_sems = refs[2 * nw:]
        x, y, c, _ = _place()
        sent = []
        for i, (q_ref, o_ref) in enumerate(zip(ins, outs)):
            mine, _ = _halves(c, q_ref.shape[0])
            sent.append(_remote(q_ref, o_ref.at[mine], send_sems.at[i], recv_sems.at[i], (x, y, 1 - c)))
            sent[-1].start()
        for i, (q_ref, o_ref) in enumerate(zip(ins, outs)):
            _, theirs = _halves(c, q_ref.shape[0])
            _remote(q_ref, o_ref.at[theirs], send_sems.at[i], recv_sems.at[i], (x, y, 1 - c)).wait_recv()
        for cp in sent:
            cp.wait_send()

    return pl.pallas_call(
        body, name="pair_join_halves", in_specs=[ANY] * nw, out_specs=[ANY] * nw,
        out_shape=[jax.ShapeDtypeStruct((2 * q.shape[0], q.shape[1]), q.dtype) for q in qs],
        scratch_shapes=[pltpu.SemaphoreType.DMA((nw,)), pltpu.SemaphoreType.DMA((nw,))],
    )(*qs)


def _all_sum_small(s, name):
    sr, w = s.shape

    def body(s_ref, o_ref, buf, send_sems, recv_sems):
        x, y, c, _ = _place()
        me = 4 * x + 2 * y + c
        buf[me] = s_ref[...]
        peers = []
        for k in range(1, 8):
            px = 1 - x if k & 4 else x
            py = 1 - y if k & 2 else y
            pc = 1 - c if k & 1 else c
            peers.append((px, py, pc))
        sent = [_remote(s_ref, buf.at[me], send_sems.at[k], recv_sems.at[k], peer) for k, peer in enumerate(peers)]
        for cp in sent:
            cp.start()
        for k, (px, py, pc) in enumerate(peers):
            _remote(s_ref, buf.at[4 * px + 2 * py + pc], send_sems.at[k], recv_sems.at[k], (px, py, pc)).wait_recv()
        for cp in sent:
            cp.wait_send()
        acc = buf[0]
        for d in range(1, 8):
            acc = acc + buf[d]
        o_ref[...] = acc

    vm = pl.BlockSpec(memory_space=pltpu.VMEM)
    return pl.pallas_call(
        body, name=name, in_specs=[vm], out_specs=vm, out_shape=jax.ShapeDtypeStruct((sr, w), F32),
        scratch_shapes=[pltpu.VMEM((8, sr, w), F32), pltpu.SemaphoreType.DMA((7,)), pltpu.SemaphoreType.DMA((7,))],
    )(s)


BIG = ("w_in", "mem_kv_w", "w_br_hgrn", "w_br_fox", "w_br_mem", "w_out", "ffn_w_up", "ffn_w_down")
KIND = {"w_in": STACK, "mem_kv_w": STACK, "w_br_hgrn": COLS, "w_br_fox": COLS, "w_br_mem": COLS, "w_out": STACK,
        "ffn_w_up": COLS, "ffn_w_down": STACK}
ROW_SHARDED = ("mem_kv_w", "w_out", "ffn_w_down")
FIRST = ("w_in",)
REST = tuple(nm for nm in BIG if nm not in FIRST)
LAST = ("w_in",)


def _put_shard(arr, kind, s, piece):
    if kind == COLS:
        return lax.dynamic_update_slice(arr, piece, (0, s * piece.shape[1]))
    return lax.dynamic_update_slice(arr, piece[None], (s, 0, 0))


def _take_shard(arr, kind, s):
    if kind == COLS:
        return lax.dynamic_slice(arr, (0, s * (arr.shape[1] // N_CHIPS)), (arr.shape[0], arr.shape[1] // N_CHIPS))
    return lax.dynamic_index_in_dim(arr, s, 0, keepdims=False)


def _w_in_pieces(cs, s1, nf):
    out = []
    for s in range(N_CHIPS):
        lo, hi = cs * s, cs * (s + 1)
        for a, b, forget in ((lo, min(hi, s1), False), (max(lo, s1), min(hi, s1 + nf), True), (max(lo, s1 + nf), hi, False)):
            if a < b:
                out.append((s, a - lo, b - lo, forget, a - s1 if forget else (a if a < s1 else a - nf)))
    return out


def _split_w_in(stacked, s1, nf):
    pieces = _w_in_pieces(stacked.shape[2], s1, nf)
    main = [stacked[s, :, a:b] for s, a, b, forget, _ in pieces if not forget]
    ff = [stacked[s, :, a:b] for s, a, b, forget, _ in pieces if forget]
    return jnp.concatenate(main, axis=1), jnp.concatenate(ff, axis=1)


def _join_w_in(g_main, g_ff, s1, nf):
    cs = (g_main.shape[1] + nf) // N_CHIPS
    shards = [[] for _ in range(N_CHIPS)]
    for s, a, b, forget, off in _w_in_pieces(cs, s1, nf):
        shards[s].append((g_ff if forget else g_main)[:, off:off + b - a])
    return jnp.stack([jnp.concatenate(p, axis=1) if len(p) > 1 else p[0] for p in shards])


SMALL = ("norm_mix_g", "norm_mem_g", "norm_ffn_g", "hgrn_lb_logits", "hgrn_norm_g", "fox_f_bias", "fox_q_norm_g",
         "fox_k_norm_g", "mem_q_norm_g", "mem_k_norm_g", "ffn_conv_b")


def _pack_small(vals):
    flats, total = [], 0
    for v in vals:
        flat = v.reshape(-1).astype(F32)
        n = -(-flat.shape[0] // FLAT_W)
        flats.append(jnp.pad(flat, (0, n * FLAT_W - flat.shape[0])))
        total += n
    if -total % 8:
        flats.append(jnp.zeros((-total % 8 * FLAT_W,), F32))
    return jnp.concatenate(flats).reshape(-1, FLAT_W)


def _unpack_small(buf, shapes):
    res, off = [], 0
    for shp in shapes:
        numel = math.prod(shp)
        n = -(-numel // FLAT_W)
        res.append(buf[off:off + n].reshape(-1)[:numel].reshape(shp))
        off += n
    return res


def _pad_lanes(v, width=LANE):
    return jnp.pad(v, ((0, 0), (0, width - v.shape[1])))


WEIGHTS = ("norm_mix_g", "norm_mem_g", "w_in", "hgrn_lb_logits", "hgrn_norm_g", "fox_f_bias", "fox_q_norm_g",
           "fox_k_norm_g", "mem_kv_w", "mem_q_norm_g", "mem_k_norm_g", "w_br_hgrn", "w_br_fox", "w_br_mem", "w_out",
           "norm_ffn_g", "ffn_w_up", "ffn_conv_w", "ffn_conv_b", "ffn_w_down")


def _local_step(x, mem, target, w, full, conv_w, late=None, hooks=None):
    b_, t_, d = x.shape
    n = b_ * t_
    hw, fw, mw = HG_HEADS * HG_D, FOX_HEADS * FOX_DH, MEM_HEADS * MEM_DH
    m_ = mem.shape[1]
    f = conv_w.shape[1]
    s1 = 4 * hw + 3 * fw
    fox_col, mem_col, gate_col = 4 * hw // LANE, s1 // LANE, (s1 + mw) // d

    w_main, w_ff = _split_w_in(full["w_in"], s1, FOX_HEADS)
    w_ff = _pad_lanes(w_ff)
    f_bias = _pad_lanes(w["fox_f_bias"])
    cb = w["ffn_conv_b"]

    x2 = x.reshape(n, d)
    h = _rmsnorm_fwd(x2, w["norm_mix_g"], name="norm_mix_fwd")
    if late:
        zm, gathered = _matmul(h, w_main, name="in_proj", rider=_gather_rider(late[0], late[1]))
        full = {**full, **late[2](gathered)}
    else:
        zm = _matmul(h, w_main, name="in_proj")
    w_up = full["ffn_w_up"]
    w_brs = [full["w_br_hgrn"], full["w_br_fox"], full["w_br_mem"]]
    w_out, w_kv, w_down = full["w_out"], full["mem_kv_w"], full["ffn_w_down"]
    zf = _matmul(h, w_ff, name="in_proj_forget")
    zm3, zf3 = zm.reshape(b_, t_, -1), zf.reshape(b_, t_, LANE)
    ya = _hgrn_fwd(zm3, w["hgrn_lb_logits"], w["hgrn_norm_g"], hw)
    fc = _fox_prep(zf3, f_bias)
    fox_gq, fox_gk = jnp.tile(w["fox_q_norm_g"], (1, 2)), jnp.tile(w["fox_k_norm_g"], (1, 2))
    yb, lse = _fox_fwd(zm3, fc, fox_gq, fox_gk, fw, fox_col)
    mem2 = mem.reshape(b_ * m_, d)
    hm = _rmsnorm_fwd(mem2, w["norm_mem_g"], name="norm_mem_fwd")
    mkv = _matmul(hm, w_kv, name="mem_kv_proj").reshape(b_, m_, 2 * mw)
    yc = _mem_fwd(zm3, mkv, w["mem_q_norm_g"], w["mem_k_norm_g"], mw, mem_col)
    ys = [ya.reshape(n, hw), yb.reshape(n, fw), yc.reshape(n, mw)]
    x1, merged = _merge_fwd(x2, ys, zm, w_brs, w_out, gate_col)
    h2 = _rmsnorm_fwd(x1, w["norm_ffn_g"], name="norm_ffn_fwd")
    u = _matmul(h2, w_up, name="ffn_up")
    u3 = u.reshape(b_, t_, 2 * f)
    yff = _glu_fwd(u3, conv_w, cb).reshape(n, f)
    ffn = _matmul(yff, w_down, name="ffn_down")
    dy, loss_vec = _loss_head(x1, ffn, target.reshape(n, d))

    grads = {}

    def ridden(name, call):
        if not hooks or name not in hooks:
            return call(None)[0]
        rider, then = hooks[name](grads)
        outs, extra = call(rider)
        then(extra)
        return outs

    dyff = _matmul(dy, w_down, tb=True, name="ffn_down_dx")
    grads["ffn_w_down"] = _matmul(yff, dy, ta=True, name="ffn_down_dw", tm=1408)
    du_a, du_v, grads["ffn_conv_w"], grads["ffn_conv_b"] = _glu_bwd(u3, dyff.reshape(b_, t_, f), conv_w, cb)
    du2 = jnp.concatenate([du_a, du_v], axis=-1).reshape(n, 2 * f)
    dh2 = _matmul(du2, w_up, tb=True, name="ffn_up_dx")
    grads["ffn_w_up"] = _matmul(h2, du2, ta=True, name="ffn_up_dw")
    dx1, grads["norm_ffn_g"] = _rmsnorm_bwd(x1, [dh2], w["norm_ffn_g"], dy, name="norm_ffn_bwd")

    dgl, dpa, dpb, dpc, dya, dyb, dyc = _merge_bwd(dx1, ys, zm, w_brs, w_out, gate_col)
    grads["w_out"] = _matmul(merged, dx1, ta=True, name="out_proj_dw")
    for nm, y_, dp_ in zip(("w_br_hgrn", "w_br_fox", "w_br_mem"), ys, (dpa, dpb, dpc)):
        grads[nm] = _matmul(y_, dp_, ta=True, name=nm + "_dw")

    dmq, dmk, dmv, grads["mem_q_norm_g"], grads["mem_k_norm_g"] = _mem_bwd(
        zm3, mkv, dyc.reshape(b_, t_, mw), w["mem_q_norm_g"], w["mem_k_norm_g"], mw, mem_col)
    dmkv = jnp.concatenate([dmk, dmv], axis=-1).reshape(b_ * m_, 2 * mw)
    grads["mem_kv_w"] = _matmul(hm, dmkv, ta=True, name="mem_kv_dw")
    dhm = _matmul(dmkv, w_kv, tb=True, name="mem_kv_dx")
    _, grads["norm_mem_g"] = _rmsnorm_bwd(mem2, [dhm], w["norm_mem_g"], None, name="norm_mem_bwd")

    dfq, dfk, dfv, dfc, g_fq, g_fk = ridden("fox_bwd", lambda rider: _fox_bwd(
        zm3, yb, dyb.reshape(b_, t_, fw), lse, fc, fox_gq, fox_gk, fw, fox_col, rider))
    grads["fox_q_norm_g"] = g_fq[:, :FOX_DH] + g_fq[:, FOX_DH:]
    grads["fox_k_norm_g"] = g_fk[:, :FOX_DH] + g_fk[:, FOX_DH:]
    dfc = dfc[..., :2].transpose(0, 2, 1, 3).reshape(b_, t_, FOX_HEADS)
    dfc = jnp.pad(dfc, ((0, 0), (0, 0), (0, LANE - FOX_HEADS)))
    dzf, g_fb = _fox_post(dfc, zf3, f_bias)
    grads["fox_f_bias"] = g_fb[:, :FOX_HEADS]

    dhq, dhf, dhi, dhg, grads["hgrn_lb_logits"], grads["hgrn_norm_g"] = ridden("hgrn_bwd", lambda rider: _hgrn_bwd(
        zm3, dya.reshape(b_, t_, hw), w["hgrn_lb_logits"], w["hgrn_norm_g"], hw, rider))

    dzm = jnp.concatenate([dhq, dhf, dhi, dhg, dfq, dfk, dfv, dmq, dgl.reshape(b_, t_, 3 * d)], axis=-1).reshape(n, -1)
    dzf2 = dzf.reshape(n, LANE)
    g_main = _matmul(h, dzm, ta=True, name="in_proj_dw")
    g_ff = _matmul(h, dzf2, ta=True, name="in_proj_forget_dw")
    grads["w_in"] = _join_w_in(g_main, g_ff[:, :FOX_HEADS], s1, FOX_HEADS)

    def in_proj_dx(rider):
        out = _matmul(dzm, w_main, tb=True, name="in_proj_dx", rider=rider)
        return ([out[0]], out[1]) if rider else ([out], None)

    dh_a, = ridden("in_proj_dx", in_proj_dx)
    dh_b = _matmul(dzf2, w_ff, tb=True, name="in_proj_forget_dx")
    grad_x, grads["norm_mix_g"] = _rmsnorm_bwd(x2, [dh_a, dh_b], w["norm_mix_g"], dx1, name="norm_mix_bwd")
    return loss_vec, grad_x.reshape(b_, t_, d), grads


def kernel(x, mem, norm_mix_g, norm_mem_g, w_in, hgrn_lb_logits, hgrn_norm_g, fox_f_bias, fox_q_norm_g, fox_k_norm_g, mem_kv_w, mem_q_norm_g, mem_k_norm_g, w_br_hgrn, w_br_fox, w_br_mem, w_out, norm_ffn_g, ffn_w_up, ffn_conv_w, ffn_conv_b, ffn_w_down, loss_target, m_norm_mix_g, m_norm_mem_g, m_w_in, m_hgrn_lb_logits, m_hgrn_norm_g, m_fox_f_bias, m_fox_q_norm_g, m_fox_k_norm_g, m_mem_kv_w, m_mem_q_norm_g, m_mem_k_norm_g, m_w_br_hgrn, m_w_br_fox, m_w_br_mem, m_w_out, m_norm_ffn_g, m_ffn_w_up, m_ffn_conv_w, m_ffn_conv_b, m_ffn_w_down, v_norm_mix_g, v_norm_mem_g, v_w_in, v_hgrn_lb_logits, v_hgrn_norm_g, v_fox_f_bias, v_fox_q_norm_g, v_fox_k_norm_g, v_mem_kv_w, v_mem_q_norm_g, v_mem_k_norm_g, v_w_br_hgrn, v_w_br_fox, v_w_br_mem, v_w_out, v_norm_ffn_g, v_ffn_w_up, v_ffn_conv_w, v_ffn_conv_b, v_ffn_w_down):
    w = dict(zip(WEIGHTS, (norm_mix_g, norm_mem_g, w_in, hgrn_lb_logits, hgrn_norm_g, fox_f_bias, fox_q_norm_g,
                           fox_k_norm_g, mem_kv_w, mem_q_norm_g, mem_k_norm_g, w_br_hgrn, w_br_fox, w_br_mem, w_out,
                           norm_ffn_g, ffn_w_up, ffn_conv_w, ffn_conv_b, ffn_w_down)))
    m = dict(zip(WEIGHTS, (m_norm_mix_g, m_norm_mem_g, m_w_in, m_hgrn_lb_logits, m_hgrn_norm_g, m_fox_f_bias,
                           m_fox_q_norm_g, m_fox_k_norm_g, m_mem_kv_w, m_mem_q_norm_g, m_mem_k_norm_g, m_w_br_hgrn,
                           m_w_br_fox, m_w_br_mem, m_w_out, m_norm_ffn_g, m_ffn_w_up, m_ffn_conv_w, m_ffn_conv_b,
                           m_ffn_w_down)))
    v = dict(zip(WEIGHTS, (v_norm_mix_g, v_norm_mem_g, v_w_in, v_hgrn_lb_logits, v_hgrn_norm_g, v_fox_f_bias,
                           v_fox_q_norm_g, v_fox_k_norm_g, v_mem_kv_w, v_mem_q_norm_g, v_mem_k_norm_g, v_w_br_hgrn,
                           v_w_br_fox, v_w_br_mem, v_w_out, v_norm_ffn_g, v_ffn_w_up, v_ffn_conv_w, v_ffn_conv_b,
                           v_ffn_w_down)))
    c_idx = lax.axis_index("c")
    chip = 2 * lax.axis_index("x") + lax.axis_index("y")

    mine = {nm: w[nm][0].astype(MXU_DTYPE) for nm in BIG}

    def gathered_full(names, arrays):
        out = {nm: _put_shard(g, KIND[nm], chip, mine[nm]) for nm, g in zip(names, arrays)}
        return {nm: g.reshape(-1, g.shape[2]) if nm in ROW_SHARDED else g for nm, g in out.items()}

    full = gathered_full(FIRST, _gather_shards([mine[nm] for nm in FIRST], [KIND[nm] for nm in FIRST]))
    late = ([mine[nm] for nm in REST], [KIND[nm] for nm in REST], lambda arrays: gathered_full(REST, arrays))
    cs = ffn_conv_w.shape[2]
    f = cs * N_CHIPS
    placed = lax.dynamic_update_slice(jnp.zeros((3, f), F32), ffn_conv_w[0] * (c_idx == 0).astype(F32), (0, chip * cs))
    conv_w = _unpack_small(_all_sum_small(_pack_small([placed]), "gather_conv_w"), [(3, f)])[0]

    c_arr = jnp.reshape(c_idx, (1,)).astype(jnp.int32)

    def stacked(nm, g):
        return g.reshape(N_CHIPS, -1, g.shape[1]) if nm in ROW_SHARDED else g

    def with_own(landed, partial, kinds):
        return [_put_shard(bq, STACK, chip, _take_shard(p, k, chip)) for bq, p, k in zip(landed, partial, kinds)]

    kinds_rest, kinds_last = [KIND[nm] for nm in REST], [KIND[nm] for nm in LAST]
    state = {}

    def swap_rest(grads):
        gs = [stacked(nm, grads[nm]) for nm in REST]

        def then(from_sibling):
            state["partial_rest"] = [_add_half(g, a, k, c_arr, "add_half_" + nm)
                                     for g, a, k, nm in zip(gs, from_sibling, kinds_rest, REST)]

        return _swap_rider(gs, kinds_rest), then

    def exchange_rest(grads):
        def then(landed):
            state["landed_rest"] = with_own(landed, state["partial_rest"], kinds_rest)

        return _exchange_rider(state["partial_rest"], kinds_rest), then

    def exchange_last(grads):
        gs = [stacked(nm, grads[nm]) for nm in LAST]
        from_sibling = _pair_swap_halves(gs, kinds_last, "pair_swap_halves_last")
        partial = [_add_half(g, a, k, c_arr, "add_half_" + nm) for g, a, k, nm in zip(gs, from_sibling, kinds_last, LAST)]

        def then(landed):
            state["landed_last"] = with_own(landed, partial, kinds_last)

        return _exchange_rider(partial, kinds_last), then

    hooks = {"fox_bwd": swap_rest, "hgrn_bwd": exchange_rest, "in_proj_dx": exchange_last}

    loss_vec, grad_x, grads = _local_step(x, mem, loss_target, w, full, conv_w, late, hooks)

    landed = dict(zip(LAST + REST, state["landed_last"] + state["landed_rest"]))
    reduced_half = [_sum_chips(landed[nm], "sum_chips_" + nm) for nm in BIG]
    joined = [lax.dynamic_update_slice(o, q, (c_idx * q.shape[0], 0))
              for o, q in zip(_pair_join_halves(reduced_half), reduced_half)]
    gshards = dict(zip(BIG, joined))

    small_names = SMALL + ("ffn_conv_w",)
    summed = _unpack_small(
        _all_sum_small(_pack_small([grads[nm] for nm in small_names] + [loss_vec]), "all_sum_small_grads"),
        [grads[nm].shape for nm in small_names] + [loss_vec.shape])
    gsmall = dict(zip(small_names, summed[:-1]))
    loss = jnp.sum(summed[-1])
    g_out = {nm: gshards[nm][None] for nm in BIG}
    for nm in SMALL:
        g_out[nm] = gsmall[nm].reshape(w[nm].shape)
    g_out["ffn_conv_w"] = lax.dynamic_slice(gsmall["ffn_conv_w"], (0, chip * cs), (3, cs))[None]

    delta, new_m, new_v = {}, {}, {}
    for nm in BIG + ("ffn_conv_w",):
        delta[nm], new_m[nm], new_v[nm] = _adamw(w[nm], g_out[nm], m[nm], v[nm], name="adamw_" + nm)
    packed = [_pack_small([t[nm] for nm in SMALL])[None] for t in (w, g_out, m, v)]
    outs = _adamw(*packed, name="adamw_small")
    shapes = [w[nm].shape for nm in SMALL]
    for res, o in zip((delta, new_m, new_v), outs):
        res.update(zip(SMALL, _unpack_small(o[0], shapes)))

    return (loss, grad_x, *[g_out[nm] for nm in WEIGHTS], *[delta[nm] for nm in WEIGHTS],
            *[new_m[nm] for nm in WEIGHTS], *[new_v[nm] for nm in WEIGHTS])
```

```python
import functools
import math

import jax
import jax.numpy as jnp
from jax import lax
from jax.experimental import pallas as pl
from jax.experimental.pallas import tpu as pltpu

F32 = jnp.float32
BF16 = jnp.bfloat16
MXU_DTYPE = jnp.bfloat16
EXCHANGE_DTYPE = jnp.bfloat16

EPS = 1e-6
HG_HEADS, HG_D = 4, 128
FOX_HEADS, FOX_DH = 8, 64
MEM_HEADS, MEM_DH = 4, 128
HG_CHUNK = 64
FOX_BLOCK = 256
LANE = 128
FFN_GROUP = 256
FLAT_W = 1024
VMEM_LIMIT = 56 * 2 ** 20
NEG = -1e30
N_CHIPS = 4

ADAM_LR, ADAM_B1, ADAM_B2, ADAM_EPS, ADAM_WD, ADAM_STEP = 0.001, 0.9, 0.999, 1e-08, 0.01, 10

MESH = pl.DeviceIdType.MESH
ANY = pl.BlockSpec(memory_space=pl.ANY)


def _mx(x):
    return x.astype(MXU_DTYPE)


def _dot(a, b, ca, cb):
    return lax.dot_general(_mx(a), _mx(b), (((ca,), (cb,)), ((), ())), preferred_element_type=F32)


def _nn(a, b):
    return _dot(a, b, 1, 0)


def _nt(a, b):
    return _dot(a, b, 1, 1)


def _tn(a, b):
    return _dot(a, b, 0, 0)


def _dotp(a, b, ca, cb):
    return lax.dot_general(a, b, (((ca,), (cb,)), ((), ())), precision=lax.Precision.HIGHEST,
                           preferred_element_type=F32)


def _tri_dot(tri_bf, x):
    hi = x.astype(BF16)
    r = x - hi.astype(F32)
    mid = r.astype(BF16)
    lo = (r - mid.astype(F32)).astype(BF16)

    def d(v):
        return lax.dot_general(tri_bf, v, (((1,), (0,)), ((), ())), preferred_element_type=F32)

    return d(hi) + d(mid) + d(lo)


def _sig(x):
    return jax.nn.sigmoid(x)


def _erf(x):
    a = jnp.abs(x)
    t = 1.0 / (1.0 + 0.3275911 * a)
    poly = t * (0.254829592 + t * (-0.284496736 + t * (1.421413741 + t * (-1.453152027 + t * 1.061405429))))
    y = 1.0 - poly * jnp.exp(-a * a)
    return jnp.where(x < 0, -y, y)


def _tile(dim, pref, unit=LANE):
    if dim <= pref:
        return dim
    t = pref - pref % unit
    while t >= unit:
        if dim % t == 0:
            return t
        t -= unit
    return dim


def _params(n_grid):
    return pltpu.CompilerParams(dimension_semantics=("arbitrary",) * n_grid, vmem_limit_bytes=VMEM_LIMIT)


def _acc(ref, val, first):
    @pl.when(first)
    def _():
        ref[...] = val

    @pl.when(jnp.logical_not(first))
    def _():
        ref[...] += val


class _Rider:
    def __init__(self, inputs, out_shapes, scratch, start, finish):
        self.inputs, self.out_shapes, self.scratch, self.start, self.finish = inputs, out_shapes, scratch, start, finish


def _ride(body, rider, n_in, n_out, grid):
    if rider is None:
        return body
    ri, ro, rs = len(rider.inputs), len(rider.out_shapes), len(rider.scratch)

    def wrapped(*refs):
        a, b, c = n_in + ri, n_in + ri + n_out, n_in + ri + n_out + ro
        base = refs[:n_in] + refs[a:b] + refs[c:len(refs) - rs]
        r_in, r_out, r_scr = refs[n_in:a], refs[b:c], refs[len(refs) - rs:]
        step = pl.program_id(0)
        for ax in range(1, len(grid)):
            step = step * grid[ax] + pl.program_id(ax)

        @pl.when(step == 0)
        def _():
            rider.start(r_in, r_out, r_scr)

        body(*base)

        @pl.when(step == math.prod(grid) - 1)
        def _():
            rider.finish(r_in, r_out, r_scr)

    return wrapped


def _ride_call(body, rider, *, name, grid, in_specs, out_specs, out_shape, scratch, args):
    n_in, n_out = len(in_specs), len(out_specs)
    if rider is None:
        outs = pl.pallas_call(body, name=name, grid=grid, in_specs=in_specs, out_specs=out_specs, out_shape=out_shape,
                              scratch_shapes=scratch, compiler_params=_params(len(grid)))(*args)
        return list(outs), None
    outs = pl.pallas_call(
        _ride(body, rider, n_in, n_out, grid), name=name, grid=grid,
        in_specs=list(in_specs) + [ANY] * len(rider.inputs), out_specs=list(out_specs) + [ANY] * len(rider.out_shapes),
        out_shape=list(out_shape) + list(rider.out_shapes), scratch_shapes=list(scratch) + list(rider.scratch),
        compiler_params=_params(len(grid)),
    )(*args, *rider.inputs)
    return list(outs[:n_out]), list(outs[n_out:])


def _matmul(a, b, *, name, ta=False, tb=False, tm=1024, tn=2048, tk=None, rider=None):
    m, k = (a.shape[1], a.shape[0]) if ta else a.shape
    n = b.shape[0] if tb else b.shape[1]
    tk = tk or (1024 if ta else 2048)
    tm, tn, tk = _tile(m, tm), _tile(n, tn), _tile(k, tk)
    nk = k // tk

    def body(a_ref, b_ref, o_ref):
        p = _dot(a_ref[...], b_ref[...], 0 if ta else 1, 1 if tb else 0)
        if nk == 1:
            o_ref[...] = p
        else:
            _acc(o_ref, p, pl.program_id(2) == 0)

    a_spec = pl.BlockSpec((tk, tm), lambda i, j, kk: (kk, i)) if ta else pl.BlockSpec((tm, tk), lambda i, j, kk: (i, kk))
    b_spec = pl.BlockSpec((tn, tk), lambda i, j, kk: (j, kk)) if tb else pl.BlockSpec((tk, tn), lambda i, j, kk: (kk, j))
    outs, extra = _ride_call(
        body, rider, name=name, grid=(m // tm, n // tn, nk), in_specs=[a_spec, b_spec],
        out_specs=[pl.BlockSpec((tm, tn), lambda i, j, kk: (i, j))], out_shape=[jax.ShapeDtypeStruct((m, n), F32)],
        scratch=[], args=(a, b))
    return (outs[0], extra) if rider else outs[0]


def _matmul_rows(a_parts, b, *, name, tb, row_ins, vec_ins, epilogue, n_vec_out, tm=512, tk=2048, rider=None):
    m, kp = a_parts[0].shape
    n = b.shape[0] if tb else b.shape[1]
    tm, tk = _tile(m, tm, 8), _tile(kp, tk)
    nk = kp // tk
    n_a, n_row, n_vec = len(a_parts), len(row_ins), len(vec_ins)

    def body(*refs):
        a_refs, b_refs = refs[:n_a], refs[n_a:2 * n_a]
        rows = refs[2 * n_a:2 * n_a + n_row]
        vecs = refs[2 * n_a + n_row:2 * n_a + n_row + n_vec]
        o_ref = refs[2 * n_a + n_row + n_vec]
        v_refs = refs[2 * n_a + n_row + n_vec + 1:-1]
        acc_ref = refs[-1]
        i, kk = pl.program_id(0), pl.program_id(1)
        p = _dot(a_refs[0][...], b_refs[0][...], 1, 1 if tb else 0)
        for a_ref, b_ref in zip(a_refs[1:], b_refs[1:]):
            p = p + _dot(a_ref[...], b_ref[...], 1, 1 if tb else 0)
        _acc(acc_ref, p, kk == 0)

        @pl.when(kk == nk - 1)
        def _():
            out, vouts = epilogue(acc_ref[...], *[r[...] for r in rows], *[v[...] for v in vecs])
            o_ref[...] = out
            for v_ref, v in zip(v_refs, vouts):
                _acc(v_ref, v, i == 0)

    a_spec = pl.BlockSpec((tm, tk), lambda i, kk: (i, kk))
    b_specs = [pl.BlockSpec((n, tk), functools.partial(lambda i, kk, q: (0, q * nk + kk), q=q)) if tb else
               pl.BlockSpec((tk, n), functools.partial(lambda i, kk, q: (q * nk + kk, 0), q=q)) for q in range(n_a)]
    row = pl.BlockSpec((tm, n), lambda i, kk: (i, 0))
    vec = pl.BlockSpec((1, n), lambda i, kk: (0, 0))
    outs, extra = _ride_call(
        body, rider, name=name, grid=(m // tm, nk),
        in_specs=[a_spec] * n_a + b_specs + [row] * n_row + [vec] * n_vec,
        out_specs=[row] + [vec] * n_vec_out,
        out_shape=[jax.ShapeDtypeStruct((m, n), F32)] + [jax.ShapeDtypeStruct((1, n), F32)] * n_vec_out,
        scratch=[pltpu.VMEM((tm, n), F32)], args=(*a_parts, *([b] * n_a), *row_ins, *vec_ins))
    return outs[0], outs[1:], extra


def _norm_bwd_epilogue(n_dh):
    def epilogue(dh, x, res, *rest):
        for extra in rest[:n_dh]:
            dh = dh + extra
        g = rest[n_dh]
        r = lax.rsqrt(jnp.mean(x * x, axis=-1, keepdims=True) + EPS)
        dhg = dh * g
        dx = res + r * dhg - x * (r * r * r) * jnp.mean(dhg * x, axis=-1, keepdims=True)
        return dx, [jnp.sum(dh * x * r, axis=0, keepdims=True)]

    return epilogue


def _loss_epilogue(y, x1, target):
    d = y.shape[1]
    err = x1 + y - target
    return err * (1.0 / d), [jnp.sum(err * err, axis=0, keepdims=True) * (0.5 / d)]


def _rmsnorm_fwd(x, g, *, name, tm=512):
    n, d = x.shape
    tm = _tile(n, tm, 8)

    def body(x_ref, g_ref, o_ref):
        xv = x_ref[...]
        r = lax.rsqrt(jnp.mean(xv * xv, axis=-1, keepdims=True) + EPS)
        o_ref[...] = (xv * r * g_ref[...]).astype(o_ref.dtype)

    return pl.pallas_call(
        body, name=name, grid=(n // tm,),
        in_specs=[pl.BlockSpec((tm, d), lambda i: (i, 0)), pl.BlockSpec((1, d), lambda i: (0, 0))],
        out_specs=pl.BlockSpec((tm, d), lambda i: (i, 0)),
        out_shape=jax.ShapeDtypeStruct((n, d), MXU_DTYPE),
        compiler_params=_params(1),
    )(x, g)


def _rmsnorm_bwd(x, dhs, g, res, *, name, tm=512):
    n, d = x.shape
    tm = _tile(n, tm, 8)
    n_dh = len(dhs)
    has_res = res is not None

    def body(*refs):
        x_ref, dh_refs, g_ref = refs[0], refs[1:1 + n_dh], refs[1 + n_dh]
        res_ref = refs[2 + n_dh] if has_res else None
        dx_ref, dg_ref = refs[-2], refs[-1]
        xv = x_ref[...]
        dh = dh_refs[0][...].astype(F32)
        for r_ in dh_refs[1:]:
            dh = dh + r_[...].astype(F32)
        r = lax.rsqrt(jnp.mean(xv * xv, axis=-1, keepdims=True) + EPS)
        dhg = dh * g_ref[...]
        dx = r * dhg - xv * (r * r * r) * jnp.mean(dhg * xv, axis=-1, keepdims=True)
        if has_res:
            dx = dx + res_ref[...]
        dx_ref[...] = dx
        _acc(dg_ref, jnp.sum(dh * xv * r, axis=0, keepdims=True), pl.program_id(0) == 0)

    row = pl.BlockSpec((tm, d), lambda i: (i, 0))
    vec = pl.BlockSpec((1, d), lambda i: (0, 0))
    ins = [x] + list(dhs) + [g] + ([res] if has_res else [])
    return pl.pallas_call(
        body, name=name, grid=(n // tm,),
        in_specs=[row] * (1 + n_dh) + [vec] + ([row] if has_res else []),
        out_specs=[row, vec],
        out_shape=[jax.ShapeDtypeStruct((n, d), F32), jax.ShapeDtypeStruct((1, d), F32)],
        compiler_params=_params(1),
    )(*ins)


def _adamw(w, g, m, v, *, name, tr=256):
    _, r, c = w.shape
    tr = _tile(r, tr, 8)
    c1 = 1.0 / (1.0 - ADAM_B1 ** ADAM_STEP)
    c2 = 1.0 / (1.0 - ADAM_B2 ** ADAM_STEP)

    def body(w_ref, g_ref, m_ref, v_ref, d_ref, mo_ref, vo_ref):
        gv = g_ref[...]
        mn = ADAM_B1 * m_ref[...] + (1.0 - ADAM_B1) * gv
        vn = ADAM_B2 * v_ref[...] + (1.0 - ADAM_B2) * (gv * gv)
        d_ref[...] = -ADAM_LR * ((mn * c1) / (jnp.sqrt(vn * c2) + ADAM_EPS) + ADAM_WD * w_ref[...])
        mo_ref[...] = mn
        vo_ref[...] = vn

    blk = pl.BlockSpec((1, tr, c), lambda i: (0, i, 0))
    sds = jax.ShapeDtypeStruct((1, r, c), F32)
    return pl.pallas_call(
        body, name=name, grid=(r // tr,), in_specs=[blk] * 4, out_specs=[blk] * 3, out_shape=[sds] * 3,
        compiler_params=_params(1),
    )(w, g, m, v)


def _bdot(a, b, ca, cb):
    return lax.dot_general(_mx(a), _mx(b), (((ca,), (cb,)), ((0,), (0,))), preferred_element_type=F32)


def _bdotp(a, b, ca, cb):
    return lax.dot_general(a, b, (((ca,), (cb,)), ((0,), (0,))), precision=lax.Precision.HIGHEST,
                           preferred_element_type=F32)


def _tri_dot_b(tri_bf, x):
    hi = x.astype(BF16)
    r = x - hi.astype(F32)
    mid = r.astype(BF16)
    lo = (r - mid.astype(F32)).astype(BF16)

    def d(v):
        return lax.dot_general(tri_bf, v, (((2,), (1,)), ((0,), (0,))), preferred_element_type=F32)

    return d(hi) + d(mid) + d(lo)


def _hgrn_forward(hq, hf, hi, lbv, tril, tril_bf):
    nc, c, _ = hq.shape
    sf = _sig(hf)
    f = lbv + (1.0 - lbv) * sf
    k = 1.0 - f
    gcum = _tri_dot_b(tril_bf, jnp.log(f))
    mid = gcum[:, c // 2 - 1:c // 2, :]
    glast = gcum[:, c - 1:c, :]
    sq = _sig(hq)
    q = hq * sq
    e_q = jnp.exp(gcum - mid)
    e_k = jnp.exp(mid - gcum)
    qe, ke = q * e_q, k * e_k
    a = jnp.where(tril, _bdot(qe, ke, 2, 2), 0.0)
    e_g = jnp.exp(gcum)
    qg = q * e_g
    e_s = jnp.exp(glast - gcum)
    kg = k * e_s
    e_l = jnp.exp(glast)
    upd = _bdot(hi, kg, 1, 1)
    st = jnp.zeros((HG_D, HG_D), F32)
    states = []
    for n in range(nc):
        states.append(st)
        st = st * e_l[n] + upd[n]
    st_all = jnp.stack(states)
    o = _bdot(a, hi, 2, 1) + _bdot(qg, st_all, 2, 2)
    return dict(sf=sf, f=f, k=k, sq=sq, q=q, e_q=e_q, e_k=e_k, qe=qe, ke=ke, a=a, e_g=e_g, qg=qg, o=o,
                e_s=e_s, kg=kg, e_l=e_l, st_all=st_all)


def _hgrn_specs(t_, hw):
    nb = hw // LANE

    def col(off):
        return pl.BlockSpec((1, t_, LANE), lambda h, b: (b, 0, off * nb + h))

    vec = pl.BlockSpec((2, LANE), lambda h, b: (0, h))
    one = pl.BlockSpec((1, LANE), lambda h, b: (0, 0))
    blk = pl.BlockSpec((1, t_, LANE), lambda h, b: (b, 0, h))
    return col, vec, one, blk


def _chunk_masks(nc, c):
    row = lax.broadcasted_iota(jnp.int32, (nc, c, c), 1)
    cl = lax.broadcasted_iota(jnp.int32, (nc, c, c), 2)
    return row >= cl, (row >= cl).astype(BF16), (row <= cl).astype(BF16)


def _hgrn_fwd(zm, lb, gn, hw):
    b_, t_, _ = zm.shape
    c = min(HG_CHUNK, t_)
    nc = t_ // c
    col, vec, one, blk = _hgrn_specs(t_, hw)

    def body(q_ref, f_ref, i_ref, g_ref, lb_ref, gn_ref, y_ref):
        lbv, gnv = _sig(lb_ref[0:1, :] - lb_ref[1:2, :]), gn_ref[...]
        tril, tril_bf, _ = _chunk_masks(nc, c)
        chunks = lambda ref: ref[0].reshape(nc, c, LANE)
        o = _hgrn_forward(chunks(q_ref), chunks(f_ref), chunks(i_ref), lbv, tril, tril_bf)["o"]
        r = lax.rsqrt(jnp.mean(o * o, axis=-1, keepdims=True) + EPS)
        hg = chunks(g_ref)
        y_ref[0] = (o * r * gnv * (hg * _sig(hg))).reshape(t_, LANE)

    return pl.pallas_call(
        body, name="hgrn_fwd", grid=(HG_HEADS, b_),
        in_specs=[col(0), col(1), col(2), col(3), vec, one], out_specs=blk,
        out_shape=jax.ShapeDtypeStruct((b_, t_, hw), F32),
        compiler_params=_params(2),
    )(zm, zm, zm, zm, lb, gn)


def _hgrn_bwd(zm, dy, lb, gn, hw, rider=None):
    b_, t_, _ = zm.shape
    c = min(HG_CHUNK, t_)
    nc = t_ // c
    col, vec, one, blk = _hgrn_specs(t_, hw)

    def body(q_ref, f_ref, i_ref, g_ref, dy_ref, lb_ref, gn_ref, dq_ref, df_ref, di_ref, dg_ref, dlb_ref, dgn_ref):
        h, b = pl.program_id(0), pl.program_id(1)
        lbv, gnv = _sig(lb_ref[0:1, :] - lb_ref[1:2, :]), gn_ref[...]
        tril, tril_bf, triu_bf = _chunk_masks(nc, c)
        last_row = lax.broadcasted_iota(jnp.int32, (nc, c, LANE), 1) == c - 1
        chunks = lambda ref: ref[0].reshape(nc, c, LANE)
        flat = lambda x: x.reshape(t_, LANE)
        hq, hi, hg = chunks(q_ref), chunks(i_ref), chunks(g_ref)
        p = _hgrn_forward(hq, chunks(f_ref), hi, lbv, tril, tril_bf)
        o, q, k, st_all, e_l = p["o"], p["q"], p["k"], p["st_all"], p["e_l"]
        dyv = chunks(dy_ref)
        sg = _sig(hg)
        r = lax.rsqrt(jnp.mean(o * o, axis=-1, keepdims=True) + EPS)
        dn = dyv * (hg * sg)
        dg_ref[0] = flat(dyv * (o * r * gnv) * (sg * (1.0 + hg * (1.0 - sg)))).astype(dg_ref.dtype)
        dgn = jnp.sum(flat(dn * o * r), axis=0, keepdims=True)
        dng = dn * gnv
        do = r * dng - o * (r * r * r) * jnp.mean(dng * o, axis=-1, keepdims=True)
        back = _bdotp(do, p["qg"], 1, 1)
        dst = jnp.zeros((HG_D, HG_D), F32)
        dsts = [None] * nc
        for n in range(nc - 1, -1, -1):
            dsts[n] = dst
            dst = dst * e_l[n] + back[n]
        dst_all = jnp.stack(dsts)
        da = jnp.where(tril, _bdotp(do, hi, 2, 2), 0.0)
        dq = _bdotp(da, p["ke"], 2, 1) * p["e_q"] + _bdotp(do, st_all, 2, 1) * p["e_g"]
        dk_state = _bdotp(hi, dst_all, 2, 1) * p["e_s"]
        dk = _bdotp(da, p["qe"], 1, 1) * p["e_k"] + dk_state
        di_ref[0] = flat(_bdot(p["a"], do, 1, 1) + _bdot(p["kg"], dst_all, 2, 2)).astype(di_ref.dtype)
        extra = (jnp.sum(k * dk_state, axis=1, keepdims=True) + e_l * jnp.sum(st_all * dst_all, axis=1, keepdims=True))
        dgc = q * dq - k * dk + jnp.where(last_row, extra, 0.0)
        dfv = _tri_dot_b(triu_bf, dgc) / p["f"] - dk
        sf, sq = p["sf"], p["sq"]
        df_ref[0] = flat(dfv * (1.0 - lbv) * sf * (1.0 - sf)).astype(df_ref.dtype)
        dlb = jnp.sum(flat(dfv * (1.0 - sf)), axis=0, keepdims=True)
        dq_ref[0] = flat(dq * (sq * (1.0 + hq * (1.0 - sq)))).astype(dq_ref.dtype)
        dl0 = dlb * lbv * (1.0 - lbv)
        _acc(dlb_ref, jnp.concatenate([dl0, -dl0], axis=0), b == 0)
        _acc(dgn_ref, dgn, jnp.logical_and(b == 0, h == 0))

    sds = jax.ShapeDtypeStruct((b_, t_, hw), MXU_DTYPE)
    return _ride_call(
        body, rider, name="hgrn_bwd", grid=(HG_HEADS, b_),
        in_specs=[col(0), col(1), col(2), col(3), blk, vec, one],
        out_specs=[blk, blk, blk, blk, vec, one],
        out_shape=[sds, sds, sds, sds, jax.ShapeDtypeStruct((2, hw), F32), jax.ShapeDtypeStruct((1, LANE), F32)],
        scratch=[], args=(zm, zm, zm, zm, dy, lb, gn))


def _fox_logf(x):
    return jnp.minimum(x, 0.0) - jnp.log(1.0 + jnp.exp(-jnp.abs(x)))


def _fox_prep(zf, bias):
    b_, t_, _ = zf.shape
    tb = min(FOX_BLOCK, t_)
    nb = t_ // tb

    def body(z_ref, b_ref, fc_ref):
        tril_bf = (lax.broadcasted_iota(jnp.int32, (tb, tb), 0) >= lax.broadcasted_iota(jnp.int32, (tb, tb), 1)).astype(BF16)
        bv = b_ref[...]

        def blk(i, carry):
            rows = pl.ds(pl.multiple_of(i * tb, tb), tb)
            fc = _tri_dot(tril_bf, _fox_logf(z_ref[0, rows, :] + bv)) + carry
            fc_ref[0, rows, :] = fc
            return fc[tb - 1:tb, :]

        lax.fori_loop(0, nb, blk, jnp.zeros((1, LANE), F32))

    blk_spec = pl.BlockSpec((1, t_, LANE), lambda b: (b, 0, 0))
    return pl.pallas_call(
        body, name="fox_prep", grid=(b_,),
        in_specs=[blk_spec, pl.BlockSpec((1, LANE), lambda b: (0, 0))], out_specs=blk_spec,
        out_shape=jax.ShapeDtypeStruct((b_, t_, LANE), F32), compiler_params=_params(1),
    )(zf, bias)


def _fox_post(dfc, zf, bias):
    b_, t_, _ = zf.shape
    npair = dfc.shape[1]
    tb = min(FOX_BLOCK, t_)
    nb = t_ // tb

    def body(d_ref, z_ref, b_ref, dz_ref, db_ref):
        triu_bf = (lax.broadcasted_iota(jnp.int32, (tb, tb), 0) <= lax.broadcasted_iota(jnp.int32, (tb, tb), 1)).astype(BF16)
        valid = lax.broadcasted_iota(jnp.int32, (tb, LANE), 1) < FOX_HEADS
        bv = b_ref[...]

        def blk(m, carry):
            tail, db = carry
            rows = pl.ds(pl.multiple_of((nb - 1 - m) * tb, tb), tb)
            dfc_rows = d_ref[0, 0, rows, :]
            for p in range(1, npair):
                dfc_rows = dfc_rows + pltpu.roll(d_ref[0, p, rows, :], 2 * p, 1)
            dlf = _tri_dot(triu_bf, dfc_rows) + tail
            dx = jnp.where(valid, dlf * _sig(-(z_ref[0, rows, :] + bv)), 0.0)
            dz_ref[0, rows, :] = dx.astype(dz_ref.dtype)
            return dlf[0:1, :], db + jnp.sum(dx, axis=0, keepdims=True)

        z1 = jnp.zeros((1, LANE), F32)
        _, db = lax.fori_loop(0, nb, blk, (z1, z1))
        _acc(db_ref, db, pl.program_id(0) == 0)

    blk_spec = pl.BlockSpec((1, t_, LANE), lambda b: (b, 0, 0))
    vec = pl.BlockSpec((1, LANE), lambda b: (0, 0))
    return pl.pallas_call(
        body, name="fox_post", grid=(b_,),
        in_specs=[pl.BlockSpec((1, npair, t_, LANE), lambda b: (b, 0, 0, 0)), blk_spec, vec], out_specs=[blk_spec, vec],
        out_shape=[jax.ShapeDtypeStruct((b_, t_, LANE), MXU_DTYPE), jax.ShapeDtypeStruct((1, LANE), F32)],
        compiler_params=_params(1),
    )(dfc, zf, bias)


FOX_TILE = 128
FOX_BAND = 512
AUG = 64


def _head_mean_matrix():
    r = lax.broadcasted_iota(jnp.int32, (LANE, LANE), 0) // FOX_DH
    c = lax.broadcasted_iota(jnp.int32, (LANE, LANE), 1) // FOX_DH
    return (r == c).astype(BF16)


def _dot_right_exact(x, m_bf):
    hi = x.astype(BF16)
    r = x - hi.astype(F32)
    mid = r.astype(BF16)
    lo = (r - mid.astype(F32)).astype(BF16)

    def d(v):
        return lax.dot_general(v, m_bf, (((1,), (0,)), ((), ())), preferred_element_type=F32)

    return d(hi) + d(mid) + d(lo)


def _pair_norm(x, g2, bd):
    r = lax.rsqrt(_dot_right_exact(x * x, bd) * (1.0 / FOX_DH) + EPS)
    return x * r * g2, r


def _pair_norm_bwd(x, r, dy, g2, bd):
    dyg = dy * g2
    dx = r * dyg - x * (r * r * r) * (_dot_right_exact(dyg * x, bd) * (1.0 / FOX_DH))
    return dx, jnp.sum(dy * x * r, axis=0, keepdims=True)


def _head_lanes(xn, hh):
    return xn if hh == 0 else pltpu.roll(xn, FOX_DH, 1)


def _split3(x):
    hi = x.astype(BF16).astype(F32)
    mid = (x - hi).astype(BF16).astype(F32)
    return hi, mid, x - hi - mid


def _fox_operands(q_ref, k_ref, v_ref, fc_ref, gq2, gk2, p, qa, ka, va):
    t_ = q_ref.shape[1]
    bd = _head_mean_matrix()
    lane = lax.broadcasted_iota(jnp.int32, (t_, LANE), 1)
    qx, kx = q_ref[0], k_ref[0]
    qn, rq = _pair_norm(qx, gq2, bd)
    kn, rk = _pair_norm(kx, gk2, bd)
    vv = v_ref[0]
    q_aug = jnp.where(jnp.logical_and(lane >= AUG, lane < AUG + 3), 1.0, 0.0)
    for hh in range(2):
        fcol = jnp.sum(jnp.where(lane == 2 * p + hh, fc_ref[0], 0.0), axis=-1, keepdims=True)
        hi, mid, lo = _split3(-fcol)
        k_aug = jnp.where(lane == AUG, hi, jnp.where(lane == AUG + 1, mid, jnp.where(lane == AUG + 2, lo,
                          jnp.where(lane == AUG + 3, 1.0, 0.0))))
        head = lane < FOX_DH
        qa[hh] = jnp.where(head, _head_lanes(qn, hh), q_aug).astype(MXU_DTYPE)
        ka[hh] = jnp.where(head, _head_lanes(kn, hh), k_aug).astype(MXU_DTYPE)
        va[hh] = jnp.where(head, _head_lanes(vv, hh), 0.0).astype(MXU_DTYPE)
    return bd, lane, qx, kx, rq, rk


def _fox_specs(t_, fw, col0):
    npair = fw // LANE

    def col(off):
        return pl.BlockSpec((1, t_, LANE), lambda b, p: (b, 0, col0 + off * npair + p))

    pair = pl.BlockSpec((1, t_, LANE), lambda b, p: (b, 0, p))
    full = pl.BlockSpec((1, t_, LANE), lambda b, p: (b, 0, 0))
    gvec = pl.BlockSpec((1, LANE), lambda b, p: (0, 0))
    lse = pl.BlockSpec((1, 1, t_, LANE), lambda b, p: (b, p, 0, 0))
    return col, pair, full, gvec, lse


def _fox_fwd(zm, fc, gq2, gk2, fw, col0):
    b_, t_, _ = zm.shape
    npair = fw // LANE
    tq = min(FOX_TILE, t_)
    bw = min(FOX_BAND, t_)
    nband, tpb = t_ // bw, bw // tq
    scale = FOX_DH ** -0.5
    col, pair, full, gvec, lse_spec = _fox_specs(t_, fw, col0)

    def body(q_ref, k_ref, v_ref, fc_ref, gq_ref, gk_ref, o_ref, lse_ref, qa, ka, va):
        p = pl.program_id(1)
        _fox_operands(q_ref, k_ref, v_ref, fc_ref, gq_ref[...] * scale, gk_ref[...], p, qa, ka, va)
        ri = lax.broadcasted_iota(jnp.int32, (tq, bw), 0)
        ci = lax.broadcasted_iota(jnp.int32, (tq, bw), 1)
        lane = lax.broadcasted_iota(jnp.int32, (tq, LANE), 1)

        for band in range(nband):
            c0 = band * bw

            def qtile(ii, _, c0=c0):
                r0 = pl.multiple_of(c0 + ii * tq, tq)
                rows = pl.ds(r0, tq)
                keep = c0 + ci <= r0 + ri
                res = []
                for hh in range(2):
                    qb = qa[hh, rows, :]
                    s_b = jnp.where(keep, _nt(qb, ka[hh, c0:c0 + bw, :]), NEG)
                    m = jnp.max(s_b, axis=-1, keepdims=True)
                    if c0:
                        s_a = _nt(qb, ka[hh, 0:c0, :])
                        m = jnp.maximum(m, jnp.max(s_a, axis=-1, keepdims=True))
                    p_b = jnp.exp(s_b - m)
                    l = jnp.sum(p_b, axis=-1, keepdims=True)
                    acc = _nn(p_b, va[hh, c0:c0 + bw, :])
                    if c0:
                        p_a = jnp.exp(s_a - m)
                        l = l + jnp.sum(p_a, axis=-1, keepdims=True)
                        acc = acc + _nn(p_a, va[hh, 0:c0, :])
                    res.append((acc / l, m + jnp.log(l)))
                (o0, e0), (o1, e1) = res
                o_ref[0, rows, :] = jnp.where(lane < FOX_DH, o0, pltpu.roll(o1, FOX_DH, 1))
                lse_ref[0, 0, rows, :] = jnp.where(lane == 0, e0, jnp.where(lane == 1, e1, 0.0))
                return 0

            lax.fori_loop(0, tpb, qtile, 0)

    return pl.pallas_call(
        body, name="fox_fwd", grid=(b_, npair),
        in_specs=[col(0), col(1), col(2), full, gvec, gvec],
        out_specs=[pair, lse_spec],
        out_shape=[jax.ShapeDtypeStruct((b_, t_, fw), F32), jax.ShapeDtypeStruct((b_, npair, t_, LANE), F32)],
        scratch_shapes=[pltpu.VMEM((2, t_, LANE), MXU_DTYPE)] * 3,
        compiler_params=_params(2),
    )(zm, zm, zm, fc, gq2, gk2)


def _norm_bwd(x, dy, g):
    r = lax.rsqrt(jnp.mean(x * x, axis=-1, keepdims=True) + EPS)
    dyg = dy * g
    dx = r * dyg - x * (r * r * r) * jnp.mean(dyg * x, axis=-1, keepdims=True)
    return dx, jnp.sum(dy * x * r, axis=0, keepdims=True)


def _fox_bwd(zm, o, do, lse, fc, gq2, gk2, fw, col0, rider=None):
    b_, t_, _ = zm.shape
    npair = fw // LANE
    tq = min(FOX_TILE, t_)
    nb = t_ // tq
    bw = min(FOX_BAND, t_)
    nband, tpb = t_ // bw, bw // tq
    scale = FOX_DH ** -0.5
    col, pair, full, gvec, lse_spec = _fox_specs(t_, fw, col0)

    def body(q_ref, k_ref, v_ref, o_ref, do_ref, lse_ref, fc_ref, gq_ref, gk_ref,
             dq_ref, dk_ref, dv_ref, dfc_ref, dgq_ref, dgk_ref, qa, ka, va, da, rowv, dq_acc, dk_acc, dv_acc):
        b, p = pl.program_id(0), pl.program_id(1)
        gq2v, gk2v = gq_ref[...] * scale, gk_ref[...]
        bd, lane, qx, kx, rq, rk = _fox_operands(q_ref, k_ref, v_ref, fc_ref, gq2v, gk2v, p, qa, ka, va)
        head = lane < FOX_DH
        dov = do_ref[0]
        dsum = _dot_right_exact(dov * o_ref[0], bd)
        eye = (lax.broadcasted_iota(jnp.int32, (tq, tq), 0) == lax.broadcasted_iota(jnp.int32, (tq, tq), 1)).astype(F32)
        for hh in range(2):
            da[hh] = jnp.where(head, _head_lanes(dov, hh), 0.0).astype(MXU_DTYPE)
            for blk in range(nb):
                rs = slice(blk * tq, (blk + 1) * tq)
                rowv[2 * hh:2 * hh + 1, rs] = jnp.sum(eye * lse_ref[0, 0, rs, hh:hh + 1], axis=0, keepdims=True)
                rowv[2 * hh + 1:2 * hh + 2, rs] = jnp.sum(eye * dsum[rs, hh * FOX_DH:hh * FOX_DH + 1], axis=0, keepdims=True)
        dq_acc[...] = jnp.zeros(dq_acc.shape, F32)
        ri = lax.broadcasted_iota(jnp.int32, (tq, bw), 0)
        ci = lax.broadcasted_iota(jnp.int32, (tq, bw), 1)

        def part(hh, kb, vb, lo, hi, keep):
            qm, dm = qa[hh, lo:hi, :], da[hh, lo:hi, :]
            pt = jnp.exp(_nt(kb, qm) - rowv[2 * hh:2 * hh + 1, lo:hi])
            if keep is not None:
                pt = jnp.where(keep, pt, 0.0)
            dst = pt * (_nt(vb, dm) - rowv[2 * hh + 1:2 * hh + 2, lo:hi])
            dq_acc[hh, lo:hi, :] += _tn(dst, kb)
            return _nn(dst, qm), _nn(pt, dm)

        for band in range(nband):
            c0 = band * bw

            def kvtile(jj, _, c0=c0):
                r0 = pl.multiple_of(c0 + jj * tq, tq)
                rows = pl.ds(r0, tq)
                keep = c0 + ci >= r0 + ri
                for hh in range(2):
                    kb, vb = ka[hh, rows, :], va[hh, rows, :]
                    dk_t, dv_t = part(hh, kb, vb, c0, c0 + bw, keep)
                    if c0 + bw < t_:
                        dk_u, dv_u = part(hh, kb, vb, c0 + bw, t_, None)
                        dk_t, dv_t = dk_t + dk_u, dv_t + dv_u
                    dk_acc[hh, rows, :] = dk_t
                    dv_acc[hh, rows, :] = dv_t
                return 0

            lax.fori_loop(0, tpb, kvtile, 0)

        dq0, dq1, dk0, dk1 = dq_acc[0], dq_acc[1], dk_acc[0], dk_acc[1]
        dqn = jnp.where(head, dq0, pltpu.roll(dq1, FOX_DH, 1))
        dkn = jnp.where(head, dk0, pltpu.roll(dk1, FOX_DH, 1))
        dqx, gq_part = _pair_norm_bwd(qx, rq, dqn, gq2v, bd)
        dkx, gk_part = _pair_norm_bwd(kx, rk, dkn, gk2v, bd)
        dq_ref[0] = dqx.astype(dq_ref.dtype)
        dk_ref[0] = dkx.astype(dk_ref.dtype)
        dv_ref[0] = jnp.where(head, dv_acc[0], pltpu.roll(dv_acc[1], FOX_DH, 1)).astype(dv_ref.dtype)

        def bias_grad(dqh, dkh):
            return (jnp.sum(jnp.where(lane == AUG + 3, dqh, 0.0), axis=-1, keepdims=True)
                    - jnp.sum(jnp.where(lane == AUG, dkh, 0.0), axis=-1, keepdims=True))

        dfc_ref[0, 0] = jnp.where(lane == 0, bias_grad(dq0, dk0), jnp.where(lane == 1, bias_grad(dq1, dk1), 0.0))
        first = jnp.logical_and(b == 0, p == 0)
        _acc(dgq_ref, gq_part * scale, first)
        _acc(dgk_ref, gk_part, first)

    sds = jax.ShapeDtypeStruct((b_, t_, fw), MXU_DTYPE)
    gs = jax.ShapeDtypeStruct((1, LANE), F32)
    return _ride_call(
        body, rider, name="fox_bwd", grid=(b_, npair),
        in_specs=[col(0), col(1), col(2), pair, pair, lse_spec, full, gvec, gvec],
        out_specs=[pair, pair, pair, lse_spec, gvec, gvec],
        out_shape=[sds, sds, sds, jax.ShapeDtypeStruct((b_, npair, t_, LANE), F32), gs, gs],
        scratch=[pltpu.VMEM((2, t_, LANE), MXU_DTYPE)] * 4
        + [pltpu.VMEM((8, t_), F32)] + [pltpu.VMEM((2, t_, LANE), F32)] * 3,
        args=(zm, zm, zm, o, do, lse, fc, gq2, gk2))


def _mem_specs(t_, m_, mw, col0):
    nh = mw // LANE
    qcol = pl.BlockSpec((1, t_, LANE), lambda b, h: (b, 0, col0 + h))
    kcol = pl.BlockSpec((1, m_, LANE), lambda b, h: (b, 0, h))
    vcol = pl.BlockSpec((1, m_, LANE), lambda b, h: (b, 0, nh + h))
    ycol = pl.BlockSpec((1, t_, LANE), lambda b, h: (b, 0, h))
    gvec = pl.BlockSpec((1, LANE), lambda b, h: (0, 0))
    return qcol, kcol, vcol, ycol, gvec


def _mem_fwd(zm, mkv, gq, gk, mw, col0):
    b_, t_, _ = zm.shape
    m_ = mkv.shape[1]
    tq = min(512, t_)
    nb = t_ // tq
    scale = MEM_DH ** -0.5
    qcol, kcol, vcol, ycol, gvec = _mem_specs(t_, m_, mw, col0)

    def body(q_ref, k_ref, v_ref, gq_ref, gk_ref, y_ref):
        gqv, gkv = gq_ref[...] * scale, gk_ref[...]
        kv = k_ref[0]
        kn = _mx(kv * lax.rsqrt(jnp.mean(kv * kv, axis=-1, keepdims=True) + EPS) * gkv)
        vv = _mx(v_ref[0])

        def blk(i, _):
            rows = pl.ds(pl.multiple_of(i * tq, tq), tq)
            qv = q_ref[0, rows, :]
            s = _nt(qv * lax.rsqrt(jnp.mean(qv * qv, axis=-1, keepdims=True) + EPS) * gqv, kn)
            e = jnp.exp(s - jnp.max(s, axis=-1, keepdims=True))
            y_ref[0, rows, :] = _nn(e / jnp.sum(e, axis=-1, keepdims=True), vv)
            return 0

        lax.fori_loop(0, nb, blk, 0)

    return pl.pallas_call(
        body, name="mem_fwd", grid=(b_, MEM_HEADS), in_specs=[qcol, kcol, vcol, gvec, gvec], out_specs=ycol,
        out_shape=jax.ShapeDtypeStruct((b_, t_, mw), F32), compiler_params=_params(2),
    )(zm, mkv, mkv, gq, gk)


def _mem_bwd(zm, mkv, dy, gq, gk, mw, col0):
    b_, t_, _ = zm.shape
    m_ = mkv.shape[1]
    tq = min(512, t_)
    nb = t_ // tq
    scale = MEM_DH ** -0.5
    qcol, kcol, vcol, ycol, gvec = _mem_specs(t_, m_, mw, col0)

    def body(q_ref, k_ref, v_ref, dy_ref, gq_ref, gk_ref, dq_ref, dk_ref, dv_ref, dgq_ref, dgk_ref):
        gqv, gkv = gq_ref[...] * scale, gk_ref[...]
        kv = k_ref[0]
        kn = _mx(kv * lax.rsqrt(jnp.mean(kv * kv, axis=-1, keepdims=True) + EPS) * gkv)
        vv = _mx(v_ref[0])

        def blk(i, carry):
            dkn, dvv, dgq = carry
            rows = pl.ds(pl.multiple_of(i * tq, tq), tq)
            qv = q_ref[0, rows, :]
            qn = _mx(qv * lax.rsqrt(jnp.mean(qv * qv, axis=-1, keepdims=True) + EPS) * gqv)
            s = _nt(qn, kn)
            e = jnp.exp(s - jnp.max(s, axis=-1, keepdims=True))
            pm = e / jnp.sum(e, axis=-1, keepdims=True)
            dob = _mx(dy_ref[0, rows, :])
            dp = _nt(dob, vv)
            ds = pm * (dp - jnp.sum(dp * pm, axis=-1, keepdims=True))
            dqv, gq_part = _norm_bwd(qv, _nn(ds, kn), gqv)
            dq_ref[0, rows, :] = dqv.astype(dq_ref.dtype)
            return dkn + _tn(ds, qn), dvv + _tn(pm, dob), dgq + gq_part * scale

        z = jnp.zeros((m_, LANE), F32)
        dkn, dvv, dgq = lax.fori_loop(0, nb, blk, (z, z, jnp.zeros((1, LANE), F32)))
        dkv, dgk = _norm_bwd(kv, dkn, gkv)
        dk_ref[0] = dkv
        dv_ref[0] = dvv
        first = jnp.logical_and(pl.program_id(0) == 0, pl.program_id(1) == 0)
        _acc(dgq_ref, dgq, first)
        _acc(dgk_ref, dgk, first)

    kblk = pl.BlockSpec((1, m_, LANE), lambda b, h: (b, 0, h))
    gs = jax.ShapeDtypeStruct((1, LANE), F32)
    ks = jax.ShapeDtypeStruct((b_, m_, mw), F32)
    return pl.pallas_call(
        body, name="mem_bwd", grid=(b_, MEM_HEADS), in_specs=[qcol, kcol, vcol, ycol, gvec, gvec],
        out_specs=[ycol, kblk, kblk, gvec, gvec],
        out_shape=[jax.ShapeDtypeStruct((b_, t_, mw), MXU_DTYPE), ks, ks, gs, gs], compiler_params=_params(2),
    )(zm, mkv, mkv, dy, gq, gk)


def _merge_specs(tm, d, w, gcol):
    row_d = pl.BlockSpec((tm, d), lambda i: (i, 0))
    row_w = pl.BlockSpec((tm, w), lambda i: (i, 0))
    gates = [pl.BlockSpec((tm, d), functools.partial(lambda i, k: (i, gcol + k), k=k)) for k in range(3)]
    w_br = pl.BlockSpec((w, d), lambda i: (0, 0))
    w_o = pl.BlockSpec((d, d), lambda i: (0, 0))
    return row_d, row_w, gates, w_br, w_o


def _merge_fwd(x, ys, zm, w_brs, w_out, gcol, tm=256):
    n, d = x.shape
    w = ys[0].shape[1]
    tm = _tile(n, tm, 8)
    row_d, row_w, gates, w_br, w_o = _merge_specs(tm, d, w, gcol)

    def body(x_ref, ya, yb, yc, g0, g1, g2, wa, wb, wc, wo, x1_ref, mg_ref):
        mg = (_sig(g0[...]) * _nn(ya[...], wa[...]) + _sig(g1[...]) * _nn(yb[...], wb[...])
              + _sig(g2[...]) * _nn(yc[...], wc[...]))
        mg_ref[...] = mg.astype(mg_ref.dtype)
        x1_ref[...] = x_ref[...] + _nn(mg, wo[...])

    return pl.pallas_call(
        body, name="merge_fwd", grid=(n // tm,),
        in_specs=[row_d, row_w, row_w, row_w] + gates + [w_br, w_br, w_br, w_o],
        out_specs=[row_d, row_d],
        out_shape=[jax.ShapeDtypeStruct((n, d), F32), jax.ShapeDtypeStruct((n, d), MXU_DTYPE)],
        compiler_params=_params(1),
    )(x, *ys, zm, zm, zm, *w_brs, w_out)


def _merge_bwd(dx1, ys, zm, w_brs, w_out, gcol, tm=256):
    n, d = dx1.shape
    w = ys[0].shape[1]
    tm = _tile(n, tm, 8)
    row_d, row_w, gates, w_br, w_o = _merge_specs(tm, d, w, gcol)

    def body(dx_ref, ya, yb, yc, g0, g1, g2, wa, wb, wc, wo, dgl_ref, dpa, dpb, dpc, dya, dyb, dyc):
        dm = _nt(dx_ref[...], wo[...])
        for k, (y, g, wr, dp_ref, dy_ref) in enumerate(((ya, g0, wa, dpa, dya), (yb, g1, wb, dpb, dyb),
                                                        (yc, g2, wc, dpc, dyc))):
            sg = _sig(g[...])
            pr = _nn(y[...], wr[...])
            dgl_ref[:, k * d:(k + 1) * d] = (dm * pr * sg * (1.0 - sg)).astype(dgl_ref.dtype)
            dp = (dm * sg).astype(dp_ref.dtype)
            dp_ref[...] = dp
            dy_ref[...] = _nt(dp, wr[...])

    sd = jax.ShapeDtypeStruct((n, d), MXU_DTYPE)
    sw = jax.ShapeDtypeStruct((n, w), F32)
    return pl.pallas_call(
        body, name="merge_bwd", grid=(n // tm,),
        in_specs=[row_d, row_w, row_w, row_w] + gates + [w_br, w_br, w_br, w_o],
        out_specs=[pl.BlockSpec((tm, 3 * d), lambda i: (i, 0)), row_d, row_d, row_d, row_w, row_w, row_w],
        out_shape=[jax.ShapeDtypeStruct((n, 3 * d), MXU_DTYPE), sd, sd, sd, sw, sw, sw],
        compiler_params=_params(1),
    )(dx1, *ys, zm, zm, zm, *w_brs, w_out)


CONV_ROWS = 256
HALO = 8


def _ext(ref, r0, t_):
    rc = min(CONV_ROWS, t_)
    a, b = max(r0 - HALO, 0), min(r0 + rc + HALO, t_)
    parts = []
    if r0 - HALO < 0:
        parts.append(jnp.zeros((HALO, ref.shape[2]), F32))
    parts.append(ref[0, a:b, :].astype(F32))
    if r0 + rc + HALO > t_:
        parts.append(jnp.zeros((HALO, ref.shape[2]), F32))
    return jnp.concatenate(parts, axis=0) if len(parts) > 1 else parts[0]


def _gelu_parts(ac):
    cdf = 0.5 * (1.0 + _erf(ac * (2.0 ** -0.5)))
    pdf = jnp.exp(-0.5 * ac * ac) * ((2.0 * math.pi) ** -0.5)
    return cdf, pdf


def _conv_taps(a_ext, cw, cb):
    return cw[0:1, :] * pltpu.roll(a_ext, 2, 0) + cw[1:2, :] * pltpu.roll(a_ext, 1, 0) + cw[2:3, :] * a_ext + cb


def _glu_specs(t_, f, g):
    gate = pl.BlockSpec((1, t_, g), lambda j, b: (b, 0, j))
    value = pl.BlockSpec((1, t_, g), lambda j, b: (b, 0, f // g + j))
    cwb = pl.BlockSpec((3, g), lambda j, b: (0, j))
    cbb = pl.BlockSpec((1, g), lambda j, b: (0, j))
    return gate, value, cwb, cbb


def _glu_fwd(u, cw, cb):
    b_, t_, f2 = u.shape
    f = f2 // 2
    g = min(FFN_GROUP, f)
    rc = min(CONV_ROWS, t_)
    gate, value, cwb, cbb = _glu_specs(t_, f, g)

    def body(a_ref, v_ref, cw_ref, cb_ref, y_ref):
        cwv, cbv = cw_ref[...], cb_ref[...]
        for r0 in range(0, t_, rc):
            ac = _conv_taps(_ext(a_ref, r0, t_), cwv, cbv)[HALO:HALO + rc]
            cdf, _ = _gelu_parts(ac)
            y_ref[0, r0:r0 + rc, :] = (ac * cdf * v_ref[0, r0:r0 + rc, :]).astype(y_ref.dtype)

    return pl.pallas_call(
        body, name="glu_fwd", grid=(f // g, b_), in_specs=[gate, value, cwb, cbb], out_specs=gate,
        out_shape=jax.ShapeDtypeStruct((b_, t_, f), MXU_DTYPE), compiler_params=_params(2),
    )(u, u, cw, cb)


def _glu_bwd(u, dy, cw, cb):
    b_, t_, f2 = u.shape
    f = f2 // 2
    g = min(FFN_GROUP, f)
    rc = min(CONV_ROWS, t_)
    ne = rc + 2 * HALO
    gate, value, cwb, cbb = _glu_specs(t_, f, g)

    def body(a_ref, v_ref, dy_ref, cw_ref, cb_ref, da_ref, dv_ref, dcw_ref, dcb_ref):
        cwv, cbv = cw_ref[...], cb_ref[...]
        dcw = [jnp.zeros((1, g), F32) for _ in range(3)]
        dcb = jnp.zeros((1, g), F32)
        for r0 in range(0, t_, rc):
            a_ext, v_ext, dy_ext = _ext(a_ref, r0, t_), _ext(v_ref, r0, t_), _ext(dy_ref, r0, t_)
            ac = _conv_taps(a_ext, cwv, cbv)
            cdf, pdf = _gelu_parts(ac)
            dac = dy_ext * v_ext * (cdf + ac * pdf)
            da = cwv[2:3, :] * dac + cwv[1:2, :] * pltpu.roll(dac, ne - 1, 0) + cwv[0:1, :] * pltpu.roll(dac, ne - 2, 0)
            mid = slice(HALO, HALO + rc)
            da_ref[0, r0:r0 + rc, :] = da[mid].astype(da_ref.dtype)
            dv_ref[0, r0:r0 + rc, :] = (dy_ext[mid] * ac[mid] * cdf[mid]).astype(dv_ref.dtype)
            dacm = dac[mid]
            dcw[0] = dcw[0] + jnp.sum(dacm * pltpu.roll(a_ext, 2, 0)[mid], axis=0, keepdims=True)
            dcw[1] = dcw[1] + jnp.sum(dacm * pltpu.roll(a_ext, 1, 0)[mid], axis=0, keepdims=True)
            dcw[2] = dcw[2] + jnp.sum(dacm * a_ext[mid], axis=0, keepdims=True)
            dcb = dcb + jnp.sum(dacm, axis=0, keepdims=True)
        first = pl.program_id(1) == 0
        _acc(dcw_ref, jnp.concatenate(dcw, axis=0), first)
        _acc(dcb_ref, dcb, first)

    sds = jax.ShapeDtypeStruct((b_, t_, f), MXU_DTYPE)
    return pl.pallas_call(
        body, name="glu_bwd", grid=(f // g, b_), in_specs=[gate, value, gate, cwb, cbb],
        out_specs=[gate, gate, cwb, cbb],
        out_shape=[sds, sds, jax.ShapeDtypeStruct((3, f), F32), jax.ShapeDtypeStruct((1, f), F32)],
        compiler_params=_params(2),
    )(u, u, dy, cw, cb)


def _place():
    x, y, c = lax.axis_index("x"), lax.axis_index("y"), lax.axis_index("c")
    chips = [(1 - x, y), (x, 1 - y), (1 - x, 1 - y)]
    return x, y, c, chips


def _remote(src, dst, send_sem, recv_sem, to):
    return pltpu.make_async_remote_copy(src_ref=src, dst_ref=dst, send_sem=send_sem, recv_sem=recv_sem,
                                        device_id=to, device_id_type=MESH)


STACK, COLS = "stack", "cols"


def _shard_ref(ref, kind, s, rows, c):
    if kind == COLS:
        cols = pl.ds(pl.multiple_of(s * c, LANE), c)
        return ref.at[:, cols] if rows is None else ref.at[rows, cols]
    return ref.at[s] if rows is None else ref.at[s, rows, :]


def _halves(c, half):
    mine = pl.ds(pl.multiple_of(c * half, 16), half)
    theirs = pl.ds(pl.multiple_of((1 - c) * half, 16), half)
    return mine, theirs


def _gather_parts(kinds):
    def first_copies(ins, outs, sems):
        x, y, c, chips = _place()
        me = 2 * x + y
        cps = []
        for i, (w_ref, o_ref, kind) in enumerate(zip(ins, outs, kinds)):
            r, cw = w_ref.shape
            mine, _ = _halves(c, r // 2)
            for j, chip in enumerate(chips):
                cps.append(_remote(w_ref.at[mine], _shard_ref(o_ref, kind, me, mine, cw), sems[0].at[6 * i + j],
                                   sems[1].at[6 * i + j], (*chip, c)))
        return cps

    def start(ins, outs, sems):
        for cp in first_copies(ins, outs, sems):
            cp.start()

    def finish(ins, outs, sems):
        x, y, c, chips = _place()
        sib = (x, y, 1 - c)
        passed = []
        for i, (w_ref, o_ref, kind) in enumerate(zip(ins, outs, kinds)):
            r, cw = w_ref.shape
            mine, _ = _halves(c, r // 2)
            for j, (px, py) in enumerate(chips):
                blk = _shard_ref(o_ref, kind, 2 * px + py, mine, cw)
                _remote(blk, blk, sems[0].at[6 * i + j], sems[1].at[6 * i + j], sib).wait_recv()
                passed.append(_remote(blk, blk, sems[0].at[6 * i + 3 + j], sems[1].at[6 * i + 3 + j], sib))
                passed[-1].start()
        for i, (w_ref, o_ref, kind) in enumerate(zip(ins, outs, kinds)):
            r, cw = w_ref.shape
            _, theirs = _halves(c, r // 2)
            for j, (px, py) in enumerate(chips):
                blk = _shard_ref(o_ref, kind, 2 * px + py, theirs, cw)
                _remote(blk, blk, sems[0].at[6 * i + 3 + j], sems[1].at[6 * i + 3 + j], sib).wait_recv()
        for cp in first_copies(ins, outs, sems) + passed:
            cp.wait_send()

    return start, finish


def _gather_shapes(shards, kinds):
    return [jax.ShapeDtypeStruct((a.shape[0], N_CHIPS * a.shape[1]) if k == COLS else (N_CHIPS,) + a.shape, a.dtype)
            for a, k in zip(shards, kinds)]


def _gather_sems(nw):
    return [pltpu.SemaphoreType.DMA((6 * nw,)), pltpu.SemaphoreType.DMA((6 * nw,))]


def _gather_shards(shards, kinds):
    nw = len(shards)
    start, finish = _gather_parts(kinds)

    def body(*refs):
        ins, outs, sems = refs[:nw], refs[nw:2 * nw], refs[2 * nw:]
        start(ins, outs, sems)
        finish(ins, outs, sems)

    return pl.pallas_call(
        body, name="gather_shards", in_specs=[ANY] * nw, out_specs=[ANY] * nw,
        out_shape=_gather_shapes(shards, kinds), scratch_shapes=_gather_sems(nw),
    )(*shards)


def _gather_rider(shards, kinds):
    start, finish = _gather_parts(kinds)
    return _Rider(list(shards), _gather_shapes(shards, kinds), _gather_sems(len(shards)), start, finish)


def _half_shape(g, kind):
    if kind == COLS:
        return (g.shape[0] // 2, g.shape[1])
    return (g.shape[0], g.shape[1] // 2, g.shape[2])


def _swap_parts(kinds):
    def copies(ins, outs, sems):
        x, y, c, _ = _place()
        cps = []
        for i, (g_ref, a_ref, kind) in enumerate(zip(ins, outs, kinds)):
            r = g_ref.shape[0] if kind == COLS else g_ref.shape[1]
            _, theirs = _halves(c, r // 2)
            src = g_ref.at[theirs] if kind == COLS else g_ref.at[:, theirs]
            cps.append(_remote(src, a_ref, sems[0].at[i], sems[1].at[i], (x, y, 1 - c)))
        return cps

    def start(ins, outs, sems):
        for cp in copies(ins, outs, sems):
            cp.start()

    def finish(ins, outs, sems):
        for cp in copies(ins, outs, sems):
            cp.wait()

    return start, finish


def _swap_shapes(gs, kinds):
    return [jax.ShapeDtypeStruct(_half_shape(g, k), g.dtype) for g, k in zip(gs, kinds)]


def _pair_swap_halves(gs, kinds, name):
    nw = len(gs)
    start, finish = _swap_parts(kinds)

    def body(*refs):
        ins, outs, sems = refs[:nw], refs[nw:2 * nw], refs[2 * nw:]
        start(ins, outs, sems)
        finish(ins, outs, sems)

    return pl.pallas_call(
        body, name=name, in_specs=[ANY] * nw, out_specs=[ANY] * nw, out_shape=_swap_shapes(gs, kinds),
        scratch_shapes=[pltpu.SemaphoreType.DMA((nw,)), pltpu.SemaphoreType.DMA((nw,))],
    )(*gs)


def _swap_rider(gs, kinds):
    start, finish = _swap_parts(kinds)
    nw = len(gs)
    return _Rider(list(gs), _swap_shapes(gs, kinds), [pltpu.SemaphoreType.DMA((nw,)), pltpu.SemaphoreType.DMA((nw,))],
                  start, finish)


def _row_tile(rows, width, itemsize=4, target=2 ** 21):
    return _tile(rows, max(8, target // (width * itemsize)), 8)


def _add_half(g, a, kind, c_idx, name):
    if kind == COLS:
        half, wd = a.shape
        tr = _row_tile(half, wd)
        nblk = half // tr
        grid = (nblk,)
        g_spec = pl.BlockSpec((tr, wd), lambda i, c_ref: (c_ref[0] * nblk + i, 0))
        a_spec = pl.BlockSpec((tr, wd), lambda i, c_ref: (i, 0))
    else:
        n, half, wd = a.shape
        tr = _row_tile(half, wd)
        nblk = half // tr
        grid = (n, nblk)
        g_spec = pl.BlockSpec((1, tr, wd), lambda s, i, c_ref: (s, c_ref[0] * nblk + i, 0))
        a_spec = pl.BlockSpec((1, tr, wd), lambda s, i, c_ref: (s, i, 0))

    def body(c_ref, g_ref, a_ref, o_ref):
        o_ref[...] = (g_ref[...] + a_ref[...]).astype(o_ref.dtype)

    return pl.pallas_call(
        body, name=name,
        grid_spec=pltpu.PrefetchScalarGridSpec(num_scalar_prefetch=1, grid=grid, in_specs=[g_spec, a_spec],
                                               out_specs=a_spec),
        out_shape=jax.ShapeDtypeStruct(a.shape, EXCHANGE_DTYPE), compiler_params=_params(len(grid)),
    )(c_idx, g, a)


def _exchange_parts(kinds):
    def copies(ins, outs, sems):
        x, y, c, chips = _place()
        me = 2 * x + y
        cps = []
        for i, (p_ref, b_ref, kind) in enumerate(zip(ins, outs, kinds)):
            cw = b_ref.shape[2]
            for j, (px, py) in enumerate(chips):
                cps.append(_remote(_shard_ref(p_ref, kind, 2 * px + py, None, cw), b_ref.at[me],
                                   sems[0].at[3 * i + j], sems[1].at[3 * i + j], (px, py, c)))
        return cps

    def start(ins, outs, sems):
        for cp in copies(ins, outs, sems):
            cp.start()

    def finish(ins, outs, sems):
        x, y, c, chips = _place()
        for i, b_ref in enumerate(outs):
            for j, (px, py) in enumerate(chips):
                blk = b_ref.at[2 * px + py]
                _remote(blk, blk, sems[0].at[3 * i + j], sems[1].at[3 * i + j], (px, py, c)).wait_recv()
        for cp in copies(ins, outs, sems):
            cp.wait_send()

    return start, finish


def _exchange_shapes(ps, kinds):
    return [jax.ShapeDtypeStruct((N_CHIPS,) + ((p.shape[0], p.shape[1] // N_CHIPS) if k == COLS else tuple(p.shape[1:])),
                                 p.dtype) for p, k in zip(ps, kinds)]


def _exchange_sems(nw):
    return [pltpu.SemaphoreType.DMA((3 * nw,)), pltpu.SemaphoreType.DMA((3 * nw,))]


def _exchange_rider(ps, kinds):
    start, finish = _exchange_parts(kinds)
    return _Rider(list(ps), _exchange_shapes(ps, kinds), _exchange_sems(len(ps)), start, finish)


def _sum_chips(bq, name):
    n, h, wd = bq.shape
    tr = _row_tile(h, wd * n)

    def body(b_ref, o_ref):
        acc = b_ref[0].astype(F32)
        for s in range(1, n):
            acc = acc + b_ref[s].astype(F32)
        o_ref[...] = acc

    return pl.pallas_call(
        body, name=name, grid=(h // tr,),
        in_specs=[pl.BlockSpec((n, tr, wd), lambda i: (0, i, 0))], out_specs=pl.BlockSpec((tr, wd), lambda i: (i, 0)),
        out_shape=jax.ShapeDtypeStruct((h, wd), F32), compiler_params=_params(1),
    )(bq)


def _pair_join_halves(qs):
    nw = len(qs)

    def body(*refs):
        ins, outs = refs[:nw], refs[nw:2 * nw]
        send_sems, recv_sems = refs[2 * nw:]
        x, y, c, _ = _place()
        sent = []
        for i, (q_ref, o_ref) in enumerate(zip(ins, outs)):
            mine, _ = _halves(c, q_ref.shape[0])
            sent.append(_remote(q_ref, o_ref.at[mine], send_sems.at[i], recv_sems.at[i], (x, y, 1 - c)))
            sent[-1].start()
        for i, (q_ref, o_ref) in enumerate(zip(ins, outs)):
            _, theirs = _halves(c, q_ref.shape[0])
            _remote(q_ref, o_ref.at[theirs], send_sems.at[i], recv_sems.at[i], (x, y, 1 - c)).wait_recv()
        for cp in sent:
            cp.wait_send()

    return pl.pallas_call(
        body, name="pair_join_halves", in_specs=[ANY] * nw, out_specs=[ANY] * nw,
        out_shape=[jax.ShapeDtypeStruct((2 * q.shape[0], q.shape[1]), q.dtype) for q in qs],
        scratch_shapes=[pltpu.SemaphoreType.DMA((nw,)), pltpu.SemaphoreType.DMA((nw,))],
    )(*qs)


def _all_sum_small(s, name):
    sr, w = s.shape

    def body(s_ref, o_ref, buf, send_sems, recv_sems):
        x, y, c, _ = _place()
        me = 4 * x + 2 * y + c
        buf[me] = s_ref[...]
        peers = []
        for k in range(1, 8):
            px = 1 - x if k & 4 else x
            py = 1 - y if k & 2 else y
            pc = 1 - c if k & 1 else c
            peers.append((px, py, pc))
        sent = [_remote(s_ref, buf.at[me], send_sems.at[k], recv_sems.at[k], peer) for k, peer in enumerate(peers)]
        for cp in sent:
            cp.start()
        for k, (px, py, pc) in enumerate(peers):
            _remote(s_ref, buf.at[4 * px + 2 * py + pc], send_sems.at[k], recv_sems.at[k], (px, py, pc)).wait_recv()
        for cp in sent:
            cp.wait_send()
        acc = buf[0]
        for d in range(1, 8):
            acc = acc + buf[d]
        o_ref[...] = acc

    vm = pl.BlockSpec(memory_space=pltpu.VMEM)
    return pl.pallas_call(
        body, name=name, in_specs=[vm], out_specs=vm, out_shape=jax.ShapeDtypeStruct((sr, w), F32),
        scratch_shapes=[pltpu.VMEM((8, sr, w), F32), pltpu.SemaphoreType.DMA((7,)), pltpu.SemaphoreType.DMA((7,))],
    )(s)


BIG = ("w_in", "mem_kv_w", "w_br_hgrn", "w_br_fox", "w_br_mem", "w_out", "ffn_w_up", "ffn_w_down")
KIND = {"w_in": STACK, "mem_kv_w": STACK, "w_br_hgrn": COLS, "w_br_fox": COLS, "w_br_mem": COLS, "w_out": STACK,
        "ffn_w_up": COLS, "ffn_w_down": STACK}
ROW_SHARDED = ("mem_kv_w", "w_out", "ffn_w_down")
FIRST = ("w_in",)
REST = tuple(nm for nm in BIG if nm not in FIRST)
LAST = ("w_in",)


def _put_shard(arr, kind, s, piece):
    if kind == COLS:
        return lax.dynamic_update_slice(arr, piece, (0, s * piece.shape[1]))
    return lax.dynamic_update_slice(arr, piece[None], (s, 0, 0))


def _take_shard(arr, kind, s):
    if kind == COLS:
        return lax.dynamic_slice(arr, (0, s * (arr.shape[1] // N_CHIPS)), (arr.shape[0], arr.shape[1] // N_CHIPS))
    return lax.dynamic_index_in_dim(arr, s, 0, keepdims=False)


def _w_in_pieces(cs, s1, nf):
    out = []
    for s in range(N_CHIPS):
        lo, hi = cs * s, cs * (s + 1)
        for a, b, forget in ((lo, min(hi, s1), False), (max(lo, s1), min(hi, s1 + nf), True), (max(lo, s1 + nf), hi, False)):
            if a < b:
                out.append((s, a - lo, b - lo, forget, a - s1 if forget else (a if a < s1 else a - nf)))
    return out


def _split_w_in(stacked, s1, nf):
    pieces = _w_in_pieces(stacked.shape[2], s1, nf)
    main = [stacked[s, :, a:b] for s, a, b, forget, _ in pieces if not forget]
    ff = [stacked[s, :, a:b] for s, a, b, forget, _ in pieces if forget]
    return jnp.concatenate(main, axis=1), jnp.concatenate(ff, axis=1)


def _join_w_in(g_main, g_ff, s1, nf):
    cs = (g_main.shape[1] + nf) // N_CHIPS
    shards = [[] for _ in range(N_CHIPS)]
    for s, a, b, forget, off in _w_in_pieces(cs, s1, nf):
        shards[s].append((g_ff if forget else g_main)[:, off:off + b - a])
    return jnp.stack([jnp.concatenate(p, axis=1) if len(p) > 1 else p[0] for p in shards])


SMALL = ("norm_mix_g", "norm_mem_g", "norm_ffn_g", "hgrn_lb_logits", "hgrn_norm_g", "fox_f_bias", "fox_q_norm_g",
         "fox_k_norm_g", "mem_q_norm_g", "mem_k_norm_g", "ffn_conv_b")


def _pack_small(vals):
    flats, total = [], 0
    for v in vals:
        flat = v.reshape(-1).astype(F32)
        n = -(-flat.shape[0] // FLAT_W)
        flats.append(jnp.pad(flat, (0, n * FLAT_W - flat.shape[0])))
        total += n
    if -total % 8:
        flats.append(jnp.zeros((-total % 8 * FLAT_W,), F32))
    return jnp.concatenate(flats).reshape(-1, FLAT_W)


def _unpack_small(buf, shapes):
    res, off = [], 0
    for shp in shapes:
        numel = math.prod(shp)
        n = -(-numel // FLAT_W)
        res.append(buf[off:off + n].reshape(-1)[:numel].reshape(shp))
        off += n
    return res


def _pad_lanes(v, width=LANE):
    return jnp.pad(v, ((0, 0), (0, width - v.shape[1])))


WEIGHTS = ("norm_mix_g", "norm_mem_g", "w_in", "hgrn_lb_logits", "hgrn_norm_g", "fox_f_bias", "fox_q_norm_g",
           "fox_k_norm_g", "mem_kv_w", "mem_q_norm_g", "mem_k_norm_g", "w_br_hgrn", "w_br_fox", "w_br_mem", "w_out",
           "norm_ffn_g", "ffn_w_up", "ffn_conv_w", "ffn_conv_b", "ffn_w_down")


def _local_step(x, mem, target, w, full, conv_w, late=None, hooks=None):
    b_, t_, d = x.shape
    n = b_ * t_
    hw, fw, mw = HG_HEADS * HG_D, FOX_HEADS * FOX_DH, MEM_HEADS * MEM_DH
    m_ = mem.shape[1]
    f = conv_w.shape[1]
    s1 = 4 * hw + 3 * fw
    fox_col, mem_col, gate_col = 4 * hw // LANE, s1 // LANE, (s1 + mw) // d

    w_main, w_ff = _split_w_in(full["w_in"], s1, FOX_HEADS)
    w_ff = _pad_lanes(w_ff)
    f_bias = _pad_lanes(w["fox_f_bias"])
    cb = w["ffn_conv_b"]

    x2 = x.reshape(n, d)
    h = _rmsnorm_fwd(x2, w["norm_mix_g"], name="norm_mix_fwd")
    if late:
        zm, gathered = _matmul(h, w_main, name="in_proj", rider=_gather_rider(late[0], late[1]))
        full = {**full, **late[2](gathered)}
    else:
        zm = _matmul(h, w_main, name="in_proj")
    w_up = full["ffn_w_up"]
    w_brs = [full["w_br_hgrn"], full["w_br_fox"], full["w_br_mem"]]
    w_out, w_kv, w_down = full["w_out"], full["mem_kv_w"], full["ffn_w_down"]
    zf = _matmul(h, w_ff, name="in_proj_forget")
    zm3, zf3 = zm.reshape(b_, t_, -1), zf.reshape(b_, t_, LANE)
    ya = _hgrn_fwd(zm3, w["hgrn_lb_logits"], w["hgrn_norm_g"], hw)
    fc = _fox_prep(zf3, f_bias)
    fox_gq, fox_gk = jnp.tile(w["fox_q_norm_g"], (1, 2)), jnp.tile(w["fox_k_norm_g"], (1, 2))
    yb, lse = _fox_fwd(zm3, fc, fox_gq, fox_gk, fw, fox_col)
    mem2 = mem.reshape(b_ * m_, d)
    hm = _rmsnorm_fwd(mem2, w["norm_mem_g"], name="norm_mem_fwd")
    mkv = _matmul(hm, w_kv, name="mem_kv_proj").reshape(b_, m_, 2 * mw)
    yc = _mem_fwd(zm3, mkv, w["mem_q_norm_g"], w["mem_k_norm_g"], mw, mem_col)
    ys = [ya.reshape(n, hw), yb.reshape(n, fw), yc.reshape(n, mw)]
    x1, merged = _merge_fwd(x2, ys, zm, w_brs, w_out, gate_col)
    h2 = _rmsnorm_fwd(x1, w["norm_ffn_g"], name="norm_ffn_fwd")
    u = _matmul(h2, w_up, name="ffn_up")
    u3 = u.reshape(b_, t_, 2 * f)
    yff = _glu_fwd(u3, conv_w, cb).reshape(n, f)
    dy, (loss_vec,), _ = _matmul_rows([yff], w_down, name="ffn_down_loss", tb=False, row_ins=[x1, target.reshape(n, d)],
                                      vec_ins=[], epilogue=_loss_epilogue, n_vec_out=1)

    grads = {}

    def ridden(name, call):
        if not hooks or name not in hooks:
            return call(None)[0]
        rider, then = hooks[name](grads)
        outs, extra = call(rider)
        then(extra)
        return outs

    dyff = _matmul(dy, w_down, tb=True, name="ffn_down_dx")
    grads["ffn_w_down"] = _matmul(yff, dy, ta=True, name="ffn_down_dw", tm=1408)
    du_a, du_v, grads["ffn_conv_w"], grads["ffn_conv_b"] = _glu_bwd(u3, dyff.reshape(b_, t_, f), conv_w, cb)
    du_a, du_v = du_a.reshape(n, f), du_v.reshape(n, f)
    dx1, (grads["norm_ffn_g"],), _ = _matmul_rows(
        [du_a, du_v], w_up, name="ffn_up_dx", tb=True, row_ins=[x1, dy], vec_ins=[w["norm_ffn_g"]],
        epilogue=_norm_bwd_epilogue(0), n_vec_out=1)
    grads["ffn_w_up"] = jnp.concatenate([_matmul(h2, du_a, ta=True, name="ffn_up_gate_dw"),
                                         _matmul(h2, du_v, ta=True, name="ffn_up_value_dw")], axis=1)

    dgl, dpa, dpb, dpc, dya, dyb, dyc = _merge_bwd(dx1, ys, zm, w_brs, w_out, gate_col)
    grads["w_out"] = _matmul(merged, dx1, ta=True, name="out_proj_dw")
    for nm, y_, dp_ in zip(("w_br_hgrn", "w_br_fox", "w_br_mem"), ys, (dpa, dpb, dpc)):
        grads[nm] = _matmul(y_, dp_, ta=True, name=nm + "_dw")

    dmq, dmk, dmv, grads["mem_q_norm_g"], grads["mem_k_norm_g"] = _mem_bwd(
        zm3, mkv, dyc.reshape(b_, t_, mw), w["mem_q_norm_g"], w["mem_k_norm_g"], mw, mem_col)
    dmkv = jnp.concatenate([dmk, dmv], axis=-1).reshape(b_ * m_, 2 * mw)
    grads["mem_kv_w"] = _matmul(hm, dmkv, ta=True, name="mem_kv_dw")
    dhm = _matmul(dmkv, w_kv, tb=True, name="mem_kv_dx")
    _, grads["norm_mem_g"] = _rmsnorm_bwd(mem2, [dhm], w["norm_mem_g"], None, name="norm_mem_bwd")

    dfq, dfk, dfv, dfc, g_fq, g_fk = ridden("fox_bwd", lambda rider: _fox_bwd(
        zm3, yb, dyb.reshape(b_, t_, fw), lse, fc, fox_gq, fox_gk, fw, fox_col, rider))
    grads["fox_q_norm_g"] = g_fq[:, :FOX_DH] + g_fq[:, FOX_DH:]
    grads["fox_k_norm_g"] = g_fk[:, :FOX_DH] + g_fk[:, FOX_DH:]
    dzf, g_fb = _fox_post(dfc, zf3, f_bias)
    grads["fox_f_bias"] = g_fb[:, :FOX_HEADS]

    dhq, dhf, dhi, dhg, grads["hgrn_lb_logits"], grads["hgrn_norm_g"] = ridden("hgrn_bwd", lambda rider: _hgrn_bwd(
        zm3, dya.reshape(b_, t_, hw), w["hgrn_lb_logits"], w["hgrn_norm_g"], hw, rider))

    dzm = jnp.concatenate([dhq, dhf, dhi, dhg, dfq, dfk, dfv, dmq, dgl.reshape(b_, t_, 3 * d)], axis=-1).reshape(n, -1)
    dzf2 = dzf.reshape(n, LANE)
    g_main = _matmul(h, dzm, ta=True, name="in_proj_dw")
    g_ff = _matmul(h, dzf2, ta=True, name="in_proj_forget_dw")
    grads["w_in"] = _join_w_in(g_main, g_ff[:, :FOX_HEADS], s1, FOX_HEADS)

    dh_b = _matmul(dzf2, w_ff, tb=True, name="in_proj_forget_dx")

    def in_proj_dx(rider):
        dx, vecs, extra = _matmul_rows([dzm], w_main, name="in_proj_dx", tb=True, row_ins=[x2, dx1, dh_b],
                                       vec_ins=[w["norm_mix_g"]], epilogue=_norm_bwd_epilogue(1), n_vec_out=1,
                                       rider=rider)
        return [dx, vecs[0]], extra

    grad_x, grads["norm_mix_g"] = ridden("in_proj_dx", in_proj_dx)
    return loss_vec, grad_x.reshape(b_, t_, d), grads


def kernel(x, mem, norm_mix_g, norm_mem_g, w_in, hgrn_lb_logits, hgrn_norm_g, fox_f_bias, fox_q_norm_g, fox_k_norm_g, mem_kv_w, mem_q_norm_g, mem_k_norm_g, w_br_hgrn, w_br_fox, w_br_mem, w_out, norm_ffn_g, ffn_w_up, ffn_conv_w, ffn_conv_b, ffn_w_down, loss_target, m_norm_mix_g, m_norm_mem_g, m_w_in, m_hgrn_lb_logits, m_hgrn_norm_g, m_fox_f_bias, m_fox_q_norm_g, m_fox_k_norm_g, m_mem_kv_w, m_mem_q_norm_g, m_mem_k_norm_g, m_w_br_hgrn, m_w_br_fox, m_w_br_mem, m_w_out, m_norm_ffn_g, m_ffn_w_up, m_ffn_conv_w, m_ffn_conv_b, m_ffn_w_down, v_norm_mix_g, v_norm_mem_g, v_w_in, v_hgrn_lb_logits, v_hgrn_norm_g, v_fox_f_bias, v_fox_q_norm_g, v_fox_k_norm_g, v_mem_kv_w, v_mem_q_norm_g, v_mem_k_norm_g, v_w_br_hgrn, v_w_br_fox, v_w_br_mem, v_w_out, v_norm_ffn_g, v_ffn_w_up, v_ffn_conv_w, v_ffn_conv_b, v_ffn_w_down):
    w = dict(zip(WEIGHTS, (norm_mix_g, norm_mem_g, w_in, hgrn_lb_logits, hgrn_norm_g, fox_f_bias, fox_q_norm_g,
                           fox_k_norm_g, mem_kv_w, mem_q_norm_g, mem_k_norm_g, w_br_hgrn, w_br_fox, w_br_mem, w_out,
                           norm_ffn_g, ffn_w_up, ffn_conv_w, ffn_conv_b, ffn_w_down)))
    m = dict(zip(WEIGHTS, (m_norm_mix_g, m_norm_mem_g, m_w_in, m_hgrn_lb_logits, m_hgrn_norm_g, m_fox_f_bias,
                           m_fox_q_norm_g, m_fox_k_norm_g, m_mem_kv_w, m_mem_q_norm_g, m_mem_k_norm_g, m_w_br_hgrn,
                           m_w_br_fox, m_w_br_mem, m_w_out, m_norm_ffn_g, m_ffn_w_up, m_ffn_conv_w, m_ffn_conv_b,
                           m_ffn_w_down)))
    v = dict(zip(WEIGHTS, (v_norm_mix_g, v_norm_mem_g, v_w_in, v_hgrn_lb_logits, v_hgrn_norm_g, v_fox_f_bias,
                           v_fox_q_norm_g, v_fox_k_norm_g, v_mem_kv_w, v_mem_q_norm_g, v_mem_k_norm_g, v_w_br_hgrn,
                           v_w_br_fox, v_w_br_mem, v_w_out, v_norm_ffn_g, v_ffn_w_up, v_ffn_conv_w, v_ffn_conv_b,
                           v_ffn_w_down)))
    c_idx = lax.axis_index("c")
    chip = 2 * lax.axis_index("x") + lax.axis_index("y")

    mine = {nm: w[nm][0].astype(MXU_DTYPE) for nm in BIG}

    def gathered_full(names, arrays):
        out = {nm: _put_shard(g, KIND[nm], chip, mine[nm]) for nm, g in zip(names, arrays)}
        return {nm: g.reshape(-1, g.shape[2]) if nm in ROW_SHARDED else g for nm, g in out.items()}

    full = gathered_full(FIRST, _gather_shards([mine[nm] for nm in FIRST], [KIND[nm] for nm in FIRST]))
    late = ([mine[nm] for nm in REST], [KIND[nm] for nm in REST], lambda arrays: gathered_full(REST, arrays))
    cs = ffn_conv_w.shape[2]
    f = cs * N_CHIPS
    placed = lax.dynamic_update_slice(jnp.zeros((3, f), F32), ffn_conv_w[0] * (c_idx == 0).astype(F32), (0, chip * cs))
    conv_w = _unpack_small(_all_sum_small(_pack_small([placed]), "gather_conv_w"), [(3, f)])[0]

    c_arr = jnp.reshape(c_idx, (1,)).astype(jnp.int32)

    def stacked(nm, g):
        return g.reshape(N_CHIPS, -1, g.shape[1]) if nm in ROW_SHARDED else g

    def with_own(landed, partial, kinds):
        return [_put_shard(bq, STACK, chip, _take_shard(p, k, chip)) for bq, p, k in zip(landed, partial, kinds)]

    kinds_rest, kinds_last = [KIND[nm] for nm in REST], [KIND[nm] for nm in LAST]
    state = {}

    def swap_rest(grads):
        gs = [stacked(nm, grads[nm]) for nm in REST]

        def then(from_sibling):
            state["partial_rest"] = [_add_half(g, a, k, c_arr, "add_half_" + nm)
                                     for g, a, k, nm in zip(gs, from_sibling, kinds_rest, REST)]

        return _swap_rider(gs, kinds_rest), then

    def exchange_rest(grads):
        def then(landed):
            state["landed_rest"] = with_own(landed, state["partial_rest"], kinds_rest)

        return _exchange_rider(state["partial_rest"], kinds_rest), then

    def exchange_last(grads):
        gs = [stacked(nm, grads[nm]) for nm in LAST]
        from_sibling = _pair_swap_halves(gs, kinds_last, "pair_swap_halves_last")
        partial = [_add_half(g, a, k, c_arr, "add_half_" + nm) for g, a, k, nm in zip(gs, from_sibling, kinds_last, LAST)]

        def then(landed):
            state["landed_last"] = with_own(landed, partial, kinds_last)

        return _exchange_rider(partial, kinds_last), then

    hooks = {"fox_bwd": swap_rest, "hgrn_bwd": exchange_rest, "in_proj_dx": exchange_last}

    loss_vec, grad_x, grads = _local_step(x, mem, loss_target, w, full, conv_w, late, hooks)

    landed = dict(zip(LAST + REST, state["landed_last"] + state["landed_rest"]))
    reduced_half = [_sum_chips(landed[nm], "sum_chips_" + nm) for nm in BIG]
    joined = [lax.dynamic_update_slice(o, q, (c_idx * q.shape[0], 0))
              for o, q in zip(_pair_join_halves(reduced_half), reduced_half)]
    gshards = dict(zip(BIG, joined))

    small_names = SMALL + ("ffn_conv_w",)
    summed = _unpack_small(
        _all_sum_small(_pack_small([grads[nm] for nm in small_names] + [loss_vec]), "all_sum_small_grads"),
        [grads[nm].shape for nm in small_names] + [loss_vec.shape])
    gsmall = dict(zip(small_names, summed[:-1]))
    loss = jnp.sum(summed[-1])
    g_out = {nm: gshards[nm][None] for nm in BIG}
    for nm in SMALL:
        g_out[nm] = gsmall[nm].reshape(w[nm].shape)
    g_out["ffn_conv_w"] = lax.dynamic_slice(gsmall["ffn_conv_w"], (0, chip * cs), (3, cs))[None]

    delta, new_m, new_v = {}, {}, {}
    for nm in BIG + ("ffn_conv_w",):
        delta[nm], new_m[nm], new_v[nm] = _adamw(w[nm], g_out[nm], m[nm], v[nm], name="adamw_" + nm)
    packed = [_pack_small([t[nm] for nm in SMALL])[None] for t in (w, g_out, m, v)]
    outs = _adamw(*packed, name="adamw_small")
    shapes = [w[nm].shape for nm in SMALL]
    for res, o in zip((delta, new_m, new_v), outs):
        res.update(zip(SMALL, _unpack_small(o[0], shapes)))

    return (loss, grad_x, *[g_out[nm] for nm in WEIGHTS], *[delta[nm] for nm in WEIGHTS],
            *[new_m[nm] for nm in WEIGHTS], *[new_v[nm] for nm in WEIGHTS])
```

```python
import functools
import math

import jax
import jax.numpy as jnp
from jax import lax
from jax.experimental import pallas as pl
from jax.experimental.pallas import tpu as pltpu

F32 = jnp.float32
BF16 = jnp.bfloat16
MXU_DTYPE = jnp.bfloat16
EXCHANGE_DTYPE = jnp.bfloat16

EPS = 1e-6
HG_HEADS, HG_D = 4, 128
FOX_HEADS, FOX_DH = 8, 64
MEM_HEADS, MEM_DH = 4, 128
HG_CHUNK = 64
FOX_BLOCK = 256
LANE = 128
FFN_GROUP = 256
FLAT_W = 1024
VMEM_LIMIT = 56 * 2 ** 20
NEG = -1e30
N_CHIPS = 4

ADAM_LR, ADAM_B1, ADAM_B2, ADAM_EPS, ADAM_WD, ADAM_STEP = 0.001, 0.9, 0.999, 1e-08, 0.01, 10

MESH = pl.DeviceIdType.MESH
ANY = pl.BlockSpec(memory_space=pl.ANY)


def _mx(x):
    return x.astype(MXU_DTYPE)


def _dot(a, b, ca, cb):
    return lax.dot_general(_mx(a), _mx(b), (((ca,), (cb,)), ((), ())), preferred_element_type=F32)


def _nn(a, b):
    return _dot(a, b, 1, 0)


def _nt(a, b):
    return _dot(a, b, 1, 1)


def _tn(a, b):
    return _dot(a, b, 0, 0)


def _dotp(a, b, ca, cb):
    return lax.dot_general(a, b, (((ca,), (cb,)), ((), ())), precision=lax.Precision.HIGHEST,
                           preferred_element_type=F32)


def _tri_dot(tri_bf, x):
    hi = x.astype(BF16)
    r = x - hi.astype(F32)
    mid = r.astype(BF16)
    lo = (r - mid.astype(F32)).astype(BF16)

    def d(v):
        return lax.dot_general(tri_bf, v, (((1,), (0,)), ((), ())), preferred_element_type=F32)

    return d(hi) + d(mid) + d(lo)


def _sig(x):
    return jax.nn.sigmoid(x)


def _erf(x):
    a = jnp.abs(x)
    t = 1.0 / (1.0 + 0.3275911 * a)
    poly = t * (0.254829592 + t * (-0.284496736 + t * (1.421413741 + t * (-1.453152027 + t * 1.061405429))))
    y = 1.0 - poly * jnp.exp(-a * a)
    return jnp.where(x < 0, -y, y)


def _tile(dim, pref, unit=LANE):
    if dim <= pref:
        return dim
    t = pref - pref % unit
    while t >= unit:
        if dim % t == 0:
            return t
        t -= unit
    return dim


def _params(n_grid):
    return pltpu.CompilerParams(dimension_semantics=("arbitrary",) * n_grid, vmem_limit_bytes=VMEM_LIMIT)


def _acc(ref, val, first):
    @pl.when(first)
    def _():
        ref[...] = val

    @pl.when(jnp.logical_not(first))
    def _():
        ref[...] += val


class _Rider:
    def __init__(self, inputs, out_shapes, scratch, start, finish):
        self.inputs, self.out_shapes, self.scratch, self.start, self.finish = inputs, out_shapes, scratch, start, finish


def _ride(body, rider, n_in, n_out, grid):
    if rider is None:
        return body
    ri, ro, rs = len(rider.inputs), len(rider.out_shapes), len(rider.scratch)

    def wrapped(*refs):
        a, b, c = n_in + ri, n_in + ri + n_out, n_in + ri + n_out + ro
        base = refs[:n_in] + refs[a:b] + refs[c:len(refs) - rs]
        r_in, r_out, r_scr = refs[n_in:a], refs[b:c], refs[len(refs) - rs:]
        step = pl.program_id(0)
        for ax in range(1, len(grid)):
            step = step * grid[ax] + pl.program_id(ax)

        @pl.when(step == 0)
        def _():
            rider.start(r_in, r_out, r_scr)

        body(*base)

        @pl.when(step == math.prod(grid) - 1)
        def _():
            rider.finish(r_in, r_out, r_scr)

    return wrapped


def _ride_call(body, rider, *, name, grid, in_specs, out_specs, out_shape, scratch, args, aliases=None):
    n_in, n_out = len(in_specs), len(out_specs)
    aliases = aliases or {}
    if rider is None:
        outs = pl.pallas_call(body, name=name, grid=grid, in_specs=in_specs, out_specs=out_specs, out_shape=out_shape,
                              scratch_shapes=scratch, input_output_aliases=aliases,
                              compiler_params=_params(len(grid)))(*args)
        return list(outs), None
    outs = pl.pallas_call(
        _ride(body, rider, n_in, n_out, grid), name=name, grid=grid,
        in_specs=list(in_specs) + [ANY] * len(rider.inputs), out_specs=list(out_specs) + [ANY] * len(rider.out_shapes),
        out_shape=list(out_shape) + list(rider.out_shapes), scratch_shapes=list(scratch) + list(rider.scratch),
        input_output_aliases=aliases, compiler_params=_params(len(grid)),
    )(*args, *rider.inputs)
    return list(outs[:n_out]), list(outs[n_out:])


def _matmul(a, b, *, name, ta=False, tb=False, tm=1024, tn=2048, tk=None, rider=None):
    m, k = (a.shape[1], a.shape[0]) if ta else a.shape
    n = b.shape[0] if tb else b.shape[1]
    tk = tk or (1024 if ta else 2048)
    tm, tn, tk = _tile(m, tm), _tile(n, tn), _tile(k, tk)
    nk = k // tk

    def body(a_ref, b_ref, o_ref):
        p = _dot(a_ref[...], b_ref[...], 0 if ta else 1, 1 if tb else 0)
        if nk == 1:
            o_ref[...] = p
        else:
            _acc(o_ref, p, pl.program_id(2) == 0)

    a_spec = pl.BlockSpec((tk, tm), lambda i, j, kk: (kk, i)) if ta else pl.BlockSpec((tm, tk), lambda i, j, kk: (i, kk))
    b_spec = pl.BlockSpec((tn, tk), lambda i, j, kk: (j, kk)) if tb else pl.BlockSpec((tk, tn), lambda i, j, kk: (kk, j))
    outs, extra = _ride_call(
        body, rider, name=name, grid=(m // tm, n // tn, nk), in_specs=[a_spec, b_spec],
        out_specs=[pl.BlockSpec((tm, tn), lambda i, j, kk: (i, j))], out_shape=[jax.ShapeDtypeStruct((m, n), F32)],
        scratch=[], args=(a, b))
    return (outs[0], extra) if rider else outs[0]


def _matmul_rows(a_parts, b, *, name, tb, row_ins, vec_ins, epilogue, n_vec_out, tm=512, tk=2048, rider=None):
    m, kp = a_parts[0].shape
    n = b.shape[0] if tb else b.shape[1]
    tm, tk = _tile(m, tm, 8), _tile(kp, tk)
    nk = kp // tk
    n_a, n_row, n_vec = len(a_parts), len(row_ins), len(vec_ins)

    def body(*refs):
        a_refs, b_refs = refs[:n_a], refs[n_a:2 * n_a]
        rows = refs[2 * n_a:2 * n_a + n_row]
        vecs = refs[2 * n_a + n_row:2 * n_a + n_row + n_vec]
        o_ref = refs[2 * n_a + n_row + n_vec]
        v_refs = refs[2 * n_a + n_row + n_vec + 1:-1]
        acc_ref = refs[-1]
        i, kk = pl.program_id(0), pl.program_id(1)
        p = _dot(a_refs[0][...], b_refs[0][...], 1, 1 if tb else 0)
        for a_ref, b_ref in zip(a_refs[1:], b_refs[1:]):
            p = p + _dot(a_ref[...], b_ref[...], 1, 1 if tb else 0)
        _acc(acc_ref, p, kk == 0)

        @pl.when(kk == nk - 1)
        def _():
            out, vouts = epilogue(acc_ref[...], *[r[...] for r in rows], *[v[...] for v in vecs])
            o_ref[...] = out
            for v_ref, v in zip(v_refs, vouts):
                _acc(v_ref, v, i == 0)

    a_spec = pl.BlockSpec((tm, tk), lambda i, kk: (i, kk))
    b_specs = [pl.BlockSpec((n, tk), functools.partial(lambda i, kk, q: (0, q * nk + kk), q=q)) if tb else
               pl.BlockSpec((tk, n), functools.partial(lambda i, kk, q: (q * nk + kk, 0), q=q)) for q in range(n_a)]
    row = pl.BlockSpec((tm, n), lambda i, kk: (i, 0))
    vec = pl.BlockSpec((1, n), lambda i, kk: (0, 0))
    outs, extra = _ride_call(
        body, rider, name=name, grid=(m // tm, nk),
        in_specs=[a_spec] * n_a + b_specs + [row] * n_row + [vec] * n_vec,
        out_specs=[row] + [vec] * n_vec_out,
        out_shape=[jax.ShapeDtypeStruct((m, n), F32)] + [jax.ShapeDtypeStruct((1, n), F32)] * n_vec_out,
        scratch=[pltpu.VMEM((tm, n), F32)], args=(*a_parts, *([b] * n_a), *row_ins, *vec_ins))
    return outs[0], outs[1:], extra


def _norm_bwd_epilogue(n_dh):
    def epilogue(dh, x, res, *rest):
        for extra in rest[:n_dh]:
            dh = dh + extra
        g = rest[n_dh]
        r = lax.rsqrt(jnp.mean(x * x, axis=-1, keepdims=True) + EPS)
        dhg = dh * g
        dx = res + r * dhg - x * (r * r * r) * jnp.mean(dhg * x, axis=-1, keepdims=True)
        return dx, [jnp.sum(dh * x * r, axis=0, keepdims=True)]

    return epilogue


def _loss_epilogue(y, x1, target):
    d = y.shape[1]
    err = x1 + y - target
    return err * (1.0 / d), [jnp.sum(err * err, axis=0, keepdims=True) * (0.5 / d)]


def _rmsnorm_fwd(x, g, *, name, tm=512):
    n, d = x.shape
    tm = _tile(n, tm, 8)

    def body(x_ref, g_ref, o_ref):
        xv = x_ref[...]
        r = lax.rsqrt(jnp.mean(xv * xv, axis=-1, keepdims=True) + EPS)
        o_ref[...] = (xv * r * g_ref[...]).astype(o_ref.dtype)

    return pl.pallas_call(
        body, name=name, grid=(n // tm,),
        in_specs=[pl.BlockSpec((tm, d), lambda i: (i, 0)), pl.BlockSpec((1, d), lambda i: (0, 0))],
        out_specs=pl.BlockSpec((tm, d), lambda i: (i, 0)),
        out_shape=jax.ShapeDtypeStruct((n, d), MXU_DTYPE),
        compiler_params=_params(1),
    )(x, g)


def _rmsnorm_bwd(x, dhs, g, res, *, name, tm=512):
    n, d = x.shape
    tm = _tile(n, tm, 8)
    n_dh = len(dhs)
    has_res = res is not None

    def body(*refs):
        x_ref, dh_refs, g_ref = refs[0], refs[1:1 + n_dh], refs[1 + n_dh]
        res_ref = refs[2 + n_dh] if has_res else None
        dx_ref, dg_ref = refs[-2], refs[-1]
        xv = x_ref[...]
        dh = dh_refs[0][...].astype(F32)
        for r_ in dh_refs[1:]:
            dh = dh + r_[...].astype(F32)
        r = lax.rsqrt(jnp.mean(xv * xv, axis=-1, keepdims=True) + EPS)
        dhg = dh * g_ref[...]
        dx = r * dhg - xv * (r * r * r) * jnp.mean(dhg * xv, axis=-1, keepdims=True)
        if has_res:
            dx = dx + res_ref[...]
        dx_ref[...] = dx
        _acc(dg_ref, jnp.sum(dh * xv * r, axis=0, keepdims=True), pl.program_id(0) == 0)

    row = pl.BlockSpec((tm, d), lambda i: (i, 0))
    vec = pl.BlockSpec((1, d), lambda i: (0, 0))
    ins = [x] + list(dhs) + [g] + ([res] if has_res else [])
    return pl.pallas_call(
        body, name=name, grid=(n // tm,),
        in_specs=[row] * (1 + n_dh) + [vec] + ([row] if has_res else []),
        out_specs=[row, vec],
        out_shape=[jax.ShapeDtypeStruct((n, d), F32), jax.ShapeDtypeStruct((1, d), F32)],
        compiler_params=_params(1),
    )(*ins)


def _adamw(w, g, m, v, *, name, tr=256):
    _, r, c = w.shape
    tr = _tile(r, tr, 8)
    c1 = 1.0 / (1.0 - ADAM_B1 ** ADAM_STEP)
    c2 = 1.0 / (1.0 - ADAM_B2 ** ADAM_STEP)

    def body(w_ref, g_ref, m_ref, v_ref, d_ref, mo_ref, vo_ref):
        gv = g_ref[...]
        mn = ADAM_B1 * m_ref[...] + (1.0 - ADAM_B1) * gv
        vn = ADAM_B2 * v_ref[...] + (1.0 - ADAM_B2) * (gv * gv)
        d_ref[...] = -ADAM_LR * ((mn * c1) / (jnp.sqrt(vn * c2) + ADAM_EPS) + ADAM_WD * w_ref[...])
        mo_ref[...] = mn
        vo_ref[...] = vn

    blk = pl.BlockSpec((1, tr, c), lambda i: (0, i, 0))
    sds = jax.ShapeDtypeStruct((1, r, c), F32)
    return pl.pallas_call(
        body, name=name, grid=(r // tr,), in_specs=[blk] * 4, out_specs=[blk] * 3, out_shape=[sds] * 3,
        compiler_params=_params(1),
    )(w, g, m, v)


def _bdot(a, b, ca, cb):
    return lax.dot_general(_mx(a), _mx(b), (((ca,), (cb,)), ((0,), (0,))), preferred_element_type=F32)


def _bdotp(a, b, ca, cb):
    return lax.dot_general(a, b, (((ca,), (cb,)), ((0,), (0,))), precision=lax.Precision.HIGHEST,
                           preferred_element_type=F32)


def _tri_dot_b(tri_bf, x):
    hi = x.astype(BF16)
    r = x - hi.astype(F32)
    mid = r.astype(BF16)
    lo = (r - mid.astype(F32)).astype(BF16)

    def d(v):
        return lax.dot_general(tri_bf, v, (((2,), (1,)), ((0,), (0,))), preferred_element_type=F32)

    return d(hi) + d(mid) + d(lo)


def _hgrn_forward(hq, hf, hi, lbv, tril, tril_bf):
    nc, c, _ = hq.shape
    sf = _sig(hf)
    f = lbv + (1.0 - lbv) * sf
    k = 1.0 - f
    gcum = _tri_dot_b(tril_bf, jnp.log(f))
    mid = gcum[:, c // 2 - 1:c // 2, :]
    glast = gcum[:, c - 1:c, :]
    sq = _sig(hq)
    q = hq * sq
    e_q = jnp.exp(gcum - mid)
    e_k = jnp.exp(mid - gcum)
    qe, ke = q * e_q, k * e_k
    a = jnp.where(tril, _bdot(qe, ke, 2, 2), 0.0)
    e_g = jnp.exp(gcum)
    qg = q * e_g
    e_s = jnp.exp(glast - gcum)
    kg = k * e_s
    e_l = jnp.exp(glast)
    upd = _bdot(hi, kg, 1, 1)
    st = jnp.zeros((HG_D, HG_D), F32)
    states = []
    for n in range(nc):
        states.append(st)
        st = st * e_l[n] + upd[n]
    st_all = jnp.stack(states)
    o = _bdot(a, hi, 2, 1) + _bdot(qg, st_all, 2, 2)
    return dict(sf=sf, f=f, k=k, sq=sq, q=q, e_q=e_q, e_k=e_k, qe=qe, ke=ke, a=a, e_g=e_g, qg=qg, o=o,
                e_s=e_s, kg=kg, e_l=e_l, st_all=st_all)


def _hgrn_specs(t_, col0):
    def col(off):
        return pl.BlockSpec((1, t_, LANE), lambda h, b: (b, 0, col0 + 4 * h + off))

    vec = pl.BlockSpec((2, LANE), lambda h, b: (0, h))
    one = pl.BlockSpec((1, LANE), lambda h, b: (0, 0))
    blk = pl.BlockSpec((1, t_, LANE), lambda h, b: (b, 0, h))
    return col, vec, one, blk


def _chunk_masks(nc, c):
    row = lax.broadcasted_iota(jnp.int32, (nc, c, c), 1)
    cl = lax.broadcasted_iota(jnp.int32, (nc, c, c), 2)
    return row >= cl, (row >= cl).astype(BF16), (row <= cl).astype(BF16)


def _hgrn_fwd(zm, lb, gn, hw, col0):
    b_, t_, _ = zm.shape
    c = min(HG_CHUNK, t_)
    nc = t_ // c
    col, vec, one, blk = _hgrn_specs(t_, col0)

    def body(q_ref, f_ref, i_ref, g_ref, lb_ref, gn_ref, y_ref):
        lbv, gnv = _sig(lb_ref[0:1, :] - lb_ref[1:2, :]), gn_ref[...]
        tril, tril_bf, _ = _chunk_masks(nc, c)
        chunks = lambda ref: ref[0].reshape(nc, c, LANE)
        o = _hgrn_forward(chunks(q_ref), chunks(f_ref), chunks(i_ref), lbv, tril, tril_bf)["o"]
        r = lax.rsqrt(jnp.mean(o * o, axis=-1, keepdims=True) + EPS)
        hg = chunks(g_ref)
        y_ref[0] = (o * r * gnv * (hg * _sig(hg))).reshape(t_, LANE)

    return pl.pallas_call(
        body, name="hgrn_fwd", grid=(HG_HEADS, b_),
        in_specs=[col(0), col(1), col(2), col(3), vec, one], out_specs=blk,
        out_shape=jax.ShapeDtypeStruct((b_, t_, hw), F32),
        compiler_params=_params(2),
    )(zm, zm, zm, zm, lb, gn)


def _hgrn_bwd(zm, dy, lb, gn, hw, col0, dz, rider=None):
    b_, t_, _ = zm.shape
    c = min(HG_CHUNK, t_)
    nc = t_ // c
    col, vec, one, blk = _hgrn_specs(t_, col0)

    def body(q_ref, f_ref, i_ref, g_ref, dy_ref, lb_ref, gn_ref, _, dz_ref, dlb_ref, dgn_ref):
        h, b = pl.program_id(0), pl.program_id(1)
        lbv, gnv = _sig(lb_ref[0:1, :] - lb_ref[1:2, :]), gn_ref[...]
        tril, tril_bf, triu_bf = _chunk_masks(nc, c)
        last_row = lax.broadcasted_iota(jnp.int32, (nc, c, LANE), 1) == c - 1
        chunks = lambda ref: ref[0].reshape(nc, c, LANE)
        flat = lambda x: x.reshape(t_, LANE)
        hq, hi, hg = chunks(q_ref), chunks(i_ref), chunks(g_ref)
        p = _hgrn_forward(hq, chunks(f_ref), hi, lbv, tril, tril_bf)
        o, q, k, st_all, e_l = p["o"], p["q"], p["k"], p["st_all"], p["e_l"]
        dyv = chunks(dy_ref)
        sg = _sig(hg)
        r = lax.rsqrt(jnp.mean(o * o, axis=-1, keepdims=True) + EPS)
        dn = dyv * (hg * sg)
        dz_ref[0, :, 3 * LANE:] = flat(dyv * (o * r * gnv) * (sg * (1.0 + hg * (1.0 - sg)))).astype(dz_ref.dtype)
        dgn = jnp.sum(flat(dn * o * r), axis=0, keepdims=True)
        dng = dn * gnv
        do = r * dng - o * (r * r * r) * jnp.mean(dng * o, axis=-1, keepdims=True)
        back = _bdotp(do, p["qg"], 1, 1)
        dst = jnp.zeros((HG_D, HG_D), F32)
        dsts = [None] * nc
        for n in range(nc - 1, -1, -1):
            dsts[n] = dst
            dst = dst * e_l[n] + back[n]
        dst_all = jnp.stack(dsts)
        da = jnp.where(tril, _bdotp(do, hi, 2, 2), 0.0)
        dq = _bdotp(da, p["ke"], 2, 1) * p["e_q"] + _bdotp(do, st_all, 2, 1) * p["e_g"]
        dk_state = _bdotp(hi, dst_all, 2, 1) * p["e_s"]
        dk = _bdotp(da, p["qe"], 1, 1) * p["e_k"] + dk_state
        dz_ref[0, :, 2 * LANE:3 * LANE] = flat(_bdot(p["a"], do, 1, 1) + _bdot(p["kg"], dst_all, 2, 2)).astype(dz_ref.dtype)
        extra = (jnp.sum(k * dk_state, axis=1, keepdims=True) + e_l * jnp.sum(st_all * dst_all, axis=1, keepdims=True))
        dgc = q * dq - k * dk + jnp.where(last_row, extra, 0.0)
        dfv = _tri_dot_b(triu_bf, dgc) / p["f"] - dk
        sf, sq = p["sf"], p["sq"]
        dz_ref[0, :, LANE:2 * LANE] = flat(dfv * (1.0 - lbv) * sf * (1.0 - sf)).astype(dz_ref.dtype)
        dlb = jnp.sum(flat(dfv * (1.0 - sf)), axis=0, keepdims=True)
        dz_ref[0, :, :LANE] = flat(dq * (sq * (1.0 + hq * (1.0 - sq)))).astype(dz_ref.dtype)
        dl0 = dlb * lbv * (1.0 - lbv)
        _acc(dlb_ref, jnp.concatenate([dl0, -dl0], axis=0), b == 0)
        _acc(dgn_ref, dgn, jnp.logical_and(b == 0, h == 0))

    return _ride_call(
        body, rider, name="hgrn_bwd", grid=(HG_HEADS, b_),
        in_specs=[col(0), col(1), col(2), col(3), blk, vec, one, ANY],
        out_specs=[pl.BlockSpec((1, t_, 4 * LANE), lambda h, b: (b, 0, col0 // 4 + h)), vec, one],
        out_shape=[jax.ShapeDtypeStruct(dz.shape, dz.dtype), jax.ShapeDtypeStruct((2, hw), F32),
                   jax.ShapeDtypeStruct((1, LANE), F32)],
        scratch=[], args=(zm, zm, zm, zm, dy, lb, gn, dz), aliases={7: 0})


def _fox_logf(x):
    return jnp.minimum(x, 0.0) - jnp.log(1.0 + jnp.exp(-jnp.abs(x)))


def _fox_prep(zf, bias):
    b_, t_, _ = zf.shape
    tb = min(FOX_BLOCK, t_)
    nb = t_ // tb

    def body(z_ref, b_ref, fc_ref):
        tril_bf = (lax.broadcasted_iota(jnp.int32, (tb, tb), 0) >= lax.broadcasted_iota(jnp.int32, (tb, tb), 1)).astype(BF16)
        bv = b_ref[...]

        def blk(i, carry):
            rows = pl.ds(pl.multiple_of(i * tb, tb), tb)
            fc = _tri_dot(tril_bf, _fox_logf(z_ref[0, rows, :] + bv)) + carry
            fc_ref[0, rows, :] = fc
            return fc[tb - 1:tb, :]

        lax.fori_loop(0, nb, blk, jnp.zeros((1, LANE), F32))

    blk_spec = pl.BlockSpec((1, t_, LANE), lambda b: (b, 0, 0))
    return pl.pallas_call(
        body, name="fox_prep", grid=(b_,),
        in_specs=[blk_spec, pl.BlockSpec((1, LANE), lambda b: (0, 0))], out_specs=blk_spec,
        out_shape=jax.ShapeDtypeStruct((b_, t_, LANE), F32), compiler_params=_params(1),
    )(zf, bias)


def _fox_post(dfc, zf, bias):
    b_, t_, _ = zf.shape
    npair = dfc.shape[1]
    tb = min(FOX_BLOCK, t_)
    nb = t_ // tb

    def body(d_ref, z_ref, b_ref, dz_ref, db_ref):
        triu_bf = (lax.broadcasted_iota(jnp.int32, (tb, tb), 0) <= lax.broadcasted_iota(jnp.int32, (tb, tb), 1)).astype(BF16)
        valid = lax.broadcasted_iota(jnp.int32, (tb, LANE), 1) < FOX_HEADS
        bv = b_ref[...]

        def blk(m, carry):
            tail, db = carry
            rows = pl.ds(pl.multiple_of((nb - 1 - m) * tb, tb), tb)
            dfc_rows = d_ref[0, 0, rows, :]
            for p in range(1, npair):
                dfc_rows = dfc_rows + pltpu.roll(d_ref[0, p, rows, :], 2 * p, 1)
            dlf = _tri_dot(triu_bf, dfc_rows) + tail
            dx = jnp.where(valid, dlf * _sig(-(z_ref[0, rows, :] + bv)), 0.0)
            dz_ref[0, rows, :] = dx.astype(dz_ref.dtype)
            return dlf[0:1, :], db + jnp.sum(dx, axis=0, keepdims=True)

        z1 = jnp.zeros((1, LANE), F32)
        _, db = lax.fori_loop(0, nb, blk, (z1, z1))
        _acc(db_ref, db, pl.program_id(0) == 0)

    blk_spec = pl.BlockSpec((1, t_, LANE), lambda b: (b, 0, 0))
    vec = pl.BlockSpec((1, LANE), lambda b: (0, 0))
    return pl.pallas_call(
        body, name="fox_post", grid=(b_,),
        in_specs=[pl.BlockSpec((1, npair, t_, LANE), lambda b: (b, 0, 0, 0)), blk_spec, vec], out_specs=[blk_spec, vec],
        out_shape=[jax.ShapeDtypeStruct((b_, t_, LANE), MXU_DTYPE), jax.ShapeDtypeStruct((1, LANE), F32)],
        compiler_params=_params(1),
    )(dfc, zf, bias)


FOX_TILE = 128
FOX_BAND = 512
AUG = 64


def _head_mean_matrix():
    r = lax.broadcasted_iota(jnp.int32, (LANE, LANE), 0) // FOX_DH
    c = lax.broadcasted_iota(jnp.int32, (LANE, LANE), 1) // FOX_DH
    return (r == c).astype(BF16)


def _dot_right_exact(x, m_bf):
    hi = x.astype(BF16)
    r = x - hi.astype(F32)
    mid = r.astype(BF16)
    lo = (r - mid.astype(F32)).astype(BF16)

    def d(v):
        return lax.dot_general(v, m_bf, (((1,), (0,)), ((), ())), preferred_element_type=F32)

    return d(hi) + d(mid) + d(lo)


def _pair_norm(x, g2, bd):
    r = lax.rsqrt(_dot_right_exact(x * x, bd) * (1.0 / FOX_DH) + EPS)
    return x * r * g2, r


def _pair_norm_bwd(x, r, dy, g2, bd):
    dyg = dy * g2
    dx = r * dyg - x * (r * r * r) * (_dot_right_exact(dyg * x, bd) * (1.0 / FOX_DH))
    return dx, jnp.sum(dy * x * r, axis=0, keepdims=True)


def _head_lanes(xn, hh):
    return xn if hh == 0 else pltpu.roll(xn, FOX_DH, 1)


def _split3(x):
    hi = x.astype(BF16).astype(F32)
    mid = (x - hi).astype(BF16).astype(F32)
    return hi, mid, x - hi - mid


def _fox_operands(q_ref, k_ref, v_ref, fc_ref, gq2, gk2, p, qa, ka, va):
    t_ = q_ref.shape[1]
    bd = _head_mean_matrix()
    lane = lax.broadcasted_iota(jnp.int32, (t_, LANE), 1)
    qx, kx = q_ref[0], k_ref[0]
    qn, rq = _pair_norm(qx, gq2, bd)
    kn, rk = _pair_norm(kx, gk2, bd)
    vv = v_ref[0]
    q_aug = jnp.where(jnp.logical_and(lane >= AUG, lane < AUG + 3), 1.0, 0.0)
    for hh in range(2):
        fcol = jnp.sum(jnp.where(lane == 2 * p + hh, fc_ref[0], 0.0), axis=-1, keepdims=True)
        hi, mid, lo = _split3(-fcol)
        k_aug = jnp.where(lane == AUG, hi, jnp.where(lane == AUG + 1, mid, jnp.where(lane == AUG + 2, lo,
                          jnp.where(lane == AUG + 3, 1.0, 0.0))))
        head = lane < FOX_DH
        qa[hh] = jnp.where(head, _head_lanes(qn, hh), q_aug).astype(MXU_DTYPE)
        ka[hh] = jnp.where(head, _head_lanes(kn, hh), k_aug).astype(MXU_DTYPE)
        va[hh] = jnp.where(head, _head_lanes(vv, hh), 0.0).astype(MXU_DTYPE)
    return bd, lane, qx, kx, rq, rk


def _fox_specs(t_, fw, col0):
    npair = fw // LANE

    def col(off):
        return pl.BlockSpec((1, t_, LANE), lambda b, p: (b, 0, col0 + 3 * p + off))

    pair = pl.BlockSpec((1, t_, LANE), lambda b, p: (b, 0, p))
    full = pl.BlockSpec((1, t_, LANE), lambda b, p: (b, 0, 0))
    gvec = pl.BlockSpec((1, LANE), lambda b, p: (0, 0))
    lse = pl.BlockSpec((1, 1, t_, LANE), lambda b, p: (b, p, 0, 0))
    return col, pair, full, gvec, lse


def _fox_fwd(zm, fc, gq2, gk2, fw, col0):
    b_, t_, _ = zm.shape
    npair = fw // LANE
    tq = min(FOX_TILE, t_)
    bw = min(FOX_BAND, t_)
    nband, tpb = t_ // bw, bw // tq
    scale = FOX_DH ** -0.5
    col, pair, full, gvec, lse_spec = _fox_specs(t_, fw, col0)

    def body(q_ref, k_ref, v_ref, fc_ref, gq_ref, gk_ref, o_ref, lse_ref, qa, ka, va):
        p = pl.program_id(1)
        _fox_operands(q_ref, k_ref, v_ref, fc_ref, gq_ref[...] * scale, gk_ref[...], p, qa, ka, va)
        ri = lax.broadcasted_iota(jnp.int32, (tq, bw), 0)
        ci = lax.broadcasted_iota(jnp.int32, (tq, bw), 1)
        lane = lax.broadcasted_iota(jnp.int32, (tq, LANE), 1)

        for band in range(nband):
            c0 = band * bw

            def qtile(ii, _, c0=c0):
                r0 = pl.multiple_of(c0 + ii * tq, tq)
                rows = pl.ds(r0, tq)
                keep = c0 + ci <= r0 + ri
                res = []
                for hh in range(2):
                    qb = qa[hh, rows, :]
                    s_b = jnp.where(keep, _nt(qb, ka[hh, c0:c0 + bw, :]), NEG)
                    m = jnp.max(s_b, axis=-1, keepdims=True)
                    if c0:
                        s_a = _nt(qb, ka[hh, 0:c0, :])
                        m = jnp.maximum(m, jnp.max(s_a, axis=-1, keepdims=True))
                    p_b = jnp.exp(s_b - m)
                    l = jnp.sum(p_b, axis=-1, keepdims=True)
                    acc = _nn(p_b, va[hh, c0:c0 + bw, :])
                    if c0:
                        p_a = jnp.exp(s_a - m)
                        l = l + jnp.sum(p_a, axis=-1, keepdims=True)
                        acc = acc + _nn(p_a, va[hh, 0:c0, :])
                    res.append((acc / l, m + jnp.log(l)))
                (o0, e0), (o1, e1) = res
                o_ref[0, rows, :] = jnp.where(lane < FOX_DH, o0, pltpu.roll(o1, FOX_DH, 1))
                lse_ref[0, 0, rows, :] = jnp.where(lane == 0, e0, jnp.where(lane == 1, e1, 0.0))
                return 0

            lax.fori_loop(0, tpb, qtile, 0)

    return pl.pallas_call(
        body, name="fox_fwd", grid=(b_, npair),
        in_specs=[col(0), col(1), col(2), full, gvec, gvec],
        out_specs=[pair, lse_spec],
        out_shape=[jax.ShapeDtypeStruct((b_, t_, fw), F32), jax.ShapeDtypeStruct((b_, npair, t_, LANE), F32)],
        scratch_shapes=[pltpu.VMEM((2, t_, LANE), MXU_DTYPE)] * 3,
        compiler_params=_params(2),
    )(zm, zm, zm, fc, gq2, gk2)


def _norm_bwd(x, dy, g):
    r = lax.rsqrt(jnp.mean(x * x, axis=-1, keepdims=True) + EPS)
    dyg = dy * g
    dx = r * dyg - x * (r * r * r) * jnp.mean(dyg * x, axis=-1, keepdims=True)
    return dx, jnp.sum(dy * x * r, axis=0, keepdims=True)


def _fox_bwd(zm, o, do, lse, fc, gq2, gk2, fw, col0, dz, rider=None):
    b_, t_, _ = zm.shape
    npair = fw // LANE
    tq = min(FOX_TILE, t_)
    nb = t_ // tq
    bw = min(FOX_BAND, t_)
    nband, tpb = t_ // bw, bw // tq
    scale = FOX_DH ** -0.5
    col, pair, full, gvec, lse_spec = _fox_specs(t_, fw, col0)

    def body(q_ref, k_ref, v_ref, o_ref, do_ref, lse_ref, fc_ref, gq_ref, gk_ref, _,
             dz_ref, dfc_ref, dgq_ref, dgk_ref, qa, ka, va, da, rowv, dq_acc, dk_acc, dv_acc):
        b, p = pl.program_id(0), pl.program_id(1)
        gq2v, gk2v = gq_ref[...] * scale, gk_ref[...]
        bd, lane, qx, kx, rq, rk = _fox_operands(q_ref, k_ref, v_ref, fc_ref, gq2v, gk2v, p, qa, ka, va)
        head = lane < FOX_DH
        dov = do_ref[0]
        dsum = _dot_right_exact(dov * o_ref[0], bd)
        eye = (lax.broadcasted_iota(jnp.int32, (tq, tq), 0) == lax.broadcasted_iota(jnp.int32, (tq, tq), 1)).astype(F32)
        for hh in range(2):
            da[hh] = jnp.where(head, _head_lanes(dov, hh), 0.0).astype(MXU_DTYPE)
            for blk in range(nb):
                rs = slice(blk * tq, (blk + 1) * tq)
                rowv[2 * hh:2 * hh + 1, rs] = jnp.sum(eye * lse_ref[0, 0, rs, hh:hh + 1], axis=0, keepdims=True)
                rowv[2 * hh + 1:2 * hh + 2, rs] = jnp.sum(eye * dsum[rs, hh * FOX_DH:hh * FOX_DH + 1], axis=0, keepdims=True)
        dq_acc[...] = jnp.zeros(dq_acc.shape, F32)
        ri = lax.broadcasted_iota(jnp.int32, (tq, bw), 0)
        ci = lax.broadcasted_iota(jnp.int32, (tq, bw), 1)

        def part(hh, kb, vb, lo, hi, keep):
            qm, dm = qa[hh, lo:hi, :], da[hh, lo:hi, :]
            pt = jnp.exp(_nt(kb, qm) - rowv[2 * hh:2 * hh + 1, lo:hi])
            if keep is not None:
                pt = jnp.where(keep, pt, 0.0)
            dst = pt * (_nt(vb, dm) - rowv[2 * hh + 1:2 * hh + 2, lo:hi])
            dq_acc[hh, lo:hi, :] += _tn(dst, kb)
            return _nn(dst, qm), _nn(pt, dm)

        for band in range(nband):
            c0 = band * bw

            def kvtile(jj, _, c0=c0):
                r0 = pl.multiple_of(c0 + jj * tq, tq)
                rows = pl.ds(r0, tq)
                keep = c0 + ci >= r0 + ri
                for hh in range(2):
                    kb, vb = ka[hh, rows, :], va[hh, rows, :]
                    dk_t, dv_t = part(hh, kb, vb, c0, c0 + bw, keep)
                    if c0 + bw < t_:
                        dk_u, dv_u = part(hh, kb, vb, c0 + bw, t_, None)
                        dk_t, dv_t = dk_t + dk_u, dv_t + dv_u
                    dk_acc[hh, rows, :] = dk_t
                    dv_acc[hh, rows, :] = dv_t
                return 0

            lax.fori_loop(0, tpb, kvtile, 0)

        dq0, dq1, dk0, dk1 = dq_acc[0], dq_acc[1], dk_acc[0], dk_acc[1]
        dqn = jnp.where(head, dq0, pltpu.roll(dq1, FOX_DH, 1))
        dkn = jnp.where(head, dk0, pltpu.roll(dk1, FOX_DH, 1))
        dqx, gq_part = _pair_norm_bwd(qx, rq, dqn, gq2v, bd)
        dkx, gk_part = _pair_norm_bwd(kx, rk, dkn, gk2v, bd)
        dz_ref[0, :, :LANE] = dqx.astype(dz_ref.dtype)
        dz_ref[0, :, LANE:2 * LANE] = dkx.astype(dz_ref.dtype)
        dz_ref[0, :, 2 * LANE:] = jnp.where(head, dv_acc[0], pltpu.roll(dv_acc[1], FOX_DH, 1)).astype(dz_ref.dtype)

        def bias_grad(dqh, dkh):
            return (jnp.sum(jnp.where(lane == AUG + 3, dqh, 0.0), axis=-1, keepdims=True)
                    - jnp.sum(jnp.where(lane == AUG, dkh, 0.0), axis=-1, keepdims=True))

        dfc_ref[0, 0] = jnp.where(lane == 0, bias_grad(dq0, dk0), jnp.where(lane == 1, bias_grad(dq1, dk1), 0.0))
        first = jnp.logical_and(b == 0, p == 0)
        _acc(dgq_ref, gq_part * scale, first)
        _acc(dgk_ref, gk_part, first)

    gs = jax.ShapeDtypeStruct((1, LANE), F32)
    return _ride_call(
        body, rider, name="fox_bwd", grid=(b_, npair),
        in_specs=[col(0), col(1), col(2), pair, pair, lse_spec, full, gvec, gvec, ANY],
        out_specs=[pl.BlockSpec((1, t_, 3 * LANE), lambda b, p: (b, 0, col0 // 3 + p)), lse_spec, gvec, gvec],
        out_shape=[jax.ShapeDtypeStruct(dz.shape, dz.dtype), jax.ShapeDtypeStruct((b_, npair, t_, LANE), F32), gs, gs],
        scratch=[pltpu.VMEM((2, t_, LANE), MXU_DTYPE)] * 4
        + [pltpu.VMEM((8, t_), F32)] + [pltpu.VMEM((2, t_, LANE), F32)] * 3,
        args=(zm, zm, zm, o, do, lse, fc, gq2, gk2, dz), aliases={9: 0})


def _mem_specs(t_, m_, mw, col0):
    nh = mw // LANE
    qcol = pl.BlockSpec((1, t_, LANE), lambda b, h: (b, 0, col0 + h))
    kcol = pl.BlockSpec((1, m_, LANE), lambda b, h: (b, 0, h))
    vcol = pl.BlockSpec((1, m_, LANE), lambda b, h: (b, 0, nh + h))
    ycol = pl.BlockSpec((1, t_, LANE), lambda b, h: (b, 0, h))
    gvec = pl.BlockSpec((1, LANE), lambda b, h: (0, 0))
    return qcol, kcol, vcol, ycol, gvec


def _mem_fwd(zm, mkv, gq, gk, mw, col0):
    b_, t_, _ = zm.shape
    m_ = mkv.shape[1]
    tq = min(512, t_)
    nb = t_ // tq
    scale = MEM_DH ** -0.5
    qcol, kcol, vcol, ycol, gvec = _mem_specs(t_, m_, mw, col0)

    def body(q_ref, k_ref, v_ref, gq_ref, gk_ref, y_ref):
        gqv, gkv = gq_ref[...] * scale, gk_ref[...]
        kv = k_ref[0]
        kn = _mx(kv * lax.rsqrt(jnp.mean(kv * kv, axis=-1, keepdims=True) + EPS) * gkv)
        vv = _mx(v_ref[0])

        def blk(i, _):
            rows = pl.ds(pl.multiple_of(i * tq, tq), tq)
            qv = q_ref[0, rows, :]
            s = _nt(qv * lax.rsqrt(jnp.mean(qv * qv, axis=-1, keepdims=True) + EPS) * gqv, kn)
            e = jnp.exp(s - jnp.max(s, axis=-1, keepdims=True))
            y_ref[0, rows, :] = _nn(e / jnp.sum(e, axis=-1, keepdims=True), vv)
            return 0

        lax.fori_loop(0, nb, blk, 0)

    return pl.pallas_call(
        body, name="mem_fwd", grid=(b_, MEM_HEADS), in_specs=[qcol, kcol, vcol, gvec, gvec], out_specs=ycol,
        out_shape=jax.ShapeDtypeStruct((b_, t_, mw), F32), compiler_params=_params(2),
    )(zm, mkv, mkv, gq, gk)


def _mem_bwd(zm, mkv, dy, gq, gk, mw, col0, dz):
    b_, t_, _ = zm.shape
    m_ = mkv.shape[1]
    tq = min(512, t_)
    nb = t_ // tq
    scale = MEM_DH ** -0.5
    qcol, kcol, vcol, ycol, gvec = _mem_specs(t_, m_, mw, col0)

    def body(q_ref, k_ref, v_ref, dy_ref, gq_ref, gk_ref, _, dq_ref, dk_ref, dv_ref, dgq_ref, dgk_ref):
        gqv, gkv = gq_ref[...] * scale, gk_ref[...]
        kv = k_ref[0]
        kn = _mx(kv * lax.rsqrt(jnp.mean(kv * kv, axis=-1, keepdims=True) + EPS) * gkv)
        vv = _mx(v_ref[0])

        def blk(i, carry):
            dkn, dvv, dgq = carry
            rows = pl.ds(pl.multiple_of(i * tq, tq), tq)
            qv = q_ref[0, rows, :]
            qn = _mx(qv * lax.rsqrt(jnp.mean(qv * qv, axis=-1, keepdims=True) + EPS) * gqv)
            s = _nt(qn, kn)
            e = jnp.exp(s - jnp.max(s, axis=-1, keepdims=True))
            pm = e / jnp.sum(e, axis=-1, keepdims=True)
            dob = _mx(dy_ref[0, rows, :])
            dp = _nt(dob, vv)
            ds = pm * (dp - jnp.sum(dp * pm, axis=-1, keepdims=True))
            dqv, gq_part = _norm_bwd(qv, _nn(ds, kn), gqv)
            dq_ref[0, rows, :] = dqv.astype(dq_ref.dtype)
            return dkn + _tn(ds, qn), dvv + _tn(pm, dob), dgq + gq_part * scale

        z = jnp.zeros((m_, LANE), F32)
        dkn, dvv, dgq = lax.fori_loop(0, nb, blk, (z, z, jnp.zeros((1, LANE), F32)))
        dkv, dgk = _norm_bwd(kv, dkn, gkv)
        dk_ref[0] = dkv
        dv_ref[0] = dvv
        first = jnp.logical_and(pl.program_id(0) == 0, pl.program_id(1) == 0)
        _acc(dgq_ref, dgq, first)
        _acc(dgk_ref, dgk, first)

    kblk = pl.BlockSpec((1, m_, LANE), lambda b, h: (b, 0, h))
    gs = jax.ShapeDtypeStruct((1, LANE), F32)
    ks = jax.ShapeDtypeStruct((b_, m_, mw), F32)
    return pl.pallas_call(
        body, name="mem_bwd", grid=(b_, MEM_HEADS), in_specs=[qcol, kcol, vcol, ycol, gvec, gvec, ANY],
        out_specs=[qcol, kblk, kblk, gvec, gvec],
        out_shape=[jax.ShapeDtypeStruct(dz.shape, dz.dtype), ks, ks, gs, gs], input_output_aliases={6: 0},
        compiler_params=_params(2),
    )(zm, mkv, mkv, dy, gq, gk, dz)


def _merge_specs(tm, d, w, gcol):
    row_d = pl.BlockSpec((tm, d), lambda i: (i, 0))
    row_w = pl.BlockSpec((tm, w), lambda i: (i, 0))
    gates = [pl.BlockSpec((tm, d), functools.partial(lambda i, k: (i, gcol + k), k=k)) for k in range(3)]
    w_br = pl.BlockSpec((w, d), lambda i: (0, 0))
    w_o = pl.BlockSpec((d, d), lambda i: (0, 0))
    return row_d, row_w, gates, w_br, w_o


def _merge_fwd(x, ys, zm, w_brs, w_out, gcol, tm=256):
    n, d = x.shape
    w = ys[0].shape[1]
    tm = _tile(n, tm, 8)
    row_d, row_w, gates, w_br, w_o = _merge_specs(tm, d, w, gcol)

    def body(x_ref, ya, yb, yc, g0, g1, g2, wa, wb, wc, wo, x1_ref, mg_ref):
        mg = (_sig(g0[...]) * _nn(ya[...], wa[...]) + _sig(g1[...]) * _nn(yb[...], wb[...])
              + _sig(g2[...]) * _nn(yc[...], wc[...]))
        mg_ref[...] = mg.astype(mg_ref.dtype)
        x1_ref[...] = x_ref[...] + _nn(mg, wo[...])

    return pl.pallas_call(
        body, name="merge_fwd", grid=(n // tm,),
        in_specs=[row_d, row_w, row_w, row_w] + gates + [w_br, w_br, w_br, w_o],
        out_specs=[row_d, row_d],
        out_shape=[jax.ShapeDtypeStruct((n, d), F32), jax.ShapeDtypeStruct((n, d), MXU_DTYPE)],
        compiler_params=_params(1),
    )(x, *ys, zm, zm, zm, *w_brs, w_out)


def _merge_bwd(dx1, ys, zm, w_brs, w_out, gcol, tm=256):
    n, d = dx1.shape
    w = ys[0].shape[1]
    tm = _tile(n, tm, 8)
    row_d, row_w, gates, w_br, w_o = _merge_specs(tm, d, w, gcol)

    def body(dx_ref, ya, yb, yc, g0, g1, g2, wa, wb, wc, wo, dgl_ref, dpa, dpb, dpc, dya, dyb, dyc):
        dm = _nt(dx_ref[...], wo[...])
        for k, (y, g, wr, dp_ref, dy_ref) in enumerate(((ya, g0, wa, dpa, dya), (yb, g1, wb, dpb, dyb),
                                                        (yc, g2, wc, dpc, dyc))):
            sg = _sig(g[...])
            pr = _nn(y[...], wr[...])
            dgl_ref[:, k * d:(k + 1) * d] = (dm * pr * sg * (1.0 - sg)).astype(dgl_ref.dtype)
            dp = (dm * sg).astype(dp_ref.dtype)
            dp_ref[...] = dp
            dy_ref[...] = _nt(dp, wr[...])

    sd = jax.ShapeDtypeStruct((n, d), MXU_DTYPE)
    sw = jax.ShapeDtypeStruct((n, w), F32)
    return pl.pallas_call(
        body, name="merge_bwd", grid=(n // tm,),
        in_specs=[row_d, row_w, row_w, row_w] + gates + [w_br, w_br, w_br, w_o],
        out_specs=[pl.BlockSpec((tm, 3 * d), lambda i: (i, 0)), row_d, row_d, row_d, row_w, row_w, row_w],
        out_shape=[jax.ShapeDtypeStruct((n, zm.shape[1]), MXU_DTYPE), sd, sd, sd, sw, sw, sw],
        compiler_params=_params(1),
    )(dx1, *ys, zm, zm, zm, *w_brs, w_out)


CONV_ROWS = 256
HALO = 8


def _ext(ref, r0, t_):
    rc = min(CONV_ROWS, t_)
    a, b = max(r0 - HALO, 0), min(r0 + rc + HALO, t_)
    parts = []
    if r0 - HALO < 0:
        parts.append(jnp.zeros((HALO, ref.shape[2]), F32))
    parts.append(ref[0, a:b, :].astype(F32))
    if r0 + rc + HALO > t_:
        parts.append(jnp.zeros((HALO, ref.shape[2]), F32))
    return jnp.concatenate(parts, axis=0) if len(parts) > 1 else parts[0]


def _gelu_parts(ac):
    cdf = 0.5 * (1.0 + _erf(ac * (2.0 ** -0.5)))
    pdf = jnp.exp(-0.5 * ac * ac) * ((2.0 * math.pi) ** -0.5)
    return cdf, pdf


def _conv_taps(a_ext, cw, cb):
    return cw[0:1, :] * pltpu.roll(a_ext, 2, 0) + cw[1:2, :] * pltpu.roll(a_ext, 1, 0) + cw[2:3, :] * a_ext + cb


def _glu_specs(t_, f, g):
    gate = pl.BlockSpec((1, t_, g), lambda j, b: (b, 0, j))
    value = pl.BlockSpec((1, t_, g), lambda j, b: (b, 0, f // g + j))
    cwb = pl.BlockSpec((3, g), lambda j, b: (0, j))
    cbb = pl.BlockSpec((1, g), lambda j, b: (0, j))
    return gate, value, cwb, cbb


def _glu_fwd(u, cw, cb):
    b_, t_, f2 = u.shape
    f = f2 // 2
    g = min(FFN_GROUP, f)
    rc = min(CONV_ROWS, t_)
    gate, value, cwb, cbb = _glu_specs(t_, f, g)

    def body(a_ref, v_ref, cw_ref, cb_ref, y_ref):
        cwv, cbv = cw_ref[...], cb_ref[...]
        for r0 in range(0, t_, rc):
            ac = _conv_taps(_ext(a_ref, r0, t_), cwv, cbv)[HALO:HALO + rc]
            cdf, _ = _gelu_parts(ac)
            y_ref[0, r0:r0 + rc, :] = (ac * cdf * v_ref[0, r0:r0 + rc, :]).astype(y_ref.dtype)

    return pl.pallas_call(
        body, name="glu_fwd", grid=(f // g, b_), in_specs=[gate, value, cwb, cbb], out_specs=gate,
        out_shape=jax.ShapeDtypeStruct((b_, t_, f), MXU_DTYPE), compiler_params=_params(2),
    )(u, u, cw, cb)


def _glu_bwd(u, dy, cw, cb):
    b_, t_, f2 = u.shape
    f = f2 // 2
    g = min(FFN_GROUP, f)
    rc = min(CONV_ROWS, t_)
    ne = rc + 2 * HALO
    gate, value, cwb, cbb = _glu_specs(t_, f, g)

    def body(a_ref, v_ref, dy_ref, cw_ref, cb_ref, da_ref, dv_ref, dcw_ref, dcb_ref):
        cwv, cbv = cw_ref[...], cb_ref[...]
        dcw = [jnp.zeros((1, g), F32) for _ in range(3)]
        dcb = jnp.zeros((1, g), F32)
        for r0 in range(0, t_, rc):
            a_ext, v_ext, dy_ext = _ext(a_ref, r0, t_), _ext(v_ref, r0, t_), _ext(dy_ref, r0, t_)
            ac = _conv_taps(a_ext, cwv, cbv)
            cdf, pdf = _gelu_parts(ac)
            dac = dy_ext * v_ext * (cdf + ac * pdf)
            da = cwv[2:3, :] * dac + cwv[1:2, :] * pltpu.roll(dac, ne - 1, 0) + cwv[0:1, :] * pltpu.roll(dac, ne - 2, 0)
            mid = slice(HALO, HALO + rc)
            da_ref[0, r0:r0 + rc, :] = da[mid].astype(da_ref.dtype)
            dv_ref[0, r0:r0 + rc, :] = (dy_ext[mid] * ac[mid] * cdf[mid]).astype(dv_ref.dtype)
            dacm = dac[mid]
            dcw[0] = dcw[0] + jnp.sum(dacm * pltpu.roll(a_ext, 2, 0)[mid], axis=0, keepdims=True)
            dcw[1] = dcw[1] + jnp.sum(dacm * pltpu.roll(a_ext, 1, 0)[mid], axis=0, keepdims=True)
            dcw[2] = dcw[2] + jnp.sum(dacm * a_ext[mid], axis=0, keepdims=True)
            dcb = dcb + jnp.sum(dacm, axis=0, keepdims=True)
        first = pl.program_id(1) == 0
        _acc(dcw_ref, jnp.concatenate(dcw, axis=0), first)
        _acc(dcb_ref, dcb, first)

    sds = jax.ShapeDtypeStruct((b_, t_, f), MXU_DTYPE)
    return pl.pallas_call(
        body, name="glu_bwd", grid=(f // g, b_), in_specs=[gate, value, gate, cwb, cbb],
        out_specs=[gate, gate, cwb, cbb],
        out_shape=[sds, sds, jax.ShapeDtypeStruct((3, f), F32), jax.ShapeDtypeStruct((1, f), F32)],
        compiler_params=_params(2),
    )(u, u, dy, cw, cb)


def _place():
    x, y, c = lax.axis_index("x"), lax.axis_index("y"), lax.axis_index("c")
    chips = [(1 - x, y), (x, 1 - y), (1 - x, 1 - y)]
    return x, y, c, chips


def _remote(src, dst, send_sem, recv_sem, to):
    return pltpu.make_async_remote_copy(src_ref=src, dst_ref=dst, send_sem=send_sem, recv_sem=recv_sem,
                                        device_id=to, device_id_type=MESH)


STACK, COLS = "stack", "cols"


def _shard_ref(ref, kind, s, rows, c):
    if kind == COLS:
        cols = pl.ds(pl.multiple_of(s * c, LANE), c)
        return ref.at[:, cols] if rows is None else ref.at[rows, cols]
    return ref.at[s] if rows is None else ref.at[s, rows, :]


def _halves(c, half):
    mine = pl.ds(pl.multiple_of(c * half, 16), half)
    theirs = pl.ds(pl.multiple_of((1 - c) * half, 16), half)
    return mine, theirs


def _gather_parts(kinds):
    def first_copies(ins, outs, sems):
        x, y, c, chips = _place()
        me = 2 * x + y
        cps = []
        for i, (w_ref, o_ref, kind) in enumerate(zip(ins, outs, kinds)):
            r, cw = w_ref.shape
            mine, _ = _halves(c, r // 2)
            for j, chip in enumerate(chips):
                cps.append(_remote(w_ref.at[mine], _shard_ref(o_ref, kind, me, mine, cw), sems[0].at[6 * i + j],
                                   sems[1].at[6 * i + j], (*chip, c)))
        return cps

    def start(ins, outs, sems):
        for cp in first_copies(ins, outs, sems):
            cp.start()

    def finish(ins, outs, sems):
        x, y, c, chips = _place()
        sib = (x, y, 1 - c)
        passed = []
        for i, (w_ref, o_ref, kind) in enumerate(zip(ins, outs, kinds)):
            r, cw = w_ref.shape
            mine, _ = _halves(c, r // 2)
            for j, (px, py) in enumerate(chips):
                blk = _shard_ref(o_ref, kind, 2 * px + py, mine, cw)
                _remote(blk, blk, sems[0].at[6 * i + j], sems[1].at[6 * i + j], sib).wait_recv()
                passed.append(_remote(blk, blk, sems[0].at[6 * i + 3 + j], sems[1].at[6 * i + 3 + j], sib))
                passed[-1].start()
        for i, (w_ref, o_ref, kind) in enumerate(zip(ins, outs, kinds)):
            r, cw = w_ref.shape
            _, theirs = _halves(c, r // 2)
            for j, (px, py) in enumerate(chips):
                blk = _shard_ref(o_ref, kind, 2 * px + py, theirs, cw)
                _remote(blk, blk, sems[0].at[6 * i + 3 + j], sems[1].at[6 * i + 3 + j], sib).wait_recv()
        for cp in first_copies(ins, outs, sems) + passed:
            cp.wait_send()

    return start, finish


def _gather_shapes(shards, kinds):
    return [jax.ShapeDtypeStruct((a.shape[0], N_CHIPS * a.shape[1]) if k == COLS else (N_CHIPS,) + a.shape, a.dtype)
            for a, k in zip(shards, kinds)]


def _gather_sems(nw):
    return [pltpu.SemaphoreType.DMA((6 * nw,)), pltpu.SemaphoreType.DMA((6 * nw,))]


def _gather_shards(shards, kinds):
    nw = len(shards)
    start, finish = _gather_parts(kinds)

    def body(*refs):
        ins, outs, sems = refs[:nw], refs[nw:2 * nw], refs[2 * nw:]
        start(ins, outs, sems)
        finish(ins, outs, sems)

    return pl.pallas_call(
        body, name="gather_shards", in_specs=[ANY] * nw, out_specs=[ANY] * nw,
        out_shape=_gather_shapes(shards, kinds), scratch_shapes=_gather_sems(nw),
    )(*shards)


def _gather_rider(shards, kinds):
    start, finish = _gather_parts(kinds)
    return _Rider(list(shards), _gather_shapes(shards, kinds), _gather_sems(len(shards)), start, finish)


def _half_shape(g, kind):
    if kind == COLS:
        return (g.shape[0] // 2, g.shape[1])
    return (g.shape[0], g.shape[1] // 2, g.shape[2])


def _swap_parts(kinds):
    def copies(ins, outs, sems):
        x, y, c, _ = _place()
        cps = []
        for i, (g_ref, a_ref, kind) in enumerate(zip(ins, outs, kinds)):
            r = g_ref.shape[0] if kind == COLS else g_ref.shape[1]
            _, theirs = _halves(c, r // 2)
            src = g_ref.at[theirs] if kind == COLS else g_ref.at[:, theirs]
            cps.append(_remote(src, a_ref, sems[0].at[i], sems[1].at[i], (x, y, 1 - c)))
        return cps

    def start(ins, outs, sems):
        for cp in copies(ins, outs, sems):
            cp.start()

    def finish(ins, outs, sems):
        for cp in copies(ins, outs, sems):
            cp.wait()

    return start, finish


def _swap_shapes(gs, kinds):
    return [jax.ShapeDtypeStruct(_half_shape(g, k), g.dtype) for g, k in zip(gs, kinds)]


def _pair_swap_halves(gs, kinds, name):
    nw = len(gs)
    start, finish = _swap_parts(kinds)

    def body(*refs):
        ins, outs, sems = refs[:nw], refs[nw:2 * nw], refs[2 * nw:]
        start(ins, outs, sems)
        finish(ins, outs, sems)

    return pl.pallas_call(
        body, name=name, in_specs=[ANY] * nw, out_specs=[ANY] * nw, out_shape=_swap_shapes(gs, kinds),
        scratch_shapes=[pltpu.SemaphoreType.DMA((nw,)), pltpu.SemaphoreType.DMA((nw,))],
    )(*gs)


def _swap_rider(gs, kinds):
    start, finish = _swap_parts(kinds)
    nw = len(gs)
    return _Rider(list(gs), _swap_shapes(gs, kinds), [pltpu.SemaphoreType.DMA((nw,)), pltpu.SemaphoreType.DMA((nw,))],
                  start, finish)


def _row_tile(rows, width, itemsize=4, target=2 ** 21):
    return _tile(rows, max(8, target // (width * itemsize)), 8)


def _add_half(g, a, kind, c_idx, name):
    if kind == COLS:
        half, wd = a.shape
        tr = _row_tile(half, wd)
        nblk = half // tr
        grid = (nblk,)
        g_spec = pl.BlockSpec((tr, wd), lambda i, c_ref: (c_ref[0] * nblk + i, 0))
        a_spec = pl.BlockSpec((tr, wd), lambda i, c_ref: (i, 0))
    else:
        n, half, wd = a.shape
        tr = _row_tile(half, wd)
        nblk = half // tr
        grid = (n, nblk)
        g_spec = pl.BlockSpec((1, tr, wd), lambda s, i, c_ref: (s, c_ref[0] * nblk + i, 0))
        a_spec = pl.BlockSpec((1, tr, wd), lambda s, i, c_ref: (s, i, 0))

    def body(c_ref, g_ref, a_ref, o_ref):
        o_ref[...] = (g_ref[...] + a_ref[...]).astype(o_ref.dtype)

    return pl.pallas_call(
        body, name=name,
        grid_spec=pltpu.PrefetchScalarGridSpec(num_scalar_prefetch=1, grid=grid, in_specs=[g_spec, a_spec],
                                               out_specs=a_spec),
        out_shape=jax.ShapeDtypeStruct(a.shape, EXCHANGE_DTYPE), compiler_params=_params(len(grid)),
    )(c_idx, g, a)


def _exchange_parts(kinds):
    def copies(ins, outs, sems):
        x, y, c, chips = _place()
        me = 2 * x + y
        cps = []
        for i, (p_ref, b_ref, kind) in enumerate(zip(ins, outs, kinds)):
            cw = b_ref.shape[2]
            for j, (px, py) in enumerate(chips):
                cps.append(_remote(_shard_ref(p_ref, kind, 2 * px + py, None, cw), b_ref.at[me],
                                   sems[0].at[3 * i + j], sems[1].at[3 * i + j], (px, py, c)))
        return cps

    def start(ins, outs, sems):
        for cp in copies(ins, outs, sems):
            cp.start()

    def finish(ins, outs, sems):
        x, y, c, chips = _place()
        for i, b_ref in enumerate(outs):
            for j, (px, py) in enumerate(chips):
                blk = b_ref.at[2 * px + py]
                _remote(blk, blk, sems[0].at[3 * i + j], sems[1].at[3 * i + j], (px, py, c)).wait_recv()
        for cp in copies(ins, outs, sems):
            cp.wait_send()

    return start, finish


def _exchange_shapes(ps, kinds):
    return [jax.ShapeDtypeStruct((N_CHIPS,) + ((p.shape[0], p.shape[1] // N_CHIPS) if k == COLS else tuple(p.shape[1:])),
                                 p.dtype) for p, k in zip(ps, kinds)]


def _exchange_sems(nw):
    return [pltpu.SemaphoreType.DMA((3 * nw,)), pltpu.SemaphoreType.DMA((3 * nw,))]


def _exchange_rider(ps, kinds):
    start, finish = _exchange_parts(kinds)
    return _Rider(list(ps), _exchange_shapes(ps, kinds), _exchange_sems(len(ps)), start, finish)


def _sum_chips(bq, name):
    n, h, wd = bq.shape
    tr = _row_tile(h, wd * n)

    def body(b_ref, o_ref):
        acc = b_ref[0].astype(F32)
        for s in range(1, n):
            acc = acc + b_ref[s].astype(F32)
        o_ref[...] = acc

    return pl.pallas_call(
        body, name=name, grid=(h // tr,),
        in_specs=[pl.BlockSpec((n, tr, wd), lambda i: (0, i, 0))], out_specs=pl.BlockSpec((tr, wd), lambda i: (i, 0)),
        out_shape=jax.ShapeDtypeStruct((h, wd), F32), compiler_params=_params(1),
    )(bq)


def _pair_join_halves(qs):
    nw = len(qs)

    def body(*refs):
        ins, outs = refs[:nw], refs[nw:2 * nw]
        send_sems, recv_sems = refs[2 * nw:]
        x, y, c, _ = _place()
        sent = []
        for i, (q_ref, o_ref) in enumerate(zip(ins, outs)):
            mine, _ = _halves(c, q_ref.shape[0])
            sent.append(_remote(q_ref, o_ref.at[mine], send_sems.at[i], recv_sems.at[i], (x, y, 1 - c)))
            sent[-1].start()
        for i, (q_ref, o_ref) in enumerate(zip(ins, outs)):
            _, theirs = _halves(c, q_ref.shape[0])
            _remote(q_ref, o_ref.at[theirs], send_sems.at[i], recv_sems.at[i], (x, y, 1 - c)).wait_recv()
        for cp in sent:
            cp.wait_send()

    return pl.pallas_call(
        body, name="pair_join_halves", in_specs=[ANY] * nw, out_specs=[ANY] * nw,
        out_shape=[jax.ShapeDtypeStruct((2 * q.shape[0], q.shape[1]), q.dtype) for q in qs],
        scratch_shapes=[pltpu.SemaphoreType.DMA((nw,)), pltpu.SemaphoreType.DMA((nw,))],
    )(*qs)


def _all_sum_small(s, name):
    sr, w = s.shape

    def body(s_ref, o_ref, buf, send_sems, recv_sems):
        x, y, c, _ = _place()
        me = 4 * x + 2 * y + c
        buf[me] = s_ref[...]
        peers = []
        for k in range(1, 8):
            px = 1 - x if k & 4 else x
            py = 1 - y if k & 2 else y
            pc = 1 - c if k & 1 else c
            peers.append((px, py, pc))
        sent = [_remote(s_ref, buf.at[me], send_sems.at[k], recv_sems.at[k], peer) for k, peer in enumerate(peers)]
        for cp in sent:
            cp.start()
        for k, (px, py, pc) in enumerate(peers):
            _remote(s_ref, buf.at[4 * px + 2 * py + pc], send_sems.at[k], recv_sems.at[k], (px, py, pc)).wait_recv()
        for cp in sent:
            cp.wait_send()
        acc = buf[0]
        for d in range(1, 8):
            acc = acc + buf[d]
        o_ref[...] = acc

    vm = pl.BlockSpec(memory_space=pltpu.VMEM)
    return pl.pallas_call(
        body, name=name, in_specs=[vm], out_specs=vm, out_shape=jax.ShapeDtypeStruct((sr, w), F32),
        scratch_shapes=[pltpu.VMEM((8, sr, w), F32), pltpu.SemaphoreType.DMA((7,)), pltpu.SemaphoreType.DMA((7,))],
    )(s)


BIG = ("w_in", "mem_kv_w", "w_br_hgrn", "w_br_fox", "w_br_mem", "w_out", "ffn_w_up", "ffn_w_down")
KIND = {"w_in": STACK, "mem_kv_w": STACK, "w_br_hgrn": COLS, "w_br_fox": COLS, "w_br_mem": COLS, "w_out": STACK,
        "ffn_w_up": COLS, "ffn_w_down": STACK}
ROW_SHARDED = ("mem_kv_w", "w_out", "ffn_w_down")
FIRST = ("w_in",)
REST = tuple(nm for nm in BIG if nm not in FIRST)
LAST = ("w_in",)


def _z_layout(d, hw, fw, mw):
    gate, npair, nh, nm = 3 * d // LANE, fw // LANE, hw // LANE, mw // LANE
    fox0, hg0 = gate, gate + 3 * npair
    o_fox, o_mem = 4 * nh, 4 * nh + 3 * npair
    order = [o_mem + nm + j for j in range(gate)]
    order += [o_fox + k * npair + p for p in range(npair) for k in range(3)]
    order += [k * nh + h for h in range(nh) for k in range(4)]
    order += [o_mem + h for h in range(nm)]
    assert fox0 % 3 == 0 and hg0 % 4 == 0
    return fox0, hg0, hg0 + 4 * nh, order


def _reorder_blocks(a, order):
    runs, start = [], 0
    for i in range(1, len(order) + 1):
        if i == len(order) or order[i] != order[i - 1] + 1:
            runs.append((order[start], order[i - 1] + 1))
            start = i
    return jnp.concatenate([a[:, lo * LANE:hi * LANE] for lo, hi in runs], axis=1)


def _put_shard(arr, kind, s, piece):
    if kind == COLS:
        return lax.dynamic_update_slice(arr, piece, (0, s * piece.shape[1]))
    return lax.dynamic_update_slice(arr, piece[None], (s, 0, 0))


def _take_shard(arr, kind, s):
    if kind == COLS:
        return lax.dynamic_slice(arr, (0, s * (arr.shape[1] // N_CHIPS)), (arr.shape[0], arr.shape[1] // N_CHIPS))
    return lax.dynamic_index_in_dim(arr, s, 0, keepdims=False)


def _w_in_pieces(cs, s1, nf):
    out = []
    for s in range(N_CHIPS):
        lo, hi = cs * s, cs * (s + 1)
        for a, b, forget in ((lo, min(hi, s1), False), (max(lo, s1), min(hi, s1 + nf), True), (max(lo, s1 + nf), hi, False)):
            if a < b:
                out.append((s, a - lo, b - lo, forget, a - s1 if forget else (a if a < s1 else a - nf)))
    return out


def _split_w_in(stacked, s1, nf):
    pieces = _w_in_pieces(stacked.shape[2], s1, nf)
    main = [stacked[s, :, a:b] for s, a, b, forget, _ in pieces if not forget]
    ff = [stacked[s, :, a:b] for s, a, b, forget, _ in pieces if forget]
    return jnp.concatenate(main, axis=1), jnp.concatenate(ff, axis=1)


def _join_w_in(g_main, g_ff, s1, nf):
    cs = (g_main.shape[1] + nf) // N_CHIPS
    shards = [[] for _ in range(N_CHIPS)]
    for s, a, b, forget, off in _w_in_pieces(cs, s1, nf):
        shards[s].append((g_ff if forget else g_main)[:, off:off + b - a])
    return jnp.stack([jnp.concatenate(p, axis=1) if len(p) > 1 else p[0] for p in shards])


SMALL = ("norm_mix_g", "norm_mem_g", "norm_ffn_g", "hgrn_lb_logits", "hgrn_norm_g", "fox_f_bias", "fox_q_norm_g",
         "fox_k_norm_g", "mem_q_norm_g", "mem_k_norm_g", "ffn_conv_b")


def _pack_small(vals):
    flats, total = [], 0
    for v in vals:
        flat = v.reshape(-1).astype(F32)
        n = -(-flat.shape[0] // FLAT_W)
        flats.append(jnp.pad(flat, (0, n * FLAT_W - flat.shape[0])))
        total += n
    if -total % 8:
        flats.append(jnp.zeros((-total % 8 * FLAT_W,), F32))
    return jnp.concatenate(flats).reshape(-1, FLAT_W)


def _unpack_small(buf, shapes):
    res, off = [], 0
    for shp in shapes:
        numel = math.prod(shp)
        n = -(-numel // FLAT_W)
        res.append(buf[off:off + n].reshape(-1)[:numel].reshape(shp))
        off += n
    return res


def _pad_lanes(v, width=LANE):
    return jnp.pad(v, ((0, 0), (0, width - v.shape[1])))


WEIGHTS = ("norm_mix_g", "norm_mem_g", "w_in", "hgrn_lb_logits", "hgrn_norm_g", "fox_f_bias", "fox_q_norm_g",
           "fox_k_norm_g", "mem_kv_w", "mem_q_norm_g", "mem_k_norm_g", "w_br_hgrn", "w_br_fox", "w_br_mem", "w_out",
           "norm_ffn_g", "ffn_w_up", "ffn_conv_w", "ffn_conv_b", "ffn_w_down")


def _local_step(x, mem, target, w, full, conv_w, late=None, hooks=None):
    b_, t_, d = x.shape
    n = b_ * t_
    hw, fw, mw = HG_HEADS * HG_D, FOX_HEADS * FOX_DH, MEM_HEADS * MEM_DH
    m_ = mem.shape[1]
    f = conv_w.shape[1]
    s1 = 4 * hw + 3 * fw
    fox_col, hg_col, mem_col, order = _z_layout(d, hw, fw, mw)
    gate_col = 0
    inverse = [order.index(j) for j in range(len(order))]

    w_main, w_ff = _split_w_in(full["w_in"], s1, FOX_HEADS)
    w_main = _reorder_blocks(w_main, order)
    w_ff = _pad_lanes(w_ff)
    f_bias = _pad_lanes(w["fox_f_bias"])
    cb = w["ffn_conv_b"]

    x2 = x.reshape(n, d)
    h = _rmsnorm_fwd(x2, w["norm_mix_g"], name="norm_mix_fwd")
    if late:
        zm, gathered = _matmul(h, w_main, name="in_proj", rider=_gather_rider(late[0], late[1]))
        full = {**full, **late[2](gathered)}
    else:
        zm = _matmul(h, w_main, name="in_proj")
    w_up = full["ffn_w_up"]
    w_brs = [full["w_br_hgrn"], full["w_br_fox"], full["w_br_mem"]]
    w_out, w_kv, w_down = full["w_out"], full["mem_kv_w"], full["ffn_w_down"]
    zf = _matmul(h, w_ff, name="in_proj_forget")
    zm3, zf3 = zm.reshape(b_, t_, -1), zf.reshape(b_, t_, LANE)
    ya = _hgrn_fwd(zm3, w["hgrn_lb_logits"], w["hgrn_norm_g"], hw, hg_col)
    fc = _fox_prep(zf3, f_bias)
    fox_gq, fox_gk = jnp.tile(w["fox_q_norm_g"], (1, 2)), jnp.tile(w["fox_k_norm_g"], (1, 2))
    yb, lse = _fox_fwd(zm3, fc, fox_gq, fox_gk, fw, fox_col)
    mem2 = mem.reshape(b_ * m_, d)
    hm = _rmsnorm_fwd(mem2, w["norm_mem_g"], name="norm_mem_fwd")
    mkv = _matmul(hm, w_kv, name="mem_kv_proj").reshape(b_, m_, 2 * mw)
    yc = _mem_fwd(zm3, mkv, w["mem_q_norm_g"], w["mem_k_norm_g"], mw, mem_col)
    ys = [ya.reshape(n, hw), yb.reshape(n, fw), yc.reshape(n, mw)]
    x1, merged = _merge_fwd(x2, ys, zm, w_brs, w_out, gate_col)
    h2 = _rmsnorm_fwd(x1, w["norm_ffn_g"], name="norm_ffn_fwd")
    u = _matmul(h2, w_up, name="ffn_up")
    u3 = u.reshape(b_, t_, 2 * f)
    yff = _glu_fwd(u3, conv_w, cb).reshape(n, f)
    dy, (loss_vec,), _ = _matmul_rows([yff], w_down, name="ffn_down_loss", tb=False, row_ins=[x1, target.reshape(n, d)],
                                      vec_ins=[], epilogue=_loss_epilogue, n_vec_out=1)

    grads = {}

    def ridden(name, call):
        if not hooks or name not in hooks:
            return call(None)[0]
        rider, then = hooks[name](grads)
        outs, extra = call(rider)
        then(extra)
        return outs

    dyff = _matmul(dy, w_down, tb=True, name="ffn_down_dx")
    grads["ffn_w_down"] = _matmul(yff, dy, ta=True, name="ffn_down_dw", tm=1408)
    du_a, du_v, grads["ffn_conv_w"], grads["ffn_conv_b"] = _glu_bwd(u3, dyff.reshape(b_, t_, f), conv_w, cb)
    du_a, du_v = du_a.reshape(n, f), du_v.reshape(n, f)
    dx1, (grads["norm_ffn_g"],), _ = _matmul_rows(
        [du_a, du_v], w_up, name="ffn_up_dx", tb=True, row_ins=[x1, dy], vec_ins=[w["norm_ffn_g"]],
        epilogue=_norm_bwd_epilogue(0), n_vec_out=1)
    grads["ffn_w_up"] = jnp.concatenate([_matmul(h2, du_a, ta=True, name="ffn_up_gate_dw"),
                                         _matmul(h2, du_v, ta=True, name="ffn_up_value_dw")], axis=1)

    dz, dpa, dpb, dpc, dya, dyb, dyc = _merge_bwd(dx1, ys, zm, w_brs, w_out, gate_col)
    dz = dz.reshape(b_, t_, -1)
    grads["w_out"] = _matmul(merged, dx1, ta=True, name="out_proj_dw")
    for nm, y_, dp_ in zip(("w_br_hgrn", "w_br_fox", "w_br_mem"), ys, (dpa, dpb, dpc)):
        grads[nm] = _matmul(y_, dp_, ta=True, name=nm + "_dw")

    dz, dmk, dmv, grads["mem_q_norm_g"], grads["mem_k_norm_g"] = _mem_bwd(
        zm3, mkv, dyc.reshape(b_, t_, mw), w["mem_q_norm_g"], w["mem_k_norm_g"], mw, mem_col, dz)
    dmkv = jnp.concatenate([dmk, dmv], axis=-1).reshape(b_ * m_, 2 * mw)
    grads["mem_kv_w"] = _matmul(hm, dmkv, ta=True, name="mem_kv_dw")
    dhm = _matmul(dmkv, w_kv, tb=True, name="mem_kv_dx")
    _, grads["norm_mem_g"] = _rmsnorm_bwd(mem2, [dhm], w["norm_mem_g"], None, name="norm_mem_bwd")

    dz, dfc, g_fq, g_fk = ridden("fox_bwd", lambda rider: _fox_bwd(
        zm3, yb, dyb.reshape(b_, t_, fw), lse, fc, fox_gq, fox_gk, fw, fox_col, dz, rider))
    grads["fox_q_norm_g"] = g_fq[:, :FOX_DH] + g_fq[:, FOX_DH:]
    grads["fox_k_norm_g"] = g_fk[:, :FOX_DH] + g_fk[:, FOX_DH:]
    dzf, g_fb = _fox_post(dfc, zf3, f_bias)
    grads["fox_f_bias"] = g_fb[:, :FOX_HEADS]

    dz, grads["hgrn_lb_logits"], grads["hgrn_norm_g"] = ridden("hgrn_bwd", lambda rider: _hgrn_bwd(
        zm3, dya.reshape(b_, t_, hw), w["hgrn_lb_logits"], w["hgrn_norm_g"], hw, hg_col, dz, rider))
    dzm = dz.reshape(n, -1)
    dzf2 = dzf.reshape(n, LANE)
    g_main = _matmul(h, dzm, ta=True, name="in_proj_dw")
    g_ff = _matmul(h, dzf2, ta=True, name="in_proj_forget_dw")
    grads["w_in"] = _join_w_in(_reorder_blocks(g_main, inverse), g_ff[:, :FOX_HEADS], s1, FOX_HEADS)

    dh_b = _matmul(dzf2, w_ff, tb=True, name="in_proj_forget_dx")

    def in_proj_dx(rider):
        dx, vecs, extra = _matmul_rows([dzm], w_main, name="in_proj_dx", tb=True, row_ins=[x2, dx1, dh_b],
                                       vec_ins=[w["norm_mix_g"]], epilogue=_norm_bwd_epilogue(1), n_vec_out=1,
                                       rider=rider)
        return [dx, vecs[0]], extra

    grad_x, grads["norm_mix_g"] = ridden("in_proj_dx", in_proj_dx)
    return loss_vec, grad_x.reshape(b_, t_, d), grads


def kernel(x, mem, norm_mix_g, norm_mem_g, w_in, hgrn_lb_logits, hgrn_norm_g, fox_f_bias, fox_q_norm_g, fox_k_norm_g, mem_kv_w, mem_q_norm_g, mem_k_norm_g, w_br_hgrn, w_br_fox, w_br_mem, w_out, norm_ffn_g, ffn_w_up, ffn_conv_w, ffn_conv_b, ffn_w_down, loss_target, m_norm_mix_g, m_norm_mem_g, m_w_in, m_hgrn_lb_logits, m_hgrn_norm_g, m_fox_f_bias, m_fox_q_norm_g, m_fox_k_norm_g, m_mem_kv_w, m_mem_q_norm_g, m_mem_k_norm_g, m_w_br_hgrn, m_w_br_fox, m_w_br_mem, m_w_out, m_norm_ffn_g, m_ffn_w_up, m_ffn_conv_w, m_ffn_conv_b, m_ffn_w_down, v_norm_mix_g, v_norm_mem_g, v_w_in, v_hgrn_lb_logits, v_hgrn_norm_g, v_fox_f_bias, v_fox_q_norm_g, v_fox_k_norm_g, v_mem_kv_w, v_mem_q_norm_g, v_mem_k_norm_g, v_w_br_hgrn, v_w_br_fox, v_w_br_mem, v_w_out, v_norm_ffn_g, v_ffn_w_up, v_ffn_conv_w, v_ffn_conv_b, v_ffn_w_down):
    w = dict(zip(WEIGHTS, (norm_mix_g, norm_mem_g, w_in, hgrn_lb_logits, hgrn_norm_g, fox_f_bias, fox_q_norm_g,
                           fox_k_norm_g, mem_kv_w, mem_q_norm_g, mem_k_norm_g, w_br_hgrn, w_br_fox, w_br_mem, w_out,
                           norm_ffn_g, ffn_w_up, ffn_conv_w, ffn_conv_b, ffn_w_down)))
    m = dict(zip(WEIGHTS, (m_norm_mix_g, m_norm_mem_g, m_w_in, m_hgrn_lb_logits, m_hgrn_norm_g, m_fox_f_bias,
                           m_fox_q_norm_g, m_fox_k_norm_g, m_mem_kv_w, m_mem_q_norm_g, m_mem_k_norm_g, m_w_br_hgrn,
                           m_w_br_fox, m_w_br_mem, m_w_out, m_norm_ffn_g, m_ffn_w_up, m_ffn_conv_w, m_ffn_conv_b,
                           m_ffn_w_down)))
    v = dict(zip(WEIGHTS, (v_norm_mix_g, v_norm_mem_g, v_w_in, v_hgrn_lb_logits, v_hgrn_norm_g, v_fox_f_bias,
                           v_fox_q_norm_g, v_fox_k_norm_g, v_mem_kv_w, v_mem_q_norm_g, v_mem_k_norm_g, v_w_br_hgrn,
                           v_w_br_fox, v_w_br_mem, v_w_out, v_norm_ffn_g, v_ffn_w_up, v_ffn_conv_w, v_ffn_conv_b,
                           v_ffn_w_down)))
    c_idx = lax.axis_index("c")
    chip = 2 * lax.axis_index("x") + lax.axis_index("y")

    mine = {nm: w[nm][0].astype(MXU_DTYPE) for nm in BIG}

    def gathered_full(names, arrays):
        out = {nm: _put_shard(g, KIND[nm], chip, mine[nm]) for nm, g in zip(names, arrays)}
        return {nm: g.reshape(-1, g.shape[2]) if nm in ROW_SHARDED else g for nm, g in out.items()}

    full = gathered_full(FIRST, _gather_shards([mine[nm] for nm in FIRST], [KIND[nm] for nm in FIRST]))
    late = ([mine[nm] for nm in REST], [KIND[nm] for nm in REST], lambda arrays: gathered_full(REST, arrays))
    cs = ffn_conv_w.shape[2]
    f = cs * N_CHIPS
    placed = lax.dynamic_update_slice(jnp.zeros((3, f), F32), ffn_conv_w[0] * (c_idx == 0).astype(F32), (0, chip * cs))
    conv_w = _unpack_small(_all_sum_small(_pack_small([placed]), "gather_conv_w"), [(3, f)])[0]

    c_arr = jnp.reshape(c_idx, (1,)).astype(jnp.int32)

    def stacked(nm, g):
        return g.reshape(N_CHIPS, -1, g.shape[1]) if nm in ROW_SHARDED else g

    def with_own(landed, partial, kinds):
        return [_put_shard(bq, STACK, chip, _take_shard(p, k, chip)) for bq, p, k in zip(landed, partial, kinds)]

    kinds_rest, kinds_last = [KIND[nm] for nm in REST], [KIND[nm] for nm in LAST]
    state = {}

    def swap_rest(grads):
        gs = [stacked(nm, grads[nm]) for nm in REST]

        def then(from_sibling):
            state["partial_rest"] = [_add_half(g, a, k, c_arr, "add_half_" + nm)
                                     for g, a, k, nm in zip(gs, from_sibling, kinds_rest, REST)]

        return _swap_rider(gs, kinds_rest), then

    def exchange_rest(grads):
        def then(landed):
            state["landed_rest"] = with_own(landed, state["partial_rest"], kinds_rest)

        return _exchange_rider(state["partial_rest"], kinds_rest), then

    def exchange_last(grads):
        gs = [stacked(nm, grads[nm]) for nm in LAST]
        from_sibling = _pair_swap_halves(gs, kinds_last, "pair_swap_halves_last")
        partial = [_add_half(g, a, k, c_arr, "add_half_" + nm) for g, a, k, nm in zip(gs, from_sibling, kinds_last, LAST)]

        def then(landed):
            state["landed_last"] = with_own(landed, partial, kinds_last)

        return _exchange_rider(partial, kinds_last), then

    hooks = {"fox_bwd": swap_rest, "hgrn_bwd": exchange_rest, "in_proj_dx": exchange_last}

    loss_vec, grad_x, grads = _local_step(x, mem, loss_target, w, full, conv_w, late, hooks)

    landed = dict(zip(LAST + REST, state["landed_last"] + state["landed_rest"]))
    reduced_half = [_sum_chips(landed[nm], "sum_chips_" + nm) for nm in BIG]
    joined = [lax.dynamic_update_slice(o, q, (c_idx * q.shape[0], 0))
              for o, q in zip(_pair_join_halves(reduced_half), reduced_half)]
    gshards = dict(zip(BIG, joined))

    small_names = SMALL + ("ffn_conv_w",)
    summed = _unpack_small(
        _all_sum_small(_pack_small([grads[nm] for nm in small_names] + [loss_vec]), "all_sum_small_grads"),
        [grads[nm].shape for nm in small_names] + [loss_vec.shape])
    gsmall = dict(zip(small_names, summed[:-1]))
    loss = jnp.sum(summed[-1])
    g_out = {nm: gshards[nm][None] for nm in BIG}
    for nm in SMALL:
        g_out[nm] = gsmall[nm].reshape(w[nm].shape)
    g_out["ffn_conv_w"] = lax.dynamic_slice(gsmall["ffn_conv_w"], (0, chip * cs), (3, cs))[None]

    delta, new_m, new_v = {}, {}, {}
    for nm in BIG + ("ffn_conv_w",):
        delta[nm], new_m[nm], new_v[nm] = _adamw(w[nm], g_out[nm], m[nm], v[nm], name="adamw_" + nm)
    packed = [_pack_small([t[nm] for nm in SMALL])[None] for t in (w, g_out, m, v)]
    outs = _adamw(*packed, name="adamw_small")
    shapes = [w[nm].shape for nm in SMALL]
    for res, o in zip((delta, new_m, new_v), outs):
        res.update(zip(SMALL, _unpack_small(o[0], shapes)))

    return (loss, grad_x, *[g_out[nm] for nm in WEIGHTS], *[delta[nm] for nm in WEIGHTS],
            *[new_m[nm] for nm in WEIGHTS], *[new_v[nm] for nm in WEIGHTS])
```

```python
import functools
import math

import jax
import jax.numpy as jnp
from jax import lax
from jax.experimental import pallas as pl
from jax.experimental.pallas import tpu as pltpu

F32 = jnp.float32
BF16 = jnp.bfloat16
MXU_DTYPE = jnp.bfloat16
EXCHANGE_DTYPE = jnp.bfloat16

EPS = 1e-6
HG_HEADS, HG_D = 4, 128
FOX_HEADS, FOX_DH = 8, 64
MEM_HEADS, MEM_DH = 4, 128
HG_CHUNK = 64
FOX_BLOCK = 256
LANE = 128
FFN_GROUP = 256
FLAT_W = 1024
VMEM_LIMIT = 56 * 2 ** 20
NEG = -1e30
N_CHIPS = 4

ADAM_LR, ADAM_B1, ADAM_B2, ADAM_EPS, ADAM_WD, ADAM_STEP = 0.001, 0.9, 0.999, 1e-08, 0.01, 10

MESH = pl.DeviceIdType.MESH
ANY = pl.BlockSpec(memory_space=pl.ANY)


def _mx(x):
    return x.astype(MXU_DTYPE)


def _dot(a, b, ca, cb):
    return lax.dot_general(_mx(a), _mx(b), (((ca,), (cb,)), ((), ())), preferred_element_type=F32)


def _nn(a, b):
    return _dot(a, b, 1, 0)


def _nt(a, b):
    return _dot(a, b, 1, 1)


def _tn(a, b):
    return _dot(a, b, 0, 0)


def _dotp(a, b, ca, cb):
    return lax.dot_general(a, b, (((ca,), (cb,)), ((), ())), precision=lax.Precision.HIGHEST,
                           preferred_element_type=F32)


def _tri_dot(tri_bf, x):
    hi = x.astype(BF16)
    r = x - hi.astype(F32)
    mid = r.astype(BF16)
    lo = (r - mid.astype(F32)).astype(BF16)

    def d(v):
        return lax.dot_general(tri_bf, v, (((1,), (0,)), ((), ())), preferred_element_type=F32)

    return d(hi) + d(mid) + d(lo)


def _sig(x):
    return jax.nn.sigmoid(x)


def _erf(x):
    a = jnp.abs(x)
    t = 1.0 / (1.0 + 0.3275911 * a)
    poly = t * (0.254829592 + t * (-0.284496736 + t * (1.421413741 + t * (-1.453152027 + t * 1.061405429))))
    y = 1.0 - poly * jnp.exp(-a * a)
    return jnp.where(x < 0, -y, y)


def _tile(dim, pref, unit=LANE):
    if dim <= pref:
        return dim
    t = pref - pref % unit
    while t >= unit:
        if dim % t == 0:
            return t
        t -= unit
    return dim


def _params(n_grid):
    return pltpu.CompilerParams(dimension_semantics=("arbitrary",) * n_grid, vmem_limit_bytes=VMEM_LIMIT)


def _acc(ref, val, first):
    @pl.when(first)
    def _():
        ref[...] = val

    @pl.when(jnp.logical_not(first))
    def _():
        ref[...] += val


class _Rider:
    def __init__(self, inputs, out_shapes, scratch, start, finish):
        self.inputs, self.out_shapes, self.scratch, self.start, self.finish = inputs, out_shapes, scratch, start, finish


def _ride(body, rider, n_in, n_out, grid):
    if rider is None:
        return body
    ri, ro, rs = len(rider.inputs), len(rider.out_shapes), len(rider.scratch)

    def wrapped(*refs):
        a, b, c = n_in + ri, n_in + ri + n_out, n_in + ri + n_out + ro
        base = refs[:n_in] + refs[a:b] + refs[c:len(refs) - rs]
        r_in, r_out, r_scr = refs[n_in:a], refs[b:c], refs[len(refs) - rs:]
        step = pl.program_id(0)
        for ax in range(1, len(grid)):
            step = step * grid[ax] + pl.program_id(ax)

        @pl.when(step == 0)
        def _():
            rider.start(r_in, r_out, r_scr)

        body(*base)

        @pl.when(step == math.prod(grid) - 1)
        def _():
            rider.finish(r_in, r_out, r_scr)

    return wrapped


def _ride_call(body, rider, *, name, grid, in_specs, out_specs, out_shape, scratch, args, aliases=None):
    n_in, n_out = len(in_specs), len(out_specs)
    aliases = aliases or {}
    if rider is None:
        outs = pl.pallas_call(body, name=name, grid=grid, in_specs=in_specs, out_specs=out_specs, out_shape=out_shape,
                              scratch_shapes=scratch, input_output_aliases=aliases,
                              compiler_params=_params(len(grid)))(*args)
        return list(outs), None
    outs = pl.pallas_call(
        _ride(body, rider, n_in, n_out, grid), name=name, grid=grid,
        in_specs=list(in_specs) + [ANY] * len(rider.inputs), out_specs=list(out_specs) + [ANY] * len(rider.out_shapes),
        out_shape=list(out_shape) + list(rider.out_shapes), scratch_shapes=list(scratch) + list(rider.scratch),
        input_output_aliases=aliases, compiler_params=_params(len(grid)),
    )(*args, *rider.inputs)
    return list(outs[:n_out]), list(outs[n_out:])


def _matmul(a, b, *, name, ta=False, tb=False, tm=1024, tn=2048, tk=None, rider=None):
    m, k = (a.shape[1], a.shape[0]) if ta else a.shape
    n = b.shape[0] if tb else b.shape[1]
    tk = tk or (1024 if ta else 2048)
    tm, tn, tk = _tile(m, tm), _tile(n, tn), _tile(k, tk)
    nk = k // tk

    def body(a_ref, b_ref, o_ref):
        p = _dot(a_ref[...], b_ref[...], 0 if ta else 1, 1 if tb else 0)
        if nk == 1:
            o_ref[...] = p
        else:
            _acc(o_ref, p, pl.program_id(2) == 0)

    a_spec = pl.BlockSpec((tk, tm), lambda i, j, kk: (kk, i)) if ta else pl.BlockSpec((tm, tk), lambda i, j, kk: (i, kk))
    b_spec = pl.BlockSpec((tn, tk), lambda i, j, kk: (j, kk)) if tb else pl.BlockSpec((tk, tn), lambda i, j, kk: (kk, j))
    outs, extra = _ride_call(
        body, rider, name=name, grid=(m // tm, n // tn, nk), in_specs=[a_spec, b_spec],
        out_specs=[pl.BlockSpec((tm, tn), lambda i, j, kk: (i, j))], out_shape=[jax.ShapeDtypeStruct((m, n), F32)],
        scratch=[], args=(a, b))
    return (outs[0], extra) if rider else outs[0]


def _matmul_rows(a_parts, b, *, name, tb, row_ins, vec_ins, epilogue, n_vec_out, tm=512, tk=2048, rider=None):
    m, kp = a_parts[0].shape
    n = b.shape[0] if tb else b.shape[1]
    tm, tk = _tile(m, tm, 8), _tile(kp, tk)
    nk = kp // tk
    n_a, n_row, n_vec = len(a_parts), len(row_ins), len(vec_ins)

    def body(*refs):
        a_refs, b_refs = refs[:n_a], refs[n_a:2 * n_a]
        rows = refs[2 * n_a:2 * n_a + n_row]
        vecs = refs[2 * n_a + n_row:2 * n_a + n_row + n_vec]
        o_ref = refs[2 * n_a + n_row + n_vec]
        v_refs = refs[2 * n_a + n_row + n_vec + 1:-1]
        acc_ref = refs[-1]
        i, kk = pl.program_id(0), pl.program_id(1)
        p = _dot(a_refs[0][...], b_refs[0][...], 1, 1 if tb else 0)
        for a_ref, b_ref in zip(a_refs[1:], b_refs[1:]):
            p = p + _dot(a_ref[...], b_ref[...], 1, 1 if tb else 0)
        _acc(acc_ref, p, kk == 0)

        @pl.when(kk == nk - 1)
        def _():
            out, vouts = epilogue(acc_ref[...], *[r[...] for r in rows], *[v[...] for v in vecs])
            o_ref[...] = out
            for v_ref, v in zip(v_refs, vouts):
                _acc(v_ref, v, i == 0)

    a_spec = pl.BlockSpec((tm, tk), lambda i, kk: (i, kk))
    b_specs = [pl.BlockSpec((n, tk), functools.partial(lambda i, kk, q: (0, q * nk + kk), q=q)) if tb else
               pl.BlockSpec((tk, n), functools.partial(lambda i, kk, q: (q * nk + kk, 0), q=q)) for q in range(n_a)]
    row = pl.BlockSpec((tm, n), lambda i, kk: (i, 0))
    vec = pl.BlockSpec((1, n), lambda i, kk: (0, 0))
    outs, extra = _ride_call(
        body, rider, name=name, grid=(m // tm, nk),
        in_specs=[a_spec] * n_a + b_specs + [row] * n_row + [vec] * n_vec,
        out_specs=[row] + [vec] * n_vec_out,
        out_shape=[jax.ShapeDtypeStruct((m, n), F32)] + [jax.ShapeDtypeStruct((1, n), F32)] * n_vec_out,
        scratch=[pltpu.VMEM((tm, n), F32)], args=(*a_parts, *([b] * n_a), *row_ins, *vec_ins))
    return outs[0], outs[1:], extra


def _norm_bwd_epilogue(n_dh):
    def epilogue(dh, x, res, *rest):
        for extra in rest[:n_dh]:
            dh = dh + extra
        g = rest[n_dh]
        r = lax.rsqrt(jnp.mean(x * x, axis=-1, keepdims=True) + EPS)
        dhg = dh * g
        dx = res + r * dhg - x * (r * r * r) * jnp.mean(dhg * x, axis=-1, keepdims=True)
        return dx, [jnp.sum(dh * x * r, axis=0, keepdims=True)]

    return epilogue


def _loss_epilogue(y, x1, target):
    d = y.shape[1]
    err = x1 + y - target
    return err * (1.0 / d), [jnp.sum(err * err, axis=0, keepdims=True) * (0.5 / d)]


def _rmsnorm_fwd(x, g, *, name, tm=512):
    n, d = x.shape
    tm = _tile(n, tm, 8)

    def body(x_ref, g_ref, o_ref):
        xv = x_ref[...]
        r = lax.rsqrt(jnp.mean(xv * xv, axis=-1, keepdims=True) + EPS)
        o_ref[...] = (xv * r * g_ref[...]).astype(o_ref.dtype)

    return pl.pallas_call(
        body, name=name, grid=(n // tm,),
        in_specs=[pl.BlockSpec((tm, d), lambda i: (i, 0)), pl.BlockSpec((1, d), lambda i: (0, 0))],
        out_specs=pl.BlockSpec((tm, d), lambda i: (i, 0)),
        out_shape=jax.ShapeDtypeStruct((n, d), MXU_DTYPE),
        compiler_params=_params(1),
    )(x, g)


def _rmsnorm_bwd(x, dhs, g, res, *, name, tm=512):
    n, d = x.shape
    tm = _tile(n, tm, 8)
    n_dh = len(dhs)
    has_res = res is not None

    def body(*refs):
        x_ref, dh_refs, g_ref = refs[0], refs[1:1 + n_dh], refs[1 + n_dh]
        res_ref = refs[2 + n_dh] if has_res else None
        dx_ref, dg_ref = refs[-2], refs[-1]
        xv = x_ref[...]
        dh = dh_refs[0][...].astype(F32)
        for r_ in dh_refs[1:]:
            dh = dh + r_[...].astype(F32)
        r = lax.rsqrt(jnp.mean(xv * xv, axis=-1, keepdims=True) + EPS)
        dhg = dh * g_ref[...]
        dx = r * dhg - xv * (r * r * r) * jnp.mean(dhg * xv, axis=-1, keepdims=True)
        if has_res:
            dx = dx + res_ref[...]
        dx_ref[...] = dx
        _acc(dg_ref, jnp.sum(dh * xv * r, axis=0, keepdims=True), pl.program_id(0) == 0)

    row = pl.BlockSpec((tm, d), lambda i: (i, 0))
    vec = pl.BlockSpec((1, d), lambda i: (0, 0))
    ins = [x] + list(dhs) + [g] + ([res] if has_res else [])
    return pl.pallas_call(
        body, name=name, grid=(n // tm,),
        in_specs=[row] * (1 + n_dh) + [vec] + ([row] if has_res else []),
        out_specs=[row, vec],
        out_shape=[jax.ShapeDtypeStruct((n, d), F32), jax.ShapeDtypeStruct((1, d), F32)],
        compiler_params=_params(1),
    )(*ins)


def _adamw(w, g, m, v, *, name, tr=256):
    _, r, c = w.shape
    c1 = 1.0 / (1.0 - ADAM_B1 ** ADAM_STEP)
    c2 = 1.0 / (1.0 - ADAM_B2 ** ADAM_STEP)

    def body(w_ref, g_ref, m_ref, v_ref, d_ref, mo_ref, vo_ref):
        gv = g_ref[...]
        mn = ADAM_B1 * m_ref[...] + (1.0 - ADAM_B1) * gv
        vn = ADAM_B2 * v_ref[...] + (1.0 - ADAM_B2) * (gv * gv)
        d_ref[...] = -ADAM_LR * ((mn * c1) / (jnp.sqrt(vn * c2) + ADAM_EPS) + ADAM_WD * w_ref[...])
        mo_ref[...] = mn
        vo_ref[...] = vn

    if r % 8 == 0 or r < 8:
        tr = _tile(r, tr, 8)
        grid, blk = (r // tr,), pl.BlockSpec((1, tr, c), lambda i: (0, i, 0))
    else:
        tc = _tile(c, tr)
        grid, blk = (c // tc,), pl.BlockSpec((1, r, tc), lambda i: (0, 0, i))
    sds = jax.ShapeDtypeStruct((1, r, c), F32)
    return pl.pallas_call(
        body, name=name, grid=grid, in_specs=[blk] * 4, out_specs=[blk] * 3, out_shape=[sds] * 3,
        compiler_params=_params(1),
    )(w, g, m, v)


def _bdot(a, b, ca, cb):
    return lax.dot_general(_mx(a), _mx(b), (((ca,), (cb,)), ((0,), (0,))), preferred_element_type=F32)


def _bdotp(a, b, ca, cb):
    return lax.dot_general(a, b, (((ca,), (cb,)), ((0,), (0,))), precision=lax.Precision.HIGHEST,
                           preferred_element_type=F32)


def _tri_dot_b(tri_bf, x):
    hi = x.astype(BF16)
    r = x - hi.astype(F32)
    mid = r.astype(BF16)
    lo = (r - mid.astype(F32)).astype(BF16)

    def d(v):
        return lax.dot_general(tri_bf, v, (((2,), (1,)), ((0,), (0,))), preferred_element_type=F32)

    return d(hi) + d(mid) + d(lo)


def _hgrn_forward(hq, hf, hi, lbv, tril, tril_bf):
    nc, c, _ = hq.shape
    sf = _sig(hf)
    f = lbv + (1.0 - lbv) * sf
    k = 1.0 - f
    gcum = _tri_dot_b(tril_bf, jnp.log(f))
    mid = gcum[:, c // 2 - 1:c // 2, :]
    glast = gcum[:, c - 1:c, :]
    sq = _sig(hq)
    q = hq * sq
    e_q = jnp.exp(gcum - mid)
    e_k = jnp.exp(mid - gcum)
    qe, ke = q * e_q, k * e_k
    a = jnp.where(tril, _bdot(qe, ke, 2, 2), 0.0)
    e_g = jnp.exp(gcum)
    qg = q * e_g
    e_s = jnp.exp(glast - gcum)
    kg = k * e_s
    e_l = jnp.exp(glast)
    upd = _bdot(hi, kg, 1, 1)
    st = jnp.zeros((HG_D, HG_D), F32)
    states = []
    for n in range(nc):
        states.append(st)
        st = st * e_l[n] + upd[n]
    st_all = jnp.stack(states)
    o = _bdot(a, hi, 2, 1) + _bdot(qg, st_all, 2, 2)
    return dict(sf=sf, f=f, k=k, sq=sq, q=q, e_q=e_q, e_k=e_k, qe=qe, ke=ke, a=a, e_g=e_g, qg=qg, o=o,
                e_s=e_s, kg=kg, e_l=e_l, st_all=st_all)


def _hgrn_specs(t_, col0):
    def col(off):
        return pl.BlockSpec((1, t_, LANE), lambda h, b: (b, 0, col0 + 4 * h + off))

    vec = pl.BlockSpec((2, LANE), lambda h, b: (0, h))
    one = pl.BlockSpec((1, LANE), lambda h, b: (0, 0))
    blk = pl.BlockSpec((1, t_, LANE), lambda h, b: (b, 0, h))
    return col, vec, one, blk


def _chunk_masks(nc, c):
    row = lax.broadcasted_iota(jnp.int32, (nc, c, c), 1)
    cl = lax.broadcasted_iota(jnp.int32, (nc, c, c), 2)
    return row >= cl, (row >= cl).astype(BF16), (row <= cl).astype(BF16)


def _hgrn_fwd(zm, lb, gn, hw, col0):
    b_, t_, _ = zm.shape
    c = min(HG_CHUNK, t_)
    nc = t_ // c
    col, vec, one, blk = _hgrn_specs(t_, col0)

    def body(q_ref, f_ref, i_ref, g_ref, lb_ref, gn_ref, y_ref):
        lbv, gnv = _sig(lb_ref[0:1, :] - lb_ref[1:2, :]), gn_ref[...]
        tril, tril_bf, _ = _chunk_masks(nc, c)
        chunks = lambda ref: ref[0].reshape(nc, c, LANE)
        o = _hgrn_forward(chunks(q_ref), chunks(f_ref), chunks(i_ref), lbv, tril, tril_bf)["o"]
        r = lax.rsqrt(jnp.mean(o * o, axis=-1, keepdims=True) + EPS)
        hg = chunks(g_ref)
        y_ref[0] = (o * r * gnv * (hg * _sig(hg))).reshape(t_, LANE)

    return pl.pallas_call(
        body, name="hgrn_fwd", grid=(HG_HEADS, b_),
        in_specs=[col(0), col(1), col(2), col(3), vec, one], out_specs=blk,
        out_shape=jax.ShapeDtypeStruct((b_, t_, hw), F32),
        compiler_params=_params(2),
    )(zm, zm, zm, zm, lb, gn)


def _hgrn_bwd(zm, dy, lb, gn, hw, col0, dz, rider=None):
    b_, t_, _ = zm.shape
    c = min(HG_CHUNK, t_)
    nc = t_ // c
    col, vec, one, blk = _hgrn_specs(t_, col0)

    def body(q_ref, f_ref, i_ref, g_ref, dy_ref, lb_ref, gn_ref, _, dz_ref, dlb_ref, dgn_ref):
        h, b = pl.program_id(0), pl.program_id(1)
        lbv, gnv = _sig(lb_ref[0:1, :] - lb_ref[1:2, :]), gn_ref[...]
        tril, tril_bf, triu_bf = _chunk_masks(nc, c)
        last_row = lax.broadcasted_iota(jnp.int32, (nc, c, LANE), 1) == c - 1
        chunks = lambda ref: ref[0].reshape(nc, c, LANE)
        flat = lambda x: x.reshape(t_, LANE)
        hq, hi, hg = chunks(q_ref), chunks(i_ref), chunks(g_ref)
        p = _hgrn_forward(hq, chunks(f_ref), hi, lbv, tril, tril_bf)
        o, q, k, st_all, e_l = p["o"], p["q"], p["k"], p["st_all"], p["e_l"]
        dyv = chunks(dy_ref)
        sg = _sig(hg)
        r = lax.rsqrt(jnp.mean(o * o, axis=-1, keepdims=True) + EPS)
        dn = dyv * (hg * sg)
        dz_ref[0, :, 3 * LANE:] = flat(dyv * (o * r * gnv) * (sg * (1.0 + hg * (1.0 - sg)))).astype(dz_ref.dtype)
        dgn = jnp.sum(flat(dn * o * r), axis=0, keepdims=True)
        dng = dn * gnv
        do = r * dng - o * (r * r * r) * jnp.mean(dng * o, axis=-1, keepdims=True)
        back = _bdotp(do, p["qg"], 1, 1)
        dst = jnp.zeros((HG_D, HG_D), F32)
        dsts = [None] * nc
        for n in range(nc - 1, -1, -1):
            dsts[n] = dst
            dst = dst * e_l[n] + back[n]
        dst_all = jnp.stack(dsts)
        da = jnp.where(tril, _bdotp(do, hi, 2, 2), 0.0)
        dq = _bdotp(da, p["ke"], 2, 1) * p["e_q"] + _bdotp(do, st_all, 2, 1) * p["e_g"]
        dk_state = _bdotp(hi, dst_all, 2, 1) * p["e_s"]
        dk = _bdotp(da, p["qe"], 1, 1) * p["e_k"] + dk_state
        dz_ref[0, :, 2 * LANE:3 * LANE] = flat(_bdot(p["a"], do, 1, 1) + _bdot(p["kg"], dst_all, 2, 2)).astype(dz_ref.dtype)
        extra = (jnp.sum(k * dk_state, axis=1, keepdims=True) + e_l * jnp.sum(st_all * dst_all, axis=1, keepdims=True))
        dgc = q * dq - k * dk + jnp.where(last_row, extra, 0.0)
        dfv = _tri_dot_b(triu_bf, dgc) / p["f"] - dk
        sf, sq = p["sf"], p["sq"]
        dz_ref[0, :, LANE:2 * LANE] = flat(dfv * (1.0 - lbv) * sf * (1.0 - sf)).astype(dz_ref.dtype)
        dlb = jnp.sum(flat(dfv * (1.0 - sf)), axis=0, keepdims=True)
        dz_ref[0, :, :LANE] = flat(dq * (sq * (1.0 + hq * (1.0 - sq)))).astype(dz_ref.dtype)
        dl0 = dlb * lbv * (1.0 - lbv)
        _acc(dlb_ref, jnp.concatenate([dl0, -dl0], axis=0), b == 0)
        _acc(dgn_ref, dgn, jnp.logical_and(b == 0, h == 0))

    return _ride_call(
        body, rider, name="hgrn_bwd", grid=(HG_HEADS, b_),
        in_specs=[col(0), col(1), col(2), col(3), blk, vec, one, ANY],
        out_specs=[pl.BlockSpec((1, t_, 4 * LANE), lambda h, b: (b, 0, col0 // 4 + h)), vec, one],
        out_shape=[jax.ShapeDtypeStruct(dz.shape, dz.dtype), jax.ShapeDtypeStruct((2, hw), F32),
                   jax.ShapeDtypeStruct((1, LANE), F32)],
        scratch=[], args=(zm, zm, zm, zm, dy, lb, gn, dz), aliases={7: 0})


def _fox_logf(x):
    return jnp.minimum(x, 0.0) - jnp.log(1.0 + jnp.exp(-jnp.abs(x)))


def _fox_prep(zf, bias):
    b_, t_, _ = zf.shape
    tb = min(FOX_BLOCK, t_)
    nb = t_ // tb

    def body(z_ref, b_ref, fc_ref):
        tril_bf = (lax.broadcasted_iota(jnp.int32, (tb, tb), 0) >= lax.broadcasted_iota(jnp.int32, (tb, tb), 1)).astype(BF16)
        bv = b_ref[...]

        def blk(i, carry):
            rows = pl.ds(pl.multiple_of(i * tb, tb), tb)
            fc = _tri_dot(tril_bf, _fox_logf(z_ref[0, rows, :] + bv)) + carry
            fc_ref[0, rows, :] = fc
            return fc[tb - 1:tb, :]

        lax.fori_loop(0, nb, blk, jnp.zeros((1, LANE), F32))

    blk_spec = pl.BlockSpec((1, t_, LANE), lambda b: (b, 0, 0))
    return pl.pallas_call(
        body, name="fox_prep", grid=(b_,),
        in_specs=[blk_spec, pl.BlockSpec((1, LANE), lambda b: (0, 0))], out_specs=blk_spec,
        out_shape=jax.ShapeDtypeStruct((b_, t_, LANE), F32), compiler_params=_params(1),
    )(zf, bias)


def _fox_post(dfc, zf, bias):
    b_, t_, _ = zf.shape
    npair = dfc.shape[1]
    tb = min(FOX_BLOCK, t_)
    nb = t_ // tb

    def body(d_ref, z_ref, b_ref, dz_ref, db_ref):
        triu_bf = (lax.broadcasted_iota(jnp.int32, (tb, tb), 0) <= lax.broadcasted_iota(jnp.int32, (tb, tb), 1)).astype(BF16)
        valid = lax.broadcasted_iota(jnp.int32, (tb, LANE), 1) < FOX_HEADS
        bv = b_ref[...]

        def blk(m, carry):
            tail, db = carry
            rows = pl.ds(pl.multiple_of((nb - 1 - m) * tb, tb), tb)
            dfc_rows = d_ref[0, 0, rows, :]
            for p in range(1, npair):
                dfc_rows = dfc_rows + pltpu.roll(d_ref[0, p, rows, :], 2 * p, 1)
            dlf = _tri_dot(triu_bf, dfc_rows) + tail
            dx = jnp.where(valid, dlf * _sig(-(z_ref[0, rows, :] + bv)), 0.0)
            dz_ref[0, rows, :] = dx.astype(dz_ref.dtype)
            return dlf[0:1, :], db + jnp.sum(dx, axis=0, keepdims=True)

        z1 = jnp.zeros((1, LANE), F32)
        _, db = lax.fori_loop(0, nb, blk, (z1, z1))
        _acc(db_ref, db, pl.program_id(0) == 0)

    blk_spec = pl.BlockSpec((1, t_, LANE), lambda b: (b, 0, 0))
    vec = pl.BlockSpec((1, LANE), lambda b: (0, 0))
    return pl.pallas_call(
        body, name="fox_post", grid=(b_,),
        in_specs=[pl.BlockSpec((1, npair, t_, LANE), lambda b: (b, 0, 0, 0)), blk_spec, vec], out_specs=[blk_spec, vec],
        out_shape=[jax.ShapeDtypeStruct((b_, t_, LANE), MXU_DTYPE), jax.ShapeDtypeStruct((1, LANE), F32)],
        compiler_params=_params(1),
    )(dfc, zf, bias)


FOX_TILE = 128
FOX_BAND = 512
AUG = 64


def _head_mean_matrix():
    r = lax.broadcasted_iota(jnp.int32, (LANE, LANE), 0) // FOX_DH
    c = lax.broadcasted_iota(jnp.int32, (LANE, LANE), 1) // FOX_DH
    return (r == c).astype(BF16)


def _dot_right_exact(x, m_bf):
    hi = x.astype(BF16)
    r = x - hi.astype(F32)
    mid = r.astype(BF16)
    lo = (r - mid.astype(F32)).astype(BF16)

    def d(v):
        return lax.dot_general(v, m_bf, (((1,), (0,)), ((), ())), preferred_element_type=F32)

    return d(hi) + d(mid) + d(lo)


def _pair_norm(x, g2, bd):
    r = lax.rsqrt(_dot_right_exact(x * x, bd) * (1.0 / FOX_DH) + EPS)
    return x * r * g2, r


def _pair_norm_bwd(x, r, dy, g2, bd):
    dyg = dy * g2
    dx = r * dyg - x * (r * r * r) * (_dot_right_exact(dyg * x, bd) * (1.0 / FOX_DH))
    return dx, jnp.sum(dy * x * r, axis=0, keepdims=True)


def _head_lanes(xn, hh):
    return xn if hh == 0 else pltpu.roll(xn, FOX_DH, 1)


def _split3(x):
    hi = x.astype(BF16).astype(F32)
    mid = (x - hi).astype(BF16).astype(F32)
    return hi, mid, x - hi - mid


def _fox_operands(q_ref, k_ref, v_ref, fc_ref, gq2, gk2, p, qa, ka, va):
    t_ = q_ref.shape[1]
    bd = _head_mean_matrix()
    lane = lax.broadcasted_iota(jnp.int32, (t_, LANE), 1)
    qx, kx = q_ref[0], k_ref[0]
    qn, rq = _pair_norm(qx, gq2, bd)
    kn, rk = _pair_norm(kx, gk2, bd)
    vv = v_ref[0]
    q_aug = jnp.where(jnp.logical_and(lane >= AUG, lane < AUG + 3), 1.0, 0.0)
    for hh in range(2):
        fcol = jnp.sum(jnp.where(lane == 2 * p + hh, fc_ref[0], 0.0), axis=-1, keepdims=True)
        hi, mid, lo = _split3(-fcol)
        k_aug = jnp.where(lane == AUG, hi, jnp.where(lane == AUG + 1, mid, jnp.where(lane == AUG + 2, lo,
                          jnp.where(lane == AUG + 3, 1.0, 0.0))))
        head = lane < FOX_DH
        qa[hh] = jnp.where(head, _head_lanes(qn, hh), q_aug).astype(MXU_DTYPE)
        ka[hh] = jnp.where(head, _head_lanes(kn, hh), k_aug).astype(MXU_DTYPE)
        va[hh] = jnp.where(head, _head_lanes(vv, hh), 0.0).astype(MXU_DTYPE)
    return bd, lane, qx, kx, rq, rk


def _fox_specs(t_, fw, col0):
    npair = fw // LANE

    def col(off):
        return pl.BlockSpec((1, t_, LANE), lambda b, p: (b, 0, col0 + 3 * p + off))

    pair = pl.BlockSpec((1, t_, LANE), lambda b, p: (b, 0, p))
    full = pl.BlockSpec((1, t_, LANE), lambda b, p: (b, 0, 0))
    gvec = pl.BlockSpec((1, LANE), lambda b, p: (0, 0))
    lse = pl.BlockSpec((1, 1, t_, LANE), lambda b, p: (b, p, 0, 0))
    return col, pair, full, gvec, lse


def _fox_fwd(zm, fc, gq2, gk2, fw, col0):
    b_, t_, _ = zm.shape
    npair = fw // LANE
    tq = min(FOX_TILE, t_)
    bw = min(FOX_BAND, t_)
    nband, tpb = t_ // bw, bw // tq
    scale = FOX_DH ** -0.5
    col, pair, full, gvec, lse_spec = _fox_specs(t_, fw, col0)

    def body(q_ref, k_ref, v_ref, fc_ref, gq_ref, gk_ref, o_ref, lse_ref, qa, ka, va):
        p = pl.program_id(1)
        _fox_operands(q_ref, k_ref, v_ref, fc_ref, gq_ref[...] * scale, gk_ref[...], p, qa, ka, va)
        ri = lax.broadcasted_iota(jnp.int32, (tq, bw), 0)
        ci = lax.broadcasted_iota(jnp.int32, (tq, bw), 1)
        lane = lax.broadcasted_iota(jnp.int32, (tq, LANE), 1)

        for band in range(nband):
            c0 = band * bw

            def qtile(ii, _, c0=c0):
                r0 = pl.multiple_of(c0 + ii * tq, tq)
                rows = pl.ds(r0, tq)
                keep = c0 + ci <= r0 + ri
                res = []
                for hh in range(2):
                    qb = qa[hh, rows, :]
                    s_b = jnp.where(keep, _nt(qb, ka[hh, c0:c0 + bw, :]), NEG)
                    m = jnp.max(s_b, axis=-1, keepdims=True)
                    if c0:
                        s_a = _nt(qb, ka[hh, 0:c0, :])
                        m = jnp.maximum(m, jnp.max(s_a, axis=-1, keepdims=True))
                    p_b = jnp.exp(s_b - m)
                    l = jnp.sum(p_b, axis=-1, keepdims=True)
                    acc = _nn(p_b, va[hh, c0:c0 + bw, :])
                    if c0:
                        p_a = jnp.exp(s_a - m)
                        l = l + jnp.sum(p_a, axis=-1, keepdims=True)
                        acc = acc + _nn(p_a, va[hh, 0:c0, :])
                    res.append((acc / l, m + jnp.log(l)))
                (o0, e0), (o1, e1) = res
                o_ref[0, rows, :] = jnp.where(lane < FOX_DH, o0, pltpu.roll(o1, FOX_DH, 1))
                lse_ref[0, 0, rows, :] = jnp.where(lane == 0, e0, jnp.where(lane == 1, e1, 0.0))
                return 0

            lax.fori_loop(0, tpb, qtile, 0)

    return pl.pallas_call(
        body, name="fox_fwd", grid=(b_, npair),
        in_specs=[col(0), col(1), col(2), full, gvec, gvec],
        out_specs=[pair, lse_spec],
        out_shape=[jax.ShapeDtypeStruct((b_, t_, fw), F32), jax.ShapeDtypeStruct((b_, npair, t_, LANE), F32)],
        scratch_shapes=[pltpu.VMEM((2, t_, LANE), MXU_DTYPE)] * 3,
        compiler_params=_params(2),
    )(zm, zm, zm, fc, gq2, gk2)


def _norm_bwd(x, dy, g):
    r = lax.rsqrt(jnp.mean(x * x, axis=-1, keepdims=True) + EPS)
    dyg = dy * g
    dx = r * dyg - x * (r * r * r) * jnp.mean(dyg * x, axis=-1, keepdims=True)
    return dx, jnp.sum(dy * x * r, axis=0, keepdims=True)


def _fox_bwd(zm, o, do, lse, fc, gq2, gk2, fw, col0, dz, rider=None):
    b_, t_, _ = zm.shape
    npair = fw // LANE
    tq = min(FOX_TILE, t_)
    nb = t_ // tq
    bw = min(FOX_BAND, t_)
    nband, tpb = t_ // bw, bw // tq
    scale = FOX_DH ** -0.5
    col, pair, full, gvec, lse_spec = _fox_specs(t_, fw, col0)

    def body(q_ref, k_ref, v_ref, o_ref, do_ref, lse_ref, fc_ref, gq_ref, gk_ref, _,
             dz_ref, dfc_ref, dgq_ref, dgk_ref, qa, ka, va, da, rowv, dq_acc, dk_acc, dv_acc):
        b, p = pl.program_id(0), pl.program_id(1)
        gq2v, gk2v = gq_ref[...] * scale, gk_ref[...]
        bd, lane, qx, kx, rq, rk = _fox_operands(q_ref, k_ref, v_ref, fc_ref, gq2v, gk2v, p, qa, ka, va)
        head = lane < FOX_DH
        dov = do_ref[0]
        dsum = _dot_right_exact(dov * o_ref[0], bd)
        eye = (lax.broadcasted_iota(jnp.int32, (tq, tq), 0) == lax.broadcasted_iota(jnp.int32, (tq, tq), 1)).astype(F32)
        for hh in range(2):
            da[hh] = jnp.where(head, _head_lanes(dov, hh), 0.0).astype(MXU_DTYPE)
            for blk in range(nb):
                rs = slice(blk * tq, (blk + 1) * tq)
                rowv[2 * hh:2 * hh + 1, rs] = jnp.sum(eye * lse_ref[0, 0, rs, hh:hh + 1], axis=0, keepdims=True)
                rowv[2 * hh + 1:2 * hh + 2, rs] = jnp.sum(eye * dsum[rs, hh * FOX_DH:hh * FOX_DH + 1], axis=0, keepdims=True)
        dq_acc[...] = jnp.zeros(dq_acc.shape, F32)
        ri = lax.broadcasted_iota(jnp.int32, (tq, bw), 0)
        ci = lax.broadcasted_iota(jnp.int32, (tq, bw), 1)

        def part(hh, kb, vb, lo, hi, keep):
            qm, dm = qa[hh, lo:hi, :], da[hh, lo:hi, :]
            pt = jnp.exp(_nt(kb, qm) - rowv[2 * hh:2 * hh + 1, lo:hi])
            if keep is not None:
                pt = jnp.where(keep, pt, 0.0)
            dst = pt * (_nt(vb, dm) - rowv[2 * hh + 1:2 * hh + 2, lo:hi])
            dq_acc[hh, lo:hi, :] += _tn(dst, kb)
            return _nn(dst, qm), _nn(pt, dm)

        for band in range(nband):
            c0 = band * bw

            def kvtile(jj, _, c0=c0):
                r0 = pl.multiple_of(c0 + jj * tq, tq)
                rows = pl.ds(r0, tq)
                keep = c0 + ci >= r0 + ri
                for hh in range(2):
                    kb, vb = ka[hh, rows, :], va[hh, rows, :]
                    dk_t, dv_t = part(hh, kb, vb, c0, c0 + bw, keep)
                    if c0 + bw < t_:
                        dk_u, dv_u = part(hh, kb, vb, c0 + bw, t_, None)
                        dk_t, dv_t = dk_t + dk_u, dv_t + dv_u
                    dk_acc[hh, rows, :] = dk_t
                    dv_acc[hh, rows, :] = dv_t
                return 0

            lax.fori_loop(0, tpb, kvtile, 0)

        dq0, dq1, dk0, dk1 = dq_acc[0], dq_acc[1], dk_acc[0], dk_acc[1]
        dqn = jnp.where(head, dq0, pltpu.roll(dq1, FOX_DH, 1))
        dkn = jnp.where(head, dk0, pltpu.roll(dk1, FOX_DH, 1))
        dqx, gq_part = _pair_norm_bwd(qx, rq, dqn, gq2v, bd)
        dkx, gk_part = _pair_norm_bwd(kx, rk, dkn, gk2v, bd)
        dz_ref[0, :, :LANE] = dqx.astype(dz_ref.dtype)
        dz_ref[0, :, LANE:2 * LANE] = dkx.astype(dz_ref.dtype)
        dz_ref[0, :, 2 * LANE:] = jnp.where(head, dv_acc[0], pltpu.roll(dv_acc[1], FOX_DH, 1)).astype(dz_ref.dtype)

        def bias_grad(dqh, dkh):
            return (jnp.sum(jnp.where(lane == AUG + 3, dqh, 0.0), axis=-1, keepdims=True)
                    - jnp.sum(jnp.where(lane == AUG, dkh, 0.0), axis=-1, keepdims=True))

        dfc_ref[0, 0] = jnp.where(lane == 0, bias_grad(dq0, dk0), jnp.where(lane == 1, bias_grad(dq1, dk1), 0.0))
        first = jnp.logical_and(b == 0, p == 0)
        _acc(dgq_ref, gq_part * scale, first)
        _acc(dgk_ref, gk_part, first)

    gs = jax.ShapeDtypeStruct((1, LANE), F32)
    return _ride_call(
        body, rider, name="fox_bwd", grid=(b_, npair),
        in_specs=[col(0), col(1), col(2), pair, pair, lse_spec, full, gvec, gvec, ANY],
        out_specs=[pl.BlockSpec((1, t_, 3 * LANE), lambda b, p: (b, 0, col0 // 3 + p)), lse_spec, gvec, gvec],
        out_shape=[jax.ShapeDtypeStruct(dz.shape, dz.dtype), jax.ShapeDtypeStruct((b_, npair, t_, LANE), F32), gs, gs],
        scratch=[pltpu.VMEM((2, t_, LANE), MXU_DTYPE)] * 4
        + [pltpu.VMEM((8, t_), F32)] + [pltpu.VMEM((2, t_, LANE), F32)] * 3,
        args=(zm, zm, zm, o, do, lse, fc, gq2, gk2, dz), aliases={9: 0})


def _mem_specs(t_, m_, mw, col0):
    nh = mw // LANE
    qcol = pl.BlockSpec((1, t_, LANE), lambda b, h: (b, 0, col0 + h))
    kcol = pl.BlockSpec((1, m_, LANE), lambda b, h: (b, 0, h))
    vcol = pl.BlockSpec((1, m_, LANE), lambda b, h: (b, 0, nh + h))
    ycol = pl.BlockSpec((1, t_, LANE), lambda b, h: (b, 0, h))
    gvec = pl.BlockSpec((1, LANE), lambda b, h: (0, 0))
    return qcol, kcol, vcol, ycol, gvec


def _mem_fwd(zm, mkv, gq, gk, mw, col0):
    b_, t_, _ = zm.shape
    m_ = mkv.shape[1]
    tq = min(512, t_)
    nb = t_ // tq
    scale = MEM_DH ** -0.5
    qcol, kcol, vcol, ycol, gvec = _mem_specs(t_, m_, mw, col0)

    def body(q_ref, k_ref, v_ref, gq_ref, gk_ref, y_ref):
        gqv, gkv = gq_ref[...] * scale, gk_ref[...]
        kv = k_ref[0]
        kn = _mx(kv * lax.rsqrt(jnp.mean(kv * kv, axis=-1, keepdims=True) + EPS) * gkv)
        vv = _mx(v_ref[0])

        def blk(i, _):
            rows = pl.ds(pl.multiple_of(i * tq, tq), tq)
            qv = q_ref[0, rows, :]
            s = _nt(qv * lax.rsqrt(jnp.mean(qv * qv, axis=-1, keepdims=True) + EPS) * gqv, kn)
            e = jnp.exp(s - jnp.max(s, axis=-1, keepdims=True))
            y_ref[0, rows, :] = _nn(e / jnp.sum(e, axis=-1, keepdims=True), vv)
            return 0

        lax.fori_loop(0, nb, blk, 0)

    return pl.pallas_call(
        body, name="mem_fwd", grid=(b_, MEM_HEADS), in_specs=[qcol, kcol, vcol, gvec, gvec], out_specs=ycol,
        out_shape=jax.ShapeDtypeStruct((b_, t_, mw), F32), compiler_params=_params(2),
    )(zm, mkv, mkv, gq, gk)


def _mem_bwd(zm, mkv, dy, gq, gk, mw, col0, dz):
    b_, t_, _ = zm.shape
    m_ = mkv.shape[1]
    tq = min(512, t_)
    nb = t_ // tq
    scale = MEM_DH ** -0.5
    qcol, kcol, vcol, ycol, gvec = _mem_specs(t_, m_, mw, col0)

    def body(q_ref, k_ref, v_ref, dy_ref, gq_ref, gk_ref, _, dq_ref, dk_ref, dv_ref, dgq_ref, dgk_ref):
        gqv, gkv = gq_ref[...] * scale, gk_ref[...]
        kv = k_ref[0]
        kn = _mx(kv * lax.rsqrt(jnp.mean(kv * kv, axis=-1, keepdims=True) + EPS) * gkv)
        vv = _mx(v_ref[0])

        def blk(i, carry):
            dkn, dvv, dgq = carry
            rows = pl.ds(pl.multiple_of(i * tq, tq), tq)
            qv = q_ref[0, rows, :]
            qn = _mx(qv * lax.rsqrt(jnp.mean(qv * qv, axis=-1, keepdims=True) + EPS) * gqv)
            s = _nt(qn, kn)
            e = jnp.exp(s - jnp.max(s, axis=-1, keepdims=True))
            pm = e / jnp.sum(e, axis=-1, keepdims=True)
            dob = _mx(dy_ref[0, rows, :])
            dp = _nt(dob, vv)
            ds = pm * (dp - jnp.sum(dp * pm, axis=-1, keepdims=True))
            dqv, gq_part = _norm_bwd(qv, _nn(ds, kn), gqv)
            dq_ref[0, rows, :] = dqv.astype(dq_ref.dtype)
            return dkn + _tn(ds, qn), dvv + _tn(pm, dob), dgq + gq_part * scale

        z = jnp.zeros((m_, LANE), F32)
        dkn, dvv, dgq = lax.fori_loop(0, nb, blk, (z, z, jnp.zeros((1, LANE), F32)))
        dkv, dgk = _norm_bwd(kv, dkn, gkv)
        dk_ref[0] = dkv
        dv_ref[0] = dvv
        first = jnp.logical_and(pl.program_id(0) == 0, pl.program_id(1) == 0)
        _acc(dgq_ref, dgq, first)
        _acc(dgk_ref, dgk, first)

    kblk = pl.BlockSpec((1, m_, LANE), lambda b, h: (b, 0, h))
    gs = jax.ShapeDtypeStruct((1, LANE), F32)
    ks = jax.ShapeDtypeStruct((b_, m_, mw), F32)
    return pl.pallas_call(
        body, name="mem_bwd", grid=(b_, MEM_HEADS), in_specs=[qcol, kcol, vcol, ycol, gvec, gvec, ANY],
        out_specs=[qcol, kblk, kblk, gvec, gvec],
        out_shape=[jax.ShapeDtypeStruct(dz.shape, dz.dtype), ks, ks, gs, gs], input_output_aliases={6: 0},
        compiler_params=_params(2),
    )(zm, mkv, mkv, dy, gq, gk, dz)


def _merge_specs(tm, d, w, gcol):
    row_d = pl.BlockSpec((tm, d), lambda i: (i, 0))
    row_w = pl.BlockSpec((tm, w), lambda i: (i, 0))
    gates = [pl.BlockSpec((tm, d), functools.partial(lambda i, k: (i, gcol + k), k=k)) for k in range(3)]
    w_br = pl.BlockSpec((w, d), lambda i: (0, 0))
    w_o = pl.BlockSpec((d, d), lambda i: (0, 0))
    return row_d, row_w, gates, w_br, w_o


def _merge_fwd(x, ys, zm, w_brs, w_out, gcol, tm=256):
    n, d = x.shape
    w = ys[0].shape[1]
    tm = _tile(n, tm, 8)
    row_d, row_w, gates, w_br, w_o = _merge_specs(tm, d, w, gcol)

    def body(x_ref, ya, yb, yc, g0, g1, g2, wa, wb, wc, wo, x1_ref, mg_ref):
        mg = (_sig(g0[...]) * _nn(ya[...], wa[...]) + _sig(g1[...]) * _nn(yb[...], wb[...])
              + _sig(g2[...]) * _nn(yc[...], wc[...]))
        mg_ref[...] = mg.astype(mg_ref.dtype)
        x1_ref[...] = x_ref[...] + _nn(mg, wo[...])

    return pl.pallas_call(
        body, name="merge_fwd", grid=(n // tm,),
        in_specs=[row_d, row_w, row_w, row_w] + gates + [w_br, w_br, w_br, w_o],
        out_specs=[row_d, row_d],
        out_shape=[jax.ShapeDtypeStruct((n, d), F32), jax.ShapeDtypeStruct((n, d), MXU_DTYPE)],
        compiler_params=_params(1),
    )(x, *ys, zm, zm, zm, *w_brs, w_out)


def _merge_bwd(dx1, ys, zm, w_brs, w_out, gcol, tm=256):
    n, d = dx1.shape
    w = ys[0].shape[1]
    tm = _tile(n, tm, 8)
    row_d, row_w, gates, w_br, w_o = _merge_specs(tm, d, w, gcol)

    def body(dx_ref, ya, yb, yc, g0, g1, g2, wa, wb, wc, wo, dgl_ref, dpa, dpb, dpc, dya, dyb, dyc):
        dm = _nt(dx_ref[...], wo[...])
        for k, (y, g, wr, dp_ref, dy_ref) in enumerate(((ya, g0, wa, dpa, dya), (yb, g1, wb, dpb, dyb),
                                                        (yc, g2, wc, dpc, dyc))):
            sg = _sig(g[...])
            pr = _nn(y[...], wr[...])
            dgl_ref[:, k * d:(k + 1) * d] = (dm * pr * sg * (1.0 - sg)).astype(dgl_ref.dtype)
            dp = (dm * sg).astype(dp_ref.dtype)
            dp_ref[...] = dp
            dy_ref[...] = _nt(dp, wr[...])

    sd = jax.ShapeDtypeStruct((n, d), MXU_DTYPE)
    sw = jax.ShapeDtypeStruct((n, w), F32)
    return pl.pallas_call(
        body, name="merge_bwd", grid=(n // tm,),
        in_specs=[row_d, row_w, row_w, row_w] + gates + [w_br, w_br, w_br, w_o],
        out_specs=[pl.BlockSpec((tm, 3 * d), lambda i: (i, 0)), row_d, row_d, row_d, row_w, row_w, row_w],
        out_shape=[jax.ShapeDtypeStruct((n, zm.shape[1]), MXU_DTYPE), sd, sd, sd, sw, sw, sw],
        compiler_params=_params(1),
    )(dx1, *ys, zm, zm, zm, *w_brs, w_out)


CONV_ROWS = 256
HALO = 8


def _ext(ref, r0, t_):
    rc = min(CONV_ROWS, t_)
    a, b = max(r0 - HALO, 0), min(r0 + rc + HALO, t_)
    parts = []
    if r0 - HALO < 0:
        parts.append(jnp.zeros((HALO, ref.shape[2]), F32))
    parts.append(ref[0, a:b, :].astype(F32))
    if r0 + rc + HALO > t_:
        parts.append(jnp.zeros((HALO, ref.shape[2]), F32))
    return jnp.concatenate(parts, axis=0) if len(parts) > 1 else parts[0]


def _gelu_parts(ac):
    cdf = 0.5 * (1.0 + _erf(ac * (2.0 ** -0.5)))
    pdf = jnp.exp(-0.5 * ac * ac) * ((2.0 * math.pi) ** -0.5)
    return cdf, pdf


def _conv_taps(a_ext, cw, cb):
    return cw[0:1, :] * pltpu.roll(a_ext, 2, 0) + cw[1:2, :] * pltpu.roll(a_ext, 1, 0) + cw[2:3, :] * a_ext + cb


def _glu_specs(t_, f, g):
    gate = pl.BlockSpec((1, t_, g), lambda j, b: (b, 0, j))
    value = pl.BlockSpec((1, t_, g), lambda j, b: (b, 0, f // g + j))
    cwb = pl.BlockSpec((3, g), lambda j, b: (0, j))
    cbb = pl.BlockSpec((1, g), lambda j, b: (0, j))
    return gate, value, cwb, cbb


def _glu_fwd(u, cw, cb):
    b_, t_, f2 = u.shape
    f = f2 // 2
    g = min(FFN_GROUP, f)
    rc = min(CONV_ROWS, t_)
    gate, value, cwb, cbb = _glu_specs(t_, f, g)

    def body(a_ref, v_ref, cw_ref, cb_ref, y_ref):
        cwv, cbv = cw_ref[...], cb_ref[...]
        for r0 in range(0, t_, rc):
            ac = _conv_taps(_ext(a_ref, r0, t_), cwv, cbv)[HALO:HALO + rc]
            cdf, _ = _gelu_parts(ac)
            y_ref[0, r0:r0 + rc, :] = (ac * cdf * v_ref[0, r0:r0 + rc, :]).astype(y_ref.dtype)

    return pl.pallas_call(
        body, name="glu_fwd", grid=(f // g, b_), in_specs=[gate, value, cwb, cbb], out_specs=gate,
        out_shape=jax.ShapeDtypeStruct((b_, t_, f), MXU_DTYPE), compiler_params=_params(2),
    )(u, u, cw, cb)


def _glu_bwd(u, dy, cw, cb):
    b_, t_, f2 = u.shape
    f = f2 // 2
    g = min(FFN_GROUP, f)
    rc = min(CONV_ROWS, t_)
    ne = rc + 2 * HALO
    gate, value, cwb, cbb = _glu_specs(t_, f, g)

    def body(a_ref, v_ref, dy_ref, cw_ref, cb_ref, da_ref, dv_ref, dcw_ref, dcb_ref):
        cwv, cbv = cw_ref[...], cb_ref[...]
        dcw = [jnp.zeros((1, g), F32) for _ in range(3)]
        dcb = jnp.zeros((1, g), F32)
        for r0 in range(0, t_, rc):
            a_ext, v_ext, dy_ext = _ext(a_ref, r0, t_), _ext(v_ref, r0, t_), _ext(dy_ref, r0, t_)
            ac = _conv_taps(a_ext, cwv, cbv)
            cdf, pdf = _gelu_parts(ac)
            dac = dy_ext * v_ext * (cdf + ac * pdf)
            da = cwv[2:3, :] * dac + cwv[1:2, :] * pltpu.roll(dac, ne - 1, 0) + cwv[0:1, :] * pltpu.roll(dac, ne - 2, 0)
            mid = slice(HALO, HALO + rc)
            da_ref[0, r0:r0 + rc, :] = da[mid].astype(da_ref.dtype)
            dv_ref[0, r0:r0 + rc, :] = (dy_ext[mid] * ac[mid] * cdf[mid]).astype(dv_ref.dtype)
            dacm = dac[mid]
            dcw[0] = dcw[0] + jnp.sum(dacm * pltpu.roll(a_ext, 2, 0)[mid], axis=0, keepdims=True)
            dcw[1] = dcw[1] + jnp.sum(dacm * pltpu.roll(a_ext, 1, 0)[mid], axis=0, keepdims=True)
            dcw[2] = dcw[2] + jnp.sum(dacm * a_ext[mid], axis=0, keepdims=True)
            dcb = dcb + jnp.sum(dacm, axis=0, keepdims=True)
        first = pl.program_id(1) == 0
        _acc(dcw_ref, jnp.concatenate(dcw, axis=0), first)
        _acc(dcb_ref, dcb, first)

    sds = jax.ShapeDtypeStruct((b_, t_, f), MXU_DTYPE)
    return pl.pallas_call(
        body, name="glu_bwd", grid=(f // g, b_), in_specs=[gate, value, gate, cwb, cbb],
        out_specs=[gate, gate, cwb, cbb],
        out_shape=[sds, sds, jax.ShapeDtypeStruct((3, f), F32), jax.ShapeDtypeStruct((1, f), F32)],
        compiler_params=_params(2),
    )(u, u, dy, cw, cb)


def _place():
    x, y, c = lax.axis_index("x"), lax.axis_index("y"), lax.axis_index("c")
    chips = [(1 - x, y), (x, 1 - y), (1 - x, 1 - y)]
    return x, y, c, chips


def _remote(src, dst, send_sem, recv_sem, to):
    return pltpu.make_async_remote_copy(src_ref=src, dst_ref=dst, send_sem=send_sem, recv_sem=recv_sem,
                                        device_id=to, device_id_type=MESH)


STACK, COLS = "stack", "cols"


def _shard_ref(ref, kind, s, rows, c):
    if kind == COLS:
        cols = pl.ds(pl.multiple_of(s * c, LANE), c)
        return ref.at[:, cols] if rows is None else ref.at[rows, cols]
    return ref.at[s] if rows is None else ref.at[s, rows, :]


def _halves(c, half):
    mine = pl.ds(pl.multiple_of(c * half, 16), half)
    theirs = pl.ds(pl.multiple_of((1 - c) * half, 16), half)
    return mine, theirs


def _gather_parts(kinds):
    def first_copies(ins, outs, sems):
        x, y, c, chips = _place()
        me = 2 * x + y
        cps = []
        for i, (w_ref, o_ref, kind) in enumerate(zip(ins, outs, kinds)):
            r, cw = w_ref.shape
            mine, _ = _halves(c, r // 2)
            for j, chip in enumerate(chips):
                cps.append(_remote(w_ref.at[mine], _shard_ref(o_ref, kind, me, mine, cw), sems[0].at[6 * i + j],
                                   sems[1].at[6 * i + j], (*chip, c)))
        return cps

    def start(ins, outs, sems):
        for cp in first_copies(ins, outs, sems):
            cp.start()

    def finish(ins, outs, sems):
        x, y, c, chips = _place()
        sib = (x, y, 1 - c)
        passed = []
        for i, (w_ref, o_ref, kind) in enumerate(zip(ins, outs, kinds)):
            r, cw = w_ref.shape
            mine, _ = _halves(c, r // 2)
            for j, (px, py) in enumerate(chips):
                blk = _shard_ref(o_ref, kind, 2 * px + py, mine, cw)
                _remote(blk, blk, sems[0].at[6 * i + j], sems[1].at[6 * i + j], sib).wait_recv()
                passed.append(_remote(blk, blk, sems[0].at[6 * i + 3 + j], sems[1].at[6 * i + 3 + j], sib))
                passed[-1].start()
        for i, (w_ref, o_ref, kind) in enumerate(zip(ins, outs, kinds)):
            r, cw = w_ref.shape
            _, theirs = _halves(c, r // 2)
            for j, (px, py) in enumerate(chips):
                blk = _shard_ref(o_ref, kind, 2 * px + py, theirs, cw)
                _remote(blk, blk, sems[0].at[6 * i + 3 + j], sems[1].at[6 * i + 3 + j], sib).wait_recv()
        for cp in first_copies(ins, outs, sems) + passed:
            cp.wait_send()

    return start, finish


def _gather_shapes(shards, kinds):
    return [jax.ShapeDtypeStruct((a.shape[0], N_CHIPS * a.shape[1]) if k == COLS else (N_CHIPS,) + a.shape, a.dtype)
            for a, k in zip(shards, kinds)]


def _gather_sems(nw):
    return [pltpu.SemaphoreType.DMA((6 * nw,)), pltpu.SemaphoreType.DMA((6 * nw,))]


def _gather_shards(shards, kinds):
    nw = len(shards)
    start, finish = _gather_parts(kinds)

    def body(*refs):
        ins, outs, sems = refs[:nw], refs[nw:2 * nw], refs[2 * nw:]
        start(ins, outs, sems)
        finish(ins, outs, sems)

    return pl.pallas_call(
        body, name="gather_shards", in_specs=[ANY] * nw, out_specs=[ANY] * nw,
        out_shape=_gather_shapes(shards, kinds), scratch_shapes=_gather_sems(nw),
    )(*shards)


def _gather_rider(shards, kinds):
    start, finish = _gather_parts(kinds)
    return _Rider(list(shards), _gather_shapes(shards, kinds), _gather_sems(len(shards)), start, finish)


def _half_shape(g, kind):
    if kind == COLS:
        return (g.shape[0] // 2, g.shape[1])
    return (g.shape[0], g.shape[1] // 2, g.shape[2])


def _swap_parts(kinds):
    def copies(ins, outs, sems):
        x, y, c, _ = _place()
        cps = []
        for i, (g_ref, a_ref, kind) in enumerate(zip(ins, outs, kinds)):
            r = g_ref.shape[0] if kind == COLS else g_ref.shape[1]
            _, theirs = _halves(c, r // 2)
            src = g_ref.at[theirs] if kind == COLS else g_ref.at[:, theirs]
            cps.append(_remote(src, a_ref, sems[0].at[i], sems[1].at[i], (x, y, 1 - c)))
        return cps

    def start(ins, outs, sems):
        for cp in copies(ins, outs, sems):
            cp.start()

    def finish(ins, outs, sems):
        for cp in copies(ins, outs, sems):
            cp.wait()

    return start, finish


def _swap_shapes(gs, kinds):
    return [jax.ShapeDtypeStruct(_half_shape(g, k), g.dtype) for g, k in zip(gs, kinds)]


def _pair_swap_halves(gs, kinds, name):
    nw = len(gs)
    start, finish = _swap_parts(kinds)

    def body(*refs):
        ins, outs, sems = refs[:nw], refs[nw:2 * nw], refs[2 * nw:]
        start(ins, outs, sems)
        finish(ins, outs, sems)

    return pl.pallas_call(
        body, name=name, in_specs=[ANY] * nw, out_specs=[ANY] * nw, out_shape=_swap_shapes(gs, kinds),
        scratch_shapes=[pltpu.SemaphoreType.DMA((nw,)), pltpu.SemaphoreType.DMA((nw,))],
    )(*gs)


def _swap_rider(gs, kinds):
    start, finish = _swap_parts(kinds)
    nw = len(gs)
    return _Rider(list(gs), _swap_shapes(gs, kinds), [pltpu.SemaphoreType.DMA((nw,)), pltpu.SemaphoreType.DMA((nw,))],
                  start, finish)


def _row_tile(rows, width, itemsize=4, target=2 ** 21):
    return _tile(rows, max(8, target // (width * itemsize)), 8)


def _add_half(g, a, kind, c_idx, name):
    if kind == COLS:
        half, wd = a.shape
        tr = _row_tile(half, wd)
        nblk = half // tr
        grid = (nblk,)
        g_spec = pl.BlockSpec((tr, wd), lambda i, c_ref: (c_ref[0] * nblk + i, 0))
        a_spec = pl.BlockSpec((tr, wd), lambda i, c_ref: (i, 0))
    else:
        n, half, wd = a.shape
        tr = _row_tile(half, wd)
        nblk = half // tr
        grid = (n, nblk)
        g_spec = pl.BlockSpec((1, tr, wd), lambda s, i, c_ref: (s, c_ref[0] * nblk + i, 0))
        a_spec = pl.BlockSpec((1, tr, wd), lambda s, i, c_ref: (s, i, 0))

    def body(c_ref, g_ref, a_ref, o_ref):
        o_ref[...] = (g_ref[...] + a_ref[...]).astype(o_ref.dtype)

    return pl.pallas_call(
        body, name=name,
        grid_spec=pltpu.PrefetchScalarGridSpec(num_scalar_prefetch=1, grid=grid, in_specs=[g_spec, a_spec],
                                               out_specs=a_spec),
        out_shape=jax.ShapeDtypeStruct(a.shape, EXCHANGE_DTYPE), compiler_params=_params(len(grid)),
    )(c_idx, g, a)


def _exchange_parts(kinds):
    def copies(ins, outs, sems):
        x, y, c, chips = _place()
        me = 2 * x + y
        cps = []
        for i, (p_ref, b_ref, kind) in enumerate(zip(ins, outs, kinds)):
            cw = b_ref.shape[2]
            for j, (px, py) in enumerate(chips):
                cps.append(_remote(_shard_ref(p_ref, kind, 2 * px + py, None, cw), b_ref.at[me],
                                   sems[0].at[3 * i + j], sems[1].at[3 * i + j], (px, py, c)))
        return cps

    def start(ins, outs, sems):
        for cp in copies(ins, outs, sems):
            cp.start()

    def finish(ins, outs, sems):
        x, y, c, chips = _place()
        for i, b_ref in enumerate(outs):
            for j, (px, py) in enumerate(chips):
                blk = b_ref.at[2 * px + py]
                _remote(blk, blk, sems[0].at[3 * i + j], sems[1].at[3 * i + j], (px, py, c)).wait_recv()
        for cp in copies(ins, outs, sems):
            cp.wait_send()

    return start, finish


def _exchange_shapes(ps, kinds):
    return [jax.ShapeDtypeStruct((N_CHIPS,) + ((p.shape[0], p.shape[1] // N_CHIPS) if k == COLS else tuple(p.shape[1:])),
                                 p.dtype) for p, k in zip(ps, kinds)]


def _exchange_sems(nw):
    return [pltpu.SemaphoreType.DMA((3 * nw,)), pltpu.SemaphoreType.DMA((3 * nw,))]


def _exchange_rider(ps, kinds):
    start, finish = _exchange_parts(kinds)
    return _Rider(list(ps), _exchange_shapes(ps, kinds), _exchange_sems(len(ps)), start, finish)


def _sum_chips(bq, name):
    n, h, wd = bq.shape
    tr = _row_tile(h, wd * n)

    def body(b_ref, o_ref):
        acc = b_ref[0].astype(F32)
        for s in range(1, n):
            acc = acc + b_ref[s].astype(F32)
        o_ref[...] = acc

    return pl.pallas_call(
        body, name=name, grid=(h // tr,),
        in_specs=[pl.BlockSpec((n, tr, wd), lambda i: (0, i, 0))], out_specs=pl.BlockSpec((tr, wd), lambda i: (i, 0)),
        out_shape=jax.ShapeDtypeStruct((h, wd), F32), compiler_params=_params(1),
    )(bq)


def _pair_join_halves(qs):
    nw = len(qs)

    def body(*refs):
        ins, outs = refs[:nw], refs[nw:2 * nw]
        send_sems, recv_sems = refs[2 * nw:]
        x, y, c, _ = _place()
        sent = []
        for i, (q_ref, o_ref) in enumerate(zip(ins, outs)):
            mine, _ = _halves(c, q_ref.shape[0])
            sent.append(_remote(q_ref, o_ref.at[mine], send_sems.at[i], recv_sems.at[i], (x, y, 1 - c)))
            sent[-1].start()
        for i, (q_ref, o_ref) in enumerate(zip(ins, outs)):
            _, theirs = _halves(c, q_ref.shape[0])
            _remote(q_ref, o_ref.at[theirs], send_sems.at[i], recv_sems.at[i], (x, y, 1 - c)).wait_recv()
        for cp in sent:
            cp.wait_send()

    return pl.pallas_call(
        body, name="pair_join_halves", in_specs=[ANY] * nw, out_specs=[ANY] * nw,
        out_shape=[jax.ShapeDtypeStruct((2 * q.shape[0], q.shape[1]), q.dtype) for q in qs],
        scratch_shapes=[pltpu.SemaphoreType.DMA((nw,)), pltpu.SemaphoreType.DMA((nw,))],
    )(*qs)


def _all_sum_small(s, name):
    sr, w = s.shape

    def body(s_ref, o_ref, buf, send_sems, recv_sems):
        x, y, c, _ = _place()
        me = 4 * x + 2 * y + c
        buf[me] = s_ref[...]
        peers = []
        for k in range(1, 8):
            px = 1 - x if k & 4 else x
            py = 1 - y if k & 2 else y
            pc = 1 - c if k & 1 else c
            peers.append((px, py, pc))
        sent = [_remote(s_ref, buf.at[me], send_sems.at[k], recv_sems.at[k], peer) for k, peer in enumerate(peers)]
        for cp in sent:
            cp.start()
        for k, (px, py, pc) in enumerate(peers):
            _remote(s_ref, buf.at[4 * px + 2 * py + pc], send_sems.at[k], recv_sems.at[k], (px, py, pc)).wait_recv()
        for cp in sent:
            cp.wait_send()
        acc = buf[0]
        for d in range(1, 8):
            acc = acc + buf[d]
        o_ref[...] = acc

    vm = pl.BlockSpec(memory_space=pltpu.VMEM)
    return pl.pallas_call(
        body, name=name, in_specs=[vm], out_specs=vm, out_shape=jax.ShapeDtypeStruct((sr, w), F32),
        scratch_shapes=[pltpu.VMEM((8, sr, w), F32), pltpu.SemaphoreType.DMA((7,)), pltpu.SemaphoreType.DMA((7,))],
    )(s)


BIG = ("w_in", "mem_kv_w", "w_br_hgrn", "w_br_fox", "w_br_mem", "w_out", "ffn_w_up", "ffn_w_down")
KIND = {"w_in": STACK, "mem_kv_w": STACK, "w_br_hgrn": COLS, "w_br_fox": COLS, "w_br_mem": COLS, "w_out": STACK,
        "ffn_w_up": COLS, "ffn_w_down": STACK}
ROW_SHARDED = ("mem_kv_w", "w_out", "ffn_w_down")
FIRST = ("w_in",)
REST = tuple(nm for nm in BIG if nm not in FIRST)
LAST = ("w_in",)
TRANSPOSED = ("w_in",)


def _z_layout(d, hw, fw, mw):
    gate, npair, nh, nm = 3 * d // LANE, fw // LANE, hw // LANE, mw // LANE
    fox0, hg0 = gate, gate + 3 * npair
    o_fox, o_mem = 4 * nh, 4 * nh + 3 * npair
    order = [o_mem + nm + j for j in range(gate)]
    order += [o_fox + k * npair + p for p in range(npair) for k in range(3)]
    order += [k * nh + h for h in range(nh) for k in range(4)]
    order += [o_mem + h for h in range(nm)]
    assert fox0 % 3 == 0 and hg0 % 4 == 0
    return fox0, hg0, hg0 + 4 * nh, order


def _reorder_blocks(a, order):
    runs, start = [], 0
    for i in range(1, len(order) + 1):
        if i == len(order) or order[i] != order[i - 1] + 1:
            runs.append((order[start], order[i - 1] + 1))
            start = i
    return jnp.concatenate([a[:, lo * LANE:hi * LANE] for lo, hi in runs], axis=1)


def _put_shard(arr, kind, s, piece):
    if kind == COLS:
        return lax.dynamic_update_slice(arr, piece, (0, s * piece.shape[1]))
    return lax.dynamic_update_slice(arr, piece[None], (s, 0, 0))


def _take_shard(arr, kind, s):
    if kind == COLS:
        return lax.dynamic_slice(arr, (0, s * (arr.shape[1] // N_CHIPS)), (arr.shape[0], arr.shape[1] // N_CHIPS))
    return lax.dynamic_index_in_dim(arr, s, 0, keepdims=False)


def _w_in_pieces(cs, s1, nf):
    out = []
    for s in range(N_CHIPS):
        lo, hi = cs * s, cs * (s + 1)
        for a, b, forget in ((lo, min(hi, s1), False), (max(lo, s1), min(hi, s1 + nf), True), (max(lo, s1 + nf), hi, False)):
            if a < b:
                out.append((s, a - lo, b - lo, forget, a - s1 if forget else (a if a < s1 else a - nf)))
    return out


def _split_w_in(stacked, s1, nf):
    pieces = _w_in_pieces(stacked.shape[2], s1, nf)
    main = [stacked[s, :, a:b] for s, a, b, forget, _ in pieces if not forget]
    ff = [stacked[s, :, a:b] for s, a, b, forget, _ in pieces if forget]
    return jnp.concatenate(main, axis=1), jnp.concatenate(ff, axis=1)


def _join_w_in(g_main, g_ff, s1, nf):
    cs = (g_main.shape[1] + nf) // N_CHIPS
    shards = [[] for _ in range(N_CHIPS)]
    for s, a, b, forget, off in _w_in_pieces(cs, s1, nf):
        shards[s].append((g_ff if forget else g_main)[:, off:off + b - a])
    return jnp.stack([jnp.concatenate(p, axis=1) if len(p) > 1 else p[0] for p in shards])


SMALL = ("norm_mix_g", "norm_mem_g", "norm_ffn_g", "hgrn_lb_logits", "hgrn_norm_g", "fox_f_bias", "fox_q_norm_g",
         "fox_k_norm_g", "mem_q_norm_g", "mem_k_norm_g", "ffn_conv_b")


def _pack_small(vals):
    flats, total = [], 0
    for v in vals:
        flat = v.reshape(-1).astype(F32)
        n = -(-flat.shape[0] // FLAT_W)
        flats.append(jnp.pad(flat, (0, n * FLAT_W - flat.shape[0])))
        total += n
    if -total % 8:
        flats.append(jnp.zeros((-total % 8 * FLAT_W,), F32))
    return jnp.concatenate(flats).reshape(-1, FLAT_W)


def _unpack_small(buf, shapes):
    res, off = [], 0
    for shp in shapes:
        numel = math.prod(shp)
        n = -(-numel // FLAT_W)
        res.append(buf[off:off + n].reshape(-1)[:numel].reshape(shp))
        off += n
    return res


def _pad_lanes(v, width=LANE):
    return jnp.pad(v, ((0, 0), (0, width - v.shape[1])))


WEIGHTS = ("norm_mix_g", "norm_mem_g", "w_in", "hgrn_lb_logits", "hgrn_norm_g", "fox_f_bias", "fox_q_norm_g",
           "fox_k_norm_g", "mem_kv_w", "mem_q_norm_g", "mem_k_norm_g", "w_br_hgrn", "w_br_fox", "w_br_mem", "w_out",
           "norm_ffn_g", "ffn_w_up", "ffn_conv_w", "ffn_conv_b", "ffn_w_down")


def _local_step(x, mem, target, w, full, conv_w, late=None, hooks=None):
    b_, t_, d = x.shape
    n = b_ * t_
    hw, fw, mw = HG_HEADS * HG_D, FOX_HEADS * FOX_DH, MEM_HEADS * MEM_DH
    m_ = mem.shape[1]
    f = conv_w.shape[1]
    s1 = 4 * hw + 3 * fw
    fox_col, hg_col, mem_col, order = _z_layout(d, hw, fw, mw)
    gate_col = 0
    inverse = [order.index(j) for j in range(len(order))]

    w_main, w_ff = _split_w_in(full["w_in"], s1, FOX_HEADS)
    w_main = _reorder_blocks(w_main, order)
    w_ff = _pad_lanes(w_ff)
    f_bias = _pad_lanes(w["fox_f_bias"])
    cb = w["ffn_conv_b"]

    x2 = x.reshape(n, d)
    h = _rmsnorm_fwd(x2, w["norm_mix_g"], name="norm_mix_fwd")
    if late:
        zm, gathered = _matmul(h, w_main, name="in_proj", rider=_gather_rider(late[0], late[1]))
        full = {**full, **late[2](gathered)}
    else:
        zm = _matmul(h, w_main, name="in_proj")
    w_up = full["ffn_w_up"]
    w_brs = [full["w_br_hgrn"], full["w_br_fox"], full["w_br_mem"]]
    w_out, w_kv, w_down = full["w_out"], full["mem_kv_w"], full["ffn_w_down"]
    zf = _matmul(h, w_ff, name="in_proj_forget")
    zm3, zf3 = zm.reshape(b_, t_, -1), zf.reshape(b_, t_, LANE)
    ya = _hgrn_fwd(zm3, w["hgrn_lb_logits"], w["hgrn_norm_g"], hw, hg_col)
    fc = _fox_prep(zf3, f_bias)
    fox_gq, fox_gk = jnp.tile(w["fox_q_norm_g"], (1, 2)), jnp.tile(w["fox_k_norm_g"], (1, 2))
    yb, lse = _fox_fwd(zm3, fc, fox_gq, fox_gk, fw, fox_col)
    mem2 = mem.reshape(b_ * m_, d)
    hm = _rmsnorm_fwd(mem2, w["norm_mem_g"], name="norm_mem_fwd")
    mkv = _matmul(hm, w_kv, name="mem_kv_proj").reshape(b_, m_, 2 * mw)
    yc = _mem_fwd(zm3, mkv, w["mem_q_norm_g"], w["mem_k_norm_g"], mw, mem_col)
    ys = [ya.reshape(n, hw), yb.reshape(n, fw), yc.reshape(n, mw)]
    x1, merged = _merge_fwd(x2, ys, zm, w_brs, w_out, gate_col)
    h2 = _rmsnorm_fwd(x1, w["norm_ffn_g"], name="norm_ffn_fwd")
    u = _matmul(h2, w_up, name="ffn_up")
    u3 = u.reshape(b_, t_, 2 * f)
    yff = _glu_fwd(u3, conv_w, cb).reshape(n, f)
    dy, (loss_vec,), _ = _matmul_rows([yff], w_down, name="ffn_down_loss", tb=False, row_ins=[x1, target.reshape(n, d)],
                                      vec_ins=[], epilogue=_loss_epilogue, n_vec_out=1)

    grads = {}

    def ridden(name, call):
        if not hooks or name not in hooks:
            return call(None)[0]
        rider, then = hooks[name](grads)
        outs, extra = call(rider)
        then(extra)
        return outs

    dyff = _matmul(dy, w_down, tb=True, name="ffn_down_dx")
    grads["ffn_w_down"] = _matmul(yff, dy, ta=True, name="ffn_down_dw", tm=1408)
    du_a, du_v, grads["ffn_conv_w"], grads["ffn_conv_b"] = _glu_bwd(u3, dyff.reshape(b_, t_, f), conv_w, cb)
    du_a, du_v = du_a.reshape(n, f), du_v.reshape(n, f)
    dx1, (grads["norm_ffn_g"],), _ = _matmul_rows(
        [du_a, du_v], w_up, name="ffn_up_dx", tb=True, row_ins=[x1, dy], vec_ins=[w["norm_ffn_g"]],
        epilogue=_norm_bwd_epilogue(0), n_vec_out=1)
    grads["ffn_w_up"] = jnp.concatenate([_matmul(h2, du_a, ta=True, name="ffn_up_gate_dw"),
                                         _matmul(h2, du_v, ta=True, name="ffn_up_value_dw")], axis=1)

    dz, dpa, dpb, dpc, dya, dyb, dyc = _merge_bwd(dx1, ys, zm, w_brs, w_out, gate_col)
    dz = dz.reshape(b_, t_, -1)
    grads["w_out"] = _matmul(merged, dx1, ta=True, name="out_proj_dw")
    for nm, y_, dp_ in zip(("w_br_hgrn", "w_br_fox", "w_br_mem"), ys, (dpa, dpb, dpc)):
        grads[nm] = _matmul(y_, dp_, ta=True, name=nm + "_dw")

    dz, dmk, dmv, grads["mem_q_norm_g"], grads["mem_k_norm_g"] = _mem_bwd(
        zm3, mkv, dyc.reshape(b_, t_, mw), w["mem_q_norm_g"], w["mem_k_norm_g"], mw, mem_col, dz)
    dmkv = jnp.concatenate([dmk, dmv], axis=-1).reshape(b_ * m_, 2 * mw)
    grads["mem_kv_w"] = _matmul(hm, dmkv, ta=True, name="mem_kv_dw")
    dhm = _matmul(dmkv, w_kv, tb=True, name="mem_kv_dx")
    _, grads["norm_mem_g"] = _rmsnorm_bwd(mem2, [dhm], w["norm_mem_g"], None, name="norm_mem_bwd")

    dz, dfc, g_fq, g_fk = ridden("fox_bwd", lambda rider: _fox_bwd(
        zm3, yb, dyb.reshape(b_, t_, fw), lse, fc, fox_gq, fox_gk, fw, fox_col, dz, rider))
    grads["fox_q_norm_g"] = g_fq[:, :FOX_DH] + g_fq[:, FOX_DH:]
    grads["fox_k_norm_g"] = g_fk[:, :FOX_DH] + g_fk[:, FOX_DH:]
    dzf, g_fb = _fox_post(dfc, zf3, f_bias)
    grads["fox_f_bias"] = g_fb[:, :FOX_HEADS]

    dz, grads["hgrn_lb_logits"], grads["hgrn_norm_g"] = ridden("hgrn_bwd", lambda rider: _hgrn_bwd(
        zm3, dya.reshape(b_, t_, hw), w["hgrn_lb_logits"], w["hgrn_norm_g"], hw, hg_col, dz, rider))
    dzm = dz.reshape(n, -1)
    dzf2 = dzf.reshape(n, LANE)
    g_main = _matmul(h, dzm, ta=True, name="in_proj_dw")
    g_ff = _matmul(h, dzf2, ta=True, name="in_proj_forget_dw")
    grads["w_in"] = _join_w_in(_reorder_blocks(g_main, inverse), g_ff[:, :FOX_HEADS], s1, FOX_HEADS)

    dh_b = _matmul(dzf2, w_ff, tb=True, name="in_proj_forget_dx")

    def in_proj_dx(rider):
        dx, vecs, extra = _matmul_rows([dzm], w_main, name="in_proj_dx", tb=True, row_ins=[x2, dx1, dh_b],
                                       vec_ins=[w["norm_mix_g"]], epilogue=_norm_bwd_epilogue(1), n_vec_out=1,
                                       rider=rider)
        return [dx, vecs[0]], extra

    grad_x, grads["norm_mix_g"] = ridden("in_proj_dx", in_proj_dx)
    return loss_vec, grad_x.reshape(b_, t_, d), grads


def kernel(x, mem, norm_mix_g, norm_mem_g, w_in, hgrn_lb_logits, hgrn_norm_g, fox_f_bias, fox_q_norm_g, fox_k_norm_g, mem_kv_w, mem_q_norm_g, mem_k_norm_g, w_br_hgrn, w_br_fox, w_br_mem, w_out, norm_ffn_g, ffn_w_up, ffn_conv_w, ffn_conv_b, ffn_w_down, loss_target, m_norm_mix_g, m_norm_mem_g, m_w_in, m_hgrn_lb_logits, m_hgrn_norm_g, m_fox_f_bias, m_fox_q_norm_g, m_fox_k_norm_g, m_mem_kv_w, m_mem_q_norm_g, m_mem_k_norm_g, m_w_br_hgrn, m_w_br_fox, m_w_br_mem, m_w_out, m_norm_ffn_g, m_ffn_w_up, m_ffn_conv_w, m_ffn_conv_b, m_ffn_w_down, v_norm_mix_g, v_norm_mem_g, v_w_in, v_hgrn_lb_logits, v_hgrn_norm_g, v_fox_f_bias, v_fox_q_norm_g, v_fox_k_norm_g, v_mem_kv_w, v_mem_q_norm_g, v_mem_k_norm_g, v_w_br_hgrn, v_w_br_fox, v_w_br_mem, v_w_out, v_norm_ffn_g, v_ffn_w_up, v_ffn_conv_w, v_ffn_conv_b, v_ffn_w_down):
    w = dict(zip(WEIGHTS, (norm_mix_g, norm_mem_g, w_in, hgrn_lb_logits, hgrn_norm_g, fox_f_bias, fox_q_norm_g,
                           fox_k_norm_g, mem_kv_w, mem_q_norm_g, mem_k_norm_g, w_br_hgrn, w_br_fox, w_br_mem, w_out,
                           norm_ffn_g, ffn_w_up, ffn_conv_w, ffn_conv_b, ffn_w_down)))
    m = dict(zip(WEIGHTS, (m_norm_mix_g, m_norm_mem_g, m_w_in, m_hgrn_lb_logits, m_hgrn_norm_g, m_fox_f_bias,
                           m_fox_q_norm_g, m_fox_k_norm_g, m_mem_kv_w, m_mem_q_norm_g, m_mem_k_norm_g, m_w_br_hgrn,
                           m_w_br_fox, m_w_br_mem, m_w_out, m_norm_ffn_g, m_ffn_w_up, m_ffn_conv_w, m_ffn_conv_b,
                           m_ffn_w_down)))
    v = dict(zip(WEIGHTS, (v_norm_mix_g, v_norm_mem_g, v_w_in, v_hgrn_lb_logits, v_hgrn_norm_g, v_fox_f_bias,
                           v_fox_q_norm_g, v_fox_k_norm_g, v_mem_kv_w, v_mem_q_norm_g, v_mem_k_norm_g, v_w_br_hgrn,
                           v_w_br_fox, v_w_br_mem, v_w_out, v_norm_ffn_g, v_ffn_w_up, v_ffn_conv_w, v_ffn_conv_b,
                           v_ffn_w_down)))
    c_idx = lax.axis_index("c")
    chip = 2 * lax.axis_index("x") + lax.axis_index("y")

    mine = {nm: w[nm][0].astype(MXU_DTYPE) for nm in BIG}

    def gathered_full(names, arrays):
        out = {nm: _put_shard(g, KIND[nm], chip, mine[nm]) for nm, g in zip(names, arrays)}
        return {nm: g.reshape(-1, g.shape[2]) if nm in ROW_SHARDED else g for nm, g in out.items()}

    full = gathered_full(FIRST, _gather_shards([mine[nm] for nm in FIRST], [KIND[nm] for nm in FIRST]))
    late = ([mine[nm] for nm in REST], [KIND[nm] for nm in REST], lambda arrays: gathered_full(REST, arrays))
    cs = ffn_conv_w.shape[2]
    f = cs * N_CHIPS
    placed = lax.dynamic_update_slice(jnp.zeros((3, f), F32), ffn_conv_w[0] * (c_idx == 0).astype(F32), (0, chip * cs))
    conv_w = _unpack_small(_all_sum_small(_pack_small([placed]), "gather_conv_w"), [(3, f)])[0]

    c_arr = jnp.reshape(c_idx, (1,)).astype(jnp.int32)

    def stacked(nm, g):
        return g.reshape(N_CHIPS, -1, g.shape[1]) if nm in ROW_SHARDED else g

    def with_own(landed, partial, kinds):
        return [_put_shard(bq, STACK, chip, _take_shard(p, k, chip)) for bq, p, k in zip(landed, partial, kinds)]

    kinds_rest, kinds_last = [KIND[nm] for nm in REST], [KIND[nm] for nm in LAST]
    state = {}

    def swap_rest(grads):
        gs = [stacked(nm, grads[nm]) for nm in REST]

        def then(from_sibling):
            state["partial_rest"] = [_add_half(g, a, k, c_arr, "add_half_" + nm)
                                     for g, a, k, nm in zip(gs, from_sibling, kinds_rest, REST)]

        return _swap_rider(gs, kinds_rest), then

    def exchange_rest(grads):
        def then(landed):
            state["landed_rest"] = with_own(landed, state["partial_rest"], kinds_rest)

        return _exchange_rider(state["partial_rest"], kinds_rest), then

    def exchange_last(grads):
        gs = [stacked(nm, grads[nm]) for nm in LAST]
        from_sibling = _pair_swap_halves(gs, kinds_last, "pair_swap_halves_last")
        partial = [_add_half(g, a, k, c_arr, "add_half_" + nm) for g, a, k, nm in zip(gs, from_sibling, kinds_last, LAST)]

        def then(landed):
            state["landed_last"] = with_own(landed, partial, kinds_last)

        return _exchange_rider(partial, kinds_last), then

    hooks = {"fox_bwd": swap_rest, "hgrn_bwd": exchange_rest, "in_proj_dx": exchange_last}

    loss_vec, grad_x, grads = _local_step(x, mem, loss_target, w, full, conv_w, late, hooks)

    landed = dict(zip(LAST + REST, state["landed_last"] + state["landed_rest"]))
    reduced_half = [_sum_chips(landed[nm], "sum_chips_" + nm) for nm in BIG]
    joined = [lax.dynamic_update_slice(o, q, (c_idx * q.shape[0], 0))
              for o, q in zip(_pair_join_halves(reduced_half), reduced_half)]
    gshards = dict(zip(BIG, joined))

    small_names = SMALL + ("ffn_conv_w",)
    summed = _unpack_small(
        _all_sum_small(_pack_small([grads[nm] for nm in small_names] + [loss_vec]), "all_sum_small_grads"),
        [grads[nm].shape for nm in small_names] + [loss_vec.shape])
    gsmall = dict(zip(small_names, summed[:-1]))
    loss = jnp.sum(summed[-1])
    g_out = {nm: gshards[nm][None] for nm in BIG}
    for nm in SMALL:
        g_out[nm] = gsmall[nm].reshape(w[nm].shape)
    g_out["ffn_conv_w"] = lax.dynamic_slice(gsmall["ffn_conv_w"], (0, chip * cs), (3, cs))[None]

    delta, new_m, new_v = {}, {}, {}
    for nm in BIG + ("ffn_conv_w",):
        operands = (w[nm], g_out[nm], m[nm], v[nm])
        if nm in TRANSPOSED:
            operands = [jnp.swapaxes(a, 1, 2) for a in operands]
        outs = _adamw(*operands, name="adamw_" + nm)
        delta[nm], new_m[nm], new_v[nm] = [jnp.swapaxes(o, 1, 2) for o in outs] if nm in TRANSPOSED else outs
    packed = [_pack_small([t[nm] for nm in SMALL])[None] for t in (w, g_out, m, v)]
    outs = _adamw(*packed, name="adamw_small")
    shapes = [w[nm].shape for nm in SMALL]
    for res, o in zip((delta, new_m, new_v), outs):
        res.update(zip(SMALL, _unpack_small(o[0], shapes)))

    return (loss, grad_x, *[g_out[nm] for nm in WEIGHTS], *[delta[nm] for nm in WEIGHTS],
            *[new_m[nm] for nm in WEIGHTS], *[new_v[nm] for nm in WEIGHTS])
```

```python
import functools
import math

import jax
import jax.numpy as jnp
from jax import lax
from jax.experimental import pallas as pl
from jax.experimental.pallas import tpu as pltpu

F32 = jnp.float32
BF16 = jnp.bfloat16
MXU_DTYPE = jnp.bfloat16
EXCHANGE_DTYPE = jnp.bfloat16

EPS = 1e-6
HG_HEADS, HG_D = 4, 128
FOX_HEADS, FOX_DH = 8, 64
MEM_HEADS, MEM_DH = 4, 128
HG_CHUNK = 64
FOX_BLOCK = 256
LANE = 128
FFN_GROUP = 256
FLAT_W = 1024
VMEM_LIMIT = 56 * 2 ** 20
NEG = -1e30
N_CHIPS = 4

ADAM_LR, ADAM_B1, ADAM_B2, ADAM_EPS, ADAM_WD, ADAM_STEP = 0.001, 0.9, 0.999, 1e-08, 0.01, 10

MESH = pl.DeviceIdType.MESH
ANY = pl.BlockSpec(memory_space=pl.ANY)


def _mx(x):
    return x.astype(MXU_DTYPE)


def _dot(a, b, ca, cb):
    return lax.dot_general(_mx(a), _mx(b), (((ca,), (cb,)), ((), ())), preferred_element_type=F32)


def _nn(a, b):
    return _dot(a, b, 1, 0)


def _nt(a, b):
    return _dot(a, b, 1, 1)


def _tn(a, b):
    return _dot(a, b, 0, 0)


def _dotp(a, b, ca, cb):
    return lax.dot_general(a, b, (((ca,), (cb,)), ((), ())), precision=lax.Precision.HIGHEST,
                           preferred_element_type=F32)


def _tri_dot(tri_bf, x):
    hi = x.astype(BF16)
    r = x - hi.astype(F32)
    mid = r.astype(BF16)
    lo = (r - mid.astype(F32)).astype(BF16)

    def d(v):
        return lax.dot_general(tri_bf, v, (((1,), (0,)), ((), ())), preferred_element_type=F32)

    return d(hi) + d(mid) + d(lo)


def _sig(x):
    return jax.nn.sigmoid(x)


def _erf(x):
    a = jnp.abs(x)
    t = 1.0 / (1.0 + 0.3275911 * a)
    poly = t * (0.254829592 + t * (-0.284496736 + t * (1.421413741 + t * (-1.453152027 + t * 1.061405429))))
    y = 1.0 - poly * jnp.exp(-a * a)
    return jnp.where(x < 0, -y, y)


def _tile(dim, pref, unit=LANE):
    if dim <= pref:
        return dim
    t = pref - pref % unit
    while t >= unit:
        if dim % t == 0:
            return t
        t -= unit
    return dim


def _params(n_grid):
    return pltpu.CompilerParams(dimension_semantics=("arbitrary",) * n_grid, vmem_limit_bytes=VMEM_LIMIT)


def _acc(ref, val, first):
    @pl.when(first)
    def _():
        ref[...] = val

    @pl.when(jnp.logical_not(first))
    def _():
        ref[...] += val


class _Rider:
    def __init__(self, inputs, out_shapes, scratch, start, finish):
        self.inputs, self.out_shapes, self.scratch, self.start, self.finish = inputs, out_shapes, scratch, start, finish


def _ride(body, rider, n_in, n_out, grid):
    if rider is None:
        return body
    ri, ro, rs = len(rider.inputs), len(rider.out_shapes), len(rider.scratch)

    def wrapped(*refs):
        a, b, c = n_in + ri, n_in + ri + n_out, n_in + ri + n_out + ro
        base = refs[:n_in] + refs[a:b] + refs[c:len(refs) - rs]
        r_in, r_out, r_scr = refs[n_in:a], refs[b:c], refs[len(refs) - rs:]
        step = pl.program_id(0)
        for ax in range(1, len(grid)):
            step = step * grid[ax] + pl.program_id(ax)

        @pl.when(step == 0)
        def _():
            rider.start(r_in, r_out, r_scr)

        body(*base)

        @pl.when(step == math.prod(grid) - 1)
        def _():
            rider.finish(r_in, r_out, r_scr)

    return wrapped


def _ride_call(body, rider, *, name, grid, in_specs, out_specs, out_shape, scratch, args, aliases=None):
    n_in, n_out = len(in_specs), len(out_specs)
    aliases = aliases or {}
    if rider is None:
        outs = pl.pallas_call(body, name=name, grid=grid, in_specs=in_specs, out_specs=out_specs, out_shape=out_shape,
                              scratch_shapes=scratch, input_output_aliases=aliases,
                              compiler_params=_params(len(grid)))(*args)
        return list(outs), None
    outs = pl.pallas_call(
        _ride(body, rider, n_in, n_out, grid), name=name, grid=grid,
        in_specs=list(in_specs) + [ANY] * len(rider.inputs), out_specs=list(out_specs) + [ANY] * len(rider.out_shapes),
        out_shape=list(out_shape) + list(rider.out_shapes), scratch_shapes=list(scratch) + list(rider.scratch),
        input_output_aliases=aliases, compiler_params=_params(len(grid)),
    )(*args, *rider.inputs)
    return list(outs[:n_out]), list(outs[n_out:])


def _matmul(a, b, *, name, ta=False, tb=False, tm=1024, tn=2048, tk=None, rider=None):
    m, k = (a.shape[1], a.shape[0]) if ta else a.shape
    n = b.shape[0] if tb else b.shape[1]
    tk = tk or (1024 if ta else 2048)
    tm, tn, tk = _tile(m, tm), _tile(n, tn), _tile(k, tk)
    nk = k // tk

    def body(a_ref, b_ref, o_ref):
        p = _dot(a_ref[...], b_ref[...], 0 if ta else 1, 1 if tb else 0)
        if nk == 1:
            o_ref[...] = p
        else:
            _acc(o_ref, p, pl.program_id(2) == 0)

    a_spec = pl.BlockSpec((tk, tm), lambda i, j, kk: (kk, i)) if ta else pl.BlockSpec((tm, tk), lambda i, j, kk: (i, kk))
    b_spec = pl.BlockSpec((tn, tk), lambda i, j, kk: (j, kk)) if tb else pl.BlockSpec((tk, tn), lambda i, j, kk: (kk, j))
    outs, extra = _ride_call(
        body, rider, name=name, grid=(m // tm, n // tn, nk), in_specs=[a_spec, b_spec],
        out_specs=[pl.BlockSpec((tm, tn), lambda i, j, kk: (i, j))], out_shape=[jax.ShapeDtypeStruct((m, n), F32)],
        scratch=[], args=(a, b))
    return (outs[0], extra) if rider else outs[0]


def _matmul_rows(a_parts, b, *, name, tb, row_ins, vec_ins, epilogue, n_vec_out, tm=512, tk=2048, rider=None):
    m, kp = a_parts[0].shape
    n = b.shape[0] if tb else b.shape[1]
    tm, tk = _tile(m, tm, 8), _tile(kp, tk)
    nk = kp // tk
    n_a, n_row, n_vec = len(a_parts), len(row_ins), len(vec_ins)

    def body(*refs):
        a_refs, b_refs = refs[:n_a], refs[n_a:2 * n_a]
        rows = refs[2 * n_a:2 * n_a + n_row]
        vecs = refs[2 * n_a + n_row:2 * n_a + n_row + n_vec]
        o_ref = refs[2 * n_a + n_row + n_vec]
        v_refs = refs[2 * n_a + n_row + n_vec + 1:-1]
        acc_ref = refs[-1]
        i, kk = pl.program_id(0), pl.program_id(1)
        p = _dot(a_refs[0][...], b_refs[0][...], 1, 1 if tb else 0)
        for a_ref, b_ref in zip(a_refs[1:], b_refs[1:]):
            p = p + _dot(a_ref[...], b_ref[...], 1, 1 if tb else 0)
        _acc(acc_ref, p, kk == 0)

        @pl.when(kk == nk - 1)
        def _():
            out, vouts = epilogue(acc_ref[...], *[r[...] for r in rows], *[v[...] for v in vecs])
            o_ref[...] = out
            for v_ref, v in zip(v_refs, vouts):
                _acc(v_ref, v, i == 0)

    a_spec = pl.BlockSpec((tm, tk), lambda i, kk: (i, kk))
    b_specs = [pl.BlockSpec((n, tk), functools.partial(lambda i, kk, q: (0, q * nk + kk), q=q)) if tb else
               pl.BlockSpec((tk, n), functools.partial(lambda i, kk, q: (q * nk + kk, 0), q=q)) for q in range(n_a)]
    row = pl.BlockSpec((tm, n), lambda i, kk: (i, 0))
    vec = pl.BlockSpec((1, n), lambda i, kk: (0, 0))
    outs, extra = _ride_call(
        body, rider, name=name, grid=(m // tm, nk),
        in_specs=[a_spec] * n_a + b_specs + [row] * n_row + [vec] * n_vec,
        out_specs=[row] + [vec] * n_vec_out,
        out_shape=[jax.ShapeDtypeStruct((m, n), F32)] + [jax.ShapeDtypeStruct((1, n), F32)] * n_vec_out,
        scratch=[pltpu.VMEM((tm, n), F32)], args=(*a_parts, *([b] * n_a), *row_ins, *vec_ins))
    return outs[0], outs[1:], extra


def _norm_bwd_epilogue(n_dh):
    def epilogue(dh, x, res, *rest):
        for extra in rest[:n_dh]:
            dh = dh + extra
        g = rest[n_dh]
        r = lax.rsqrt(jnp.mean(x * x, axis=-1, keepdims=True) + EPS)
        dhg = dh * g
        dx = res + r * dhg - x * (r * r * r) * jnp.mean(dhg * x, axis=-1, keepdims=True)
        return dx, [jnp.sum(dh * x * r, axis=0, keepdims=True)]

    return epilogue


def _loss_epilogue(y, x1, target):
    d = y.shape[1]
    err = x1 + y - target
    return err * (1.0 / d), [jnp.sum(err * err, axis=0, keepdims=True) * (0.5 / d)]


def _rmsnorm_fwd(x, g, *, name, tm=512):
    n, d = x.shape
    tm = _tile(n, tm, 8)

    def body(x_ref, g_ref, o_ref):
        xv = x_ref[...]
        r = lax.rsqrt(jnp.mean(xv * xv, axis=-1, keepdims=True) + EPS)
        o_ref[...] = (xv * r * g_ref[...]).astype(o_ref.dtype)

    return pl.pallas_call(
        body, name=name, grid=(n // tm,),
        in_specs=[pl.BlockSpec((tm, d), lambda i: (i, 0)), pl.BlockSpec((1, d), lambda i: (0, 0))],
        out_specs=pl.BlockSpec((tm, d), lambda i: (i, 0)),
        out_shape=jax.ShapeDtypeStruct((n, d), MXU_DTYPE),
        compiler_params=_params(1),
    )(x, g)


def _rmsnorm_bwd(x, dhs, g, res, *, name, tm=512):
    n, d = x.shape
    tm = _tile(n, tm, 8)
    n_dh = len(dhs)
    has_res = res is not None

    def body(*refs):
        x_ref, dh_refs, g_ref = refs[0], refs[1:1 + n_dh], refs[1 + n_dh]
        res_ref = refs[2 + n_dh] if has_res else None
        dx_ref, dg_ref = refs[-2], refs[-1]
        xv = x_ref[...]
        dh = dh_refs[0][...].astype(F32)
        for r_ in dh_refs[1:]:
            dh = dh + r_[...].astype(F32)
        r = lax.rsqrt(jnp.mean(xv * xv, axis=-1, keepdims=True) + EPS)
        dhg = dh * g_ref[...]
        dx = r * dhg - xv * (r * r * r) * jnp.mean(dhg * xv, axis=-1, keepdims=True)
        if has_res:
            dx = dx + res_ref[...]
        dx_ref[...] = dx
        _acc(dg_ref, jnp.sum(dh * xv * r, axis=0, keepdims=True), pl.program_id(0) == 0)

    row = pl.BlockSpec((tm, d), lambda i: (i, 0))
    vec = pl.BlockSpec((1, d), lambda i: (0, 0))
    ins = [x] + list(dhs) + [g] + ([res] if has_res else [])
    return pl.pallas_call(
        body, name=name, grid=(n // tm,),
        in_specs=[row] * (1 + n_dh) + [vec] + ([row] if has_res else []),
        out_specs=[row, vec],
        out_shape=[jax.ShapeDtypeStruct((n, d), F32), jax.ShapeDtypeStruct((1, d), F32)],
        compiler_params=_params(1),
    )(*ins)


def _adamw(w, g, m, v, *, name, tr=256):
    _, r, c = w.shape
    c1 = 1.0 / (1.0 - ADAM_B1 ** ADAM_STEP)
    c2 = 1.0 / (1.0 - ADAM_B2 ** ADAM_STEP)

    def body(w_ref, g_ref, m_ref, v_ref, d_ref, mo_ref, vo_ref):
        gv = g_ref[...]
        mn = ADAM_B1 * m_ref[...] + (1.0 - ADAM_B1) * gv
        vn = ADAM_B2 * v_ref[...] + (1.0 - ADAM_B2) * (gv * gv)
        d_ref[...] = -ADAM_LR * ((mn * c1) / (jnp.sqrt(vn * c2) + ADAM_EPS) + ADAM_WD * w_ref[...])
        mo_ref[...] = mn
        vo_ref[...] = vn

    if r % 8 == 0 or r < 8:
        tr = _tile(r, tr, 8)
        grid, blk = (r // tr,), pl.BlockSpec((1, tr, c), lambda i: (0, i, 0))
    else:
        tc = _tile(c, tr)
        grid, blk = (c // tc,), pl.BlockSpec((1, r, tc), lambda i: (0, 0, i))
    sds = jax.ShapeDtypeStruct((1, r, c), F32)
    return pl.pallas_call(
        body, name=name, grid=grid, in_specs=[blk] * 4, out_specs=[blk] * 3, out_shape=[sds] * 3,
        compiler_params=_params(1),
    )(w, g, m, v)


def _bdot(a, b, ca, cb):
    return lax.dot_general(_mx(a), _mx(b), (((ca,), (cb,)), ((0,), (0,))), preferred_element_type=F32)


def _bdotp(a, b, ca, cb):
    return lax.dot_general(a, b, (((ca,), (cb,)), ((0,), (0,))), precision=lax.Precision.HIGHEST,
                           preferred_element_type=F32)


def _tri_dot_b(tri_bf, x):
    hi = x.astype(BF16)
    r = x - hi.astype(F32)
    mid = r.astype(BF16)
    lo = (r - mid.astype(F32)).astype(BF16)

    def d(v):
        return lax.dot_general(tri_bf, v, (((2,), (1,)), ((0,), (0,))), preferred_element_type=F32)

    return d(hi) + d(mid) + d(lo)


def _hgrn_forward(hq, hf, hi, lbv, tril, tril_bf):
    nc, c, _ = hq.shape
    sf = _sig(hf)
    f = lbv + (1.0 - lbv) * sf
    k = 1.0 - f
    gcum = _tri_dot_b(tril_bf, jnp.log(f))
    mid = gcum[:, c // 2 - 1:c // 2, :]
    glast = gcum[:, c - 1:c, :]
    sq = _sig(hq)
    q = hq * sq
    e_q = jnp.exp(gcum - mid)
    e_k = jnp.exp(mid - gcum)
    qe, ke = q * e_q, k * e_k
    a = jnp.where(tril, _bdot(qe, ke, 2, 2), 0.0)
    e_g = jnp.exp(gcum)
    qg = q * e_g
    e_s = jnp.exp(glast - gcum)
    kg = k * e_s
    e_l = jnp.exp(glast)
    upd = _bdot(hi, kg, 1, 1)
    st = jnp.zeros((HG_D, HG_D), F32)
    states = []
    for n in range(nc):
        states.append(st)
        st = st * e_l[n] + upd[n]
    st_all = jnp.stack(states)
    o = _bdot(a, hi, 2, 1) + _bdot(qg, st_all, 2, 2)
    return dict(sf=sf, f=f, k=k, sq=sq, q=q, e_q=e_q, e_k=e_k, qe=qe, ke=ke, a=a, e_g=e_g, qg=qg, o=o,
                e_s=e_s, kg=kg, e_l=e_l, st_all=st_all)


def _hgrn_specs(t_, col0):
    def col(off):
        return pl.BlockSpec((1, t_, LANE), lambda h, b: (b, 0, col0 + 4 * h + off))

    vec = pl.BlockSpec((2, LANE), lambda h, b: (0, h))
    one = pl.BlockSpec((1, LANE), lambda h, b: (0, 0))
    blk = pl.BlockSpec((1, t_, LANE), lambda h, b: (b, 0, h))
    return col, vec, one, blk


def _chunk_masks(nc, c):
    row = lax.broadcasted_iota(jnp.int32, (nc, c, c), 1)
    cl = lax.broadcasted_iota(jnp.int32, (nc, c, c), 2)
    return row >= cl, (row >= cl).astype(BF16), (row <= cl).astype(BF16)


def _hgrn_fwd(zm, lb, gn, hw, col0):
    b_, t_, _ = zm.shape
    c = min(HG_CHUNK, t_)
    nc = t_ // c
    col, vec, one, blk = _hgrn_specs(t_, col0)

    def body(q_ref, f_ref, i_ref, g_ref, lb_ref, gn_ref, y_ref):
        lbv, gnv = _sig(lb_ref[0:1, :] - lb_ref[1:2, :]), gn_ref[...]
        tril, tril_bf, _ = _chunk_masks(nc, c)
        chunks = lambda ref: ref[0].reshape(nc, c, LANE)
        o = _hgrn_forward(chunks(q_ref), chunks(f_ref), chunks(i_ref), lbv, tril, tril_bf)["o"]
        r = lax.rsqrt(jnp.mean(o * o, axis=-1, keepdims=True) + EPS)
        hg = chunks(g_ref)
        y_ref[0] = (o * r * gnv * (hg * _sig(hg))).reshape(t_, LANE)

    return pl.pallas_call(
        body, name="hgrn_fwd", grid=(HG_HEADS, b_),
        in_specs=[col(0), col(1), col(2), col(3), vec, one], out_specs=blk,
        out_shape=jax.ShapeDtypeStruct((b_, t_, hw), F32),
        compiler_params=_params(2),
    )(zm, zm, zm, zm, lb, gn)


def _hgrn_bwd(zm, dy, lb, gn, hw, col0, dz, rider=None):
    b_, t_, _ = zm.shape
    c = min(HG_CHUNK, t_)
    nc = t_ // c
    col, vec, one, blk = _hgrn_specs(t_, col0)

    def body(q_ref, f_ref, i_ref, g_ref, dy_ref, lb_ref, gn_ref, _, dz_ref, dlb_ref, dgn_ref):
        h, b = pl.program_id(0), pl.program_id(1)
        lbv, gnv = _sig(lb_ref[0:1, :] - lb_ref[1:2, :]), gn_ref[...]
        tril, tril_bf, triu_bf = _chunk_masks(nc, c)
        last_row = lax.broadcasted_iota(jnp.int32, (nc, c, LANE), 1) == c - 1
        chunks = lambda ref: ref[0].reshape(nc, c, LANE)
        flat = lambda x: x.reshape(t_, LANE)
        hq, hi, hg = chunks(q_ref), chunks(i_ref), chunks(g_ref)
        p = _hgrn_forward(hq, chunks(f_ref), hi, lbv, tril, tril_bf)
        o, q, k, st_all, e_l = p["o"], p["q"], p["k"], p["st_all"], p["e_l"]
        dyv = chunks(dy_ref)
        sg = _sig(hg)
        r = lax.rsqrt(jnp.mean(o * o, axis=-1, keepdims=True) + EPS)
        dn = dyv * (hg * sg)
        dz_ref[0, :, 3 * LANE:] = flat(dyv * (o * r * gnv) * (sg * (1.0 + hg * (1.0 - sg)))).astype(dz_ref.dtype)
        dgn = jnp.sum(flat(dn * o * r), axis=0, keepdims=True)
        dng = dn * gnv
        do = r * dng - o * (r * r * r) * jnp.mean(dng * o, axis=-1, keepdims=True)
        back = _bdotp(do, p["qg"], 1, 1)
        dst = jnp.zeros((HG_D, HG_D), F32)
        dsts = [None] * nc
        for n in range(nc - 1, -1, -1):
            dsts[n] = dst
            dst = dst * e_l[n] + back[n]
        dst_all = jnp.stack(dsts)
        da = jnp.where(tril, _bdotp(do, hi, 2, 2), 0.0)
        dq = _bdotp(da, p["ke"], 2, 1) * p["e_q"] + _bdotp(do, st_all, 2, 1) * p["e_g"]
        dk_state = _bdotp(hi, dst_all, 2, 1) * p["e_s"]
        dk = _bdotp(da, p["qe"], 1, 1) * p["e_k"] + dk_state
        dz_ref[0, :, 2 * LANE:3 * LANE] = flat(_bdot(p["a"], do, 1, 1) + _bdot(p["kg"], dst_all, 2, 2)).astype(dz_ref.dtype)
        extra = (jnp.sum(k * dk_state, axis=1, keepdims=True) + e_l * jnp.sum(st_all * dst_all, axis=1, keepdims=True))
        dgc = q * dq - k * dk + jnp.where(last_row, extra, 0.0)
        dfv = _tri_dot_b(triu_bf, dgc) / p["f"] - dk
        sf, sq = p["sf"], p["sq"]
        dz_ref[0, :, LANE:2 * LANE] = flat(dfv * (1.0 - lbv) * sf * (1.0 - sf)).astype(dz_ref.dtype)
        dlb = jnp.sum(flat(dfv * (1.0 - sf)), axis=0, keepdims=True)
        dz_ref[0, :, :LANE] = flat(dq * (sq * (1.0 + hq * (1.0 - sq)))).astype(dz_ref.dtype)
        dl0 = dlb * lbv * (1.0 - lbv)
        _acc(dlb_ref, jnp.concatenate([dl0, -dl0], axis=0), b == 0)
        _acc(dgn_ref, dgn, jnp.logical_and(b == 0, h == 0))

    return _ride_call(
        body, rider, name="hgrn_bwd", grid=(HG_HEADS, b_),
        in_specs=[col(0), col(1), col(2), col(3), blk, vec, one, ANY],
        out_specs=[pl.BlockSpec((1, t_, 4 * LANE), lambda h, b: (b, 0, col0 // 4 + h)), vec, one],
        out_shape=[jax.ShapeDtypeStruct(dz.shape, dz.dtype), jax.ShapeDtypeStruct((2, hw), F32),
                   jax.ShapeDtypeStruct((1, LANE), F32)],
        scratch=[], args=(zm, zm, zm, zm, dy, lb, gn, dz), aliases={7: 0})


def _fox_logf(x):
    return jnp.minimum(x, 0.0) - jnp.log(1.0 + jnp.exp(-jnp.abs(x)))


def _fox_prep(zf, bias):
    b_, t_, _ = zf.shape
    tb = min(FOX_BLOCK, t_)
    nb = t_ // tb

    def body(z_ref, b_ref, fc_ref):
        tril_bf = (lax.broadcasted_iota(jnp.int32, (tb, tb), 0) >= lax.broadcasted_iota(jnp.int32, (tb, tb), 1)).astype(BF16)
        bv = b_ref[...]

        def blk(i, carry):
            rows = pl.ds(pl.multiple_of(i * tb, tb), tb)
            fc = _tri_dot(tril_bf, _fox_logf(z_ref[0, rows, :] + bv)) + carry
            fc_ref[0, rows, :] = fc
            return fc[tb - 1:tb, :]

        lax.fori_loop(0, nb, blk, jnp.zeros((1, LANE), F32))

    blk_spec = pl.BlockSpec((1, t_, LANE), lambda b: (b, 0, 0))
    return pl.pallas_call(
        body, name="fox_prep", grid=(b_,),
        in_specs=[blk_spec, pl.BlockSpec((1, LANE), lambda b: (0, 0))], out_specs=blk_spec,
        out_shape=jax.ShapeDtypeStruct((b_, t_, LANE), F32), compiler_params=_params(1),
    )(zf, bias)


def _fox_post(dfc, zf, bias):
    b_, t_, _ = zf.shape
    npair = dfc.shape[1]
    tb = min(FOX_BLOCK, t_)
    nb = t_ // tb

    def body(d_ref, z_ref, b_ref, dz_ref, db_ref):
        triu_bf = (lax.broadcasted_iota(jnp.int32, (tb, tb), 0) <= lax.broadcasted_iota(jnp.int32, (tb, tb), 1)).astype(BF16)
        valid = lax.broadcasted_iota(jnp.int32, (tb, LANE), 1) < FOX_HEADS
        bv = b_ref[...]

        def blk(m, carry):
            tail, db = carry
            rows = pl.ds(pl.multiple_of((nb - 1 - m) * tb, tb), tb)
            dfc_rows = d_ref[0, 0, rows, :]
            for p in range(1, npair):
                dfc_rows = dfc_rows + pltpu.roll(d_ref[0, p, rows, :], 2 * p, 1)
            dlf = _tri_dot(triu_bf, dfc_rows) + tail
            dx = jnp.where(valid, dlf * _sig(-(z_ref[0, rows, :] + bv)), 0.0)
            dz_ref[0, rows, :] = dx.astype(dz_ref.dtype)
            return dlf[0:1, :], db + jnp.sum(dx, axis=0, keepdims=True)

        z1 = jnp.zeros((1, LANE), F32)
        _, db = lax.fori_loop(0, nb, blk, (z1, z1))
        _acc(db_ref, db, pl.program_id(0) == 0)

    blk_spec = pl.BlockSpec((1, t_, LANE), lambda b: (b, 0, 0))
    vec = pl.BlockSpec((1, LANE), lambda b: (0, 0))
    return pl.pallas_call(
        body, name="fox_post", grid=(b_,),
        in_specs=[pl.BlockSpec((1, npair, t_, LANE), lambda b: (b, 0, 0, 0)), blk_spec, vec], out_specs=[blk_spec, vec],
        out_shape=[jax.ShapeDtypeStruct((b_, t_, LANE), MXU_DTYPE), jax.ShapeDtypeStruct((1, LANE), F32)],
        compiler_params=_params(1),
    )(dfc, zf, bias)


FOX_TILE = 256
FOX_BAND = 512
AUG = 64


def _head_mean_matrix():
    r = lax.broadcasted_iota(jnp.int32, (LANE, LANE), 0) // FOX_DH
    c = lax.broadcasted_iota(jnp.int32, (LANE, LANE), 1) // FOX_DH
    return (r == c).astype(BF16)


def _dot_right_exact(x, m_bf):
    hi = x.astype(BF16)
    r = x - hi.astype(F32)
    mid = r.astype(BF16)
    lo = (r - mid.astype(F32)).astype(BF16)

    def d(v):
        return lax.dot_general(v, m_bf, (((1,), (0,)), ((), ())), preferred_element_type=F32)

    return d(hi) + d(mid) + d(lo)


def _pair_norm(x, g2, bd):
    r = lax.rsqrt(_dot_right_exact(x * x, bd) * (1.0 / FOX_DH) + EPS)
    return x * r * g2, r


def _pair_norm_bwd(x, r, dy, g2, bd):
    dyg = dy * g2
    dx = r * dyg - x * (r * r * r) * (_dot_right_exact(dyg * x, bd) * (1.0 / FOX_DH))
    return dx, jnp.sum(dy * x * r, axis=0, keepdims=True)


def _head_lanes(xn, hh):
    return xn if hh == 0 else pltpu.roll(xn, FOX_DH, 1)


def _split3(x):
    hi = x.astype(BF16).astype(F32)
    mid = (x - hi).astype(BF16).astype(F32)
    return hi, mid, x - hi - mid


def _fox_operands(q_ref, k_ref, v_ref, fc_ref, gq2, gk2, p, qa, ka, va):
    t_ = q_ref.shape[1]
    bd = _head_mean_matrix()
    lane = lax.broadcasted_iota(jnp.int32, (t_, LANE), 1)
    qx, kx = q_ref[0], k_ref[0]
    qn, rq = _pair_norm(qx, gq2, bd)
    kn, rk = _pair_norm(kx, gk2, bd)
    vv = v_ref[0]
    q_aug = jnp.where(jnp.logical_and(lane >= AUG, lane < AUG + 3), 1.0, 0.0)
    for hh in range(2):
        fcol = jnp.sum(jnp.where(lane == 2 * p + hh, fc_ref[0], 0.0), axis=-1, keepdims=True)
        hi, mid, lo = _split3(-fcol)
        k_aug = jnp.where(lane == AUG, hi, jnp.where(lane == AUG + 1, mid, jnp.where(lane == AUG + 2, lo,
                          jnp.where(lane == AUG + 3, 1.0, 0.0))))
        head = lane < FOX_DH
        qa[hh] = jnp.where(head, _head_lanes(qn, hh), q_aug).astype(MXU_DTYPE)
        ka[hh] = jnp.where(head, _head_lanes(kn, hh), k_aug).astype(MXU_DTYPE)
        va[hh] = jnp.where(head, _head_lanes(vv, hh), 0.0).astype(MXU_DTYPE)
    return bd, lane, qx, kx, rq, rk


def _fox_specs(t_, fw, col0):
    npair = fw // LANE

    def col(off):
        return pl.BlockSpec((1, t_, LANE), lambda b, p: (b, 0, col0 + 3 * p + off))

    pair = pl.BlockSpec((1, t_, LANE), lambda b, p: (b, 0, p))
    full = pl.BlockSpec((1, t_, LANE), lambda b, p: (b, 0, 0))
    gvec = pl.BlockSpec((1, LANE), lambda b, p: (0, 0))
    lse = pl.BlockSpec((1, 1, t_, LANE), lambda b, p: (b, p, 0, 0))
    return col, pair, full, gvec, lse


def _fox_fwd(zm, fc, gq2, gk2, fw, col0):
    b_, t_, _ = zm.shape
    npair = fw // LANE
    tq = min(FOX_TILE, t_)
    bw = min(FOX_BAND, t_)
    nband, tpb = t_ // bw, bw // tq
    scale = FOX_DH ** -0.5
    col, pair, full, gvec, lse_spec = _fox_specs(t_, fw, col0)

    def body(q_ref, k_ref, v_ref, fc_ref, gq_ref, gk_ref, o_ref, lse_ref, qa, ka, va):
        p = pl.program_id(1)
        _fox_operands(q_ref, k_ref, v_ref, fc_ref, gq_ref[...] * scale, gk_ref[...], p, qa, ka, va)
        ahead = lax.broadcasted_iota(jnp.int32, (tq, bw), 1) - lax.broadcasted_iota(jnp.int32, (tq, bw), 0)
        lane = lax.broadcasted_iota(jnp.int32, (tq, LANE), 1)

        for band in range(nband):
            c0 = band * bw

            def qtile(ii, _, c0=c0):
                r0 = pl.multiple_of(c0 + ii * tq, tq)
                rows = pl.ds(r0, tq)
                keep = ahead <= r0 - c0
                res = []
                for hh in range(2):
                    qb = qa[hh, rows, :]
                    s_b = jnp.where(keep, _nt(qb, ka[hh, c0:c0 + bw, :]), NEG)
                    m = jnp.max(s_b, axis=-1, keepdims=True)
                    if c0:
                        s_a = _nt(qb, ka[hh, 0:c0, :])
                        m = jnp.maximum(m, jnp.max(s_a, axis=-1, keepdims=True))
                    p_b = jnp.exp(s_b - m)
                    l = jnp.sum(p_b, axis=-1, keepdims=True)
                    acc = _nn(p_b, va[hh, c0:c0 + bw, :])
                    if c0:
                        p_a = jnp.exp(s_a - m)
                        l = l + jnp.sum(p_a, axis=-1, keepdims=True)
                        acc = acc + _nn(p_a, va[hh, 0:c0, :])
                    res.append((acc / l, m + jnp.log(l)))
                (o0, e0), (o1, e1) = res
                o_ref[0, rows, :] = jnp.where(lane < FOX_DH, o0, pltpu.roll(o1, FOX_DH, 1))
                lse_ref[0, 0, rows, :] = jnp.where(lane == 0, e0, jnp.where(lane == 1, e1, 0.0))
                return 0

            lax.fori_loop(0, tpb, qtile, 0)

    return pl.pallas_call(
        body, name="fox_fwd", grid=(b_, npair),
        in_specs=[col(0), col(1), col(2), full, gvec, gvec],
        out_specs=[pair, lse_spec],
        out_shape=[jax.ShapeDtypeStruct((b_, t_, fw), F32), jax.ShapeDtypeStruct((b_, npair, t_, LANE), F32)],
        scratch_shapes=[pltpu.VMEM((2, t_, LANE), MXU_DTYPE)] * 3,
        compiler_params=_params(2),
    )(zm, zm, zm, fc, gq2, gk2)


def _norm_bwd(x, dy, g):
    r = lax.rsqrt(jnp.mean(x * x, axis=-1, keepdims=True) + EPS)
    dyg = dy * g
    dx = r * dyg - x * (r * r * r) * jnp.mean(dyg * x, axis=-1, keepdims=True)
    return dx, jnp.sum(dy * x * r, axis=0, keepdims=True)


def _fox_bwd(zm, o, do, lse, fc, gq2, gk2, fw, col0, dz, rider=None):
    b_, t_, _ = zm.shape
    npair = fw // LANE
    tq = min(FOX_TILE, t_)
    nb = t_ // tq
    bw = min(FOX_BAND, t_)
    nband, tpb = t_ // bw, bw // tq
    scale = FOX_DH ** -0.5
    col, pair, full, gvec, lse_spec = _fox_specs(t_, fw, col0)

    def body(q_ref, k_ref, v_ref, o_ref, do_ref, lse_ref, fc_ref, gq_ref, gk_ref, _,
             dz_ref, dfc_ref, dgq_ref, dgk_ref, qa, ka, va, da, rowv, dq_acc, dk_acc, dv_acc):
        b, p = pl.program_id(0), pl.program_id(1)
        gq2v, gk2v = gq_ref[...] * scale, gk_ref[...]
        bd, lane, qx, kx, rq, rk = _fox_operands(q_ref, k_ref, v_ref, fc_ref, gq2v, gk2v, p, qa, ka, va)
        head = lane < FOX_DH
        dov = do_ref[0]
        dsum = _dot_right_exact(dov * o_ref[0], bd)
        eye = (lax.broadcasted_iota(jnp.int32, (tq, tq), 0) == lax.broadcasted_iota(jnp.int32, (tq, tq), 1)).astype(F32)
        for hh in range(2):
            da[hh] = jnp.where(head, _head_lanes(dov, hh), 0.0).astype(MXU_DTYPE)
            for blk in range(nb):
                rs = slice(blk * tq, (blk + 1) * tq)
                rowv[2 * hh:2 * hh + 1, rs] = jnp.sum(eye * lse_ref[0, 0, rs, hh:hh + 1], axis=0, keepdims=True)
                rowv[2 * hh + 1:2 * hh + 2, rs] = jnp.sum(eye * dsum[rs, hh * FOX_DH:hh * FOX_DH + 1], axis=0, keepdims=True)
        dq_acc[...] = jnp.zeros(dq_acc.shape, F32)
        ahead = lax.broadcasted_iota(jnp.int32, (tq, bw), 1) - lax.broadcasted_iota(jnp.int32, (tq, bw), 0)

        def part(hh, kb, vb, lo, hi, keep):
            qm, dm = qa[hh, lo:hi, :], da[hh, lo:hi, :]
            pt = jnp.exp(_nt(kb, qm) - rowv[2 * hh:2 * hh + 1, lo:hi])
            if keep is not None:
                pt = jnp.where(keep, pt, 0.0)
            dst = pt * (_nt(vb, dm) - rowv[2 * hh + 1:2 * hh + 2, lo:hi])
            dq_acc[hh, lo:hi, :] += _tn(dst, kb)
            return _nn(dst, qm), _nn(pt, dm)

        for band in range(nband):
            c0 = band * bw

            def kvtile(jj, _, c0=c0):
                r0 = pl.multiple_of(c0 + jj * tq, tq)
                rows = pl.ds(r0, tq)
                keep = ahead >= r0 - c0
                for hh in range(2):
                    kb, vb = ka[hh, rows, :], va[hh, rows, :]
                    dk_t, dv_t = part(hh, kb, vb, c0, c0 + bw, keep)
                    if c0 + bw < t_:
                        dk_u, dv_u = part(hh, kb, vb, c0 + bw, t_, None)
                        dk_t, dv_t = dk_t + dk_u, dv_t + dv_u
                    dk_acc[hh, rows, :] = dk_t
                    dv_acc[hh, rows, :] = dv_t
                return 0

            lax.fori_loop(0, tpb, kvtile, 0)

        dq0, dq1, dk0, dk1 = dq_acc[0], dq_acc[1], dk_acc[0], dk_acc[1]
        dqn = jnp.where(head, dq0, pltpu.roll(dq1, FOX_DH, 1))
        dkn = jnp.where(head, dk0, pltpu.roll(dk1, FOX_DH, 1))
        dqx, gq_part = _pair_norm_bwd(qx, rq, dqn, gq2v, bd)
        dkx, gk_part = _pair_norm_bwd(kx, rk, dkn, gk2v, bd)
        dz_ref[0, :, :LANE] = dqx.astype(dz_ref.dtype)
        dz_ref[0, :, LANE:2 * LANE] = dkx.astype(dz_ref.dtype)
        dz_ref[0, :, 2 * LANE:] = jnp.where(head, dv_acc[0], pltpu.roll(dv_acc[1], FOX_DH, 1)).astype(dz_ref.dtype)

        def bias_grad(dqh, dkh):
            return (jnp.sum(jnp.where(lane == AUG + 3, dqh, 0.0), axis=-1, keepdims=True)
                    - jnp.sum(jnp.where(lane == AUG, dkh, 0.0), axis=-1, keepdims=True))

        dfc_ref[0, 0] = jnp.where(lane == 0, bias_grad(dq0, dk0), jnp.where(lane == 1, bias_grad(dq1, dk1), 0.0))
        first = jnp.logical_and(b == 0, p == 0)
        _acc(dgq_ref, gq_part * scale, first)
        _acc(dgk_ref, gk_part, first)

    gs = jax.ShapeDtypeStruct((1, LANE), F32)
    return _ride_call(
        body, rider, name="fox_bwd", grid=(b_, npair),
        in_specs=[col(0), col(1), col(2), pair, pair, lse_spec, full, gvec, gvec, ANY],
        out_specs=[pl.BlockSpec((1, t_, 3 * LANE), lambda b, p: (b, 0, col0 // 3 + p)), lse_spec, gvec, gvec],
        out_shape=[jax.ShapeDtypeStruct(dz.shape, dz.dtype), jax.ShapeDtypeStruct((b_, npair, t_, LANE), F32), gs, gs],
        scratch=[pltpu.VMEM((2, t_, LANE), MXU_DTYPE)] * 4
        + [pltpu.VMEM((8, t_), F32)] + [pltpu.VMEM((2, t_, LANE), F32)] * 3,
        args=(zm, zm, zm, o, do, lse, fc, gq2, gk2, dz), aliases={9: 0})


def _mem_specs(t_, m_, mw, col0):
    nh = mw // LANE
    qcol = pl.BlockSpec((1, t_, LANE), lambda b, h: (b, 0, col0 + h))
    kcol = pl.BlockSpec((1, m_, LANE), lambda b, h: (b, 0, h))
    vcol = pl.BlockSpec((1, m_, LANE), lambda b, h: (b, 0, nh + h))
    ycol = pl.BlockSpec((1, t_, LANE), lambda b, h: (b, 0, h))
    gvec = pl.BlockSpec((1, LANE), lambda b, h: (0, 0))
    return qcol, kcol, vcol, ycol, gvec


def _mem_fwd(zm, mkv, gq, gk, mw, col0):
    b_, t_, _ = zm.shape
    m_ = mkv.shape[1]
    tq = min(512, t_)
    nb = t_ // tq
    scale = MEM_DH ** -0.5
    qcol, kcol, vcol, ycol, gvec = _mem_specs(t_, m_, mw, col0)

    def body(q_ref, k_ref, v_ref, gq_ref, gk_ref, y_ref):
        gqv, gkv = gq_ref[...] * scale, gk_ref[...]
        kv = k_ref[0]
        kn = _mx(kv * lax.rsqrt(jnp.mean(kv * kv, axis=-1, keepdims=True) + EPS) * gkv)
        vv = _mx(v_ref[0])

        def blk(i, _):
            rows = pl.ds(pl.multiple_of(i * tq, tq), tq)
            qv = q_ref[0, rows, :]
            s = _nt(qv * lax.rsqrt(jnp.mean(qv * qv, axis=-1, keepdims=True) + EPS) * gqv, kn)
            e = jnp.exp(s - jnp.max(s, axis=-1, keepdims=True))
            y_ref[0, rows, :] = _nn(e / jnp.sum(e, axis=-1, keepdims=True), vv)
            return 0

        lax.fori_loop(0, nb, blk, 0)

    return pl.pallas_call(
        body, name="mem_fwd", grid=(b_, MEM_HEADS), in_specs=[qcol, kcol, vcol, gvec, gvec], out_specs=ycol,
        out_shape=jax.ShapeDtypeStruct((b_, t_, mw), F32), compiler_params=_params(2),
    )(zm, mkv, mkv, gq, gk)


def _mem_bwd(zm, mkv, dy, gq, gk, mw, col0, dz):
    b_, t_, _ = zm.shape
    m_ = mkv.shape[1]
    tq = min(512, t_)
    nb = t_ // tq
    scale = MEM_DH ** -0.5
    qcol, kcol, vcol, ycol, gvec = _mem_specs(t_, m_, mw, col0)

    def body(q_ref, k_ref, v_ref, dy_ref, gq_ref, gk_ref, _, dq_ref, dk_ref, dv_ref, dgq_ref, dgk_ref):
        gqv, gkv = gq_ref[...] * scale, gk_ref[...]
        kv = k_ref[0]
        kn = _mx(kv * lax.rsqrt(jnp.mean(kv * kv, axis=-1, keepdims=True) + EPS) * gkv)
        vv = _mx(v_ref[0])

        def blk(i, carry):
            dkn, dvv, dgq = carry
            rows = pl.ds(pl.multiple_of(i * tq, tq), tq)
            qv = q_ref[0, rows, :]
            qn = _mx(qv * lax.rsqrt(jnp.mean(qv * qv, axis=-1, keepdims=True) + EPS) * gqv)
            s = _nt(qn, kn)
            e = jnp.exp(s - jnp.max(s, axis=-1, keepdims=True))
            pm = e / jnp.sum(e, axis=-1, keepdims=True)
            dob = _mx(dy_ref[0, rows, :])
            dp = _nt(dob, vv)
            ds = pm * (dp - jnp.sum(dp * pm, axis=-1, keepdims=True))
            dqv, gq_part = _norm_bwd(qv, _nn(ds, kn), gqv)
            dq_ref[0, rows, :] = dqv.astype(dq_ref.dtype)
            return dkn + _tn(ds, qn), dvv + _tn(pm, dob), dgq + gq_part * scale

        z = jnp.zeros((m_, LANE), F32)
        dkn, dvv, dgq = lax.fori_loop(0, nb, blk, (z, z, jnp.zeros((1, LANE), F32)))
        dkv, dgk = _norm_bwd(kv, dkn, gkv)
        dk_ref[0] = dkv
        dv_ref[0] = dvv
        first = jnp.logical_and(pl.program_id(0) == 0, pl.program_id(1) == 0)
        _acc(dgq_ref, dgq, first)
        _acc(dgk_ref, dgk, first)

    kblk = pl.BlockSpec((1, m_, LANE), lambda b, h: (b, 0, h))
    gs = jax.ShapeDtypeStruct((1, LANE), F32)
    ks = jax.ShapeDtypeStruct((b_, m_, mw), F32)
    return pl.pallas_call(
        body, name="mem_bwd", grid=(b_, MEM_HEADS), in_specs=[qcol, kcol, vcol, ycol, gvec, gvec, ANY],
        out_specs=[qcol, kblk, kblk, gvec, gvec],
        out_shape=[jax.ShapeDtypeStruct(dz.shape, dz.dtype), ks, ks, gs, gs], input_output_aliases={6: 0},
        compiler_params=_params(2),
    )(zm, mkv, mkv, dy, gq, gk, dz)


def _merge_specs(tm, d, w, gcol):
    row_d = pl.BlockSpec((tm, d), lambda i: (i, 0))
    row_w = pl.BlockSpec((tm, w), lambda i: (i, 0))
    gates = [pl.BlockSpec((tm, d), functools.partial(lambda i, k: (i, gcol + k), k=k)) for k in range(3)]
    w_br = pl.BlockSpec((w, d), lambda i: (0, 0))
    w_o = pl.BlockSpec((d, d), lambda i: (0, 0))
    return row_d, row_w, gates, w_br, w_o


def _merge_fwd(x, ys, zm, w_brs, w_out, gcol, tm=256):
    n, d = x.shape
    w = ys[0].shape[1]
    tm = _tile(n, tm, 8)
    row_d, row_w, gates, w_br, w_o = _merge_specs(tm, d, w, gcol)

    def body(x_ref, ya, yb, yc, g0, g1, g2, wa, wb, wc, wo, x1_ref, mg_ref):
        mg = (_sig(g0[...]) * _nn(ya[...], wa[...]) + _sig(g1[...]) * _nn(yb[...], wb[...])
              + _sig(g2[...]) * _nn(yc[...], wc[...]))
        mg_ref[...] = mg.astype(mg_ref.dtype)
        x1_ref[...] = x_ref[...] + _nn(mg, wo[...])

    return pl.pallas_call(
        body, name="merge_fwd", grid=(n // tm,),
        in_specs=[row_d, row_w, row_w, row_w] + gates + [w_br, w_br, w_br, w_o],
        out_specs=[row_d, row_d],
        out_shape=[jax.ShapeDtypeStruct((n, d), F32), jax.ShapeDtypeStruct((n, d), MXU_DTYPE)],
        compiler_params=_params(1),
    )(x, *ys, zm, zm, zm, *w_brs, w_out)


def _merge_bwd(dx1, ys, zm, w_brs, w_out, gcol, tm=256):
    n, d = dx1.shape
    w = ys[0].shape[1]
    tm = _tile(n, tm, 8)
    row_d, row_w, gates, w_br, w_o = _merge_specs(tm, d, w, gcol)

    def body(dx_ref, ya, yb, yc, g0, g1, g2, wa, wb, wc, wo, dgl_ref, dpa, dpb, dpc, dya, dyb, dyc):
        dm = _nt(dx_ref[...], wo[...])
        for k, (y, g, wr, dp_ref, dy_ref) in enumerate(((ya, g0, wa, dpa, dya), (yb, g1, wb, dpb, dyb),
                                                        (yc, g2, wc, dpc, dyc))):
            sg = _sig(g[...])
            pr = _nn(y[...], wr[...])
            dgl_ref[:, k * d:(k + 1) * d] = (dm * pr * sg * (1.0 - sg)).astype(dgl_ref.dtype)
            dp = (dm * sg).astype(dp_ref.dtype)
            dp_ref[...] = dp
            dy_ref[...] = _nt(dp, wr[...])

    sd = jax.ShapeDtypeStruct((n, d), MXU_DTYPE)
    sw = jax.ShapeDtypeStruct((n, w), F32)
    return pl.pallas_call(
        body, name="merge_bwd", grid=(n // tm,),
        in_specs=[row_d, row_w, row_w, row_w] + gates + [w_br, w_br, w_br, w_o],
        out_specs=[pl.BlockSpec((tm, 3 * d), lambda i: (i, 0)), row_d, row_d, row_d, row_w, row_w, row_w],
        out_shape=[jax.ShapeDtypeStruct((n, zm.shape[1]), MXU_DTYPE), sd, sd, sd, sw, sw, sw],
        compiler_params=_params(1),
    )(dx1, *ys, zm, zm, zm, *w_brs, w_out)


CONV_ROWS = 256
HALO = 8


def _ext(ref, r0, t_):
    rc = min(CONV_ROWS, t_)
    a, b = max(r0 - HALO, 0), min(r0 + rc + HALO, t_)
    parts = []
    if r0 - HALO < 0:
        parts.append(jnp.zeros((HALO, ref.shape[2]), F32))
    parts.append(ref[0, a:b, :].astype(F32))
    if r0 + rc + HALO > t_:
        parts.append(jnp.zeros((HALO, ref.shape[2]), F32))
    return jnp.concatenate(parts, axis=0) if len(parts) > 1 else parts[0]


def _gelu_parts(ac):
    cdf = 0.5 * (1.0 + _erf(ac * (2.0 ** -0.5)))
    pdf = jnp.exp(-0.5 * ac * ac) * ((2.0 * math.pi) ** -0.5)
    return cdf, pdf


def _conv_taps(a_ext, cw, cb):
    return cw[0:1, :] * pltpu.roll(a_ext, 2, 0) + cw[1:2, :] * pltpu.roll(a_ext, 1, 0) + cw[2:3, :] * a_ext + cb


def _glu_specs(t_, f, g):
    gate = pl.BlockSpec((1, t_, g), lambda j, b: (b, 0, j))
    value = pl.BlockSpec((1, t_, g), lambda j, b: (b, 0, f // g + j))
    cwb = pl.BlockSpec((3, g), lambda j, b: (0, j))
    cbb = pl.BlockSpec((1, g), lambda j, b: (0, j))
    return gate, value, cwb, cbb


def _glu_fwd(u, cw, cb):
    b_, t_, f2 = u.shape
    f = f2 // 2
    g = min(FFN_GROUP, f)
    rc = min(CONV_ROWS, t_)
    gate, value, cwb, cbb = _glu_specs(t_, f, g)

    def body(a_ref, v_ref, cw_ref, cb_ref, y_ref):
        cwv, cbv = cw_ref[...], cb_ref[...]
        for r0 in range(0, t_, rc):
            ac = _conv_taps(_ext(a_ref, r0, t_), cwv, cbv)[HALO:HALO + rc]
            cdf, _ = _gelu_parts(ac)
            y_ref[0, r0:r0 + rc, :] = (ac * cdf * v_ref[0, r0:r0 + rc, :]).astype(y_ref.dtype)

    return pl.pallas_call(
        body, name="glu_fwd", grid=(f // g, b_), in_specs=[gate, value, cwb, cbb], out_specs=gate,
        out_shape=jax.ShapeDtypeStruct((b_, t_, f), MXU_DTYPE), compiler_params=_params(2),
    )(u, u, cw, cb)


def _glu_bwd(u, dy, cw, cb):
    b_, t_, f2 = u.shape
    f = f2 // 2
    g = min(FFN_GROUP, f)
    rc = min(CONV_ROWS, t_)
    ne = rc + 2 * HALO
    gate, value, cwb, cbb = _glu_specs(t_, f, g)

    def body(a_ref, v_ref, dy_ref, cw_ref, cb_ref, da_ref, dv_ref, dcw_ref, dcb_ref):
        cwv, cbv = cw_ref[...], cb_ref[...]
        dcw = [jnp.zeros((1, g), F32) for _ in range(3)]
        dcb = jnp.zeros((1, g), F32)
        for r0 in range(0, t_, rc):
            a_ext, v_ext, dy_ext = _ext(a_ref, r0, t_), _ext(v_ref, r0, t_), _ext(dy_ref, r0, t_)
            ac = _conv_taps(a_ext, cwv, cbv)
            cdf, pdf = _gelu_parts(ac)
            dac = dy_ext * v_ext * (cdf + ac * pdf)
            da = cwv[2:3, :] * dac + cwv[1:2, :] * pltpu.roll(dac, ne - 1, 0) + cwv[0:1, :] * pltpu.roll(dac, ne - 2, 0)
            mid = slice(HALO, HALO + rc)
            da_ref[0, r0:r0 + rc, :] = da[mid].astype(da_ref.dtype)
            dv_ref[0, r0:r0 + rc, :] = (dy_ext[mid] * ac[mid] * cdf[mid]).astype(dv_ref.dtype)
            dacm = dac[mid]
            dcw[0] = dcw[0] + jnp.sum(dacm * pltpu.roll(a_ext, 2, 0)[mid], axis=0, keepdims=True)
            dcw[1] = dcw[1] + jnp.sum(dacm * pltpu.roll(a_ext, 1, 0)[mid], axis=0, keepdims=True)
            dcw[2] = dcw[2] + jnp.sum(dacm * a_ext[mid], axis=0, keepdims=True)
            dcb = dcb + jnp.sum(dacm, axis=0, keepdims=True)
        first = pl.program_id(1) == 0
        _acc(dcw_ref, jnp.concatenate(dcw, axis=0), first)
        _acc(dcb_ref, dcb, first)

    sds = jax.ShapeDtypeStruct((b_, t_, f), MXU_DTYPE)
    return pl.pallas_call(
        body, name="glu_bwd", grid=(f // g, b_), in_specs=[gate, value, gate, cwb, cbb],
        out_specs=[gate, gate, cwb, cbb],
        out_shape=[sds, sds, jax.ShapeDtypeStruct((3, f), F32), jax.ShapeDtypeStruct((1, f), F32)],
        compiler_params=_params(2),
    )(u, u, dy, cw, cb)


def _place():
    x, y, c = lax.axis_index("x"), lax.axis_index("y"), lax.axis_index("c")
    chips = [(1 - x, y), (x, 1 - y), (1 - x, 1 - y)]
    return x, y, c, chips


def _remote(src, dst, send_sem, recv_sem, to):
    return pltpu.make_async_remote_copy(src_ref=src, dst_ref=dst, send_sem=send_sem, recv_sem=recv_sem,
                                        device_id=to, device_id_type=MESH)


STACK, COLS = "stack", "cols"


def _shard_ref(ref, kind, s, rows, c):
    if kind == COLS:
        cols = pl.ds(pl.multiple_of(s * c, LANE), c)
        return ref.at[:, cols] if rows is None else ref.at[rows, cols]
    return ref.at[s] if rows is None else ref.at[s, rows, :]


def _halves(c, half):
    mine = pl.ds(pl.multiple_of(c * half, 16), half)
    theirs = pl.ds(pl.multiple_of((1 - c) * half, 16), half)
    return mine, theirs


def _gather_parts(kinds):
    def first_copies(ins, outs, sems):
        x, y, c, chips = _place()
        me = 2 * x + y
        cps = []
        for i, (w_ref, o_ref, kind) in enumerate(zip(ins, outs, kinds)):
            r, cw = w_ref.shape
            mine, _ = _halves(c, r // 2)
            for j, chip in enumerate(chips):
                cps.append(_remote(w_ref.at[mine], _shard_ref(o_ref, kind, me, mine, cw), sems[0].at[6 * i + j],
                                   sems[1].at[6 * i + j], (*chip, c)))
        return cps

    def start(ins, outs, sems):
        for cp in first_copies(ins, outs, sems):
            cp.start()

    def finish(ins, outs, sems):
        x, y, c, chips = _place()
        sib = (x, y, 1 - c)
        passed = []
        for i, (w_ref, o_ref, kind) in enumerate(zip(ins, outs, kinds)):
            r, cw = w_ref.shape
            mine, _ = _halves(c, r // 2)
            for j, (px, py) in enumerate(chips):
                blk = _shard_ref(o_ref, kind, 2 * px + py, mine, cw)
                _remote(blk, blk, sems[0].at[6 * i + j], sems[1].at[6 * i + j], sib).wait_recv()
                passed.append(_remote(blk, blk, sems[0].at[6 * i + 3 + j], sems[1].at[6 * i + 3 + j], sib))
                passed[-1].start()
        for i, (w_ref, o_ref, kind) in enumerate(zip(ins, outs, kinds)):
            r, cw = w_ref.shape
            _, theirs = _halves(c, r // 2)
            for j, (px, py) in enumerate(chips):
                blk = _shard_ref(o_ref, kind, 2 * px + py, theirs, cw)
                _remote(blk, blk, sems[0].at[6 * i + 3 + j], sems[1].at[6 * i + 3 + j], sib).wait_recv()
        for cp in first_copies(ins, outs, sems) + passed:
            cp.wait_send()

    return start, finish


def _gather_shapes(shards, kinds):
    return [jax.ShapeDtypeStruct((a.shape[0], N_CHIPS * a.shape[1]) if k == COLS else (N_CHIPS,) + a.shape, a.dtype)
            for a, k in zip(shards, kinds)]


def _gather_sems(nw):
    return [pltpu.SemaphoreType.DMA((6 * nw,)), pltpu.SemaphoreType.DMA((6 * nw,))]


def _gather_shards(shards, kinds):
    nw = len(shards)
    start, finish = _gather_parts(kinds)

    def body(*refs):
        ins, outs, sems = refs[:nw], refs[nw:2 * nw], refs[2 * nw:]
        start(ins, outs, sems)
        finish(ins, outs, sems)

    return pl.pallas_call(
        body, name="gather_shards", in_specs=[ANY] * nw, out_specs=[ANY] * nw,
        out_shape=_gather_shapes(shards, kinds), scratch_shapes=_gather_sems(nw),
    )(*shards)


def _gather_rider(shards, kinds):
    start, finish = _gather_parts(kinds)
    return _Rider(list(shards), _gather_shapes(shards, kinds), _gather_sems(len(shards)), start, finish)


def _half_shape(g, kind):
    if kind == COLS:
        return (g.shape[0] // 2, g.shape[1])
    return (g.shape[0], g.shape[1] // 2, g.shape[2])


def _swap_parts(kinds):
    def copies(ins, outs, sems):
        x, y, c, _ = _place()
        cps = []
        for i, (g_ref, a_ref, kind) in enumerate(zip(ins, outs, kinds)):
            r = g_ref.shape[0] if kind == COLS else g_ref.shape[1]
            _, theirs = _halves(c, r // 2)
            src = g_ref.at[theirs] if kind == COLS else g_ref.at[:, theirs]
            cps.append(_remote(src, a_ref, sems[0].at[i], sems[1].at[i], (x, y, 1 - c)))
        return cps

    def start(ins, outs, sems):
        for cp in copies(ins, outs, sems):
            cp.start()

    def finish(ins, outs, sems):
        for cp in copies(ins, outs, sems):
            cp.wait()

    return start, finish


def _swap_shapes(gs, kinds):
    return [jax.ShapeDtypeStruct(_half_shape(g, k), g.dtype) for g, k in zip(gs, kinds)]


def _pair_swap_halves(gs, kinds, name):
    nw = len(gs)
    start, finish = _swap_parts(kinds)

    def body(*refs):
        ins, outs, sems = refs[:nw], refs[nw:2 * nw], refs[2 * nw:]
        start(ins, outs, sems)
        finish(ins, outs, sems)

    return pl.pallas_call(
        body, name=name, in_specs=[ANY] * nw, out_specs=[ANY] * nw, out_shape=_swap_shapes(gs, kinds),
        scratch_shapes=[pltpu.SemaphoreType.DMA((nw,)), pltpu.SemaphoreType.DMA((nw,))],
    )(*gs)


def _swap_rider(gs, kinds):
    start, finish = _swap_parts(kinds)
    nw = len(gs)
    return _Rider(list(gs), _swap_shapes(gs, kinds), [pltpu.SemaphoreType.DMA((nw,)), pltpu.SemaphoreType.DMA((nw,))],
                  start, finish)


def _row_tile(rows, width, itemsize=4, target=2 ** 21):
    return _tile(rows, max(8, target // (width * itemsize)), 8)


def _add_half(g, a, kind, c_idx, name):
    if kind == COLS:
        half, wd = a.shape
        tr = _row_tile(half, wd)
        nblk = half // tr
        grid = (nblk,)
        g_spec = pl.BlockSpec((tr, wd), lambda i, c_ref: (c_ref[0] * nblk + i, 0))
        a_spec = pl.BlockSpec((tr, wd), lambda i, c_ref: (i, 0))
    else:
        n, half, wd = a.shape
        tr = _row_tile(half, wd)
        nblk = half // tr
        grid = (n, nblk)
        g_spec = pl.BlockSpec((1, tr, wd), lambda s, i, c_ref: (s, c_ref[0] * nblk + i, 0))
        a_spec = pl.BlockSpec((1, tr, wd), lambda s, i, c_ref: (s, i, 0))

    def body(c_ref, g_ref, a_ref, o_ref):
        o_ref[...] = (g_ref[...] + a_ref[...]).astype(o_ref.dtype)

    return pl.pallas_call(
        body, name=name,
        grid_spec=pltpu.PrefetchScalarGridSpec(num_scalar_prefetch=1, grid=grid, in_specs=[g_spec, a_spec],
                                               out_specs=a_spec),
        out_shape=jax.ShapeDtypeStruct(a.shape, EXCHANGE_DTYPE), compiler_params=_params(len(grid)),
    )(c_idx, g, a)


def _exchange_parts(kinds):
    def copies(ins, outs, sems):
        x, y, c, chips = _place()
        me = 2 * x + y
        cps = []
        for i, (p_ref, b_ref, kind) in enumerate(zip(ins, outs, kinds)):
            cw = b_ref.shape[2]
            for j, (px, py) in enumerate(chips):
                cps.append(_remote(_shard_ref(p_ref, kind, 2 * px + py, None, cw), b_ref.at[me],
                                   sems[0].at[3 * i + j], sems[1].at[3 * i + j], (px, py, c)))
        return cps

    def start(ins, outs, sems):
        for cp in copies(ins, outs, sems):
            cp.start()

    def finish(ins, outs, sems):
        x, y, c, chips = _place()
        for i, b_ref in enumerate(outs):
            for j, (px, py) in enumerate(chips):
                blk = b_ref.at[2 * px + py]
                _remote(blk, blk, sems[0].at[3 * i + j], sems[1].at[3 * i + j], (px, py, c)).wait_recv()
        for cp in copies(ins, outs, sems):
            cp.wait_send()

    return start, finish


def _exchange_shapes(ps, kinds):
    return [jax.ShapeDtypeStruct((N_CHIPS,) + ((p.shape[0], p.shape[1] // N_CHIPS) if k == COLS else tuple(p.shape[1:])),
                                 p.dtype) for p, k in zip(ps, kinds)]


def _exchange_sems(nw):
    return [pltpu.SemaphoreType.DMA((3 * nw,)), pltpu.SemaphoreType.DMA((3 * nw,))]


def _exchange_rider(ps, kinds):
    start, finish = _exchange_parts(kinds)
    return _Rider(list(ps), _exchange_shapes(ps, kinds), _exchange_sems(len(ps)), start, finish)


def _sum_chips(bq, name):
    n, h, wd = bq.shape
    tr = _row_tile(h, wd * n)

    def body(b_ref, o_ref):
        acc = b_ref[0].astype(F32)
        for s in range(1, n):
            acc = acc + b_ref[s].astype(F32)
        o_ref[...] = acc

    return pl.pallas_call(
        body, name=name, grid=(h // tr,),
        in_specs=[pl.BlockSpec((n, tr, wd), lambda i: (0, i, 0))], out_specs=pl.BlockSpec((tr, wd), lambda i: (i, 0)),
        out_shape=jax.ShapeDtypeStruct((h, wd), F32), compiler_params=_params(1),
    )(bq)


def _pair_join_halves(qs):
    nw = len(qs)

    def body(*refs):
        ins, outs = refs[:nw], refs[nw:2 * nw]
        send_sems, recv_sems = refs[2 * nw:]
        x, y, c, _ = _place()
        sent = []
        for i, (q_ref, o_ref) in enumerate(zip(ins, outs)):
            mine, _ = _halves(c, q_ref.shape[0])
            sent.append(_remote(q_ref, o_ref.at[mine], send_sems.at[i], recv_sems.at[i], (x, y, 1 - c)))
            sent[-1].start()
        for i, (q_ref, o_ref) in enumerate(zip(ins, outs)):
            _, theirs = _halves(c, q_ref.shape[0])
            _remote(q_ref, o_ref.at[theirs], send_sems.at[i], recv_sems.at[i], (x, y, 1 - c)).wait_recv()
        for cp in sent:
            cp.wait_send()

    return pl.pallas_call(
        body, name="pair_join_halves", in_specs=[ANY] * nw, out_specs=[ANY] * nw,
        out_shape=[jax.ShapeDtypeStruct((2 * q.shape[0], q.shape[1]), q.dtype) for q in qs],
        scratch_shapes=[pltpu.SemaphoreType.DMA((nw,)), pltpu.SemaphoreType.DMA((nw,))],
    )(*qs)


def _all_sum_small(s, name):
    sr, w = s.shape

    def body(s_ref, o_ref, buf, send_sems, recv_sems):
        x, y, c, _ = _place()
        me = 4 * x + 2 * y + c
        buf[me] = s_ref[...]
        peers = []
        for k in range(1, 8):
            px = 1 - x if k & 4 else x
            py = 1 - y if k & 2 else y
            pc = 1 - c if k & 1 else c
            peers.append((px, py, pc))
        sent = [_remote(s_ref, buf.at[me], send_sems.at[k], recv_sems.at[k], peer) for k, peer in enumerate(peers)]
        for cp in sent:
            cp.start()
        for k, (px, py, pc) in enumerate(peers):
            _remote(s_ref, buf.at[4 * px + 2 * py + pc], send_sems.at[k], recv_sems.at[k], (px, py, pc)).wait_recv()
        for cp in sent:
            cp.wait_send()
        acc = buf[0]
        for d in range(1, 8):
            acc = acc + buf[d]
        o_ref[...] = acc

    vm = pl.BlockSpec(memory_space=pltpu.VMEM)
    return pl.pallas_call(
        body, name=name, in_specs=[vm], out_specs=vm, out_shape=jax.ShapeDtypeStruct((sr, w), F32),
        scratch_shapes=[pltpu.VMEM((8, sr, w), F32), pltpu.SemaphoreType.DMA((7,)), pltpu.SemaphoreType.DMA((7,))],
    )(s)


BIG = ("w_in", "mem_kv_w", "w_br_hgrn", "w_br_fox", "w_br_mem", "w_out", "ffn_w_up", "ffn_w_down")
KIND = {"w_in": STACK, "mem_kv_w": STACK, "w_br_hgrn": COLS, "w_br_fox": COLS, "w_br_mem": COLS, "w_out": STACK,
        "ffn_w_up": COLS, "ffn_w_down": STACK}
ROW_SHARDED = ("mem_kv_w", "w_out", "ffn_w_down")
FIRST = ("w_in",)
REST = tuple(nm for nm in BIG if nm not in FIRST)
LAST = ("w_in",)
TRANSPOSED = ("w_in",)


def _z_layout(d, hw, fw, mw):
    gate, npair, nh, nm = 3 * d // LANE, fw // LANE, hw // LANE, mw // LANE
    fox0, hg0 = gate, gate + 3 * npair
    o_fox, o_mem = 4 * nh, 4 * nh + 3 * npair
    order = [o_mem + nm + j for j in range(gate)]
    order += [o_fox + k * npair + p for p in range(npair) for k in range(3)]
    order += [k * nh + h for h in range(nh) for k in range(4)]
    order += [o_mem + h for h in range(nm)]
    assert fox0 % 3 == 0 and hg0 % 4 == 0
    return fox0, hg0, hg0 + 4 * nh, order


def _reorder_blocks(a, order):
    runs, start = [], 0
    for i in range(1, len(order) + 1):
        if i == len(order) or order[i] != order[i - 1] + 1:
            runs.append((order[start], order[i - 1] + 1))
            start = i
    return jnp.concatenate([a[:, lo * LANE:hi * LANE] for lo, hi in runs], axis=1)


def _put_shard(arr, kind, s, piece):
    if kind == COLS:
        return lax.dynamic_update_slice(arr, piece, (0, s * piece.shape[1]))
    return lax.dynamic_update_slice(arr, piece[None], (s, 0, 0))


def _take_shard(arr, kind, s):
    if kind == COLS:
        return lax.dynamic_slice(arr, (0, s * (arr.shape[1] // N_CHIPS)), (arr.shape[0], arr.shape[1] // N_CHIPS))
    return lax.dynamic_index_in_dim(arr, s, 0, keepdims=False)


def _w_in_pieces(cs, s1, nf):
    out = []
    for s in range(N_CHIPS):
        lo, hi = cs * s, cs * (s + 1)
        for a, b, forget in ((lo, min(hi, s1), False), (max(lo, s1), min(hi, s1 + nf), True), (max(lo, s1 + nf), hi, False)):
            if a < b:
                out.append((s, a - lo, b - lo, forget, a - s1 if forget else (a if a < s1 else a - nf)))
    return out


def _split_w_in(stacked, s1, nf):
    pieces = _w_in_pieces(stacked.shape[2], s1, nf)
    main = [stacked[s, :, a:b] for s, a, b, forget, _ in pieces if not forget]
    ff = [stacked[s, :, a:b] for s, a, b, forget, _ in pieces if forget]
    return jnp.concatenate(main, axis=1), jnp.concatenate(ff, axis=1)


def _join_w_in(g_main, g_ff, s1, nf):
    cs = (g_main.shape[1] + nf) // N_CHIPS
    shards = [[] for _ in range(N_CHIPS)]
    for s, a, b, forget, off in _w_in_pieces(cs, s1, nf):
        shards[s].append((g_ff if forget else g_main)[:, off:off + b - a])
    return jnp.stack([jnp.concatenate(p, axis=1) if len(p) > 1 else p[0] for p in shards])


SMALL = ("norm_mix_g", "norm_mem_g", "norm_ffn_g", "hgrn_lb_logits", "hgrn_norm_g", "fox_f_bias", "fox_q_norm_g",
         "fox_k_norm_g", "mem_q_norm_g", "mem_k_norm_g", "ffn_conv_b")


def _pack_small(vals):
    flats, total = [], 0
    for v in vals:
        flat = v.reshape(-1).astype(F32)
        n = -(-flat.shape[0] // FLAT_W)
        flats.append(jnp.pad(flat, (0, n * FLAT_W - flat.shape[0])))
        total += n
    if -total % 8:
        flats.append(jnp.zeros((-total % 8 * FLAT_W,), F32))
    return jnp.concatenate(flats).reshape(-1, FLAT_W)


def _unpack_small(buf, shapes):
    res, off = [], 0
    for shp in shapes:
        numel = math.prod(shp)
        n = -(-numel // FLAT_W)
        res.append(buf[off:off + n].reshape(-1)[:numel].reshape(shp))
        off += n
    return res


def _pad_lanes(v, width=LANE):
    return jnp.pad(v, ((0, 0), (0, width - v.shape[1])))


WEIGHTS = ("norm_mix_g", "norm_mem_g", "w_in", "hgrn_lb_logits", "hgrn_norm_g", "fox_f_bias", "fox_q_norm_g",
           "fox_k_norm_g", "mem_kv_w", "mem_q_norm_g", "mem_k_norm_g", "w_br_hgrn", "w_br_fox", "w_br_mem", "w_out",
           "norm_ffn_g", "ffn_w_up", "ffn_conv_w", "ffn_conv_b", "ffn_w_down")


def _local_step(x, mem, target, w, full, conv_w, late=None, hooks=None):
    b_, t_, d = x.shape
    n = b_ * t_
    hw, fw, mw = HG_HEADS * HG_D, FOX_HEADS * FOX_DH, MEM_HEADS * MEM_DH
    m_ = mem.shape[1]
    f = conv_w.shape[1]
    s1 = 4 * hw + 3 * fw
    fox_col, hg_col, mem_col, order = _z_layout(d, hw, fw, mw)
    gate_col = 0
    inverse = [order.index(j) for j in range(len(order))]

    w_main, w_ff = _split_w_in(full["w_in"], s1, FOX_HEADS)
    w_main = _reorder_blocks(w_main, order)
    w_ff = _pad_lanes(w_ff)
    f_bias = _pad_lanes(w["fox_f_bias"])
    cb = w["ffn_conv_b"]

    x2 = x.reshape(n, d)
    h = _rmsnorm_fwd(x2, w["norm_mix_g"], name="norm_mix_fwd")
    if late:
        zm, gathered = _matmul(h, w_main, name="in_proj", rider=_gather_rider(late[0], late[1]))
        full = {**full, **late[2](gathered)}
    else:
        zm = _matmul(h, w_main, name="in_proj")
    w_up = full["ffn_w_up"]
    w_brs = [full["w_br_hgrn"], full["w_br_fox"], full["w_br_mem"]]
    w_out, w_kv, w_down = full["w_out"], full["mem_kv_w"], full["ffn_w_down"]
    zf = _matmul(h, w_ff, name="in_proj_forget")
    zm3, zf3 = zm.reshape(b_, t_, -1), zf.reshape(b_, t_, LANE)
    ya = _hgrn_fwd(zm3, w["hgrn_lb_logits"], w["hgrn_norm_g"], hw, hg_col)
    fc = _fox_prep(zf3, f_bias)
    fox_gq, fox_gk = jnp.tile(w["fox_q_norm_g"], (1, 2)), jnp.tile(w["fox_k_norm_g"], (1, 2))
    yb, lse = _fox_fwd(zm3, fc, fox_gq, fox_gk, fw, fox_col)
    mem2 = mem.reshape(b_ * m_, d)
    hm = _rmsnorm_fwd(mem2, w["norm_mem_g"], name="norm_mem_fwd")
    mkv = _matmul(hm, w_kv, name="mem_kv_proj").reshape(b_, m_, 2 * mw)
    yc = _mem_fwd(zm3, mkv, w["mem_q_norm_g"], w["mem_k_norm_g"], mw, mem_col)
    ys = [ya.reshape(n, hw), yb.reshape(n, fw), yc.reshape(n, mw)]
    x1, merged = _merge_fwd(x2, ys, zm, w_brs, w_out, gate_col)
    h2 = _rmsnorm_fwd(x1, w["norm_ffn_g"], name="norm_ffn_fwd")
    u = _matmul(h2, w_up, name="ffn_up")
    u3 = u.reshape(b_, t_, 2 * f)
    yff = _glu_fwd(u3, conv_w, cb).reshape(n, f)
    dy, (loss_vec,), _ = _matmul_rows([yff], w_down, name="ffn_down_loss", tb=False, row_ins=[x1, target.reshape(n, d)],
                                      vec_ins=[], epilogue=_loss_epilogue, n_vec_out=1)

    grads = {}

    def ridden(name, call):
        if not hooks or name not in hooks:
            return call(None)[0]
        rider, then = hooks[name](grads)
        outs, extra = call(rider)
        then(extra)
        return outs

    dyff = _matmul(dy, w_down, tb=True, name="ffn_down_dx")
    grads["ffn_w_down"] = _matmul(yff, dy, ta=True, name="ffn_down_dw", tm=1408)
    du_a, du_v, grads["ffn_conv_w"], grads["ffn_conv_b"] = _glu_bwd(u3, dyff.reshape(b_, t_, f), conv_w, cb)
    du_a, du_v = du_a.reshape(n, f), du_v.reshape(n, f)
    dx1, (grads["norm_ffn_g"],), _ = _matmul_rows(
        [du_a, du_v], w_up, name="ffn_up_dx", tb=True, row_ins=[x1, dy], vec_ins=[w["norm_ffn_g"]],
        epilogue=_norm_bwd_epilogue(0), n_vec_out=1)
    grads["ffn_w_up"] = jnp.concatenate([_matmul(h2, du_a, ta=True, name="ffn_up_gate_dw"),
                                         _matmul(h2, du_v, ta=True, name="ffn_up_value_dw")], axis=1)

    dz, dpa, dpb, dpc, dya, dyb, dyc = _merge_bwd(dx1, ys, zm, w_brs, w_out, gate_col)
    dz = dz.reshape(b_, t_, -1)
    grads["w_out"] = _matmul(merged, dx1, ta=True, name="out_proj_dw")
    for nm, y_, dp_ in zip(("w_br_hgrn", "w_br_fox", "w_br_mem"), ys, (dpa, dpb, dpc)):
        grads[nm] = _matmul(y_, dp_, ta=True, name=nm + "_dw")

    dz, dmk, dmv, grads["mem_q_norm_g"], grads["mem_k_norm_g"] = _mem_bwd(
        zm3, mkv, dyc.reshape(b_, t_, mw), w["mem_q_norm_g"], w["mem_k_norm_g"], mw, mem_col, dz)
    dmkv = jnp.concatenate([dmk, dmv], axis=-1).reshape(b_ * m_, 2 * mw)
    grads["mem_kv_w"] = _matmul(hm, dmkv, ta=True, name="mem_kv_dw")
    dhm = _matmul(dmkv, w_kv, tb=True, name="mem_kv_dx")
    _, grads["norm_mem_g"] = _rmsnorm_bwd(mem2, [dhm], w["norm_mem_g"], None, name="norm_mem_bwd")

    dz, dfc, g_fq, g_fk = ridden("fox_bwd", lambda rider: _fox_bwd(
        zm3, yb, dyb.reshape(b_, t_, fw), lse, fc, fox_gq, fox_gk, fw, fox_col, dz, rider))
    grads["fox_q_norm_g"] = g_fq[:, :FOX_DH] + g_fq[:, FOX_DH:]
    grads["fox_k_norm_g"] = g_fk[:, :FOX_DH] + g_fk[:, FOX_DH:]
    dzf, g_fb = _fox_post(dfc, zf3, f_bias)
    grads["fox_f_bias"] = g_fb[:, :FOX_HEADS]

    dz, grads["hgrn_lb_logits"], grads["hgrn_norm_g"] = ridden("hgrn_bwd", lambda rider: _hgrn_bwd(
        zm3, dya.reshape(b_, t_, hw), w["hgrn_lb_logits"], w["hgrn_norm_g"], hw, hg_col, dz, rider))
    dzm = dz.reshape(n, -1)
    dzf2 = dzf.reshape(n, LANE)
    g_main = _matmul(h, dzm, ta=True, name="in_proj_dw")
    g_ff = _matmul(h, dzf2, ta=True, name="in_proj_forget_dw")
    grads["w_in"] = _join_w_in(_reorder_blocks(g_main, inverse), g_ff[:, :FOX_HEADS], s1, FOX_HEADS)

    dh_b = _matmul(dzf2, w_ff, tb=True, name="in_proj_forget_dx")

    def in_proj_dx(rider):
        dx, vecs, extra = _matmul_rows([dzm], w_main, name="in_proj_dx", tb=True, row_ins=[x2, dx1, dh_b],
                                       vec_ins=[w["norm_mix_g"]], epilogue=_norm_bwd_epilogue(1), n_vec_out=1,
                                       rider=rider)
        return [dx, vecs[0]], extra

    grad_x, grads["norm_mix_g"] = ridden("in_proj_dx", in_proj_dx)
    return loss_vec, grad_x.reshape(b_, t_, d), grads


def kernel(x, mem, norm_mix_g, norm_mem_g, w_in, hgrn_lb_logits, hgrn_norm_g, fox_f_bias, fox_q_norm_g, fox_k_norm_g, mem_kv_w, mem_q_norm_g, mem_k_norm_g, w_br_hgrn, w_br_fox, w_br_mem, w_out, norm_ffn_g, ffn_w_up, ffn_conv_w, ffn_conv_b, ffn_w_down, loss_target, m_norm_mix_g, m_norm_mem_g, m_w_in, m_hgrn_lb_logits, m_hgrn_norm_g, m_fox_f_bias, m_fox_q_norm_g, m_fox_k_norm_g, m_mem_kv_w, m_mem_q_norm_g, m_mem_k_norm_g, m_w_br_hgrn, m_w_br_fox, m_w_br_mem, m_w_out, m_norm_ffn_g, m_ffn_w_up, m_ffn_conv_w, m_ffn_conv_b, m_ffn_w_down, v_norm_mix_g, v_norm_mem_g, v_w_in, v_hgrn_lb_logits, v_hgrn_norm_g, v_fox_f_bias, v_fox_q_norm_g, v_fox_k_norm_g, v_mem_kv_w, v_mem_q_norm_g, v_mem_k_norm_g, v_w_br_hgrn, v_w_br_fox, v_w_br_mem, v_w_out, v_norm_ffn_g, v_ffn_w_up, v_ffn_conv_w, v_ffn_conv_b, v_ffn_w_down):
    w = dict(zip(WEIGHTS, (norm_mix_g, norm_mem_g, w_in, hgrn_lb_logits, hgrn_norm_g, fox_f_bias, fox_q_norm_g,
                           fox_k_norm_g, mem_kv_w, mem_q_norm_g, mem_k_norm_g, w_br_hgrn, w_br_fox, w_br_mem, w_out,
                           norm_ffn_g, ffn_w_up, ffn_conv_w, ffn_conv_b, ffn_w_down)))
    m = dict(zip(WEIGHTS, (m_norm_mix_g, m_norm_mem_g, m_w_in, m_hgrn_lb_logits, m_hgrn_norm_g, m_fox_f_bias,
                           m_fox_q_norm_g, m_fox_k_norm_g, m_mem_kv_w, m_mem_q_norm_g, m_mem_k_norm_g, m_w_br_hgrn,
                           m_w_br_fox, m_w_br_mem, m_w_out, m_norm_ffn_g, m_ffn_w_up, m_ffn_conv_w, m_ffn_conv_b,
                           m_ffn_w_down)))
    v = dict(zip(WEIGHTS, (v_norm_mix_g, v_norm_mem_g, v_w_in, v_hgrn_lb_logits, v_hgrn_norm_g, v_fox_f_bias,
                           v_fox_q_norm_g, v_fox_k_norm_g, v_mem_kv_w, v_mem_q_norm_g, v_mem_k_norm_g, v_w_br_hgrn,
                           v_w_br_fox, v_w_br_mem, v_w_out, v_norm_ffn_g, v_ffn_w_up, v_ffn_conv_w, v_ffn_conv_b,
                           v_ffn_w_down)))
    c_idx = lax.axis_index("c")
    chip = 2 * lax.axis_index("x") + lax.axis_index("y")

    mine = {nm: w[nm][0].astype(MXU_DTYPE) for nm in BIG}

    def gathered_full(names, arrays):
        out = {nm: _put_shard(g, KIND[nm], chip, mine[nm]) for nm, g in zip(names, arrays)}
        return {nm: g.reshape(-1, g.shape[2]) if nm in ROW_SHARDED else g for nm, g in out.items()}

    full = gathered_full(FIRST, _gather_shards([mine[nm] for nm in FIRST], [KIND[nm] for nm in FIRST]))
    late = ([mine[nm] for nm in REST], [KIND[nm] for nm in REST], lambda arrays: gathered_full(REST, arrays))
    cs = ffn_conv_w.shape[2]
    f = cs * N_CHIPS
    placed = lax.dynamic_update_slice(jnp.zeros((3, f), F32), ffn_conv_w[0] * (c_idx == 0).astype(F32), (0, chip * cs))
    conv_w = _unpack_small(_all_sum_small(_pack_small([placed]), "gather_conv_w"), [(3, f)])[0]

    c_arr = jnp.reshape(c_idx, (1,)).astype(jnp.int32)

    def stacked(nm, g):
        return g.reshape(N_CHIPS, -1, g.shape[1]) if nm in ROW_SHARDED else g

    def with_own(landed, partial, kinds):
        return [_put_shard(bq, STACK, chip, _take_shard(p, k, chip)) for bq, p, k in zip(landed, partial, kinds)]

    kinds_rest, kinds_last = [KIND[nm] for nm in REST], [KIND[nm] for nm in LAST]
    state = {}

    def swap_rest(grads):
        gs = [stacked(nm, grads[nm]) for nm in REST]

        def then(from_sibling):
            state["partial_rest"] = [_add_half(g, a, k, c_arr, "add_half_" + nm)
                                     for g, a, k, nm in zip(gs, from_sibling, kinds_rest, REST)]

        return _swap_rider(gs, kinds_rest), then

    def exchange_rest(grads):
        def then(landed):
            state["landed_rest"] = with_own(landed, state["partial_rest"], kinds_rest)

        return _exchange_rider(state["partial_rest"], kinds_rest), then

    def exchange_last(grads):
        gs = [stacked(nm, grads[nm]) for nm in LAST]
        from_sibling = _pair_swap_halves(gs, kinds_last, "pair_swap_halves_last")
        partial = [_add_half(g, a, k, c_arr, "add_half_" + nm) for g, a, k, nm in zip(gs, from_sibling, kinds_last, LAST)]

        def then(landed):
            state["landed_last"] = with_own(landed, partial, kinds_last)

        return _exchange_rider(partial, kinds_last), then

    hooks = {"fox_bwd": swap_rest, "hgrn_bwd": exchange_rest, "in_proj_dx": exchange_last}

    loss_vec, grad_x, grads = _local_step(x, mem, loss_target, w, full, conv_w, late, hooks)

    landed = dict(zip(LAST + REST, state["landed_last"] + state["landed_rest"]))
    reduced_half = [_sum_chips(landed[nm], "sum_chips_" + nm) for nm in BIG]
    joined = [lax.dynamic_update_slice(o, q, (c_idx * q.shape[0], 0))
              for o, q in zip(_pair_join_halves(reduced_half), reduced_half)]
    gshards = dict(zip(BIG, joined))

    small_names = SMALL + ("ffn_conv_w",)
    summed = _unpack_small(
        _all_sum_small(_pack_small([grads[nm] for nm in small_names] + [loss_vec]), "all_sum_small_grads"),
        [grads[nm].shape for nm in small_names] + [loss_vec.shape])
    gsmall = dict(zip(small_names, summed[:-1]))
    loss = jnp.sum(summed[-1])
    g_out = {nm: gshards[nm][None] for nm in BIG}
    for nm in SMALL:
        g_out[nm] = gsmall[nm].reshape(w[nm].shape)
    g_out["ffn_conv_w"] = lax.dynamic_slice(gsmall["ffn_conv_w"], (0, chip * cs), (3, cs))[None]

    delta, new_m, new_v = {}, {}, {}
    for nm in BIG + ("ffn_conv_w",):
        operands = (w[nm], g_out[nm], m[nm], v[nm])
        if nm in TRANSPOSED:
            operands = [jnp.swapaxes(a, 1, 2) for a in operands]
        outs = _adamw(*operands, name="adamw_" + nm)
        delta[nm], new_m[nm], new_v[nm] = [jnp.swapaxes(o, 1, 2) for o in outs] if nm in TRANSPOSED else outs
    packed = [_pack_small([t[nm] for nm in SMALL])[None] for t in (w, g_out, m, v)]
    outs = _adamw(*packed, name="adamw_small")
    shapes = [w[nm].shape for nm in SMALL]
    for res, o in zip((delta, new_m, new_v), outs):
        res.update(zip(SMALL, _unpack_small(o[0], shapes)))

    return (loss, grad_x, *[g_out[nm] for nm in WEIGHTS], *[delta[nm] for nm in WEIGHTS],
            *[new_m[nm] for nm in WEIGHTS], *[new_v[nm] for nm in WEIGHTS])
```

```python
import functools
import math

import jax
import jax.numpy as jnp
from jax import lax
from jax.experimental import pallas as pl
from jax.experimental.pallas import tpu as pltpu

F32 = jnp.float32
BF16 = jnp.bfloat16
MXU_DTYPE = jnp.bfloat16
EXCHANGE_DTYPE = jnp.bfloat16

EPS = 1e-6
HG_HEADS, HG_D = 4, 128
FOX_HEADS, FOX_DH = 8, 64
MEM_HEADS, MEM_DH = 4, 128
HG_CHUNK = 64
FOX_BLOCK = 256
LANE = 128
FFN_GROUP = 256
FLAT_W = 1024
VMEM_LIMIT = 56 * 2 ** 20
NEG = -1e30
N_CHIPS = 4

ADAM_LR, ADAM_B1, ADAM_B2, ADAM_EPS, ADAM_WD, ADAM_STEP = 0.001, 0.9, 0.999, 1e-08, 0.01, 10

MESH = pl.DeviceIdType.MESH
ANY = pl.BlockSpec(memory_space=pl.ANY)


def _mx(x):
    return x.astype(MXU_DTYPE)


def _dot(a, b, ca, cb):
    return lax.dot_general(_mx(a), _mx(b), (((ca,), (cb,)), ((), ())), preferred_element_type=F32)


def _nn(a, b):
    return _dot(a, b, 1, 0)


def _nt(a, b):
    return _dot(a, b, 1, 1)


def _tn(a, b):
    return _dot(a, b, 0, 0)


def _dotp(a, b, ca, cb):
    return lax.dot_general(a, b, (((ca,), (cb,)), ((), ())), precision=lax.Precision.HIGHEST,
                           preferred_element_type=F32)


def _tri_dot(tri_bf, x):
    hi = x.astype(BF16)
    r = x - hi.astype(F32)
    mid = r.astype(BF16)
    lo = (r - mid.astype(F32)).astype(BF16)

    def d(v):
        return lax.dot_general(tri_bf, v, (((1,), (0,)), ((), ())), preferred_element_type=F32)

    return d(hi) + d(mid) + d(lo)


def _sig(x):
    return jax.nn.sigmoid(x)


def _erf(x):
    a = jnp.abs(x)
    t = 1.0 / (1.0 + 0.3275911 * a)
    poly = t * (0.254829592 + t * (-0.284496736 + t * (1.421413741 + t * (-1.453152027 + t * 1.061405429))))
    y = 1.0 - poly * jnp.exp(-a * a)
    return jnp.where(x < 0, -y, y)


def _tile(dim, pref, unit=LANE):
    if dim <= pref:
        return dim
    t = pref - pref % unit
    while t >= unit:
        if dim % t == 0:
            return t
        t -= unit
    return dim


def _params(n_grid):
    return pltpu.CompilerParams(dimension_semantics=("arbitrary",) * n_grid, vmem_limit_bytes=VMEM_LIMIT)


def _acc(ref, val, first):
    @pl.when(first)
    def _():
        ref[...] = val

    @pl.when(jnp.logical_not(first))
    def _():
        ref[...] += val


class _Rider:
    def __init__(self, inputs, out_shapes, scratch, start, finish):
        self.inputs, self.out_shapes, self.scratch, self.start, self.finish = inputs, out_shapes, scratch, start, finish


def _ride(body, rider, n_in, n_out, grid):
    if rider is None:
        return body
    ri, ro, rs = len(rider.inputs), len(rider.out_shapes), len(rider.scratch)

    def wrapped(*refs):
        a, b, c = n_in + ri, n_in + ri + n_out, n_in + ri + n_out + ro
        base = refs[:n_in] + refs[a:b] + refs[c:len(refs) - rs]
        r_in, r_out, r_scr = refs[n_in:a], refs[b:c], refs[len(refs) - rs:]
        step = pl.program_id(0)
        for ax in range(1, len(grid)):
            step = step * grid[ax] + pl.program_id(ax)

        @pl.when(step == 0)
        def _():
            rider.start(r_in, r_out, r_scr)

        body(*base)

        @pl.when(step == math.prod(grid) - 1)
        def _():
            rider.finish(r_in, r_out, r_scr)

    return wrapped


def _ride_call(body, rider, *, name, grid, in_specs, out_specs, out_shape, scratch, args, aliases=None):
    n_in, n_out = len(in_specs), len(out_specs)
    aliases = aliases or {}
    if rider is None:
        outs = pl.pallas_call(body, name=name, grid=grid, in_specs=in_specs, out_specs=out_specs, out_shape=out_shape,
                              scratch_shapes=scratch, input_output_aliases=aliases,
                              compiler_params=_params(len(grid)))(*args)
        return list(outs), None
    outs = pl.pallas_call(
        _ride(body, rider, n_in, n_out, grid), name=name, grid=grid,
        in_specs=list(in_specs) + [ANY] * len(rider.inputs), out_specs=list(out_specs) + [ANY] * len(rider.out_shapes),
        out_shape=list(out_shape) + list(rider.out_shapes), scratch_shapes=list(scratch) + list(rider.scratch),
        input_output_aliases=aliases, compiler_params=_params(len(grid)),
    )(*args, *rider.inputs)
    return list(outs[:n_out]), list(outs[n_out:])


def _matmul(a, b, *, name, ta=False, tb=False, tm=1024, tn=2048, tk=None, rider=None):
    m, k = (a.shape[1], a.shape[0]) if ta else a.shape
    n = b.shape[0] if tb else b.shape[1]
    tk = tk or (1024 if ta else 2048)
    tm, tn, tk = _tile(m, tm), _tile(n, tn), _tile(k, tk)
    nk = k // tk

    def body(a_ref, b_ref, o_ref):
        p = _dot(a_ref[...], b_ref[...], 0 if ta else 1, 1 if tb else 0)
        if nk == 1:
            o_ref[...] = p
        else:
            _acc(o_ref, p, pl.program_id(2) == 0)

    a_spec = pl.BlockSpec((tk, tm), lambda i, j, kk: (kk, i)) if ta else pl.BlockSpec((tm, tk), lambda i, j, kk: (i, kk))
    b_spec = pl.BlockSpec((tn, tk), lambda i, j, kk: (j, kk)) if tb else pl.BlockSpec((tk, tn), lambda i, j, kk: (kk, j))
    outs, extra = _ride_call(
        body, rider, name=name, grid=(m // tm, n // tn, nk), in_specs=[a_spec, b_spec],
        out_specs=[pl.BlockSpec((tm, tn), lambda i, j, kk: (i, j))], out_shape=[jax.ShapeDtypeStruct((m, n), F32)],
        scratch=[], args=(a, b))
    return (outs[0], extra) if rider else outs[0]


def _matmul_rows(a_parts, b, *, name, tb, row_ins, vec_ins, epilogue, n_vec_out, tm=512, tk=2048, rider=None):
    m, kp = a_parts[0].shape
    n = b.shape[0] if tb else b.shape[1]
    tm, tk = _tile(m, tm, 8), _tile(kp, tk)
    nk = kp // tk
    n_a, n_row, n_vec = len(a_parts), len(row_ins), len(vec_ins)

    def body(*refs):
        a_refs, b_refs = refs[:n_a], refs[n_a:2 * n_a]
        rows = refs[2 * n_a:2 * n_a + n_row]
        vecs = refs[2 * n_a + n_row:2 * n_a + n_row + n_vec]
        o_ref = refs[2 * n_a + n_row + n_vec]
        v_refs = refs[2 * n_a + n_row + n_vec + 1:-1]
        acc_ref = refs[-1]
        i, kk = pl.program_id(0), pl.program_id(1)
        p = _dot(a_refs[0][...], b_refs[0][...], 1, 1 if tb else 0)
        for a_ref, b_ref in zip(a_refs[1:], b_refs[1:]):
            p = p + _dot(a_ref[...], b_ref[...], 1, 1 if tb else 0)
        _acc(acc_ref, p, kk == 0)

        @pl.when(kk == nk - 1)
        def _():
            out, vouts = epilogue(acc_ref[...], *[r[...] for r in rows], *[v[...] for v in vecs])
            o_ref[...] = out
            for v_ref, v in zip(v_refs, vouts):
                _acc(v_ref, v, i == 0)

    a_spec = pl.BlockSpec((tm, tk), lambda i, kk: (i, kk))
    b_specs = [pl.BlockSpec((n, tk), functools.partial(lambda i, kk, q: (0, q * nk + kk), q=q)) if tb else
               pl.BlockSpec((tk, n), functools.partial(lambda i, kk, q: (q * nk + kk, 0), q=q)) for q in range(n_a)]
    row = pl.BlockSpec((tm, n), lambda i, kk: (i, 0))
    vec = pl.BlockSpec((1, n), lambda i, kk: (0, 0))
    outs, extra = _ride_call(
        body, rider, name=name, grid=(m // tm, nk),
        in_specs=[a_spec] * n_a + b_specs + [row] * n_row + [vec] * n_vec,
        out_specs=[row] + [vec] * n_vec_out,
        out_shape=[jax.ShapeDtypeStruct((m, n), F32)] + [jax.ShapeDtypeStruct((1, n), F32)] * n_vec_out,
        scratch=[pltpu.VMEM((tm, n), F32)], args=(*a_parts, *([b] * n_a), *row_ins, *vec_ins))
    return outs[0], outs[1:], extra


def _norm_bwd_epilogue(n_dh):
    def epilogue(dh, x, res, *rest):
        for extra in rest[:n_dh]:
            dh = dh + extra
        g = rest[n_dh]
        r = lax.rsqrt(jnp.mean(x * x, axis=-1, keepdims=True) + EPS)
        dhg = dh * g
        dx = res + r * dhg - x * (r * r * r) * jnp.mean(dhg * x, axis=-1, keepdims=True)
        return dx, [jnp.sum(dh * x * r, axis=0, keepdims=True)]

    return epilogue


def _loss_epilogue(y, x1, target):
    d = y.shape[1]
    err = x1 + y - target
    return err * (1.0 / d), [jnp.sum(err * err, axis=0, keepdims=True) * (0.5 / d)]


def _rmsnorm_fwd(x, g, *, name, tm=512):
    n, d = x.shape
    tm = _tile(n, tm, 8)

    def body(x_ref, g_ref, o_ref):
        xv = x_ref[...]
        r = lax.rsqrt(jnp.mean(xv * xv, axis=-1, keepdims=True) + EPS)
        o_ref[...] = (xv * r * g_ref[...]).astype(o_ref.dtype)

    return pl.pallas_call(
        body, name=name, grid=(n // tm,),
        in_specs=[pl.BlockSpec((tm, d), lambda i: (i, 0)), pl.BlockSpec((1, d), lambda i: (0, 0))],
        out_specs=pl.BlockSpec((tm, d), lambda i: (i, 0)),
        out_shape=jax.ShapeDtypeStruct((n, d), MXU_DTYPE),
        compiler_params=_params(1),
    )(x, g)


def _rmsnorm_bwd(x, dhs, g, res, *, name, tm=512):
    n, d = x.shape
    tm = _tile(n, tm, 8)
    n_dh = len(dhs)
    has_res = res is not None

    def body(*refs):
        x_ref, dh_refs, g_ref = refs[0], refs[1:1 + n_dh], refs[1 + n_dh]
        res_ref = refs[2 + n_dh] if has_res else None
        dx_ref, dg_ref = refs[-2], refs[-1]
        xv = x_ref[...]
        dh = dh_refs[0][...].astype(F32)
        for r_ in dh_refs[1:]:
            dh = dh + r_[...].astype(F32)
        r = lax.rsqrt(jnp.mean(xv * xv, axis=-1, keepdims=True) + EPS)
        dhg = dh * g_ref[...]
        dx = r * dhg - xv * (r * r * r) * jnp.mean(dhg * xv, axis=-1, keepdims=True)
        if has_res:
            dx = dx + res_ref[...]
        dx_ref[...] = dx
        _acc(dg_ref, jnp.sum(dh * xv * r, axis=0, keepdims=True), pl.program_id(0) == 0)

    row = pl.BlockSpec((tm, d), lambda i: (i, 0))
    vec = pl.BlockSpec((1, d), lambda i: (0, 0))
    ins = [x] + list(dhs) + [g] + ([res] if has_res else [])
    return pl.pallas_call(
        body, name=name, grid=(n // tm,),
        in_specs=[row] * (1 + n_dh) + [vec] + ([row] if has_res else []),
        out_specs=[row, vec],
        out_shape=[jax.ShapeDtypeStruct((n, d), F32), jax.ShapeDtypeStruct((1, d), F32)],
        compiler_params=_params(1),
    )(*ins)


def _adamw(w, g, m, v, *, name, tr=256):
    _, r, c = w.shape
    c1 = 1.0 / (1.0 - ADAM_B1 ** ADAM_STEP)
    c2 = 1.0 / (1.0 - ADAM_B2 ** ADAM_STEP)

    def body(w_ref, g_ref, m_ref, v_ref, d_ref, mo_ref, vo_ref):
        gv = g_ref[...]
        mn = ADAM_B1 * m_ref[...] + (1.0 - ADAM_B1) * gv
        vn = ADAM_B2 * v_ref[...] + (1.0 - ADAM_B2) * (gv * gv)
        d_ref[...] = -ADAM_LR * ((mn * c1) / (jnp.sqrt(vn * c2) + ADAM_EPS) + ADAM_WD * w_ref[...])
        mo_ref[...] = mn
        vo_ref[...] = vn

    if r % 8 == 0 or r < 8:
        tr = _tile(r, tr, 8)
        grid, blk = (r // tr,), pl.BlockSpec((1, tr, c), lambda i: (0, i, 0))
    else:
        tc = _tile(c, tr)
        grid, blk = (c // tc,), pl.BlockSpec((1, r, tc), lambda i: (0, 0, i))
    sds = jax.ShapeDtypeStruct((1, r, c), F32)
    return pl.pallas_call(
        body, name=name, grid=grid, in_specs=[blk] * 4, out_specs=[blk] * 3, out_shape=[sds] * 3,
        compiler_params=_params(1),
    )(w, g, m, v)


def _bdot(a, b, ca, cb):
    return lax.dot_general(_mx(a), _mx(b), (((ca,), (cb,)), ((0,), (0,))), preferred_element_type=F32)


def _bdotp(a, b, ca, cb):
    return lax.dot_general(a, b, (((ca,), (cb,)), ((0,), (0,))), precision=lax.Precision.HIGHEST,
                           preferred_element_type=F32)


def _tri_dot_b(tri_bf, x):
    hi = x.astype(BF16)
    r = x - hi.astype(F32)
    mid = r.astype(BF16)
    lo = (r - mid.astype(F32)).astype(BF16)

    def d(v):
        return lax.dot_general(tri_bf, v, (((2,), (1,)), ((0,), (0,))), preferred_element_type=F32)

    return d(hi) + d(mid) + d(lo)


def _hgrn_forward(hq, hf, hi, lbv, tril, tril_bf):
    nc, c, _ = hq.shape
    sf = _sig(hf)
    f = lbv + (1.0 - lbv) * sf
    k = 1.0 - f
    gcum = _tri_dot_b(tril_bf, jnp.log(f))
    mid = gcum[:, c // 2 - 1:c // 2, :]
    glast = gcum[:, c - 1:c, :]
    sq = _sig(hq)
    q = hq * sq
    e_q = jnp.exp(gcum - mid)
    e_k = jnp.exp(mid - gcum)
    qe, ke = q * e_q, k * e_k
    a = jnp.where(tril, _bdot(qe, ke, 2, 2), 0.0)
    e_g = jnp.exp(gcum)
    qg = q * e_g
    e_s = jnp.exp(glast - gcum)
    kg = k * e_s
    e_l = jnp.exp(glast)
    upd = _bdot(hi, kg, 1, 1)
    st = jnp.zeros((HG_D, HG_D), F32)
    states = []
    for n in range(nc):
        states.append(st)
        st = st * e_l[n] + upd[n]
    st_all = jnp.stack(states)
    o = _bdot(a, hi, 2, 1) + _bdot(qg, st_all, 2, 2)
    return dict(sf=sf, f=f, k=k, sq=sq, q=q, e_q=e_q, e_k=e_k, qe=qe, ke=ke, a=a, e_g=e_g, qg=qg, o=o,
                e_s=e_s, kg=kg, e_l=e_l, st_all=st_all)


def _hgrn_specs(t_, col0):
    def col(off):
        return pl.BlockSpec((1, t_, LANE), lambda h, b: (b, 0, col0 + 4 * h + off))

    vec = pl.BlockSpec((2, LANE), lambda h, b: (0, h))
    one = pl.BlockSpec((1, LANE), lambda h, b: (0, 0))
    blk = pl.BlockSpec((1, t_, LANE), lambda h, b: (b, 0, h))
    return col, vec, one, blk


def _chunk_masks(nc, c):
    row = lax.broadcasted_iota(jnp.int32, (nc, c, c), 1)
    cl = lax.broadcasted_iota(jnp.int32, (nc, c, c), 2)
    return row >= cl, (row >= cl).astype(BF16), (row <= cl).astype(BF16)


def _hgrn_fwd(zm, lb, gn, hw, col0):
    b_, t_, _ = zm.shape
    c = min(HG_CHUNK, t_)
    nc = t_ // c
    col, vec, one, blk = _hgrn_specs(t_, col0)

    def body(q_ref, f_ref, i_ref, g_ref, lb_ref, gn_ref, y_ref):
        lbv, gnv = _sig(lb_ref[0:1, :] - lb_ref[1:2, :]), gn_ref[...]
        tril, tril_bf, _ = _chunk_masks(nc, c)
        chunks = lambda ref: ref[0].reshape(nc, c, LANE)
        o = _hgrn_forward(chunks(q_ref), chunks(f_ref), chunks(i_ref), lbv, tril, tril_bf)["o"]
        r = lax.rsqrt(jnp.mean(o * o, axis=-1, keepdims=True) + EPS)
        hg = chunks(g_ref)
        y_ref[0] = (o * r * gnv * (hg * _sig(hg))).reshape(t_, LANE)

    return pl.pallas_call(
        body, name="hgrn_fwd", grid=(HG_HEADS, b_),
        in_specs=[col(0), col(1), col(2), col(3), vec, one], out_specs=blk,
        out_shape=jax.ShapeDtypeStruct((b_, t_, hw), F32),
        compiler_params=_params(2),
    )(zm, zm, zm, zm, lb, gn)


def _hgrn_bwd(zm, dy, lb, gn, hw, col0, dz, rider=None):
    b_, t_, _ = zm.shape
    c = min(HG_CHUNK, t_)
    nc = t_ // c
    col, vec, one, blk = _hgrn_specs(t_, col0)

    def body(q_ref, f_ref, i_ref, g_ref, dy_ref, lb_ref, gn_ref, _, dz_ref, dlb_ref, dgn_ref):
        h, b = pl.program_id(0), pl.program_id(1)
        lbv, gnv = _sig(lb_ref[0:1, :] - lb_ref[1:2, :]), gn_ref[...]
        tril, tril_bf, triu_bf = _chunk_masks(nc, c)
        last_row = lax.broadcasted_iota(jnp.int32, (nc, c, LANE), 1) == c - 1
        chunks = lambda ref: ref[0].reshape(nc, c, LANE)
        flat = lambda x: x.reshape(t_, LANE)
        hq, hi, hg = chunks(q_ref), chunks(i_ref), chunks(g_ref)
        p = _hgrn_forward(hq, chunks(f_ref), hi, lbv, tril, tril_bf)
        o, q, k, st_all, e_l = p["o"], p["q"], p["k"], p["st_all"], p["e_l"]
        dyv = chunks(dy_ref)
        sg = _sig(hg)
        r = lax.rsqrt(jnp.mean(o * o, axis=-1, keepdims=True) + EPS)
        dn = dyv * (hg * sg)
        dz_ref[0, :, 3 * LANE:] = flat(dyv * (o * r * gnv) * (sg * (1.0 + hg * (1.0 - sg)))).astype(dz_ref.dtype)
        dgn = jnp.sum(flat(dn * o * r), axis=0, keepdims=True)
        dng = dn * gnv
        do = r * dng - o * (r * r * r) * jnp.mean(dng * o, axis=-1, keepdims=True)
        back = _bdotp(do, p["qg"], 1, 1)
        dst = jnp.zeros((HG_D, HG_D), F32)
        dsts = [None] * nc
        for n in range(nc - 1, -1, -1):
            dsts[n] = dst
            dst = dst * e_l[n] + back[n]
        dst_all = jnp.stack(dsts)
        da = jnp.where(tril, _bdotp(do, hi, 2, 2), 0.0)
        dq = _bdotp(da, p["ke"], 2, 1) * p["e_q"] + _bdotp(do, st_all, 2, 1) * p["e_g"]
        dk_state = _bdotp(hi, dst_all, 2, 1) * p["e_s"]
        dk = _bdotp(da, p["qe"], 1, 1) * p["e_k"] + dk_state
        dz_ref[0, :, 2 * LANE:3 * LANE] = flat(_bdot(p["a"], do, 1, 1) + _bdot(p["kg"], dst_all, 2, 2)).astype(dz_ref.dtype)
        extra = (jnp.sum(k * dk_state, axis=1, keepdims=True) + e_l * jnp.sum(st_all * dst_all, axis=1, keepdims=True))
        dgc = q * dq - k * dk + jnp.where(last_row, extra, 0.0)
        dfv = _tri_dot_b(triu_bf, dgc) / p["f"] - dk
        sf, sq = p["sf"], p["sq"]
        dz_ref[0, :, LANE:2 * LANE] = flat(dfv * (1.0 - lbv) * sf * (1.0 - sf)).astype(dz_ref.dtype)
        dlb = jnp.sum(flat(dfv * (1.0 - sf)), axis=0, keepdims=True)
        dz_ref[0, :, :LANE] = flat(dq * (sq * (1.0 + hq * (1.0 - sq)))).astype(dz_ref.dtype)
        dl0 = dlb * lbv * (1.0 - lbv)
        _acc(dlb_ref, jnp.concatenate([dl0, -dl0], axis=0), b == 0)
        _acc(dgn_ref, dgn, jnp.logical_and(b == 0, h == 0))

    return _ride_call(
        body, rider, name="hgrn_bwd", grid=(HG_HEADS, b_),
        in_specs=[col(0), col(1), col(2), col(3), blk, vec, one, ANY],
        out_specs=[pl.BlockSpec((1, t_, 4 * LANE), lambda h, b: (b, 0, col0 // 4 + h)), vec, one],
        out_shape=[jax.ShapeDtypeStruct(dz.shape, dz.dtype), jax.ShapeDtypeStruct((2, hw), F32),
                   jax.ShapeDtypeStruct((1, LANE), F32)],
        scratch=[], args=(zm, zm, zm, zm, dy, lb, gn, dz), aliases={7: 0})


def _fox_logf(x):
    return jnp.minimum(x, 0.0) - jnp.log(1.0 + jnp.exp(-jnp.abs(x)))


def _fox_prep(zf, bias):
    b_, t_, _ = zf.shape
    tb = min(FOX_BLOCK, t_)
    nb = t_ // tb

    def body(z_ref, b_ref, fc_ref):
        tril_bf = (lax.broadcasted_iota(jnp.int32, (tb, tb), 0) >= lax.broadcasted_iota(jnp.int32, (tb, tb), 1)).astype(BF16)
        bv = b_ref[...]

        def blk(i, carry):
            rows = pl.ds(pl.multiple_of(i * tb, tb), tb)
            fc = _tri_dot(tril_bf, _fox_logf(z_ref[0, rows, :] + bv)) + carry
            fc_ref[0, rows, :] = fc
            return fc[tb - 1:tb, :]

        lax.fori_loop(0, nb, blk, jnp.zeros((1, LANE), F32))

    blk_spec = pl.BlockSpec((1, t_, LANE), lambda b: (b, 0, 0))
    return pl.pallas_call(
        body, name="fox_prep", grid=(b_,),
        in_specs=[blk_spec, pl.BlockSpec((1, LANE), lambda b: (0, 0))], out_specs=blk_spec,
        out_shape=jax.ShapeDtypeStruct((b_, t_, LANE), F32), compiler_params=_params(1),
    )(zf, bias)


def _fox_post(dfc, zf, bias):
    b_, t_, _ = zf.shape
    npair = dfc.shape[1]
    tb = min(FOX_BLOCK, t_)
    nb = t_ // tb

    def body(d_ref, z_ref, b_ref, dz_ref, db_ref):
        triu_bf = (lax.broadcasted_iota(jnp.int32, (tb, tb), 0) <= lax.broadcasted_iota(jnp.int32, (tb, tb), 1)).astype(BF16)
        valid = lax.broadcasted_iota(jnp.int32, (tb, LANE), 1) < FOX_HEADS
        bv = b_ref[...]

        def blk(m, carry):
            tail, db = carry
            rows = pl.ds(pl.multiple_of((nb - 1 - m) * tb, tb), tb)
            dfc_rows = d_ref[0, 0, rows, :]
            for p in range(1, npair):
                dfc_rows = dfc_rows + pltpu.roll(d_ref[0, p, rows, :], 2 * p, 1)
            dlf = _tri_dot(triu_bf, dfc_rows) + tail
            dx = jnp.where(valid, dlf * _sig(-(z_ref[0, rows, :] + bv)), 0.0)
            dz_ref[0, rows, :] = dx.astype(dz_ref.dtype)
            return dlf[0:1, :], db + jnp.sum(dx, axis=0, keepdims=True)

        z1 = jnp.zeros((1, LANE), F32)
        _, db = lax.fori_loop(0, nb, blk, (z1, z1))
        _acc(db_ref, db, pl.program_id(0) == 0)

    blk_spec = pl.BlockSpec((1, t_, LANE), lambda b: (b, 0, 0))
    vec = pl.BlockSpec((1, LANE), lambda b: (0, 0))
    return pl.pallas_call(
        body, name="fox_post", grid=(b_,),
        in_specs=[pl.BlockSpec((1, npair, t_, LANE), lambda b: (b, 0, 0, 0)), blk_spec, vec], out_specs=[blk_spec, vec],
        out_shape=[jax.ShapeDtypeStruct((b_, t_, LANE), MXU_DTYPE), jax.ShapeDtypeStruct((1, LANE), F32)],
        compiler_params=_params(1),
    )(dfc, zf, bias)


FOX_TILE = 256
FOX_BAND = 512
AUG = 64


def _head_mean_matrix():
    r = lax.broadcasted_iota(jnp.int32, (LANE, LANE), 0) // FOX_DH
    c = lax.broadcasted_iota(jnp.int32, (LANE, LANE), 1) // FOX_DH
    return (r == c).astype(BF16)


def _dot_right_exact(x, m_bf):
    hi = x.astype(BF16)
    r = x - hi.astype(F32)
    mid = r.astype(BF16)
    lo = (r - mid.astype(F32)).astype(BF16)

    def d(v):
        return lax.dot_general(v, m_bf, (((1,), (0,)), ((), ())), preferred_element_type=F32)

    return d(hi) + d(mid) + d(lo)


def _pair_norm(x, g2, bd):
    r = lax.rsqrt(_dot_right_exact(x * x, bd) * (1.0 / FOX_DH) + EPS)
    return x * r * g2, r


def _pair_norm_bwd(x, r, dy, g2, bd):
    dyg = dy * g2
    dx = r * dyg - x * (r * r * r) * (_dot_right_exact(dyg * x, bd) * (1.0 / FOX_DH))
    return dx, jnp.sum(dy * x * r, axis=0, keepdims=True)


def _head_lanes(xn, hh):
    return xn if hh == 0 else pltpu.roll(xn, FOX_DH, 1)


def _split3(x):
    hi = x.astype(BF16).astype(F32)
    mid = (x - hi).astype(BF16).astype(F32)
    return hi, mid, x - hi - mid


def _fox_operands(q_ref, k_ref, v_ref, fc_ref, gq2, gk2, p, qa, ka, va):
    t_ = q_ref.shape[1]
    bd = _head_mean_matrix()
    lane = lax.broadcasted_iota(jnp.int32, (t_, LANE), 1)
    qx, kx = q_ref[0], k_ref[0]
    qn, rq = _pair_norm(qx, gq2, bd)
    kn, rk = _pair_norm(kx, gk2, bd)
    vv = v_ref[0]
    q_aug = jnp.where(jnp.logical_and(lane >= AUG, lane < AUG + 3), 1.0, 0.0)
    for hh in range(2):
        fcol = jnp.sum(jnp.where(lane == 2 * p + hh, fc_ref[0], 0.0), axis=-1, keepdims=True)
        hi, mid, lo = _split3(-fcol)
        k_aug = jnp.where(lane == AUG, hi, jnp.where(lane == AUG + 1, mid, jnp.where(lane == AUG + 2, lo,
                          jnp.where(lane == AUG + 3, 1.0, 0.0))))
        head = lane < FOX_DH
        qa[hh] = jnp.where(head, _head_lanes(qn, hh), q_aug).astype(MXU_DTYPE)
        ka[hh] = jnp.where(head, _head_lanes(kn, hh), k_aug).astype(MXU_DTYPE)
        va[hh] = jnp.where(head, _head_lanes(vv, hh), 0.0).astype(MXU_DTYPE)
    return bd, lane, qx, kx, rq, rk


def _fox_specs(t_, fw, col0):
    npair = fw // LANE

    def col(off):
        return pl.BlockSpec((1, t_, LANE), lambda b, p: (b, 0, col0 + 3 * p + off))

    pair = pl.BlockSpec((1, t_, LANE), lambda b, p: (b, 0, p))
    full = pl.BlockSpec((1, t_, LANE), lambda b, p: (b, 0, 0))
    gvec = pl.BlockSpec((1, LANE), lambda b, p: (0, 0))
    lse = pl.BlockSpec((1, 1, t_, LANE), lambda b, p: (b, p, 0, 0))
    return col, pair, full, gvec, lse


def _fox_fwd(zm, fc, gq2, gk2, fw, col0):
    b_, t_, _ = zm.shape
    npair = fw // LANE
    tq = min(FOX_TILE, t_)
    bw = min(FOX_BAND, t_)
    nband, tpb = t_ // bw, bw // tq
    scale = FOX_DH ** -0.5
    col, pair, full, gvec, lse_spec = _fox_specs(t_, fw, col0)

    def body(q_ref, k_ref, v_ref, fc_ref, gq_ref, gk_ref, o_ref, lse_ref, qa, ka, va):
        p = pl.program_id(1)
        _fox_operands(q_ref, k_ref, v_ref, fc_ref, gq_ref[...] * scale, gk_ref[...], p, qa, ka, va)
        ahead = lax.broadcasted_iota(jnp.int32, (tq, bw), 1) - lax.broadcasted_iota(jnp.int32, (tq, bw), 0)
        lane = lax.broadcasted_iota(jnp.int32, (tq, LANE), 1)

        for band in range(nband):
            c0 = band * bw

            def qtile(ii, _, c0=c0):
                r0 = pl.multiple_of(c0 + ii * tq, tq)
                rows = pl.ds(r0, tq)
                keep = ahead <= r0 - c0
                res = []
                for hh in range(2):
                    qb = qa[hh, rows, :]
                    s_b = jnp.where(keep, _nt(qb, ka[hh, c0:c0 + bw, :]), NEG)
                    m = jnp.max(s_b, axis=-1, keepdims=True)
                    if c0:
                        s_a = _nt(qb, ka[hh, 0:c0, :])
                        m = jnp.maximum(m, jnp.max(s_a, axis=-1, keepdims=True))
                    p_b = jnp.exp(s_b - m)
                    l = jnp.sum(p_b, axis=-1, keepdims=True)
                    acc = _nn(p_b, va[hh, c0:c0 + bw, :])
                    if c0:
                        p_a = jnp.exp(s_a - m)
                        l = l + jnp.sum(p_a, axis=-1, keepdims=True)
                        acc = acc + _nn(p_a, va[hh, 0:c0, :])
                    res.append((acc / l, m + jnp.log(l)))
                (o0, e0), (o1, e1) = res
                o_ref[0, rows, :] = jnp.where(lane < FOX_DH, o0, pltpu.roll(o1, FOX_DH, 1))
                lse_ref[0, 0, rows, :] = jnp.where(lane == 0, e0, jnp.where(lane == 1, e1, 0.0))
                return 0

            lax.fori_loop(0, tpb, qtile, 0)

    return pl.pallas_call(
        body, name="fox_fwd", grid=(b_, npair),
        in_specs=[col(0), col(1), col(2), full, gvec, gvec],
        out_specs=[pair, lse_spec],
        out_shape=[jax.ShapeDtypeStruct((b_, t_, fw), F32), jax.ShapeDtypeStruct((b_, npair, t_, LANE), F32)],
        scratch_shapes=[pltpu.VMEM((2, t_, LANE), MXU_DTYPE)] * 3,
        compiler_params=_params(2),
    )(zm, zm, zm, fc, gq2, gk2)


def _norm_bwd(x, dy, g):
    r = lax.rsqrt(jnp.mean(x * x, axis=-1, keepdims=True) + EPS)
    dyg = dy * g
    dx = r * dyg - x * (r * r * r) * jnp.mean(dyg * x, axis=-1, keepdims=True)
    return dx, jnp.sum(dy * x * r, axis=0, keepdims=True)


def _fox_bwd(zm, o, do, lse, fc, gq2, gk2, fw, col0, dz, rider=None):
    b_, t_, _ = zm.shape
    npair = fw // LANE
    tq = min(FOX_TILE, t_)
    nb = t_ // tq
    bw = min(FOX_BAND, t_)
    nband, tpb = t_ // bw, bw // tq
    scale = FOX_DH ** -0.5
    col, pair, full, gvec, lse_spec = _fox_specs(t_, fw, col0)

    def body(q_ref, k_ref, v_ref, o_ref, do_ref, lse_ref, fc_ref, gq_ref, gk_ref, _,
             dz_ref, dfc_ref, dgq_ref, dgk_ref, qa, ka, va, da, rowv, dq_acc, dk_acc, dv_acc):
        b, p = pl.program_id(0), pl.program_id(1)
        gq2v, gk2v = gq_ref[...] * scale, gk_ref[...]
        bd, lane, qx, kx, rq, rk = _fox_operands(q_ref, k_ref, v_ref, fc_ref, gq2v, gk2v, p, qa, ka, va)
        head = lane < FOX_DH
        dov = do_ref[0]
        dsum = _dot_right_exact(dov * o_ref[0], bd)
        eye = (lax.broadcasted_iota(jnp.int32, (tq, tq), 0) == lax.broadcasted_iota(jnp.int32, (tq, tq), 1)).astype(F32)
        for hh in range(2):
            da[hh] = jnp.where(head, _head_lanes(dov, hh), 0.0).astype(MXU_DTYPE)
            for blk in range(nb):
                rs = slice(blk * tq, (blk + 1) * tq)
                rowv[2 * hh:2 * hh + 1, rs] = jnp.sum(eye * lse_ref[0, 0, rs, hh:hh + 1], axis=0, keepdims=True)
                rowv[2 * hh + 1:2 * hh + 2, rs] = jnp.sum(eye * dsum[rs, hh * FOX_DH:hh * FOX_DH + 1], axis=0, keepdims=True)
        dq_acc[...] = jnp.zeros(dq_acc.shape, F32)
        ahead = lax.broadcasted_iota(jnp.int32, (tq, bw), 1) - lax.broadcasted_iota(jnp.int32, (tq, bw), 0)

        def part(hh, kb, vb, lo, hi, keep):
            qm, dm = qa[hh, lo:hi, :], da[hh, lo:hi, :]
            pt = jnp.exp(_nt(kb, qm) - rowv[2 * hh:2 * hh + 1, lo:hi])
            if keep is not None:
                pt = jnp.where(keep, pt, 0.0)
            dst = pt * (_nt(vb, dm) - rowv[2 * hh + 1:2 * hh + 2, lo:hi])
            dq_acc[hh, lo:hi, :] += _tn(dst, kb)
            return _nn(dst, qm), _nn(pt, dm)

        for band in range(nband):
            c0 = band * bw

            def kvtile(jj, _, c0=c0):
                r0 = pl.multiple_of(c0 + jj * tq, tq)
                rows = pl.ds(r0, tq)
                keep = ahead >= r0 - c0
                for hh in range(2):
                    kb, vb = ka[hh, rows, :], va[hh, rows, :]
                    dk_t, dv_t = part(hh, kb, vb, c0, c0 + bw, keep)
                    if c0 + bw < t_:
                        dk_u, dv_u = part(hh, kb, vb, c0 + bw, t_, None)
                        dk_t, dv_t = dk_t + dk_u, dv_t + dv_u
                    dk_acc[hh, rows, :] = dk_t
                    dv_acc[hh, rows, :] = dv_t
                return 0

            lax.fori_loop(0, tpb, kvtile, 0)

        dq0, dq1, dk0, dk1 = dq_acc[0], dq_acc[1], dk_acc[0], dk_acc[1]
        dqn = jnp.where(head, dq0, pltpu.roll(dq1, FOX_DH, 1))
        dkn = jnp.where(head, dk0, pltpu.roll(dk1, FOX_DH, 1))
        dqx, gq_part = _pair_norm_bwd(qx, rq, dqn, gq2v, bd)
        dkx, gk_part = _pair_norm_bwd(kx, rk, dkn, gk2v, bd)
        dz_ref[0, :, :LANE] = dqx.astype(dz_ref.dtype)
        dz_ref[0, :, LANE:2 * LANE] = dkx.astype(dz_ref.dtype)
        dz_ref[0, :, 2 * LANE:] = jnp.where(head, dv_acc[0], pltpu.roll(dv_acc[1], FOX_DH, 1)).astype(dz_ref.dtype)

        def bias_grad(dqh, dkh):
            return (jnp.sum(jnp.where(lane == AUG + 3, dqh, 0.0), axis=-1, keepdims=True)
                    - jnp.sum(jnp.where(lane == AUG, dkh, 0.0), axis=-1, keepdims=True))

        dfc_ref[0, 0] = jnp.where(lane == 0, bias_grad(dq0, dk0), jnp.where(lane == 1, bias_grad(dq1, dk1), 0.0))
        first = jnp.logical_and(b == 0, p == 0)
        _acc(dgq_ref, gq_part * scale, first)
        _acc(dgk_ref, gk_part, first)

    gs = jax.ShapeDtypeStruct((1, LANE), F32)
    return _ride_call(
        body, rider, name="fox_bwd", grid=(b_, npair),
        in_specs=[col(0), col(1), col(2), pair, pair, lse_spec, full, gvec, gvec, ANY],
        out_specs=[pl.BlockSpec((1, t_, 3 * LANE), lambda b, p: (b, 0, col0 // 3 + p)), lse_spec, gvec, gvec],
        out_shape=[jax.ShapeDtypeStruct(dz.shape, dz.dtype), jax.ShapeDtypeStruct((b_, npair, t_, LANE), F32), gs, gs],
        scratch=[pltpu.VMEM((2, t_, LANE), MXU_DTYPE)] * 4
        + [pltpu.VMEM((8, t_), F32)] + [pltpu.VMEM((2, t_, LANE), F32)] * 3,
        args=(zm, zm, zm, o, do, lse, fc, gq2, gk2, dz), aliases={9: 0})


def _mem_specs(t_, m_, mw, col0):
    nh = mw // LANE
    qcol = pl.BlockSpec((1, t_, LANE), lambda b, h: (b, 0, col0 + h))
    kcol = pl.BlockSpec((1, m_, LANE), lambda b, h: (b, 0, h))
    vcol = pl.BlockSpec((1, m_, LANE), lambda b, h: (b, 0, nh + h))
    ycol = pl.BlockSpec((1, t_, LANE), lambda b, h: (b, 0, h))
    gvec = pl.BlockSpec((1, LANE), lambda b, h: (0, 0))
    return qcol, kcol, vcol, ycol, gvec


def _mem_fwd(zm, mkv, gq, gk, mw, col0):
    b_, t_, _ = zm.shape
    m_ = mkv.shape[1]
    tq = min(512, t_)
    nb = t_ // tq
    scale = MEM_DH ** -0.5
    qcol, kcol, vcol, ycol, gvec = _mem_specs(t_, m_, mw, col0)

    def body(q_ref, k_ref, v_ref, gq_ref, gk_ref, y_ref):
        gqv, gkv = gq_ref[...] * scale, gk_ref[...]
        kv = k_ref[0]
        kn = _mx(kv * lax.rsqrt(jnp.mean(kv * kv, axis=-1, keepdims=True) + EPS) * gkv)
        vv = _mx(v_ref[0])

        def blk(i, _):
            rows = pl.ds(pl.multiple_of(i * tq, tq), tq)
            qv = q_ref[0, rows, :]
            s = _nt(qv * lax.rsqrt(jnp.mean(qv * qv, axis=-1, keepdims=True) + EPS) * gqv, kn)
            e = jnp.exp(s - jnp.max(s, axis=-1, keepdims=True))
            y_ref[0, rows, :] = _nn(e / jnp.sum(e, axis=-1, keepdims=True), vv)
            return 0

        lax.fori_loop(0, nb, blk, 0)

    return pl.pallas_call(
        body, name="mem_fwd", grid=(b_, MEM_HEADS), in_specs=[qcol, kcol, vcol, gvec, gvec], out_specs=ycol,
        out_shape=jax.ShapeDtypeStruct((b_, t_, mw), F32), compiler_params=_params(2),
    )(zm, mkv, mkv, gq, gk)


def _mem_bwd(zm, mkv, dy, gq, gk, mw, col0, dz):
    b_, t_, _ = zm.shape
    m_ = mkv.shape[1]
    tq = min(512, t_)
    nb = t_ // tq
    scale = MEM_DH ** -0.5
    qcol, kcol, vcol, ycol, gvec = _mem_specs(t_, m_, mw, col0)

    def body(q_ref, k_ref, v_ref, dy_ref, gq_ref, gk_ref, _, dq_ref, dk_ref, dv_ref, dgq_ref, dgk_ref):
        gqv, gkv = gq_ref[...] * scale, gk_ref[...]
        kv = k_ref[0]
        kn = _mx(kv * lax.rsqrt(jnp.mean(kv * kv, axis=-1, keepdims=True) + EPS) * gkv)
        vv = _mx(v_ref[0])

        def blk(i, carry):
            dkn, dvv, dgq = carry
            rows = pl.ds(pl.multiple_of(i * tq, tq), tq)
            qv = q_ref[0, rows, :]
            qn = _mx(qv * lax.rsqrt(jnp.mean(qv * qv, axis=-1, keepdims=True) + EPS) * gqv)
            s = _nt(qn, kn)
            e = jnp.exp(s - jnp.max(s, axis=-1, keepdims=True))
            pm = e / jnp.sum(e, axis=-1, keepdims=True)
            dob = _mx(dy_ref[0, rows, :])
            dp = _nt(dob, vv)
            ds = pm * (dp - jnp.sum(dp * pm, axis=-1, keepdims=True))
            dqv, gq_part = _norm_bwd(qv, _nn(ds, kn), gqv)
            dq_ref[0, rows, :] = dqv.astype(dq_ref.dtype)
            return dkn + _tn(ds, qn), dvv + _tn(pm, dob), dgq + gq_part * scale

        z = jnp.zeros((m_, LANE), F32)
        dkn, dvv, dgq = lax.fori_loop(0, nb, blk, (z, z, jnp.zeros((1, LANE), F32)))
        dkv, dgk = _norm_bwd(kv, dkn, gkv)
        dk_ref[0] = dkv
        dv_ref[0] = dvv
        first = jnp.logical_and(pl.program_id(0) == 0, pl.program_id(1) == 0)
        _acc(dgq_ref, dgq, first)
        _acc(dgk_ref, dgk, first)

    kblk = pl.BlockSpec((1, m_, LANE), lambda b, h: (b, 0, h))
    gs = jax.ShapeDtypeStruct((1, LANE), F32)
    ks = jax.ShapeDtypeStruct((b_, m_, mw), F32)
    return pl.pallas_call(
        body, name="mem_bwd", grid=(b_, MEM_HEADS), in_specs=[qcol, kcol, vcol, ycol, gvec, gvec, ANY],
        out_specs=[qcol, kblk, kblk, gvec, gvec],
        out_shape=[jax.ShapeDtypeStruct(dz.shape, dz.dtype), ks, ks, gs, gs], input_output_aliases={6: 0},
        compiler_params=_params(2),
    )(zm, mkv, mkv, dy, gq, gk, dz)


def _merge_specs(tm, d, w, gcol):
    row_d = pl.BlockSpec((tm, d), lambda i: (i, 0))
    row_w = pl.BlockSpec((tm, w), lambda i: (i, 0))
    gates = [pl.BlockSpec((tm, d), functools.partial(lambda i, k: (i, gcol + k), k=k)) for k in range(3)]
    w_br = pl.BlockSpec((w, d), lambda i: (0, 0))
    w_o = pl.BlockSpec((d, d), lambda i: (0, 0))
    return row_d, row_w, gates, w_br, w_o


def _merge_fwd(x, ys, zm, w_brs, w_out, gcol, tm=256):
    n, d = x.shape
    w = ys[0].shape[1]
    tm = _tile(n, tm, 8)
    row_d, row_w, gates, w_br, w_o = _merge_specs(tm, d, w, gcol)

    def body(x_ref, ya, yb, yc, g0, g1, g2, wa, wb, wc, wo, x1_ref, mg_ref):
        mg = (_sig(g0[...]) * _nn(ya[...], wa[...]) + _sig(g1[...]) * _nn(yb[...], wb[...])
              + _sig(g2[...]) * _nn(yc[...], wc[...]))
        mg_ref[...] = mg.astype(mg_ref.dtype)
        x1_ref[...] = x_ref[...] + _nn(mg, wo[...])

    return pl.pallas_call(
        body, name="merge_fwd", grid=(n // tm,),
        in_specs=[row_d, row_w, row_w, row_w] + gates + [w_br, w_br, w_br, w_o],
        out_specs=[row_d, row_d],
        out_shape=[jax.ShapeDtypeStruct((n, d), F32), jax.ShapeDtypeStruct((n, d), MXU_DTYPE)],
        compiler_params=_params(1),
    )(x, *ys, zm, zm, zm, *w_brs, w_out)


def _merge_bwd(dx1, ys, zm, w_brs, w_out, gcol, tm=256):
    n, d = dx1.shape
    w = ys[0].shape[1]
    tm = _tile(n, tm, 8)
    row_d, row_w, gates, w_br, w_o = _merge_specs(tm, d, w, gcol)

    def body(dx_ref, ya, yb, yc, g0, g1, g2, wa, wb, wc, wo, dgl_ref, dpa, dpb, dpc, dya, dyb, dyc):
        dm = _nt(dx_ref[...], wo[...])
        for k, (y, g, wr, dp_ref, dy_ref) in enumerate(((ya, g0, wa, dpa, dya), (yb, g1, wb, dpb, dyb),
                                                        (yc, g2, wc, dpc, dyc))):
            sg = _sig(g[...])
            pr = _nn(y[...], wr[...])
            dgl_ref[:, k * d:(k + 1) * d] = (dm * pr * sg * (1.0 - sg)).astype(dgl_ref.dtype)
            dp = (dm * sg).astype(dp_ref.dtype)
            dp_ref[...] = dp
            dy_ref[...] = _nt(dp, wr[...])

    sd = jax.ShapeDtypeStruct((n, d), MXU_DTYPE)
    sw = jax.ShapeDtypeStruct((n, w), F32)
    return pl.pallas_call(
        body, name="merge_bwd", grid=(n // tm,),
        in_specs=[row_d, row_w, row_w, row_w] + gates + [w_br, w_br, w_br, w_o],
        out_specs=[pl.BlockSpec((tm, 3 * d), lambda i: (i, 0)), row_d, row_d, row_d, row_w, row_w, row_w],
        out_shape=[jax.ShapeDtypeStruct((n, zm.shape[1]), MXU_DTYPE), sd, sd, sd, sw, sw, sw],
        compiler_params=_params(1),
    )(dx1, *ys, zm, zm, zm, *w_brs, w_out)


CONV_ROWS = 256
HALO = 8


def _ext(ref, r0, t_):
    rc = min(CONV_ROWS, t_)
    a, b = max(r0 - HALO, 0), min(r0 + rc + HALO, t_)
    parts = []
    if r0 - HALO < 0:
        parts.append(jnp.zeros((HALO, ref.shape[2]), F32))
    parts.append(ref[0, a:b, :].astype(F32))
    if r0 + rc + HALO > t_:
        parts.append(jnp.zeros((HALO, ref.shape[2]), F32))
    return jnp.concatenate(parts, axis=0) if len(parts) > 1 else parts[0]


def _gelu_parts(ac):
    cdf = 0.5 * (1.0 + _erf(ac * (2.0 ** -0.5)))
    pdf = jnp.exp(-0.5 * ac * ac) * ((2.0 * math.pi) ** -0.5)
    return cdf, pdf


def _conv_taps(a_ext, cw, cb):
    return cw[0:1, :] * pltpu.roll(a_ext, 2, 0) + cw[1:2, :] * pltpu.roll(a_ext, 1, 0) + cw[2:3, :] * a_ext + cb


def _glu_specs(t_, f, g):
    gate = pl.BlockSpec((1, t_, g), lambda j, b: (b, 0, j))
    value = pl.BlockSpec((1, t_, g), lambda j, b: (b, 0, f // g + j))
    cwb = pl.BlockSpec((3, g), lambda j, b: (0, j))
    cbb = pl.BlockSpec((1, g), lambda j, b: (0, j))
    return gate, value, cwb, cbb


def _glu_fwd(u, cw, cb):
    b_, t_, f2 = u.shape
    f = f2 // 2
    g = min(FFN_GROUP, f)
    rc = min(CONV_ROWS, t_)
    gate, value, cwb, cbb = _glu_specs(t_, f, g)

    def body(a_ref, v_ref, cw_ref, cb_ref, y_ref):
        cwv, cbv = cw_ref[...], cb_ref[...]
        for r0 in range(0, t_, rc):
            ac = _conv_taps(_ext(a_ref, r0, t_), cwv, cbv)[HALO:HALO + rc]
            cdf, _ = _gelu_parts(ac)
            y_ref[0, r0:r0 + rc, :] = (ac * cdf * v_ref[0, r0:r0 + rc, :]).astype(y_ref.dtype)

    return pl.pallas_call(
        body, name="glu_fwd", grid=(f // g, b_), in_specs=[gate, value, cwb, cbb], out_specs=gate,
        out_shape=jax.ShapeDtypeStruct((b_, t_, f), MXU_DTYPE), compiler_params=_params(2),
    )(u, u, cw, cb)


def _glu_bwd(u, dy, cw, cb):
    b_, t_, f2 = u.shape
    f = f2 // 2
    g = min(FFN_GROUP, f)
    rc = min(CONV_ROWS, t_)
    ne = rc + 2 * HALO
    gate, value, cwb, cbb = _glu_specs(t_, f, g)

    def body(a_ref, v_ref, dy_ref, cw_ref, cb_ref, da_ref, dv_ref, dcw_ref, dcb_ref):
        cwv, cbv = cw_ref[...], cb_ref[...]
        dcw = [jnp.zeros((1, g), F32) for _ in range(3)]
        dcb = jnp.zeros((1, g), F32)
        for r0 in range(0, t_, rc):
            a_ext, v_ext, dy_ext = _ext(a_ref, r0, t_), _ext(v_ref, r0, t_), _ext(dy_ref, r0, t_)
            ac = _conv_taps(a_ext, cwv, cbv)
            cdf, pdf = _gelu_parts(ac)
            dac = dy_ext * v_ext * (cdf + ac * pdf)
            da = cwv[2:3, :] * dac + cwv[1:2, :] * pltpu.roll(dac, ne - 1, 0) + cwv[0:1, :] * pltpu.roll(dac, ne - 2, 0)
            mid = slice(HALO, HALO + rc)
            da_ref[0, r0:r0 + rc, :] = da[mid].astype(da_ref.dtype)
            dv_ref[0, r0:r0 + rc, :] = (dy_ext[mid] * ac[mid] * cdf[mid]).astype(dv_ref.dtype)
            dacm = dac[mid]
            dcw[0] = dcw[0] + jnp.sum(dacm * pltpu.roll(a_ext, 2, 0)[mid], axis=0, keepdims=True)
            dcw[1] = dcw[1] + jnp.sum(dacm * pltpu.roll(a_ext, 1, 0)[mid], axis=0, keepdims=True)
            dcw[2] = dcw[2] + jnp.sum(dacm * a_ext[mid], axis=0, keepdims=True)
            dcb = dcb + jnp.sum(dacm, axis=0, keepdims=True)
        first = pl.program_id(1) == 0
        _acc(dcw_ref, jnp.concatenate(dcw, axis=0), first)
        _acc(dcb_ref, dcb, first)

    sds = jax.ShapeDtypeStruct((b_, t_, f), MXU_DTYPE)
    return pl.pallas_call(
        body, name="glu_bwd", grid=(f // g, b_), in_specs=[gate, value, gate, cwb, cbb],
        out_specs=[gate, gate, cwb, cbb],
        out_shape=[sds, sds, jax.ShapeDtypeStruct((3, f), F32), jax.ShapeDtypeStruct((1, f), F32)],
        compiler_params=_params(2),
    )(u, u, dy, cw, cb)


def _place():
    x, y, c = lax.axis_index("x"), lax.axis_index("y"), lax.axis_index("c")
    chips = [(1 - x, y), (x, 1 - y), (1 - x, 1 - y)]
    return x, y, c, chips


def _remote(src, dst, send_sem, recv_sem, to):
    return pltpu.make_async_remote_copy(src_ref=src, dst_ref=dst, send_sem=send_sem, recv_sem=recv_sem,
                                        device_id=to, device_id_type=MESH)


STACK, COLS = "stack", "cols"


def _shard_ref(ref, kind, s, rows, c):
    if kind == COLS:
        cols = pl.ds(pl.multiple_of(s * c, LANE), c)
        return ref.at[:, cols] if rows is None else ref.at[rows, cols]
    return ref.at[s] if rows is None else ref.at[s, rows, :]


def _halves(c, half):
    mine = pl.ds(pl.multiple_of(c * half, 16), half)
    theirs = pl.ds(pl.multiple_of((1 - c) * half, 16), half)
    return mine, theirs


def _gather_parts(kinds):
    def first_copies(ins, outs, sems):
        x, y, c, chips = _place()
        me = 2 * x + y
        cps = []
        for i, (w_ref, o_ref, kind) in enumerate(zip(ins, outs, kinds)):
            r, cw = w_ref.shape
            mine, _ = _halves(c, r // 2)
            for j, chip in enumerate(chips):
                cps.append(_remote(w_ref.at[mine], _shard_ref(o_ref, kind, me, mine, cw), sems[0].at[6 * i + j],
                                   sems[1].at[6 * i + j], (*chip, c)))
        return cps

    def start(ins, outs, sems):
        for cp in first_copies(ins, outs, sems):
            cp.start()

    def finish(ins, outs, sems):
        x, y, c, chips = _place()
        sib = (x, y, 1 - c)
        passed = []
        for i, (w_ref, o_ref, kind) in enumerate(zip(ins, outs, kinds)):
            r, cw = w_ref.shape
            mine, _ = _halves(c, r // 2)
            for j, (px, py) in enumerate(chips):
                blk = _shard_ref(o_ref, kind, 2 * px + py, mine, cw)
                _remote(blk, blk, sems[0].at[6 * i + j], sems[1].at[6 * i + j], sib).wait_recv()
                passed.append(_remote(blk, blk, sems[0].at[6 * i + 3 + j], sems[1].at[6 * i + 3 + j], sib))
                passed[-1].start()
        for i, (w_ref, o_ref, kind) in enumerate(zip(ins, outs, kinds)):
            r, cw = w_ref.shape
            _, theirs = _halves(c, r // 2)
            for j, (px, py) in enumerate(chips):
                blk = _shard_ref(o_ref, kind, 2 * px + py, theirs, cw)
                _remote(blk, blk, sems[0].at[6 * i + 3 + j], sems[1].at[6 * i + 3 + j], sib).wait_recv()
        for cp in first_copies(ins, outs, sems) + passed:
            cp.wait_send()

    return start, finish


def _gather_shapes(shards, kinds):
    return [jax.ShapeDtypeStruct((a.shape[0], N_CHIPS * a.shape[1]) if k == COLS else (N_CHIPS,) + a.shape, a.dtype)
            for a, k in zip(shards, kinds)]


def _gather_sems(nw):
    return [pltpu.SemaphoreType.DMA((6 * nw,)), pltpu.SemaphoreType.DMA((6 * nw,))]


def _gather_shards(shards, kinds):
    nw = len(shards)
    start, finish = _gather_parts(kinds)

    def body(*refs):
        ins, outs, sems = refs[:nw], refs[nw:2 * nw], refs[2 * nw:]
        start(ins, outs, sems)
        finish(ins, outs, sems)

    return pl.pallas_call(
        body, name="gather_shards", in_specs=[ANY] * nw, out_specs=[ANY] * nw,
        out_shape=_gather_shapes(shards, kinds), scratch_shapes=_gather_sems(nw),
    )(*shards)


def _gather_rider(shards, kinds):
    start, finish = _gather_parts(kinds)
    return _Rider(list(shards), _gather_shapes(shards, kinds), _gather_sems(len(shards)), start, finish)


def _half_shape(g, kind):
    if kind == COLS:
        return (g.shape[0] // 2, g.shape[1])
    return (g.shape[0], g.shape[1] // 2, g.shape[2])


def _swap_parts(kinds):
    def copies(ins, outs, sems):
        x, y, c, _ = _place()
        cps = []
        for i, (g_ref, a_ref, kind) in enumerate(zip(ins, outs, kinds)):
            r = g_ref.shape[0] if kind == COLS else g_ref.shape[1]
            _, theirs = _halves(c, r // 2)
            src = g_ref.at[theirs] if kind == COLS else g_ref.at[:, theirs]
            cps.append(_remote(src, a_ref, sems[0].at[i], sems[1].at[i], (x, y, 1 - c)))
        return cps

    def start(ins, outs, sems):
        for cp in copies(ins, outs, sems):
            cp.start()

    def finish(ins, outs, sems):
        for cp in copies(ins, outs, sems):
            cp.wait()

    return start, finish


def _swap_shapes(gs, kinds):
    return [jax.ShapeDtypeStruct(_half_shape(g, k), g.dtype) for g, k in zip(gs, kinds)]


def _pair_swap_halves(gs, kinds, name):
    nw = len(gs)
    start, finish = _swap_parts(kinds)

    def body(*refs):
        ins, outs, sems = refs[:nw], refs[nw:2 * nw], refs[2 * nw:]
        start(ins, outs, sems)
        finish(ins, outs, sems)

    return pl.pallas_call(
        body, name=name, in_specs=[ANY] * nw, out_specs=[ANY] * nw, out_shape=_swap_shapes(gs, kinds),
        scratch_shapes=[pltpu.SemaphoreType.DMA((nw,)), pltpu.SemaphoreType.DMA((nw,))],
    )(*gs)


def _swap_rider(gs, kinds):
    start, finish = _swap_parts(kinds)
    nw = len(gs)
    return _Rider(list(gs), _swap_shapes(gs, kinds), [pltpu.SemaphoreType.DMA((nw,)), pltpu.SemaphoreType.DMA((nw,))],
                  start, finish)


def _row_tile(rows, width, itemsize=4, target=2 ** 21):
    return _tile(rows, max(8, target // (width * itemsize)), 8)


def _add_half(g, a, kind, c_idx, name):
    if kind == COLS:
        half, wd = a.shape
        tr = _row_tile(half, wd)
        nblk = half // tr
        grid = (nblk,)
        g_spec = pl.BlockSpec((tr, wd), lambda i, c_ref: (c_ref[0] * nblk + i, 0))
        a_spec = pl.BlockSpec((tr, wd), lambda i, c_ref: (i, 0))
    else:
        n, half, wd = a.shape
        tr = _row_tile(half, wd)
        nblk = half // tr
        grid = (n, nblk)
        g_spec = pl.BlockSpec((1, tr, wd), lambda s, i, c_ref: (s, c_ref[0] * nblk + i, 0))
        a_spec = pl.BlockSpec((1, tr, wd), lambda s, i, c_ref: (s, i, 0))

    def body(c_ref, g_ref, a_ref, o_ref):
        o_ref[...] = (g_ref[...] + a_ref[...]).astype(o_ref.dtype)

    return pl.pallas_call(
        body, name=name,
        grid_spec=pltpu.PrefetchScalarGridSpec(num_scalar_prefetch=1, grid=grid, in_specs=[g_spec, a_spec],
                                               out_specs=a_spec),
        out_shape=jax.ShapeDtypeStruct(a.shape, EXCHANGE_DTYPE), compiler_params=_params(len(grid)),
    )(c_idx, g, a)


def _exchange_parts(kinds):
    def copies(ins, outs, sems):
        x, y, c, chips = _place()
        me = 2 * x + y
        cps = []
        for i, (p_ref, b_ref, kind) in enumerate(zip(ins, outs, kinds)):
            cw = b_ref.shape[2]
            for j, (px, py) in enumerate(chips):
                cps.append(_remote(_shard_ref(p_ref, kind, 2 * px + py, None, cw), b_ref.at[me],
                                   sems[0].at[3 * i + j], sems[1].at[3 * i + j], (px, py, c)))
        return cps

    def start(ins, outs, sems):
        for cp in copies(ins, outs, sems):
            cp.start()

    def finish(ins, outs, sems):
        x, y, c, chips = _place()
        for i, b_ref in enumerate(outs):
            for j, (px, py) in enumerate(chips):
                blk = b_ref.at[2 * px + py]
                _remote(blk, blk, sems[0].at[3 * i + j], sems[1].at[3 * i + j], (px, py, c)).wait_recv()
        for cp in copies(ins, outs, sems):
            cp.wait_send()

    return start, finish


def _exchange_shapes(ps, kinds):
    return [jax.ShapeDtypeStruct((N_CHIPS,) + ((p.shape[0], p.shape[1] // N_CHIPS) if k == COLS else tuple(p.shape[1:])),
                                 p.dtype) for p, k in zip(ps, kinds)]


def _exchange_sems(nw):
    return [pltpu.SemaphoreType.DMA((3 * nw,)), pltpu.SemaphoreType.DMA((3 * nw,))]


def _exchange_rider(ps, kinds):
    start, finish = _exchange_parts(kinds)
    return _Rider(list(ps), _exchange_shapes(ps, kinds), _exchange_sems(len(ps)), start, finish)


def _sum_chips(bq, name):
    n, h, wd = bq.shape
    tr = _row_tile(h, wd * n)

    def body(b_ref, o_ref):
        acc = b_ref[0].astype(F32)
        for s in range(1, n):
            acc = acc + b_ref[s].astype(F32)
        o_ref[...] = acc

    return pl.pallas_call(
        body, name=name, grid=(h // tr,),
        in_specs=[pl.BlockSpec((n, tr, wd), lambda i: (0, i, 0))], out_specs=pl.BlockSpec((tr, wd), lambda i: (i, 0)),
        out_shape=jax.ShapeDtypeStruct((h, wd), F32), compiler_params=_params(1),
    )(bq)


def _pair_join_halves(qs):
    nw = len(qs)

    def body(*refs):
        ins, outs = refs[:nw], refs[nw:2 * nw]
        send_sems, recv_sems = refs[2 * nw:]
        x, y, c, _ = _place()
        sent = []
        for i, (q_ref, o_ref) in enumerate(zip(ins, outs)):
            mine, _ = _halves(c, q_ref.shape[0])
            sent.append(_remote(q_ref, o_ref.at[mine], send_sems.at[i], recv_sems.at[i], (x, y, 1 - c)))
            sent[-1].start()
        for i, (q_ref, o_ref) in enumerate(zip(ins, outs)):
            _, theirs = _halves(c, q_ref.shape[0])
            _remote(q_ref, o_ref.at[theirs], send_sems.at[i], recv_sems.at[i], (x, y, 1 - c)).wait_recv()
        for cp in sent:
            cp.wait_send()

    return pl.pallas_call(
        body, name="pair_join_halves", in_specs=[ANY] * nw, out_specs=[ANY] * nw,
        out_shape=[jax.ShapeDtypeStruct((2 * q.shape[0], q.shape[1]), q.dtype) for q in qs],
        scratch_shapes=[pltpu.SemaphoreType.DMA((nw,)), pltpu.SemaphoreType.DMA((nw,))],
    )(*qs)


def _all_sum_small(s, name):
    sr, w = s.shape

    def body(s_ref, o_ref, buf, send_sems, recv_sems):
        x, y, c, _ = _place()
        me = 4 * x + 2 * y + c
        buf[me] = s_ref[...]
        peers = []
        for k in range(1, 8):
            px = 1 - x if k & 4 else x
            py = 1 - y if k & 2 else y
            pc = 1 - c if k & 1 else c
            peers.append((px, py, pc))
        sent = [_remote(s_ref, buf.at[me], send_sems.at[k], recv_sems.at[k], peer) for k, peer in enumerate(peers)]
        for cp in sent:
            cp.start()
        for k, (px, py, pc) in enumerate(peers):
            _remote(s_ref, buf.at[4 * px + 2 * py + pc], send_sems.at[k], recv_sems.at[k], (px, py, pc)).wait_recv()
        for cp in sent:
            cp.wait_send()
        acc = buf[0]
        for d in range(1, 8):
            acc = acc + buf[d]
        o_ref[...] = acc

    vm = pl.BlockSpec(memory_space=pltpu.VMEM)
    return pl.pallas_call(
        body, name=name, in_specs=[vm], out_specs=vm, out_shape=jax.ShapeDtypeStruct((sr, w), F32),
        scratch_shapes=[pltpu.VMEM((8, sr, w), F32), pltpu.SemaphoreType.DMA((7,)), pltpu.SemaphoreType.DMA((7,))],
    )(s)


BIG = ("w_in", "mem_kv_w", "w_br_hgrn", "w_br_fox", "w_br_mem", "w_out", "ffn_w_up", "ffn_w_down")
KIND = {"w_in": STACK, "mem_kv_w": STACK, "w_br_hgrn": COLS, "w_br_fox": COLS, "w_br_mem": COLS, "w_out": STACK,
        "ffn_w_up": COLS, "ffn_w_down": STACK}
ROW_SHARDED = ("mem_kv_w", "w_out", "ffn_w_down")
FIRST = ("w_in",)
REST = tuple(nm for nm in BIG if nm not in FIRST)
LAST = ("w_in",)
TRANSPOSED = ("w_in",)


def _z_layout(d, hw, fw, mw):
    gate, npair, nh, nm = 3 * d // LANE, fw // LANE, hw // LANE, mw // LANE
    fox0, hg0 = gate, gate + 3 * npair
    o_fox, o_mem = 4 * nh, 4 * nh + 3 * npair
    order = [o_mem + nm + j for j in range(gate)]
    order += [o_fox + k * npair + p for p in range(npair) for k in range(3)]
    order += [k * nh + h for h in range(nh) for k in range(4)]
    order += [o_mem + h for h in range(nm)]
    assert fox0 % 3 == 0 and hg0 % 4 == 0
    return fox0, hg0, hg0 + 4 * nh, order


def _reorder_blocks(a, order):
    runs, start = [], 0
    for i in range(1, len(order) + 1):
        if i == len(order) or order[i] != order[i - 1] + 1:
            runs.append((order[start], order[i - 1] + 1))
            start = i
    return jnp.concatenate([a[:, lo * LANE:hi * LANE] for lo, hi in runs], axis=1)


def _put_shard(arr, kind, s, piece):
    if kind == COLS:
        return lax.dynamic_update_slice(arr, piece, (0, s * piece.shape[1]))
    return lax.dynamic_update_slice(arr, piece[None], (s, 0, 0))


def _take_shard(arr, kind, s):
    if kind == COLS:
        return lax.dynamic_slice(arr, (0, s * (arr.shape[1] // N_CHIPS)), (arr.shape[0], arr.shape[1] // N_CHIPS))
    return lax.dynamic_index_in_dim(arr, s, 0, keepdims=False)


def _w_in_pieces(cs, s1, nf):
    out = []
    for s in range(N_CHIPS):
        lo, hi = cs * s, cs * (s + 1)
        for a, b, forget in ((lo, min(hi, s1), False), (max(lo, s1), min(hi, s1 + nf), True), (max(lo, s1 + nf), hi, False)):
            if a < b:
                out.append((s, a - lo, b - lo, forget, a - s1 if forget else (a if a < s1 else a - nf)))
    return out


def _split_w_in(stacked, s1, nf):
    pieces = _w_in_pieces(stacked.shape[2], s1, nf)
    main = [stacked[s, :, a:b] for s, a, b, forget, _ in pieces if not forget]
    ff = [stacked[s, :, a:b] for s, a, b, forget, _ in pieces if forget]
    return jnp.concatenate(main, axis=1), jnp.concatenate(ff, axis=1)


def _join_w_in(g_main, g_ff, s1, nf):
    cs = (g_main.shape[1] + nf) // N_CHIPS
    shards = [[] for _ in range(N_CHIPS)]
    for s, a, b, forget, off in _w_in_pieces(cs, s1, nf):
        shards[s].append((g_ff if forget else g_main)[:, off:off + b - a])
    return jnp.stack([jnp.concatenate(p, axis=1) if len(p) > 1 else p[0] for p in shards])


SMALL = ("norm_mix_g", "norm_mem_g", "norm_ffn_g", "hgrn_lb_logits", "hgrn_norm_g", "fox_f_bias", "fox_q_norm_g",
         "fox_k_norm_g", "mem_q_norm_g", "mem_k_norm_g", "ffn_conv_b")


def _small_rows(shapes):
    rows = []
    for a, (r, c) in enumerate(shapes):
        for i in range(r):
            for lo in range(0, c, FLAT_W):
                rows.append((a, i, lo, min(FLAT_W, c - lo)))
    return rows


def _pack_small(vals):
    rows = _small_rows([v.shape for v in vals])
    sr = -(-len(rows) // 8) * 8

    def body(*refs):
        o_ref = refs[-1]
        o_ref[...] = jnp.zeros(o_ref.shape, F32)
        for k, (a, i, lo, wd) in enumerate(rows):
            o_ref[k:k + 1, 0:wd] = refs[a][i:i + 1, lo:lo + wd]

    vm = pl.BlockSpec(memory_space=pltpu.VMEM)
    return pl.pallas_call(body, name="pack_small", in_specs=[vm] * len(vals), out_specs=vm,
                          out_shape=jax.ShapeDtypeStruct((sr, FLAT_W), F32))(*vals)


def _row_of(buf_ref, rows, a, i):
    parts = [buf_ref[k:k + 1, 0:wd] for k, (a2, i2, _, wd) in enumerate(rows) if (a2, i2) == (a, i)]
    return jnp.concatenate(parts, axis=1) if len(parts) > 1 else parts[0]


def _unpack_small(buf, shapes):
    rows = _small_rows(shapes)

    def body(buf_ref, *outs):
        for a, (r, _) in enumerate(shapes):
            for i in range(r):
                outs[a][i:i + 1, :] = _row_of(buf_ref, rows, a, i)

    vm = pl.BlockSpec(memory_space=pltpu.VMEM)
    return pl.pallas_call(body, name="unpack_small", in_specs=[vm], out_specs=[vm] * len(shapes),
                          out_shape=[jax.ShapeDtypeStruct(shp, F32) for shp in shapes])(buf)


def _adamw_small(buf, shapes, ws, ms, vs):
    n = len(ws)
    rows = _small_rows(shapes)
    c1 = 1.0 / (1.0 - ADAM_B1 ** ADAM_STEP)
    c2 = 1.0 / (1.0 - ADAM_B2 ** ADAM_STEP)

    def body(buf_ref, *refs):
        w_refs, m_refs, v_refs = refs[:n], refs[n:2 * n], refs[2 * n:3 * n]
        outs = refs[3 * n:]
        g_out, d_out, m_out, v_out, rest = outs[:n], outs[n:2 * n], outs[2 * n:3 * n], outs[3 * n:4 * n], outs[4 * n:]
        for a, (r, _) in enumerate(shapes):
            for i in range(r):
                gv = _row_of(buf_ref, rows, a, i)
                if a >= n:
                    rest[a - n][i:i + 1, :] = gv
                    continue
                row = slice(i, i + 1)
                mn = ADAM_B1 * m_refs[a][row, :] + (1.0 - ADAM_B1) * gv
                vn = ADAM_B2 * v_refs[a][row, :] + (1.0 - ADAM_B2) * (gv * gv)
                g_out[a][row, :] = gv
                d_out[a][row, :] = -ADAM_LR * ((mn * c1) / (jnp.sqrt(vn * c2) + ADAM_EPS) + ADAM_WD * w_refs[a][row, :])
                m_out[a][row, :] = mn
                v_out[a][row, :] = vn

    vm = pl.BlockSpec(memory_space=pltpu.VMEM)
    own = [jax.ShapeDtypeStruct(shp, F32) for shp in shapes[:n]]
    outs = pl.pallas_call(
        body, name="adamw_small", in_specs=[vm] * (1 + 3 * n), out_specs=[vm] * (4 * n + len(shapes) - n),
        out_shape=own * 4 + [jax.ShapeDtypeStruct(shp, F32) for shp in shapes[n:]],
    )(buf, *ws, *ms, *vs)
    return outs[:n], outs[n:2 * n], outs[2 * n:3 * n], outs[3 * n:4 * n], outs[4 * n:]


def _pad_lanes(v, width=LANE):
    return jnp.pad(v, ((0, 0), (0, width - v.shape[1])))


WEIGHTS = ("norm_mix_g", "norm_mem_g", "w_in", "hgrn_lb_logits", "hgrn_norm_g", "fox_f_bias", "fox_q_norm_g",
           "fox_k_norm_g", "mem_kv_w", "mem_q_norm_g", "mem_k_norm_g", "w_br_hgrn", "w_br_fox", "w_br_mem", "w_out",
           "norm_ffn_g", "ffn_w_up", "ffn_conv_w", "ffn_conv_b", "ffn_w_down")


def _local_step(x, mem, target, w, full, conv_w, late=None, hooks=None):
    b_, t_, d = x.shape
    n = b_ * t_
    hw, fw, mw = HG_HEADS * HG_D, FOX_HEADS * FOX_DH, MEM_HEADS * MEM_DH
    m_ = mem.shape[1]
    f = conv_w.shape[1]
    s1 = 4 * hw + 3 * fw
    fox_col, hg_col, mem_col, order = _z_layout(d, hw, fw, mw)
    gate_col = 0
    inverse = [order.index(j) for j in range(len(order))]

    w_main, w_ff = _split_w_in(full["w_in"], s1, FOX_HEADS)
    w_main = _reorder_blocks(w_main, order)
    w_ff = _pad_lanes(w_ff)
    f_bias = _pad_lanes(w["fox_f_bias"])
    cb = w["ffn_conv_b"]

    x2 = x.reshape(n, d)
    h = _rmsnorm_fwd(x2, w["norm_mix_g"], name="norm_mix_fwd")
    if late:
        zm, gathered = _matmul(h, w_main, name="in_proj", rider=_gather_rider(late[0], late[1]))
        full = {**full, **late[2](gathered)}
    else:
        zm = _matmul(h, w_main, name="in_proj")
    w_up = full["ffn_w_up"]
    w_brs = [full["w_br_hgrn"], full["w_br_fox"], full["w_br_mem"]]
    w_out, w_kv, w_down = full["w_out"], full["mem_kv_w"], full["ffn_w_down"]
    zf = _matmul(h, w_ff, name="in_proj_forget")
    zm3, zf3 = zm.reshape(b_, t_, -1), zf.reshape(b_, t_, LANE)
    ya = _hgrn_fwd(zm3, w["hgrn_lb_logits"], w["hgrn_norm_g"], hw, hg_col)
    fc = _fox_prep(zf3, f_bias)
    fox_gq, fox_gk = jnp.tile(w["fox_q_norm_g"], (1, 2)), jnp.tile(w["fox_k_norm_g"], (1, 2))
    yb, lse = _fox_fwd(zm3, fc, fox_gq, fox_gk, fw, fox_col)
    mem2 = mem.reshape(b_ * m_, d)
    hm = _rmsnorm_fwd(mem2, w["norm_mem_g"], name="norm_mem_fwd")
    mkv = _matmul(hm, w_kv, name="mem_kv_proj").reshape(b_, m_, 2 * mw)
    yc = _mem_fwd(zm3, mkv, w["mem_q_norm_g"], w["mem_k_norm_g"], mw, mem_col)
    ys = [ya.reshape(n, hw), yb.reshape(n, fw), yc.reshape(n, mw)]
    x1, merged = _merge_fwd(x2, ys, zm, w_brs, w_out, gate_col)
    h2 = _rmsnorm_fwd(x1, w["norm_ffn_g"], name="norm_ffn_fwd")
    u = _matmul(h2, w_up, name="ffn_up")
    u3 = u.reshape(b_, t_, 2 * f)
    yff = _glu_fwd(u3, conv_w, cb).reshape(n, f)
    dy, (loss_vec,), _ = _matmul_rows([yff], w_down, name="ffn_down_loss", tb=False, row_ins=[x1, target.reshape(n, d)],
                                      vec_ins=[], epilogue=_loss_epilogue, n_vec_out=1)

    grads = {}

    def ridden(name, call):
        if not hooks or name not in hooks:
            return call(None)[0]
        rider, then = hooks[name](grads)
        outs, extra = call(rider)
        then(extra)
        return outs

    dyff = _matmul(dy, w_down, tb=True, name="ffn_down_dx")
    grads["ffn_w_down"] = _matmul(yff, dy, ta=True, name="ffn_down_dw", tm=1408)
    du_a, du_v, grads["ffn_conv_w"], grads["ffn_conv_b"] = _glu_bwd(u3, dyff.reshape(b_, t_, f), conv_w, cb)
    du_a, du_v = du_a.reshape(n, f), du_v.reshape(n, f)
    dx1, (grads["norm_ffn_g"],), _ = _matmul_rows(
        [du_a, du_v], w_up, name="ffn_up_dx", tb=True, row_ins=[x1, dy], vec_ins=[w["norm_ffn_g"]],
        epilogue=_norm_bwd_epilogue(0), n_vec_out=1)
    grads["ffn_w_up"] = jnp.concatenate([_matmul(h2, du_a, ta=True, name="ffn_up_gate_dw"),
                                         _matmul(h2, du_v, ta=True, name="ffn_up_value_dw")], axis=1)

    dz, dpa, dpb, dpc, dya, dyb, dyc = _merge_bwd(dx1, ys, zm, w_brs, w_out, gate_col)
    dz = dz.reshape(b_, t_, -1)
    grads["w_out"] = _matmul(merged, dx1, ta=True, name="out_proj_dw")
    for nm, y_, dp_ in zip(("w_br_hgrn", "w_br_fox", "w_br_mem"), ys, (dpa, dpb, dpc)):
        grads[nm] = _matmul(y_, dp_, ta=True, name=nm + "_dw")

    dz, dmk, dmv, grads["mem_q_norm_g"], grads["mem_k_norm_g"] = _mem_bwd(
        zm3, mkv, dyc.reshape(b_, t_, mw), w["mem_q_norm_g"], w["mem_k_norm_g"], mw, mem_col, dz)
    dmkv = jnp.concatenate([dmk, dmv], axis=-1).reshape(b_ * m_, 2 * mw)
    grads["mem_kv_w"] = _matmul(hm, dmkv, ta=True, name="mem_kv_dw")
    dhm = _matmul(dmkv, w_kv, tb=True, name="mem_kv_dx")
    _, grads["norm_mem_g"] = _rmsnorm_bwd(mem2, [dhm], w["norm_mem_g"], None, name="norm_mem_bwd")

    dz, dfc, g_fq, g_fk = ridden("fox_bwd", lambda rider: _fox_bwd(
        zm3, yb, dyb.reshape(b_, t_, fw), lse, fc, fox_gq, fox_gk, fw, fox_col, dz, rider))
    grads["fox_q_norm_g"] = g_fq[:, :FOX_DH] + g_fq[:, FOX_DH:]
    grads["fox_k_norm_g"] = g_fk[:, :FOX_DH] + g_fk[:, FOX_DH:]
    dzf, g_fb = _fox_post(dfc, zf3, f_bias)
    grads["fox_f_bias"] = g_fb[:, :FOX_HEADS]

    dz, grads["hgrn_lb_logits"], grads["hgrn_norm_g"] = ridden("hgrn_bwd", lambda rider: _hgrn_bwd(
        zm3, dya.reshape(b_, t_, hw), w["hgrn_lb_logits"], w["hgrn_norm_g"], hw, hg_col, dz, rider))
    dzm = dz.reshape(n, -1)
    dzf2 = dzf.reshape(n, LANE)
    g_main = _matmul(h, dzm, ta=True, name="in_proj_dw")
    g_ff = _matmul(h, dzf2, ta=True, name="in_proj_forget_dw")
    grads["w_in"] = _join_w_in(_reorder_blocks(g_main, inverse), g_ff[:, :FOX_HEADS], s1, FOX_HEADS)

    dh_b = _matmul(dzf2, w_ff, tb=True, name="in_proj_forget_dx")

    def in_proj_dx(rider):
        dx, vecs, extra = _matmul_rows([dzm], w_main, name="in_proj_dx", tb=True, row_ins=[x2, dx1, dh_b],
                                       vec_ins=[w["norm_mix_g"]], epilogue=_norm_bwd_epilogue(1), n_vec_out=1,
                                       rider=rider)
        return [dx, vecs[0]], extra

    grad_x, grads["norm_mix_g"] = ridden("in_proj_dx", in_proj_dx)
    return loss_vec, grad_x.reshape(b_, t_, d), grads


def kernel(x, mem, norm_mix_g, norm_mem_g, w_in, hgrn_lb_logits, hgrn_norm_g, fox_f_bias, fox_q_norm_g, fox_k_norm_g, mem_kv_w, mem_q_norm_g, mem_k_norm_g, w_br_hgrn, w_br_fox, w_br_mem, w_out, norm_ffn_g, ffn_w_up, ffn_conv_w, ffn_conv_b, ffn_w_down, loss_target, m_norm_mix_g, m_norm_mem_g, m_w_in, m_hgrn_lb_logits, m_hgrn_norm_g, m_fox_f_bias, m_fox_q_norm_g, m_fox_k_norm_g, m_mem_kv_w, m_mem_q_norm_g, m_mem_k_norm_g, m_w_br_hgrn, m_w_br_fox, m_w_br_mem, m_w_out, m_norm_ffn_g, m_ffn_w_up, m_ffn_conv_w, m_ffn_conv_b, m_ffn_w_down, v_norm_mix_g, v_norm_mem_g, v_w_in, v_hgrn_lb_logits, v_hgrn_norm_g, v_fox_f_bias, v_fox_q_norm_g, v_fox_k_norm_g, v_mem_kv_w, v_mem_q_norm_g, v_mem_k_norm_g, v_w_br_hgrn, v_w_br_fox, v_w_br_mem, v_w_out, v_norm_ffn_g, v_ffn_w_up, v_ffn_conv_w, v_ffn_conv_b, v_ffn_w_down):
    w = dict(zip(WEIGHTS, (norm_mix_g, norm_mem_g, w_in, hgrn_lb_logits, hgrn_norm_g, fox_f_bias, fox_q_norm_g,
                           fox_k_norm_g, mem_kv_w, mem_q_norm_g, mem_k_norm_g, w_br_hgrn, w_br_fox, w_br_mem, w_out,
                           norm_ffn_g, ffn_w_up, ffn_conv_w, ffn_conv_b, ffn_w_down)))
    m = dict(zip(WEIGHTS, (m_norm_mix_g, m_norm_mem_g, m_w_in, m_hgrn_lb_logits, m_hgrn_norm_g, m_fox_f_bias,
                           m_fox_q_norm_g, m_fox_k_norm_g, m_mem_kv_w, m_mem_q_norm_g, m_mem_k_norm_g, m_w_br_hgrn,
                           m_w_br_fox, m_w_br_mem, m_w_out, m_norm_ffn_g, m_ffn_w_up, m_ffn_conv_w, m_ffn_conv_b,
                           m_ffn_w_down)))
    v = dict(zip(WEIGHTS, (v_norm_mix_g, v_norm_mem_g, v_w_in, v_hgrn_lb_logits, v_hgrn_norm_g, v_fox_f_bias,
                           v_fox_q_norm_g, v_fox_k_norm_g, v_mem_kv_w, v_mem_q_norm_g, v_mem_k_norm_g, v_w_br_hgrn,
                           v_w_br_fox, v_w_br_mem, v_w_out, v_norm_ffn_g, v_ffn_w_up, v_ffn_conv_w, v_ffn_conv_b,
                           v_ffn_w_down)))
    c_idx = lax.axis_index("c")
    chip = 2 * lax.axis_index("x") + lax.axis_index("y")

    mine = {nm: w[nm][0].astype(MXU_DTYPE) for nm in BIG}

    def gathered_full(names, arrays):
        out = {nm: _put_shard(g, KIND[nm], chip, mine[nm]) for nm, g in zip(names, arrays)}
        return {nm: g.reshape(-1, g.shape[2]) if nm in ROW_SHARDED else g for nm, g in out.items()}

    full = gathered_full(FIRST, _gather_shards([mine[nm] for nm in FIRST], [KIND[nm] for nm in FIRST]))
    late = ([mine[nm] for nm in REST], [KIND[nm] for nm in REST], lambda arrays: gathered_full(REST, arrays))
    cs = ffn_conv_w.shape[2]
    f = cs * N_CHIPS
    placed = lax.dynamic_update_slice(jnp.zeros((3, f), F32), ffn_conv_w[0] * (c_idx == 0).astype(F32), (0, chip * cs))
    conv_w = _unpack_small(_all_sum_small(_pack_small([placed]), "gather_conv_w"), [(3, f)])[0]

    c_arr = jnp.reshape(c_idx, (1,)).astype(jnp.int32)

    def stacked(nm, g):
        return g.reshape(N_CHIPS, -1, g.shape[1]) if nm in ROW_SHARDED else g

    def with_own(landed, partial, kinds):
        return [_put_shard(bq, STACK, chip, _take_shard(p, k, chip)) for bq, p, k in zip(landed, partial, kinds)]

    kinds_rest, kinds_last = [KIND[nm] for nm in REST], [KIND[nm] for nm in LAST]
    state = {}

    def swap_rest(grads):
        gs = [stacked(nm, grads[nm]) for nm in REST]

        def then(from_sibling):
            state["partial_rest"] = [_add_half(g, a, k, c_arr, "add_half_" + nm)
                                     for g, a, k, nm in zip(gs, from_sibling, kinds_rest, REST)]

        return _swap_rider(gs, kinds_rest), then

    def exchange_rest(grads):
        def then(landed):
            state["landed_rest"] = with_own(landed, state["partial_rest"], kinds_rest)

        return _exchange_rider(state["partial_rest"], kinds_rest), then

    def exchange_last(grads):
        gs = [stacked(nm, grads[nm]) for nm in LAST]
        from_sibling = _pair_swap_halves(gs, kinds_last, "pair_swap_halves_last")
        partial = [_add_half(g, a, k, c_arr, "add_half_" + nm) for g, a, k, nm in zip(gs, from_sibling, kinds_last, LAST)]

        def then(landed):
            state["landed_last"] = with_own(landed, partial, kinds_last)

        return _exchange_rider(partial, kinds_last), then

    hooks = {"fox_bwd": swap_rest, "hgrn_bwd": exchange_rest, "in_proj_dx": exchange_last}

    loss_vec, grad_x, grads = _local_step(x, mem, loss_target, w, full, conv_w, late, hooks)

    landed = dict(zip(LAST + REST, state["landed_last"] + state["landed_rest"]))
    reduced_half = [_sum_chips(landed[nm], "sum_chips_" + nm) for nm in BIG]
    joined = [lax.dynamic_update_slice(o, q, (c_idx * q.shape[0], 0))
              for o, q in zip(_pair_join_halves(reduced_half), reduced_half)]
    gshards = dict(zip(BIG, joined))

    small_shapes = [w[nm].shape for nm in SMALL] + [grads["ffn_conv_w"].shape, loss_vec.shape]
    summed = _all_sum_small(_pack_small([grads[nm] for nm in SMALL] + [grads["ffn_conv_w"], loss_vec]),
                            "all_sum_small_grads")
    g_small, d_small, m_small, v_small, (g_conv_w, loss_row) = _adamw_small(
        summed, small_shapes, [w[nm] for nm in SMALL], [m[nm] for nm in SMALL], [v[nm] for nm in SMALL])
    loss = jnp.sum(loss_row)
    g_out = {nm: gshards[nm][None] for nm in BIG}
    g_out["ffn_conv_w"] = lax.dynamic_slice(g_conv_w, (0, chip * cs), (3, cs))[None]
    delta, new_m, new_v = dict(zip(SMALL, d_small)), dict(zip(SMALL, m_small)), dict(zip(SMALL, v_small))
    g_out.update(zip(SMALL, g_small))
    for nm in BIG + ("ffn_conv_w",):
        operands = (w[nm], g_out[nm], m[nm], v[nm])
        if nm in TRANSPOSED:
            operands = [jnp.swapaxes(a, 1, 2) for a in operands]
        outs = _adamw(*operands, name="adamw_" + nm)
        delta[nm], new_m[nm], new_v[nm] = [jnp.swapaxes(o, 1, 2) for o in outs] if nm in TRANSPOSED else outs

    return (loss, grad_x, *[g_out[nm] for nm in WEIGHTS], *[delta[nm] for nm in WEIGHTS],
            *[new_m[nm] for nm in WEIGHTS], *[new_v[nm] for nm in WEIGHTS])
```

```python
import functools
import math

import jax
import jax.numpy as jnp
from jax import lax
from jax.experimental import pallas as pl
from jax.experimental.pallas import tpu as pltpu

F32 = jnp.float32
BF16 = jnp.bfloat16
MXU_DTYPE = jnp.bfloat16
EXCHANGE_DTYPE = jnp.bfloat16

EPS = 1e-6
HG_HEADS, HG_D = 4, 128
FOX_HEADS, FOX_DH = 8, 64
MEM_HEADS, MEM_DH = 4, 128
HG_CHUNK = 64
FOX_BLOCK = 256
LANE = 128
FFN_GROUP = 256
FLAT_W = 1024
VMEM_LIMIT = 56 * 2 ** 20
NEG = -1e30
N_CHIPS = 4

ADAM_LR, ADAM_B1, ADAM_B2, ADAM_EPS, ADAM_WD, ADAM_STEP = 0.001, 0.9, 0.999, 1e-08, 0.01, 10

MESH = pl.DeviceIdType.MESH
ANY = pl.BlockSpec(memory_space=pl.ANY)


def _mx(x):
    return x.astype(MXU_DTYPE)


def _dot(a, b, ca, cb):
    return lax.dot_general(_mx(a), _mx(b), (((ca,), (cb,)), ((), ())), preferred_element_type=F32)


def _nn(a, b):
    return _dot(a, b, 1, 0)


def _nt(a, b):
    return _dot(a, b, 1, 1)


def _tn(a, b):
    return _dot(a, b, 0, 0)


def _dotp(a, b, ca, cb):
    return lax.dot_general(a, b, (((ca,), (cb,)), ((), ())), precision=lax.Precision.HIGHEST,
                           preferred_element_type=F32)


def _tri_dot(tri_bf, x):
    hi = x.astype(BF16)
    r = x - hi.astype(F32)
    mid = r.astype(BF16)
    lo = (r - mid.astype(F32)).astype(BF16)

    def d(v):
        return lax.dot_general(tri_bf, v, (((1,), (0,)), ((), ())), preferred_element_type=F32)

    return d(hi) + d(mid) + d(lo)


def _sig(x):
    return jax.nn.sigmoid(x)


def _erf(x):
    a = jnp.abs(x)
    t = 1.0 / (1.0 + 0.3275911 * a)
    poly = t * (0.254829592 + t * (-0.284496736 + t * (1.421413741 + t * (-1.453152027 + t * 1.061405429))))
    y = 1.0 - poly * jnp.exp(-a * a)
    return jnp.where(x < 0, -y, y)


def _tile(dim, pref, unit=LANE):
    if dim <= pref:
        return dim
    t = pref - pref % unit
    while t >= unit:
        if dim % t == 0:
            return t
        t -= unit
    return dim


def _params(n_grid):
    return pltpu.CompilerParams(dimension_semantics=("arbitrary",) * n_grid, vmem_limit_bytes=VMEM_LIMIT)


def _acc(ref, val, first):
    @pl.when(first)
    def _():
        ref[...] = val

    @pl.when(jnp.logical_not(first))
    def _():
        ref[...] += val


class _Rider:
    def __init__(self, inputs, out_shapes, scratch, start, finish):
        self.inputs, self.out_shapes, self.scratch, self.start, self.finish = inputs, out_shapes, scratch, start, finish


def _ride(body, rider, n_in, n_out, grid):
    if rider is None:
        return body
    ri, ro, rs = len(rider.inputs), len(rider.out_shapes), len(rider.scratch)

    def wrapped(*refs):
        a, b, c = n_in + ri, n_in + ri + n_out, n_in + ri + n_out + ro
        base = refs[:n_in] + refs[a:b] + refs[c:len(refs) - rs]
        r_in, r_out, r_scr = refs[n_in:a], refs[b:c], refs[len(refs) - rs:]
        step = pl.program_id(0)
        for ax in range(1, len(grid)):
            step = step * grid[ax] + pl.program_id(ax)

        @pl.when(step == 0)
        def _():
            rider.start(r_in, r_out, r_scr)

        body(*base)

        @pl.when(step == math.prod(grid) - 1)
        def _():
            rider.finish(r_in, r_out, r_scr)

    return wrapped


def _ride_call(body, rider, *, name, grid, in_specs, out_specs, out_shape, scratch, args, aliases=None):
    n_in, n_out = len(in_specs), len(out_specs)
    aliases = aliases or {}
    if rider is None:
        outs = pl.pallas_call(body, name=name, grid=grid, in_specs=in_specs, out_specs=out_specs, out_shape=out_shape,
                              scratch_shapes=scratch, input_output_aliases=aliases,
                              compiler_params=_params(len(grid)))(*args)
        return list(outs), None
    outs = pl.pallas_call(
        _ride(body, rider, n_in, n_out, grid), name=name, grid=grid,
        in_specs=list(in_specs) + [ANY] * len(rider.inputs), out_specs=list(out_specs) + [ANY] * len(rider.out_shapes),
        out_shape=list(out_shape) + list(rider.out_shapes), scratch_shapes=list(scratch) + list(rider.scratch),
        input_output_aliases=aliases, compiler_params=_params(len(grid)),
    )(*args, *rider.inputs)
    return list(outs[:n_out]), list(outs[n_out:])


def _matmul(a, b, *, name, ta=False, tb=False, tm=1024, tn=2048, tk=None, rider=None, b_parts=None, stack_out=False):
    m, k = (a.shape[1], a.shape[0]) if ta else a.shape
    tk = tk or (1024 if ta else 2048)
    stacked_b = b is not None and b.ndim == 3
    if b_parts:
        n, tn = 2 * b_parts[0].shape[1], _tile(b_parts[0].shape[1], tn)
    elif stacked_b:
        n, tn = b.shape[0] * b.shape[2], b.shape[2]
    else:
        n = b.shape[0] if tb else b.shape[1]
        tn = _tile(n, tn)
    tm, tk = _tile(m, tm), _tile(k, tk)
    nk, nj = k // tk, n // tn

    def body(a_ref, *refs):
        o_ref = refs[-1]
        if b_parts:
            bv = jnp.where(pl.program_id(1) < nj // 2, refs[0][...], refs[1][...])
        else:
            bv = refs[0][...]
        p = _dot(a_ref[...], bv, 0 if ta else 1, 1 if tb else 0)
        if nk == 1:
            o_ref[...] = p
        else:
            _acc(o_ref, p, pl.program_id(2) == 0)

    a_spec = pl.BlockSpec((tk, tm), lambda i, j, kk: (kk, i)) if ta else pl.BlockSpec((tm, tk), lambda i, j, kk: (i, kk))
    if b_parts:
        half = nj // 2
        b_specs = [pl.BlockSpec((tk, tn), lambda i, j, kk: (kk, jnp.minimum(j, half - 1))),
                   pl.BlockSpec((tk, tn), lambda i, j, kk: (kk, jnp.maximum(j - half, 0)))]
        b_args = list(b_parts)
    elif stacked_b:
        b_specs, b_args = [pl.BlockSpec((None, tk, tn), lambda i, j, kk: (j, kk, 0))], [b]
    else:
        b_specs = [pl.BlockSpec((tn, tk), lambda i, j, kk: (j, kk)) if tb else pl.BlockSpec((tk, tn), lambda i, j, kk: (kk, j))]
        b_args = [b]
    if stack_out:
        o_spec, o_sds = pl.BlockSpec((None, tm, tn), lambda i, j, kk: (j, i, 0)), jax.ShapeDtypeStruct((nj, m, tn), F32)
    else:
        o_spec, o_sds = pl.BlockSpec((tm, tn), lambda i, j, kk: (i, j)), jax.ShapeDtypeStruct((m, n), F32)
    outs, extra = _ride_call(body, rider, name=name, grid=(m // tm, nj, nk), in_specs=[a_spec] + b_specs,
                             out_specs=[o_spec], out_shape=[o_sds], scratch=[], args=(a, *b_args))
    return (outs[0], extra) if rider else outs[0]


def _matmul_rows(a_parts, b, *, name, tb, row_ins, vec_ins, epilogue, n_vec_out, tm=512, tk=2048, rider=None):
    m, kp = a_parts[0].shape
    stacked_b = b.ndim == 3
    n = b.shape[1] if stacked_b else (b.shape[0] if tb else b.shape[1])
    tm, tk = _tile(m, tm, 8), (b.shape[2] if stacked_b else _tile(kp, tk))
    nk = kp // tk
    n_a, n_row, n_vec = len(a_parts), len(row_ins), len(vec_ins)

    def body(*refs):
        a_refs, b_refs = refs[:n_a], refs[n_a:2 * n_a]
        rows = refs[2 * n_a:2 * n_a + n_row]
        vecs = refs[2 * n_a + n_row:2 * n_a + n_row + n_vec]
        o_ref = refs[2 * n_a + n_row + n_vec]
        v_refs = refs[2 * n_a + n_row + n_vec + 1:-1]
        acc_ref = refs[-1]
        i, kk = pl.program_id(0), pl.program_id(1)
        p = _dot(a_refs[0][...], b_refs[0][...], 1, 1 if tb else 0)
        for a_ref, b_ref in zip(a_refs[1:], b_refs[1:]):
            p = p + _dot(a_ref[...], b_ref[...], 1, 1 if tb else 0)
        _acc(acc_ref, p, kk == 0)

        @pl.when(kk == nk - 1)
        def _():
            out, vouts = epilogue(acc_ref[...], *[r[...] for r in rows], *[v[...] for v in vecs])
            o_ref[...] = out
            for v_ref, v in zip(v_refs, vouts):
                _acc(v_ref, v, i == 0)

    a_spec = pl.BlockSpec((tm, tk), lambda i, kk: (i, kk))
    if stacked_b:
        b_specs = [pl.BlockSpec((None, n, tk), functools.partial(lambda i, kk, q: (q * nk + kk, 0, 0), q=q))
                   for q in range(n_a)]
    else:
        b_specs = [pl.BlockSpec((n, tk), functools.partial(lambda i, kk, q: (0, q * nk + kk), q=q)) if tb else
                   pl.BlockSpec((tk, n), functools.partial(lambda i, kk, q: (q * nk + kk, 0), q=q)) for q in range(n_a)]
    row = pl.BlockSpec((tm, n), lambda i, kk: (i, 0))
    vec = pl.BlockSpec((1, n), lambda i, kk: (0, 0))
    outs, extra = _ride_call(
        body, rider, name=name, grid=(m // tm, nk),
        in_specs=[a_spec] * n_a + b_specs + [row] * n_row + [vec] * n_vec,
        out_specs=[row] + [vec] * n_vec_out,
        out_shape=[jax.ShapeDtypeStruct((m, n), F32)] + [jax.ShapeDtypeStruct((1, n), F32)] * n_vec_out,
        scratch=[pltpu.VMEM((tm, n), F32)], args=(*a_parts, *([b] * n_a), *row_ins, *vec_ins))
    return outs[0], outs[1:], extra


def _norm_bwd_epilogue(n_dh):
    def epilogue(dh, x, res, *rest):
        for extra in rest[:n_dh]:
            dh = dh + extra
        g = rest[n_dh]
        r = lax.rsqrt(jnp.mean(x * x, axis=-1, keepdims=True) + EPS)
        dhg = dh * g
        dx = res + r * dhg - x * (r * r * r) * jnp.mean(dhg * x, axis=-1, keepdims=True)
        return dx, [jnp.sum(dh * x * r, axis=0, keepdims=True)]

    return epilogue


def _loss_epilogue(y, x1, target):
    d = y.shape[1]
    err = x1 + y - target
    return err * (1.0 / d), [jnp.sum(err * err, axis=0, keepdims=True) * (0.5 / d)]


def _rmsnorm_fwd(x, g, *, name, tm=512):
    n, d = x.shape
    tm = _tile(n, tm, 8)

    def body(x_ref, g_ref, o_ref):
        xv = x_ref[...]
        r = lax.rsqrt(jnp.mean(xv * xv, axis=-1, keepdims=True) + EPS)
        o_ref[...] = (xv * r * g_ref[...]).astype(o_ref.dtype)

    return pl.pallas_call(
        body, name=name, grid=(n // tm,),
        in_specs=[pl.BlockSpec((tm, d), lambda i: (i, 0)), pl.BlockSpec((1, d), lambda i: (0, 0))],
        out_specs=pl.BlockSpec((tm, d), lambda i: (i, 0)),
        out_shape=jax.ShapeDtypeStruct((n, d), MXU_DTYPE),
        compiler_params=_params(1),
    )(x, g)


def _rmsnorm_bwd(x, dhs, g, res, *, name, tm=512):
    n, d = x.shape
    tm = _tile(n, tm, 8)
    n_dh = len(dhs)
    has_res = res is not None

    def body(*refs):
        x_ref, dh_refs, g_ref = refs[0], refs[1:1 + n_dh], refs[1 + n_dh]
        res_ref = refs[2 + n_dh] if has_res else None
        dx_ref, dg_ref = refs[-2], refs[-1]
        xv = x_ref[...]
        dh = dh_refs[0][...].astype(F32)
        for r_ in dh_refs[1:]:
            dh = dh + r_[...].astype(F32)
        r = lax.rsqrt(jnp.mean(xv * xv, axis=-1, keepdims=True) + EPS)
        dhg = dh * g_ref[...]
        dx = r * dhg - xv * (r * r * r) * jnp.mean(dhg * xv, axis=-1, keepdims=True)
        if has_res:
            dx = dx + res_ref[...]
        dx_ref[...] = dx
        _acc(dg_ref, jnp.sum(dh * xv * r, axis=0, keepdims=True), pl.program_id(0) == 0)

    row = pl.BlockSpec((tm, d), lambda i: (i, 0))
    vec = pl.BlockSpec((1, d), lambda i: (0, 0))
    ins = [x] + list(dhs) + [g] + ([res] if has_res else [])
    return pl.pallas_call(
        body, name=name, grid=(n // tm,),
        in_specs=[row] * (1 + n_dh) + [vec] + ([row] if has_res else []),
        out_specs=[row, vec],
        out_shape=[jax.ShapeDtypeStruct((n, d), F32), jax.ShapeDtypeStruct((1, d), F32)],
        compiler_params=_params(1),
    )(*ins)


def _adamw(w, g, m, v, *, name, tr=256):
    _, r, c = w.shape
    c1 = 1.0 / (1.0 - ADAM_B1 ** ADAM_STEP)
    c2 = 1.0 / (1.0 - ADAM_B2 ** ADAM_STEP)

    def body(w_ref, g_ref, m_ref, v_ref, d_ref, mo_ref, vo_ref):
        gv = g_ref[...]
        mn = ADAM_B1 * m_ref[...] + (1.0 - ADAM_B1) * gv
        vn = ADAM_B2 * v_ref[...] + (1.0 - ADAM_B2) * (gv * gv)
        d_ref[...] = -ADAM_LR * ((mn * c1) / (jnp.sqrt(vn * c2) + ADAM_EPS) + ADAM_WD * w_ref[...])
        mo_ref[...] = mn
        vo_ref[...] = vn

    if r % 8 == 0 or r < 8:
        tr = _tile(r, tr, 8)
        grid, blk = (r // tr,), pl.BlockSpec((1, tr, c), lambda i: (0, i, 0))
    else:
        tc = _tile(c, tr)
        grid, blk = (c // tc,), pl.BlockSpec((1, r, tc), lambda i: (0, 0, i))
    sds = jax.ShapeDtypeStruct((1, r, c), F32)
    return pl.pallas_call(
        body, name=name, grid=grid, in_specs=[blk] * 4, out_specs=[blk] * 3, out_shape=[sds] * 3,
        compiler_params=_params(1),
    )(w, g, m, v)


def _bdot(a, b, ca, cb):
    return lax.dot_general(_mx(a), _mx(b), (((ca,), (cb,)), ((0,), (0,))), preferred_element_type=F32)


def _bdotp(a, b, ca, cb):
    return lax.dot_general(a, b, (((ca,), (cb,)), ((0,), (0,))), precision=lax.Precision.HIGHEST,
                           preferred_element_type=F32)


def _tri_dot_b(tri_bf, x):
    hi = x.astype(BF16)
    r = x - hi.astype(F32)
    mid = r.astype(BF16)
    lo = (r - mid.astype(F32)).astype(BF16)

    def d(v):
        return lax.dot_general(tri_bf, v, (((2,), (1,)), ((0,), (0,))), preferred_element_type=F32)

    return d(hi) + d(mid) + d(lo)


def _hgrn_forward(hq, hf, hi, lbv, tril, tril_bf):
    nc, c, _ = hq.shape
    sf = _sig(hf)
    f = lbv + (1.0 - lbv) * sf
    k = 1.0 - f
    gcum = _tri_dot_b(tril_bf, jnp.log(f))
    mid = gcum[:, c // 2 - 1:c // 2, :]
    glast = gcum[:, c - 1:c, :]
    sq = _sig(hq)
    q = hq * sq
    e_q = jnp.exp(gcum - mid)
    e_k = jnp.exp(mid - gcum)
    qe, ke = q * e_q, k * e_k
    a = jnp.where(tril, _bdot(qe, ke, 2, 2), 0.0)
    e_g = jnp.exp(gcum)
    qg = q * e_g
    e_s = jnp.exp(glast - gcum)
    kg = k * e_s
    e_l = jnp.exp(glast)
    upd = _bdot(hi, kg, 1, 1)
    st = jnp.zeros((HG_D, HG_D), F32)
    states = []
    for n in range(nc):
        states.append(st)
        st = st * e_l[n] + upd[n]
    st_all = jnp.stack(states)
    o = _bdot(a, hi, 2, 1) + _bdot(qg, st_all, 2, 2)
    return dict(sf=sf, f=f, k=k, sq=sq, q=q, e_q=e_q, e_k=e_k, qe=qe, ke=ke, a=a, e_g=e_g, qg=qg, o=o,
                e_s=e_s, kg=kg, e_l=e_l, st_all=st_all)


def _hgrn_specs(t_, col0):
    def col(off):
        return pl.BlockSpec((1, t_, LANE), lambda h, b: (b, 0, col0 + 4 * h + off))

    vec = pl.BlockSpec((2, LANE), lambda h, b: (0, h))
    one = pl.BlockSpec((1, LANE), lambda h, b: (0, 0))
    blk = pl.BlockSpec((1, t_, LANE), lambda h, b: (b, 0, h))
    return col, vec, one, blk


def _chunk_masks(nc, c):
    row = lax.broadcasted_iota(jnp.int32, (nc, c, c), 1)
    cl = lax.broadcasted_iota(jnp.int32, (nc, c, c), 2)
    return row >= cl, (row >= cl).astype(BF16), (row <= cl).astype(BF16)


def _hgrn_fwd(zm, lb, gn, hw, col0):
    b_, t_, _ = zm.shape
    c = min(HG_CHUNK, t_)
    nc = t_ // c
    col, vec, one, blk = _hgrn_specs(t_, col0)

    def body(q_ref, f_ref, i_ref, g_ref, lb_ref, gn_ref, y_ref):
        lbv, gnv = _sig(lb_ref[0:1, :] - lb_ref[1:2, :]), gn_ref[...]
        tril, tril_bf, _ = _chunk_masks(nc, c)
        chunks = lambda ref: ref[0].reshape(nc, c, LANE)
        o = _hgrn_forward(chunks(q_ref), chunks(f_ref), chunks(i_ref), lbv, tril, tril_bf)["o"]
        r = lax.rsqrt(jnp.mean(o * o, axis=-1, keepdims=True) + EPS)
        hg = chunks(g_ref)
        y_ref[0] = (o * r * gnv * (hg * _sig(hg))).reshape(t_, LANE)

    return pl.pallas_call(
        body, name="hgrn_fwd", grid=(HG_HEADS, b_),
        in_specs=[col(0), col(1), col(2), col(3), vec, one], out_specs=blk,
        out_shape=jax.ShapeDtypeStruct((b_, t_, hw), F32),
        compiler_params=_params(2),
    )(zm, zm, zm, zm, lb, gn)


def _hgrn_bwd(zm, dy, lb, gn, hw, col0, dz, rider=None):
    b_, t_, _ = zm.shape
    c = min(HG_CHUNK, t_)
    nc = t_ // c
    col, vec, one, blk = _hgrn_specs(t_, col0)

    def body(q_ref, f_ref, i_ref, g_ref, dy_ref, lb_ref, gn_ref, _, dz_ref, dlb_ref, dgn_ref):
        h, b = pl.program_id(0), pl.program_id(1)
        lbv, gnv = _sig(lb_ref[0:1, :] - lb_ref[1:2, :]), gn_ref[...]
        tril, tril_bf, triu_bf = _chunk_masks(nc, c)
        last_row = lax.broadcasted_iota(jnp.int32, (nc, c, LANE), 1) == c - 1
        chunks = lambda ref: ref[0].reshape(nc, c, LANE)
        flat = lambda x: x.reshape(t_, LANE)
        hq, hi, hg = chunks(q_ref), chunks(i_ref), chunks(g_ref)
        p = _hgrn_forward(hq, chunks(f_ref), hi, lbv, tril, tril_bf)
        o, q, k, st_all, e_l = p["o"], p["q"], p["k"], p["st_all"], p["e_l"]
        dyv = chunks(dy_ref)
        sg = _sig(hg)
        r = lax.rsqrt(jnp.mean(o * o, axis=-1, keepdims=True) + EPS)
        dn = dyv * (hg * sg)
        dz_ref[0, :, 3 * LANE:] = flat(dyv * (o * r * gnv) * (sg * (1.0 + hg * (1.0 - sg)))).astype(dz_ref.dtype)
        dgn = jnp.sum(flat(dn * o * r), axis=0, keepdims=True)
        dng = dn * gnv
        do = r * dng - o * (r * r * r) * jnp.mean(dng * o, axis=-1, keepdims=True)
        back = _bdotp(do, p["qg"], 1, 1)
        dst = jnp.zeros((HG_D, HG_D), F32)
        dsts = [None] * nc
        for n in range(nc - 1, -1, -1):
            dsts[n] = dst
            dst = dst * e_l[n] + back[n]
        dst_all = jnp.stack(dsts)
        da = jnp.where(tril, _bdotp(do, hi, 2, 2), 0.0)
        dq = _bdotp(da, p["ke"], 2, 1) * p["e_q"] + _bdotp(do, st_all, 2, 1) * p["e_g"]
        dk_state = _bdotp(hi, dst_all, 2, 1) * p["e_s"]
        dk = _bdotp(da, p["qe"], 1, 1) * p["e_k"] + dk_state
        dz_ref[0, :, 2 * LANE:3 * LANE] = flat(_bdot(p["a"], do, 1, 1) + _bdot(p["kg"], dst_all, 2, 2)).astype(dz_ref.dtype)
        extra = (jnp.sum(k * dk_state, axis=1, keepdims=True) + e_l * jnp.sum(st_all * dst_all, axis=1, keepdims=True))
        dgc = q * dq - k * dk + jnp.where(last_row, extra, 0.0)
        dfv = _tri_dot_b(triu_bf, dgc) / p["f"] - dk
        sf, sq = p["sf"], p["sq"]
        dz_ref[0, :, LANE:2 * LANE] = flat(dfv * (1.0 - lbv) * sf * (1.0 - sf)).astype(dz_ref.dtype)
        dlb = jnp.sum(flat(dfv * (1.0 - sf)), axis=0, keepdims=True)
        dz_ref[0, :, :LANE] = flat(dq * (sq * (1.0 + hq * (1.0 - sq)))).astype(dz_ref.dtype)
        dl0 = dlb * lbv * (1.0 - lbv)
        _acc(dlb_ref, jnp.concatenate([dl0, -dl0], axis=0), b == 0)
        _acc(dgn_ref, dgn, jnp.logical_and(b == 0, h == 0))

    return _ride_call(
        body, rider, name="hgrn_bwd", grid=(HG_HEADS, b_),
        in_specs=[col(0), col(1), col(2), col(3), blk, vec, one, ANY],
        out_specs=[pl.BlockSpec((1, t_, 4 * LANE), lambda h, b: (b, 0, col0 // 4 + h)), vec, one],
        out_shape=[jax.ShapeDtypeStruct(dz.shape, dz.dtype), jax.ShapeDtypeStruct((2, hw), F32),
                   jax.ShapeDtypeStruct((1, LANE), F32)],
        scratch=[], args=(zm, zm, zm, zm, dy, lb, gn, dz), aliases={7: 0})


def _fox_logf(x):
    return jnp.minimum(x, 0.0) - jnp.log(1.0 + jnp.exp(-jnp.abs(x)))


def _fox_prep(zf, bias):
    b_, t_, _ = zf.shape
    tb = min(FOX_BLOCK, t_)
    nb = t_ // tb

    def body(z_ref, b_ref, fc_ref):
        tril_bf = (lax.broadcasted_iota(jnp.int32, (tb, tb), 0) >= lax.broadcasted_iota(jnp.int32, (tb, tb), 1)).astype(BF16)
        bv = b_ref[...]

        def blk(i, carry):
            rows = pl.ds(pl.multiple_of(i * tb, tb), tb)
            fc = _tri_dot(tril_bf, _fox_logf(z_ref[0, rows, :] + bv)) + carry
            fc_ref[0, rows, :] = fc
            return fc[tb - 1:tb, :]

        lax.fori_loop(0, nb, blk, jnp.zeros((1, LANE), F32))

    blk_spec = pl.BlockSpec((1, t_, LANE), lambda b: (b, 0, 0))
    return pl.pallas_call(
        body, name="fox_prep", grid=(b_,),
        in_specs=[blk_spec, pl.BlockSpec((1, LANE), lambda b: (0, 0))], out_specs=blk_spec,
        out_shape=jax.ShapeDtypeStruct((b_, t_, LANE), F32), compiler_params=_params(1),
    )(zf, bias)


def _fox_post(dfc, zf, bias):
    b_, t_, _ = zf.shape
    npair = dfc.shape[1]
    tb = min(FOX_BLOCK, t_)
    nb = t_ // tb

    def body(d_ref, z_ref, b_ref, dz_ref, db_ref):
        triu_bf = (lax.broadcasted_iota(jnp.int32, (tb, tb), 0) <= lax.broadcasted_iota(jnp.int32, (tb, tb), 1)).astype(BF16)
        valid = lax.broadcasted_iota(jnp.int32, (tb, LANE), 1) < FOX_HEADS
        bv = b_ref[...]

        def blk(m, carry):
            tail, db = carry
            rows = pl.ds(pl.multiple_of((nb - 1 - m) * tb, tb), tb)
            dfc_rows = d_ref[0, 0, rows, :]
            for p in range(1, npair):
                dfc_rows = dfc_rows + pltpu.roll(d_ref[0, p, rows, :], 2 * p, 1)
            dlf = _tri_dot(triu_bf, dfc_rows) + tail
            dx = jnp.where(valid, dlf * _sig(-(z_ref[0, rows, :] + bv)), 0.0)
            dz_ref[0, rows, :] = dx.astype(dz_ref.dtype)
            return dlf[0:1, :], db + jnp.sum(dx, axis=0, keepdims=True)

        z1 = jnp.zeros((1, LANE), F32)
        _, db = lax.fori_loop(0, nb, blk, (z1, z1))
        _acc(db_ref, db, pl.program_id(0) == 0)

    blk_spec = pl.BlockSpec((1, t_, LANE), lambda b: (b, 0, 0))
    vec = pl.BlockSpec((1, LANE), lambda b: (0, 0))
    return pl.pallas_call(
        body, name="fox_post", grid=(b_,),
        in_specs=[pl.BlockSpec((1, npair, t_, LANE), lambda b: (b, 0, 0, 0)), blk_spec, vec], out_specs=[blk_spec, vec],
        out_shape=[jax.ShapeDtypeStruct((b_, t_, LANE), MXU_DTYPE), jax.ShapeDtypeStruct((1, LANE), F32)],
        compiler_params=_params(1),
    )(dfc, zf, bias)


FOX_TILE = 256
FOX_BAND = 512
AUG = 64


def _head_mean_matrix():
    r = lax.broadcasted_iota(jnp.int32, (LANE, LANE), 0) // FOX_DH
    c = lax.broadcasted_iota(jnp.int32, (LANE, LANE), 1) // FOX_DH
    return (r == c).astype(BF16)


def _dot_right_exact(x, m_bf):
    hi = x.astype(BF16)
    r = x - hi.astype(F32)
    mid = r.astype(BF16)
    lo = (r - mid.astype(F32)).astype(BF16)

    def d(v):
        return lax.dot_general(v, m_bf, (((1,), (0,)), ((), ())), preferred_element_type=F32)

    return d(hi) + d(mid) + d(lo)


def _pair_norm(x, g2, bd):
    r = lax.rsqrt(_dot_right_exact(x * x, bd) * (1.0 / FOX_DH) + EPS)
    return x * r * g2, r


def _pair_norm_bwd(x, r, dy, g2, bd):
    dyg = dy * g2
    dx = r * dyg - x * (r * r * r) * (_dot_right_exact(dyg * x, bd) * (1.0 / FOX_DH))
    return dx, jnp.sum(dy * x * r, axis=0, keepdims=True)


def _head_lanes(xn, hh):
    return xn if hh == 0 else pltpu.roll(xn, FOX_DH, 1)


def _split3(x):
    hi = x.astype(BF16).astype(F32)
    mid = (x - hi).astype(BF16).astype(F32)
    return hi, mid, x - hi - mid


def _fox_operands(q_ref, k_ref, v_ref, fc_ref, gq2, gk2, p, qa, ka, va):
    t_ = q_ref.shape[1]
    bd = _head_mean_matrix()
    lane = lax.broadcasted_iota(jnp.int32, (t_, LANE), 1)
    qx, kx = q_ref[0], k_ref[0]
    qn, rq = _pair_norm(qx, gq2, bd)
    kn, rk = _pair_norm(kx, gk2, bd)
    vv = v_ref[0]
    q_aug = jnp.where(jnp.logical_and(lane >= AUG, lane < AUG + 3), 1.0, 0.0)
    for hh in range(2):
        fcol = jnp.sum(jnp.where(lane == 2 * p + hh, fc_ref[0], 0.0), axis=-1, keepdims=True)
        hi, mid, lo = _split3(-fcol)
        k_aug = jnp.where(lane == AUG, hi, jnp.where(lane == AUG + 1, mid, jnp.where(lane == AUG + 2, lo,
                          jnp.where(lane == AUG + 3, 1.0, 0.0))))
        head = lane < FOX_DH
        qa[hh] = jnp.where(head, _head_lanes(qn, hh), q_aug).astype(MXU_DTYPE)
        ka[hh] = jnp.where(head, _head_lanes(kn, hh), k_aug).astype(MXU_DTYPE)
        va[hh] = jnp.where(head, _head_lanes(vv, hh), 0.0).astype(MXU_DTYPE)
    return bd, lane, qx, kx, rq, rk


def _fox_specs(t_, fw, col0):
    npair = fw // LANE

    def col(off):
        return pl.BlockSpec((1, t_, LANE), lambda b, p: (b, 0, col0 + 3 * p + off))

    pair = pl.BlockSpec((1, t_, LANE), lambda b, p: (b, 0, p))
    full = pl.BlockSpec((1, t_, LANE), lambda b, p: (b, 0, 0))
    gvec = pl.BlockSpec((1, LANE), lambda b, p: (0, 0))
    lse = pl.BlockSpec((1, 1, t_, LANE), lambda b, p: (b, p, 0, 0))
    return col, pair, full, gvec, lse


def _fox_fwd(zm, fc, gq2, gk2, fw, col0):
    b_, t_, _ = zm.shape
    npair = fw // LANE
    tq = min(FOX_TILE, t_)
    bw = min(FOX_BAND, t_)
    nband, tpb = t_ // bw, bw // tq
    scale = FOX_DH ** -0.5
    col, pair, full, gvec, lse_spec = _fox_specs(t_, fw, col0)

    def body(q_ref, k_ref, v_ref, fc_ref, gq_ref, gk_ref, o_ref, lse_ref, qa, ka, va):
        p = pl.program_id(1)
        _fox_operands(q_ref, k_ref, v_ref, fc_ref, gq_ref[...] * scale, gk_ref[...], p, qa, ka, va)
        ahead = lax.broadcasted_iota(jnp.int32, (tq, bw), 1) - lax.broadcasted_iota(jnp.int32, (tq, bw), 0)
        lane = lax.broadcasted_iota(jnp.int32, (tq, LANE), 1)

        for band in range(nband):
            c0 = band * bw

            def qtile(ii, _, c0=c0):
                r0 = pl.multiple_of(c0 + ii * tq, tq)
                rows = pl.ds(r0, tq)
                keep = ahead <= r0 - c0
                res = []
                for hh in range(2):
                    qb = qa[hh, rows, :]
                    s_b = jnp.where(keep, _nt(qb, ka[hh, c0:c0 + bw, :]), NEG)
                    m = jnp.max(s_b, axis=-1, keepdims=True)
                    if c0:
                        s_a = _nt(qb, ka[hh, 0:c0, :])
                        m = jnp.maximum(m, jnp.max(s_a, axis=-1, keepdims=True))
                    p_b = jnp.exp(s_b - m)
                    l = jnp.sum(p_b, axis=-1, keepdims=True)
                    acc = _nn(p_b, va[hh, c0:c0 + bw, :])
                    if c0:
                        p_a = jnp.exp(s_a - m)
                        l = l + jnp.sum(p_a, axis=-1, keepdims=True)
                        acc = acc + _nn(p_a, va[hh, 0:c0, :])
                    res.append((acc / l, m + jnp.log(l)))
                (o0, e0), (o1, e1) = res
                o_ref[0, rows, :] = jnp.where(lane < FOX_DH, o0, pltpu.roll(o1, FOX_DH, 1))
                lse_ref[0, 0, rows, :] = jnp.where(lane == 0, e0, jnp.where(lane == 1, e1, 0.0))
                return 0

            lax.fori_loop(0, tpb, qtile, 0)

    return pl.pallas_call(
        body, name="fox_fwd", grid=(b_, npair),
        in_specs=[col(0), col(1), col(2), full, gvec, gvec],
        out_specs=[pair, lse_spec],
        out_shape=[jax.ShapeDtypeStruct((b_, t_, fw), F32), jax.ShapeDtypeStruct((b_, npair, t_, LANE), F32)],
        scratch_shapes=[pltpu.VMEM((2, t_, LANE), MXU_DTYPE)] * 3,
        compiler_params=_params(2),
    )(zm, zm, zm, fc, gq2, gk2)


def _norm_bwd(x, dy, g):
    r = lax.rsqrt(jnp.mean(x * x, axis=-1, keepdims=True) + EPS)
    dyg = dy * g
    dx = r * dyg - x * (r * r * r) * jnp.mean(dyg * x, axis=-1, keepdims=True)
    return dx, jnp.sum(dy * x * r, axis=0, keepdims=True)


def _fox_bwd(zm, o, do, lse, fc, gq2, gk2, fw, col0, dz, rider=None):
    b_, t_, _ = zm.shape
    npair = fw // LANE
    tq = min(FOX_TILE, t_)
    nb = t_ // tq
    bw = min(FOX_BAND, t_)
    nband, tpb = t_ // bw, bw // tq
    scale = FOX_DH ** -0.5
    col, pair, full, gvec, lse_spec = _fox_specs(t_, fw, col0)

    def body(q_ref, k_ref, v_ref, o_ref, do_ref, lse_ref, fc_ref, gq_ref, gk_ref, _,
             dz_ref, dfc_ref, dgq_ref, dgk_ref, qa, ka, va, da, rowv, dq_acc, dk_acc, dv_acc):
        b, p = pl.program_id(0), pl.program_id(1)
        gq2v, gk2v = gq_ref[...] * scale, gk_ref[...]
        bd, lane, qx, kx, rq, rk = _fox_operands(q_ref, k_ref, v_ref, fc_ref, gq2v, gk2v, p, qa, ka, va)
        head = lane < FOX_DH
        dov = do_ref[0]
        dsum = _dot_right_exact(dov * o_ref[0], bd)
        eye = (lax.broadcasted_iota(jnp.int32, (tq, tq), 0) == lax.broadcasted_iota(jnp.int32, (tq, tq), 1)).astype(F32)
        for hh in range(2):
            da[hh] = jnp.where(head, _head_lanes(dov, hh), 0.0).astype(MXU_DTYPE)
            for blk in range(nb):
                rs = slice(blk * tq, (blk + 1) * tq)
                rowv[2 * hh:2 * hh + 1, rs] = jnp.sum(eye * lse_ref[0, 0, rs, hh:hh + 1], axis=0, keepdims=True)
                rowv[2 * hh + 1:2 * hh + 2, rs] = jnp.sum(eye * dsum[rs, hh * FOX_DH:hh * FOX_DH + 1], axis=0, keepdims=True)
        dq_acc[...] = jnp.zeros(dq_acc.shape, F32)
        ahead = lax.broadcasted_iota(jnp.int32, (tq, bw), 1) - lax.broadcasted_iota(jnp.int32, (tq, bw), 0)

        def part(hh, kb, vb, lo, hi, keep):
            qm, dm = qa[hh, lo:hi, :], da[hh, lo:hi, :]
            pt = jnp.exp(_nt(kb, qm) - rowv[2 * hh:2 * hh + 1, lo:hi])
            if keep is not None:
                pt = jnp.where(keep, pt, 0.0)
            dst = pt * (_nt(vb, dm) - rowv[2 * hh + 1:2 * hh + 2, lo:hi])
            dq_acc[hh, lo:hi, :] += _tn(dst, kb)
            return _nn(dst, qm), _nn(pt, dm)

        for band in range(nband):
            c0 = band * bw

            def kvtile(jj, _, c0=c0):
                r0 = pl.multiple_of(c0 + jj * tq, tq)
                rows = pl.ds(r0, tq)
                keep = ahead >= r0 - c0
                for hh in range(2):
                    kb, vb = ka[hh, rows, :], va[hh, rows, :]
                    dk_t, dv_t = part(hh, kb, vb, c0, c0 + bw, keep)
                    if c0 + bw < t_:
                        dk_u, dv_u = part(hh, kb, vb, c0 + bw, t_, None)
                        dk_t, dv_t = dk_t + dk_u, dv_t + dv_u
                    dk_acc[hh, rows, :] = dk_t
                    dv_acc[hh, rows, :] = dv_t
                return 0

            lax.fori_loop(0, tpb, kvtile, 0)

        dq0, dq1, dk0, dk1 = dq_acc[0], dq_acc[1], dk_acc[0], dk_acc[1]
        dqn = jnp.where(head, dq0, pltpu.roll(dq1, FOX_DH, 1))
        dkn = jnp.where(head, dk0, pltpu.roll(dk1, FOX_DH, 1))
        dqx, gq_part = _pair_norm_bwd(qx, rq, dqn, gq2v, bd)
        dkx, gk_part = _pair_norm_bwd(kx, rk, dkn, gk2v, bd)
        dz_ref[0, :, :LANE] = dqx.astype(dz_ref.dtype)
        dz_ref[0, :, LANE:2 * LANE] = dkx.astype(dz_ref.dtype)
        dz_ref[0, :, 2 * LANE:] = jnp.where(head, dv_acc[0], pltpu.roll(dv_acc[1], FOX_DH, 1)).astype(dz_ref.dtype)

        def bias_grad(dqh, dkh):
            return (jnp.sum(jnp.where(lane == AUG + 3, dqh, 0.0), axis=-1, keepdims=True)
                    - jnp.sum(jnp.where(lane == AUG, dkh, 0.0), axis=-1, keepdims=True))

        dfc_ref[0, 0] = jnp.where(lane == 0, bias_grad(dq0, dk0), jnp.where(lane == 1, bias_grad(dq1, dk1), 0.0))
        first = jnp.logical_and(b == 0, p == 0)
        _acc(dgq_ref, gq_part * scale, first)
        _acc(dgk_ref, gk_part, first)

    gs = jax.ShapeDtypeStruct((1, LANE), F32)
    return _ride_call(
        body, rider, name="fox_bwd", grid=(b_, npair),
        in_specs=[col(0), col(1), col(2), pair, pair, lse_spec, full, gvec, gvec, ANY],
        out_specs=[pl.BlockSpec((1, t_, 3 * LANE), lambda b, p: (b, 0, col0 // 3 + p)), lse_spec, gvec, gvec],
        out_shape=[jax.ShapeDtypeStruct(dz.shape, dz.dtype), jax.ShapeDtypeStruct((b_, npair, t_, LANE), F32), gs, gs],
        scratch=[pltpu.VMEM((2, t_, LANE), MXU_DTYPE)] * 4
        + [pltpu.VMEM((8, t_), F32)] + [pltpu.VMEM((2, t_, LANE), F32)] * 3,
        args=(zm, zm, zm, o, do, lse, fc, gq2, gk2, dz), aliases={9: 0})


def _mem_specs(t_, m_, mw, col0):
    nh = mw // LANE
    qcol = pl.BlockSpec((1, t_, LANE), lambda b, h: (b, 0, col0 + h))
    kcol = pl.BlockSpec((1, m_, LANE), lambda b, h: (b, 0, h))
    vcol = pl.BlockSpec((1, m_, LANE), lambda b, h: (b, 0, nh + h))
    ycol = pl.BlockSpec((1, t_, LANE), lambda b, h: (b, 0, h))
    gvec = pl.BlockSpec((1, LANE), lambda b, h: (0, 0))
    return qcol, kcol, vcol, ycol, gvec


def _mem_fwd(zm, mkv, gq, gk, mw, col0):
    b_, t_, _ = zm.shape
    m_ = mkv.shape[1]
    tq = min(512, t_)
    nb = t_ // tq
    scale = MEM_DH ** -0.5
    qcol, kcol, vcol, ycol, gvec = _mem_specs(t_, m_, mw, col0)

    def body(q_ref, k_ref, v_ref, gq_ref, gk_ref, y_ref):
        gqv, gkv = gq_ref[...] * scale, gk_ref[...]
        kv = k_ref[0]
        kn = _mx(kv * lax.rsqrt(jnp.mean(kv * kv, axis=-1, keepdims=True) + EPS) * gkv)
        vv = _mx(v_ref[0])

        def blk(i, _):
            rows = pl.ds(pl.multiple_of(i * tq, tq), tq)
            qv = q_ref[0, rows, :]
            s = _nt(qv * lax.rsqrt(jnp.mean(qv * qv, axis=-1, keepdims=True) + EPS) * gqv, kn)
            e = jnp.exp(s - jnp.max(s, axis=-1, keepdims=True))
            y_ref[0, rows, :] = _nn(e / jnp.sum(e, axis=-1, keepdims=True), vv)
            return 0

        lax.fori_loop(0, nb, blk, 0)

    return pl.pallas_call(
        body, name="mem_fwd", grid=(b_, MEM_HEADS), in_specs=[qcol, kcol, vcol, gvec, gvec], out_specs=ycol,
        out_shape=jax.ShapeDtypeStruct((b_, t_, mw), F32), compiler_params=_params(2),
    )(zm, mkv, mkv, gq, gk)


def _mem_bwd(zm, mkv, dy, gq, gk, mw, col0, dz):
    b_, t_, _ = zm.shape
    m_ = mkv.shape[1]
    tq = min(512, t_)
    nb = t_ // tq
    scale = MEM_DH ** -0.5
    qcol, kcol, vcol, ycol, gvec = _mem_specs(t_, m_, mw, col0)

    def body(q_ref, k_ref, v_ref, dy_ref, gq_ref, gk_ref, _, dq_ref, dk_ref, dv_ref, dgq_ref, dgk_ref):
        gqv, gkv = gq_ref[...] * scale, gk_ref[...]
        kv = k_ref[0]
        kn = _mx(kv * lax.rsqrt(jnp.mean(kv * kv, axis=-1, keepdims=True) + EPS) * gkv)
        vv = _mx(v_ref[0])

        def blk(i, carry):
            dkn, dvv, dgq = carry
            rows = pl.ds(pl.multiple_of(i * tq, tq), tq)
            qv = q_ref[0, rows, :]
            qn = _mx(qv * lax.rsqrt(jnp.mean(qv * qv, axis=-1, keepdims=True) + EPS) * gqv)
            s = _nt(qn, kn)
            e = jnp.exp(s - jnp.max(s, axis=-1, keepdims=True))
            pm = e / jnp.sum(e, axis=-1, keepdims=True)
            dob = _mx(dy_ref[0, rows, :])
            dp = _nt(dob, vv)
            ds = pm * (dp - jnp.sum(dp * pm, axis=-1, keepdims=True))
            dqv, gq_part = _norm_bwd(qv, _nn(ds, kn), gqv)
            dq_ref[0, rows, :] = dqv.astype(dq_ref.dtype)
            return dkn + _tn(ds, qn), dvv + _tn(pm, dob), dgq + gq_part * scale

        z = jnp.zeros((m_, LANE), F32)
        dkn, dvv, dgq = lax.fori_loop(0, nb, blk, (z, z, jnp.zeros((1, LANE), F32)))
        dkv, dgk = _norm_bwd(kv, dkn, gkv)
        dk_ref[0] = dkv
        dv_ref[0] = dvv
        first = jnp.logical_and(pl.program_id(0) == 0, pl.program_id(1) == 0)
        _acc(dgq_ref, dgq, first)
        _acc(dgk_ref, dgk, first)

    kblk = pl.BlockSpec((1, m_, LANE), lambda b, h: (b, 0, h))
    gs = jax.ShapeDtypeStruct((1, LANE), F32)
    ks = jax.ShapeDtypeStruct((b_, m_, mw), F32)
    return pl.pallas_call(
        body, name="mem_bwd", grid=(b_, MEM_HEADS), in_specs=[qcol, kcol, vcol, ycol, gvec, gvec, ANY],
        out_specs=[qcol, kblk, kblk, gvec, gvec],
        out_shape=[jax.ShapeDtypeStruct(dz.shape, dz.dtype), ks, ks, gs, gs], input_output_aliases={6: 0},
        compiler_params=_params(2),
    )(zm, mkv, mkv, dy, gq, gk, dz)


def _merge_specs(tm, d, w, gcol):
    row_d = pl.BlockSpec((tm, d), lambda i: (i, 0))
    row_w = pl.BlockSpec((tm, w), lambda i: (i, 0))
    gates = [pl.BlockSpec((tm, d), functools.partial(lambda i, k: (i, gcol + k), k=k)) for k in range(3)]
    w_br = pl.BlockSpec((w, d), lambda i: (0, 0))
    w_o = pl.BlockSpec((d, d), lambda i: (0, 0))
    return row_d, row_w, gates, w_br, w_o


def _merge_fwd(x, ys, zm, w_brs, w_out, gcol, tm=256):
    n, d = x.shape
    w = ys[0].shape[1]
    tm = _tile(n, tm, 8)
    row_d, row_w, gates, w_br, w_o = _merge_specs(tm, d, w, gcol)

    def body(x_ref, ya, yb, yc, g0, g1, g2, wa, wb, wc, wo, x1_ref, mg_ref):
        mg = (_sig(g0[...]) * _nn(ya[...], wa[...]) + _sig(g1[...]) * _nn(yb[...], wb[...])
              + _sig(g2[...]) * _nn(yc[...], wc[...]))
        mg_ref[...] = mg.astype(mg_ref.dtype)
        x1_ref[...] = x_ref[...] + _nn(mg, wo[...])

    return pl.pallas_call(
        body, name="merge_fwd", grid=(n // tm,),
        in_specs=[row_d, row_w, row_w, row_w] + gates + [w_br, w_br, w_br, w_o],
        out_specs=[row_d, row_d],
        out_shape=[jax.ShapeDtypeStruct((n, d), F32), jax.ShapeDtypeStruct((n, d), MXU_DTYPE)],
        compiler_params=_params(1),
    )(x, *ys, zm, zm, zm, *w_brs, w_out)


def _merge_bwd(dx1, ys, zm, w_brs, w_out, gcol, tm=256):
    n, d = dx1.shape
    w = ys[0].shape[1]
    tm = _tile(n, tm, 8)
    row_d, row_w, gates, w_br, w_o = _merge_specs(tm, d, w, gcol)

    def body(dx_ref, ya, yb, yc, g0, g1, g2, wa, wb, wc, wo, dgl_ref, dpa, dpb, dpc, dya, dyb, dyc):
        dm = _nt(dx_ref[...], wo[...])
        for k, (y, g, wr, dp_ref, dy_ref) in enumerate(((ya, g0, wa, dpa, dya), (yb, g1, wb, dpb, dyb),
                                                        (yc, g2, wc, dpc, dyc))):
            sg = _sig(g[...])
            pr = _nn(y[...], wr[...])
            dgl_ref[:, k * d:(k + 1) * d] = (dm * pr * sg * (1.0 - sg)).astype(dgl_ref.dtype)
            dp = (dm * sg).astype(dp_ref.dtype)
            dp_ref[...] = dp
            dy_ref[...] = _nt(dp, wr[...])

    sd = jax.ShapeDtypeStruct((n, d), MXU_DTYPE)
    sw = jax.ShapeDtypeStruct((n, w), F32)
    return pl.pallas_call(
        body, name="merge_bwd", grid=(n // tm,),
        in_specs=[row_d, row_w, row_w, row_w] + gates + [w_br, w_br, w_br, w_o],
        out_specs=[pl.BlockSpec((tm, 3 * d), lambda i: (i, 0)), row_d, row_d, row_d, row_w, row_w, row_w],
        out_shape=[jax.ShapeDtypeStruct((n, zm.shape[1]), MXU_DTYPE), sd, sd, sd, sw, sw, sw],
        compiler_params=_params(1),
    )(dx1, *ys, zm, zm, zm, *w_brs, w_out)


CONV_ROWS = 256
HALO = 8


def _ext(ref, r0, t_):
    rc = min(CONV_ROWS, t_)
    a, b = max(r0 - HALO, 0), min(r0 + rc + HALO, t_)
    parts = []
    if r0 - HALO < 0:
        parts.append(jnp.zeros((HALO, ref.shape[2]), F32))
    parts.append(ref[0, a:b, :].astype(F32))
    if r0 + rc + HALO > t_:
        parts.append(jnp.zeros((HALO, ref.shape[2]), F32))
    return jnp.concatenate(parts, axis=0) if len(parts) > 1 else parts[0]


def _gelu_parts(ac):
    cdf = 0.5 * (1.0 + _erf(ac * (2.0 ** -0.5)))
    pdf = jnp.exp(-0.5 * ac * ac) * ((2.0 * math.pi) ** -0.5)
    return cdf, pdf


def _conv_taps(a_ext, cw, cb):
    return cw[0:1, :] * pltpu.roll(a_ext, 2, 0) + cw[1:2, :] * pltpu.roll(a_ext, 1, 0) + cw[2:3, :] * a_ext + cb


def _glu_specs(t_, f, g):
    gate = pl.BlockSpec((1, t_, g), lambda j, b: (b, 0, j))
    value = pl.BlockSpec((1, t_, g), lambda j, b: (b, 0, f // g + j))
    cwb = pl.BlockSpec((3, g), lambda j, b: (0, j))
    cbb = pl.BlockSpec((1, g), lambda j, b: (0, j))
    return gate, value, cwb, cbb


def _glu_fwd(u, cw, cb):
    b_, t_, f2 = u.shape
    f = f2 // 2
    g = min(FFN_GROUP, f)
    rc = min(CONV_ROWS, t_)
    gate, value, cwb, cbb = _glu_specs(t_, f, g)

    def body(a_ref, v_ref, cw_ref, cb_ref, y_ref):
        cwv, cbv = cw_ref[...], cb_ref[...]
        for r0 in range(0, t_, rc):
            ac = _conv_taps(_ext(a_ref, r0, t_), cwv, cbv)[HALO:HALO + rc]
            cdf, _ = _gelu_parts(ac)
            y_ref[0, r0:r0 + rc, :] = (ac * cdf * v_ref[0, r0:r0 + rc, :]).astype(y_ref.dtype)

    return pl.pallas_call(
        body, name="glu_fwd", grid=(f // g, b_), in_specs=[gate, value, cwb, cbb], out_specs=gate,
        out_shape=jax.ShapeDtypeStruct((b_, t_, f), MXU_DTYPE), compiler_params=_params(2),
    )(u, u, cw, cb)


def _glu_bwd(u, dy, cw, cb):
    b_, t_, f2 = u.shape
    f = f2 // 2
    g = min(FFN_GROUP, f)
    rc = min(CONV_ROWS, t_)
    ne = rc + 2 * HALO
    gate, value, cwb, cbb = _glu_specs(t_, f, g)

    def body(a_ref, v_ref, dy_ref, cw_ref, cb_ref, da_ref, dv_ref, dcw_ref, dcb_ref):
        cwv, cbv = cw_ref[...], cb_ref[...]
        dcw = [jnp.zeros((1, g), F32) for _ in range(3)]
        dcb = jnp.zeros((1, g), F32)
        for r0 in range(0, t_, rc):
            a_ext, v_ext, dy_ext = _ext(a_ref, r0, t_), _ext(v_ref, r0, t_), _ext(dy_ref, r0, t_)
            ac = _conv_taps(a_ext, cwv, cbv)
            cdf, pdf = _gelu_parts(ac)
            dac = dy_ext * v_ext * (cdf + ac * pdf)
            da = cwv[2:3, :] * dac + cwv[1:2, :] * pltpu.roll(dac, ne - 1, 0) + cwv[0:1, :] * pltpu.roll(dac, ne - 2, 0)
            mid = slice(HALO, HALO + rc)
            da_ref[0, r0:r0 + rc, :] = da[mid].astype(da_ref.dtype)
            dv_ref[0, r0:r0 + rc, :] = (dy_ext[mid] * ac[mid] * cdf[mid]).astype(dv_ref.dtype)
            dacm = dac[mid]
            dcw[0] = dcw[0] + jnp.sum(dacm * pltpu.roll(a_ext, 2, 0)[mid], axis=0, keepdims=True)
            dcw[1] = dcw[1] + jnp.sum(dacm * pltpu.roll(a_ext, 1, 0)[mid], axis=0, keepdims=True)
            dcw[2] = dcw[2] + jnp.sum(dacm * a_ext[mid], axis=0, keepdims=True)
            dcb = dcb + jnp.sum(dacm, axis=0, keepdims=True)
        first = pl.program_id(1) == 0
        _acc(dcw_ref, jnp.concatenate(dcw, axis=0), first)
        _acc(dcb_ref, dcb, first)

    sds = jax.ShapeDtypeStruct((b_, t_, f), MXU_DTYPE)
    return pl.pallas_call(
        body, name="glu_bwd", grid=(f // g, b_), in_specs=[gate, value, gate, cwb, cbb],
        out_specs=[gate, gate, cwb, cbb],
        out_shape=[sds, sds, jax.ShapeDtypeStruct((3, f), F32), jax.ShapeDtypeStruct((1, f), F32)],
        compiler_params=_params(2),
    )(u, u, dy, cw, cb)


def _place():
    x, y, c = lax.axis_index("x"), lax.axis_index("y"), lax.axis_index("c")
    chips = [(1 - x, y), (x, 1 - y), (1 - x, 1 - y)]
    return x, y, c, chips


def _remote(src, dst, send_sem, recv_sem, to):
    return pltpu.make_async_remote_copy(src_ref=src, dst_ref=dst, send_sem=send_sem, recv_sem=recv_sem,
                                        device_id=to, device_id_type=MESH)


STACK, COLS = "stack", "cols"


def _shard_ref(ref, kind, s, rows, c):
    if kind == COLS:
        cols = pl.ds(pl.multiple_of(s * c, LANE), c)
        return ref.at[:, cols] if rows is None else ref.at[rows, cols]
    return ref.at[s] if rows is None else ref.at[s, rows, :]


def _halves(c, half):
    mine = pl.ds(pl.multiple_of(c * half, 16), half)
    theirs = pl.ds(pl.multiple_of((1 - c) * half, 16), half)
    return mine, theirs


def _gather_parts(kinds):
    def first_copies(ins, outs, sems):
        x, y, c, chips = _place()
        me = 2 * x + y
        cps = []
        for i, (w_ref, o_ref, kind) in enumerate(zip(ins, outs, kinds)):
            r, cw = w_ref.shape
            mine, _ = _halves(c, r // 2)
            for j, chip in enumerate(chips):
                cps.append(_remote(w_ref.at[mine], _shard_ref(o_ref, kind, me, mine, cw), sems[0].at[6 * i + j],
                                   sems[1].at[6 * i + j], (*chip, c)))
        return cps

    def start(ins, outs, sems):
        for cp in first_copies(ins, outs, sems):
            cp.start()

    def finish(ins, outs, sems):
        x, y, c, chips = _place()
        sib = (x, y, 1 - c)
        passed = []
        for i, (w_ref, o_ref, kind) in enumerate(zip(ins, outs, kinds)):
            r, cw = w_ref.shape
            mine, _ = _halves(c, r // 2)
            for j, (px, py) in enumerate(chips):
                blk = _shard_ref(o_ref, kind, 2 * px + py, mine, cw)
                _remote(blk, blk, sems[0].at[6 * i + j], sems[1].at[6 * i + j], sib).wait_recv()
                passed.append(_remote(blk, blk, sems[0].at[6 * i + 3 + j], sems[1].at[6 * i + 3 + j], sib))
                passed[-1].start()
        for i, (w_ref, o_ref, kind) in enumerate(zip(ins, outs, kinds)):
            r, cw = w_ref.shape
            _, theirs = _halves(c, r // 2)
            for j, (px, py) in enumerate(chips):
                blk = _shard_ref(o_ref, kind, 2 * px + py, theirs, cw)
                _remote(blk, blk, sems[0].at[6 * i + 3 + j], sems[1].at[6 * i + 3 + j], sib).wait_recv()
        for cp in first_copies(ins, outs, sems) + passed:
            cp.wait_send()

    return start, finish


def _gather_shapes(shards, kinds):
    return [jax.ShapeDtypeStruct((a.shape[0], N_CHIPS * a.shape[1]) if k == COLS else (N_CHIPS,) + a.shape, a.dtype)
            for a, k in zip(shards, kinds)]


def _gather_sems(nw):
    return [pltpu.SemaphoreType.DMA((6 * nw,)), pltpu.SemaphoreType.DMA((6 * nw,))]


def _gather_shards(shards, kinds):
    nw = len(shards)
    start, finish = _gather_parts(kinds)

    def body(*refs):
        ins, outs, sems = refs[:nw], refs[nw:2 * nw], refs[2 * nw:]
        start(ins, outs, sems)
        finish(ins, outs, sems)

    return pl.pallas_call(
        body, name="gather_shards", in_specs=[ANY] * nw, out_specs=[ANY] * nw,
        out_shape=_gather_shapes(shards, kinds), scratch_shapes=_gather_sems(nw),
    )(*shards)


def _gather_rider(shards, kinds):
    start, finish = _gather_parts(kinds)
    return _Rider(list(shards), _gather_shapes(shards, kinds), _gather_sems(len(shards)), start, finish)


def _half_shape(g, kind):
    if kind == COLS:
        return (g.shape[0] // 2, g.shape[1])
    return (g.shape[0], g.shape[1] // 2, g.shape[2])


def _swap_parts(kinds):
    def copies(ins, outs, sems):
        x, y, c, _ = _place()
        cps = []
        for i, (g_ref, a_ref, kind) in enumerate(zip(ins, outs, kinds)):
            r = g_ref.shape[0] if kind == COLS else g_ref.shape[1]
            _, theirs = _halves(c, r // 2)
            src = g_ref.at[theirs] if kind == COLS else g_ref.at[:, theirs]
            cps.append(_remote(src, a_ref, sems[0].at[i], sems[1].at[i], (x, y, 1 - c)))
        return cps

    def start(ins, outs, sems):
        for cp in copies(ins, outs, sems):
            cp.start()

    def finish(ins, outs, sems):
        for cp in copies(ins, outs, sems):
            cp.wait()

    return start, finish


def _swap_shapes(gs, kinds):
    return [jax.ShapeDtypeStruct(_half_shape(g, k), g.dtype) for g, k in zip(gs, kinds)]


def _pair_swap_halves(gs, kinds, name):
    nw = len(gs)
    start, finish = _swap_parts(kinds)

    def body(*refs):
        ins, outs, sems = refs[:nw], refs[nw:2 * nw], refs[2 * nw:]
        start(ins, outs, sems)
        finish(ins, outs, sems)

    return pl.pallas_call(
        body, name=name, in_specs=[ANY] * nw, out_specs=[ANY] * nw, out_shape=_swap_shapes(gs, kinds),
        scratch_shapes=[pltpu.SemaphoreType.DMA((nw,)), pltpu.SemaphoreType.DMA((nw,))],
    )(*gs)


def _swap_rider(gs, kinds):
    start, finish = _swap_parts(kinds)
    nw = len(gs)
    return _Rider(list(gs), _swap_shapes(gs, kinds), [pltpu.SemaphoreType.DMA((nw,)), pltpu.SemaphoreType.DMA((nw,))],
                  start, finish)


def _row_tile(rows, width, itemsize=4, target=2 ** 21):
    return _tile(rows, max(8, target // (width * itemsize)), 8)


def _add_half(g, a, kind, c_idx, name):
    if kind == COLS:
        half, wd = a.shape
        tr = _row_tile(half, wd)
        nblk = half // tr
        grid = (nblk,)
        g_spec = pl.BlockSpec((tr, wd), lambda i, c_ref: (c_ref[0] * nblk + i, 0))
        a_spec = pl.BlockSpec((tr, wd), lambda i, c_ref: (i, 0))
    else:
        n, half, wd = a.shape
        tr = _row_tile(half, wd)
        nblk = half // tr
        grid = (n, nblk)
        g_spec = pl.BlockSpec((1, tr, wd), lambda s, i, c_ref: (s, c_ref[0] * nblk + i, 0))
        a_spec = pl.BlockSpec((1, tr, wd), lambda s, i, c_ref: (s, i, 0))

    def body(c_ref, g_ref, a_ref, o_ref):
        o_ref[...] = (g_ref[...] + a_ref[...]).astype(o_ref.dtype)

    return pl.pallas_call(
        body, name=name,
        grid_spec=pltpu.PrefetchScalarGridSpec(num_scalar_prefetch=1, grid=grid, in_specs=[g_spec, a_spec],
                                               out_specs=a_spec),
        out_shape=jax.ShapeDtypeStruct(a.shape, EXCHANGE_DTYPE), compiler_params=_params(len(grid)),
    )(c_idx, g, a)


def _exchange_parts(kinds):
    def copies(ins, outs, sems):
        x, y, c, chips = _place()
        me = 2 * x + y
        cps = []
        for i, (p_ref, b_ref, kind) in enumerate(zip(ins, outs, kinds)):
            cw = b_ref.shape[2]
            for j, (px, py) in enumerate(chips):
                cps.append(_remote(_shard_ref(p_ref, kind, 2 * px + py, None, cw), b_ref.at[me],
                                   sems[0].at[3 * i + j], sems[1].at[3 * i + j], (px, py, c)))
        return cps

    def start(ins, outs, sems):
        for cp in copies(ins, outs, sems):
            cp.start()

    def finish(ins, outs, sems):
        x, y, c, chips = _place()
        for i, b_ref in enumerate(outs):
            for j, (px, py) in enumerate(chips):
                blk = b_ref.at[2 * px + py]
                _remote(blk, blk, sems[0].at[3 * i + j], sems[1].at[3 * i + j], (px, py, c)).wait_recv()
        for cp in copies(ins, outs, sems):
            cp.wait_send()

    return start, finish


def _exchange_shapes(ps, kinds):
    return [jax.ShapeDtypeStruct((N_CHIPS,) + ((p.shape[0], p.shape[1] // N_CHIPS) if k == COLS else tuple(p.shape[1:])),
                                 p.dtype) for p, k in zip(ps, kinds)]


def _exchange_sems(nw):
    return [pltpu.SemaphoreType.DMA((3 * nw,)), pltpu.SemaphoreType.DMA((3 * nw,))]


def _exchange_rider(ps, kinds):
    start, finish = _exchange_parts(kinds)
    return _Rider(list(ps), _exchange_shapes(ps, kinds), _exchange_sems(len(ps)), start, finish)


def _sum_chips(bq, name):
    n, h, wd = bq.shape
    tr = _row_tile(h, wd * n)

    def body(b_ref, o_ref):
        acc = b_ref[0].astype(F32)
        for s in range(1, n):
            acc = acc + b_ref[s].astype(F32)
        o_ref[...] = acc

    return pl.pallas_call(
        body, name=name, grid=(h // tr,),
        in_specs=[pl.BlockSpec((n, tr, wd), lambda i: (0, i, 0))], out_specs=pl.BlockSpec((tr, wd), lambda i: (i, 0)),
        out_shape=jax.ShapeDtypeStruct((h, wd), F32), compiler_params=_params(1),
    )(bq)


def _pair_join_halves(qs):
    nw = len(qs)

    def body(*refs):
        ins, outs = refs[:nw], refs[nw:2 * nw]
        send_sems, recv_sems = refs[2 * nw:]
        x, y, c, _ = _place()
        sent = []
        for i, (q_ref, o_ref) in enumerate(zip(ins, outs)):
            mine, _ = _halves(c, q_ref.shape[0])
            sent.append(_remote(q_ref, o_ref.at[mine], send_sems.at[i], recv_sems.at[i], (x, y, 1 - c)))
            sent[-1].start()
        for i, (q_ref, o_ref) in enumerate(zip(ins, outs)):
            _, theirs = _halves(c, q_ref.shape[0])
            _remote(q_ref, o_ref.at[theirs], send_sems.at[i], recv_sems.at[i], (x, y, 1 - c)).wait_recv()
        for cp in sent:
            cp.wait_send()

    return pl.pallas_call(
        body, name="pair_join_halves", in_specs=[ANY] * nw, out_specs=[ANY] * nw,
        out_shape=[jax.ShapeDtypeStruct((2 * q.shape[0], q.shape[1]), q.dtype) for q in qs],
        scratch_shapes=[pltpu.SemaphoreType.DMA((nw,)), pltpu.SemaphoreType.DMA((nw,))],
    )(*qs)


def _all_sum_small(s, name):
    sr, w = s.shape

    def body(s_ref, o_ref, buf, send_sems, recv_sems):
        x, y, c, _ = _place()
        me = 4 * x + 2 * y + c
        buf[me] = s_ref[...]
        peers = []
        for k in range(1, 8):
            px = 1 - x if k & 4 else x
            py = 1 - y if k & 2 else y
            pc = 1 - c if k & 1 else c
            peers.append((px, py, pc))
        sent = [_remote(s_ref, buf.at[me], send_sems.at[k], recv_sems.at[k], peer) for k, peer in enumerate(peers)]
        for cp in sent:
            cp.start()
        for k, (px, py, pc) in enumerate(peers):
            _remote(s_ref, buf.at[4 * px + 2 * py + pc], send_sems.at[k], recv_sems.at[k], (px, py, pc)).wait_recv()
        for cp in sent:
            cp.wait_send()
        acc = buf[0]
        for d in range(1, 8):
            acc = acc + buf[d]
        o_ref[...] = acc

    vm = pl.BlockSpec(memory_space=pltpu.VMEM)
    return pl.pallas_call(
        body, name=name, in_specs=[vm], out_specs=vm, out_shape=jax.ShapeDtypeStruct((sr, w), F32),
        scratch_shapes=[pltpu.VMEM((8, sr, w), F32), pltpu.SemaphoreType.DMA((7,)), pltpu.SemaphoreType.DMA((7,))],
    )(s)


BIG = ("w_in", "mem_kv_w", "w_br_hgrn", "w_br_fox", "w_br_mem", "w_out", "ffn_w_up", "ffn_w_down")
KIND = {"w_in": STACK, "mem_kv_w": STACK, "w_br_hgrn": COLS, "w_br_fox": COLS, "w_br_mem": COLS, "w_out": STACK,
        "ffn_w_up": STACK, "ffn_w_down": STACK}
ROW_SHARDED = ("mem_kv_w", "w_out", "ffn_w_down")
FIRST = ("w_in",)
REST = tuple(nm for nm in BIG if nm not in FIRST)
LAST = ("w_in",)
TRANSPOSED = ("w_in",)


def _z_layout(d, hw, fw, mw):
    gate, npair, nh, nm = 3 * d // LANE, fw // LANE, hw // LANE, mw // LANE
    fox0, hg0 = gate, gate + 3 * npair
    o_fox, o_mem = 4 * nh, 4 * nh + 3 * npair
    order = [o_mem + nm + j for j in range(gate)]
    order += [o_fox + k * npair + p for p in range(npair) for k in range(3)]
    order += [k * nh + h for h in range(nh) for k in range(4)]
    order += [o_mem + h for h in range(nm)]
    assert fox0 % 3 == 0 and hg0 % 4 == 0
    return fox0, hg0, hg0 + 4 * nh, order


def _reorder_blocks(a, order):
    runs, start = [], 0
    for i in range(1, len(order) + 1):
        if i == len(order) or order[i] != order[i - 1] + 1:
            runs.append((order[start], order[i - 1] + 1))
            start = i
    return jnp.concatenate([a[:, lo * LANE:hi * LANE] for lo, hi in runs], axis=1)


def _put_shard(arr, kind, s, piece):
    if kind == COLS:
        return lax.dynamic_update_slice(arr, piece, (0, s * piece.shape[1]))
    return lax.dynamic_update_slice(arr, piece[None], (s, 0, 0))


def _take_shard(arr, kind, s):
    if kind == COLS:
        return lax.dynamic_slice(arr, (0, s * (arr.shape[1] // N_CHIPS)), (arr.shape[0], arr.shape[1] // N_CHIPS))
    return lax.dynamic_index_in_dim(arr, s, 0, keepdims=False)


def _w_in_pieces(cs, s1, nf):
    out = []
    for s in range(N_CHIPS):
        lo, hi = cs * s, cs * (s + 1)
        for a, b, forget in ((lo, min(hi, s1), False), (max(lo, s1), min(hi, s1 + nf), True), (max(lo, s1 + nf), hi, False)):
            if a < b:
                out.append((s, a - lo, b - lo, forget, a - s1 if forget else (a if a < s1 else a - nf)))
    return out


def _split_w_in(stacked, s1, nf):
    pieces = _w_in_pieces(stacked.shape[2], s1, nf)
    main = [stacked[s, :, a:b] for s, a, b, forget, _ in pieces if not forget]
    ff = [stacked[s, :, a:b] for s, a, b, forget, _ in pieces if forget]
    return jnp.concatenate(main, axis=1), jnp.concatenate(ff, axis=1)


def _join_w_in(g_main, g_ff, s1, nf):
    cs = (g_main.shape[1] + nf) // N_CHIPS
    shards = [[] for _ in range(N_CHIPS)]
    for s, a, b, forget, off in _w_in_pieces(cs, s1, nf):
        shards[s].append((g_ff if forget else g_main)[:, off:off + b - a])
    return jnp.stack([jnp.concatenate(p, axis=1) if len(p) > 1 else p[0] for p in shards])


SMALL = ("norm_mix_g", "norm_mem_g", "norm_ffn_g", "hgrn_lb_logits", "hgrn_norm_g", "fox_f_bias", "fox_q_norm_g",
         "fox_k_norm_g", "mem_q_norm_g", "mem_k_norm_g", "ffn_conv_b")


def _small_rows(shapes):
    rows = []
    for a, (r, c) in enumerate(shapes):
        for i in range(r):
            for lo in range(0, c, FLAT_W):
                rows.append((a, i, lo, min(FLAT_W, c - lo)))
    return rows


def _pack_small(vals):
    rows = _small_rows([v.shape for v in vals])
    sr = -(-len(rows) // 8) * 8

    def body(*refs):
        o_ref = refs[-1]
        o_ref[...] = jnp.zeros(o_ref.shape, F32)
        for k, (a, i, lo, wd) in enumerate(rows):
            o_ref[k:k + 1, 0:wd] = refs[a][i:i + 1, lo:lo + wd]

    vm = pl.BlockSpec(memory_space=pltpu.VMEM)
    return pl.pallas_call(body, name="pack_small", in_specs=[vm] * len(vals), out_specs=vm,
                          out_shape=jax.ShapeDtypeStruct((sr, FLAT_W), F32))(*vals)


def _row_of(buf_ref, rows, a, i):
    parts = [buf_ref[k:k + 1, 0:wd] for k, (a2, i2, _, wd) in enumerate(rows) if (a2, i2) == (a, i)]
    return jnp.concatenate(parts, axis=1) if len(parts) > 1 else parts[0]


def _unpack_small(buf, shapes):
    rows = _small_rows(shapes)

    def body(buf_ref, *outs):
        for a, (r, _) in enumerate(shapes):
            for i in range(r):
                outs[a][i:i + 1, :] = _row_of(buf_ref, rows, a, i)

    vm = pl.BlockSpec(memory_space=pltpu.VMEM)
    return pl.pallas_call(body, name="unpack_small", in_specs=[vm], out_specs=[vm] * len(shapes),
                          out_shape=[jax.ShapeDtypeStruct(shp, F32) for shp in shapes])(buf)


def _adamw_small(buf, shapes, ws, ms, vs):
    n = len(ws)
    rows = _small_rows(shapes)
    c1 = 1.0 / (1.0 - ADAM_B1 ** ADAM_STEP)
    c2 = 1.0 / (1.0 - ADAM_B2 ** ADAM_STEP)

    def body(buf_ref, *refs):
        w_refs, m_refs, v_refs = refs[:n], refs[n:2 * n], refs[2 * n:3 * n]
        outs = refs[3 * n:]
        g_out, d_out, m_out, v_out, rest = outs[:n], outs[n:2 * n], outs[2 * n:3 * n], outs[3 * n:4 * n], outs[4 * n:]
        for a, (r, _) in enumerate(shapes):
            for i in range(r):
                gv = _row_of(buf_ref, rows, a, i)
                if a >= n:
                    rest[a - n][i:i + 1, :] = gv
                    continue
                row = slice(i, i + 1)
                mn = ADAM_B1 * m_refs[a][row, :] + (1.0 - ADAM_B1) * gv
                vn = ADAM_B2 * v_refs[a][row, :] + (1.0 - ADAM_B2) * (gv * gv)
                g_out[a][row, :] = gv
                d_out[a][row, :] = -ADAM_LR * ((mn * c1) / (jnp.sqrt(vn * c2) + ADAM_EPS) + ADAM_WD * w_refs[a][row, :])
                m_out[a][row, :] = mn
                v_out[a][row, :] = vn

    vm = pl.BlockSpec(memory_space=pltpu.VMEM)
    own = [jax.ShapeDtypeStruct(shp, F32) for shp in shapes[:n]]
    outs = pl.pallas_call(
        body, name="adamw_small", in_specs=[vm] * (1 + 3 * n), out_specs=[vm] * (4 * n + len(shapes) - n),
        out_shape=own * 4 + [jax.ShapeDtypeStruct(shp, F32) for shp in shapes[n:]],
    )(buf, *ws, *ms, *vs)
    return outs[:n], outs[n:2 * n], outs[2 * n:3 * n], outs[3 * n:4 * n], outs[4 * n:]


def _pad_lanes(v, width=LANE):
    return jnp.pad(v, ((0, 0), (0, width - v.shape[1])))


WEIGHTS = ("norm_mix_g", "norm_mem_g", "w_in", "hgrn_lb_logits", "hgrn_norm_g", "fox_f_bias", "fox_q_norm_g",
           "fox_k_norm_g", "mem_kv_w", "mem_q_norm_g", "mem_k_norm_g", "w_br_hgrn", "w_br_fox", "w_br_mem", "w_out",
           "norm_ffn_g", "ffn_w_up", "ffn_conv_w", "ffn_conv_b", "ffn_w_down")


def _local_step(x, mem, target, w, full, conv_w, late=None, hooks=None):
    b_, t_, d = x.shape
    n = b_ * t_
    hw, fw, mw = HG_HEADS * HG_D, FOX_HEADS * FOX_DH, MEM_HEADS * MEM_DH
    m_ = mem.shape[1]
    f = conv_w.shape[1]
    s1 = 4 * hw + 3 * fw
    fox_col, hg_col, mem_col, order = _z_layout(d, hw, fw, mw)
    gate_col = 0
    inverse = [order.index(j) for j in range(len(order))]

    w_main, w_ff = _split_w_in(full["w_in"], s1, FOX_HEADS)
    w_main = _reorder_blocks(w_main, order)
    w_ff = _pad_lanes(w_ff)
    f_bias = _pad_lanes(w["fox_f_bias"])
    cb = w["ffn_conv_b"]

    x2 = x.reshape(n, d)
    h = _rmsnorm_fwd(x2, w["norm_mix_g"], name="norm_mix_fwd")
    if late:
        zm, gathered = _matmul(h, w_main, name="in_proj", rider=_gather_rider(late[0], late[1]))
        full = {**full, **late[2](gathered)}
    else:
        zm = _matmul(h, w_main, name="in_proj")
    w_up = full["ffn_w_up"]
    w_brs = [full["w_br_hgrn"], full["w_br_fox"], full["w_br_mem"]]
    w_out, w_kv, w_down = full["w_out"], full["mem_kv_w"], full["ffn_w_down"]
    zf = _matmul(h, w_ff, name="in_proj_forget")
    zm3, zf3 = zm.reshape(b_, t_, -1), zf.reshape(b_, t_, LANE)
    ya = _hgrn_fwd(zm3, w["hgrn_lb_logits"], w["hgrn_norm_g"], hw, hg_col)
    fc = _fox_prep(zf3, f_bias)
    fox_gq, fox_gk = jnp.tile(w["fox_q_norm_g"], (1, 2)), jnp.tile(w["fox_k_norm_g"], (1, 2))
    yb, lse = _fox_fwd(zm3, fc, fox_gq, fox_gk, fw, fox_col)
    mem2 = mem.reshape(b_ * m_, d)
    hm = _rmsnorm_fwd(mem2, w["norm_mem_g"], name="norm_mem_fwd")
    mkv = _matmul(hm, w_kv, name="mem_kv_proj").reshape(b_, m_, 2 * mw)
    yc = _mem_fwd(zm3, mkv, w["mem_q_norm_g"], w["mem_k_norm_g"], mw, mem_col)
    ys = [ya.reshape(n, hw), yb.reshape(n, fw), yc.reshape(n, mw)]
    x1, merged = _merge_fwd(x2, ys, zm, w_brs, w_out, gate_col)
    h2 = _rmsnorm_fwd(x1, w["norm_ffn_g"], name="norm_ffn_fwd")
    u = _matmul(h2, w_up, name="ffn_up")
    u3 = u.reshape(b_, t_, 2 * f)
    yff = _glu_fwd(u3, conv_w, cb).reshape(n, f)
    dy, (loss_vec,), _ = _matmul_rows([yff], w_down, name="ffn_down_loss", tb=False, row_ins=[x1, target.reshape(n, d)],
                                      vec_ins=[], epilogue=_loss_epilogue, n_vec_out=1)

    grads = {}

    def ridden(name, call):
        if not hooks or name not in hooks:
            return call(None)[0]
        rider, then = hooks[name](grads)
        outs, extra = call(rider)
        then(extra)
        return outs

    dyff = _matmul(dy, w_down, tb=True, name="ffn_down_dx")
    grads["ffn_w_down"] = _matmul(yff, dy, ta=True, name="ffn_down_dw", tm=1408)
    du_a, du_v, grads["ffn_conv_w"], grads["ffn_conv_b"] = _glu_bwd(u3, dyff.reshape(b_, t_, f), conv_w, cb)
    du_a, du_v = du_a.reshape(n, f), du_v.reshape(n, f)
    dx1, (grads["norm_ffn_g"],), _ = _matmul_rows(
        [du_a, du_v], w_up, name="ffn_up_dx", tb=True, row_ins=[x1, dy], vec_ins=[w["norm_ffn_g"]],
        epilogue=_norm_bwd_epilogue(0), n_vec_out=1)
    grads["ffn_w_up"] = _matmul(h2, None, ta=True, name="ffn_up_dw", b_parts=[du_a, du_v], tn=f // 2, stack_out=True)

    dz, dpa, dpb, dpc, dya, dyb, dyc = _merge_bwd(dx1, ys, zm, w_brs, w_out, gate_col)
    dz = dz.reshape(b_, t_, -1)
    grads["w_out"] = _matmul(merged, dx1, ta=True, name="out_proj_dw")
    for nm, y_, dp_ in zip(("w_br_hgrn", "w_br_fox", "w_br_mem"), ys, (dpa, dpb, dpc)):
        grads[nm] = _matmul(y_, dp_, ta=True, name=nm + "_dw")

    dz, dmk, dmv, grads["mem_q_norm_g"], grads["mem_k_norm_g"] = _mem_bwd(
        zm3, mkv, dyc.reshape(b_, t_, mw), w["mem_q_norm_g"], w["mem_k_norm_g"], mw, mem_col, dz)
    dmkv = jnp.concatenate([dmk, dmv], axis=-1).reshape(b_ * m_, 2 * mw)
    grads["mem_kv_w"] = _matmul(hm, dmkv, ta=True, name="mem_kv_dw")
    dhm = _matmul(dmkv, w_kv, tb=True, name="mem_kv_dx")
    _, grads["norm_mem_g"] = _rmsnorm_bwd(mem2, [dhm], w["norm_mem_g"], None, name="norm_mem_bwd")

    dz, dfc, g_fq, g_fk = ridden("fox_bwd", lambda rider: _fox_bwd(
        zm3, yb, dyb.reshape(b_, t_, fw), lse, fc, fox_gq, fox_gk, fw, fox_col, dz, rider))
    grads["fox_q_norm_g"] = g_fq[:, :FOX_DH] + g_fq[:, FOX_DH:]
    grads["fox_k_norm_g"] = g_fk[:, :FOX_DH] + g_fk[:, FOX_DH:]
    dzf, g_fb = _fox_post(dfc, zf3, f_bias)
    grads["fox_f_bias"] = g_fb[:, :FOX_HEADS]

    dz, grads["hgrn_lb_logits"], grads["hgrn_norm_g"] = ridden("hgrn_bwd", lambda rider: _hgrn_bwd(
        zm3, dya.reshape(b_, t_, hw), w["hgrn_lb_logits"], w["hgrn_norm_g"], hw, hg_col, dz, rider))
    dzm = dz.reshape(n, -1)
    dzf2 = dzf.reshape(n, LANE)
    g_main = _matmul(h, dzm, ta=True, name="in_proj_dw")
    g_ff = _matmul(h, dzf2, ta=True, name="in_proj_forget_dw")
    grads["w_in"] = _join_w_in(_reorder_blocks(g_main, inverse), g_ff[:, :FOX_HEADS], s1, FOX_HEADS)

    dh_b = _matmul(dzf2, w_ff, tb=True, name="in_proj_forget_dx")

    def in_proj_dx(rider):
        dx, vecs, extra = _matmul_rows([dzm], w_main, name="in_proj_dx", tb=True, row_ins=[x2, dx1, dh_b],
                                       vec_ins=[w["norm_mix_g"]], epilogue=_norm_bwd_epilogue(1), n_vec_out=1,
                                       rider=rider)
        return [dx, vecs[0]], extra

    grad_x, grads["norm_mix_g"] = ridden("in_proj_dx", in_proj_dx)
    return loss_vec, grad_x.reshape(b_, t_, d), grads


def kernel(x, mem, norm_mix_g, norm_mem_g, w_in, hgrn_lb_logits, hgrn_norm_g, fox_f_bias, fox_q_norm_g, fox_k_norm_g, mem_kv_w, mem_q_norm_g, mem_k_norm_g, w_br_hgrn, w_br_fox, w_br_mem, w_out, norm_ffn_g, ffn_w_up, ffn_conv_w, ffn_conv_b, ffn_w_down, loss_target, m_norm_mix_g, m_norm_mem_g, m_w_in, m_hgrn_lb_logits, m_hgrn_norm_g, m_fox_f_bias, m_fox_q_norm_g, m_fox_k_norm_g, m_mem_kv_w, m_mem_q_norm_g, m_mem_k_norm_g, m_w_br_hgrn, m_w_br_fox, m_w_br_mem, m_w_out, m_norm_ffn_g, m_ffn_w_up, m_ffn_conv_w, m_ffn_conv_b, m_ffn_w_down, v_norm_mix_g, v_norm_mem_g, v_w_in, v_hgrn_lb_logits, v_hgrn_norm_g, v_fox_f_bias, v_fox_q_norm_g, v_fox_k_norm_g, v_mem_kv_w, v_mem_q_norm_g, v_mem_k_norm_g, v_w_br_hgrn, v_w_br_fox, v_w_br_mem, v_w_out, v_norm_ffn_g, v_ffn_w_up, v_ffn_conv_w, v_ffn_conv_b, v_ffn_w_down):
    w = dict(zip(WEIGHTS, (norm_mix_g, norm_mem_g, w_in, hgrn_lb_logits, hgrn_norm_g, fox_f_bias, fox_q_norm_g,
                           fox_k_norm_g, mem_kv_w, mem_q_norm_g, mem_k_norm_g, w_br_hgrn, w_br_fox, w_br_mem, w_out,
                           norm_ffn_g, ffn_w_up, ffn_conv_w, ffn_conv_b, ffn_w_down)))
    m = dict(zip(WEIGHTS, (m_norm_mix_g, m_norm_mem_g, m_w_in, m_hgrn_lb_logits, m_hgrn_norm_g, m_fox_f_bias,
                           m_fox_q_norm_g, m_fox_k_norm_g, m_mem_kv_w, m_mem_q_norm_g, m_mem_k_norm_g, m_w_br_hgrn,
                           m_w_br_fox, m_w_br_mem, m_w_out, m_norm_ffn_g, m_ffn_w_up, m_ffn_conv_w, m_ffn_conv_b,
                           m_ffn_w_down)))
    v = dict(zip(WEIGHTS, (v_norm_mix_g, v_norm_mem_g, v_w_in, v_hgrn_lb_logits, v_hgrn_norm_g, v_fox_f_bias,
                           v_fox_q_norm_g, v_fox_k_norm_g, v_mem_kv_w, v_mem_q_norm_g, v_mem_k_norm_g, v_w_br_hgrn,
                           v_w_br_fox, v_w_br_mem, v_w_out, v_norm_ffn_g, v_ffn_w_up, v_ffn_conv_w, v_ffn_conv_b,
                           v_ffn_w_down)))
    c_idx = lax.axis_index("c")
    chip = 2 * lax.axis_index("x") + lax.axis_index("y")

    mine = {nm: w[nm][0].astype(MXU_DTYPE) for nm in BIG}

    def gathered_full(names, arrays):
        out = {nm: _put_shard(g, KIND[nm], chip, mine[nm]) for nm, g in zip(names, arrays)}
        return {nm: g.reshape(-1, g.shape[2]) if nm in ROW_SHARDED else g for nm, g in out.items()}

    full = gathered_full(FIRST, _gather_shards([mine[nm] for nm in FIRST], [KIND[nm] for nm in FIRST]))
    late = ([mine[nm] for nm in REST], [KIND[nm] for nm in REST], lambda arrays: gathered_full(REST, arrays))
    cs = ffn_conv_w.shape[2]
    f = cs * N_CHIPS
    placed = lax.dynamic_update_slice(jnp.zeros((3, f), F32), ffn_conv_w[0] * (c_idx == 0).astype(F32), (0, chip * cs))
    conv_w = _unpack_small(_all_sum_small(_pack_small([placed]), "gather_conv_w"), [(3, f)])[0]

    c_arr = jnp.reshape(c_idx, (1,)).astype(jnp.int32)

    def stacked(nm, g):
        return g.reshape(N_CHIPS, -1, g.shape[1]) if nm in ROW_SHARDED else g

    def with_own(landed, partial, kinds):
        return [_put_shard(bq, STACK, chip, _take_shard(p, k, chip)) for bq, p, k in zip(landed, partial, kinds)]

    kinds_rest, kinds_last = [KIND[nm] for nm in REST], [KIND[nm] for nm in LAST]
    state = {}

    def swap_rest(grads):
        gs = [stacked(nm, grads[nm]) for nm in REST]

        def then(from_sibling):
            state["partial_rest"] = [_add_half(g, a, k, c_arr, "add_half_" + nm)
                                     for g, a, k, nm in zip(gs, from_sibling, kinds_rest, REST)]

        return _swap_rider(gs, kinds_rest), then

    def exchange_rest(grads):
        def then(landed):
            state["landed_rest"] = with_own(landed, state["partial_rest"], kinds_rest)

        return _exchange_rider(state["partial_rest"], kinds_rest), then

    def exchange_last(grads):
        gs = [stacked(nm, grads[nm]) for nm in LAST]
        from_sibling = _pair_swap_halves(gs, kinds_last, "pair_swap_halves_last")
        partial = [_add_half(g, a, k, c_arr, "add_half_" + nm) for g, a, k, nm in zip(gs, from_sibling, kinds_last, LAST)]

        def then(landed):
            state["landed_last"] = with_own(landed, partial, kinds_last)

        return _exchange_rider(partial, kinds_last), then

    hooks = {"fox_bwd": swap_rest, "hgrn_bwd": exchange_rest, "in_proj_dx": exchange_last}

    loss_vec, grad_x, grads = _local_step(x, mem, loss_target, w, full, conv_w, late, hooks)

    landed = dict(zip(LAST + REST, state["landed_last"] + state["landed_rest"]))
    reduced_half = [_sum_chips(landed[nm], "sum_chips_" + nm) for nm in BIG]
    joined = [lax.dynamic_update_slice(o, q, (c_idx * q.shape[0], 0))
              for o, q in zip(_pair_join_halves(reduced_half), reduced_half)]
    gshards = dict(zip(BIG, joined))

    small_shapes = [w[nm].shape for nm in SMALL] + [grads["ffn_conv_w"].shape, loss_vec.shape]
    summed = _all_sum_small(_pack_small([grads[nm] for nm in SMALL] + [grads["ffn_conv_w"], loss_vec]),
                            "all_sum_small_grads")
    g_small, d_small, m_small, v_small, (g_conv_w, loss_row) = _adamw_small(
        summed, small_shapes, [w[nm] for nm in SMALL], [m[nm] for nm in SMALL], [v[nm] for nm in SMALL])
    loss = jnp.sum(loss_row)
    g_out = {nm: gshards[nm][None] for nm in BIG}
    g_out["ffn_conv_w"] = lax.dynamic_slice(g_conv_w, (0, chip * cs), (3, cs))[None]
    delta, new_m, new_v = dict(zip(SMALL, d_small)), dict(zip(SMALL, m_small)), dict(zip(SMALL, v_small))
    g_out.update(zip(SMALL, g_small))
    for nm in BIG + ("ffn_conv_w",):
        operands = (w[nm], g_out[nm], m[nm], v[nm])
        if nm in TRANSPOSED:
            operands = [jnp.swapaxes(a, 1, 2) for a in operands]
        outs = _adamw(*operands, name="adamw_" + nm)
        delta[nm], new_m[nm], new_v[nm] = [jnp.swapaxes(o, 1, 2) for o in outs] if nm in TRANSPOSED else outs

    return (loss, grad_x, *[g_out[nm] for nm in WEIGHTS], *[delta[nm] for nm in WEIGHTS],
            *[new_m[nm] for nm in WEIGHTS], *[new_v[nm] for nm in WEIGHTS])
```

```python
import functools
import math

import jax
import jax.numpy as jnp
from jax import lax
from jax.experimental import pallas as pl
from jax.experimental.pallas import tpu as pltpu

F32 = jnp.float32
BF16 = jnp.bfloat16
MXU_DTYPE = jnp.bfloat16
EXCHANGE_DTYPE = jnp.bfloat16

EPS = 1e-6
HG_HEADS, HG_D = 4, 128
FOX_HEADS, FOX_DH = 8, 64
MEM_HEADS, MEM_DH = 4, 128
HG_CHUNK = 64
FOX_BLOCK = 256
LANE = 128
FFN_GROUP = 256
FLAT_W = 1024
VMEM_LIMIT = 56 * 2 ** 20
NEG = -1e30
N_CHIPS = 4

ADAM_LR, ADAM_B1, ADAM_B2, ADAM_EPS, ADAM_WD, ADAM_STEP = 0.001, 0.9, 0.999, 1e-08, 0.01, 10

MESH = pl.DeviceIdType.MESH
ANY = pl.BlockSpec(memory_space=pl.ANY)


def _mx(x):
    return x.astype(MXU_DTYPE)


def _dot(a, b, ca, cb):
    return lax.dot_general(_mx(a), _mx(b), (((ca,), (cb,)), ((), ())), preferred_element_type=F32)


def _nn(a, b):
    return _dot(a, b, 1, 0)


def _nt(a, b):
    return _dot(a, b, 1, 1)


def _tn(a, b):
    return _dot(a, b, 0, 0)


def _dotp(a, b, ca, cb):
    return lax.dot_general(a, b, (((ca,), (cb,)), ((), ())), precision=lax.Precision.HIGHEST,
                           preferred_element_type=F32)


def _tri_dot(tri_bf, x):
    hi = x.astype(BF16)
    r = x - hi.astype(F32)
    mid = r.astype(BF16)
    lo = (r - mid.astype(F32)).astype(BF16)

    def d(v):
        return lax.dot_general(tri_bf, v, (((1,), (0,)), ((), ())), preferred_element_type=F32)

    return d(hi) + d(mid) + d(lo)


def _sig(x):
    return jax.nn.sigmoid(x)


def _erf(x):
    a = jnp.abs(x)
    t = 1.0 / (1.0 + 0.3275911 * a)
    poly = t * (0.254829592 + t * (-0.284496736 + t * (1.421413741 + t * (-1.453152027 + t * 1.061405429))))
    y = 1.0 - poly * jnp.exp(-a * a)
    return jnp.where(x < 0, -y, y)


def _tile(dim, pref, unit=LANE):
    if dim <= pref:
        return dim
    t = pref - pref % unit
    while t >= unit:
        if dim % t == 0:
            return t
        t -= unit
    return dim


def _params(n_grid):
    return pltpu.CompilerParams(dimension_semantics=("arbitrary",) * n_grid, vmem_limit_bytes=VMEM_LIMIT)


def _acc(ref, val, first):
    @pl.when(first)
    def _():
        ref[...] = val

    @pl.when(jnp.logical_not(first))
    def _():
        ref[...] += val


class _Rider:
    def __init__(self, inputs, out_shapes, scratch, start, finish):
        self.inputs, self.out_shapes, self.scratch, self.start, self.finish = inputs, out_shapes, scratch, start, finish


def _ride(body, rider, n_in, n_out, grid):
    if rider is None:
        return body
    ri, ro, rs = len(rider.inputs), len(rider.out_shapes), len(rider.scratch)

    def wrapped(*refs):
        a, b, c = n_in + ri, n_in + ri + n_out, n_in + ri + n_out + ro
        base = refs[:n_in] + refs[a:b] + refs[c:len(refs) - rs]
        r_in, r_out, r_scr = refs[n_in:a], refs[b:c], refs[len(refs) - rs:]
        step = pl.program_id(0)
        for ax in range(1, len(grid)):
            step = step * grid[ax] + pl.program_id(ax)

        @pl.when(step == 0)
        def _():
            rider.start(r_in, r_out, r_scr)

        body(*base)

        @pl.when(step == math.prod(grid) - 1)
        def _():
            rider.finish(r_in, r_out, r_scr)

    return wrapped


def _ride_call(body, rider, *, name, grid, in_specs, out_specs, out_shape, scratch, args, aliases=None):
    n_in, n_out = len(in_specs), len(out_specs)
    aliases = aliases or {}
    if rider is None:
        outs = pl.pallas_call(body, name=name, grid=grid, in_specs=in_specs, out_specs=out_specs, out_shape=out_shape,
                              scratch_shapes=scratch, input_output_aliases=aliases,
                              compiler_params=_params(len(grid)))(*args)
        return list(outs), None
    outs = pl.pallas_call(
        _ride(body, rider, n_in, n_out, grid), name=name, grid=grid,
        in_specs=list(in_specs) + [ANY] * len(rider.inputs), out_specs=list(out_specs) + [ANY] * len(rider.out_shapes),
        out_shape=list(out_shape) + list(rider.out_shapes), scratch_shapes=list(scratch) + list(rider.scratch),
        input_output_aliases=aliases, compiler_params=_params(len(grid)),
    )(*args, *rider.inputs)
    return list(outs[:n_out]), list(outs[n_out:])


def _matmul(a, b, *, name, ta=False, tb=False, tm=1024, tn=2048, tk=None, rider=None, b_parts=None, stack_out=False):
    m, k = (a.shape[1], a.shape[0]) if ta else a.shape
    tk = tk or (1024 if ta else 2048)
    stacked_b = b is not None and b.ndim == 3
    if b_parts:
        n, tn = 2 * b_parts[0].shape[1], _tile(b_parts[0].shape[1], tn)
    elif stacked_b:
        n, tn = b.shape[0] * b.shape[2], b.shape[2]
    else:
        n = b.shape[0] if tb else b.shape[1]
        tn = _tile(n, tn)
    tm, tk = _tile(m, tm), _tile(k, tk)
    nk, nj = k // tk, n // tn

    def body(a_ref, *refs):
        o_ref = refs[-1]
        if b_parts:
            bv = jnp.where(pl.program_id(1) < nj // 2, refs[0][...], refs[1][...])
        else:
            bv = refs[0][...]
        p = _dot(a_ref[...], bv, 0 if ta else 1, 1 if tb else 0)
        if nk == 1:
            o_ref[...] = p
        else:
            _acc(o_ref, p, pl.program_id(2) == 0)

    a_spec = pl.BlockSpec((tk, tm), lambda i, j, kk: (kk, i)) if ta else pl.BlockSpec((tm, tk), lambda i, j, kk: (i, kk))
    if b_parts:
        half = nj // 2
        b_specs = [pl.BlockSpec((tk, tn), lambda i, j, kk: (kk, jnp.minimum(j, half - 1))),
                   pl.BlockSpec((tk, tn), lambda i, j, kk: (kk, jnp.maximum(j - half, 0)))]
        b_args = list(b_parts)
    elif stacked_b:
        b_specs, b_args = [pl.BlockSpec((None, tk, tn), lambda i, j, kk: (j, kk, 0))], [b]
    else:
        b_specs = [pl.BlockSpec((tn, tk), lambda i, j, kk: (j, kk)) if tb else pl.BlockSpec((tk, tn), lambda i, j, kk: (kk, j))]
        b_args = [b]
    if stack_out:
        o_spec, o_sds = pl.BlockSpec((None, tm, tn), lambda i, j, kk: (j, i, 0)), jax.ShapeDtypeStruct((nj, m, tn), F32)
    else:
        o_spec, o_sds = pl.BlockSpec((tm, tn), lambda i, j, kk: (i, j)), jax.ShapeDtypeStruct((m, n), F32)
    outs, extra = _ride_call(body, rider, name=name, grid=(m // tm, nj, nk), in_specs=[a_spec] + b_specs,
                             out_specs=[o_spec], out_shape=[o_sds], scratch=[], args=(a, *b_args))
    return (outs[0], extra) if rider else outs[0]


def _matmul_rows(a_parts, b, *, name, tb, row_ins, vec_ins, epilogue, n_vec_out, tm=512, tk=2048, rider=None):
    m, kp = a_parts[0].shape
    stacked_b = b.ndim == 3
    n = b.shape[1] if stacked_b else (b.shape[0] if tb else b.shape[1])
    tm, tk = _tile(m, tm, 8), (b.shape[2] if stacked_b else _tile(kp, tk))
    nk = kp // tk
    n_a, n_row, n_vec = len(a_parts), len(row_ins), len(vec_ins)

    def body(*refs):
        a_refs, b_refs = refs[:n_a], refs[n_a:2 * n_a]
        rows = refs[2 * n_a:2 * n_a + n_row]
        vecs = refs[2 * n_a + n_row:2 * n_a + n_row + n_vec]
        o_ref = refs[2 * n_a + n_row + n_vec]
        v_refs = refs[2 * n_a + n_row + n_vec + 1:-1]
        acc_ref = refs[-1]
        i, kk = pl.program_id(0), pl.program_id(1)
        p = _dot(a_refs[0][...], b_refs[0][...], 1, 1 if tb else 0)
        for a_ref, b_ref in zip(a_refs[1:], b_refs[1:]):
            p = p + _dot(a_ref[...], b_ref[...], 1, 1 if tb else 0)
        _acc(acc_ref, p, kk == 0)

        @pl.when(kk == nk - 1)
        def _():
            out, vouts = epilogue(acc_ref[...], *[r[...] for r in rows], *[v[...] for v in vecs])
            o_ref[...] = out
            for v_ref, v in zip(v_refs, vouts):
                _acc(v_ref, v, i == 0)

    a_spec = pl.BlockSpec((tm, tk), lambda i, kk: (i, kk))
    if stacked_b:
        b_specs = [pl.BlockSpec((None, n, tk), functools.partial(lambda i, kk, q: (q * nk + kk, 0, 0), q=q))
                   for q in range(n_a)]
    else:
        b_specs = [pl.BlockSpec((n, tk), functools.partial(lambda i, kk, q: (0, q * nk + kk), q=q)) if tb else
                   pl.BlockSpec((tk, n), functools.partial(lambda i, kk, q: (q * nk + kk, 0), q=q)) for q in range(n_a)]
    row = pl.BlockSpec((tm, n), lambda i, kk: (i, 0))
    vec = pl.BlockSpec((1, n), lambda i, kk: (0, 0))
    outs, extra = _ride_call(
        body, rider, name=name, grid=(m // tm, nk),
        in_specs=[a_spec] * n_a + b_specs + [row] * n_row + [vec] * n_vec,
        out_specs=[row] + [vec] * n_vec_out,
        out_shape=[jax.ShapeDtypeStruct((m, n), F32)] + [jax.ShapeDtypeStruct((1, n), F32)] * n_vec_out,
        scratch=[pltpu.VMEM((tm, n), F32)], args=(*a_parts, *([b] * n_a), *row_ins, *vec_ins))
    return outs[0], outs[1:], extra


def _norm_bwd_epilogue(n_dh):
    def epilogue(dh, x, res, *rest):
        for extra in rest[:n_dh]:
            dh = dh + extra
        g = rest[n_dh]
        r = lax.rsqrt(jnp.mean(x * x, axis=-1, keepdims=True) + EPS)
        dhg = dh * g
        dx = res + r * dhg - x * (r * r * r) * jnp.mean(dhg * x, axis=-1, keepdims=True)
        return dx, [jnp.sum(dh * x * r, axis=0, keepdims=True)]

    return epilogue


def _loss_epilogue(y, x1, target):
    d = y.shape[1]
    err = x1 + y - target
    return err * (1.0 / d), [jnp.sum(err * err, axis=0, keepdims=True) * (0.5 / d)]


def _rmsnorm_fwd(x, g, *, name, tm=512):
    n, d = x.shape
    tm = _tile(n, tm, 8)

    def body(x_ref, g_ref, o_ref):
        xv = x_ref[...]
        r = lax.rsqrt(jnp.mean(xv * xv, axis=-1, keepdims=True) + EPS)
        o_ref[...] = (xv * r * g_ref[...]).astype(o_ref.dtype)

    return pl.pallas_call(
        body, name=name, grid=(n // tm,),
        in_specs=[pl.BlockSpec((tm, d), lambda i: (i, 0)), pl.BlockSpec((1, d), lambda i: (0, 0))],
        out_specs=pl.BlockSpec((tm, d), lambda i: (i, 0)),
        out_shape=jax.ShapeDtypeStruct((n, d), MXU_DTYPE),
        compiler_params=_params(1),
    )(x, g)


def _rmsnorm_bwd(x, dhs, g, res, *, name, tm=512):
    n, d = x.shape
    tm = _tile(n, tm, 8)
    n_dh = len(dhs)
    has_res = res is not None

    def body(*refs):
        x_ref, dh_refs, g_ref = refs[0], refs[1:1 + n_dh], refs[1 + n_dh]
        res_ref = refs[2 + n_dh] if has_res else None
        dx_ref, dg_ref = refs[-2], refs[-1]
        xv = x_ref[...]
        dh = dh_refs[0][...].astype(F32)
        for r_ in dh_refs[1:]:
            dh = dh + r_[...].astype(F32)
        r = lax.rsqrt(jnp.mean(xv * xv, axis=-1, keepdims=True) + EPS)
        dhg = dh * g_ref[...]
        dx = r * dhg - xv * (r * r * r) * jnp.mean(dhg * xv, axis=-1, keepdims=True)
        if has_res:
            dx = dx + res_ref[...]
        dx_ref[...] = dx
        _acc(dg_ref, jnp.sum(dh * xv * r, axis=0, keepdims=True), pl.program_id(0) == 0)

    row = pl.BlockSpec((tm, d), lambda i: (i, 0))
    vec = pl.BlockSpec((1, d), lambda i: (0, 0))
    ins = [x] + list(dhs) + [g] + ([res] if has_res else [])
    return pl.pallas_call(
        body, name=name, grid=(n // tm,),
        in_specs=[row] * (1 + n_dh) + [vec] + ([row] if has_res else []),
        out_specs=[row, vec],
        out_shape=[jax.ShapeDtypeStruct((n, d), F32), jax.ShapeDtypeStruct((1, d), F32)],
        compiler_params=_params(1),
    )(*ins)


def _adamw(w, g, m, v, *, name, tr=256):
    _, r, c = w.shape
    c1 = 1.0 / (1.0 - ADAM_B1 ** ADAM_STEP)
    c2 = 1.0 / (1.0 - ADAM_B2 ** ADAM_STEP)

    def body(w_ref, g_ref, m_ref, v_ref, d_ref, mo_ref, vo_ref):
        gv = g_ref[...]
        mn = ADAM_B1 * m_ref[...] + (1.0 - ADAM_B1) * gv
        vn = ADAM_B2 * v_ref[...] + (1.0 - ADAM_B2) * (gv * gv)
        d_ref[...] = -ADAM_LR * ((mn * c1) / (jnp.sqrt(vn * c2) + ADAM_EPS) + ADAM_WD * w_ref[...])
        mo_ref[...] = mn
        vo_ref[...] = vn

    if r % 8 == 0 or r < 8:
        tr = _tile(r, tr, 8)
        grid, blk = (r // tr,), pl.BlockSpec((1, tr, c), lambda i: (0, i, 0))
    else:
        tc = _tile(c, tr)
        grid, blk = (c // tc,), pl.BlockSpec((1, r, tc), lambda i: (0, 0, i))
    sds = jax.ShapeDtypeStruct((1, r, c), F32)
    return pl.pallas_call(
        body, name=name, grid=grid, in_specs=[blk] * 4, out_specs=[blk] * 3, out_shape=[sds] * 3,
        compiler_params=_params(1),
    )(w, g, m, v)


def _bdot(a, b, ca, cb):
    return lax.dot_general(_mx(a), _mx(b), (((ca,), (cb,)), ((0,), (0,))), preferred_element_type=F32)


def _bdotp(a, b, ca, cb):
    return lax.dot_general(a, b, (((ca,), (cb,)), ((0,), (0,))), precision=lax.Precision.HIGHEST,
                           preferred_element_type=F32)


def _tri_dot_b(tri_bf, x):
    hi = x.astype(BF16)
    r = x - hi.astype(F32)
    mid = r.astype(BF16)
    lo = (r - mid.astype(F32)).astype(BF16)

    def d(v):
        return lax.dot_general(tri_bf, v, (((2,), (1,)), ((0,), (0,))), preferred_element_type=F32)

    return d(hi) + d(mid) + d(lo)


def _hgrn_forward(hq, hf, hi, lbv, tril, tril_bf):
    nc, c, _ = hq.shape
    sf = _sig(hf)
    f = lbv + (1.0 - lbv) * sf
    k = 1.0 - f
    gcum = _tri_dot_b(tril_bf, jnp.log(f))
    mid = gcum[:, c // 2 - 1:c // 2, :]
    glast = gcum[:, c - 1:c, :]
    sq = _sig(hq)
    q = hq * sq
    e_q = jnp.exp(gcum - mid)
    e_k = jnp.exp(mid - gcum)
    qe, ke = q * e_q, k * e_k
    a = jnp.where(tril, _bdot(qe, ke, 2, 2), 0.0)
    e_g = jnp.exp(gcum)
    qg = q * e_g
    e_s = jnp.exp(glast - gcum)
    kg = k * e_s
    e_l = jnp.exp(glast)
    upd = _bdot(hi, kg, 1, 1)
    st = jnp.zeros((HG_D, HG_D), F32)
    states = []
    for n in range(nc):
        states.append(st)
        st = st * e_l[n] + upd[n]
    st_all = jnp.stack(states)
    o = _bdot(a, hi, 2, 1) + _bdot(qg, st_all, 2, 2)
    return dict(sf=sf, f=f, k=k, sq=sq, q=q, e_q=e_q, e_k=e_k, qe=qe, ke=ke, a=a, e_g=e_g, qg=qg, o=o,
                e_s=e_s, kg=kg, e_l=e_l, st_all=st_all)


def _hgrn_specs(t_, col0):
    def col(off):
        return pl.BlockSpec((1, t_, LANE), lambda h, b: (b, 0, col0 + 4 * h + off))

    vec = pl.BlockSpec((2, LANE), lambda h, b: (0, h))
    one = pl.BlockSpec((1, LANE), lambda h, b: (0, 0))
    blk = pl.BlockSpec((1, t_, LANE), lambda h, b: (b, 0, h))
    return col, vec, one, blk


def _chunk_masks(nc, c):
    row = lax.broadcasted_iota(jnp.int32, (nc, c, c), 1)
    cl = lax.broadcasted_iota(jnp.int32, (nc, c, c), 2)
    return row >= cl, (row >= cl).astype(BF16), (row <= cl).astype(BF16)


def _hgrn_fwd(zm, lb, gn, hw, col0):
    b_, t_, _ = zm.shape
    c = min(HG_CHUNK, t_)
    nc = t_ // c
    col, vec, one, blk = _hgrn_specs(t_, col0)

    def body(q_ref, f_ref, i_ref, g_ref, lb_ref, gn_ref, y_ref):
        lbv, gnv = _sig(lb_ref[0:1, :] - lb_ref[1:2, :]), gn_ref[...]
        tril, tril_bf, _ = _chunk_masks(nc, c)
        chunks = lambda ref: ref[0].reshape(nc, c, LANE)
        o = _hgrn_forward(chunks(q_ref), chunks(f_ref), chunks(i_ref), lbv, tril, tril_bf)["o"]
        r = lax.rsqrt(jnp.mean(o * o, axis=-1, keepdims=True) + EPS)
        hg = chunks(g_ref)
        y_ref[0] = (o * r * gnv * (hg * _sig(hg))).reshape(t_, LANE)

    return pl.pallas_call(
        body, name="hgrn_fwd", grid=(HG_HEADS, b_),
        in_specs=[col(0), col(1), col(2), col(3), vec, one], out_specs=blk,
        out_shape=jax.ShapeDtypeStruct((b_, t_, hw), F32),
        compiler_params=_params(2),
    )(zm, zm, zm, zm, lb, gn)


def _hgrn_bwd(zm, dy, lb, gn, hw, col0, dz, rider=None):
    b_, t_, _ = zm.shape
    c = min(HG_CHUNK, t_)
    nc = t_ // c
    col, vec, one, blk = _hgrn_specs(t_, col0)

    def body(q_ref, f_ref, i_ref, g_ref, dy_ref, lb_ref, gn_ref, _, dz_ref, dlb_ref, dgn_ref):
        h, b = pl.program_id(0), pl.program_id(1)
        lbv, gnv = _sig(lb_ref[0:1, :] - lb_ref[1:2, :]), gn_ref[...]
        tril, tril_bf, triu_bf = _chunk_masks(nc, c)
        last_row = lax.broadcasted_iota(jnp.int32, (nc, c, LANE), 1) == c - 1
        chunks = lambda ref: ref[0].reshape(nc, c, LANE)
        flat = lambda x: x.reshape(t_, LANE)
        hq, hi, hg = chunks(q_ref), chunks(i_ref), chunks(g_ref)
        p = _hgrn_forward(hq, chunks(f_ref), hi, lbv, tril, tril_bf)
        o, q, k, st_all, e_l = p["o"], p["q"], p["k"], p["st_all"], p["e_l"]
        dyv = chunks(dy_ref)
        sg = _sig(hg)
        r = lax.rsqrt(jnp.mean(o * o, axis=-1, keepdims=True) + EPS)
        dn = dyv * (hg * sg)
        dz_ref[0, :, 3 * LANE:] = flat(dyv * (o * r * gnv) * (sg * (1.0 + hg * (1.0 - sg)))).astype(dz_ref.dtype)
        dgn = jnp.sum(flat(dn * o * r), axis=0, keepdims=True)
        dng = dn * gnv
        do = r * dng - o * (r * r * r) * jnp.mean(dng * o, axis=-1, keepdims=True)
        back = _bdotp(do, p["qg"], 1, 1)
        dst = jnp.zeros((HG_D, HG_D), F32)
        dsts = [None] * nc
        for n in range(nc - 1, -1, -1):
            dsts[n] = dst
            dst = dst * e_l[n] + back[n]
        dst_all = jnp.stack(dsts)
        da = jnp.where(tril, _bdotp(do, hi, 2, 2), 0.0)
        dq = _bdotp(da, p["ke"], 2, 1) * p["e_q"] + _bdotp(do, st_all, 2, 1) * p["e_g"]
        dk_state = _bdotp(hi, dst_all, 2, 1) * p["e_s"]
        dk = _bdotp(da, p["qe"], 1, 1) * p["e_k"] + dk_state
        dz_ref[0, :, 2 * LANE:3 * LANE] = flat(_bdot(p["a"], do, 1, 1) + _bdot(p["kg"], dst_all, 2, 2)).astype(dz_ref.dtype)
        extra = (jnp.sum(k * dk_state, axis=1, keepdims=True) + e_l * jnp.sum(st_all * dst_all, axis=1, keepdims=True))
        dgc = q * dq - k * dk + jnp.where(last_row, extra, 0.0)
        dfv = _tri_dot_b(triu_bf, dgc) / p["f"] - dk
        sf, sq = p["sf"], p["sq"]
        dz_ref[0, :, LANE:2 * LANE] = flat(dfv * (1.0 - lbv) * sf * (1.0 - sf)).astype(dz_ref.dtype)
        dlb = jnp.sum(flat(dfv * (1.0 - sf)), axis=0, keepdims=True)
        dz_ref[0, :, :LANE] = flat(dq * (sq * (1.0 + hq * (1.0 - sq)))).astype(dz_ref.dtype)
        dl0 = dlb * lbv * (1.0 - lbv)
        _acc(dlb_ref, jnp.concatenate([dl0, -dl0], axis=0), b == 0)
        _acc(dgn_ref, dgn, jnp.logical_and(b == 0, h == 0))

    return _ride_call(
        body, rider, name="hgrn_bwd", grid=(HG_HEADS, b_),
        in_specs=[col(0), col(1), col(2), col(3), blk, vec, one, ANY],
        out_specs=[pl.BlockSpec((1, t_, 4 * LANE), lambda h, b: (b, 0, col0 // 4 + h)), vec, one],
        out_shape=[jax.ShapeDtypeStruct(dz.shape, dz.dtype), jax.ShapeDtypeStruct((2, hw), F32),
                   jax.ShapeDtypeStruct((1, LANE), F32)],
        scratch=[], args=(zm, zm, zm, zm, dy, lb, gn, dz), aliases={7: 0})


def _fox_logf(x):
    return jnp.minimum(x, 0.0) - jnp.log(1.0 + jnp.exp(-jnp.abs(x)))


def _fox_prep(zf, bias):
    b_, t_, _ = zf.shape
    tb = min(FOX_BLOCK, t_)
    nb = t_ // tb

    def body(z_ref, b_ref, fc_ref):
        tril_bf = (lax.broadcasted_iota(jnp.int32, (tb, tb), 0) >= lax.broadcasted_iota(jnp.int32, (tb, tb), 1)).astype(BF16)
        bv = b_ref[...]

        def blk(i, carry):
            rows = pl.ds(pl.multiple_of(i * tb, tb), tb)
            fc = _tri_dot(tril_bf, _fox_logf(z_ref[0, rows, :] + bv)) + carry
            fc_ref[0, rows, :] = fc
            return fc[tb - 1:tb, :]

        lax.fori_loop(0, nb, blk, jnp.zeros((1, LANE), F32))

    blk_spec = pl.BlockSpec((1, t_, LANE), lambda b: (b, 0, 0))
    return pl.pallas_call(
        body, name="fox_prep", grid=(b_,),
        in_specs=[blk_spec, pl.BlockSpec((1, LANE), lambda b: (0, 0))], out_specs=blk_spec,
        out_shape=jax.ShapeDtypeStruct((b_, t_, LANE), F32), compiler_params=_params(1),
    )(zf, bias)


def _fox_post(dfc, zf, bias):
    b_, t_, _ = zf.shape
    npair = dfc.shape[1]
    tb = min(FOX_BLOCK, t_)
    nb = t_ // tb

    def body(d_ref, z_ref, b_ref, dz_ref, db_ref):
        triu_bf = (lax.broadcasted_iota(jnp.int32, (tb, tb), 0) <= lax.broadcasted_iota(jnp.int32, (tb, tb), 1)).astype(BF16)
        valid = lax.broadcasted_iota(jnp.int32, (tb, LANE), 1) < FOX_HEADS
        bv = b_ref[...]

        def blk(m, carry):
            tail, db = carry
            rows = pl.ds(pl.multiple_of((nb - 1 - m) * tb, tb), tb)
            dfc_rows = d_ref[0, 0, rows, :]
            for p in range(1, npair):
                dfc_rows = dfc_rows + pltpu.roll(d_ref[0, p, rows, :], 2 * p, 1)
            dlf = _tri_dot(triu_bf, dfc_rows) + tail
            dx = jnp.where(valid, dlf * _sig(-(z_ref[0, rows, :] + bv)), 0.0)
            dz_ref[0, rows, :] = dx.astype(dz_ref.dtype)
            return dlf[0:1, :], db + jnp.sum(dx, axis=0, keepdims=True)

        z1 = jnp.zeros((1, LANE), F32)
        _, db = lax.fori_loop(0, nb, blk, (z1, z1))
        _acc(db_ref, db, pl.program_id(0) == 0)

    blk_spec = pl.BlockSpec((1, t_, LANE), lambda b: (b, 0, 0))
    vec = pl.BlockSpec((1, LANE), lambda b: (0, 0))
    return pl.pallas_call(
        body, name="fox_post", grid=(b_,),
        in_specs=[pl.BlockSpec((1, npair, t_, LANE), lambda b: (b, 0, 0, 0)), blk_spec, vec], out_specs=[blk_spec, vec],
        out_shape=[jax.ShapeDtypeStruct((b_, t_, LANE), MXU_DTYPE), jax.ShapeDtypeStruct((1, LANE), F32)],
        compiler_params=_params(1),
    )(dfc, zf, bias)


FOX_TILE = 256
FOX_BAND = 512
AUG = 64


def _head_mean_matrix():
    r = lax.broadcasted_iota(jnp.int32, (LANE, LANE), 0) // FOX_DH
    c = lax.broadcasted_iota(jnp.int32, (LANE, LANE), 1) // FOX_DH
    return (r == c).astype(BF16)


def _dot_right_exact(x, m_bf):
    hi = x.astype(BF16)
    r = x - hi.astype(F32)
    mid = r.astype(BF16)
    lo = (r - mid.astype(F32)).astype(BF16)

    def d(v):
        return lax.dot_general(v, m_bf, (((1,), (0,)), ((), ())), preferred_element_type=F32)

    return d(hi) + d(mid) + d(lo)


def _pair_norm(x, g2, bd):
    r = lax.rsqrt(_dot_right_exact(x * x, bd) * (1.0 / FOX_DH) + EPS)
    return x * r * g2, r


def _pair_norm_bwd(x, r, dy, g2, bd):
    dyg = dy * g2
    dx = r * dyg - x * (r * r * r) * (_dot_right_exact(dyg * x, bd) * (1.0 / FOX_DH))
    return dx, jnp.sum(dy * x * r, axis=0, keepdims=True)


def _head_lanes(xn, hh):
    return xn if hh == 0 else pltpu.roll(xn, FOX_DH, 1)


def _split3(x):
    hi = x.astype(BF16).astype(F32)
    mid = (x - hi).astype(BF16).astype(F32)
    return hi, mid, x - hi - mid


def _fox_operands(q_ref, k_ref, v_ref, fc_ref, gq2, gk2, p, qa, ka, va):
    t_ = q_ref.shape[1]
    bd = _head_mean_matrix()
    lane = lax.broadcasted_iota(jnp.int32, (t_, LANE), 1)
    qx, kx = q_ref[0], k_ref[0]
    qn, rq = _pair_norm(qx, gq2, bd)
    kn, rk = _pair_norm(kx, gk2, bd)
    vv = v_ref[0]
    q_aug = jnp.where(jnp.logical_and(lane >= AUG, lane < AUG + 3), 1.0, 0.0)
    for hh in range(2):
        fcol = jnp.sum(jnp.where(lane == 2 * p + hh, fc_ref[0], 0.0), axis=-1, keepdims=True)
        hi, mid, lo = _split3(-fcol)
        k_aug = jnp.where(lane == AUG, hi, jnp.where(lane == AUG + 1, mid, jnp.where(lane == AUG + 2, lo,
                          jnp.where(lane == AUG + 3, 1.0, 0.0))))
        head = lane < FOX_DH
        qa[hh] = jnp.where(head, _head_lanes(qn, hh), q_aug).astype(MXU_DTYPE)
        ka[hh] = jnp.where(head, _head_lanes(kn, hh), k_aug).astype(MXU_DTYPE)
        va[hh] = jnp.where(head, _head_lanes(vv, hh), 0.0).astype(MXU_DTYPE)
    return bd, lane, qx, kx, rq, rk


def _fox_specs(t_, fw, col0):
    npair = fw // LANE

    def col(off):
        return pl.BlockSpec((1, t_, LANE), lambda b, p: (b, 0, col0 + 3 * p + off))

    pair = pl.BlockSpec((1, t_, LANE), lambda b, p: (b, 0, p))
    full = pl.BlockSpec((1, t_, LANE), lambda b, p: (b, 0, 0))
    gvec = pl.BlockSpec((1, LANE), lambda b, p: (0, 0))
    lse = pl.BlockSpec((1, 1, t_, LANE), lambda b, p: (b, p, 0, 0))
    return col, pair, full, gvec, lse


def _fox_fwd(zm, fc, gq2, gk2, fw, col0, rider=None):
    b_, t_, _ = zm.shape
    npair = fw // LANE
    tq = min(FOX_TILE, t_)
    bw = min(FOX_BAND, t_)
    nband, tpb = t_ // bw, bw // tq
    scale = FOX_DH ** -0.5
    col, pair, full, gvec, lse_spec = _fox_specs(t_, fw, col0)

    def body(q_ref, k_ref, v_ref, fc_ref, gq_ref, gk_ref, o_ref, lse_ref, qa, ka, va):
        p = pl.program_id(1)
        _fox_operands(q_ref, k_ref, v_ref, fc_ref, gq_ref[...] * scale, gk_ref[...], p, qa, ka, va)
        ahead = lax.broadcasted_iota(jnp.int32, (tq, bw), 1) - lax.broadcasted_iota(jnp.int32, (tq, bw), 0)
        lane = lax.broadcasted_iota(jnp.int32, (tq, LANE), 1)

        for band in range(nband):
            c0 = band * bw

            def qtile(ii, _, c0=c0):
                r0 = pl.multiple_of(c0 + ii * tq, tq)
                rows = pl.ds(r0, tq)
                keep = ahead <= r0 - c0
                res = []
                for hh in range(2):
                    qb = qa[hh, rows, :]
                    s_b = jnp.where(keep, _nt(qb, ka[hh, c0:c0 + bw, :]), NEG)
                    m = jnp.max(s_b, axis=-1, keepdims=True)
                    if c0:
                        s_a = _nt(qb, ka[hh, 0:c0, :])
                        m = jnp.maximum(m, jnp.max(s_a, axis=-1, keepdims=True))
                    p_b = jnp.exp(s_b - m)
                    l = jnp.sum(p_b, axis=-1, keepdims=True)
                    acc = _nn(p_b, va[hh, c0:c0 + bw, :])
                    if c0:
                        p_a = jnp.exp(s_a - m)
                        l = l + jnp.sum(p_a, axis=-1, keepdims=True)
                        acc = acc + _nn(p_a, va[hh, 0:c0, :])
                    res.append((acc / l, m + jnp.log(l)))
                (o0, e0), (o1, e1) = res
                o_ref[0, rows, :] = jnp.where(lane < FOX_DH, o0, pltpu.roll(o1, FOX_DH, 1))
                lse_ref[0, 0, rows, :] = jnp.where(lane == 0, e0, jnp.where(lane == 1, e1, 0.0))
                return 0

            lax.fori_loop(0, tpb, qtile, 0)

    return _ride_call(
        body, rider, name="fox_fwd", grid=(b_, npair),
        in_specs=[col(0), col(1), col(2), full, gvec, gvec],
        out_specs=[pair, lse_spec],
        out_shape=[jax.ShapeDtypeStruct((b_, t_, fw), F32), jax.ShapeDtypeStruct((b_, npair, t_, LANE), F32)],
        scratch=[pltpu.VMEM((2, t_, LANE), MXU_DTYPE)] * 3, args=(zm, zm, zm, fc, gq2, gk2))


def _norm_bwd(x, dy, g):
    r = lax.rsqrt(jnp.mean(x * x, axis=-1, keepdims=True) + EPS)
    dyg = dy * g
    dx = r * dyg - x * (r * r * r) * jnp.mean(dyg * x, axis=-1, keepdims=True)
    return dx, jnp.sum(dy * x * r, axis=0, keepdims=True)


def _fox_bwd(zm, o, do, lse, fc, gq2, gk2, fw, col0, dz, rider=None):
    b_, t_, _ = zm.shape
    npair = fw // LANE
    tq = min(FOX_TILE, t_)
    nb = t_ // tq
    bw = min(FOX_BAND, t_)
    nband, tpb = t_ // bw, bw // tq
    scale = FOX_DH ** -0.5
    col, pair, full, gvec, lse_spec = _fox_specs(t_, fw, col0)

    def body(q_ref, k_ref, v_ref, o_ref, do_ref, lse_ref, fc_ref, gq_ref, gk_ref, _,
             dz_ref, dfc_ref, dgq_ref, dgk_ref, qa, ka, va, da, rowv, dq_acc, dk_acc, dv_acc):
        b, p = pl.program_id(0), pl.program_id(1)
        gq2v, gk2v = gq_ref[...] * scale, gk_ref[...]
        bd, lane, qx, kx, rq, rk = _fox_operands(q_ref, k_ref, v_ref, fc_ref, gq2v, gk2v, p, qa, ka, va)
        head = lane < FOX_DH
        dov = do_ref[0]
        dsum = _dot_right_exact(dov * o_ref[0], bd)
        eye = (lax.broadcasted_iota(jnp.int32, (tq, tq), 0) == lax.broadcasted_iota(jnp.int32, (tq, tq), 1)).astype(F32)
        for hh in range(2):
            da[hh] = jnp.where(head, _head_lanes(dov, hh), 0.0).astype(MXU_DTYPE)
            for blk in range(nb):
                rs = slice(blk * tq, (blk + 1) * tq)
                rowv[2 * hh:2 * hh + 1, rs] = jnp.sum(eye * lse_ref[0, 0, rs, hh:hh + 1], axis=0, keepdims=True)
                rowv[2 * hh + 1:2 * hh + 2, rs] = jnp.sum(eye * dsum[rs, hh * FOX_DH:hh * FOX_DH + 1], axis=0, keepdims=True)
        dq_acc[...] = jnp.zeros(dq_acc.shape, F32)
        ahead = lax.broadcasted_iota(jnp.int32, (tq, bw), 1) - lax.broadcasted_iota(jnp.int32, (tq, bw), 0)

        def part(hh, kb, vb, lo, hi, keep):
            qm, dm = qa[hh, lo:hi, :], da[hh, lo:hi, :]
            pt = jnp.exp(_nt(kb, qm) - rowv[2 * hh:2 * hh + 1, lo:hi])
            if keep is not None:
                pt = jnp.where(keep, pt, 0.0)
            dst = pt * (_nt(vb, dm) - rowv[2 * hh + 1:2 * hh + 2, lo:hi])
            dq_acc[hh, lo:hi, :] += _tn(dst, kb)
            return _nn(dst, qm), _nn(pt, dm)

        for band in range(nband):
            c0 = band * bw

            def kvtile(jj, _, c0=c0):
                r0 = pl.multiple_of(c0 + jj * tq, tq)
                rows = pl.ds(r0, tq)
                keep = ahead >= r0 - c0
                for hh in range(2):
                    kb, vb = ka[hh, rows, :], va[hh, rows, :]
                    dk_t, dv_t = part(hh, kb, vb, c0, c0 + bw, keep)
                    if c0 + bw < t_:
                        dk_u, dv_u = part(hh, kb, vb, c0 + bw, t_, None)
                        dk_t, dv_t = dk_t + dk_u, dv_t + dv_u
                    dk_acc[hh, rows, :] = dk_t
                    dv_acc[hh, rows, :] = dv_t
                return 0

            lax.fori_loop(0, tpb, kvtile, 0)

        dq0, dq1, dk0, dk1 = dq_acc[0], dq_acc[1], dk_acc[0], dk_acc[1]
        dqn = jnp.where(head, dq0, pltpu.roll(dq1, FOX_DH, 1))
        dkn = jnp.where(head, dk0, pltpu.roll(dk1, FOX_DH, 1))
        dqx, gq_part = _pair_norm_bwd(qx, rq, dqn, gq2v, bd)
        dkx, gk_part = _pair_norm_bwd(kx, rk, dkn, gk2v, bd)
        dz_ref[0, :, :LANE] = dqx.astype(dz_ref.dtype)
        dz_ref[0, :, LANE:2 * LANE] = dkx.astype(dz_ref.dtype)
        dz_ref[0, :, 2 * LANE:] = jnp.where(head, dv_acc[0], pltpu.roll(dv_acc[1], FOX_DH, 1)).astype(dz_ref.dtype)

        def bias_grad(dqh, dkh):
            return (jnp.sum(jnp.where(lane == AUG + 3, dqh, 0.0), axis=-1, keepdims=True)
                    - jnp.sum(jnp.where(lane == AUG, dkh, 0.0), axis=-1, keepdims=True))

        dfc_ref[0, 0] = jnp.where(lane == 0, bias_grad(dq0, dk0), jnp.where(lane == 1, bias_grad(dq1, dk1), 0.0))
        first = jnp.logical_and(b == 0, p == 0)
        _acc(dgq_ref, gq_part * scale, first)
        _acc(dgk_ref, gk_part, first)

    gs = jax.ShapeDtypeStruct((1, LANE), F32)
    return _ride_call(
        body, rider, name="fox_bwd", grid=(b_, npair),
        in_specs=[col(0), col(1), col(2), pair, pair, lse_spec, full, gvec, gvec, ANY],
        out_specs=[pl.BlockSpec((1, t_, 3 * LANE), lambda b, p: (b, 0, col0 // 3 + p)), lse_spec, gvec, gvec],
        out_shape=[jax.ShapeDtypeStruct(dz.shape, dz.dtype), jax.ShapeDtypeStruct((b_, npair, t_, LANE), F32), gs, gs],
        scratch=[pltpu.VMEM((2, t_, LANE), MXU_DTYPE)] * 4
        + [pltpu.VMEM((8, t_), F32)] + [pltpu.VMEM((2, t_, LANE), F32)] * 3,
        args=(zm, zm, zm, o, do, lse, fc, gq2, gk2, dz), aliases={9: 0})


def _mem_specs(t_, m_, mw, col0):
    nh = mw // LANE
    qcol = pl.BlockSpec((1, t_, LANE), lambda b, h: (b, 0, col0 + h))
    kcol = pl.BlockSpec((1, m_, LANE), lambda b, h: (b, 0, h))
    vcol = pl.BlockSpec((1, m_, LANE), lambda b, h: (b, 0, nh + h))
    ycol = pl.BlockSpec((1, t_, LANE), lambda b, h: (b, 0, h))
    gvec = pl.BlockSpec((1, LANE), lambda b, h: (0, 0))
    return qcol, kcol, vcol, ycol, gvec


def _mem_fwd(zm, mkv, gq, gk, mw, col0):
    b_, t_, _ = zm.shape
    m_ = mkv.shape[1]
    tq = min(512, t_)
    nb = t_ // tq
    scale = MEM_DH ** -0.5
    qcol, kcol, vcol, ycol, gvec = _mem_specs(t_, m_, mw, col0)

    def body(q_ref, k_ref, v_ref, gq_ref, gk_ref, y_ref):
        gqv, gkv = gq_ref[...] * scale, gk_ref[...]
        kv = k_ref[0]
        kn = _mx(kv * lax.rsqrt(jnp.mean(kv * kv, axis=-1, keepdims=True) + EPS) * gkv)
        vv = _mx(v_ref[0])

        def blk(i, _):
            rows = pl.ds(pl.multiple_of(i * tq, tq), tq)
            qv = q_ref[0, rows, :]
            s = _nt(qv * lax.rsqrt(jnp.mean(qv * qv, axis=-1, keepdims=True) + EPS) * gqv, kn)
            e = jnp.exp(s - jnp.max(s, axis=-1, keepdims=True))
            y_ref[0, rows, :] = _nn(e / jnp.sum(e, axis=-1, keepdims=True), vv)
            return 0

        lax.fori_loop(0, nb, blk, 0)

    return pl.pallas_call(
        body, name="mem_fwd", grid=(b_, MEM_HEADS), in_specs=[qcol, kcol, vcol, gvec, gvec], out_specs=ycol,
        out_shape=jax.ShapeDtypeStruct((b_, t_, mw), F32), compiler_params=_params(2),
    )(zm, mkv, mkv, gq, gk)


def _mem_bwd(zm, mkv, dy, gq, gk, mw, col0, dz):
    b_, t_, _ = zm.shape
    m_ = mkv.shape[1]
    tq = min(512, t_)
    nb = t_ // tq
    scale = MEM_DH ** -0.5
    qcol, kcol, vcol, ycol, gvec = _mem_specs(t_, m_, mw, col0)

    def body(q_ref, k_ref, v_ref, dy_ref, gq_ref, gk_ref, _, dq_ref, dk_ref, dv_ref, dgq_ref, dgk_ref):
        gqv, gkv = gq_ref[...] * scale, gk_ref[...]
        kv = k_ref[0]
        kn = _mx(kv * lax.rsqrt(jnp.mean(kv * kv, axis=-1, keepdims=True) + EPS) * gkv)
        vv = _mx(v_ref[0])

        def blk(i, carry):
            dkn, dvv, dgq = carry
            rows = pl.ds(pl.multiple_of(i * tq, tq), tq)
            qv = q_ref[0, rows, :]
            qn = _mx(qv * lax.rsqrt(jnp.mean(qv * qv, axis=-1, keepdims=True) + EPS) * gqv)
            s = _nt(qn, kn)
            e = jnp.exp(s - jnp.max(s, axis=-1, keepdims=True))
            pm = e / jnp.sum(e, axis=-1, keepdims=True)
            dob = _mx(dy_ref[0, rows, :])
            dp = _nt(dob, vv)
            ds = pm * (dp - jnp.sum(dp * pm, axis=-1, keepdims=True))
            dqv, gq_part = _norm_bwd(qv, _nn(ds, kn), gqv)
            dq_ref[0, rows, :] = dqv.astype(dq_ref.dtype)
            return dkn + _tn(ds, qn), dvv + _tn(pm, dob), dgq + gq_part * scale

        z = jnp.zeros((m_, LANE), F32)
        dkn, dvv, dgq = lax.fori_loop(0, nb, blk, (z, z, jnp.zeros((1, LANE), F32)))
        dkv, dgk = _norm_bwd(kv, dkn, gkv)
        dk_ref[0] = dkv
        dv_ref[0] = dvv
        first = jnp.logical_and(pl.program_id(0) == 0, pl.program_id(1) == 0)
        _acc(dgq_ref, dgq, first)
        _acc(dgk_ref, dgk, first)

    kblk = pl.BlockSpec((1, m_, LANE), lambda b, h: (b, 0, h))
    gs = jax.ShapeDtypeStruct((1, LANE), F32)
    ks = jax.ShapeDtypeStruct((b_, m_, mw), F32)
    return pl.pallas_call(
        body, name="mem_bwd", grid=(b_, MEM_HEADS), in_specs=[qcol, kcol, vcol, ycol, gvec, gvec, ANY],
        out_specs=[qcol, kblk, kblk, gvec, gvec],
        out_shape=[jax.ShapeDtypeStruct(dz.shape, dz.dtype), ks, ks, gs, gs], input_output_aliases={6: 0},
        compiler_params=_params(2),
    )(zm, mkv, mkv, dy, gq, gk, dz)


def _merge_specs(tm, d, w, gcol):
    row_d = pl.BlockSpec((tm, d), lambda i: (i, 0))
    row_w = pl.BlockSpec((tm, w), lambda i: (i, 0))
    gates = [pl.BlockSpec((tm, d), functools.partial(lambda i, k: (i, gcol + k), k=k)) for k in range(3)]
    w_br = pl.BlockSpec((w, d), lambda i: (0, 0))
    w_o = pl.BlockSpec((d, d), lambda i: (0, 0))
    return row_d, row_w, gates, w_br, w_o


def _merge_fwd(x, ys, zm, w_brs, w_out, gcol, tm=256):
    n, d = x.shape
    w = ys[0].shape[1]
    tm = _tile(n, tm, 8)
    row_d, row_w, gates, w_br, w_o = _merge_specs(tm, d, w, gcol)

    def body(x_ref, ya, yb, yc, g0, g1, g2, wa, wb, wc, wo, x1_ref, mg_ref):
        mg = (_sig(g0[...]) * _nn(ya[...], wa[...]) + _sig(g1[...]) * _nn(yb[...], wb[...])
              + _sig(g2[...]) * _nn(yc[...], wc[...]))
        mg_ref[...] = mg.astype(mg_ref.dtype)
        x1_ref[...] = x_ref[...] + _nn(mg, wo[...])

    return pl.pallas_call(
        body, name="merge_fwd", grid=(n // tm,),
        in_specs=[row_d, row_w, row_w, row_w] + gates + [w_br, w_br, w_br, w_o],
        out_specs=[row_d, row_d],
        out_shape=[jax.ShapeDtypeStruct((n, d), F32), jax.ShapeDtypeStruct((n, d), MXU_DTYPE)],
        compiler_params=_params(1),
    )(x, *ys, zm, zm, zm, *w_brs, w_out)


def _merge_bwd(dx1, ys, zm, w_brs, w_out, gcol, tm=256):
    n, d = dx1.shape
    w = ys[0].shape[1]
    tm = _tile(n, tm, 8)
    row_d, row_w, gates, w_br, w_o = _merge_specs(tm, d, w, gcol)

    def body(dx_ref, ya, yb, yc, g0, g1, g2, wa, wb, wc, wo, dgl_ref, dpa, dpb, dpc, dya, dyb, dyc):
        dm = _nt(dx_ref[...], wo[...])
        for k, (y, g, wr, dp_ref, dy_ref) in enumerate(((ya, g0, wa, dpa, dya), (yb, g1, wb, dpb, dyb),
                                                        (yc, g2, wc, dpc, dyc))):
            sg = _sig(g[...])
            pr = _nn(y[...], wr[...])
            dgl_ref[:, k * d:(k + 1) * d] = (dm * pr * sg * (1.0 - sg)).astype(dgl_ref.dtype)
            dp = (dm * sg).astype(dp_ref.dtype)
            dp_ref[...] = dp
            dy_ref[...] = _nt(dp, wr[...])

    sd = jax.ShapeDtypeStruct((n, d), MXU_DTYPE)
    sw = jax.ShapeDtypeStruct((n, w), F32)
    return pl.pallas_call(
        body, name="merge_bwd", grid=(n // tm,),
        in_specs=[row_d, row_w, row_w, row_w] + gates + [w_br, w_br, w_br, w_o],
        out_specs=[pl.BlockSpec((tm, 3 * d), lambda i: (i, 0)), row_d, row_d, row_d, row_w, row_w, row_w],
        out_shape=[jax.ShapeDtypeStruct((n, zm.shape[1]), MXU_DTYPE), sd, sd, sd, sw, sw, sw],
        compiler_params=_params(1),
    )(dx1, *ys, zm, zm, zm, *w_brs, w_out)


CONV_ROWS = 256
HALO = 8


def _ext(ref, r0, t_):
    rc = min(CONV_ROWS, t_)
    a, b = max(r0 - HALO, 0), min(r0 + rc + HALO, t_)
    parts = []
    if r0 - HALO < 0:
        parts.append(jnp.zeros((HALO, ref.shape[2]), F32))
    parts.append(ref[0, a:b, :].astype(F32))
    if r0 + rc + HALO > t_:
        parts.append(jnp.zeros((HALO, ref.shape[2]), F32))
    return jnp.concatenate(parts, axis=0) if len(parts) > 1 else parts[0]


def _gelu_parts(ac):
    cdf = 0.5 * (1.0 + _erf(ac * (2.0 ** -0.5)))
    pdf = jnp.exp(-0.5 * ac * ac) * ((2.0 * math.pi) ** -0.5)
    return cdf, pdf


def _conv_taps(a_ext, cw, cb):
    return cw[0:1, :] * pltpu.roll(a_ext, 2, 0) + cw[1:2, :] * pltpu.roll(a_ext, 1, 0) + cw[2:3, :] * a_ext + cb


def _glu_specs(t_, f, g):
    gate = pl.BlockSpec((1, t_, g), lambda j, b: (b, 0, j))
    value = pl.BlockSpec((1, t_, g), lambda j, b: (b, 0, f // g + j))
    cwb = pl.BlockSpec((3, g), lambda j, b: (0, j))
    cbb = pl.BlockSpec((1, g), lambda j, b: (0, j))
    return gate, value, cwb, cbb


def _glu_fwd(u, cw, cb):
    b_, t_, f2 = u.shape
    f = f2 // 2
    g = min(FFN_GROUP, f)
    rc = min(CONV_ROWS, t_)
    gate, value, cwb, cbb = _glu_specs(t_, f, g)

    def body(a_ref, v_ref, cw_ref, cb_ref, y_ref):
        cwv, cbv = cw_ref[...], cb_ref[...]
        for r0 in range(0, t_, rc):
            ac = _conv_taps(_ext(a_ref, r0, t_), cwv, cbv)[HALO:HALO + rc]
            cdf, _ = _gelu_parts(ac)
            y_ref[0, r0:r0 + rc, :] = (ac * cdf * v_ref[0, r0:r0 + rc, :]).astype(y_ref.dtype)

    return pl.pallas_call(
        body, name="glu_fwd", grid=(f // g, b_), in_specs=[gate, value, cwb, cbb], out_specs=gate,
        out_shape=jax.ShapeDtypeStruct((b_, t_, f), MXU_DTYPE), compiler_params=_params(2),
    )(u, u, cw, cb)


def _glu_bwd(u, dy, cw, cb):
    b_, t_, f2 = u.shape
    f = f2 // 2
    g = min(FFN_GROUP, f)
    rc = min(CONV_ROWS, t_)
    ne = rc + 2 * HALO
    gate, value, cwb, cbb = _glu_specs(t_, f, g)

    def body(a_ref, v_ref, dy_ref, cw_ref, cb_ref, da_ref, dv_ref, dcw_ref, dcb_ref):
        cwv, cbv = cw_ref[...], cb_ref[...]
        dcw = [jnp.zeros((1, g), F32) for _ in range(3)]
        dcb = jnp.zeros((1, g), F32)
        for r0 in range(0, t_, rc):
            a_ext, v_ext, dy_ext = _ext(a_ref, r0, t_), _ext(v_ref, r0, t_), _ext(dy_ref, r0, t_)
            ac = _conv_taps(a_ext, cwv, cbv)
            cdf, pdf = _gelu_parts(ac)
            dac = dy_ext * v_ext * (cdf + ac * pdf)
            da = cwv[2:3, :] * dac + cwv[1:2, :] * pltpu.roll(dac, ne - 1, 0) + cwv[0:1, :] * pltpu.roll(dac, ne - 2, 0)
            mid = slice(HALO, HALO + rc)
            da_ref[0, r0:r0 + rc, :] = da[mid].astype(da_ref.dtype)
            dv_ref[0, r0:r0 + rc, :] = (dy_ext[mid] * ac[mid] * cdf[mid]).astype(dv_ref.dtype)
            dacm = dac[mid]
            dcw[0] = dcw[0] + jnp.sum(dacm * pltpu.roll(a_ext, 2, 0)[mid], axis=0, keepdims=True)
            dcw[1] = dcw[1] + jnp.sum(dacm * pltpu.roll(a_ext, 1, 0)[mid], axis=0, keepdims=True)
            dcw[2] = dcw[2] + jnp.sum(dacm * a_ext[mid], axis=0, keepdims=True)
            dcb = dcb + jnp.sum(dacm, axis=0, keepdims=True)
        first = pl.program_id(1) == 0
        _acc(dcw_ref, jnp.concatenate(dcw, axis=0), first)
        _acc(dcb_ref, dcb, first)

    sds = jax.ShapeDtypeStruct((b_, t_, f), MXU_DTYPE)
    return pl.pallas_call(
        body, name="glu_bwd", grid=(f // g, b_), in_specs=[gate, value, gate, cwb, cbb],
        out_specs=[gate, gate, cwb, cbb],
        out_shape=[sds, sds, jax.ShapeDtypeStruct((3, f), F32), jax.ShapeDtypeStruct((1, f), F32)],
        compiler_params=_params(2),
    )(u, u, dy, cw, cb)


def _place():
    x, y, c = lax.axis_index("x"), lax.axis_index("y"), lax.axis_index("c")
    chips = [(1 - x, y), (x, 1 - y), (1 - x, 1 - y)]
    return x, y, c, chips


def _remote(src, dst, send_sem, recv_sem, to):
    return pltpu.make_async_remote_copy(src_ref=src, dst_ref=dst, send_sem=send_sem, recv_sem=recv_sem,
                                        device_id=to, device_id_type=MESH)


STACK, COLS = "stack", "cols"


def _shard_ref(ref, kind, s, rows, c):
    if kind == COLS:
        cols = pl.ds(pl.multiple_of(s * c, LANE), c)
        return ref.at[:, cols] if rows is None else ref.at[rows, cols]
    return ref.at[s] if rows is None else ref.at[s, rows, :]


def _halves(c, half):
    mine = pl.ds(pl.multiple_of(c * half, 16), half)
    theirs = pl.ds(pl.multiple_of((1 - c) * half, 16), half)
    return mine, theirs


def _gather_parts(kinds):
    def first_copies(ins, outs, sems):
        x, y, c, chips = _place()
        me = 2 * x + y
        cps = []
        for i, (w_ref, o_ref, kind) in enumerate(zip(ins, outs, kinds)):
            r, cw = w_ref.shape
            mine, _ = _halves(c, r // 2)
            for j, chip in enumerate(chips):
                cps.append(_remote(w_ref.at[mine], _shard_ref(o_ref, kind, me, mine, cw), sems[0].at[6 * i + j],
                                   sems[1].at[6 * i + j], (*chip, c)))
        return cps

    def start(ins, outs, sems):
        for cp in first_copies(ins, outs, sems):
            cp.start()

    def finish(ins, outs, sems):
        x, y, c, chips = _place()
        sib = (x, y, 1 - c)
        passed = []
        for i, (w_ref, o_ref, kind) in enumerate(zip(ins, outs, kinds)):
            r, cw = w_ref.shape
            mine, _ = _halves(c, r // 2)
            for j, (px, py) in enumerate(chips):
                blk = _shard_ref(o_ref, kind, 2 * px + py, mine, cw)
                _remote(blk, blk, sems[0].at[6 * i + j], sems[1].at[6 * i + j], sib).wait_recv()
                passed.append(_remote(blk, blk, sems[0].at[6 * i + 3 + j], sems[1].at[6 * i + 3 + j], sib))
                passed[-1].start()
        for i, (w_ref, o_ref, kind) in enumerate(zip(ins, outs, kinds)):
            r, cw = w_ref.shape
            _, theirs = _halves(c, r // 2)
            for j, (px, py) in enumerate(chips):
                blk = _shard_ref(o_ref, kind, 2 * px + py, theirs, cw)
                _remote(blk, blk, sems[0].at[6 * i + 3 + j], sems[1].at[6 * i + 3 + j], sib).wait_recv()
        for cp in first_copies(ins, outs, sems) + passed:
            cp.wait_send()

    return start, finish


def _gather_shapes(shards, kinds):
    return [jax.ShapeDtypeStruct((a.shape[0], N_CHIPS * a.shape[1]) if k == COLS else (N_CHIPS,) + a.shape, a.dtype)
            for a, k in zip(shards, kinds)]


def _gather_sems(nw):
    return [pltpu.SemaphoreType.DMA((6 * nw,)), pltpu.SemaphoreType.DMA((6 * nw,))]


def _gather_shards(shards, kinds):
    nw = len(shards)
    start, finish = _gather_parts(kinds)

    def body(*refs):
        ins, outs, sems = refs[:nw], refs[nw:2 * nw], refs[2 * nw:]
        start(ins, outs, sems)
        finish(ins, outs, sems)

    return pl.pallas_call(
        body, name="gather_shards", in_specs=[ANY] * nw, out_specs=[ANY] * nw,
        out_shape=_gather_shapes(shards, kinds), scratch_shapes=_gather_sems(nw),
    )(*shards)


def _gather_rider(shards, kinds):
    start, finish = _gather_parts(kinds)
    return _Rider(list(shards), _gather_shapes(shards, kinds), _gather_sems(len(shards)), start, finish)


def _half_shape(g, kind):
    if kind == COLS:
        return (g.shape[0] // 2, g.shape[1])
    return (g.shape[0], g.shape[1] // 2, g.shape[2])


def _swap_parts(kinds):
    def copies(ins, outs, sems):
        x, y, c, _ = _place()
        cps = []
        for i, (g_ref, a_ref, kind) in enumerate(zip(ins, outs, kinds)):
            r = g_ref.shape[0] if kind == COLS else g_ref.shape[1]
            _, theirs = _halves(c, r // 2)
            src = g_ref.at[theirs] if kind == COLS else g_ref.at[:, theirs]
            cps.append(_remote(src, a_ref, sems[0].at[i], sems[1].at[i], (x, y, 1 - c)))
        return cps

    def start(ins, outs, sems):
        for cp in copies(ins, outs, sems):
            cp.start()

    def finish(ins, outs, sems):
        for cp in copies(ins, outs, sems):
            cp.wait()

    return start, finish


def _swap_shapes(gs, kinds):
    return [jax.ShapeDtypeStruct(_half_shape(g, k), g.dtype) for g, k in zip(gs, kinds)]


def _pair_swap_halves(gs, kinds, name):
    nw = len(gs)
    start, finish = _swap_parts(kinds)

    def body(*refs):
        ins, outs, sems = refs[:nw], refs[nw:2 * nw], refs[2 * nw:]
        start(ins, outs, sems)
        finish(ins, outs, sems)

    return pl.pallas_call(
        body, name=name, in_specs=[ANY] * nw, out_specs=[ANY] * nw, out_shape=_swap_shapes(gs, kinds),
        scratch_shapes=[pltpu.SemaphoreType.DMA((nw,)), pltpu.SemaphoreType.DMA((nw,))],
    )(*gs)


def _swap_rider(gs, kinds):
    start, finish = _swap_parts(kinds)
    nw = len(gs)
    return _Rider(list(gs), _swap_shapes(gs, kinds), [pltpu.SemaphoreType.DMA((nw,)), pltpu.SemaphoreType.DMA((nw,))],
                  start, finish)


def _row_tile(rows, width, itemsize=4, target=2 ** 21):
    return _tile(rows, max(8, target // (width * itemsize)), 8)


def _add_half(g, a, kind, c_idx, name):
    if kind == COLS:
        half, wd = a.shape
        tr = _row_tile(half, wd)
        nblk = half // tr
        grid = (nblk,)
        g_spec = pl.BlockSpec((tr, wd), lambda i, c_ref: (c_ref[0] * nblk + i, 0))
        a_spec = pl.BlockSpec((tr, wd), lambda i, c_ref: (i, 0))
    else:
        n, half, wd = a.shape
        tr = _row_tile(half, wd)
        nblk = half // tr
        grid = (n, nblk)
        g_spec = pl.BlockSpec((1, tr, wd), lambda s, i, c_ref: (s, c_ref[0] * nblk + i, 0))
        a_spec = pl.BlockSpec((1, tr, wd), lambda s, i, c_ref: (s, i, 0))

    def body(c_ref, g_ref, a_ref, o_ref):
        o_ref[...] = (g_ref[...] + a_ref[...]).astype(o_ref.dtype)

    return pl.pallas_call(
        body, name=name,
        grid_spec=pltpu.PrefetchScalarGridSpec(num_scalar_prefetch=1, grid=grid, in_specs=[g_spec, a_spec],
                                               out_specs=a_spec),
        out_shape=jax.ShapeDtypeStruct(a.shape, EXCHANGE_DTYPE), compiler_params=_params(len(grid)),
    )(c_idx, g, a)


def _exchange_parts(kinds):
    def copies(ins, outs, sems):
        x, y, c, chips = _place()
        me = 2 * x + y
        cps = []
        for i, (p_ref, b_ref, kind) in enumerate(zip(ins, outs, kinds)):
            cw = b_ref.shape[2]
            for j, (px, py) in enumerate(chips):
                cps.append(_remote(_shard_ref(p_ref, kind, 2 * px + py, None, cw), b_ref.at[me],
                                   sems[0].at[3 * i + j], sems[1].at[3 * i + j], (px, py, c)))
        return cps

    def start(ins, outs, sems):
        for cp in copies(ins, outs, sems):
            cp.start()

    def finish(ins, outs, sems):
        x, y, c, chips = _place()
        for i, b_ref in enumerate(outs):
            for j, (px, py) in enumerate(chips):
                blk = b_ref.at[2 * px + py]
                _remote(blk, blk, sems[0].at[3 * i + j], sems[1].at[3 * i + j], (px, py, c)).wait_recv()
        for cp in copies(ins, outs, sems):
            cp.wait_send()

    return start, finish


def _exchange_shapes(ps, kinds):
    return [jax.ShapeDtypeStruct((N_CHIPS,) + ((p.shape[0], p.shape[1] // N_CHIPS) if k == COLS else tuple(p.shape[1:])),
                                 p.dtype) for p, k in zip(ps, kinds)]


def _exchange_sems(nw):
    return [pltpu.SemaphoreType.DMA((3 * nw,)), pltpu.SemaphoreType.DMA((3 * nw,))]


def _exchange_rider(ps, kinds):
    start, finish = _exchange_parts(kinds)
    return _Rider(list(ps), _exchange_shapes(ps, kinds), _exchange_sems(len(ps)), start, finish)


def _sum_chips(bq, name):
    n, h, wd = bq.shape
    tr = _row_tile(h, wd * n)

    def body(b_ref, o_ref):
        acc = b_ref[0].astype(F32)
        for s in range(1, n):
            acc = acc + b_ref[s].astype(F32)
        o_ref[...] = acc

    return pl.pallas_call(
        body, name=name, grid=(h // tr,),
        in_specs=[pl.BlockSpec((n, tr, wd), lambda i: (0, i, 0))], out_specs=pl.BlockSpec((tr, wd), lambda i: (i, 0)),
        out_shape=jax.ShapeDtypeStruct((h, wd), F32), compiler_params=_params(1),
    )(bq)


def _pair_join_halves(qs):
    nw = len(qs)

    def body(*refs):
        ins, outs = refs[:nw], refs[nw:2 * nw]
        send_sems, recv_sems = refs[2 * nw:]
        x, y, c, _ = _place()
        sent = []
        for i, (q_ref, o_ref) in enumerate(zip(ins, outs)):
            mine, _ = _halves(c, q_ref.shape[0])
            sent.append(_remote(q_ref, o_ref.at[mine], send_sems.at[i], recv_sems.at[i], (x, y, 1 - c)))
            sent[-1].start()
        for i, (q_ref, o_ref) in enumerate(zip(ins, outs)):
            _, theirs = _halves(c, q_ref.shape[0])
            _remote(q_ref, o_ref.at[theirs], send_sems.at[i], recv_sems.at[i], (x, y, 1 - c)).wait_recv()
        for cp in sent:
            cp.wait_send()

    return pl.pallas_call(
        body, name="pair_join_halves", in_specs=[ANY] * nw, out_specs=[ANY] * nw,
        out_shape=[jax.ShapeDtypeStruct((2 * q.shape[0], q.shape[1]), q.dtype) for q in qs],
        scratch_shapes=[pltpu.SemaphoreType.DMA((nw,)), pltpu.SemaphoreType.DMA((nw,))],
    )(*qs)


def _all_sum_small(s, name):
    sr, w = s.shape

    def body(s_ref, o_ref, buf, send_sems, recv_sems):
        x, y, c, _ = _place()
        me = 4 * x + 2 * y + c
        buf[me] = s_ref[...]
        peers = []
        for k in range(1, 8):
            px = 1 - x if k & 4 else x
            py = 1 - y if k & 2 else y
            pc = 1 - c if k & 1 else c
            peers.append((px, py, pc))
        sent = [_remote(s_ref, buf.at[me], send_sems.at[k], recv_sems.at[k], peer) for k, peer in enumerate(peers)]
        for cp in sent:
            cp.start()
        for k, (px, py, pc) in enumerate(peers):
            _remote(s_ref, buf.at[4 * px + 2 * py + pc], send_sems.at[k], recv_sems.at[k], (px, py, pc)).wait_recv()
        for cp in sent:
            cp.wait_send()
        acc = buf[0]
        for d in range(1, 8):
            acc = acc + buf[d]
        o_ref[...] = acc

    vm = pl.BlockSpec(memory_space=pltpu.VMEM)
    return pl.pallas_call(
        body, name=name, in_specs=[vm], out_specs=vm, out_shape=jax.ShapeDtypeStruct((sr, w), F32),
        scratch_shapes=[pltpu.VMEM((8, sr, w), F32), pltpu.SemaphoreType.DMA((7,)), pltpu.SemaphoreType.DMA((7,))],
    )(s)


BIG = ("w_in", "mem_kv_w", "w_br_hgrn", "w_br_fox", "w_br_mem", "w_out", "ffn_w_up", "ffn_w_down")
KIND = {"w_in": STACK, "mem_kv_w": STACK, "w_br_hgrn": COLS, "w_br_fox": COLS, "w_br_mem": COLS, "w_out": STACK,
        "ffn_w_up": STACK, "ffn_w_down": STACK}
ROW_SHARDED = ("mem_kv_w", "w_out", "ffn_w_down")
FIRST = ("w_in",)
REST = tuple(nm for nm in BIG if nm not in FIRST)
LATE = {"in_proj": tuple(nm for nm in REST if not nm.startswith("ffn_")),
        "fox_fwd": tuple(nm for nm in REST if nm.startswith("ffn_"))}
LAST = ("w_in",)
TRANSPOSED = ("w_in",)


def _z_layout(d, hw, fw, mw):
    gate, npair, nh, nm = 3 * d // LANE, fw // LANE, hw // LANE, mw // LANE
    fox0, hg0 = gate, gate + 3 * npair
    o_fox, o_mem = 4 * nh, 4 * nh + 3 * npair
    order = [o_mem + nm + j for j in range(gate)]
    order += [o_fox + k * npair + p for p in range(npair) for k in range(3)]
    order += [k * nh + h for h in range(nh) for k in range(4)]
    order += [o_mem + h for h in range(nm)]
    assert fox0 % 3 == 0 and hg0 % 4 == 0
    return fox0, hg0, hg0 + 4 * nh, order


def _reorder_blocks(a, order):
    runs, start = [], 0
    for i in range(1, len(order) + 1):
        if i == len(order) or order[i] != order[i - 1] + 1:
            runs.append((order[start], order[i - 1] + 1))
            start = i
    return jnp.concatenate([a[:, lo * LANE:hi * LANE] for lo, hi in runs], axis=1)


def _put_shard(arr, kind, s, piece):
    if kind == COLS:
        return lax.dynamic_update_slice(arr, piece, (0, s * piece.shape[1]))
    return lax.dynamic_update_slice(arr, piece[None], (s, 0, 0))


def _take_shard(arr, kind, s):
    if kind == COLS:
        return lax.dynamic_slice(arr, (0, s * (arr.shape[1] // N_CHIPS)), (arr.shape[0], arr.shape[1] // N_CHIPS))
    return lax.dynamic_index_in_dim(arr, s, 0, keepdims=False)


def _w_in_pieces(cs, s1, nf):
    out = []
    for s in range(N_CHIPS):
        lo, hi = cs * s, cs * (s + 1)
        for a, b, forget in ((lo, min(hi, s1), False), (max(lo, s1), min(hi, s1 + nf), True), (max(lo, s1 + nf), hi, False)):
            if a < b:
                out.append((s, a - lo, b - lo, forget, a - s1 if forget else (a if a < s1 else a - nf)))
    return out


def _split_w_in(stacked, s1, nf):
    pieces = _w_in_pieces(stacked.shape[2], s1, nf)
    main = [stacked[s, :, a:b] for s, a, b, forget, _ in pieces if not forget]
    ff = [stacked[s, :, a:b] for s, a, b, forget, _ in pieces if forget]
    return jnp.concatenate(main, axis=1), jnp.concatenate(ff, axis=1)


def _join_w_in(g_main, g_ff, s1, nf):
    cs = (g_main.shape[1] + nf) // N_CHIPS
    shards = [[] for _ in range(N_CHIPS)]
    for s, a, b, forget, off in _w_in_pieces(cs, s1, nf):
        shards[s].append((g_ff if forget else g_main)[:, off:off + b - a])
    return jnp.stack([jnp.concatenate(p, axis=1) if len(p) > 1 else p[0] for p in shards])


SMALL = ("norm_mix_g", "norm_mem_g", "norm_ffn_g", "hgrn_lb_logits", "hgrn_norm_g", "fox_f_bias", "fox_q_norm_g",
         "fox_k_norm_g", "mem_q_norm_g", "mem_k_norm_g", "ffn_conv_b")


def _small_rows(shapes):
    rows = []
    for a, (r, c) in enumerate(shapes):
        for i in range(r):
            for lo in range(0, c, FLAT_W):
                rows.append((a, i, lo, min(FLAT_W, c - lo)))
    return rows


def _pack_small(vals):
    rows = _small_rows([v.shape for v in vals])
    sr = -(-len(rows) // 8) * 8

    def body(*refs):
        o_ref = refs[-1]
        o_ref[...] = jnp.zeros(o_ref.shape, F32)
        for k, (a, i, lo, wd) in enumerate(rows):
            o_ref[k:k + 1, 0:wd] = refs[a][i:i + 1, lo:lo + wd]

    vm = pl.BlockSpec(memory_space=pltpu.VMEM)
    return pl.pallas_call(body, name="pack_small", in_specs=[vm] * len(vals), out_specs=vm,
                          out_shape=jax.ShapeDtypeStruct((sr, FLAT_W), F32))(*vals)


def _row_of(buf_ref, rows, a, i):
    parts = [buf_ref[k:k + 1, 0:wd] for k, (a2, i2, _, wd) in enumerate(rows) if (a2, i2) == (a, i)]
    return jnp.concatenate(parts, axis=1) if len(parts) > 1 else parts[0]


def _unpack_small(buf, shapes):
    rows = _small_rows(shapes)

    def body(buf_ref, *outs):
        for a, (r, _) in enumerate(shapes):
            for i in range(r):
                outs[a][i:i + 1, :] = _row_of(buf_ref, rows, a, i)

    vm = pl.BlockSpec(memory_space=pltpu.VMEM)
    return pl.pallas_call(body, name="unpack_small", in_specs=[vm], out_specs=[vm] * len(shapes),
                          out_shape=[jax.ShapeDtypeStruct(shp, F32) for shp in shapes])(buf)


def _adamw_small(buf, shapes, ws, ms, vs):
    n = len(ws)
    rows = _small_rows(shapes)
    c1 = 1.0 / (1.0 - ADAM_B1 ** ADAM_STEP)
    c2 = 1.0 / (1.0 - ADAM_B2 ** ADAM_STEP)

    def body(buf_ref, *refs):
        w_refs, m_refs, v_refs = refs[:n], refs[n:2 * n], refs[2 * n:3 * n]
        outs = refs[3 * n:]
        g_out, d_out, m_out, v_out, rest = outs[:n], outs[n:2 * n], outs[2 * n:3 * n], outs[3 * n:4 * n], outs[4 * n:]
        for a, (r, _) in enumerate(shapes):
            for i in range(r):
                gv = _row_of(buf_ref, rows, a, i)
                if a >= n:
                    rest[a - n][i:i + 1, :] = gv
                    continue
                row = slice(i, i + 1)
                mn = ADAM_B1 * m_refs[a][row, :] + (1.0 - ADAM_B1) * gv
                vn = ADAM_B2 * v_refs[a][row, :] + (1.0 - ADAM_B2) * (gv * gv)
                g_out[a][row, :] = gv
                d_out[a][row, :] = -ADAM_LR * ((mn * c1) / (jnp.sqrt(vn * c2) + ADAM_EPS) + ADAM_WD * w_refs[a][row, :])
                m_out[a][row, :] = mn
                v_out[a][row, :] = vn

    vm = pl.BlockSpec(memory_space=pltpu.VMEM)
    own = [jax.ShapeDtypeStruct(shp, F32) for shp in shapes[:n]]
    outs = pl.pallas_call(
        body, name="adamw_small", in_specs=[vm] * (1 + 3 * n), out_specs=[vm] * (4 * n + len(shapes) - n),
        out_shape=own * 4 + [jax.ShapeDtypeStruct(shp, F32) for shp in shapes[n:]],
    )(buf, *ws, *ms, *vs)
    return outs[:n], outs[n:2 * n], outs[2 * n:3 * n], outs[3 * n:4 * n], outs[4 * n:]


def _pad_lanes(v, width=LANE):
    return jnp.pad(v, ((0, 0), (0, width - v.shape[1])))


WEIGHTS = ("norm_mix_g", "norm_mem_g", "w_in", "hgrn_lb_logits", "hgrn_norm_g", "fox_f_bias", "fox_q_norm_g",
           "fox_k_norm_g", "mem_kv_w", "mem_q_norm_g", "mem_k_norm_g", "w_br_hgrn", "w_br_fox", "w_br_mem", "w_out",
           "norm_ffn_g", "ffn_w_up", "ffn_conv_w", "ffn_conv_b", "ffn_w_down")


def _local_step(x, mem, target, w, full, conv_w, late=None, hooks=None):
    b_, t_, d = x.shape
    n = b_ * t_
    hw, fw, mw = HG_HEADS * HG_D, FOX_HEADS * FOX_DH, MEM_HEADS * MEM_DH
    m_ = mem.shape[1]
    f = conv_w.shape[1]
    s1 = 4 * hw + 3 * fw
    fox_col, hg_col, mem_col, order = _z_layout(d, hw, fw, mw)
    gate_col = 0
    inverse = [order.index(j) for j in range(len(order))]

    w_main, w_ff = _split_w_in(full["w_in"], s1, FOX_HEADS)
    w_main = _reorder_blocks(w_main, order)
    w_ff = _pad_lanes(w_ff)
    f_bias = _pad_lanes(w["fox_f_bias"])
    cb = w["ffn_conv_b"]

    x2 = x.reshape(n, d)
    h = _rmsnorm_fwd(x2, w["norm_mix_g"], name="norm_mix_fwd")
    if late:
        pieces, kinds, finish = late["in_proj"]
        zm, gathered = _matmul(h, w_main, name="in_proj", rider=_gather_rider(pieces, kinds))
        full = {**full, **finish(gathered)}
    else:
        zm = _matmul(h, w_main, name="in_proj")
    w_brs = [full["w_br_hgrn"], full["w_br_fox"], full["w_br_mem"]]
    w_out, w_kv = full["w_out"], full["mem_kv_w"]
    zf = _matmul(h, w_ff, name="in_proj_forget")
    zm3, zf3 = zm.reshape(b_, t_, -1), zf.reshape(b_, t_, LANE)
    ya = _hgrn_fwd(zm3, w["hgrn_lb_logits"], w["hgrn_norm_g"], hw, hg_col)
    fc = _fox_prep(zf3, f_bias)
    fox_gq, fox_gk = jnp.tile(w["fox_q_norm_g"], (1, 2)), jnp.tile(w["fox_k_norm_g"], (1, 2))
    if late:
        pieces, kinds, finish = late["fox_fwd"]
        (yb, lse), gathered = _fox_fwd(zm3, fc, fox_gq, fox_gk, fw, fox_col, _gather_rider(pieces, kinds))
        full = {**full, **finish(gathered)}
    else:
        yb, lse = _fox_fwd(zm3, fc, fox_gq, fox_gk, fw, fox_col)[0]
    w_up, w_down = full["ffn_w_up"], full["ffn_w_down"]
    mem2 = mem.reshape(b_ * m_, d)
    hm = _rmsnorm_fwd(mem2, w["norm_mem_g"], name="norm_mem_fwd")
    mkv = _matmul(hm, w_kv, name="mem_kv_proj").reshape(b_, m_, 2 * mw)
    yc = _mem_fwd(zm3, mkv, w["mem_q_norm_g"], w["mem_k_norm_g"], mw, mem_col)
    ys = [ya.reshape(n, hw), yb.reshape(n, fw), yc.reshape(n, mw)]
    x1, merged = _merge_fwd(x2, ys, zm, w_brs, w_out, gate_col)
    h2 = _rmsnorm_fwd(x1, w["norm_ffn_g"], name="norm_ffn_fwd")
    u = _matmul(h2, w_up, name="ffn_up")
    u3 = u.reshape(b_, t_, 2 * f)
    yff = _glu_fwd(u3, conv_w, cb).reshape(n, f)
    dy, (loss_vec,), _ = _matmul_rows([yff], w_down, name="ffn_down_loss", tb=False, row_ins=[x1, target.reshape(n, d)],
                                      vec_ins=[], epilogue=_loss_epilogue, n_vec_out=1)

    grads = {}

    def ridden(name, call):
        if not hooks or name not in hooks:
            return call(None)[0]
        rider, then = hooks[name](grads)
        outs, extra = call(rider)
        then(extra)
        return outs

    dyff = _matmul(dy, w_down, tb=True, name="ffn_down_dx")
    grads["ffn_w_down"] = _matmul(yff, dy, ta=True, name="ffn_down_dw", tm=1408)
    du_a, du_v, grads["ffn_conv_w"], grads["ffn_conv_b"] = _glu_bwd(u3, dyff.reshape(b_, t_, f), conv_w, cb)
    du_a, du_v = du_a.reshape(n, f), du_v.reshape(n, f)
    dx1, (grads["norm_ffn_g"],), _ = _matmul_rows(
        [du_a, du_v], w_up, name="ffn_up_dx", tb=True, row_ins=[x1, dy], vec_ins=[w["norm_ffn_g"]],
        epilogue=_norm_bwd_epilogue(0), n_vec_out=1)
    grads["ffn_w_up"] = _matmul(h2, None, ta=True, name="ffn_up_dw", b_parts=[du_a, du_v], tn=f // 2, stack_out=True)

    dz, dpa, dpb, dpc, dya, dyb, dyc = _merge_bwd(dx1, ys, zm, w_brs, w_out, gate_col)
    dz = dz.reshape(b_, t_, -1)
    grads["w_out"] = _matmul(merged, dx1, ta=True, name="out_proj_dw")
    for nm, y_, dp_ in zip(("w_br_hgrn", "w_br_fox", "w_br_mem"), ys, (dpa, dpb, dpc)):
        grads[nm] = _matmul(y_, dp_, ta=True, name=nm + "_dw")

    dz, dmk, dmv, grads["mem_q_norm_g"], grads["mem_k_norm_g"] = _mem_bwd(
        zm3, mkv, dyc.reshape(b_, t_, mw), w["mem_q_norm_g"], w["mem_k_norm_g"], mw, mem_col, dz)
    dmkv = jnp.concatenate([dmk, dmv], axis=-1).reshape(b_ * m_, 2 * mw)
    grads["mem_kv_w"] = _matmul(hm, dmkv, ta=True, name="mem_kv_dw")
    dhm = _matmul(dmkv, w_kv, tb=True, name="mem_kv_dx")
    _, grads["norm_mem_g"] = _rmsnorm_bwd(mem2, [dhm], w["norm_mem_g"], None, name="norm_mem_bwd")

    dz, dfc, g_fq, g_fk = ridden("fox_bwd", lambda rider: _fox_bwd(
        zm3, yb, dyb.reshape(b_, t_, fw), lse, fc, fox_gq, fox_gk, fw, fox_col, dz, rider))
    grads["fox_q_norm_g"] = g_fq[:, :FOX_DH] + g_fq[:, FOX_DH:]
    grads["fox_k_norm_g"] = g_fk[:, :FOX_DH] + g_fk[:, FOX_DH:]
    dzf, g_fb = _fox_post(dfc, zf3, f_bias)
    grads["fox_f_bias"] = g_fb[:, :FOX_HEADS]

    dz, grads["hgrn_lb_logits"], grads["hgrn_norm_g"] = ridden("hgrn_bwd", lambda rider: _hgrn_bwd(
        zm3, dya.reshape(b_, t_, hw), w["hgrn_lb_logits"], w["hgrn_norm_g"], hw, hg_col, dz, rider))
    dzm = dz.reshape(n, -1)
    dzf2 = dzf.reshape(n, LANE)
    g_main = _matmul(h, dzm, ta=True, name="in_proj_dw")
    g_ff = _matmul(h, dzf2, ta=True, name="in_proj_forget_dw")
    grads["w_in"] = _join_w_in(_reorder_blocks(g_main, inverse), g_ff[:, :FOX_HEADS], s1, FOX_HEADS)

    dh_b = _matmul(dzf2, w_ff, tb=True, name="in_proj_forget_dx")

    def in_proj_dx(rider):
        out = _matmul(dzm, w_main, tb=True, name="in_proj_dx", rider=rider)
        return ([out[0]], out[1]) if rider else ([out], None)

    dh_a, = ridden("in_proj_dx", in_proj_dx)
    grad_x, grads["norm_mix_g"] = _rmsnorm_bwd(x2, [dh_a, dh_b], w["norm_mix_g"], dx1, name="norm_mix_bwd")
    return loss_vec, grad_x.reshape(b_, t_, d), grads


def kernel(x, mem, norm_mix_g, norm_mem_g, w_in, hgrn_lb_logits, hgrn_norm_g, fox_f_bias, fox_q_norm_g, fox_k_norm_g, mem_kv_w, mem_q_norm_g, mem_k_norm_g, w_br_hgrn, w_br_fox, w_br_mem, w_out, norm_ffn_g, ffn_w_up, ffn_conv_w, ffn_conv_b, ffn_w_down, loss_target, m_norm_mix_g, m_norm_mem_g, m_w_in, m_hgrn_lb_logits, m_hgrn_norm_g, m_fox_f_bias, m_fox_q_norm_g, m_fox_k_norm_g, m_mem_kv_w, m_mem_q_norm_g, m_mem_k_norm_g, m_w_br_hgrn, m_w_br_fox, m_w_br_mem, m_w_out, m_norm_ffn_g, m_ffn_w_up, m_ffn_conv_w, m_ffn_conv_b, m_ffn_w_down, v_norm_mix_g, v_norm_mem_g, v_w_in, v_hgrn_lb_logits, v_hgrn_norm_g, v_fox_f_bias, v_fox_q_norm_g, v_fox_k_norm_g, v_mem_kv_w, v_mem_q_norm_g, v_mem_k_norm_g, v_w_br_hgrn, v_w_br_fox, v_w_br_mem, v_w_out, v_norm_ffn_g, v_ffn_w_up, v_ffn_conv_w, v_ffn_conv_b, v_ffn_w_down):
    w = dict(zip(WEIGHTS, (norm_mix_g, norm_mem_g, w_in, hgrn_lb_logits, hgrn_norm_g, fox_f_bias, fox_q_norm_g,
                           fox_k_norm_g, mem_kv_w, mem_q_norm_g, mem_k_norm_g, w_br_hgrn, w_br_fox, w_br_mem, w_out,
                           norm_ffn_g, ffn_w_up, ffn_conv_w, ffn_conv_b, ffn_w_down)))
    m = dict(zip(WEIGHTS, (m_norm_mix_g, m_norm_mem_g, m_w_in, m_hgrn_lb_logits, m_hgrn_norm_g, m_fox_f_bias,
                           m_fox_q_norm_g, m_fox_k_norm_g, m_mem_kv_w, m_mem_q_norm_g, m_mem_k_norm_g, m_w_br_hgrn,
                           m_w_br_fox, m_w_br_mem, m_w_out, m_norm_ffn_g, m_ffn_w_up, m_ffn_conv_w, m_ffn_conv_b,
                           m_ffn_w_down)))
    v = dict(zip(WEIGHTS, (v_norm_mix_g, v_norm_mem_g, v_w_in, v_hgrn_lb_logits, v_hgrn_norm_g, v_fox_f_bias,
                           v_fox_q_norm_g, v_fox_k_norm_g, v_mem_kv_w, v_mem_q_norm_g, v_mem_k_norm_g, v_w_br_hgrn,
                           v_w_br_fox, v_w_br_mem, v_w_out, v_norm_ffn_g, v_ffn_w_up, v_ffn_conv_w, v_ffn_conv_b,
                           v_ffn_w_down)))
    c_idx = lax.axis_index("c")
    chip = 2 * lax.axis_index("x") + lax.axis_index("y")

    mine = {nm: w[nm][0].astype(MXU_DTYPE) for nm in BIG}

    def gathered_full(names, arrays):
        out = {nm: _put_shard(g, KIND[nm], chip, mine[nm]) for nm, g in zip(names, arrays)}
        return {nm: g.reshape(-1, g.shape[2]) if nm in ROW_SHARDED else g for nm, g in out.items()}

    full = gathered_full(FIRST, _gather_shards([mine[nm] for nm in FIRST], [KIND[nm] for nm in FIRST]))
    late = {host: ([mine[nm] for nm in names], [KIND[nm] for nm in names],
                   functools.partial(gathered_full, names)) for host, names in LATE.items()}
    cs = ffn_conv_w.shape[2]
    f = cs * N_CHIPS
    placed = lax.dynamic_update_slice(jnp.zeros((3, f), F32), ffn_conv_w[0] * (c_idx == 0).astype(F32), (0, chip * cs))
    conv_w = _unpack_small(_all_sum_small(_pack_small([placed]), "gather_conv_w"), [(3, f)])[0]

    c_arr = jnp.reshape(c_idx, (1,)).astype(jnp.int32)

    def stacked(nm, g):
        return g.reshape(N_CHIPS, -1, g.shape[1]) if nm in ROW_SHARDED else g

    def with_own(landed, partial, kinds):
        return [_put_shard(bq, STACK, chip, _take_shard(p, k, chip)) for bq, p, k in zip(landed, partial, kinds)]

    kinds_rest, kinds_last = [KIND[nm] for nm in REST], [KIND[nm] for nm in LAST]
    state = {}

    def swap_rest(grads):
        gs = [stacked(nm, grads[nm]) for nm in REST]

        def then(from_sibling):
            state["partial_rest"] = [_add_half(g, a, k, c_arr, "add_half_" + nm)
                                     for g, a, k, nm in zip(gs, from_sibling, kinds_rest, REST)]

        return _swap_rider(gs, kinds_rest), then

    def exchange_rest(grads):
        def then(landed):
            state["landed_rest"] = with_own(landed, state["partial_rest"], kinds_rest)

        return _exchange_rider(state["partial_rest"], kinds_rest), then

    def exchange_last(grads):
        gs = [stacked(nm, grads[nm]) for nm in LAST]
        from_sibling = _pair_swap_halves(gs, kinds_last, "pair_swap_halves_last")
        partial = [_add_half(g, a, k, c_arr, "add_half_" + nm) for g, a, k, nm in zip(gs, from_sibling, kinds_last, LAST)]

        def then(landed):
            state["landed_last"] = with_own(landed, partial, kinds_last)

        return _exchange_rider(partial, kinds_last), then

    hooks = {"fox_bwd": swap_rest, "hgrn_bwd": exchange_rest, "in_proj_dx": exchange_last}

    loss_vec, grad_x, grads = _local_step(x, mem, loss_target, w, full, conv_w, late, hooks)

    landed = dict(zip(LAST + REST, state["landed_last"] + state["landed_rest"]))
    reduced_half = [_sum_chips(landed[nm], "sum_chips_" + nm) for nm in BIG]
    joined = [lax.dynamic_update_slice(o, q, (c_idx * q.shape[0], 0))
              for o, q in zip(_pair_join_halves(reduced_half), reduced_half)]
    gshards = dict(zip(BIG, joined))

    small_shapes = [w[nm].shape for nm in SMALL] + [grads["ffn_conv_w"].shape, loss_vec.shape]
    summed = _all_sum_small(_pack_small([grads[nm] for nm in SMALL] + [grads["ffn_conv_w"], loss_vec]),
                            "all_sum_small_grads")
    g_small, d_small, m_small, v_small, (g_conv_w, loss_row) = _adamw_small(
        summed, small_shapes, [w[nm] for nm in SMALL], [m[nm] for nm in SMALL], [v[nm] for nm in SMALL])
    loss = jnp.sum(loss_row)
    g_out = {nm: gshards[nm][None] for nm in BIG}
    g_out["ffn_conv_w"] = lax.dynamic_slice(g_conv_w, (0, chip * cs), (3, cs))[None]
    delta, new_m, new_v = dict(zip(SMALL, d_small)), dict(zip(SMALL, m_small)), dict(zip(SMALL, v_small))
    g_out.update(zip(SMALL, g_small))
    for nm in BIG + ("ffn_conv_w",):
        operands = (w[nm], g_out[nm], m[nm], v[nm])
        if nm in TRANSPOSED:
            operands = [jnp.swapaxes(a, 1, 2) for a in operands]
        outs = _adamw(*operands, name="adamw_" + nm)
        delta[nm], new_m[nm], new_v[nm] = [jnp.swapaxes(o, 1, 2) for o in outs] if nm in TRANSPOSED else outs

    return (loss, grad_x, *[g_out[nm] for nm in WEIGHTS], *[delta[nm] for nm in WEIGHTS],
            *[new_m[nm] for nm in WEIGHTS], *[new_v[nm] for nm in WEIGHTS])
```

```python
import functools
import math

import jax
import jax.numpy as jnp
from jax import lax
from jax.experimental import pallas as pl
from jax.experimental.pallas import tpu as pltpu

F32 = jnp.float32
BF16 = jnp.bfloat16
MXU_DTYPE = jnp.bfloat16
EXCHANGE_DTYPE = jnp.bfloat16

EPS = 1e-6
HG_HEADS, HG_D = 4, 128
FOX_HEADS, FOX_DH = 8, 64
MEM_HEADS, MEM_DH = 4, 128
HG_CHUNK = 64
FOX_BLOCK = 256
LANE = 128
FFN_GROUP = 256
FLAT_W = 1024
VMEM_LIMIT = 56 * 2 ** 20
NEG = -1e30
N_CHIPS = 4

ADAM_LR, ADAM_B1, ADAM_B2, ADAM_EPS, ADAM_WD, ADAM_STEP = 0.001, 0.9, 0.999, 1e-08, 0.01, 10

MESH = pl.DeviceIdType.MESH
ANY = pl.BlockSpec(memory_space=pl.ANY)


def _mx(x):
    return x.astype(MXU_DTYPE)


def _dot(a, b, ca, cb):
    return lax.dot_general(_mx(a), _mx(b), (((ca,), (cb,)), ((), ())), preferred_element_type=F32)


def _nn(a, b):
    return _dot(a, b, 1, 0)


def _nt(a, b):
    return _dot(a, b, 1, 1)


def _tn(a, b):
    return _dot(a, b, 0, 0)


def _dotp(a, b, ca, cb):
    return lax.dot_general(a, b, (((ca,), (cb,)), ((), ())), precision=lax.Precision.HIGHEST,
                           preferred_element_type=F32)


def _tri_dot(tri_bf, x):
    hi = x.astype(BF16)
    r = x - hi.astype(F32)
    mid = r.astype(BF16)
    lo = (r - mid.astype(F32)).astype(BF16)

    def d(v):
        return lax.dot_general(tri_bf, v, (((1,), (0,)), ((), ())), preferred_element_type=F32)

    return d(hi) + d(mid) + d(lo)


def _sig(x):
    return jax.nn.sigmoid(x)


def _tile(dim, pref, unit=LANE):
    if dim <= pref:
        return dim
    t = pref - pref % unit
    while t >= unit:
        if dim % t == 0:
            return t
        t -= unit
    return dim


def _params(n_grid):
    return pltpu.CompilerParams(dimension_semantics=("arbitrary",) * n_grid, vmem_limit_bytes=VMEM_LIMIT)


def _acc(ref, val, first):
    @pl.when(first)
    def _():
        ref[...] = val

    @pl.when(jnp.logical_not(first))
    def _():
        ref[...] += val


class _Rider:
    def __init__(self, inputs, out_shapes, scratch, start, finish):
        self.inputs, self.out_shapes, self.scratch, self.start, self.finish = inputs, out_shapes, scratch, start, finish


def _ride(body, rider, n_in, n_out, grid):
    if rider is None:
        return body
    ri, ro, rs = len(rider.inputs), len(rider.out_shapes), len(rider.scratch)

    def wrapped(*refs):
        a, b, c = n_in + ri, n_in + ri + n_out, n_in + ri + n_out + ro
        base = refs[:n_in] + refs[a:b] + refs[c:len(refs) - rs]
        r_in, r_out, r_scr = refs[n_in:a], refs[b:c], refs[len(refs) - rs:]
        step = pl.program_id(0)
        for ax in range(1, len(grid)):
            step = step * grid[ax] + pl.program_id(ax)

        @pl.when(step == 0)
        def _():
            rider.start(r_in, r_out, r_scr)

        body(*base)

        @pl.when(step == math.prod(grid) - 1)
        def _():
            rider.finish(r_in, r_out, r_scr)

    return wrapped


def _ride_call(body, rider, *, name, grid, in_specs, out_specs, out_shape, scratch, args, aliases=None):
    n_in, n_out = len(in_specs), len(out_specs)
    aliases = aliases or {}
    if rider is None:
        outs = pl.pallas_call(body, name=name, grid=grid, in_specs=in_specs, out_specs=out_specs, out_shape=out_shape,
                              scratch_shapes=scratch, input_output_aliases=aliases,
                              compiler_params=_params(len(grid)))(*args)
        return list(outs), None
    outs = pl.pallas_call(
        _ride(body, rider, n_in, n_out, grid), name=name, grid=grid,
        in_specs=list(in_specs) + [ANY] * len(rider.inputs), out_specs=list(out_specs) + [ANY] * len(rider.out_shapes),
        out_shape=list(out_shape) + list(rider.out_shapes), scratch_shapes=list(scratch) + list(rider.scratch),
        input_output_aliases=aliases, compiler_params=_params(len(grid)),
    )(*args, *rider.inputs)
    return list(outs[:n_out]), list(outs[n_out:])


def _matmul(a, b, *, name, ta=False, tb=False, tm=1024, tn=2048, tk=None, rider=None, b_parts=None, stack_out=False):
    m, k = (a.shape[1], a.shape[0]) if ta else a.shape
    tk = tk or (1024 if ta else 2048)
    stacked_b = b is not None and b.ndim == 3
    if b_parts:
        n, tn = 2 * b_parts[0].shape[1], _tile(b_parts[0].shape[1], tn)
    elif stacked_b:
        n, tn = b.shape[0] * b.shape[2], b.shape[2]
    else:
        n = b.shape[0] if tb else b.shape[1]
        tn = _tile(n, tn)
    tm, tk = _tile(m, tm), _tile(k, tk)
    nk, nj = k // tk, n // tn

    def body(a_ref, *refs):
        o_ref = refs[-1]
        if b_parts:
            bv = jnp.where(pl.program_id(1) < nj // 2, refs[0][...], refs[1][...])
        else:
            bv = refs[0][...]
        p = _dot(a_ref[...], bv, 0 if ta else 1, 1 if tb else 0)
        if nk == 1:
            o_ref[...] = p
        else:
            _acc(o_ref, p, pl.program_id(2) == 0)

    a_spec = pl.BlockSpec((tk, tm), lambda i, j, kk: (kk, i)) if ta else pl.BlockSpec((tm, tk), lambda i, j, kk: (i, kk))
    if b_parts:
        half = nj // 2
        b_specs = [pl.BlockSpec((tk, tn), lambda i, j, kk: (kk, jnp.minimum(j, half - 1))),
                   pl.BlockSpec((tk, tn), lambda i, j, kk: (kk, jnp.maximum(j - half, 0)))]
        b_args = list(b_parts)
    elif stacked_b:
        b_specs, b_args = [pl.BlockSpec((None, tk, tn), lambda i, j, kk: (j, kk, 0))], [b]
    else:
        b_specs = [pl.BlockSpec((tn, tk), lambda i, j, kk: (j, kk)) if tb else pl.BlockSpec((tk, tn), lambda i, j, kk: (kk, j))]
        b_args = [b]
    if stack_out:
        o_spec, o_sds = pl.BlockSpec((None, tm, tn), lambda i, j, kk: (j, i, 0)), jax.ShapeDtypeStruct((nj, m, tn), F32)
    else:
        o_spec, o_sds = pl.BlockSpec((tm, tn), lambda i, j, kk: (i, j)), jax.ShapeDtypeStruct((m, n), F32)
    outs, extra = _ride_call(body, rider, name=name, grid=(m // tm, nj, nk), in_specs=[a_spec] + b_specs,
                             out_specs=[o_spec], out_shape=[o_sds], scratch=[], args=(a, *b_args))
    return (outs[0], extra) if rider else outs[0]


def _matmul_rows(a_parts, b, *, name, tb, row_ins, vec_ins, epilogue, n_vec_out, tm=512, tk=2048, rider=None):
    m, kp = a_parts[0].shape
    stacked_b = b.ndim == 3
    n = b.shape[1] if stacked_b else (b.shape[0] if tb else b.shape[1])
    tm, tk = _tile(m, tm, 8), (b.shape[2] if stacked_b else _tile(kp, tk))
    nk = kp // tk
    n_a, n_row, n_vec = len(a_parts), len(row_ins), len(vec_ins)

    def body(*refs):
        a_refs, b_refs = refs[:n_a], refs[n_a:2 * n_a]
        rows = refs[2 * n_a:2 * n_a + n_row]
        vecs = refs[2 * n_a + n_row:2 * n_a + n_row + n_vec]
        o_ref = refs[2 * n_a + n_row + n_vec]
        v_refs = refs[2 * n_a + n_row + n_vec + 1:-1]
        acc_ref = refs[-1]
        i, kk = pl.program_id(0), pl.program_id(1)
        p = _dot(a_refs[0][...], b_refs[0][...], 1, 1 if tb else 0)
        for a_ref, b_ref in zip(a_refs[1:], b_refs[1:]):
            p = p + _dot(a_ref[...], b_ref[...], 1, 1 if tb else 0)
        _acc(acc_ref, p, kk == 0)

        @pl.when(kk == nk - 1)
        def _():
            out, vouts = epilogue(acc_ref[...], *[r[...] for r in rows], *[v[...] for v in vecs])
            o_ref[...] = out
            for v_ref, v in zip(v_refs, vouts):
                _acc(v_ref, v, i == 0)

    a_spec = pl.BlockSpec((tm, tk), lambda i, kk: (i, kk))
    if stacked_b:
        b_specs = [pl.BlockSpec((None, n, tk), functools.partial(lambda i, kk, q: (q * nk + kk, 0, 0), q=q))
                   for q in range(n_a)]
    else:
        b_specs = [pl.BlockSpec((n, tk), functools.partial(lambda i, kk, q: (0, q * nk + kk), q=q)) if tb else
                   pl.BlockSpec((tk, n), functools.partial(lambda i, kk, q: (q * nk + kk, 0), q=q)) for q in range(n_a)]
    row = pl.BlockSpec((tm, n), lambda i, kk: (i, 0))
    vec = pl.BlockSpec((1, n), lambda i, kk: (0, 0))
    outs, extra = _ride_call(
        body, rider, name=name, grid=(m // tm, nk),
        in_specs=[a_spec] * n_a + b_specs + [row] * n_row + [vec] * n_vec,
        out_specs=[row] + [vec] * n_vec_out,
        out_shape=[jax.ShapeDtypeStruct((m, n), F32)] + [jax.ShapeDtypeStruct((1, n), F32)] * n_vec_out,
        scratch=[pltpu.VMEM((tm, n), F32)], args=(*a_parts, *([b] * n_a), *row_ins, *vec_ins))
    return outs[0], outs[1:], extra


def _norm_bwd_epilogue(n_dh):
    def epilogue(dh, x, res, *rest):
        for extra in rest[:n_dh]:
            dh = dh + extra
        g = rest[n_dh]
        r = lax.rsqrt(jnp.mean(x * x, axis=-1, keepdims=True) + EPS)
        dhg = dh * g
        dx = res + r * dhg - x * (r * r * r) * jnp.mean(dhg * x, axis=-1, keepdims=True)
        return dx, [jnp.sum(dh * x * r, axis=0, keepdims=True)]

    return epilogue


def _loss_epilogue(y, x1, target):
    d = y.shape[1]
    err = x1 + y - target
    return err * (1.0 / d), [jnp.sum(err * err, axis=0, keepdims=True) * (0.5 / d)]


def _rmsnorm_fwd(x, g, *, name, tm=512):
    n, d = x.shape
    tm = _tile(n, tm, 8)

    def body(x_ref, g_ref, o_ref):
        xv = x_ref[...]
        r = lax.rsqrt(jnp.mean(xv * xv, axis=-1, keepdims=True) + EPS)
        o_ref[...] = (xv * r * g_ref[...]).astype(o_ref.dtype)

    return pl.pallas_call(
        body, name=name, grid=(n // tm,),
        in_specs=[pl.BlockSpec((tm, d), lambda i: (i, 0)), pl.BlockSpec((1, d), lambda i: (0, 0))],
        out_specs=pl.BlockSpec((tm, d), lambda i: (i, 0)),
        out_shape=jax.ShapeDtypeStruct((n, d), MXU_DTYPE),
        compiler_params=_params(1),
    )(x, g)


def _rmsnorm_bwd(x, dhs, g, res, *, name, tm=512):
    n, d = x.shape
    tm = _tile(n, tm, 8)
    n_dh = len(dhs)
    has_res = res is not None

    def body(*refs):
        x_ref, dh_refs, g_ref = refs[0], refs[1:1 + n_dh], refs[1 + n_dh]
        res_ref = refs[2 + n_dh] if has_res else None
        dx_ref, dg_ref = refs[-2], refs[-1]
        xv = x_ref[...]
        dh = dh_refs[0][...].astype(F32)
        for r_ in dh_refs[1:]:
            dh = dh + r_[...].astype(F32)
        r = lax.rsqrt(jnp.mean(xv * xv, axis=-1, keepdims=True) + EPS)
        dhg = dh * g_ref[...]
        dx = r * dhg - xv * (r * r * r) * jnp.mean(dhg * xv, axis=-1, keepdims=True)
        if has_res:
            dx = dx + res_ref[...]
        dx_ref[...] = dx
        _acc(dg_ref, jnp.sum(dh * xv * r, axis=0, keepdims=True), pl.program_id(0) == 0)

    row = pl.BlockSpec((tm, d), lambda i: (i, 0))
    vec = pl.BlockSpec((1, d), lambda i: (0, 0))
    ins = [x] + list(dhs) + [g] + ([res] if has_res else [])
    return pl.pallas_call(
        body, name=name, grid=(n // tm,),
        in_specs=[row] * (1 + n_dh) + [vec] + ([row] if has_res else []),
        out_specs=[row, vec],
        out_shape=[jax.ShapeDtypeStruct((n, d), F32), jax.ShapeDtypeStruct((1, d), F32)],
        compiler_params=_params(1),
    )(*ins)


def _adamw(w, g, m, v, *, name, tr=256):
    _, r, c = w.shape
    c1 = 1.0 / (1.0 - ADAM_B1 ** ADAM_STEP)
    c2 = 1.0 / (1.0 - ADAM_B2 ** ADAM_STEP)

    def body(w_ref, g_ref, m_ref, v_ref, d_ref, mo_ref, vo_ref):
        gv = g_ref[...]
        mn = ADAM_B1 * m_ref[...] + (1.0 - ADAM_B1) * gv
        vn = ADAM_B2 * v_ref[...] + (1.0 - ADAM_B2) * (gv * gv)
        d_ref[...] = -ADAM_LR * ((mn * c1) / (jnp.sqrt(vn * c2) + ADAM_EPS) + ADAM_WD * w_ref[...])
        mo_ref[...] = mn
        vo_ref[...] = vn

    if r % 8 == 0 or r < 8:
        tr = _tile(r, tr, 8)
        grid, blk = (r // tr,), pl.BlockSpec((1, tr, c), lambda i: (0, i, 0))
    else:
        tc = _tile(c, tr)
        grid, blk = (c // tc,), pl.BlockSpec((1, r, tc), lambda i: (0, 0, i))
    sds = jax.ShapeDtypeStruct((1, r, c), F32)
    return pl.pallas_call(
        body, name=name, grid=grid, in_specs=[blk] * 4, out_specs=[blk] * 3, out_shape=[sds] * 3,
        compiler_params=_params(1),
    )(w, g, m, v)


def _bdot(a, b, ca, cb):
    return lax.dot_general(_mx(a), _mx(b), (((ca,), (cb,)), ((0,), (0,))), preferred_element_type=F32)


def _bdotp(a, b, ca, cb):
    return lax.dot_general(a, b, (((ca,), (cb,)), ((0,), (0,))), precision=lax.Precision.HIGHEST,
                           preferred_element_type=F32)


def _tri_dot_b(tri_bf, x):
    hi = x.astype(BF16)
    r = x - hi.astype(F32)
    mid = r.astype(BF16)
    lo = (r - mid.astype(F32)).astype(BF16)

    def d(v):
        return lax.dot_general(tri_bf, v, (((2,), (1,)), ((0,), (0,))), preferred_element_type=F32)

    return d(hi) + d(mid) + d(lo)


def _hgrn_forward(hq, hf, hi, lbv, tril, tril_bf):
    nc, c, _ = hq.shape
    sf = _sig(hf)
    f = lbv + (1.0 - lbv) * sf
    k = 1.0 - f
    gcum = _tri_dot_b(tril_bf, jnp.log(f))
    mid = gcum[:, c // 2 - 1:c // 2, :]
    glast = gcum[:, c - 1:c, :]
    sq = _sig(hq)
    q = hq * sq
    e_q = jnp.exp(gcum - mid)
    e_k = jnp.exp(mid - gcum)
    qe, ke = q * e_q, k * e_k
    a = jnp.where(tril, _bdot(qe, ke, 2, 2), 0.0)
    e_g = jnp.exp(gcum)
    qg = q * e_g
    e_s = jnp.exp(glast - gcum)
    kg = k * e_s
    e_l = jnp.exp(glast)
    upd = _bdot(hi, kg, 1, 1)
    st = jnp.zeros((HG_D, HG_D), F32)
    states = []
    for n in range(nc):
        states.append(st)
        st = st * e_l[n] + upd[n]
    st_all = jnp.stack(states)
    o = _bdot(a, hi, 2, 1) + _bdot(qg, st_all, 2, 2)
    return dict(sf=sf, f=f, k=k, sq=sq, q=q, e_q=e_q, e_k=e_k, qe=qe, ke=ke, a=a, e_g=e_g, qg=qg, o=o,
                e_s=e_s, kg=kg, e_l=e_l, st_all=st_all)


def _hgrn_specs(t_, col0):
    def col(off):
        return pl.BlockSpec((1, t_, LANE), lambda h, b: (b, 0, col0 + 4 * h + off))

    vec = pl.BlockSpec((2, LANE), lambda h, b: (0, h))
    one = pl.BlockSpec((1, LANE), lambda h, b: (0, 0))
    blk = pl.BlockSpec((1, t_, LANE), lambda h, b: (b, 0, h))
    return col, vec, one, blk


def _chunk_masks(nc, c):
    row = lax.broadcasted_iota(jnp.int32, (nc, c, c), 1)
    cl = lax.broadcasted_iota(jnp.int32, (nc, c, c), 2)
    return row >= cl, (row >= cl).astype(BF16), (row <= cl).astype(BF16)


def _hgrn_fwd(zm, lb, gn, hw, col0):
    b_, t_, _ = zm.shape
    c = min(HG_CHUNK, t_)
    nc = t_ // c
    col, vec, one, blk = _hgrn_specs(t_, col0)

    def body(q_ref, f_ref, i_ref, g_ref, lb_ref, gn_ref, y_ref):
        lbv, gnv = _sig(lb_ref[0:1, :] - lb_ref[1:2, :]), gn_ref[...]
        tril, tril_bf, _ = _chunk_masks(nc, c)
        chunks = lambda ref: ref[0].reshape(nc, c, LANE)
        o = _hgrn_forward(chunks(q_ref), chunks(f_ref), chunks(i_ref), lbv, tril, tril_bf)["o"]
        r = lax.rsqrt(jnp.mean(o * o, axis=-1, keepdims=True) + EPS)
        hg = chunks(g_ref)
        y_ref[0] = (o * r * gnv * (hg * _sig(hg))).reshape(t_, LANE)

    return pl.pallas_call(
        body, name="hgrn_fwd", grid=(HG_HEADS, b_),
        in_specs=[col(0), col(1), col(2), col(3), vec, one], out_specs=blk,
        out_shape=jax.ShapeDtypeStruct((b_, t_, hw), F32),
        compiler_params=_params(2),
    )(zm, zm, zm, zm, lb, gn)


def _hgrn_bwd(zm, dy, lb, gn, hw, col0, dz, rider=None):
    b_, t_, _ = zm.shape
    c = min(HG_CHUNK, t_)
    nc = t_ // c
    col, vec, one, blk = _hgrn_specs(t_, col0)

    def body(q_ref, f_ref, i_ref, g_ref, dy_ref, lb_ref, gn_ref, _, dz_ref, dlb_ref, dgn_ref):
        h, b = pl.program_id(0), pl.program_id(1)
        lbv, gnv = _sig(lb_ref[0:1, :] - lb_ref[1:2, :]), gn_ref[...]
        tril, tril_bf, triu_bf = _chunk_masks(nc, c)
        last_row = lax.broadcasted_iota(jnp.int32, (nc, c, LANE), 1) == c - 1
        chunks = lambda ref: ref[0].reshape(nc, c, LANE)
        flat = lambda x: x.reshape(t_, LANE)
        hq, hi, hg = chunks(q_ref), chunks(i_ref), chunks(g_ref)
        p = _hgrn_forward(hq, chunks(f_ref), hi, lbv, tril, tril_bf)
        o, q, k, st_all, e_l = p["o"], p["q"], p["k"], p["st_all"], p["e_l"]
        dyv = chunks(dy_ref)
        sg = _sig(hg)
        r = lax.rsqrt(jnp.mean(o * o, axis=-1, keepdims=True) + EPS)
        dn = dyv * (hg * sg)
        dz_ref[0, :, 3 * LANE:] = flat(dyv * (o * r * gnv) * (sg * (1.0 + hg * (1.0 - sg)))).astype(dz_ref.dtype)
        dgn = jnp.sum(flat(dn * o * r), axis=0, keepdims=True)
        dng = dn * gnv
        do = r * dng - o * (r * r * r) * jnp.mean(dng * o, axis=-1, keepdims=True)
        back = _bdotp(do, p["qg"], 1, 1)
        dst = jnp.zeros((HG_D, HG_D), F32)
        dsts = [None] * nc
        for n in range(nc - 1, -1, -1):
            dsts[n] = dst
            dst = dst * e_l[n] + back[n]
        dst_all = jnp.stack(dsts)
        da = jnp.where(tril, _bdotp(do, hi, 2, 2), 0.0)
        dq = _bdotp(da, p["ke"], 2, 1) * p["e_q"] + _bdotp(do, st_all, 2, 1) * p["e_g"]
        dk_state = _bdotp(hi, dst_all, 2, 1) * p["e_s"]
        dk = _bdotp(da, p["qe"], 1, 1) * p["e_k"] + dk_state
        dz_ref[0, :, 2 * LANE:3 * LANE] = flat(_bdot(p["a"], do, 1, 1) + _bdot(p["kg"], dst_all, 2, 2)).astype(dz_ref.dtype)
        extra = (jnp.sum(k * dk_state, axis=1, keepdims=True) + e_l * jnp.sum(st_all * dst_all, axis=1, keepdims=True))
        dgc = q * dq - k * dk + jnp.where(last_row, extra, 0.0)
        dfv = _tri_dot_b(triu_bf, dgc) / p["f"] - dk
        sf, sq = p["sf"], p["sq"]
        dz_ref[0, :, LANE:2 * LANE] = flat(dfv * (1.0 - lbv) * sf * (1.0 - sf)).astype(dz_ref.dtype)
        dlb = jnp.sum(flat(dfv * (1.0 - sf)), axis=0, keepdims=True)
        dz_ref[0, :, :LANE] = flat(dq * (sq * (1.0 + hq * (1.0 - sq)))).astype(dz_ref.dtype)
        dl0 = dlb * lbv * (1.0 - lbv)
        _acc(dlb_ref, jnp.concatenate([dl0, -dl0], axis=0), b == 0)
        _acc(dgn_ref, dgn, jnp.logical_and(b == 0, h == 0))

    return _ride_call(
        body, rider, name="hgrn_bwd", grid=(HG_HEADS, b_),
        in_specs=[col(0), col(1), col(2), col(3), blk, vec, one, ANY],
        out_specs=[pl.BlockSpec((1, t_, 4 * LANE), lambda h, b: (b, 0, col0 // 4 + h)), vec, one],
        out_shape=[jax.ShapeDtypeStruct(dz.shape, dz.dtype), jax.ShapeDtypeStruct((2, hw), F32),
                   jax.ShapeDtypeStruct((1, LANE), F32)],
        scratch=[], args=(zm, zm, zm, zm, dy, lb, gn, dz), aliases={7: 0})


def _fox_logf(x):
    return jnp.minimum(x, 0.0) - jnp.log(1.0 + jnp.exp(-jnp.abs(x)))


def _fox_prep(zf, bias):
    b_, t_, _ = zf.shape
    tb = min(FOX_BLOCK, t_)
    nb = t_ // tb

    def body(z_ref, b_ref, fc_ref):
        tril_bf = (lax.broadcasted_iota(jnp.int32, (tb, tb), 0) >= lax.broadcasted_iota(jnp.int32, (tb, tb), 1)).astype(BF16)
        bv = b_ref[...]

        def blk(i, carry):
            rows = pl.ds(pl.multiple_of(i * tb, tb), tb)
            fc = _tri_dot(tril_bf, _fox_logf(z_ref[0, rows, :] + bv)) + carry
            fc_ref[0, rows, :] = fc
            return fc[tb - 1:tb, :]

        lax.fori_loop(0, nb, blk, jnp.zeros((1, LANE), F32))

    blk_spec = pl.BlockSpec((1, t_, LANE), lambda b: (b, 0, 0))
    return pl.pallas_call(
        body, name="fox_prep", grid=(b_,),
        in_specs=[blk_spec, pl.BlockSpec((1, LANE), lambda b: (0, 0))], out_specs=blk_spec,
        out_shape=jax.ShapeDtypeStruct((b_, t_, LANE), F32), compiler_params=_params(1),
    )(zf, bias)


def _fox_post(dfc, zf, bias):
    b_, t_, _ = zf.shape
    npair = dfc.shape[1]
    tb = min(FOX_BLOCK, t_)
    nb = t_ // tb

    def body(d_ref, z_ref, b_ref, dz_ref, db_ref):
        triu_bf = (lax.broadcasted_iota(jnp.int32, (tb, tb), 0) <= lax.broadcasted_iota(jnp.int32, (tb, tb), 1)).astype(BF16)
        valid = lax.broadcasted_iota(jnp.int32, (tb, LANE), 1) < FOX_HEADS
        bv = b_ref[...]

        def blk(m, carry):
            tail, db = carry
            rows = pl.ds(pl.multiple_of((nb - 1 - m) * tb, tb), tb)
            dfc_rows = d_ref[0, 0, rows, :]
            for p in range(1, npair):
                dfc_rows = dfc_rows + pltpu.roll(d_ref[0, p, rows, :], 2 * p, 1)
            dlf = _tri_dot(triu_bf, dfc_rows) + tail
            dx = jnp.where(valid, dlf * _sig(-(z_ref[0, rows, :] + bv)), 0.0)
            dz_ref[0, rows, :] = dx.astype(dz_ref.dtype)
            return dlf[0:1, :], db + jnp.sum(dx, axis=0, keepdims=True)

        z1 = jnp.zeros((1, LANE), F32)
        _, db = lax.fori_loop(0, nb, blk, (z1, z1))
        _acc(db_ref, db, pl.program_id(0) == 0)

    blk_spec = pl.BlockSpec((1, t_, LANE), lambda b: (b, 0, 0))
    vec = pl.BlockSpec((1, LANE), lambda b: (0, 0))
    return pl.pallas_call(
        body, name="fox_post", grid=(b_,),
        in_specs=[pl.BlockSpec((1, npair, t_, LANE), lambda b: (b, 0, 0, 0)), blk_spec, vec], out_specs=[blk_spec, vec],
        out_shape=[jax.ShapeDtypeStruct((b_, t_, LANE), MXU_DTYPE), jax.ShapeDtypeStruct((1, LANE), F32)],
        compiler_params=_params(1),
    )(dfc, zf, bias)


FOX_TILE = 256
FOX_BAND = 512
AUG = 64


def _head_mean_matrix():
    r = lax.broadcasted_iota(jnp.int32, (LANE, LANE), 0) // FOX_DH
    c = lax.broadcasted_iota(jnp.int32, (LANE, LANE), 1) // FOX_DH
    return (r == c).astype(BF16)


def _dot_right_exact(x, m_bf):
    hi = x.astype(BF16)
    r = x - hi.astype(F32)
    mid = r.astype(BF16)
    lo = (r - mid.astype(F32)).astype(BF16)

    def d(v):
        return lax.dot_general(v, m_bf, (((1,), (0,)), ((), ())), preferred_element_type=F32)

    return d(hi) + d(mid) + d(lo)


def _pair_norm(x, g2, bd):
    r = lax.rsqrt(_dot_right_exact(x * x, bd) * (1.0 / FOX_DH) + EPS)
    return x * r * g2, r


def _pair_norm_bwd(x, r, dy, g2, bd):
    dyg = dy * g2
    dx = r * dyg - x * (r * r * r) * (_dot_right_exact(dyg * x, bd) * (1.0 / FOX_DH))
    return dx, jnp.sum(dy * x * r, axis=0, keepdims=True)


def _head_lanes(xn, hh):
    return xn if hh == 0 else pltpu.roll(xn, FOX_DH, 1)


def _split3(x):
    hi = x.astype(BF16).astype(F32)
    mid = (x - hi).astype(BF16).astype(F32)
    return hi, mid, x - hi - mid


def _fox_operands(q_ref, k_ref, v_ref, fc_ref, gq2, gk2, p, qa, ka, va):
    t_ = q_ref.shape[1]
    bd = _head_mean_matrix()
    lane = lax.broadcasted_iota(jnp.int32, (t_, LANE), 1)
    qx, kx = q_ref[0], k_ref[0]
    qn, rq = _pair_norm(qx, gq2, bd)
    kn, rk = _pair_norm(kx, gk2, bd)
    vv = v_ref[0]
    q_aug = jnp.where(jnp.logical_and(lane >= AUG, lane < AUG + 3), 1.0, 0.0)
    for hh in range(2):
        fcol = jnp.sum(jnp.where(lane == 2 * p + hh, fc_ref[0], 0.0), axis=-1, keepdims=True)
        hi, mid, lo = _split3(-fcol)
        k_aug = jnp.where(lane == AUG, hi, jnp.where(lane == AUG + 1, mid, jnp.where(lane == AUG + 2, lo,
                          jnp.where(lane == AUG + 3, 1.0, 0.0))))
        head = lane < FOX_DH
        qa[hh] = jnp.where(head, _head_lanes(qn, hh), q_aug).astype(MXU_DTYPE)
        ka[hh] = jnp.where(head, _head_lanes(kn, hh), k_aug).astype(MXU_DTYPE)
        va[hh] = jnp.where(head, _head_lanes(vv, hh), 0.0).astype(MXU_DTYPE)
    return bd, lane, qx, kx, rq, rk


def _fox_specs(t_, fw, col0):
    npair = fw // LANE

    def col(off):
        return pl.BlockSpec((1, t_, LANE), lambda b, p: (b, 0, col0 + 3 * p + off))

    pair = pl.BlockSpec((1, t_, LANE), lambda b, p: (b, 0, p))
    full = pl.BlockSpec((1, t_, LANE), lambda b, p: (b, 0, 0))
    gvec = pl.BlockSpec((1, LANE), lambda b, p: (0, 0))
    lse = pl.BlockSpec((1, 1, t_, LANE), lambda b, p: (b, p, 0, 0))
    return col, pair, full, gvec, lse


def _fox_fwd(zm, fc, gq2, gk2, fw, col0, rider=None):
    b_, t_, _ = zm.shape
    npair = fw // LANE
    tq = min(FOX_TILE, t_)
    bw = min(FOX_BAND, t_)
    nband, tpb = t_ // bw, bw // tq
    scale = FOX_DH ** -0.5
    col, pair, full, gvec, lse_spec = _fox_specs(t_, fw, col0)

    def body(q_ref, k_ref, v_ref, fc_ref, gq_ref, gk_ref, o_ref, lse_ref, qa, ka, va):
        p = pl.program_id(1)
        _fox_operands(q_ref, k_ref, v_ref, fc_ref, gq_ref[...] * scale, gk_ref[...], p, qa, ka, va)
        ahead = lax.broadcasted_iota(jnp.int32, (tq, bw), 1) - lax.broadcasted_iota(jnp.int32, (tq, bw), 0)
        lane = lax.broadcasted_iota(jnp.int32, (tq, LANE), 1)

        for band in range(nband):
            c0 = band * bw

            def qtile(ii, _, c0=c0):
                r0 = pl.multiple_of(c0 + ii * tq, tq)
                rows = pl.ds(r0, tq)
                keep = ahead <= r0 - c0
                res = []
                for hh in range(2):
                    qb = qa[hh, rows, :]
                    s_b = jnp.where(keep, _nt(qb, ka[hh, c0:c0 + bw, :]), NEG)
                    m = jnp.max(s_b, axis=-1, keepdims=True)
                    if c0:
                        s_a = _nt(qb, ka[hh, 0:c0, :])
                        m = jnp.maximum(m, jnp.max(s_a, axis=-1, keepdims=True))
                    p_b = jnp.exp(s_b - m)
                    l = jnp.sum(p_b, axis=-1, keepdims=True)
                    acc = _nn(p_b, va[hh, c0:c0 + bw, :])
                    if c0:
                        p_a = jnp.exp(s_a - m)
                        l = l + jnp.sum(p_a, axis=-1, keepdims=True)
                        acc = acc + _nn(p_a, va[hh, 0:c0, :])
                    res.append((acc / l, m + jnp.log(l)))
                (o0, e0), (o1, e1) = res
                o_ref[0, rows, :] = jnp.where(lane < FOX_DH, o0, pltpu.roll(o1, FOX_DH, 1))
                lse_ref[0, 0, rows, :] = jnp.where(lane == 0, e0, jnp.where(lane == 1, e1, 0.0))
                return 0

            lax.fori_loop(0, tpb, qtile, 0)

    return _ride_call(
        body, rider, name="fox_fwd", grid=(b_, npair),
        in_specs=[col(0), col(1), col(2), full, gvec, gvec],
        out_specs=[pair, lse_spec],
        out_shape=[jax.ShapeDtypeStruct((b_, t_, fw), F32), jax.ShapeDtypeStruct((b_, npair, t_, LANE), F32)],
        scratch=[pltpu.VMEM((2, t_, LANE), MXU_DTYPE)] * 3, args=(zm, zm, zm, fc, gq2, gk2))


def _norm_bwd(x, dy, g):
    r = lax.rsqrt(jnp.mean(x * x, axis=-1, keepdims=True) + EPS)
    dyg = dy * g
    dx = r * dyg - x * (r * r * r) * jnp.mean(dyg * x, axis=-1, keepdims=True)
    return dx, jnp.sum(dy * x * r, axis=0, keepdims=True)


def _fox_bwd(zm, o, do, lse, fc, gq2, gk2, fw, col0, dz, rider=None):
    b_, t_, _ = zm.shape
    npair = fw // LANE
    tq = min(FOX_TILE, t_)
    nb = t_ // tq
    bw = min(FOX_BAND, t_)
    nband, tpb = t_ // bw, bw // tq
    scale = FOX_DH ** -0.5
    col, pair, full, gvec, lse_spec = _fox_specs(t_, fw, col0)

    def body(q_ref, k_ref, v_ref, o_ref, do_ref, lse_ref, fc_ref, gq_ref, gk_ref, _,
             dz_ref, dfc_ref, dgq_ref, dgk_ref, qa, ka, va, da, rowv, dq_acc, dk_acc, dv_acc):
        b, p = pl.program_id(0), pl.program_id(1)
        gq2v, gk2v = gq_ref[...] * scale, gk_ref[...]
        bd, lane, qx, kx, rq, rk = _fox_operands(q_ref, k_ref, v_ref, fc_ref, gq2v, gk2v, p, qa, ka, va)
        head = lane < FOX_DH
        dov = do_ref[0]
        dsum = _dot_right_exact(dov * o_ref[0], bd)
        eye = (lax.broadcasted_iota(jnp.int32, (tq, tq), 0) == lax.broadcasted_iota(jnp.int32, (tq, tq), 1)).astype(F32)
        for hh in range(2):
            da[hh] = jnp.where(head, _head_lanes(dov, hh), 0.0).astype(MXU_DTYPE)
            for blk in range(nb):
                rs = slice(blk * tq, (blk + 1) * tq)
                rowv[2 * hh:2 * hh + 1, rs] = jnp.sum(eye * lse_ref[0, 0, rs, hh:hh + 1], axis=0, keepdims=True)
                rowv[2 * hh + 1:2 * hh + 2, rs] = jnp.sum(eye * dsum[rs, hh * FOX_DH:hh * FOX_DH + 1], axis=0, keepdims=True)
        dq_acc[...] = jnp.zeros(dq_acc.shape, F32)
        ahead = lax.broadcasted_iota(jnp.int32, (tq, bw), 1) - lax.broadcasted_iota(jnp.int32, (tq, bw), 0)

        def part(hh, kb, vb, lo, hi, keep):
            qm, dm = qa[hh, lo:hi, :], da[hh, lo:hi, :]
            pt = jnp.exp(_nt(kb, qm) - rowv[2 * hh:2 * hh + 1, lo:hi])
            if keep is not None:
                pt = jnp.where(keep, pt, 0.0)
            dst = pt * (_nt(vb, dm) - rowv[2 * hh + 1:2 * hh + 2, lo:hi])
            dq_acc[hh, lo:hi, :] += _tn(dst, kb)
            return _nn(dst, qm), _nn(pt, dm)

        for band in range(nband):
            c0 = band * bw

            def kvtile(jj, _, c0=c0):
                r0 = pl.multiple_of(c0 + jj * tq, tq)
                rows = pl.ds(r0, tq)
                keep = ahead >= r0 - c0
                for hh in range(2):
                    kb, vb = ka[hh, rows, :], va[hh, rows, :]
                    dk_t, dv_t = part(hh, kb, vb, c0, c0 + bw, keep)
                    if c0 + bw < t_:
                        dk_u, dv_u = part(hh, kb, vb, c0 + bw, t_, None)
                        dk_t, dv_t = dk_t + dk_u, dv_t + dv_u
                    dk_acc[hh, rows, :] = dk_t
                    dv_acc[hh, rows, :] = dv_t
                return 0

            lax.fori_loop(0, tpb, kvtile, 0)

        dq0, dq1, dk0, dk1 = dq_acc[0], dq_acc[1], dk_acc[0], dk_acc[1]
        dqn = jnp.where(head, dq0, pltpu.roll(dq1, FOX_DH, 1))
        dkn = jnp.where(head, dk0, pltpu.roll(dk1, FOX_DH, 1))
        dqx, gq_part = _pair_norm_bwd(qx, rq, dqn, gq2v, bd)
        dkx, gk_part = _pair_norm_bwd(kx, rk, dkn, gk2v, bd)
        dz_ref[0, :, :LANE] = dqx.astype(dz_ref.dtype)
        dz_ref[0, :, LANE:2 * LANE] = dkx.astype(dz_ref.dtype)
        dz_ref[0, :, 2 * LANE:] = jnp.where(head, dv_acc[0], pltpu.roll(dv_acc[1], FOX_DH, 1)).astype(dz_ref.dtype)

        def bias_grad(dqh, dkh):
            return (jnp.sum(jnp.where(lane == AUG + 3, dqh, 0.0), axis=-1, keepdims=True)
                    - jnp.sum(jnp.where(lane == AUG, dkh, 0.0), axis=-1, keepdims=True))

        dfc_ref[0, 0] = jnp.where(lane == 0, bias_grad(dq0, dk0), jnp.where(lane == 1, bias_grad(dq1, dk1), 0.0))
        first = jnp.logical_and(b == 0, p == 0)
        _acc(dgq_ref, gq_part * scale, first)
        _acc(dgk_ref, gk_part, first)

    gs = jax.ShapeDtypeStruct((1, LANE), F32)
    return _ride_call(
        body, rider, name="fox_bwd", grid=(b_, npair),
        in_specs=[col(0), col(1), col(2), pair, pair, lse_spec, full, gvec, gvec, ANY],
        out_specs=[pl.BlockSpec((1, t_, 3 * LANE), lambda b, p: (b, 0, col0 // 3 + p)), lse_spec, gvec, gvec],
        out_shape=[jax.ShapeDtypeStruct(dz.shape, dz.dtype), jax.ShapeDtypeStruct((b_, npair, t_, LANE), F32), gs, gs],
        scratch=[pltpu.VMEM((2, t_, LANE), MXU_DTYPE)] * 4
        + [pltpu.VMEM((8, t_), F32)] + [pltpu.VMEM((2, t_, LANE), F32)] * 3,
        args=(zm, zm, zm, o, do, lse, fc, gq2, gk2, dz), aliases={9: 0})


def _mem_specs(t_, m_, mw, col0):
    nh = mw // LANE
    qcol = pl.BlockSpec((1, t_, LANE), lambda b, h: (b, 0, col0 + h))
    kcol = pl.BlockSpec((1, m_, LANE), lambda b, h: (b, 0, h))
    vcol = pl.BlockSpec((1, m_, LANE), lambda b, h: (b, 0, nh + h))
    ycol = pl.BlockSpec((1, t_, LANE), lambda b, h: (b, 0, h))
    gvec = pl.BlockSpec((1, LANE), lambda b, h: (0, 0))
    return qcol, kcol, vcol, ycol, gvec


def _mem_fwd(zm, mkv, gq, gk, mw, col0):
    b_, t_, _ = zm.shape
    m_ = mkv.shape[1]
    tq = min(512, t_)
    nb = t_ // tq
    scale = MEM_DH ** -0.5
    qcol, kcol, vcol, ycol, gvec = _mem_specs(t_, m_, mw, col0)

    def body(q_ref, k_ref, v_ref, gq_ref, gk_ref, y_ref):
        gqv, gkv = gq_ref[...] * scale, gk_ref[...]
        kv = k_ref[0]
        kn = _mx(kv * lax.rsqrt(jnp.mean(kv * kv, axis=-1, keepdims=True) + EPS) * gkv)
        vv = _mx(v_ref[0])

        def blk(i, _):
            rows = pl.ds(pl.multiple_of(i * tq, tq), tq)
            qv = q_ref[0, rows, :]
            s = _nt(qv * lax.rsqrt(jnp.mean(qv * qv, axis=-1, keepdims=True) + EPS) * gqv, kn)
            e = jnp.exp(s - jnp.max(s, axis=-1, keepdims=True))
            y_ref[0, rows, :] = _nn(e / jnp.sum(e, axis=-1, keepdims=True), vv)
            return 0

        lax.fori_loop(0, nb, blk, 0)

    return pl.pallas_call(
        body, name="mem_fwd", grid=(b_, MEM_HEADS), in_specs=[qcol, kcol, vcol, gvec, gvec], out_specs=ycol,
        out_shape=jax.ShapeDtypeStruct((b_, t_, mw), F32), compiler_params=_params(2),
    )(zm, mkv, mkv, gq, gk)


def _mem_bwd(zm, mkv, dy, gq, gk, mw, col0, dz):
    b_, t_, _ = zm.shape
    m_ = mkv.shape[1]
    tq = min(512, t_)
    nb = t_ // tq
    scale = MEM_DH ** -0.5
    qcol, kcol, vcol, ycol, gvec = _mem_specs(t_, m_, mw, col0)

    def body(q_ref, k_ref, v_ref, dy_ref, gq_ref, gk_ref, _, dq_ref, dk_ref, dv_ref, dgq_ref, dgk_ref):
        gqv, gkv = gq_ref[...] * scale, gk_ref[...]
        kv = k_ref[0]
        kn = _mx(kv * lax.rsqrt(jnp.mean(kv * kv, axis=-1, keepdims=True) + EPS) * gkv)
        vv = _mx(v_ref[0])

        def blk(i, carry):
            dkn, dvv, dgq = carry
            rows = pl.ds(pl.multiple_of(i * tq, tq), tq)
            qv = q_ref[0, rows, :]
            qn = _mx(qv * lax.rsqrt(jnp.mean(qv * qv, axis=-1, keepdims=True) + EPS) * gqv)
            s = _nt(qn, kn)
            e = jnp.exp(s - jnp.max(s, axis=-1, keepdims=True))
            pm = e / jnp.sum(e, axis=-1, keepdims=True)
            dob = _mx(dy_ref[0, rows, :])
            dp = _nt(dob, vv)
            ds = pm * (dp - jnp.sum(dp * pm, axis=-1, keepdims=True))
            dqv, gq_part = _norm_bwd(qv, _nn(ds, kn), gqv)
            dq_ref[0, rows, :] = dqv.astype(dq_ref.dtype)
            return dkn + _tn(ds, qn), dvv + _tn(pm, dob), dgq + gq_part * scale

        z = jnp.zeros((m_, LANE), F32)
        dkn, dvv, dgq = lax.fori_loop(0, nb, blk, (z, z, jnp.zeros((1, LANE), F32)))
        dkv, dgk = _norm_bwd(kv, dkn, gkv)
        dk_ref[0] = dkv
        dv_ref[0] = dvv
        first = jnp.logical_and(pl.program_id(0) == 0, pl.program_id(1) == 0)
        _acc(dgq_ref, dgq, first)
        _acc(dgk_ref, dgk, first)

    kblk = pl.BlockSpec((1, m_, LANE), lambda b, h: (b, 0, h))
    gs = jax.ShapeDtypeStruct((1, LANE), F32)
    ks = jax.ShapeDtypeStruct((b_, m_, mw), F32)
    return pl.pallas_call(
        body, name="mem_bwd", grid=(b_, MEM_HEADS), in_specs=[qcol, kcol, vcol, ycol, gvec, gvec, ANY],
        out_specs=[qcol, kblk, kblk, gvec, gvec],
        out_shape=[jax.ShapeDtypeStruct(dz.shape, dz.dtype), ks, ks, gs, gs], input_output_aliases={6: 0},
        compiler_params=_params(2),
    )(zm, mkv, mkv, dy, gq, gk, dz)


def _merge_specs(tm, d, w, gcol):
    row_d = pl.BlockSpec((tm, d), lambda i: (i, 0))
    row_w = pl.BlockSpec((tm, w), lambda i: (i, 0))
    gates = [pl.BlockSpec((tm, d), functools.partial(lambda i, k: (i, gcol + k), k=k)) for k in range(3)]
    w_br = pl.BlockSpec((w, d), lambda i: (0, 0))
    w_o = pl.BlockSpec((d, d), lambda i: (0, 0))
    return row_d, row_w, gates, w_br, w_o


def _merge_fwd(x, ys, zm, w_brs, w_out, gcol, g_next, tm=256):
    n, d = x.shape
    w = ys[0].shape[1]
    tm = _tile(n, tm, 8)
    row_d, row_w, gates, w_br, w_o = _merge_specs(tm, d, w, gcol)

    def body(x_ref, ya, yb, yc, g0, g1, g2, wa, wb, wc, wo, gn_ref, x1_ref, mg_ref, h_ref):
        mg = (_sig(g0[...]) * _nn(ya[...], wa[...]) + _sig(g1[...]) * _nn(yb[...], wb[...])
              + _sig(g2[...]) * _nn(yc[...], wc[...]))
        mg_ref[...] = mg.astype(mg_ref.dtype)
        x1 = x_ref[...] + _nn(mg, wo[...])
        x1_ref[...] = x1
        h_ref[...] = (x1 * lax.rsqrt(jnp.mean(x1 * x1, axis=-1, keepdims=True) + EPS) * gn_ref[...]).astype(h_ref.dtype)

    half = jax.ShapeDtypeStruct((n, d), MXU_DTYPE)
    return pl.pallas_call(
        body, name="merge_fwd", grid=(n // tm,),
        in_specs=[row_d, row_w, row_w, row_w] + gates + [w_br, w_br, w_br, w_o, pl.BlockSpec((1, d), lambda i: (0, 0))],
        out_specs=[row_d, row_d, row_d],
        out_shape=[jax.ShapeDtypeStruct((n, d), F32), half, half],
        compiler_params=_params(1),
    )(x, *ys, zm, zm, zm, *w_brs, w_out, g_next)


def _merge_bwd(dx1, ys, zm, w_brs, w_out, gcol, tm=256):
    n, d = dx1.shape
    w = ys[0].shape[1]
    tm = _tile(n, tm, 8)
    row_d, row_w, gates, w_br, w_o = _merge_specs(tm, d, w, gcol)

    def body(dx_ref, ya, yb, yc, g0, g1, g2, wa, wb, wc, wo, dgl_ref, dpa, dpb, dpc, dya, dyb, dyc):
        dm = _nt(dx_ref[...], wo[...])
        for k, (y, g, wr, dp_ref, dy_ref) in enumerate(((ya, g0, wa, dpa, dya), (yb, g1, wb, dpb, dyb),
                                                        (yc, g2, wc, dpc, dyc))):
            sg = _sig(g[...])
            pr = _nn(y[...], wr[...])
            dgl_ref[:, k * d:(k + 1) * d] = (dm * pr * sg * (1.0 - sg)).astype(dgl_ref.dtype)
            dp = (dm * sg).astype(dp_ref.dtype)
            dp_ref[...] = dp
            dy_ref[...] = _nt(dp, wr[...])

    sd = jax.ShapeDtypeStruct((n, d), MXU_DTYPE)
    sw = jax.ShapeDtypeStruct((n, w), F32)
    return pl.pallas_call(
        body, name="merge_bwd", grid=(n // tm,),
        in_specs=[row_d, row_w, row_w, row_w] + gates + [w_br, w_br, w_br, w_o],
        out_specs=[pl.BlockSpec((tm, 3 * d), lambda i: (i, 0)), row_d, row_d, row_d, row_w, row_w, row_w],
        out_shape=[jax.ShapeDtypeStruct((n, zm.shape[1]), MXU_DTYPE), sd, sd, sd, sw, sw, sw],
        compiler_params=_params(1),
    )(dx1, *ys, zm, zm, zm, *w_brs, w_out)


CONV_ROWS = 256
HALO = 8


def _ext(ref, r0, t_):
    rc = min(CONV_ROWS, t_)
    a, b = max(r0 - HALO, 0), min(r0 + rc + HALO, t_)
    parts = []
    if r0 - HALO < 0:
        parts.append(jnp.zeros((HALO, ref.shape[2]), F32))
    parts.append(ref[0, a:b, :].astype(F32))
    if r0 + rc + HALO > t_:
        parts.append(jnp.zeros((HALO, ref.shape[2]), F32))
    return jnp.concatenate(parts, axis=0) if len(parts) > 1 else parts[0]


def _gelu_parts(ac):
    e = jnp.exp(-0.5 * ac * ac)
    t = 1.0 / (1.0 + (0.3275911 * 2.0 ** -0.5) * jnp.abs(ac))
    tail = (0.5 * e) * (t * (0.254829592 + t * (-0.284496736 + t * (1.421413741 + t * (-1.453152027 + t * 1.061405429)))))
    return jnp.where(ac < 0, tail, 1.0 - tail), e * ((2.0 * math.pi) ** -0.5)


def _conv_taps(a_ext, cw, cb):
    a2, a1 = pltpu.roll(a_ext, 2, 0), pltpu.roll(a_ext, 1, 0)
    return cw[0:1, :] * a2 + cw[1:2, :] * a1 + cw[2:3, :] * a_ext + cb, a2, a1


def _glu_specs(t_, f, g):
    gate = pl.BlockSpec((1, t_, g), lambda j, b: (b, 0, j))
    value = pl.BlockSpec((1, t_, g), lambda j, b: (b, 0, f // g + j))
    cwb = pl.BlockSpec((3, g), lambda j, b: (0, j))
    cbb = pl.BlockSpec((1, g), lambda j, b: (0, j))
    return gate, value, cwb, cbb


def _glu_fwd(u, cw, cb):
    b_, t_, f2 = u.shape
    f = f2 // 2
    g = min(FFN_GROUP, f)
    rc = min(CONV_ROWS, t_)
    gate, value, cwb, cbb = _glu_specs(t_, f, g)

    def body(a_ref, v_ref, cw_ref, cb_ref, y_ref):
        cwv, cbv = cw_ref[...], cb_ref[...]
        for r0 in range(0, t_, rc):
            ac = _conv_taps(_ext(a_ref, r0, t_), cwv, cbv)[0][HALO:HALO + rc]
            cdf, _ = _gelu_parts(ac)
            y_ref[0, r0:r0 + rc, :] = (ac * cdf * v_ref[0, r0:r0 + rc, :]).astype(y_ref.dtype)

    return pl.pallas_call(
        body, name="glu_fwd", grid=(f // g, b_), in_specs=[gate, value, cwb, cbb], out_specs=gate,
        out_shape=jax.ShapeDtypeStruct((b_, t_, f), MXU_DTYPE), compiler_params=_params(2),
    )(u, u, cw, cb)


def _glu_bwd(u, dy, cw, cb):
    b_, t_, f2 = u.shape
    f = f2 // 2
    g = min(FFN_GROUP, f)
    rc = min(CONV_ROWS, t_)
    ne = rc + 2 * HALO
    gate, value, cwb, cbb = _glu_specs(t_, f, g)

    def body(a_ref, v_ref, dy_ref, cw_ref, cb_ref, da_ref, dv_ref, dcw_ref, dcb_ref):
        cwv, cbv = cw_ref[...], cb_ref[...]
        dcw = [jnp.zeros((1, g), F32) for _ in range(3)]
        dcb = jnp.zeros((1, g), F32)
        for r0 in range(0, t_, rc):
            a_ext, v_ext, dy_ext = _ext(a_ref, r0, t_), _ext(v_ref, r0, t_), _ext(dy_ref, r0, t_)
            ac, a2, a1 = _conv_taps(a_ext, cwv, cbv)
            cdf, pdf = _gelu_parts(ac)
            dac = dy_ext * v_ext * (cdf + ac * pdf)
            da = cwv[2:3, :] * dac + cwv[1:2, :] * pltpu.roll(dac, ne - 1, 0) + cwv[0:1, :] * pltpu.roll(dac, ne - 2, 0)
            mid = slice(HALO, HALO + rc)
            da_ref[0, r0:r0 + rc, :] = da[mid].astype(da_ref.dtype)
            dv_ref[0, r0:r0 + rc, :] = (dy_ext[mid] * ac[mid] * cdf[mid]).astype(dv_ref.dtype)
            dacm = dac[mid]
            dcw[0] = dcw[0] + jnp.sum(dacm * a2[mid], axis=0, keepdims=True)
            dcw[1] = dcw[1] + jnp.sum(dacm * a1[mid], axis=0, keepdims=True)
            dcw[2] = dcw[2] + jnp.sum(dacm * a_ext[mid], axis=0, keepdims=True)
            dcb = dcb + jnp.sum(dacm, axis=0, keepdims=True)
        first = pl.program_id(1) == 0
        _acc(dcw_ref, jnp.concatenate(dcw, axis=0), first)
        _acc(dcb_ref, dcb, first)

    sds = jax.ShapeDtypeStruct((b_, t_, f), MXU_DTYPE)
    return pl.pallas_call(
        body, name="glu_bwd", grid=(f // g, b_), in_specs=[gate, value, gate, cwb, cbb],
        out_specs=[gate, gate, cwb, cbb],
        out_shape=[sds, sds, jax.ShapeDtypeStruct((3, f), F32), jax.ShapeDtypeStruct((1, f), F32)],
        compiler_params=_params(2),
    )(u, u, dy, cw, cb)


def _place():
    x, y, c = lax.axis_index("x"), lax.axis_index("y"), lax.axis_index("c")
    chips = [(1 - x, y), (x, 1 - y), (1 - x, 1 - y)]
    return x, y, c, chips


def _remote(src, dst, send_sem, recv_sem, to):
    return pltpu.make_async_remote_copy(src_ref=src, dst_ref=dst, send_sem=send_sem, recv_sem=recv_sem,
                                        device_id=to, device_id_type=MESH)


STACK, COLS = "stack", "cols"


def _shard_ref(ref, kind, s, rows, c):
    if kind == COLS:
        cols = pl.ds(pl.multiple_of(s * c, LANE), c)
        return ref.at[:, cols] if rows is None else ref.at[rows, cols]
    return ref.at[s] if rows is None else ref.at[s, rows, :]


def _halves(c, half):
    mine = pl.ds(pl.multiple_of(c * half, 16), half)
    theirs = pl.ds(pl.multiple_of((1 - c) * half, 16), half)
    return mine, theirs


def _gather_parts(kinds):
    def first_copies(ins, outs, sems):
        x, y, c, chips = _place()
        me = 2 * x + y
        cps = []
        for i, (w_ref, o_ref, kind) in enumerate(zip(ins, outs, kinds)):
            r, cw = w_ref.shape
            mine, _ = _halves(c, r // 2)
            for j, chip in enumerate(chips):
                cps.append(_remote(w_ref.at[mine], _shard_ref(o_ref, kind, me, mine, cw), sems[0].at[6 * i + j],
                                   sems[1].at[6 * i + j], (*chip, c)))
        return cps

    def start(ins, outs, sems):
        for cp in first_copies(ins, outs, sems):
            cp.start()

    def finish(ins, outs, sems):
        x, y, c, chips = _place()
        sib = (x, y, 1 - c)
        passed = []
        for i, (w_ref, o_ref, kind) in enumerate(zip(ins, outs, kinds)):
            r, cw = w_ref.shape
            mine, _ = _halves(c, r // 2)
            for j, (px, py) in enumerate(chips):
                blk = _shard_ref(o_ref, kind, 2 * px + py, mine, cw)
                _remote(blk, blk, sems[0].at[6 * i + j], sems[1].at[6 * i + j], sib).wait_recv()
                passed.append(_remote(blk, blk, sems[0].at[6 * i + 3 + j], sems[1].at[6 * i + 3 + j], sib))
                passed[-1].start()
        for i, (w_ref, o_ref, kind) in enumerate(zip(ins, outs, kinds)):
            r, cw = w_ref.shape
            _, theirs = _halves(c, r // 2)
            for j, (px, py) in enumerate(chips):
                blk = _shard_ref(o_ref, kind, 2 * px + py, theirs, cw)
                _remote(blk, blk, sems[0].at[6 * i + 3 + j], sems[1].at[6 * i + 3 + j], sib).wait_recv()
        for cp in first_copies(ins, outs, sems) + passed:
            cp.wait_send()

    return start, finish


def _gather_shapes(shards, kinds):
    return [jax.ShapeDtypeStruct((a.shape[0], N_CHIPS * a.shape[1]) if k == COLS else (N_CHIPS,) + a.shape, a.dtype)
            for a, k in zip(shards, kinds)]


def _gather_sems(nw):
    return [pltpu.SemaphoreType.DMA((6 * nw,)), pltpu.SemaphoreType.DMA((6 * nw,))]


def _gather_shards(shards, kinds):
    nw = len(shards)
    start, finish = _gather_parts(kinds)

    def body(*refs):
        ins, outs, sems = refs[:nw], refs[nw:2 * nw], refs[2 * nw:]
        start(ins, outs, sems)
        finish(ins, outs, sems)

    return pl.pallas_call(
        body, name="gather_shards", in_specs=[ANY] * nw, out_specs=[ANY] * nw,
        out_shape=_gather_shapes(shards, kinds), scratch_shapes=_gather_sems(nw),
    )(*shards)


def _gather_rider(shards, kinds):
    start, finish = _gather_parts(kinds)
    return _Rider(list(shards), _gather_shapes(shards, kinds), _gather_sems(len(shards)), start, finish)


def _half_shape(g, kind):
    if kind == COLS:
        return (g.shape[0] // 2, g.shape[1])
    return (g.shape[0], g.shape[1] // 2, g.shape[2])


def _swap_parts(kinds):
    def copies(ins, outs, sems):
        x, y, c, _ = _place()
        cps = []
        for i, (g_ref, a_ref, kind) in enumerate(zip(ins, outs, kinds)):
            r = g_ref.shape[0] if kind == COLS else g_ref.shape[1]
            _, theirs = _halves(c, r // 2)
            src = g_ref.at[theirs] if kind == COLS else g_ref.at[:, theirs]
            cps.append(_remote(src, a_ref, sems[0].at[i], sems[1].at[i], (x, y, 1 - c)))
        return cps

    def start(ins, outs, sems):
        for cp in copies(ins, outs, sems):
            cp.start()

    def finish(ins, outs, sems):
        for cp in copies(ins, outs, sems):
            cp.wait()

    return start, finish


def _swap_shapes(gs, kinds):
    return [jax.ShapeDtypeStruct(_half_shape(g, k), g.dtype) for g, k in zip(gs, kinds)]


def _pair_swap_halves(gs, kinds, name):
    nw = len(gs)
    start, finish = _swap_parts(kinds)

    def body(*refs):
        ins, outs, sems = refs[:nw], refs[nw:2 * nw], refs[2 * nw:]
        start(ins, outs, sems)
        finish(ins, outs, sems)

    return pl.pallas_call(
        body, name=name, in_specs=[ANY] * nw, out_specs=[ANY] * nw, out_shape=_swap_shapes(gs, kinds),
        scratch_shapes=[pltpu.SemaphoreType.DMA((nw,)), pltpu.SemaphoreType.DMA((nw,))],
    )(*gs)


def _swap_rider(gs, kinds):
    start, finish = _swap_parts(kinds)
    nw = len(gs)
    return _Rider(list(gs), _swap_shapes(gs, kinds), [pltpu.SemaphoreType.DMA((nw,)), pltpu.SemaphoreType.DMA((nw,))],
                  start, finish)


def _row_tile(rows, width, itemsize=4, target=2 ** 21):
    return _tile(rows, max(8, target // (width * itemsize)), 8)


def _add_half(g, a, kind, c_idx, name):
    if kind == COLS:
        half, wd = a.shape
        tr = _row_tile(half, wd)
        nblk = half // tr
        grid = (nblk,)
        g_spec = pl.BlockSpec((tr, wd), lambda i, c_ref: (c_ref[0] * nblk + i, 0))
        a_spec = pl.BlockSpec((tr, wd), lambda i, c_ref: (i, 0))
    else:
        n, half, wd = a.shape
        tr = _row_tile(half, wd)
        nblk = half // tr
        grid = (n, nblk)
        g_spec = pl.BlockSpec((1, tr, wd), lambda s, i, c_ref: (s, c_ref[0] * nblk + i, 0))
        a_spec = pl.BlockSpec((1, tr, wd), lambda s, i, c_ref: (s, i, 0))

    def body(c_ref, g_ref, a_ref, o_ref):
        o_ref[...] = (g_ref[...] + a_ref[...]).astype(o_ref.dtype)

    return pl.pallas_call(
        body, name=name,
        grid_spec=pltpu.PrefetchScalarGridSpec(num_scalar_prefetch=1, grid=grid, in_specs=[g_spec, a_spec],
                                               out_specs=a_spec),
        out_shape=jax.ShapeDtypeStruct(a.shape, EXCHANGE_DTYPE), compiler_params=_params(len(grid)),
    )(c_idx, g, a)


def _exchange_parts(kinds):
    def copies(ins, outs, sems):
        x, y, c, chips = _place()
        me = 2 * x + y
        cps = []
        for i, (p_ref, b_ref, kind) in enumerate(zip(ins, outs, kinds)):
            cw = b_ref.shape[2]
            for j, (px, py) in enumerate(chips):
                cps.append(_remote(_shard_ref(p_ref, kind, 2 * px + py, None, cw), b_ref.at[me],
                                   sems[0].at[3 * i + j], sems[1].at[3 * i + j], (px, py, c)))
        return cps

    def start(ins, outs, sems):
        for cp in copies(ins, outs, sems):
            cp.start()

    def finish(ins, outs, sems):
        x, y, c, chips = _place()
        for i, b_ref in enumerate(outs):
            for j, (px, py) in enumerate(chips):
                blk = b_ref.at[2 * px + py]
                _remote(blk, blk, sems[0].at[3 * i + j], sems[1].at[3 * i + j], (px, py, c)).wait_recv()
        for cp in copies(ins, outs, sems):
            cp.wait_send()

    return start, finish


def _exchange_shapes(ps, kinds):
    return [jax.ShapeDtypeStruct((N_CHIPS,) + ((p.shape[0], p.shape[1] // N_CHIPS) if k == COLS else tuple(p.shape[1:])),
                                 p.dtype) for p, k in zip(ps, kinds)]


def _exchange_sems(nw):
    return [pltpu.SemaphoreType.DMA((3 * nw,)), pltpu.SemaphoreType.DMA((3 * nw,))]


def _exchange_rider(ps, kinds):
    start, finish = _exchange_parts(kinds)
    return _Rider(list(ps), _exchange_shapes(ps, kinds), _exchange_sems(len(ps)), start, finish)


def _sum_chips(bq, name):
    n, h, wd = bq.shape
    tr = _row_tile(h, wd * n)

    def body(b_ref, o_ref):
        acc = b_ref[0].astype(F32)
        for s in range(1, n):
            acc = acc + b_ref[s].astype(F32)
        o_ref[...] = acc

    return pl.pallas_call(
        body, name=name, grid=(h // tr,),
        in_specs=[pl.BlockSpec((n, tr, wd), lambda i: (0, i, 0))], out_specs=pl.BlockSpec((tr, wd), lambda i: (i, 0)),
        out_shape=jax.ShapeDtypeStruct((h, wd), F32), compiler_params=_params(1),
    )(bq)


def _pair_join_halves(qs):
    nw = len(qs)

    def body(*refs):
        ins, outs = refs[:nw], refs[nw:2 * nw]
        send_sems, recv_sems = refs[2 * nw:]
        x, y, c, _ = _place()
        sent = []
        for i, (q_ref, o_ref) in enumerate(zip(ins, outs)):
            mine, _ = _halves(c, q_ref.shape[0])
            sent.append(_remote(q_ref, o_ref.at[mine], send_sems.at[i], recv_sems.at[i], (x, y, 1 - c)))
            sent[-1].start()
        for i, (q_ref, o_ref) in enumerate(zip(ins, outs)):
            _, theirs = _halves(c, q_ref.shape[0])
            _remote(q_ref, o_ref.at[theirs], send_sems.at[i], recv_sems.at[i], (x, y, 1 - c)).wait_recv()
        for cp in sent:
            cp.wait_send()

    return pl.pallas_call(
        body, name="pair_join_halves", in_specs=[ANY] * nw, out_specs=[ANY] * nw,
        out_shape=[jax.ShapeDtypeStruct((2 * q.shape[0], q.shape[1]), q.dtype) for q in qs],
        scratch_shapes=[pltpu.SemaphoreType.DMA((nw,)), pltpu.SemaphoreType.DMA((nw,))],
    )(*qs)


def _all_sum_small(s, name):
    sr, w = s.shape

    def body(s_ref, o_ref, buf, send_sems, recv_sems):
        x, y, c, _ = _place()
        me = 4 * x + 2 * y + c
        buf[me] = s_ref[...]
        peers = []
        for k in range(1, 8):
            px = 1 - x if k & 4 else x
            py = 1 - y if k & 2 else y
            pc = 1 - c if k & 1 else c
            peers.append((px, py, pc))
        sent = [_remote(s_ref, buf.at[me], send_sems.at[k], recv_sems.at[k], peer) for k, peer in enumerate(peers)]
        for cp in sent:
            cp.start()
        for k, (px, py, pc) in enumerate(peers):
            _remote(s_ref, buf.at[4 * px + 2 * py + pc], send_sems.at[k], recv_sems.at[k], (px, py, pc)).wait_recv()
        for cp in sent:
            cp.wait_send()
        acc = buf[0]
        for d in range(1, 8):
            acc = acc + buf[d]
        o_ref[...] = acc

    vm = pl.BlockSpec(memory_space=pltpu.VMEM)
    return pl.pallas_call(
        body, name=name, in_specs=[vm], out_specs=vm, out_shape=jax.ShapeDtypeStruct((sr, w), F32),
        scratch_shapes=[pltpu.VMEM((8, sr, w), F32), pltpu.SemaphoreType.DMA((7,)), pltpu.SemaphoreType.DMA((7,))],
    )(s)


BIG = ("w_in", "mem_kv_w", "w_br_hgrn", "w_br_fox", "w_br_mem", "w_out", "ffn_w_up", "ffn_w_down")
KIND = {"w_in": STACK, "mem_kv_w": STACK, "w_br_hgrn": COLS, "w_br_fox": COLS, "w_br_mem": COLS, "w_out": STACK,
        "ffn_w_up": STACK, "ffn_w_down": STACK}
ROW_SHARDED = ("mem_kv_w", "w_out", "ffn_w_down")
FIRST = ("w_in",)
REST = tuple(nm for nm in BIG if nm not in FIRST)
LATE = {"in_proj": tuple(nm for nm in REST if not nm.startswith("ffn_")),
        "fox_fwd": tuple(nm for nm in REST if nm.startswith("ffn_"))}
LAST = ("w_in",)
TRANSPOSED = ("w_in",)


def _z_layout(d, hw, fw, mw):
    gate, npair, nh, nm = 3 * d // LANE, fw // LANE, hw // LANE, mw // LANE
    fox0, hg0 = gate, gate + 3 * npair
    o_fox, o_mem = 4 * nh, 4 * nh + 3 * npair
    order = [o_mem + nm + j for j in range(gate)]
    order += [o_fox + k * npair + p for p in range(npair) for k in range(3)]
    order += [k * nh + h for h in range(nh) for k in range(4)]
    order += [o_mem + h for h in range(nm)]
    assert fox0 % 3 == 0 and hg0 % 4 == 0
    return fox0, hg0, hg0 + 4 * nh, order


def _reorder_blocks(a, order):
    runs, start = [], 0
    for i in range(1, len(order) + 1):
        if i == len(order) or order[i] != order[i - 1] + 1:
            runs.append((order[start], order[i - 1] + 1))
            start = i
    return jnp.concatenate([a[:, lo * LANE:hi * LANE] for lo, hi in runs], axis=1)


def _put_shard(arr, kind, s, piece):
    if kind == COLS:
        return lax.dynamic_update_slice(arr, piece, (0, s * piece.shape[1]))
    return lax.dynamic_update_slice(arr, piece[None], (s, 0, 0))


def _take_shard(arr, kind, s):
    if kind == COLS:
        return lax.dynamic_slice(arr, (0, s * (arr.shape[1] // N_CHIPS)), (arr.shape[0], arr.shape[1] // N_CHIPS))
    return lax.dynamic_index_in_dim(arr, s, 0, keepdims=False)


def _w_in_pieces(cs, s1, nf):
    out = []
    for s in range(N_CHIPS):
        lo, hi = cs * s, cs * (s + 1)
        for a, b, forget in ((lo, min(hi, s1), False), (max(lo, s1), min(hi, s1 + nf), True), (max(lo, s1 + nf), hi, False)):
            if a < b:
                out.append((s, a - lo, b - lo, forget, a - s1 if forget else (a if a < s1 else a - nf)))
    return out


def _split_w_in(stacked, s1, nf):
    pieces = _w_in_pieces(stacked.shape[2], s1, nf)
    main = [stacked[s, :, a:b] for s, a, b, forget, _ in pieces if not forget]
    ff = [stacked[s, :, a:b] for s, a, b, forget, _ in pieces if forget]
    return jnp.concatenate(main, axis=1), jnp.concatenate(ff, axis=1)


def _join_w_in(g_main, g_ff, s1, nf):
    cs = (g_main.shape[1] + nf) // N_CHIPS
    shards = [[] for _ in range(N_CHIPS)]
    for s, a, b, forget, off in _w_in_pieces(cs, s1, nf):
        shards[s].append((g_ff if forget else g_main)[:, off:off + b - a])
    return jnp.stack([jnp.concatenate(p, axis=1) if len(p) > 1 else p[0] for p in shards])


SMALL = ("norm_mix_g", "norm_mem_g", "norm_ffn_g", "hgrn_lb_logits", "hgrn_norm_g", "fox_f_bias", "fox_q_norm_g",
         "fox_k_norm_g", "mem_q_norm_g", "mem_k_norm_g", "ffn_conv_b")


def _small_rows(shapes):
    rows = []
    for a, (r, c) in enumerate(shapes):
        for i in range(r):
            for lo in range(0, c, FLAT_W):
                rows.append((a, i, lo, min(FLAT_W, c - lo)))
    return rows


def _pack_small(vals):
    rows = _small_rows([v.shape for v in vals])
    sr = -(-len(rows) // 8) * 8

    def body(*refs):
        o_ref = refs[-1]
        o_ref[...] = jnp.zeros(o_ref.shape, F32)
        for k, (a, i, lo, wd) in enumerate(rows):
            o_ref[k:k + 1, 0:wd] = refs[a][i:i + 1, lo:lo + wd]

    vm = pl.BlockSpec(memory_space=pltpu.VMEM)
    return pl.pallas_call(body, name="pack_small", in_specs=[vm] * len(vals), out_specs=vm,
                          out_shape=jax.ShapeDtypeStruct((sr, FLAT_W), F32))(*vals)


def _row_of(buf_ref, rows, a, i):
    parts = [buf_ref[k:k + 1, 0:wd] for k, (a2, i2, _, wd) in enumerate(rows) if (a2, i2) == (a, i)]
    return jnp.concatenate(parts, axis=1) if len(parts) > 1 else parts[0]


def _unpack_small(buf, shapes):
    rows = _small_rows(shapes)

    def body(buf_ref, *outs):
        for a, (r, _) in enumerate(shapes):
            for i in range(r):
                outs[a][i:i + 1, :] = _row_of(buf_ref, rows, a, i)

    vm = pl.BlockSpec(memory_space=pltpu.VMEM)
    return pl.pallas_call(body, name="unpack_small", in_specs=[vm], out_specs=[vm] * len(shapes),
                          out_shape=[jax.ShapeDtypeStruct(shp, F32) for shp in shapes])(buf)


def _adamw_small(buf, shapes, ws, ms, vs):
    n = len(ws)
    rows = _small_rows(shapes)
    c1 = 1.0 / (1.0 - ADAM_B1 ** ADAM_STEP)
    c2 = 1.0 / (1.0 - ADAM_B2 ** ADAM_STEP)

    def body(buf_ref, *refs):
        w_refs, m_refs, v_refs = refs[:n], refs[n:2 * n], refs[2 * n:3 * n]
        outs = refs[3 * n:]
        g_out, d_out, m_out, v_out, rest = outs[:n], outs[n:2 * n], outs[2 * n:3 * n], outs[3 * n:4 * n], outs[4 * n:]
        for a, (r, _) in enumerate(shapes):
            for i in range(r):
                gv = _row_of(buf_ref, rows, a, i)
                if a >= n:
                    rest[a - n][i:i + 1, :] = gv
                    continue
                row = slice(i, i + 1)
                mn = ADAM_B1 * m_refs[a][row, :] + (1.0 - ADAM_B1) * gv
                vn = ADAM_B2 * v_refs[a][row, :] + (1.0 - ADAM_B2) * (gv * gv)
                g_out[a][row, :] = gv
                d_out[a][row, :] = -ADAM_LR * ((mn * c1) / (jnp.sqrt(vn * c2) + ADAM_EPS) + ADAM_WD * w_refs[a][row, :])
                m_out[a][row, :] = mn
                v_out[a][row, :] = vn

    vm = pl.BlockSpec(memory_space=pltpu.VMEM)
    own = [jax.ShapeDtypeStruct(shp, F32) for shp in shapes[:n]]
    outs = pl.pallas_call(
        body, name="adamw_small", in_specs=[vm] * (1 + 3 * n), out_specs=[vm] * (4 * n + len(shapes) - n),
        out_shape=own * 4 + [jax.ShapeDtypeStruct(shp, F32) for shp in shapes[n:]],
    )(buf, *ws, *ms, *vs)
    return outs[:n], outs[n:2 * n], outs[2 * n:3 * n], outs[3 * n:4 * n], outs[4 * n:]


def _pad_lanes(v, width=LANE):
    return jnp.pad(v, ((0, 0), (0, width - v.shape[1])))


WEIGHTS = ("norm_mix_g", "norm_mem_g", "w_in", "hgrn_lb_logits", "hgrn_norm_g", "fox_f_bias", "fox_q_norm_g",
           "fox_k_norm_g", "mem_kv_w", "mem_q_norm_g", "mem_k_norm_g", "w_br_hgrn", "w_br_fox", "w_br_mem", "w_out",
           "norm_ffn_g", "ffn_w_up", "ffn_conv_w", "ffn_conv_b", "ffn_w_down")


def _local_step(x, mem, target, w, full, conv_w, late=None, hooks=None):
    b_, t_, d = x.shape
    n = b_ * t_
    hw, fw, mw = HG_HEADS * HG_D, FOX_HEADS * FOX_DH, MEM_HEADS * MEM_DH
    m_ = mem.shape[1]
    f = conv_w.shape[1]
    s1 = 4 * hw + 3 * fw
    fox_col, hg_col, mem_col, order = _z_layout(d, hw, fw, mw)
    gate_col = 0
    inverse = [order.index(j) for j in range(len(order))]

    w_main, w_ff = _split_w_in(full["w_in"], s1, FOX_HEADS)
    w_main = _reorder_blocks(w_main, order)
    w_ff = _pad_lanes(w_ff)
    f_bias = _pad_lanes(w["fox_f_bias"])
    cb = w["ffn_conv_b"]

    x2 = x.reshape(n, d)
    h = _rmsnorm_fwd(x2, w["norm_mix_g"], name="norm_mix_fwd")
    if late:
        pieces, kinds, finish = late["in_proj"]
        zm, gathered = _matmul(h, w_main, name="in_proj", rider=_gather_rider(pieces, kinds))
        full = {**full, **finish(gathered)}
    else:
        zm = _matmul(h, w_main, name="in_proj")
    w_brs = [full["w_br_hgrn"], full["w_br_fox"], full["w_br_mem"]]
    w_out, w_kv = full["w_out"], full["mem_kv_w"]
    zf = _matmul(h, w_ff, name="in_proj_forget")
    zm3, zf3 = zm.reshape(b_, t_, -1), zf.reshape(b_, t_, LANE)
    ya = _hgrn_fwd(zm3, w["hgrn_lb_logits"], w["hgrn_norm_g"], hw, hg_col)
    fc = _fox_prep(zf3, f_bias)
    fox_gq, fox_gk = jnp.tile(w["fox_q_norm_g"], (1, 2)), jnp.tile(w["fox_k_norm_g"], (1, 2))
    if late:
        pieces, kinds, finish = late["fox_fwd"]
        (yb, lse), gathered = _fox_fwd(zm3, fc, fox_gq, fox_gk, fw, fox_col, _gather_rider(pieces, kinds))
        full = {**full, **finish(gathered)}
    else:
        yb, lse = _fox_fwd(zm3, fc, fox_gq, fox_gk, fw, fox_col)[0]
    w_up, w_down = full["ffn_w_up"], full["ffn_w_down"]
    mem2 = mem.reshape(b_ * m_, d)
    hm = _rmsnorm_fwd(mem2, w["norm_mem_g"], name="norm_mem_fwd")
    mkv = _matmul(hm, w_kv, name="mem_kv_proj").reshape(b_, m_, 2 * mw)
    yc = _mem_fwd(zm3, mkv, w["mem_q_norm_g"], w["mem_k_norm_g"], mw, mem_col)
    ys = [ya.reshape(n, hw), yb.reshape(n, fw), yc.reshape(n, mw)]
    x1, merged, h2 = _merge_fwd(x2, ys, zm, w_brs, w_out, gate_col, w["norm_ffn_g"])
    u = _matmul(h2, w_up, name="ffn_up")
    u3 = u.reshape(b_, t_, 2 * f)
    yff = _glu_fwd(u3, conv_w, cb).reshape(n, f)
    dy, (loss_vec,), _ = _matmul_rows([yff], w_down, name="ffn_down_loss", tb=False, row_ins=[x1, target.reshape(n, d)],
                                      vec_ins=[], epilogue=_loss_epilogue, n_vec_out=1)

    grads = {}

    def ridden(name, call):
        if not hooks or name not in hooks:
            return call(None)[0]
        rider, then = hooks[name](grads)
        outs, extra = call(rider)
        then(extra)
        return outs

    dyff = _matmul(dy, w_down, tb=True, name="ffn_down_dx")
    grads["ffn_w_down"] = _matmul(yff, dy, ta=True, name="ffn_down_dw", tm=1408)
    du_a, du_v, grads["ffn_conv_w"], grads["ffn_conv_b"] = _glu_bwd(u3, dyff.reshape(b_, t_, f), conv_w, cb)
    du_a, du_v = du_a.reshape(n, f), du_v.reshape(n, f)
    dx1, (grads["norm_ffn_g"],), _ = _matmul_rows(
        [du_a, du_v], w_up, name="ffn_up_dx", tb=True, row_ins=[x1, dy], vec_ins=[w["norm_ffn_g"]],
        epilogue=_norm_bwd_epilogue(0), n_vec_out=1)
    grads["ffn_w_up"] = _matmul(h2, None, ta=True, name="ffn_up_dw", b_parts=[du_a, du_v], tn=f // 2, stack_out=True)

    dz, dpa, dpb, dpc, dya, dyb, dyc = _merge_bwd(dx1, ys, zm, w_brs, w_out, gate_col)
    dz = dz.reshape(b_, t_, -1)
    grads["w_out"] = _matmul(merged, dx1, ta=True, name="out_proj_dw")
    for nm, y_, dp_ in zip(("w_br_hgrn", "w_br_fox", "w_br_mem"), ys, (dpa, dpb, dpc)):
        grads[nm] = _matmul(y_, dp_, ta=True, name=nm + "_dw")

    dz, dmk, dmv, grads["mem_q_norm_g"], grads["mem_k_norm_g"] = _mem_bwd(
        zm3, mkv, dyc.reshape(b_, t_, mw), w["mem_q_norm_g"], w["mem_k_norm_g"], mw, mem_col, dz)
    dmkv = jnp.concatenate([dmk, dmv], axis=-1).reshape(b_ * m_, 2 * mw)
    grads["mem_kv_w"] = _matmul(hm, dmkv, ta=True, name="mem_kv_dw")
    dhm = _matmul(dmkv, w_kv, tb=True, name="mem_kv_dx")
    _, grads["norm_mem_g"] = _rmsnorm_bwd(mem2, [dhm], w["norm_mem_g"], None, name="norm_mem_bwd")

    dz, dfc, g_fq, g_fk = ridden("fox_bwd", lambda rider: _fox_bwd(
        zm3, yb, dyb.reshape(b_, t_, fw), lse, fc, fox_gq, fox_gk, fw, fox_col, dz, rider))
    grads["fox_q_norm_g"] = g_fq[:, :FOX_DH] + g_fq[:, FOX_DH:]
    grads["fox_k_norm_g"] = g_fk[:, :FOX_DH] + g_fk[:, FOX_DH:]
    dzf, g_fb = _fox_post(dfc, zf3, f_bias)
    grads["fox_f_bias"] = g_fb[:, :FOX_HEADS]

    dz, grads["hgrn_lb_logits"], grads["hgrn_norm_g"] = ridden("hgrn_bwd", lambda rider: _hgrn_bwd(
        zm3, dya.reshape(b_, t_, hw), w["hgrn_lb_logits"], w["hgrn_norm_g"], hw, hg_col, dz, rider))
    dzm = dz.reshape(n, -1)
    dzf2 = dzf.reshape(n, LANE)
    g_main = _matmul(h, dzm, ta=True, name="in_proj_dw")
    g_ff = _matmul(h, dzf2, ta=True, name="in_proj_forget_dw")
    grads["w_in"] = _join_w_in(_reorder_blocks(g_main, inverse), g_ff[:, :FOX_HEADS], s1, FOX_HEADS)

    dh_b = _matmul(dzf2, w_ff, tb=True, name="in_proj_forget_dx")

    def in_proj_dx(rider):
        out = _matmul(dzm, w_main, tb=True, name="in_proj_dx", rider=rider)
        return ([out[0]], out[1]) if rider else ([out], None)

    dh_a, = ridden("in_proj_dx", in_proj_dx)
    grad_x, grads["norm_mix_g"] = _rmsnorm_bwd(x2, [dh_a, dh_b], w["norm_mix_g"], dx1, name="norm_mix_bwd")
    return loss_vec, grad_x.reshape(b_, t_, d), grads


def kernel(x, mem, norm_mix_g, norm_mem_g, w_in, hgrn_lb_logits, hgrn_norm_g, fox_f_bias, fox_q_norm_g, fox_k_norm_g, mem_kv_w, mem_q_norm_g, mem_k_norm_g, w_br_hgrn, w_br_fox, w_br_mem, w_out, norm_ffn_g, ffn_w_up, ffn_conv_w, ffn_conv_b, ffn_w_down, loss_target, m_norm_mix_g, m_norm_mem_g, m_w_in, m_hgrn_lb_logits, m_hgrn_norm_g, m_fox_f_bias, m_fox_q_norm_g, m_fox_k_norm_g, m_mem_kv_w, m_mem_q_norm_g, m_mem_k_norm_g, m_w_br_hgrn, m_w_br_fox, m_w_br_mem, m_w_out, m_norm_ffn_g, m_ffn_w_up, m_ffn_conv_w, m_ffn_conv_b, m_ffn_w_down, v_norm_mix_g, v_norm_mem_g, v_w_in, v_hgrn_lb_logits, v_hgrn_norm_g, v_fox_f_bias, v_fox_q_norm_g, v_fox_k_norm_g, v_mem_kv_w, v_mem_q_norm_g, v_mem_k_norm_g, v_w_br_hgrn, v_w_br_fox, v_w_br_mem, v_w_out, v_norm_ffn_g, v_ffn_w_up, v_ffn_conv_w, v_ffn_conv_b, v_ffn_w_down):
    w = dict(zip(WEIGHTS, (norm_mix_g, norm_mem_g, w_in, hgrn_lb_logits, hgrn_norm_g, fox_f_bias, fox_q_norm_g,
                           fox_k_norm_g, mem_kv_w, mem_q_norm_g, mem_k_norm_g, w_br_hgrn, w_br_fox, w_br_mem, w_out,
                           norm_ffn_g, ffn_w_up, ffn_conv_w, ffn_conv_b, ffn_w_down)))
    m = dict(zip(WEIGHTS, (m_norm_mix_g, m_norm_mem_g, m_w_in, m_hgrn_lb_logits, m_hgrn_norm_g, m_fox_f_bias,
                           m_fox_q_norm_g, m_fox_k_norm_g, m_mem_kv_w, m_mem_q_norm_g, m_mem_k_norm_g, m_w_br_hgrn,
                           m_w_br_fox, m_w_br_mem, m_w_out, m_norm_ffn_g, m_ffn_w_up, m_ffn_conv_w, m_ffn_conv_b,
                           m_ffn_w_down)))
    v = dict(zip(WEIGHTS, (v_norm_mix_g, v_norm_mem_g, v_w_in, v_hgrn_lb_logits, v_hgrn_norm_g, v_fox_f_bias,
                           v_fox_q_norm_g, v_fox_k_norm_g, v_mem_kv_w, v_mem_q_norm_g, v_mem_k_norm_g, v_w_br_hgrn,
                           v_w_br_fox, v_w_br_mem, v_w_out, v_norm_ffn_g, v_ffn_w_up, v_ffn_conv_w, v_ffn_conv_b,
                           v_ffn_w_down)))
    c_idx = lax.axis_index("c")
    chip = 2 * lax.axis_index("x") + lax.axis_index("y")

    mine = {nm: w[nm][0].astype(MXU_DTYPE) for nm in BIG}

    def gathered_full(names, arrays):
        out = {nm: _put_shard(g, KIND[nm], chip, mine[nm]) for nm, g in zip(names, arrays)}
        return {nm: g.reshape(-1, g.shape[2]) if nm in ROW_SHARDED else g for nm, g in out.items()}

    full = gathered_full(FIRST, _gather_shards([mine[nm] for nm in FIRST], [KIND[nm] for nm in FIRST]))
    late = {host: ([mine[nm] for nm in names], [KIND[nm] for nm in names],
                   functools.partial(gathered_full, names)) for host, names in LATE.items()}
    cs = ffn_conv_w.shape[2]
    f = cs * N_CHIPS
    placed = lax.dynamic_update_slice(jnp.zeros((3, f), F32), ffn_conv_w[0] * (c_idx == 0).astype(F32), (0, chip * cs))
    conv_w = _unpack_small(_all_sum_small(_pack_small([placed]), "gather_conv_w"), [(3, f)])[0]

    c_arr = jnp.reshape(c_idx, (1,)).astype(jnp.int32)

    def stacked(nm, g):
        return g.reshape(N_CHIPS, -1, g.shape[1]) if nm in ROW_SHARDED else g

    def with_own(landed, partial, kinds):
        return [_put_shard(bq, STACK, chip, _take_shard(p, k, chip)) for bq, p, k in zip(landed, partial, kinds)]

    kinds_rest, kinds_last = [KIND[nm] for nm in REST], [KIND[nm] for nm in LAST]
    state = {}

    def swap_rest(grads):
        gs = [stacked(nm, grads[nm]) for nm in REST]

        def then(from_sibling):
            state["partial_rest"] = [_add_half(g, a, k, c_arr, "add_half_" + nm)
                                     for g, a, k, nm in zip(gs, from_sibling, kinds_rest, REST)]

        return _swap_rider(gs, kinds_rest), then

    def exchange_rest(grads):
        def then(landed):
            state["landed_rest"] = with_own(landed, state["partial_rest"], kinds_rest)

        return _exchange_rider(state["partial_rest"], kinds_rest), then

    def exchange_last(grads):
        gs = [stacked(nm, grads[nm]) for nm in LAST]
        from_sibling = _pair_swap_halves(gs, kinds_last, "pair_swap_halves_last")
        partial = [_add_half(g, a, k, c_arr, "add_half_" + nm) for g, a, k, nm in zip(gs, from_sibling, kinds_last, LAST)]

        def then(landed):
            state["landed_last"] = with_own(landed, partial, kinds_last)

        return _exchange_rider(partial, kinds_last), then

    hooks = {"fox_bwd": swap_rest, "hgrn_bwd": exchange_rest, "in_proj_dx": exchange_last}

    loss_vec, grad_x, grads = _local_step(x, mem, loss_target, w, full, conv_w, late, hooks)

    landed = dict(zip(LAST + REST, state["landed_last"] + state["landed_rest"]))
    reduced_half = [_sum_chips(landed[nm], "sum_chips_" + nm) for nm in BIG]
    joined = [lax.dynamic_update_slice(o, q, (c_idx * q.shape[0], 0))
              for o, q in zip(_pair_join_halves(reduced_half), reduced_half)]
    gshards = dict(zip(BIG, joined))

    small_shapes = [w[nm].shape for nm in SMALL] + [grads["ffn_conv_w"].shape, loss_vec.shape]
    summed = _all_sum_small(_pack_small([grads[nm] for nm in SMALL] + [grads["ffn_conv_w"], loss_vec]),
                            "all_sum_small_grads")
    g_small, d_small, m_small, v_small, (g_conv_w, loss_row) = _adamw_small(
        summed, small_shapes, [w[nm] for nm in SMALL], [m[nm] for nm in SMALL], [v[nm] for nm in SMALL])
    loss = jnp.sum(loss_row)
    g_out = {nm: gshards[nm][None] for nm in BIG}
    g_out["ffn_conv_w"] = lax.dynamic_slice(g_conv_w, (0, chip * cs), (3, cs))[None]
    delta, new_m, new_v = dict(zip(SMALL, d_small)), dict(zip(SMALL, m_small)), dict(zip(SMALL, v_small))
    g_out.update(zip(SMALL, g_small))
    for nm in BIG + ("ffn_conv_w",):
        operands = (w[nm], g_out[nm], m[nm], v[nm])
        if nm in TRANSPOSED:
            operands = [jnp.swapaxes(a, 1, 2) for a in operands]
        outs = _adamw(*operands, name="adamw_" + nm)
        delta[nm], new_m[nm], new_v[nm] = [jnp.swapaxes(o, 1, 2) for o in outs] if nm in TRANSPOSED else outs

    return (loss, grad_x, *[g_out[nm] for nm in WEIGHTS], *[delta[nm] for nm in WEIGHTS],
            *[new_m[nm] for nm in WEIGHTS], *[new_v[nm] for nm in WEIGHTS])
```

```python
import functools
import math

import jax
import jax.numpy as jnp
from jax import lax
from jax.experimental import pallas as pl
from jax.experimental.pallas import tpu as pltpu

F32 = jnp.float32
BF16 = jnp.bfloat16
MXU_DTYPE = jnp.bfloat16
EXCHANGE_DTYPE = jnp.bfloat16

EPS = 1e-6
HG_HEADS, HG_D = 4, 128
FOX_HEADS, FOX_DH = 8, 64
MEM_HEADS, MEM_DH = 4, 128
HG_CHUNK = 64
FOX_BLOCK = 256
LANE = 128
FFN_GROUP = 256
FLAT_W = 1024
VMEM_LIMIT = 56 * 2 ** 20
NEG = -1e30
N_CHIPS = 4

ADAM_LR, ADAM_B1, ADAM_B2, ADAM_EPS, ADAM_WD, ADAM_STEP = 0.001, 0.9, 0.999, 1e-08, 0.01, 10

MESH = pl.DeviceIdType.MESH
ANY = pl.BlockSpec(memory_space=pl.ANY)


def _mx(x):
    return x.astype(MXU_DTYPE)


def _dot(a, b, ca, cb):
    return lax.dot_general(_mx(a), _mx(b), (((ca,), (cb,)), ((), ())), preferred_element_type=F32)


def _nn(a, b):
    return _dot(a, b, 1, 0)


def _nt(a, b):
    return _dot(a, b, 1, 1)


def _tn(a, b):
    return _dot(a, b, 0, 0)


def _dotp(a, b, ca, cb):
    return lax.dot_general(a, b, (((ca,), (cb,)), ((), ())), precision=lax.Precision.HIGHEST,
                           preferred_element_type=F32)


def _tri_dot(tri_bf, x):
    hi = x.astype(BF16)
    r = x - hi.astype(F32)
    mid = r.astype(BF16)
    lo = (r - mid.astype(F32)).astype(BF16)

    def d(v):
        return lax.dot_general(tri_bf, v, (((1,), (0,)), ((), ())), preferred_element_type=F32)

    return d(hi) + d(mid) + d(lo)


def _sig(x):
    return jax.nn.sigmoid(x)


def _tile(dim, pref, unit=LANE):
    if dim <= pref:
        return dim
    t = pref - pref % unit
    while t >= unit:
        if dim % t == 0:
            return t
        t -= unit
    return dim


def _params(n_grid):
    return pltpu.CompilerParams(dimension_semantics=("arbitrary",) * n_grid, vmem_limit_bytes=VMEM_LIMIT)


def _acc(ref, val, first):
    @pl.when(first)
    def _():
        ref[...] = val

    @pl.when(jnp.logical_not(first))
    def _():
        ref[...] += val


class _Rider:
    def __init__(self, inputs, out_shapes, scratch, start, finish):
        self.inputs, self.out_shapes, self.scratch, self.start, self.finish = inputs, out_shapes, scratch, start, finish


def _ride(body, rider, n_in, n_out, grid):
    if rider is None:
        return body
    ri, ro, rs = len(rider.inputs), len(rider.out_shapes), len(rider.scratch)

    def wrapped(*refs):
        a, b, c = n_in + ri, n_in + ri + n_out, n_in + ri + n_out + ro
        base = refs[:n_in] + refs[a:b] + refs[c:len(refs) - rs]
        r_in, r_out, r_scr = refs[n_in:a], refs[b:c], refs[len(refs) - rs:]
        step = pl.program_id(0)
        for ax in range(1, len(grid)):
            step = step * grid[ax] + pl.program_id(ax)

        @pl.when(step == 0)
        def _():
            rider.start(r_in, r_out, r_scr)

        body(*base)

        @pl.when(step == math.prod(grid) - 1)
        def _():
            rider.finish(r_in, r_out, r_scr)

    return wrapped


def _ride_call(body, rider, *, name, grid, in_specs, out_specs, out_shape, scratch, args, aliases=None):
    n_in, n_out = len(in_specs), len(out_specs)
    aliases = aliases or {}
    if rider is None:
        outs = pl.pallas_call(body, name=name, grid=grid, in_specs=in_specs, out_specs=out_specs, out_shape=out_shape,
                              scratch_shapes=scratch, input_output_aliases=aliases,
                              compiler_params=_params(len(grid)))(*args)
        return list(outs), None
    outs = pl.pallas_call(
        _ride(body, rider, n_in, n_out, grid), name=name, grid=grid,
        in_specs=list(in_specs) + [ANY] * len(rider.inputs), out_specs=list(out_specs) + [ANY] * len(rider.out_shapes),
        out_shape=list(out_shape) + list(rider.out_shapes), scratch_shapes=list(scratch) + list(rider.scratch),
        input_output_aliases=aliases, compiler_params=_params(len(grid)),
    )(*args, *rider.inputs)
    return list(outs[:n_out]), list(outs[n_out:])


def _matmul(a, b, *, name, ta=False, tb=False, tm=1024, tn=2048, tk=None, rider=None, b_parts=None, stack_out=False):
    m, k = (a.shape[1], a.shape[0]) if ta else a.shape
    tk = tk or (1024 if ta else 2048)
    stacked_b = b is not None and b.ndim == 3
    if b_parts:
        n, tn = 2 * b_parts[0].shape[1], _tile(b_parts[0].shape[1], tn)
    elif stacked_b:
        n, tn = b.shape[0] * b.shape[2], b.shape[2]
    else:
        n = b.shape[0] if tb else b.shape[1]
        tn = _tile(n, tn)
    tm, tk = _tile(m, tm), _tile(k, tk)
    nk, nj = k // tk, n // tn

    def body(a_ref, *refs):
        o_ref = refs[-1]
        if b_parts:
            bv = jnp.where(pl.program_id(1) < nj // 2, refs[0][...], refs[1][...])
        else:
            bv = refs[0][...]
        p = _dot(a_ref[...], bv, 0 if ta else 1, 1 if tb else 0)
        if nk == 1:
            o_ref[...] = p
        else:
            _acc(o_ref, p, pl.program_id(2) == 0)

    a_spec = pl.BlockSpec((tk, tm), lambda i, j, kk: (kk, i)) if ta else pl.BlockSpec((tm, tk), lambda i, j, kk: (i, kk))
    if b_parts:
        half = nj // 2
        b_specs = [pl.BlockSpec((tk, tn), lambda i, j, kk: (kk, jnp.minimum(j, half - 1))),
                   pl.BlockSpec((tk, tn), lambda i, j, kk: (kk, jnp.maximum(j - half, 0)))]
        b_args = list(b_parts)
    elif stacked_b:
        b_specs, b_args = [pl.BlockSpec((None, tk, tn), lambda i, j, kk: (j, kk, 0))], [b]
    else:
        b_specs = [pl.BlockSpec((tn, tk), lambda i, j, kk: (j, kk)) if tb else pl.BlockSpec((tk, tn), lambda i, j, kk: (kk, j))]
        b_args = [b]
    if stack_out:
        o_spec, o_sds = pl.BlockSpec((None, tm, tn), lambda i, j, kk: (j, i, 0)), jax.ShapeDtypeStruct((nj, m, tn), F32)
    else:
        o_spec, o_sds = pl.BlockSpec((tm, tn), lambda i, j, kk: (i, j)), jax.ShapeDtypeStruct((m, n), F32)
    outs, extra = _ride_call(body, rider, name=name, grid=(m // tm, nj, nk), in_specs=[a_spec] + b_specs,
                             out_specs=[o_spec], out_shape=[o_sds], scratch=[], args=(a, *b_args))
    return (outs[0], extra) if rider else outs[0]


def _matmul_rows(a_parts, b, *, name, tb, row_ins, vec_ins, epilogue, n_vec_out, tm=512, tk=2048, rider=None):
    m, kp = a_parts[0].shape
    stacked_b = b.ndim == 3
    n = b.shape[1] if stacked_b else (b.shape[0] if tb else b.shape[1])
    tm, tk = _tile(m, tm, 8), (b.shape[2] if stacked_b else _tile(kp, tk))
    nk = kp // tk
    n_a, n_row, n_vec = len(a_parts), len(row_ins), len(vec_ins)

    def body(*refs):
        a_refs, b_refs = refs[:n_a], refs[n_a:2 * n_a]
        rows = refs[2 * n_a:2 * n_a + n_row]
        vecs = refs[2 * n_a + n_row:2 * n_a + n_row + n_vec]
        o_ref = refs[2 * n_a + n_row + n_vec]
        v_refs = refs[2 * n_a + n_row + n_vec + 1:-1]
        acc_ref = refs[-1]
        i, kk = pl.program_id(0), pl.program_id(1)
        p = _dot(a_refs[0][...], b_refs[0][...], 1, 1 if tb else 0)
        for a_ref, b_ref in zip(a_refs[1:], b_refs[1:]):
            p = p + _dot(a_ref[...], b_ref[...], 1, 1 if tb else 0)
        _acc(acc_ref, p, kk == 0)

        @pl.when(kk == nk - 1)
        def _():
            out, vouts = epilogue(acc_ref[...], *[r[...] for r in rows], *[v[...] for v in vecs])
            o_ref[...] = out
            for v_ref, v in zip(v_refs, vouts):
                _acc(v_ref, v, i == 0)

    a_spec = pl.BlockSpec((tm, tk), lambda i, kk: (i, kk))
    if stacked_b:
        b_specs = [pl.BlockSpec((None, n, tk), functools.partial(lambda i, kk, q: (q * nk + kk, 0, 0), q=q))
                   for q in range(n_a)]
    else:
        b_specs = [pl.BlockSpec((n, tk), functools.partial(lambda i, kk, q: (0, q * nk + kk), q=q)) if tb else
                   pl.BlockSpec((tk, n), functools.partial(lambda i, kk, q: (q * nk + kk, 0), q=q)) for q in range(n_a)]
    row = pl.BlockSpec((tm, n), lambda i, kk: (i, 0))
    vec = pl.BlockSpec((1, n), lambda i, kk: (0, 0))
    outs, extra = _ride_call(
        body, rider, name=name, grid=(m // tm, nk),
        in_specs=[a_spec] * n_a + b_specs + [row] * n_row + [vec] * n_vec,
        out_specs=[row] + [vec] * n_vec_out,
        out_shape=[jax.ShapeDtypeStruct((m, n), F32)] + [jax.ShapeDtypeStruct((1, n), F32)] * n_vec_out,
        scratch=[pltpu.VMEM((tm, n), F32)], args=(*a_parts, *([b] * n_a), *row_ins, *vec_ins))
    return outs[0], outs[1:], extra


def _norm_bwd_epilogue(n_dh):
    def epilogue(dh, x, res, *rest):
        for extra in rest[:n_dh]:
            dh = dh + extra
        g = rest[n_dh]
        r = lax.rsqrt(jnp.mean(x * x, axis=-1, keepdims=True) + EPS)
        dhg = dh * g
        dx = res + r * dhg - x * (r * r * r) * jnp.mean(dhg * x, axis=-1, keepdims=True)
        return dx, [jnp.sum(dh * x * r, axis=0, keepdims=True)]

    return epilogue


def _loss_epilogue(y, x1, target):
    d = y.shape[1]
    err = x1 + y - target
    return err * (1.0 / d), [jnp.sum(err * err, axis=0, keepdims=True) * (0.5 / d)]


def _rmsnorm_fwd(x, g, *, name, tm=512):
    n, d = x.shape
    tm = _tile(n, tm, 8)

    def body(x_ref, g_ref, o_ref):
        xv = x_ref[...]
        r = lax.rsqrt(jnp.mean(xv * xv, axis=-1, keepdims=True) + EPS)
        o_ref[...] = (xv * r * g_ref[...]).astype(o_ref.dtype)

    return pl.pallas_call(
        body, name=name, grid=(n // tm,),
        in_specs=[pl.BlockSpec((tm, d), lambda i: (i, 0)), pl.BlockSpec((1, d), lambda i: (0, 0))],
        out_specs=pl.BlockSpec((tm, d), lambda i: (i, 0)),
        out_shape=jax.ShapeDtypeStruct((n, d), MXU_DTYPE),
        compiler_params=_params(1),
    )(x, g)


def _rmsnorm_bwd(x, dhs, g, res, *, name, tm=512):
    n, d = x.shape
    tm = _tile(n, tm, 8)
    n_dh = len(dhs)
    has_res = res is not None

    def body(*refs):
        x_ref, dh_refs, g_ref = refs[0], refs[1:1 + n_dh], refs[1 + n_dh]
        res_ref = refs[2 + n_dh] if has_res else None
        dx_ref, dg_ref = refs[-2], refs[-1]
        xv = x_ref[...]
        dh = dh_refs[0][...].astype(F32)
        for r_ in dh_refs[1:]:
            dh = dh + r_[...].astype(F32)
        r = lax.rsqrt(jnp.mean(xv * xv, axis=-1, keepdims=True) + EPS)
        dhg = dh * g_ref[...]
        dx = r * dhg - xv * (r * r * r) * jnp.mean(dhg * xv, axis=-1, keepdims=True)
        if has_res:
            dx = dx + res_ref[...]
        dx_ref[...] = dx
        _acc(dg_ref, jnp.sum(dh * xv * r, axis=0, keepdims=True), pl.program_id(0) == 0)

    row = pl.BlockSpec((tm, d), lambda i: (i, 0))
    vec = pl.BlockSpec((1, d), lambda i: (0, 0))
    ins = [x] + list(dhs) + [g] + ([res] if has_res else [])
    return pl.pallas_call(
        body, name=name, grid=(n // tm,),
        in_specs=[row] * (1 + n_dh) + [vec] + ([row] if has_res else []),
        out_specs=[row, vec],
        out_shape=[jax.ShapeDtypeStruct((n, d), F32), jax.ShapeDtypeStruct((1, d), F32)],
        compiler_params=_params(1),
    )(*ins)


def _adamw(w, g, m, v, *, name, tr=256):
    _, r, c = w.shape
    c1 = 1.0 / (1.0 - ADAM_B1 ** ADAM_STEP)
    c2 = 1.0 / (1.0 - ADAM_B2 ** ADAM_STEP)

    def body(w_ref, g_ref, m_ref, v_ref, d_ref, mo_ref, vo_ref):
        gv = g_ref[...]
        mn = ADAM_B1 * m_ref[...] + (1.0 - ADAM_B1) * gv
        vn = ADAM_B2 * v_ref[...] + (1.0 - ADAM_B2) * (gv * gv)
        d_ref[...] = -ADAM_LR * ((mn * c1) / (jnp.sqrt(vn * c2) + ADAM_EPS) + ADAM_WD * w_ref[...])
        mo_ref[...] = mn
        vo_ref[...] = vn

    if r % 8 == 0 or r < 8:
        tr = _tile(r, tr, 8)
        grid, blk = (r // tr,), pl.BlockSpec((1, tr, c), lambda i: (0, i, 0))
    else:
        tc = _tile(c, tr)
        grid, blk = (c // tc,), pl.BlockSpec((1, r, tc), lambda i: (0, 0, i))
    sds = jax.ShapeDtypeStruct((1, r, c), F32)
    return pl.pallas_call(
        body, name=name, grid=grid, in_specs=[blk] * 4, out_specs=[blk] * 3, out_shape=[sds] * 3,
        compiler_params=_params(1),
    )(w, g, m, v)


def _bdot(a, b, ca, cb):
    return lax.dot_general(_mx(a), _mx(b), (((ca,), (cb,)), ((0,), (0,))), preferred_element_type=F32)


def _bdotp(a, b, ca, cb):
    return lax.dot_general(a, b, (((ca,), (cb,)), ((0,), (0,))), precision=lax.Precision.HIGHEST,
                           preferred_element_type=F32)


def _tri_dot_b(tri_bf, x):
    hi = x.astype(BF16)
    r = x - hi.astype(F32)
    mid = r.astype(BF16)
    lo = (r - mid.astype(F32)).astype(BF16)

    def d(v):
        return lax.dot_general(tri_bf, v, (((2,), (1,)), ((0,), (0,))), preferred_element_type=F32)

    return d(hi) + d(mid) + d(lo)


def _hgrn_forward(hq, hf, hi, lbv, tril, tril_bf):
    nc, c, _ = hq.shape
    sf = _sig(hf)
    f = lbv + (1.0 - lbv) * sf
    k = 1.0 - f
    gcum = _tri_dot_b(tril_bf, jnp.log(f))
    mid = gcum[:, c // 2 - 1:c // 2, :]
    glast = gcum[:, c - 1:c, :]
    sq = _sig(hq)
    q = hq * sq
    e_q = jnp.exp(gcum - mid)
    e_k = jnp.exp(mid - gcum)
    qe, ke = q * e_q, k * e_k
    a = jnp.where(tril, _bdot(qe, ke, 2, 2), 0.0)
    e_g = jnp.exp(gcum)
    qg = q * e_g
    e_s = jnp.exp(glast - gcum)
    kg = k * e_s
    e_l = jnp.exp(glast)
    upd = _bdot(hi, kg, 1, 1)
    st = jnp.zeros((HG_D, HG_D), F32)
    states = []
    for n in range(nc):
        states.append(st)
        st = st * e_l[n] + upd[n]
    st_all = jnp.stack(states)
    o = _bdot(a, hi, 2, 1) + _bdot(qg, st_all, 2, 2)
    return dict(sf=sf, f=f, k=k, sq=sq, q=q, e_q=e_q, e_k=e_k, qe=qe, ke=ke, a=a, e_g=e_g, qg=qg, o=o,
                e_s=e_s, kg=kg, e_l=e_l, st_all=st_all)


def _hgrn_specs(t_, col0):
    def col(off):
        return pl.BlockSpec((1, t_, LANE), lambda h, b: (b, 0, col0 + 4 * h + off))

    vec = pl.BlockSpec((2, LANE), lambda h, b: (0, h))
    one = pl.BlockSpec((1, LANE), lambda h, b: (0, 0))
    blk = pl.BlockSpec((1, t_, LANE), lambda h, b: (b, 0, h))
    return col, vec, one, blk


def _chunk_masks(nc, c):
    row = lax.broadcasted_iota(jnp.int32, (nc, c, c), 1)
    cl = lax.broadcasted_iota(jnp.int32, (nc, c, c), 2)
    return row >= cl, (row >= cl).astype(BF16), (row <= cl).astype(BF16)


def _hgrn_fwd(zm, lb, gn, hw, col0):
    b_, t_, _ = zm.shape
    c = min(HG_CHUNK, t_)
    nc = t_ // c
    col, vec, one, blk = _hgrn_specs(t_, col0)

    def body(q_ref, f_ref, i_ref, g_ref, lb_ref, gn_ref, y_ref):
        lbv, gnv = _sig(lb_ref[0:1, :] - lb_ref[1:2, :]), gn_ref[...]
        tril, tril_bf, _ = _chunk_masks(nc, c)
        chunks = lambda ref: ref[0].reshape(nc, c, LANE)
        o = _hgrn_forward(chunks(q_ref), chunks(f_ref), chunks(i_ref), lbv, tril, tril_bf)["o"]
        r = lax.rsqrt(jnp.mean(o * o, axis=-1, keepdims=True) + EPS)
        hg = chunks(g_ref)
        y_ref[0] = (o * r * gnv * (hg * _sig(hg))).reshape(t_, LANE)

    return pl.pallas_call(
        body, name="hgrn_fwd", grid=(HG_HEADS, b_),
        in_specs=[col(0), col(1), col(2), col(3), vec, one], out_specs=blk,
        out_shape=jax.ShapeDtypeStruct((b_, t_, hw), F32),
        compiler_params=_params(2),
    )(zm, zm, zm, zm, lb, gn)


def _hgrn_bwd(zm, dy, lb, gn, hw, col0, dz, rider=None):
    b_, t_, _ = zm.shape
    c = min(HG_CHUNK, t_)
    nc = t_ // c
    col, vec, one, blk = _hgrn_specs(t_, col0)

    def body(q_ref, f_ref, i_ref, g_ref, dy_ref, lb_ref, gn_ref, _, dz_ref, dlb_ref, dgn_ref):
        h, b = pl.program_id(0), pl.program_id(1)
        lbv, gnv = _sig(lb_ref[0:1, :] - lb_ref[1:2, :]), gn_ref[...]
        tril, tril_bf, triu_bf = _chunk_masks(nc, c)
        last_row = lax.broadcasted_iota(jnp.int32, (nc, c, LANE), 1) == c - 1
        chunks = lambda ref: ref[0].reshape(nc, c, LANE)
        flat = lambda x: x.reshape(t_, LANE)
        hq, hi, hg = chunks(q_ref), chunks(i_ref), chunks(g_ref)
        p = _hgrn_forward(hq, chunks(f_ref), hi, lbv, tril, tril_bf)
        o, q, k, st_all, e_l = p["o"], p["q"], p["k"], p["st_all"], p["e_l"]
        dyv = chunks(dy_ref)
        sg = _sig(hg)
        r = lax.rsqrt(jnp.mean(o * o, axis=-1, keepdims=True) + EPS)
        dn = dyv * (hg * sg)
        dz_ref[0, :, 3 * LANE:] = flat(dyv * (o * r * gnv) * (sg * (1.0 + hg * (1.0 - sg)))).astype(dz_ref.dtype)
        dgn = jnp.sum(flat(dn * o * r), axis=0, keepdims=True)
        dng = dn * gnv
        do = r * dng - o * (r * r * r) * jnp.mean(dng * o, axis=-1, keepdims=True)
        back = _bdotp(do, p["qg"], 1, 1)
        dst = jnp.zeros((HG_D, HG_D), F32)
        dsts = [None] * nc
        for n in range(nc - 1, -1, -1):
            dsts[n] = dst
            dst = dst * e_l[n] + back[n]
        dst_all = jnp.stack(dsts)
        da = jnp.where(tril, _bdotp(do, hi, 2, 2), 0.0)
        dq = _bdotp(da, p["ke"], 2, 1) * p["e_q"] + _bdotp(do, st_all, 2, 1) * p["e_g"]
        dk_state = _bdotp(hi, dst_all, 2, 1) * p["e_s"]
        dk = _bdotp(da, p["qe"], 1, 1) * p["e_k"] + dk_state
        dz_ref[0, :, 2 * LANE:3 * LANE] = flat(_bdot(p["a"], do, 1, 1) + _bdot(p["kg"], dst_all, 2, 2)).astype(dz_ref.dtype)
        extra = (jnp.sum(k * dk_state, axis=1, keepdims=True) + e_l * jnp.sum(st_all * dst_all, axis=1, keepdims=True))
        dgc = q * dq - k * dk + jnp.where(last_row, extra, 0.0)
        dfv = _tri_dot_b(triu_bf, dgc) / p["f"] - dk
        sf, sq = p["sf"], p["sq"]
        dz_ref[0, :, LANE:2 * LANE] = flat(dfv * (1.0 - lbv) * sf * (1.0 - sf)).astype(dz_ref.dtype)
        dlb = jnp.sum(flat(dfv * (1.0 - sf)), axis=0, keepdims=True)
        dz_ref[0, :, :LANE] = flat(dq * (sq * (1.0 + hq * (1.0 - sq)))).astype(dz_ref.dtype)
        dl0 = dlb * lbv * (1.0 - lbv)
        _acc(dlb_ref, jnp.concatenate([dl0, -dl0], axis=0), b == 0)
        _acc(dgn_ref, dgn, jnp.logical_and(b == 0, h == 0))

    return _ride_call(
        body, rider, name="hgrn_bwd", grid=(HG_HEADS, b_),
        in_specs=[col(0), col(1), col(2), col(3), blk, vec, one, ANY],
        out_specs=[pl.BlockSpec((1, t_, 4 * LANE), lambda h, b: (b, 0, col0 // 4 + h)), vec, one],
        out_shape=[jax.ShapeDtypeStruct(dz.shape, dz.dtype), jax.ShapeDtypeStruct((2, hw), F32),
                   jax.ShapeDtypeStruct((1, LANE), F32)],
        scratch=[], args=(zm, zm, zm, zm, dy, lb, gn, dz), aliases={7: 0})


def _fox_logf(x):
    return jnp.minimum(x, 0.0) - jnp.log(1.0 + jnp.exp(-jnp.abs(x)))


def _fox_prep(zf, bias):
    b_, t_, _ = zf.shape
    tb = min(FOX_BLOCK, t_)
    nb = t_ // tb

    def body(z_ref, b_ref, fc_ref):
        tril_bf = (lax.broadcasted_iota(jnp.int32, (tb, tb), 0) >= lax.broadcasted_iota(jnp.int32, (tb, tb), 1)).astype(BF16)
        bv = b_ref[...]

        def blk(i, carry):
            rows = pl.ds(pl.multiple_of(i * tb, tb), tb)
            fc = _tri_dot(tril_bf, _fox_logf(z_ref[0, rows, :] + bv)) + carry
            fc_ref[0, rows, :] = fc
            return fc[tb - 1:tb, :]

        lax.fori_loop(0, nb, blk, jnp.zeros((1, LANE), F32))

    blk_spec = pl.BlockSpec((1, t_, LANE), lambda b: (b, 0, 0))
    return pl.pallas_call(
        body, name="fox_prep", grid=(b_,),
        in_specs=[blk_spec, pl.BlockSpec((1, LANE), lambda b: (0, 0))], out_specs=blk_spec,
        out_shape=jax.ShapeDtypeStruct((b_, t_, LANE), F32), compiler_params=_params(1),
    )(zf, bias)


def _fox_post(dfc, zf, bias):
    b_, t_, _ = zf.shape
    npair = dfc.shape[1]
    tb = min(FOX_BLOCK, t_)
    nb = t_ // tb

    def body(d_ref, z_ref, b_ref, dz_ref, db_ref):
        triu_bf = (lax.broadcasted_iota(jnp.int32, (tb, tb), 0) <= lax.broadcasted_iota(jnp.int32, (tb, tb), 1)).astype(BF16)
        valid = lax.broadcasted_iota(jnp.int32, (tb, LANE), 1) < FOX_HEADS
        bv = b_ref[...]

        def blk(m, carry):
            tail, db = carry
            rows = pl.ds(pl.multiple_of((nb - 1 - m) * tb, tb), tb)
            dfc_rows = d_ref[0, 0, rows, :]
            for p in range(1, npair):
                dfc_rows = dfc_rows + pltpu.roll(d_ref[0, p, rows, :], 2 * p, 1)
            dlf = _tri_dot(triu_bf, dfc_rows) + tail
            dx = jnp.where(valid, dlf * _sig(-(z_ref[0, rows, :] + bv)), 0.0)
            dz_ref[0, rows, :] = dx.astype(dz_ref.dtype)
            return dlf[0:1, :], db + jnp.sum(dx, axis=0, keepdims=True)

        z1 = jnp.zeros((1, LANE), F32)
        _, db = lax.fori_loop(0, nb, blk, (z1, z1))
        _acc(db_ref, db, pl.program_id(0) == 0)

    blk_spec = pl.BlockSpec((1, t_, LANE), lambda b: (b, 0, 0))
    vec = pl.BlockSpec((1, LANE), lambda b: (0, 0))
    return pl.pallas_call(
        body, name="fox_post", grid=(b_,),
        in_specs=[pl.BlockSpec((1, npair, t_, LANE), lambda b: (b, 0, 0, 0)), blk_spec, vec], out_specs=[blk_spec, vec],
        out_shape=[jax.ShapeDtypeStruct((b_, t_, LANE), MXU_DTYPE), jax.ShapeDtypeStruct((1, LANE), F32)],
        compiler_params=_params(1),
    )(dfc, zf, bias)


FOX_TILE = 256
FOX_BAND = 512
AUG = 64


def _head_mean_matrix():
    r = lax.broadcasted_iota(jnp.int32, (LANE, LANE), 0) // FOX_DH
    c = lax.broadcasted_iota(jnp.int32, (LANE, LANE), 1) // FOX_DH
    return (r == c).astype(BF16)


def _dot_right_exact(x, m_bf):
    hi = x.astype(BF16)
    lo = (x - hi.astype(F32)).astype(BF16)

    def d(v):
        return lax.dot_general(v, m_bf, (((1,), (0,)), ((), ())), preferred_element_type=F32)

    return d(hi) + d(lo)


def _pair_norm(x, g2, bd):
    r = lax.rsqrt(_dot_right_exact(x * x, bd) * (1.0 / FOX_DH) + EPS)
    return x * r * g2, r


def _pair_norm_bwd(x, r, dy, g2, bd):
    dyg = dy * g2
    dx = r * dyg - x * (r * r * r) * (_dot_right_exact(dyg * x, bd) * (1.0 / FOX_DH))
    return dx, jnp.sum(dy * x * r, axis=0, keepdims=True)


def _head_lanes(xn, hh):
    return xn if hh == 0 else pltpu.roll(xn, FOX_DH, 1)


def _split3(x):
    hi = x.astype(BF16).astype(F32)
    mid = (x - hi).astype(BF16).astype(F32)
    return hi, mid, x - hi - mid


def _fox_operands(q_ref, k_ref, v_ref, fc_ref, gq2, gk2, p, qa, ka, va):
    t_ = q_ref.shape[1]
    bd = _head_mean_matrix()
    lane = lax.broadcasted_iota(jnp.int32, (t_, LANE), 1)
    qx, kx = q_ref[0], k_ref[0]
    qn, rq = _pair_norm(qx, gq2, bd)
    kn, rk = _pair_norm(kx, gk2, bd)
    vv = v_ref[0]
    q_aug = jnp.where(jnp.logical_and(lane >= AUG, lane < AUG + 3), 1.0, 0.0)
    for hh in range(2):
        fcol = jnp.sum(jnp.where(lane == 2 * p + hh, fc_ref[0], 0.0), axis=-1, keepdims=True)
        hi, mid, lo = _split3(-fcol)
        k_aug = jnp.where(lane == AUG, hi, jnp.where(lane == AUG + 1, mid, jnp.where(lane == AUG + 2, lo,
                          jnp.where(lane == AUG + 3, 1.0, 0.0))))
        head = lane < FOX_DH
        qa[hh] = jnp.where(head, _head_lanes(qn, hh), q_aug).astype(MXU_DTYPE)
        ka[hh] = jnp.where(head, _head_lanes(kn, hh), k_aug).astype(MXU_DTYPE)
        va[hh] = jnp.where(head, _head_lanes(vv, hh), 0.0).astype(MXU_DTYPE)
    return bd, lane, qx, kx, rq, rk


def _fox_specs(t_, fw, col0):
    npair = fw // LANE

    def col(off):
        return pl.BlockSpec((1, t_, LANE), lambda b, p: (b, 0, col0 + 3 * p + off))

    pair = pl.BlockSpec((1, t_, LANE), lambda b, p: (b, 0, p))
    full = pl.BlockSpec((1, t_, LANE), lambda b, p: (b, 0, 0))
    gvec = pl.BlockSpec((1, LANE), lambda b, p: (0, 0))
    lse = pl.BlockSpec((1, 1, t_, LANE), lambda b, p: (b, p, 0, 0))
    return col, pair, full, gvec, lse


def _fox_fwd(zm, fc, gq2, gk2, fw, col0, rider=None):
    b_, t_, _ = zm.shape
    npair = fw // LANE
    tq = min(FOX_TILE, t_)
    bw = min(FOX_BAND, t_)
    nband, tpb = t_ // bw, bw // tq
    scale = FOX_DH ** -0.5
    col, pair, full, gvec, lse_spec = _fox_specs(t_, fw, col0)

    def body(q_ref, k_ref, v_ref, fc_ref, gq_ref, gk_ref, o_ref, lse_ref, qa, ka, va):
        p = pl.program_id(1)
        _fox_operands(q_ref, k_ref, v_ref, fc_ref, gq_ref[...] * scale, gk_ref[...], p, qa, ka, va)
        ahead = lax.broadcasted_iota(jnp.int32, (tq, bw), 1) - lax.broadcasted_iota(jnp.int32, (tq, bw), 0)
        lane = lax.broadcasted_iota(jnp.int32, (tq, LANE), 1)

        for band in range(nband):
            c0 = band * bw

            def qtile(ii, _, c0=c0):
                r0 = pl.multiple_of(c0 + ii * tq, tq)
                rows = pl.ds(r0, tq)
                keep = ahead <= r0 - c0
                res = []
                for hh in range(2):
                    qb = qa[hh, rows, :]
                    s_b = jnp.where(keep, _nt(qb, ka[hh, c0:c0 + bw, :]), NEG)
                    m = jnp.max(s_b, axis=-1, keepdims=True)
                    if c0:
                        s_a = _nt(qb, ka[hh, 0:c0, :])
                        m = jnp.maximum(m, jnp.max(s_a, axis=-1, keepdims=True))
                    p_b = jnp.exp(s_b - m)
                    l = jnp.sum(p_b, axis=-1, keepdims=True)
                    acc = _nn(p_b, va[hh, c0:c0 + bw, :])
                    if c0:
                        p_a = jnp.exp(s_a - m)
                        l = l + jnp.sum(p_a, axis=-1, keepdims=True)
                        acc = acc + _nn(p_a, va[hh, 0:c0, :])
                    res.append((acc / l, m + jnp.log(l)))
                (o0, e0), (o1, e1) = res
                o_ref[0, rows, :] = jnp.where(lane < FOX_DH, o0, pltpu.roll(o1, FOX_DH, 1))
                lse_ref[0, 0, rows, :] = jnp.where(lane == 0, e0, jnp.where(lane == 1, e1, 0.0))
                return 0

            lax.fori_loop(0, tpb, qtile, 0)

    return _ride_call(
        body, rider, name="fox_fwd", grid=(b_, npair),
        in_specs=[col(0), col(1), col(2), full, gvec, gvec],
        out_specs=[pair, lse_spec],
        out_shape=[jax.ShapeDtypeStruct((b_, t_, fw), F32), jax.ShapeDtypeStruct((b_, npair, t_, LANE), F32)],
        scratch=[pltpu.VMEM((2, t_, LANE), MXU_DTYPE)] * 3, args=(zm, zm, zm, fc, gq2, gk2))


def _norm_bwd(x, dy, g):
    r = lax.rsqrt(jnp.mean(x * x, axis=-1, keepdims=True) + EPS)
    dyg = dy * g
    dx = r * dyg - x * (r * r * r) * jnp.mean(dyg * x, axis=-1, keepdims=True)
    return dx, jnp.sum(dy * x * r, axis=0, keepdims=True)


def _fox_bwd(zm, o, do, lse, fc, gq2, gk2, fw, col0, dz, rider=None):
    b_, t_, _ = zm.shape
    npair = fw // LANE
    tq = min(FOX_TILE, t_)
    nb = t_ // tq
    bw = min(FOX_BAND, t_)
    nband, tpb = t_ // bw, bw // tq
    scale = FOX_DH ** -0.5
    col, pair, full, gvec, lse_spec = _fox_specs(t_, fw, col0)

    def body(q_ref, k_ref, v_ref, o_ref, do_ref, lse_ref, fc_ref, gq_ref, gk_ref, _,
             dz_ref, dfc_ref, dgq_ref, dgk_ref, qa, ka, va, da, rowv, dq_acc, dk_acc, dv_acc):
        b, p = pl.program_id(0), pl.program_id(1)
        gq2v, gk2v = gq_ref[...] * scale, gk_ref[...]
        bd, lane, qx, kx, rq, rk = _fox_operands(q_ref, k_ref, v_ref, fc_ref, gq2v, gk2v, p, qa, ka, va)
        head = lane < FOX_DH
        dov = do_ref[0]
        dsum = _dot_right_exact(dov * o_ref[0], bd)
        eye = (lax.broadcasted_iota(jnp.int32, (tq, tq), 0) == lax.broadcasted_iota(jnp.int32, (tq, tq), 1)).astype(F32)
        for hh in range(2):
            da[hh] = jnp.where(head, _head_lanes(dov, hh), 0.0).astype(MXU_DTYPE)
            for blk in range(nb):
                rs = slice(blk * tq, (blk + 1) * tq)
                rowv[2 * hh:2 * hh + 1, rs] = jnp.sum(eye * lse_ref[0, 0, rs, hh:hh + 1], axis=0, keepdims=True)
                rowv[2 * hh + 1:2 * hh + 2, rs] = jnp.sum(eye * dsum[rs, hh * FOX_DH:hh * FOX_DH + 1], axis=0, keepdims=True)
        dq_acc[...] = jnp.zeros(dq_acc.shape, F32)
        ahead = lax.broadcasted_iota(jnp.int32, (tq, bw), 1) - lax.broadcasted_iota(jnp.int32, (tq, bw), 0)

        def part(hh, kb, vb, lo, hi, keep):
            qm, dm = qa[hh, lo:hi, :], da[hh, lo:hi, :]
            pt = jnp.exp(_nt(kb, qm) - rowv[2 * hh:2 * hh + 1, lo:hi])
            if keep is not None:
                pt = jnp.where(keep, pt, 0.0)
            dst = pt * (_nt(vb, dm) - rowv[2 * hh + 1:2 * hh + 2, lo:hi])
            dq_acc[hh, lo:hi, :] += _tn(dst, kb)
            return _nn(dst, qm), _nn(pt, dm)

        for band in range(nband):
            c0 = band * bw

            def kvtile(jj, _, c0=c0):
                r0 = pl.multiple_of(c0 + jj * tq, tq)
                rows = pl.ds(r0, tq)
                keep = ahead >= r0 - c0
                for hh in range(2):
                    kb, vb = ka[hh, rows, :], va[hh, rows, :]
                    dk_t, dv_t = part(hh, kb, vb, c0, c0 + bw, keep)
                    if c0 + bw < t_:
                        dk_u, dv_u = part(hh, kb, vb, c0 + bw, t_, None)
                        dk_t, dv_t = dk_t + dk_u, dv_t + dv_u
                    dk_acc[hh, rows, :] = dk_t
                    dv_acc[hh, rows, :] = dv_t
                return 0

            lax.fori_loop(0, tpb, kvtile, 0)

        dq0, dq1, dk0, dk1 = dq_acc[0], dq_acc[1], dk_acc[0], dk_acc[1]
        dqn = jnp.where(head, dq0, pltpu.roll(dq1, FOX_DH, 1))
        dkn = jnp.where(head, dk0, pltpu.roll(dk1, FOX_DH, 1))
        dqx, gq_part = _pair_norm_bwd(qx, rq, dqn, gq2v, bd)
        dkx, gk_part = _pair_norm_bwd(kx, rk, dkn, gk2v, bd)
        dz_ref[0, :, :LANE] = dqx.astype(dz_ref.dtype)
        dz_ref[0, :, LANE:2 * LANE] = dkx.astype(dz_ref.dtype)
        dz_ref[0, :, 2 * LANE:] = jnp.where(head, dv_acc[0], pltpu.roll(dv_acc[1], FOX_DH, 1)).astype(dz_ref.dtype)

        def bias_grad(dqh, dkh):
            return dqh[:, AUG + 3:AUG + 4] - dkh[:, AUG:AUG + 1]

        dfc_ref[0, 0] = jnp.where(lane == 0, bias_grad(dq0, dk0), jnp.where(lane == 1, bias_grad(dq1, dk1), 0.0))
        first = jnp.logical_and(b == 0, p == 0)
        _acc(dgq_ref, gq_part * scale, first)
        _acc(dgk_ref, gk_part, first)

    gs = jax.ShapeDtypeStruct((1, LANE), F32)
    return _ride_call(
        body, rider, name="fox_bwd", grid=(b_, npair),
        in_specs=[col(0), col(1), col(2), pair, pair, lse_spec, full, gvec, gvec, ANY],
        out_specs=[pl.BlockSpec((1, t_, 3 * LANE), lambda b, p: (b, 0, col0 // 3 + p)), lse_spec, gvec, gvec],
        out_shape=[jax.ShapeDtypeStruct(dz.shape, dz.dtype), jax.ShapeDtypeStruct((b_, npair, t_, LANE), F32), gs, gs],
        scratch=[pltpu.VMEM((2, t_, LANE), MXU_DTYPE)] * 4
        + [pltpu.VMEM((8, t_), F32)] + [pltpu.VMEM((2, t_, LANE), F32)] * 3,
        args=(zm, zm, zm, o, do, lse, fc, gq2, gk2, dz), aliases={9: 0})


def _mem_specs(t_, m_, mw, col0):
    nh = mw // LANE
    qcol = pl.BlockSpec((1, t_, LANE), lambda b, h: (b, 0, col0 + h))
    kcol = pl.BlockSpec((1, m_, LANE), lambda b, h: (b, 0, h))
    vcol = pl.BlockSpec((1, m_, LANE), lambda b, h: (b, 0, nh + h))
    ycol = pl.BlockSpec((1, t_, LANE), lambda b, h: (b, 0, h))
    gvec = pl.BlockSpec((1, LANE), lambda b, h: (0, 0))
    return qcol, kcol, vcol, ycol, gvec


def _mem_fwd(zm, mkv, gq, gk, mw, col0):
    b_, t_, _ = zm.shape
    m_ = mkv.shape[1]
    tq = min(512, t_)
    nb = t_ // tq
    scale = MEM_DH ** -0.5
    qcol, kcol, vcol, ycol, gvec = _mem_specs(t_, m_, mw, col0)

    def body(q_ref, k_ref, v_ref, gq_ref, gk_ref, y_ref):
        gqv, gkv = gq_ref[...] * scale, gk_ref[...]
        kv = k_ref[0]
        kn = _mx(kv * lax.rsqrt(jnp.mean(kv * kv, axis=-1, keepdims=True) + EPS) * gkv)
        vv = _mx(v_ref[0])

        def blk(i, _):
            rows = pl.ds(pl.multiple_of(i * tq, tq), tq)
            qv = q_ref[0, rows, :]
            s = _nt(qv * lax.rsqrt(jnp.mean(qv * qv, axis=-1, keepdims=True) + EPS) * gqv, kn)
            e = jnp.exp(s - jnp.max(s, axis=-1, keepdims=True))
            y_ref[0, rows, :] = _nn(e / jnp.sum(e, axis=-1, keepdims=True), vv)
            return 0

        lax.fori_loop(0, nb, blk, 0)

    return pl.pallas_call(
        body, name="mem_fwd", grid=(b_, MEM_HEADS), in_specs=[qcol, kcol, vcol, gvec, gvec], out_specs=ycol,
        out_shape=jax.ShapeDtypeStruct((b_, t_, mw), F32), compiler_params=_params(2),
    )(zm, mkv, mkv, gq, gk)


def _mem_bwd(zm, mkv, dy, gq, gk, mw, col0, dz):
    b_, t_, _ = zm.shape
    m_ = mkv.shape[1]
    tq = min(512, t_)
    nb = t_ // tq
    scale = MEM_DH ** -0.5
    qcol, kcol, vcol, ycol, gvec = _mem_specs(t_, m_, mw, col0)

    def body(q_ref, k_ref, v_ref, dy_ref, gq_ref, gk_ref, _, dq_ref, dk_ref, dv_ref, dgq_ref, dgk_ref):
        gqv, gkv = gq_ref[...] * scale, gk_ref[...]
        kv = k_ref[0]
        kn = _mx(kv * lax.rsqrt(jnp.mean(kv * kv, axis=-1, keepdims=True) + EPS) * gkv)
        vv = _mx(v_ref[0])

        def blk(i, carry):
            dkn, dvv, dgq = carry
            rows = pl.ds(pl.multiple_of(i * tq, tq), tq)
            qv = q_ref[0, rows, :]
            qn = _mx(qv * lax.rsqrt(jnp.mean(qv * qv, axis=-1, keepdims=True) + EPS) * gqv)
            s = _nt(qn, kn)
            e = jnp.exp(s - jnp.max(s, axis=-1, keepdims=True))
            pm = e / jnp.sum(e, axis=-1, keepdims=True)
            dob = _mx(dy_ref[0, rows, :])
            dp = _nt(dob, vv)
            ds = pm * (dp - jnp.sum(dp * pm, axis=-1, keepdims=True))
            dqv, gq_part = _norm_bwd(qv, _nn(ds, kn), gqv)
            dq_ref[0, rows, :] = dqv.astype(dq_ref.dtype)
            return dkn + _tn(ds, qn), dvv + _tn(pm, dob), dgq + gq_part * scale

        z = jnp.zeros((m_, LANE), F32)
        dkn, dvv, dgq = lax.fori_loop(0, nb, blk, (z, z, jnp.zeros((1, LANE), F32)))
        dkv, dgk = _norm_bwd(kv, dkn, gkv)
        dk_ref[0] = dkv
        dv_ref[0] = dvv
        first = jnp.logical_and(pl.program_id(0) == 0, pl.program_id(1) == 0)
        _acc(dgq_ref, dgq, first)
        _acc(dgk_ref, dgk, first)

    kblk = pl.BlockSpec((1, m_, LANE), lambda b, h: (b, 0, h))
    gs = jax.ShapeDtypeStruct((1, LANE), F32)
    ks = jax.ShapeDtypeStruct((b_, m_, mw), F32)
    return pl.pallas_call(
        body, name="mem_bwd", grid=(b_, MEM_HEADS), in_specs=[qcol, kcol, vcol, ycol, gvec, gvec, ANY],
        out_specs=[qcol, kblk, kblk, gvec, gvec],
        out_shape=[jax.ShapeDtypeStruct(dz.shape, dz.dtype), ks, ks, gs, gs], input_output_aliases={6: 0},
        compiler_params=_params(2),
    )(zm, mkv, mkv, dy, gq, gk, dz)


def _merge_specs(tm, d, w, gcol):
    row_d = pl.BlockSpec((tm, d), lambda i: (i, 0))
    row_w = pl.BlockSpec((tm, w), lambda i: (i, 0))
    gates = [pl.BlockSpec((tm, d), functools.partial(lambda i, k: (i, gcol + k), k=k)) for k in range(3)]
    w_br = pl.BlockSpec((w, d), lambda i: (0, 0))
    w_o = pl.BlockSpec((d, d), lambda i: (0, 0))
    return row_d, row_w, gates, w_br, w_o


def _merge_fwd(x, ys, zm, w_brs, w_out, gcol, g_next, tm=256):
    n, d = x.shape
    w = ys[0].shape[1]
    tm = _tile(n, tm, 8)
    row_d, row_w, gates, w_br, w_o = _merge_specs(tm, d, w, gcol)

    def body(x_ref, ya, yb, yc, g0, g1, g2, wa, wb, wc, wo, gn_ref, x1_ref, mg_ref, h_ref):
        mg = (_sig(g0[...]) * _nn(ya[...], wa[...]) + _sig(g1[...]) * _nn(yb[...], wb[...])
              + _sig(g2[...]) * _nn(yc[...], wc[...]))
        mg_ref[...] = mg.astype(mg_ref.dtype)
        x1 = x_ref[...] + _nn(mg, wo[...])
        x1_ref[...] = x1
        h_ref[...] = (x1 * lax.rsqrt(jnp.mean(x1 * x1, axis=-1, keepdims=True) + EPS) * gn_ref[...]).astype(h_ref.dtype)

    half = jax.ShapeDtypeStruct((n, d), MXU_DTYPE)
    return pl.pallas_call(
        body, name="merge_fwd", grid=(n // tm,),
        in_specs=[row_d, row_w, row_w, row_w] + gates + [w_br, w_br, w_br, w_o, pl.BlockSpec((1, d), lambda i: (0, 0))],
        out_specs=[row_d, row_d, row_d],
        out_shape=[jax.ShapeDtypeStruct((n, d), F32), half, half],
        compiler_params=_params(1),
    )(x, *ys, zm, zm, zm, *w_brs, w_out, g_next)


def _merge_bwd(dx1, ys, zm, w_brs, w_out, gcol, tm=256):
    n, d = dx1.shape
    w = ys[0].shape[1]
    tm = _tile(n, tm, 8)
    row_d, row_w, gates, w_br, w_o = _merge_specs(tm, d, w, gcol)

    def body(dx_ref, ya, yb, yc, g0, g1, g2, wa, wb, wc, wo, dgl_ref, dpa, dpb, dpc, dya, dyb, dyc):
        dm = _nt(dx_ref[...], wo[...])
        for k, (y, g, wr, dp_ref, dy_ref) in enumerate(((ya, g0, wa, dpa, dya), (yb, g1, wb, dpb, dyb),
                                                        (yc, g2, wc, dpc, dyc))):
            sg = _sig(g[...])
            pr = _nn(y[...], wr[...])
            dgl_ref[:, k * d:(k + 1) * d] = (dm * pr * sg * (1.0 - sg)).astype(dgl_ref.dtype)
            dp = (dm * sg).astype(dp_ref.dtype)
            dp_ref[...] = dp
            dy_ref[...] = _nt(dp, wr[...])

    sd = jax.ShapeDtypeStruct((n, d), MXU_DTYPE)
    sw = jax.ShapeDtypeStruct((n, w), F32)
    return pl.pallas_call(
        body, name="merge_bwd", grid=(n // tm,),
        in_specs=[row_d, row_w, row_w, row_w] + gates + [w_br, w_br, w_br, w_o],
        out_specs=[pl.BlockSpec((tm, 3 * d), lambda i: (i, 0)), row_d, row_d, row_d, row_w, row_w, row_w],
        out_shape=[jax.ShapeDtypeStruct((n, zm.shape[1]), MXU_DTYPE), sd, sd, sd, sw, sw, sw],
        compiler_params=_params(1),
    )(dx1, *ys, zm, zm, zm, *w_brs, w_out)


CONV_ROWS = 512
HALO = 8


def _ext(ref, r0, t_):
    rc = min(CONV_ROWS, t_)
    a, b = max(r0 - HALO, 0), min(r0 + rc + HALO, t_)
    parts = []
    if r0 - HALO < 0:
        parts.append(jnp.zeros((HALO, ref.shape[2]), F32))
    parts.append(ref[0, a:b, :].astype(F32))
    if r0 + rc + HALO > t_:
        parts.append(jnp.zeros((HALO, ref.shape[2]), F32))
    return jnp.concatenate(parts, axis=0) if len(parts) > 1 else parts[0]


def _gelu_parts(ac):
    e = jnp.exp(-0.5 * ac * ac)
    t = 1.0 / (1.0 + (0.3275911 * 2.0 ** -0.5) * jnp.abs(ac))
    tail = (0.5 * e) * (t * (0.254829592 + t * (-0.284496736 + t * (1.421413741 + t * (-1.453152027 + t * 1.061405429)))))
    return jnp.where(ac < 0, tail, 1.0 - tail), e * ((2.0 * math.pi) ** -0.5)


def _conv_taps(a_ext, cw, cb):
    a2, a1 = pltpu.roll(a_ext, 2, 0), pltpu.roll(a_ext, 1, 0)
    return cw[0:1, :] * a2 + cw[1:2, :] * a1 + cw[2:3, :] * a_ext + cb, a2, a1


def _glu_specs(t_, f, g):
    gate = pl.BlockSpec((1, t_, g), lambda j, b: (b, 0, j))
    value = pl.BlockSpec((1, t_, g), lambda j, b: (b, 0, f // g + j))
    cwb = pl.BlockSpec((3, g), lambda j, b: (0, j))
    cbb = pl.BlockSpec((1, g), lambda j, b: (0, j))
    return gate, value, cwb, cbb


def _glu_fwd(u, cw, cb):
    b_, t_, f2 = u.shape
    f = f2 // 2
    g = min(FFN_GROUP, f)
    rc = min(CONV_ROWS, t_)
    gate, value, cwb, cbb = _glu_specs(t_, f, g)

    def body(a_ref, v_ref, cw_ref, cb_ref, y_ref):
        cwv, cbv = cw_ref[...], cb_ref[...]
        for r0 in range(0, t_, rc):
            ac = _conv_taps(_ext(a_ref, r0, t_), cwv, cbv)[0][HALO:HALO + rc]
            cdf, _ = _gelu_parts(ac)
            y_ref[0, r0:r0 + rc, :] = (ac * cdf * v_ref[0, r0:r0 + rc, :]).astype(y_ref.dtype)

    return pl.pallas_call(
        body, name="glu_fwd", grid=(f // g, b_), in_specs=[gate, value, cwb, cbb], out_specs=gate,
        out_shape=jax.ShapeDtypeStruct((b_, t_, f), MXU_DTYPE), compiler_params=_params(2),
    )(u, u, cw, cb)


def _glu_bwd(u, dy, cw, cb):
    b_, t_, f2 = u.shape
    f = f2 // 2
    g = min(FFN_GROUP, f)
    rc = min(CONV_ROWS, t_)
    ne = rc + 2 * HALO
    gate, value, cwb, cbb = _glu_specs(t_, f, g)

    def body(a_ref, v_ref, dy_ref, cw_ref, cb_ref, da_ref, dv_ref, dcw_ref, dcb_ref):
        cwv, cbv = cw_ref[...], cb_ref[...]
        dcw = [jnp.zeros((1, g), F32) for _ in range(3)]
        dcb = jnp.zeros((1, g), F32)
        for r0 in range(0, t_, rc):
            a_ext, v_ext, dy_ext = _ext(a_ref, r0, t_), _ext(v_ref, r0, t_), _ext(dy_ref, r0, t_)
            ac, a2, a1 = _conv_taps(a_ext, cwv, cbv)
            cdf, pdf = _gelu_parts(ac)
            dac = dy_ext * v_ext * (cdf + ac * pdf)
            da = cwv[2:3, :] * dac + cwv[1:2, :] * pltpu.roll(dac, ne - 1, 0) + cwv[0:1, :] * pltpu.roll(dac, ne - 2, 0)
            mid = slice(HALO, HALO + rc)
            da_ref[0, r0:r0 + rc, :] = da[mid].astype(da_ref.dtype)
            dv_ref[0, r0:r0 + rc, :] = (dy_ext[mid] * ac[mid] * cdf[mid]).astype(dv_ref.dtype)
            dacm = dac[mid]
            dcw[0] = dcw[0] + jnp.sum(dacm * a2[mid], axis=0, keepdims=True)
            dcw[1] = dcw[1] + jnp.sum(dacm * a1[mid], axis=0, keepdims=True)
            dcw[2] = dcw[2] + jnp.sum(dacm * a_ext[mid], axis=0, keepdims=True)
            dcb = dcb + jnp.sum(dacm, axis=0, keepdims=True)
        first = pl.program_id(1) == 0
        _acc(dcw_ref, jnp.concatenate(dcw, axis=0), first)
        _acc(dcb_ref, dcb, first)

    sds = jax.ShapeDtypeStruct((b_, t_, f), MXU_DTYPE)
    return pl.pallas_call(
        body, name="glu_bwd", grid=(f // g, b_), in_specs=[gate, value, gate, cwb, cbb],
        out_specs=[gate, gate, cwb, cbb],
        out_shape=[sds, sds, jax.ShapeDtypeStruct((3, f), F32), jax.ShapeDtypeStruct((1, f), F32)],
        compiler_params=_params(2),
    )(u, u, dy, cw, cb)


def _place():
    x, y, c = lax.axis_index("x"), lax.axis_index("y"), lax.axis_index("c")
    chips = [(1 - x, y), (x, 1 - y), (1 - x, 1 - y)]
    return x, y, c, chips


def _remote(src, dst, send_sem, recv_sem, to):
    return pltpu.make_async_remote_copy(src_ref=src, dst_ref=dst, send_sem=send_sem, recv_sem=recv_sem,
                                        device_id=to, device_id_type=MESH)


STACK, COLS = "stack", "cols"


def _shard_ref(ref, kind, s, rows, c):
    if kind == COLS:
        cols = pl.ds(pl.multiple_of(s * c, LANE), c)
        return ref.at[:, cols] if rows is None else ref.at[rows, cols]
    return ref.at[s] if rows is None else ref.at[s, rows, :]


def _halves(c, half):
    mine = pl.ds(pl.multiple_of(c * half, 16), half)
    theirs = pl.ds(pl.multiple_of((1 - c) * half, 16), half)
    return mine, theirs


def _gather_parts(kinds):
    def first_copies(ins, outs, sems):
        x, y, c, chips = _place()
        me = 2 * x + y
        cps = []
        for i, (w_ref, o_ref, kind) in enumerate(zip(ins, outs, kinds)):
            r, cw = w_ref.shape
            mine, _ = _halves(c, r // 2)
            for j, chip in enumerate(chips):
                cps.append(_remote(w_ref.at[mine], _shard_ref(o_ref, kind, me, mine, cw), sems[0].at[6 * i + j],
                                   sems[1].at[6 * i + j], (*chip, c)))
        return cps

    def start(ins, outs, sems):
        for cp in first_copies(ins, outs, sems):
            cp.start()

    def finish(ins, outs, sems):
        x, y, c, chips = _place()
        sib = (x, y, 1 - c)
        passed = []
        for i, (w_ref, o_ref, kind) in enumerate(zip(ins, outs, kinds)):
            r, cw = w_ref.shape
            mine, _ = _halves(c, r // 2)
            for j, (px, py) in enumerate(chips):
                blk = _shard_ref(o_ref, kind, 2 * px + py, mine, cw)
                _remote(blk, blk, sems[0].at[6 * i + j], sems[1].at[6 * i + j], sib).wait_recv()
                passed.append(_remote(blk, blk, sems[0].at[6 * i + 3 + j], sems[1].at[6 * i + 3 + j], sib))
                passed[-1].start()
        for i, (w_ref, o_ref, kind) in enumerate(zip(ins, outs, kinds)):
            r, cw = w_ref.shape
            _, theirs = _halves(c, r // 2)
            for j, (px, py) in enumerate(chips):
                blk = _shard_ref(o_ref, kind, 2 * px + py, theirs, cw)
                _remote(blk, blk, sems[0].at[6 * i + 3 + j], sems[1].at[6 * i + 3 + j], sib).wait_recv()
        for cp in first_copies(ins, outs, sems) + passed:
            cp.wait_send()

    return start, finish


def _gather_shapes(shards, kinds):
    return [jax.ShapeDtypeStruct((a.shape[0], N_CHIPS * a.shape[1]) if k == COLS else (N_CHIPS,) + a.shape, a.dtype)
            for a, k in zip(shards, kinds)]


def _gather_sems(nw):
    return [pltpu.SemaphoreType.DMA((6 * nw,)), pltpu.SemaphoreType.DMA((6 * nw,))]


def _gather_shards(shards, kinds):
    nw = len(shards)
    start, finish = _gather_parts(kinds)

    def body(*refs):
        ins, outs, sems = refs[:nw], refs[nw:2 * nw], refs[2 * nw:]
        start(ins, outs, sems)
        finish(ins, outs, sems)

    return pl.pallas_call(
        body, name="gather_shards", in_specs=[ANY] * nw, out_specs=[ANY] * nw,
        out_shape=_gather_shapes(shards, kinds), scratch_shapes=_gather_sems(nw),
    )(*shards)


def _gather_rider(shards, kinds):
    start, finish = _gather_parts(kinds)
    return _Rider(list(shards), _gather_shapes(shards, kinds), _gather_sems(len(shards)), start, finish)


def _half_shape(g, kind):
    if kind == COLS:
        return (g.shape[0] // 2, g.shape[1])
    return (g.shape[0], g.shape[1] // 2, g.shape[2])


def _swap_parts(kinds):
    def copies(ins, outs, sems):
        x, y, c, _ = _place()
        cps = []
        for i, (g_ref, a_ref, kind) in enumerate(zip(ins, outs, kinds)):
            r = g_ref.shape[0] if kind == COLS else g_ref.shape[1]
            _, theirs = _halves(c, r // 2)
            src = g_ref.at[theirs] if kind == COLS else g_ref.at[:, theirs]
            cps.append(_remote(src, a_ref, sems[0].at[i], sems[1].at[i], (x, y, 1 - c)))
        return cps

    def start(ins, outs, sems):
        for cp in copies(ins, outs, sems):
            cp.start()

    def finish(ins, outs, sems):
        for cp in copies(ins, outs, sems):
            cp.wait()

    return start, finish


def _swap_shapes(gs, kinds):
    return [jax.ShapeDtypeStruct(_half_shape(g, k), g.dtype) for g, k in zip(gs, kinds)]


def _pair_swap_halves(gs, kinds, name):
    nw = len(gs)
    start, finish = _swap_parts(kinds)

    def body(*refs):
        ins, outs, sems = refs[:nw], refs[nw:2 * nw], refs[2 * nw:]
        start(ins, outs, sems)
        finish(ins, outs, sems)

    return pl.pallas_call(
        body, name=name, in_specs=[ANY] * nw, out_specs=[ANY] * nw, out_shape=_swap_shapes(gs, kinds),
        scratch_shapes=[pltpu.SemaphoreType.DMA((nw,)), pltpu.SemaphoreType.DMA((nw,))],
    )(*gs)


def _swap_rider(gs, kinds):
    start, finish = _swap_parts(kinds)
    nw = len(gs)
    return _Rider(list(gs), _swap_shapes(gs, kinds), [pltpu.SemaphoreType.DMA((nw,)), pltpu.SemaphoreType.DMA((nw,))],
                  start, finish)


def _row_tile(rows, width, itemsize=4, target=2 ** 21):
    return _tile(rows, max(8, target // (width * itemsize)), 8)


def _add_half(g, a, kind, c_idx, name):
    if kind == COLS:
        half, wd = a.shape
        tr = _row_tile(half, wd)
        nblk = half // tr
        grid = (nblk,)
        g_spec = pl.BlockSpec((tr, wd), lambda i, c_ref: (c_ref[0] * nblk + i, 0))
        a_spec = pl.BlockSpec((tr, wd), lambda i, c_ref: (i, 0))
    else:
        n, half, wd = a.shape
        tr = _row_tile(half, wd)
        nblk = half // tr
        grid = (n, nblk)
        g_spec = pl.BlockSpec((1, tr, wd), lambda s, i, c_ref: (s, c_ref[0] * nblk + i, 0))
        a_spec = pl.BlockSpec((1, tr, wd), lambda s, i, c_ref: (s, i, 0))

    def body(c_ref, g_ref, a_ref, o_ref):
        o_ref[...] = (g_ref[...] + a_ref[...]).astype(o_ref.dtype)

    return pl.pallas_call(
        body, name=name,
        grid_spec=pltpu.PrefetchScalarGridSpec(num_scalar_prefetch=1, grid=grid, in_specs=[g_spec, a_spec],
                                               out_specs=a_spec),
        out_shape=jax.ShapeDtypeStruct(a.shape, EXCHANGE_DTYPE), compiler_params=_params(len(grid)),
    )(c_idx, g, a)


def _exchange_parts(kinds):
    def copies(ins, outs, sems):
        x, y, c, chips = _place()
        me = 2 * x + y
        cps = []
        for i, (p_ref, b_ref, kind) in enumerate(zip(ins, outs, kinds)):
            cw = b_ref.shape[2]
            for j, (px, py) in enumerate(chips):
                cps.append(_remote(_shard_ref(p_ref, kind, 2 * px + py, None, cw), b_ref.at[me],
                                   sems[0].at[3 * i + j], sems[1].at[3 * i + j], (px, py, c)))
        return cps

    def start(ins, outs, sems):
        for cp in copies(ins, outs, sems):
            cp.start()

    def finish(ins, outs, sems):
        x, y, c, chips = _place()
        for i, b_ref in enumerate(outs):
            for j, (px, py) in enumerate(chips):
                blk = b_ref.at[2 * px + py]
                _remote(blk, blk, sems[0].at[3 * i + j], sems[1].at[3 * i + j], (px, py, c)).wait_recv()
        for cp in copies(ins, outs, sems):
            cp.wait_send()

    return start, finish


def _exchange_shapes(ps, kinds):
    return [jax.ShapeDtypeStruct((N_CHIPS,) + ((p.shape[0], p.shape[1] // N_CHIPS) if k == COLS else tuple(p.shape[1:])),
                                 p.dtype) for p, k in zip(ps, kinds)]


def _exchange_sems(nw):
    return [pltpu.SemaphoreType.DMA((3 * nw,)), pltpu.SemaphoreType.DMA((3 * nw,))]


def _exchange_rider(ps, kinds):
    start, finish = _exchange_parts(kinds)
    return _Rider(list(ps), _exchange_shapes(ps, kinds), _exchange_sems(len(ps)), start, finish)


def _sum_chips(bq, name):
    n, h, wd = bq.shape
    tr = _row_tile(h, wd * n)

    def body(b_ref, o_ref):
        acc = b_ref[0].astype(F32)
        for s in range(1, n):
            acc = acc + b_ref[s].astype(F32)
        o_ref[...] = acc

    return pl.pallas_call(
        body, name=name, grid=(h // tr,),
        in_specs=[pl.BlockSpec((n, tr, wd), lambda i: (0, i, 0))], out_specs=pl.BlockSpec((tr, wd), lambda i: (i, 0)),
        out_shape=jax.ShapeDtypeStruct((h, wd), F32), compiler_params=_params(1),
    )(bq)


def _pair_join_halves(qs):
    nw = len(qs)

    def body(*refs):
        ins, outs = refs[:nw], refs[nw:2 * nw]
        send_sems, recv_sems = refs[2 * nw:]
        x, y, c, _ = _place()
        sent = []
        for i, (q_ref, o_ref) in enumerate(zip(ins, outs)):
            mine, _ = _halves(c, q_ref.shape[0])
            sent.append(_remote(q_ref, o_ref.at[mine], send_sems.at[i], recv_sems.at[i], (x, y, 1 - c)))
            sent[-1].start()
        for i, (q_ref, o_ref) in enumerate(zip(ins, outs)):
            _, theirs = _halves(c, q_ref.shape[0])
            _remote(q_ref, o_ref.at[theirs], send_sems.at[i], recv_sems.at[i], (x, y, 1 - c)).wait_recv()
        for cp in sent:
            cp.wait_send()

    return pl.pallas_call(
        body, name="pair_join_halves", in_specs=[ANY] * nw, out_specs=[ANY] * nw,
        out_shape=[jax.ShapeDtypeStruct((2 * q.shape[0], q.shape[1]), q.dtype) for q in qs],
        scratch_shapes=[pltpu.SemaphoreType.DMA((nw,)), pltpu.SemaphoreType.DMA((nw,))],
    )(*qs)


def _all_sum_small(s, name):
    sr, w = s.shape

    def body(s_ref, o_ref, buf, send_sems, recv_sems):
        x, y, c, _ = _place()
        me = 4 * x + 2 * y + c
        buf[me] = s_ref[...]
        peers = []
        for k in range(1, 8):
            px = 1 - x if k & 4 else x
            py = 1 - y if k & 2 else y
            pc = 1 - c if k & 1 else c
            peers.append((px, py, pc))
        sent = [_remote(s_ref, buf.at[me], send_sems.at[k], recv_sems.at[k], peer) for k, peer in enumerate(peers)]
        for cp in sent:
            cp.start()
        for k, (px, py, pc) in enumerate(peers):
            _remote(s_ref, buf.at[4 * px + 2 * py + pc], send_sems.at[k], recv_sems.at[k], (px, py, pc)).wait_recv()
        for cp in sent:
            cp.wait_send()
        acc = buf[0]
        for d in range(1, 8):
            acc = acc + buf[d]
        o_ref[...] = acc

    vm = pl.BlockSpec(memory_space=pltpu.VMEM)
    return pl.pallas_call(
        body, name=name, in_specs=[vm], out_specs=vm, out_shape=jax.ShapeDtypeStruct((sr, w), F32),
        scratch_shapes=[pltpu.VMEM((8, sr, w), F32), pltpu.SemaphoreType.DMA((7,)), pltpu.SemaphoreType.DMA((7,))],
    )(s)


BIG = ("w_in", "mem_kv_w", "w_br_hgrn", "w_br_fox", "w_br_mem", "w_out", "ffn_w_up", "ffn_w_down")
KIND = {"w_in": STACK, "mem_kv_w": STACK, "w_br_hgrn": COLS, "w_br_fox": COLS, "w_br_mem": COLS, "w_out": STACK,
        "ffn_w_up": STACK, "ffn_w_down": STACK}
ROW_SHARDED = ("mem_kv_w", "w_out", "ffn_w_down")
FIRST = ("w_in",)
REST = tuple(nm for nm in BIG if nm not in FIRST)
LATE = {"in_proj": tuple(nm for nm in REST if not nm.startswith("ffn_")),
        "fox_fwd": tuple(nm for nm in REST if nm.startswith("ffn_"))}
LAST = ("w_in",)
TRANSPOSED = ("w_in",)


def _z_layout(d, hw, fw, mw):
    gate, npair, nh, nm = 3 * d // LANE, fw // LANE, hw // LANE, mw // LANE
    fox0, hg0 = gate, gate + 3 * npair
    o_fox, o_mem = 4 * nh, 4 * nh + 3 * npair
    order = [o_mem + nm + j for j in range(gate)]
    order += [o_fox + k * npair + p for p in range(npair) for k in range(3)]
    order += [k * nh + h for h in range(nh) for k in range(4)]
    order += [o_mem + h for h in range(nm)]
    assert fox0 % 3 == 0 and hg0 % 4 == 0
    return fox0, hg0, hg0 + 4 * nh, order


def _reorder_blocks(a, order):
    runs, start = [], 0
    for i in range(1, len(order) + 1):
        if i == len(order) or order[i] != order[i - 1] + 1:
            runs.append((order[start], order[i - 1] + 1))
            start = i
    return jnp.concatenate([a[:, lo * LANE:hi * LANE] for lo, hi in runs], axis=1)


def _put_shard(arr, kind, s, piece):
    if kind == COLS:
        return lax.dynamic_update_slice(arr, piece, (0, s * piece.shape[1]))
    return lax.dynamic_update_slice(arr, piece[None], (s, 0, 0))


def _take_shard(arr, kind, s):
    if kind == COLS:
        return lax.dynamic_slice(arr, (0, s * (arr.shape[1] // N_CHIPS)), (arr.shape[0], arr.shape[1] // N_CHIPS))
    return lax.dynamic_index_in_dim(arr, s, 0, keepdims=False)


def _w_in_pieces(cs, s1, nf):
    out = []
    for s in range(N_CHIPS):
        lo, hi = cs * s, cs * (s + 1)
        for a, b, forget in ((lo, min(hi, s1), False), (max(lo, s1), min(hi, s1 + nf), True), (max(lo, s1 + nf), hi, False)):
            if a < b:
                out.append((s, a - lo, b - lo, forget, a - s1 if forget else (a if a < s1 else a - nf)))
    return out


def _split_w_in(stacked, s1, nf):
    pieces = _w_in_pieces(stacked.shape[2], s1, nf)
    main = [stacked[s, :, a:b] for s, a, b, forget, _ in pieces if not forget]
    ff = [stacked[s, :, a:b] for s, a, b, forget, _ in pieces if forget]
    return jnp.concatenate(main, axis=1), jnp.concatenate(ff, axis=1)


def _join_w_in(g_main, g_ff, s1, nf):
    cs = (g_main.shape[1] + nf) // N_CHIPS
    shards = [[] for _ in range(N_CHIPS)]
    for s, a, b, forget, off in _w_in_pieces(cs, s1, nf):
        shards[s].append((g_ff if forget else g_main)[:, off:off + b - a])
    return jnp.stack([jnp.concatenate(p, axis=1) if len(p) > 1 else p[0] for p in shards])


SMALL = ("norm_mix_g", "norm_mem_g", "norm_ffn_g", "hgrn_lb_logits", "hgrn_norm_g", "fox_f_bias", "fox_q_norm_g",
         "fox_k_norm_g", "mem_q_norm_g", "mem_k_norm_g", "ffn_conv_b")


def _small_rows(shapes):
    rows = []
    for a, (r, c) in enumerate(shapes):
        for i in range(r):
            for lo in range(0, c, FLAT_W):
                rows.append((a, i, lo, min(FLAT_W, c - lo)))
    return rows


def _pack_small(vals):
    rows = _small_rows([v.shape for v in vals])
    sr = -(-len(rows) // 8) * 8

    def body(*refs):
        o_ref = refs[-1]
        o_ref[...] = jnp.zeros(o_ref.shape, F32)
        for k, (a, i, lo, wd) in enumerate(rows):
            o_ref[k:k + 1, 0:wd] = refs[a][i:i + 1, lo:lo + wd]

    vm = pl.BlockSpec(memory_space=pltpu.VMEM)
    return pl.pallas_call(body, name="pack_small", in_specs=[vm] * len(vals), out_specs=vm,
                          out_shape=jax.ShapeDtypeStruct((sr, FLAT_W), F32))(*vals)


def _row_of(buf_ref, rows, a, i):
    parts = [buf_ref[k:k + 1, 0:wd] for k, (a2, i2, _, wd) in enumerate(rows) if (a2, i2) == (a, i)]
    return jnp.concatenate(parts, axis=1) if len(parts) > 1 else parts[0]


def _unpack_small(buf, shapes):
    rows = _small_rows(shapes)

    def body(buf_ref, *outs):
        for a, (r, _) in enumerate(shapes):
            for i in range(r):
                outs[a][i:i + 1, :] = _row_of(buf_ref, rows, a, i)

    vm = pl.BlockSpec(memory_space=pltpu.VMEM)
    return pl.pallas_call(body, name="unpack_small", in_specs=[vm], out_specs=[vm] * len(shapes),
                          out_shape=[jax.ShapeDtypeStruct(shp, F32) for shp in shapes])(buf)


def _adamw_small(buf, shapes, ws, ms, vs):
    n = len(ws)
    rows = _small_rows(shapes)
    c1 = 1.0 / (1.0 - ADAM_B1 ** ADAM_STEP)
    c2 = 1.0 / (1.0 - ADAM_B2 ** ADAM_STEP)

    def body(buf_ref, *refs):
        w_refs, m_refs, v_refs = refs[:n], refs[n:2 * n], refs[2 * n:3 * n]
        outs = refs[3 * n:]
        g_out, d_out, m_out, v_out, rest = outs[:n], outs[n:2 * n], outs[2 * n:3 * n], outs[3 * n:4 * n], outs[4 * n:]
        for a, (r, _) in enumerate(shapes):
            for i in range(r):
                gv = _row_of(buf_ref, rows, a, i)
                if a >= n:
                    rest[a - n][i:i + 1, :] = gv
                    continue
                row = slice(i, i + 1)
                mn = ADAM_B1 * m_refs[a][row, :] + (1.0 - ADAM_B1) * gv
                vn = ADAM_B2 * v_refs[a][row, :] + (1.0 - ADAM_B2) * (gv * gv)
                g_out[a][row, :] = gv
                d_out[a][row, :] = -ADAM_LR * ((mn * c1) / (jnp.sqrt(vn * c2) + ADAM_EPS) + ADAM_WD * w_refs[a][row, :])
                m_out[a][row, :] = mn
                v_out[a][row, :] = vn

    vm = pl.BlockSpec(memory_space=pltpu.VMEM)
    own = [jax.ShapeDtypeStruct(shp, F32) for shp in shapes[:n]]
    outs = pl.pallas_call(
        body, name="adamw_small", in_specs=[vm] * (1 + 3 * n), out_specs=[vm] * (4 * n + len(shapes) - n),
        out_shape=own * 4 + [jax.ShapeDtypeStruct(shp, F32) for shp in shapes[n:]],
    )(buf, *ws, *ms, *vs)
    return outs[:n], outs[n:2 * n], outs[2 * n:3 * n], outs[3 * n:4 * n], outs[4 * n:]


def _pad_lanes(v, width=LANE):
    return jnp.pad(v, ((0, 0), (0, width - v.shape[1])))


WEIGHTS = ("norm_mix_g", "norm_mem_g", "w_in", "hgrn_lb_logits", "hgrn_norm_g", "fox_f_bias", "fox_q_norm_g",
           "fox_k_norm_g", "mem_kv_w", "mem_q_norm_g", "mem_k_norm_g", "w_br_hgrn", "w_br_fox", "w_br_mem", "w_out",
           "norm_ffn_g", "ffn_w_up", "ffn_conv_w", "ffn_conv_b", "ffn_w_down")


def _local_step(x, mem, target, w, full, conv_w, late=None, hooks=None):
    b_, t_, d = x.shape
    n = b_ * t_
    hw, fw, mw = HG_HEADS * HG_D, FOX_HEADS * FOX_DH, MEM_HEADS * MEM_DH
    m_ = mem.shape[1]
    f = conv_w.shape[1]
    s1 = 4 * hw + 3 * fw
    fox_col, hg_col, mem_col, order = _z_layout(d, hw, fw, mw)
    gate_col = 0
    inverse = [order.index(j) for j in range(len(order))]

    w_main, w_ff = _split_w_in(full["w_in"], s1, FOX_HEADS)
    w_main = _reorder_blocks(w_main, order)
    w_ff = _pad_lanes(w_ff)
    f_bias = _pad_lanes(w["fox_f_bias"])
    cb = w["ffn_conv_b"]

    x2 = x.reshape(n, d)
    h = _rmsnorm_fwd(x2, w["norm_mix_g"], name="norm_mix_fwd")
    if late:
        pieces, kinds, finish = late["in_proj"]
        zm, gathered = _matmul(h, w_main, name="in_proj", rider=_gather_rider(pieces, kinds))
        full = {**full, **finish(gathered)}
    else:
        zm = _matmul(h, w_main, name="in_proj")
    w_brs = [full["w_br_hgrn"], full["w_br_fox"], full["w_br_mem"]]
    w_out, w_kv = full["w_out"], full["mem_kv_w"]
    zf = _matmul(h, w_ff, name="in_proj_forget")
    zm3, zf3 = zm.reshape(b_, t_, -1), zf.reshape(b_, t_, LANE)
    ya = _hgrn_fwd(zm3, w["hgrn_lb_logits"], w["hgrn_norm_g"], hw, hg_col)
    fc = _fox_prep(zf3, f_bias)
    fox_gq, fox_gk = jnp.tile(w["fox_q_norm_g"], (1, 2)), jnp.tile(w["fox_k_norm_g"], (1, 2))
    if late:
        pieces, kinds, finish = late["fox_fwd"]
        (yb, lse), gathered = _fox_fwd(zm3, fc, fox_gq, fox_gk, fw, fox_col, _gather_rider(pieces, kinds))
        full = {**full, **finish(gathered)}
    else:
        yb, lse = _fox_fwd(zm3, fc, fox_gq, fox_gk, fw, fox_col)[0]
    w_up, w_down = full["ffn_w_up"], full["ffn_w_down"]
    mem2 = mem.reshape(b_ * m_, d)
    hm = _rmsnorm_fwd(mem2, w["norm_mem_g"], name="norm_mem_fwd")
    mkv = _matmul(hm, w_kv, name="mem_kv_proj").reshape(b_, m_, 2 * mw)
    yc = _mem_fwd(zm3, mkv, w["mem_q_norm_g"], w["mem_k_norm_g"], mw, mem_col)
    ys = [ya.reshape(n, hw), yb.reshape(n, fw), yc.reshape(n, mw)]
    x1, merged, h2 = _merge_fwd(x2, ys, zm, w_brs, w_out, gate_col, w["norm_ffn_g"])
    u = _matmul(h2, w_up, name="ffn_up")
    u3 = u.reshape(b_, t_, 2 * f)
    yff = _glu_fwd(u3, conv_w, cb).reshape(n, f)
    dy, (loss_vec,), _ = _matmul_rows([yff], w_down, name="ffn_down_loss", tb=False, row_ins=[x1, target.reshape(n, d)],
                                      vec_ins=[], epilogue=_loss_epilogue, n_vec_out=1)

    grads = {}

    def ridden(name, call):
        if not hooks or name not in hooks:
            return call(None)[0]
        rider, then = hooks[name](grads)
        outs, extra = call(rider)
        then(extra)
        return outs

    dyff = _matmul(dy, w_down, tb=True, name="ffn_down_dx")
    grads["ffn_w_down"] = _matmul(yff, dy, ta=True, name="ffn_down_dw", tm=1408)
    du_a, du_v, grads["ffn_conv_w"], grads["ffn_conv_b"] = _glu_bwd(u3, dyff.reshape(b_, t_, f), conv_w, cb)
    du_a, du_v = du_a.reshape(n, f), du_v.reshape(n, f)
    dx1, (grads["norm_ffn_g"],), _ = _matmul_rows(
        [du_a, du_v], w_up, name="ffn_up_dx", tb=True, row_ins=[x1, dy], vec_ins=[w["norm_ffn_g"]],
        epilogue=_norm_bwd_epilogue(0), n_vec_out=1)
    grads["ffn_w_up"] = _matmul(h2, None, ta=True, name="ffn_up_dw", b_parts=[du_a, du_v], tn=f // 2, stack_out=True)

    dz, dpa, dpb, dpc, dya, dyb, dyc = _merge_bwd(dx1, ys, zm, w_brs, w_out, gate_col)
    dz = dz.reshape(b_, t_, -1)
    grads["w_out"] = _matmul(merged, dx1, ta=True, name="out_proj_dw")
    for nm, y_, dp_ in zip(("w_br_hgrn", "w_br_fox", "w_br_mem"), ys, (dpa, dpb, dpc)):
        grads[nm] = _matmul(y_, dp_, ta=True, name=nm + "_dw")

    dz, dmk, dmv, grads["mem_q_norm_g"], grads["mem_k_norm_g"] = _mem_bwd(
        zm3, mkv, dyc.reshape(b_, t_, mw), w["mem_q_norm_g"], w["mem_k_norm_g"], mw, mem_col, dz)
    dmkv = jnp.concatenate([dmk, dmv], axis=-1).reshape(b_ * m_, 2 * mw)
    grads["mem_kv_w"] = _matmul(hm, dmkv, ta=True, name="mem_kv_dw")
    dhm = _matmul(dmkv, w_kv, tb=True, name="mem_kv_dx")
    _, grads["norm_mem_g"] = _rmsnorm_bwd(mem2, [dhm], w["norm_mem_g"], None, name="norm_mem_bwd")

    dz, dfc, g_fq, g_fk = ridden("fox_bwd", lambda rider: _fox_bwd(
        zm3, yb, dyb.reshape(b_, t_, fw), lse, fc, fox_gq, fox_gk, fw, fox_col, dz, rider))
    grads["fox_q_norm_g"] = g_fq[:, :FOX_DH] + g_fq[:, FOX_DH:]
    grads["fox_k_norm_g"] = g_fk[:, :FOX_DH] + g_fk[:, FOX_DH:]
    dzf, g_fb = _fox_post(dfc, zf3, f_bias)
    grads["fox_f_bias"] = g_fb[:, :FOX_HEADS]

    dz, grads["hgrn_lb_logits"], grads["hgrn_norm_g"] = ridden("hgrn_bwd", lambda rider: _hgrn_bwd(
        zm3, dya.reshape(b_, t_, hw), w["hgrn_lb_logits"], w["hgrn_norm_g"], hw, hg_col, dz, rider))
    dzm = dz.reshape(n, -1)
    dzf2 = dzf.reshape(n, LANE)
    g_main = _matmul(h, dzm, ta=True, name="in_proj_dw")
    g_ff = _matmul(h, dzf2, ta=True, name="in_proj_forget_dw")
    grads["w_in"] = _join_w_in(_reorder_blocks(g_main, inverse), g_ff[:, :FOX_HEADS], s1, FOX_HEADS)

    dh_b = _matmul(dzf2, w_ff, tb=True, name="in_proj_forget_dx")

    def in_proj_dx(rider):
        out = _matmul(dzm, w_main, tb=True, name="in_proj_dx", rider=rider)
        return ([out[0]], out[1]) if rider else ([out], None)

    dh_a, = ridden("in_proj_dx", in_proj_dx)
    grad_x, grads["norm_mix_g"] = _rmsnorm_bwd(x2, [dh_a, dh_b], w["norm_mix_g"], dx1, name="norm_mix_bwd")
    return loss_vec, grad_x.reshape(b_, t_, d), grads


def kernel(x, mem, norm_mix_g, norm_mem_g, w_in, hgrn_lb_logits, hgrn_norm_g, fox_f_bias, fox_q_norm_g, fox_k_norm_g, mem_kv_w, mem_q_norm_g, mem_k_norm_g, w_br_hgrn, w_br_fox, w_br_mem, w_out, norm_ffn_g, ffn_w_up, ffn_conv_w, ffn_conv_b, ffn_w_down, loss_target, m_norm_mix_g, m_norm_mem_g, m_w_in, m_hgrn_lb_logits, m_hgrn_norm_g, m_fox_f_bias, m_fox_q_norm_g, m_fox_k_norm_g, m_mem_kv_w, m_mem_q_norm_g, m_mem_k_norm_g, m_w_br_hgrn, m_w_br_fox, m_w_br_mem, m_w_out, m_norm_ffn_g, m_ffn_w_up, m_ffn_conv_w, m_ffn_conv_b, m_ffn_w_down, v_norm_mix_g, v_norm_mem_g, v_w_in, v_hgrn_lb_logits, v_hgrn_norm_g, v_fox_f_bias, v_fox_q_norm_g, v_fox_k_norm_g, v_mem_kv_w, v_mem_q_norm_g, v_mem_k_norm_g, v_w_br_hgrn, v_w_br_fox, v_w_br_mem, v_w_out, v_norm_ffn_g, v_ffn_w_up, v_ffn_conv_w, v_ffn_conv_b, v_ffn_w_down):
    w = dict(zip(WEIGHTS, (norm_mix_g, norm_mem_g, w_in, hgrn_lb_logits, hgrn_norm_g, fox_f_bias, fox_q_norm_g,
                           fox_k_norm_g, mem_kv_w, mem_q_norm_g, mem_k_norm_g, w_br_hgrn, w_br_fox, w_br_mem, w_out,
                           norm_ffn_g, ffn_w_up, ffn_conv_w, ffn_conv_b, ffn_w_down)))
    m = dict(zip(WEIGHTS, (m_norm_mix_g, m_norm_mem_g, m_w_in, m_hgrn_lb_logits, m_hgrn_norm_g, m_fox_f_bias,
                           m_fox_q_norm_g, m_fox_k_norm_g, m_mem_kv_w, m_mem_q_norm_g, m_mem_k_norm_g, m_w_br_hgrn,
                           m_w_br_fox, m_w_br_mem, m_w_out, m_norm_ffn_g, m_ffn_w_up, m_ffn_conv_w, m_ffn_conv_b,
                           m_ffn_w_down)))
    v = dict(zip(WEIGHTS, (v_norm_mix_g, v_norm_mem_g, v_w_in, v_hgrn_lb_logits, v_hgrn_norm_g, v_fox_f_bias,
                           v_fox_q_norm_g, v_fox_k_norm_g, v_mem_kv_w, v_mem_q_norm_g, v_mem_k_norm_g, v_w_br_hgrn,
                           v_w_br_fox, v_w_br_mem, v_w_out, v_norm_ffn_g, v_ffn_w_up, v_ffn_conv_w, v_ffn_conv_b,
                           v_ffn_w_down)))
    c_idx = lax.axis_index("c")
    chip = 2 * lax.axis_index("x") + lax.axis_index("y")

    mine = {nm: w[nm][0].astype(MXU_DTYPE) for nm in BIG}

    def gathered_full(names, arrays):
        out = {nm: _put_shard(g, KIND[nm], chip, mine[nm]) for nm, g in zip(names, arrays)}
        return {nm: g.reshape(-1, g.shape[2]) if nm in ROW_SHARDED else g for nm, g in out.items()}

    full = gathered_full(FIRST, _gather_shards([mine[nm] for nm in FIRST], [KIND[nm] for nm in FIRST]))
    late = {host: ([mine[nm] for nm in names], [KIND[nm] for nm in names],
                   functools.partial(gathered_full, names)) for host, names in LATE.items()}
    cs = ffn_conv_w.shape[2]
    f = cs * N_CHIPS
    placed = lax.dynamic_update_slice(jnp.zeros((3, f), F32), ffn_conv_w[0] * (c_idx == 0).astype(F32), (0, chip * cs))
    conv_w = _unpack_small(_all_sum_small(_pack_small([placed]), "gather_conv_w"), [(3, f)])[0]

    c_arr = jnp.reshape(c_idx, (1,)).astype(jnp.int32)

    def stacked(nm, g):
        return g.reshape(N_CHIPS, -1, g.shape[1]) if nm in ROW_SHARDED else g

    def with_own(landed, partial, kinds):
        return [_put_shard(bq, STACK, chip, _take_shard(p, k, chip)) for bq, p, k in zip(landed, partial, kinds)]

    kinds_rest, kinds_last = [KIND[nm] for nm in REST], [KIND[nm] for nm in LAST]
    state = {}

    def swap_rest(grads):
        gs = [stacked(nm, grads[nm]) for nm in REST]

        def then(from_sibling):
            state["partial_rest"] = [_add_half(g, a, k, c_arr, "add_half_" + nm)
                                     for g, a, k, nm in zip(gs, from_sibling, kinds_rest, REST)]

        return _swap_rider(gs, kinds_rest), then

    def exchange_rest(grads):
        def then(landed):
            state["landed_rest"] = with_own(landed, state["partial_rest"], kinds_rest)

        return _exchange_rider(state["partial_rest"], kinds_rest), then

    def exchange_last(grads):
        gs = [stacked(nm, grads[nm]) for nm in LAST]
        from_sibling = _pair_swap_halves(gs, kinds_last, "pair_swap_halves_last")
        partial = [_add_half(g, a, k, c_arr, "add_half_" + nm) for g, a, k, nm in zip(gs, from_sibling, kinds_last, LAST)]

        def then(landed):
            state["landed_last"] = with_own(landed, partial, kinds_last)

        return _exchange_rider(partial, kinds_last), then

    hooks = {"fox_bwd": swap_rest, "hgrn_bwd": exchange_rest, "in_proj_dx": exchange_last}

    loss_vec, grad_x, grads = _local_step(x, mem, loss_target, w, full, conv_w, late, hooks)

    landed = dict(zip(LAST + REST, state["landed_last"] + state["landed_rest"]))
    reduced_half = [_sum_chips(landed[nm], "sum_chips_" + nm) for nm in BIG]
    joined = [lax.dynamic_update_slice(o, q, (c_idx * q.shape[0], 0))
              for o, q in zip(_pair_join_halves(reduced_half), reduced_half)]
    gshards = dict(zip(BIG, joined))

    small_shapes = [w[nm].shape for nm in SMALL] + [grads["ffn_conv_w"].shape, loss_vec.shape]
    summed = _all_sum_small(_pack_small([grads[nm] for nm in SMALL] + [grads["ffn_conv_w"], loss_vec]),
                            "all_sum_small_grads")
    g_small, d_small, m_small, v_small, (g_conv_w, loss_row) = _adamw_small(
        summed, small_shapes, [w[nm] for nm in SMALL], [m[nm] for nm in SMALL], [v[nm] for nm in SMALL])
    loss = jnp.sum(loss_row)
    g_out = {nm: gshards[nm][None] for nm in BIG}
    g_out["ffn_conv_w"] = lax.dynamic_slice(g_conv_w, (0, chip * cs), (3, cs))[None]
    delta, new_m, new_v = dict(zip(SMALL, d_small)), dict(zip(SMALL, m_small)), dict(zip(SMALL, v_small))
    g_out.update(zip(SMALL, g_small))
    for nm in BIG + ("ffn_conv_w",):
        operands = (w[nm], g_out[nm], m[nm], v[nm])
        if nm in TRANSPOSED:
            operands = [jnp.swapaxes(a, 1, 2) for a in operands]
        outs = _adamw(*operands, name="adamw_" + nm)
        delta[nm], new_m[nm], new_v[nm] = [jnp.swapaxes(o, 1, 2) for o in outs] if nm in TRANSPOSED else outs

    return (loss, grad_x, *[g_out[nm] for nm in WEIGHTS], *[delta[nm] for nm in WEIGHTS],
            *[new_m[nm] for nm in WEIGHTS], *[new_v[nm] for nm in WEIGHTS])
```

```python
import functools
import math

import jax
import jax.numpy as jnp
from jax import lax
from jax.experimental import pallas as pl
from jax.experimental.pallas import tpu as pltpu

F32 = jnp.float32
BF16 = jnp.bfloat16
MXU_DTYPE = jnp.bfloat16
EXCHANGE_DTYPE = jnp.bfloat16

EPS = 1e-6
HG_HEADS, HG_D = 4, 128
FOX_HEADS, FOX_DH = 8, 64
MEM_HEADS, MEM_DH = 4, 128
HG_CHUNK = 64
FOX_BLOCK = 256
LANE = 128
FFN_GROUP = 256
FLAT_W = 1024
VMEM_LIMIT = 56 * 2 ** 20
NEG = -1e30
N_CHIPS = 4

ADAM_LR, ADAM_B1, ADAM_B2, ADAM_EPS, ADAM_WD, ADAM_STEP = 0.001, 0.9, 0.999, 1e-08, 0.01, 10

MESH = pl.DeviceIdType.MESH
ANY = pl.BlockSpec(memory_space=pl.ANY)


def _mx(x):
    return x.astype(MXU_DTYPE)


def _dot(a, b, ca, cb):
    return lax.dot_general(_mx(a), _mx(b), (((ca,), (cb,)), ((), ())), preferred_element_type=F32)


def _nn(a, b):
    return _dot(a, b, 1, 0)


def _nt(a, b):
    return _dot(a, b, 1, 1)


def _tn(a, b):
    return _dot(a, b, 0, 0)


def _tri_dot(tri_bf, x):
    hi = x.astype(BF16)
    r = x - hi.astype(F32)
    mid = r.astype(BF16)
    lo = (r - mid.astype(F32)).astype(BF16)

    def d(v):
        return lax.dot_general(tri_bf, v, (((1,), (0,)), ((), ())), preferred_element_type=F32)

    return d(hi) + d(mid) + d(lo)


def _sig(x):
    return jax.nn.sigmoid(x)


def _tile(dim, pref, unit=LANE):
    if dim <= pref:
        return dim
    t = pref - pref % unit
    while t >= unit:
        if dim % t == 0:
            return t
        t -= unit
    return dim


def _params(n_grid):
    return pltpu.CompilerParams(dimension_semantics=("arbitrary",) * n_grid, vmem_limit_bytes=VMEM_LIMIT)


def _acc(ref, val, first):
    @pl.when(first)
    def _():
        ref[...] = val

    @pl.when(jnp.logical_not(first))
    def _():
        ref[...] += val


class _Rider:
    def __init__(self, inputs, out_shapes, scratch, start, finish):
        self.inputs, self.out_shapes, self.scratch, self.start, self.finish = inputs, out_shapes, scratch, start, finish


def _ride(body, rider, n_in, n_out, grid):
    if rider is None:
        return body
    ri, ro, rs = len(rider.inputs), len(rider.out_shapes), len(rider.scratch)

    def wrapped(*refs):
        a, b, c = n_in + ri, n_in + ri + n_out, n_in + ri + n_out + ro
        base = refs[:n_in] + refs[a:b] + refs[c:len(refs) - rs]
        r_in, r_out, r_scr = refs[n_in:a], refs[b:c], refs[len(refs) - rs:]
        step = pl.program_id(0)
        for ax in range(1, len(grid)):
            step = step * grid[ax] + pl.program_id(ax)

        @pl.when(step == 0)
        def _():
            rider.start(r_in, r_out, r_scr)

        body(*base)

        @pl.when(step == math.prod(grid) - 1)
        def _():
            rider.finish(r_in, r_out, r_scr)

    return wrapped


def _ride_call(body, rider, *, name, grid, in_specs, out_specs, out_shape, scratch, args, aliases=None):
    n_in, n_out = len(in_specs), len(out_specs)
    aliases = aliases or {}
    if rider is None:
        outs = pl.pallas_call(body, name=name, grid=grid, in_specs=in_specs, out_specs=out_specs, out_shape=out_shape,
                              scratch_shapes=scratch, input_output_aliases=aliases,
                              compiler_params=_params(len(grid)))(*args)
        return list(outs), None
    outs = pl.pallas_call(
        _ride(body, rider, n_in, n_out, grid), name=name, grid=grid,
        in_specs=list(in_specs) + [ANY] * len(rider.inputs), out_specs=list(out_specs) + [ANY] * len(rider.out_shapes),
        out_shape=list(out_shape) + list(rider.out_shapes), scratch_shapes=list(scratch) + list(rider.scratch),
        input_output_aliases=aliases, compiler_params=_params(len(grid)),
    )(*args, *rider.inputs)
    return list(outs[:n_out]), list(outs[n_out:])


def _matmul(a, b, *, name, ta=False, tb=False, tm=1024, tn=2048, tk=None, rider=None, b_parts=None, stack_out=False):
    m, k = (a.shape[1], a.shape[0]) if ta else a.shape
    tk = tk or (1024 if ta else 2048)
    stacked_b = b is not None and b.ndim == 3
    if b_parts:
        n, tn = 2 * b_parts[0].shape[1], _tile(b_parts[0].shape[1], tn)
    elif stacked_b:
        n, tn = b.shape[0] * b.shape[2], b.shape[2]
    else:
        n = b.shape[0] if tb else b.shape[1]
        tn = _tile(n, tn)
    tm, tk = _tile(m, tm), _tile(k, tk)
    nk, nj = k // tk, n // tn

    def body(a_ref, *refs):
        o_ref = refs[-1]
        if b_parts:
            bv = jnp.where(pl.program_id(1) < nj // 2, refs[0][...], refs[1][...])
        else:
            bv = refs[0][...]
        p = _dot(a_ref[...], bv, 0 if ta else 1, 1 if tb else 0)
        if nk == 1:
            o_ref[...] = p
        else:
            _acc(o_ref, p, pl.program_id(2) == 0)

    a_spec = pl.BlockSpec((tk, tm), lambda i, j, kk: (kk, i)) if ta else pl.BlockSpec((tm, tk), lambda i, j, kk: (i, kk))
    if b_parts:
        half = nj // 2
        b_specs = [pl.BlockSpec((tk, tn), lambda i, j, kk: (kk, jnp.minimum(j, half - 1))),
                   pl.BlockSpec((tk, tn), lambda i, j, kk: (kk, jnp.maximum(j - half, 0)))]
        b_args = list(b_parts)
    elif stacked_b:
        b_specs, b_args = [pl.BlockSpec((None, tk, tn), lambda i, j, kk: (j, kk, 0))], [b]
    else:
        b_specs = [pl.BlockSpec((tn, tk), lambda i, j, kk: (j, kk)) if tb else pl.BlockSpec((tk, tn), lambda i, j, kk: (kk, j))]
        b_args = [b]
    if stack_out:
        o_spec, o_sds = pl.BlockSpec((None, tm, tn), lambda i, j, kk: (j, i, 0)), jax.ShapeDtypeStruct((nj, m, tn), F32)
    else:
        o_spec, o_sds = pl.BlockSpec((tm, tn), lambda i, j, kk: (i, j)), jax.ShapeDtypeStruct((m, n), F32)
    outs, extra = _ride_call(body, rider, name=name, grid=(m // tm, nj, nk), in_specs=[a_spec] + b_specs,
                             out_specs=[o_spec], out_shape=[o_sds], scratch=[], args=(a, *b_args))
    return (outs[0], extra) if rider else outs[0]


def _matmul_rows(a_parts, b, *, name, tb, row_ins, vec_ins, epilogue, n_vec_out, tm=512, tk=2048, rider=None):
    m, kp = a_parts[0].shape
    stacked_b = b.ndim == 3
    n = b.shape[1] if stacked_b else (b.shape[0] if tb else b.shape[1])
    tm, tk = _tile(m, tm, 8), (b.shape[2] if stacked_b else _tile(kp, tk))
    nk = kp // tk
    n_a, n_row, n_vec = len(a_parts), len(row_ins), len(vec_ins)

    def body(*refs):
        a_refs, b_refs = refs[:n_a], refs[n_a:2 * n_a]
        rows = refs[2 * n_a:2 * n_a + n_row]
        vecs = refs[2 * n_a + n_row:2 * n_a + n_row + n_vec]
        o_ref = refs[2 * n_a + n_row + n_vec]
        v_refs = refs[2 * n_a + n_row + n_vec + 1:-1]
        acc_ref = refs[-1]
        i, kk = pl.program_id(0), pl.program_id(1)
        p = _dot(a_refs[0][...], b_refs[0][...], 1, 1 if tb else 0)
        for a_ref, b_ref in zip(a_refs[1:], b_refs[1:]):
            p = p + _dot(a_ref[...], b_ref[...], 1, 1 if tb else 0)
        _acc(acc_ref, p, kk == 0)

        @pl.when(kk == nk - 1)
        def _():
            out, vouts = epilogue(acc_ref[...], *[r[...] for r in rows], *[v[...] for v in vecs])
            o_ref[...] = out
            for v_ref, v in zip(v_refs, vouts):
                _acc(v_ref, v, i == 0)

    a_spec = pl.BlockSpec((tm, tk), lambda i, kk: (i, kk))
    if stacked_b:
        b_specs = [pl.BlockSpec((None, n, tk), functools.partial(lambda i, kk, q: (q * nk + kk, 0, 0), q=q))
                   for q in range(n_a)]
    else:
        b_specs = [pl.BlockSpec((n, tk), functools.partial(lambda i, kk, q: (0, q * nk + kk), q=q)) if tb else
                   pl.BlockSpec((tk, n), functools.partial(lambda i, kk, q: (q * nk + kk, 0), q=q)) for q in range(n_a)]
    row = pl.BlockSpec((tm, n), lambda i, kk: (i, 0))
    vec = pl.BlockSpec((1, n), lambda i, kk: (0, 0))
    outs, extra = _ride_call(
        body, rider, name=name, grid=(m // tm, nk),
        in_specs=[a_spec] * n_a + b_specs + [row] * n_row + [vec] * n_vec,
        out_specs=[row] + [vec] * n_vec_out,
        out_shape=[jax.ShapeDtypeStruct((m, n), F32)] + [jax.ShapeDtypeStruct((1, n), F32)] * n_vec_out,
        scratch=[pltpu.VMEM((tm, n), F32)], args=(*a_parts, *([b] * n_a), *row_ins, *vec_ins))
    return outs[0], outs[1:], extra


def _norm_bwd_epilogue(n_dh):
    def epilogue(dh, x, res, *rest):
        for extra in rest[:n_dh]:
            dh = dh + extra
        g = rest[n_dh]
        r = lax.rsqrt(jnp.mean(x * x, axis=-1, keepdims=True) + EPS)
        dhg = dh * g
        dx = res + r * dhg - x * (r * r * r) * jnp.mean(dhg * x, axis=-1, keepdims=True)
        return dx, [jnp.sum(dh * x * r, axis=0, keepdims=True)]

    return epilogue


def _loss_epilogue(y, x1, target):
    d = y.shape[1]
    err = x1 + y - target
    return err * (1.0 / d), [jnp.sum(err * err, axis=0, keepdims=True) * (0.5 / d)]


def _rmsnorm_fwd(x, g, *, name, tm=512):
    n, d = x.shape
    tm = _tile(n, tm, 8)

    def body(x_ref, g_ref, o_ref):
        xv = x_ref[...]
        r = lax.rsqrt(jnp.mean(xv * xv, axis=-1, keepdims=True) + EPS)
        o_ref[...] = (xv * r * g_ref[...]).astype(o_ref.dtype)

    return pl.pallas_call(
        body, name=name, grid=(n // tm,),
        in_specs=[pl.BlockSpec((tm, d), lambda i: (i, 0)), pl.BlockSpec((1, d), lambda i: (0, 0))],
        out_specs=pl.BlockSpec((tm, d), lambda i: (i, 0)),
        out_shape=jax.ShapeDtypeStruct((n, d), MXU_DTYPE),
        compiler_params=_params(1),
    )(x, g)


def _rmsnorm_bwd(x, dhs, g, res, *, name, tm=512):
    n, d = x.shape
    tm = _tile(n, tm, 8)
    n_dh = len(dhs)
    has_res = res is not None

    def body(*refs):
        x_ref, dh_refs, g_ref = refs[0], refs[1:1 + n_dh], refs[1 + n_dh]
        res_ref = refs[2 + n_dh] if has_res else None
        dx_ref, dg_ref = refs[-2], refs[-1]
        xv = x_ref[...]
        dh = dh_refs[0][...].astype(F32)
        for r_ in dh_refs[1:]:
            dh = dh + r_[...].astype(F32)
        r = lax.rsqrt(jnp.mean(xv * xv, axis=-1, keepdims=True) + EPS)
        dhg = dh * g_ref[...]
        dx = r * dhg - xv * (r * r * r) * jnp.mean(dhg * xv, axis=-1, keepdims=True)
        if has_res:
            dx = dx + res_ref[...]
        dx_ref[...] = dx
        _acc(dg_ref, jnp.sum(dh * xv * r, axis=0, keepdims=True), pl.program_id(0) == 0)

    row = pl.BlockSpec((tm, d), lambda i: (i, 0))
    vec = pl.BlockSpec((1, d), lambda i: (0, 0))
    ins = [x] + list(dhs) + [g] + ([res] if has_res else [])
    return pl.pallas_call(
        body, name=name, grid=(n // tm,),
        in_specs=[row] * (1 + n_dh) + [vec] + ([row] if has_res else []),
        out_specs=[row, vec],
        out_shape=[jax.ShapeDtypeStruct((n, d), F32), jax.ShapeDtypeStruct((1, d), F32)],
        compiler_params=_params(1),
    )(*ins)


def _adamw(w, g, m, v, *, name, tr=256):
    _, r, c = w.shape
    c1 = 1.0 / (1.0 - ADAM_B1 ** ADAM_STEP)
    c2 = 1.0 / (1.0 - ADAM_B2 ** ADAM_STEP)

    def body(w_ref, g_ref, m_ref, v_ref, d_ref, mo_ref, vo_ref):
        gv = g_ref[...]
        mn = ADAM_B1 * m_ref[...] + (1.0 - ADAM_B1) * gv
        vn = ADAM_B2 * v_ref[...] + (1.0 - ADAM_B2) * (gv * gv)
        d_ref[...] = -ADAM_LR * ((mn * c1) / (jnp.sqrt(vn * c2) + ADAM_EPS) + ADAM_WD * w_ref[...])
        mo_ref[...] = mn
        vo_ref[...] = vn

    if r % 8 == 0 or r < 8:
        tr = _tile(r, tr, 8)
        grid, blk = (r // tr,), pl.BlockSpec((1, tr, c), lambda i: (0, i, 0))
    else:
        tc = _tile(c, tr)
        grid, blk = (c // tc,), pl.BlockSpec((1, r, tc), lambda i: (0, 0, i))
    sds = jax.ShapeDtypeStruct((1, r, c), F32)
    return pl.pallas_call(
        body, name=name, grid=grid, in_specs=[blk] * 4, out_specs=[blk] * 3, out_shape=[sds] * 3,
        compiler_params=_params(1),
    )(w, g, m, v)


def _bdot(a, b, ca, cb):
    return lax.dot_general(_mx(a), _mx(b), (((ca,), (cb,)), ((0,), (0,))), preferred_element_type=F32)


def _split2(x):
    hi = x.astype(BF16)
    return hi, (x - hi.astype(F32)).astype(BF16)


def _bdotp(a, b, ca, cb):
    def d(u, v):
        return lax.dot_general(u, v, (((ca,), (cb,)), ((0,), (0,))), preferred_element_type=F32)

    return d(a[0], b[0]) + d(a[0], b[1]) + d(a[1], b[0])


def _tri_dot_b(tri_bf, x):
    hi = x.astype(BF16)
    r = x - hi.astype(F32)
    mid = r.astype(BF16)
    lo = (r - mid.astype(F32)).astype(BF16)

    def d(v):
        return lax.dot_general(tri_bf, v, (((2,), (1,)), ((0,), (0,))), preferred_element_type=F32)

    return d(hi) + d(mid) + d(lo)


def _hgrn_forward(hq, hf, hi, lbv, tril, tril_bf):
    nc, c, _ = hq.shape
    sf = _sig(hf)
    f = lbv + (1.0 - lbv) * sf
    k = 1.0 - f
    gcum = _tri_dot_b(tril_bf, jnp.log(f))
    mid = gcum[:, c // 2 - 1:c // 2, :]
    glast = gcum[:, c - 1:c, :]
    sq = _sig(hq)
    q = hq * sq
    e_q = jnp.exp(gcum - mid)
    e_k = jnp.exp(mid - gcum)
    qe, ke = q * e_q, k * e_k
    a = jnp.where(tril, _bdot(qe, ke, 2, 2), 0.0)
    e_g = jnp.exp(gcum)
    qg = q * e_g
    e_s = jnp.exp(glast - gcum)
    kg = k * e_s
    e_l = jnp.exp(glast)
    upd = _bdot(hi, kg, 1, 1)
    st = jnp.zeros((HG_D, HG_D), F32)
    states = []
    for n in range(nc):
        states.append(st)
        st = st * e_l[n] + upd[n]
    st_all = jnp.stack(states)
    o = _bdot(a, hi, 2, 1) + _bdot(qg, st_all, 2, 2)
    return dict(sf=sf, f=f, k=k, sq=sq, q=q, e_q=e_q, e_k=e_k, qe=qe, ke=ke, a=a, e_g=e_g, qg=qg, o=o,
                e_s=e_s, kg=kg, e_l=e_l, st_all=st_all)


def _hgrn_specs(t_, col0):
    def col(off):
        return pl.BlockSpec((1, t_, LANE), lambda h, b: (b, 0, col0 + 4 * h + off))

    vec = pl.BlockSpec((2, LANE), lambda h, b: (0, h))
    one = pl.BlockSpec((1, LANE), lambda h, b: (0, 0))
    blk = pl.BlockSpec((1, t_, LANE), lambda h, b: (b, 0, h))
    return col, vec, one, blk


def _chunk_masks(nc, c):
    row = lax.broadcasted_iota(jnp.int32, (nc, c, c), 1)
    cl = lax.broadcasted_iota(jnp.int32, (nc, c, c), 2)
    return row >= cl, (row >= cl).astype(BF16), (row <= cl).astype(BF16)


def _hgrn_fwd(zm, lb, gn, hw, col0):
    b_, t_, _ = zm.shape
    c = min(HG_CHUNK, t_)
    nc = t_ // c
    col, vec, one, blk = _hgrn_specs(t_, col0)

    def body(q_ref, f_ref, i_ref, g_ref, lb_ref, gn_ref, y_ref):
        lbv, gnv = _sig(lb_ref[0:1, :] - lb_ref[1:2, :]), gn_ref[...]
        tril, tril_bf, _ = _chunk_masks(nc, c)
        chunks = lambda ref: ref[0].reshape(nc, c, LANE)
        o = _hgrn_forward(chunks(q_ref), chunks(f_ref), chunks(i_ref), lbv, tril, tril_bf)["o"]
        r = lax.rsqrt(jnp.mean(o * o, axis=-1, keepdims=True) + EPS)
        hg = chunks(g_ref)
        y_ref[0] = (o * r * gnv * (hg * _sig(hg))).reshape(t_, LANE)

    return pl.pallas_call(
        body, name="hgrn_fwd", grid=(HG_HEADS, b_),
        in_specs=[col(0), col(1), col(2), col(3), vec, one], out_specs=blk,
        out_shape=jax.ShapeDtypeStruct((b_, t_, hw), F32),
        compiler_params=_params(2),
    )(zm, zm, zm, zm, lb, gn)


def _hgrn_bwd(zm, dy, lb, gn, hw, col0, dz, rider=None):
    b_, t_, _ = zm.shape
    c = min(HG_CHUNK, t_)
    nc = t_ // c
    col, vec, one, blk = _hgrn_specs(t_, col0)

    def body(q_ref, f_ref, i_ref, g_ref, dy_ref, lb_ref, gn_ref, _, dz_ref, dlb_ref, dgn_ref):
        h, b = pl.program_id(0), pl.program_id(1)
        lbv, gnv = _sig(lb_ref[0:1, :] - lb_ref[1:2, :]), gn_ref[...]
        tril, tril_bf, triu_bf = _chunk_masks(nc, c)
        last_row = lax.broadcasted_iota(jnp.int32, (nc, c, LANE), 1) == c - 1
        chunks = lambda ref: ref[0].reshape(nc, c, LANE)
        flat = lambda x: x.reshape(t_, LANE)
        hq, hi, hg = chunks(q_ref), chunks(i_ref), chunks(g_ref)
        p = _hgrn_forward(hq, chunks(f_ref), hi, lbv, tril, tril_bf)
        o, q, k, st_all, e_l = p["o"], p["q"], p["k"], p["st_all"], p["e_l"]
        dyv = chunks(dy_ref)
        sg = _sig(hg)
        r = lax.rsqrt(jnp.mean(o * o, axis=-1, keepdims=True) + EPS)
        dn = dyv * (hg * sg)
        dz_ref[0, :, 3 * LANE:] = flat(dyv * (o * r * gnv) * (sg * (1.0 + hg * (1.0 - sg)))).astype(dz_ref.dtype)
        dgn = jnp.sum(flat(dn * o * r), axis=0, keepdims=True)
        dng = dn * gnv
        do = r * dng - o * (r * r * r) * jnp.mean(dng * o, axis=-1, keepdims=True)
        do2, hi2, qg2, ke2, qe2, st2 = (_split2(t) for t in (do, hi, p["qg"], p["ke"], p["qe"], st_all))
        back = _bdotp(do2, qg2, 1, 1)
        dst = jnp.zeros((HG_D, HG_D), F32)
        dsts = [None] * nc
        for n in range(nc - 1, -1, -1):
            dsts[n] = dst
            dst = dst * e_l[n] + back[n]
        dst_all = jnp.stack(dsts)
        da = jnp.where(tril, _bdotp(do2, hi2, 2, 2), 0.0)
        da2 = _split2(da)
        dq = _bdotp(da2, ke2, 2, 1) * p["e_q"] + _bdotp(do2, st2, 2, 1) * p["e_g"]
        dk_state = _bdotp(hi2, _split2(dst_all), 2, 1) * p["e_s"]
        dk = _bdotp(da2, qe2, 1, 1) * p["e_k"] + dk_state
        dz_ref[0, :, 2 * LANE:3 * LANE] = flat(_bdot(p["a"], do, 1, 1) + _bdot(p["kg"], dst_all, 2, 2)).astype(dz_ref.dtype)
        extra = (jnp.sum(k * dk_state, axis=1, keepdims=True) + e_l * jnp.sum(st_all * dst_all, axis=1, keepdims=True))
        dgc = q * dq - k * dk + jnp.where(last_row, extra, 0.0)
        dfv = _tri_dot_b(triu_bf, dgc) / p["f"] - dk
        sf, sq = p["sf"], p["sq"]
        dz_ref[0, :, LANE:2 * LANE] = flat(dfv * (1.0 - lbv) * sf * (1.0 - sf)).astype(dz_ref.dtype)
        dlb = jnp.sum(flat(dfv * (1.0 - sf)), axis=0, keepdims=True)
        dz_ref[0, :, :LANE] = flat(dq * (sq * (1.0 + hq * (1.0 - sq)))).astype(dz_ref.dtype)
        dl0 = dlb * lbv * (1.0 - lbv)
        _acc(dlb_ref, jnp.concatenate([dl0, -dl0], axis=0), b == 0)
        _acc(dgn_ref, dgn, jnp.logical_and(b == 0, h == 0))

    return _ride_call(
        body, rider, name="hgrn_bwd", grid=(HG_HEADS, b_),
        in_specs=[col(0), col(1), col(2), col(3), blk, vec, one, ANY],
        out_specs=[pl.BlockSpec((1, t_, 4 * LANE), lambda h, b: (b, 0, col0 // 4 + h)), vec, one],
        out_shape=[jax.ShapeDtypeStruct(dz.shape, dz.dtype), jax.ShapeDtypeStruct((2, hw), F32),
                   jax.ShapeDtypeStruct((1, LANE), F32)],
        scratch=[], args=(zm, zm, zm, zm, dy, lb, gn, dz), aliases={7: 0})


def _fox_logf(x):
    return jnp.minimum(x, 0.0) - jnp.log(1.0 + jnp.exp(-jnp.abs(x)))


def _fox_prep(zf, bias):
    b_, t_, _ = zf.shape
    tb = min(FOX_BLOCK, t_)
    nb = t_ // tb

    def body(z_ref, b_ref, fc_ref):
        tril_bf = (lax.broadcasted_iota(jnp.int32, (tb, tb), 0) >= lax.broadcasted_iota(jnp.int32, (tb, tb), 1)).astype(BF16)
        bv = b_ref[...]

        def blk(i, carry):
            rows = pl.ds(pl.multiple_of(i * tb, tb), tb)
            fc = _tri_dot(tril_bf, _fox_logf(z_ref[0, rows, :] + bv)) + carry
            fc_ref[0, rows, :] = fc
            return fc[tb - 1:tb, :]

        lax.fori_loop(0, nb, blk, jnp.zeros((1, LANE), F32))

    blk_spec = pl.BlockSpec((1, t_, LANE), lambda b: (b, 0, 0))
    return pl.pallas_call(
        body, name="fox_prep", grid=(b_,),
        in_specs=[blk_spec, pl.BlockSpec((1, LANE), lambda b: (0, 0))], out_specs=blk_spec,
        out_shape=jax.ShapeDtypeStruct((b_, t_, LANE), F32), compiler_params=_params(1),
    )(zf, bias)


def _fox_post(dfc, zf, bias):
    b_, t_, _ = zf.shape
    npair = dfc.shape[1]
    tb = min(FOX_BLOCK, t_)
    nb = t_ // tb

    def body(d_ref, z_ref, b_ref, dz_ref, db_ref):
        triu_bf = (lax.broadcasted_iota(jnp.int32, (tb, tb), 0) <= lax.broadcasted_iota(jnp.int32, (tb, tb), 1)).astype(BF16)
        valid = lax.broadcasted_iota(jnp.int32, (tb, LANE), 1) < FOX_HEADS
        bv = b_ref[...]

        def blk(m, carry):
            tail, db = carry
            rows = pl.ds(pl.multiple_of((nb - 1 - m) * tb, tb), tb)
            dfc_rows = d_ref[0, 0, rows, :]
            for p in range(1, npair):
                dfc_rows = dfc_rows + pltpu.roll(d_ref[0, p, rows, :], 2 * p, 1)
            dlf = _tri_dot(triu_bf, dfc_rows) + tail
            dx = jnp.where(valid, dlf * _sig(-(z_ref[0, rows, :] + bv)), 0.0)
            dz_ref[0, rows, :] = dx.astype(dz_ref.dtype)
            return dlf[0:1, :], db + jnp.sum(dx, axis=0, keepdims=True)

        z1 = jnp.zeros((1, LANE), F32)
        _, db = lax.fori_loop(0, nb, blk, (z1, z1))
        _acc(db_ref, db, pl.program_id(0) == 0)

    blk_spec = pl.BlockSpec((1, t_, LANE), lambda b: (b, 0, 0))
    vec = pl.BlockSpec((1, LANE), lambda b: (0, 0))
    return pl.pallas_call(
        body, name="fox_post", grid=(b_,),
        in_specs=[pl.BlockSpec((1, npair, t_, LANE), lambda b: (b, 0, 0, 0)), blk_spec, vec], out_specs=[blk_spec, vec],
        out_shape=[jax.ShapeDtypeStruct((b_, t_, LANE), MXU_DTYPE), jax.ShapeDtypeStruct((1, LANE), F32)],
        compiler_params=_params(1),
    )(dfc, zf, bias)


FOX_TILE = 256
FOX_BAND = 512
AUG = 64


def _head_mean_matrix():
    r = lax.broadcasted_iota(jnp.int32, (LANE, LANE), 0) // FOX_DH
    c = lax.broadcasted_iota(jnp.int32, (LANE, LANE), 1) // FOX_DH
    return (r == c).astype(BF16)


def _dot_right_exact(x, m_bf):
    hi = x.astype(BF16)
    lo = (x - hi.astype(F32)).astype(BF16)

    def d(v):
        return lax.dot_general(v, m_bf, (((1,), (0,)), ((), ())), preferred_element_type=F32)

    return d(hi) + d(lo)


def _pair_norm(x, g2, bd):
    r = lax.rsqrt(_dot_right_exact(x * x, bd) * (1.0 / FOX_DH) + EPS)
    return x * r * g2, r


def _pair_norm_bwd(x, r, dy, g2, bd):
    dyg = dy * g2
    dx = r * dyg - x * (r * r * r) * (_dot_right_exact(dyg * x, bd) * (1.0 / FOX_DH))
    return dx, jnp.sum(dy * x * r, axis=0, keepdims=True)


def _head_lanes(xn, hh):
    return xn if hh == 0 else pltpu.roll(xn, FOX_DH, 1)


def _split3(x):
    hi = x.astype(BF16).astype(F32)
    mid = (x - hi).astype(BF16).astype(F32)
    return hi, mid, x - hi - mid


def _fox_operands(q_ref, k_ref, v_ref, fc_ref, gq2, gk2, p, qa, ka, va):
    t_ = q_ref.shape[1]
    bd = _head_mean_matrix()
    lane = lax.broadcasted_iota(jnp.int32, (t_, LANE), 1)
    qx, kx = q_ref[0], k_ref[0]
    qn, rq = _pair_norm(qx, gq2, bd)
    kn, rk = _pair_norm(kx, gk2, bd)
    vv = v_ref[0]
    q_aug = jnp.where(jnp.logical_and(lane >= AUG, lane < AUG + 3), 1.0, 0.0)
    for hh in range(2):
        fcol = jnp.sum(jnp.where(lane == 2 * p + hh, fc_ref[0], 0.0), axis=-1, keepdims=True)
        hi, mid, lo = _split3(-fcol)
        k_aug = jnp.where(lane == AUG, hi, jnp.where(lane == AUG + 1, mid, jnp.where(lane == AUG + 2, lo,
                          jnp.where(lane == AUG + 3, 1.0, 0.0))))
        head = lane < FOX_DH
        qa[hh] = jnp.where(head, _head_lanes(qn, hh), q_aug).astype(MXU_DTYPE)
        ka[hh] = jnp.where(head, _head_lanes(kn, hh), k_aug).astype(MXU_DTYPE)
        va[hh] = jnp.where(head, _head_lanes(vv, hh), 0.0).astype(MXU_DTYPE)
    return bd, lane, qx, kx, rq, rk


def _fox_specs(t_, fw, col0):
    npair = fw // LANE

    def col(off):
        return pl.BlockSpec((1, t_, LANE), lambda b, p: (b, 0, col0 + 3 * p + off))

    pair = pl.BlockSpec((1, t_, LANE), lambda b, p: (b, 0, p))
    full = pl.BlockSpec((1, t_, LANE), lambda b, p: (b, 0, 0))
    gvec = pl.BlockSpec((1, LANE), lambda b, p: (0, 0))
    lse = pl.BlockSpec((1, 1, t_, LANE), lambda b, p: (b, p, 0, 0))
    return col, pair, full, gvec, lse


def _fox_fwd(zm, fc, gq2, gk2, fw, col0, rider=None):
    b_, t_, _ = zm.shape
    npair = fw // LANE
    tq = min(FOX_TILE, t_)
    bw = min(FOX_BAND, t_)
    nband, tpb = t_ // bw, bw // tq
    scale = FOX_DH ** -0.5
    col, pair, full, gvec, lse_spec = _fox_specs(t_, fw, col0)

    def body(q_ref, k_ref, v_ref, fc_ref, gq_ref, gk_ref, o_ref, lse_ref, qa, ka, va):
        p = pl.program_id(1)
        _fox_operands(q_ref, k_ref, v_ref, fc_ref, gq_ref[...] * scale, gk_ref[...], p, qa, ka, va)
        ahead = lax.broadcasted_iota(jnp.int32, (tq, bw), 1) - lax.broadcasted_iota(jnp.int32, (tq, bw), 0)
        lane = lax.broadcasted_iota(jnp.int32, (tq, LANE), 1)

        for band in range(nband):
            c0 = band * bw

            def qtile(ii, _, c0=c0):
                r0 = pl.multiple_of(c0 + ii * tq, tq)
                rows = pl.ds(r0, tq)
                keep = ahead <= r0 - c0
                res = []
                for hh in range(2):
                    qb = qa[hh, rows, :]
                    s_b = jnp.where(keep, _nt(qb, ka[hh, c0:c0 + bw, :]), NEG)
                    m = jnp.max(s_b, axis=-1, keepdims=True)
                    if c0:
                        s_a = _nt(qb, ka[hh, 0:c0, :])
                        m = jnp.maximum(m, jnp.max(s_a, axis=-1, keepdims=True))
                    p_b = jnp.exp(s_b - m)
                    l = jnp.sum(p_b, axis=-1, keepdims=True)
                    acc = _nn(p_b, va[hh, c0:c0 + bw, :])
                    if c0:
                        p_a = jnp.exp(s_a - m)
                        l = l + jnp.sum(p_a, axis=-1, keepdims=True)
                        acc = acc + _nn(p_a, va[hh, 0:c0, :])
                    res.append((acc / l, m + jnp.log(l)))
                (o0, e0), (o1, e1) = res
                o_ref[0, rows, :] = jnp.where(lane < FOX_DH, o0, pltpu.roll(o1, FOX_DH, 1))
                lse_ref[0, 0, rows, :] = jnp.where(lane == 0, e0, jnp.where(lane == 1, e1, 0.0))
                return 0

            lax.fori_loop(0, tpb, qtile, 0)

    return _ride_call(
        body, rider, name="fox_fwd", grid=(b_, npair),
        in_specs=[col(0), col(1), col(2), full, gvec, gvec],
        out_specs=[pair, lse_spec],
        out_shape=[jax.ShapeDtypeStruct((b_, t_, fw), F32), jax.ShapeDtypeStruct((b_, npair, t_, LANE), F32)],
        scratch=[pltpu.VMEM((2, t_, LANE), MXU_DTYPE)] * 3, args=(zm, zm, zm, fc, gq2, gk2))


def _norm_bwd(x, dy, g):
    r = lax.rsqrt(jnp.mean(x * x, axis=-1, keepdims=True) + EPS)
    dyg = dy * g
    dx = r * dyg - x * (r * r * r) * jnp.mean(dyg * x, axis=-1, keepdims=True)
    return dx, jnp.sum(dy * x * r, axis=0, keepdims=True)


def _fox_bwd(zm, o, do, lse, fc, gq2, gk2, fw, col0, dz, rider=None):
    b_, t_, _ = zm.shape
    npair = fw // LANE
    tq = min(FOX_TILE, t_)
    nb = t_ // tq
    bw = min(FOX_BAND, t_)
    nband, tpb = t_ // bw, bw // tq
    scale = FOX_DH ** -0.5
    col, pair, full, gvec, lse_spec = _fox_specs(t_, fw, col0)

    def body(q_ref, k_ref, v_ref, o_ref, do_ref, lse_ref, fc_ref, gq_ref, gk_ref, _,
             dz_ref, dfc_ref, dgq_ref, dgk_ref, qa, ka, va, da, rowv, dq_acc, dk_acc, dv_acc):
        b, p = pl.program_id(0), pl.program_id(1)
        gq2v, gk2v = gq_ref[...] * scale, gk_ref[...]
        bd, lane, qx, kx, rq, rk = _fox_operands(q_ref, k_ref, v_ref, fc_ref, gq2v, gk2v, p, qa, ka, va)
        head = lane < FOX_DH
        dov = do_ref[0]
        dsum = _dot_right_exact(dov * o_ref[0], bd)
        eye = (lax.broadcasted_iota(jnp.int32, (tq, tq), 0) == lax.broadcasted_iota(jnp.int32, (tq, tq), 1)).astype(F32)
        for hh in range(2):
            da[hh] = jnp.where(head, _head_lanes(dov, hh), 0.0).astype(MXU_DTYPE)
            for blk in range(nb):
                rs = slice(blk * tq, (blk + 1) * tq)
                rowv[2 * hh:2 * hh + 1, rs] = jnp.sum(eye * lse_ref[0, 0, rs, hh:hh + 1], axis=0, keepdims=True)
                rowv[2 * hh + 1:2 * hh + 2, rs] = jnp.sum(eye * dsum[rs, hh * FOX_DH:hh * FOX_DH + 1], axis=0, keepdims=True)
        dq_acc[...] = jnp.zeros(dq_acc.shape, F32)
        ahead = lax.broadcasted_iota(jnp.int32, (tq, bw), 1) - lax.broadcasted_iota(jnp.int32, (tq, bw), 0)

        def part(hh, kb, vb, lo, hi, keep):
            qm, dm = qa[hh, lo:hi, :], da[hh, lo:hi, :]
            pt = jnp.exp(_nt(kb, qm) - rowv[2 * hh:2 * hh + 1, lo:hi])
            if keep is not None:
                pt = jnp.where(keep, pt, 0.0)
            dst = pt * (_nt(vb, dm) - rowv[2 * hh + 1:2 * hh + 2, lo:hi])
            dq_acc[hh, lo:hi, :] += _tn(dst, kb)
            return _nn(dst, qm), _nn(pt, dm)

        for band in range(nband):
            c0 = band * bw

            def kvtile(jj, _, c0=c0):
                r0 = pl.multiple_of(c0 + jj * tq, tq)
                rows = pl.ds(r0, tq)
                keep = ahead >= r0 - c0
                for hh in range(2):
                    kb, vb = ka[hh, rows, :], va[hh, rows, :]
                    dk_t, dv_t = part(hh, kb, vb, c0, c0 + bw, keep)
                    if c0 + bw < t_:
                        dk_u, dv_u = part(hh, kb, vb, c0 + bw, t_, None)
                        dk_t, dv_t = dk_t + dk_u, dv_t + dv_u
                    dk_acc[hh, rows, :] = dk_t
                    dv_acc[hh, rows, :] = dv_t
                return 0

            lax.fori_loop(0, tpb, kvtile, 0)

        dq0, dq1, dk0, dk1 = dq_acc[0], dq_acc[1], dk_acc[0], dk_acc[1]
        dqn = jnp.where(head, dq0, pltpu.roll(dq1, FOX_DH, 1))
        dkn = jnp.where(head, dk0, pltpu.roll(dk1, FOX_DH, 1))
        dqx, gq_part = _pair_norm_bwd(qx, rq, dqn, gq2v, bd)
        dkx, gk_part = _pair_norm_bwd(kx, rk, dkn, gk2v, bd)
        dz_ref[0, :, :LANE] = dqx.astype(dz_ref.dtype)
        dz_ref[0, :, LANE:2 * LANE] = dkx.astype(dz_ref.dtype)
        dz_ref[0, :, 2 * LANE:] = jnp.where(head, dv_acc[0], pltpu.roll(dv_acc[1], FOX_DH, 1)).astype(dz_ref.dtype)

        def bias_grad(dqh, dkh):
            return dqh[:, AUG + 3:AUG + 4] - dkh[:, AUG:AUG + 1]

        dfc_ref[0, 0] = jnp.where(lane == 0, bias_grad(dq0, dk0), jnp.where(lane == 1, bias_grad(dq1, dk1), 0.0))
        first = jnp.logical_and(b == 0, p == 0)
        _acc(dgq_ref, gq_part * scale, first)
        _acc(dgk_ref, gk_part, first)

    gs = jax.ShapeDtypeStruct((1, LANE), F32)
    return _ride_call(
        body, rider, name="fox_bwd", grid=(b_, npair),
        in_specs=[col(0), col(1), col(2), pair, pair, lse_spec, full, gvec, gvec, ANY],
        out_specs=[pl.BlockSpec((1, t_, 3 * LANE), lambda b, p: (b, 0, col0 // 3 + p)), lse_spec, gvec, gvec],
        out_shape=[jax.ShapeDtypeStruct(dz.shape, dz.dtype), jax.ShapeDtypeStruct((b_, npair, t_, LANE), F32), gs, gs],
        scratch=[pltpu.VMEM((2, t_, LANE), MXU_DTYPE)] * 4
        + [pltpu.VMEM((8, t_), F32)] + [pltpu.VMEM((2, t_, LANE), F32)] * 3,
        args=(zm, zm, zm, o, do, lse, fc, gq2, gk2, dz), aliases={9: 0})


def _mem_specs(t_, m_, mw, col0):
    nh = mw // LANE
    qcol = pl.BlockSpec((1, t_, LANE), lambda b, h: (b, 0, col0 + h))
    kcol = pl.BlockSpec((1, m_, LANE), lambda b, h: (b, 0, h))
    vcol = pl.BlockSpec((1, m_, LANE), lambda b, h: (b, 0, nh + h))
    ycol = pl.BlockSpec((1, t_, LANE), lambda b, h: (b, 0, h))
    gvec = pl.BlockSpec((1, LANE), lambda b, h: (0, 0))
    return qcol, kcol, vcol, ycol, gvec


def _mem_fwd(zm, mkv, gq, gk, mw, col0):
    b_, t_, _ = zm.shape
    m_ = mkv.shape[1]
    tq = min(512, t_)
    nb = t_ // tq
    scale = MEM_DH ** -0.5
    qcol, kcol, vcol, ycol, gvec = _mem_specs(t_, m_, mw, col0)

    def body(q_ref, k_ref, v_ref, gq_ref, gk_ref, y_ref):
        gqv, gkv = gq_ref[...] * scale, gk_ref[...]
        kv = k_ref[0]
        kn = _mx(kv * lax.rsqrt(jnp.mean(kv * kv, axis=-1, keepdims=True) + EPS) * gkv)
        vv = _mx(v_ref[0])

        def blk(i, _):
            rows = pl.ds(pl.multiple_of(i * tq, tq), tq)
            qv = q_ref[0, rows, :]
            s = _nt(qv * lax.rsqrt(jnp.mean(qv * qv, axis=-1, keepdims=True) + EPS) * gqv, kn)
            e = jnp.exp(s - jnp.max(s, axis=-1, keepdims=True))
            y_ref[0, rows, :] = _nn(e / jnp.sum(e, axis=-1, keepdims=True), vv)
            return 0

        lax.fori_loop(0, nb, blk, 0)

    return pl.pallas_call(
        body, name="mem_fwd", grid=(b_, MEM_HEADS), in_specs=[qcol, kcol, vcol, gvec, gvec], out_specs=ycol,
        out_shape=jax.ShapeDtypeStruct((b_, t_, mw), F32), compiler_params=_params(2),
    )(zm, mkv, mkv, gq, gk)


def _mem_bwd(zm, mkv, dy, gq, gk, mw, col0, dz):
    b_, t_, _ = zm.shape
    m_ = mkv.shape[1]
    tq = min(512, t_)
    nb = t_ // tq
    scale = MEM_DH ** -0.5
    qcol, kcol, vcol, ycol, gvec = _mem_specs(t_, m_, mw, col0)

    def body(q_ref, k_ref, v_ref, dy_ref, gq_ref, gk_ref, _, dq_ref, dk_ref, dv_ref, dgq_ref, dgk_ref):
        gqv, gkv = gq_ref[...] * scale, gk_ref[...]
        kv = k_ref[0]
        kn = _mx(kv * lax.rsqrt(jnp.mean(kv * kv, axis=-1, keepdims=True) + EPS) * gkv)
        vv = _mx(v_ref[0])

        def blk(i, carry):
            dkn, dvv, dgq = carry
            rows = pl.ds(pl.multiple_of(i * tq, tq), tq)
            qv = q_ref[0, rows, :]
            qn = _mx(qv * lax.rsqrt(jnp.mean(qv * qv, axis=-1, keepdims=True) + EPS) * gqv)
            s = _nt(qn, kn)
            e = jnp.exp(s - jnp.max(s, axis=-1, keepdims=True))
            pm = e / jnp.sum(e, axis=-1, keepdims=True)
            dob = _mx(dy_ref[0, rows, :])
            dp = _nt(dob, vv)
            ds = pm * (dp - jnp.sum(dp * pm, axis=-1, keepdims=True))
            dqv, gq_part = _norm_bwd(qv, _nn(ds, kn), gqv)
            dq_ref[0, rows, :] = dqv.astype(dq_ref.dtype)
            return dkn + _tn(ds, qn), dvv + _tn(pm, dob), dgq + gq_part * scale

        z = jnp.zeros((m_, LANE), F32)
        dkn, dvv, dgq = lax.fori_loop(0, nb, blk, (z, z, jnp.zeros((1, LANE), F32)))
        dkv, dgk = _norm_bwd(kv, dkn, gkv)
        dk_ref[0] = dkv
        dv_ref[0] = dvv
        first = jnp.logical_and(pl.program_id(0) == 0, pl.program_id(1) == 0)
        _acc(dgq_ref, dgq, first)
        _acc(dgk_ref, dgk, first)

    kblk = pl.BlockSpec((1, m_, LANE), lambda b, h: (b, 0, h))
    gs = jax.ShapeDtypeStruct((1, LANE), F32)
    ks = jax.ShapeDtypeStruct((b_, m_, mw), F32)
    return pl.pallas_call(
        body, name="mem_bwd", grid=(b_, MEM_HEADS), in_specs=[qcol, kcol, vcol, ycol, gvec, gvec, ANY],
        out_specs=[qcol, kblk, kblk, gvec, gvec],
        out_shape=[jax.ShapeDtypeStruct(dz.shape, dz.dtype), ks, ks, gs, gs], input_output_aliases={6: 0},
        compiler_params=_params(2),
    )(zm, mkv, mkv, dy, gq, gk, dz)


def _merge_specs(tm, d, w, gcol):
    row_d = pl.BlockSpec((tm, d), lambda i: (i, 0))
    row_w = pl.BlockSpec((tm, w), lambda i: (i, 0))
    gates = [pl.BlockSpec((tm, d), functools.partial(lambda i, k: (i, gcol + k), k=k)) for k in range(3)]
    w_br = pl.BlockSpec((w, d), lambda i: (0, 0))
    w_o = pl.BlockSpec((d, d), lambda i: (0, 0))
    return row_d, row_w, gates, w_br, w_o


def _merge_fwd(x, ys, zm, w_brs, w_out, gcol, g_next, tm=256):
    n, d = x.shape
    w = ys[0].shape[1]
    tm = _tile(n, tm, 8)
    row_d, row_w, gates, w_br, w_o = _merge_specs(tm, d, w, gcol)

    def body(x_ref, ya, yb, yc, g0, g1, g2, wa, wb, wc, wo, gn_ref, x1_ref, mg_ref, h_ref):
        mg = (_sig(g0[...]) * _nn(ya[...], wa[...]) + _sig(g1[...]) * _nn(yb[...], wb[...])
              + _sig(g2[...]) * _nn(yc[...], wc[...]))
        mg_ref[...] = mg.astype(mg_ref.dtype)
        x1 = x_ref[...] + _nn(mg, wo[...])
        x1_ref[...] = x1
        h_ref[...] = (x1 * lax.rsqrt(jnp.mean(x1 * x1, axis=-1, keepdims=True) + EPS) * gn_ref[...]).astype(h_ref.dtype)

    half = jax.ShapeDtypeStruct((n, d), MXU_DTYPE)
    return pl.pallas_call(
        body, name="merge_fwd", grid=(n // tm,),
        in_specs=[row_d, row_w, row_w, row_w] + gates + [w_br, w_br, w_br, w_o, pl.BlockSpec((1, d), lambda i: (0, 0))],
        out_specs=[row_d, row_d, row_d],
        out_shape=[jax.ShapeDtypeStruct((n, d), F32), half, half],
        compiler_params=_params(1),
    )(x, *ys, zm, zm, zm, *w_brs, w_out, g_next)


def _merge_bwd(dx1, ys, zm, w_brs, w_out, gcol, tm=256):
    n, d = dx1.shape
    w = ys[0].shape[1]
    tm = _tile(n, tm, 8)
    row_d, row_w, gates, w_br, w_o = _merge_specs(tm, d, w, gcol)

    def body(dx_ref, ya, yb, yc, g0, g1, g2, wa, wb, wc, wo, dgl_ref, dpa, dpb, dpc, dya, dyb, dyc):
        dm = _nt(dx_ref[...], wo[...])
        for k, (y, g, wr, dp_ref, dy_ref) in enumerate(((ya, g0, wa, dpa, dya), (yb, g1, wb, dpb, dyb),
                                                        (yc, g2, wc, dpc, dyc))):
            sg = _sig(g[...])
            pr = _nn(y[...], wr[...])
            dgl_ref[:, k * d:(k + 1) * d] = (dm * pr * sg * (1.0 - sg)).astype(dgl_ref.dtype)
            dp = (dm * sg).astype(dp_ref.dtype)
            dp_ref[...] = dp
            dy_ref[...] = _nt(dp, wr[...])

    sd = jax.ShapeDtypeStruct((n, d), MXU_DTYPE)
    sw = jax.ShapeDtypeStruct((n, w), F32)
    return pl.pallas_call(
        body, name="merge_bwd", grid=(n // tm,),
        in_specs=[row_d, row_w, row_w, row_w] + gates + [w_br, w_br, w_br, w_o],
        out_specs=[pl.BlockSpec((tm, 3 * d), lambda i: (i, 0)), row_d, row_d, row_d, row_w, row_w, row_w],
        out_shape=[jax.ShapeDtypeStruct((n, zm.shape[1]), MXU_DTYPE), sd, sd, sd, sw, sw, sw],
        compiler_params=_params(1),
    )(dx1, *ys, zm, zm, zm, *w_brs, w_out)


CONV_ROWS = 512
HALO = 8


def _ext(ref, r0, t_):
    rc = min(CONV_ROWS, t_)
    a, b = max(r0 - HALO, 0), min(r0 + rc + HALO, t_)
    parts = []
    if r0 - HALO < 0:
        parts.append(jnp.zeros((HALO, ref.shape[2]), F32))
    parts.append(ref[0, a:b, :].astype(F32))
    if r0 + rc + HALO > t_:
        parts.append(jnp.zeros((HALO, ref.shape[2]), F32))
    return jnp.concatenate(parts, axis=0) if len(parts) > 1 else parts[0]


def _gelu_parts(ac):
    e = jnp.exp(-0.5 * ac * ac)
    t = 1.0 / (1.0 + (0.3275911 * 2.0 ** -0.5) * jnp.abs(ac))
    tail = (0.5 * e) * (t * (0.254829592 + t * (-0.284496736 + t * (1.421413741 + t * (-1.453152027 + t * 1.061405429)))))
    return jnp.where(ac < 0, tail, 1.0 - tail), e * ((2.0 * math.pi) ** -0.5)


def _conv_taps(a_ext, cw, cb):
    a2, a1 = pltpu.roll(a_ext, 2, 0), pltpu.roll(a_ext, 1, 0)
    return cw[0:1, :] * a2 + cw[1:2, :] * a1 + cw[2:3, :] * a_ext + cb, a2, a1


def _glu_specs(t_, f, g):
    gate = pl.BlockSpec((1, t_, g), lambda j, b: (b, 0, j))
    value = pl.BlockSpec((1, t_, g), lambda j, b: (b, 0, f // g + j))
    cwb = pl.BlockSpec((3, g), lambda j, b: (0, j))
    cbb = pl.BlockSpec((1, g), lambda j, b: (0, j))
    return gate, value, cwb, cbb


def _glu_fwd(u, cw, cb):
    b_, t_, f2 = u.shape
    f = f2 // 2
    g = min(FFN_GROUP, f)
    rc = min(CONV_ROWS, t_)
    gate, value, cwb, cbb = _glu_specs(t_, f, g)

    def body(a_ref, v_ref, cw_ref, cb_ref, y_ref):
        cwv, cbv = cw_ref[...], cb_ref[...]
        for r0 in range(0, t_, rc):
            ac = _conv_taps(_ext(a_ref, r0, t_), cwv, cbv)[0][HALO:HALO + rc]
            cdf, _ = _gelu_parts(ac)
            y_ref[0, r0:r0 + rc, :] = (ac * cdf * v_ref[0, r0:r0 + rc, :]).astype(y_ref.dtype)

    return pl.pallas_call(
        body, name="glu_fwd", grid=(f // g, b_), in_specs=[gate, value, cwb, cbb], out_specs=gate,
        out_shape=jax.ShapeDtypeStruct((b_, t_, f), MXU_DTYPE), compiler_params=_params(2),
    )(u, u, cw, cb)


def _glu_bwd(u, dy, cw, cb):
    b_, t_, f2 = u.shape
    f = f2 // 2
    g = min(FFN_GROUP, f)
    rc = min(CONV_ROWS, t_)
    ne = rc + 2 * HALO
    gate, value, cwb, cbb = _glu_specs(t_, f, g)

    def body(a_ref, v_ref, dy_ref, cw_ref, cb_ref, da_ref, dv_ref, dcw_ref, dcb_ref):
        cwv, cbv = cw_ref[...], cb_ref[...]
        dcw = [jnp.zeros((1, g), F32) for _ in range(3)]
        dcb = jnp.zeros((1, g), F32)
        for r0 in range(0, t_, rc):
            a_ext, v_ext, dy_ext = _ext(a_ref, r0, t_), _ext(v_ref, r0, t_), _ext(dy_ref, r0, t_)
            ac, a2, a1 = _conv_taps(a_ext, cwv, cbv)
            cdf, pdf = _gelu_parts(ac)
            dac = dy_ext * v_ext * (cdf + ac * pdf)
            da = cwv[2:3, :] * dac + cwv[1:2, :] * pltpu.roll(dac, ne - 1, 0) + cwv[0:1, :] * pltpu.roll(dac, ne - 2, 0)
            mid = slice(HALO, HALO + rc)
            da_ref[0, r0:r0 + rc, :] = da[mid].astype(da_ref.dtype)
            dv_ref[0, r0:r0 + rc, :] = (dy_ext[mid] * ac[mid] * cdf[mid]).astype(dv_ref.dtype)
            dacm = dac[mid]
            dcw[0] = dcw[0] + jnp.sum(dacm * a2[mid], axis=0, keepdims=True)
            dcw[1] = dcw[1] + jnp.sum(dacm * a1[mid], axis=0, keepdims=True)
            dcw[2] = dcw[2] + jnp.sum(dacm * a_ext[mid], axis=0, keepdims=True)
            dcb = dcb + jnp.sum(dacm, axis=0, keepdims=True)
        first = pl.program_id(1) == 0
        _acc(dcw_ref, jnp.concatenate(dcw, axis=0), first)
        _acc(dcb_ref, dcb, first)

    sds = jax.ShapeDtypeStruct((b_, t_, f), MXU_DTYPE)
    return pl.pallas_call(
        body, name="glu_bwd", grid=(f // g, b_), in_specs=[gate, value, gate, cwb, cbb],
        out_specs=[gate, gate, cwb, cbb],
        out_shape=[sds, sds, jax.ShapeDtypeStruct((3, f), F32), jax.ShapeDtypeStruct((1, f), F32)],
        compiler_params=_params(2),
    )(u, u, dy, cw, cb)


def _place():
    x, y, c = lax.axis_index("x"), lax.axis_index("y"), lax.axis_index("c")
    chips = [(1 - x, y), (x, 1 - y), (1 - x, 1 - y)]
    return x, y, c, chips


def _remote(src, dst, send_sem, recv_sem, to):
    return pltpu.make_async_remote_copy(src_ref=src, dst_ref=dst, send_sem=send_sem, recv_sem=recv_sem,
                                        device_id=to, device_id_type=MESH)


STACK, COLS = "stack", "cols"


def _shard_ref(ref, kind, s, rows, c):
    if kind == COLS:
        cols = pl.ds(pl.multiple_of(s * c, LANE), c)
        return ref.at[:, cols] if rows is None else ref.at[rows, cols]
    return ref.at[s] if rows is None else ref.at[s, rows, :]


def _halves(c, half):
    mine = pl.ds(pl.multiple_of(c * half, 16), half)
    theirs = pl.ds(pl.multiple_of((1 - c) * half, 16), half)
    return mine, theirs


def _gather_parts(kinds):
    def first_copies(ins, outs, sems):
        x, y, c, chips = _place()
        me = 2 * x + y
        cps = []
        for i, (w_ref, o_ref, kind) in enumerate(zip(ins, outs, kinds)):
            r, cw = w_ref.shape
            mine, _ = _halves(c, r // 2)
            for j, chip in enumerate(chips):
                cps.append(_remote(w_ref.at[mine], _shard_ref(o_ref, kind, me, mine, cw), sems[0].at[6 * i + j],
                                   sems[1].at[6 * i + j], (*chip, c)))
        return cps

    def start(ins, outs, sems):
        for cp in first_copies(ins, outs, sems):
            cp.start()

    def finish(ins, outs, sems):
        x, y, c, chips = _place()
        sib = (x, y, 1 - c)
        passed = []
        for i, (w_ref, o_ref, kind) in enumerate(zip(ins, outs, kinds)):
            r, cw = w_ref.shape
            mine, _ = _halves(c, r // 2)
            for j, (px, py) in enumerate(chips):
                blk = _shard_ref(o_ref, kind, 2 * px + py, mine, cw)
                _remote(blk, blk, sems[0].at[6 * i + j], sems[1].at[6 * i + j], sib).wait_recv()
                passed.append(_remote(blk, blk, sems[0].at[6 * i + 3 + j], sems[1].at[6 * i + 3 + j], sib))
                passed[-1].start()
        for i, (w_ref, o_ref, kind) in enumerate(zip(ins, outs, kinds)):
            r, cw = w_ref.shape
            _, theirs = _halves(c, r // 2)
            for j, (px, py) in enumerate(chips):
                blk = _shard_ref(o_ref, kind, 2 * px + py, theirs, cw)
                _remote(blk, blk, sems[0].at[6 * i + 3 + j], sems[1].at[6 * i + 3 + j], sib).wait_recv()
        for cp in first_copies(ins, outs, sems) + passed:
            cp.wait_send()

    return start, finish


def _gather_shapes(shards, kinds):
    return [jax.ShapeDtypeStruct((a.shape[0], N_CHIPS * a.shape[1]) if k == COLS else (N_CHIPS,) + a.shape, a.dtype)
            for a, k in zip(shards, kinds)]


def _gather_sems(nw):
    return [pltpu.SemaphoreType.DMA((6 * nw,)), pltpu.SemaphoreType.DMA((6 * nw,))]


def _gather_shards(shards, kinds):
    nw = len(shards)
    start, finish = _gather_parts(kinds)

    def body(*refs):
        ins, outs, sems = refs[:nw], refs[nw:2 * nw], refs[2 * nw:]
        start(ins, outs, sems)
        finish(ins, outs, sems)

    return pl.pallas_call(
        body, name="gather_shards", in_specs=[ANY] * nw, out_specs=[ANY] * nw,
        out_shape=_gather_shapes(shards, kinds), scratch_shapes=_gather_sems(nw),
    )(*shards)


def _gather_rider(shards, kinds):
    start, finish = _gather_parts(kinds)
    return _Rider(list(shards), _gather_shapes(shards, kinds), _gather_sems(len(shards)), start, finish)


def _half_shape(g, kind):
    if kind == COLS:
        return (g.shape[0] // 2, g.shape[1])
    return (g.shape[0], g.shape[1] // 2, g.shape[2])


def _swap_parts(kinds):
    def copies(ins, outs, sems):
        x, y, c, _ = _place()
        cps = []
        for i, (g_ref, a_ref, kind) in enumerate(zip(ins, outs, kinds)):
            r = g_ref.shape[0] if kind == COLS else g_ref.shape[1]
            _, theirs = _halves(c, r // 2)
            src = g_ref.at[theirs] if kind == COLS else g_ref.at[:, theirs]
            cps.append(_remote(src, a_ref, sems[0].at[i], sems[1].at[i], (x, y, 1 - c)))
        return cps

    def start(ins, outs, sems):
        for cp in copies(ins, outs, sems):
            cp.start()

    def finish(ins, outs, sems):
        for cp in copies(ins, outs, sems):
            cp.wait()

    return start, finish


def _swap_shapes(gs, kinds):
    return [jax.ShapeDtypeStruct(_half_shape(g, k), g.dtype) for g, k in zip(gs, kinds)]


def _pair_swap_halves(gs, kinds, name):
    nw = len(gs)
    start, finish = _swap_parts(kinds)

    def body(*refs):
        ins, outs, sems = refs[:nw], refs[nw:2 * nw], refs[2 * nw:]
        start(ins, outs, sems)
        finish(ins, outs, sems)

    return pl.pallas_call(
        body, name=name, in_specs=[ANY] * nw, out_specs=[ANY] * nw, out_shape=_swap_shapes(gs, kinds),
        scratch_shapes=[pltpu.SemaphoreType.DMA((nw,)), pltpu.SemaphoreType.DMA((nw,))],
    )(*gs)


def _swap_rider(gs, kinds):
    start, finish = _swap_parts(kinds)
    nw = len(gs)
    return _Rider(list(gs), _swap_shapes(gs, kinds), [pltpu.SemaphoreType.DMA((nw,)), pltpu.SemaphoreType.DMA((nw,))],
                  start, finish)


def _row_tile(rows, width, itemsize=4, target=2 ** 21):
    return _tile(rows, max(8, target // (width * itemsize)), 8)


def _add_half(g, a, kind, c_idx, name):
    if kind == COLS:
        half, wd = a.shape
        tr = _row_tile(half, wd)
        nblk = half // tr
        grid = (nblk,)
        g_spec = pl.BlockSpec((tr, wd), lambda i, c_ref: (c_ref[0] * nblk + i, 0))
        a_spec = pl.BlockSpec((tr, wd), lambda i, c_ref: (i, 0))
    else:
        n, half, wd = a.shape
        tr = _row_tile(half, wd)
        nblk = half // tr
        grid = (n, nblk)
        g_spec = pl.BlockSpec((1, tr, wd), lambda s, i, c_ref: (s, c_ref[0] * nblk + i, 0))
        a_spec = pl.BlockSpec((1, tr, wd), lambda s, i, c_ref: (s, i, 0))

    def body(c_ref, g_ref, a_ref, o_ref):
        o_ref[...] = (g_ref[...] + a_ref[...]).astype(o_ref.dtype)

    return pl.pallas_call(
        body, name=name,
        grid_spec=pltpu.PrefetchScalarGridSpec(num_scalar_prefetch=1, grid=grid, in_specs=[g_spec, a_spec],
                                               out_specs=a_spec),
        out_shape=jax.ShapeDtypeStruct(a.shape, EXCHANGE_DTYPE), compiler_params=_params(len(grid)),
    )(c_idx, g, a)


def _exchange_parts(kinds):
    def copies(ins, outs, sems):
        x, y, c, chips = _place()
        me = 2 * x + y
        cps = []
        for i, (p_ref, b_ref, kind) in enumerate(zip(ins, outs, kinds)):
            cw = b_ref.shape[2]
            for j, (px, py) in enumerate(chips):
                cps.append(_remote(_shard_ref(p_ref, kind, 2 * px + py, None, cw), b_ref.at[me],
                                   sems[0].at[3 * i + j], sems[1].at[3 * i + j], (px, py, c)))
        return cps

    def start(ins, outs, sems):
        for cp in copies(ins, outs, sems):
            cp.start()

    def finish(ins, outs, sems):
        x, y, c, chips = _place()
        for i, b_ref in enumerate(outs):
            for j, (px, py) in enumerate(chips):
                blk = b_ref.at[2 * px + py]
                _remote(blk, blk, sems[0].at[3 * i + j], sems[1].at[3 * i + j], (px, py, c)).wait_recv()
        for cp in copies(ins, outs, sems):
            cp.wait_send()

    return start, finish


def _exchange_shapes(ps, kinds):
    return [jax.ShapeDtypeStruct((N_CHIPS,) + ((p.shape[0], p.shape[1] // N_CHIPS) if k == COLS else tuple(p.shape[1:])),
                                 p.dtype) for p, k in zip(ps, kinds)]


def _exchange_sems(nw):
    return [pltpu.SemaphoreType.DMA((3 * nw,)), pltpu.SemaphoreType.DMA((3 * nw,))]


def _exchange_rider(ps, kinds):
    start, finish = _exchange_parts(kinds)
    return _Rider(list(ps), _exchange_shapes(ps, kinds), _exchange_sems(len(ps)), start, finish)


def _sum_chips(bq, name):
    n, h, wd = bq.shape
    tr = _row_tile(h, wd * n)

    def body(b_ref, o_ref):
        acc = b_ref[0].astype(F32)
        for s in range(1, n):
            acc = acc + b_ref[s].astype(F32)
        o_ref[...] = acc

    return pl.pallas_call(
        body, name=name, grid=(h // tr,),
        in_specs=[pl.BlockSpec((n, tr, wd), lambda i: (0, i, 0))], out_specs=pl.BlockSpec((tr, wd), lambda i: (i, 0)),
        out_shape=jax.ShapeDtypeStruct((h, wd), F32), compiler_params=_params(1),
    )(bq)


def _pair_join_halves(qs):
    nw = len(qs)

    def body(*refs):
        ins, outs = refs[:nw], refs[nw:2 * nw]
        send_sems, recv_sems = refs[2 * nw:]
        x, y, c, _ = _place()
        sent = []
        for i, (q_ref, o_ref) in enumerate(zip(ins, outs)):
            mine, _ = _halves(c, q_ref.shape[0])
            sent.append(_remote(q_ref, o_ref.at[mine], send_sems.at[i], recv_sems.at[i], (x, y, 1 - c)))
            sent[-1].start()
        for i, (q_ref, o_ref) in enumerate(zip(ins, outs)):
            _, theirs = _halves(c, q_ref.shape[0])
            _remote(q_ref, o_ref.at[theirs], send_sems.at[i], recv_sems.at[i], (x, y, 1 - c)).wait_recv()
        for cp in sent:
            cp.wait_send()

    return pl.pallas_call(
        body, name="pair_join_halves", in_specs=[ANY] * nw, out_specs=[ANY] * nw,
        out_shape=[jax.ShapeDtypeStruct((2 * q.shape[0], q.shape[1]), q.dtype) for q in qs],
        scratch_shapes=[pltpu.SemaphoreType.DMA((nw,)), pltpu.SemaphoreType.DMA((nw,))],
    )(*qs)


def _all_sum_small(s, name):
    sr, w = s.shape

    def body(s_ref, o_ref, buf, send_sems, recv_sems):
        x, y, c, _ = _place()
        me = 4 * x + 2 * y + c
        buf[me] = s_ref[...]
        peers = []
        for k in range(1, 8):
            px = 1 - x if k & 4 else x
            py = 1 - y if k & 2 else y
            pc = 1 - c if k & 1 else c
            peers.append((px, py, pc))
        sent = [_remote(s_ref, buf.at[me], send_sems.at[k], recv_sems.at[k], peer) for k, peer in enumerate(peers)]
        for cp in sent:
            cp.start()
        for k, (px, py, pc) in enumerate(peers):
            _remote(s_ref, buf.at[4 * px + 2 * py + pc], send_sems.at[k], recv_sems.at[k], (px, py, pc)).wait_recv()
        for cp in sent:
            cp.wait_send()
        acc = buf[0]
        for d in range(1, 8):
            acc = acc + buf[d]
        o_ref[...] = acc

    vm = pl.BlockSpec(memory_space=pltpu.VMEM)
    return pl.pallas_call(
        body, name=name, in_specs=[vm], out_specs=vm, out_shape=jax.ShapeDtypeStruct((sr, w), F32),
        scratch_shapes=[pltpu.VMEM((8, sr, w), F32), pltpu.SemaphoreType.DMA((7,)), pltpu.SemaphoreType.DMA((7,))],
    )(s)


BIG = ("w_in", "mem_kv_w", "w_br_hgrn", "w_br_fox", "w_br_mem", "w_out", "ffn_w_up", "ffn_w_down")
KIND = {"w_in": STACK, "mem_kv_w": STACK, "w_br_hgrn": COLS, "w_br_fox": COLS, "w_br_mem": COLS, "w_out": STACK,
        "ffn_w_up": STACK, "ffn_w_down": STACK}
ROW_SHARDED = ("mem_kv_w", "w_out", "ffn_w_down")
FIRST = ("w_in",)
REST = tuple(nm for nm in BIG if nm not in FIRST)
LATE = {"in_proj": tuple(nm for nm in REST if not nm.startswith("ffn_")),
        "fox_fwd": tuple(nm for nm in REST if nm.startswith("ffn_"))}
LAST = ("w_in",)
TRANSPOSED = ("w_in",)


def _z_layout(d, hw, fw, mw):
    gate, npair, nh, nm = 3 * d // LANE, fw // LANE, hw // LANE, mw // LANE
    fox0, hg0 = gate, gate + 3 * npair
    o_fox, o_mem = 4 * nh, 4 * nh + 3 * npair
    order = [o_mem + nm + j for j in range(gate)]
    order += [o_fox + k * npair + p for p in range(npair) for k in range(3)]
    order += [k * nh + h for h in range(nh) for k in range(4)]
    order += [o_mem + h for h in range(nm)]
    assert fox0 % 3 == 0 and hg0 % 4 == 0
    return fox0, hg0, hg0 + 4 * nh, order


def _reorder_blocks(a, order):
    runs, start = [], 0
    for i in range(1, len(order) + 1):
        if i == len(order) or order[i] != order[i - 1] + 1:
            runs.append((order[start], order[i - 1] + 1))
            start = i
    return jnp.concatenate([a[:, lo * LANE:hi * LANE] for lo, hi in runs], axis=1)


def _put_shard(arr, kind, s, piece):
    if kind == COLS:
        return lax.dynamic_update_slice(arr, piece, (0, s * piece.shape[1]))
    return lax.dynamic_update_slice(arr, piece[None], (s, 0, 0))


def _take_shard(arr, kind, s):
    if kind == COLS:
        return lax.dynamic_slice(arr, (0, s * (arr.shape[1] // N_CHIPS)), (arr.shape[0], arr.shape[1] // N_CHIPS))
    return lax.dynamic_index_in_dim(arr, s, 0, keepdims=False)


def _w_in_pieces(cs, s1, nf):
    out = []
    for s in range(N_CHIPS):
        lo, hi = cs * s, cs * (s + 1)
        for a, b, forget in ((lo, min(hi, s1), False), (max(lo, s1), min(hi, s1 + nf), True), (max(lo, s1 + nf), hi, False)):
            if a < b:
                out.append((s, a - lo, b - lo, forget, a - s1 if forget else (a if a < s1 else a - nf)))
    return out


def _split_w_in(stacked, s1, nf):
    pieces = _w_in_pieces(stacked.shape[2], s1, nf)
    main = [stacked[s, :, a:b] for s, a, b, forget, _ in pieces if not forget]
    ff = [stacked[s, :, a:b] for s, a, b, forget, _ in pieces if forget]
    return jnp.concatenate(main, axis=1), jnp.concatenate(ff, axis=1)


def _join_w_in(g_main, g_ff, s1, nf):
    cs = (g_main.shape[1] + nf) // N_CHIPS
    shards = [[] for _ in range(N_CHIPS)]
    for s, a, b, forget, off in _w_in_pieces(cs, s1, nf):
        shards[s].append((g_ff if forget else g_main)[:, off:off + b - a])
    return jnp.stack([jnp.concatenate(p, axis=1) if len(p) > 1 else p[0] for p in shards])


SMALL = ("norm_mix_g", "norm_mem_g", "norm_ffn_g", "hgrn_lb_logits", "hgrn_norm_g", "fox_f_bias", "fox_q_norm_g",
         "fox_k_norm_g", "mem_q_norm_g", "mem_k_norm_g", "ffn_conv_b")


def _small_rows(shapes):
    rows = []
    for a, (r, c) in enumerate(shapes):
        for i in range(r):
            for lo in range(0, c, FLAT_W):
                rows.append((a, i, lo, min(FLAT_W, c - lo)))
    return rows


def _pack_small(vals):
    rows = _small_rows([v.shape for v in vals])
    sr = -(-len(rows) // 8) * 8

    def body(*refs):
        o_ref = refs[-1]
        o_ref[...] = jnp.zeros(o_ref.shape, F32)
        for k, (a, i, lo, wd) in enumerate(rows):
            o_ref[k:k + 1, 0:wd] = refs[a][i:i + 1, lo:lo + wd]

    vm = pl.BlockSpec(memory_space=pltpu.VMEM)
    return pl.pallas_call(body, name="pack_small", in_specs=[vm] * len(vals), out_specs=vm,
                          out_shape=jax.ShapeDtypeStruct((sr, FLAT_W), F32))(*vals)


def _row_of(buf_ref, rows, a, i):
    parts = [buf_ref[k:k + 1, 0:wd] for k, (a2, i2, _, wd) in enumerate(rows) if (a2, i2) == (a, i)]
    return jnp.concatenate(parts, axis=1) if len(parts) > 1 else parts[0]


def _unpack_small(buf, shapes):
    rows = _small_rows(shapes)

    def body(buf_ref, *outs):
        for a, (r, _) in enumerate(shapes):
            for i in range(r):
                outs[a][i:i + 1, :] = _row_of(buf_ref, rows, a, i)

    vm = pl.BlockSpec(memory_space=pltpu.VMEM)
    return pl.pallas_call(body, name="unpack_small", in_specs=[vm], out_specs=[vm] * len(shapes),
                          out_shape=[jax.ShapeDtypeStruct(shp, F32) for shp in shapes])(buf)


def _adamw_small(buf, shapes, ws, ms, vs):
    n = len(ws)
    rows = _small_rows(shapes)
    c1 = 1.0 / (1.0 - ADAM_B1 ** ADAM_STEP)
    c2 = 1.0 / (1.0 - ADAM_B2 ** ADAM_STEP)

    def body(buf_ref, *refs):
        w_refs, m_refs, v_refs = refs[:n], refs[n:2 * n], refs[2 * n:3 * n]
        outs = refs[3 * n:]
        g_out, d_out, m_out, v_out, rest = outs[:n], outs[n:2 * n], outs[2 * n:3 * n], outs[3 * n:4 * n], outs[4 * n:]
        for a, (r, _) in enumerate(shapes):
            for i in range(r):
                gv = _row_of(buf_ref, rows, a, i)
                if a >= n:
                    rest[a - n][i:i + 1, :] = gv
                    continue
                row = slice(i, i + 1)
                mn = ADAM_B1 * m_refs[a][row, :] + (1.0 - ADAM_B1) * gv
                vn = ADAM_B2 * v_refs[a][row, :] + (1.0 - ADAM_B2) * (gv * gv)
                g_out[a][row, :] = gv
                d_out[a][row, :] = -ADAM_LR * ((mn * c1) / (jnp.sqrt(vn * c2) + ADAM_EPS) + ADAM_WD * w_refs[a][row, :])
                m_out[a][row, :] = mn
                v_out[a][row, :] = vn

    vm = pl.BlockSpec(memory_space=pltpu.VMEM)
    own = [jax.ShapeDtypeStruct(shp, F32) for shp in shapes[:n]]
    outs = pl.pallas_call(
        body, name="adamw_small", in_specs=[vm] * (1 + 3 * n), out_specs=[vm] * (4 * n + len(shapes) - n),
        out_shape=own * 4 + [jax.ShapeDtypeStruct(shp, F32) for shp in shapes[n:]],
    )(buf, *ws, *ms, *vs)
    return outs[:n], outs[n:2 * n], outs[2 * n:3 * n], outs[3 * n:4 * n], outs[4 * n:]


def _pad_lanes(v, width=LANE):
    return jnp.pad(v, ((0, 0), (0, width - v.shape[1])))


WEIGHTS = ("norm_mix_g", "norm_mem_g", "w_in", "hgrn_lb_logits", "hgrn_norm_g", "fox_f_bias", "fox_q_norm_g",
           "fox_k_norm_g", "mem_kv_w", "mem_q_norm_g", "mem_k_norm_g", "w_br_hgrn", "w_br_fox", "w_br_mem", "w_out",
           "norm_ffn_g", "ffn_w_up", "ffn_conv_w", "ffn_conv_b", "ffn_w_down")


def _local_step(x, mem, target, w, full, conv_w, late=None, hooks=None):
    b_, t_, d = x.shape
    n = b_ * t_
    hw, fw, mw = HG_HEADS * HG_D, FOX_HEADS * FOX_DH, MEM_HEADS * MEM_DH
    m_ = mem.shape[1]
    f = conv_w.shape[1]
    s1 = 4 * hw + 3 * fw
    fox_col, hg_col, mem_col, order = _z_layout(d, hw, fw, mw)
    gate_col = 0
    inverse = [order.index(j) for j in range(len(order))]

    w_main, w_ff = _split_w_in(full["w_in"], s1, FOX_HEADS)
    w_main = _reorder_blocks(w_main, order)
    w_ff = _pad_lanes(w_ff)
    f_bias = _pad_lanes(w["fox_f_bias"])
    cb = w["ffn_conv_b"]

    x2 = x.reshape(n, d)
    h = _rmsnorm_fwd(x2, w["norm_mix_g"], name="norm_mix_fwd")
    if late:
        pieces, kinds, finish = late["in_proj"]
        zm, gathered = _matmul(h, w_main, name="in_proj", rider=_gather_rider(pieces, kinds))
        full = {**full, **finish(gathered)}
    else:
        zm = _matmul(h, w_main, name="in_proj")
    w_brs = [full["w_br_hgrn"], full["w_br_fox"], full["w_br_mem"]]
    w_out, w_kv = full["w_out"], full["mem_kv_w"]
    zf = _matmul(h, w_ff, name="in_proj_forget")
    zm3, zf3 = zm.reshape(b_, t_, -1), zf.reshape(b_, t_, LANE)
    ya = _hgrn_fwd(zm3, w["hgrn_lb_logits"], w["hgrn_norm_g"], hw, hg_col)
    fc = _fox_prep(zf3, f_bias)
    fox_gq, fox_gk = jnp.tile(w["fox_q_norm_g"], (1, 2)), jnp.tile(w["fox_k_norm_g"], (1, 2))
    if late:
        pieces, kinds, finish = late["fox_fwd"]
        (yb, lse), gathered = _fox_fwd(zm3, fc, fox_gq, fox_gk, fw, fox_col, _gather_rider(pieces, kinds))
        full = {**full, **finish(gathered)}
    else:
        yb, lse = _fox_fwd(zm3, fc, fox_gq, fox_gk, fw, fox_col)[0]
    w_up, w_down = full["ffn_w_up"], full["ffn_w_down"]
    mem2 = mem.reshape(b_ * m_, d)
    hm = _rmsnorm_fwd(mem2, w["norm_mem_g"], name="norm_mem_fwd")
    mkv = _matmul(hm, w_kv, name="mem_kv_proj").reshape(b_, m_, 2 * mw)
    yc = _mem_fwd(zm3, mkv, w["mem_q_norm_g"], w["mem_k_norm_g"], mw, mem_col)
    ys = [ya.reshape(n, hw), yb.reshape(n, fw), yc.reshape(n, mw)]
    x1, merged, h2 = _merge_fwd(x2, ys, zm, w_brs, w_out, gate_col, w["norm_ffn_g"])
    u = _matmul(h2, w_up, name="ffn_up")
    u3 = u.reshape(b_, t_, 2 * f)
    yff = _glu_fwd(u3, conv_w, cb).reshape(n, f)
    dy, (loss_vec,), _ = _matmul_rows([yff], w_down, name="ffn_down_loss", tb=False, row_ins=[x1, target.reshape(n, d)],
                                      vec_ins=[], epilogue=_loss_epilogue, n_vec_out=1)

    grads = {}

    def ridden(name, call):
        if not hooks or name not in hooks:
            return call(None)[0]
        rider, then = hooks[name](grads)
        outs, extra = call(rider)
        then(extra)
        return outs

    dyff = _matmul(dy, w_down, tb=True, name="ffn_down_dx")
    grads["ffn_w_down"] = _matmul(yff, dy, ta=True, name="ffn_down_dw", tm=1408)
    du_a, du_v, grads["ffn_conv_w"], grads["ffn_conv_b"] = _glu_bwd(u3, dyff.reshape(b_, t_, f), conv_w, cb)
    du_a, du_v = du_a.reshape(n, f), du_v.reshape(n, f)
    dx1, (grads["norm_ffn_g"],), _ = _matmul_rows(
        [du_a, du_v], w_up, name="ffn_up_dx", tb=True, row_ins=[x1, dy], vec_ins=[w["norm_ffn_g"]],
        epilogue=_norm_bwd_epilogue(0), n_vec_out=1)
    grads["ffn_w_up"] = _matmul(h2, None, ta=True, name="ffn_up_dw", b_parts=[du_a, du_v], tn=f // 2, stack_out=True)

    dz, dpa, dpb, dpc, dya, dyb, dyc = _merge_bwd(dx1, ys, zm, w_brs, w_out, gate_col)
    dz = dz.reshape(b_, t_, -1)
    grads["w_out"] = _matmul(merged, dx1, ta=True, name="out_proj_dw")
    for nm, y_, dp_ in zip(("w_br_hgrn", "w_br_fox", "w_br_mem"), ys, (dpa, dpb, dpc)):
        grads[nm] = _matmul(y_, dp_, ta=True, name=nm + "_dw")

    dz, dmk, dmv, grads["mem_q_norm_g"], grads["mem_k_norm_g"] = _mem_bwd(
        zm3, mkv, dyc.reshape(b_, t_, mw), w["mem_q_norm_g"], w["mem_k_norm_g"], mw, mem_col, dz)
    dmkv = jnp.concatenate([dmk, dmv], axis=-1).reshape(b_ * m_, 2 * mw)
    grads["mem_kv_w"] = _matmul(hm, dmkv, ta=True, name="mem_kv_dw")
    dhm = _matmul(dmkv, w_kv, tb=True, name="mem_kv_dx")
    _, grads["norm_mem_g"] = _rmsnorm_bwd(mem2, [dhm], w["norm_mem_g"], None, name="norm_mem_bwd")

    dz, dfc, g_fq, g_fk = ridden("fox_bwd", lambda rider: _fox_bwd(
        zm3, yb, dyb.reshape(b_, t_, fw), lse, fc, fox_gq, fox_gk, fw, fox_col, dz, rider))
    grads["fox_q_norm_g"] = g_fq[:, :FOX_DH] + g_fq[:, FOX_DH:]
    grads["fox_k_norm_g"] = g_fk[:, :FOX_DH] + g_fk[:, FOX_DH:]
    dzf, g_fb = _fox_post(dfc, zf3, f_bias)
    grads["fox_f_bias"] = g_fb[:, :FOX_HEADS]

    dz, grads["hgrn_lb_logits"], grads["hgrn_norm_g"] = ridden("hgrn_bwd", lambda rider: _hgrn_bwd(
        zm3, dya.reshape(b_, t_, hw), w["hgrn_lb_logits"], w["hgrn_norm_g"], hw, hg_col, dz, rider))
    dzm = dz.reshape(n, -1)
    dzf2 = dzf.reshape(n, LANE)
    g_main = _matmul(h, dzm, ta=True, name="in_proj_dw")
    g_ff = _matmul(h, dzf2, ta=True, name="in_proj_forget_dw")
    grads["w_in"] = _join_w_in(_reorder_blocks(g_main, inverse), g_ff[:, :FOX_HEADS], s1, FOX_HEADS)

    dh_b = _matmul(dzf2, w_ff, tb=True, name="in_proj_forget_dx")

    def in_proj_dx(rider):
        out = _matmul(dzm, w_main, tb=True, name="in_proj_dx", rider=rider)
        return ([out[0]], out[1]) if rider else ([out], None)

    dh_a, = ridden("in_proj_dx", in_proj_dx)
    grad_x, grads["norm_mix_g"] = _rmsnorm_bwd(x2, [dh_a, dh_b], w["norm_mix_g"], dx1, name="norm_mix_bwd")
    return loss_vec, grad_x.reshape(b_, t_, d), grads


def kernel(x, mem, norm_mix_g, norm_mem_g, w_in, hgrn_lb_logits, hgrn_norm_g, fox_f_bias, fox_q_norm_g, fox_k_norm_g, mem_kv_w, mem_q_norm_g, mem_k_norm_g, w_br_hgrn, w_br_fox, w_br_mem, w_out, norm_ffn_g, ffn_w_up, ffn_conv_w, ffn_conv_b, ffn_w_down, loss_target, m_norm_mix_g, m_norm_mem_g, m_w_in, m_hgrn_lb_logits, m_hgrn_norm_g, m_fox_f_bias, m_fox_q_norm_g, m_fox_k_norm_g, m_mem_kv_w, m_mem_q_norm_g, m_mem_k_norm_g, m_w_br_hgrn, m_w_br_fox, m_w_br_mem, m_w_out, m_norm_ffn_g, m_ffn_w_up, m_ffn_conv_w, m_ffn_conv_b, m_ffn_w_down, v_norm_mix_g, v_norm_mem_g, v_w_in, v_hgrn_lb_logits, v_hgrn_norm_g, v_fox_f_bias, v_fox_q_norm_g, v_fox_k_norm_g, v_mem_kv_w, v_mem_q_norm_g, v_mem_k_norm_g, v_w_br_hgrn, v_w_br_fox, v_w_br_mem, v_w_out, v_norm_ffn_g, v_ffn_w_up, v_ffn_conv_w, v_ffn_conv_b, v_ffn_w_down):
    w = dict(zip(WEIGHTS, (norm_mix_g, norm_mem_g, w_in, hgrn_lb_logits, hgrn_norm_g, fox_f_bias, fox_q_norm_g,
                           fox_k_norm_g, mem_kv_w, mem_q_norm_g, mem_k_norm_g, w_br_hgrn, w_br_fox, w_br_mem, w_out,
                           norm_ffn_g, ffn_w_up, ffn_conv_w, ffn_conv_b, ffn_w_down)))
    m = dict(zip(WEIGHTS, (m_norm_mix_g, m_norm_mem_g, m_w_in, m_hgrn_lb_logits, m_hgrn_norm_g, m_fox_f_bias,
                           m_fox_q_norm_g, m_fox_k_norm_g, m_mem_kv_w, m_mem_q_norm_g, m_mem_k_norm_g, m_w_br_hgrn,
                           m_w_br_fox, m_w_br_mem, m_w_out, m_norm_ffn_g, m_ffn_w_up, m_ffn_conv_w, m_ffn_conv_b,
                           m_ffn_w_down)))
    v = dict(zip(WEIGHTS, (v_norm_mix_g, v_norm_mem_g, v_w_in, v_hgrn_lb_logits, v_hgrn_norm_g, v_fox_f_bias,
                           v_fox_q_norm_g, v_fox_k_norm_g, v_mem_kv_w, v_mem_q_norm_g, v_mem_k_norm_g, v_w_br_hgrn,
                           v_w_br_fox, v_w_br_mem, v_w_out, v_norm_ffn_g, v_ffn_w_up, v_ffn_conv_w, v_ffn_conv_b,
                           v_ffn_w_down)))
    c_idx = lax.axis_index("c")
    chip = 2 * lax.axis_index("x") + lax.axis_index("y")

    mine = {nm: w[nm][0].astype(MXU_DTYPE) for nm in BIG}

    def gathered_full(names, arrays):
        out = {nm: _put_shard(g, KIND[nm], chip, mine[nm]) for nm, g in zip(names, arrays)}
        return {nm: g.reshape(-1, g.shape[2]) if nm in ROW_SHARDED else g for nm, g in out.items()}

    full = gathered_full(FIRST, _gather_shards([mine[nm] for nm in FIRST], [KIND[nm] for nm in FIRST]))
    late = {host: ([mine[nm] for nm in names], [KIND[nm] for nm in names],
                   functools.partial(gathered_full, names)) for host, names in LATE.items()}
    cs = ffn_conv_w.shape[2]
    f = cs * N_CHIPS
    placed = lax.dynamic_update_slice(jnp.zeros((3, f), F32), ffn_conv_w[0] * (c_idx == 0).astype(F32), (0, chip * cs))
    conv_w = _unpack_small(_all_sum_small(_pack_small([placed]), "gather_conv_w"), [(3, f)])[0]

    c_arr = jnp.reshape(c_idx, (1,)).astype(jnp.int32)

    def stacked(nm, g):
        return g.reshape(N_CHIPS, -1, g.shape[1]) if nm in ROW_SHARDED else g

    def with_own(landed, partial, kinds):
        return [_put_shard(bq, STACK, chip, _take_shard(p, k, chip)) for bq, p, k in zip(landed, partial, kinds)]

    kinds_rest, kinds_last = [KIND[nm] for nm in REST], [KIND[nm] for nm in LAST]
    state = {}

    def swap_rest(grads):
        gs = [stacked(nm, grads[nm]) for nm in REST]

        def then(from_sibling):
            state["partial_rest"] = [_add_half(g, a, k, c_arr, "add_half_" + nm)
                                     for g, a, k, nm in zip(gs, from_sibling, kinds_rest, REST)]

        return _swap_rider(gs, kinds_rest), then

    def exchange_rest(grads):
        def then(landed):
            state["landed_rest"] = with_own(landed, state["partial_rest"], kinds_rest)

        return _exchange_rider(state["partial_rest"], kinds_rest), then

    def exchange_last(grads):
        gs = [stacked(nm, grads[nm]) for nm in LAST]
        from_sibling = _pair_swap_halves(gs, kinds_last, "pair_swap_halves_last")
        partial = [_add_half(g, a, k, c_arr, "add_half_" + nm) for g, a, k, nm in zip(gs, from_sibling, kinds_last, LAST)]

        def then(landed):
            state["landed_last"] = with_own(landed, partial, kinds_last)

        return _exchange_rider(partial, kinds_last), then

    hooks = {"fox_bwd": swap_rest, "hgrn_bwd": exchange_rest, "in_proj_dx": exchange_last}

    loss_vec, grad_x, grads = _local_step(x, mem, loss_target, w, full, conv_w, late, hooks)

    landed = dict(zip(LAST + REST, state["landed_last"] + state["landed_rest"]))
    reduced_half = [_sum_chips(landed[nm], "sum_chips_" + nm) for nm in BIG]
    joined = [lax.dynamic_update_slice(o, q, (c_idx * q.shape[0], 0))
              for o, q in zip(_pair_join_halves(reduced_half), reduced_half)]
    gshards = dict(zip(BIG, joined))

    small_shapes = [w[nm].shape for nm in SMALL] + [grads["ffn_conv_w"].shape, loss_vec.shape]
    summed = _all_sum_small(_pack_small([grads[nm] for nm in SMALL] + [grads["ffn_conv_w"], loss_vec]),
                            "all_sum_small_grads")
    g_small, d_small, m_small, v_small, (g_conv_w, loss_row) = _adamw_small(
        summed, small_shapes, [w[nm] for nm in SMALL], [m[nm] for nm in SMALL], [v[nm] for nm in SMALL])
    loss = jnp.sum(loss_row)
    g_out = {nm: gshards[nm][None] for nm in BIG}
    g_out["ffn_conv_w"] = lax.dynamic_slice(g_conv_w, (0, chip * cs), (3, cs))[None]
    delta, new_m, new_v = dict(zip(SMALL, d_small)), dict(zip(SMALL, m_small)), dict(zip(SMALL, v_small))
    g_out.update(zip(SMALL, g_small))
    for nm in BIG + ("ffn_conv_w",):
        operands = (w[nm], g_out[nm], m[nm], v[nm])
        if nm in TRANSPOSED:
            operands = [jnp.swapaxes(a, 1, 2) for a in operands]
        outs = _adamw(*operands, name="adamw_" + nm)
        delta[nm], new_m[nm], new_v[nm] = [jnp.swapaxes(o, 1, 2) for o in outs] if nm in TRANSPOSED else outs

    return (loss, grad_x, *[g_out[nm] for nm in WEIGHTS], *[delta[nm] for nm in WEIGHTS],
            *[new_m[nm] for nm in WEIGHTS], *[new_v[nm] for nm in WEIGHTS])
```

```python
import functools
import math

import jax
import jax.numpy as jnp
from jax import lax
from jax.experimental import pallas as pl
from jax.experimental.pallas import tpu as pltpu

F32 = jnp.float32
BF16 = jnp.bfloat16
MXU_DTYPE = jnp.bfloat16
EXCHANGE_DTYPE = jnp.bfloat16

EPS = 1e-6
HG_HEADS, HG_D = 4, 128
FOX_HEADS, FOX_DH = 8, 64
MEM_HEADS, MEM_DH = 4, 128
HG_CHUNK = 64
FOX_BLOCK = 256
LANE = 128
FFN_GROUP = 256
FLAT_W = 1024
VMEM_LIMIT = 56 * 2 ** 20
NEG = -1e30
N_CHIPS = 4

ADAM_LR, ADAM_B1, ADAM_B2, ADAM_EPS, ADAM_WD, ADAM_STEP = 0.001, 0.9, 0.999, 1e-08, 0.01, 10

MESH = pl.DeviceIdType.MESH
ANY = pl.BlockSpec(memory_space=pl.ANY)


def _mx(x):
    return x.astype(MXU_DTYPE)


def _dot(a, b, ca, cb):
    return lax.dot_general(_mx(a), _mx(b), (((ca,), (cb,)), ((), ())), preferred_element_type=F32)


def _nn(a, b):
    return _dot(a, b, 1, 0)


def _nt(a, b):
    return _dot(a, b, 1, 1)


def _tn(a, b):
    return _dot(a, b, 0, 0)


def _tri_dot(tri_bf, x):
    hi = x.astype(BF16)
    r = x - hi.astype(F32)
    mid = r.astype(BF16)
    lo = (r - mid.astype(F32)).astype(BF16)

    def d(v):
        return lax.dot_general(tri_bf, v, (((1,), (0,)), ((), ())), preferred_element_type=F32)

    return d(hi) + d(mid) + d(lo)


def _sig(x):
    return jax.nn.sigmoid(x)


def _tile(dim, pref, unit=LANE):
    if dim <= pref:
        return dim
    t = pref - pref % unit
    while t >= unit:
        if dim % t == 0:
            return t
        t -= unit
    return dim


def _params(n_grid):
    return pltpu.CompilerParams(dimension_semantics=("arbitrary",) * n_grid, vmem_limit_bytes=VMEM_LIMIT)


def _acc(ref, val, first):
    @pl.when(first)
    def _():
        ref[...] = val

    @pl.when(jnp.logical_not(first))
    def _():
        ref[...] += val


class _Rider:
    def __init__(self, inputs, out_shapes, scratch, start, finish):
        self.inputs, self.out_shapes, self.scratch, self.start, self.finish = inputs, out_shapes, scratch, start, finish


def _ride(body, rider, n_in, n_out, grid):
    if rider is None:
        return body
    ri, ro, rs = len(rider.inputs), len(rider.out_shapes), len(rider.scratch)

    def wrapped(*refs):
        a, b, c = n_in + ri, n_in + ri + n_out, n_in + ri + n_out + ro
        base = refs[:n_in] + refs[a:b] + refs[c:len(refs) - rs]
        r_in, r_out, r_scr = refs[n_in:a], refs[b:c], refs[len(refs) - rs:]
        step = pl.program_id(0)
        for ax in range(1, len(grid)):
            step = step * grid[ax] + pl.program_id(ax)

        @pl.when(step == 0)
        def _():
            rider.start(r_in, r_out, r_scr)

        body(*base)

        @pl.when(step == math.prod(grid) - 1)
        def _():
            rider.finish(r_in, r_out, r_scr)

    return wrapped


def _ride_call(body, rider, *, name, grid, in_specs, out_specs, out_shape, scratch, args, aliases=None):
    n_in, n_out = len(in_specs), len(out_specs)
    aliases = aliases or {}
    if rider is None:
        outs = pl.pallas_call(body, name=name, grid=grid, in_specs=in_specs, out_specs=out_specs, out_shape=out_shape,
                              scratch_shapes=scratch, input_output_aliases=aliases,
                              compiler_params=_params(len(grid)))(*args)
        return list(outs), None
    outs = pl.pallas_call(
        _ride(body, rider, n_in, n_out, grid), name=name, grid=grid,
        in_specs=list(in_specs) + [ANY] * len(rider.inputs), out_specs=list(out_specs) + [ANY] * len(rider.out_shapes),
        out_shape=list(out_shape) + list(rider.out_shapes), scratch_shapes=list(scratch) + list(rider.scratch),
        input_output_aliases=aliases, compiler_params=_params(len(grid)),
    )(*args, *rider.inputs)
    return list(outs[:n_out]), list(outs[n_out:])


def _matmul(a, b, *, name, ta=False, tb=False, tm=1024, tn=2048, tk=None, rider=None, b_parts=None, stack_out=False):
    m, k = (a.shape[1], a.shape[0]) if ta else a.shape
    tk = tk or (1024 if ta else 2048)
    stacked_b = b is not None and b.ndim == 3
    if b_parts:
        n, tn = 2 * b_parts[0].shape[1], _tile(b_parts[0].shape[1], tn)
    elif stacked_b:
        n, tn = b.shape[0] * b.shape[2], b.shape[2]
    else:
        n = b.shape[0] if tb else b.shape[1]
        tn = _tile(n, tn)
    tm, tk = _tile(m, tm), _tile(k, tk)
    nk, nj = k // tk, n // tn

    def body(a_ref, *refs):
        o_ref = refs[-1]
        if b_parts:
            bv = jnp.where(pl.program_id(1) < nj // 2, refs[0][...], refs[1][...])
        else:
            bv = refs[0][...]
        p = _dot(a_ref[...], bv, 0 if ta else 1, 1 if tb else 0)
        if nk == 1:
            o_ref[...] = p
        else:
            _acc(o_ref, p, pl.program_id(2) == 0)

    a_spec = pl.BlockSpec((tk, tm), lambda i, j, kk: (kk, i)) if ta else pl.BlockSpec((tm, tk), lambda i, j, kk: (i, kk))
    if b_parts:
        half = nj // 2
        b_specs = [pl.BlockSpec((tk, tn), lambda i, j, kk: (kk, jnp.minimum(j, half - 1))),
                   pl.BlockSpec((tk, tn), lambda i, j, kk: (kk, jnp.maximum(j - half, 0)))]
        b_args = list(b_parts)
    elif stacked_b:
        b_specs, b_args = [pl.BlockSpec((None, tk, tn), lambda i, j, kk: (j, kk, 0))], [b]
    else:
        b_specs = [pl.BlockSpec((tn, tk), lambda i, j, kk: (j, kk)) if tb else pl.BlockSpec((tk, tn), lambda i, j, kk: (kk, j))]
        b_args = [b]
    if stack_out:
        o_spec, o_sds = pl.BlockSpec((None, tm, tn), lambda i, j, kk: (j, i, 0)), jax.ShapeDtypeStruct((nj, m, tn), F32)
    else:
        o_spec, o_sds = pl.BlockSpec((tm, tn), lambda i, j, kk: (i, j)), jax.ShapeDtypeStruct((m, n), F32)
    outs, extra = _ride_call(body, rider, name=name, grid=(m // tm, nj, nk), in_specs=[a_spec] + b_specs,
                             out_specs=[o_spec], out_shape=[o_sds], scratch=[], args=(a, *b_args))
    return (outs[0], extra) if rider else outs[0]


def _matmul_rows(a_parts, b, *, name, tb, row_ins, vec_ins, epilogue, n_vec_out, tm=512, tk=2048, rider=None):
    m, kp = a_parts[0].shape
    stacked_b = b.ndim == 3
    n = b.shape[1] if stacked_b else (b.shape[0] if tb else b.shape[1])
    tm, tk = _tile(m, tm, 8), (b.shape[2] if stacked_b else _tile(kp, tk))
    nk = kp // tk
    n_a, n_row, n_vec = len(a_parts), len(row_ins), len(vec_ins)

    def body(*refs):
        a_refs, b_refs = refs[:n_a], refs[n_a:2 * n_a]
        rows = refs[2 * n_a:2 * n_a + n_row]
        vecs = refs[2 * n_a + n_row:2 * n_a + n_row + n_vec]
        o_ref = refs[2 * n_a + n_row + n_vec]
        v_refs = refs[2 * n_a + n_row + n_vec + 1:-1]
        acc_ref = refs[-1]
        i, kk = pl.program_id(0), pl.program_id(1)
        p = _dot(a_refs[0][...], b_refs[0][...], 1, 1 if tb else 0)
        for a_ref, b_ref in zip(a_refs[1:], b_refs[1:]):
            p = p + _dot(a_ref[...], b_ref[...], 1, 1 if tb else 0)
        _acc(acc_ref, p, kk == 0)

        @pl.when(kk == nk - 1)
        def _():
            out, vouts = epilogue(acc_ref[...], *[r[...] for r in rows], *[v[...] for v in vecs])
            o_ref[...] = out
            for v_ref, v in zip(v_refs, vouts):
                _acc(v_ref, v, i == 0)

    a_spec = pl.BlockSpec((tm, tk), lambda i, kk: (i, kk))
    if stacked_b:
        b_specs = [pl.BlockSpec((None, n, tk), functools.partial(lambda i, kk, q: (q * nk + kk, 0, 0), q=q))
                   for q in range(n_a)]
    else:
        b_specs = [pl.BlockSpec((n, tk), functools.partial(lambda i, kk, q: (0, q * nk + kk), q=q)) if tb else
                   pl.BlockSpec((tk, n), functools.partial(lambda i, kk, q: (q * nk + kk, 0), q=q)) for q in range(n_a)]
    row = pl.BlockSpec((tm, n), lambda i, kk: (i, 0))
    vec = pl.BlockSpec((1, n), lambda i, kk: (0, 0))
    outs, extra = _ride_call(
        body, rider, name=name, grid=(m // tm, nk),
        in_specs=[a_spec] * n_a + b_specs + [row] * n_row + [vec] * n_vec,
        out_specs=[row] + [vec] * n_vec_out,
        out_shape=[jax.ShapeDtypeStruct((m, n), F32)] + [jax.ShapeDtypeStruct((1, n), F32)] * n_vec_out,
        scratch=[pltpu.VMEM((tm, n), F32)], args=(*a_parts, *([b] * n_a), *row_ins, *vec_ins))
    return outs[0], outs[1:], extra


def _norm_bwd_epilogue(n_dh):
    def epilogue(dh, x, res, *rest):
        for extra in rest[:n_dh]:
            dh = dh + extra
        g = rest[n_dh]
        r = lax.rsqrt(jnp.mean(x * x, axis=-1, keepdims=True) + EPS)
        dhg = dh * g
        dx = res + r * dhg - x * (r * r * r) * jnp.mean(dhg * x, axis=-1, keepdims=True)
        return dx, [jnp.sum(dh * x * r, axis=0, keepdims=True)]

    return epilogue


def _loss_epilogue(y, x1, target):
    d = y.shape[1]
    err = x1 + y - target
    return err * (1.0 / d), [jnp.sum(err * err, axis=0, keepdims=True) * (0.5 / d)]


def _rmsnorm_fwd(x, g, *, name, tm=512):
    n, d = x.shape
    tm = _tile(n, tm, 8)

    def body(x_ref, g_ref, o_ref):
        xv = x_ref[...]
        r = lax.rsqrt(jnp.mean(xv * xv, axis=-1, keepdims=True) + EPS)
        o_ref[...] = (xv * r * g_ref[...]).astype(o_ref.dtype)

    return pl.pallas_call(
        body, name=name, grid=(n // tm,),
        in_specs=[pl.BlockSpec((tm, d), lambda i: (i, 0)), pl.BlockSpec((1, d), lambda i: (0, 0))],
        out_specs=pl.BlockSpec((tm, d), lambda i: (i, 0)),
        out_shape=jax.ShapeDtypeStruct((n, d), MXU_DTYPE),
        compiler_params=_params(1),
    )(x, g)


def _rmsnorm_bwd(x, dhs, g, res, *, name, tm=512):
    n, d = x.shape
    tm = _tile(n, tm, 8)
    n_dh = len(dhs)
    has_res = res is not None

    def body(*refs):
        x_ref, dh_refs, g_ref = refs[0], refs[1:1 + n_dh], refs[1 + n_dh]
        res_ref = refs[2 + n_dh] if has_res else None
        dx_ref, dg_ref = refs[-2], refs[-1]
        xv = x_ref[...]
        dh = dh_refs[0][...].astype(F32)
        for r_ in dh_refs[1:]:
            dh = dh + r_[...].astype(F32)
        r = lax.rsqrt(jnp.mean(xv * xv, axis=-1, keepdims=True) + EPS)
        dhg = dh * g_ref[...]
        dx = r * dhg - xv * (r * r * r) * jnp.mean(dhg * xv, axis=-1, keepdims=True)
        if has_res:
            dx = dx + res_ref[...]
        dx_ref[...] = dx
        _acc(dg_ref, jnp.sum(dh * xv * r, axis=0, keepdims=True), pl.program_id(0) == 0)

    row = pl.BlockSpec((tm, d), lambda i: (i, 0))
    vec = pl.BlockSpec((1, d), lambda i: (0, 0))
    ins = [x] + list(dhs) + [g] + ([res] if has_res else [])
    return pl.pallas_call(
        body, name=name, grid=(n // tm,),
        in_specs=[row] * (1 + n_dh) + [vec] + ([row] if has_res else []),
        out_specs=[row, vec],
        out_shape=[jax.ShapeDtypeStruct((n, d), F32), jax.ShapeDtypeStruct((1, d), F32)],
        compiler_params=_params(1),
    )(*ins)


def _adamw(w, g, m, v, *, name, tr=256):
    _, r, c = w.shape
    c1 = 1.0 / (1.0 - ADAM_B1 ** ADAM_STEP)
    c2 = 1.0 / (1.0 - ADAM_B2 ** ADAM_STEP)

    def body(w_ref, g_ref, m_ref, v_ref, d_ref, mo_ref, vo_ref):
        gv = g_ref[...]
        mn = ADAM_B1 * m_ref[...] + (1.0 - ADAM_B1) * gv
        vn = ADAM_B2 * v_ref[...] + (1.0 - ADAM_B2) * (gv * gv)
        d_ref[...] = -ADAM_LR * ((mn * c1) / (jnp.sqrt(vn * c2) + ADAM_EPS) + ADAM_WD * w_ref[...])
        mo_ref[...] = mn
        vo_ref[...] = vn

    if r % 8 == 0 or r < 8:
        tr = _tile(r, tr, 8)
        grid, blk = (r // tr,), pl.BlockSpec((1, tr, c), lambda i: (0, i, 0))
    else:
        tc = _tile(c, tr)
        grid, blk = (c // tc,), pl.BlockSpec((1, r, tc), lambda i: (0, 0, i))
    sds = jax.ShapeDtypeStruct((1, r, c), F32)
    return pl.pallas_call(
        body, name=name, grid=grid, in_specs=[blk] * 4, out_specs=[blk] * 3, out_shape=[sds] * 3,
        compiler_params=_params(1),
    )(w, g, m, v)


def _bdot(a, b, ca, cb):
    return lax.dot_general(_mx(a), _mx(b), (((ca,), (cb,)), ((0,), (0,))), preferred_element_type=F32)


def _split2(x):
    hi = x.astype(BF16)
    return hi, (x - hi.astype(F32)).astype(BF16)


def _bdotp(a, b, ca, cb):
    def d(u, v):
        return lax.dot_general(u, v, (((ca,), (cb,)), ((0,), (0,))), preferred_element_type=F32)

    return d(a[0], b[0]) + d(a[0], b[1]) + d(a[1], b[0])


def _tri_dot_b(tri_bf, x):
    hi = x.astype(BF16)
    r = x - hi.astype(F32)
    mid = r.astype(BF16)
    lo = (r - mid.astype(F32)).astype(BF16)

    def d(v):
        return lax.dot_general(tri_bf, v, (((2,), (1,)), ((0,), (0,))), preferred_element_type=F32)

    return d(hi) + d(mid) + d(lo)


def _hgrn_forward(hq, hf, hi, lbv, tril, tril_bf):
    nc, c, _ = hq.shape
    sf = _sig(hf)
    f = lbv + (1.0 - lbv) * sf
    k = 1.0 - f
    gcum = _tri_dot_b(tril_bf, jnp.log(f))
    mid = gcum[:, c // 2 - 1:c // 2, :]
    glast = gcum[:, c - 1:c, :]
    sq = _sig(hq)
    q = hq * sq
    e_q = jnp.exp(gcum - mid)
    e_k = jnp.exp(mid - gcum)
    qe, ke = q * e_q, k * e_k
    a = jnp.where(tril, _bdot(qe, ke, 2, 2), 0.0)
    e_g = jnp.exp(gcum)
    qg = q * e_g
    e_s = jnp.exp(glast - gcum)
    kg = k * e_s
    e_l = jnp.exp(glast)
    upd = _bdot(hi, kg, 1, 1)
    st = jnp.zeros((HG_D, HG_D), F32)
    states = []
    for n in range(nc):
        states.append(st)
        st = st * e_l[n] + upd[n]
    st_all = jnp.stack(states)
    o = _bdot(a, hi, 2, 1) + _bdot(qg, st_all, 2, 2)
    return dict(sf=sf, f=f, k=k, sq=sq, q=q, e_q=e_q, e_k=e_k, qe=qe, ke=ke, a=a, e_g=e_g, qg=qg, o=o,
                e_s=e_s, kg=kg, e_l=e_l, st_all=st_all)


def _hgrn_specs(t_, col0):
    def col(off):
        return pl.BlockSpec((1, t_, LANE), lambda h, b: (b, 0, col0 + 4 * h + off))

    vec = pl.BlockSpec((2, LANE), lambda h, b: (0, h))
    one = pl.BlockSpec((1, LANE), lambda h, b: (0, 0))
    blk = pl.BlockSpec((1, t_, LANE), lambda h, b: (b, 0, h))
    return col, vec, one, blk


def _chunk_masks(nc, c):
    row = lax.broadcasted_iota(jnp.int32, (nc, c, c), 1)
    cl = lax.broadcasted_iota(jnp.int32, (nc, c, c), 2)
    return row >= cl, (row >= cl).astype(BF16), (row <= cl).astype(BF16)


def _hgrn_fwd(zm, lb, gn, hw, col0):
    b_, t_, _ = zm.shape
    c = min(HG_CHUNK, t_)
    nc = t_ // c
    col, vec, one, blk = _hgrn_specs(t_, col0)

    def body(q_ref, f_ref, i_ref, g_ref, lb_ref, gn_ref, y_ref):
        lbv, gnv = _sig(lb_ref[0:1, :] - lb_ref[1:2, :]), gn_ref[...]
        tril, tril_bf, _ = _chunk_masks(nc, c)
        chunks = lambda ref: ref[0].reshape(nc, c, LANE)
        o = _hgrn_forward(chunks(q_ref), chunks(f_ref), chunks(i_ref), lbv, tril, tril_bf)["o"]
        r = lax.rsqrt(jnp.mean(o * o, axis=-1, keepdims=True) + EPS)
        hg = chunks(g_ref)
        y_ref[0] = (o * r * gnv * (hg * _sig(hg))).reshape(t_, LANE)

    return pl.pallas_call(
        body, name="hgrn_fwd", grid=(HG_HEADS, b_),
        in_specs=[col(0), col(1), col(2), col(3), vec, one], out_specs=blk,
        out_shape=jax.ShapeDtypeStruct((b_, t_, hw), F32),
        compiler_params=_params(2),
    )(zm, zm, zm, zm, lb, gn)


def _hgrn_bwd(zm, dy, lb, gn, hw, col0, dz, rider=None):
    b_, t_, _ = zm.shape
    c = min(HG_CHUNK, t_)
    nc = t_ // c
    col, vec, one, blk = _hgrn_specs(t_, col0)

    def body(q_ref, f_ref, i_ref, g_ref, dy_ref, lb_ref, gn_ref, _, dz_ref, dlb_ref, dgn_ref):
        h, b = pl.program_id(0), pl.program_id(1)
        lbv, gnv = _sig(lb_ref[0:1, :] - lb_ref[1:2, :]), gn_ref[...]
        tril, tril_bf, triu_bf = _chunk_masks(nc, c)
        last_row = lax.broadcasted_iota(jnp.int32, (nc, c, LANE), 1) == c - 1
        chunks = lambda ref: ref[0].reshape(nc, c, LANE)
        flat = lambda x: x.reshape(t_, LANE)
        hq, hi, hg = chunks(q_ref), chunks(i_ref), chunks(g_ref)
        p = _hgrn_forward(hq, chunks(f_ref), hi, lbv, tril, tril_bf)
        o, q, k, st_all, e_l = p["o"], p["q"], p["k"], p["st_all"], p["e_l"]
        dyv = chunks(dy_ref)
        sg = _sig(hg)
        r = lax.rsqrt(jnp.mean(o * o, axis=-1, keepdims=True) + EPS)
        dn = dyv * (hg * sg)
        dz_ref[0, :, 3 * LANE:] = flat(dyv * (o * r * gnv) * (sg * (1.0 + hg * (1.0 - sg)))).astype(dz_ref.dtype)
        dgn = jnp.sum(flat(dn * o * r), axis=0, keepdims=True)
        dng = dn * gnv
        do = r * dng - o * (r * r * r) * jnp.mean(dng * o, axis=-1, keepdims=True)
        do2, hi2, qg2, ke2, qe2, st2 = (_split2(t) for t in (do, hi, p["qg"], p["ke"], p["qe"], st_all))
        back = _bdotp(do2, qg2, 1, 1)
        dst = jnp.zeros((HG_D, HG_D), F32)
        dsts = [None] * nc
        for n in range(nc - 1, -1, -1):
            dsts[n] = dst
            dst = dst * e_l[n] + back[n]
        dst_all = jnp.stack(dsts)
        da = jnp.where(tril, _bdotp(do2, hi2, 2, 2), 0.0)
        da2 = _split2(da)
        dq = _bdotp(da2, ke2, 2, 1) * p["e_q"] + _bdotp(do2, st2, 2, 1) * p["e_g"]
        dk_state = _bdotp(hi2, _split2(dst_all), 2, 1) * p["e_s"]
        dk = _bdotp(da2, qe2, 1, 1) * p["e_k"] + dk_state
        dz_ref[0, :, 2 * LANE:3 * LANE] = flat(_bdot(p["a"], do, 1, 1) + _bdot(p["kg"], dst_all, 2, 2)).astype(dz_ref.dtype)
        extra = (jnp.sum(k * dk_state, axis=1, keepdims=True) + e_l * jnp.sum(st_all * dst_all, axis=1, keepdims=True))
        dgc = q * dq - k * dk + jnp.where(last_row, extra, 0.0)
        dfv = _tri_dot_b(triu_bf, dgc) / p["f"] - dk
        sf, sq = p["sf"], p["sq"]
        dz_ref[0, :, LANE:2 * LANE] = flat(dfv * (1.0 - lbv) * sf * (1.0 - sf)).astype(dz_ref.dtype)
        dlb = jnp.sum(flat(dfv * (1.0 - sf)), axis=0, keepdims=True)
        dz_ref[0, :, :LANE] = flat(dq * (sq * (1.0 + hq * (1.0 - sq)))).astype(dz_ref.dtype)
        dl0 = dlb * lbv * (1.0 - lbv)
        _acc(dlb_ref, jnp.concatenate([dl0, -dl0], axis=0), b == 0)
        _acc(dgn_ref, dgn, jnp.logical_and(b == 0, h == 0))

    return _ride_call(
        body, rider, name="hgrn_bwd", grid=(HG_HEADS, b_),
        in_specs=[col(0), col(1), col(2), col(3), blk, vec, one, ANY],
        out_specs=[pl.BlockSpec((1, t_, 4 * LANE), lambda h, b: (b, 0, col0 // 4 + h)), vec, one],
        out_shape=[jax.ShapeDtypeStruct(dz.shape, dz.dtype), jax.ShapeDtypeStruct((2, hw), F32),
                   jax.ShapeDtypeStruct((1, LANE), F32)],
        scratch=[], args=(zm, zm, zm, zm, dy, lb, gn, dz), aliases={7: 0})


def _fox_logf(x):
    return jnp.minimum(x, 0.0) - jnp.log(1.0 + jnp.exp(-jnp.abs(x)))


def _fox_prep(zf, bias):
    b_, t_, _ = zf.shape
    tb = min(FOX_BLOCK, t_)
    nb = t_ // tb

    def body(z_ref, b_ref, fc_ref):
        tril_bf = (lax.broadcasted_iota(jnp.int32, (tb, tb), 0) >= lax.broadcasted_iota(jnp.int32, (tb, tb), 1)).astype(BF16)
        bv = b_ref[...]

        def blk(i, carry):
            rows = pl.ds(pl.multiple_of(i * tb, tb), tb)
            fc = _tri_dot(tril_bf, _fox_logf(z_ref[0, rows, :] + bv)) + carry
            fc_ref[0, rows, :] = fc
            return fc[tb - 1:tb, :]

        lax.fori_loop(0, nb, blk, jnp.zeros((1, LANE), F32))

    blk_spec = pl.BlockSpec((1, t_, LANE), lambda b: (b, 0, 0))
    return pl.pallas_call(
        body, name="fox_prep", grid=(b_,),
        in_specs=[blk_spec, pl.BlockSpec((1, LANE), lambda b: (0, 0))], out_specs=blk_spec,
        out_shape=jax.ShapeDtypeStruct((b_, t_, LANE), F32), compiler_params=_params(1),
    )(zf, bias)


def _fox_post(dfc, zf, bias):
    b_, t_, _ = zf.shape
    npair = dfc.shape[1]
    tb = min(FOX_BLOCK, t_)
    nb = t_ // tb

    def body(d_ref, z_ref, b_ref, dz_ref, db_ref):
        triu_bf = (lax.broadcasted_iota(jnp.int32, (tb, tb), 0) <= lax.broadcasted_iota(jnp.int32, (tb, tb), 1)).astype(BF16)
        valid = lax.broadcasted_iota(jnp.int32, (tb, LANE), 1) < FOX_HEADS
        bv = b_ref[...]

        def blk(m, carry):
            tail, db = carry
            rows = pl.ds(pl.multiple_of((nb - 1 - m) * tb, tb), tb)
            dfc_rows = d_ref[0, 0, rows, :]
            for p in range(1, npair):
                dfc_rows = dfc_rows + pltpu.roll(d_ref[0, p, rows, :], 2 * p, 1)
            dlf = _tri_dot(triu_bf, dfc_rows) + tail
            dx = jnp.where(valid, dlf * _sig(-(z_ref[0, rows, :] + bv)), 0.0)
            dz_ref[0, rows, :] = dx.astype(dz_ref.dtype)
            return dlf[0:1, :], db + jnp.sum(dx, axis=0, keepdims=True)

        z1 = jnp.zeros((1, LANE), F32)
        _, db = lax.fori_loop(0, nb, blk, (z1, z1))
        _acc(db_ref, db, pl.program_id(0) == 0)

    blk_spec = pl.BlockSpec((1, t_, LANE), lambda b: (b, 0, 0))
    vec = pl.BlockSpec((1, LANE), lambda b: (0, 0))
    return pl.pallas_call(
        body, name="fox_post", grid=(b_,),
        in_specs=[pl.BlockSpec((1, npair, t_, LANE), lambda b: (b, 0, 0, 0)), blk_spec, vec], out_specs=[blk_spec, vec],
        out_shape=[jax.ShapeDtypeStruct((b_, t_, LANE), MXU_DTYPE), jax.ShapeDtypeStruct((1, LANE), F32)],
        compiler_params=_params(1),
    )(dfc, zf, bias)


FOX_TILE = 256
FOX_BAND = 512
AUG = 64


def _head_mean_matrix():
    r = lax.broadcasted_iota(jnp.int32, (LANE, LANE), 0) // FOX_DH
    c = lax.broadcasted_iota(jnp.int32, (LANE, LANE), 1) // FOX_DH
    return (r == c).astype(BF16)


def _dot_right_exact(x, m_bf):
    hi = x.astype(BF16)
    lo = (x - hi.astype(F32)).astype(BF16)

    def d(v):
        return lax.dot_general(v, m_bf, (((1,), (0,)), ((), ())), preferred_element_type=F32)

    return d(hi) + d(lo)


def _pair_norm(x, g2, bd):
    r = lax.rsqrt(_dot_right_exact(x * x, bd) * (1.0 / FOX_DH) + EPS)
    return x * r * g2, r


def _pair_norm_bwd(x, r, dy, g2, bd):
    dyg = dy * g2
    dx = r * dyg - x * (r * r * r) * (_dot_right_exact(dyg * x, bd) * (1.0 / FOX_DH))
    return dx, jnp.sum(dy * x * r, axis=0, keepdims=True)


def _head_lanes(xn, hh):
    return xn if hh == 0 else pltpu.roll(xn, FOX_DH, 1)


def _split3(x):
    hi = x.astype(BF16).astype(F32)
    mid = (x - hi).astype(BF16).astype(F32)
    return hi, mid, x - hi - mid


def _fox_operands(q_ref, k_ref, v_ref, fc_ref, gq2, gk2, p, qa, ka, va):
    t_ = q_ref.shape[1]
    bd = _head_mean_matrix()
    lane = lax.broadcasted_iota(jnp.int32, (t_, LANE), 1)
    qx, kx = q_ref[0], k_ref[0]
    qn, rq = _pair_norm(qx, gq2, bd)
    kn, rk = _pair_norm(kx, gk2, bd)
    vv = v_ref[0]
    q_aug = jnp.where(jnp.logical_and(lane >= AUG, lane < AUG + 3), 1.0, 0.0)
    for hh in range(2):
        fcol = jnp.sum(jnp.where(lane == 2 * p + hh, fc_ref[0], 0.0), axis=-1, keepdims=True)
        hi, mid, lo = _split3(-fcol)
        k_aug = jnp.where(lane == AUG, hi, jnp.where(lane == AUG + 1, mid, jnp.where(lane == AUG + 2, lo,
                          jnp.where(lane == AUG + 3, 1.0, 0.0))))
        head = lane < FOX_DH
        qa[hh] = jnp.where(head, _head_lanes(qn, hh), q_aug).astype(MXU_DTYPE)
        ka[hh] = jnp.where(head, _head_lanes(kn, hh), k_aug).astype(MXU_DTYPE)
        va[hh] = jnp.where(head, _head_lanes(vv, hh), 0.0).astype(MXU_DTYPE)
    return bd, lane, qx, kx, rq, rk


def _fox_specs(t_, fw, col0):
    npair = fw // LANE

    def col(off):
        return pl.BlockSpec((1, t_, LANE), lambda b, p: (b, 0, col0 + 3 * p + off))

    pair = pl.BlockSpec((1, t_, LANE), lambda b, p: (b, 0, p))
    full = pl.BlockSpec((1, t_, LANE), lambda b, p: (b, 0, 0))
    gvec = pl.BlockSpec((1, LANE), lambda b, p: (0, 0))
    lse = pl.BlockSpec((1, 1, t_, LANE), lambda b, p: (b, p, 0, 0))
    return col, pair, full, gvec, lse


def _fox_fwd(zm, fc, gq2, gk2, fw, col0, rider=None):
    b_, t_, _ = zm.shape
    npair = fw // LANE
    tq = min(FOX_TILE, t_)
    bw = min(FOX_BAND, t_)
    nband, tpb = t_ // bw, bw // tq
    scale = FOX_DH ** -0.5
    col, pair, full, gvec, lse_spec = _fox_specs(t_, fw, col0)

    def body(q_ref, k_ref, v_ref, fc_ref, gq_ref, gk_ref, o_ref, lse_ref, qa, ka, va):
        p = pl.program_id(1)
        _fox_operands(q_ref, k_ref, v_ref, fc_ref, gq_ref[...] * scale, gk_ref[...], p, qa, ka, va)
        ahead = lax.broadcasted_iota(jnp.int32, (tq, bw), 1) - lax.broadcasted_iota(jnp.int32, (tq, bw), 0)
        lane = lax.broadcasted_iota(jnp.int32, (tq, LANE), 1)

        for band in range(nband):
            c0 = band * bw

            def qtile(ii, _, c0=c0):
                r0 = pl.multiple_of(c0 + ii * tq, tq)
                rows = pl.ds(r0, tq)
                keep = ahead <= r0 - c0
                res = []
                for hh in range(2):
                    qb = qa[hh, rows, :]
                    s_b = jnp.where(keep, _nt(qb, ka[hh, c0:c0 + bw, :]), NEG)
                    m = jnp.max(s_b, axis=-1, keepdims=True)
                    if c0:
                        s_a = _nt(qb, ka[hh, 0:c0, :])
                        m = jnp.maximum(m, jnp.max(s_a, axis=-1, keepdims=True))
                    p_b = jnp.exp(s_b - m)
                    l = jnp.sum(p_b, axis=-1, keepdims=True)
                    acc = _nn(p_b, va[hh, c0:c0 + bw, :])
                    if c0:
                        p_a = jnp.exp(s_a - m)
                        l = l + jnp.sum(p_a, axis=-1, keepdims=True)
                        acc = acc + _nn(p_a, va[hh, 0:c0, :])
                    res.append((acc / l, m + jnp.log(l)))
                (o0, e0), (o1, e1) = res
                o_ref[0, rows, :] = jnp.where(lane < FOX_DH, o0, pltpu.roll(o1, FOX_DH, 1))
                lse_ref[0, 0, rows, :] = jnp.where(lane == 0, e0, jnp.where(lane == 1, e1, 0.0))
                return 0

            lax.fori_loop(0, tpb, qtile, 0)

    return _ride_call(
        body, rider, name="fox_fwd", grid=(b_, npair),
        in_specs=[col(0), col(1), col(2), full, gvec, gvec],
        out_specs=[pair, lse_spec],
        out_shape=[jax.ShapeDtypeStruct((b_, t_, fw), F32), jax.ShapeDtypeStruct((b_, npair, t_, LANE), F32)],
        scratch=[pltpu.VMEM((2, t_, LANE), MXU_DTYPE)] * 3, args=(zm, zm, zm, fc, gq2, gk2))


def _norm_bwd(x, dy, g):
    r = lax.rsqrt(jnp.mean(x * x, axis=-1, keepdims=True) + EPS)
    dyg = dy * g
    dx = r * dyg - x * (r * r * r) * jnp.mean(dyg * x, axis=-1, keepdims=True)
    return dx, jnp.sum(dy * x * r, axis=0, keepdims=True)


def _fox_bwd(zm, o, do, lse, fc, gq2, gk2, fw, col0, dz, rider=None):
    b_, t_, _ = zm.shape
    npair = fw // LANE
    tq = min(FOX_TILE, t_)
    nb = t_ // tq
    bw = min(FOX_BAND, t_)
    nband, tpb = t_ // bw, bw // tq
    scale = FOX_DH ** -0.5
    col, pair, full, gvec, lse_spec = _fox_specs(t_, fw, col0)

    def body(q_ref, k_ref, v_ref, o_ref, do_ref, lse_ref, fc_ref, gq_ref, gk_ref, _,
             dz_ref, dfc_ref, dgq_ref, dgk_ref, qa, ka, va, da, rowv, dq_acc, dk_acc, dv_acc):
        b, p = pl.program_id(0), pl.program_id(1)
        gq2v, gk2v = gq_ref[...] * scale, gk_ref[...]
        bd, lane, qx, kx, rq, rk = _fox_operands(q_ref, k_ref, v_ref, fc_ref, gq2v, gk2v, p, qa, ka, va)
        head = lane < FOX_DH
        dov = do_ref[0]
        dsum = _dot_right_exact(dov * o_ref[0], bd)
        eye = (lax.broadcasted_iota(jnp.int32, (tq, tq), 0) == lax.broadcasted_iota(jnp.int32, (tq, tq), 1)).astype(F32)
        for hh in range(2):
            da[hh] = jnp.where(head, _head_lanes(dov, hh), 0.0).astype(MXU_DTYPE)
            for blk in range(nb):
                rs = slice(blk * tq, (blk + 1) * tq)
                rowv[2 * hh:2 * hh + 1, rs] = jnp.sum(eye * lse_ref[0, 0, rs, hh:hh + 1], axis=0, keepdims=True)
                rowv[2 * hh + 1:2 * hh + 2, rs] = jnp.sum(eye * dsum[rs, hh * FOX_DH:hh * FOX_DH + 1], axis=0, keepdims=True)
        dq_acc[...] = jnp.zeros(dq_acc.shape, F32)
        ahead = lax.broadcasted_iota(jnp.int32, (tq, bw), 1) - lax.broadcasted_iota(jnp.int32, (tq, bw), 0)

        def part(hh, kb, vb, lo, hi, keep):
            qm, dm = qa[hh, lo:hi, :], da[hh, lo:hi, :]
            pt = jnp.exp(_nt(kb, qm) - rowv[2 * hh:2 * hh + 1, lo:hi])
            if keep is not None:
                pt = jnp.where(keep, pt, 0.0)
            dst = pt * (_nt(vb, dm) - rowv[2 * hh + 1:2 * hh + 2, lo:hi])
            dq_acc[hh, lo:hi, :] += _tn(dst, kb)
            return _nn(dst, qm), _nn(pt, dm)

        for band in range(nband):
            c0 = band * bw

            def kvtile(jj, _, c0=c0):
                r0 = pl.multiple_of(c0 + jj * tq, tq)
                rows = pl.ds(r0, tq)
                keep = ahead >= r0 - c0
                for hh in range(2):
                    kb, vb = ka[hh, rows, :], va[hh, rows, :]
                    dk_t, dv_t = part(hh, kb, vb, c0, c0 + bw, keep)
                    if c0 + bw < t_:
                        dk_u, dv_u = part(hh, kb, vb, c0 + bw, t_, None)
                        dk_t, dv_t = dk_t + dk_u, dv_t + dv_u
                    dk_acc[hh, rows, :] = dk_t
                    dv_acc[hh, rows, :] = dv_t
                return 0

            lax.fori_loop(0, tpb, kvtile, 0)

        dq0, dq1, dk0, dk1 = dq_acc[0], dq_acc[1], dk_acc[0], dk_acc[1]
        dqn = jnp.where(head, dq0, pltpu.roll(dq1, FOX_DH, 1))
        dkn = jnp.where(head, dk0, pltpu.roll(dk1, FOX_DH, 1))
        dqx, gq_part = _pair_norm_bwd(qx, rq, dqn, gq2v, bd)
        dkx, gk_part = _pair_norm_bwd(kx, rk, dkn, gk2v, bd)
        dz_ref[0, :, :LANE] = dqx.astype(dz_ref.dtype)
        dz_ref[0, :, LANE:2 * LANE] = dkx.astype(dz_ref.dtype)
        dz_ref[0, :, 2 * LANE:] = jnp.where(head, dv_acc[0], pltpu.roll(dv_acc[1], FOX_DH, 1)).astype(dz_ref.dtype)

        def bias_grad(dqh, dkh):
            return dqh[:, AUG + 3:AUG + 4] - dkh[:, AUG:AUG + 1]

        dfc_ref[0, 0] = jnp.where(lane == 0, bias_grad(dq0, dk0), jnp.where(lane == 1, bias_grad(dq1, dk1), 0.0))
        first = jnp.logical_and(b == 0, p == 0)
        _acc(dgq_ref, gq_part * scale, first)
        _acc(dgk_ref, gk_part, first)

    gs = jax.ShapeDtypeStruct((1, LANE), F32)
    return _ride_call(
        body, rider, name="fox_bwd", grid=(b_, npair),
        in_specs=[col(0), col(1), col(2), pair, pair, lse_spec, full, gvec, gvec, ANY],
        out_specs=[pl.BlockSpec((1, t_, 3 * LANE), lambda b, p: (b, 0, col0 // 3 + p)), lse_spec, gvec, gvec],
        out_shape=[jax.ShapeDtypeStruct(dz.shape, dz.dtype), jax.ShapeDtypeStruct((b_, npair, t_, LANE), F32), gs, gs],
        scratch=[pltpu.VMEM((2, t_, LANE), MXU_DTYPE)] * 4
        + [pltpu.VMEM((8, t_), F32)] + [pltpu.VMEM((2, t_, LANE), F32)] * 3,
        args=(zm, zm, zm, o, do, lse, fc, gq2, gk2, dz), aliases={9: 0})


def _mem_specs(t_, m_, mw, col0):
    nh = mw // LANE
    qcol = pl.BlockSpec((1, t_, LANE), lambda b, h: (b, 0, col0 + h))
    kcol = pl.BlockSpec((1, m_, LANE), lambda b, h: (b, 0, h))
    vcol = pl.BlockSpec((1, m_, LANE), lambda b, h: (b, 0, nh + h))
    ycol = pl.BlockSpec((1, t_, LANE), lambda b, h: (b, 0, h))
    gvec = pl.BlockSpec((1, LANE), lambda b, h: (0, 0))
    return qcol, kcol, vcol, ycol, gvec


def _mem_fwd(zm, mkv, gq, gk, mw, col0):
    b_, t_, _ = zm.shape
    m_ = mkv.shape[1]
    tq = min(512, t_)
    nb = t_ // tq
    scale = MEM_DH ** -0.5
    qcol, kcol, vcol, ycol, gvec = _mem_specs(t_, m_, mw, col0)

    def body(q_ref, k_ref, v_ref, gq_ref, gk_ref, y_ref):
        gqv, gkv = gq_ref[...] * scale, gk_ref[...]
        kv = k_ref[0]
        kn = _mx(kv * lax.rsqrt(jnp.mean(kv * kv, axis=-1, keepdims=True) + EPS) * gkv)
        vv = _mx(v_ref[0])

        def blk(i, _):
            rows = pl.ds(pl.multiple_of(i * tq, tq), tq)
            qv = q_ref[0, rows, :]
            s = _nt(qv * lax.rsqrt(jnp.mean(qv * qv, axis=-1, keepdims=True) + EPS) * gqv, kn)
            e = jnp.exp(s - jnp.max(s, axis=-1, keepdims=True))
            y_ref[0, rows, :] = _nn(e / jnp.sum(e, axis=-1, keepdims=True), vv)
            return 0

        lax.fori_loop(0, nb, blk, 0)

    return pl.pallas_call(
        body, name="mem_fwd", grid=(b_, MEM_HEADS), in_specs=[qcol, kcol, vcol, gvec, gvec], out_specs=ycol,
        out_shape=jax.ShapeDtypeStruct((b_, t_, mw), F32), compiler_params=_params(2),
    )(zm, mkv, mkv, gq, gk)


def _mem_bwd(zm, mkv, dy, gq, gk, mw, col0, dz):
    b_, t_, _ = zm.shape
    m_ = mkv.shape[1]
    tq = min(512, t_)
    nb = t_ // tq
    scale = MEM_DH ** -0.5
    qcol, kcol, vcol, ycol, gvec = _mem_specs(t_, m_, mw, col0)

    def body(q_ref, k_ref, v_ref, dy_ref, gq_ref, gk_ref, _, dq_ref, dk_ref, dv_ref, dgq_ref, dgk_ref):
        gqv, gkv = gq_ref[...] * scale, gk_ref[...]
        kv = k_ref[0]
        kn = _mx(kv * lax.rsqrt(jnp.mean(kv * kv, axis=-1, keepdims=True) + EPS) * gkv)
        vv = _mx(v_ref[0])

        def blk(i, carry):
            dkn, dvv, dgq = carry
            rows = pl.ds(pl.multiple_of(i * tq, tq), tq)
            qv = q_ref[0, rows, :]
            qn = _mx(qv * lax.rsqrt(jnp.mean(qv * qv, axis=-1, keepdims=True) + EPS) * gqv)
            s = _nt(qn, kn)
            e = jnp.exp(s - jnp.max(s, axis=-1, keepdims=True))
            pm = e / jnp.sum(e, axis=-1, keepdims=True)
            dob = _mx(dy_ref[0, rows, :])
            dp = _nt(dob, vv)
            ds = pm * (dp - jnp.sum(dp * pm, axis=-1, keepdims=True))
            dqv, gq_part = _norm_bwd(qv, _nn(ds, kn), gqv)
            dq_ref[0, rows, :] = dqv.astype(dq_ref.dtype)
            return dkn + _tn(ds, qn), dvv + _tn(pm, dob), dgq + gq_part * scale

        z = jnp.zeros((m_, LANE), F32)
        dkn, dvv, dgq = lax.fori_loop(0, nb, blk, (z, z, jnp.zeros((1, LANE), F32)))
        dkv, dgk = _norm_bwd(kv, dkn, gkv)
        dk_ref[0] = dkv
        dv_ref[0] = dvv
        first = jnp.logical_and(pl.program_id(0) == 0, pl.program_id(1) == 0)
        _acc(dgq_ref, dgq, first)
        _acc(dgk_ref, dgk, first)

    kblk = pl.BlockSpec((1, m_, LANE), lambda b, h: (b, 0, h))
    gs = jax.ShapeDtypeStruct((1, LANE), F32)
    ks = jax.ShapeDtypeStruct((b_, m_, mw), F32)
    return pl.pallas_call(
        body, name="mem_bwd", grid=(b_, MEM_HEADS), in_specs=[qcol, kcol, vcol, ycol, gvec, gvec, ANY],
        out_specs=[qcol, kblk, kblk, gvec, gvec],
        out_shape=[jax.ShapeDtypeStruct(dz.shape, dz.dtype), ks, ks, gs, gs], input_output_aliases={6: 0},
        compiler_params=_params(2),
    )(zm, mkv, mkv, dy, gq, gk, dz)


def _merge_specs(tm, d, w, gcol):
    row_d = pl.BlockSpec((tm, d), lambda i: (i, 0))
    row_w = pl.BlockSpec((tm, w), lambda i: (i, 0))
    gates = [pl.BlockSpec((tm, d), functools.partial(lambda i, k: (i, gcol + k), k=k)) for k in range(3)]
    w_br = pl.BlockSpec((w, d), lambda i: (0, 0))
    w_o = pl.BlockSpec((d, d), lambda i: (0, 0))
    return row_d, row_w, gates, w_br, w_o


def _merge_fwd(x, ys, zm, w_brs, w_out, gcol, g_next, tm=256):
    n, d = x.shape
    w = ys[0].shape[1]
    tm = _tile(n, tm, 8)
    row_d, row_w, gates, w_br, w_o = _merge_specs(tm, d, w, gcol)

    def body(x_ref, ya, yb, yc, g0, g1, g2, wa, wb, wc, wo, gn_ref, x1_ref, mg_ref, h_ref):
        mg = (_sig(g0[...]) * _nn(ya[...], wa[...]) + _sig(g1[...]) * _nn(yb[...], wb[...])
              + _sig(g2[...]) * _nn(yc[...], wc[...]))
        mg_ref[...] = mg.astype(mg_ref.dtype)
        x1 = x_ref[...] + _nn(mg, wo[...])
        x1_ref[...] = x1
        h_ref[...] = (x1 * lax.rsqrt(jnp.mean(x1 * x1, axis=-1, keepdims=True) + EPS) * gn_ref[...]).astype(h_ref.dtype)

    half = jax.ShapeDtypeStruct((n, d), MXU_DTYPE)
    return pl.pallas_call(
        body, name="merge_fwd", grid=(n // tm,),
        in_specs=[row_d, row_w, row_w, row_w] + gates + [w_br, w_br, w_br, w_o, pl.BlockSpec((1, d), lambda i: (0, 0))],
        out_specs=[row_d, row_d, row_d],
        out_shape=[jax.ShapeDtypeStruct((n, d), F32), half, half],
        compiler_params=_params(1),
    )(x, *ys, zm, zm, zm, *w_brs, w_out, g_next)


def _merge_bwd(dx1, ys, zm, w_brs, w_out, gcol, tm=256):
    n, d = dx1.shape
    w = ys[0].shape[1]
    tm = _tile(n, tm, 8)
    row_d, row_w, gates, w_br, w_o = _merge_specs(tm, d, w, gcol)

    def body(dx_ref, ya, yb, yc, g0, g1, g2, wa, wb, wc, wo, dgl_ref, dpa, dpb, dpc, dya, dyb, dyc):
        dm = _nt(dx_ref[...], wo[...])
        for k, (y, g, wr, dp_ref, dy_ref) in enumerate(((ya, g0, wa, dpa, dya), (yb, g1, wb, dpb, dyb),
                                                        (yc, g2, wc, dpc, dyc))):
            sg = _sig(g[...])
            pr = _nn(y[...], wr[...])
            dgl_ref[:, k * d:(k + 1) * d] = (dm * pr * sg * (1.0 - sg)).astype(dgl_ref.dtype)
            dp = (dm * sg).astype(dp_ref.dtype)
            dp_ref[...] = dp
            dy_ref[...] = _nt(dp, wr[...])

    sd = jax.ShapeDtypeStruct((n, d), MXU_DTYPE)
    sw = jax.ShapeDtypeStruct((n, w), F32)
    return pl.pallas_call(
        body, name="merge_bwd", grid=(n // tm,),
        in_specs=[row_d, row_w, row_w, row_w] + gates + [w_br, w_br, w_br, w_o],
        out_specs=[pl.BlockSpec((tm, 3 * d), lambda i: (i, 0)), row_d, row_d, row_d, row_w, row_w, row_w],
        out_shape=[jax.ShapeDtypeStruct((n, zm.shape[1]), MXU_DTYPE), sd, sd, sd, sw, sw, sw],
        compiler_params=_params(1),
    )(dx1, *ys, zm, zm, zm, *w_brs, w_out)


CONV_ROWS = 512
HALO = 8


def _ext(ref, r0, t_):
    rc = min(CONV_ROWS, t_)
    a, b = max(r0 - HALO, 0), min(r0 + rc + HALO, t_)
    parts = []
    if r0 - HALO < 0:
        parts.append(jnp.zeros((HALO, ref.shape[2]), F32))
    parts.append(ref[0, a:b, :].astype(F32))
    if r0 + rc + HALO > t_:
        parts.append(jnp.zeros((HALO, ref.shape[2]), F32))
    return jnp.concatenate(parts, axis=0) if len(parts) > 1 else parts[0]


def _gelu_parts(ac):
    e = jnp.exp(-0.5 * ac * ac)
    t = 1.0 / (1.0 + (0.3275911 * 2.0 ** -0.5) * jnp.abs(ac))
    tail = (0.5 * e) * (t * (0.254829592 + t * (-0.284496736 + t * (1.421413741 + t * (-1.453152027 + t * 1.061405429)))))
    return jnp.where(ac < 0, tail, 1.0 - tail), e * ((2.0 * math.pi) ** -0.5)


def _conv_taps(a_ext, cw, cb):
    a2, a1 = pltpu.roll(a_ext, 2, 0), pltpu.roll(a_ext, 1, 0)
    return cw[0:1, :] * a2 + cw[1:2, :] * a1 + cw[2:3, :] * a_ext + cb, a2, a1


def _glu_specs(t_, f, g):
    gate = pl.BlockSpec((1, t_, g), lambda j, b: (b, 0, j))
    value = pl.BlockSpec((1, t_, g), lambda j, b: (b, 0, f // g + j))
    cwb = pl.BlockSpec((3, g), lambda j, b: (0, j))
    cbb = pl.BlockSpec((1, g), lambda j, b: (0, j))
    return gate, value, cwb, cbb


def _glu_fwd(u, cw, cb):
    b_, t_, f2 = u.shape
    f = f2 // 2
    g = min(FFN_GROUP, f)
    rc = min(CONV_ROWS, t_)
    gate, value, cwb, cbb = _glu_specs(t_, f, g)

    def body(a_ref, v_ref, cw_ref, cb_ref, y_ref):
        cwv, cbv = cw_ref[...], cb_ref[...]
        for r0 in range(0, t_, rc):
            ac = _conv_taps(_ext(a_ref, r0, t_), cwv, cbv)[0][HALO:HALO + rc]
            cdf, _ = _gelu_parts(ac)
            y_ref[0, r0:r0 + rc, :] = (ac * cdf * v_ref[0, r0:r0 + rc, :]).astype(y_ref.dtype)

    return pl.pallas_call(
        body, name="glu_fwd", grid=(f // g, b_), in_specs=[gate, value, cwb, cbb], out_specs=gate,
        out_shape=jax.ShapeDtypeStruct((b_, t_, f), MXU_DTYPE), compiler_params=_params(2),
    )(u, u, cw, cb)


def _glu_bwd(u, dy, cw, cb):
    b_, t_, f2 = u.shape
    f = f2 // 2
    g = min(FFN_GROUP, f)
    rc = min(CONV_ROWS, t_)
    ne = rc + 2 * HALO
    gate, value, cwb, cbb = _glu_specs(t_, f, g)

    def body(a_ref, v_ref, dy_ref, cw_ref, cb_ref, da_ref, dv_ref, dcw_ref, dcb_ref):
        cwv, cbv = cw_ref[...], cb_ref[...]
        dcw = [jnp.zeros((1, g), F32) for _ in range(3)]
        dcb = jnp.zeros((1, g), F32)
        for r0 in range(0, t_, rc):
            a_ext, v_ext, dy_ext = _ext(a_ref, r0, t_), _ext(v_ref, r0, t_), _ext(dy_ref, r0, t_)
            ac, a2, a1 = _conv_taps(a_ext, cwv, cbv)
            cdf, pdf = _gelu_parts(ac)
            dac = dy_ext * v_ext * (cdf + ac * pdf)
            da = cwv[2:3, :] * dac + cwv[1:2, :] * pltpu.roll(dac, ne - 1, 0) + cwv[0:1, :] * pltpu.roll(dac, ne - 2, 0)
            mid = slice(HALO, HALO + rc)
            da_ref[0, r0:r0 + rc, :] = da[mid].astype(da_ref.dtype)
            dv_ref[0, r0:r0 + rc, :] = (dy_ext[mid] * ac[mid] * cdf[mid]).astype(dv_ref.dtype)
            dacm = dac[mid]
            dcw[0] = dcw[0] + jnp.sum(dacm * a2[mid], axis=0, keepdims=True)
            dcw[1] = dcw[1] + jnp.sum(dacm * a1[mid], axis=0, keepdims=True)
            dcw[2] = dcw[2] + jnp.sum(dacm * a_ext[mid], axis=0, keepdims=True)
            dcb = dcb + jnp.sum(dacm, axis=0, keepdims=True)
        first = pl.program_id(1) == 0
        _acc(dcw_ref, jnp.concatenate(dcw, axis=0), first)
        _acc(dcb_ref, dcb, first)

    sds = jax.ShapeDtypeStruct((b_, t_, f), MXU_DTYPE)
    return pl.pallas_call(
        body, name="glu_bwd", grid=(f // g, b_), in_specs=[gate, value, gate, cwb, cbb],
        out_specs=[gate, gate, cwb, cbb],
        out_shape=[sds, sds, jax.ShapeDtypeStruct((3, f), F32), jax.ShapeDtypeStruct((1, f), F32)],
        compiler_params=_params(2),
    )(u, u, dy, cw, cb)


def _place():
    x, y, c = lax.axis_index("x"), lax.axis_index("y"), lax.axis_index("c")
    chips = [(1 - x, y), (x, 1 - y), (1 - x, 1 - y)]
    return x, y, c, chips


def _remote(src, dst, send_sem, recv_sem, to):
    return pltpu.make_async_remote_copy(src_ref=src, dst_ref=dst, send_sem=send_sem, recv_sem=recv_sem,
                                        device_id=to, device_id_type=MESH)


STACK, COLS = "stack", "cols"


def _shard_ref(ref, kind, s, rows, c):
    if kind == COLS:
        cols = pl.ds(pl.multiple_of(s * c, LANE), c)
        return ref.at[:, cols] if rows is None else ref.at[rows, cols]
    return ref.at[s] if rows is None else ref.at[s, rows, :]


def _halves(c, half):
    mine = pl.ds(pl.multiple_of(c * half, 16), half)
    theirs = pl.ds(pl.multiple_of((1 - c) * half, 16), half)
    return mine, theirs


def _gather_parts(kinds):
    def first_copies(ins, outs, sems):
        x, y, c, chips = _place()
        me = 2 * x + y
        cps = []
        for i, (w_ref, o_ref, kind) in enumerate(zip(ins, outs, kinds)):
            r, cw = w_ref.shape
            mine, _ = _halves(c, r // 2)
            for j, chip in enumerate(chips):
                cps.append(_remote(w_ref.at[mine], _shard_ref(o_ref, kind, me, mine, cw), sems[0].at[6 * i + j],
                                   sems[1].at[6 * i + j], (*chip, c)))
        return cps

    def start(ins, outs, sems):
        for cp in first_copies(ins, outs, sems):
            cp.start()

    def finish(ins, outs, sems):
        x, y, c, chips = _place()
        sib = (x, y, 1 - c)
        passed = []
        for i, (w_ref, o_ref, kind) in enumerate(zip(ins, outs, kinds)):
            r, cw = w_ref.shape
            mine, _ = _halves(c, r // 2)
            for j, (px, py) in enumerate(chips):
                blk = _shard_ref(o_ref, kind, 2 * px + py, mine, cw)
                _remote(blk, blk, sems[0].at[6 * i + j], sems[1].at[6 * i + j], sib).wait_recv()
                passed.append(_remote(blk, blk, sems[0].at[6 * i + 3 + j], sems[1].at[6 * i + 3 + j], sib))
                passed[-1].start()
        for i, (w_ref, o_ref, kind) in enumerate(zip(ins, outs, kinds)):
            r, cw = w_ref.shape
            _, theirs = _halves(c, r // 2)
            for j, (px, py) in enumerate(chips):
                blk = _shard_ref(o_ref, kind, 2 * px + py, theirs, cw)
                _remote(blk, blk, sems[0].at[6 * i + 3 + j], sems[1].at[6 * i + 3 + j], sib).wait_recv()
        for cp in first_copies(ins, outs, sems) + passed:
            cp.wait_send()

    return start, finish


def _gather_shapes(shards, kinds):
    return [jax.ShapeDtypeStruct((a.shape[0], N_CHIPS * a.shape[1]) if k == COLS else (N_CHIPS,) + a.shape, a.dtype)
            for a, k in zip(shards, kinds)]


def _gather_sems(nw):
    return [pltpu.SemaphoreType.DMA((6 * nw,)), pltpu.SemaphoreType.DMA((6 * nw,))]


def _gather_shards(shards, kinds):
    nw = len(shards)
    start, finish = _gather_parts(kinds)

    def body(*refs):
        ins, outs, sems = refs[:nw], refs[nw:2 * nw], refs[2 * nw:]
        start(ins, outs, sems)
        finish(ins, outs, sems)

    return pl.pallas_call(
        body, name="gather_shards", in_specs=[ANY] * nw, out_specs=[ANY] * nw,
        out_shape=_gather_shapes(shards, kinds), scratch_shapes=_gather_sems(nw),
    )(*shards)


def _gather_rider(shards, kinds):
    start, finish = _gather_parts(kinds)
    return _Rider(list(shards), _gather_shapes(shards, kinds), _gather_sems(len(shards)), start, finish)


def _half_shape(g, kind):
    if kind == COLS:
        return (g.shape[0] // 2, g.shape[1])
    return (g.shape[0], g.shape[1] // 2, g.shape[2])


def _swap_parts(kinds):
    def copies(ins, outs, sems):
        x, y, c, _ = _place()
        cps = []
        for i, (g_ref, a_ref, kind) in enumerate(zip(ins, outs, kinds)):
            r = g_ref.shape[0] if kind == COLS else g_ref.shape[1]
            _, theirs = _halves(c, r // 2)
            src = g_ref.at[theirs] if kind == COLS else g_ref.at[:, theirs]
            cps.append(_remote(src, a_ref, sems[0].at[i], sems[1].at[i], (x, y, 1 - c)))
        return cps

    def start(ins, outs, sems):
        for cp in copies(ins, outs, sems):
            cp.start()

    def finish(ins, outs, sems):
        for cp in copies(ins, outs, sems):
            cp.wait()

    return start, finish


def _swap_shapes(gs, kinds):
    return [jax.ShapeDtypeStruct(_half_shape(g, k), g.dtype) for g, k in zip(gs, kinds)]


def _pair_swap_halves(gs, kinds, name):
    nw = len(gs)
    start, finish = _swap_parts(kinds)

    def body(*refs):
        ins, outs, sems = refs[:nw], refs[nw:2 * nw], refs[2 * nw:]
        start(ins, outs, sems)
        finish(ins, outs, sems)

    return pl.pallas_call(
        body, name=name, in_specs=[ANY] * nw, out_specs=[ANY] * nw, out_shape=_swap_shapes(gs, kinds),
        scratch_shapes=[pltpu.SemaphoreType.DMA((nw,)), pltpu.SemaphoreType.DMA((nw,))],
    )(*gs)


def _swap_rider(gs, kinds):
    start, finish = _swap_parts(kinds)
    nw = len(gs)
    return _Rider(list(gs), _swap_shapes(gs, kinds), [pltpu.SemaphoreType.DMA((nw,)), pltpu.SemaphoreType.DMA((nw,))],
                  start, finish)


def _row_tile(rows, width, itemsize=4, target=2 ** 21):
    return _tile(rows, max(8, target // (width * itemsize)), 8)


def _add_half(g, a, kind, c_idx, name):
    if kind == COLS:
        half, wd = a.shape
        tr = _row_tile(half, wd)
        nblk = half // tr
        grid = (nblk,)
        g_spec = pl.BlockSpec((tr, wd), lambda i, c_ref: (c_ref[0] * nblk + i, 0))
        a_spec = pl.BlockSpec((tr, wd), lambda i, c_ref: (i, 0))
    else:
        n, half, wd = a.shape
        tr = _row_tile(half, wd)
        nblk = half // tr
        grid = (n, nblk)
        g_spec = pl.BlockSpec((1, tr, wd), lambda s, i, c_ref: (s, c_ref[0] * nblk + i, 0))
        a_spec = pl.BlockSpec((1, tr, wd), lambda s, i, c_ref: (s, i, 0))

    def body(c_ref, g_ref, a_ref, o_ref):
        o_ref[...] = (g_ref[...] + a_ref[...]).astype(o_ref.dtype)

    return pl.pallas_call(
        body, name=name,
        grid_spec=pltpu.PrefetchScalarGridSpec(num_scalar_prefetch=1, grid=grid, in_specs=[g_spec, a_spec],
                                               out_specs=a_spec),
        out_shape=jax.ShapeDtypeStruct(a.shape, EXCHANGE_DTYPE), compiler_params=_params(len(grid)),
    )(c_idx, g, a)


def _exchange_parts(kinds):
    def copies(ins, outs, sems):
        x, y, c, chips = _place()
        me = 2 * x + y
        cps = []
        for i, (p_ref, b_ref, kind) in enumerate(zip(ins, outs, kinds)):
            cw = b_ref.shape[2]
            for j, (px, py) in enumerate(chips):
                cps.append(_remote(_shard_ref(p_ref, kind, 2 * px + py, None, cw), b_ref.at[me],
                                   sems[0].at[3 * i + j], sems[1].at[3 * i + j], (px, py, c)))
        return cps

    def start(ins, outs, sems):
        for cp in copies(ins, outs, sems):
            cp.start()

    def finish(ins, outs, sems):
        x, y, c, chips = _place()
        for i, b_ref in enumerate(outs):
            for j, (px, py) in enumerate(chips):
                blk = b_ref.at[2 * px + py]
                _remote(blk, blk, sems[0].at[3 * i + j], sems[1].at[3 * i + j], (px, py, c)).wait_recv()
        for cp in copies(ins, outs, sems):
            cp.wait_send()

    return start, finish


def _exchange_shapes(ps, kinds):
    return [jax.ShapeDtypeStruct((N_CHIPS,) + ((p.shape[0], p.shape[1] // N_CHIPS) if k == COLS else tuple(p.shape[1:])),
                                 p.dtype) for p, k in zip(ps, kinds)]


def _exchange_sems(nw):
    return [pltpu.SemaphoreType.DMA((3 * nw,)), pltpu.SemaphoreType.DMA((3 * nw,))]


def _exchange_rider(ps, kinds):
    start, finish = _exchange_parts(kinds)
    return _Rider(list(ps), _exchange_shapes(ps, kinds), _exchange_sems(len(ps)), start, finish)


def _sum_chips(bq, name):
    n, h, wd = bq.shape
    tr = _row_tile(h, wd * n)

    def body(b_ref, o_ref):
        acc = b_ref[0].astype(F32)
        for s in range(1, n):
            acc = acc + b_ref[s].astype(F32)
        o_ref[...] = acc

    return pl.pallas_call(
        body, name=name, grid=(h // tr,),
        in_specs=[pl.BlockSpec((n, tr, wd), lambda i: (0, i, 0))], out_specs=pl.BlockSpec((tr, wd), lambda i: (i, 0)),
        out_shape=jax.ShapeDtypeStruct((h, wd), F32), compiler_params=_params(1),
    )(bq)


def _pair_join_halves(qs):
    nw = len(qs)

    def body(*refs):
        ins, outs = refs[:nw], refs[nw:2 * nw]
        send_sems, recv_sems = refs[2 * nw:]
        x, y, c, _ = _place()
        sent = []
        for i, (q_ref, o_ref) in enumerate(zip(ins, outs)):
            sent.append(_remote(q_ref, o_ref.at[c], send_sems.at[i], recv_sems.at[i], (x, y, 1 - c)))
            sent[-1].start()
        for i, (q_ref, o_ref) in enumerate(zip(ins, outs)):
            _remote(q_ref, o_ref.at[1 - c], send_sems.at[i], recv_sems.at[i], (x, y, 1 - c)).wait_recv()
        for cp in sent:
            cp.wait_send()

    return pl.pallas_call(
        body, name="pair_join_halves", in_specs=[ANY] * nw, out_specs=[ANY] * nw,
        out_shape=[jax.ShapeDtypeStruct((2,) + q.shape, q.dtype) for q in qs],
        scratch_shapes=[pltpu.SemaphoreType.DMA((nw,)), pltpu.SemaphoreType.DMA((nw,))],
    )(*qs)


def _all_sum_small(s, name):
    sr, w = s.shape

    def body(s_ref, o_ref, buf, send_sems, recv_sems):
        x, y, c, _ = _place()
        me = 4 * x + 2 * y + c
        buf[me] = s_ref[...]
        peers = []
        for k in range(1, 8):
            px = 1 - x if k & 4 else x
            py = 1 - y if k & 2 else y
            pc = 1 - c if k & 1 else c
            peers.append((px, py, pc))
        sent = [_remote(s_ref, buf.at[me], send_sems.at[k], recv_sems.at[k], peer) for k, peer in enumerate(peers)]
        for cp in sent:
            cp.start()
        for k, (px, py, pc) in enumerate(peers):
            _remote(s_ref, buf.at[4 * px + 2 * py + pc], send_sems.at[k], recv_sems.at[k], (px, py, pc)).wait_recv()
        for cp in sent:
            cp.wait_send()
        acc = buf[0]
        for d in range(1, 8):
            acc = acc + buf[d]
        o_ref[...] = acc

    vm = pl.BlockSpec(memory_space=pltpu.VMEM)
    return pl.pallas_call(
        body, name=name, in_specs=[vm], out_specs=vm, out_shape=jax.ShapeDtypeStruct((sr, w), F32),
        scratch_shapes=[pltpu.VMEM((8, sr, w), F32), pltpu.SemaphoreType.DMA((7,)), pltpu.SemaphoreType.DMA((7,))],
    )(s)


BIG = ("w_in", "mem_kv_w", "w_br_hgrn", "w_br_fox", "w_br_mem", "w_out", "ffn_w_up", "ffn_w_down")
KIND = {"w_in": STACK, "mem_kv_w": STACK, "w_br_hgrn": COLS, "w_br_fox": COLS, "w_br_mem": COLS, "w_out": STACK,
        "ffn_w_up": STACK, "ffn_w_down": STACK}
ROW_SHARDED = ("mem_kv_w", "w_out", "ffn_w_down")
FIRST = ("w_in",)
REST = tuple(nm for nm in BIG if nm not in FIRST)
LATE = {"in_proj": tuple(nm for nm in REST if not nm.startswith("ffn_")),
        "fox_fwd": tuple(nm for nm in REST if nm.startswith("ffn_"))}
LAST = ("w_in",)
TRANSPOSED = ("w_in",)


def _z_layout(d, hw, fw, mw):
    gate, npair, nh, nm = 3 * d // LANE, fw // LANE, hw // LANE, mw // LANE
    fox0, hg0 = gate, gate + 3 * npair
    o_fox, o_mem = 4 * nh, 4 * nh + 3 * npair
    order = [o_mem + nm + j for j in range(gate)]
    order += [o_fox + k * npair + p for p in range(npair) for k in range(3)]
    order += [k * nh + h for h in range(nh) for k in range(4)]
    order += [o_mem + h for h in range(nm)]
    assert fox0 % 3 == 0 and hg0 % 4 == 0
    return fox0, hg0, hg0 + 4 * nh, order


def _reorder_blocks(a, order):
    runs, start = [], 0
    for i in range(1, len(order) + 1):
        if i == len(order) or order[i] != order[i - 1] + 1:
            runs.append((order[start], order[i - 1] + 1))
            start = i
    return jnp.concatenate([a[:, lo * LANE:hi * LANE] for lo, hi in runs], axis=1)


def _put_shard(arr, kind, s, piece):
    if kind == COLS:
        return lax.dynamic_update_slice(arr, piece, (0, s * piece.shape[1]))
    return lax.dynamic_update_slice(arr, piece[None], (s, 0, 0))


def _take_shard(arr, kind, s):
    if kind == COLS:
        return lax.dynamic_slice(arr, (0, s * (arr.shape[1] // N_CHIPS)), (arr.shape[0], arr.shape[1] // N_CHIPS))
    return lax.dynamic_index_in_dim(arr, s, 0, keepdims=False)


def _w_in_pieces(cs, s1, nf):
    out = []
    for s in range(N_CHIPS):
        lo, hi = cs * s, cs * (s + 1)
        for a, b, forget in ((lo, min(hi, s1), False), (max(lo, s1), min(hi, s1 + nf), True), (max(lo, s1 + nf), hi, False)):
            if a < b:
                out.append((s, a - lo, b - lo, forget, a - s1 if forget else (a if a < s1 else a - nf)))
    return out


def _split_w_in(stacked, s1, nf):
    pieces = _w_in_pieces(stacked.shape[2], s1, nf)
    main = [stacked[s, :, a:b] for s, a, b, forget, _ in pieces if not forget]
    ff = [stacked[s, :, a:b] for s, a, b, forget, _ in pieces if forget]
    return jnp.concatenate(main, axis=1), jnp.concatenate(ff, axis=1)


def _join_w_in(g_main, g_ff, s1, nf):
    cs = (g_main.shape[1] + nf) // N_CHIPS
    shards = [[] for _ in range(N_CHIPS)]
    for s, a, b, forget, off in _w_in_pieces(cs, s1, nf):
        shards[s].append((g_ff if forget else g_main)[:, off:off + b - a])
    return jnp.stack([jnp.concatenate(p, axis=1) if len(p) > 1 else p[0] for p in shards])


SMALL = ("norm_mix_g", "norm_mem_g", "norm_ffn_g", "hgrn_lb_logits", "hgrn_norm_g", "fox_f_bias", "fox_q_norm_g",
         "fox_k_norm_g", "mem_q_norm_g", "mem_k_norm_g", "ffn_conv_b")


def _small_rows(shapes):
    rows = []
    for a, (r, c) in enumerate(shapes):
        for i in range(r):
            for lo in range(0, c, FLAT_W):
                rows.append((a, i, lo, min(FLAT_W, c - lo)))
    return rows


def _pack_small(vals):
    rows = _small_rows([v.shape for v in vals])
    sr = -(-len(rows) // 8) * 8

    def body(*refs):
        o_ref = refs[-1]
        o_ref[...] = jnp.zeros(o_ref.shape, F32)
        for k, (a, i, lo, wd) in enumerate(rows):
            o_ref[k:k + 1, 0:wd] = refs[a][i:i + 1, lo:lo + wd]

    vm = pl.BlockSpec(memory_space=pltpu.VMEM)
    return pl.pallas_call(body, name="pack_small", in_specs=[vm] * len(vals), out_specs=vm,
                          out_shape=jax.ShapeDtypeStruct((sr, FLAT_W), F32))(*vals)


def _row_of(buf_ref, rows, a, i):
    parts = [buf_ref[k:k + 1, 0:wd] for k, (a2, i2, _, wd) in enumerate(rows) if (a2, i2) == (a, i)]
    return jnp.concatenate(parts, axis=1) if len(parts) > 1 else parts[0]


def _unpack_small(buf, shapes):
    rows = _small_rows(shapes)

    def body(buf_ref, *outs):
        for a, (r, _) in enumerate(shapes):
            for i in range(r):
                outs[a][i:i + 1, :] = _row_of(buf_ref, rows, a, i)

    vm = pl.BlockSpec(memory_space=pltpu.VMEM)
    return pl.pallas_call(body, name="unpack_small", in_specs=[vm], out_specs=[vm] * len(shapes),
                          out_shape=[jax.ShapeDtypeStruct(shp, F32) for shp in shapes])(buf)


def _adamw_small(buf, shapes, ws, ms, vs):
    n = len(ws)
    rows = _small_rows(shapes)
    c1 = 1.0 / (1.0 - ADAM_B1 ** ADAM_STEP)
    c2 = 1.0 / (1.0 - ADAM_B2 ** ADAM_STEP)

    def body(buf_ref, *refs):
        w_refs, m_refs, v_refs = refs[:n], refs[n:2 * n], refs[2 * n:3 * n]
        outs = refs[3 * n:]
        g_out, d_out, m_out, v_out, rest = outs[:n], outs[n:2 * n], outs[2 * n:3 * n], outs[3 * n:4 * n], outs[4 * n:]
        for a, (r, _) in enumerate(shapes):
            for i in range(r):
                gv = _row_of(buf_ref, rows, a, i)
                if a >= n:
                    rest[a - n][i:i + 1, :] = gv
                    continue
                row = slice(i, i + 1)
                mn = ADAM_B1 * m_refs[a][row, :] + (1.0 - ADAM_B1) * gv
                vn = ADAM_B2 * v_refs[a][row, :] + (1.0 - ADAM_B2) * (gv * gv)
                g_out[a][row, :] = gv
                d_out[a][row, :] = -ADAM_LR * ((mn * c1) / (jnp.sqrt(vn * c2) + ADAM_EPS) + ADAM_WD * w_refs[a][row, :])
                m_out[a][row, :] = mn
                v_out[a][row, :] = vn

    vm = pl.BlockSpec(memory_space=pltpu.VMEM)
    own = [jax.ShapeDtypeStruct(shp, F32) for shp in shapes[:n]]
    outs = pl.pallas_call(
        body, name="adamw_small", in_specs=[vm] * (1 + 3 * n), out_specs=[vm] * (4 * n + len(shapes) - n),
        out_shape=own * 4 + [jax.ShapeDtypeStruct(shp, F32) for shp in shapes[n:]],
    )(buf, *ws, *ms, *vs)
    return outs[:n], outs[n:2 * n], outs[2 * n:3 * n], outs[3 * n:4 * n], outs[4 * n:]


def _pad_lanes(v, width=LANE):
    return jnp.pad(v, ((0, 0), (0, width - v.shape[1])))


WEIGHTS = ("norm_mix_g", "norm_mem_g", "w_in", "hgrn_lb_logits", "hgrn_norm_g", "fox_f_bias", "fox_q_norm_g",
           "fox_k_norm_g", "mem_kv_w", "mem_q_norm_g", "mem_k_norm_g", "w_br_hgrn", "w_br_fox", "w_br_mem", "w_out",
           "norm_ffn_g", "ffn_w_up", "ffn_conv_w", "ffn_conv_b", "ffn_w_down")


def _local_step(x, mem, target, w, full, conv_w, late=None, hooks=None):
    b_, t_, d = x.shape
    n = b_ * t_
    hw, fw, mw = HG_HEADS * HG_D, FOX_HEADS * FOX_DH, MEM_HEADS * MEM_DH
    m_ = mem.shape[1]
    f = conv_w.shape[1]
    s1 = 4 * hw + 3 * fw
    fox_col, hg_col, mem_col, order = _z_layout(d, hw, fw, mw)
    gate_col = 0
    inverse = [order.index(j) for j in range(len(order))]

    w_main, w_ff = _split_w_in(full["w_in"], s1, FOX_HEADS)
    w_main = _reorder_blocks(w_main, order)
    w_ff = _pad_lanes(w_ff)
    f_bias = _pad_lanes(w["fox_f_bias"])
    cb = w["ffn_conv_b"]

    x2 = x.reshape(n, d)
    h = _rmsnorm_fwd(x2, w["norm_mix_g"], name="norm_mix_fwd")
    if late:
        pieces, kinds, finish = late["in_proj"]
        zm, gathered = _matmul(h, w_main, name="in_proj", rider=_gather_rider(pieces, kinds))
        full = {**full, **finish(gathered)}
    else:
        zm = _matmul(h, w_main, name="in_proj")
    w_brs = [full["w_br_hgrn"], full["w_br_fox"], full["w_br_mem"]]
    w_out, w_kv = full["w_out"], full["mem_kv_w"]
    zf = _matmul(h, w_ff, name="in_proj_forget")
    zm3, zf3 = zm.reshape(b_, t_, -1), zf.reshape(b_, t_, LANE)
    ya = _hgrn_fwd(zm3, w["hgrn_lb_logits"], w["hgrn_norm_g"], hw, hg_col)
    fc = _fox_prep(zf3, f_bias)
    fox_gq, fox_gk = jnp.tile(w["fox_q_norm_g"], (1, 2)), jnp.tile(w["fox_k_norm_g"], (1, 2))
    if late:
        pieces, kinds, finish = late["fox_fwd"]
        (yb, lse), gathered = _fox_fwd(zm3, fc, fox_gq, fox_gk, fw, fox_col, _gather_rider(pieces, kinds))
        full = {**full, **finish(gathered)}
    else:
        yb, lse = _fox_fwd(zm3, fc, fox_gq, fox_gk, fw, fox_col)[0]
    w_up, w_down = full["ffn_w_up"], full["ffn_w_down"]
    mem2 = mem.reshape(b_ * m_, d)
    hm = _rmsnorm_fwd(mem2, w["norm_mem_g"], name="norm_mem_fwd")
    mkv = _matmul(hm, w_kv, name="mem_kv_proj").reshape(b_, m_, 2 * mw)
    yc = _mem_fwd(zm3, mkv, w["mem_q_norm_g"], w["mem_k_norm_g"], mw, mem_col)
    ys = [ya.reshape(n, hw), yb.reshape(n, fw), yc.reshape(n, mw)]
    x1, merged, h2 = _merge_fwd(x2, ys, zm, w_brs, w_out, gate_col, w["norm_ffn_g"])
    u = _matmul(h2, w_up, name="ffn_up")
    u3 = u.reshape(b_, t_, 2 * f)
    yff = _glu_fwd(u3, conv_w, cb).reshape(n, f)
    dy, (loss_vec,), _ = _matmul_rows([yff], w_down, name="ffn_down_loss", tb=False, row_ins=[x1, target.reshape(n, d)],
                                      vec_ins=[], epilogue=_loss_epilogue, n_vec_out=1)

    grads = {}

    def ridden(name, call):
        if not hooks or name not in hooks:
            return call(None)[0]
        rider, then = hooks[name](grads)
        outs, extra = call(rider)
        then(extra)
        return outs

    dyff = _matmul(dy, w_down, tb=True, name="ffn_down_dx")
    grads["ffn_w_down"] = _matmul(yff, dy, ta=True, name="ffn_down_dw", tm=1408)
    du_a, du_v, grads["ffn_conv_w"], grads["ffn_conv_b"] = _glu_bwd(u3, dyff.reshape(b_, t_, f), conv_w, cb)
    du_a, du_v = du_a.reshape(n, f), du_v.reshape(n, f)
    dx1, (grads["norm_ffn_g"],), _ = _matmul_rows(
        [du_a, du_v], w_up, name="ffn_up_dx", tb=True, row_ins=[x1, dy], vec_ins=[w["norm_ffn_g"]],
        epilogue=_norm_bwd_epilogue(0), n_vec_out=1)
    grads["ffn_w_up"] = _matmul(h2, None, ta=True, name="ffn_up_dw", b_parts=[du_a, du_v], tn=f // 2, stack_out=True)

    dz, dpa, dpb, dpc, dya, dyb, dyc = _merge_bwd(dx1, ys, zm, w_brs, w_out, gate_col)
    dz = dz.reshape(b_, t_, -1)
    grads["w_out"] = _matmul(merged, dx1, ta=True, name="out_proj_dw")
    for nm, y_, dp_ in zip(("w_br_hgrn", "w_br_fox", "w_br_mem"), ys, (dpa, dpb, dpc)):
        grads[nm] = _matmul(y_, dp_, ta=True, name=nm + "_dw")

    dz, dmk, dmv, grads["mem_q_norm_g"], grads["mem_k_norm_g"] = _mem_bwd(
        zm3, mkv, dyc.reshape(b_, t_, mw), w["mem_q_norm_g"], w["mem_k_norm_g"], mw, mem_col, dz)
    dmkv = jnp.concatenate([dmk, dmv], axis=-1).reshape(b_ * m_, 2 * mw)
    grads["mem_kv_w"] = _matmul(hm, dmkv, ta=True, name="mem_kv_dw")
    dhm = _matmul(dmkv, w_kv, tb=True, name="mem_kv_dx")
    _, grads["norm_mem_g"] = _rmsnorm_bwd(mem2, [dhm], w["norm_mem_g"], None, name="norm_mem_bwd")

    dz, dfc, g_fq, g_fk = ridden("fox_bwd", lambda rider: _fox_bwd(
        zm3, yb, dyb.reshape(b_, t_, fw), lse, fc, fox_gq, fox_gk, fw, fox_col, dz, rider))
    grads["fox_q_norm_g"] = g_fq[:, :FOX_DH] + g_fq[:, FOX_DH:]
    grads["fox_k_norm_g"] = g_fk[:, :FOX_DH] + g_fk[:, FOX_DH:]
    dzf, g_fb = _fox_post(dfc, zf3, f_bias)
    grads["fox_f_bias"] = g_fb[:, :FOX_HEADS]

    dz, grads["hgrn_lb_logits"], grads["hgrn_norm_g"] = ridden("hgrn_bwd", lambda rider: _hgrn_bwd(
        zm3, dya.reshape(b_, t_, hw), w["hgrn_lb_logits"], w["hgrn_norm_g"], hw, hg_col, dz, rider))
    dzm = dz.reshape(n, -1)
    dzf2 = dzf.reshape(n, LANE)
    g_main = _matmul(h, dzm, ta=True, name="in_proj_dw")
    g_ff = _matmul(h, dzf2, ta=True, name="in_proj_forget_dw")
    grads["w_in"] = _join_w_in(_reorder_blocks(g_main, inverse), g_ff[:, :FOX_HEADS], s1, FOX_HEADS)

    dh_b = _matmul(dzf2, w_ff, tb=True, name="in_proj_forget_dx")

    def in_proj_dx(rider):
        out = _matmul(dzm, w_main, tb=True, name="in_proj_dx", rider=rider)
        return ([out[0]], out[1]) if rider else ([out], None)

    dh_a, = ridden("in_proj_dx", in_proj_dx)
    grad_x, grads["norm_mix_g"] = _rmsnorm_bwd(x2, [dh_a, dh_b], w["norm_mix_g"], dx1, name="norm_mix_bwd")
    return loss_vec, grad_x.reshape(b_, t_, d), grads


def kernel(x, mem, norm_mix_g, norm_mem_g, w_in, hgrn_lb_logits, hgrn_norm_g, fox_f_bias, fox_q_norm_g, fox_k_norm_g, mem_kv_w, mem_q_norm_g, mem_k_norm_g, w_br_hgrn, w_br_fox, w_br_mem, w_out, norm_ffn_g, ffn_w_up, ffn_conv_w, ffn_conv_b, ffn_w_down, loss_target, m_norm_mix_g, m_norm_mem_g, m_w_in, m_hgrn_lb_logits, m_hgrn_norm_g, m_fox_f_bias, m_fox_q_norm_g, m_fox_k_norm_g, m_mem_kv_w, m_mem_q_norm_g, m_mem_k_norm_g, m_w_br_hgrn, m_w_br_fox, m_w_br_mem, m_w_out, m_norm_ffn_g, m_ffn_w_up, m_ffn_conv_w, m_ffn_conv_b, m_ffn_w_down, v_norm_mix_g, v_norm_mem_g, v_w_in, v_hgrn_lb_logits, v_hgrn_norm_g, v_fox_f_bias, v_fox_q_norm_g, v_fox_k_norm_g, v_mem_kv_w, v_mem_q_norm_g, v_mem_k_norm_g, v_w_br_hgrn, v_w_br_fox, v_w_br_mem, v_w_out, v_norm_ffn_g, v_ffn_w_up, v_ffn_conv_w, v_ffn_conv_b, v_ffn_w_down):
    w = dict(zip(WEIGHTS, (norm_mix_g, norm_mem_g, w_in, hgrn_lb_logits, hgrn_norm_g, fox_f_bias, fox_q_norm_g,
                           fox_k_norm_g, mem_kv_w, mem_q_norm_g, mem_k_norm_g, w_br_hgrn, w_br_fox, w_br_mem, w_out,
                           norm_ffn_g, ffn_w_up, ffn_conv_w, ffn_conv_b, ffn_w_down)))
    m = dict(zip(WEIGHTS, (m_norm_mix_g, m_norm_mem_g, m_w_in, m_hgrn_lb_logits, m_hgrn_norm_g, m_fox_f_bias,
                           m_fox_q_norm_g, m_fox_k_norm_g, m_mem_kv_w, m_mem_q_norm_g, m_mem_k_norm_g, m_w_br_hgrn,
                           m_w_br_fox, m_w_br_mem, m_w_out, m_norm_ffn_g, m_ffn_w_up, m_ffn_conv_w, m_ffn_conv_b,
                           m_ffn_w_down)))
    v = dict(zip(WEIGHTS, (v_norm_mix_g, v_norm_mem_g, v_w_in, v_hgrn_lb_logits, v_hgrn_norm_g, v_fox_f_bias,
                           v_fox_q_norm_g, v_fox_k_norm_g, v_mem_kv_w, v_mem_q_norm_g, v_mem_k_norm_g, v_w_br_hgrn,
                           v_w_br_fox, v_w_br_mem, v_w_out, v_norm_ffn_g, v_ffn_w_up, v_ffn_conv_w, v_ffn_conv_b,
                           v_ffn_w_down)))
    c_idx = lax.axis_index("c")
    chip = 2 * lax.axis_index("x") + lax.axis_index("y")

    mine = {nm: w[nm][0].astype(MXU_DTYPE) for nm in BIG}

    def gathered_full(names, arrays):
        out = {nm: _put_shard(g, KIND[nm], chip, mine[nm]) for nm, g in zip(names, arrays)}
        return {nm: g.reshape(-1, g.shape[2]) if nm in ROW_SHARDED else g for nm, g in out.items()}

    full = gathered_full(FIRST, _gather_shards([mine[nm] for nm in FIRST], [KIND[nm] for nm in FIRST]))
    late = {host: ([mine[nm] for nm in names], [KIND[nm] for nm in names],
                   functools.partial(gathered_full, names)) for host, names in LATE.items()}
    cs = ffn_conv_w.shape[2]
    f = cs * N_CHIPS
    placed = lax.dynamic_update_slice(jnp.zeros((3, f), F32), ffn_conv_w[0] * (c_idx == 0).astype(F32), (0, chip * cs))
    conv_w = _unpack_small(_all_sum_small(_pack_small([placed]), "gather_conv_w"), [(3, f)])[0]

    c_arr = jnp.reshape(c_idx, (1,)).astype(jnp.int32)

    def stacked(nm, g):
        return g.reshape(N_CHIPS, -1, g.shape[1]) if nm in ROW_SHARDED else g

    def with_own(landed, partial, kinds):
        return [_put_shard(bq, STACK, chip, _take_shard(p, k, chip)) for bq, p, k in zip(landed, partial, kinds)]

    kinds_rest, kinds_last = [KIND[nm] for nm in REST], [KIND[nm] for nm in LAST]
    state = {}

    def swap_rest(grads):
        gs = [stacked(nm, grads[nm]) for nm in REST]

        def then(from_sibling):
            state["partial_rest"] = [_add_half(g, a, k, c_arr, "add_half_" + nm)
                                     for g, a, k, nm in zip(gs, from_sibling, kinds_rest, REST)]

        return _swap_rider(gs, kinds_rest), then

    def exchange_rest(grads):
        def then(landed):
            state["landed_rest"] = with_own(landed, state["partial_rest"], kinds_rest)

        return _exchange_rider(state["partial_rest"], kinds_rest), then

    def exchange_last(grads):
        gs = [stacked(nm, grads[nm]) for nm in LAST]
        from_sibling = _pair_swap_halves(gs, kinds_last, "pair_swap_halves_last")
        partial = [_add_half(g, a, k, c_arr, "add_half_" + nm) for g, a, k, nm in zip(gs, from_sibling, kinds_last, LAST)]

        def then(landed):
            state["landed_last"] = with_own(landed, partial, kinds_last)

        return _exchange_rider(partial, kinds_last), then

    hooks = {"fox_bwd": swap_rest, "hgrn_bwd": exchange_rest, "in_proj_dx": exchange_last}

    loss_vec, grad_x, grads = _local_step(x, mem, loss_target, w, full, conv_w, late, hooks)

    landed = dict(zip(LAST + REST, state["landed_last"] + state["landed_rest"]))
    reduced_half = [_sum_chips(landed[nm], "sum_chips_" + nm) for nm in BIG]
    joined = [lax.dynamic_update_slice(o, q[None], (c_idx, 0, 0)).reshape(2 * q.shape[0], q.shape[1])
              for o, q in zip(_pair_join_halves(reduced_half), reduced_half)]
    gshards = dict(zip(BIG, joined))

    small_shapes = [w[nm].shape for nm in SMALL] + [grads["ffn_conv_w"].shape, loss_vec.shape]
    summed = _all_sum_small(_pack_small([grads[nm] for nm in SMALL] + [grads["ffn_conv_w"], loss_vec]),
                            "all_sum_small_grads")
    g_small, d_small, m_small, v_small, (g_conv_w, loss_row) = _adamw_small(
        summed, small_shapes, [w[nm] for nm in SMALL], [m[nm] for nm in SMALL], [v[nm] for nm in SMALL])
    loss = jnp.sum(loss_row)
    g_out = {nm: gshards[nm][None] for nm in BIG}
    g_out["ffn_conv_w"] = lax.dynamic_slice(g_conv_w, (0, chip * cs), (3, cs))[None]
    delta, new_m, new_v = dict(zip(SMALL, d_small)), dict(zip(SMALL, m_small)), dict(zip(SMALL, v_small))
    g_out.update(zip(SMALL, g_small))
    for nm in BIG + ("ffn_conv_w",):
        operands = (w[nm], g_out[nm], m[nm], v[nm])
        if nm in TRANSPOSED:
            operands = [jnp.swapaxes(a, 1, 2) for a in operands]
        outs = _adamw(*operands, name="adamw_" + nm)
        delta[nm], new_m[nm], new_v[nm] = [jnp.swapaxes(o, 1, 2) for o in outs] if nm in TRANSPOSED else outs

    return (loss, grad_x, *[g_out[nm] for nm in WEIGHTS], *[delta[nm] for nm in WEIGHTS],
            *[new_m[nm] for nm in WEIGHTS], *[new_v[nm] for nm in WEIGHTS])
```

```python
import functools
import math

import jax
import jax.numpy as jnp
from jax import lax
from jax.experimental import pallas as pl
from jax.experimental.pallas import tpu as pltpu

F32 = jnp.float32
BF16 = jnp.bfloat16
MXU_DTYPE = jnp.bfloat16
EXCHANGE_DTYPE = jnp.bfloat16

EPS = 1e-6
HG_HEADS, HG_D = 4, 128
FOX_HEADS, FOX_DH = 8, 64
MEM_HEADS, MEM_DH = 4, 128
HG_CHUNK = 64
FOX_BLOCK = 256
LANE = 128
FFN_GROUP = 256
FLAT_W = 1024
VMEM_LIMIT = 56 * 2 ** 20
NEG = -1e30
N_CHIPS = 4

ADAM_LR, ADAM_B1, ADAM_B2, ADAM_EPS, ADAM_WD, ADAM_STEP = 0.001, 0.9, 0.999, 1e-08, 0.01, 10

MESH = pl.DeviceIdType.MESH
ANY = pl.BlockSpec(memory_space=pl.ANY)


def _mx(x):
    return x.astype(MXU_DTYPE)


def _dot(a, b, ca, cb):
    return lax.dot_general(_mx(a), _mx(b), (((ca,), (cb,)), ((), ())), preferred_element_type=F32)


def _nn(a, b):
    return _dot(a, b, 1, 0)


def _nt(a, b):
    return _dot(a, b, 1, 1)


def _tn(a, b):
    return _dot(a, b, 0, 0)


def _tri_dot(tri_bf, x):
    hi = x.astype(BF16)
    r = x - hi.astype(F32)
    mid = r.astype(BF16)
    lo = (r - mid.astype(F32)).astype(BF16)

    def d(v):
        return lax.dot_general(tri_bf, v, (((1,), (0,)), ((), ())), preferred_element_type=F32)

    return d(hi) + d(mid) + d(lo)


def _sig(x):
    return jax.nn.sigmoid(x)


def _tile(dim, pref, unit=LANE):
    if dim <= pref:
        return dim
    t = pref - pref % unit
    while t >= unit:
        if dim % t == 0:
            return t
        t -= unit
    return dim


def _params(n_grid):
    return pltpu.CompilerParams(dimension_semantics=("arbitrary",) * n_grid, vmem_limit_bytes=VMEM_LIMIT)


def _acc(ref, val, first):
    @pl.when(first)
    def _():
        ref[...] = val

    @pl.when(jnp.logical_not(first))
    def _():
        ref[...] += val


class _Rider:
    def __init__(self, inputs, out_shapes, scratch, start, finish):
        self.inputs, self.out_shapes, self.scratch, self.start, self.finish = inputs, out_shapes, scratch, start, finish


def _ride(body, rider, n_in, n_out, grid):
    if rider is None:
        return body
    ri, ro, rs = len(rider.inputs), len(rider.out_shapes), len(rider.scratch)

    def wrapped(*refs):
        a, b, c = n_in + ri, n_in + ri + n_out, n_in + ri + n_out + ro
        base = refs[:n_in] + refs[a:b] + refs[c:len(refs) - rs]
        r_in, r_out, r_scr = refs[n_in:a], refs[b:c], refs[len(refs) - rs:]
        step = pl.program_id(0)
        for ax in range(1, len(grid)):
            step = step * grid[ax] + pl.program_id(ax)

        @pl.when(step == 0)
        def _():
            rider.start(r_in, r_out, r_scr)

        body(*base)

        @pl.when(step == math.prod(grid) - 1)
        def _():
            rider.finish(r_in, r_out, r_scr)

    return wrapped


def _ride_call(body, rider, *, name, grid, in_specs, out_specs, out_shape, scratch, args, aliases=None):
    n_in, n_out = len(in_specs), len(out_specs)
    aliases = aliases or {}
    if rider is None:
        outs = pl.pallas_call(body, name=name, grid=grid, in_specs=in_specs, out_specs=out_specs, out_shape=out_shape,
                              scratch_shapes=scratch, input_output_aliases=aliases,
                              compiler_params=_params(len(grid)))(*args)
        return list(outs), None
    outs = pl.pallas_call(
        _ride(body, rider, n_in, n_out, grid), name=name, grid=grid,
        in_specs=list(in_specs) + [ANY] * len(rider.inputs), out_specs=list(out_specs) + [ANY] * len(rider.out_shapes),
        out_shape=list(out_shape) + list(rider.out_shapes), scratch_shapes=list(scratch) + list(rider.scratch),
        input_output_aliases=aliases, compiler_params=_params(len(grid)),
    )(*args, *rider.inputs)
    return list(outs[:n_out]), list(outs[n_out:])


def _matmul(a, b, *, name, ta=False, tb=False, tm=1024, tn=2048, tk=None, rider=None, b_parts=None, stack_out=False):
    m, k = (a.shape[1], a.shape[0]) if ta else a.shape
    tk = tk or (1024 if ta else 2048)
    stacked_b = b is not None and b.ndim == 3
    if b_parts:
        n, tn = 2 * b_parts[0].shape[1], _tile(b_parts[0].shape[1], tn)
    elif stacked_b:
        n, tn = b.shape[0] * b.shape[2], b.shape[2]
    else:
        n = b.shape[0] if tb else b.shape[1]
        tn = _tile(n, tn)
    tm, tk = _tile(m, tm), _tile(k, tk)
    nk, nj = k // tk, n // tn

    def body(a_ref, *refs):
        o_ref = refs[-1]
        if b_parts:
            bv = jnp.where(pl.program_id(1) < nj // 2, refs[0][...], refs[1][...])
        else:
            bv = refs[0][...]
        p = _dot(a_ref[...], bv, 0 if ta else 1, 1 if tb else 0)
        if nk == 1:
            o_ref[...] = p
        else:
            _acc(o_ref, p, pl.program_id(2) == 0)

    a_spec = pl.BlockSpec((tk, tm), lambda i, j, kk: (kk, i)) if ta else pl.BlockSpec((tm, tk), lambda i, j, kk: (i, kk))
    if b_parts:
        half = nj // 2
        b_specs = [pl.BlockSpec((tk, tn), lambda i, j, kk: (kk, jnp.minimum(j, half - 1))),
                   pl.BlockSpec((tk, tn), lambda i, j, kk: (kk, jnp.maximum(j - half, 0)))]
        b_args = list(b_parts)
    elif stacked_b:
        b_specs, b_args = [pl.BlockSpec((None, tk, tn), lambda i, j, kk: (j, kk, 0))], [b]
    else:
        b_specs = [pl.BlockSpec((tn, tk), lambda i, j, kk: (j, kk)) if tb else pl.BlockSpec((tk, tn), lambda i, j, kk: (kk, j))]
        b_args = [b]
    if stack_out:
        o_spec, o_sds = pl.BlockSpec((None, tm, tn), lambda i, j, kk: (j, i, 0)), jax.ShapeDtypeStruct((nj, m, tn), F32)
    else:
        o_spec, o_sds = pl.BlockSpec((tm, tn), lambda i, j, kk: (i, j)), jax.ShapeDtypeStruct((m, n), F32)
    outs, extra = _ride_call(body, rider, name=name, grid=(m // tm, nj, nk), in_specs=[a_spec] + b_specs,
                             out_specs=[o_spec], out_shape=[o_sds], scratch=[], args=(a, *b_args))
    return (outs[0], extra) if rider else outs[0]


def _matmul_rows(a_parts, b, *, name, tb, row_ins, vec_ins, epilogue, n_vec_out, tm=512, tk=2048, rider=None):
    m, kp = a_parts[0].shape
    stacked_b = b.ndim == 3
    n = b.shape[1] if stacked_b else (b.shape[0] if tb else b.shape[1])
    tm, tk = _tile(m, tm, 8), (b.shape[2] if stacked_b else _tile(kp, tk))
    nk = kp // tk
    n_a, n_row, n_vec = len(a_parts), len(row_ins), len(vec_ins)

    def body(*refs):
        a_refs, b_refs = refs[:n_a], refs[n_a:2 * n_a]
        rows = refs[2 * n_a:2 * n_a + n_row]
        vecs = refs[2 * n_a + n_row:2 * n_a + n_row + n_vec]
        o_ref = refs[2 * n_a + n_row + n_vec]
        v_refs = refs[2 * n_a + n_row + n_vec + 1:-1]
        acc_ref = refs[-1]
        i, kk = pl.program_id(0), pl.program_id(1)
        p = _dot(a_refs[0][...], b_refs[0][...], 1, 1 if tb else 0)
        for a_ref, b_ref in zip(a_refs[1:], b_refs[1:]):
            p = p + _dot(a_ref[...], b_ref[...], 1, 1 if tb else 0)
        _acc(acc_ref, p, kk == 0)

        @pl.when(kk == nk - 1)
        def _():
            out, vouts = epilogue(acc_ref[...], *[r[...] for r in rows], *[v[...] for v in vecs])
            o_ref[...] = out
            for v_ref, v in zip(v_refs, vouts):
                _acc(v_ref, v, i == 0)

    a_spec = pl.BlockSpec((tm, tk), lambda i, kk: (i, kk))
    if stacked_b:
        b_specs = [pl.BlockSpec((None, n, tk), functools.partial(lambda i, kk, q: (q * nk + kk, 0, 0), q=q))
                   for q in range(n_a)]
    else:
        b_specs = [pl.BlockSpec((n, tk), functools.partial(lambda i, kk, q: (0, q * nk + kk), q=q)) if tb else
                   pl.BlockSpec((tk, n), functools.partial(lambda i, kk, q: (q * nk + kk, 0), q=q)) for q in range(n_a)]
    row = pl.BlockSpec((tm, n), lambda i, kk: (i, 0))
    vec = pl.BlockSpec((1, n), lambda i, kk: (0, 0))
    outs, extra = _ride_call(
        body, rider, name=name, grid=(m // tm, nk),
        in_specs=[a_spec] * n_a + b_specs + [row] * n_row + [vec] * n_vec,
        out_specs=[row] + [vec] * n_vec_out,
        out_shape=[jax.ShapeDtypeStruct((m, n), F32)] + [jax.ShapeDtypeStruct((1, n), F32)] * n_vec_out,
        scratch=[pltpu.VMEM((tm, n), F32)], args=(*a_parts, *([b] * n_a), *row_ins, *vec_ins))
    return outs[0], outs[1:], extra


def _norm_bwd_epilogue(n_dh):
    def epilogue(dh, x, res, *rest):
        for extra in rest[:n_dh]:
            dh = dh + extra
        g = rest[n_dh]
        r = lax.rsqrt(jnp.mean(x * x, axis=-1, keepdims=True) + EPS)
        dhg = dh * g
        dx = res + r * dhg - x * (r * r * r) * jnp.mean(dhg * x, axis=-1, keepdims=True)
        return dx, [jnp.sum(dh * x * r, axis=0, keepdims=True)]

    return epilogue


def _loss_epilogue(y, x1, target):
    d = y.shape[1]
    err = x1 + y - target
    return err * (1.0 / d), [jnp.sum(err * err, axis=0, keepdims=True) * (0.5 / d)]


def _rmsnorm_fwd(x, g, *, name, tm=512):
    n, d = x.shape
    tm = _tile(n, tm, 8)

    def body(x_ref, g_ref, o_ref):
        xv = x_ref[...]
        r = lax.rsqrt(jnp.mean(xv * xv, axis=-1, keepdims=True) + EPS)
        o_ref[...] = (xv * r * g_ref[...]).astype(o_ref.dtype)

    return pl.pallas_call(
        body, name=name, grid=(n // tm,),
        in_specs=[pl.BlockSpec((tm, d), lambda i: (i, 0)), pl.BlockSpec((1, d), lambda i: (0, 0))],
        out_specs=pl.BlockSpec((tm, d), lambda i: (i, 0)),
        out_shape=jax.ShapeDtypeStruct((n, d), MXU_DTYPE),
        compiler_params=_params(1),
    )(x, g)


def _rmsnorm_bwd(x, dhs, g, res, *, name, tm=512):
    n, d = x.shape
    tm = _tile(n, tm, 8)
    n_dh = len(dhs)
    has_res = res is not None

    def body(*refs):
        x_ref, dh_refs, g_ref = refs[0], refs[1:1 + n_dh], refs[1 + n_dh]
        res_ref = refs[2 + n_dh] if has_res else None
        dx_ref, dg_ref = refs[-2], refs[-1]
        xv = x_ref[...]
        dh = dh_refs[0][...].astype(F32)
        for r_ in dh_refs[1:]:
            dh = dh + r_[...].astype(F32)
        r = lax.rsqrt(jnp.mean(xv * xv, axis=-1, keepdims=True) + EPS)
        dhg = dh * g_ref[...]
        dx = r * dhg - xv * (r * r * r) * jnp.mean(dhg * xv, axis=-1, keepdims=True)
        if has_res:
            dx = dx + res_ref[...]
        dx_ref[...] = dx
        _acc(dg_ref, jnp.sum(dh * xv * r, axis=0, keepdims=True), pl.program_id(0) == 0)

    row = pl.BlockSpec((tm, d), lambda i: (i, 0))
    vec = pl.BlockSpec((1, d), lambda i: (0, 0))
    ins = [x] + list(dhs) + [g] + ([res] if has_res else [])
    return pl.pallas_call(
        body, name=name, grid=(n // tm,),
        in_specs=[row] * (1 + n_dh) + [vec] + ([row] if has_res else []),
        out_specs=[row, vec],
        out_shape=[jax.ShapeDtypeStruct((n, d), F32), jax.ShapeDtypeStruct((1, d), F32)],
        compiler_params=_params(1),
    )(*ins)


def _adamw(w, g, m, v, *, name, tr=256):
    _, r, c = w.shape
    c1 = 1.0 / (1.0 - ADAM_B1 ** ADAM_STEP)
    c2 = 1.0 / (1.0 - ADAM_B2 ** ADAM_STEP)

    def body(w_ref, g_ref, m_ref, v_ref, d_ref, mo_ref, vo_ref):
        gv = g_ref[...]
        mn = ADAM_B1 * m_ref[...] + (1.0 - ADAM_B1) * gv
        vn = ADAM_B2 * v_ref[...] + (1.0 - ADAM_B2) * (gv * gv)
        d_ref[...] = -ADAM_LR * ((mn * c1) / (jnp.sqrt(vn * c2) + ADAM_EPS) + ADAM_WD * w_ref[...])
        mo_ref[...] = mn
        vo_ref[...] = vn

    if r % 8 == 0 or r < 8:
        tr = _tile(r, tr, 8)
        grid, blk = (r // tr,), pl.BlockSpec((1, tr, c), lambda i: (0, i, 0))
    else:
        tc = _tile(c, tr)
        grid, blk = (c // tc,), pl.BlockSpec((1, r, tc), lambda i: (0, 0, i))
    sds = jax.ShapeDtypeStruct((1, r, c), F32)
    return pl.pallas_call(
        body, name=name, grid=grid, in_specs=[blk] * 4, out_specs=[blk] * 3, out_shape=[sds] * 3,
        compiler_params=_params(1),
    )(w, g, m, v)


def _bdot(a, b, ca, cb):
    return lax.dot_general(_mx(a), _mx(b), (((ca,), (cb,)), ((0,), (0,))), preferred_element_type=F32)


def _split2(x):
    hi = x.astype(BF16)
    return hi, (x - hi.astype(F32)).astype(BF16)


def _bdotp(a, b, ca, cb):
    def d(u, v):
        return lax.dot_general(u, v, (((ca,), (cb,)), ((0,), (0,))), preferred_element_type=F32)

    return d(a[0], b[0]) + d(a[0], b[1]) + d(a[1], b[0])


def _tri_dot_b(tri_bf, x):
    hi = x.astype(BF16)
    r = x - hi.astype(F32)
    mid = r.astype(BF16)
    lo = (r - mid.astype(F32)).astype(BF16)

    def d(v):
        return lax.dot_general(tri_bf, v, (((2,), (1,)), ((0,), (0,))), preferred_element_type=F32)

    return d(hi) + d(mid) + d(lo)


def _hgrn_forward(hq, hf, hi, lbv, tril, tril_bf):
    nc, c, _ = hq.shape
    sf = _sig(hf)
    f = lbv + (1.0 - lbv) * sf
    k = 1.0 - f
    gcum = _tri_dot_b(tril_bf, jnp.log(f))
    mid = gcum[:, c // 2 - 1:c // 2, :]
    glast = gcum[:, c - 1:c, :]
    sq = _sig(hq)
    q = hq * sq
    e_q = jnp.exp(gcum - mid)
    e_k = jnp.exp(mid - gcum)
    qe, ke = q * e_q, k * e_k
    a = jnp.where(tril, _bdot(qe, ke, 2, 2), 0.0)
    e_g = jnp.exp(gcum)
    qg = q * e_g
    e_s = jnp.exp(glast - gcum)
    kg = k * e_s
    e_l = jnp.exp(glast)
    upd = _bdot(hi, kg, 1, 1)
    st = jnp.zeros((HG_D, HG_D), F32)
    states = []
    for n in range(nc):
        states.append(st)
        st = st * e_l[n] + upd[n]
    st_all = jnp.stack(states)
    o = _bdot(a, hi, 2, 1) + _bdot(qg, st_all, 2, 2)
    return dict(sf=sf, f=f, k=k, sq=sq, q=q, e_q=e_q, e_k=e_k, qe=qe, ke=ke, a=a, e_g=e_g, qg=qg, o=o,
                e_s=e_s, kg=kg, e_l=e_l, st_all=st_all)


def _hgrn_specs(t_, col0):
    def col(off):
        return pl.BlockSpec((1, t_, LANE), lambda h, b: (b, 0, col0 + 4 * h + off))

    vec = pl.BlockSpec((2, LANE), lambda h, b: (0, h))
    one = pl.BlockSpec((1, LANE), lambda h, b: (0, 0))
    blk = pl.BlockSpec((1, t_, LANE), lambda h, b: (b, 0, h))
    return col, vec, one, blk


def _chunk_masks(nc, c):
    row = lax.broadcasted_iota(jnp.int32, (nc, c, c), 1)
    cl = lax.broadcasted_iota(jnp.int32, (nc, c, c), 2)
    return row >= cl, (row >= cl).astype(BF16), (row <= cl).astype(BF16)


def _hgrn_fwd(zm, lb, gn, hw, col0):
    b_, t_, _ = zm.shape
    c = min(HG_CHUNK, t_)
    nc = t_ // c
    col, vec, one, blk = _hgrn_specs(t_, col0)

    def body(q_ref, f_ref, i_ref, g_ref, lb_ref, gn_ref, y_ref):
        lbv, gnv = _sig(lb_ref[0:1, :] - lb_ref[1:2, :]), gn_ref[...]
        tril, tril_bf, _ = _chunk_masks(nc, c)
        chunks = lambda ref: ref[0].reshape(nc, c, LANE)
        o = _hgrn_forward(chunks(q_ref), chunks(f_ref), chunks(i_ref), lbv, tril, tril_bf)["o"]
        r = lax.rsqrt(jnp.mean(o * o, axis=-1, keepdims=True) + EPS)
        hg = chunks(g_ref)
        y_ref[0] = (o * r * gnv * (hg * _sig(hg))).reshape(t_, LANE)

    return pl.pallas_call(
        body, name="hgrn_fwd", grid=(HG_HEADS, b_),
        in_specs=[col(0), col(1), col(2), col(3), vec, one], out_specs=blk,
        out_shape=jax.ShapeDtypeStruct((b_, t_, hw), F32),
        compiler_params=_params(2),
    )(zm, zm, zm, zm, lb, gn)


def _hgrn_bwd(zm, dy, lb, gn, hw, col0, dz, rider=None):
    b_, t_, _ = zm.shape
    c = min(HG_CHUNK, t_)
    nc = t_ // c
    col, vec, one, blk = _hgrn_specs(t_, col0)

    def body(q_ref, f_ref, i_ref, g_ref, dy_ref, lb_ref, gn_ref, _, dz_ref, dlb_ref, dgn_ref):
        h, b = pl.program_id(0), pl.program_id(1)
        lbv, gnv = _sig(lb_ref[0:1, :] - lb_ref[1:2, :]), gn_ref[...]
        tril, tril_bf, triu_bf = _chunk_masks(nc, c)
        last_row = lax.broadcasted_iota(jnp.int32, (nc, c, LANE), 1) == c - 1
        chunks = lambda ref: ref[0].reshape(nc, c, LANE)
        flat = lambda x: x.reshape(t_, LANE)
        hq, hi, hg = chunks(q_ref), chunks(i_ref), chunks(g_ref)
        p = _hgrn_forward(hq, chunks(f_ref), hi, lbv, tril, tril_bf)
        o, q, k, st_all, e_l = p["o"], p["q"], p["k"], p["st_all"], p["e_l"]
        dyv = chunks(dy_ref)
        sg = _sig(hg)
        r = lax.rsqrt(jnp.mean(o * o, axis=-1, keepdims=True) + EPS)
        dn = dyv * (hg * sg)
        dz_ref[0, :, 3 * LANE:] = flat(dyv * (o * r * gnv) * (sg * (1.0 + hg * (1.0 - sg)))).astype(dz_ref.dtype)
        dgn = jnp.sum(flat(dn * o * r), axis=0, keepdims=True)
        dng = dn * gnv
        do = r * dng - o * (r * r * r) * jnp.mean(dng * o, axis=-1, keepdims=True)
        do2, hi2, qg2, ke2, qe2, st2 = (_split2(t) for t in (do, hi, p["qg"], p["ke"], p["qe"], st_all))
        back = _bdotp(do2, qg2, 1, 1)
        dst = jnp.zeros((HG_D, HG_D), F32)
        dsts = [None] * nc
        for n in range(nc - 1, -1, -1):
            dsts[n] = dst
            dst = dst * e_l[n] + back[n]
        dst_all = jnp.stack(dsts)
        da = jnp.where(tril, _bdotp(do2, hi2, 2, 2), 0.0)
        da2 = _split2(da)
        dq = _bdotp(da2, ke2, 2, 1) * p["e_q"] + _bdotp(do2, st2, 2, 1) * p["e_g"]
        dk_state = _bdotp(hi2, _split2(dst_all), 2, 1) * p["e_s"]
        dk = _bdotp(da2, qe2, 1, 1) * p["e_k"] + dk_state
        dz_ref[0, :, 2 * LANE:3 * LANE] = flat(_bdot(p["a"], do, 1, 1) + _bdot(p["kg"], dst_all, 2, 2)).astype(dz_ref.dtype)
        extra = (jnp.sum(k * dk_state, axis=1, keepdims=True) + e_l * jnp.sum(st_all * dst_all, axis=1, keepdims=True))
        dgc = q * dq - k * dk + jnp.where(last_row, extra, 0.0)
        dfv = _tri_dot_b(triu_bf, dgc) / p["f"] - dk
        sf, sq = p["sf"], p["sq"]
        dz_ref[0, :, LANE:2 * LANE] = flat(dfv * (1.0 - lbv) * sf * (1.0 - sf)).astype(dz_ref.dtype)
        dlb = jnp.sum(flat(dfv * (1.0 - sf)), axis=0, keepdims=True)
        dz_ref[0, :, :LANE] = flat(dq * (sq * (1.0 + hq * (1.0 - sq)))).astype(dz_ref.dtype)
        dl0 = dlb * lbv * (1.0 - lbv)
        _acc(dlb_ref, jnp.concatenate([dl0, -dl0], axis=0), b == 0)
        _acc(dgn_ref, dgn, jnp.logical_and(b == 0, h == 0))

    return _ride_call(
        body, rider, name="hgrn_bwd", grid=(HG_HEADS, b_),
        in_specs=[col(0), col(1), col(2), col(3), blk, vec, one, ANY],
        out_specs=[pl.BlockSpec((1, t_, 4 * LANE), lambda h, b: (b, 0, col0 // 4 + h)), vec, one],
        out_shape=[jax.ShapeDtypeStruct(dz.shape, dz.dtype), jax.ShapeDtypeStruct((2, hw), F32),
                   jax.ShapeDtypeStruct((1, LANE), F32)],
        scratch=[], args=(zm, zm, zm, zm, dy, lb, gn, dz), aliases={7: 0})


def _fox_logf(x):
    return jnp.minimum(x, 0.0) - jnp.log(1.0 + jnp.exp(-jnp.abs(x)))


def _fox_prep(zf, bias):
    b_, t_, _ = zf.shape
    tb = min(FOX_BLOCK, t_)
    nb = t_ // tb

    def body(z_ref, b_ref, fc_ref):
        tril_bf = (lax.broadcasted_iota(jnp.int32, (tb, tb), 0) >= lax.broadcasted_iota(jnp.int32, (tb, tb), 1)).astype(BF16)
        bv = b_ref[...]

        def blk(i, carry):
            rows = pl.ds(pl.multiple_of(i * tb, tb), tb)
            fc = _tri_dot(tril_bf, _fox_logf(z_ref[0, rows, :] + bv)) + carry
            fc_ref[0, rows, :] = fc
            return fc[tb - 1:tb, :]

        lax.fori_loop(0, nb, blk, jnp.zeros((1, LANE), F32))

    blk_spec = pl.BlockSpec((1, t_, LANE), lambda b: (b, 0, 0))
    return pl.pallas_call(
        body, name="fox_prep", grid=(b_,),
        in_specs=[blk_spec, pl.BlockSpec((1, LANE), lambda b: (0, 0))], out_specs=blk_spec,
        out_shape=jax.ShapeDtypeStruct((b_, t_, LANE), F32), compiler_params=_params(1),
    )(zf, bias)


def _fox_post(dfc, zf, bias):
    b_, t_, _ = zf.shape
    npair = dfc.shape[1]
    tb = min(FOX_BLOCK, t_)
    nb = t_ // tb

    def body(d_ref, z_ref, b_ref, dz_ref, db_ref):
        triu_bf = (lax.broadcasted_iota(jnp.int32, (tb, tb), 0) <= lax.broadcasted_iota(jnp.int32, (tb, tb), 1)).astype(BF16)
        valid = lax.broadcasted_iota(jnp.int32, (tb, LANE), 1) < FOX_HEADS
        bv = b_ref[...]

        def blk(m, carry):
            tail, db = carry
            rows = pl.ds(pl.multiple_of((nb - 1 - m) * tb, tb), tb)
            dfc_rows = d_ref[0, 0, rows, :]
            for p in range(1, npair):
                dfc_rows = dfc_rows + pltpu.roll(d_ref[0, p, rows, :], 2 * p, 1)
            dlf = _tri_dot(triu_bf, dfc_rows) + tail
            dx = jnp.where(valid, dlf * _sig(-(z_ref[0, rows, :] + bv)), 0.0)
            dz_ref[0, rows, :] = dx.astype(dz_ref.dtype)
            return dlf[0:1, :], db + jnp.sum(dx, axis=0, keepdims=True)

        z1 = jnp.zeros((1, LANE), F32)
        _, db = lax.fori_loop(0, nb, blk, (z1, z1))
        _acc(db_ref, db, pl.program_id(0) == 0)

    blk_spec = pl.BlockSpec((1, t_, LANE), lambda b: (b, 0, 0))
    vec = pl.BlockSpec((1, LANE), lambda b: (0, 0))
    return pl.pallas_call(
        body, name="fox_post", grid=(b_,),
        in_specs=[pl.BlockSpec((1, npair, t_, LANE), lambda b: (b, 0, 0, 0)), blk_spec, vec], out_specs=[blk_spec, vec],
        out_shape=[jax.ShapeDtypeStruct((b_, t_, LANE), MXU_DTYPE), jax.ShapeDtypeStruct((1, LANE), F32)],
        compiler_params=_params(1),
    )(dfc, zf, bias)


FOX_TILE = 256
FOX_BAND = 512
AUG = 64


def _head_mean_matrix():
    r = lax.broadcasted_iota(jnp.int32, (LANE, LANE), 0) // FOX_DH
    c = lax.broadcasted_iota(jnp.int32, (LANE, LANE), 1) // FOX_DH
    return (r == c).astype(BF16)


def _dot_right_exact(x, m_bf):
    hi = x.astype(BF16)
    lo = (x - hi.astype(F32)).astype(BF16)

    def d(v):
        return lax.dot_general(v, m_bf, (((1,), (0,)), ((), ())), preferred_element_type=F32)

    return d(hi) + d(lo)


def _pair_norm(x, g2, bd):
    r = lax.rsqrt(_dot_right_exact(x * x, bd) * (1.0 / FOX_DH) + EPS)
    return x * r * g2, r


def _pair_norm_bwd(x, r, dy, g2, bd):
    dyg = dy * g2
    dx = r * dyg - x * (r * r * r) * (_dot_right_exact(dyg * x, bd) * (1.0 / FOX_DH))
    return dx, jnp.sum(dy * x * r, axis=0, keepdims=True)


def _head_lanes(xn, hh):
    return xn if hh == 0 else pltpu.roll(xn, FOX_DH, 1)


def _split3(x):
    hi = x.astype(BF16).astype(F32)
    mid = (x - hi).astype(BF16).astype(F32)
    return hi, mid, x - hi - mid


def _fox_operands(q_ref, k_ref, v_ref, fc_ref, gq2, gk2, p, qa, ka, va):
    t_ = q_ref.shape[1]
    bd = _head_mean_matrix()
    lane = lax.broadcasted_iota(jnp.int32, (t_, LANE), 1)
    qx, kx = q_ref[0], k_ref[0]
    qn, rq = _pair_norm(qx, gq2, bd)
    kn, rk = _pair_norm(kx, gk2, bd)
    vv = v_ref[0]
    q_aug = jnp.where(jnp.logical_and(lane >= AUG, lane < AUG + 3), 1.0, 0.0)
    for hh in range(2):
        fcol = jnp.sum(jnp.where(lane == 2 * p + hh, fc_ref[0], 0.0), axis=-1, keepdims=True)
        hi, mid, lo = _split3(-fcol)
        k_aug = jnp.where(lane == AUG, hi, jnp.where(lane == AUG + 1, mid, jnp.where(lane == AUG + 2, lo,
                          jnp.where(lane == AUG + 3, 1.0, 0.0))))
        head = lane < FOX_DH
        qa[hh] = jnp.where(head, _head_lanes(qn, hh), q_aug).astype(MXU_DTYPE)
        ka[hh] = jnp.where(head, _head_lanes(kn, hh), k_aug).astype(MXU_DTYPE)
        va[hh] = jnp.where(head, _head_lanes(vv, hh), 0.0).astype(MXU_DTYPE)
    return bd, lane, qx, kx, rq, rk


def _fox_specs(t_, fw, col0):
    npair = fw // LANE

    def col(off):
        return pl.BlockSpec((1, t_, LANE), lambda b, p: (b, 0, col0 + 3 * p + off))

    pair = pl.BlockSpec((1, t_, LANE), lambda b, p: (b, 0, p))
    full = pl.BlockSpec((1, t_, LANE), lambda b, p: (b, 0, 0))
    gvec = pl.BlockSpec((1, LANE), lambda b, p: (0, 0))
    lse = pl.BlockSpec((1, 1, t_, LANE), lambda b, p: (b, p, 0, 0))
    return col, pair, full, gvec, lse


def _fox_fwd(zm, fc, gq2, gk2, fw, col0, rider=None):
    b_, t_, _ = zm.shape
    npair = fw // LANE
    tq = min(FOX_TILE, t_)
    bw = min(FOX_BAND, t_)
    nband, tpb = t_ // bw, bw // tq
    scale = FOX_DH ** -0.5
    col, pair, full, gvec, lse_spec = _fox_specs(t_, fw, col0)

    def body(q_ref, k_ref, v_ref, fc_ref, gq_ref, gk_ref, o_ref, lse_ref, qa, ka, va):
        p = pl.program_id(1)
        _fox_operands(q_ref, k_ref, v_ref, fc_ref, gq_ref[...] * scale, gk_ref[...], p, qa, ka, va)
        ahead = lax.broadcasted_iota(jnp.int32, (tq, bw), 1) - lax.broadcasted_iota(jnp.int32, (tq, bw), 0)
        lane = lax.broadcasted_iota(jnp.int32, (tq, LANE), 1)

        for band in range(nband):
            c0 = band * bw

            def qtile(ii, _, c0=c0):
                r0 = pl.multiple_of(c0 + ii * tq, tq)
                rows = pl.ds(r0, tq)
                keep = ahead <= r0 - c0
                res = []
                for hh in range(2):
                    qb = qa[hh, rows, :]
                    s_b = jnp.where(keep, _nt(qb, ka[hh, c0:c0 + bw, :]), NEG)
                    m = jnp.max(s_b, axis=-1, keepdims=True)
                    if c0:
                        s_a = _nt(qb, ka[hh, 0:c0, :])
                        m = jnp.maximum(m, jnp.max(s_a, axis=-1, keepdims=True))
                    p_b = jnp.exp(s_b - m)
                    l = jnp.sum(p_b, axis=-1, keepdims=True)
                    acc = _nn(p_b, va[hh, c0:c0 + bw, :])
                    if c0:
                        p_a = jnp.exp(s_a - m)
                        l = l + jnp.sum(p_a, axis=-1, keepdims=True)
                        acc = acc + _nn(p_a, va[hh, 0:c0, :])
                    res.append((acc / l, m + jnp.log(l)))
                (o0, e0), (o1, e1) = res
                o_ref[0, rows, :] = jnp.where(lane < FOX_DH, o0, pltpu.roll(o1, FOX_DH, 1))
                lse_ref[0, 0, rows, :] = jnp.where(lane == 0, e0, jnp.where(lane == 1, e1, 0.0))
                return 0

            lax.fori_loop(0, tpb, qtile, 0)

    return _ride_call(
        body, rider, name="fox_fwd", grid=(b_, npair),
        in_specs=[col(0), col(1), col(2), full, gvec, gvec],
        out_specs=[pair, lse_spec],
        out_shape=[jax.ShapeDtypeStruct((b_, t_, fw), F32), jax.ShapeDtypeStruct((b_, npair, t_, LANE), F32)],
        scratch=[pltpu.VMEM((2, t_, LANE), MXU_DTYPE)] * 3, args=(zm, zm, zm, fc, gq2, gk2))


def _norm_bwd(x, dy, g):
    r = lax.rsqrt(jnp.mean(x * x, axis=-1, keepdims=True) + EPS)
    dyg = dy * g
    dx = r * dyg - x * (r * r * r) * jnp.mean(dyg * x, axis=-1, keepdims=True)
    return dx, jnp.sum(dy * x * r, axis=0, keepdims=True)


def _fox_bwd(zm, o, do, lse, fc, gq2, gk2, fw, col0, dz, rider=None):
    b_, t_, _ = zm.shape
    npair = fw // LANE
    tq = min(FOX_TILE, t_)
    nb = t_ // tq
    bw = min(FOX_BAND, t_)
    nband, tpb = t_ // bw, bw // tq
    scale = FOX_DH ** -0.5
    col, pair, full, gvec, lse_spec = _fox_specs(t_, fw, col0)

    def body(q_ref, k_ref, v_ref, o_ref, do_ref, lse_ref, fc_ref, gq_ref, gk_ref, _,
             dz_ref, dfc_ref, dgq_ref, dgk_ref, qa, ka, va, da, rowv, dq_acc, dk_acc, dv_acc):
        b, p = pl.program_id(0), pl.program_id(1)
        gq2v, gk2v = gq_ref[...] * scale, gk_ref[...]
        bd, lane, qx, kx, rq, rk = _fox_operands(q_ref, k_ref, v_ref, fc_ref, gq2v, gk2v, p, qa, ka, va)
        head = lane < FOX_DH
        dov = do_ref[0]
        dsum = _dot_right_exact(dov * o_ref[0], bd)
        eye = (lax.broadcasted_iota(jnp.int32, (tq, tq), 0) == lax.broadcasted_iota(jnp.int32, (tq, tq), 1)).astype(F32)
        for hh in range(2):
            da[hh] = jnp.where(head, _head_lanes(dov, hh), 0.0).astype(MXU_DTYPE)
            for blk in range(nb):
                rs = slice(blk * tq, (blk + 1) * tq)
                rowv[2 * hh:2 * hh + 1, rs] = jnp.sum(eye * lse_ref[0, 0, rs, hh:hh + 1], axis=0, keepdims=True)
                rowv[2 * hh + 1:2 * hh + 2, rs] = jnp.sum(eye * dsum[rs, hh * FOX_DH:hh * FOX_DH + 1], axis=0, keepdims=True)
        dq_acc[...] = jnp.zeros(dq_acc.shape, F32)
        ahead = lax.broadcasted_iota(jnp.int32, (tq, bw), 1) - lax.broadcasted_iota(jnp.int32, (tq, bw), 0)

        def part(hh, kb, vb, lo, hi, keep):
            qm, dm = qa[hh, lo:hi, :], da[hh, lo:hi, :]
            pt = jnp.exp(_nt(kb, qm) - rowv[2 * hh:2 * hh + 1, lo:hi])
            if keep is not None:
                pt = jnp.where(keep, pt, 0.0)
            dst = pt * (_nt(vb, dm) - rowv[2 * hh + 1:2 * hh + 2, lo:hi])
            dq_acc[hh, lo:hi, :] += _tn(dst, kb)
            return _nn(dst, qm), _nn(pt, dm)

        for band in range(nband):
            c0 = band * bw

            def kvtile(jj, _, c0=c0):
                r0 = pl.multiple_of(c0 + jj * tq, tq)
                rows = pl.ds(r0, tq)
                keep = ahead >= r0 - c0
                for hh in range(2):
                    kb, vb = ka[hh, rows, :], va[hh, rows, :]
                    dk_t, dv_t = part(hh, kb, vb, c0, c0 + bw, keep)
                    if c0 + bw < t_:
                        dk_u, dv_u = part(hh, kb, vb, c0 + bw, t_, None)
                        dk_t, dv_t = dk_t + dk_u, dv_t + dv_u
                    dk_acc[hh, rows, :] = dk_t
                    dv_acc[hh, rows, :] = dv_t
                return 0

            lax.fori_loop(0, tpb, kvtile, 0)

        dq0, dq1, dk0, dk1 = dq_acc[0], dq_acc[1], dk_acc[0], dk_acc[1]
        dqn = jnp.where(head, dq0, pltpu.roll(dq1, FOX_DH, 1))
        dkn = jnp.where(head, dk0, pltpu.roll(dk1, FOX_DH, 1))
        dqx, gq_part = _pair_norm_bwd(qx, rq, dqn, gq2v, bd)
        dkx, gk_part = _pair_norm_bwd(kx, rk, dkn, gk2v, bd)
        dz_ref[0, :, :LANE] = dqx.astype(dz_ref.dtype)
        dz_ref[0, :, LANE:2 * LANE] = dkx.astype(dz_ref.dtype)
        dz_ref[0, :, 2 * LANE:] = jnp.where(head, dv_acc[0], pltpu.roll(dv_acc[1], FOX_DH, 1)).astype(dz_ref.dtype)

        def bias_grad(dqh, dkh):
            return dqh[:, AUG + 3:AUG + 4] - dkh[:, AUG:AUG + 1]

        dfc_ref[0, 0] = jnp.where(lane == 0, bias_grad(dq0, dk0), jnp.where(lane == 1, bias_grad(dq1, dk1), 0.0))
        first = jnp.logical_and(b == 0, p == 0)
        _acc(dgq_ref, gq_part * scale, first)
        _acc(dgk_ref, gk_part, first)

    gs = jax.ShapeDtypeStruct((1, LANE), F32)
    return _ride_call(
        body, rider, name="fox_bwd", grid=(b_, npair),
        in_specs=[col(0), col(1), col(2), pair, pair, lse_spec, full, gvec, gvec, ANY],
        out_specs=[pl.BlockSpec((1, t_, 3 * LANE), lambda b, p: (b, 0, col0 // 3 + p)), lse_spec, gvec, gvec],
        out_shape=[jax.ShapeDtypeStruct(dz.shape, dz.dtype), jax.ShapeDtypeStruct((b_, npair, t_, LANE), F32), gs, gs],
        scratch=[pltpu.VMEM((2, t_, LANE), MXU_DTYPE)] * 4
        + [pltpu.VMEM((8, t_), F32)] + [pltpu.VMEM((2, t_, LANE), F32)] * 3,
        args=(zm, zm, zm, o, do, lse, fc, gq2, gk2, dz), aliases={9: 0})


def _mem_specs(t_, m_, mw, col0):
    nh = mw // LANE
    qcol = pl.BlockSpec((1, t_, LANE), lambda b, h: (b, 0, col0 + h))
    kcol = pl.BlockSpec((1, m_, LANE), lambda b, h: (b, 0, h))
    vcol = pl.BlockSpec((1, m_, LANE), lambda b, h: (b, 0, nh + h))
    ycol = pl.BlockSpec((1, t_, LANE), lambda b, h: (b, 0, h))
    gvec = pl.BlockSpec((1, LANE), lambda b, h: (0, 0))
    return qcol, kcol, vcol, ycol, gvec


def _mem_fwd(zm, mkv, gq, gk, mw, col0):
    b_, t_, _ = zm.shape
    m_ = mkv.shape[1]
    tq = min(512, t_)
    nb = t_ // tq
    scale = MEM_DH ** -0.5
    qcol, kcol, vcol, ycol, gvec = _mem_specs(t_, m_, mw, col0)

    def body(q_ref, k_ref, v_ref, gq_ref, gk_ref, y_ref):
        gqv, gkv = gq_ref[...] * scale, gk_ref[...]
        kv = k_ref[0]
        kn = _mx(kv * lax.rsqrt(jnp.mean(kv * kv, axis=-1, keepdims=True) + EPS) * gkv)
        vv = _mx(v_ref[0])

        def blk(i, _):
            rows = pl.ds(pl.multiple_of(i * tq, tq), tq)
            qv = q_ref[0, rows, :]
            s = _nt(qv * lax.rsqrt(jnp.mean(qv * qv, axis=-1, keepdims=True) + EPS) * gqv, kn)
            e = jnp.exp(s - jnp.max(s, axis=-1, keepdims=True))
            y_ref[0, rows, :] = _nn(e / jnp.sum(e, axis=-1, keepdims=True), vv)
            return 0

        lax.fori_loop(0, nb, blk, 0)

    return pl.pallas_call(
        body, name="mem_fwd", grid=(b_, MEM_HEADS), in_specs=[qcol, kcol, vcol, gvec, gvec], out_specs=ycol,
        out_shape=jax.ShapeDtypeStruct((b_, t_, mw), F32), compiler_params=_params(2),
    )(zm, mkv, mkv, gq, gk)


def _mem_bwd(zm, mkv, dy, gq, gk, mw, col0, dz):
    b_, t_, _ = zm.shape
    m_ = mkv.shape[1]
    tq = min(512, t_)
    nb = t_ // tq
    scale = MEM_DH ** -0.5
    qcol, kcol, vcol, ycol, gvec = _mem_specs(t_, m_, mw, col0)

    def body(q_ref, k_ref, v_ref, dy_ref, gq_ref, gk_ref, _, dq_ref, dk_ref, dv_ref, dgq_ref, dgk_ref):
        gqv, gkv = gq_ref[...] * scale, gk_ref[...]
        kv = k_ref[0]
        kn = _mx(kv * lax.rsqrt(jnp.mean(kv * kv, axis=-1, keepdims=True) + EPS) * gkv)
        vv = _mx(v_ref[0])

        def blk(i, carry):
            dkn, dvv, dgq = carry
            rows = pl.ds(pl.multiple_of(i * tq, tq), tq)
            qv = q_ref[0, rows, :]
            qn = _mx(qv * lax.rsqrt(jnp.mean(qv * qv, axis=-1, keepdims=True) + EPS) * gqv)
            s = _nt(qn, kn)
            e = jnp.exp(s - jnp.max(s, axis=-1, keepdims=True))
            pm = e / jnp.sum(e, axis=-1, keepdims=True)
            dob = _mx(dy_ref[0, rows, :])
            dp = _nt(dob, vv)
            ds = pm * (dp - jnp.sum(dp * pm, axis=-1, keepdims=True))
            dqv, gq_part = _norm_bwd(qv, _nn(ds, kn), gqv)
            dq_ref[0, rows, :] = dqv.astype(dq_ref.dtype)
            return dkn + _tn(ds, qn), dvv + _tn(pm, dob), dgq + gq_part * scale

        z = jnp.zeros((m_, LANE), F32)
        dkn, dvv, dgq = lax.fori_loop(0, nb, blk, (z, z, jnp.zeros((1, LANE), F32)))
        dkv, dgk = _norm_bwd(kv, dkn, gkv)
        dk_ref[0] = dkv
        dv_ref[0] = dvv
        first = jnp.logical_and(pl.program_id(0) == 0, pl.program_id(1) == 0)
        _acc(dgq_ref, dgq, first)
        _acc(dgk_ref, dgk, first)

    kblk = pl.BlockSpec((1, m_, LANE), lambda b, h: (b, 0, h))
    gs = jax.ShapeDtypeStruct((1, LANE), F32)
    ks = jax.ShapeDtypeStruct((b_, m_, mw), F32)
    return pl.pallas_call(
        body, name="mem_bwd", grid=(b_, MEM_HEADS), in_specs=[qcol, kcol, vcol, ycol, gvec, gvec, ANY],
        out_specs=[qcol, kblk, kblk, gvec, gvec],
        out_shape=[jax.ShapeDtypeStruct(dz.shape, dz.dtype), ks, ks, gs, gs], input_output_aliases={6: 0},
        compiler_params=_params(2),
    )(zm, mkv, mkv, dy, gq, gk, dz)


def _merge_specs(tm, d, w, gcol):
    row_d = pl.BlockSpec((tm, d), lambda i: (i, 0))
    row_w = pl.BlockSpec((tm, w), lambda i: (i, 0))
    gates = [pl.BlockSpec((tm, d), functools.partial(lambda i, k: (i, gcol + k), k=k)) for k in range(3)]
    w_br = pl.BlockSpec((w, d), lambda i: (0, 0))
    w_o = pl.BlockSpec((d, d), lambda i: (0, 0))
    return row_d, row_w, gates, w_br, w_o


def _merge_fwd(x, ys, zm, w_brs, w_out, gcol, g_next, tm=256):
    n, d = x.shape
    w = ys[0].shape[1]
    tm = _tile(n, tm, 8)
    row_d, row_w, gates, w_br, w_o = _merge_specs(tm, d, w, gcol)

    def body(x_ref, ya, yb, yc, g0, g1, g2, wa, wb, wc, wo, gn_ref, x1_ref, mg_ref, h_ref):
        mg = (_sig(g0[...]) * _nn(ya[...], wa[...]) + _sig(g1[...]) * _nn(yb[...], wb[...])
              + _sig(g2[...]) * _nn(yc[...], wc[...]))
        mg_ref[...] = mg.astype(mg_ref.dtype)
        x1 = x_ref[...] + _nn(mg, wo[...])
        x1_ref[...] = x1
        h_ref[...] = (x1 * lax.rsqrt(jnp.mean(x1 * x1, axis=-1, keepdims=True) + EPS) * gn_ref[...]).astype(h_ref.dtype)

    half = jax.ShapeDtypeStruct((n, d), MXU_DTYPE)
    return pl.pallas_call(
        body, name="merge_fwd", grid=(n // tm,),
        in_specs=[row_d, row_w, row_w, row_w] + gates + [w_br, w_br, w_br, w_o, pl.BlockSpec((1, d), lambda i: (0, 0))],
        out_specs=[row_d, row_d, row_d],
        out_shape=[jax.ShapeDtypeStruct((n, d), F32), half, half],
        compiler_params=_params(1),
    )(x, *ys, zm, zm, zm, *w_brs, w_out, g_next)


def _merge_bwd(dx1, ys, zm, w_brs, w_out, gcol, tm=256):
    n, d = dx1.shape
    w = ys[0].shape[1]
    tm = _tile(n, tm, 8)
    row_d, row_w, gates, w_br, w_o = _merge_specs(tm, d, w, gcol)

    def body(dx_ref, ya, yb, yc, g0, g1, g2, wa, wb, wc, wo, dgl_ref, dpa, dpb, dpc, dya, dyb, dyc):
        dm = _nt(dx_ref[...], wo[...])
        for k, (y, g, wr, dp_ref, dy_ref) in enumerate(((ya, g0, wa, dpa, dya), (yb, g1, wb, dpb, dyb),
                                                        (yc, g2, wc, dpc, dyc))):
            sg = _sig(g[...])
            pr = _nn(y[...], wr[...])
            dgl_ref[:, k * d:(k + 1) * d] = (dm * pr * sg * (1.0 - sg)).astype(dgl_ref.dtype)
            dp = (dm * sg).astype(dp_ref.dtype)
            dp_ref[...] = dp
            dy_ref[...] = _nt(dp, wr[...])

    sd = jax.ShapeDtypeStruct((n, d), MXU_DTYPE)
    sw = jax.ShapeDtypeStruct((n, w), F32)
    return pl.pallas_call(
        body, name="merge_bwd", grid=(n // tm,),
        in_specs=[row_d, row_w, row_w, row_w] + gates + [w_br, w_br, w_br, w_o],
        out_specs=[pl.BlockSpec((tm, 3 * d), lambda i: (i, 0)), row_d, row_d, row_d, row_w, row_w, row_w],
        out_shape=[jax.ShapeDtypeStruct((n, zm.shape[1]), MXU_DTYPE), sd, sd, sd, sw, sw, sw],
        compiler_params=_params(1),
    )(dx1, *ys, zm, zm, zm, *w_brs, w_out)


CONV_ROWS = 512
HALO = 8


def _ext(ref, r0, t_):
    rc = min(CONV_ROWS, t_)
    a, b = max(r0 - HALO, 0), min(r0 + rc + HALO, t_)
    parts = []
    if r0 - HALO < 0:
        parts.append(jnp.zeros((HALO, ref.shape[2]), F32))
    parts.append(ref[0, a:b, :].astype(F32))
    if r0 + rc + HALO > t_:
        parts.append(jnp.zeros((HALO, ref.shape[2]), F32))
    return jnp.concatenate(parts, axis=0) if len(parts) > 1 else parts[0]


def _gelu_parts(ac):
    e = jnp.exp(-0.5 * ac * ac)
    t = 1.0 / (1.0 + (0.3275911 * 2.0 ** -0.5) * jnp.abs(ac))
    tail = (0.5 * e) * (t * (0.254829592 + t * (-0.284496736 + t * (1.421413741 + t * (-1.453152027 + t * 1.061405429)))))
    return jnp.where(ac < 0, tail, 1.0 - tail), e * ((2.0 * math.pi) ** -0.5)


def _conv_taps(a_ext, cw, cb):
    a2, a1 = pltpu.roll(a_ext, 2, 0), pltpu.roll(a_ext, 1, 0)
    return cw[0:1, :] * a2 + cw[1:2, :] * a1 + cw[2:3, :] * a_ext + cb, a2, a1


def _glu_specs(t_, f, g):
    gate = pl.BlockSpec((1, t_, g), lambda j, b: (b, 0, j))
    value = pl.BlockSpec((1, t_, g), lambda j, b: (b, 0, f // g + j))
    cwb = pl.BlockSpec((3, g), lambda j, b: (0, j))
    cbb = pl.BlockSpec((1, g), lambda j, b: (0, j))
    return gate, value, cwb, cbb


def _glu_fwd(u, cw, cb):
    b_, t_, f2 = u.shape
    f = f2 // 2
    g = min(FFN_GROUP, f)
    rc = min(CONV_ROWS, t_)
    gate, value, cwb, cbb = _glu_specs(t_, f, g)

    def body(a_ref, v_ref, cw_ref, cb_ref, y_ref):
        cwv, cbv = cw_ref[...], cb_ref[...]
        for r0 in range(0, t_, rc):
            ac = _conv_taps(_ext(a_ref, r0, t_), cwv, cbv)[0][HALO:HALO + rc]
            cdf, _ = _gelu_parts(ac)
            y_ref[0, r0:r0 + rc, :] = (ac * cdf * v_ref[0, r0:r0 + rc, :]).astype(y_ref.dtype)

    return pl.pallas_call(
        body, name="glu_fwd", grid=(f // g, b_), in_specs=[gate, value, cwb, cbb], out_specs=gate,
        out_shape=jax.ShapeDtypeStruct((b_, t_, f), MXU_DTYPE), compiler_params=_params(2),
    )(u, u, cw, cb)


def _glu_bwd(u, dy, cw, cb):
    b_, t_, f2 = u.shape
    f = f2 // 2
    g = min(FFN_GROUP, f)
    rc = min(CONV_ROWS, t_)
    ne = rc + 2 * HALO
    gate, value, cwb, cbb = _glu_specs(t_, f, g)

    def body(a_ref, v_ref, dy_ref, cw_ref, cb_ref, da_ref, dv_ref, dcw_ref, dcb_ref):
        cwv, cbv = cw_ref[...], cb_ref[...]
        dcw = [jnp.zeros((1, g), F32) for _ in range(3)]
        dcb = jnp.zeros((1, g), F32)
        for r0 in range(0, t_, rc):
            a_ext, v_ext, dy_ext = _ext(a_ref, r0, t_), _ext(v_ref, r0, t_), _ext(dy_ref, r0, t_)
            ac, a2, a1 = _conv_taps(a_ext, cwv, cbv)
            cdf, pdf = _gelu_parts(ac)
            dac = dy_ext * v_ext * (cdf + ac * pdf)
            da = cwv[2:3, :] * dac + cwv[1:2, :] * pltpu.roll(dac, ne - 1, 0) + cwv[0:1, :] * pltpu.roll(dac, ne - 2, 0)
            mid = slice(HALO, HALO + rc)
            da_ref[0, r0:r0 + rc, :] = da[mid].astype(da_ref.dtype)
            dv_ref[0, r0:r0 + rc, :] = (dy_ext[mid] * ac[mid] * cdf[mid]).astype(dv_ref.dtype)
            dacm = dac[mid]
            dcw[0] = dcw[0] + jnp.sum(dacm * a2[mid], axis=0, keepdims=True)
            dcw[1] = dcw[1] + jnp.sum(dacm * a1[mid], axis=0, keepdims=True)
            dcw[2] = dcw[2] + jnp.sum(dacm * a_ext[mid], axis=0, keepdims=True)
            dcb = dcb + jnp.sum(dacm, axis=0, keepdims=True)
        first = pl.program_id(1) == 0
        _acc(dcw_ref, jnp.concatenate(dcw, axis=0), first)
        _acc(dcb_ref, dcb, first)

    sds = jax.ShapeDtypeStruct((b_, t_, f), MXU_DTYPE)
    return pl.pallas_call(
        body, name="glu_bwd", grid=(f // g, b_), in_specs=[gate, value, gate, cwb, cbb],
        out_specs=[gate, gate, cwb, cbb],
        out_shape=[sds, sds, jax.ShapeDtypeStruct((3, f), F32), jax.ShapeDtypeStruct((1, f), F32)],
        compiler_params=_params(2),
    )(u, u, dy, cw, cb)


def _place():
    x, y, c = lax.axis_index("x"), lax.axis_index("y"), lax.axis_index("c")
    chips = [(1 - x, y), (x, 1 - y), (1 - x, 1 - y)]
    return x, y, c, chips


def _remote(src, dst, send_sem, recv_sem, to):
    return pltpu.make_async_remote_copy(src_ref=src, dst_ref=dst, send_sem=send_sem, recv_sem=recv_sem,
                                        device_id=to, device_id_type=MESH)


STACK, COLS = "stack", "cols"


def _shard_ref(ref, kind, s, rows, c):
    if kind == COLS:
        cols = pl.ds(pl.multiple_of(s * c, LANE), c)
        return ref.at[:, cols] if rows is None else ref.at[rows, cols]
    return ref.at[s] if rows is None else ref.at[s, rows, :]


def _halves(c, half):
    mine = pl.ds(pl.multiple_of(c * half, 16), half)
    theirs = pl.ds(pl.multiple_of((1 - c) * half, 16), half)
    return mine, theirs


def _gather_parts(kinds):
    def first_copies(ins, outs, sems):
        x, y, c, chips = _place()
        me = 2 * x + y
        cps = []
        for i, (w_ref, o_ref, kind) in enumerate(zip(ins, outs, kinds)):
            r, cw = w_ref.shape
            mine, _ = _halves(c, r // 2)
            for j, chip in enumerate(chips):
                cps.append(_remote(w_ref.at[mine], _shard_ref(o_ref, kind, me, mine, cw), sems[0].at[6 * i + j],
                                   sems[1].at[6 * i + j], (*chip, c)))
        return cps

    def start(ins, outs, sems):
        for cp in first_copies(ins, outs, sems):
            cp.start()

    def finish(ins, outs, sems):
        x, y, c, chips = _place()
        sib = (x, y, 1 - c)
        passed = []
        for i, (w_ref, o_ref, kind) in enumerate(zip(ins, outs, kinds)):
            r, cw = w_ref.shape
            mine, _ = _halves(c, r // 2)
            for j, (px, py) in enumerate(chips):
                blk = _shard_ref(o_ref, kind, 2 * px + py, mine, cw)
                _remote(blk, blk, sems[0].at[6 * i + j], sems[1].at[6 * i + j], sib).wait_recv()
                passed.append(_remote(blk, blk, sems[0].at[6 * i + 3 + j], sems[1].at[6 * i + 3 + j], sib))
                passed[-1].start()
        for i, (w_ref, o_ref, kind) in enumerate(zip(ins, outs, kinds)):
            r, cw = w_ref.shape
            _, theirs = _halves(c, r // 2)
            for j, (px, py) in enumerate(chips):
                blk = _shard_ref(o_ref, kind, 2 * px + py, theirs, cw)
                _remote(blk, blk, sems[0].at[6 * i + 3 + j], sems[1].at[6 * i + 3 + j], sib).wait_recv()
        for cp in first_copies(ins, outs, sems) + passed:
            cp.wait_send()

    return start, finish


def _gather_shapes(shards, kinds):
    return [jax.ShapeDtypeStruct((a.shape[0], N_CHIPS * a.shape[1]) if k == COLS else (N_CHIPS,) + a.shape, a.dtype)
            for a, k in zip(shards, kinds)]


def _gather_sems(nw):
    return [pltpu.SemaphoreType.DMA((6 * nw,)), pltpu.SemaphoreType.DMA((6 * nw,))]


def _gather_shards(shards, kinds):
    nw = len(shards)
    start, finish = _gather_parts(kinds)

    def body(*refs):
        ins, outs, sems = refs[:nw], refs[nw:2 * nw], refs[2 * nw:]
        start(ins, outs, sems)
        finish(ins, outs, sems)

    return pl.pallas_call(
        body, name="gather_shards", in_specs=[ANY] * nw, out_specs=[ANY] * nw,
        out_shape=_gather_shapes(shards, kinds), scratch_shapes=_gather_sems(nw),
    )(*shards)


def _gather_rider(shards, kinds):
    start, finish = _gather_parts(kinds)
    return _Rider(list(shards), _gather_shapes(shards, kinds), _gather_sems(len(shards)), start, finish)


def _half_shape(g, kind):
    if kind == COLS:
        return (g.shape[0] // 2, g.shape[1])
    return (g.shape[0], g.shape[1] // 2, g.shape[2])


def _swap_parts(kinds):
    def copies(ins, outs, sems):
        x, y, c, _ = _place()
        cps = []
        for i, (g_ref, a_ref, kind) in enumerate(zip(ins, outs, kinds)):
            r = g_ref.shape[0] if kind == COLS else g_ref.shape[1]
            _, theirs = _halves(c, r // 2)
            src = g_ref.at[theirs] if kind == COLS else g_ref.at[:, theirs]
            cps.append(_remote(src, a_ref, sems[0].at[i], sems[1].at[i], (x, y, 1 - c)))
        return cps

    def start(ins, outs, sems):
        for cp in copies(ins, outs, sems):
            cp.start()

    def finish(ins, outs, sems):
        for cp in copies(ins, outs, sems):
            cp.wait()

    return start, finish


def _swap_shapes(gs, kinds):
    return [jax.ShapeDtypeStruct(_half_shape(g, k), g.dtype) for g, k in zip(gs, kinds)]


def _pair_swap_halves(gs, kinds, name):
    nw = len(gs)
    start, finish = _swap_parts(kinds)

    def body(*refs):
        ins, outs, sems = refs[:nw], refs[nw:2 * nw], refs[2 * nw:]
        start(ins, outs, sems)
        finish(ins, outs, sems)

    return pl.pallas_call(
        body, name=name, in_specs=[ANY] * nw, out_specs=[ANY] * nw, out_shape=_swap_shapes(gs, kinds),
        scratch_shapes=[pltpu.SemaphoreType.DMA((nw,)), pltpu.SemaphoreType.DMA((nw,))],
    )(*gs)


def _swap_rider(gs, kinds):
    start, finish = _swap_parts(kinds)
    nw = len(gs)
    return _Rider(list(gs), _swap_shapes(gs, kinds), [pltpu.SemaphoreType.DMA((nw,)), pltpu.SemaphoreType.DMA((nw,))],
                  start, finish)


def _row_tile(rows, width, itemsize=4, target=2 ** 21):
    return _tile(rows, max(8, target // (width * itemsize)), 8)


def _add_half(g, a, kind, c_idx, name):
    if kind == COLS:
        half, wd = a.shape
        tr = _row_tile(half, wd)
        nblk = half // tr
        grid = (nblk,)
        g_spec = pl.BlockSpec((tr, wd), lambda i, c_ref: (c_ref[0] * nblk + i, 0))
        a_spec = pl.BlockSpec((tr, wd), lambda i, c_ref: (i, 0))
    else:
        n, half, wd = a.shape
        tr = _row_tile(half, wd)
        nblk = half // tr
        grid = (n, nblk)
        g_spec = pl.BlockSpec((1, tr, wd), lambda s, i, c_ref: (s, c_ref[0] * nblk + i, 0))
        a_spec = pl.BlockSpec((1, tr, wd), lambda s, i, c_ref: (s, i, 0))

    def body(c_ref, g_ref, a_ref, o_ref):
        o_ref[...] = (g_ref[...] + a_ref[...]).astype(o_ref.dtype)

    return pl.pallas_call(
        body, name=name,
        grid_spec=pltpu.PrefetchScalarGridSpec(num_scalar_prefetch=1, grid=grid, in_specs=[g_spec, a_spec],
                                               out_specs=a_spec),
        out_shape=jax.ShapeDtypeStruct(a.shape, EXCHANGE_DTYPE), compiler_params=_params(len(grid)),
    )(c_idx, g, a)


def _exchange_parts(kinds):
    def copies(ins, outs, sems):
        x, y, c, chips = _place()
        me = 2 * x + y
        cps = []
        for i, (p_ref, b_ref, kind) in enumerate(zip(ins, outs, kinds)):
            cw = b_ref.shape[2]
            for j, (px, py) in enumerate(chips):
                cps.append(_remote(_shard_ref(p_ref, kind, 2 * px + py, None, cw), b_ref.at[me],
                                   sems[0].at[3 * i + j], sems[1].at[3 * i + j], (px, py, c)))
        return cps

    def start(ins, outs, sems):
        for cp in copies(ins, outs, sems):
            cp.start()

    def finish(ins, outs, sems):
        x, y, c, chips = _place()
        for i, b_ref in enumerate(outs):
            for j, (px, py) in enumerate(chips):
                blk = b_ref.at[2 * px + py]
                _remote(blk, blk, sems[0].at[3 * i + j], sems[1].at[3 * i + j], (px, py, c)).wait_recv()
        for cp in copies(ins, outs, sems):
            cp.wait_send()

    return start, finish


def _exchange_shapes(ps, kinds):
    return [jax.ShapeDtypeStruct((N_CHIPS,) + ((p.shape[0], p.shape[1] // N_CHIPS) if k == COLS else tuple(p.shape[1:])),
                                 p.dtype) for p, k in zip(ps, kinds)]


def _exchange_sems(nw):
    return [pltpu.SemaphoreType.DMA((3 * nw,)), pltpu.SemaphoreType.DMA((3 * nw,))]


def _exchange_rider(ps, kinds):
    start, finish = _exchange_parts(kinds)
    return _Rider(list(ps), _exchange_shapes(ps, kinds), _exchange_sems(len(ps)), start, finish)


def _sum_chips(bq, name):
    n, h, wd = bq.shape
    tr = _row_tile(h, wd * n)

    def body(b_ref, o_ref):
        acc = b_ref[0].astype(F32)
        for s in range(1, n):
            acc = acc + b_ref[s].astype(F32)
        o_ref[...] = acc

    return pl.pallas_call(
        body, name=name, grid=(h // tr,),
        in_specs=[pl.BlockSpec((n, tr, wd), lambda i: (0, i, 0))], out_specs=pl.BlockSpec((tr, wd), lambda i: (i, 0)),
        out_shape=jax.ShapeDtypeStruct((h, wd), F32), compiler_params=_params(1),
    )(bq)


def _pair_join_halves(qs):
    nw = len(qs)

    def body(*refs):
        ins, outs = refs[:nw], refs[nw:2 * nw]
        send_sems, recv_sems = refs[2 * nw:]
        x, y, c, _ = _place()
        sent = []
        for i, (q_ref, o_ref) in enumerate(zip(ins, outs)):
            sent.append(_remote(q_ref, o_ref.at[c], send_sems.at[i], recv_sems.at[i], (x, y, 1 - c)))
            sent[-1].start()
        for i, (q_ref, o_ref) in enumerate(zip(ins, outs)):
            _remote(q_ref, o_ref.at[1 - c], send_sems.at[i], recv_sems.at[i], (x, y, 1 - c)).wait_recv()
        for cp in sent:
            cp.wait_send()

    return pl.pallas_call(
        body, name="pair_join_halves", in_specs=[ANY] * nw, out_specs=[ANY] * nw,
        out_shape=[jax.ShapeDtypeStruct((2,) + q.shape, q.dtype) for q in qs],
        scratch_shapes=[pltpu.SemaphoreType.DMA((nw,)), pltpu.SemaphoreType.DMA((nw,))],
    )(*qs)


def _all_sum_small(s, name):
    sr, w = s.shape

    def body(s_ref, o_ref, buf, send_sems, recv_sems):
        x, y, c, _ = _place()
        me = 4 * x + 2 * y + c
        buf[me] = s_ref[...]
        peers = []
        for k in range(1, 8):
            px = 1 - x if k & 4 else x
            py = 1 - y if k & 2 else y
            pc = 1 - c if k & 1 else c
            peers.append((px, py, pc))
        sent = [_remote(s_ref, buf.at[me], send_sems.at[k], recv_sems.at[k], peer) for k, peer in enumerate(peers)]
        for cp in sent:
            cp.start()
        for k, (px, py, pc) in enumerate(peers):
            _remote(s_ref, buf.at[4 * px + 2 * py + pc], send_sems.at[k], recv_sems.at[k], (px, py, pc)).wait_recv()
        for cp in sent:
            cp.wait_send()
        acc = buf[0]
        for d in range(1, 8):
            acc = acc + buf[d]
        o_ref[...] = acc

    vm = pl.BlockSpec(memory_space=pltpu.VMEM)
    return pl.pallas_call(
        body, name=name, in_specs=[vm], out_specs=vm, out_shape=jax.ShapeDtypeStruct((sr, w), F32),
        scratch_shapes=[pltpu.VMEM((8, sr, w), F32), pltpu.SemaphoreType.DMA((7,)), pltpu.SemaphoreType.DMA((7,))],
    )(s)


BIG = ("w_in", "mem_kv_w", "w_br_hgrn", "w_br_fox", "w_br_mem", "w_out", "ffn_w_up", "ffn_w_down")
KIND = {"w_in": STACK, "mem_kv_w": STACK, "w_br_hgrn": COLS, "w_br_fox": COLS, "w_br_mem": COLS, "w_out": STACK,
        "ffn_w_up": STACK, "ffn_w_down": STACK}
ROW_SHARDED = ("mem_kv_w", "w_out", "ffn_w_down")
FIRST = ("w_in",)
REST = tuple(nm for nm in BIG if nm not in FIRST)
LATE = {"in_proj": tuple(nm for nm in REST if not nm.startswith("ffn_")),
        "fox_fwd": tuple(nm for nm in REST if nm.startswith("ffn_"))}
LAST = ("w_in",)
TRANSPOSED = ("w_in",)


def _z_layout(d, hw, fw, mw):
    gate, npair, nh, nm = 3 * d // LANE, fw // LANE, hw // LANE, mw // LANE
    fox0, hg0 = gate, gate + 3 * npair
    o_fox, o_mem = 4 * nh, 4 * nh + 3 * npair
    order = [o_mem + nm + j for j in range(gate)]
    order += [o_fox + k * npair + p for p in range(npair) for k in range(3)]
    order += [k * nh + h for h in range(nh) for k in range(4)]
    order += [o_mem + h for h in range(nm)]
    assert fox0 % 3 == 0 and hg0 % 4 == 0
    return fox0, hg0, hg0 + 4 * nh, order


def _reorder_blocks(a, order):
    runs, start = [], 0
    for i in range(1, len(order) + 1):
        if i == len(order) or order[i] != order[i - 1] + 1:
            runs.append((order[start], order[i - 1] + 1))
            start = i
    return jnp.concatenate([a[:, lo * LANE:hi * LANE] for lo, hi in runs], axis=1)


def _put_shard(arr, kind, s, piece):
    if kind == COLS:
        return lax.dynamic_update_slice(arr, piece, (0, s * piece.shape[1]))
    return lax.dynamic_update_slice(arr, piece[None], (s, 0, 0))


def _take_shard(arr, kind, s):
    if kind == COLS:
        return lax.dynamic_slice(arr, (0, s * (arr.shape[1] // N_CHIPS)), (arr.shape[0], arr.shape[1] // N_CHIPS))
    return lax.dynamic_index_in_dim(arr, s, 0, keepdims=False)


def _w_in_pieces(cs, s1, nf):
    out = []
    for s in range(N_CHIPS):
        lo, hi = cs * s, cs * (s + 1)
        for a, b, forget in ((lo, min(hi, s1), False), (max(lo, s1), min(hi, s1 + nf), True), (max(lo, s1 + nf), hi, False)):
            if a < b:
                out.append((s, a - lo, b - lo, forget, a - s1 if forget else (a if a < s1 else a - nf)))
    return out


def _split_w_in(stacked, s1, nf):
    pieces = _w_in_pieces(stacked.shape[2], s1, nf)
    main = [stacked[s, :, a:b] for s, a, b, forget, _ in pieces if not forget]
    ff = [stacked[s, :, a:b] for s, a, b, forget, _ in pieces if forget]
    return jnp.concatenate(main, axis=1), jnp.concatenate(ff, axis=1)


def _join_w_in(g_main, g_ff, s1, nf):
    cs = (g_main.shape[1] + nf) // N_CHIPS
    shards = [[] for _ in range(N_CHIPS)]
    for s, a, b, forget, off in _w_in_pieces(cs, s1, nf):
        shards[s].append((g_ff if forget else g_main)[:, off:off + b - a])
    return jnp.stack([jnp.concatenate(p, axis=1) if len(p) > 1 else p[0] for p in shards])


SMALL = ("norm_mix_g", "norm_mem_g", "norm_ffn_g", "hgrn_lb_logits", "hgrn_norm_g", "fox_f_bias", "fox_q_norm_g",
         "fox_k_norm_g", "mem_q_norm_g", "mem_k_norm_g", "ffn_conv_b")


def _small_rows(shapes):
    rows = []
    for a, (r, c) in enumerate(shapes):
        for i in range(r):
            for lo in range(0, c, FLAT_W):
                rows.append((a, i, lo, min(FLAT_W, c - lo)))
    return rows


def _pack_small(vals):
    rows = _small_rows([v.shape for v in vals])
    sr = -(-len(rows) // 8) * 8

    def body(*refs):
        o_ref = refs[-1]
        o_ref[...] = jnp.zeros(o_ref.shape, F32)
        for k, (a, i, lo, wd) in enumerate(rows):
            o_ref[k:k + 1, 0:wd] = refs[a][i:i + 1, lo:lo + wd]

    vm = pl.BlockSpec(memory_space=pltpu.VMEM)
    return pl.pallas_call(body, name="pack_small", in_specs=[vm] * len(vals), out_specs=vm,
                          out_shape=jax.ShapeDtypeStruct((sr, FLAT_W), F32))(*vals)


def _row_of(buf_ref, rows, a, i):
    parts = [buf_ref[k:k + 1, 0:wd] for k, (a2, i2, _, wd) in enumerate(rows) if (a2, i2) == (a, i)]
    return jnp.concatenate(parts, axis=1) if len(parts) > 1 else parts[0]


def _unpack_small(buf, shapes):
    rows = _small_rows(shapes)

    def body(buf_ref, *outs):
        for a, (r, _) in enumerate(shapes):
            for i in range(r):
                outs[a][i:i + 1, :] = _row_of(buf_ref, rows, a, i)

    vm = pl.BlockSpec(memory_space=pltpu.VMEM)
    return pl.pallas_call(body, name="unpack_small", in_specs=[vm], out_specs=[vm] * len(shapes),
                          out_shape=[jax.ShapeDtypeStruct(shp, F32) for shp in shapes])(buf)


def _adamw_small(buf, shapes, ws, ms, vs):
    n = len(ws)
    rows = _small_rows(shapes)
    c1 = 1.0 / (1.0 - ADAM_B1 ** ADAM_STEP)
    c2 = 1.0 / (1.0 - ADAM_B2 ** ADAM_STEP)

    def body(buf_ref, *refs):
        w_refs, m_refs, v_refs = refs[:n], refs[n:2 * n], refs[2 * n:3 * n]
        outs = refs[3 * n:]
        g_out, d_out, m_out, v_out, rest = outs[:n], outs[n:2 * n], outs[2 * n:3 * n], outs[3 * n:4 * n], outs[4 * n:]
        for a, (r, _) in enumerate(shapes):
            for i in range(r):
                gv = _row_of(buf_ref, rows, a, i)
                if a >= n:
                    rest[a - n][i:i + 1, :] = gv
                    continue
                row = slice(i, i + 1)
                mn = ADAM_B1 * m_refs[a][row, :] + (1.0 - ADAM_B1) * gv
                vn = ADAM_B2 * v_refs[a][row, :] + (1.0 - ADAM_B2) * (gv * gv)
                g_out[a][row, :] = gv
                d_out[a][row, :] = -ADAM_LR * ((mn * c1) / (jnp.sqrt(vn * c2) + ADAM_EPS) + ADAM_WD * w_refs[a][row, :])
                m_out[a][row, :] = mn
                v_out[a][row, :] = vn

    vm = pl.BlockSpec(memory_space=pltpu.VMEM)
    own = [jax.ShapeDtypeStruct(shp, F32) for shp in shapes[:n]]
    outs = pl.pallas_call(
        body, name="adamw_small", in_specs=[vm] * (1 + 3 * n), out_specs=[vm] * (4 * n + len(shapes) - n),
        out_shape=own * 4 + [jax.ShapeDtypeStruct(shp, F32) for shp in shapes[n:]],
    )(buf, *ws, *ms, *vs)
    return outs[:n], outs[n:2 * n], outs[2 * n:3 * n], outs[3 * n:4 * n], outs[4 * n:]


def _pad_lanes(v, width=LANE):
    return jnp.pad(v, ((0, 0), (0, width - v.shape[1])))


WEIGHTS = ("norm_mix_g", "norm_mem_g", "w_in", "hgrn_lb_logits", "hgrn_norm_g", "fox_f_bias", "fox_q_norm_g",
           "fox_k_norm_g", "mem_kv_w", "mem_q_norm_g", "mem_k_norm_g", "w_br_hgrn", "w_br_fox", "w_br_mem", "w_out",
           "norm_ffn_g", "ffn_w_up", "ffn_conv_w", "ffn_conv_b", "ffn_w_down")


def _local_step(x, mem, target, w, full, conv_w, late=None, hooks=None):
    b_, t_, d = x.shape
    n = b_ * t_
    hw, fw, mw = HG_HEADS * HG_D, FOX_HEADS * FOX_DH, MEM_HEADS * MEM_DH
    m_ = mem.shape[1]
    f = conv_w.shape[1]
    s1 = 4 * hw + 3 * fw
    fox_col, hg_col, mem_col, order = _z_layout(d, hw, fw, mw)
    gate_col = 0
    inverse = [order.index(j) for j in range(len(order))]

    w_main, w_ff = _split_w_in(full["w_in"], s1, FOX_HEADS)
    w_main = _reorder_blocks(w_main, order)
    w_ff = _pad_lanes(w_ff)
    f_bias = _pad_lanes(w["fox_f_bias"])
    cb = w["ffn_conv_b"]

    x2 = x.reshape(n, d)
    h = _rmsnorm_fwd(x2, w["norm_mix_g"], name="norm_mix_fwd")
    if late:
        pieces, kinds, finish = late["in_proj"]
        zm, gathered = _matmul(h, w_main, name="in_proj", rider=_gather_rider(pieces, kinds))
        full = {**full, **finish(gathered)}
    else:
        zm = _matmul(h, w_main, name="in_proj")
    w_brs = [full["w_br_hgrn"], full["w_br_fox"], full["w_br_mem"]]
    w_out, w_kv = full["w_out"], full["mem_kv_w"]
    zf = _matmul(h, w_ff, name="in_proj_forget")
    zm3, zf3 = zm.reshape(b_, t_, -1), zf.reshape(b_, t_, LANE)
    ya = _hgrn_fwd(zm3, w["hgrn_lb_logits"], w["hgrn_norm_g"], hw, hg_col)
    fc = _fox_prep(zf3, f_bias)
    fox_gq, fox_gk = jnp.tile(w["fox_q_norm_g"], (1, 2)), jnp.tile(w["fox_k_norm_g"], (1, 2))
    if late:
        pieces, kinds, finish = late["fox_fwd"]
        (yb, lse), gathered = _fox_fwd(zm3, fc, fox_gq, fox_gk, fw, fox_col, _gather_rider(pieces, kinds))
        full = {**full, **finish(gathered)}
    else:
        yb, lse = _fox_fwd(zm3, fc, fox_gq, fox_gk, fw, fox_col)[0]
    w_up, w_down = full["ffn_w_up"], full["ffn_w_down"]
    mem2 = mem.reshape(b_ * m_, d)
    hm = _rmsnorm_fwd(mem2, w["norm_mem_g"], name="norm_mem_fwd")
    mkv = _matmul(hm, w_kv, name="mem_kv_proj").reshape(b_, m_, 2 * mw)
    yc = _mem_fwd(zm3, mkv, w["mem_q_norm_g"], w["mem_k_norm_g"], mw, mem_col)
    ys = [ya.reshape(n, hw), yb.reshape(n, fw), yc.reshape(n, mw)]
    x1, merged, h2 = _merge_fwd(x2, ys, zm, w_brs, w_out, gate_col, w["norm_ffn_g"])
    u = _matmul(h2, w_up, name="ffn_up")
    u3 = u.reshape(b_, t_, 2 * f)
    yff = _glu_fwd(u3, conv_w, cb).reshape(n, f)
    dy, (loss_vec,), _ = _matmul_rows([yff], w_down, name="ffn_down_loss", tb=False, row_ins=[x1, target.reshape(n, d)],
                                      vec_ins=[], epilogue=_loss_epilogue, n_vec_out=1)

    grads = {}

    def ridden(name, call):
        if not hooks or name not in hooks:
            return call(None)[0]
        rider, then = hooks[name](grads)
        outs, extra = call(rider)
        then(extra)
        return outs

    dyff = _matmul(dy, w_down, tb=True, name="ffn_down_dx")
    grads["ffn_w_down"] = _matmul(yff, dy, ta=True, name="ffn_down_dw", tm=1408)
    du_a, du_v, grads["ffn_conv_w"], grads["ffn_conv_b"] = _glu_bwd(u3, dyff.reshape(b_, t_, f), conv_w, cb)
    du_a, du_v = du_a.reshape(n, f), du_v.reshape(n, f)
    dx1, (grads["norm_ffn_g"],), _ = _matmul_rows(
        [du_a, du_v], w_up, name="ffn_up_dx", tb=True, row_ins=[x1, dy], vec_ins=[w["norm_ffn_g"]],
        epilogue=_norm_bwd_epilogue(0), n_vec_out=1)
    grads["ffn_w_up"] = _matmul(h2, None, ta=True, name="ffn_up_dw", b_parts=[du_a, du_v], tn=f // 2, stack_out=True)

    dz, dpa, dpb, dpc, dya, dyb, dyc = _merge_bwd(dx1, ys, zm, w_brs, w_out, gate_col)
    dz = dz.reshape(b_, t_, -1)
    grads["w_out"] = _matmul(merged, dx1, ta=True, name="out_proj_dw")
    for nm, y_, dp_ in zip(("w_br_hgrn", "w_br_fox", "w_br_mem"), ys, (dpa, dpb, dpc)):
        grads[nm] = _matmul(y_, dp_, ta=True, name=nm + "_dw")

    dz, dmk, dmv, grads["mem_q_norm_g"], grads["mem_k_norm_g"] = _mem_bwd(
        zm3, mkv, dyc.reshape(b_, t_, mw), w["mem_q_norm_g"], w["mem_k_norm_g"], mw, mem_col, dz)
    dmkv = jnp.concatenate([dmk, dmv], axis=-1).reshape(b_ * m_, 2 * mw)
    grads["mem_kv_w"] = _matmul(hm, dmkv, ta=True, name="mem_kv_dw")
    dhm = _matmul(dmkv, w_kv, tb=True, name="mem_kv_dx")
    _, grads["norm_mem_g"] = _rmsnorm_bwd(mem2, [dhm], w["norm_mem_g"], None, name="norm_mem_bwd")

    dz, dfc, g_fq, g_fk = ridden("fox_bwd", lambda rider: _fox_bwd(
        zm3, yb, dyb.reshape(b_, t_, fw), lse, fc, fox_gq, fox_gk, fw, fox_col, dz, rider))
    grads["fox_q_norm_g"] = g_fq[:, :FOX_DH] + g_fq[:, FOX_DH:]
    grads["fox_k_norm_g"] = g_fk[:, :FOX_DH] + g_fk[:, FOX_DH:]
    dzf, g_fb = _fox_post(dfc, zf3, f_bias)
    grads["fox_f_bias"] = g_fb[:, :FOX_HEADS]

    dz, grads["hgrn_lb_logits"], grads["hgrn_norm_g"] = ridden("hgrn_bwd", lambda rider: _hgrn_bwd(
        zm3, dya.reshape(b_, t_, hw), w["hgrn_lb_logits"], w["hgrn_norm_g"], hw, hg_col, dz, rider))
    dzm = dz.reshape(n, -1)
    dzf2 = dzf.reshape(n, LANE)
    g_main = _matmul(h, dzm, ta=True, name="in_proj_dw")
    g_ff = _matmul(h, dzf2, ta=True, name="in_proj_forget_dw")
    grads["w_in"] = _join_w_in(_reorder_blocks(g_main, inverse), g_ff[:, :FOX_HEADS], s1, FOX_HEADS)

    dh_b = _matmul(dzf2, w_ff, tb=True, name="in_proj_forget_dx")

    def in_proj_dx(rider):
        out = _matmul(dzm, w_main, tb=True, name="in_proj_dx", rider=rider)
        return ([out[0]], out[1]) if rider else ([out], None)

    dh_a, = ridden("in_proj_dx", in_proj_dx)
    grad_x, grads["norm_mix_g"] = _rmsnorm_bwd(x2, [dh_a, dh_b], w["norm_mix_g"], dx1, name="norm_mix_bwd")
    return loss_vec, grad_x.reshape(b_, t_, d), grads


def kernel(x, mem, norm_mix_g, norm_mem_g, w_in, hgrn_lb_logits, hgrn_norm_g, fox_f_bias, fox_q_norm_g, fox_k_norm_g, mem_kv_w, mem_q_norm_g, mem_k_norm_g, w_br_hgrn, w_br_fox, w_br_mem, w_out, norm_ffn_g, ffn_w_up, ffn_conv_w, ffn_conv_b, ffn_w_down, loss_target, m_norm_mix_g, m_norm_mem_g, m_w_in, m_hgrn_lb_logits, m_hgrn_norm_g, m_fox_f_bias, m_fox_q_norm_g, m_fox_k_norm_g, m_mem_kv_w, m_mem_q_norm_g, m_mem_k_norm_g, m_w_br_hgrn, m_w_br_fox, m_w_br_mem, m_w_out, m_norm_ffn_g, m_ffn_w_up, m_ffn_conv_w, m_ffn_conv_b, m_ffn_w_down, v_norm_mix_g, v_norm_mem_g, v_w_in, v_hgrn_lb_logits, v_hgrn_norm_g, v_fox_f_bias, v_fox_q_norm_g, v_fox_k_norm_g, v_mem_kv_w, v_mem_q_norm_g, v_mem_k_norm_g, v_w_br_hgrn, v_w_br_fox, v_w_br_mem, v_w_out, v_norm_ffn_g, v_ffn_w_up, v_ffn_conv_w, v_ffn_conv_b, v_ffn_w_down):
    w = dict(zip(WEIGHTS, (norm_mix_g, norm_mem_g, w_in, hgrn_lb_logits, hgrn_norm_g, fox_f_bias, fox_q_norm_g,
                           fox_k_norm_g, mem_kv_w, mem_q_norm_g, mem_k_norm_g, w_br_hgrn, w_br_fox, w_br_mem, w_out,
                           norm_ffn_g, ffn_w_up, ffn_conv_w, ffn_conv_b, ffn_w_down)))
    m = dict(zip(WEIGHTS, (m_norm_mix_g, m_norm_mem_g, m_w_in, m_hgrn_lb_logits, m_hgrn_norm_g, m_fox_f_bias,
                           m_fox_q_norm_g, m_fox_k_norm_g, m_mem_kv_w, m_mem_q_norm_g, m_mem_k_norm_g, m_w_br_hgrn,
                           m_w_br_fox, m_w_br_mem, m_w_out, m_norm_ffn_g, m_ffn_w_up, m_ffn_conv_w, m_ffn_conv_b,
                           m_ffn_w_down)))
    v = dict(zip(WEIGHTS, (v_norm_mix_g, v_norm_mem_g, v_w_in, v_hgrn_lb_logits, v_hgrn_norm_g, v_fox_f_bias,
                           v_fox_q_norm_g, v_fox_k_norm_g, v_mem_kv_w, v_mem_q_norm_g, v_mem_k_norm_g, v_w_br_hgrn,
                           v_w_br_fox, v_w_br_mem, v_w_out, v_norm_ffn_g, v_ffn_w_up, v_ffn_conv_w, v_ffn_conv_b,
                           v_ffn_w_down)))
    c_idx = lax.axis_index("c")
    chip = 2 * lax.axis_index("x") + lax.axis_index("y")

    mine = {nm: w[nm][0].astype(MXU_DTYPE) for nm in BIG}

    def gathered_full(names, arrays):
        out = {nm: _put_shard(g, KIND[nm], chip, mine[nm]) for nm, g in zip(names, arrays)}
        return {nm: g.reshape(-1, g.shape[2]) if nm in ROW_SHARDED else g for nm, g in out.items()}

    full = gathered_full(FIRST, _gather_shards([mine[nm] for nm in FIRST], [KIND[nm] for nm in FIRST]))
    late = {host: ([mine[nm] for nm in names], [KIND[nm] for nm in names],
                   functools.partial(gathered_full, names)) for host, names in LATE.items()}
    cs = ffn_conv_w.shape[2]
    f = cs * N_CHIPS
    placed = lax.dynamic_update_slice(jnp.zeros((3, f), F32), ffn_conv_w[0] * (c_idx == 0).astype(F32), (0, chip * cs))
    conv_w = _unpack_small(_all_sum_small(_pack_small([placed]), "gather_conv_w"), [(3, f)])[0]

    c_arr = jnp.reshape(c_idx, (1,)).astype(jnp.int32)

    def stacked(nm, g):
        return g.reshape(N_CHIPS, -1, g.shape[1]) if nm in ROW_SHARDED else g

    def with_own(landed, partial, kinds):
        return [_put_shard(bq, STACK, chip, _take_shard(p, k, chip)) for bq, p, k in zip(landed, partial, kinds)]

    kinds_rest, kinds_last = [KIND[nm] for nm in REST], [KIND[nm] for nm in LAST]
    state = {}

    def swap_rest(grads):
        gs = [stacked(nm, grads[nm]) for nm in REST]

        def then(from_sibling):
            state["partial_rest"] = [_add_half(g, a, k, c_arr, "add_half_" + nm)
                                     for g, a, k, nm in zip(gs, from_sibling, kinds_rest, REST)]

        return _swap_rider(gs, kinds_rest), then

    def exchange_rest(grads):
        def then(landed):
            state["landed_rest"] = with_own(landed, state["partial_rest"], kinds_rest)

        return _exchange_rider(state["partial_rest"], kinds_rest), then

    def exchange_last(grads):
        gs = [stacked(nm, grads[nm]) for nm in LAST]
        from_sibling = _pair_swap_halves(gs, kinds_last, "pair_swap_halves_last")
        partial = [_add_half(g, a, k, c_arr, "add_half_" + nm) for g, a, k, nm in zip(gs, from_sibling, kinds_last, LAST)]

        def then(landed):
            state["landed_last"] = with_own(landed, partial, kinds_last)

        return _exchange_rider(partial, kinds_last), then

    hooks = {"fox_bwd": swap_rest, "hgrn_bwd": exchange_rest, "in_proj_dx": exchange_last}

    loss_vec, grad_x, grads = _local_step(x, mem, loss_target, w, full, conv_w, late, hooks)

    landed = dict(zip(LAST + REST, state["landed_last"] + state["landed_rest"]))
    reduced_half = [_sum_chips(landed[nm], "sum_chips_" + nm) for nm in BIG]
    joined = [lax.dynamic_update_slice(o, q[None], (c_idx, 0, 0)).reshape(2 * q.shape[0], q.shape[1])
              for o, q in zip(_pair_join_halves(reduced_half), reduced_half)]
    gshards = dict(zip(BIG, joined))

    small_shapes = [w[nm].shape for nm in SMALL] + [grads["ffn_conv_w"].shape, loss_vec.shape]
    summed = _all_sum_small(_pack_small([grads[nm] for nm in SMALL] + [grads["ffn_conv_w"], loss_vec]),
                            "all_sum_small_grads")
    g_small, d_small, m_small, v_small, (g_conv_w, loss_row) = _adamw_small(
        summed, small_shapes, [w[nm] for nm in SMALL], [m[nm] for nm in SMALL], [v[nm] for nm in SMALL])
    loss = jnp.sum(loss_row)
    g_out = {nm: gshards[nm][None] for nm in BIG}
    g_out["ffn_conv_w"] = lax.dynamic_slice(g_conv_w, (0, chip * cs), (3, cs))[None]
    delta, new_m, new_v = dict(zip(SMALL, d_small)), dict(zip(SMALL, m_small)), dict(zip(SMALL, v_small))
    g_out.update(zip(SMALL, g_small))
    for nm in BIG + ("ffn_conv_w",):
        operands = (w[nm], g_out[nm], m[nm], v[nm])
        if nm in TRANSPOSED:
            _, r, c = w[nm].shape
            outs = _adamw(*[jnp.swapaxes(a, 1, 2).reshape(1, -1, LANE) for a in operands], name="adamw_" + nm, tr=4096)
            outs = [jnp.swapaxes(o.reshape(1, c, r), 1, 2) for o in outs]
        else:
            outs = _adamw(*operands, name="adamw_" + nm)
        delta[nm], new_m[nm], new_v[nm] = outs

    return (loss, grad_x, *[g_out[nm] for nm in WEIGHTS], *[delta[nm] for nm in WEIGHTS],
            *[new_m[nm] for nm in WEIGHTS], *[new_v[nm] for nm in WEIGHTS])
```

```python
import functools
import math

import jax
import jax.numpy as jnp
from jax import lax
from jax.experimental import pallas as pl
from jax.experimental.pallas import tpu as pltpu

F32 = jnp.float32
BF16 = jnp.bfloat16
MXU_DTYPE = jnp.bfloat16
EXCHANGE_DTYPE = jnp.bfloat16

EPS = 1e-6
HG_HEADS, HG_D = 4, 128
FOX_HEADS, FOX_DH = 8, 64
MEM_HEADS, MEM_DH = 4, 128
MEM_TILE = 2048
HG_CHUNK = 64
FOX_BLOCK = 256
LANE = 128
FFN_GROUP = 256
FLAT_W = 1024
VMEM_LIMIT = 56 * 2 ** 20
NEG = -1e30
N_CHIPS = 4

ADAM_LR, ADAM_B1, ADAM_B2, ADAM_EPS, ADAM_WD, ADAM_STEP = 0.001, 0.9, 0.999, 1e-08, 0.01, 10

MESH = pl.DeviceIdType.MESH
ANY = pl.BlockSpec(memory_space=pl.ANY)


def _mx(x):
    return x.astype(MXU_DTYPE)


def _dot(a, b, ca, cb):
    return lax.dot_general(_mx(a), _mx(b), (((ca,), (cb,)), ((), ())), preferred_element_type=F32)


def _nn(a, b):
    return _dot(a, b, 1, 0)


def _nt(a, b):
    return _dot(a, b, 1, 1)


def _tn(a, b):
    return _dot(a, b, 0, 0)


def _tri_dot(tri_bf, x):
    hi = x.astype(BF16)
    r = x - hi.astype(F32)
    mid = r.astype(BF16)
    lo = (r - mid.astype(F32)).astype(BF16)

    def d(v):
        return lax.dot_general(tri_bf, v, (((1,), (0,)), ((), ())), preferred_element_type=F32)

    return d(hi) + d(mid) + d(lo)


def _sig(x):
    return jax.nn.sigmoid(x)


def _tile(dim, pref, unit=LANE):
    if dim <= pref:
        return dim
    t = pref - pref % unit
    while t >= unit:
        if dim % t == 0:
            return t
        t -= unit
    return dim


def _params(n_grid):
    return pltpu.CompilerParams(dimension_semantics=("arbitrary",) * n_grid, vmem_limit_bytes=VMEM_LIMIT)


def _acc(ref, val, first):
    @pl.when(first)
    def _():
        ref[...] = val

    @pl.when(jnp.logical_not(first))
    def _():
        ref[...] += val


class _Rider:
    def __init__(self, inputs, out_shapes, scratch, start, finish):
        self.inputs, self.out_shapes, self.scratch, self.start, self.finish = inputs, out_shapes, scratch, start, finish


def _ride(body, rider, n_in, n_out, grid):
    if rider is None:
        return body
    ri, ro, rs = len(rider.inputs), len(rider.out_shapes), len(rider.scratch)

    def wrapped(*refs):
        a, b, c = n_in + ri, n_in + ri + n_out, n_in + ri + n_out + ro
        base = refs[:n_in] + refs[a:b] + refs[c:len(refs) - rs]
        r_in, r_out, r_scr = refs[n_in:a], refs[b:c], refs[len(refs) - rs:]
        step = pl.program_id(0)
        for ax in range(1, len(grid)):
            step = step * grid[ax] + pl.program_id(ax)

        @pl.when(step == 0)
        def _():
            rider.start(r_in, r_out, r_scr)

        body(*base)

        @pl.when(step == math.prod(grid) - 1)
        def _():
            rider.finish(r_in, r_out, r_scr)

    return wrapped


def _ride_call(body, rider, *, name, grid, in_specs, out_specs, out_shape, scratch, args, aliases=None):
    n_in, n_out = len(in_specs), len(out_specs)
    aliases = aliases or {}
    if rider is None:
        outs = pl.pallas_call(body, name=name, grid=grid, in_specs=in_specs, out_specs=out_specs, out_shape=out_shape,
                              scratch_shapes=scratch, input_output_aliases=aliases,
                              compiler_params=_params(len(grid)))(*args)
        return list(outs), None
    outs = pl.pallas_call(
        _ride(body, rider, n_in, n_out, grid), name=name, grid=grid,
        in_specs=list(in_specs) + [ANY] * len(rider.inputs), out_specs=list(out_specs) + [ANY] * len(rider.out_shapes),
        out_shape=list(out_shape) + list(rider.out_shapes), scratch_shapes=list(scratch) + list(rider.scratch),
        input_output_aliases=aliases, compiler_params=_params(len(grid)),
    )(*args, *rider.inputs)
    return list(outs[:n_out]), list(outs[n_out:])


def _matmul(a, b, *, name, ta=False, tb=False, tm=1024, tn=2048, tk=None, rider=None, b_parts=None, stack_out=False):
    m, k = (a.shape[1], a.shape[0]) if ta else a.shape
    tk = tk or (1024 if ta else 2048)
    stacked_b = b is not None and b.ndim == 3
    if b_parts:
        n, tn = 2 * b_parts[0].shape[1], _tile(b_parts[0].shape[1], tn)
    elif stacked_b:
        n, tn = b.shape[0] * b.shape[2], b.shape[2]
    else:
        n = b.shape[0] if tb else b.shape[1]
        tn = _tile(n, tn)
    tm, tk = _tile(m, tm), _tile(k, tk)
    nk, nj = k // tk, n // tn

    def body(a_ref, *refs):
        o_ref = refs[-1]
        if b_parts:
            bv = jnp.where(pl.program_id(1) < nj // 2, refs[0][...], refs[1][...])
        else:
            bv = refs[0][...]
        p = _dot(a_ref[...], bv, 0 if ta else 1, 1 if tb else 0)
        if nk == 1:
            o_ref[...] = p
        else:
            _acc(o_ref, p, pl.program_id(2) == 0)

    a_spec = pl.BlockSpec((tk, tm), lambda i, j, kk: (kk, i)) if ta else pl.BlockSpec((tm, tk), lambda i, j, kk: (i, kk))
    if b_parts:
        half = nj // 2
        b_specs = [pl.BlockSpec((tk, tn), lambda i, j, kk: (kk, jnp.minimum(j, half - 1))),
                   pl.BlockSpec((tk, tn), lambda i, j, kk: (kk, jnp.maximum(j - half, 0)))]
        b_args = list(b_parts)
    elif stacked_b:
        b_specs, b_args = [pl.BlockSpec((None, tk, tn), lambda i, j, kk: (j, kk, 0))], [b]
    else:
        b_specs = [pl.BlockSpec((tn, tk), lambda i, j, kk: (j, kk)) if tb else pl.BlockSpec((tk, tn), lambda i, j, kk: (kk, j))]
        b_args = [b]
    if stack_out:
        o_spec, o_sds = pl.BlockSpec((None, tm, tn), lambda i, j, kk: (j, i, 0)), jax.ShapeDtypeStruct((nj, m, tn), F32)
    else:
        o_spec, o_sds = pl.BlockSpec((tm, tn), lambda i, j, kk: (i, j)), jax.ShapeDtypeStruct((m, n), F32)
    outs, extra = _ride_call(body, rider, name=name, grid=(m // tm, nj, nk), in_specs=[a_spec] + b_specs,
                             out_specs=[o_spec], out_shape=[o_sds], scratch=[], args=(a, *b_args))
    return (outs[0], extra) if rider else outs[0]


def _matmul_rows(a_parts, b, *, name, tb, row_ins, vec_ins, epilogue, n_vec_out, tm=512, tk=2048, rider=None):
    m, kp = a_parts[0].shape
    stacked_b = b.ndim == 3
    n = b.shape[1] if stacked_b else (b.shape[0] if tb else b.shape[1])
    tm, tk = _tile(m, tm, 8), (b.shape[2] if stacked_b else _tile(kp, tk))
    nk = kp // tk
    n_a, n_row, n_vec = len(a_parts), len(row_ins), len(vec_ins)

    def body(*refs):
        a_refs, b_refs = refs[:n_a], refs[n_a:2 * n_a]
        rows = refs[2 * n_a:2 * n_a + n_row]
        vecs = refs[2 * n_a + n_row:2 * n_a + n_row + n_vec]
        o_ref = refs[2 * n_a + n_row + n_vec]
        v_refs = refs[2 * n_a + n_row + n_vec + 1:-1]
        acc_ref = refs[-1]
        i, kk = pl.program_id(0), pl.program_id(1)
        p = _dot(a_refs[0][...], b_refs[0][...], 1, 1 if tb else 0)
        for a_ref, b_ref in zip(a_refs[1:], b_refs[1:]):
            p = p + _dot(a_ref[...], b_ref[...], 1, 1 if tb else 0)
        _acc(acc_ref, p, kk == 0)

        @pl.when(kk == nk - 1)
        def _():
            out, vouts = epilogue(acc_ref[...], *[r[...] for r in rows], *[v[...] for v in vecs])
            o_ref[...] = out
            for v_ref, v in zip(v_refs, vouts):
                _acc(v_ref, v, i == 0)

    a_spec = pl.BlockSpec((tm, tk), lambda i, kk: (i, kk))
    if stacked_b:
        b_specs = [pl.BlockSpec((None, n, tk), functools.partial(lambda i, kk, q: (q * nk + kk, 0, 0), q=q))
                   for q in range(n_a)]
    else:
        b_specs = [pl.BlockSpec((n, tk), functools.partial(lambda i, kk, q: (0, q * nk + kk), q=q)) if tb else
                   pl.BlockSpec((tk, n), functools.partial(lambda i, kk, q: (q * nk + kk, 0), q=q)) for q in range(n_a)]
    row = pl.BlockSpec((tm, n), lambda i, kk: (i, 0))
    vec = pl.BlockSpec((1, n), lambda i, kk: (0, 0))
    outs, extra = _ride_call(
        body, rider, name=name, grid=(m // tm, nk),
        in_specs=[a_spec] * n_a + b_specs + [row] * n_row + [vec] * n_vec,
        out_specs=[row] + [vec] * n_vec_out,
        out_shape=[jax.ShapeDtypeStruct((m, n), F32)] + [jax.ShapeDtypeStruct((1, n), F32)] * n_vec_out,
        scratch=[pltpu.VMEM((tm, n), F32)], args=(*a_parts, *([b] * n_a), *row_ins, *vec_ins))
    return outs[0], outs[1:], extra


def _norm_bwd_epilogue(n_dh):
    def epilogue(dh, x, res, *rest):
        for extra in rest[:n_dh]:
            dh = dh + extra
        g = rest[n_dh]
        r = lax.rsqrt(jnp.mean(x * x, axis=-1, keepdims=True) + EPS)
        dhg = dh * g
        dx = res + r * dhg - x * (r * r * r) * jnp.mean(dhg * x, axis=-1, keepdims=True)
        return dx, [jnp.sum(dh * x * r, axis=0, keepdims=True)]

    return epilogue


def _loss_epilogue(y, x1, target):
    d = y.shape[1]
    err = x1 + y - target
    return err * (1.0 / d), [jnp.sum(err * err, axis=0, keepdims=True) * (0.5 / d)]


def _rmsnorm_fwd(x, g, *, name, tm=512):
    n, d = x.shape
    tm = _tile(n, tm, 8)

    def body(x_ref, g_ref, o_ref):
        xv = x_ref[...]
        r = lax.rsqrt(jnp.mean(xv * xv, axis=-1, keepdims=True) + EPS)
        o_ref[...] = (xv * r * g_ref[...]).astype(o_ref.dtype)

    return pl.pallas_call(
        body, name=name, grid=(n // tm,),
        in_specs=[pl.BlockSpec((tm, d), lambda i: (i, 0)), pl.BlockSpec((1, d), lambda i: (0, 0))],
        out_specs=pl.BlockSpec((tm, d), lambda i: (i, 0)),
        out_shape=jax.ShapeDtypeStruct((n, d), MXU_DTYPE),
        compiler_params=_params(1),
    )(x, g)


def _rmsnorm_bwd(x, dhs, g, res, *, name, tm=512):
    n, d = x.shape
    tm = _tile(n, tm, 8)
    n_dh = len(dhs)
    has_res = res is not None

    def body(*refs):
        x_ref, dh_refs, g_ref = refs[0], refs[1:1 + n_dh], refs[1 + n_dh]
        res_ref = refs[2 + n_dh] if has_res else None
        dx_ref, dg_ref = refs[-2], refs[-1]
        xv = x_ref[...]
        dh = dh_refs[0][...].astype(F32)
        for r_ in dh_refs[1:]:
            dh = dh + r_[...].astype(F32)
        r = lax.rsqrt(jnp.mean(xv * xv, axis=-1, keepdims=True) + EPS)
        dhg = dh * g_ref[...]
        dx = r * dhg - xv * (r * r * r) * jnp.mean(dhg * xv, axis=-1, keepdims=True)
        if has_res:
            dx = dx + res_ref[...]
        dx_ref[...] = dx
        _acc(dg_ref, jnp.sum(dh * xv * r, axis=0, keepdims=True), pl.program_id(0) == 0)

    row = pl.BlockSpec((tm, d), lambda i: (i, 0))
    vec = pl.BlockSpec((1, d), lambda i: (0, 0))
    ins = [x] + list(dhs) + [g] + ([res] if has_res else [])
    return pl.pallas_call(
        body, name=name, grid=(n // tm,),
        in_specs=[row] * (1 + n_dh) + [vec] + ([row] if has_res else []),
        out_specs=[row, vec],
        out_shape=[jax.ShapeDtypeStruct((n, d), F32), jax.ShapeDtypeStruct((1, d), F32)],
        compiler_params=_params(1),
    )(*ins)


def _adamw(w, g, m, v, *, name, tr=256):
    _, r, c = w.shape
    c1 = 1.0 / (1.0 - ADAM_B1 ** ADAM_STEP)
    c2 = 1.0 / (1.0 - ADAM_B2 ** ADAM_STEP)

    def body(w_ref, g_ref, m_ref, v_ref, d_ref, mo_ref, vo_ref):
        gv = g_ref[...]
        mn = ADAM_B1 * m_ref[...] + (1.0 - ADAM_B1) * gv
        vn = ADAM_B2 * v_ref[...] + (1.0 - ADAM_B2) * (gv * gv)
        d_ref[...] = -ADAM_LR * ((mn * c1) / (jnp.sqrt(vn * c2) + ADAM_EPS) + ADAM_WD * w_ref[...])
        mo_ref[...] = mn
        vo_ref[...] = vn

    if r % 8 == 0 or r < 8:
        tr = _tile(r, tr, 8)
        grid, blk = (r // tr,), pl.BlockSpec((1, tr, c), lambda i: (0, i, 0))
    else:
        tc = _tile(c, tr)
        grid, blk = (c // tc,), pl.BlockSpec((1, r, tc), lambda i: (0, 0, i))
    sds = jax.ShapeDtypeStruct((1, r, c), F32)
    return pl.pallas_call(
        body, name=name, grid=grid, in_specs=[blk] * 4, out_specs=[blk] * 3, out_shape=[sds] * 3,
        compiler_params=_params(1),
    )(w, g, m, v)


def _bdot(a, b, ca, cb):
    return lax.dot_general(_mx(a), _mx(b), (((ca,), (cb,)), ((0,), (0,))), preferred_element_type=F32)


def _split2(x):
    hi = x.astype(BF16)
    return hi, (x - hi.astype(F32)).astype(BF16)


def _bdotp(a, b, ca, cb):
    def d(u, v):
        return lax.dot_general(u, v, (((ca,), (cb,)), ((0,), (0,))), preferred_element_type=F32)

    return d(a[0], b[0]) + d(a[0], b[1]) + d(a[1], b[0])


def _tri_dot_b(tri_bf, x):
    hi = x.astype(BF16)
    r = x - hi.astype(F32)
    mid = r.astype(BF16)
    lo = (r - mid.astype(F32)).astype(BF16)

    def d(v):
        return lax.dot_general(tri_bf, v, (((2,), (1,)), ((0,), (0,))), preferred_element_type=F32)

    return d(hi) + d(mid) + d(lo)


def _hgrn_forward(hq, hf, hi, lbv, tril, tril_bf):
    nc, c, _ = hq.shape
    sf = _sig(hf)
    f = lbv + (1.0 - lbv) * sf
    k = 1.0 - f
    gcum = _tri_dot_b(tril_bf, jnp.log(f))
    mid = gcum[:, c // 2 - 1:c // 2, :]
    glast = gcum[:, c - 1:c, :]
    sq = _sig(hq)
    q = hq * sq
    e_q = jnp.exp(gcum - mid)
    e_k = jnp.exp(mid - gcum)
    qe, ke = q * e_q, k * e_k
    a = jnp.where(tril, _bdot(qe, ke, 2, 2), 0.0)
    e_g = jnp.exp(gcum)
    qg = q * e_g
    e_s = jnp.exp(glast - gcum)
    kg = k * e_s
    e_l = jnp.exp(glast)
    upd = _bdot(hi, kg, 1, 1)
    st = jnp.zeros((HG_D, HG_D), F32)
    states = []
    for n in range(nc):
        states.append(st)
        st = st * e_l[n] + upd[n]
    st_all = jnp.stack(states)
    o = _bdot(a, hi, 2, 1) + _bdot(qg, st_all, 2, 2)
    return dict(sf=sf, f=f, k=k, sq=sq, q=q, e_q=e_q, e_k=e_k, qe=qe, ke=ke, a=a, e_g=e_g, qg=qg, o=o,
                e_s=e_s, kg=kg, e_l=e_l, st_all=st_all)


def _hgrn_specs(t_, col0):
    def col(off):
        return pl.BlockSpec((1, t_, LANE), lambda h, b: (b, 0, col0 + 4 * h + off))

    vec = pl.BlockSpec((2, LANE), lambda h, b: (0, h))
    one = pl.BlockSpec((1, LANE), lambda h, b: (0, 0))
    blk = pl.BlockSpec((1, t_, LANE), lambda h, b: (b, 0, h))
    return col, vec, one, blk


def _chunk_masks(nc, c):
    row = lax.broadcasted_iota(jnp.int32, (nc, c, c), 1)
    cl = lax.broadcasted_iota(jnp.int32, (nc, c, c), 2)
    return row >= cl, (row >= cl).astype(BF16), (row <= cl).astype(BF16)


def _hgrn_fwd(zm, lb, gn, hw, col0):
    b_, t_, _ = zm.shape
    c = min(HG_CHUNK, t_)
    nc = t_ // c
    col, vec, one, blk = _hgrn_specs(t_, col0)

    def body(q_ref, f_ref, i_ref, g_ref, lb_ref, gn_ref, y_ref):
        lbv, gnv = _sig(lb_ref[0:1, :] - lb_ref[1:2, :]), gn_ref[...]
        tril, tril_bf, _ = _chunk_masks(nc, c)
        chunks = lambda ref: ref[0].reshape(nc, c, LANE)
        o = _hgrn_forward(chunks(q_ref), chunks(f_ref), chunks(i_ref), lbv, tril, tril_bf)["o"]
        r = lax.rsqrt(jnp.mean(o * o, axis=-1, keepdims=True) + EPS)
        hg = chunks(g_ref)
        y_ref[0] = (o * r * gnv * (hg * _sig(hg))).reshape(t_, LANE)

    return pl.pallas_call(
        body, name="hgrn_fwd", grid=(HG_HEADS, b_),
        in_specs=[col(0), col(1), col(2), col(3), vec, one], out_specs=blk,
        out_shape=jax.ShapeDtypeStruct((b_, t_, hw), F32),
        compiler_params=_params(2),
    )(zm, zm, zm, zm, lb, gn)


def _hgrn_bwd(zm, dy, lb, gn, hw, col0, dz, rider=None):
    b_, t_, _ = zm.shape
    c = min(HG_CHUNK, t_)
    nc = t_ // c
    col, vec, one, blk = _hgrn_specs(t_, col0)

    def body(q_ref, f_ref, i_ref, g_ref, dy_ref, lb_ref, gn_ref, _, dz_ref, dlb_ref, dgn_ref):
        h, b = pl.program_id(0), pl.program_id(1)
        lbv, gnv = _sig(lb_ref[0:1, :] - lb_ref[1:2, :]), gn_ref[...]
        tril, tril_bf, triu_bf = _chunk_masks(nc, c)
        last_row = lax.broadcasted_iota(jnp.int32, (nc, c, LANE), 1) == c - 1
        chunks = lambda ref: ref[0].reshape(nc, c, LANE)
        flat = lambda x: x.reshape(t_, LANE)
        hq, hi, hg = chunks(q_ref), chunks(i_ref), chunks(g_ref)
        p = _hgrn_forward(hq, chunks(f_ref), hi, lbv, tril, tril_bf)
        o, q, k, st_all, e_l = p["o"], p["q"], p["k"], p["st_all"], p["e_l"]
        dyv = chunks(dy_ref)
        sg = _sig(hg)
        r = lax.rsqrt(jnp.mean(o * o, axis=-1, keepdims=True) + EPS)
        dn = dyv * (hg * sg)
        dz_ref[0, :, 3 * LANE:] = flat(dyv * (o * r * gnv) * (sg * (1.0 + hg * (1.0 - sg)))).astype(dz_ref.dtype)
        dgn = jnp.sum(flat(dn * o * r), axis=0, keepdims=True)
        dng = dn * gnv
        do = r * dng - o * (r * r * r) * jnp.mean(dng * o, axis=-1, keepdims=True)
        do2, hi2, qg2, ke2, qe2, st2 = (_split2(t) for t in (do, hi, p["qg"], p["ke"], p["qe"], st_all))
        back = _bdotp(do2, qg2, 1, 1)
        dst = jnp.zeros((HG_D, HG_D), F32)
        dsts = [None] * nc
        for n in range(nc - 1, -1, -1):
            dsts[n] = dst
            dst = dst * e_l[n] + back[n]
        dst_all = jnp.stack(dsts)
        da = jnp.where(tril, _bdotp(do2, hi2, 2, 2), 0.0)
        da2 = _split2(da)
        dq = _bdotp(da2, ke2, 2, 1) * p["e_q"] + _bdotp(do2, st2, 2, 1) * p["e_g"]
        dk_state = _bdotp(hi2, _split2(dst_all), 2, 1) * p["e_s"]
        dk = _bdotp(da2, qe2, 1, 1) * p["e_k"] + dk_state
        dz_ref[0, :, 2 * LANE:3 * LANE] = flat(_bdot(p["a"], do, 1, 1) + _bdot(p["kg"], dst_all, 2, 2)).astype(dz_ref.dtype)
        extra = (jnp.sum(k * dk_state, axis=1, keepdims=True) + e_l * jnp.sum(st_all * dst_all, axis=1, keepdims=True))
        dgc = q * dq - k * dk + jnp.where(last_row, extra, 0.0)
        dfv = _tri_dot_b(triu_bf, dgc) / p["f"] - dk
        sf, sq = p["sf"], p["sq"]
        dz_ref[0, :, LANE:2 * LANE] = flat(dfv * (1.0 - lbv) * sf * (1.0 - sf)).astype(dz_ref.dtype)
        dlb = jnp.sum(flat(dfv * (1.0 - sf)), axis=0, keepdims=True)
        dz_ref[0, :, :LANE] = flat(dq * (sq * (1.0 + hq * (1.0 - sq)))).astype(dz_ref.dtype)
        dl0 = dlb * lbv * (1.0 - lbv)
        _acc(dlb_ref, jnp.concatenate([dl0, -dl0], axis=0), b == 0)
        _acc(dgn_ref, dgn, jnp.logical_and(b == 0, h == 0))

    return _ride_call(
        body, rider, name="hgrn_bwd", grid=(HG_HEADS, b_),
        in_specs=[col(0), col(1), col(2), col(3), blk, vec, one, ANY],
        out_specs=[pl.BlockSpec((1, t_, 4 * LANE), lambda h, b: (b, 0, col0 // 4 + h)), vec, one],
        out_shape=[jax.ShapeDtypeStruct(dz.shape, dz.dtype), jax.ShapeDtypeStruct((2, hw), F32),
                   jax.ShapeDtypeStruct((1, LANE), F32)],
        scratch=[], args=(zm, zm, zm, zm, dy, lb, gn, dz), aliases={7: 0})


def _fox_logf(x):
    return jnp.minimum(x, 0.0) - jnp.log(1.0 + jnp.exp(-jnp.abs(x)))


def _fox_prep(zf, bias):
    b_, t_, _ = zf.shape
    tb = min(FOX_BLOCK, t_)
    nb = t_ // tb

    def body(z_ref, b_ref, fc_ref):
        tril_bf = (lax.broadcasted_iota(jnp.int32, (tb, tb), 0) >= lax.broadcasted_iota(jnp.int32, (tb, tb), 1)).astype(BF16)
        bv = b_ref[...]

        def blk(i, carry):
            rows = pl.ds(pl.multiple_of(i * tb, tb), tb)
            fc = _tri_dot(tril_bf, _fox_logf(z_ref[0, rows, :] + bv)) + carry
            fc_ref[0, rows, :] = fc
            return fc[tb - 1:tb, :]

        lax.fori_loop(0, nb, blk, jnp.zeros((1, LANE), F32))

    blk_spec = pl.BlockSpec((1, t_, LANE), lambda b: (b, 0, 0))
    return pl.pallas_call(
        body, name="fox_prep", grid=(b_,),
        in_specs=[blk_spec, pl.BlockSpec((1, LANE), lambda b: (0, 0))], out_specs=blk_spec,
        out_shape=jax.ShapeDtypeStruct((b_, t_, LANE), F32), compiler_params=_params(1),
    )(zf, bias)


def _fox_post(dfc, zf, bias):
    b_, t_, _ = zf.shape
    npair = dfc.shape[1]
    tb = min(FOX_BLOCK, t_)
    nb = t_ // tb

    def body(d_ref, z_ref, b_ref, dz_ref, db_ref):
        triu_bf = (lax.broadcasted_iota(jnp.int32, (tb, tb), 0) <= lax.broadcasted_iota(jnp.int32, (tb, tb), 1)).astype(BF16)
        valid = lax.broadcasted_iota(jnp.int32, (tb, LANE), 1) < FOX_HEADS
        bv = b_ref[...]

        def blk(m, carry):
            tail, db = carry
            rows = pl.ds(pl.multiple_of((nb - 1 - m) * tb, tb), tb)
            dfc_rows = d_ref[0, 0, rows, :]
            for p in range(1, npair):
                dfc_rows = dfc_rows + pltpu.roll(d_ref[0, p, rows, :], 2 * p, 1)
            dlf = _tri_dot(triu_bf, dfc_rows) + tail
            dx = jnp.where(valid, dlf * _sig(-(z_ref[0, rows, :] + bv)), 0.0)
            dz_ref[0, rows, :] = dx.astype(dz_ref.dtype)
            return dlf[0:1, :], db + jnp.sum(dx, axis=0, keepdims=True)

        z1 = jnp.zeros((1, LANE), F32)
        _, db = lax.fori_loop(0, nb, blk, (z1, z1))
        _acc(db_ref, db, pl.program_id(0) == 0)

    blk_spec = pl.BlockSpec((1, t_, LANE), lambda b: (b, 0, 0))
    vec = pl.BlockSpec((1, LANE), lambda b: (0, 0))
    return pl.pallas_call(
        body, name="fox_post", grid=(b_,),
        in_specs=[pl.BlockSpec((1, npair, t_, LANE), lambda b: (b, 0, 0, 0)), blk_spec, vec], out_specs=[blk_spec, vec],
        out_shape=[jax.ShapeDtypeStruct((b_, t_, LANE), MXU_DTYPE), jax.ShapeDtypeStruct((1, LANE), F32)],
        compiler_params=_params(1),
    )(dfc, zf, bias)


FOX_TILE = 256
FOX_TILE_FWD = 512
FOX_BAND = 512
AUG = 64


def _head_mean_matrix():
    r = lax.broadcasted_iota(jnp.int32, (LANE, LANE), 0) // FOX_DH
    c = lax.broadcasted_iota(jnp.int32, (LANE, LANE), 1) // FOX_DH
    return (r == c).astype(BF16)


def _dot_right_exact(x, m_bf):
    hi = x.astype(BF16)
    lo = (x - hi.astype(F32)).astype(BF16)

    def d(v):
        return lax.dot_general(v, m_bf, (((1,), (0,)), ((), ())), preferred_element_type=F32)

    return d(hi) + d(lo)


def _pair_norm(x, g2, bd):
    r = lax.rsqrt(_dot_right_exact(x * x, bd) * (1.0 / FOX_DH) + EPS)
    return x * r * g2, r


def _pair_norm_bwd(x, r, dy, g2, bd):
    dyg = dy * g2
    dx = r * dyg - x * (r * r * r) * (_dot_right_exact(dyg * x, bd) * (1.0 / FOX_DH))
    return dx, jnp.sum(dy * x * r, axis=0, keepdims=True)


def _head_lanes(xn, hh):
    return xn if hh == 0 else pltpu.roll(xn, FOX_DH, 1)


def _split3(x):
    hi = x.astype(BF16).astype(F32)
    mid = (x - hi).astype(BF16).astype(F32)
    return hi, mid, x - hi - mid


def _fox_operands(q_ref, k_ref, v_ref, fc_ref, gq2, gk2, p, qa, ka, va):
    t_ = q_ref.shape[1]
    bd = _head_mean_matrix()
    lane = lax.broadcasted_iota(jnp.int32, (t_, LANE), 1)
    qx, kx = q_ref[0], k_ref[0]
    qn, rq = _pair_norm(qx, gq2, bd)
    kn, rk = _pair_norm(kx, gk2, bd)
    vv = v_ref[0]
    q_aug = jnp.where(jnp.logical_and(lane >= AUG, lane < AUG + 3), 1.0, 0.0)
    for hh in range(2):
        fcol = jnp.sum(jnp.where(lane == 2 * p + hh, fc_ref[0], 0.0), axis=-1, keepdims=True)
        hi, mid, lo = _split3(-fcol)
        k_aug = jnp.where(lane == AUG, hi, jnp.where(lane == AUG + 1, mid, jnp.where(lane == AUG + 2, lo,
                          jnp.where(lane == AUG + 3, 1.0, 0.0))))
        head = lane < FOX_DH
        qa[hh] = jnp.where(head, _head_lanes(qn, hh), q_aug).astype(MXU_DTYPE)
        ka[hh] = jnp.where(head, _head_lanes(kn, hh), k_aug).astype(MXU_DTYPE)
        va[hh] = jnp.where(head, _head_lanes(vv, hh), 0.0).astype(MXU_DTYPE)
    return bd, lane, qx, kx, rq, rk


def _fox_specs(t_, fw, col0):
    npair = fw // LANE

    def col(off):
        return pl.BlockSpec((1, t_, LANE), lambda b, p: (b, 0, col0 + 3 * p + off))

    pair = pl.BlockSpec((1, t_, LANE), lambda b, p: (b, 0, p))
    full = pl.BlockSpec((1, t_, LANE), lambda b, p: (b, 0, 0))
    gvec = pl.BlockSpec((1, LANE), lambda b, p: (0, 0))
    lse = pl.BlockSpec((1, 1, t_, LANE), lambda b, p: (b, p, 0, 0))
    return col, pair, full, gvec, lse


def _fox_fwd(zm, fc, gq2, gk2, fw, col0, rider=None):
    b_, t_, _ = zm.shape
    npair = fw // LANE
    tq = min(FOX_TILE_FWD, t_)
    bw = min(FOX_BAND, t_)
    nband, tpb = t_ // bw, bw // tq
    scale = FOX_DH ** -0.5
    col, pair, full, gvec, lse_spec = _fox_specs(t_, fw, col0)

    def body(q_ref, k_ref, v_ref, fc_ref, gq_ref, gk_ref, o_ref, lse_ref, qa, ka, va):
        p = pl.program_id(1)
        _fox_operands(q_ref, k_ref, v_ref, fc_ref, gq_ref[...] * scale, gk_ref[...], p, qa, ka, va)
        ahead = lax.broadcasted_iota(jnp.int32, (tq, bw), 1) - lax.broadcasted_iota(jnp.int32, (tq, bw), 0)
        lane = lax.broadcasted_iota(jnp.int32, (tq, LANE), 1)

        for band in range(nband):
            c0 = band * bw

            def qtile(ii, _, c0=c0):
                r0 = pl.multiple_of(c0 + ii * tq, tq)
                rows = pl.ds(r0, tq)
                keep = ahead <= r0 - c0
                res = []
                for hh in range(2):
                    qb = qa[hh, rows, :]
                    s_b = jnp.where(keep, _nt(qb, ka[hh, c0:c0 + bw, :]), NEG)
                    m = jnp.max(s_b, axis=-1, keepdims=True)
                    if c0:
                        s_a = _nt(qb, ka[hh, 0:c0, :])
                        m = jnp.maximum(m, jnp.max(s_a, axis=-1, keepdims=True))
                    p_b = jnp.exp(s_b - m)
                    l = jnp.sum(p_b, axis=-1, keepdims=True)
                    acc = _nn(p_b, va[hh, c0:c0 + bw, :])
                    if c0:
                        p_a = jnp.exp(s_a - m)
                        l = l + jnp.sum(p_a, axis=-1, keepdims=True)
                        acc = acc + _nn(p_a, va[hh, 0:c0, :])
                    res.append((acc / l, m + jnp.log(l)))
                (o0, e0), (o1, e1) = res
                o_ref[0, rows, :] = jnp.where(lane < FOX_DH, o0, pltpu.roll(o1, FOX_DH, 1))
                lse_ref[0, 0, rows, :] = jnp.where(lane == 0, e0, jnp.where(lane == 1, e1, 0.0))
                return 0

            lax.fori_loop(0, tpb, qtile, 0)

    return _ride_call(
        body, rider, name="fox_fwd", grid=(b_, npair),
        in_specs=[col(0), col(1), col(2), full, gvec, gvec],
        out_specs=[pair, lse_spec],
        out_shape=[jax.ShapeDtypeStruct((b_, t_, fw), F32), jax.ShapeDtypeStruct((b_, npair, t_, LANE), F32)],
        scratch=[pltpu.VMEM((2, t_, LANE), MXU_DTYPE)] * 3, args=(zm, zm, zm, fc, gq2, gk2))


def _norm_bwd(x, dy, g):
    r = lax.rsqrt(jnp.mean(x * x, axis=-1, keepdims=True) + EPS)
    dyg = dy * g
    dx = r * dyg - x * (r * r * r) * jnp.mean(dyg * x, axis=-1, keepdims=True)
    return dx, jnp.sum(dy * x * r, axis=0, keepdims=True)


def _fox_bwd(zm, o, do, lse, fc, gq2, gk2, fw, col0, dz, rider=None):
    b_, t_, _ = zm.shape
    npair = fw // LANE
    tq = min(FOX_TILE, t_)
    nb = t_ // tq
    bw = min(FOX_BAND, t_)
    nband, tpb = t_ // bw, bw // tq
    scale = FOX_DH ** -0.5
    col, pair, full, gvec, lse_spec = _fox_specs(t_, fw, col0)

    def body(q_ref, k_ref, v_ref, o_ref, do_ref, lse_ref, fc_ref, gq_ref, gk_ref, _,
             dz_ref, dfc_ref, dgq_ref, dgk_ref, qa, ka, va, da, rowv, dq_acc, dk_acc, dv_acc):
        b, p = pl.program_id(0), pl.program_id(1)
        gq2v, gk2v = gq_ref[...] * scale, gk_ref[...]
        bd, lane, qx, kx, rq, rk = _fox_operands(q_ref, k_ref, v_ref, fc_ref, gq2v, gk2v, p, qa, ka, va)
        head = lane < FOX_DH
        dov = do_ref[0]
        dsum = _dot_right_exact(dov * o_ref[0], bd)
        eye = (lax.broadcasted_iota(jnp.int32, (tq, tq), 0) == lax.broadcasted_iota(jnp.int32, (tq, tq), 1)).astype(F32)
        for hh in range(2):
            da[hh] = jnp.where(head, _head_lanes(dov, hh), 0.0).astype(MXU_DTYPE)
            for blk in range(nb):
                rs = slice(blk * tq, (blk + 1) * tq)
                rowv[2 * hh:2 * hh + 1, rs] = jnp.sum(eye * lse_ref[0, 0, rs, hh:hh + 1], axis=0, keepdims=True)
                rowv[2 * hh + 1:2 * hh + 2, rs] = jnp.sum(eye * dsum[rs, hh * FOX_DH:hh * FOX_DH + 1], axis=0, keepdims=True)
        dq_acc[...] = jnp.zeros(dq_acc.shape, F32)
        ahead = lax.broadcasted_iota(jnp.int32, (tq, bw), 1) - lax.broadcasted_iota(jnp.int32, (tq, bw), 0)

        def part(hh, kb, vb, lo, hi, keep):
            qm, dm = qa[hh, lo:hi, :], da[hh, lo:hi, :]
            pt = jnp.exp(_nt(kb, qm) - rowv[2 * hh:2 * hh + 1, lo:hi])
            if keep is not None:
                pt = jnp.where(keep, pt, 0.0)
            dst = pt * (_nt(vb, dm) - rowv[2 * hh + 1:2 * hh + 2, lo:hi])
            dq_acc[hh, lo:hi, :] += _tn(dst, kb)
            return _nn(dst, qm), _nn(pt, dm)

        for band in range(nband):
            c0 = band * bw

            def kvtile(jj, _, c0=c0):
                r0 = pl.multiple_of(c0 + jj * tq, tq)
                rows = pl.ds(r0, tq)
                keep = ahead >= r0 - c0
                for hh in range(2):
                    kb, vb = ka[hh, rows, :], va[hh, rows, :]
                    dk_t, dv_t = part(hh, kb, vb, c0, c0 + bw, keep)
                    if c0 + bw < t_:
                        dk_u, dv_u = part(hh, kb, vb, c0 + bw, t_, None)
                        dk_t, dv_t = dk_t + dk_u, dv_t + dv_u
                    dk_acc[hh, rows, :] = dk_t
                    dv_acc[hh, rows, :] = dv_t
                return 0

            lax.fori_loop(0, tpb, kvtile, 0)

        dq0, dq1, dk0, dk1 = dq_acc[0], dq_acc[1], dk_acc[0], dk_acc[1]
        dqn = jnp.where(head, dq0, pltpu.roll(dq1, FOX_DH, 1))
        dkn = jnp.where(head, dk0, pltpu.roll(dk1, FOX_DH, 1))
        dqx, gq_part = _pair_norm_bwd(qx, rq, dqn, gq2v, bd)
        dkx, gk_part = _pair_norm_bwd(kx, rk, dkn, gk2v, bd)
        dz_ref[0, :, :LANE] = dqx.astype(dz_ref.dtype)
        dz_ref[0, :, LANE:2 * LANE] = dkx.astype(dz_ref.dtype)
        dz_ref[0, :, 2 * LANE:] = jnp.where(head, dv_acc[0], pltpu.roll(dv_acc[1], FOX_DH, 1)).astype(dz_ref.dtype)

        def bias_grad(dqh, dkh):
            return dqh[:, AUG + 3:AUG + 4] - dkh[:, AUG:AUG + 1]

        dfc_ref[0, 0] = jnp.where(lane == 0, bias_grad(dq0, dk0), jnp.where(lane == 1, bias_grad(dq1, dk1), 0.0))
        first = jnp.logical_and(b == 0, p == 0)
        _acc(dgq_ref, gq_part * scale, first)
        _acc(dgk_ref, gk_part, first)

    gs = jax.ShapeDtypeStruct((1, LANE), F32)
    return _ride_call(
        body, rider, name="fox_bwd", grid=(b_, npair),
        in_specs=[col(0), col(1), col(2), pair, pair, lse_spec, full, gvec, gvec, ANY],
        out_specs=[pl.BlockSpec((1, t_, 3 * LANE), lambda b, p: (b, 0, col0 // 3 + p)), lse_spec, gvec, gvec],
        out_shape=[jax.ShapeDtypeStruct(dz.shape, dz.dtype), jax.ShapeDtypeStruct((b_, npair, t_, LANE), F32), gs, gs],
        scratch=[pltpu.VMEM((2, t_, LANE), MXU_DTYPE)] * 4
        + [pltpu.VMEM((8, t_), F32)] + [pltpu.VMEM((2, t_, LANE), F32)] * 3,
        args=(zm, zm, zm, o, do, lse, fc, gq2, gk2, dz), aliases={9: 0})


def _mem_specs(t_, m_, mw, col0):
    nh = mw // LANE
    qcol = pl.BlockSpec((1, t_, LANE), lambda b, h: (b, 0, col0 + h))
    kcol = pl.BlockSpec((1, m_, LANE), lambda b, h: (b, 0, h))
    vcol = pl.BlockSpec((1, m_, LANE), lambda b, h: (b, 0, nh + h))
    ycol = pl.BlockSpec((1, t_, LANE), lambda b, h: (b, 0, h))
    gvec = pl.BlockSpec((1, LANE), lambda b, h: (0, 0))
    return qcol, kcol, vcol, ycol, gvec


def _mem_fwd(zm, mkv, gq, gk, mw, col0):
    b_, t_, _ = zm.shape
    m_ = mkv.shape[1]
    tq = min(MEM_TILE, t_)
    nb = t_ // tq
    scale = MEM_DH ** -0.5
    qcol, kcol, vcol, ycol, gvec = _mem_specs(t_, m_, mw, col0)

    def body(q_ref, k_ref, v_ref, gq_ref, gk_ref, y_ref):
        gqv, gkv = gq_ref[...] * scale, gk_ref[...]
        kv = k_ref[0]
        kn = _mx(kv * lax.rsqrt(jnp.mean(kv * kv, axis=-1, keepdims=True) + EPS) * gkv)
        vv = _mx(v_ref[0])

        def blk(i, _):
            rows = pl.ds(pl.multiple_of(i * tq, tq), tq)
            qv = q_ref[0, rows, :]
            s = _nt(qv * lax.rsqrt(jnp.mean(qv * qv, axis=-1, keepdims=True) + EPS) * gqv, kn)
            e = jnp.exp(s - jnp.max(s, axis=-1, keepdims=True))
            y_ref[0, rows, :] = _nn(e / jnp.sum(e, axis=-1, keepdims=True), vv)
            return 0

        lax.fori_loop(0, nb, blk, 0)

    return pl.pallas_call(
        body, name="mem_fwd", grid=(b_, MEM_HEADS), in_specs=[qcol, kcol, vcol, gvec, gvec], out_specs=ycol,
        out_shape=jax.ShapeDtypeStruct((b_, t_, mw), F32), compiler_params=_params(2),
    )(zm, mkv, mkv, gq, gk)


def _mem_bwd(zm, mkv, dy, gq, gk, mw, col0, dz):
    b_, t_, _ = zm.shape
    m_ = mkv.shape[1]
    tq = min(MEM_TILE, t_)
    nb = t_ // tq
    scale = MEM_DH ** -0.5
    qcol, kcol, vcol, ycol, gvec = _mem_specs(t_, m_, mw, col0)

    def body(q_ref, k_ref, v_ref, dy_ref, gq_ref, gk_ref, _, dq_ref, dk_ref, dv_ref, dgq_ref, dgk_ref):
        gqv, gkv = gq_ref[...] * scale, gk_ref[...]
        kv = k_ref[0]
        kn = _mx(kv * lax.rsqrt(jnp.mean(kv * kv, axis=-1, keepdims=True) + EPS) * gkv)
        vv = _mx(v_ref[0])

        def blk(i, carry):
            dkn, dvv, dgq = carry
            rows = pl.ds(pl.multiple_of(i * tq, tq), tq)
            qv = q_ref[0, rows, :]
            qn = _mx(qv * lax.rsqrt(jnp.mean(qv * qv, axis=-1, keepdims=True) + EPS) * gqv)
            s = _nt(qn, kn)
            e = jnp.exp(s - jnp.max(s, axis=-1, keepdims=True))
            pm = e / jnp.sum(e, axis=-1, keepdims=True)
            dob = _mx(dy_ref[0, rows, :])
            dp = _nt(dob, vv)
            ds = pm * (dp - jnp.sum(dp * pm, axis=-1, keepdims=True))
            dqv, gq_part = _norm_bwd(qv, _nn(ds, kn), gqv)
            dq_ref[0, rows, :] = dqv.astype(dq_ref.dtype)
            return dkn + _tn(ds, qn), dvv + _tn(pm, dob), dgq + gq_part * scale

        z = jnp.zeros((m_, LANE), F32)
        dkn, dvv, dgq = lax.fori_loop(0, nb, blk, (z, z, jnp.zeros((1, LANE), F32)))
        dkv, dgk = _norm_bwd(kv, dkn, gkv)
        dk_ref[0] = dkv
        dv_ref[0] = dvv
        first = jnp.logical_and(pl.program_id(0) == 0, pl.program_id(1) == 0)
        _acc(dgq_ref, dgq, first)
        _acc(dgk_ref, dgk, first)

    kblk = pl.BlockSpec((1, m_, LANE), lambda b, h: (b, 0, h))
    gs = jax.ShapeDtypeStruct((1, LANE), F32)
    ks = jax.ShapeDtypeStruct((b_, m_, mw), F32)
    return pl.pallas_call(
        body, name="mem_bwd", grid=(b_, MEM_HEADS), in_specs=[qcol, kcol, vcol, ycol, gvec, gvec, ANY],
        out_specs=[qcol, kblk, kblk, gvec, gvec],
        out_shape=[jax.ShapeDtypeStruct(dz.shape, dz.dtype), ks, ks, gs, gs], input_output_aliases={6: 0},
        compiler_params=_params(2),
    )(zm, mkv, mkv, dy, gq, gk, dz)


def _merge_specs(tm, d, w, gcol):
    row_d = pl.BlockSpec((tm, d), lambda i: (i, 0))
    row_w = pl.BlockSpec((tm, w), lambda i: (i, 0))
    gates = [pl.BlockSpec((tm, d), functools.partial(lambda i, k: (i, gcol + k), k=k)) for k in range(3)]
    w_br = pl.BlockSpec((w, d), lambda i: (0, 0))
    w_o = pl.BlockSpec((d, d), lambda i: (0, 0))
    return row_d, row_w, gates, w_br, w_o


def _merge_fwd(x, ys, zm, w_brs, w_out, gcol, g_next, tm=256):
    n, d = x.shape
    w = ys[0].shape[1]
    tm = _tile(n, tm, 8)
    row_d, row_w, gates, w_br, w_o = _merge_specs(tm, d, w, gcol)

    def body(x_ref, ya, yb, yc, g0, g1, g2, wa, wb, wc, wo, gn_ref, x1_ref, mg_ref, h_ref):
        mg = (_sig(g0[...]) * _nn(ya[...], wa[...]) + _sig(g1[...]) * _nn(yb[...], wb[...])
              + _sig(g2[...]) * _nn(yc[...], wc[...]))
        mg_ref[...] = mg.astype(mg_ref.dtype)
        x1 = x_ref[...] + _nn(mg, wo[...])
        x1_ref[...] = x1
        h_ref[...] = (x1 * lax.rsqrt(jnp.mean(x1 * x1, axis=-1, keepdims=True) + EPS) * gn_ref[...]).astype(h_ref.dtype)

    half = jax.ShapeDtypeStruct((n, d), MXU_DTYPE)
    return pl.pallas_call(
        body, name="merge_fwd", grid=(n // tm,),
        in_specs=[row_d, row_w, row_w, row_w] + gates + [w_br, w_br, w_br, w_o, pl.BlockSpec((1, d), lambda i: (0, 0))],
        out_specs=[row_d, row_d, row_d],
        out_shape=[jax.ShapeDtypeStruct((n, d), F32), half, half],
        compiler_params=_params(1),
    )(x, *ys, zm, zm, zm, *w_brs, w_out, g_next)


def _merge_bwd(dx1, ys, zm, w_brs, w_out, gcol, tm=256):
    n, d = dx1.shape
    w = ys[0].shape[1]
    tm = _tile(n, tm, 8)
    row_d, row_w, gates, w_br, w_o = _merge_specs(tm, d, w, gcol)

    def body(dx_ref, ya, yb, yc, g0, g1, g2, wa, wb, wc, wo, dgl_ref, dpa, dpb, dpc, dya, dyb, dyc):
        dm = _nt(dx_ref[...], wo[...])
        for k, (y, g, wr, dp_ref, dy_ref) in enumerate(((ya, g0, wa, dpa, dya), (yb, g1, wb, dpb, dyb),
                                                        (yc, g2, wc, dpc, dyc))):
            sg = _sig(g[...])
            pr = _nn(y[...], wr[...])
            dgl_ref[:, k * d:(k + 1) * d] = (dm * pr * sg * (1.0 - sg)).astype(dgl_ref.dtype)
            dp = (dm * sg).astype(dp_ref.dtype)
            dp_ref[...] = dp
            dy_ref[...] = _nt(dp, wr[...])

    sd = jax.ShapeDtypeStruct((n, d), MXU_DTYPE)
    sw = jax.ShapeDtypeStruct((n, w), F32)
    return pl.pallas_call(
        body, name="merge_bwd", grid=(n // tm,),
        in_specs=[row_d, row_w, row_w, row_w] + gates + [w_br, w_br, w_br, w_o],
        out_specs=[pl.BlockSpec((tm, 3 * d), lambda i: (i, 0)), row_d, row_d, row_d, row_w, row_w, row_w],
        out_shape=[jax.ShapeDtypeStruct((n, zm.shape[1]), MXU_DTYPE), sd, sd, sd, sw, sw, sw],
        compiler_params=_params(1),
    )(dx1, *ys, zm, zm, zm, *w_brs, w_out)


CONV_ROWS = 512
HALO = 8


def _ext(ref, r0, t_):
    rc = min(CONV_ROWS, t_)
    a, b = max(r0 - HALO, 0), min(r0 + rc + HALO, t_)
    parts = []
    if r0 - HALO < 0:
        parts.append(jnp.zeros((HALO, ref.shape[2]), F32))
    parts.append(ref[0, a:b, :].astype(F32))
    if r0 + rc + HALO > t_:
        parts.append(jnp.zeros((HALO, ref.shape[2]), F32))
    return jnp.concatenate(parts, axis=0) if len(parts) > 1 else parts[0]


def _gelu_parts(ac):
    e = jnp.exp(-0.5 * ac * ac)
    t = 1.0 / (1.0 + (0.3275911 * 2.0 ** -0.5) * jnp.abs(ac))
    tail = (0.5 * e) * (t * (0.254829592 + t * (-0.284496736 + t * (1.421413741 + t * (-1.453152027 + t * 1.061405429)))))
    return jnp.where(ac < 0, tail, 1.0 - tail), e * ((2.0 * math.pi) ** -0.5)


def _conv_taps(a_ext, cw, cb):
    a2, a1 = pltpu.roll(a_ext, 2, 0), pltpu.roll(a_ext, 1, 0)
    return cw[0:1, :] * a2 + cw[1:2, :] * a1 + cw[2:3, :] * a_ext + cb, a2, a1


def _glu_specs(t_, f, g):
    gate = pl.BlockSpec((1, t_, g), lambda j, b: (b, 0, j))
    value = pl.BlockSpec((1, t_, g), lambda j, b: (b, 0, f // g + j))
    cwb = pl.BlockSpec((3, g), lambda j, b: (0, j))
    cbb = pl.BlockSpec((1, g), lambda j, b: (0, j))
    return gate, value, cwb, cbb


def _glu_fwd(u, cw, cb):
    b_, t_, f2 = u.shape
    f = f2 // 2
    g = min(FFN_GROUP, f)
    rc = min(CONV_ROWS, t_)
    gate, value, cwb, cbb = _glu_specs(t_, f, g)

    def body(a_ref, v_ref, cw_ref, cb_ref, y_ref):
        cwv, cbv = cw_ref[...], cb_ref[...]
        for r0 in range(0, t_, rc):
            ac = _conv_taps(_ext(a_ref, r0, t_), cwv, cbv)[0][HALO:HALO + rc]
            cdf, _ = _gelu_parts(ac)
            y_ref[0, r0:r0 + rc, :] = (ac * cdf * v_ref[0, r0:r0 + rc, :]).astype(y_ref.dtype)

    return pl.pallas_call(
        body, name="glu_fwd", grid=(f // g, b_), in_specs=[gate, value, cwb, cbb], out_specs=gate,
        out_shape=jax.ShapeDtypeStruct((b_, t_, f), MXU_DTYPE), compiler_params=_params(2),
    )(u, u, cw, cb)


def _glu_bwd(u, dy, cw, cb):
    b_, t_, f2 = u.shape
    f = f2 // 2
    g = min(FFN_GROUP, f)
    rc = min(CONV_ROWS, t_)
    ne = rc + 2 * HALO
    gate, value, cwb, cbb = _glu_specs(t_, f, g)

    def body(a_ref, v_ref, dy_ref, cw_ref, cb_ref, da_ref, dv_ref, dcw_ref, dcb_ref):
        cwv, cbv = cw_ref[...], cb_ref[...]
        dcw = [jnp.zeros((1, g), F32) for _ in range(3)]
        dcb = jnp.zeros((1, g), F32)
        for r0 in range(0, t_, rc):
            a_ext, v_ext, dy_ext = _ext(a_ref, r0, t_), _ext(v_ref, r0, t_), _ext(dy_ref, r0, t_)
            ac, a2, a1 = _conv_taps(a_ext, cwv, cbv)
            cdf, pdf = _gelu_parts(ac)
            dac = dy_ext * v_ext * (cdf + ac * pdf)
            da = cwv[2:3, :] * dac + cwv[1:2, :] * pltpu.roll(dac, ne - 1, 0) + cwv[0:1, :] * pltpu.roll(dac, ne - 2, 0)
            mid = slice(HALO, HALO + rc)
            da_ref[0, r0:r0 + rc, :] = da[mid].astype(da_ref.dtype)
            dv_ref[0, r0:r0 + rc, :] = (dy_ext[mid] * ac[mid] * cdf[mid]).astype(dv_ref.dtype)
            dacm = dac[mid]
            dcw[0] = dcw[0] + jnp.sum(dacm * a2[mid], axis=0, keepdims=True)
            dcw[1] = dcw[1] + jnp.sum(dacm * a1[mid], axis=0, keepdims=True)
            dcw[2] = dcw[2] + jnp.sum(dacm * a_ext[mid], axis=0, keepdims=True)
            dcb = dcb + jnp.sum(dacm, axis=0, keepdims=True)
        first = pl.program_id(1) == 0
        _acc(dcw_ref, jnp.concatenate(dcw, axis=0), first)
        _acc(dcb_ref, dcb, first)

    sds = jax.ShapeDtypeStruct((b_, t_, f), MXU_DTYPE)
    return pl.pallas_call(
        body, name="glu_bwd", grid=(f // g, b_), in_specs=[gate, value, gate, cwb, cbb],
        out_specs=[gate, gate, cwb, cbb],
        out_shape=[sds, sds, jax.ShapeDtypeStruct((3, f), F32), jax.ShapeDtypeStruct((1, f), F32)],
        compiler_params=_params(2),
    )(u, u, dy, cw, cb)


def _place():
    x, y, c = lax.axis_index("x"), lax.axis_index("y"), lax.axis_index("c")
    chips = [(1 - x, y), (x, 1 - y), (1 - x, 1 - y)]
    return x, y, c, chips


def _remote(src, dst, send_sem, recv_sem, to):
    return pltpu.make_async_remote_copy(src_ref=src, dst_ref=dst, send_sem=send_sem, recv_sem=recv_sem,
                                        device_id=to, device_id_type=MESH)


STACK, COLS = "stack", "cols"


def _shard_ref(ref, kind, s, rows, c):
    if kind == COLS:
        cols = pl.ds(pl.multiple_of(s * c, LANE), c)
        return ref.at[:, cols] if rows is None else ref.at[rows, cols]
    return ref.at[s] if rows is None else ref.at[s, rows, :]


def _halves(c, half):
    mine = pl.ds(pl.multiple_of(c * half, 16), half)
    theirs = pl.ds(pl.multiple_of((1 - c) * half, 16), half)
    return mine, theirs


def _gather_parts(kinds):
    def first_copies(ins, outs, sems):
        x, y, c, chips = _place()
        me = 2 * x + y
        cps = []
        for i, (w_ref, o_ref, kind) in enumerate(zip(ins, outs, kinds)):
            r, cw = w_ref.shape
            mine, _ = _halves(c, r // 2)
            for j, chip in enumerate(chips):
                cps.append(_remote(w_ref.at[mine], _shard_ref(o_ref, kind, me, mine, cw), sems[0].at[6 * i + j],
                                   sems[1].at[6 * i + j], (*chip, c)))
        return cps

    def start(ins, outs, sems):
        for cp in first_copies(ins, outs, sems):
            cp.start()

    def finish(ins, outs, sems):
        x, y, c, chips = _place()
        sib = (x, y, 1 - c)
        passed = []
        for i, (w_ref, o_ref, kind) in enumerate(zip(ins, outs, kinds)):
            r, cw = w_ref.shape
            mine, _ = _halves(c, r // 2)
            for j, (px, py) in enumerate(chips):
                blk = _shard_ref(o_ref, kind, 2 * px + py, mine, cw)
                _remote(blk, blk, sems[0].at[6 * i + j], sems[1].at[6 * i + j], sib).wait_recv()
                passed.append(_remote(blk, blk, sems[0].at[6 * i + 3 + j], sems[1].at[6 * i + 3 + j], sib))
                passed[-1].start()
        for i, (w_ref, o_ref, kind) in enumerate(zip(ins, outs, kinds)):
            r, cw = w_ref.shape
            _, theirs = _halves(c, r // 2)
            for j, (px, py) in enumerate(chips):
                blk = _shard_ref(o_ref, kind, 2 * px + py, theirs, cw)
                _remote(blk, blk, sems[0].at[6 * i + 3 + j], sems[1].at[6 * i + 3 + j], sib).wait_recv()
        for cp in first_copies(ins, outs, sems) + passed:
            cp.wait_send()

    return start, finish


def _gather_shapes(shards, kinds):
    return [jax.ShapeDtypeStruct((a.shape[0], N_CHIPS * a.shape[1]) if k == COLS else (N_CHIPS,) + a.shape, a.dtype)
            for a, k in zip(shards, kinds)]


def _gather_sems(nw):
    return [pltpu.SemaphoreType.DMA((6 * nw,)), pltpu.SemaphoreType.DMA((6 * nw,))]


def _gather_shards(shards, kinds):
    nw = len(shards)
    start, finish = _gather_parts(kinds)

    def body(*refs):
        ins, outs, sems = refs[:nw], refs[nw:2 * nw], refs[2 * nw:]
        start(ins, outs, sems)
        finish(ins, outs, sems)

    return pl.pallas_call(
        body, name="gather_shards", in_specs=[ANY] * nw, out_specs=[ANY] * nw,
        out_shape=_gather_shapes(shards, kinds), scratch_shapes=_gather_sems(nw),
    )(*shards)


def _gather_rider(shards, kinds):
    start, finish = _gather_parts(kinds)
    return _Rider(list(shards), _gather_shapes(shards, kinds), _gather_sems(len(shards)), start, finish)


def _half_shape(g, kind):
    if kind == COLS:
        return (g.shape[0] // 2, g.shape[1])
    return (g.shape[0], g.shape[1] // 2, g.shape[2])


def _swap_parts(kinds):
    def copies(ins, outs, sems):
        x, y, c, _ = _place()
        cps = []
        for i, (g_ref, a_ref, kind) in enumerate(zip(ins, outs, kinds)):
            r = g_ref.shape[0] if kind == COLS else g_ref.shape[1]
            _, theirs = _halves(c, r // 2)
            src = g_ref.at[theirs] if kind == COLS else g_ref.at[:, theirs]
            cps.append(_remote(src, a_ref, sems[0].at[i], sems[1].at[i], (x, y, 1 - c)))
        return cps

    def start(ins, outs, sems):
        for cp in copies(ins, outs, sems):
            cp.start()

    def finish(ins, outs, sems):
        for cp in copies(ins, outs, sems):
            cp.wait()

    return start, finish


def _swap_shapes(gs, kinds):
    return [jax.ShapeDtypeStruct(_half_shape(g, k), g.dtype) for g, k in zip(gs, kinds)]


def _pair_swap_halves(gs, kinds, name):
    nw = len(gs)
    start, finish = _swap_parts(kinds)

    def body(*refs):
        ins, outs, sems = refs[:nw], refs[nw:2 * nw], refs[2 * nw:]
        start(ins, outs, sems)
        finish(ins, outs, sems)

    return pl.pallas_call(
        body, name=name, in_specs=[ANY] * nw, out_specs=[ANY] * nw, out_shape=_swap_shapes(gs, kinds),
        scratch_shapes=[pltpu.SemaphoreType.DMA((nw,)), pltpu.SemaphoreType.DMA((nw,))],
    )(*gs)


def _swap_rider(gs, kinds):
    start, finish = _swap_parts(kinds)
    nw = len(gs)
    return _Rider(list(gs), _swap_shapes(gs, kinds), [pltpu.SemaphoreType.DMA((nw,)), pltpu.SemaphoreType.DMA((nw,))],
                  start, finish)


def _row_tile(rows, width, itemsize=4, target=2 ** 21):
    return _tile(rows, max(8, target // (width * itemsize)), 8)


def _add_half(g, a, kind, c_idx, name):
    if kind == COLS:
        half, wd = a.shape
        tr = _row_tile(half, wd)
        nblk = half // tr
        grid = (nblk,)
        g_spec = pl.BlockSpec((tr, wd), lambda i, c_ref: (c_ref[0] * nblk + i, 0))
        a_spec = pl.BlockSpec((tr, wd), lambda i, c_ref: (i, 0))
    else:
        n, half, wd = a.shape
        tr = _row_tile(half, wd)
        nblk = half // tr
        grid = (n, nblk)
        g_spec = pl.BlockSpec((1, tr, wd), lambda s, i, c_ref: (s, c_ref[0] * nblk + i, 0))
        a_spec = pl.BlockSpec((1, tr, wd), lambda s, i, c_ref: (s, i, 0))

    def body(c_ref, g_ref, a_ref, o_ref):
        o_ref[...] = (g_ref[...] + a_ref[...]).astype(o_ref.dtype)

    return pl.pallas_call(
        body, name=name,
        grid_spec=pltpu.PrefetchScalarGridSpec(num_scalar_prefetch=1, grid=grid, in_specs=[g_spec, a_spec],
                                               out_specs=a_spec),
        out_shape=jax.ShapeDtypeStruct(a.shape, EXCHANGE_DTYPE), compiler_params=_params(len(grid)),
    )(c_idx, g, a)


def _exchange_parts(kinds):
    def copies(ins, outs, sems):
        x, y, c, chips = _place()
        me = 2 * x + y
        cps = []
        for i, (p_ref, b_ref, kind) in enumerate(zip(ins, outs, kinds)):
            cw = b_ref.shape[2]
            for j, (px, py) in enumerate(chips):
                cps.append(_remote(_shard_ref(p_ref, kind, 2 * px + py, None, cw), b_ref.at[me],
                                   sems[0].at[3 * i + j], sems[1].at[3 * i + j], (px, py, c)))
        return cps

    def start(ins, outs, sems):
        for cp in copies(ins, outs, sems):
            cp.start()

    def finish(ins, outs, sems):
        x, y, c, chips = _place()
        for i, b_ref in enumerate(outs):
            for j, (px, py) in enumerate(chips):
                blk = b_ref.at[2 * px + py]
                _remote(blk, blk, sems[0].at[3 * i + j], sems[1].at[3 * i + j], (px, py, c)).wait_recv()
        for cp in copies(ins, outs, sems):
            cp.wait_send()

    return start, finish


def _exchange_shapes(ps, kinds):
    return [jax.ShapeDtypeStruct((N_CHIPS,) + ((p.shape[0], p.shape[1] // N_CHIPS) if k == COLS else tuple(p.shape[1:])),
                                 p.dtype) for p, k in zip(ps, kinds)]


def _exchange_sems(nw):
    return [pltpu.SemaphoreType.DMA((3 * nw,)), pltpu.SemaphoreType.DMA((3 * nw,))]


def _exchange_rider(ps, kinds):
    start, finish = _exchange_parts(kinds)
    return _Rider(list(ps), _exchange_shapes(ps, kinds), _exchange_sems(len(ps)), start, finish)


def _sum_chips(bq, name):
    n, h, wd = bq.shape
    tr = _row_tile(h, wd * n)

    def body(b_ref, o_ref):
        acc = b_ref[0].astype(F32)
        for s in range(1, n):
            acc = acc + b_ref[s].astype(F32)
        o_ref[...] = acc

    return pl.pallas_call(
        body, name=name, grid=(h // tr,),
        in_specs=[pl.BlockSpec((n, tr, wd), lambda i: (0, i, 0))], out_specs=pl.BlockSpec((tr, wd), lambda i: (i, 0)),
        out_shape=jax.ShapeDtypeStruct((h, wd), F32), compiler_params=_params(1),
    )(bq)


def _pair_join_halves(qs):
    nw = len(qs)

    def body(*refs):
        ins, outs = refs[:nw], refs[nw:2 * nw]
        send_sems, recv_sems = refs[2 * nw:]
        x, y, c, _ = _place()
        sent = []
        for i, (q_ref, o_ref) in enumerate(zip(ins, outs)):
            sent.append(_remote(q_ref, o_ref.at[c], send_sems.at[i], recv_sems.at[i], (x, y, 1 - c)))
            sent[-1].start()
        for i, (q_ref, o_ref) in enumerate(zip(ins, outs)):
            _remote(q_ref, o_ref.at[1 - c], send_sems.at[i], recv_sems.at[i], (x, y, 1 - c)).wait_recv()
        for cp in sent:
            cp.wait_send()

    return pl.pallas_call(
        body, name="pair_join_halves", in_specs=[ANY] * nw, out_specs=[ANY] * nw,
        out_shape=[jax.ShapeDtypeStruct((2,) + q.shape, q.dtype) for q in qs],
        scratch_shapes=[pltpu.SemaphoreType.DMA((nw,)), pltpu.SemaphoreType.DMA((nw,))],
    )(*qs)


def _all_sum_small(s, name):
    sr, w = s.shape

    def body(s_ref, o_ref, buf, send_sems, recv_sems):
        x, y, c, _ = _place()
        me = 4 * x + 2 * y + c
        buf[me] = s_ref[...]
        peers = []
        for k in range(1, 8):
            px = 1 - x if k & 4 else x
            py = 1 - y if k & 2 else y
            pc = 1 - c if k & 1 else c
            peers.append((px, py, pc))
        sent = [_remote(s_ref, buf.at[me], send_sems.at[k], recv_sems.at[k], peer) for k, peer in enumerate(peers)]
        for cp in sent:
            cp.start()
        for k, (px, py, pc) in enumerate(peers):
            _remote(s_ref, buf.at[4 * px + 2 * py + pc], send_sems.at[k], recv_sems.at[k], (px, py, pc)).wait_recv()
        for cp in sent:
            cp.wait_send()
        acc = buf[0]
        for d in range(1, 8):
            acc = acc + buf[d]
        o_ref[...] = acc

    vm = pl.BlockSpec(memory_space=pltpu.VMEM)
    return pl.pallas_call(
        body, name=name, in_specs=[vm], out_specs=vm, out_shape=jax.ShapeDtypeStruct((sr, w), F32),
        scratch_shapes=[pltpu.VMEM((8, sr, w), F32), pltpu.SemaphoreType.DMA((7,)), pltpu.SemaphoreType.DMA((7,))],
    )(s)


BIG = ("w_in", "mem_kv_w", "w_br_hgrn", "w_br_fox", "w_br_mem", "w_out", "ffn_w_up", "ffn_w_down")
KIND = {"w_in": STACK, "mem_kv_w": STACK, "w_br_hgrn": COLS, "w_br_fox": COLS, "w_br_mem": COLS, "w_out": STACK,
        "ffn_w_up": STACK, "ffn_w_down": STACK}
ROW_SHARDED = ("mem_kv_w", "w_out", "ffn_w_down")
FIRST = ("w_in",)
REST = tuple(nm for nm in BIG if nm not in FIRST)
LATE = {"in_proj": tuple(nm for nm in REST if not nm.startswith("ffn_")),
        "fox_fwd": tuple(nm for nm in REST if nm.startswith("ffn_"))}
LAST = ("w_in",)
TRANSPOSED = ("w_in",)


def _z_layout(d, hw, fw, mw):
    gate, npair, nh, nm = 3 * d // LANE, fw // LANE, hw // LANE, mw // LANE
    fox0, hg0 = gate, gate + 3 * npair
    o_fox, o_mem = 4 * nh, 4 * nh + 3 * npair
    order = [o_mem + nm + j for j in range(gate)]
    order += [o_fox + k * npair + p for p in range(npair) for k in range(3)]
    order += [k * nh + h for h in range(nh) for k in range(4)]
    order += [o_mem + h for h in range(nm)]
    assert fox0 % 3 == 0 and hg0 % 4 == 0
    return fox0, hg0, hg0 + 4 * nh, order


def _reorder_blocks(a, order):
    runs, start = [], 0
    for i in range(1, len(order) + 1):
        if i == len(order) or order[i] != order[i - 1] + 1:
            runs.append((order[start], order[i - 1] + 1))
            start = i
    return jnp.concatenate([a[:, lo * LANE:hi * LANE] for lo, hi in runs], axis=1)


def _put_shard(arr, kind, s, piece):
    if kind == COLS:
        return lax.dynamic_update_slice(arr, piece, (0, s * piece.shape[1]))
    return lax.dynamic_update_slice(arr, piece[None], (s, 0, 0))


def _take_shard(arr, kind, s):
    if kind == COLS:
        return lax.dynamic_slice(arr, (0, s * (arr.shape[1] // N_CHIPS)), (arr.shape[0], arr.shape[1] // N_CHIPS))
    return lax.dynamic_index_in_dim(arr, s, 0, keepdims=False)


def _w_in_pieces(cs, s1, nf):
    out = []
    for s in range(N_CHIPS):
        lo, hi = cs * s, cs * (s + 1)
        for a, b, forget in ((lo, min(hi, s1), False), (max(lo, s1), min(hi, s1 + nf), True), (max(lo, s1 + nf), hi, False)):
            if a < b:
                out.append((s, a - lo, b - lo, forget, a - s1 if forget else (a if a < s1 else a - nf)))
    return out


def _split_w_in(stacked, s1, nf):
    pieces = _w_in_pieces(stacked.shape[2], s1, nf)
    main = [stacked[s, :, a:b] for s, a, b, forget, _ in pieces if not forget]
    ff = [stacked[s, :, a:b] for s, a, b, forget, _ in pieces if forget]
    return jnp.concatenate(main, axis=1), jnp.concatenate(ff, axis=1)


def _join_w_in(g_main, g_ff, s1, nf):
    cs = (g_main.shape[1] + nf) // N_CHIPS
    shards = [[] for _ in range(N_CHIPS)]
    for s, a, b, forget, off in _w_in_pieces(cs, s1, nf):
        shards[s].append((g_ff if forget else g_main)[:, off:off + b - a])
    return jnp.stack([jnp.concatenate(p, axis=1) if len(p) > 1 else p[0] for p in shards])


SMALL = ("norm_mix_g", "norm_mem_g", "norm_ffn_g", "hgrn_lb_logits", "hgrn_norm_g", "fox_f_bias", "fox_q_norm_g",
         "fox_k_norm_g", "mem_q_norm_g", "mem_k_norm_g", "ffn_conv_b")


def _small_rows(shapes):
    rows = []
    for a, (r, c) in enumerate(shapes):
        for i in range(r):
            for lo in range(0, c, FLAT_W):
                rows.append((a, i, lo, min(FLAT_W, c - lo)))
    return rows


def _pack_small(vals):
    rows = _small_rows([v.shape for v in vals])
    sr = -(-len(rows) // 8) * 8

    def body(*refs):
        o_ref = refs[-1]
        o_ref[...] = jnp.zeros(o_ref.shape, F32)
        for k, (a, i, lo, wd) in enumerate(rows):
            o_ref[k:k + 1, 0:wd] = refs[a][i:i + 1, lo:lo + wd]

    vm = pl.BlockSpec(memory_space=pltpu.VMEM)
    return pl.pallas_call(body, name="pack_small", in_specs=[vm] * len(vals), out_specs=vm,
                          out_shape=jax.ShapeDtypeStruct((sr, FLAT_W), F32))(*vals)


def _row_of(buf_ref, rows, a, i):
    parts = [buf_ref[k:k + 1, 0:wd] for k, (a2, i2, _, wd) in enumerate(rows) if (a2, i2) == (a, i)]
    return jnp.concatenate(parts, axis=1) if len(parts) > 1 else parts[0]


def _unpack_small(buf, shapes):
    rows = _small_rows(shapes)

    def body(buf_ref, *outs):
        for a, (r, _) in enumerate(shapes):
            for i in range(r):
                outs[a][i:i + 1, :] = _row_of(buf_ref, rows, a, i)

    vm = pl.BlockSpec(memory_space=pltpu.VMEM)
    return pl.pallas_call(body, name="unpack_small", in_specs=[vm], out_specs=[vm] * len(shapes),
                          out_shape=[jax.ShapeDtypeStruct(shp, F32) for shp in shapes])(buf)


def _adamw_small(buf, shapes, ws, ms, vs):
    n = len(ws)
    rows = _small_rows(shapes)
    c1 = 1.0 / (1.0 - ADAM_B1 ** ADAM_STEP)
    c2 = 1.0 / (1.0 - ADAM_B2 ** ADAM_STEP)

    def body(buf_ref, *refs):
        w_refs, m_refs, v_refs = refs[:n], refs[n:2 * n], refs[2 * n:3 * n]
        outs = refs[3 * n:]
        g_out, d_out, m_out, v_out, rest = outs[:n], outs[n:2 * n], outs[2 * n:3 * n], outs[3 * n:4 * n], outs[4 * n:]
        for a, (r, _) in enumerate(shapes):
            for i in range(r):
                gv = _row_of(buf_ref, rows, a, i)
                if a >= n:
                    rest[a - n][i:i + 1, :] = gv
                    continue
                row = slice(i, i + 1)
                mn = ADAM_B1 * m_refs[a][row, :] + (1.0 - ADAM_B1) * gv
                vn = ADAM_B2 * v_refs[a][row, :] + (1.0 - ADAM_B2) * (gv * gv)
                g_out[a][row, :] = gv
                d_out[a][row, :] = -ADAM_LR * ((mn * c1) / (jnp.sqrt(vn * c2) + ADAM_EPS) + ADAM_WD * w_refs[a][row, :])
                m_out[a][row, :] = mn
                v_out[a][row, :] = vn

    vm = pl.BlockSpec(memory_space=pltpu.VMEM)
    own = [jax.ShapeDtypeStruct(shp, F32) for shp in shapes[:n]]
    outs = pl.pallas_call(
        body, name="adamw_small", in_specs=[vm] * (1 + 3 * n), out_specs=[vm] * (4 * n + len(shapes) - n),
        out_shape=own * 4 + [jax.ShapeDtypeStruct(shp, F32) for shp in shapes[n:]],
    )(buf, *ws, *ms, *vs)
    return outs[:n], outs[n:2 * n], outs[2 * n:3 * n], outs[3 * n:4 * n], outs[4 * n:]


def _pad_lanes(v, width=LANE):
    return jnp.pad(v, ((0, 0), (0, width - v.shape[1])))


WEIGHTS = ("norm_mix_g", "norm_mem_g", "w_in", "hgrn_lb_logits", "hgrn_norm_g", "fox_f_bias", "fox_q_norm_g",
           "fox_k_norm_g", "mem_kv_w", "mem_q_norm_g", "mem_k_norm_g", "w_br_hgrn", "w_br_fox", "w_br_mem", "w_out",
           "norm_ffn_g", "ffn_w_up", "ffn_conv_w", "ffn_conv_b", "ffn_w_down")


def _local_step(x, mem, target, w, full, conv_w, late=None, hooks=None):
    b_, t_, d = x.shape
    n = b_ * t_
    hw, fw, mw = HG_HEADS * HG_D, FOX_HEADS * FOX_DH, MEM_HEADS * MEM_DH
    m_ = mem.shape[1]
    f = conv_w.shape[1]
    s1 = 4 * hw + 3 * fw
    fox_col, hg_col, mem_col, order = _z_layout(d, hw, fw, mw)
    gate_col = 0
    inverse = [order.index(j) for j in range(len(order))]

    w_main, w_ff = _split_w_in(full["w_in"], s1, FOX_HEADS)
    w_main = _reorder_blocks(w_main, order)
    w_ff = _pad_lanes(w_ff)
    f_bias = _pad_lanes(w["fox_f_bias"])
    cb = w["ffn_conv_b"]

    x2 = x.reshape(n, d)
    h = _rmsnorm_fwd(x2, w["norm_mix_g"], name="norm_mix_fwd")
    if late:
        pieces, kinds, finish = late["in_proj"]
        zm, gathered = _matmul(h, w_main, name="in_proj", rider=_gather_rider(pieces, kinds))
        full = {**full, **finish(gathered)}
    else:
        zm = _matmul(h, w_main, name="in_proj")
    w_brs = [full["w_br_hgrn"], full["w_br_fox"], full["w_br_mem"]]
    w_out, w_kv = full["w_out"], full["mem_kv_w"]
    zf = _matmul(h, w_ff, name="in_proj_forget")
    zm3, zf3 = zm.reshape(b_, t_, -1), zf.reshape(b_, t_, LANE)
    ya = _hgrn_fwd(zm3, w["hgrn_lb_logits"], w["hgrn_norm_g"], hw, hg_col)
    fc = _fox_prep(zf3, f_bias)
    fox_gq, fox_gk = jnp.tile(w["fox_q_norm_g"], (1, 2)), jnp.tile(w["fox_k_norm_g"], (1, 2))
    if late:
        pieces, kinds, finish = late["fox_fwd"]
        (yb, lse), gathered = _fox_fwd(zm3, fc, fox_gq, fox_gk, fw, fox_col, _gather_rider(pieces, kinds))
        full = {**full, **finish(gathered)}
    else:
        yb, lse = _fox_fwd(zm3, fc, fox_gq, fox_gk, fw, fox_col)[0]
    w_up, w_down = full["ffn_w_up"], full["ffn_w_down"]
    mem2 = mem.reshape(b_ * m_, d)
    hm = _rmsnorm_fwd(mem2, w["norm_mem_g"], name="norm_mem_fwd")
    mkv = _matmul(hm, w_kv, name="mem_kv_proj").reshape(b_, m_, 2 * mw)
    yc = _mem_fwd(zm3, mkv, w["mem_q_norm_g"], w["mem_k_norm_g"], mw, mem_col)
    ys = [ya.reshape(n, hw), yb.reshape(n, fw), yc.reshape(n, mw)]
    x1, merged, h2 = _merge_fwd(x2, ys, zm, w_brs, w_out, gate_col, w["norm_ffn_g"])
    u = _matmul(h2, w_up, name="ffn_up")
    u3 = u.reshape(b_, t_, 2 * f)
    yff = _glu_fwd(u3, conv_w, cb).reshape(n, f)
    dy, (loss_vec,), _ = _matmul_rows([yff], w_down, name="ffn_down_loss", tb=False, row_ins=[x1, target.reshape(n, d)],
                                      vec_ins=[], epilogue=_loss_epilogue, n_vec_out=1)

    grads = {}

    def ridden(name, call):
        if not hooks or name not in hooks:
            return call(None)[0]
        rider, then = hooks[name](grads)
        outs, extra = call(rider)
        then(extra)
        return outs

    dyff = _matmul(dy, w_down, tb=True, name="ffn_down_dx")
    grads["ffn_w_down"] = _matmul(yff, dy, ta=True, name="ffn_down_dw", tm=1408)
    du_a, du_v, grads["ffn_conv_w"], grads["ffn_conv_b"] = _glu_bwd(u3, dyff.reshape(b_, t_, f), conv_w, cb)
    du_a, du_v = du_a.reshape(n, f), du_v.reshape(n, f)
    dx1, (grads["norm_ffn_g"],), _ = _matmul_rows(
        [du_a, du_v], w_up, name="ffn_up_dx", tb=True, row_ins=[x1, dy], vec_ins=[w["norm_ffn_g"]],
        epilogue=_norm_bwd_epilogue(0), n_vec_out=1)
    grads["ffn_w_up"] = _matmul(h2, None, ta=True, name="ffn_up_dw", b_parts=[du_a, du_v], tn=f // 2, stack_out=True)

    dz, dpa, dpb, dpc, dya, dyb, dyc = _merge_bwd(dx1, ys, zm, w_brs, w_out, gate_col)
    dz = dz.reshape(b_, t_, -1)
    grads["w_out"] = _matmul(merged, dx1, ta=True, name="out_proj_dw")
    for nm, y_, dp_ in zip(("w_br_hgrn", "w_br_fox", "w_br_mem"), ys, (dpa, dpb, dpc)):
        grads[nm] = _matmul(y_, dp_, ta=True, name=nm + "_dw")

    dz, dmk, dmv, grads["mem_q_norm_g"], grads["mem_k_norm_g"] = _mem_bwd(
        zm3, mkv, dyc.reshape(b_, t_, mw), w["mem_q_norm_g"], w["mem_k_norm_g"], mw, mem_col, dz)
    dmkv = jnp.concatenate([dmk, dmv], axis=-1).reshape(b_ * m_, 2 * mw)
    grads["mem_kv_w"] = _matmul(hm, dmkv, ta=True, name="mem_kv_dw")
    dhm = _matmul(dmkv, w_kv, tb=True, name="mem_kv_dx")
    _, grads["norm_mem_g"] = _rmsnorm_bwd(mem2, [dhm], w["norm_mem_g"], None, name="norm_mem_bwd")

    dz, dfc, g_fq, g_fk = ridden("fox_bwd", lambda rider: _fox_bwd(
        zm3, yb, dyb.reshape(b_, t_, fw), lse, fc, fox_gq, fox_gk, fw, fox_col, dz, rider))
    grads["fox_q_norm_g"] = g_fq[:, :FOX_DH] + g_fq[:, FOX_DH:]
    grads["fox_k_norm_g"] = g_fk[:, :FOX_DH] + g_fk[:, FOX_DH:]
    dzf, g_fb = _fox_post(dfc, zf3, f_bias)
    grads["fox_f_bias"] = g_fb[:, :FOX_HEADS]

    dz, grads["hgrn_lb_logits"], grads["hgrn_norm_g"] = ridden("hgrn_bwd", lambda rider: _hgrn_bwd(
        zm3, dya.reshape(b_, t_, hw), w["hgrn_lb_logits"], w["hgrn_norm_g"], hw, hg_col, dz, rider))
    dzm = dz.reshape(n, -1)
    dzf2 = dzf.reshape(n, LANE)
    g_main = _matmul(h, dzm, ta=True, name="in_proj_dw")
    g_ff = _matmul(h, dzf2, ta=True, name="in_proj_forget_dw")
    grads["w_in"] = _join_w_in(_reorder_blocks(g_main, inverse), g_ff[:, :FOX_HEADS], s1, FOX_HEADS)

    dh_b = _matmul(dzf2, w_ff, tb=True, name="in_proj_forget_dx")

    def in_proj_dx(rider):
        out = _matmul(dzm, w_main, tb=True, name="in_proj_dx", rider=rider)
        return ([out[0]], out[1]) if rider else ([out], None)

    dh_a, = ridden("in_proj_dx", in_proj_dx)
    grad_x, grads["norm_mix_g"] = _rmsnorm_bwd(x2, [dh_a, dh_b], w["norm_mix_g"], dx1, name="norm_mix_bwd")
    return loss_vec, grad_x.reshape(b_, t_, d), grads


def kernel(x, mem, norm_mix_g, norm_mem_g, w_in, hgrn_lb_logits, hgrn_norm_g, fox_f_bias, fox_q_norm_g, fox_k_norm_g, mem_kv_w, mem_q_norm_g, mem_k_norm_g, w_br_hgrn, w_br_fox, w_br_mem, w_out, norm_ffn_g, ffn_w_up, ffn_conv_w, ffn_conv_b, ffn_w_down, loss_target, m_norm_mix_g, m_norm_mem_g, m_w_in, m_hgrn_lb_logits, m_hgrn_norm_g, m_fox_f_bias, m_fox_q_norm_g, m_fox_k_norm_g, m_mem_kv_w, m_mem_q_norm_g, m_mem_k_norm_g, m_w_br_hgrn, m_w_br_fox, m_w_br_mem, m_w_out, m_norm_ffn_g, m_ffn_w_up, m_ffn_conv_w, m_ffn_conv_b, m_ffn_w_down, v_norm_mix_g, v_norm_mem_g, v_w_in, v_hgrn_lb_logits, v_hgrn_norm_g, v_fox_f_bias, v_fox_q_norm_g, v_fox_k_norm_g, v_mem_kv_w, v_mem_q_norm_g, v_mem_k_norm_g, v_w_br_hgrn, v_w_br_fox, v_w_br_mem, v_w_out, v_norm_ffn_g, v_ffn_w_up, v_ffn_conv_w, v_ffn_conv_b, v_ffn_w_down):
    w = dict(zip(WEIGHTS, (norm_mix_g, norm_mem_g, w_in, hgrn_lb_logits, hgrn_norm_g, fox_f_bias, fox_q_norm_g,
                           fox_k_norm_g, mem_kv_w, mem_q_norm_g, mem_k_norm_g, w_br_hgrn, w_br_fox, w_br_mem, w_out,
                           norm_ffn_g, ffn_w_up, ffn_conv_w, ffn_conv_b, ffn_w_down)))
    m = dict(zip(WEIGHTS, (m_norm_mix_g, m_norm_mem_g, m_w_in, m_hgrn_lb_logits, m_hgrn_norm_g, m_fox_f_bias,
                           m_fox_q_norm_g, m_fox_k_norm_g, m_mem_kv_w, m_mem_q_norm_g, m_mem_k_norm_g, m_w_br_hgrn,
                           m_w_br_fox, m_w_br_mem, m_w_out, m_norm_ffn_g, m_ffn_w_up, m_ffn_conv_w, m_ffn_conv_b,
                           m_ffn_w_down)))
    v = dict(zip(WEIGHTS, (v_norm_mix_g, v_norm_mem_g, v_w_in, v_hgrn_lb_logits, v_hgrn_norm_g, v_fox_f_bias,
                           v_fox_q_norm_g, v_fox_k_norm_g, v_mem_kv_w, v_mem_q_norm_g, v_mem_k_norm_g, v_w_br_hgrn,
                           v_w_br_fox, v_w_br_mem, v_w_out, v_norm_ffn_g, v_ffn_w_up, v_ffn_conv_w, v_ffn_conv_b,
                           v_ffn_w_down)))
    c_idx = lax.axis_index("c")
    chip = 2 * lax.axis_index("x") + lax.axis_index("y")

    mine = {nm: w[nm][0].astype(MXU_DTYPE) for nm in BIG}

    def gathered_full(names, arrays):
        out = {nm: _put_shard(g, KIND[nm], chip, mine[nm]) for nm, g in zip(names, arrays)}
        return {nm: g.reshape(-1, g.shape[2]) if nm in ROW_SHARDED else g for nm, g in out.items()}

    full = gathered_full(FIRST, _gather_shards([mine[nm] for nm in FIRST], [KIND[nm] for nm in FIRST]))
    late = {host: ([mine[nm] for nm in names], [KIND[nm] for nm in names],
                   functools.partial(gathered_full, names)) for host, names in LATE.items()}
    cs = ffn_conv_w.shape[2]
    f = cs * N_CHIPS
    placed = lax.dynamic_update_slice(jnp.zeros((3, f), F32), ffn_conv_w[0] * (c_idx == 0).astype(F32), (0, chip * cs))
    conv_w = _unpack_small(_all_sum_small(_pack_small([placed]), "gather_conv_w"), [(3, f)])[0]

    c_arr = jnp.reshape(c_idx, (1,)).astype(jnp.int32)

    def stacked(nm, g):
        return g.reshape(N_CHIPS, -1, g.shape[1]) if nm in ROW_SHARDED else g

    def with_own(landed, partial, kinds):
        return [_put_shard(bq, STACK, chip, _take_shard(p, k, chip)) for bq, p, k in zip(landed, partial, kinds)]

    kinds_rest, kinds_last = [KIND[nm] for nm in REST], [KIND[nm] for nm in LAST]
    state = {}

    def swap_rest(grads):
        gs = [stacked(nm, grads[nm]) for nm in REST]

        def then(from_sibling):
            state["partial_rest"] = [_add_half(g, a, k, c_arr, "add_half_" + nm)
                                     for g, a, k, nm in zip(gs, from_sibling, kinds_rest, REST)]

        return _swap_rider(gs, kinds_rest), then

    def exchange_rest(grads):
        def then(landed):
            state["landed_rest"] = with_own(landed, state["partial_rest"], kinds_rest)

        return _exchange_rider(state["partial_rest"], kinds_rest), then

    def exchange_last(grads):
        gs = [stacked(nm, grads[nm]) for nm in LAST]
        from_sibling = _pair_swap_halves(gs, kinds_last, "pair_swap_halves_last")
        partial = [_add_half(g, a, k, c_arr, "add_half_" + nm) for g, a, k, nm in zip(gs, from_sibling, kinds_last, LAST)]

        def then(landed):
            state["landed_last"] = with_own(landed, partial, kinds_last)

        return _exchange_rider(partial, kinds_last), then

    hooks = {"fox_bwd": swap_rest, "hgrn_bwd": exchange_rest, "in_proj_dx": exchange_last}

    loss_vec, grad_x, grads = _local_step(x, mem, loss_target, w, full, conv_w, late, hooks)

    landed = dict(zip(LAST + REST, state["landed_last"] + state["landed_rest"]))
    reduced_half = [_sum_chips(landed[nm], "sum_chips_" + nm) for nm in BIG]
    joined = [lax.dynamic_update_slice(o, q[None], (c_idx, 0, 0)).reshape(2 * q.shape[0], q.shape[1])
              for o, q in zip(_pair_join_halves(reduced_half), reduced_half)]
    gshards = dict(zip(BIG, joined))

    small_shapes = [w[nm].shape for nm in SMALL] + [grads["ffn_conv_w"].shape, loss_vec.shape]
    summed = _all_sum_small(_pack_small([grads[nm] for nm in SMALL] + [grads["ffn_conv_w"], loss_vec]),
                            "all_sum_small_grads")
    g_small, d_small, m_small, v_small, (g_conv_w, loss_row) = _adamw_small(
        summed, small_shapes, [w[nm] for nm in SMALL], [m[nm] for nm in SMALL], [v[nm] for nm in SMALL])
    loss = jnp.sum(loss_row)
    g_out = {nm: gshards[nm][None] for nm in BIG}
    g_out["ffn_conv_w"] = lax.dynamic_slice(g_conv_w, (0, chip * cs), (3, cs))[None]
    delta, new_m, new_v = dict(zip(SMALL, d_small)), dict(zip(SMALL, m_small)), dict(zip(SMALL, v_small))
    g_out.update(zip(SMALL, g_small))
    for nm in BIG + ("ffn_conv_w",):
        operands = (w[nm], g_out[nm], m[nm], v[nm])
        if nm in TRANSPOSED:
            operands = [jnp.swapaxes(a, 1, 2) for a in operands]
        outs = _adamw(*operands, name="adamw_" + nm)
        delta[nm], new_m[nm], new_v[nm] = [jnp.swapaxes(o, 1, 2) for o in outs] if nm in TRANSPOSED else outs

    return (loss, grad_x, *[g_out[nm] for nm in WEIGHTS], *[delta[nm] for nm in WEIGHTS],
            *[new_m[nm] for nm in WEIGHTS], *[new_v[nm] for nm in WEIGHTS])
```

```python
import functools
import math

import jax
import jax.numpy as jnp
from jax import lax
from jax.experimental import pallas as pl
from jax.experimental.pallas import tpu as pltpu

F32 = jnp.float32
BF16 = jnp.bfloat16
MXU_DTYPE = jnp.bfloat16
EXCHANGE_DTYPE = jnp.bfloat16

EPS = 1e-6
HG_HEADS, HG_D = 4, 128
FOX_HEADS, FOX_DH = 8, 64
MEM_HEADS, MEM_DH = 4, 128
MEM_TILE = 2048
HG_CHUNK = 64
FOX_BLOCK = 256
LANE = 128
FFN_GROUP = 256
FLAT_W = 1024
VMEM_LIMIT = 56 * 2 ** 20
NEG = -1e30
N_CHIPS = 4

ADAM_LR, ADAM_B1, ADAM_B2, ADAM_EPS, ADAM_WD, ADAM_STEP = 0.001, 0.9, 0.999, 1e-08, 0.01, 10

MESH = pl.DeviceIdType.MESH
ANY = pl.BlockSpec(memory_space=pl.ANY)


def _mx(x):
    return x.astype(MXU_DTYPE)


def _dot(a, b, ca, cb):
    return lax.dot_general(_mx(a), _mx(b), (((ca,), (cb,)), ((), ())), preferred_element_type=F32)


def _nn(a, b):
    return _dot(a, b, 1, 0)


def _nt(a, b):
    return _dot(a, b, 1, 1)


def _tn(a, b):
    return _dot(a, b, 0, 0)


def _tri_dot(tri_bf, x):
    hi = x.astype(BF16)
    r = x - hi.astype(F32)
    mid = r.astype(BF16)
    lo = (r - mid.astype(F32)).astype(BF16)

    def d(v):
        return lax.dot_general(tri_bf, v, (((1,), (0,)), ((), ())), preferred_element_type=F32)

    return d(hi) + d(mid) + d(lo)


def _sig(x):
    return jax.nn.sigmoid(x)


def _tile(dim, pref, unit=LANE):
    if dim <= pref:
        return dim
    t = pref - pref % unit
    while t >= unit:
        if dim % t == 0:
            return t
        t -= unit
    return dim


def _params(n_grid):
    return pltpu.CompilerParams(dimension_semantics=("arbitrary",) * n_grid, vmem_limit_bytes=VMEM_LIMIT)


def _acc(ref, val, first):
    @pl.when(first)
    def _():
        ref[...] = val

    @pl.when(jnp.logical_not(first))
    def _():
        ref[...] += val


class _Rider:
    def __init__(self, inputs, out_shapes, scratch, start, finish):
        self.inputs, self.out_shapes, self.scratch, self.start, self.finish = inputs, out_shapes, scratch, start, finish


def _ride(body, rider, n_in, n_out, grid):
    if rider is None:
        return body
    ri, ro, rs = len(rider.inputs), len(rider.out_shapes), len(rider.scratch)

    def wrapped(*refs):
        a, b, c = n_in + ri, n_in + ri + n_out, n_in + ri + n_out + ro
        base = refs[:n_in] + refs[a:b] + refs[c:len(refs) - rs]
        r_in, r_out, r_scr = refs[n_in:a], refs[b:c], refs[len(refs) - rs:]
        step = pl.program_id(0)
        for ax in range(1, len(grid)):
            step = step * grid[ax] + pl.program_id(ax)

        @pl.when(step == 0)
        def _():
            rider.start(r_in, r_out, r_scr)

        body(*base)

        @pl.when(step == math.prod(grid) - 1)
        def _():
            rider.finish(r_in, r_out, r_scr)

    return wrapped


def _ride_call(body, rider, *, name, grid, in_specs, out_specs, out_shape, scratch, args, aliases=None):
    n_in, n_out = len(in_specs), len(out_specs)
    aliases = aliases or {}
    if rider is None:
        outs = pl.pallas_call(body, name=name, grid=grid, in_specs=in_specs, out_specs=out_specs, out_shape=out_shape,
                              scratch_shapes=scratch, input_output_aliases=aliases,
                              compiler_params=_params(len(grid)))(*args)
        return list(outs), None
    outs = pl.pallas_call(
        _ride(body, rider, n_in, n_out, grid), name=name, grid=grid,
        in_specs=list(in_specs) + [ANY] * len(rider.inputs), out_specs=list(out_specs) + [ANY] * len(rider.out_shapes),
        out_shape=list(out_shape) + list(rider.out_shapes), scratch_shapes=list(scratch) + list(rider.scratch),
        input_output_aliases=aliases, compiler_params=_params(len(grid)),
    )(*args, *rider.inputs)
    return list(outs[:n_out]), list(outs[n_out:])


def _matmul(a, b, *, name, ta=False, tb=False, tm=1024, tn=2048, tk=None, rider=None, b_parts=None, stack_out=False):
    m, k = (a.shape[1], a.shape[0]) if ta else a.shape
    tk = tk or (1024 if ta else 2048)
    stacked_b = b is not None and b.ndim == 3
    if b_parts:
        n, tn = 2 * b_parts[0].shape[1], _tile(b_parts[0].shape[1], tn)
    elif stacked_b:
        n, tn = b.shape[0] * b.shape[2], b.shape[2]
    else:
        n = b.shape[0] if tb else b.shape[1]
        tn = _tile(n, tn)
    tm, tk = _tile(m, tm), _tile(k, tk)
    nk, nj = k // tk, n // tn

    def body(a_ref, *refs):
        o_ref = refs[-1]
        if b_parts:
            bv = jnp.where(pl.program_id(1) < nj // 2, refs[0][...], refs[1][...])
        else:
            bv = refs[0][...]
        p = _dot(a_ref[...], bv, 0 if ta else 1, 1 if tb else 0)
        if nk == 1:
            o_ref[...] = p
        else:
            _acc(o_ref, p, pl.program_id(2) == 0)

    a_spec = pl.BlockSpec((tk, tm), lambda i, j, kk: (kk, i)) if ta else pl.BlockSpec((tm, tk), lambda i, j, kk: (i, kk))
    if b_parts:
        half = nj // 2
        b_specs = [pl.BlockSpec((tk, tn), lambda i, j, kk: (kk, jnp.minimum(j, half - 1))),
                   pl.BlockSpec((tk, tn), lambda i, j, kk: (kk, jnp.maximum(j - half, 0)))]
        b_args = list(b_parts)
    elif stacked_b:
        b_specs, b_args = [pl.BlockSpec((None, tk, tn), lambda i, j, kk: (j, kk, 0))], [b]
    else:
        b_specs = [pl.BlockSpec((tn, tk), lambda i, j, kk: (j, kk)) if tb else pl.BlockSpec((tk, tn), lambda i, j, kk: (kk, j))]
        b_args = [b]
    if stack_out:
        o_spec, o_sds = pl.BlockSpec((None, tm, tn), lambda i, j, kk: (j, i, 0)), jax.ShapeDtypeStruct((nj, m, tn), F32)
    else:
        o_spec, o_sds = pl.BlockSpec((tm, tn), lambda i, j, kk: (i, j)), jax.ShapeDtypeStruct((m, n), F32)
    outs, extra = _ride_call(body, rider, name=name, grid=(m // tm, nj, nk), in_specs=[a_spec] + b_specs,
                             out_specs=[o_spec], out_shape=[o_sds], scratch=[], args=(a, *b_args))
    return (outs[0], extra) if rider else outs[0]


def _matmul_rows(a_parts, b, *, name, tb, row_ins, vec_ins, epilogue, n_vec_out, tm=512, tk=2048, rider=None):
    m, kp = a_parts[0].shape
    stacked_b = b.ndim == 3
    n = b.shape[1] if stacked_b else (b.shape[0] if tb else b.shape[1])
    tm, tk = _tile(m, tm, 8), (b.shape[2] if stacked_b else _tile(kp, tk))
    nk = kp // tk
    n_a, n_row, n_vec = len(a_parts), len(row_ins), len(vec_ins)

    def body(*refs):
        a_refs, b_refs = refs[:n_a], refs[n_a:2 * n_a]
        rows = refs[2 * n_a:2 * n_a + n_row]
        vecs = refs[2 * n_a + n_row:2 * n_a + n_row + n_vec]
        o_ref = refs[2 * n_a + n_row + n_vec]
        v_refs = refs[2 * n_a + n_row + n_vec + 1:-1]
        acc_ref = refs[-1]
        i, kk = pl.program_id(0), pl.program_id(1)
        p = _dot(a_refs[0][...], b_refs[0][...], 1, 1 if tb else 0)
        for a_ref, b_ref in zip(a_refs[1:], b_refs[1:]):
            p = p + _dot(a_ref[...], b_ref[...], 1, 1 if tb else 0)
        _acc(acc_ref, p, kk == 0)

        @pl.when(kk == nk - 1)
        def _():
            out, vouts = epilogue(acc_ref[...], *[r[...] for r in rows], *[v[...] for v in vecs])
            o_ref[...] = out
            for v_ref, v in zip(v_refs, vouts):
                _acc(v_ref, v, i == 0)

    a_spec = pl.BlockSpec((tm, tk), lambda i, kk: (i, kk))
    if stacked_b:
        b_specs = [pl.BlockSpec((None, n, tk), functools.partial(lambda i, kk, q: (q * nk + kk, 0, 0), q=q))
                   for q in range(n_a)]
    else:
        b_specs = [pl.BlockSpec((n, tk), functools.partial(lambda i, kk, q: (0, q * nk + kk), q=q)) if tb else
                   pl.BlockSpec((tk, n), functools.partial(lambda i, kk, q: (q * nk + kk, 0), q=q)) for q in range(n_a)]
    row = pl.BlockSpec((tm, n), lambda i, kk: (i, 0))
    vec = pl.BlockSpec((1, n), lambda i, kk: (0, 0))
    outs, extra = _ride_call(
        body, rider, name=name, grid=(m // tm, nk),
        in_specs=[a_spec] * n_a + b_specs + [row] * n_row + [vec] * n_vec,
        out_specs=[row] + [vec] * n_vec_out,
        out_shape=[jax.ShapeDtypeStruct((m, n), F32)] + [jax.ShapeDtypeStruct((1, n), F32)] * n_vec_out,
        scratch=[pltpu.VMEM((tm, n), F32)], args=(*a_parts, *([b] * n_a), *row_ins, *vec_ins))
    return outs[0], outs[1:], extra


def _norm_bwd_epilogue(n_dh):
    def epilogue(dh, x, res, *rest):
        for extra in rest[:n_dh]:
            dh = dh + extra
        g = rest[n_dh]
        r = lax.rsqrt(jnp.mean(x * x, axis=-1, keepdims=True) + EPS)
        dhg = dh * g
        dx = res + r * dhg - x * (r * r * r) * jnp.mean(dhg * x, axis=-1, keepdims=True)
        return dx, [jnp.sum(dh * x * r, axis=0, keepdims=True)]

    return epilogue


def _loss_epilogue(y, x1, target):
    d = y.shape[1]
    err = x1 + y - target
    return err * (1.0 / d), [jnp.sum(err * err, axis=0, keepdims=True) * (0.5 / d)]


def _rmsnorm_fwd(x, g, *, name, tm=512):
    n, d = x.shape
    tm = _tile(n, tm, 8)

    def body(x_ref, g_ref, o_ref):
        xv = x_ref[...]
        r = lax.rsqrt(jnp.mean(xv * xv, axis=-1, keepdims=True) + EPS)
        o_ref[...] = (xv * r * g_ref[...]).astype(o_ref.dtype)

    return pl.pallas_call(
        body, name=name, grid=(n // tm,),
        in_specs=[pl.BlockSpec((tm, d), lambda i: (i, 0)), pl.BlockSpec((1, d), lambda i: (0, 0))],
        out_specs=pl.BlockSpec((tm, d), lambda i: (i, 0)),
        out_shape=jax.ShapeDtypeStruct((n, d), MXU_DTYPE),
        compiler_params=_params(1),
    )(x, g)


def _rmsnorm_bwd(x, dhs, g, res, *, name, tm=512):
    n, d = x.shape
    tm = _tile(n, tm, 8)
    n_dh = len(dhs)
    has_res = res is not None

    def body(*refs):
        x_ref, dh_refs, g_ref = refs[0], refs[1:1 + n_dh], refs[1 + n_dh]
        res_ref = refs[2 + n_dh] if has_res else None
        dx_ref, dg_ref = refs[-2], refs[-1]
        xv = x_ref[...]
        dh = dh_refs[0][...].astype(F32)
        for r_ in dh_refs[1:]:
            dh = dh + r_[...].astype(F32)
        r = lax.rsqrt(jnp.mean(xv * xv, axis=-1, keepdims=True) + EPS)
        dhg = dh * g_ref[...]
        dx = r * dhg - xv * (r * r * r) * jnp.mean(dhg * xv, axis=-1, keepdims=True)
        if has_res:
            dx = dx + res_ref[...]
        dx_ref[...] = dx
        _acc(dg_ref, jnp.sum(dh * xv * r, axis=0, keepdims=True), pl.program_id(0) == 0)

    row = pl.BlockSpec((tm, d), lambda i: (i, 0))
    vec = pl.BlockSpec((1, d), lambda i: (0, 0))
    ins = [x] + list(dhs) + [g] + ([res] if has_res else [])
    return pl.pallas_call(
        body, name=name, grid=(n // tm,),
        in_specs=[row] * (1 + n_dh) + [vec] + ([row] if has_res else []),
        out_specs=[row, vec],
        out_shape=[jax.ShapeDtypeStruct((n, d), F32), jax.ShapeDtypeStruct((1, d), F32)],
        compiler_params=_params(1),
    )(*ins)


def _adamw(w, g, m, v, *, name, tr=256):
    _, r, c = w.shape
    c1 = 1.0 / (1.0 - ADAM_B1 ** ADAM_STEP)
    c2 = 1.0 / (1.0 - ADAM_B2 ** ADAM_STEP)

    def body(w_ref, g_ref, m_ref, v_ref, d_ref, mo_ref, vo_ref):
        gv = g_ref[...]
        mn = ADAM_B1 * m_ref[...] + (1.0 - ADAM_B1) * gv
        vn = ADAM_B2 * v_ref[...] + (1.0 - ADAM_B2) * (gv * gv)
        d_ref[...] = -ADAM_LR * ((mn * c1) / (jnp.sqrt(vn * c2) + ADAM_EPS) + ADAM_WD * w_ref[...])
        mo_ref[...] = mn
        vo_ref[...] = vn

    if r % 8 == 0 or r < 8:
        tr = _tile(r, tr, 8)
        grid, blk = (r // tr,), pl.BlockSpec((1, tr, c), lambda i: (0, i, 0))
    else:
        tc = _tile(c, tr)
        grid, blk = (c // tc,), pl.BlockSpec((1, r, tc), lambda i: (0, 0, i))
    sds = jax.ShapeDtypeStruct((1, r, c), F32)
    return pl.pallas_call(
        body, name=name, grid=grid, in_specs=[blk] * 4, out_specs=[blk] * 3, out_shape=[sds] * 3,
        compiler_params=_params(1),
    )(w, g, m, v)


def _bdot(a, b, ca, cb):
    return lax.dot_general(_mx(a), _mx(b), (((ca,), (cb,)), ((0,), (0,))), preferred_element_type=F32)


def _split2(x):
    hi = x.astype(BF16)
    return hi, (x - hi.astype(F32)).astype(BF16)


def _bdotp(a, b, ca, cb):
    def d(u, v):
        return lax.dot_general(u, v, (((ca,), (cb,)), ((0,), (0,))), preferred_element_type=F32)

    return d(a[0], b[0]) + d(a[0], b[1]) + d(a[1], b[0])


def _tri_dot_b(tri_bf, x):
    hi = x.astype(BF16)
    r = x - hi.astype(F32)
    mid = r.astype(BF16)
    lo = (r - mid.astype(F32)).astype(BF16)

    def d(v):
        return lax.dot_general(tri_bf, v, (((2,), (1,)), ((0,), (0,))), preferred_element_type=F32)

    return d(hi) + d(mid) + d(lo)


def _hgrn_forward(hq, hf, hi, lbv, tril, tril_bf):
    nc, c, _ = hq.shape
    sf = _sig(hf)
    f = lbv + (1.0 - lbv) * sf
    k = 1.0 - f
    gcum = _tri_dot_b(tril_bf, jnp.log(f))
    mid = gcum[:, c // 2 - 1:c // 2, :]
    glast = gcum[:, c - 1:c, :]
    sq = _sig(hq)
    q = hq * sq
    e_q = jnp.exp(gcum - mid)
    e_k = jnp.exp(mid - gcum)
    qe, ke = q * e_q, k * e_k
    a = jnp.where(tril, _bdot(qe, ke, 2, 2), 0.0)
    e_g = jnp.exp(gcum)
    qg = q * e_g
    e_s = jnp.exp(glast - gcum)
    kg = k * e_s
    e_l = jnp.exp(glast)
    upd = _bdot(hi, kg, 1, 1)
    st = jnp.zeros((HG_D, HG_D), F32)
    states = []
    for n in range(nc):
        states.append(st)
        st = st * e_l[n] + upd[n]
    st_all = jnp.stack(states)
    o = _bdot(a, hi, 2, 1) + _bdot(qg, st_all, 2, 2)
    return dict(sf=sf, f=f, k=k, sq=sq, q=q, e_q=e_q, e_k=e_k, qe=qe, ke=ke, a=a, e_g=e_g, qg=qg, o=o,
                e_s=e_s, kg=kg, e_l=e_l, st_all=st_all)


def _hgrn_specs(t_, col0):
    def col(off):
        return pl.BlockSpec((1, t_, LANE), lambda h, b: (b, 0, col0 + 4 * h + off))

    vec = pl.BlockSpec((2, LANE), lambda h, b: (0, h))
    one = pl.BlockSpec((1, LANE), lambda h, b: (0, 0))
    blk = pl.BlockSpec((1, t_, LANE), lambda h, b: (b, 0, h))
    return col, vec, one, blk


def _chunk_masks(nc, c):
    row = lax.broadcasted_iota(jnp.int32, (nc, c, c), 1)
    cl = lax.broadcasted_iota(jnp.int32, (nc, c, c), 2)
    return row >= cl, (row >= cl).astype(BF16), (row <= cl).astype(BF16)


def _hgrn_fwd(zm, lb, gn, hw, col0):
    b_, t_, _ = zm.shape
    c = min(HG_CHUNK, t_)
    nc = t_ // c
    col, vec, one, blk = _hgrn_specs(t_, col0)

    def body(q_ref, f_ref, i_ref, g_ref, lb_ref, gn_ref, y_ref):
        lbv, gnv = _sig(lb_ref[0:1, :] - lb_ref[1:2, :]), gn_ref[...]
        tril, tril_bf, _ = _chunk_masks(nc, c)
        chunks = lambda ref: ref[0].reshape(nc, c, LANE)
        o = _hgrn_forward(chunks(q_ref), chunks(f_ref), chunks(i_ref), lbv, tril, tril_bf)["o"]
        r = lax.rsqrt(jnp.mean(o * o, axis=-1, keepdims=True) + EPS)
        hg = chunks(g_ref)
        y_ref[0] = (o * r * gnv * (hg * _sig(hg))).reshape(t_, LANE)

    return pl.pallas_call(
        body, name="hgrn_fwd", grid=(HG_HEADS, b_),
        in_specs=[col(0), col(1), col(2), col(3), vec, one], out_specs=blk,
        out_shape=jax.ShapeDtypeStruct((b_, t_, hw), F32),
        compiler_params=_params(2),
    )(zm, zm, zm, zm, lb, gn)


def _hgrn_bwd(zm, dy, lb, gn, hw, col0, dz, rider=None):
    b_, t_, _ = zm.shape
    c = min(HG_CHUNK, t_)
    nc = t_ // c
    col, vec, one, blk = _hgrn_specs(t_, col0)

    def body(q_ref, f_ref, i_ref, g_ref, dy_ref, lb_ref, gn_ref, _, dz_ref, dlb_ref, dgn_ref):
        h, b = pl.program_id(0), pl.program_id(1)
        lbv, gnv = _sig(lb_ref[0:1, :] - lb_ref[1:2, :]), gn_ref[...]
        tril, tril_bf, triu_bf = _chunk_masks(nc, c)
        last_row = lax.broadcasted_iota(jnp.int32, (nc, c, LANE), 1) == c - 1
        chunks = lambda ref: ref[0].reshape(nc, c, LANE)
        flat = lambda x: x.reshape(t_, LANE)
        hq, hi, hg = chunks(q_ref), chunks(i_ref), chunks(g_ref)
        p = _hgrn_forward(hq, chunks(f_ref), hi, lbv, tril, tril_bf)
        o, q, k, st_all, e_l = p["o"], p["q"], p["k"], p["st_all"], p["e_l"]
        dyv = chunks(dy_ref)
        sg = _sig(hg)
        r = lax.rsqrt(jnp.mean(o * o, axis=-1, keepdims=True) + EPS)
        dn = dyv * (hg * sg)
        dz_ref[0, :, 3 * LANE:] = flat(dyv * (o * r * gnv) * (sg * (1.0 + hg * (1.0 - sg)))).astype(dz_ref.dtype)
        dgn = jnp.sum(flat(dn * o * r), axis=0, keepdims=True)
        dng = dn * gnv
        do = r * dng - o * (r * r * r) * jnp.mean(dng * o, axis=-1, keepdims=True)
        do2, hi2, qg2, ke2, qe2, st2 = (_split2(t) for t in (do, hi, p["qg"], p["ke"], p["qe"], st_all))
        back = _bdotp(do2, qg2, 1, 1)
        dst = jnp.zeros((HG_D, HG_D), F32)
        dsts = [None] * nc
        for n in range(nc - 1, -1, -1):
            dsts[n] = dst
            dst = dst * e_l[n] + back[n]
        dst_all = jnp.stack(dsts)
        da = jnp.where(tril, _bdotp(do2, hi2, 2, 2), 0.0)
        da2 = _split2(da)
        dq = _bdotp(da2, ke2, 2, 1) * p["e_q"] + _bdotp(do2, st2, 2, 1) * p["e_g"]
        dk_state = _bdotp(hi2, _split2(dst_all), 2, 1) * p["e_s"]
        dk = _bdotp(da2, qe2, 1, 1) * p["e_k"] + dk_state
        dz_ref[0, :, 2 * LANE:3 * LANE] = flat(_bdot(p["a"], do, 1, 1) + _bdot(p["kg"], dst_all, 2, 2)).astype(dz_ref.dtype)
        extra = (jnp.sum(k * dk_state, axis=1, keepdims=True) + e_l * jnp.sum(st_all * dst_all, axis=1, keepdims=True))
        dgc = q * dq - k * dk + jnp.where(last_row, extra, 0.0)
        dfv = _tri_dot_b(triu_bf, dgc) / p["f"] - dk
        sf, sq = p["sf"], p["sq"]
        dz_ref[0, :, LANE:2 * LANE] = flat(dfv * (1.0 - lbv) * sf * (1.0 - sf)).astype(dz_ref.dtype)
        dlb = jnp.sum(flat(dfv * (1.0 - sf)), axis=0, keepdims=True)
        dz_ref[0, :, :LANE] = flat(dq * (sq * (1.0 + hq * (1.0 - sq)))).astype(dz_ref.dtype)
        dl0 = dlb * lbv * (1.0 - lbv)
        _acc(dlb_ref, jnp.concatenate([dl0, -dl0], axis=0), b == 0)
        _acc(dgn_ref, dgn, jnp.logical_and(b == 0, h == 0))

    return _ride_call(
        body, rider, name="hgrn_bwd", grid=(HG_HEADS, b_),
        in_specs=[col(0), col(1), col(2), col(3), blk, vec, one, ANY],
        out_specs=[pl.BlockSpec((1, t_, 4 * LANE), lambda h, b: (b, 0, col0 // 4 + h)), vec, one],
        out_shape=[jax.ShapeDtypeStruct(dz.shape, dz.dtype), jax.ShapeDtypeStruct((2, hw), F32),
                   jax.ShapeDtypeStruct((1, LANE), F32)],
        scratch=[], args=(zm, zm, zm, zm, dy, lb, gn, dz), aliases={7: 0})


def _fox_logf(x):
    return jnp.minimum(x, 0.0) - jnp.log(1.0 + jnp.exp(-jnp.abs(x)))


def _fox_prep(zf, bias):
    b_, t_, _ = zf.shape
    tb = min(FOX_BLOCK, t_)
    nb = t_ // tb

    def body(z_ref, b_ref, fc_ref):
        tril_bf = (lax.broadcasted_iota(jnp.int32, (tb, tb), 0) >= lax.broadcasted_iota(jnp.int32, (tb, tb), 1)).astype(BF16)
        bv = b_ref[...]

        def blk(i, carry):
            rows = pl.ds(pl.multiple_of(i * tb, tb), tb)
            fc = _tri_dot(tril_bf, _fox_logf(z_ref[0, rows, :] + bv)) + carry
            fc_ref[0, rows, :] = fc
            return fc[tb - 1:tb, :]

        lax.fori_loop(0, nb, blk, jnp.zeros((1, LANE), F32))

    blk_spec = pl.BlockSpec((1, t_, LANE), lambda b: (b, 0, 0))
    return pl.pallas_call(
        body, name="fox_prep", grid=(b_,),
        in_specs=[blk_spec, pl.BlockSpec((1, LANE), lambda b: (0, 0))], out_specs=blk_spec,
        out_shape=jax.ShapeDtypeStruct((b_, t_, LANE), F32), compiler_params=_params(1),
    )(zf, bias)


def _fox_post(dfc, zf, bias):
    b_, t_, _ = zf.shape
    npair = dfc.shape[1]
    tb = min(FOX_BLOCK, t_)
    nb = t_ // tb

    def body(d_ref, z_ref, b_ref, dz_ref, db_ref):
        triu_bf = (lax.broadcasted_iota(jnp.int32, (tb, tb), 0) <= lax.broadcasted_iota(jnp.int32, (tb, tb), 1)).astype(BF16)
        valid = lax.broadcasted_iota(jnp.int32, (tb, LANE), 1) < FOX_HEADS
        bv = b_ref[...]

        def blk(m, carry):
            tail, db = carry
            rows = pl.ds(pl.multiple_of((nb - 1 - m) * tb, tb), tb)
            dfc_rows = d_ref[0, 0, rows, :]
            for p in range(1, npair):
                dfc_rows = dfc_rows + pltpu.roll(d_ref[0, p, rows, :], 2 * p, 1)
            dlf = _tri_dot(triu_bf, dfc_rows) + tail
            dx = jnp.where(valid, dlf * _sig(-(z_ref[0, rows, :] + bv)), 0.0)
            dz_ref[0, rows, :] = dx.astype(dz_ref.dtype)
            return dlf[0:1, :], db + jnp.sum(dx, axis=0, keepdims=True)

        z1 = jnp.zeros((1, LANE), F32)
        _, db = lax.fori_loop(0, nb, blk, (z1, z1))
        _acc(db_ref, db, pl.program_id(0) == 0)

    blk_spec = pl.BlockSpec((1, t_, LANE), lambda b: (b, 0, 0))
    vec = pl.BlockSpec((1, LANE), lambda b: (0, 0))
    return pl.pallas_call(
        body, name="fox_post", grid=(b_,),
        in_specs=[pl.BlockSpec((1, npair, t_, LANE), lambda b: (b, 0, 0, 0)), blk_spec, vec], out_specs=[blk_spec, vec],
        out_shape=[jax.ShapeDtypeStruct((b_, t_, LANE), MXU_DTYPE), jax.ShapeDtypeStruct((1, LANE), F32)],
        compiler_params=_params(1),
    )(dfc, zf, bias)


FOX_TILE = 512
FOX_TILE_FWD = 512
FOX_BAND = 512
AUG = 64


def _head_mean_matrix():
    r = lax.broadcasted_iota(jnp.int32, (LANE, LANE), 0) // FOX_DH
    c = lax.broadcasted_iota(jnp.int32, (LANE, LANE), 1) // FOX_DH
    return (r == c).astype(BF16)


def _dot_right_exact(x, m_bf):
    hi = x.astype(BF16)
    lo = (x - hi.astype(F32)).astype(BF16)

    def d(v):
        return lax.dot_general(v, m_bf, (((1,), (0,)), ((), ())), preferred_element_type=F32)

    return d(hi) + d(lo)


def _pair_norm(x, g2, bd):
    r = lax.rsqrt(_dot_right_exact(x * x, bd) * (1.0 / FOX_DH) + EPS)
    return x * r * g2, r


def _pair_norm_bwd(x, r, dy, g2, bd):
    dyg = dy * g2
    dx = r * dyg - x * (r * r * r) * (_dot_right_exact(dyg * x, bd) * (1.0 / FOX_DH))
    return dx, jnp.sum(dy * x * r, axis=0, keepdims=True)


def _head_lanes(xn, hh):
    return xn if hh == 0 else pltpu.roll(xn, FOX_DH, 1)


def _split3(x):
    hi = x.astype(BF16).astype(F32)
    mid = (x - hi).astype(BF16).astype(F32)
    return hi, mid, x - hi - mid


def _fox_operands(q_ref, k_ref, v_ref, fc_ref, gq2, gk2, p, qa, ka, va):
    t_ = q_ref.shape[1]
    bd = _head_mean_matrix()
    lane = lax.broadcasted_iota(jnp.int32, (t_, LANE), 1)
    qx, kx = q_ref[0], k_ref[0]
    qn, rq = _pair_norm(qx, gq2, bd)
    kn, rk = _pair_norm(kx, gk2, bd)
    vv = v_ref[0]
    q_aug = jnp.where(jnp.logical_and(lane >= AUG, lane < AUG + 3), 1.0, 0.0)
    for hh in range(2):
        fcol = jnp.sum(jnp.where(lane == 2 * p + hh, fc_ref[0], 0.0), axis=-1, keepdims=True)
        hi, mid, lo = _split3(-fcol)
        k_aug = jnp.where(lane == AUG, hi, jnp.where(lane == AUG + 1, mid, jnp.where(lane == AUG + 2, lo,
                          jnp.where(lane == AUG + 3, 1.0, 0.0))))
        head = lane < FOX_DH
        qa[hh] = jnp.where(head, _head_lanes(qn, hh), q_aug).astype(MXU_DTYPE)
        ka[hh] = jnp.where(head, _head_lanes(kn, hh), k_aug).astype(MXU_DTYPE)
        va[hh] = jnp.where(head, _head_lanes(vv, hh), 0.0).astype(MXU_DTYPE)
    return bd, lane, qx, kx, rq, rk


def _fox_specs(t_, fw, col0):
    npair = fw // LANE

    def col(off):
        return pl.BlockSpec((1, t_, LANE), lambda b, p: (b, 0, col0 + 3 * p + off))

    pair = pl.BlockSpec((1, t_, LANE), lambda b, p: (b, 0, p))
    full = pl.BlockSpec((1, t_, LANE), lambda b, p: (b, 0, 0))
    gvec = pl.BlockSpec((1, LANE), lambda b, p: (0, 0))
    lse = pl.BlockSpec((1, 1, t_, LANE), lambda b, p: (b, p, 0, 0))
    return col, pair, full, gvec, lse


def _fox_fwd(zm, fc, gq2, gk2, fw, col0, rider=None):
    b_, t_, _ = zm.shape
    npair = fw // LANE
    tq = min(FOX_TILE_FWD, t_)
    bw = min(FOX_BAND, t_)
    nband, tpb = t_ // bw, bw // tq
    scale = FOX_DH ** -0.5
    col, pair, full, gvec, lse_spec = _fox_specs(t_, fw, col0)

    def body(q_ref, k_ref, v_ref, fc_ref, gq_ref, gk_ref, o_ref, lse_ref, qa, ka, va):
        p = pl.program_id(1)
        _fox_operands(q_ref, k_ref, v_ref, fc_ref, gq_ref[...] * scale, gk_ref[...], p, qa, ka, va)
        ahead = lax.broadcasted_iota(jnp.int32, (tq, bw), 1) - lax.broadcasted_iota(jnp.int32, (tq, bw), 0)
        lane = lax.broadcasted_iota(jnp.int32, (tq, LANE), 1)

        for band in range(nband):
            c0 = band * bw

            def qtile(ii, _, c0=c0):
                r0 = pl.multiple_of(c0 + ii * tq, tq)
                rows = pl.ds(r0, tq)
                keep = ahead <= r0 - c0
                res = []
                for hh in range(2):
                    qb = qa[hh, rows, :]
                    s_b = jnp.where(keep, _nt(qb, ka[hh, c0:c0 + bw, :]), NEG)
                    m = jnp.max(s_b, axis=-1, keepdims=True)
                    if c0:
                        s_a = _nt(qb, ka[hh, 0:c0, :])
                        m = jnp.maximum(m, jnp.max(s_a, axis=-1, keepdims=True))
                    p_b = jnp.exp(s_b - m)
                    l = jnp.sum(p_b, axis=-1, keepdims=True)
                    acc = _nn(p_b, va[hh, c0:c0 + bw, :])
                    if c0:
                        p_a = jnp.exp(s_a - m)
                        l = l + jnp.sum(p_a, axis=-1, keepdims=True)
                        acc = acc + _nn(p_a, va[hh, 0:c0, :])
                    res.append((acc / l, m + jnp.log(l)))
                (o0, e0), (o1, e1) = res
                o_ref[0, rows, :] = jnp.where(lane < FOX_DH, o0, pltpu.roll(o1, FOX_DH, 1))
                lse_ref[0, 0, rows, :] = jnp.where(lane == 0, e0, jnp.where(lane == 1, e1, 0.0))
                return 0

            lax.fori_loop(0, tpb, qtile, 0)

    return _ride_call(
        body, rider, name="fox_fwd", grid=(b_, npair),
        in_specs=[col(0), col(1), col(2), full, gvec, gvec],
        out_specs=[pair, lse_spec],
        out_shape=[jax.ShapeDtypeStruct((b_, t_, fw), F32), jax.ShapeDtypeStruct((b_, npair, t_, LANE), F32)],
        scratch=[pltpu.VMEM((2, t_, LANE), MXU_DTYPE)] * 3, args=(zm, zm, zm, fc, gq2, gk2))


def _norm_bwd(x, dy, g):
    r = lax.rsqrt(jnp.mean(x * x, axis=-1, keepdims=True) + EPS)
    dyg = dy * g
    dx = r * dyg - x * (r * r * r) * jnp.mean(dyg * x, axis=-1, keepdims=True)
    return dx, jnp.sum(dy * x * r, axis=0, keepdims=True)


def _fox_bwd(zm, o, do, lse, fc, gq2, gk2, fw, col0, dz, rider=None):
    b_, t_, _ = zm.shape
    npair = fw // LANE
    tq = min(FOX_TILE, t_)
    nb = t_ // tq
    bw = min(FOX_BAND, t_)
    nband, tpb = t_ // bw, bw // tq
    scale = FOX_DH ** -0.5
    col, pair, full, gvec, lse_spec = _fox_specs(t_, fw, col0)

    def body(q_ref, k_ref, v_ref, o_ref, do_ref, lse_ref, fc_ref, gq_ref, gk_ref, _,
             dz_ref, dfc_ref, dgq_ref, dgk_ref, qa, ka, va, da, rowv, dq_acc, dk_acc, dv_acc):
        b, p = pl.program_id(0), pl.program_id(1)
        gq2v, gk2v = gq_ref[...] * scale, gk_ref[...]
        bd, lane, qx, kx, rq, rk = _fox_operands(q_ref, k_ref, v_ref, fc_ref, gq2v, gk2v, p, qa, ka, va)
        head = lane < FOX_DH
        dov = do_ref[0]
        dsum = _dot_right_exact(dov * o_ref[0], bd)
        eye = (lax.broadcasted_iota(jnp.int32, (tq, tq), 0) == lax.broadcasted_iota(jnp.int32, (tq, tq), 1)).astype(F32)
        for hh in range(2):
            da[hh] = jnp.where(head, _head_lanes(dov, hh), 0.0).astype(MXU_DTYPE)
            for blk in range(nb):
                rs = slice(blk * tq, (blk + 1) * tq)
                rowv[2 * hh:2 * hh + 1, rs] = jnp.sum(eye * lse_ref[0, 0, rs, hh:hh + 1], axis=0, keepdims=True)
                rowv[2 * hh + 1:2 * hh + 2, rs] = jnp.sum(eye * dsum[rs, hh * FOX_DH:hh * FOX_DH + 1], axis=0, keepdims=True)
        dq_acc[...] = jnp.zeros(dq_acc.shape, F32)
        ahead = lax.broadcasted_iota(jnp.int32, (tq, bw), 1) - lax.broadcasted_iota(jnp.int32, (tq, bw), 0)

        def part(hh, kb, vb, lo, hi, keep):
            qm, dm = qa[hh, lo:hi, :], da[hh, lo:hi, :]
            pt = jnp.exp(_nt(kb, qm) - rowv[2 * hh:2 * hh + 1, lo:hi])
            if keep is not None:
                pt = jnp.where(keep, pt, 0.0)
            dst = pt * (_nt(vb, dm) - rowv[2 * hh + 1:2 * hh + 2, lo:hi])
            dq_acc[hh, lo:hi, :] += _tn(dst, kb)
            return _nn(dst, qm), _nn(pt, dm)

        for band in range(nband):
            c0 = band * bw

            def kvtile(jj, _, c0=c0):
                r0 = pl.multiple_of(c0 + jj * tq, tq)
                rows = pl.ds(r0, tq)
                keep = ahead >= r0 - c0
                for hh in range(2):
                    kb, vb = ka[hh, rows, :], va[hh, rows, :]
                    dk_t, dv_t = part(hh, kb, vb, c0, c0 + bw, keep)
                    if c0 + bw < t_:
                        dk_u, dv_u = part(hh, kb, vb, c0 + bw, t_, None)
                        dk_t, dv_t = dk_t + dk_u, dv_t + dv_u
                    dk_acc[hh, rows, :] = dk_t
                    dv_acc[hh, rows, :] = dv_t
                return 0

            lax.fori_loop(0, tpb, kvtile, 0)

        dq0, dq1, dk0, dk1 = dq_acc[0], dq_acc[1], dk_acc[0], dk_acc[1]
        dqn = jnp.where(head, dq0, pltpu.roll(dq1, FOX_DH, 1))
        dkn = jnp.where(head, dk0, pltpu.roll(dk1, FOX_DH, 1))
        dqx, gq_part = _pair_norm_bwd(qx, rq, dqn, gq2v, bd)
        dkx, gk_part = _pair_norm_bwd(kx, rk, dkn, gk2v, bd)
        dz_ref[0, :, :LANE] = dqx.astype(dz_ref.dtype)
        dz_ref[0, :, LANE:2 * LANE] = dkx.astype(dz_ref.dtype)
        dz_ref[0, :, 2 * LANE:] = jnp.where(head, dv_acc[0], pltpu.roll(dv_acc[1], FOX_DH, 1)).astype(dz_ref.dtype)

        def bias_grad(dqh, dkh):
            return dqh[:, AUG + 3:AUG + 4] - dkh[:, AUG:AUG + 1]

        dfc_ref[0, 0] = jnp.where(lane == 0, bias_grad(dq0, dk0), jnp.where(lane == 1, bias_grad(dq1, dk1), 0.0))
        first = jnp.logical_and(b == 0, p == 0)
        _acc(dgq_ref, gq_part * scale, first)
        _acc(dgk_ref, gk_part, first)

    gs = jax.ShapeDtypeStruct((1, LANE), F32)
    return _ride_call(
        body, rider, name="fox_bwd", grid=(b_, npair),
        in_specs=[col(0), col(1), col(2), pair, pair, lse_spec, full, gvec, gvec, ANY],
        out_specs=[pl.BlockSpec((1, t_, 3 * LANE), lambda b, p: (b, 0, col0 // 3 + p)), lse_spec, gvec, gvec],
        out_shape=[jax.ShapeDtypeStruct(dz.shape, dz.dtype), jax.ShapeDtypeStruct((b_, npair, t_, LANE), F32), gs, gs],
        scratch=[pltpu.VMEM((2, t_, LANE), MXU_DTYPE)] * 4
        + [pltpu.VMEM((8, t_), F32)] + [pltpu.VMEM((2, t_, LANE), F32)] * 3,
        args=(zm, zm, zm, o, do, lse, fc, gq2, gk2, dz), aliases={9: 0})


def _mem_specs(t_, m_, mw, col0):
    nh = mw // LANE
    qcol = pl.BlockSpec((1, t_, LANE), lambda b, h: (b, 0, col0 + h))
    kcol = pl.BlockSpec((1, m_, LANE), lambda b, h: (b, 0, h))
    vcol = pl.BlockSpec((1, m_, LANE), lambda b, h: (b, 0, nh + h))
    ycol = pl.BlockSpec((1, t_, LANE), lambda b, h: (b, 0, h))
    gvec = pl.BlockSpec((1, LANE), lambda b, h: (0, 0))
    return qcol, kcol, vcol, ycol, gvec


def _mem_fwd(zm, mkv, gq, gk, mw, col0):
    b_, t_, _ = zm.shape
    m_ = mkv.shape[1]
    tq = min(MEM_TILE, t_)
    nb = t_ // tq
    scale = MEM_DH ** -0.5
    qcol, kcol, vcol, ycol, gvec = _mem_specs(t_, m_, mw, col0)

    def body(q_ref, k_ref, v_ref, gq_ref, gk_ref, y_ref):
        gqv, gkv = gq_ref[...] * scale, gk_ref[...]
        kv = k_ref[0]
        kn = _mx(kv * lax.rsqrt(jnp.mean(kv * kv, axis=-1, keepdims=True) + EPS) * gkv)
        vv = _mx(v_ref[0])

        def blk(i, _):
            rows = pl.ds(pl.multiple_of(i * tq, tq), tq)
            qv = q_ref[0, rows, :]
            s = _nt(qv * lax.rsqrt(jnp.mean(qv * qv, axis=-1, keepdims=True) + EPS) * gqv, kn)
            e = jnp.exp(s - jnp.max(s, axis=-1, keepdims=True))
            y_ref[0, rows, :] = _nn(e / jnp.sum(e, axis=-1, keepdims=True), vv)
            return 0

        lax.fori_loop(0, nb, blk, 0)

    return pl.pallas_call(
        body, name="mem_fwd", grid=(b_, MEM_HEADS), in_specs=[qcol, kcol, vcol, gvec, gvec], out_specs=ycol,
        out_shape=jax.ShapeDtypeStruct((b_, t_, mw), F32), compiler_params=_params(2),
    )(zm, mkv, mkv, gq, gk)


def _mem_bwd(zm, mkv, dy, gq, gk, mw, col0, dz):
    b_, t_, _ = zm.shape
    m_ = mkv.shape[1]
    tq = min(MEM_TILE, t_)
    nb = t_ // tq
    scale = MEM_DH ** -0.5
    qcol, kcol, vcol, ycol, gvec = _mem_specs(t_, m_, mw, col0)

    def body(q_ref, k_ref, v_ref, dy_ref, gq_ref, gk_ref, _, dq_ref, dk_ref, dv_ref, dgq_ref, dgk_ref):
        gqv, gkv = gq_ref[...] * scale, gk_ref[...]
        kv = k_ref[0]
        kn = _mx(kv * lax.rsqrt(jnp.mean(kv * kv, axis=-1, keepdims=True) + EPS) * gkv)
        vv = _mx(v_ref[0])

        def blk(i, carry):
            dkn, dvv, dgq = carry
            rows = pl.ds(pl.multiple_of(i * tq, tq), tq)
            qv = q_ref[0, rows, :]
            qn = _mx(qv * lax.rsqrt(jnp.mean(qv * qv, axis=-1, keepdims=True) + EPS) * gqv)
            s = _nt(qn, kn)
            e = jnp.exp(s - jnp.max(s, axis=-1, keepdims=True))
            pm = e / jnp.sum(e, axis=-1, keepdims=True)
            dob = _mx(dy_ref[0, rows, :])
            dp = _nt(dob, vv)
            ds = pm * (dp - jnp.sum(dp * pm, axis=-1, keepdims=True))
            dqv, gq_part = _norm_bwd(qv, _nn(ds, kn), gqv)
            dq_ref[0, rows, :] = dqv.astype(dq_ref.dtype)
            return dkn + _tn(ds, qn), dvv + _tn(pm, dob), dgq + gq_part * scale

        z = jnp.zeros((m_, LANE), F32)
        dkn, dvv, dgq = lax.fori_loop(0, nb, blk, (z, z, jnp.zeros((1, LANE), F32)))
        dkv, dgk = _norm_bwd(kv, dkn, gkv)
        dk_ref[0] = dkv
        dv_ref[0] = dvv
        first = jnp.logical_and(pl.program_id(0) == 0, pl.program_id(1) == 0)
        _acc(dgq_ref, dgq, first)
        _acc(dgk_ref, dgk, first)

    kblk = pl.BlockSpec((1, m_, LANE), lambda b, h: (b, 0, h))
    gs = jax.ShapeDtypeStruct((1, LANE), F32)
    ks = jax.ShapeDtypeStruct((b_, m_, mw), F32)
    return pl.pallas_call(
        body, name="mem_bwd", grid=(b_, MEM_HEADS), in_specs=[qcol, kcol, vcol, ycol, gvec, gvec, ANY],
        out_specs=[qcol, kblk, kblk, gvec, gvec],
        out_shape=[jax.ShapeDtypeStruct(dz.shape, dz.dtype), ks, ks, gs, gs], input_output_aliases={6: 0},
        compiler_params=_params(2),
    )(zm, mkv, mkv, dy, gq, gk, dz)


def _merge_specs(tm, d, w, gcol):
    row_d = pl.BlockSpec((tm, d), lambda i: (i, 0))
    row_w = pl.BlockSpec((tm, w), lambda i: (i, 0))
    gates = [pl.BlockSpec((tm, d), functools.partial(lambda i, k: (i, gcol + k), k=k)) for k in range(3)]
    w_br = pl.BlockSpec((w, d), lambda i: (0, 0))
    w_o = pl.BlockSpec((d, d), lambda i: (0, 0))
    return row_d, row_w, gates, w_br, w_o


def _merge_fwd(x, ys, zm, w_brs, w_out, gcol, g_next, tm=256):
    n, d = x.shape
    w = ys[0].shape[1]
    tm = _tile(n, tm, 8)
    row_d, row_w, gates, w_br, w_o = _merge_specs(tm, d, w, gcol)

    def body(x_ref, ya, yb, yc, g0, g1, g2, wa, wb, wc, wo, gn_ref, x1_ref, mg_ref, h_ref):
        mg = (_sig(g0[...]) * _nn(ya[...], wa[...]) + _sig(g1[...]) * _nn(yb[...], wb[...])
              + _sig(g2[...]) * _nn(yc[...], wc[...]))
        mg_ref[...] = mg.astype(mg_ref.dtype)
        x1 = x_ref[...] + _nn(mg, wo[...])
        x1_ref[...] = x1
        h_ref[...] = (x1 * lax.rsqrt(jnp.mean(x1 * x1, axis=-1, keepdims=True) + EPS) * gn_ref[...]).astype(h_ref.dtype)

    half = jax.ShapeDtypeStruct((n, d), MXU_DTYPE)
    return pl.pallas_call(
        body, name="merge_fwd", grid=(n // tm,),
        in_specs=[row_d, row_w, row_w, row_w] + gates + [w_br, w_br, w_br, w_o, pl.BlockSpec((1, d), lambda i: (0, 0))],
        out_specs=[row_d, row_d, row_d],
        out_shape=[jax.ShapeDtypeStruct((n, d), F32), half, half],
        compiler_params=_params(1),
    )(x, *ys, zm, zm, zm, *w_brs, w_out, g_next)


def _merge_bwd(dx1, ys, zm, w_brs, w_out, gcol, tm=256):
    n, d = dx1.shape
    w = ys[0].shape[1]
    tm = _tile(n, tm, 8)
    row_d, row_w, gates, w_br, w_o = _merge_specs(tm, d, w, gcol)

    def body(dx_ref, ya, yb, yc, g0, g1, g2, wa, wb, wc, wo, dgl_ref, dpa, dpb, dpc, dya, dyb, dyc):
        dm = _nt(dx_ref[...], wo[...])
        for k, (y, g, wr, dp_ref, dy_ref) in enumerate(((ya, g0, wa, dpa, dya), (yb, g1, wb, dpb, dyb),
                                                        (yc, g2, wc, dpc, dyc))):
            sg = _sig(g[...])
            pr = _nn(y[...], wr[...])
            dgl_ref[:, k * d:(k + 1) * d] = (dm * pr * sg * (1.0 - sg)).astype(dgl_ref.dtype)
            dp = (dm * sg).astype(dp_ref.dtype)
            dp_ref[...] = dp
            dy_ref[...] = _nt(dp, wr[...])

    sd = jax.ShapeDtypeStruct((n, d), MXU_DTYPE)
    sw = jax.ShapeDtypeStruct((n, w), F32)
    return pl.pallas_call(
        body, name="merge_bwd", grid=(n // tm,),
        in_specs=[row_d, row_w, row_w, row_w] + gates + [w_br, w_br, w_br, w_o],
        out_specs=[pl.BlockSpec((tm, 3 * d), lambda i: (i, 0)), row_d, row_d, row_d, row_w, row_w, row_w],
        out_shape=[jax.ShapeDtypeStruct((n, zm.shape[1]), MXU_DTYPE), sd, sd, sd, sw, sw, sw],
        compiler_params=_params(1),
    )(dx1, *ys, zm, zm, zm, *w_brs, w_out)


CONV_ROWS = 512
HALO = 8


def _ext(ref, r0, t_):
    rc = min(CONV_ROWS, t_)
    a, b = max(r0 - HALO, 0), min(r0 + rc + HALO, t_)
    parts = []
    if r0 - HALO < 0:
        parts.append(jnp.zeros((HALO, ref.shape[2]), F32))
    parts.append(ref[0, a:b, :].astype(F32))
    if r0 + rc + HALO > t_:
        parts.append(jnp.zeros((HALO, ref.shape[2]), F32))
    return jnp.concatenate(parts, axis=0) if len(parts) > 1 else parts[0]


def _gelu_parts(ac):
    e = jnp.exp(-0.5 * ac * ac)
    t = 1.0 / (1.0 + (0.3275911 * 2.0 ** -0.5) * jnp.abs(ac))
    tail = (0.5 * e) * (t * (0.254829592 + t * (-0.284496736 + t * (1.421413741 + t * (-1.453152027 + t * 1.061405429)))))
    return jnp.where(ac < 0, tail, 1.0 - tail), e * ((2.0 * math.pi) ** -0.5)


def _conv_taps(a_ext, cw, cb):
    a2, a1 = pltpu.roll(a_ext, 2, 0), pltpu.roll(a_ext, 1, 0)
    return cw[0:1, :] * a2 + cw[1:2, :] * a1 + cw[2:3, :] * a_ext + cb, a2, a1


def _glu_specs(t_, f, g):
    gate = pl.BlockSpec((1, t_, g), lambda j, b: (b, 0, j))
    value = pl.BlockSpec((1, t_, g), lambda j, b: (b, 0, f // g + j))
    cwb = pl.BlockSpec((3, g), lambda j, b: (0, j))
    cbb = pl.BlockSpec((1, g), lambda j, b: (0, j))
    return gate, value, cwb, cbb


def _glu_fwd(u, cw, cb):
    b_, t_, f2 = u.shape
    f = f2 // 2
    g = min(FFN_GROUP, f)
    rc = min(CONV_ROWS, t_)
    gate, value, cwb, cbb = _glu_specs(t_, f, g)

    def body(a_ref, v_ref, cw_ref, cb_ref, y_ref):
        cwv, cbv = cw_ref[...], cb_ref[...]
        for r0 in range(0, t_, rc):
            ac = _conv_taps(_ext(a_ref, r0, t_), cwv, cbv)[0][HALO:HALO + rc]
            cdf, _ = _gelu_parts(ac)
            y_ref[0, r0:r0 + rc, :] = (ac * cdf * v_ref[0, r0:r0 + rc, :]).astype(y_ref.dtype)

    return pl.pallas_call(
        body, name="glu_fwd", grid=(f // g, b_), in_specs=[gate, value, cwb, cbb], out_specs=gate,
        out_shape=jax.ShapeDtypeStruct((b_, t_, f), MXU_DTYPE), compiler_params=_params(2),
    )(u, u, cw, cb)


def _glu_bwd(u, dy, cw, cb):
    b_, t_, f2 = u.shape
    f = f2 // 2
    g = min(FFN_GROUP, f)
    rc = min(CONV_ROWS, t_)
    ne = rc + 2 * HALO
    gate, value, cwb, cbb = _glu_specs(t_, f, g)

    def body(a_ref, v_ref, dy_ref, cw_ref, cb_ref, da_ref, dv_ref, dcw_ref, dcb_ref):
        cwv, cbv = cw_ref[...], cb_ref[...]
        dcw = [jnp.zeros((1, g), F32) for _ in range(3)]
        dcb = jnp.zeros((1, g), F32)
        for r0 in range(0, t_, rc):
            a_ext, v_ext, dy_ext = _ext(a_ref, r0, t_), _ext(v_ref, r0, t_), _ext(dy_ref, r0, t_)
            ac, a2, a1 = _conv_taps(a_ext, cwv, cbv)
            cdf, pdf = _gelu_parts(ac)
            dac = dy_ext * v_ext * (cdf + ac * pdf)
            da = cwv[2:3, :] * dac + cwv[1:2, :] * pltpu.roll(dac, ne - 1, 0) + cwv[0:1, :] * pltpu.roll(dac, ne - 2, 0)
            mid = slice(HALO, HALO + rc)
            da_ref[0, r0:r0 + rc, :] = da[mid].astype(da_ref.dtype)
            dv_ref[0, r0:r0 + rc, :] = (dy_ext[mid] * ac[mid] * cdf[mid]).astype(dv_ref.dtype)
            dacm = dac[mid]
            dcw[0] = dcw[0] + jnp.sum(dacm * a2[mid], axis=0, keepdims=True)
            dcw[1] = dcw[1] + jnp.sum(dacm * a1[mid], axis=0, keepdims=True)
            dcw[2] = dcw[2] + jnp.sum(dacm * a_ext[mid], axis=0, keepdims=True)
            dcb = dcb + jnp.sum(dacm, axis=0, keepdims=True)
        first = pl.program_id(1) == 0
        _acc(dcw_ref, jnp.concatenate(dcw, axis=0), first)
        _acc(dcb_ref, dcb, first)

    sds = jax.ShapeDtypeStruct((b_, t_, f), MXU_DTYPE)
    return pl.pallas_call(
        body, name="glu_bwd", grid=(f // g, b_), in_specs=[gate, value, gate, cwb, cbb],
        out_specs=[gate, gate, cwb, cbb],
        out_shape=[sds, sds, jax.ShapeDtypeStruct((3, f), F32), jax.ShapeDtypeStruct((1, f), F32)],
        compiler_params=_params(2),
    )(u, u, dy, cw, cb)


def _place():
    x, y, c = lax.axis_index("x"), lax.axis_index("y"), lax.axis_index("c")
    chips = [(1 - x, y), (x, 1 - y), (1 - x, 1 - y)]
    return x, y, c, chips


def _remote(src, dst, send_sem, recv_sem, to):
    return pltpu.make_async_remote_copy(src_ref=src, dst_ref=dst, send_sem=send_sem, recv_sem=recv_sem,
                                        device_id=to, device_id_type=MESH)


STACK, COLS = "stack", "cols"


def _shard_ref(ref, kind, s, rows, c):
    if kind == COLS:
        cols = pl.ds(pl.multiple_of(s * c, LANE), c)
        return ref.at[:, cols] if rows is None else ref.at[rows, cols]
    return ref.at[s] if rows is None else ref.at[s, rows, :]


def _halves(c, half):
    mine = pl.ds(pl.multiple_of(c * half, 16), half)
    theirs = pl.ds(pl.multiple_of((1 - c) * half, 16), half)
    return mine, theirs


def _gather_parts(kinds):
    def first_copies(ins, outs, sems):
        x, y, c, chips = _place()
        me = 2 * x + y
        cps = []
        for i, (w_ref, o_ref, kind) in enumerate(zip(ins, outs, kinds)):
            r, cw = w_ref.shape
            mine, _ = _halves(c, r // 2)
            for j, chip in enumerate(chips):
                cps.append(_remote(w_ref.at[mine], _shard_ref(o_ref, kind, me, mine, cw), sems[0].at[6 * i + j],
                                   sems[1].at[6 * i + j], (*chip, c)))
        return cps

    def start(ins, outs, sems):
        for cp in first_copies(ins, outs, sems):
            cp.start()

    def finish(ins, outs, sems):
        x, y, c, chips = _place()
        sib = (x, y, 1 - c)
        passed = []
        for i, (w_ref, o_ref, kind) in enumerate(zip(ins, outs, kinds)):
            r, cw = w_ref.shape
            mine, _ = _halves(c, r // 2)
            for j, (px, py) in enumerate(chips):
                blk = _shard_ref(o_ref, kind, 2 * px + py, mine, cw)
                _remote(blk, blk, sems[0].at[6 * i + j], sems[1].at[6 * i + j], sib).wait_recv()
                passed.append(_remote(blk, blk, sems[0].at[6 * i + 3 + j], sems[1].at[6 * i + 3 + j], sib))
                passed[-1].start()
        for i, (w_ref, o_ref, kind) in enumerate(zip(ins, outs, kinds)):
            r, cw = w_ref.shape
            _, theirs = _halves(c, r // 2)
            for j, (px, py) in enumerate(chips):
                blk = _shard_ref(o_ref, kind, 2 * px + py, theirs, cw)
                _remote(blk, blk, sems[0].at[6 * i + 3 + j], sems[1].at[6 * i + 3 + j], sib).wait_recv()
        for cp in first_copies(ins, outs, sems) + passed:
            cp.wait_send()

    return start, finish


def _gather_shapes(shards, kinds):
    return [jax.ShapeDtypeStruct((a.shape[0], N_CHIPS * a.shape[1]) if k == COLS else (N_CHIPS,) + a.shape, a.dtype)
            for a, k in zip(shards, kinds)]


def _gather_sems(nw):
    return [pltpu.SemaphoreType.DMA((6 * nw,)), pltpu.SemaphoreType.DMA((6 * nw,))]


def _gather_shards(shards, kinds):
    nw = len(shards)
    start, finish = _gather_parts(kinds)

    def body(*refs):
        ins, outs, sems = refs[:nw], refs[nw:2 * nw], refs[2 * nw:]
        start(ins, outs, sems)
        finish(ins, outs, sems)

    return pl.pallas_call(
        body, name="gather_shards", in_specs=[ANY] * nw, out_specs=[ANY] * nw,
        out_shape=_gather_shapes(shards, kinds), scratch_shapes=_gather_sems(nw),
    )(*shards)


def _gather_rider(shards, kinds):
    start, finish = _gather_parts(kinds)
    return _Rider(list(shards), _gather_shapes(shards, kinds), _gather_sems(len(shards)), start, finish)


def _half_shape(g, kind):
    if kind == COLS:
        return (g.shape[0] // 2, g.shape[1])
    return (g.shape[0], g.shape[1] // 2, g.shape[2])


def _swap_parts(kinds):
    def copies(ins, outs, sems):
        x, y, c, _ = _place()
        cps = []
        for i, (g_ref, a_ref, kind) in enumerate(zip(ins, outs, kinds)):
            r = g_ref.shape[0] if kind == COLS else g_ref.shape[1]
            _, theirs = _halves(c, r // 2)
            src = g_ref.at[theirs] if kind == COLS else g_ref.at[:, theirs]
            cps.append(_remote(src, a_ref, sems[0].at[i], sems[1].at[i], (x, y, 1 - c)))
        return cps

    def start(ins, outs, sems):
        for cp in copies(ins, outs, sems):
            cp.start()

    def finish(ins, outs, sems):
        for cp in copies(ins, outs, sems):
            cp.wait()

    return start, finish


def _swap_shapes(gs, kinds):
    return [jax.ShapeDtypeStruct(_half_shape(g, k), g.dtype) for g, k in zip(gs, kinds)]


def _pair_swap_halves(gs, kinds, name):
    nw = len(gs)
    start, finish = _swap_parts(kinds)

    def body(*refs):
        ins, outs, sems = refs[:nw], refs[nw:2 * nw], refs[2 * nw:]
        start(ins, outs, sems)
        finish(ins, outs, sems)

    return pl.pallas_call(
        body, name=name, in_specs=[ANY] * nw, out_specs=[ANY] * nw, out_shape=_swap_shapes(gs, kinds),
        scratch_shapes=[pltpu.SemaphoreType.DMA((nw,)), pltpu.SemaphoreType.DMA((nw,))],
    )(*gs)


def _swap_rider(gs, kinds):
    start, finish = _swap_parts(kinds)
    nw = len(gs)
    return _Rider(list(gs), _swap_shapes(gs, kinds), [pltpu.SemaphoreType.DMA((nw,)), pltpu.SemaphoreType.DMA((nw,))],
                  start, finish)


def _row_tile(rows, width, itemsize=4, target=2 ** 21):
    return _tile(rows, max(8, target // (width * itemsize)), 8)


def _add_half(g, a, kind, c_idx, name):
    if kind == COLS:
        half, wd = a.shape
        tr = _row_tile(half, wd)
        nblk = half // tr
        grid = (nblk,)
        g_spec = pl.BlockSpec((tr, wd), lambda i, c_ref: (c_ref[0] * nblk + i, 0))
        a_spec = pl.BlockSpec((tr, wd), lambda i, c_ref: (i, 0))
    else:
        n, half, wd = a.shape
        tr = _row_tile(half, wd)
        nblk = half // tr
        grid = (n, nblk)
        g_spec = pl.BlockSpec((1, tr, wd), lambda s, i, c_ref: (s, c_ref[0] * nblk + i, 0))
        a_spec = pl.BlockSpec((1, tr, wd), lambda s, i, c_ref: (s, i, 0))

    def body(c_ref, g_ref, a_ref, o_ref):
        o_ref[...] = (g_ref[...] + a_ref[...]).astype(o_ref.dtype)

    return pl.pallas_call(
        body, name=name,
        grid_spec=pltpu.PrefetchScalarGridSpec(num_scalar_prefetch=1, grid=grid, in_specs=[g_spec, a_spec],
                                               out_specs=a_spec),
        out_shape=jax.ShapeDtypeStruct(a.shape, EXCHANGE_DTYPE), compiler_params=_params(len(grid)),
    )(c_idx, g, a)


def _exchange_parts(kinds):
    def copies(ins, outs, sems):
        x, y, c, chips = _place()
        me = 2 * x + y
        cps = []
        for i, (p_ref, b_ref, kind) in enumerate(zip(ins, outs, kinds)):
            cw = b_ref.shape[2]
            for j, (px, py) in enumerate(chips):
                cps.append(_remote(_shard_ref(p_ref, kind, 2 * px + py, None, cw), b_ref.at[me],
                                   sems[0].at[3 * i + j], sems[1].at[3 * i + j], (px, py, c)))
        return cps

    def start(ins, outs, sems):
        for cp in copies(ins, outs, sems):
            cp.start()

    def finish(ins, outs, sems):
        x, y, c, chips = _place()
        for i, b_ref in enumerate(outs):
            for j, (px, py) in enumerate(chips):
                blk = b_ref.at[2 * px + py]
                _remote(blk, blk, sems[0].at[3 * i + j], sems[1].at[3 * i + j], (px, py, c)).wait_recv()
        for cp in copies(ins, outs, sems):
            cp.wait_send()

    return start, finish


def _exchange_shapes(ps, kinds):
    return [jax.ShapeDtypeStruct((N_CHIPS,) + ((p.shape[0], p.shape[1] // N_CHIPS) if k == COLS else tuple(p.shape[1:])),
                                 p.dtype) for p, k in zip(ps, kinds)]


def _exchange_sems(nw):
    return [pltpu.SemaphoreType.DMA((3 * nw,)), pltpu.SemaphoreType.DMA((3 * nw,))]


def _exchange_rider(ps, kinds):
    start, finish = _exchange_parts(kinds)
    return _Rider(list(ps), _exchange_shapes(ps, kinds), _exchange_sems(len(ps)), start, finish)


def _sum_chips(bq, name):
    n, h, wd = bq.shape
    tr = _row_tile(h, wd * n)

    def body(b_ref, o_ref):
        acc = b_ref[0].astype(F32)
        for s in range(1, n):
            acc = acc + b_ref[s].astype(F32)
        o_ref[...] = acc

    return pl.pallas_call(
        body, name=name, grid=(h // tr,),
        in_specs=[pl.BlockSpec((n, tr, wd), lambda i: (0, i, 0))], out_specs=pl.BlockSpec((tr, wd), lambda i: (i, 0)),
        out_shape=jax.ShapeDtypeStruct((h, wd), F32), compiler_params=_params(1),
    )(bq)


def _pair_join_halves(qs):
    nw = len(qs)

    def body(*refs):
        ins, outs = refs[:nw], refs[nw:2 * nw]
        send_sems, recv_sems = refs[2 * nw:]
        x, y, c, _ = _place()
        sent = []
        for i, (q_ref, o_ref) in enumerate(zip(ins, outs)):
            sent.append(_remote(q_ref, o_ref.at[c], send_sems.at[i], recv_sems.at[i], (x, y, 1 - c)))
            sent[-1].start()
        for i, (q_ref, o_ref) in enumerate(zip(ins, outs)):
            _remote(q_ref, o_ref.at[1 - c], send_sems.at[i], recv_sems.at[i], (x, y, 1 - c)).wait_recv()
        for cp in sent:
            cp.wait_send()

    return pl.pallas_call(
        body, name="pair_join_halves", in_specs=[ANY] * nw, out_specs=[ANY] * nw,
        out_shape=[jax.ShapeDtypeStruct((2,) + q.shape, q.dtype) for q in qs],
        scratch_shapes=[pltpu.SemaphoreType.DMA((nw,)), pltpu.SemaphoreType.DMA((nw,))],
    )(*qs)


def _all_sum_small(s, name):
    sr, w = s.shape

    def body(s_ref, o_ref, buf, send_sems, recv_sems):
        x, y, c, _ = _place()
        me = 4 * x + 2 * y + c
        buf[me] = s_ref[...]
        peers = []
        for k in range(1, 8):
            px = 1 - x if k & 4 else x
            py = 1 - y if k & 2 else y
            pc = 1 - c if k & 1 else c
            peers.append((px, py, pc))
        sent = [_remote(s_ref, buf.at[me], send_sems.at[k], recv_sems.at[k], peer) for k, peer in enumerate(peers)]
        for cp in sent:
            cp.start()
        for k, (px, py, pc) in enumerate(peers):
            _remote(s_ref, buf.at[4 * px + 2 * py + pc], send_sems.at[k], recv_sems.at[k], (px, py, pc)).wait_recv()
        for cp in sent:
            cp.wait_send()
        acc = buf[0]
        for d in range(1, 8):
            acc = acc + buf[d]
        o_ref[...] = acc

    vm = pl.BlockSpec(memory_space=pltpu.VMEM)
    return pl.pallas_call(
        body, name=name, in_specs=[vm], out_specs=vm, out_shape=jax.ShapeDtypeStruct((sr, w), F32),
        scratch_shapes=[pltpu.VMEM((8, sr, w), F32), pltpu.SemaphoreType.DMA((7,)), pltpu.SemaphoreType.DMA((7,))],
    )(s)


BIG = ("w_in", "mem_kv_w", "w_br_hgrn", "w_br_fox", "w_br_mem", "w_out", "ffn_w_up", "ffn_w_down")
KIND = {"w_in": STACK, "mem_kv_w": STACK, "w_br_hgrn": COLS, "w_br_fox": COLS, "w_br_mem": COLS, "w_out": STACK,
        "ffn_w_up": STACK, "ffn_w_down": STACK}
ROW_SHARDED = ("mem_kv_w", "w_out", "ffn_w_down")
FIRST = ("w_in",)
REST = tuple(nm for nm in BIG if nm not in FIRST)
LATE = {"in_proj": tuple(nm for nm in REST if not nm.startswith("ffn_")),
        "fox_fwd": tuple(nm for nm in REST if nm.startswith("ffn_"))}
LAST = ("w_in",)
TRANSPOSED = ("w_in",)


def _z_layout(d, hw, fw, mw):
    gate, npair, nh, nm = 3 * d // LANE, fw // LANE, hw // LANE, mw // LANE
    fox0, hg0 = gate, gate + 3 * npair
    o_fox, o_mem = 4 * nh, 4 * nh + 3 * npair
    order = [o_mem + nm + j for j in range(gate)]
    order += [o_fox + k * npair + p for p in range(npair) for k in range(3)]
    order += [k * nh + h for h in range(nh) for k in range(4)]
    order += [o_mem + h for h in range(nm)]
    assert fox0 % 3 == 0 and hg0 % 4 == 0
    return fox0, hg0, hg0 + 4 * nh, order


def _reorder_blocks(a, order):
    runs, start = [], 0
    for i in range(1, len(order) + 1):
        if i == len(order) or order[i] != order[i - 1] + 1:
            runs.append((order[start], order[i - 1] + 1))
            start = i
    return jnp.concatenate([a[:, lo * LANE:hi * LANE] for lo, hi in runs], axis=1)


def _put_shard(arr, kind, s, piece):
    if kind == COLS:
        return lax.dynamic_update_slice(arr, piece, (0, s * piece.shape[1]))
    return lax.dynamic_update_slice(arr, piece[None], (s, 0, 0))


def _take_shard(arr, kind, s):
    if kind == COLS:
        return lax.dynamic_slice(arr, (0, s * (arr.shape[1] // N_CHIPS)), (arr.shape[0], arr.shape[1] // N_CHIPS))
    return lax.dynamic_index_in_dim(arr, s, 0, keepdims=False)


def _w_in_pieces(cs, s1, nf):
    out = []
    for s in range(N_CHIPS):
        lo, hi = cs * s, cs * (s + 1)
        for a, b, forget in ((lo, min(hi, s1), False), (max(lo, s1), min(hi, s1 + nf), True), (max(lo, s1 + nf), hi, False)):
            if a < b:
                out.append((s, a - lo, b - lo, forget, a - s1 if forget else (a if a < s1 else a - nf)))
    return out


def _split_w_in(stacked, s1, nf):
    pieces = _w_in_pieces(stacked.shape[2], s1, nf)
    main = [stacked[s, :, a:b] for s, a, b, forget, _ in pieces if not forget]
    ff = [stacked[s, :, a:b] for s, a, b, forget, _ in pieces if forget]
    return jnp.concatenate(main, axis=1), jnp.concatenate(ff, axis=1)


def _join_w_in(g_main, g_ff, s1, nf):
    cs = (g_main.shape[1] + nf) // N_CHIPS
    shards = [[] for _ in range(N_CHIPS)]
    for s, a, b, forget, off in _w_in_pieces(cs, s1, nf):
        shards[s].append((g_ff if forget else g_main)[:, off:off + b - a])
    return jnp.stack([jnp.concatenate(p, axis=1) if len(p) > 1 else p[0] for p in shards])


SMALL = ("norm_mix_g", "norm_mem_g", "norm_ffn_g", "hgrn_lb_logits", "hgrn_norm_g", "fox_f_bias", "fox_q_norm_g",
         "fox_k_norm_g", "mem_q_norm_g", "mem_k_norm_g", "ffn_conv_b")


def _small_rows(shapes):
    rows = []
    for a, (r, c) in enumerate(shapes):
        for i in range(r):
            for lo in range(0, c, FLAT_W):
                rows.append((a, i, lo, min(FLAT_W, c - lo)))
    return rows


def _pack_small(vals):
    rows = _small_rows([v.shape for v in vals])
    sr = -(-len(rows) // 8) * 8

    def body(*refs):
        o_ref = refs[-1]
        o_ref[...] = jnp.zeros(o_ref.shape, F32)
        for k, (a, i, lo, wd) in enumerate(rows):
            o_ref[k:k + 1, 0:wd] = refs[a][i:i + 1, lo:lo + wd]

    vm = pl.BlockSpec(memory_space=pltpu.VMEM)
    return pl.pallas_call(body, name="pack_small", in_specs=[vm] * len(vals), out_specs=vm,
                          out_shape=jax.ShapeDtypeStruct((sr, FLAT_W), F32))(*vals)


def _row_of(buf_ref, rows, a, i):
    parts = [buf_ref[k:k + 1, 0:wd] for k, (a2, i2, _, wd) in enumerate(rows) if (a2, i2) == (a, i)]
    return jnp.concatenate(parts, axis=1) if len(parts) > 1 else parts[0]


def _unpack_small(buf, shapes):
    rows = _small_rows(shapes)

    def body(buf_ref, *outs):
        for a, (r, _) in enumerate(shapes):
            for i in range(r):
                outs[a][i:i + 1, :] = _row_of(buf_ref, rows, a, i)

    vm = pl.BlockSpec(memory_space=pltpu.VMEM)
    return pl.pallas_call(body, name="unpack_small", in_specs=[vm], out_specs=[vm] * len(shapes),
                          out_shape=[jax.ShapeDtypeStruct(shp, F32) for shp in shapes])(buf)


def _adamw_small(buf, shapes, ws, ms, vs):
    n = len(ws)
    rows = _small_rows(shapes)
    c1 = 1.0 / (1.0 - ADAM_B1 ** ADAM_STEP)
    c2 = 1.0 / (1.0 - ADAM_B2 ** ADAM_STEP)

    def body(buf_ref, *refs):
        w_refs, m_refs, v_refs = refs[:n], refs[n:2 * n], refs[2 * n:3 * n]
        outs = refs[3 * n:]
        g_out, d_out, m_out, v_out, rest = outs[:n], outs[n:2 * n], outs[2 * n:3 * n], outs[3 * n:4 * n], outs[4 * n:]
        for a, (r, _) in enumerate(shapes):
            for i in range(r):
                gv = _row_of(buf_ref, rows, a, i)
                if a >= n:
                    rest[a - n][i:i + 1, :] = gv
                    continue
                row = slice(i, i + 1)
                mn = ADAM_B1 * m_refs[a][row, :] + (1.0 - ADAM_B1) * gv
                vn = ADAM_B2 * v_refs[a][row, :] + (1.0 - ADAM_B2) * (gv * gv)
                g_out[a][row, :] = gv
                d_out[a][row, :] = -ADAM_LR * ((mn * c1) / (jnp.sqrt(vn * c2) + ADAM_EPS) + ADAM_WD * w_refs[a][row, :])
                m_out[a][row, :] = mn
                v_out[a][row, :] = vn

    vm = pl.BlockSpec(memory_space=pltpu.VMEM)
    own = [jax.ShapeDtypeStruct(shp, F32) for shp in shapes[:n]]
    outs = pl.pallas_call(
        body, name="adamw_small", in_specs=[vm] * (1 + 3 * n), out_specs=[vm] * (4 * n + len(shapes) - n),
        out_shape=own * 4 + [jax.ShapeDtypeStruct(shp, F32) for shp in shapes[n:]],
    )(buf, *ws, *ms, *vs)
    return outs[:n], outs[n:2 * n], outs[2 * n:3 * n], outs[3 * n:4 * n], outs[4 * n:]


def _pad_lanes(v, width=LANE):
    return jnp.pad(v, ((0, 0), (0, width - v.shape[1])))


WEIGHTS = ("norm_mix_g", "norm_mem_g", "w_in", "hgrn_lb_logits", "hgrn_norm_g", "fox_f_bias", "fox_q_norm_g",
           "fox_k_norm_g", "mem_kv_w", "mem_q_norm_g", "mem_k_norm_g", "w_br_hgrn", "w_br_fox", "w_br_mem", "w_out",
           "norm_ffn_g", "ffn_w_up", "ffn_conv_w", "ffn_conv_b", "ffn_w_down")


def _local_step(x, mem, target, w, full, conv_w, late=None, hooks=None):
    b_, t_, d = x.shape
    n = b_ * t_
    hw, fw, mw = HG_HEADS * HG_D, FOX_HEADS * FOX_DH, MEM_HEADS * MEM_DH
    m_ = mem.shape[1]
    f = conv_w.shape[1]
    s1 = 4 * hw + 3 * fw
    fox_col, hg_col, mem_col, order = _z_layout(d, hw, fw, mw)
    gate_col = 0
    inverse = [order.index(j) for j in range(len(order))]

    w_main, w_ff = _split_w_in(full["w_in"], s1, FOX_HEADS)
    w_main = _reorder_blocks(w_main, order)
    w_ff = _pad_lanes(w_ff)
    f_bias = _pad_lanes(w["fox_f_bias"])
    cb = w["ffn_conv_b"]

    x2 = x.reshape(n, d)
    h = _rmsnorm_fwd(x2, w["norm_mix_g"], name="norm_mix_fwd")
    if late:
        pieces, kinds, finish = late["in_proj"]
        zm, gathered = _matmul(h, w_main, name="in_proj", rider=_gather_rider(pieces, kinds))
        full = {**full, **finish(gathered)}
    else:
        zm = _matmul(h, w_main, name="in_proj")
    w_brs = [full["w_br_hgrn"], full["w_br_fox"], full["w_br_mem"]]
    w_out, w_kv = full["w_out"], full["mem_kv_w"]
    zf = _matmul(h, w_ff, name="in_proj_forget")
    zm3, zf3 = zm.reshape(b_, t_, -1), zf.reshape(b_, t_, LANE)
    ya = _hgrn_fwd(zm3, w["hgrn_lb_logits"], w["hgrn_norm_g"], hw, hg_col)
    fc = _fox_prep(zf3, f_bias)
    fox_gq, fox_gk = jnp.tile(w["fox_q_norm_g"], (1, 2)), jnp.tile(w["fox_k_norm_g"], (1, 2))
    if late:
        pieces, kinds, finish = late["fox_fwd"]
        (yb, lse), gathered = _fox_fwd(zm3, fc, fox_gq, fox_gk, fw, fox_col, _gather_rider(pieces, kinds))
        full = {**full, **finish(gathered)}
    else:
        yb, lse = _fox_fwd(zm3, fc, fox_gq, fox_gk, fw, fox_col)[0]
    w_up, w_down = full["ffn_w_up"], full["ffn_w_down"]
    mem2 = mem.reshape(b_ * m_, d)
    hm = _rmsnorm_fwd(mem2, w["norm_mem_g"], name="norm_mem_fwd")
    mkv = _matmul(hm, w_kv, name="mem_kv_proj").reshape(b_, m_, 2 * mw)
    yc = _mem_fwd(zm3, mkv, w["mem_q_norm_g"], w["mem_k_norm_g"], mw, mem_col)
    ys = [ya.reshape(n, hw), yb.reshape(n, fw), yc.reshape(n, mw)]
    x1, merged, h2 = _merge_fwd(x2, ys, zm, w_brs, w_out, gate_col, w["norm_ffn_g"])
    u = _matmul(h2, w_up, name="ffn_up")
    u3 = u.reshape(b_, t_, 2 * f)
    yff = _glu_fwd(u3, conv_w, cb).reshape(n, f)
    dy, (loss_vec,), _ = _matmul_rows([yff], w_down, name="ffn_down_loss", tb=False, row_ins=[x1, target.reshape(n, d)],
                                      vec_ins=[], epilogue=_loss_epilogue, n_vec_out=1)

    grads = {}

    def ridden(name, call):
        if not hooks or name not in hooks:
            return call(None)[0]
        rider, then = hooks[name](grads)
        outs, extra = call(rider)
        then(extra)
        return outs

    dyff = _matmul(dy, w_down, tb=True, name="ffn_down_dx")
    grads["ffn_w_down"] = _matmul(yff, dy, ta=True, name="ffn_down_dw", tm=1408)
    du_a, du_v, grads["ffn_conv_w"], grads["ffn_conv_b"] = _glu_bwd(u3, dyff.reshape(b_, t_, f), conv_w, cb)
    du_a, du_v = du_a.reshape(n, f), du_v.reshape(n, f)
    dx1, (grads["norm_ffn_g"],), _ = _matmul_rows(
        [du_a, du_v], w_up, name="ffn_up_dx", tb=True, row_ins=[x1, dy], vec_ins=[w["norm_ffn_g"]],
        epilogue=_norm_bwd_epilogue(0), n_vec_out=1)
    grads["ffn_w_up"] = _matmul(h2, None, ta=True, name="ffn_up_dw", b_parts=[du_a, du_v], tn=f // 2, stack_out=True)

    dz, dpa, dpb, dpc, dya, dyb, dyc = _merge_bwd(dx1, ys, zm, w_brs, w_out, gate_col)
    dz = dz.reshape(b_, t_, -1)
    grads["w_out"] = _matmul(merged, dx1, ta=True, name="out_proj_dw")
    for nm, y_, dp_ in zip(("w_br_hgrn", "w_br_fox", "w_br_mem"), ys, (dpa, dpb, dpc)):
        grads[nm] = _matmul(y_, dp_, ta=True, name=nm + "_dw")

    dz, dmk, dmv, grads["mem_q_norm_g"], grads["mem_k_norm_g"] = _mem_bwd(
        zm3, mkv, dyc.reshape(b_, t_, mw), w["mem_q_norm_g"], w["mem_k_norm_g"], mw, mem_col, dz)
    dmkv = jnp.concatenate([dmk, dmv], axis=-1).reshape(b_ * m_, 2 * mw)
    grads["mem_kv_w"] = _matmul(hm, dmkv, ta=True, name="mem_kv_dw")
    dhm = _matmul(dmkv, w_kv, tb=True, name="mem_kv_dx")
    _, grads["norm_mem_g"] = _rmsnorm_bwd(mem2, [dhm], w["norm_mem_g"], None, name="norm_mem_bwd")

    dz, dfc, g_fq, g_fk = ridden("fox_bwd", lambda rider: _fox_bwd(
        zm3, yb, dyb.reshape(b_, t_, fw), lse, fc, fox_gq, fox_gk, fw, fox_col, dz, rider))
    grads["fox_q_norm_g"] = g_fq[:, :FOX_DH] + g_fq[:, FOX_DH:]
    grads["fox_k_norm_g"] = g_fk[:, :FOX_DH] + g_fk[:, FOX_DH:]
    dzf, g_fb = _fox_post(dfc, zf3, f_bias)
    grads["fox_f_bias"] = g_fb[:, :FOX_HEADS]

    dz, grads["hgrn_lb_logits"], grads["hgrn_norm_g"] = ridden("hgrn_bwd", lambda rider: _hgrn_bwd(
        zm3, dya.reshape(b_, t_, hw), w["hgrn_lb_logits"], w["hgrn_norm_g"], hw, hg_col, dz, rider))
    dzm = dz.reshape(n, -1)
    dzf2 = dzf.reshape(n, LANE)
    g_main = _matmul(h, dzm, ta=True, name="in_proj_dw")
    g_ff = _matmul(h, dzf2, ta=True, name="in_proj_forget_dw")
    grads["w_in"] = _join_w_in(_reorder_blocks(g_main, inverse), g_ff[:, :FOX_HEADS], s1, FOX_HEADS)

    dh_b = _matmul(dzf2, w_ff, tb=True, name="in_proj_forget_dx")

    def in_proj_dx(rider):
        out = _matmul(dzm, w_main, tb=True, name="in_proj_dx", rider=rider)
        return ([out[0]], out[1]) if rider else ([out], None)

    dh_a, = ridden("in_proj_dx", in_proj_dx)
    grad_x, grads["norm_mix_g"] = _rmsnorm_bwd(x2, [dh_a, dh_b], w["norm_mix_g"], dx1, name="norm_mix_bwd")
    return loss_vec, grad_x.reshape(b_, t_, d), grads


def kernel(x, mem, norm_mix_g, norm_mem_g, w_in, hgrn_lb_logits, hgrn_norm_g, fox_f_bias, fox_q_norm_g, fox_k_norm_g, mem_kv_w, mem_q_norm_g, mem_k_norm_g, w_br_hgrn, w_br_fox, w_br_mem, w_out, norm_ffn_g, ffn_w_up, ffn_conv_w, ffn_conv_b, ffn_w_down, loss_target, m_norm_mix_g, m_norm_mem_g, m_w_in, m_hgrn_lb_logits, m_hgrn_norm_g, m_fox_f_bias, m_fox_q_norm_g, m_fox_k_norm_g, m_mem_kv_w, m_mem_q_norm_g, m_mem_k_norm_g, m_w_br_hgrn, m_w_br_fox, m_w_br_mem, m_w_out, m_norm_ffn_g, m_ffn_w_up, m_ffn_conv_w, m_ffn_conv_b, m_ffn_w_down, v_norm_mix_g, v_norm_mem_g, v_w_in, v_hgrn_lb_logits, v_hgrn_norm_g, v_fox_f_bias, v_fox_q_norm_g, v_fox_k_norm_g, v_mem_kv_w, v_mem_q_norm_g, v_mem_k_norm_g, v_w_br_hgrn, v_w_br_fox, v_w_br_mem, v_w_out, v_norm_ffn_g, v_ffn_w_up, v_ffn_conv_w, v_ffn_conv_b, v_ffn_w_down):
    w = dict(zip(WEIGHTS, (norm_mix_g, norm_mem_g, w_in, hgrn_lb_logits, hgrn_norm_g, fox_f_bias, fox_q_norm_g,
                           fox_k_norm_g, mem_kv_w, mem_q_norm_g, mem_k_norm_g, w_br_hgrn, w_br_fox, w_br_mem, w_out,
                           norm_ffn_g, ffn_w_up, ffn_conv_w, ffn_conv_b, ffn_w_down)))
    m = dict(zip(WEIGHTS, (m_norm_mix_g, m_norm_mem_g, m_w_in, m_hgrn_lb_logits, m_hgrn_norm_g, m_fox_f_bias,
                           m_fox_q_norm_g, m_fox_k_norm_g, m_mem_kv_w, m_mem_q_norm_g, m_mem_k_norm_g, m_w_br_hgrn,
                           m_w_br_fox, m_w_br_mem, m_w_out, m_norm_ffn_g, m_ffn_w_up, m_ffn_conv_w, m_ffn_conv_b,
                           m_ffn_w_down)))
    v = dict(zip(WEIGHTS, (v_norm_mix_g, v_norm_mem_g, v_w_in, v_hgrn_lb_logits, v_hgrn_norm_g, v_fox_f_bias,
                           v_fox_q_norm_g, v_fox_k_norm_g, v_mem_kv_w, v_mem_q_norm_g, v_mem_k_norm_g, v_w_br_hgrn,
                           v_w_br_fox, v_w_br_mem, v_w_out, v_norm_ffn_g, v_ffn_w_up, v_ffn_conv_w, v_ffn_conv_b,
                           v_ffn_w_down)))
    c_idx = lax.axis_index("c")
    chip = 2 * lax.axis_index("x") + lax.axis_index("y")

    mine = {nm: w[nm][0].astype(MXU_DTYPE) for nm in BIG}

    def gathered_full(names, arrays):
        out = {nm: _put_shard(g, KIND[nm], chip, mine[nm]) for nm, g in zip(names, arrays)}
        return {nm: g.reshape(-1, g.shape[2]) if nm in ROW_SHARDED else g for nm, g in out.items()}

    full = gathered_full(FIRST, _gather_shards([mine[nm] for nm in FIRST], [KIND[nm] for nm in FIRST]))
    late = {host: ([mine[nm] for nm in names], [KIND[nm] for nm in names],
                   functools.partial(gathered_full, names)) for host, names in LATE.items()}
    cs = ffn_conv_w.shape[2]
    f = cs * N_CHIPS
    placed = lax.dynamic_update_slice(jnp.zeros((3, f), F32), ffn_conv_w[0] * (c_idx == 0).astype(F32), (0, chip * cs))
    conv_w = _unpack_small(_all_sum_small(_pack_small([placed]), "gather_conv_w"), [(3, f)])[0]

    c_arr = jnp.reshape(c_idx, (1,)).astype(jnp.int32)

    def stacked(nm, g):
        return g.reshape(N_CHIPS, -1, g.shape[1]) if nm in ROW_SHARDED else g

    def with_own(landed, partial, kinds):
        return [_put_shard(bq, STACK, chip, _take_shard(p, k, chip)) for bq, p, k in zip(landed, partial, kinds)]

    kinds_rest, kinds_last = [KIND[nm] for nm in REST], [KIND[nm] for nm in LAST]
    state = {}

    def swap_rest(grads):
        gs = [stacked(nm, grads[nm]) for nm in REST]

        def then(from_sibling):
            state["partial_rest"] = [_add_half(g, a, k, c_arr, "add_half_" + nm)
                                     for g, a, k, nm in zip(gs, from_sibling, kinds_rest, REST)]

        return _swap_rider(gs, kinds_rest), then

    def exchange_rest(grads):
        def then(landed):
            state["landed_rest"] = with_own(landed, state["partial_rest"], kinds_rest)

        return _exchange_rider(state["partial_rest"], kinds_rest), then

    def exchange_last(grads):
        gs = [stacked(nm, grads[nm]) for nm in LAST]
        from_sibling = _pair_swap_halves(gs, kinds_last, "pair_swap_halves_last")
        partial = [_add_half(g, a, k, c_arr, "add_half_" + nm) for g, a, k, nm in zip(gs, from_sibling, kinds_last, LAST)]

        def then(landed):
            state["landed_last"] = with_own(landed, partial, kinds_last)

        return _exchange_rider(partial, kinds_last), then

    hooks = {"fox_bwd": swap_rest, "hgrn_bwd": exchange_rest, "in_proj_dx": exchange_last}

    loss_vec, grad_x, grads = _local_step(x, mem, loss_target, w, full, conv_w, late, hooks)

    landed = dict(zip(LAST + REST, state["landed_last"] + state["landed_rest"]))
    reduced_half = [_sum_chips(landed[nm], "sum_chips_" + nm) for nm in BIG]
    joined = [lax.dynamic_update_slice(o, q[None], (c_idx, 0, 0)).reshape(2 * q.shape[0], q.shape[1])
              for o, q in zip(_pair_join_halves(reduced_half), reduced_half)]
    gshards = dict(zip(BIG, joined))

    small_shapes = [w[nm].shape for nm in SMALL] + [grads["ffn_conv_w"].shape, loss_vec.shape]
    summed = _all_sum_small(_pack_small([grads[nm] for nm in SMALL] + [grads["ffn_conv_w"], loss_vec]),
                            "all_sum_small_grads")
    g_small, d_small, m_small, v_small, (g_conv_w, loss_row) = _adamw_small(
        summed, small_shapes, [w[nm] for nm in SMALL], [m[nm] for nm in SMALL], [v[nm] for nm in SMALL])
    loss = jnp.sum(loss_row)
    g_out = {nm: gshards[nm][None] for nm in BIG}
    g_out["ffn_conv_w"] = lax.dynamic_slice(g_conv_w, (0, chip * cs), (3, cs))[None]
    delta, new_m, new_v = dict(zip(SMALL, d_small)), dict(zip(SMALL, m_small)), dict(zip(SMALL, v_small))
    g_out.update(zip(SMALL, g_small))
    for nm in BIG + ("ffn_conv_w",):
        operands = (w[nm], g_out[nm], m[nm], v[nm])
        if nm in TRANSPOSED:
            operands = [jnp.swapaxes(a, 1, 2) for a in operands]
        outs = _adamw(*operands, name="adamw_" + nm)
        delta[nm], new_m[nm], new_v[nm] = [jnp.swapaxes(o, 1, 2) for o in outs] if nm in TRANSPOSED else outs

    return (loss, grad_x, *[g_out[nm] for nm in WEIGHTS], *[delta[nm] for nm in WEIGHTS],
            *[new_m[nm] for nm in WEIGHTS], *[new_v[nm] for nm in WEIGHTS])
```

```python
import functools
import math

import jax
import jax.numpy as jnp
from jax import lax
from jax.experimental import pallas as pl
from jax.experimental.pallas import tpu as pltpu

F32 = jnp.float32
BF16 = jnp.bfloat16
MXU_DTYPE = jnp.bfloat16
EXCHANGE_DTYPE = jnp.bfloat16

EPS = 1e-6
HG_HEADS, HG_D = 4, 128
FOX_HEADS, FOX_DH = 8, 64
MEM_HEADS, MEM_DH = 4, 128
MEM_TILE = 2048
HG_CHUNK = 64
FOX_BLOCK = 256
LANE = 128
FFN_GROUP = 256
FLAT_W = 1024
VMEM_LIMIT = 56 * 2 ** 20
NEG = -1e30
N_CHIPS = 4

ADAM_LR, ADAM_B1, ADAM_B2, ADAM_EPS, ADAM_WD, ADAM_STEP = 0.001, 0.9, 0.999, 1e-08, 0.01, 10

MESH = pl.DeviceIdType.MESH
ANY = pl.BlockSpec(memory_space=pl.ANY)


def _mx(x):
    return x.astype(MXU_DTYPE)


def _dot(a, b, ca, cb):
    return lax.dot_general(_mx(a), _mx(b), (((ca,), (cb,)), ((), ())), preferred_element_type=F32)


def _nn(a, b):
    return _dot(a, b, 1, 0)


def _nt(a, b):
    return _dot(a, b, 1, 1)


def _tn(a, b):
    return _dot(a, b, 0, 0)


def _tri_dot(tri_bf, x):
    hi = x.astype(BF16)
    r = x - hi.astype(F32)
    mid = r.astype(BF16)
    lo = (r - mid.astype(F32)).astype(BF16)

    def d(v):
        return lax.dot_general(tri_bf, v, (((1,), (0,)), ((), ())), preferred_element_type=F32)

    return d(hi) + d(mid) + d(lo)


def _sig(x):
    return jax.nn.sigmoid(x)


def _tile(dim, pref, unit=LANE):
    if dim <= pref:
        return dim
    t = pref - pref % unit
    while t >= unit:
        if dim % t == 0:
            return t
        t -= unit
    return dim


def _params(n_grid):
    return pltpu.CompilerParams(dimension_semantics=("arbitrary",) * n_grid, vmem_limit_bytes=VMEM_LIMIT)


def _acc(ref, val, first):
    @pl.when(first)
    def _():
        ref[...] = val

    @pl.when(jnp.logical_not(first))
    def _():
        ref[...] += val


class _Rider:
    def __init__(self, inputs, out_shapes, scratch, start, finish):
        self.inputs, self.out_shapes, self.scratch, self.start, self.finish = inputs, out_shapes, scratch, start, finish


def _ride(body, rider, n_in, n_out, grid):
    if rider is None:
        return body
    ri, ro, rs = len(rider.inputs), len(rider.out_shapes), len(rider.scratch)

    def wrapped(*refs):
        a, b, c = n_in + ri, n_in + ri + n_out, n_in + ri + n_out + ro
        base = refs[:n_in] + refs[a:b] + refs[c:len(refs) - rs]
        r_in, r_out, r_scr = refs[n_in:a], refs[b:c], refs[len(refs) - rs:]
        step = pl.program_id(0)
        for ax in range(1, len(grid)):
            step = step * grid[ax] + pl.program_id(ax)

        @pl.when(step == 0)
        def _():
            rider.start(r_in, r_out, r_scr)

        body(*base)

        @pl.when(step == math.prod(grid) - 1)
        def _():
            rider.finish(r_in, r_out, r_scr)

    return wrapped


def _ride_call(body, rider, *, name, grid, in_specs, out_specs, out_shape, scratch, args, aliases=None):
    n_in, n_out = len(in_specs), len(out_specs)
    aliases = aliases or {}
    if rider is None:
        outs = pl.pallas_call(body, name=name, grid=grid, in_specs=in_specs, out_specs=out_specs, out_shape=out_shape,
                              scratch_shapes=scratch, input_output_aliases=aliases,
                              compiler_params=_params(len(grid)))(*args)
        return list(outs), None
    outs = pl.pallas_call(
        _ride(body, rider, n_in, n_out, grid), name=name, grid=grid,
        in_specs=list(in_specs) + [ANY] * len(rider.inputs), out_specs=list(out_specs) + [ANY] * len(rider.out_shapes),
        out_shape=list(out_shape) + list(rider.out_shapes), scratch_shapes=list(scratch) + list(rider.scratch),
        input_output_aliases=aliases, compiler_params=_params(len(grid)),
    )(*args, *rider.inputs)
    return list(outs[:n_out]), list(outs[n_out:])


def _matmul(a, b, *, name, ta=False, tb=False, tm=2048, tn=2048, tk=None, rider=None, b_parts=None, stack_out=False):
    m, k = (a.shape[1], a.shape[0]) if ta else a.shape
    tk = tk or (1024 if ta else 2048)
    stacked_b = b is not None and b.ndim == 3
    if b_parts:
        n, tn = 2 * b_parts[0].shape[1], _tile(b_parts[0].shape[1], tn)
    elif stacked_b:
        n, tn = b.shape[0] * b.shape[2], b.shape[2]
    else:
        n = b.shape[0] if tb else b.shape[1]
        tn = _tile(n, tn)
    tm, tk = _tile(m, tm), _tile(k, tk)
    nk, nj = k // tk, n // tn

    def body(a_ref, *refs):
        o_ref = refs[-1]
        if b_parts:
            bv = jnp.where(pl.program_id(1) < nj // 2, refs[0][...], refs[1][...])
        else:
            bv = refs[0][...]
        p = _dot(a_ref[...], bv, 0 if ta else 1, 1 if tb else 0)
        if nk == 1:
            o_ref[...] = p
        else:
            _acc(o_ref, p, pl.program_id(2) == 0)

    a_spec = pl.BlockSpec((tk, tm), lambda i, j, kk: (kk, i)) if ta else pl.BlockSpec((tm, tk), lambda i, j, kk: (i, kk))
    if b_parts:
        half = nj // 2
        b_specs = [pl.BlockSpec((tk, tn), lambda i, j, kk: (kk, jnp.minimum(j, half - 1))),
                   pl.BlockSpec((tk, tn), lambda i, j, kk: (kk, jnp.maximum(j - half, 0)))]
        b_args = list(b_parts)
    elif stacked_b:
        b_specs, b_args = [pl.BlockSpec((None, tk, tn), lambda i, j, kk: (j, kk, 0))], [b]
    else:
        b_specs = [pl.BlockSpec((tn, tk), lambda i, j, kk: (j, kk)) if tb else pl.BlockSpec((tk, tn), lambda i, j, kk: (kk, j))]
        b_args = [b]
    if stack_out:
        o_spec, o_sds = pl.BlockSpec((None, tm, tn), lambda i, j, kk: (j, i, 0)), jax.ShapeDtypeStruct((nj, m, tn), F32)
    else:
        o_spec, o_sds = pl.BlockSpec((tm, tn), lambda i, j, kk: (i, j)), jax.ShapeDtypeStruct((m, n), F32)
    outs, extra = _ride_call(body, rider, name=name, grid=(m // tm, nj, nk), in_specs=[a_spec] + b_specs,
                             out_specs=[o_spec], out_shape=[o_sds], scratch=[], args=(a, *b_args))
    return (outs[0], extra) if rider else outs[0]


def _matmul_rows(a_parts, b, *, name, tb, row_ins, vec_ins, epilogue, n_vec_out, tm=512, tk=2048, rider=None):
    m, kp = a_parts[0].shape
    stacked_b = b.ndim == 3
    n = b.shape[1] if stacked_b else (b.shape[0] if tb else b.shape[1])
    tm, tk = _tile(m, tm, 8), (b.shape[2] if stacked_b else _tile(kp, tk))
    nk = kp // tk
    n_a, n_row, n_vec = len(a_parts), len(row_ins), len(vec_ins)

    def body(*refs):
        a_refs, b_refs = refs[:n_a], refs[n_a:2 * n_a]
        rows = refs[2 * n_a:2 * n_a + n_row]
        vecs = refs[2 * n_a + n_row:2 * n_a + n_row + n_vec]
        o_ref = refs[2 * n_a + n_row + n_vec]
        v_refs = refs[2 * n_a + n_row + n_vec + 1:-1]
        acc_ref = refs[-1]
        i, kk = pl.program_id(0), pl.program_id(1)
        p = _dot(a_refs[0][...], b_refs[0][...], 1, 1 if tb else 0)
        for a_ref, b_ref in zip(a_refs[1:], b_refs[1:]):
            p = p + _dot(a_ref[...], b_ref[...], 1, 1 if tb else 0)
        _acc(acc_ref, p, kk == 0)

        @pl.when(kk == nk - 1)
        def _():
            out, vouts = epilogue(acc_ref[...], *[r[...] for r in rows], *[v[...] for v in vecs])
            o_ref[...] = out
            for v_ref, v in zip(v_refs, vouts):
                _acc(v_ref, v, i == 0)

    a_spec = pl.BlockSpec((tm, tk), lambda i, kk: (i, kk))
    if stacked_b:
        b_specs = [pl.BlockSpec((None, n, tk), functools.partial(lambda i, kk, q: (q * nk + kk, 0, 0), q=q))
                   for q in range(n_a)]
    else:
        b_specs = [pl.BlockSpec((n, tk), functools.partial(lambda i, kk, q: (0, q * nk + kk), q=q)) if tb else
                   pl.BlockSpec((tk, n), functools.partial(lambda i, kk, q: (q * nk + kk, 0), q=q)) for q in range(n_a)]
    row = pl.BlockSpec((tm, n), lambda i, kk: (i, 0))
    vec = pl.BlockSpec((1, n), lambda i, kk: (0, 0))
    outs, extra = _ride_call(
        body, rider, name=name, grid=(m // tm, nk),
        in_specs=[a_spec] * n_a + b_specs + [row] * n_row + [vec] * n_vec,
        out_specs=[row] + [vec] * n_vec_out,
        out_shape=[jax.ShapeDtypeStruct((m, n), F32)] + [jax.ShapeDtypeStruct((1, n), F32)] * n_vec_out,
        scratch=[pltpu.VMEM((tm, n), F32)], args=(*a_parts, *([b] * n_a), *row_ins, *vec_ins))
    return outs[0], outs[1:], extra


def _norm_bwd_epilogue(n_dh):
    def epilogue(dh, x, res, *rest):
        for extra in rest[:n_dh]:
            dh = dh + extra
        g = rest[n_dh]
        r = lax.rsqrt(jnp.mean(x * x, axis=-1, keepdims=True) + EPS)
        dhg = dh * g
        dx = res + r * dhg - x * (r * r * r) * jnp.mean(dhg * x, axis=-1, keepdims=True)
        return dx, [jnp.sum(dh * x * r, axis=0, keepdims=True)]

    return epilogue


def _loss_epilogue(y, x1, target):
    d = y.shape[1]
    err = x1 + y - target
    return err * (1.0 / d), [jnp.sum(err * err, axis=0, keepdims=True) * (0.5 / d)]


def _rmsnorm_fwd(x, g, *, name, tm=512):
    n, d = x.shape
    tm = _tile(n, tm, 8)

    def body(x_ref, g_ref, o_ref):
        xv = x_ref[...]
        r = lax.rsqrt(jnp.mean(xv * xv, axis=-1, keepdims=True) + EPS)
        o_ref[...] = (xv * r * g_ref[...]).astype(o_ref.dtype)

    return pl.pallas_call(
        body, name=name, grid=(n // tm,),
        in_specs=[pl.BlockSpec((tm, d), lambda i: (i, 0)), pl.BlockSpec((1, d), lambda i: (0, 0))],
        out_specs=pl.BlockSpec((tm, d), lambda i: (i, 0)),
        out_shape=jax.ShapeDtypeStruct((n, d), MXU_DTYPE),
        compiler_params=_params(1),
    )(x, g)


def _rmsnorm_bwd(x, dhs, g, res, *, name, tm=512):
    n, d = x.shape
    tm = _tile(n, tm, 8)
    n_dh = len(dhs)
    has_res = res is not None

    def body(*refs):
        x_ref, dh_refs, g_ref = refs[0], refs[1:1 + n_dh], refs[1 + n_dh]
        res_ref = refs[2 + n_dh] if has_res else None
        dx_ref, dg_ref = refs[-2], refs[-1]
        xv = x_ref[...]
        dh = dh_refs[0][...].astype(F32)
        for r_ in dh_refs[1:]:
            dh = dh + r_[...].astype(F32)
        r = lax.rsqrt(jnp.mean(xv * xv, axis=-1, keepdims=True) + EPS)
        dhg = dh * g_ref[...]
        dx = r * dhg - xv * (r * r * r) * jnp.mean(dhg * xv, axis=-1, keepdims=True)
        if has_res:
            dx = dx + res_ref[...]
        dx_ref[...] = dx
        _acc(dg_ref, jnp.sum(dh * xv * r, axis=0, keepdims=True), pl.program_id(0) == 0)

    row = pl.BlockSpec((tm, d), lambda i: (i, 0))
    vec = pl.BlockSpec((1, d), lambda i: (0, 0))
    ins = [x] + list(dhs) + [g] + ([res] if has_res else [])
    return pl.pallas_call(
        body, name=name, grid=(n // tm,),
        in_specs=[row] * (1 + n_dh) + [vec] + ([row] if has_res else []),
        out_specs=[row, vec],
        out_shape=[jax.ShapeDtypeStruct((n, d), F32), jax.ShapeDtypeStruct((1, d), F32)],
        compiler_params=_params(1),
    )(*ins)


def _adamw(w, g, m, v, *, name, tr=256):
    _, r, c = w.shape
    c1 = 1.0 / (1.0 - ADAM_B1 ** ADAM_STEP)
    c2 = 1.0 / (1.0 - ADAM_B2 ** ADAM_STEP)

    def body(w_ref, g_ref, m_ref, v_ref, d_ref, mo_ref, vo_ref):
        gv = g_ref[...]
        mn = ADAM_B1 * m_ref[...] + (1.0 - ADAM_B1) * gv
        vn = ADAM_B2 * v_ref[...] + (1.0 - ADAM_B2) * (gv * gv)
        d_ref[...] = -ADAM_LR * ((mn * c1) / (jnp.sqrt(vn * c2) + ADAM_EPS) + ADAM_WD * w_ref[...])
        mo_ref[...] = mn
        vo_ref[...] = vn

    if r % 8 == 0 or r < 8:
        tr = _tile(r, tr, 8)
        grid, blk = (r // tr,), pl.BlockSpec((1, tr, c), lambda i: (0, i, 0))
    else:
        tc = _tile(c, tr)
        grid, blk = (c // tc,), pl.BlockSpec((1, r, tc), lambda i: (0, 0, i))
    sds = jax.ShapeDtypeStruct((1, r, c), F32)
    return pl.pallas_call(
        body, name=name, grid=grid, in_specs=[blk] * 4, out_specs=[blk] * 3, out_shape=[sds] * 3,
        compiler_params=_params(1),
    )(w, g, m, v)


def _bdot(a, b, ca, cb):
    return lax.dot_general(_mx(a), _mx(b), (((ca,), (cb,)), ((0,), (0,))), preferred_element_type=F32)


def _split2(x):
    hi = x.astype(BF16)
    return hi, (x - hi.astype(F32)).astype(BF16)


def _bdotp(a, b, ca, cb):
    def d(u, v):
        return lax.dot_general(u, v, (((ca,), (cb,)), ((0,), (0,))), preferred_element_type=F32)

    return d(a[0], b[0]) + d(a[0], b[1]) + d(a[1], b[0])


def _tri_dot_b(tri_bf, x):
    hi = x.astype(BF16)
    r = x - hi.astype(F32)
    mid = r.astype(BF16)
    lo = (r - mid.astype(F32)).astype(BF16)

    def d(v):
        return lax.dot_general(tri_bf, v, (((2,), (1,)), ((0,), (0,))), preferred_element_type=F32)

    return d(hi) + d(mid) + d(lo)


def _hgrn_forward(hq, hf, hi, lbv, tril, tril_bf):
    nc, c, _ = hq.shape
    sf = _sig(hf)
    f = lbv + (1.0 - lbv) * sf
    k = 1.0 - f
    gcum = _tri_dot_b(tril_bf, jnp.log(f))
    mid = gcum[:, c // 2 - 1:c // 2, :]
    glast = gcum[:, c - 1:c, :]
    sq = _sig(hq)
    q = hq * sq
    e_q = jnp.exp(gcum - mid)
    e_k = jnp.exp(mid - gcum)
    qe, ke = q * e_q, k * e_k
    a = jnp.where(tril, _bdot(qe, ke, 2, 2), 0.0)
    e_g = jnp.exp(gcum)
    qg = q * e_g
    e_s = jnp.exp(glast - gcum)
    kg = k * e_s
    e_l = jnp.exp(glast)
    upd = _bdot(hi, kg, 1, 1)
    st = jnp.zeros((HG_D, HG_D), F32)
    states = []
    for n in range(nc):
        states.append(st)
        st = st * e_l[n] + upd[n]
    st_all = jnp.stack(states)
    o = _bdot(a, hi, 2, 1) + _bdot(qg, st_all, 2, 2)
    return dict(sf=sf, f=f, k=k, sq=sq, q=q, e_q=e_q, e_k=e_k, qe=qe, ke=ke, a=a, e_g=e_g, qg=qg, o=o,
                e_s=e_s, kg=kg, e_l=e_l, st_all=st_all)


def _hgrn_specs(t_, col0):
    def col(off):
        return pl.BlockSpec((1, t_, LANE), lambda h, b: (b, 0, col0 + 4 * h + off))

    vec = pl.BlockSpec((2, LANE), lambda h, b: (0, h))
    one = pl.BlockSpec((1, LANE), lambda h, b: (0, 0))
    blk = pl.BlockSpec((1, t_, LANE), lambda h, b: (b, 0, h))
    return col, vec, one, blk


def _chunk_masks(nc, c):
    row = lax.broadcasted_iota(jnp.int32, (nc, c, c), 1)
    cl = lax.broadcasted_iota(jnp.int32, (nc, c, c), 2)
    return row >= cl, (row >= cl).astype(BF16), (row <= cl).astype(BF16)


def _hgrn_fwd(zm, lb, gn, hw, col0):
    b_, t_, _ = zm.shape
    c = min(HG_CHUNK, t_)
    nc = t_ // c
    col, vec, one, blk = _hgrn_specs(t_, col0)

    def body(q_ref, f_ref, i_ref, g_ref, lb_ref, gn_ref, y_ref):
        lbv, gnv = _sig(lb_ref[0:1, :] - lb_ref[1:2, :]), gn_ref[...]
        tril, tril_bf, _ = _chunk_masks(nc, c)
        chunks = lambda ref: ref[0].reshape(nc, c, LANE)
        o = _hgrn_forward(chunks(q_ref), chunks(f_ref), chunks(i_ref), lbv, tril, tril_bf)["o"]
        r = lax.rsqrt(jnp.mean(o * o, axis=-1, keepdims=True) + EPS)
        hg = chunks(g_ref)
        y_ref[0] = (o * r * gnv * (hg * _sig(hg))).reshape(t_, LANE)

    return pl.pallas_call(
        body, name="hgrn_fwd", grid=(HG_HEADS, b_),
        in_specs=[col(0), col(1), col(2), col(3), vec, one], out_specs=blk,
        out_shape=jax.ShapeDtypeStruct((b_, t_, hw), F32),
        compiler_params=_params(2),
    )(zm, zm, zm, zm, lb, gn)


def _hgrn_bwd(zm, dy, lb, gn, hw, col0, dz, rider=None):
    b_, t_, _ = zm.shape
    c = min(HG_CHUNK, t_)
    nc = t_ // c
    col, vec, one, blk = _hgrn_specs(t_, col0)

    def body(q_ref, f_ref, i_ref, g_ref, dy_ref, lb_ref, gn_ref, _, dz_ref, dlb_ref, dgn_ref):
        h, b = pl.program_id(0), pl.program_id(1)
        lbv, gnv = _sig(lb_ref[0:1, :] - lb_ref[1:2, :]), gn_ref[...]
        tril, tril_bf, triu_bf = _chunk_masks(nc, c)
        last_row = lax.broadcasted_iota(jnp.int32, (nc, c, LANE), 1) == c - 1
        chunks = lambda ref: ref[0].reshape(nc, c, LANE)
        flat = lambda x: x.reshape(t_, LANE)
        hq, hi, hg = chunks(q_ref), chunks(i_ref), chunks(g_ref)
        p = _hgrn_forward(hq, chunks(f_ref), hi, lbv, tril, tril_bf)
        o, q, k, st_all, e_l = p["o"], p["q"], p["k"], p["st_all"], p["e_l"]
        dyv = chunks(dy_ref)
        sg = _sig(hg)
        r = lax.rsqrt(jnp.mean(o * o, axis=-1, keepdims=True) + EPS)
        dn = dyv * (hg * sg)
        dz_ref[0, :, 3 * LANE:] = flat(dyv * (o * r * gnv) * (sg * (1.0 + hg * (1.0 - sg)))).astype(dz_ref.dtype)
        dgn = jnp.sum(flat(dn * o * r), axis=0, keepdims=True)
        dng = dn * gnv
        do = r * dng - o * (r * r * r) * jnp.mean(dng * o, axis=-1, keepdims=True)
        do2, hi2, qg2, ke2, qe2, st2 = (_split2(t) for t in (do, hi, p["qg"], p["ke"], p["qe"], st_all))
        back = _bdotp(do2, qg2, 1, 1)
        dst = jnp.zeros((HG_D, HG_D), F32)
        dsts = [None] * nc
        for n in range(nc - 1, -1, -1):
            dsts[n] = dst
            dst = dst * e_l[n] + back[n]
        dst_all = jnp.stack(dsts)
        da = jnp.where(tril, _bdotp(do2, hi2, 2, 2), 0.0)
        da2 = _split2(da)
        dq = _bdotp(da2, ke2, 2, 1) * p["e_q"] + _bdotp(do2, st2, 2, 1) * p["e_g"]
        dk_state = _bdotp(hi2, _split2(dst_all), 2, 1) * p["e_s"]
        dk = _bdotp(da2, qe2, 1, 1) * p["e_k"] + dk_state
        dz_ref[0, :, 2 * LANE:3 * LANE] = flat(_bdot(p["a"], do, 1, 1) + _bdot(p["kg"], dst_all, 2, 2)).astype(dz_ref.dtype)
        extra = (jnp.sum(k * dk_state, axis=1, keepdims=True) + e_l * jnp.sum(st_all * dst_all, axis=1, keepdims=True))
        dgc = q * dq - k * dk + jnp.where(last_row, extra, 0.0)
        dfv = _tri_dot_b(triu_bf, dgc) / p["f"] - dk
        sf, sq = p["sf"], p["sq"]
        dz_ref[0, :, LANE:2 * LANE] = flat(dfv * (1.0 - lbv) * sf * (1.0 - sf)).astype(dz_ref.dtype)
        dlb = jnp.sum(flat(dfv * (1.0 - sf)), axis=0, keepdims=True)
        dz_ref[0, :, :LANE] = flat(dq * (sq * (1.0 + hq * (1.0 - sq)))).astype(dz_ref.dtype)
        dl0 = dlb * lbv * (1.0 - lbv)
        _acc(dlb_ref, jnp.concatenate([dl0, -dl0], axis=0), b == 0)
        _acc(dgn_ref, dgn, jnp.logical_and(b == 0, h == 0))

    return _ride_call(
        body, rider, name="hgrn_bwd", grid=(HG_HEADS, b_),
        in_specs=[col(0), col(1), col(2), col(3), blk, vec, one, ANY],
        out_specs=[pl.BlockSpec((1, t_, 4 * LANE), lambda h, b: (b, 0, col0 // 4 + h)), vec, one],
        out_shape=[jax.ShapeDtypeStruct(dz.shape, dz.dtype), jax.ShapeDtypeStruct((2, hw), F32),
                   jax.ShapeDtypeStruct((1, LANE), F32)],
        scratch=[], args=(zm, zm, zm, zm, dy, lb, gn, dz), aliases={7: 0})


def _fox_logf(x):
    return jnp.minimum(x, 0.0) - jnp.log(1.0 + jnp.exp(-jnp.abs(x)))


def _fox_prep(zf, bias):
    b_, t_, _ = zf.shape
    tb = min(FOX_BLOCK, t_)
    nb = t_ // tb

    def body(z_ref, b_ref, fc_ref):
        tril_bf = (lax.broadcasted_iota(jnp.int32, (tb, tb), 0) >= lax.broadcasted_iota(jnp.int32, (tb, tb), 1)).astype(BF16)
        bv = b_ref[...]

        def blk(i, carry):
            rows = pl.ds(pl.multiple_of(i * tb, tb), tb)
            fc = _tri_dot(tril_bf, _fox_logf(z_ref[0, rows, :] + bv)) + carry
            fc_ref[0, rows, :] = fc
            return fc[tb - 1:tb, :]

        lax.fori_loop(0, nb, blk, jnp.zeros((1, LANE), F32))

    blk_spec = pl.BlockSpec((1, t_, LANE), lambda b: (b, 0, 0))
    return pl.pallas_call(
        body, name="fox_prep", grid=(b_,),
        in_specs=[blk_spec, pl.BlockSpec((1, LANE), lambda b: (0, 0))], out_specs=blk_spec,
        out_shape=jax.ShapeDtypeStruct((b_, t_, LANE), F32), compiler_params=_params(1),
    )(zf, bias)


def _fox_post(dfc, zf, bias):
    b_, t_, _ = zf.shape
    npair = dfc.shape[1]
    tb = min(FOX_BLOCK, t_)
    nb = t_ // tb

    def body(d_ref, z_ref, b_ref, dz_ref, db_ref):
        triu_bf = (lax.broadcasted_iota(jnp.int32, (tb, tb), 0) <= lax.broadcasted_iota(jnp.int32, (tb, tb), 1)).astype(BF16)
        valid = lax.broadcasted_iota(jnp.int32, (tb, LANE), 1) < FOX_HEADS
        bv = b_ref[...]

        def blk(m, carry):
            tail, db = carry
            rows = pl.ds(pl.multiple_of((nb - 1 - m) * tb, tb), tb)
            dfc_rows = d_ref[0, 0, rows, :]
            for p in range(1, npair):
                dfc_rows = dfc_rows + pltpu.roll(d_ref[0, p, rows, :], 2 * p, 1)
            dlf = _tri_dot(triu_bf, dfc_rows) + tail
            dx = jnp.where(valid, dlf * _sig(-(z_ref[0, rows, :] + bv)), 0.0)
            dz_ref[0, rows, :] = dx.astype(dz_ref.dtype)
            return dlf[0:1, :], db + jnp.sum(dx, axis=0, keepdims=True)

        z1 = jnp.zeros((1, LANE), F32)
        _, db = lax.fori_loop(0, nb, blk, (z1, z1))
        _acc(db_ref, db, pl.program_id(0) == 0)

    blk_spec = pl.BlockSpec((1, t_, LANE), lambda b: (b, 0, 0))
    vec = pl.BlockSpec((1, LANE), lambda b: (0, 0))
    return pl.pallas_call(
        body, name="fox_post", grid=(b_,),
        in_specs=[pl.BlockSpec((1, npair, t_, LANE), lambda b: (b, 0, 0, 0)), blk_spec, vec], out_specs=[blk_spec, vec],
        out_shape=[jax.ShapeDtypeStruct((b_, t_, LANE), MXU_DTYPE), jax.ShapeDtypeStruct((1, LANE), F32)],
        compiler_params=_params(1),
    )(dfc, zf, bias)


FOX_TILE = 512
FOX_TILE_FWD = 512
FOX_BAND = 512
AUG = 64


def _head_mean_matrix():
    r = lax.broadcasted_iota(jnp.int32, (LANE, LANE), 0) // FOX_DH
    c = lax.broadcasted_iota(jnp.int32, (LANE, LANE), 1) // FOX_DH
    return (r == c).astype(BF16)


def _dot_right_exact(x, m_bf):
    hi = x.astype(BF16)
    lo = (x - hi.astype(F32)).astype(BF16)

    def d(v):
        return lax.dot_general(v, m_bf, (((1,), (0,)), ((), ())), preferred_element_type=F32)

    return d(hi) + d(lo)


def _pair_norm(x, g2, bd):
    r = lax.rsqrt(_dot_right_exact(x * x, bd) * (1.0 / FOX_DH) + EPS)
    return x * r * g2, r


def _pair_norm_bwd(x, r, dy, g2, bd):
    dyg = dy * g2
    dx = r * dyg - x * (r * r * r) * (_dot_right_exact(dyg * x, bd) * (1.0 / FOX_DH))
    return dx, jnp.sum(dy * x * r, axis=0, keepdims=True)


def _head_lanes(xn, hh):
    return xn if hh == 0 else pltpu.roll(xn, FOX_DH, 1)


def _split3(x):
    hi = x.astype(BF16).astype(F32)
    mid = (x - hi).astype(BF16).astype(F32)
    return hi, mid, x - hi - mid


def _fox_operands(q_ref, k_ref, v_ref, fc_ref, gq2, gk2, p, qa, ka, va):
    t_ = q_ref.shape[1]
    bd = _head_mean_matrix()
    lane = lax.broadcasted_iota(jnp.int32, (t_, LANE), 1)
    qx, kx = q_ref[0], k_ref[0]
    qn, rq = _pair_norm(qx, gq2, bd)
    kn, rk = _pair_norm(kx, gk2, bd)
    vv = v_ref[0]
    q_aug = jnp.where(jnp.logical_and(lane >= AUG, lane < AUG + 3), 1.0, 0.0)
    for hh in range(2):
        fcol = jnp.sum(jnp.where(lane == 2 * p + hh, fc_ref[0], 0.0), axis=-1, keepdims=True)
        hi, mid, lo = _split3(-fcol)
        k_aug = jnp.where(lane == AUG, hi, jnp.where(lane == AUG + 1, mid, jnp.where(lane == AUG + 2, lo,
                          jnp.where(lane == AUG + 3, 1.0, 0.0))))
        head = lane < FOX_DH
        qa[hh] = jnp.where(head, _head_lanes(qn, hh), q_aug).astype(MXU_DTYPE)
        ka[hh] = jnp.where(head, _head_lanes(kn, hh), k_aug).astype(MXU_DTYPE)
        va[hh] = jnp.where(head, _head_lanes(vv, hh), 0.0).astype(MXU_DTYPE)
    return bd, lane, qx, kx, rq, rk


def _fox_specs(t_, fw, col0):
    npair = fw // LANE

    def col(off):
        return pl.BlockSpec((1, t_, LANE), lambda b, p: (b, 0, col0 + 3 * p + off))

    pair = pl.BlockSpec((1, t_, LANE), lambda b, p: (b, 0, p))
    full = pl.BlockSpec((1, t_, LANE), lambda b, p: (b, 0, 0))
    gvec = pl.BlockSpec((1, LANE), lambda b, p: (0, 0))
    lse = pl.BlockSpec((1, 1, t_, LANE), lambda b, p: (b, p, 0, 0))
    return col, pair, full, gvec, lse


def _fox_fwd(zm, fc, gq2, gk2, fw, col0, rider=None):
    b_, t_, _ = zm.shape
    npair = fw // LANE
    tq = min(FOX_TILE_FWD, t_)
    bw = min(FOX_BAND, t_)
    nband, tpb = t_ // bw, bw // tq
    scale = FOX_DH ** -0.5
    col, pair, full, gvec, lse_spec = _fox_specs(t_, fw, col0)

    def body(q_ref, k_ref, v_ref, fc_ref, gq_ref, gk_ref, o_ref, lse_ref, qa, ka, va):
        p = pl.program_id(1)
        _fox_operands(q_ref, k_ref, v_ref, fc_ref, gq_ref[...] * scale, gk_ref[...], p, qa, ka, va)
        ahead = lax.broadcasted_iota(jnp.int32, (tq, bw), 1) - lax.broadcasted_iota(jnp.int32, (tq, bw), 0)
        lane = lax.broadcasted_iota(jnp.int32, (tq, LANE), 1)

        for band in range(nband):
            c0 = band * bw

            def qtile(ii, _, c0=c0):
                r0 = pl.multiple_of(c0 + ii * tq, tq)
                rows = pl.ds(r0, tq)
                keep = ahead <= r0 - c0
                res = []
                for hh in range(2):
                    qb = qa[hh, rows, :]
                    s_b = jnp.where(keep, _nt(qb, ka[hh, c0:c0 + bw, :]), NEG)
                    m = jnp.max(s_b, axis=-1, keepdims=True)
                    if c0:
                        s_a = _nt(qb, ka[hh, 0:c0, :])
                        m = jnp.maximum(m, jnp.max(s_a, axis=-1, keepdims=True))
                    p_b = jnp.exp(s_b - m)
                    l = jnp.sum(p_b, axis=-1, keepdims=True)
                    acc = _nn(p_b, va[hh, c0:c0 + bw, :])
                    if c0:
                        p_a = jnp.exp(s_a - m)
                        l = l + jnp.sum(p_a, axis=-1, keepdims=True)
                        acc = acc + _nn(p_a, va[hh, 0:c0, :])
                    res.append((acc / l, m + jnp.log(l)))
                (o0, e0), (o1, e1) = res
                o_ref[0, rows, :] = jnp.where(lane < FOX_DH, o0, pltpu.roll(o1, FOX_DH, 1))
                lse_ref[0, 0, rows, :] = jnp.where(lane == 0, e0, jnp.where(lane == 1, e1, 0.0))
                return 0

            lax.fori_loop(0, tpb, qtile, 0)

    return _ride_call(
        body, rider, name="fox_fwd", grid=(b_, npair),
        in_specs=[col(0), col(1), col(2), full, gvec, gvec],
        out_specs=[pair, lse_spec],
        out_shape=[jax.ShapeDtypeStruct((b_, t_, fw), F32), jax.ShapeDtypeStruct((b_, npair, t_, LANE), F32)],
        scratch=[pltpu.VMEM((2, t_, LANE), MXU_DTYPE)] * 3, args=(zm, zm, zm, fc, gq2, gk2))


def _norm_bwd(x, dy, g):
    r = lax.rsqrt(jnp.mean(x * x, axis=-1, keepdims=True) + EPS)
    dyg = dy * g
    dx = r * dyg - x * (r * r * r) * jnp.mean(dyg * x, axis=-1, keepdims=True)
    return dx, jnp.sum(dy * x * r, axis=0, keepdims=True)


def _fox_bwd(zm, o, do, lse, fc, gq2, gk2, fw, col0, dz, rider=None):
    b_, t_, _ = zm.shape
    npair = fw // LANE
    tq = min(FOX_TILE, t_)
    nb = t_ // tq
    bw = min(FOX_BAND, t_)
    nband, tpb = t_ // bw, bw // tq
    scale = FOX_DH ** -0.5
    col, pair, full, gvec, lse_spec = _fox_specs(t_, fw, col0)

    def body(q_ref, k_ref, v_ref, o_ref, do_ref, lse_ref, fc_ref, gq_ref, gk_ref, _,
             dz_ref, dfc_ref, dgq_ref, dgk_ref, qa, ka, va, da, rowv, dq_acc, dk_acc, dv_acc):
        b, p = pl.program_id(0), pl.program_id(1)
        gq2v, gk2v = gq_ref[...] * scale, gk_ref[...]
        bd, lane, qx, kx, rq, rk = _fox_operands(q_ref, k_ref, v_ref, fc_ref, gq2v, gk2v, p, qa, ka, va)
        head = lane < FOX_DH
        dov = do_ref[0]
        dsum = _dot_right_exact(dov * o_ref[0], bd)
        eye = (lax.broadcasted_iota(jnp.int32, (tq, tq), 0) == lax.broadcasted_iota(jnp.int32, (tq, tq), 1)).astype(F32)
        for hh in range(2):
            da[hh] = jnp.where(head, _head_lanes(dov, hh), 0.0).astype(MXU_DTYPE)
            for blk in range(nb):
                rs = slice(blk * tq, (blk + 1) * tq)
                rowv[2 * hh:2 * hh + 1, rs] = jnp.sum(eye * lse_ref[0, 0, rs, hh:hh + 1], axis=0, keepdims=True)
                rowv[2 * hh + 1:2 * hh + 2, rs] = jnp.sum(eye * dsum[rs, hh * FOX_DH:hh * FOX_DH + 1], axis=0, keepdims=True)
        dq_acc[...] = jnp.zeros(dq_acc.shape, F32)
        ahead = lax.broadcasted_iota(jnp.int32, (tq, bw), 1) - lax.broadcasted_iota(jnp.int32, (tq, bw), 0)

        def part(hh, kb, vb, lo, hi, keep):
            qm, dm = qa[hh, lo:hi, :], da[hh, lo:hi, :]
            pt = jnp.exp(_nt(kb, qm) - rowv[2 * hh:2 * hh + 1, lo:hi])
            if keep is not None:
                pt = jnp.where(keep, pt, 0.0)
            dst = pt * (_nt(vb, dm) - rowv[2 * hh + 1:2 * hh + 2, lo:hi])
            dq_acc[hh, lo:hi, :] += _tn(dst, kb)
            return _nn(dst, qm), _nn(pt, dm)

        for band in range(nband):
            c0 = band * bw

            def kvtile(jj, _, c0=c0):
                r0 = pl.multiple_of(c0 + jj * tq, tq)
                rows = pl.ds(r0, tq)
                keep = ahead >= r0 - c0
                for hh in range(2):
                    kb, vb = ka[hh, rows, :], va[hh, rows, :]
                    dk_t, dv_t = part(hh, kb, vb, c0, c0 + bw, keep)
                    if c0 + bw < t_:
                        dk_u, dv_u = part(hh, kb, vb, c0 + bw, t_, None)
                        dk_t, dv_t = dk_t + dk_u, dv_t + dv_u
                    dk_acc[hh, rows, :] = dk_t
                    dv_acc[hh, rows, :] = dv_t
                return 0

            lax.fori_loop(0, tpb, kvtile, 0)

        dq0, dq1, dk0, dk1 = dq_acc[0], dq_acc[1], dk_acc[0], dk_acc[1]
        dqn = jnp.where(head, dq0, pltpu.roll(dq1, FOX_DH, 1))
        dkn = jnp.where(head, dk0, pltpu.roll(dk1, FOX_DH, 1))
        dqx, gq_part = _pair_norm_bwd(qx, rq, dqn, gq2v, bd)
        dkx, gk_part = _pair_norm_bwd(kx, rk, dkn, gk2v, bd)
        dz_ref[0, :, :LANE] = dqx.astype(dz_ref.dtype)
        dz_ref[0, :, LANE:2 * LANE] = dkx.astype(dz_ref.dtype)
        dz_ref[0, :, 2 * LANE:] = jnp.where(head, dv_acc[0], pltpu.roll(dv_acc[1], FOX_DH, 1)).astype(dz_ref.dtype)

        def bias_grad(dqh, dkh):
            return dqh[:, AUG + 3:AUG + 4] - dkh[:, AUG:AUG + 1]

        dfc_ref[0, 0] = jnp.where(lane == 0, bias_grad(dq0, dk0), jnp.where(lane == 1, bias_grad(dq1, dk1), 0.0))
        first = jnp.logical_and(b == 0, p == 0)
        _acc(dgq_ref, gq_part * scale, first)
        _acc(dgk_ref, gk_part, first)

    gs = jax.ShapeDtypeStruct((1, LANE), F32)
    return _ride_call(
        body, rider, name="fox_bwd", grid=(b_, npair),
        in_specs=[col(0), col(1), col(2), pair, pair, lse_spec, full, gvec, gvec, ANY],
        out_specs=[pl.BlockSpec((1, t_, 3 * LANE), lambda b, p: (b, 0, col0 // 3 + p)), lse_spec, gvec, gvec],
        out_shape=[jax.ShapeDtypeStruct(dz.shape, dz.dtype), jax.ShapeDtypeStruct((b_, npair, t_, LANE), F32), gs, gs],
        scratch=[pltpu.VMEM((2, t_, LANE), MXU_DTYPE)] * 4
        + [pltpu.VMEM((8, t_), F32)] + [pltpu.VMEM((2, t_, LANE), F32)] * 3,
        args=(zm, zm, zm, o, do, lse, fc, gq2, gk2, dz), aliases={9: 0})


def _mem_specs(t_, m_, mw, col0):
    nh = mw // LANE
    qcol = pl.BlockSpec((1, t_, LANE), lambda b, h: (b, 0, col0 + h))
    kcol = pl.BlockSpec((1, m_, LANE), lambda b, h: (b, 0, h))
    vcol = pl.BlockSpec((1, m_, LANE), lambda b, h: (b, 0, nh + h))
    ycol = pl.BlockSpec((1, t_, LANE), lambda b, h: (b, 0, h))
    gvec = pl.BlockSpec((1, LANE), lambda b, h: (0, 0))
    return qcol, kcol, vcol, ycol, gvec


def _mem_fwd(zm, mkv, gq, gk, mw, col0):
    b_, t_, _ = zm.shape
    m_ = mkv.shape[1]
    tq = min(MEM_TILE, t_)
    nb = t_ // tq
    scale = MEM_DH ** -0.5
    qcol, kcol, vcol, ycol, gvec = _mem_specs(t_, m_, mw, col0)

    def body(q_ref, k_ref, v_ref, gq_ref, gk_ref, y_ref):
        gqv, gkv = gq_ref[...] * scale, gk_ref[...]
        kv = k_ref[0]
        kn = _mx(kv * lax.rsqrt(jnp.mean(kv * kv, axis=-1, keepdims=True) + EPS) * gkv)
        vv = _mx(v_ref[0])

        def blk(i, _):
            rows = pl.ds(pl.multiple_of(i * tq, tq), tq)
            qv = q_ref[0, rows, :]
            s = _nt(qv * lax.rsqrt(jnp.mean(qv * qv, axis=-1, keepdims=True) + EPS) * gqv, kn)
            e = jnp.exp(s - jnp.max(s, axis=-1, keepdims=True))
            y_ref[0, rows, :] = _nn(e / jnp.sum(e, axis=-1, keepdims=True), vv)
            return 0

        lax.fori_loop(0, nb, blk, 0)

    return pl.pallas_call(
        body, name="mem_fwd", grid=(b_, MEM_HEADS), in_specs=[qcol, kcol, vcol, gvec, gvec], out_specs=ycol,
        out_shape=jax.ShapeDtypeStruct((b_, t_, mw), F32), compiler_params=_params(2),
    )(zm, mkv, mkv, gq, gk)


def _mem_bwd(zm, mkv, dy, gq, gk, mw, col0, dz):
    b_, t_, _ = zm.shape
    m_ = mkv.shape[1]
    tq = min(MEM_TILE, t_)
    nb = t_ // tq
    scale = MEM_DH ** -0.5
    qcol, kcol, vcol, ycol, gvec = _mem_specs(t_, m_, mw, col0)

    def body(q_ref, k_ref, v_ref, dy_ref, gq_ref, gk_ref, _, dq_ref, dk_ref, dv_ref, dgq_ref, dgk_ref):
        gqv, gkv = gq_ref[...] * scale, gk_ref[...]
        kv = k_ref[0]
        kn = _mx(kv * lax.rsqrt(jnp.mean(kv * kv, axis=-1, keepdims=True) + EPS) * gkv)
        vv = _mx(v_ref[0])

        def blk(i, carry):
            dkn, dvv, dgq = carry
            rows = pl.ds(pl.multiple_of(i * tq, tq), tq)
            qv = q_ref[0, rows, :]
            qn = _mx(qv * lax.rsqrt(jnp.mean(qv * qv, axis=-1, keepdims=True) + EPS) * gqv)
            s = _nt(qn, kn)
            e = jnp.exp(s - jnp.max(s, axis=-1, keepdims=True))
            pm = e / jnp.sum(e, axis=-1, keepdims=True)
            dob = _mx(dy_ref[0, rows, :])
            dp = _nt(dob, vv)
            ds = pm * (dp - jnp.sum(dp * pm, axis=-1, keepdims=True))
            dqv, gq_part = _norm_bwd(qv, _nn(ds, kn), gqv)
            dq_ref[0, rows, :] = dqv.astype(dq_ref.dtype)
            return dkn + _tn(ds, qn), dvv + _tn(pm, dob), dgq + gq_part * scale

        z = jnp.zeros((m_, LANE), F32)
        dkn, dvv, dgq = lax.fori_loop(0, nb, blk, (z, z, jnp.zeros((1, LANE), F32)))
        dkv, dgk = _norm_bwd(kv, dkn, gkv)
        dk_ref[0] = dkv
        dv_ref[0] = dvv
        first = jnp.logical_and(pl.program_id(0) == 0, pl.program_id(1) == 0)
        _acc(dgq_ref, dgq, first)
        _acc(dgk_ref, dgk, first)

    kblk = pl.BlockSpec((1, m_, LANE), lambda b, h: (b, 0, h))
    gs = jax.ShapeDtypeStruct((1, LANE), F32)
    ks = jax.ShapeDtypeStruct((b_, m_, mw), F32)
    return pl.pallas_call(
        body, name="mem_bwd", grid=(b_, MEM_HEADS), in_specs=[qcol, kcol, vcol, ycol, gvec, gvec, ANY],
        out_specs=[qcol, kblk, kblk, gvec, gvec],
        out_shape=[jax.ShapeDtypeStruct(dz.shape, dz.dtype), ks, ks, gs, gs], input_output_aliases={6: 0},
        compiler_params=_params(2),
    )(zm, mkv, mkv, dy, gq, gk, dz)


def _merge_specs(tm, d, w, gcol):
    row_d = pl.BlockSpec((tm, d), lambda i: (i, 0))
    row_w = pl.BlockSpec((tm, w), lambda i: (i, 0))
    gates = [pl.BlockSpec((tm, d), functools.partial(lambda i, k: (i, gcol + k), k=k)) for k in range(3)]
    w_br = pl.BlockSpec((w, d), lambda i: (0, 0))
    w_o = pl.BlockSpec((d, d), lambda i: (0, 0))
    return row_d, row_w, gates, w_br, w_o


def _merge_fwd(x, ys, zm, w_brs, w_out, gcol, g_next, tm=256):
    n, d = x.shape
    w = ys[0].shape[1]
    tm = _tile(n, tm, 8)
    row_d, row_w, gates, w_br, w_o = _merge_specs(tm, d, w, gcol)

    def body(x_ref, ya, yb, yc, g0, g1, g2, wa, wb, wc, wo, gn_ref, x1_ref, mg_ref, h_ref):
        mg = (_sig(g0[...]) * _nn(ya[...], wa[...]) + _sig(g1[...]) * _nn(yb[...], wb[...])
              + _sig(g2[...]) * _nn(yc[...], wc[...]))
        mg_ref[...] = mg.astype(mg_ref.dtype)
        x1 = x_ref[...] + _nn(mg, wo[...])
        x1_ref[...] = x1
        h_ref[...] = (x1 * lax.rsqrt(jnp.mean(x1 * x1, axis=-1, keepdims=True) + EPS) * gn_ref[...]).astype(h_ref.dtype)

    half = jax.ShapeDtypeStruct((n, d), MXU_DTYPE)
    return pl.pallas_call(
        body, name="merge_fwd", grid=(n // tm,),
        in_specs=[row_d, row_w, row_w, row_w] + gates + [w_br, w_br, w_br, w_o, pl.BlockSpec((1, d), lambda i: (0, 0))],
        out_specs=[row_d, row_d, row_d],
        out_shape=[jax.ShapeDtypeStruct((n, d), F32), half, half],
        compiler_params=_params(1),
    )(x, *ys, zm, zm, zm, *w_brs, w_out, g_next)


def _merge_bwd(dx1, ys, zm, w_brs, w_out, gcol, tm=256):
    n, d = dx1.shape
    w = ys[0].shape[1]
    tm = _tile(n, tm, 8)
    row_d, row_w, gates, w_br, w_o = _merge_specs(tm, d, w, gcol)

    def body(dx_ref, ya, yb, yc, g0, g1, g2, wa, wb, wc, wo, dgl_ref, dpa, dpb, dpc, dya, dyb, dyc):
        dm = _nt(dx_ref[...], wo[...])
        for k, (y, g, wr, dp_ref, dy_ref) in enumerate(((ya, g0, wa, dpa, dya), (yb, g1, wb, dpb, dyb),
                                                        (yc, g2, wc, dpc, dyc))):
            sg = _sig(g[...])
            pr = _nn(y[...], wr[...])
            dgl_ref[:, k * d:(k + 1) * d] = (dm * pr * sg * (1.0 - sg)).astype(dgl_ref.dtype)
            dp = (dm * sg).astype(dp_ref.dtype)
            dp_ref[...] = dp
            dy_ref[...] = _nt(dp, wr[...])

    sd = jax.ShapeDtypeStruct((n, d), MXU_DTYPE)
    sw = jax.ShapeDtypeStruct((n, w), F32)
    return pl.pallas_call(
        body, name="merge_bwd", grid=(n // tm,),
        in_specs=[row_d, row_w, row_w, row_w] + gates + [w_br, w_br, w_br, w_o],
        out_specs=[pl.BlockSpec((tm, 3 * d), lambda i: (i, 0)), row_d, row_d, row_d, row_w, row_w, row_w],
        out_shape=[jax.ShapeDtypeStruct((n, zm.shape[1]), MXU_DTYPE), sd, sd, sd, sw, sw, sw],
        compiler_params=_params(1),
    )(dx1, *ys, zm, zm, zm, *w_brs, w_out)


CONV_ROWS = 512
HALO = 8


def _ext(ref, r0, t_):
    rc = min(CONV_ROWS, t_)
    a, b = max(r0 - HALO, 0), min(r0 + rc + HALO, t_)
    parts = []
    if r0 - HALO < 0:
        parts.append(jnp.zeros((HALO, ref.shape[2]), F32))
    parts.append(ref[0, a:b, :].astype(F32))
    if r0 + rc + HALO > t_:
        parts.append(jnp.zeros((HALO, ref.shape[2]), F32))
    return jnp.concatenate(parts, axis=0) if len(parts) > 1 else parts[0]


def _gelu_parts(ac):
    e = jnp.exp(-0.5 * ac * ac)
    t = 1.0 / (1.0 + (0.3275911 * 2.0 ** -0.5) * jnp.abs(ac))
    tail = (0.5 * e) * (t * (0.254829592 + t * (-0.284496736 + t * (1.421413741 + t * (-1.453152027 + t * 1.061405429)))))
    return jnp.where(ac < 0, tail, 1.0 - tail), e * ((2.0 * math.pi) ** -0.5)


def _conv_taps(a_ext, cw, cb):
    a2, a1 = pltpu.roll(a_ext, 2, 0), pltpu.roll(a_ext, 1, 0)
    return cw[0:1, :] * a2 + cw[1:2, :] * a1 + cw[2:3, :] * a_ext + cb, a2, a1


def _glu_specs(t_, f, g):
    gate = pl.BlockSpec((1, t_, g), lambda j, b: (b, 0, j))
    value = pl.BlockSpec((1, t_, g), lambda j, b: (b, 0, f // g + j))
    cwb = pl.BlockSpec((3, g), lambda j, b: (0, j))
    cbb = pl.BlockSpec((1, g), lambda j, b: (0, j))
    return gate, value, cwb, cbb


def _glu_fwd(u, cw, cb):
    b_, t_, f2 = u.shape
    f = f2 // 2
    g = min(FFN_GROUP, f)
    rc = min(CONV_ROWS, t_)
    gate, value, cwb, cbb = _glu_specs(t_, f, g)

    def body(a_ref, v_ref, cw_ref, cb_ref, y_ref):
        cwv, cbv = cw_ref[...], cb_ref[...]
        for r0 in range(0, t_, rc):
            ac = _conv_taps(_ext(a_ref, r0, t_), cwv, cbv)[0][HALO:HALO + rc]
            cdf, _ = _gelu_parts(ac)
            y_ref[0, r0:r0 + rc, :] = (ac * cdf * v_ref[0, r0:r0 + rc, :]).astype(y_ref.dtype)

    return pl.pallas_call(
        body, name="glu_fwd", grid=(f // g, b_), in_specs=[gate, value, cwb, cbb], out_specs=gate,
        out_shape=jax.ShapeDtypeStruct((b_, t_, f), MXU_DTYPE), compiler_params=_params(2),
    )(u, u, cw, cb)


def _glu_bwd(u, dy, cw, cb):
    b_, t_, f2 = u.shape
    f = f2 // 2
    g = min(FFN_GROUP, f)
    rc = min(CONV_ROWS, t_)
    ne = rc + 2 * HALO
    gate, value, cwb, cbb = _glu_specs(t_, f, g)

    def body(a_ref, v_ref, dy_ref, cw_ref, cb_ref, da_ref, dv_ref, dcw_ref, dcb_ref):
        cwv, cbv = cw_ref[...], cb_ref[...]
        dcw = [jnp.zeros((1, g), F32) for _ in range(3)]
        dcb = jnp.zeros((1, g), F32)
        for r0 in range(0, t_, rc):
            a_ext, v_ext, dy_ext = _ext(a_ref, r0, t_), _ext(v_ref, r0, t_), _ext(dy_ref, r0, t_)
            ac, a2, a1 = _conv_taps(a_ext, cwv, cbv)
            cdf, pdf = _gelu_parts(ac)
            dac = dy_ext * v_ext * (cdf + ac * pdf)
            da = cwv[2:3, :] * dac + cwv[1:2, :] * pltpu.roll(dac, ne - 1, 0) + cwv[0:1, :] * pltpu.roll(dac, ne - 2, 0)
            mid = slice(HALO, HALO + rc)
            da_ref[0, r0:r0 + rc, :] = da[mid].astype(da_ref.dtype)
            dv_ref[0, r0:r0 + rc, :] = (dy_ext[mid] * ac[mid] * cdf[mid]).astype(dv_ref.dtype)
            dacm = dac[mid]
            dcw[0] = dcw[0] + jnp.sum(dacm * a2[mid], axis=0, keepdims=True)
            dcw[1] = dcw[1] + jnp.sum(dacm * a1[mid], axis=0, keepdims=True)
            dcw[2] = dcw[2] + jnp.sum(dacm * a_ext[mid], axis=0, keepdims=True)
            dcb = dcb + jnp.sum(dacm, axis=0, keepdims=True)
        first = pl.program_id(1) == 0
        _acc(dcw_ref, jnp.concatenate(dcw, axis=0), first)
        _acc(dcb_ref, dcb, first)

    sds = jax.ShapeDtypeStruct((b_, t_, f), MXU_DTYPE)
    return pl.pallas_call(
        body, name="glu_bwd", grid=(f // g, b_), in_specs=[gate, value, gate, cwb, cbb],
        out_specs=[gate, gate, cwb, cbb],
        out_shape=[sds, sds, jax.ShapeDtypeStruct((3, f), F32), jax.ShapeDtypeStruct((1, f), F32)],
        compiler_params=_params(2),
    )(u, u, dy, cw, cb)


def _place():
    x, y, c = lax.axis_index("x"), lax.axis_index("y"), lax.axis_index("c")
    chips = [(1 - x, y), (x, 1 - y), (1 - x, 1 - y)]
    return x, y, c, chips


def _remote(src, dst, send_sem, recv_sem, to):
    return pltpu.make_async_remote_copy(src_ref=src, dst_ref=dst, send_sem=send_sem, recv_sem=recv_sem,
                                        device_id=to, device_id_type=MESH)


STACK, COLS = "stack", "cols"


def _shard_ref(ref, kind, s, rows, c):
    if kind == COLS:
        cols = pl.ds(pl.multiple_of(s * c, LANE), c)
        return ref.at[:, cols] if rows is None else ref.at[rows, cols]
    return ref.at[s] if rows is None else ref.at[s, rows, :]


def _halves(c, half):
    mine = pl.ds(pl.multiple_of(c * half, 16), half)
    theirs = pl.ds(pl.multiple_of((1 - c) * half, 16), half)
    return mine, theirs


def _gather_parts(kinds):
    def first_copies(ins, outs, sems):
        x, y, c, chips = _place()
        me = 2 * x + y
        cps = []
        for i, (w_ref, o_ref, kind) in enumerate(zip(ins, outs, kinds)):
            r, cw = w_ref.shape
            mine, _ = _halves(c, r // 2)
            for j, chip in enumerate(chips):
                cps.append(_remote(w_ref.at[mine], _shard_ref(o_ref, kind, me, mine, cw), sems[0].at[6 * i + j],
                                   sems[1].at[6 * i + j], (*chip, c)))
        return cps

    def start(ins, outs, sems):
        for cp in first_copies(ins, outs, sems):
            cp.start()

    def finish(ins, outs, sems):
        x, y, c, chips = _place()
        sib = (x, y, 1 - c)
        passed = []
        for i, (w_ref, o_ref, kind) in enumerate(zip(ins, outs, kinds)):
            r, cw = w_ref.shape
            mine, _ = _halves(c, r // 2)
            for j, (px, py) in enumerate(chips):
                blk = _shard_ref(o_ref, kind, 2 * px + py, mine, cw)
                _remote(blk, blk, sems[0].at[6 * i + j], sems[1].at[6 * i + j], sib).wait_recv()
                passed.append(_remote(blk, blk, sems[0].at[6 * i + 3 + j], sems[1].at[6 * i + 3 + j], sib))
                passed[-1].start()
        for i, (w_ref, o_ref, kind) in enumerate(zip(ins, outs, kinds)):
            r, cw = w_ref.shape
            _, theirs = _halves(c, r // 2)
            for j, (px, py) in enumerate(chips):
                blk = _shard_ref(o_ref, kind, 2 * px + py, theirs, cw)
                _remote(blk, blk, sems[0].at[6 * i + 3 + j], sems[1].at[6 * i + 3 + j], sib).wait_recv()
        for cp in first_copies(ins, outs, sems) + passed:
            cp.wait_send()

    return start, finish


def _gather_shapes(shards, kinds):
    return [jax.ShapeDtypeStruct((a.shape[0], N_CHIPS * a.shape[1]) if k == COLS else (N_CHIPS,) + a.shape, a.dtype)
            for a, k in zip(shards, kinds)]


def _gather_sems(nw):
    return [pltpu.SemaphoreType.DMA((6 * nw,)), pltpu.SemaphoreType.DMA((6 * nw,))]


def _gather_shards(shards, kinds):
    nw = len(shards)
    start, finish = _gather_parts(kinds)

    def body(*refs):
        ins, outs, sems = refs[:nw], refs[nw:2 * nw], refs[2 * nw:]
        start(ins, outs, sems)
        finish(ins, outs, sems)

    return pl.pallas_call(
        body, name="gather_shards", in_specs=[ANY] * nw, out_specs=[ANY] * nw,
        out_shape=_gather_shapes(shards, kinds), scratch_shapes=_gather_sems(nw),
    )(*shards)


def _gather_rider(shards, kinds):
    start, finish = _gather_parts(kinds)
    return _Rider(list(shards), _gather_shapes(shards, kinds), _gather_sems(len(shards)), start, finish)


def _half_shape(g, kind):
    if kind == COLS:
        return (g.shape[0] // 2, g.shape[1])
    return (g.shape[0], g.shape[1] // 2, g.shape[2])


def _swap_parts(kinds):
    def copies(ins, outs, sems):
        x, y, c, _ = _place()
        cps = []
        for i, (g_ref, a_ref, kind) in enumerate(zip(ins, outs, kinds)):
            r = g_ref.shape[0] if kind == COLS else g_ref.shape[1]
            _, theirs = _halves(c, r // 2)
            src = g_ref.at[theirs] if kind == COLS else g_ref.at[:, theirs]
            cps.append(_remote(src, a_ref, sems[0].at[i], sems[1].at[i], (x, y, 1 - c)))
        return cps

    def start(ins, outs, sems):
        for cp in copies(ins, outs, sems):
            cp.start()

    def finish(ins, outs, sems):
        for cp in copies(ins, outs, sems):
            cp.wait()

    return start, finish


def _swap_shapes(gs, kinds):
    return [jax.ShapeDtypeStruct(_half_shape(g, k), g.dtype) for g, k in zip(gs, kinds)]


def _pair_swap_halves(gs, kinds, name):
    nw = len(gs)
    start, finish = _swap_parts(kinds)

    def body(*refs):
        ins, outs, sems = refs[:nw], refs[nw:2 * nw], refs[2 * nw:]
        start(ins, outs, sems)
        finish(ins, outs, sems)

    return pl.pallas_call(
        body, name=name, in_specs=[ANY] * nw, out_specs=[ANY] * nw, out_shape=_swap_shapes(gs, kinds),
        scratch_shapes=[pltpu.SemaphoreType.DMA((nw,)), pltpu.SemaphoreType.DMA((nw,))],
    )(*gs)


def _swap_rider(gs, kinds):
    start, finish = _swap_parts(kinds)
    nw = len(gs)
    return _Rider(list(gs), _swap_shapes(gs, kinds), [pltpu.SemaphoreType.DMA((nw,)), pltpu.SemaphoreType.DMA((nw,))],
                  start, finish)


def _row_tile(rows, width, itemsize=4, target=2 ** 21):
    return _tile(rows, max(8, target // (width * itemsize)), 8)


def _add_half(g, a, kind, c_idx, name):
    if kind == COLS:
        half, wd = a.shape
        tr = _row_tile(half, wd)
        nblk = half // tr
        grid = (nblk,)
        g_spec = pl.BlockSpec((tr, wd), lambda i, c_ref: (c_ref[0] * nblk + i, 0))
        a_spec = pl.BlockSpec((tr, wd), lambda i, c_ref: (i, 0))
    else:
        n, half, wd = a.shape
        tr = _row_tile(half, wd)
        nblk = half // tr
        grid = (n, nblk)
        g_spec = pl.BlockSpec((1, tr, wd), lambda s, i, c_ref: (s, c_ref[0] * nblk + i, 0))
        a_spec = pl.BlockSpec((1, tr, wd), lambda s, i, c_ref: (s, i, 0))

    def body(c_ref, g_ref, a_ref, o_ref):
        o_ref[...] = (g_ref[...] + a_ref[...]).astype(o_ref.dtype)

    return pl.pallas_call(
        body, name=name,
        grid_spec=pltpu.PrefetchScalarGridSpec(num_scalar_prefetch=1, grid=grid, in_specs=[g_spec, a_spec],
                                               out_specs=a_spec),
        out_shape=jax.ShapeDtypeStruct(a.shape, EXCHANGE_DTYPE), compiler_params=_params(len(grid)),
    )(c_idx, g, a)


def _exchange_parts(kinds):
    def copies(ins, outs, sems):
        x, y, c, chips = _place()
        me = 2 * x + y
        cps = []
        for i, (p_ref, b_ref, kind) in enumerate(zip(ins, outs, kinds)):
            cw = b_ref.shape[2]
            for j, (px, py) in enumerate(chips):
                cps.append(_remote(_shard_ref(p_ref, kind, 2 * px + py, None, cw), b_ref.at[me],
                                   sems[0].at[3 * i + j], sems[1].at[3 * i + j], (px, py, c)))
        return cps

    def start(ins, outs, sems):
        for cp in copies(ins, outs, sems):
            cp.start()

    def finish(ins, outs, sems):
        x, y, c, chips = _place()
        for i, b_ref in enumerate(outs):
            for j, (px, py) in enumerate(chips):
                blk = b_ref.at[2 * px + py]
                _remote(blk, blk, sems[0].at[3 * i + j], sems[1].at[3 * i + j], (px, py, c)).wait_recv()
        for cp in copies(ins, outs, sems):
            cp.wait_send()

    return start, finish


def _exchange_shapes(ps, kinds):
    return [jax.ShapeDtypeStruct((N_CHIPS,) + ((p.shape[0], p.shape[1] // N_CHIPS) if k == COLS else tuple(p.shape[1:])),
                                 p.dtype) for p, k in zip(ps, kinds)]


def _exchange_sems(nw):
    return [pltpu.SemaphoreType.DMA((3 * nw,)), pltpu.SemaphoreType.DMA((3 * nw,))]


def _exchange_rider(ps, kinds):
    start, finish = _exchange_parts(kinds)
    return _Rider(list(ps), _exchange_shapes(ps, kinds), _exchange_sems(len(ps)), start, finish)


def _sum_chips(bq, name):
    n, h, wd = bq.shape
    tr = _row_tile(h, wd * n)

    def body(b_ref, o_ref):
        acc = b_ref[0].astype(F32)
        for s in range(1, n):
            acc = acc + b_ref[s].astype(F32)
        o_ref[...] = acc

    return pl.pallas_call(
        body, name=name, grid=(h // tr,),
        in_specs=[pl.BlockSpec((n, tr, wd), lambda i: (0, i, 0))], out_specs=pl.BlockSpec((tr, wd), lambda i: (i, 0)),
        out_shape=jax.ShapeDtypeStruct((h, wd), F32), compiler_params=_params(1),
    )(bq)


def _pair_join_halves(qs):
    nw = len(qs)

    def body(*refs):
        ins, outs = refs[:nw], refs[nw:2 * nw]
        send_sems, recv_sems = refs[2 * nw:]
        x, y, c, _ = _place()
        sent = []
        for i, (q_ref, o_ref) in enumerate(zip(ins, outs)):
            sent.append(_remote(q_ref, o_ref.at[c], send_sems.at[i], recv_sems.at[i], (x, y, 1 - c)))
            sent[-1].start()
        for i, (q_ref, o_ref) in enumerate(zip(ins, outs)):
            _remote(q_ref, o_ref.at[1 - c], send_sems.at[i], recv_sems.at[i], (x, y, 1 - c)).wait_recv()
        for cp in sent:
            cp.wait_send()

    return pl.pallas_call(
        body, name="pair_join_halves", in_specs=[ANY] * nw, out_specs=[ANY] * nw,
        out_shape=[jax.ShapeDtypeStruct((2,) + q.shape, q.dtype) for q in qs],
        scratch_shapes=[pltpu.SemaphoreType.DMA((nw,)), pltpu.SemaphoreType.DMA((nw,))],
    )(*qs)


def _all_sum_small(s, name):
    sr, w = s.shape

    def body(s_ref, o_ref, buf, send_sems, recv_sems):
        x, y, c, _ = _place()
        me = 4 * x + 2 * y + c
        buf[me] = s_ref[...]
        peers = []
        for k in range(1, 8):
            px = 1 - x if k & 4 else x
            py = 1 - y if k & 2 else y
            pc = 1 - c if k & 1 else c
            peers.append((px, py, pc))
        sent = [_remote(s_ref, buf.at[me], send_sems.at[k], recv_sems.at[k], peer) for k, peer in enumerate(peers)]
        for cp in sent:
            cp.start()
        for k, (px, py, pc) in enumerate(peers):
            _remote(s_ref, buf.at[4 * px + 2 * py + pc], send_sems.at[k], recv_sems.at[k], (px, py, pc)).wait_recv()
        for cp in sent:
            cp.wait_send()
        acc = buf[0]
        for d in range(1, 8):
            acc = acc + buf[d]
        o_ref[...] = acc

    vm = pl.BlockSpec(memory_space=pltpu.VMEM)
    return pl.pallas_call(
        body, name=name, in_specs=[vm], out_specs=vm, out_shape=jax.ShapeDtypeStruct((sr, w), F32),
        scratch_shapes=[pltpu.VMEM((8, sr, w), F32), pltpu.SemaphoreType.DMA((7,)), pltpu.SemaphoreType.DMA((7,))],
    )(s)


BIG = ("w_in", "mem_kv_w", "w_br_hgrn", "w_br_fox", "w_br_mem", "w_out", "ffn_w_up", "ffn_w_down")
KIND = {"w_in": STACK, "mem_kv_w": STACK, "w_br_hgrn": COLS, "w_br_fox": COLS, "w_br_mem": COLS, "w_out": STACK,
        "ffn_w_up": STACK, "ffn_w_down": STACK}
ROW_SHARDED = ("mem_kv_w", "w_out", "ffn_w_down")
FIRST = ("w_in",)
REST = tuple(nm for nm in BIG if nm not in FIRST)
LATE = {"in_proj": tuple(nm for nm in REST if not nm.startswith("ffn_")),
        "fox_fwd": tuple(nm for nm in REST if nm.startswith("ffn_"))}
LAST = ("w_in",)
TRANSPOSED = ("w_in",)


def _z_layout(d, hw, fw, mw):
    gate, npair, nh, nm = 3 * d // LANE, fw // LANE, hw // LANE, mw // LANE
    fox0, hg0 = gate, gate + 3 * npair
    o_fox, o_mem = 4 * nh, 4 * nh + 3 * npair
    order = [o_mem + nm + j for j in range(gate)]
    order += [o_fox + k * npair + p for p in range(npair) for k in range(3)]
    order += [k * nh + h for h in range(nh) for k in range(4)]
    order += [o_mem + h for h in range(nm)]
    assert fox0 % 3 == 0 and hg0 % 4 == 0
    return fox0, hg0, hg0 + 4 * nh, order


def _reorder_blocks(a, order):
    runs, start = [], 0
    for i in range(1, len(order) + 1):
        if i == len(order) or order[i] != order[i - 1] + 1:
            runs.append((order[start], order[i - 1] + 1))
            start = i
    return jnp.concatenate([a[:, lo * LANE:hi * LANE] for lo, hi in runs], axis=1)


def _put_shard(arr, kind, s, piece):
    if kind == COLS:
        return lax.dynamic_update_slice(arr, piece, (0, s * piece.shape[1]))
    return lax.dynamic_update_slice(arr, piece[None], (s, 0, 0))


def _take_shard(arr, kind, s):
    if kind == COLS:
        return lax.dynamic_slice(arr, (0, s * (arr.shape[1] // N_CHIPS)), (arr.shape[0], arr.shape[1] // N_CHIPS))
    return lax.dynamic_index_in_dim(arr, s, 0, keepdims=False)


def _w_in_pieces(cs, s1, nf):
    out = []
    for s in range(N_CHIPS):
        lo, hi = cs * s, cs * (s + 1)
        for a, b, forget in ((lo, min(hi, s1), False), (max(lo, s1), min(hi, s1 + nf), True), (max(lo, s1 + nf), hi, False)):
            if a < b:
                out.append((s, a - lo, b - lo, forget, a - s1 if forget else (a if a < s1 else a - nf)))
    return out


def _split_w_in(stacked, s1, nf):
    pieces = _w_in_pieces(stacked.shape[2], s1, nf)
    main = [stacked[s, :, a:b] for s, a, b, forget, _ in pieces if not forget]
    ff = [stacked[s, :, a:b] for s, a, b, forget, _ in pieces if forget]
    return jnp.concatenate(main, axis=1), jnp.concatenate(ff, axis=1)


def _join_w_in(g_main, g_ff, s1, nf):
    cs = (g_main.shape[1] + nf) // N_CHIPS
    shards = [[] for _ in range(N_CHIPS)]
    for s, a, b, forget, off in _w_in_pieces(cs, s1, nf):
        shards[s].append((g_ff if forget else g_main)[:, off:off + b - a])
    return jnp.stack([jnp.concatenate(p, axis=1) if len(p) > 1 else p[0] for p in shards])


SMALL = ("norm_mix_g", "norm_mem_g", "norm_ffn_g", "hgrn_lb_logits", "hgrn_norm_g", "fox_f_bias", "fox_q_norm_g",
         "fox_k_norm_g", "mem_q_norm_g", "mem_k_norm_g", "ffn_conv_b")


def _small_rows(shapes):
    rows = []
    for a, (r, c) in enumerate(shapes):
        for i in range(r):
            for lo in range(0, c, FLAT_W):
                rows.append((a, i, lo, min(FLAT_W, c - lo)))
    return rows


def _pack_small(vals):
    rows = _small_rows([v.shape for v in vals])
    sr = -(-len(rows) // 8) * 8

    def body(*refs):
        o_ref = refs[-1]
        o_ref[...] = jnp.zeros(o_ref.shape, F32)
        for k, (a, i, lo, wd) in enumerate(rows):
            o_ref[k:k + 1, 0:wd] = refs[a][i:i + 1, lo:lo + wd]

    vm = pl.BlockSpec(memory_space=pltpu.VMEM)
    return pl.pallas_call(body, name="pack_small", in_specs=[vm] * len(vals), out_specs=vm,
                          out_shape=jax.ShapeDtypeStruct((sr, FLAT_W), F32))(*vals)


def _row_of(buf_ref, rows, a, i):
    parts = [buf_ref[k:k + 1, 0:wd] for k, (a2, i2, _, wd) in enumerate(rows) if (a2, i2) == (a, i)]
    return jnp.concatenate(parts, axis=1) if len(parts) > 1 else parts[0]


def _unpack_small(buf, shapes):
    rows = _small_rows(shapes)

    def body(buf_ref, *outs):
        for a, (r, _) in enumerate(shapes):
            for i in range(r):
                outs[a][i:i + 1, :] = _row_of(buf_ref, rows, a, i)

    vm = pl.BlockSpec(memory_space=pltpu.VMEM)
    return pl.pallas_call(body, name="unpack_small", in_specs=[vm], out_specs=[vm] * len(shapes),
                          out_shape=[jax.ShapeDtypeStruct(shp, F32) for shp in shapes])(buf)


def _adamw_small(buf, shapes, ws, ms, vs):
    n = len(ws)
    rows = _small_rows(shapes)
    c1 = 1.0 / (1.0 - ADAM_B1 ** ADAM_STEP)
    c2 = 1.0 / (1.0 - ADAM_B2 ** ADAM_STEP)

    def body(buf_ref, *refs):
        w_refs, m_refs, v_refs = refs[:n], refs[n:2 * n], refs[2 * n:3 * n]
        outs = refs[3 * n:]
        g_out, d_out, m_out, v_out, rest = outs[:n], outs[n:2 * n], outs[2 * n:3 * n], outs[3 * n:4 * n], outs[4 * n:]
        for a, (r, _) in enumerate(shapes):
            for i in range(r):
                gv = _row_of(buf_ref, rows, a, i)
                if a >= n:
                    rest[a - n][i:i + 1, :] = gv
                    continue
                row = slice(i, i + 1)
                mn = ADAM_B1 * m_refs[a][row, :] + (1.0 - ADAM_B1) * gv
                vn = ADAM_B2 * v_refs[a][row, :] + (1.0 - ADAM_B2) * (gv * gv)
                g_out[a][row, :] = gv
                d_out[a][row, :] = -ADAM_LR * ((mn * c1) / (jnp.sqrt(vn * c2) + ADAM_EPS) + ADAM_WD * w_refs[a][row, :])
                m_out[a][row, :] = mn
                v_out[a][row, :] = vn

    vm = pl.BlockSpec(memory_space=pltpu.VMEM)
    own = [jax.ShapeDtypeStruct(shp, F32) for shp in shapes[:n]]
    outs = pl.pallas_call(
        body, name="adamw_small", in_specs=[vm] * (1 + 3 * n), out_specs=[vm] * (4 * n + len(shapes) - n),
        out_shape=own * 4 + [jax.ShapeDtypeStruct(shp, F32) for shp in shapes[n:]],
    )(buf, *ws, *ms, *vs)
    return outs[:n], outs[n:2 * n], outs[2 * n:3 * n], outs[3 * n:4 * n], outs[4 * n:]


def _pad_lanes(v, width=LANE):
    return jnp.pad(v, ((0, 0), (0, width - v.shape[1])))


WEIGHTS = ("norm_mix_g", "norm_mem_g", "w_in", "hgrn_lb_logits", "hgrn_norm_g", "fox_f_bias", "fox_q_norm_g",
           "fox_k_norm_g", "mem_kv_w", "mem_q_norm_g", "mem_k_norm_g", "w_br_hgrn", "w_br_fox", "w_br_mem", "w_out",
           "norm_ffn_g", "ffn_w_up", "ffn_conv_w", "ffn_conv_b", "ffn_w_down")


def _local_step(x, mem, target, w, full, conv_w, late=None, hooks=None):
    b_, t_, d = x.shape
    n = b_ * t_
    hw, fw, mw = HG_HEADS * HG_D, FOX_HEADS * FOX_DH, MEM_HEADS * MEM_DH
    m_ = mem.shape[1]
    f = conv_w.shape[1]
    s1 = 4 * hw + 3 * fw
    fox_col, hg_col, mem_col, order = _z_layout(d, hw, fw, mw)
    gate_col = 0
    inverse = [order.index(j) for j in range(len(order))]

    w_main, w_ff = _split_w_in(full["w_in"], s1, FOX_HEADS)
    w_main = _reorder_blocks(w_main, order)
    w_ff = _pad_lanes(w_ff)
    f_bias = _pad_lanes(w["fox_f_bias"])
    cb = w["ffn_conv_b"]

    x2 = x.reshape(n, d)
    h = _rmsnorm_fwd(x2, w["norm_mix_g"], name="norm_mix_fwd")
    if late:
        pieces, kinds, finish = late["in_proj"]
        zm, gathered = _matmul(h, w_main, name="in_proj", rider=_gather_rider(pieces, kinds))
        full = {**full, **finish(gathered)}
    else:
        zm = _matmul(h, w_main, name="in_proj")
    w_brs = [full["w_br_hgrn"], full["w_br_fox"], full["w_br_mem"]]
    w_out, w_kv = full["w_out"], full["mem_kv_w"]
    zf = _matmul(h, w_ff, name="in_proj_forget")
    zm3, zf3 = zm.reshape(b_, t_, -1), zf.reshape(b_, t_, LANE)
    ya = _hgrn_fwd(zm3, w["hgrn_lb_logits"], w["hgrn_norm_g"], hw, hg_col)
    fc = _fox_prep(zf3, f_bias)
    fox_gq, fox_gk = jnp.tile(w["fox_q_norm_g"], (1, 2)), jnp.tile(w["fox_k_norm_g"], (1, 2))
    if late:
        pieces, kinds, finish = late["fox_fwd"]
        (yb, lse), gathered = _fox_fwd(zm3, fc, fox_gq, fox_gk, fw, fox_col, _gather_rider(pieces, kinds))
        full = {**full, **finish(gathered)}
    else:
        yb, lse = _fox_fwd(zm3, fc, fox_gq, fox_gk, fw, fox_col)[0]
    w_up, w_down = full["ffn_w_up"], full["ffn_w_down"]
    mem2 = mem.reshape(b_ * m_, d)
    hm = _rmsnorm_fwd(mem2, w["norm_mem_g"], name="norm_mem_fwd")
    mkv = _matmul(hm, w_kv, name="mem_kv_proj").reshape(b_, m_, 2 * mw)
    yc = _mem_fwd(zm3, mkv, w["mem_q_norm_g"], w["mem_k_norm_g"], mw, mem_col)
    ys = [ya.reshape(n, hw), yb.reshape(n, fw), yc.reshape(n, mw)]
    x1, merged, h2 = _merge_fwd(x2, ys, zm, w_brs, w_out, gate_col, w["norm_ffn_g"])
    u = _matmul(h2, w_up, name="ffn_up")
    u3 = u.reshape(b_, t_, 2 * f)
    yff = _glu_fwd(u3, conv_w, cb).reshape(n, f)
    dy, (loss_vec,), _ = _matmul_rows([yff], w_down, name="ffn_down_loss", tb=False, row_ins=[x1, target.reshape(n, d)],
                                      vec_ins=[], epilogue=_loss_epilogue, n_vec_out=1)

    grads = {}

    def ridden(name, call):
        if not hooks or name not in hooks:
            return call(None)[0]
        rider, then = hooks[name](grads)
        outs, extra = call(rider)
        then(extra)
        return outs

    dyff = _matmul(dy, w_down, tb=True, name="ffn_down_dx")
    grads["ffn_w_down"] = _matmul(yff, dy, ta=True, name="ffn_down_dw", tm=1408)
    du_a, du_v, grads["ffn_conv_w"], grads["ffn_conv_b"] = _glu_bwd(u3, dyff.reshape(b_, t_, f), conv_w, cb)
    du_a, du_v = du_a.reshape(n, f), du_v.reshape(n, f)
    dx1, (grads["norm_ffn_g"],), _ = _matmul_rows(
        [du_a, du_v], w_up, name="ffn_up_dx", tb=True, row_ins=[x1, dy], vec_ins=[w["norm_ffn_g"]],
        epilogue=_norm_bwd_epilogue(0), n_vec_out=1)
    grads["ffn_w_up"] = _matmul(h2, None, ta=True, name="ffn_up_dw", b_parts=[du_a, du_v], tn=f // 2, stack_out=True)

    dz, dpa, dpb, dpc, dya, dyb, dyc = _merge_bwd(dx1, ys, zm, w_brs, w_out, gate_col)
    dz = dz.reshape(b_, t_, -1)
    grads["w_out"] = _matmul(merged, dx1, ta=True, name="out_proj_dw")
    for nm, y_, dp_ in zip(("w_br_hgrn", "w_br_fox", "w_br_mem"), ys, (dpa, dpb, dpc)):
        grads[nm] = _matmul(y_, dp_, ta=True, name=nm + "_dw")

    dz, dmk, dmv, grads["mem_q_norm_g"], grads["mem_k_norm_g"] = _mem_bwd(
        zm3, mkv, dyc.reshape(b_, t_, mw), w["mem_q_norm_g"], w["mem_k_norm_g"], mw, mem_col, dz)
    dmkv = jnp.concatenate([dmk, dmv], axis=-1).reshape(b_ * m_, 2 * mw)
    grads["mem_kv_w"] = _matmul(hm, dmkv, ta=True, name="mem_kv_dw")
    dhm = _matmul(dmkv, w_kv, tb=True, name="mem_kv_dx")
    _, grads["norm_mem_g"] = _rmsnorm_bwd(mem2, [dhm], w["norm_mem_g"], None, name="norm_mem_bwd")

    dz, dfc, g_fq, g_fk = ridden("fox_bwd", lambda rider: _fox_bwd(
        zm3, yb, dyb.reshape(b_, t_, fw), lse, fc, fox_gq, fox_gk, fw, fox_col, dz, rider))
    grads["fox_q_norm_g"] = g_fq[:, :FOX_DH] + g_fq[:, FOX_DH:]
    grads["fox_k_norm_g"] = g_fk[:, :FOX_DH] + g_fk[:, FOX_DH:]
    dzf, g_fb = _fox_post(dfc, zf3, f_bias)
    grads["fox_f_bias"] = g_fb[:, :FOX_HEADS]

    dz, grads["hgrn_lb_logits"], grads["hgrn_norm_g"] = ridden("hgrn_bwd", lambda rider: _hgrn_bwd(
        zm3, dya.reshape(b_, t_, hw), w["hgrn_lb_logits"], w["hgrn_norm_g"], hw, hg_col, dz, rider))
    dzm = dz.reshape(n, -1)
    dzf2 = dzf.reshape(n, LANE)
    g_main = _matmul(h, dzm, ta=True, name="in_proj_dw")
    g_ff = _matmul(h, dzf2, ta=True, name="in_proj_forget_dw")
    grads["w_in"] = _join_w_in(_reorder_blocks(g_main, inverse), g_ff[:, :FOX_HEADS], s1, FOX_HEADS)

    dh_b = _matmul(dzf2, w_ff, tb=True, name="in_proj_forget_dx")

    def in_proj_dx(rider):
        out = _matmul(dzm, w_main, tb=True, name="in_proj_dx", rider=rider)
        return ([out[0]], out[1]) if rider else ([out], None)

    dh_a, = ridden("in_proj_dx", in_proj_dx)
    grad_x, grads["norm_mix_g"] = _rmsnorm_bwd(x2, [dh_a, dh_b], w["norm_mix_g"], dx1, name="norm_mix_bwd")
    return loss_vec, grad_x.reshape(b_, t_, d), grads


def kernel(x, mem, norm_mix_g, norm_mem_g, w_in, hgrn_lb_logits, hgrn_norm_g, fox_f_bias, fox_q_norm_g, fox_k_norm_g, mem_kv_w, mem_q_norm_g, mem_k_norm_g, w_br_hgrn, w_br_fox, w_br_mem, w_out, norm_ffn_g, ffn_w_up, ffn_conv_w, ffn_conv_b, ffn_w_down, loss_target, m_norm_mix_g, m_norm_mem_g, m_w_in, m_hgrn_lb_logits, m_hgrn_norm_g, m_fox_f_bias, m_fox_q_norm_g, m_fox_k_norm_g, m_mem_kv_w, m_mem_q_norm_g, m_mem_k_norm_g, m_w_br_hgrn, m_w_br_fox, m_w_br_mem, m_w_out, m_norm_ffn_g, m_ffn_w_up, m_ffn_conv_w, m_ffn_conv_b, m_ffn_w_down, v_norm_mix_g, v_norm_mem_g, v_w_in, v_hgrn_lb_logits, v_hgrn_norm_g, v_fox_f_bias, v_fox_q_norm_g, v_fox_k_norm_g, v_mem_kv_w, v_mem_q_norm_g, v_mem_k_norm_g, v_w_br_hgrn, v_w_br_fox, v_w_br_mem, v_w_out, v_norm_ffn_g, v_ffn_w_up, v_ffn_conv_w, v_ffn_conv_b, v_ffn_w_down):
    w = dict(zip(WEIGHTS, (norm_mix_g, norm_mem_g, w_in, hgrn_lb_logits, hgrn_norm_g, fox_f_bias, fox_q_norm_g,
                           fox_k_norm_g, mem_kv_w, mem_q_norm_g, mem_k_norm_g, w_br_hgrn, w_br_fox, w_br_mem, w_out,
                           norm_ffn_g, ffn_w_up, ffn_conv_w, ffn_conv_b, ffn_w_down)))
    m = dict(zip(WEIGHTS, (m_norm_mix_g, m_norm_mem_g, m_w_in, m_hgrn_lb_logits, m_hgrn_norm_g, m_fox_f_bias,
                           m_fox_q_norm_g, m_fox_k_norm_g, m_mem_kv_w, m_mem_q_norm_g, m_mem_k_norm_g, m_w_br_hgrn,
                           m_w_br_fox, m_w_br_mem, m_w_out, m_norm_ffn_g, m_ffn_w_up, m_ffn_conv_w, m_ffn_conv_b,
                           m_ffn_w_down)))
    v = dict(zip(WEIGHTS, (v_norm_mix_g, v_norm_mem_g, v_w_in, v_hgrn_lb_logits, v_hgrn_norm_g, v_fox_f_bias,
                           v_fox_q_norm_g, v_fox_k_norm_g, v_mem_kv_w, v_mem_q_norm_g, v_mem_k_norm_g, v_w_br_hgrn,
                           v_w_br_fox, v_w_br_mem, v_w_out, v_norm_ffn_g, v_ffn_w_up, v_ffn_conv_w, v_ffn_conv_b,
                           v_ffn_w_down)))
    c_idx = lax.axis_index("c")
    chip = 2 * lax.axis_index("x") + lax.axis_index("y")

    mine = {nm: w[nm][0].astype(MXU_DTYPE) for nm in BIG}

    def gathered_full(names, arrays):
        out = {nm: _put_shard(g, KIND[nm], chip, mine[nm]) for nm, g in zip(names, arrays)}
        return {nm: g.reshape(-1, g.shape[2]) if nm in ROW_SHARDED else g for nm, g in out.items()}

    full = gathered_full(FIRST, _gather_shards([mine[nm] for nm in FIRST], [KIND[nm] for nm in FIRST]))
    late = {host: ([mine[nm] for nm in names], [KIND[nm] for nm in names],
                   functools.partial(gathered_full, names)) for host, names in LATE.items()}
    cs = ffn_conv_w.shape[2]
    f = cs * N_CHIPS
    placed = lax.dynamic_update_slice(jnp.zeros((3, f), F32), ffn_conv_w[0] * (c_idx == 0).astype(F32), (0, chip * cs))
    conv_w = _unpack_small(_all_sum_small(_pack_small([placed]), "gather_conv_w"), [(3, f)])[0]

    c_arr = jnp.reshape(c_idx, (1,)).astype(jnp.int32)

    def stacked(nm, g):
        return g.reshape(N_CHIPS, -1, g.shape[1]) if nm in ROW_SHARDED else g

    def with_own(landed, partial, kinds):
        return [_put_shard(bq, STACK, chip, _take_shard(p, k, chip)) for bq, p, k in zip(landed, partial, kinds)]

    kinds_rest, kinds_last = [KIND[nm] for nm in REST], [KIND[nm] for nm in LAST]
    state = {}

    def swap_rest(grads):
        gs = [stacked(nm, grads[nm]) for nm in REST]

        def then(from_sibling):
            state["partial_rest"] = [_add_half(g, a, k, c_arr, "add_half_" + nm)
                                     for g, a, k, nm in zip(gs, from_sibling, kinds_rest, REST)]

        return _swap_rider(gs, kinds_rest), then

    def exchange_rest(grads):
        def then(landed):
            state["landed_rest"] = with_own(landed, state["partial_rest"], kinds_rest)

        return _exchange_rider(state["partial_rest"], kinds_rest), then

    def exchange_last(grads):
        gs = [stacked(nm, grads[nm]) for nm in LAST]
        from_sibling = _pair_swap_halves(gs, kinds_last, "pair_swap_halves_last")
        partial = [_add_half(g, a, k, c_arr, "add_half_" + nm) for g, a, k, nm in zip(gs, from_sibling, kinds_last, LAST)]

        def then(landed):
            state["landed_last"] = with_own(landed, partial, kinds_last)

        return _exchange_rider(partial, kinds_last), then

    hooks = {"fox_bwd": swap_rest, "hgrn_bwd": exchange_rest, "in_proj_dx": exchange_last}

    loss_vec, grad_x, grads = _local_step(x, mem, loss_target, w, full, conv_w, late, hooks)

    landed = dict(zip(LAST + REST, state["landed_last"] + state["landed_rest"]))
    reduced_half = [_sum_chips(landed[nm], "sum_chips_" + nm) for nm in BIG]
    joined = [lax.dynamic_update_slice(o, q[None], (c_idx, 0, 0)).reshape(2 * q.shape[0], q.shape[1])
              for o, q in zip(_pair_join_halves(reduced_half), reduced_half)]
    gshards = dict(zip(BIG, joined))

    small_shapes = [w[nm].shape for nm in SMALL] + [grads["ffn_conv_w"].shape, loss_vec.shape]
    summed = _all_sum_small(_pack_small([grads[nm] for nm in SMALL] + [grads["ffn_conv_w"], loss_vec]),
                            "all_sum_small_grads")
    g_small, d_small, m_small, v_small, (g_conv_w, loss_row) = _adamw_small(
        summed, small_shapes, [w[nm] for nm in SMALL], [m[nm] for nm in SMALL], [v[nm] for nm in SMALL])
    loss = jnp.sum(loss_row)
    g_out = {nm: gshards[nm][None] for nm in BIG}
    g_out["ffn_conv_w"] = lax.dynamic_slice(g_conv_w, (0, chip * cs), (3, cs))[None]
    delta, new_m, new_v = dict(zip(SMALL, d_small)), dict(zip(SMALL, m_small)), dict(zip(SMALL, v_small))
    g_out.update(zip(SMALL, g_small))
    for nm in BIG + ("ffn_conv_w",):
        operands = (w[nm], g_out[nm], m[nm], v[nm])
        if nm in TRANSPOSED:
            operands = [jnp.swapaxes(a, 1, 2) for a in operands]
        outs = _adamw(*operands, name="adamw_" + nm)
        delta[nm], new_m[nm], new_v[nm] = [jnp.swapaxes(o, 1, 2) for o in outs] if nm in TRANSPOSED else outs

    return (loss, grad_x, *[g_out[nm] for nm in WEIGHTS], *[delta[nm] for nm in WEIGHTS],
            *[new_m[nm] for nm in WEIGHTS], *[new_v[nm] for nm in WEIGHTS])
```

```python
import functools
import math

import jax
import jax.numpy as jnp
from jax import lax
from jax.experimental import pallas as pl
from jax.experimental.pallas import tpu as pltpu

F32 = jnp.float32
BF16 = jnp.bfloat16
MXU_DTYPE = jnp.bfloat16
EXCHANGE_DTYPE = jnp.bfloat16

EPS = 1e-6
HG_HEADS, HG_D = 4, 128
FOX_HEADS, FOX_DH = 8, 64
MEM_HEADS, MEM_DH = 4, 128
MEM_TILE = 2048
HG_CHUNK = 64
FOX_BLOCK = 256
LANE = 128
FFN_GROUP = 256
FLAT_W = 1024
VMEM_LIMIT = 56 * 2 ** 20
NEG = -1e30
N_CHIPS = 4

ADAM_LR, ADAM_B1, ADAM_B2, ADAM_EPS, ADAM_WD, ADAM_STEP = 0.001, 0.9, 0.999, 1e-08, 0.01, 10

MESH = pl.DeviceIdType.MESH
ANY = pl.BlockSpec(memory_space=pl.ANY)


def _mx(x):
    return x.astype(MXU_DTYPE)


def _dot(a, b, ca, cb):
    return lax.dot_general(_mx(a), _mx(b), (((ca,), (cb,)), ((), ())), preferred_element_type=F32)


def _nn(a, b):
    return _dot(a, b, 1, 0)


def _nt(a, b):
    return _dot(a, b, 1, 1)


def _tn(a, b):
    return _dot(a, b, 0, 0)


def _tri_dot(tri_bf, x):
    hi = x.astype(BF16)
    r = x - hi.astype(F32)
    mid = r.astype(BF16)
    lo = (r - mid.astype(F32)).astype(BF16)

    def d(v):
        return lax.dot_general(tri_bf, v, (((1,), (0,)), ((), ())), preferred_element_type=F32)

    return d(hi) + d(mid) + d(lo)


def _sig(x):
    return jax.nn.sigmoid(x)


def _tile(dim, pref, unit=LANE):
    if dim <= pref:
        return dim
    t = pref - pref % unit
    while t >= unit:
        if dim % t == 0:
            return t
        t -= unit
    return dim


def _params(n_grid):
    return pltpu.CompilerParams(dimension_semantics=("arbitrary",) * n_grid, vmem_limit_bytes=VMEM_LIMIT)


def _acc(ref, val, first):
    @pl.when(first)
    def _():
        ref[...] = val

    @pl.when(jnp.logical_not(first))
    def _():
        ref[...] += val


class _Rider:
    def __init__(self, inputs, out_shapes, scratch, start, finish):
        self.inputs, self.out_shapes, self.scratch, self.start, self.finish = inputs, out_shapes, scratch, start, finish


def _ride(body, rider, n_in, n_out, grid):
    if rider is None:
        return body
    ri, ro, rs = len(rider.inputs), len(rider.out_shapes), len(rider.scratch)

    def wrapped(*refs):
        a, b, c = n_in + ri, n_in + ri + n_out, n_in + ri + n_out + ro
        base = refs[:n_in] + refs[a:b] + refs[c:len(refs) - rs]
        r_in, r_out, r_scr = refs[n_in:a], refs[b:c], refs[len(refs) - rs:]
        step = pl.program_id(0)
        for ax in range(1, len(grid)):
            step = step * grid[ax] + pl.program_id(ax)

        @pl.when(step == 0)
        def _():
            rider.start(r_in, r_out, r_scr)

        body(*base)

        @pl.when(step == math.prod(grid) - 1)
        def _():
            rider.finish(r_in, r_out, r_scr)

    return wrapped


def _ride_call(body, rider, *, name, grid, in_specs, out_specs, out_shape, scratch, args, aliases=None):
    n_in, n_out = len(in_specs), len(out_specs)
    aliases = aliases or {}
    if rider is None:
        outs = pl.pallas_call(body, name=name, grid=grid, in_specs=in_specs, out_specs=out_specs, out_shape=out_shape,
                              scratch_shapes=scratch, input_output_aliases=aliases,
                              compiler_params=_params(len(grid)))(*args)
        return list(outs), None
    outs = pl.pallas_call(
        _ride(body, rider, n_in, n_out, grid), name=name, grid=grid,
        in_specs=list(in_specs) + [ANY] * len(rider.inputs), out_specs=list(out_specs) + [ANY] * len(rider.out_shapes),
        out_shape=list(out_shape) + list(rider.out_shapes), scratch_shapes=list(scratch) + list(rider.scratch),
        input_output_aliases=aliases, compiler_params=_params(len(grid)),
    )(*args, *rider.inputs)
    return list(outs[:n_out]), list(outs[n_out:])


def _matmul(a, b, *, name, ta=False, tb=False, tm=2048, tn=2048, tk=None, rider=None, b_parts=None, stack_out=False):
    m, k = (a.shape[1], a.shape[0]) if ta else a.shape
    tk = tk or (1024 if ta else 2048)
    stacked_b = b is not None and b.ndim == 3
    if b_parts:
        n, tn = 2 * b_parts[0].shape[1], _tile(b_parts[0].shape[1], tn)
    elif stacked_b:
        n, tn = b.shape[0] * b.shape[2], b.shape[2]
    else:
        n = b.shape[0] if tb else b.shape[1]
        tn = _tile(n, tn)
    tm, tk = _tile(m, tm), _tile(k, tk)
    nk, nj = k // tk, n // tn

    def body(a_ref, *refs):
        o_ref = refs[-1]
        if b_parts:
            bv = jnp.where(pl.program_id(1) < nj // 2, refs[0][...], refs[1][...])
        else:
            bv = refs[0][...]
        p = _dot(a_ref[...], bv, 0 if ta else 1, 1 if tb else 0)
        if nk == 1:
            o_ref[...] = p
        else:
            _acc(o_ref, p, pl.program_id(2) == 0)

    a_spec = pl.BlockSpec((tk, tm), lambda i, j, kk: (kk, i)) if ta else pl.BlockSpec((tm, tk), lambda i, j, kk: (i, kk))
    if b_parts:
        half = nj // 2
        b_specs = [pl.BlockSpec((tk, tn), lambda i, j, kk: (kk, jnp.minimum(j, half - 1))),
                   pl.BlockSpec((tk, tn), lambda i, j, kk: (kk, jnp.maximum(j - half, 0)))]
        b_args = list(b_parts)
    elif stacked_b:
        b_specs, b_args = [pl.BlockSpec((None, tk, tn), lambda i, j, kk: (j, kk, 0))], [b]
    else:
        b_specs = [pl.BlockSpec((tn, tk), lambda i, j, kk: (j, kk)) if tb else pl.BlockSpec((tk, tn), lambda i, j, kk: (kk, j))]
        b_args = [b]
    if stack_out:
        o_spec, o_sds = pl.BlockSpec((None, tm, tn), lambda i, j, kk: (j, i, 0)), jax.ShapeDtypeStruct((nj, m, tn), F32)
    else:
        o_spec, o_sds = pl.BlockSpec((tm, tn), lambda i, j, kk: (i, j)), jax.ShapeDtypeStruct((m, n), F32)
    outs, extra = _ride_call(body, rider, name=name, grid=(m // tm, nj, nk), in_specs=[a_spec] + b_specs,
                             out_specs=[o_spec], out_shape=[o_sds], scratch=[], args=(a, *b_args))
    return (outs[0], extra) if rider else outs[0]


def _matmul_rows(a_parts, b, *, name, tb, row_ins, vec_ins, epilogue, n_vec_out, tm=512, tk=2048, rider=None):
    m, kp = a_parts[0].shape
    stacked_b = b.ndim == 3
    n = b.shape[1] if stacked_b else (b.shape[0] if tb else b.shape[1])
    tm, tk = _tile(m, tm, 8), (b.shape[2] if stacked_b else _tile(kp, tk))
    nk = kp // tk
    n_a, n_row, n_vec = len(a_parts), len(row_ins), len(vec_ins)

    def body(*refs):
        a_refs, b_refs = refs[:n_a], refs[n_a:2 * n_a]
        rows = refs[2 * n_a:2 * n_a + n_row]
        vecs = refs[2 * n_a + n_row:2 * n_a + n_row + n_vec]
        o_ref = refs[2 * n_a + n_row + n_vec]
        v_refs = refs[2 * n_a + n_row + n_vec + 1:-1]
        acc_ref = refs[-1]
        i, kk = pl.program_id(0), pl.program_id(1)
        p = _dot(a_refs[0][...], b_refs[0][...], 1, 1 if tb else 0)
        for a_ref, b_ref in zip(a_refs[1:], b_refs[1:]):
            p = p + _dot(a_ref[...], b_ref[...], 1, 1 if tb else 0)
        _acc(acc_ref, p, kk == 0)

        @pl.when(kk == nk - 1)
        def _():
            out, vouts = epilogue(acc_ref[...], *[r[...] for r in rows], *[v[...] for v in vecs])
            o_ref[...] = out
            for v_ref, v in zip(v_refs, vouts):
                _acc(v_ref, v, i == 0)

    a_spec = pl.BlockSpec((tm, tk), lambda i, kk: (i, kk))
    if stacked_b:
        b_specs = [pl.BlockSpec((None, n, tk), functools.partial(lambda i, kk, q: (q * nk + kk, 0, 0), q=q))
                   for q in range(n_a)]
    else:
        b_specs = [pl.BlockSpec((n, tk), functools.partial(lambda i, kk, q: (0, q * nk + kk), q=q)) if tb else
                   pl.BlockSpec((tk, n), functools.partial(lambda i, kk, q: (q * nk + kk, 0), q=q)) for q in range(n_a)]
    row = pl.BlockSpec((tm, n), lambda i, kk: (i, 0))
    vec = pl.BlockSpec((1, n), lambda i, kk: (0, 0))
    outs, extra = _ride_call(
        body, rider, name=name, grid=(m // tm, nk),
        in_specs=[a_spec] * n_a + b_specs + [row] * n_row + [vec] * n_vec,
        out_specs=[row] + [vec] * n_vec_out,
        out_shape=[jax.ShapeDtypeStruct((m, n), F32)] + [jax.ShapeDtypeStruct((1, n), F32)] * n_vec_out,
        scratch=[pltpu.VMEM((tm, n), F32)], args=(*a_parts, *([b] * n_a), *row_ins, *vec_ins))
    return outs[0], outs[1:], extra


def _norm_bwd_epilogue(n_dh):
    def epilogue(dh, x, res, *rest):
        for extra in rest[:n_dh]:
            dh = dh + extra
        g = rest[n_dh]
        r = lax.rsqrt(jnp.mean(x * x, axis=-1, keepdims=True) + EPS)
        dhg = dh * g
        dx = res + r * dhg - x * (r * r * r) * jnp.mean(dhg * x, axis=-1, keepdims=True)
        return dx, [jnp.sum(dh * x * r, axis=0, keepdims=True)]

    return epilogue


def _loss_epilogue(y, x1, target):
    d = y.shape[1]
    err = x1 + y - target
    return err * (1.0 / d), [jnp.sum(err * err, axis=0, keepdims=True) * (0.5 / d)]


def _rmsnorm_fwd(x, g, *, name, tm=1024):
    n, d = x.shape
    tm = _tile(n, tm, 8)

    def body(x_ref, g_ref, o_ref):
        xv = x_ref[...]
        r = lax.rsqrt(jnp.mean(xv * xv, axis=-1, keepdims=True) + EPS)
        o_ref[...] = (xv * r * g_ref[...]).astype(o_ref.dtype)

    return pl.pallas_call(
        body, name=name, grid=(n // tm,),
        in_specs=[pl.BlockSpec((tm, d), lambda i: (i, 0)), pl.BlockSpec((1, d), lambda i: (0, 0))],
        out_specs=pl.BlockSpec((tm, d), lambda i: (i, 0)),
        out_shape=jax.ShapeDtypeStruct((n, d), MXU_DTYPE),
        compiler_params=_params(1),
    )(x, g)


def _rmsnorm_bwd(x, dhs, g, res, *, name, tm=1024):
    n, d = x.shape
    tm = _tile(n, tm, 8)
    n_dh = len(dhs)
    has_res = res is not None

    def body(*refs):
        x_ref, dh_refs, g_ref = refs[0], refs[1:1 + n_dh], refs[1 + n_dh]
        res_ref = refs[2 + n_dh] if has_res else None
        dx_ref, dg_ref = refs[-2], refs[-1]
        xv = x_ref[...]
        dh = dh_refs[0][...].astype(F32)
        for r_ in dh_refs[1:]:
            dh = dh + r_[...].astype(F32)
        r = lax.rsqrt(jnp.mean(xv * xv, axis=-1, keepdims=True) + EPS)
        dhg = dh * g_ref[...]
        dx = r * dhg - xv * (r * r * r) * jnp.mean(dhg * xv, axis=-1, keepdims=True)
        if has_res:
            dx = dx + res_ref[...]
        dx_ref[...] = dx
        _acc(dg_ref, jnp.sum(dh * xv * r, axis=0, keepdims=True), pl.program_id(0) == 0)

    row = pl.BlockSpec((tm, d), lambda i: (i, 0))
    vec = pl.BlockSpec((1, d), lambda i: (0, 0))
    ins = [x] + list(dhs) + [g] + ([res] if has_res else [])
    return pl.pallas_call(
        body, name=name, grid=(n // tm,),
        in_specs=[row] * (1 + n_dh) + [vec] + ([row] if has_res else []),
        out_specs=[row, vec],
        out_shape=[jax.ShapeDtypeStruct((n, d), F32), jax.ShapeDtypeStruct((1, d), F32)],
        compiler_params=_params(1),
    )(*ins)


def _adamw(w, g, m, v, *, name, tr=256):
    _, r, c = w.shape
    c1 = 1.0 / (1.0 - ADAM_B1 ** ADAM_STEP)
    c2 = 1.0 / (1.0 - ADAM_B2 ** ADAM_STEP)

    def body(w_ref, g_ref, m_ref, v_ref, d_ref, mo_ref, vo_ref):
        gv = g_ref[...]
        mn = ADAM_B1 * m_ref[...] + (1.0 - ADAM_B1) * gv
        vn = ADAM_B2 * v_ref[...] + (1.0 - ADAM_B2) * (gv * gv)
        d_ref[...] = -ADAM_LR * ((mn * c1) / (jnp.sqrt(vn * c2) + ADAM_EPS) + ADAM_WD * w_ref[...])
        mo_ref[...] = mn
        vo_ref[...] = vn

    if r % 8 == 0 or r < 8:
        tr = _tile(r, tr, 8)
        grid, blk = (r // tr,), pl.BlockSpec((1, tr, c), lambda i: (0, i, 0))
    else:
        tc = _tile(c, tr)
        grid, blk = (c // tc,), pl.BlockSpec((1, r, tc), lambda i: (0, 0, i))
    sds = jax.ShapeDtypeStruct((1, r, c), F32)
    return pl.pallas_call(
        body, name=name, grid=grid, in_specs=[blk] * 4, out_specs=[blk] * 3, out_shape=[sds] * 3,
        compiler_params=_params(1),
    )(w, g, m, v)


def _bdot(a, b, ca, cb):
    return lax.dot_general(_mx(a), _mx(b), (((ca,), (cb,)), ((0,), (0,))), preferred_element_type=F32)


def _split2(x):
    hi = x.astype(BF16)
    return hi, (x - hi.astype(F32)).astype(BF16)


def _bdotp(a, b, ca, cb):
    def d(u, v):
        return lax.dot_general(u, v, (((ca,), (cb,)), ((0,), (0,))), preferred_element_type=F32)

    return d(a[0], b[0]) + d(a[0], b[1]) + d(a[1], b[0])


def _tri_dot_b(tri_bf, x):
    hi = x.astype(BF16)
    r = x - hi.astype(F32)
    mid = r.astype(BF16)
    lo = (r - mid.astype(F32)).astype(BF16)

    def d(v):
        return lax.dot_general(tri_bf, v, (((2,), (1,)), ((0,), (0,))), preferred_element_type=F32)

    return d(hi) + d(mid) + d(lo)


def _hgrn_forward(hq, hf, hi, lbv, tril, tril_bf):
    nc, c, _ = hq.shape
    sf = _sig(hf)
    f = lbv + (1.0 - lbv) * sf
    k = 1.0 - f
    gcum = _tri_dot_b(tril_bf, jnp.log(f))
    mid = gcum[:, c // 2 - 1:c // 2, :]
    glast = gcum[:, c - 1:c, :]
    sq = _sig(hq)
    q = hq * sq
    e_q = jnp.exp(gcum - mid)
    e_k = jnp.exp(mid - gcum)
    qe, ke = q * e_q, k * e_k
    a = jnp.where(tril, _bdot(qe, ke, 2, 2), 0.0)
    e_g = jnp.exp(gcum)
    qg = q * e_g
    e_s = jnp.exp(glast - gcum)
    kg = k * e_s
    e_l = jnp.exp(glast)
    upd = _bdot(hi, kg, 1, 1)
    st = jnp.zeros((HG_D, HG_D), F32)
    states = []
    for n in range(nc):
        states.append(st)
        st = st * e_l[n] + upd[n]
    st_all = jnp.stack(states)
    o = _bdot(a, hi, 2, 1) + _bdot(qg, st_all, 2, 2)
    return dict(sf=sf, f=f, k=k, sq=sq, q=q, e_q=e_q, e_k=e_k, qe=qe, ke=ke, a=a, e_g=e_g, qg=qg, o=o,
                e_s=e_s, kg=kg, e_l=e_l, st_all=st_all)


def _hgrn_specs(t_, col0):
    def col(off):
        return pl.BlockSpec((1, t_, LANE), lambda h, b: (b, 0, col0 + 4 * h + off))

    vec = pl.BlockSpec((2, LANE), lambda h, b: (0, h))
    one = pl.BlockSpec((1, LANE), lambda h, b: (0, 0))
    blk = pl.BlockSpec((1, t_, LANE), lambda h, b: (b, 0, h))
    return col, vec, one, blk


def _chunk_masks(nc, c):
    row = lax.broadcasted_iota(jnp.int32, (nc, c, c), 1)
    cl = lax.broadcasted_iota(jnp.int32, (nc, c, c), 2)
    return row >= cl, (row >= cl).astype(BF16), (row <= cl).astype(BF16)


def _hgrn_fwd(zm, lb, gn, hw, col0):
    b_, t_, _ = zm.shape
    c = min(HG_CHUNK, t_)
    nc = t_ // c
    col, vec, one, blk = _hgrn_specs(t_, col0)

    def body(q_ref, f_ref, i_ref, g_ref, lb_ref, gn_ref, y_ref):
        lbv, gnv = _sig(lb_ref[0:1, :] - lb_ref[1:2, :]), gn_ref[...]
        tril, tril_bf, _ = _chunk_masks(nc, c)
        chunks = lambda ref: ref[0].reshape(nc, c, LANE)
        o = _hgrn_forward(chunks(q_ref), chunks(f_ref), chunks(i_ref), lbv, tril, tril_bf)["o"]
        r = lax.rsqrt(jnp.mean(o * o, axis=-1, keepdims=True) + EPS)
        hg = chunks(g_ref)
        y_ref[0] = (o * r * gnv * (hg * _sig(hg))).reshape(t_, LANE)

    return pl.pallas_call(
        body, name="hgrn_fwd", grid=(HG_HEADS, b_),
        in_specs=[col(0), col(1), col(2), col(3), vec, one], out_specs=blk,
        out_shape=jax.ShapeDtypeStruct((b_, t_, hw), F32),
        compiler_params=_params(2),
    )(zm, zm, zm, zm, lb, gn)


def _hgrn_bwd(zm, dy, lb, gn, hw, col0, dz, rider=None):
    b_, t_, _ = zm.shape
    c = min(HG_CHUNK, t_)
    nc = t_ // c
    col, vec, one, blk = _hgrn_specs(t_, col0)

    def body(q_ref, f_ref, i_ref, g_ref, dy_ref, lb_ref, gn_ref, _, dz_ref, dlb_ref, dgn_ref):
        h, b = pl.program_id(0), pl.program_id(1)
        lbv, gnv = _sig(lb_ref[0:1, :] - lb_ref[1:2, :]), gn_ref[...]
        tril, tril_bf, triu_bf = _chunk_masks(nc, c)
        last_row = lax.broadcasted_iota(jnp.int32, (nc, c, LANE), 1) == c - 1
        chunks = lambda ref: ref[0].reshape(nc, c, LANE)
        flat = lambda x: x.reshape(t_, LANE)
        hq, hi, hg = chunks(q_ref), chunks(i_ref), chunks(g_ref)
        p = _hgrn_forward(hq, chunks(f_ref), hi, lbv, tril, tril_bf)
        o, q, k, st_all, e_l = p["o"], p["q"], p["k"], p["st_all"], p["e_l"]
        dyv = chunks(dy_ref)
        sg = _sig(hg)
        r = lax.rsqrt(jnp.mean(o * o, axis=-1, keepdims=True) + EPS)
        dn = dyv * (hg * sg)
        dz_ref[0, :, 3 * LANE:] = flat(dyv * (o * r * gnv) * (sg * (1.0 + hg * (1.0 - sg)))).astype(dz_ref.dtype)
        dgn = jnp.sum(flat(dn * o * r), axis=0, keepdims=True)
        dng = dn * gnv
        do = r * dng - o * (r * r * r) * jnp.mean(dng * o, axis=-1, keepdims=True)
        do2, hi2, qg2, ke2, qe2, st2 = (_split2(t) for t in (do, hi, p["qg"], p["ke"], p["qe"], st_all))
        back = _bdotp(do2, qg2, 1, 1)
        dst = jnp.zeros((HG_D, HG_D), F32)
        dsts = [None] * nc
        for n in range(nc - 1, -1, -1):
            dsts[n] = dst
            dst = dst * e_l[n] + back[n]
        dst_all = jnp.stack(dsts)
        da = jnp.where(tril, _bdotp(do2, hi2, 2, 2), 0.0)
        da2 = _split2(da)
        dq = _bdotp(da2, ke2, 2, 1) * p["e_q"] + _bdotp(do2, st2, 2, 1) * p["e_g"]
        dk_state = _bdotp(hi2, _split2(dst_all), 2, 1) * p["e_s"]
        dk = _bdotp(da2, qe2, 1, 1) * p["e_k"] + dk_state
        dz_ref[0, :, 2 * LANE:3 * LANE] = flat(_bdot(p["a"], do, 1, 1) + _bdot(p["kg"], dst_all, 2, 2)).astype(dz_ref.dtype)
        extra = (jnp.sum(k * dk_state, axis=1, keepdims=True) + e_l * jnp.sum(st_all * dst_all, axis=1, keepdims=True))
        dgc = q * dq - k * dk + jnp.where(last_row, extra, 0.0)
        dfv = _tri_dot_b(triu_bf, dgc) / p["f"] - dk
        sf, sq = p["sf"], p["sq"]
        dz_ref[0, :, LANE:2 * LANE] = flat(dfv * (1.0 - lbv) * sf * (1.0 - sf)).astype(dz_ref.dtype)
        dlb = jnp.sum(flat(dfv * (1.0 - sf)), axis=0, keepdims=True)
        dz_ref[0, :, :LANE] = flat(dq * (sq * (1.0 + hq * (1.0 - sq)))).astype(dz_ref.dtype)
        dl0 = dlb * lbv * (1.0 - lbv)
        _acc(dlb_ref, jnp.concatenate([dl0, -dl0], axis=0), b == 0)
        _acc(dgn_ref, dgn, jnp.logical_and(b == 0, h == 0))

    return _ride_call(
        body, rider, name="hgrn_bwd", grid=(HG_HEADS, b_),
        in_specs=[col(0), col(1), col(2), col(3), blk, vec, one, ANY],
        out_specs=[pl.BlockSpec((1, t_, 4 * LANE), lambda h, b: (b, 0, col0 // 4 + h)), vec, one],
        out_shape=[jax.ShapeDtypeStruct(dz.shape, dz.dtype), jax.ShapeDtypeStruct((2, hw), F32),
                   jax.ShapeDtypeStruct((1, LANE), F32)],
        scratch=[], args=(zm, zm, zm, zm, dy, lb, gn, dz), aliases={7: 0})


def _fox_logf(x):
    return jnp.minimum(x, 0.0) - jnp.log(1.0 + jnp.exp(-jnp.abs(x)))


def _fox_prep(zf, bias):
    b_, t_, _ = zf.shape
    tb = min(FOX_BLOCK, t_)
    nb = t_ // tb

    def body(z_ref, b_ref, fc_ref):
        tril_bf = (lax.broadcasted_iota(jnp.int32, (tb, tb), 0) >= lax.broadcasted_iota(jnp.int32, (tb, tb), 1)).astype(BF16)
        bv = b_ref[...]

        def blk(i, carry):
            rows = pl.ds(pl.multiple_of(i * tb, tb), tb)
            fc = _tri_dot(tril_bf, _fox_logf(z_ref[0, rows, :] + bv)) + carry
            fc_ref[0, rows, :] = fc
            return fc[tb - 1:tb, :]

        lax.fori_loop(0, nb, blk, jnp.zeros((1, LANE), F32))

    blk_spec = pl.BlockSpec((1, t_, LANE), lambda b: (b, 0, 0))
    return pl.pallas_call(
        body, name="fox_prep", grid=(b_,),
        in_specs=[blk_spec, pl.BlockSpec((1, LANE), lambda b: (0, 0))], out_specs=blk_spec,
        out_shape=jax.ShapeDtypeStruct((b_, t_, LANE), F32), compiler_params=_params(1),
    )(zf, bias)


def _fox_post(dfc, zf, bias):
    b_, t_, _ = zf.shape
    npair = dfc.shape[1]
    tb = min(FOX_BLOCK, t_)
    nb = t_ // tb

    def body(d_ref, z_ref, b_ref, dz_ref, db_ref):
        triu_bf = (lax.broadcasted_iota(jnp.int32, (tb, tb), 0) <= lax.broadcasted_iota(jnp.int32, (tb, tb), 1)).astype(BF16)
        valid = lax.broadcasted_iota(jnp.int32, (tb, LANE), 1) < FOX_HEADS
        bv = b_ref[...]

        def blk(m, carry):
            tail, db = carry
            rows = pl.ds(pl.multiple_of((nb - 1 - m) * tb, tb), tb)
            dfc_rows = d_ref[0, 0, rows, :]
            for p in range(1, npair):
                dfc_rows = dfc_rows + pltpu.roll(d_ref[0, p, rows, :], 2 * p, 1)
            dlf = _tri_dot(triu_bf, dfc_rows) + tail
            dx = jnp.where(valid, dlf * _sig(-(z_ref[0, rows, :] + bv)), 0.0)
            dz_ref[0, rows, :] = dx.astype(dz_ref.dtype)
            return dlf[0:1, :], db + jnp.sum(dx, axis=0, keepdims=True)

        z1 = jnp.zeros((1, LANE), F32)
        _, db = lax.fori_loop(0, nb, blk, (z1, z1))
        _acc(db_ref, db, pl.program_id(0) == 0)

    blk_spec = pl.BlockSpec((1, t_, LANE), lambda b: (b, 0, 0))
    vec = pl.BlockSpec((1, LANE), lambda b: (0, 0))
    return pl.pallas_call(
        body, name="fox_post", grid=(b_,),
        in_specs=[pl.BlockSpec((1, npair, t_, LANE), lambda b: (b, 0, 0, 0)), blk_spec, vec], out_specs=[blk_spec, vec],
        out_shape=[jax.ShapeDtypeStruct((b_, t_, LANE), MXU_DTYPE), jax.ShapeDtypeStruct((1, LANE), F32)],
        compiler_params=_params(1),
    )(dfc, zf, bias)


FOX_TILE = 512
FOX_TILE_FWD = 512
FOX_BAND = 512
AUG = 64


def _head_mean_matrix():
    r = lax.broadcasted_iota(jnp.int32, (LANE, LANE), 0) // FOX_DH
    c = lax.broadcasted_iota(jnp.int32, (LANE, LANE), 1) // FOX_DH
    return (r == c).astype(BF16)


def _dot_right_exact(x, m_bf):
    hi = x.astype(BF16)
    lo = (x - hi.astype(F32)).astype(BF16)

    def d(v):
        return lax.dot_general(v, m_bf, (((1,), (0,)), ((), ())), preferred_element_type=F32)

    return d(hi) + d(lo)


def _pair_norm(x, g2, bd):
    r = lax.rsqrt(_dot_right_exact(x * x, bd) * (1.0 / FOX_DH) + EPS)
    return x * r * g2, r


def _pair_norm_bwd(x, r, dy, g2, bd):
    dyg = dy * g2
    dx = r * dyg - x * (r * r * r) * (_dot_right_exact(dyg * x, bd) * (1.0 / FOX_DH))
    return dx, jnp.sum(dy * x * r, axis=0, keepdims=True)


def _head_lanes(xn, hh):
    return xn if hh == 0 else pltpu.roll(xn, FOX_DH, 1)


def _split3(x):
    hi = x.astype(BF16).astype(F32)
    mid = (x - hi).astype(BF16).astype(F32)
    return hi, mid, x - hi - mid


def _fox_operands(q_ref, k_ref, v_ref, fc_ref, gq2, gk2, p, qa, ka, va):
    t_ = q_ref.shape[1]
    bd = _head_mean_matrix()
    lane = lax.broadcasted_iota(jnp.int32, (t_, LANE), 1)
    qx, kx = q_ref[0], k_ref[0]
    qn, rq = _pair_norm(qx, gq2, bd)
    kn, rk = _pair_norm(kx, gk2, bd)
    vv = v_ref[0]
    q_aug = jnp.where(jnp.logical_and(lane >= AUG, lane < AUG + 3), 1.0, 0.0)
    for hh in range(2):
        fcol = jnp.sum(jnp.where(lane == 2 * p + hh, fc_ref[0], 0.0), axis=-1, keepdims=True)
        hi, mid, lo = _split3(-fcol)
        k_aug = jnp.where(lane == AUG, hi, jnp.where(lane == AUG + 1, mid, jnp.where(lane == AUG + 2, lo,
                          jnp.where(lane == AUG + 3, 1.0, 0.0))))
        head = lane < FOX_DH
        qa[hh] = jnp.where(head, _head_lanes(qn, hh), q_aug).astype(MXU_DTYPE)
        ka[hh] = jnp.where(head, _head_lanes(kn, hh), k_aug).astype(MXU_DTYPE)
        va[hh] = jnp.where(head, _head_lanes(vv, hh), 0.0).astype(MXU_DTYPE)
    return bd, lane, qx, kx, rq, rk


def _fox_specs(t_, fw, col0):
    npair = fw // LANE

    def col(off):
        return pl.BlockSpec((1, t_, LANE), lambda b, p: (b, 0, col0 + 3 * p + off))

    pair = pl.BlockSpec((1, t_, LANE), lambda b, p: (b, 0, p))
    full = pl.BlockSpec((1, t_, LANE), lambda b, p: (b, 0, 0))
    gvec = pl.BlockSpec((1, LANE), lambda b, p: (0, 0))
    lse = pl.BlockSpec((1, 1, t_, LANE), lambda b, p: (b, p, 0, 0))
    return col, pair, full, gvec, lse


def _fox_fwd(zm, fc, gq2, gk2, fw, col0, rider=None):
    b_, t_, _ = zm.shape
    npair = fw // LANE
    tq = min(FOX_TILE_FWD, t_)
    bw = min(FOX_BAND, t_)
    nband, tpb = t_ // bw, bw // tq
    scale = FOX_DH ** -0.5
    col, pair, full, gvec, lse_spec = _fox_specs(t_, fw, col0)

    def body(q_ref, k_ref, v_ref, fc_ref, gq_ref, gk_ref, o_ref, lse_ref, qa, ka, va):
        p = pl.program_id(1)
        _fox_operands(q_ref, k_ref, v_ref, fc_ref, gq_ref[...] * scale, gk_ref[...], p, qa, ka, va)
        ahead = lax.broadcasted_iota(jnp.int32, (tq, bw), 1) - lax.broadcasted_iota(jnp.int32, (tq, bw), 0)
        lane = lax.broadcasted_iota(jnp.int32, (tq, LANE), 1)

        for band in range(nband):
            c0 = band * bw

            def qtile(ii, _, c0=c0):
                r0 = pl.multiple_of(c0 + ii * tq, tq)
                rows = pl.ds(r0, tq)
                keep = ahead <= r0 - c0
                res = []
                for hh in range(2):
                    qb = qa[hh, rows, :]
                    s_b = jnp.where(keep, _nt(qb, ka[hh, c0:c0 + bw, :]), NEG)
                    m = jnp.max(s_b, axis=-1, keepdims=True)
                    if c0:
                        s_a = _nt(qb, ka[hh, 0:c0, :])
                        m = jnp.maximum(m, jnp.max(s_a, axis=-1, keepdims=True))
                    p_b = jnp.exp(s_b - m)
                    l = jnp.sum(p_b, axis=-1, keepdims=True)
                    acc = _nn(p_b, va[hh, c0:c0 + bw, :])
                    if c0:
                        p_a = jnp.exp(s_a - m)
                        l = l + jnp.sum(p_a, axis=-1, keepdims=True)
                        acc = acc + _nn(p_a, va[hh, 0:c0, :])
                    res.append((acc / l, m + jnp.log(l)))
                (o0, e0), (o1, e1) = res
                o_ref[0, rows, :] = jnp.where(lane < FOX_DH, o0, pltpu.roll(o1, FOX_DH, 1))
                lse_ref[0, 0, rows, :] = jnp.where(lane == 0, e0, jnp.where(lane == 1, e1, 0.0))
                return 0

            lax.fori_loop(0, tpb, qtile, 0)

    return _ride_call(
        body, rider, name="fox_fwd", grid=(b_, npair),
        in_specs=[col(0), col(1), col(2), full, gvec, gvec],
        out_specs=[pair, lse_spec],
        out_shape=[jax.ShapeDtypeStruct((b_, t_, fw), F32), jax.ShapeDtypeStruct((b_, npair, t_, LANE), F32)],
        scratch=[pltpu.VMEM((2, t_, LANE), MXU_DTYPE)] * 3, args=(zm, zm, zm, fc, gq2, gk2))


def _norm_bwd(x, dy, g):
    r = lax.rsqrt(jnp.mean(x * x, axis=-1, keepdims=True) + EPS)
    dyg = dy * g
    dx = r * dyg - x * (r * r * r) * jnp.mean(dyg * x, axis=-1, keepdims=True)
    return dx, jnp.sum(dy * x * r, axis=0, keepdims=True)


def _fox_bwd(zm, o, do, lse, fc, gq2, gk2, fw, col0, dz, rider=None):
    b_, t_, _ = zm.shape
    npair = fw // LANE
    tq = min(FOX_TILE, t_)
    nb = t_ // tq
    bw = min(FOX_BAND, t_)
    nband, tpb = t_ // bw, bw // tq
    scale = FOX_DH ** -0.5
    col, pair, full, gvec, lse_spec = _fox_specs(t_, fw, col0)

    def body(q_ref, k_ref, v_ref, o_ref, do_ref, lse_ref, fc_ref, gq_ref, gk_ref, _,
             dz_ref, dfc_ref, dgq_ref, dgk_ref, qa, ka, va, da, rowv, dq_acc, dk_acc, dv_acc):
        b, p = pl.program_id(0), pl.program_id(1)
        gq2v, gk2v = gq_ref[...] * scale, gk_ref[...]
        bd, lane, qx, kx, rq, rk = _fox_operands(q_ref, k_ref, v_ref, fc_ref, gq2v, gk2v, p, qa, ka, va)
        head = lane < FOX_DH
        dov = do_ref[0]
        dsum = _dot_right_exact(dov * o_ref[0], bd)
        eye = (lax.broadcasted_iota(jnp.int32, (tq, tq), 0) == lax.broadcasted_iota(jnp.int32, (tq, tq), 1)).astype(F32)
        for hh in range(2):
            da[hh] = jnp.where(head, _head_lanes(dov, hh), 0.0).astype(MXU_DTYPE)
            for blk in range(nb):
                rs = slice(blk * tq, (blk + 1) * tq)
                rowv[2 * hh:2 * hh + 1, rs] = jnp.sum(eye * lse_ref[0, 0, rs, hh:hh + 1], axis=0, keepdims=True)
                rowv[2 * hh + 1:2 * hh + 2, rs] = jnp.sum(eye * dsum[rs, hh * FOX_DH:hh * FOX_DH + 1], axis=0, keepdims=True)
        dq_acc[...] = jnp.zeros(dq_acc.shape, F32)
        ahead = lax.broadcasted_iota(jnp.int32, (tq, bw), 1) - lax.broadcasted_iota(jnp.int32, (tq, bw), 0)

        def part(hh, kb, vb, lo, hi, keep):
            qm, dm = qa[hh, lo:hi, :], da[hh, lo:hi, :]
            pt = jnp.exp(_nt(kb, qm) - rowv[2 * hh:2 * hh + 1, lo:hi])
            if keep is not None:
                pt = jnp.where(keep, pt, 0.0)
            dst = pt * (_nt(vb, dm) - rowv[2 * hh + 1:2 * hh + 2, lo:hi])
            dq_acc[hh, lo:hi, :] += _tn(dst, kb)
            return _nn(dst, qm), _nn(pt, dm)

        for band in range(nband):
            c0 = band * bw

            def kvtile(jj, _, c0=c0):
                r0 = pl.multiple_of(c0 + jj * tq, tq)
                rows = pl.ds(r0, tq)
                keep = ahead >= r0 - c0
                for hh in range(2):
                    kb, vb = ka[hh, rows, :], va[hh, rows, :]
                    dk_t, dv_t = part(hh, kb, vb, c0, c0 + bw, keep)
                    if c0 + bw < t_:
                        dk_u, dv_u = part(hh, kb, vb, c0 + bw, t_, None)
                        dk_t, dv_t = dk_t + dk_u, dv_t + dv_u
                    dk_acc[hh, rows, :] = dk_t
                    dv_acc[hh, rows, :] = dv_t
                return 0

            lax.fori_loop(0, tpb, kvtile, 0)

        dq0, dq1, dk0, dk1 = dq_acc[0], dq_acc[1], dk_acc[0], dk_acc[1]
        dqn = jnp.where(head, dq0, pltpu.roll(dq1, FOX_DH, 1))
        dkn = jnp.where(head, dk0, pltpu.roll(dk1, FOX_DH, 1))
        dqx, gq_part = _pair_norm_bwd(qx, rq, dqn, gq2v, bd)
        dkx, gk_part = _pair_norm_bwd(kx, rk, dkn, gk2v, bd)
        dz_ref[0, :, :LANE] = dqx.astype(dz_ref.dtype)
        dz_ref[0, :, LANE:2 * LANE] = dkx.astype(dz_ref.dtype)
        dz_ref[0, :, 2 * LANE:] = jnp.where(head, dv_acc[0], pltpu.roll(dv_acc[1], FOX_DH, 1)).astype(dz_ref.dtype)

        def bias_grad(dqh, dkh):
            return dqh[:, AUG + 3:AUG + 4] - dkh[:, AUG:AUG + 1]

        dfc_ref[0, 0] = jnp.where(lane == 0, bias_grad(dq0, dk0), jnp.where(lane == 1, bias_grad(dq1, dk1), 0.0))
        first = jnp.logical_and(b == 0, p == 0)
        _acc(dgq_ref, gq_part * scale, first)
        _acc(dgk_ref, gk_part, first)

    gs = jax.ShapeDtypeStruct((1, LANE), F32)
    return _ride_call(
        body, rider, name="fox_bwd", grid=(b_, npair),
        in_specs=[col(0), col(1), col(2), pair, pair, lse_spec, full, gvec, gvec, ANY],
        out_specs=[pl.BlockSpec((1, t_, 3 * LANE), lambda b, p: (b, 0, col0 // 3 + p)), lse_spec, gvec, gvec],
        out_shape=[jax.ShapeDtypeStruct(dz.shape, dz.dtype), jax.ShapeDtypeStruct((b_, npair, t_, LANE), F32), gs, gs],
        scratch=[pltpu.VMEM((2, t_, LANE), MXU_DTYPE)] * 4
        + [pltpu.VMEM((8, t_), F32)] + [pltpu.VMEM((2, t_, LANE), F32)] * 3,
        args=(zm, zm, zm, o, do, lse, fc, gq2, gk2, dz), aliases={9: 0})


def _mem_specs(t_, m_, mw, col0):
    nh = mw // LANE
    qcol = pl.BlockSpec((1, t_, LANE), lambda b, h: (b, 0, col0 + h))
    kcol = pl.BlockSpec((1, m_, LANE), lambda b, h: (b, 0, h))
    vcol = pl.BlockSpec((1, m_, LANE), lambda b, h: (b, 0, nh + h))
    ycol = pl.BlockSpec((1, t_, LANE), lambda b, h: (b, 0, h))
    gvec = pl.BlockSpec((1, LANE), lambda b, h: (0, 0))
    return qcol, kcol, vcol, ycol, gvec


def _mem_fwd(zm, mkv, gq, gk, mw, col0):
    b_, t_, _ = zm.shape
    m_ = mkv.shape[1]
    tq = min(MEM_TILE, t_)
    nb = t_ // tq
    scale = MEM_DH ** -0.5
    qcol, kcol, vcol, ycol, gvec = _mem_specs(t_, m_, mw, col0)

    def body(q_ref, k_ref, v_ref, gq_ref, gk_ref, y_ref):
        gqv, gkv = gq_ref[...] * scale, gk_ref[...]
        kv = k_ref[0]
        kn = _mx(kv * lax.rsqrt(jnp.mean(kv * kv, axis=-1, keepdims=True) + EPS) * gkv)
        vv = _mx(v_ref[0])

        def blk(i, _):
            rows = pl.ds(pl.multiple_of(i * tq, tq), tq)
            qv = q_ref[0, rows, :]
            s = _nt(qv * lax.rsqrt(jnp.mean(qv * qv, axis=-1, keepdims=True) + EPS) * gqv, kn)
            e = jnp.exp(s - jnp.max(s, axis=-1, keepdims=True))
            y_ref[0, rows, :] = _nn(e / jnp.sum(e, axis=-1, keepdims=True), vv)
            return 0

        lax.fori_loop(0, nb, blk, 0)

    return pl.pallas_call(
        body, name="mem_fwd", grid=(b_, MEM_HEADS), in_specs=[qcol, kcol, vcol, gvec, gvec], out_specs=ycol,
        out_shape=jax.ShapeDtypeStruct((b_, t_, mw), F32), compiler_params=_params(2),
    )(zm, mkv, mkv, gq, gk)


def _mem_bwd(zm, mkv, dy, gq, gk, mw, col0, dz):
    b_, t_, _ = zm.shape
    m_ = mkv.shape[1]
    tq = min(MEM_TILE, t_)
    nb = t_ // tq
    scale = MEM_DH ** -0.5
    qcol, kcol, vcol, ycol, gvec = _mem_specs(t_, m_, mw, col0)

    def body(q_ref, k_ref, v_ref, dy_ref, gq_ref, gk_ref, _, dq_ref, dk_ref, dv_ref, dgq_ref, dgk_ref):
        gqv, gkv = gq_ref[...] * scale, gk_ref[...]
        kv = k_ref[0]
        kn = _mx(kv * lax.rsqrt(jnp.mean(kv * kv, axis=-1, keepdims=True) + EPS) * gkv)
        vv = _mx(v_ref[0])

        def blk(i, carry):
            dkn, dvv, dgq = carry
            rows = pl.ds(pl.multiple_of(i * tq, tq), tq)
            qv = q_ref[0, rows, :]
            qn = _mx(qv * lax.rsqrt(jnp.mean(qv * qv, axis=-1, keepdims=True) + EPS) * gqv)
            s = _nt(qn, kn)
            e = jnp.exp(s - jnp.max(s, axis=-1, keepdims=True))
            pm = e / jnp.sum(e, axis=-1, keepdims=True)
            dob = _mx(dy_ref[0, rows, :])
            dp = _nt(dob, vv)
            ds = pm * (dp - jnp.sum(dp * pm, axis=-1, keepdims=True))
            dqv, gq_part = _norm_bwd(qv, _nn(ds, kn), gqv)
            dq_ref[0, rows, :] = dqv.astype(dq_ref.dtype)
            return dkn + _tn(ds, qn), dvv + _tn(pm, dob), dgq + gq_part * scale

        z = jnp.zeros((m_, LANE), F32)
        dkn, dvv, dgq = lax.fori_loop(0, nb, blk, (z, z, jnp.zeros((1, LANE), F32)))
        dkv, dgk = _norm_bwd(kv, dkn, gkv)
        dk_ref[0] = dkv
        dv_ref[0] = dvv
        first = jnp.logical_and(pl.program_id(0) == 0, pl.program_id(1) == 0)
        _acc(dgq_ref, dgq, first)
        _acc(dgk_ref, dgk, first)

    kblk = pl.BlockSpec((1, m_, LANE), lambda b, h: (b, 0, h))
    gs = jax.ShapeDtypeStruct((1, LANE), F32)
    ks = jax.ShapeDtypeStruct((b_, m_, mw), F32)
    return pl.pallas_call(
        body, name="mem_bwd", grid=(b_, MEM_HEADS), in_specs=[qcol, kcol, vcol, ycol, gvec, gvec, ANY],
        out_specs=[qcol, kblk, kblk, gvec, gvec],
        out_shape=[jax.ShapeDtypeStruct(dz.shape, dz.dtype), ks, ks, gs, gs], input_output_aliases={6: 0},
        compiler_params=_params(2),
    )(zm, mkv, mkv, dy, gq, gk, dz)


def _merge_specs(tm, d, w, gcol):
    row_d = pl.BlockSpec((tm, d), lambda i: (i, 0))
    row_w = pl.BlockSpec((tm, w), lambda i: (i, 0))
    gates = [pl.BlockSpec((tm, d), functools.partial(lambda i, k: (i, gcol + k), k=k)) for k in range(3)]
    w_br = pl.BlockSpec((w, d), lambda i: (0, 0))
    w_o = pl.BlockSpec((d, d), lambda i: (0, 0))
    return row_d, row_w, gates, w_br, w_o


def _merge_fwd(x, ys, zm, w_brs, w_out, gcol, g_next, tm=256):
    n, d = x.shape
    w = ys[0].shape[1]
    tm = _tile(n, tm, 8)
    row_d, row_w, gates, w_br, w_o = _merge_specs(tm, d, w, gcol)

    def body(x_ref, ya, yb, yc, g0, g1, g2, wa, wb, wc, wo, gn_ref, x1_ref, mg_ref, h_ref):
        mg = (_sig(g0[...]) * _nn(ya[...], wa[...]) + _sig(g1[...]) * _nn(yb[...], wb[...])
              + _sig(g2[...]) * _nn(yc[...], wc[...]))
        mg_ref[...] = mg.astype(mg_ref.dtype)
        x1 = x_ref[...] + _nn(mg, wo[...])
        x1_ref[...] = x1
        h_ref[...] = (x1 * lax.rsqrt(jnp.mean(x1 * x1, axis=-1, keepdims=True) + EPS) * gn_ref[...]).astype(h_ref.dtype)

    half = jax.ShapeDtypeStruct((n, d), MXU_DTYPE)
    return pl.pallas_call(
        body, name="merge_fwd", grid=(n // tm,),
        in_specs=[row_d, row_w, row_w, row_w] + gates + [w_br, w_br, w_br, w_o, pl.BlockSpec((1, d), lambda i: (0, 0))],
        out_specs=[row_d, row_d, row_d],
        out_shape=[jax.ShapeDtypeStruct((n, d), F32), half, half],
        compiler_params=_params(1),
    )(x, *ys, zm, zm, zm, *w_brs, w_out, g_next)


def _merge_bwd(dx1, ys, zm, w_brs, w_out, gcol, tm=256):
    n, d = dx1.shape
    w = ys[0].shape[1]
    tm = _tile(n, tm, 8)
    row_d, row_w, gates, w_br, w_o = _merge_specs(tm, d, w, gcol)

    def body(dx_ref, ya, yb, yc, g0, g1, g2, wa, wb, wc, wo, dgl_ref, dpa, dpb, dpc, dya, dyb, dyc):
        dm = _nt(dx_ref[...], wo[...])
        for k, (y, g, wr, dp_ref, dy_ref) in enumerate(((ya, g0, wa, dpa, dya), (yb, g1, wb, dpb, dyb),
                                                        (yc, g2, wc, dpc, dyc))):
            sg = _sig(g[...])
            pr = _nn(y[...], wr[...])
            dgl_ref[:, k * d:(k + 1) * d] = (dm * pr * sg * (1.0 - sg)).astype(dgl_ref.dtype)
            dp = (dm * sg).astype(dp_ref.dtype)
            dp_ref[...] = dp
            dy_ref[...] = _nt(dp, wr[...])

    sd = jax.ShapeDtypeStruct((n, d), MXU_DTYPE)
    sw = jax.ShapeDtypeStruct((n, w), F32)
    return pl.pallas_call(
        body, name="merge_bwd", grid=(n // tm,),
        in_specs=[row_d, row_w, row_w, row_w] + gates + [w_br, w_br, w_br, w_o],
        out_specs=[pl.BlockSpec((tm, 3 * d), lambda i: (i, 0)), row_d, row_d, row_d, row_w, row_w, row_w],
        out_shape=[jax.ShapeDtypeStruct((n, zm.shape[1]), MXU_DTYPE), sd, sd, sd, sw, sw, sw],
        compiler_params=_params(1),
    )(dx1, *ys, zm, zm, zm, *w_brs, w_out)


CONV_ROWS = 512
HALO = 8


def _ext(ref, r0, t_):
    rc = min(CONV_ROWS, t_)
    a, b = max(r0 - HALO, 0), min(r0 + rc + HALO, t_)
    parts = []
    if r0 - HALO < 0:
        parts.append(jnp.zeros((HALO, ref.shape[2]), F32))
    parts.append(ref[0, a:b, :].astype(F32))
    if r0 + rc + HALO > t_:
        parts.append(jnp.zeros((HALO, ref.shape[2]), F32))
    return jnp.concatenate(parts, axis=0) if len(parts) > 1 else parts[0]


def _gelu_parts(ac):
    e = jnp.exp(-0.5 * ac * ac)
    t = 1.0 / (1.0 + (0.3275911 * 2.0 ** -0.5) * jnp.abs(ac))
    tail = (0.5 * e) * (t * (0.254829592 + t * (-0.284496736 + t * (1.421413741 + t * (-1.453152027 + t * 1.061405429)))))
    return jnp.where(ac < 0, tail, 1.0 - tail), e * ((2.0 * math.pi) ** -0.5)


def _conv_taps(a_ext, cw, cb):
    a2, a1 = pltpu.roll(a_ext, 2, 0), pltpu.roll(a_ext, 1, 0)
    return cw[0:1, :] * a2 + cw[1:2, :] * a1 + cw[2:3, :] * a_ext + cb, a2, a1


def _glu_specs(t_, f, g):
    gate = pl.BlockSpec((1, t_, g), lambda j, b: (b, 0, j))
    value = pl.BlockSpec((1, t_, g), lambda j, b: (b, 0, f // g + j))
    cwb = pl.BlockSpec((3, g), lambda j, b: (0, j))
    cbb = pl.BlockSpec((1, g), lambda j, b: (0, j))
    return gate, value, cwb, cbb


def _glu_fwd(u, cw, cb):
    b_, t_, f2 = u.shape
    f = f2 // 2
    g = min(FFN_GROUP, f)
    rc = min(CONV_ROWS, t_)
    gate, value, cwb, cbb = _glu_specs(t_, f, g)

    def body(a_ref, v_ref, cw_ref, cb_ref, y_ref):
        cwv, cbv = cw_ref[...], cb_ref[...]
        for r0 in range(0, t_, rc):
            ac = _conv_taps(_ext(a_ref, r0, t_), cwv, cbv)[0][HALO:HALO + rc]
            cdf, _ = _gelu_parts(ac)
            y_ref[0, r0:r0 + rc, :] = (ac * cdf * v_ref[0, r0:r0 + rc, :]).astype(y_ref.dtype)

    return pl.pallas_call(
        body, name="glu_fwd", grid=(f // g, b_), in_specs=[gate, value, cwb, cbb], out_specs=gate,
        out_shape=jax.ShapeDtypeStruct((b_, t_, f), MXU_DTYPE), compiler_params=_params(2),
    )(u, u, cw, cb)


def _glu_bwd(u, dy, cw, cb):
    b_, t_, f2 = u.shape
    f = f2 // 2
    g = min(FFN_GROUP, f)
    rc = min(CONV_ROWS, t_)
    ne = rc + 2 * HALO
    gate, value, cwb, cbb = _glu_specs(t_, f, g)

    def body(a_ref, v_ref, dy_ref, cw_ref, cb_ref, da_ref, dv_ref, dcw_ref, dcb_ref):
        cwv, cbv = cw_ref[...], cb_ref[...]
        dcw = [jnp.zeros((1, g), F32) for _ in range(3)]
        dcb = jnp.zeros((1, g), F32)
        for r0 in range(0, t_, rc):
            a_ext, v_ext, dy_ext = _ext(a_ref, r0, t_), _ext(v_ref, r0, t_), _ext(dy_ref, r0, t_)
            ac, a2, a1 = _conv_taps(a_ext, cwv, cbv)
            cdf, pdf = _gelu_parts(ac)
            dac = dy_ext * v_ext * (cdf + ac * pdf)
            da = cwv[2:3, :] * dac + cwv[1:2, :] * pltpu.roll(dac, ne - 1, 0) + cwv[0:1, :] * pltpu.roll(dac, ne - 2, 0)
            mid = slice(HALO, HALO + rc)
            da_ref[0, r0:r0 + rc, :] = da[mid].astype(da_ref.dtype)
            dv_ref[0, r0:r0 + rc, :] = (dy_ext[mid] * ac[mid] * cdf[mid]).astype(dv_ref.dtype)
            dacm = dac[mid]
            dcw[0] = dcw[0] + jnp.sum(dacm * a2[mid], axis=0, keepdims=True)
            dcw[1] = dcw[1] + jnp.sum(dacm * a1[mid], axis=0, keepdims=True)
            dcw[2] = dcw[2] + jnp.sum(dacm * a_ext[mid], axis=0, keepdims=True)
            dcb = dcb + jnp.sum(dacm, axis=0, keepdims=True)
        first = pl.program_id(1) == 0
        _acc(dcw_ref, jnp.concatenate(dcw, axis=0), first)
        _acc(dcb_ref, dcb, first)

    sds = jax.ShapeDtypeStruct((b_, t_, f), MXU_DTYPE)
    return pl.pallas_call(
        body, name="glu_bwd", grid=(f // g, b_), in_specs=[gate, value, gate, cwb, cbb],
        out_specs=[gate, gate, cwb, cbb],
        out_shape=[sds, sds, jax.ShapeDtypeStruct((3, f), F32), jax.ShapeDtypeStruct((1, f), F32)],
        compiler_params=_params(2),
    )(u, u, dy, cw, cb)


def _place():
    x, y, c = lax.axis_index("x"), lax.axis_index("y"), lax.axis_index("c")
    chips = [(1 - x, y), (x, 1 - y), (1 - x, 1 - y)]
    return x, y, c, chips


def _remote(src, dst, send_sem, recv_sem, to):
    return pltpu.make_async_remote_copy(src_ref=src, dst_ref=dst, send_sem=send_sem, recv_sem=recv_sem,
                                        device_id=to, device_id_type=MESH)


STACK, COLS = "stack", "cols"


def _shard_ref(ref, kind, s, rows, c):
    if kind == COLS:
        cols = pl.ds(pl.multiple_of(s * c, LANE), c)
        return ref.at[:, cols] if rows is None else ref.at[rows, cols]
    return ref.at[s] if rows is None else ref.at[s, rows, :]


def _halves(c, half):
    mine = pl.ds(pl.multiple_of(c * half, 16), half)
    theirs = pl.ds(pl.multiple_of((1 - c) * half, 16), half)
    return mine, theirs


def _gather_parts(kinds):
    def first_copies(ins, outs, sems):
        x, y, c, chips = _place()
        me = 2 * x + y
        cps = []
        for i, (w_ref, o_ref, kind) in enumerate(zip(ins, outs, kinds)):
            r, cw = w_ref.shape
            mine, _ = _halves(c, r // 2)
            for j, chip in enumerate(chips):
                cps.append(_remote(w_ref.at[mine], _shard_ref(o_ref, kind, me, mine, cw), sems[0].at[6 * i + j],
                                   sems[1].at[6 * i + j], (*chip, c)))
        return cps

    def start(ins, outs, sems):
        for cp in first_copies(ins, outs, sems):
            cp.start()

    def finish(ins, outs, sems):
        x, y, c, chips = _place()
        sib = (x, y, 1 - c)
        passed = []
        for i, (w_ref, o_ref, kind) in enumerate(zip(ins, outs, kinds)):
            r, cw = w_ref.shape
            mine, _ = _halves(c, r // 2)
            for j, (px, py) in enumerate(chips):
                blk = _shard_ref(o_ref, kind, 2 * px + py, mine, cw)
                _remote(blk, blk, sems[0].at[6 * i + j], sems[1].at[6 * i + j], sib).wait_recv()
                passed.append(_remote(blk, blk, sems[0].at[6 * i + 3 + j], sems[1].at[6 * i + 3 + j], sib))
                passed[-1].start()
        for i, (w_ref, o_ref, kind) in enumerate(zip(ins, outs, kinds)):
            r, cw = w_ref.shape
            _, theirs = _halves(c, r // 2)
            for j, (px, py) in enumerate(chips):
                blk = _shard_ref(o_ref, kind, 2 * px + py, theirs, cw)
                _remote(blk, blk, sems[0].at[6 * i + 3 + j], sems[1].at[6 * i + 3 + j], sib).wait_recv()
        for cp in first_copies(ins, outs, sems) + passed:
            cp.wait_send()

    return start, finish


def _gather_shapes(shards, kinds):
    return [jax.ShapeDtypeStruct((a.shape[0], N_CHIPS * a.shape[1]) if k == COLS else (N_CHIPS,) + a.shape, a.dtype)
            for a, k in zip(shards, kinds)]


def _gather_sems(nw):
    return [pltpu.SemaphoreType.DMA((6 * nw,)), pltpu.SemaphoreType.DMA((6 * nw,))]


def _gather_shards(shards, kinds):
    nw = len(shards)
    start, finish = _gather_parts(kinds)

    def body(*refs):
        ins, outs, sems = refs[:nw], refs[nw:2 * nw], refs[2 * nw:]
        start(ins, outs, sems)
        finish(ins, outs, sems)

    return pl.pallas_call(
        body, name="gather_shards", in_specs=[ANY] * nw, out_specs=[ANY] * nw,
        out_shape=_gather_shapes(shards, kinds), scratch_shapes=_gather_sems(nw),
    )(*shards)


def _gather_rider(shards, kinds):
    start, finish = _gather_parts(kinds)
    return _Rider(list(shards), _gather_shapes(shards, kinds), _gather_sems(len(shards)), start, finish)


def _half_shape(g, kind):
    if kind == COLS:
        return (g.shape[0] // 2, g.shape[1])
    return (g.shape[0], g.shape[1] // 2, g.shape[2])


def _swap_parts(kinds):
    def copies(ins, outs, sems):
        x, y, c, _ = _place()
        cps = []
        for i, (g_ref, a_ref, kind) in enumerate(zip(ins, outs, kinds)):
            r = g_ref.shape[0] if kind == COLS else g_ref.shape[1]
            _, theirs = _halves(c, r // 2)
            src = g_ref.at[theirs] if kind == COLS else g_ref.at[:, theirs]
            cps.append(_remote(src, a_ref, sems[0].at[i], sems[1].at[i], (x, y, 1 - c)))
        return cps

    def start(ins, outs, sems):
        for cp in copies(ins, outs, sems):
            cp.start()

    def finish(ins, outs, sems):
        for cp in copies(ins, outs, sems):
            cp.wait()

    return start, finish


def _swap_shapes(gs, kinds):
    return [jax.ShapeDtypeStruct(_half_shape(g, k), g.dtype) for g, k in zip(gs, kinds)]


def _pair_swap_halves(gs, kinds, name):
    nw = len(gs)
    start, finish = _swap_parts(kinds)

    def body(*refs):
        ins, outs, sems = refs[:nw], refs[nw:2 * nw], refs[2 * nw:]
        start(ins, outs, sems)
        finish(ins, outs, sems)

    return pl.pallas_call(
        body, name=name, in_specs=[ANY] * nw, out_specs=[ANY] * nw, out_shape=_swap_shapes(gs, kinds),
        scratch_shapes=[pltpu.SemaphoreType.DMA((nw,)), pltpu.SemaphoreType.DMA((nw,))],
    )(*gs)


def _swap_rider(gs, kinds):
    start, finish = _swap_parts(kinds)
    nw = len(gs)
    return _Rider(list(gs), _swap_shapes(gs, kinds), [pltpu.SemaphoreType.DMA((nw,)), pltpu.SemaphoreType.DMA((nw,))],
                  start, finish)


def _row_tile(rows, width, itemsize=4, target=2 ** 21):
    return _tile(rows, max(8, target // (width * itemsize)), 8)


def _add_half(g, a, kind, c_idx, name):
    if kind == COLS:
        half, wd = a.shape
        tr = _row_tile(half, wd)
        nblk = half // tr
        grid = (nblk,)
        g_spec = pl.BlockSpec((tr, wd), lambda i, c_ref: (c_ref[0] * nblk + i, 0))
        a_spec = pl.BlockSpec((tr, wd), lambda i, c_ref: (i, 0))
    else:
        n, half, wd = a.shape
        tr = _row_tile(half, wd)
        nblk = half // tr
        grid = (n, nblk)
        g_spec = pl.BlockSpec((1, tr, wd), lambda s, i, c_ref: (s, c_ref[0] * nblk + i, 0))
        a_spec = pl.BlockSpec((1, tr, wd), lambda s, i, c_ref: (s, i, 0))

    def body(c_ref, g_ref, a_ref, o_ref):
        o_ref[...] = (g_ref[...] + a_ref[...]).astype(o_ref.dtype)

    return pl.pallas_call(
        body, name=name,
        grid_spec=pltpu.PrefetchScalarGridSpec(num_scalar_prefetch=1, grid=grid, in_specs=[g_spec, a_spec],
                                               out_specs=a_spec),
        out_shape=jax.ShapeDtypeStruct(a.shape, EXCHANGE_DTYPE), compiler_params=_params(len(grid)),
    )(c_idx, g, a)


def _exchange_parts(kinds):
    def copies(ins, outs, sems):
        x, y, c, chips = _place()
        me = 2 * x + y
        cps = []
        for i, (p_ref, b_ref, kind) in enumerate(zip(ins, outs, kinds)):
            cw = b_ref.shape[2]
            for j, (px, py) in enumerate(chips):
                cps.append(_remote(_shard_ref(p_ref, kind, 2 * px + py, None, cw), b_ref.at[me],
                                   sems[0].at[3 * i + j], sems[1].at[3 * i + j], (px, py, c)))
        return cps

    def start(ins, outs, sems):
        for cp in copies(ins, outs, sems):
            cp.start()

    def finish(ins, outs, sems):
        x, y, c, chips = _place()
        for i, b_ref in enumerate(outs):
            for j, (px, py) in enumerate(chips):
                blk = b_ref.at[2 * px + py]
                _remote(blk, blk, sems[0].at[3 * i + j], sems[1].at[3 * i + j], (px, py, c)).wait_recv()
        for cp in copies(ins, outs, sems):
            cp.wait_send()

    return start, finish


def _exchange_shapes(ps, kinds):
    return [jax.ShapeDtypeStruct((N_CHIPS,) + ((p.shape[0], p.shape[1] // N_CHIPS) if k == COLS else tuple(p.shape[1:])),
                                 p.dtype) for p, k in zip(ps, kinds)]


def _exchange_sems(nw):
    return [pltpu.SemaphoreType.DMA((3 * nw,)), pltpu.SemaphoreType.DMA((3 * nw,))]


def _exchange_rider(ps, kinds):
    start, finish = _exchange_parts(kinds)
    return _Rider(list(ps), _exchange_shapes(ps, kinds), _exchange_sems(len(ps)), start, finish)


def _sum_chips(bq, name):
    n, h, wd = bq.shape
    tr = _row_tile(h, wd * n)

    def body(b_ref, o_ref):
        acc = b_ref[0].astype(F32)
        for s in range(1, n):
            acc = acc + b_ref[s].astype(F32)
        o_ref[...] = acc

    return pl.pallas_call(
        body, name=name, grid=(h // tr,),
        in_specs=[pl.BlockSpec((n, tr, wd), lambda i: (0, i, 0))], out_specs=pl.BlockSpec((tr, wd), lambda i: (i, 0)),
        out_shape=jax.ShapeDtypeStruct((h, wd), F32), compiler_params=_params(1),
    )(bq)


def _pair_join_halves(qs):
    nw = len(qs)

    def body(*refs):
        ins, outs = refs[:nw], refs[nw:2 * nw]
        send_sems, recv_sems = refs[2 * nw:]
        x, y, c, _ = _place()
        sent = []
        for i, (q_ref, o_ref) in enumerate(zip(ins, outs)):
            sent.append(_remote(q_ref, o_ref.at[c], send_sems.at[i], recv_sems.at[i], (x, y, 1 - c)))
            sent[-1].start()
        for i, (q_ref, o_ref) in enumerate(zip(ins, outs)):
            _remote(q_ref, o_ref.at[1 - c], send_sems.at[i], recv_sems.at[i], (x, y, 1 - c)).wait_recv()
        for cp in sent:
            cp.wait_send()

    return pl.pallas_call(
        body, name="pair_join_halves", in_specs=[ANY] * nw, out_specs=[ANY] * nw,
        out_shape=[jax.ShapeDtypeStruct((2,) + q.shape, q.dtype) for q in qs],
        scratch_shapes=[pltpu.SemaphoreType.DMA((nw,)), pltpu.SemaphoreType.DMA((nw,))],
    )(*qs)


def _all_sum_small(s, name):
    sr, w = s.shape

    def body(s_ref, o_ref, buf, send_sems, recv_sems):
        x, y, c, _ = _place()
        me = 4 * x + 2 * y + c
        buf[me] = s_ref[...]
        peers = []
        for k in range(1, 8):
            px = 1 - x if k & 4 else x
            py = 1 - y if k & 2 else y
            pc = 1 - c if k & 1 else c
            peers.append((px, py, pc))
        sent = [_remote(s_ref, buf.at[me], send_sems.at[k], recv_sems.at[k], peer) for k, peer in enumerate(peers)]
        for cp in sent:
            cp.start()
        for k, (px, py, pc) in enumerate(peers):
            _remote(s_ref, buf.at[4 * px + 2 * py + pc], send_sems.at[k], recv_sems.at[k], (px, py, pc)).wait_recv()
        for cp in sent:
            cp.wait_send()
        acc = buf[0]
        for d in range(1, 8):
            acc = acc + buf[d]
        o_ref[...] = acc

    vm = pl.BlockSpec(memory_space=pltpu.VMEM)
    return pl.pallas_call(
        body, name=name, in_specs=[vm], out_specs=vm, out_shape=jax.ShapeDtypeStruct((sr, w), F32),
        scratch_shapes=[pltpu.VMEM((8, sr, w), F32), pltpu.SemaphoreType.DMA((7,)), pltpu.SemaphoreType.DMA((7,))],
    )(s)


BIG = ("w_in", "mem_kv_w", "w_br_hgrn", "w_br_fox", "w_br_mem", "w_out", "ffn_w_up", "ffn_w_down")
KIND = {"w_in": STACK, "mem_kv_w": STACK, "w_br_hgrn": COLS, "w_br_fox": COLS, "w_br_mem": COLS, "w_out": STACK,
        "ffn_w_up": STACK, "ffn_w_down": STACK}
ROW_SHARDED = ("mem_kv_w", "w_out", "ffn_w_down")
FIRST = ("w_in",)
REST = tuple(nm for nm in BIG if nm not in FIRST)
LATE = {"in_proj": tuple(nm for nm in REST if not nm.startswith("ffn_")),
        "fox_fwd": tuple(nm for nm in REST if nm.startswith("ffn_"))}
LAST = ("w_in",)
TRANSPOSED = ("w_in",)


def _z_layout(d, hw, fw, mw):
    gate, npair, nh, nm = 3 * d // LANE, fw // LANE, hw // LANE, mw // LANE
    fox0, hg0 = gate, gate + 3 * npair
    o_fox, o_mem = 4 * nh, 4 * nh + 3 * npair
    order = [o_mem + nm + j for j in range(gate)]
    order += [o_fox + k * npair + p for p in range(npair) for k in range(3)]
    order += [k * nh + h for h in range(nh) for k in range(4)]
    order += [o_mem + h for h in range(nm)]
    assert fox0 % 3 == 0 and hg0 % 4 == 0
    return fox0, hg0, hg0 + 4 * nh, order


def _reorder_blocks(a, order):
    runs, start = [], 0
    for i in range(1, len(order) + 1):
        if i == len(order) or order[i] != order[i - 1] + 1:
            runs.append((order[start], order[i - 1] + 1))
            start = i
    return jnp.concatenate([a[:, lo * LANE:hi * LANE] for lo, hi in runs], axis=1)


def _put_shard(arr, kind, s, piece):
    if kind == COLS:
        return lax.dynamic_update_slice(arr, piece, (0, s * piece.shape[1]))
    return lax.dynamic_update_slice(arr, piece[None], (s, 0, 0))


def _take_shard(arr, kind, s):
    if kind == COLS:
        return lax.dynamic_slice(arr, (0, s * (arr.shape[1] // N_CHIPS)), (arr.shape[0], arr.shape[1] // N_CHIPS))
    return lax.dynamic_index_in_dim(arr, s, 0, keepdims=False)


def _w_in_pieces(cs, s1, nf):
    out = []
    for s in range(N_CHIPS):
        lo, hi = cs * s, cs * (s + 1)
        for a, b, forget in ((lo, min(hi, s1), False), (max(lo, s1), min(hi, s1 + nf), True), (max(lo, s1 + nf), hi, False)):
            if a < b:
                out.append((s, a - lo, b - lo, forget, a - s1 if forget else (a if a < s1 else a - nf)))
    return out


def _split_w_in(stacked, s1, nf):
    pieces = _w_in_pieces(stacked.shape[2], s1, nf)
    main = [stacked[s, :, a:b] for s, a, b, forget, _ in pieces if not forget]
    ff = [stacked[s, :, a:b] for s, a, b, forget, _ in pieces if forget]
    return jnp.concatenate(main, axis=1), jnp.concatenate(ff, axis=1)


def _join_w_in(g_main, g_ff, s1, nf):
    cs = (g_main.shape[1] + nf) // N_CHIPS
    shards = [[] for _ in range(N_CHIPS)]
    for s, a, b, forget, off in _w_in_pieces(cs, s1, nf):
        shards[s].append((g_ff if forget else g_main)[:, off:off + b - a])
    return jnp.stack([jnp.concatenate(p, axis=1) if len(p) > 1 else p[0] for p in shards])


SMALL = ("norm_mix_g", "norm_mem_g", "norm_ffn_g", "hgrn_lb_logits", "hgrn_norm_g", "fox_f_bias", "fox_q_norm_g",
         "fox_k_norm_g", "mem_q_norm_g", "mem_k_norm_g", "ffn_conv_b")


def _small_rows(shapes):
    rows = []
    for a, (r, c) in enumerate(shapes):
        for i in range(r):
            for lo in range(0, c, FLAT_W):
                rows.append((a, i, lo, min(FLAT_W, c - lo)))
    return rows


def _pack_small(vals):
    rows = _small_rows([v.shape for v in vals])
    sr = -(-len(rows) // 8) * 8

    def body(*refs):
        o_ref = refs[-1]
        o_ref[...] = jnp.zeros(o_ref.shape, F32)
        for k, (a, i, lo, wd) in enumerate(rows):
            o_ref[k:k + 1, 0:wd] = refs[a][i:i + 1, lo:lo + wd]

    vm = pl.BlockSpec(memory_space=pltpu.VMEM)
    return pl.pallas_call(body, name="pack_small", in_specs=[vm] * len(vals), out_specs=vm,
                          out_shape=jax.ShapeDtypeStruct((sr, FLAT_W), F32))(*vals)


def _row_of(buf_ref, rows, a, i):
    parts = [buf_ref[k:k + 1, 0:wd] for k, (a2, i2, _, wd) in enumerate(rows) if (a2, i2) == (a, i)]
    return jnp.concatenate(parts, axis=1) if len(parts) > 1 else parts[0]


def _unpack_small(buf, shapes):
    rows = _small_rows(shapes)

    def body(buf_ref, *outs):
        for a, (r, _) in enumerate(shapes):
            for i in range(r):
                outs[a][i:i + 1, :] = _row_of(buf_ref, rows, a, i)

    vm = pl.BlockSpec(memory_space=pltpu.VMEM)
    return pl.pallas_call(body, name="unpack_small", in_specs=[vm], out_specs=[vm] * len(shapes),
                          out_shape=[jax.ShapeDtypeStruct(shp, F32) for shp in shapes])(buf)


def _adamw_small(buf, shapes, ws, ms, vs):
    n = len(ws)
    rows = _small_rows(shapes)
    c1 = 1.0 / (1.0 - ADAM_B1 ** ADAM_STEP)
    c2 = 1.0 / (1.0 - ADAM_B2 ** ADAM_STEP)

    def body(buf_ref, *refs):
        w_refs, m_refs, v_refs = refs[:n], refs[n:2 * n], refs[2 * n:3 * n]
        outs = refs[3 * n:]
        g_out, d_out, m_out, v_out, rest = outs[:n], outs[n:2 * n], outs[2 * n:3 * n], outs[3 * n:4 * n], outs[4 * n:]
        for a, (r, _) in enumerate(shapes):
            for i in range(r):
                gv = _row_of(buf_ref, rows, a, i)
                if a >= n:
                    rest[a - n][i:i + 1, :] = gv
                    continue
                row = slice(i, i + 1)
                mn = ADAM_B1 * m_refs[a][row, :] + (1.0 - ADAM_B1) * gv
                vn = ADAM_B2 * v_refs[a][row, :] + (1.0 - ADAM_B2) * (gv * gv)
                g_out[a][row, :] = gv
                d_out[a][row, :] = -ADAM_LR * ((mn * c1) / (jnp.sqrt(vn * c2) + ADAM_EPS) + ADAM_WD * w_refs[a][row, :])
                m_out[a][row, :] = mn
                v_out[a][row, :] = vn

    vm = pl.BlockSpec(memory_space=pltpu.VMEM)
    own = [jax.ShapeDtypeStruct(shp, F32) for shp in shapes[:n]]
    outs = pl.pallas_call(
        body, name="adamw_small", in_specs=[vm] * (1 + 3 * n), out_specs=[vm] * (4 * n + len(shapes) - n),
        out_shape=own * 4 + [jax.ShapeDtypeStruct(shp, F32) for shp in shapes[n:]],
    )(buf, *ws, *ms, *vs)
    return outs[:n], outs[n:2 * n], outs[2 * n:3 * n], outs[3 * n:4 * n], outs[4 * n:]


def _pad_lanes(v, width=LANE):
    return jnp.pad(v, ((0, 0), (0, width - v.shape[1])))


WEIGHTS = ("norm_mix_g", "norm_mem_g", "w_in", "hgrn_lb_logits", "hgrn_norm_g", "fox_f_bias", "fox_q_norm_g",
           "fox_k_norm_g", "mem_kv_w", "mem_q_norm_g", "mem_k_norm_g", "w_br_hgrn", "w_br_fox", "w_br_mem", "w_out",
           "norm_ffn_g", "ffn_w_up", "ffn_conv_w", "ffn_conv_b", "ffn_w_down")


def _local_step(x, mem, target, w, full, conv_w, late=None, hooks=None):
    b_, t_, d = x.shape
    n = b_ * t_
    hw, fw, mw = HG_HEADS * HG_D, FOX_HEADS * FOX_DH, MEM_HEADS * MEM_DH
    m_ = mem.shape[1]
    f = conv_w.shape[1]
    s1 = 4 * hw + 3 * fw
    fox_col, hg_col, mem_col, order = _z_layout(d, hw, fw, mw)
    gate_col = 0
    inverse = [order.index(j) for j in range(len(order))]

    w_main, w_ff = _split_w_in(full["w_in"], s1, FOX_HEADS)
    w_main = _reorder_blocks(w_main, order)
    w_ff = _pad_lanes(w_ff)
    f_bias = _pad_lanes(w["fox_f_bias"])
    cb = w["ffn_conv_b"]

    x2 = x.reshape(n, d)
    h = _rmsnorm_fwd(x2, w["norm_mix_g"], name="norm_mix_fwd")
    if late:
        pieces, kinds, finish = late["in_proj"]
        zm, gathered = _matmul(h, w_main, name="in_proj", rider=_gather_rider(pieces, kinds))
        full = {**full, **finish(gathered)}
    else:
        zm = _matmul(h, w_main, name="in_proj")
    w_brs = [full["w_br_hgrn"], full["w_br_fox"], full["w_br_mem"]]
    w_out, w_kv = full["w_out"], full["mem_kv_w"]
    zf = _matmul(h, w_ff, name="in_proj_forget")
    zm3, zf3 = zm.reshape(b_, t_, -1), zf.reshape(b_, t_, LANE)
    ya = _hgrn_fwd(zm3, w["hgrn_lb_logits"], w["hgrn_norm_g"], hw, hg_col)
    fc = _fox_prep(zf3, f_bias)
    fox_gq, fox_gk = jnp.tile(w["fox_q_norm_g"], (1, 2)), jnp.tile(w["fox_k_norm_g"], (1, 2))
    if late:
        pieces, kinds, finish = late["fox_fwd"]
        (yb, lse), gathered = _fox_fwd(zm3, fc, fox_gq, fox_gk, fw, fox_col, _gather_rider(pieces, kinds))
        full = {**full, **finish(gathered)}
    else:
        yb, lse = _fox_fwd(zm3, fc, fox_gq, fox_gk, fw, fox_col)[0]
    w_up, w_down = full["ffn_w_up"], full["ffn_w_down"]
    mem2 = mem.reshape(b_ * m_, d)
    hm = _rmsnorm_fwd(mem2, w["norm_mem_g"], name="norm_mem_fwd")
    mkv = _matmul(hm, w_kv, name="mem_kv_proj").reshape(b_, m_, 2 * mw)
    yc = _mem_fwd(zm3, mkv, w["mem_q_norm_g"], w["mem_k_norm_g"], mw, mem_col)
    ys = [ya.reshape(n, hw), yb.reshape(n, fw), yc.reshape(n, mw)]
    x1, merged, h2 = _merge_fwd(x2, ys, zm, w_brs, w_out, gate_col, w["norm_ffn_g"])
    u = _matmul(h2, w_up, name="ffn_up")
    u3 = u.reshape(b_, t_, 2 * f)
    yff = _glu_fwd(u3, conv_w, cb).reshape(n, f)
    dy, (loss_vec,), _ = _matmul_rows([yff], w_down, name="ffn_down_loss", tb=False, row_ins=[x1, target.reshape(n, d)],
                                      vec_ins=[], epilogue=_loss_epilogue, n_vec_out=1)

    grads = {}

    def ridden(name, call):
        if not hooks or name not in hooks:
            return call(None)[0]
        rider, then = hooks[name](grads)
        outs, extra = call(rider)
        then(extra)
        return outs

    dyff = _matmul(dy, w_down, tb=True, name="ffn_down_dx")
    grads["ffn_w_down"] = _matmul(yff, dy, ta=True, name="ffn_down_dw", tm=1408)
    du_a, du_v, grads["ffn_conv_w"], grads["ffn_conv_b"] = _glu_bwd(u3, dyff.reshape(b_, t_, f), conv_w, cb)
    du_a, du_v = du_a.reshape(n, f), du_v.reshape(n, f)
    dx1, (grads["norm_ffn_g"],), _ = _matmul_rows(
        [du_a, du_v], w_up, name="ffn_up_dx", tb=True, row_ins=[x1, dy], vec_ins=[w["norm_ffn_g"]],
        epilogue=_norm_bwd_epilogue(0), n_vec_out=1)
    grads["ffn_w_up"] = _matmul(h2, None, ta=True, name="ffn_up_dw", b_parts=[du_a, du_v], tn=f // 2, stack_out=True)

    dz, dpa, dpb, dpc, dya, dyb, dyc = _merge_bwd(dx1, ys, zm, w_brs, w_out, gate_col)
    dz = dz.reshape(b_, t_, -1)
    grads["w_out"] = _matmul(merged, dx1, ta=True, name="out_proj_dw")
    for nm, y_, dp_ in zip(("w_br_hgrn", "w_br_fox", "w_br_mem"), ys, (dpa, dpb, dpc)):
        grads[nm] = _matmul(y_, dp_, ta=True, name=nm + "_dw")

    dz, dmk, dmv, grads["mem_q_norm_g"], grads["mem_k_norm_g"] = _mem_bwd(
        zm3, mkv, dyc.reshape(b_, t_, mw), w["mem_q_norm_g"], w["mem_k_norm_g"], mw, mem_col, dz)
    dmkv = jnp.concatenate([dmk, dmv], axis=-1).reshape(b_ * m_, 2 * mw)
    grads["mem_kv_w"] = _matmul(hm, dmkv, ta=True, name="mem_kv_dw")
    dhm = _matmul(dmkv, w_kv, tb=True, name="mem_kv_dx")
    _, grads["norm_mem_g"] = _rmsnorm_bwd(mem2, [dhm], w["norm_mem_g"], None, name="norm_mem_bwd")

    dz, dfc, g_fq, g_fk = ridden("fox_bwd", lambda rider: _fox_bwd(
        zm3, yb, dyb.reshape(b_, t_, fw), lse, fc, fox_gq, fox_gk, fw, fox_col, dz, rider))
    grads["fox_q_norm_g"] = g_fq[:, :FOX_DH] + g_fq[:, FOX_DH:]
    grads["fox_k_norm_g"] = g_fk[:, :FOX_DH] + g_fk[:, FOX_DH:]
    dzf, g_fb = _fox_post(dfc, zf3, f_bias)
    grads["fox_f_bias"] = g_fb[:, :FOX_HEADS]

    dz, grads["hgrn_lb_logits"], grads["hgrn_norm_g"] = ridden("hgrn_bwd", lambda rider: _hgrn_bwd(
        zm3, dya.reshape(b_, t_, hw), w["hgrn_lb_logits"], w["hgrn_norm_g"], hw, hg_col, dz, rider))
    dzm = dz.reshape(n, -1)
    dzf2 = dzf.reshape(n, LANE)
    g_main = _matmul(h, dzm, ta=True, name="in_proj_dw")
    g_ff = _matmul(h, dzf2, ta=True, name="in_proj_forget_dw")
    grads["w_in"] = _join_w_in(_reorder_blocks(g_main, inverse), g_ff[:, :FOX_HEADS], s1, FOX_HEADS)

    dh_b = _matmul(dzf2, w_ff, tb=True, name="in_proj_forget_dx")

    def in_proj_dx(rider):
        out = _matmul(dzm, w_main, tb=True, name="in_proj_dx", rider=rider)
        return ([out[0]], out[1]) if rider else ([out], None)

    dh_a, = ridden("in_proj_dx", in_proj_dx)
    grad_x, grads["norm_mix_g"] = _rmsnorm_bwd(x2, [dh_a, dh_b], w["norm_mix_g"], dx1, name="norm_mix_bwd")
    return loss_vec, grad_x.reshape(b_, t_, d), grads


def kernel(x, mem, norm_mix_g, norm_mem_g, w_in, hgrn_lb_logits, hgrn_norm_g, fox_f_bias, fox_q_norm_g, fox_k_norm_g, mem_kv_w, mem_q_norm_g, mem_k_norm_g, w_br_hgrn, w_br_fox, w_br_mem, w_out, norm_ffn_g, ffn_w_up, ffn_conv_w, ffn_conv_b, ffn_w_down, loss_target, m_norm_mix_g, m_norm_mem_g, m_w_in, m_hgrn_lb_logits, m_hgrn_norm_g, m_fox_f_bias, m_fox_q_norm_g, m_fox_k_norm_g, m_mem_kv_w, m_mem_q_norm_g, m_mem_k_norm_g, m_w_br_hgrn, m_w_br_fox, m_w_br_mem, m_w_out, m_norm_ffn_g, m_ffn_w_up, m_ffn_conv_w, m_ffn_conv_b, m_ffn_w_down, v_norm_mix_g, v_norm_mem_g, v_w_in, v_hgrn_lb_logits, v_hgrn_norm_g, v_fox_f_bias, v_fox_q_norm_g, v_fox_k_norm_g, v_mem_kv_w, v_mem_q_norm_g, v_mem_k_norm_g, v_w_br_hgrn, v_w_br_fox, v_w_br_mem, v_w_out, v_norm_ffn_g, v_ffn_w_up, v_ffn_conv_w, v_ffn_conv_b, v_ffn_w_down):
    w = dict(zip(WEIGHTS, (norm_mix_g, norm_mem_g, w_in, hgrn_lb_logits, hgrn_norm_g, fox_f_bias, fox_q_norm_g,
                           fox_k_norm_g, mem_kv_w, mem_q_norm_g, mem_k_norm_g, w_br_hgrn, w_br_fox, w_br_mem, w_out,
                           norm_ffn_g, ffn_w_up, ffn_conv_w, ffn_conv_b, ffn_w_down)))
    m = dict(zip(WEIGHTS, (m_norm_mix_g, m_norm_mem_g, m_w_in, m_hgrn_lb_logits, m_hgrn_norm_g, m_fox_f_bias,
                           m_fox_q_norm_g, m_fox_k_norm_g, m_mem_kv_w, m_mem_q_norm_g, m_mem_k_norm_g, m_w_br_hgrn,
                           m_w_br_fox, m_w_br_mem, m_w_out, m_norm_ffn_g, m_ffn_w_up, m_ffn_conv_w, m_ffn_conv_b,
                           m_ffn_w_down)))
    v = dict(zip(WEIGHTS, (v_norm_mix_g, v_norm_mem_g, v_w_in, v_hgrn_lb_logits, v_hgrn_norm_g, v_fox_f_bias,
                           v_fox_q_norm_g, v_fox_k_norm_g, v_mem_kv_w, v_mem_q_norm_g, v_mem_k_norm_g, v_w_br_hgrn,
                           v_w_br_fox, v_w_br_mem, v_w_out, v_norm_ffn_g, v_ffn_w_up, v_ffn_conv_w, v_ffn_conv_b,
                           v_ffn_w_down)))
    c_idx = lax.axis_index("c")
    chip = 2 * lax.axis_index("x") + lax.axis_index("y")

    mine = {nm: w[nm][0].astype(MXU_DTYPE) for nm in BIG}

    def gathered_full(names, arrays):
        out = {nm: _put_shard(g, KIND[nm], chip, mine[nm]) for nm, g in zip(names, arrays)}
        return {nm: g.reshape(-1, g.shape[2]) if nm in ROW_SHARDED else g for nm, g in out.items()}

    full = gathered_full(FIRST, _gather_shards([mine[nm] for nm in FIRST], [KIND[nm] for nm in FIRST]))
    late = {host: ([mine[nm] for nm in names], [KIND[nm] for nm in names],
                   functools.partial(gathered_full, names)) for host, names in LATE.items()}
    cs = ffn_conv_w.shape[2]
    f = cs * N_CHIPS
    placed = lax.dynamic_update_slice(jnp.zeros((3, f), F32), ffn_conv_w[0] * (c_idx == 0).astype(F32), (0, chip * cs))
    conv_w = _unpack_small(_all_sum_small(_pack_small([placed]), "gather_conv_w"), [(3, f)])[0]

    c_arr = jnp.reshape(c_idx, (1,)).astype(jnp.int32)

    def stacked(nm, g):
        return g.reshape(N_CHIPS, -1, g.shape[1]) if nm in ROW_SHARDED else g

    def with_own(landed, partial, kinds):
        return [_put_shard(bq, STACK, chip, _take_shard(p, k, chip)) for bq, p, k in zip(landed, partial, kinds)]

    kinds_rest, kinds_last = [KIND[nm] for nm in REST], [KIND[nm] for nm in LAST]
    state = {}

    def swap_rest(grads):
        gs = [stacked(nm, grads[nm]) for nm in REST]

        def then(from_sibling):
            state["partial_rest"] = [_add_half(g, a, k, c_arr, "add_half_" + nm)
                                     for g, a, k, nm in zip(gs, from_sibling, kinds_rest, REST)]

        return _swap_rider(gs, kinds_rest), then

    def exchange_rest(grads):
        def then(landed):
            state["landed_rest"] = with_own(landed, state["partial_rest"], kinds_rest)

        return _exchange_rider(state["partial_rest"], kinds_rest), then

    def exchange_last(grads):
        gs = [stacked(nm, grads[nm]) for nm in LAST]
        from_sibling = _pair_swap_halves(gs, kinds_last, "pair_swap_halves_last")
        partial = [_add_half(g, a, k, c_arr, "add_half_" + nm) for g, a, k, nm in zip(gs, from_sibling, kinds_last, LAST)]

        def then(landed):
            state["landed_last"] = with_own(landed, partial, kinds_last)

        return _exchange_rider(partial, kinds_last), then

    hooks = {"fox_bwd": swap_rest, "hgrn_bwd": exchange_rest, "in_proj_dx": exchange_last}

    loss_vec, grad_x, grads = _local_step(x, mem, loss_target, w, full, conv_w, late, hooks)

    landed = dict(zip(LAST + REST, state["landed_last"] + state["landed_rest"]))
    reduced_half = [_sum_chips(landed[nm], "sum_chips_" + nm) for nm in BIG]
    joined = [lax.dynamic_update_slice(o, q[None], (c_idx, 0, 0)).reshape(2 * q.shape[0], q.shape[1])
              for o, q in zip(_pair_join_halves(reduced_half), reduced_half)]
    gshards = dict(zip(BIG, joined))

    small_shapes = [w[nm].shape for nm in SMALL] + [grads["ffn_conv_w"].shape, loss_vec.shape]
    summed = _all_sum_small(_pack_small([grads[nm] for nm in SMALL] + [grads["ffn_conv_w"], loss_vec]),
                            "all_sum_small_grads")
    g_small, d_small, m_small, v_small, (g_conv_w, loss_row) = _adamw_small(
        summed, small_shapes, [w[nm] for nm in SMALL], [m[nm] for nm in SMALL], [v[nm] for nm in SMALL])
    loss = jnp.sum(loss_row)
    g_out = {nm: gshards[nm][None] for nm in BIG}
    g_out["ffn_conv_w"] = lax.dynamic_slice(g_conv_w, (0, chip * cs), (3, cs))[None]
    delta, new_m, new_v = dict(zip(SMALL, d_small)), dict(zip(SMALL, m_small)), dict(zip(SMALL, v_small))
    g_out.update(zip(SMALL, g_small))
    for nm in BIG + ("ffn_conv_w",):
        operands = (w[nm], g_out[nm], m[nm], v[nm])
        if nm in TRANSPOSED:
            operands = [jnp.swapaxes(a, 1, 2) for a in operands]
        outs = _adamw(*operands, name="adamw_" + nm)
        delta[nm], new_m[nm], new_v[nm] = [jnp.swapaxes(o, 1, 2) for o in outs] if nm in TRANSPOSED else outs

    return (loss, grad_x, *[g_out[nm] for nm in WEIGHTS], *[delta[nm] for nm in WEIGHTS],
            *[new_m[nm] for nm in WEIGHTS], *[new_v[nm] for nm in WEIGHTS])
```
